```python
import jax, jax.numpy as jnp
from jax import lax
import numpy as np

D_MODEL = 1024
BATCH = 8
SEQ = 2048
DEPTH = 1

GRID_W = 64
N_META = 16
EPS = 1e-6

NA_HEADS = 8
NA_HEAD_DIM = 64
NA_WIDTH = NA_HEADS * NA_HEAD_DIM
NA_WIN_H = 8
NA_WIN_W = 16

HG_HEADS = 4
HG_DK = 128
HG_DV = 128
HG_KDIM = HG_HEADS * HG_DK
HG_VDIM = HG_HEADS * HG_DV
HG_CHUNK = 16

D_FF = 4 * D_MODEL

IN_SPLIT = (NA_WIDTH, NA_WIDTH, NA_WIDTH,
            HG_KDIM, HG_KDIM, HG_KDIM, HG_VDIM, HG_VDIM,
            D_MODEL, D_MODEL)
IN_COLS = sum(IN_SPLIT)

kernel_name = "hybrid_natten_hgrn2_griffin_block"


def rms_norm(x, g):
    xf = x.astype(jnp.float32)
    y = xf * lax.rsqrt(jnp.mean(xf * xf, axis=-1, keepdims=True) + EPS)
    return (y * g.astype(jnp.float32)).astype(x.dtype)


def split_cols(a):
    outs, off = [], 0
    for n in IN_SPLIT:
        outs.append(a[..., off:off + n])
        off += n
    return outs


def neighbourhood_attention(q, k, v, rpb):
    B, L, H, dh = q.shape
    T = L - N_META
    rows = T // GRID_W
    kh = min(NA_WIN_H, rows)
    scale = dh ** -0.5
    qm, km, vm = q[:, :N_META], k[:, :N_META], v[:, :N_META]
    qg = q[:, N_META:].reshape(B, rows, GRID_W, H, dh)
    kg = k[:, N_META:].reshape(B, rows, GRID_W, H, dh)
    vg = v[:, N_META:].reshape(B, rows, GRID_W, H, dh)

    r = jnp.arange(rows)
    row_start = jnp.clip(r - kh // 2, 0, rows - kh)
    row_idx = row_start[:, None] + jnp.arange(kh)[None, :]
    k_blk = kg[:, row_idx]
    v_blk = vg[:, row_idx]

    s_win = jnp.einsum('brchd,brjwhd->bhrcjw', qg, k_blk).astype(jnp.float32) * scale
    c = jnp.arange(GRID_W)
    col_start = jnp.clip(c - NA_WIN_W // 2, 0, GRID_W - NA_WIN_W)
    in_win = (c[None, :] >= col_start[:, None]) & (c[None, :] < col_start[:, None] + NA_WIN_W)
    dr = row_idx - r[:, None]
    dc = jnp.clip(c[None, :] - c[:, None], -(NA_WIN_W - 1), NA_WIN_W - 1)
    bias = rpb.astype(jnp.float32)[:, dr[:, None, :, None] + NA_WIN_H - 1,
                                   dc[None, :, None, :] + NA_WIN_W - 1]
    s_win = jnp.where(in_win[:, None, :], s_win + bias[None], -1e30)

    s_meta = jnp.einsum('brchd,bmhd->bhrcm', qg, km).astype(jnp.float32) * scale
    s = jnp.concatenate([s_win.reshape(B, H, rows, GRID_W, kh * GRID_W), s_meta], axis=-1)
    p = jax.nn.softmax(s, axis=-1).astype(v.dtype)
    p_win = p[..., :kh * GRID_W].reshape(B, H, rows, GRID_W, kh, GRID_W)
    p_meta = p[..., kh * GRID_W:]
    o_grid = (jnp.einsum('bhrcjw,brjwhd->brchd', p_win, v_blk)
              + jnp.einsum('bhrcm,bmhd->brchd', p_meta, vm)).reshape(B, T, H, dh)

    s_mm = jnp.einsum('bmhd,bnhd->bhmn', qm, km).astype(jnp.float32) * scale
    p_mm = jax.nn.softmax(s_mm, axis=-1).astype(v.dtype)
    o_meta = jnp.einsum('bhmn,bnhd->bmhd', p_mm, vm)
    return jnp.concatenate([o_meta, o_grid], axis=1)


def chunk_scan(q, k, v, log_f):
    B, L, H, dk = q.shape
    dv = v.shape[-1]
    n = L // HG_CHUNK

    def to_chunks(a):
        return a.reshape(B, n, HG_CHUNK, H, a.shape[-1]).transpose(1, 0, 3, 2, 4)

    tri = jnp.tril(jnp.ones((HG_CHUNK, HG_CHUNK), dtype=bool))

    def step(S, inp):
        qi, ki, vi, gi = inp
        b = jnp.cumsum(gi, axis=-2)
        o_inter = jnp.einsum('bhtk,bhkv->bhtv', qi * jnp.exp(b), S)
        diff = jnp.where(tri[:, :, None], b[..., :, None, :] - b[..., None, :, :], -jnp.inf)
        A = jnp.einsum('bhtk,bhsk,bhtsk->bhts', qi, ki, jnp.exp(diff))
        o_intra = jnp.einsum('bhts,bhsv->bhtv', A, vi)
        b_last = b[..., -1:, :]
        S_new = (jnp.exp(b_last[..., 0, :])[..., None] * S
                 + jnp.einsum('bhsk,bhsv->bhkv', ki * jnp.exp(b_last - b), vi))
        return S_new, o_inter + o_intra

    S0 = jnp.zeros((B, H, dk, dv), jnp.float32)
    _, o = lax.scan(step, S0, (to_chunks(q), to_chunks(k), to_chunks(v), to_chunks(log_f)))
    return o.transpose(1, 0, 3, 2, 4).reshape(B, L, H, dv)


def hgrn2_branch(q, z_fwd, z_bwd, i, g, lb, gain):
    B, L, _ = q.shape
    dtype = q.dtype

    def heads(a, d):
        return a.astype(jnp.float32).reshape(B, L, HG_HEADS, d)

    qh = jax.nn.silu(heads(q, HG_DK))
    vh = heads(i, HG_DV)

    def gates(z, lb_dir):
        lb_h = lb_dir.reshape(HG_HEADS, HG_DK)
        log_f = jnp.logaddexp(jnp.log(lb_h), jnp.log1p(-lb_h) + jax.nn.log_sigmoid(heads(z, HG_DK)))
        return -jnp.expm1(log_f), log_f

    k_f, lf_f = gates(z_fwd, lb[0])
    k_b, lf_b = gates(z_bwd, lb[1])
    rev = lambda a: jnp.flip(a, axis=1)
    o = chunk_scan(qh, k_f, vh, lf_f) + rev(chunk_scan(rev(qh), rev(k_b), rev(vh), rev(lf_b)))
    o = o * lax.rsqrt(jnp.mean(o * o, axis=-1, keepdims=True) + EPS)
    o = o.reshape(B, L, HG_VDIM) * gain.astype(jnp.float32) * jax.nn.silu(g.astype(jnp.float32))
    return o.astype(dtype)


def _fwd_setup_inputs(seed: int = 0) -> dict:
    key = jax.random.key(seed)
    ks = jax.random.split(key, 14)
    f32 = jnp.float32
    nrm = lambda k, shape, s: jax.random.normal(k, shape, f32) * s
    return {
        "x": nrm(ks[0], (BATCH, SEQ, D_MODEL), 1.0),
        "meta_tokens": nrm(ks[1], (N_META, D_MODEL), 1.0),
        "w_in": nrm(ks[2], (DEPTH, D_MODEL, IN_COLS), D_MODEL ** -0.5),
        "w_na_out": nrm(ks[3], (DEPTH, NA_WIDTH, D_MODEL), NA_WIDTH ** -0.5),
        "w_hg_out": nrm(ks[4], (DEPTH, HG_VDIM, D_MODEL), HG_VDIM ** -0.5),
        "w_o": nrm(ks[5], (DEPTH, D_MODEL, D_MODEL), D_MODEL ** -0.5),
        "w_up": nrm(ks[6], (DEPTH, D_MODEL, D_FF), D_MODEL ** -0.5),
        "w_down": nrm(ks[7], (DEPTH, D_FF, D_MODEL), D_FF ** -0.5),
        "norm_mix": 1.0 + nrm(ks[8], (DEPTH, D_MODEL), 0.05),
        "norm_mlp": 1.0 + nrm(ks[9], (DEPTH, D_MODEL), 0.05),
        "norm_final": 1.0 + nrm(ks[10], (D_MODEL,), 0.05),
        "hg_norm": 1.0 + nrm(ks[11], (DEPTH, HG_VDIM), 0.05),
        "na_rpb": nrm(ks[12], (DEPTH, NA_HEADS, 2 * NA_WIN_H - 1, 2 * NA_WIN_W - 1), 0.1),
        "hg_lb_logits": nrm(ks[13], (2, DEPTH + 1, HG_KDIM), 0.5),
    }


def _fwd_reference(x, meta_tokens, w_in, w_na_out, w_hg_out, w_o, w_up, w_down,
              norm_mix, norm_mlp, norm_final, hg_norm, na_rpb, hg_lb_logits):
    B = x.shape[0]
    h = jnp.concatenate([jnp.broadcast_to(meta_tokens.astype(x.dtype)[None], (B, N_META, D_MODEL)), x], axis=1)
    L = h.shape[1]
    lb_all = jnp.cumsum(jax.nn.softmax(hg_lb_logits.astype(jnp.float32), axis=1), axis=1)
    for l in range(DEPTH):
        a = rms_norm(h, norm_mix[l])
        (q_na, k_na, v_na, q_hg, z_f, z_b, i_hg, g_hg, gate_na, gate_hg) = split_cols(a @ w_in[l])
        hd = lambda t: t.reshape(B, L, NA_HEADS, NA_HEAD_DIM)
        y_na = neighbourhood_attention(hd(q_na), hd(k_na), hd(v_na), na_rpb[l]).reshape(B, L, NA_WIDTH) @ w_na_out[l]
        y_hg = hgrn2_branch(q_hg, z_f, z_b, i_hg, g_hg, lb_all[:, l], hg_norm[l]) @ w_hg_out[l]
        mix = jax.nn.sigmoid(gate_na) * y_na + jax.nn.sigmoid(gate_hg) * y_hg
        h = h + mix @ w_o[l]
        m = rms_norm(h, norm_mlp[l])
        h = h + jnp.square(jax.nn.relu(m @ w_up[l])) @ w_down[l]
    h = rms_norm(h, norm_final)
    return h[:, N_META:]


import jax as _jax
import jax.numpy as _jnp

TWIN_FORMAT = 'train_step'
FWD_PARAMS = ['x', 'meta_tokens', 'w_in', 'w_na_out', 'w_hg_out', 'w_o', 'w_up', 'w_down', 'norm_mix', 'norm_mlp', 'norm_final', 'hg_norm', 'na_rpb', 'hg_lb_logits']
TWIN_WEIGHTS = ['meta_tokens', 'w_in', 'w_na_out', 'w_hg_out', 'w_o', 'w_up', 'w_down', 'norm_mix', 'norm_mlp', 'norm_final', 'hg_norm', 'na_rpb', 'hg_lb_logits']
TWIN_DIFF_INPUT = 'x'
TWIN_INPUTS = ['x', 'meta_tokens', 'w_in', 'w_na_out', 'w_hg_out', 'w_o', 'w_up', 'w_down', 'norm_mix', 'norm_mlp', 'norm_final', 'hg_norm', 'na_rpb', 'hg_lb_logits', 'loss_target', 'm_meta_tokens', 'm_w_in', 'm_w_na_out', 'm_w_hg_out', 'm_w_o', 'm_w_up', 'm_w_down', 'm_norm_mix', 'm_norm_mlp', 'm_norm_final', 'm_hg_norm', 'm_na_rpb', 'm_hg_lb_logits', 'v_meta_tokens', 'v_w_in', 'v_w_na_out', 'v_w_hg_out', 'v_w_o', 'v_w_up', 'v_w_down', 'v_norm_mix', 'v_norm_mlp', 'v_norm_final', 'v_hg_norm', 'v_na_rpb', 'v_hg_lb_logits']
TWIN_OUTPUTS = ['loss', 'grad_x', 'grad_meta_tokens', 'grad_w_in', 'grad_w_na_out', 'grad_w_hg_out', 'grad_w_o', 'grad_w_up', 'grad_w_down', 'grad_norm_mix', 'grad_norm_mlp', 'grad_norm_final', 'grad_hg_norm', 'grad_na_rpb', 'grad_hg_lb_logits', 'delta_meta_tokens', 'delta_w_in', 'delta_w_na_out', 'delta_w_hg_out', 'delta_w_o', 'delta_w_up', 'delta_w_down', 'delta_norm_mix', 'delta_norm_mlp', 'delta_norm_final', 'delta_hg_norm', 'delta_na_rpb', 'delta_hg_lb_logits', 'new_m_meta_tokens', 'new_m_w_in', 'new_m_w_na_out', 'new_m_w_hg_out', 'new_m_w_o', 'new_m_w_up', 'new_m_w_down', 'new_m_norm_mix', 'new_m_norm_mlp', 'new_m_norm_final', 'new_m_hg_norm', 'new_m_na_rpb', 'new_m_hg_lb_logits', 'new_v_meta_tokens', 'new_v_w_in', 'new_v_w_na_out', 'new_v_w_hg_out', 'new_v_w_o', 'new_v_w_up', 'new_v_w_down', 'new_v_norm_mix', 'new_v_norm_mlp', 'new_v_norm_final', 'new_v_hg_norm', 'new_v_na_rpb', 'new_v_hg_lb_logits']
TWIN_LEAF_KINDS = {'loss': 'loss', 'grad_x': 'grad_x', 'grad_meta_tokens': 'grad_w', 'grad_w_in': 'grad_w', 'grad_w_na_out': 'grad_w', 'grad_w_hg_out': 'grad_w', 'grad_w_o': 'grad_w', 'grad_w_up': 'grad_w', 'grad_w_down': 'grad_w', 'grad_norm_mix': 'grad_w', 'grad_norm_mlp': 'grad_w', 'grad_norm_final': 'grad_w', 'grad_hg_norm': 'grad_w', 'grad_na_rpb': 'grad_w', 'grad_hg_lb_logits': 'grad_w', 'delta_meta_tokens': 'delta_w', 'delta_w_in': 'delta_w', 'delta_w_na_out': 'delta_w', 'delta_w_hg_out': 'delta_w', 'delta_w_o': 'delta_w', 'delta_w_up': 'delta_w', 'delta_w_down': 'delta_w', 'delta_norm_mix': 'delta_w', 'delta_norm_mlp': 'delta_w', 'delta_norm_final': 'delta_w', 'delta_hg_norm': 'delta_w', 'delta_na_rpb': 'delta_w', 'delta_hg_lb_logits': 'delta_w', 'new_m_meta_tokens': 'new_m', 'new_m_w_in': 'new_m', 'new_m_w_na_out': 'new_m', 'new_m_w_hg_out': 'new_m', 'new_m_w_o': 'new_m', 'new_m_w_up': 'new_m', 'new_m_w_down': 'new_m', 'new_m_norm_mix': 'new_m', 'new_m_norm_mlp': 'new_m', 'new_m_norm_final': 'new_m', 'new_m_hg_norm': 'new_m', 'new_m_na_rpb': 'new_m', 'new_m_hg_lb_logits': 'new_m', 'new_v_meta_tokens': 'new_v', 'new_v_w_in': 'new_v', 'new_v_w_na_out': 'new_v', 'new_v_w_hg_out': 'new_v', 'new_v_w_o': 'new_v', 'new_v_w_up': 'new_v', 'new_v_w_down': 'new_v', 'new_v_norm_mix': 'new_v', 'new_v_norm_mlp': 'new_v', 'new_v_norm_final': 'new_v', 'new_v_hg_norm': 'new_v', 'new_v_na_rpb': 'new_v', 'new_v_hg_lb_logits': 'new_v'}


def _forward(args):
    return _fwd_reference(*[args[k] for k in FWD_PARAMS])


def _output_shape():
    out = _jax.eval_shape(lambda: _forward(_fwd_setup_inputs(0)))
    return out.shape, out.dtype

N_MICROBATCH = 1
ADAM_LR = 0.001
ADAM_B1 = 0.9
ADAM_B2 = 0.999
ADAM_EPS = 1e-08
ADAM_WD = 0.01
ADAM_STEP = 10
PER_EXAMPLE_BATCH_AXIS = {'x': 0, 'loss_target': 0}
SHARED_INPUTS = []
_WEIGHT_DTYPES = {'meta_tokens': _jnp.float32, 'w_in': _jnp.float32, 'w_na_out': _jnp.float32, 'w_hg_out': _jnp.float32, 'w_o': _jnp.float32, 'w_up': _jnp.float32, 'w_down': _jnp.float32, 'norm_mix': _jnp.float32, 'norm_mlp': _jnp.float32, 'norm_final': _jnp.float32, 'hg_norm': _jnp.float32, 'na_rpb': _jnp.float32, 'hg_lb_logits': _jnp.float32}
MOMENT_SCALE = {'meta_tokens': 2.157747e-03, 'w_in': 2.753172e-02, 'w_na_out': 1.321391e-02, 'w_hg_out': 4.244160e-02, 'w_o': 4.430403e-02, 'w_up': 5.564172e-02, 'w_down': 1.445113e-01, 'norm_mix': 6.914054e-02, 'norm_mlp': 1.061533e-01, 'norm_final': 1.619559e+01, 'hg_norm': 6.353990e-02, 'na_rpb': 5.499314e-03, 'hg_lb_logits': 3.162602e-03}


def _to_microbatches(a, axis):
    t = _jnp.moveaxis(a, axis, 0)
    t = t.reshape((N_MICROBATCH, t.shape[0] // N_MICROBATCH) + t.shape[1:])
    return _jnp.moveaxis(t, 1, axis + 1)


def setup_inputs(seed: int = 0) -> dict:
    inp = _fwd_setup_inputs(seed)
    key = _jax.random.fold_in(_jax.random.key(seed), 7919)
    shape, _ = _output_shape()
    out = dict(inp)
    out["loss_target"] = _jax.random.normal(_jax.random.fold_in(key, 0), shape, _jnp.float32)
    for i, name in enumerate(TWIN_WEIGHTS):
        w = inp[name].astype(_jnp.float32)
        if MOMENT_SCALE is None:
            s = _jnp.sqrt(_jnp.mean(_jnp.square(w)) + 1e-30)
        else:
            s = MOMENT_SCALE[name]
        km, kv = _jax.random.split(_jax.random.fold_in(key, i + 1))
        out[name] = w
        out["m_" + name] = s * _jax.random.normal(km, w.shape, _jnp.float32)
        out["v_" + name] = (s * s) * _jax.random.uniform(kv, w.shape, _jnp.float32, 0.5, 1.5)
    if N_MICROBATCH > 1:
        for name, axis in PER_EXAMPLE_BATCH_AXIS.items():
            out[name] = _to_microbatches(out[name], axis)
    return {'x': out['x'], 'meta_tokens': out['meta_tokens'], 'w_in': out['w_in'], 'w_na_out': out['w_na_out'], 'w_hg_out': out['w_hg_out'], 'w_o': out['w_o'], 'w_up': out['w_up'], 'w_down': out['w_down'], 'norm_mix': out['norm_mix'], 'norm_mlp': out['norm_mlp'], 'norm_final': out['norm_final'], 'hg_norm': out['hg_norm'], 'na_rpb': out['na_rpb'], 'hg_lb_logits': out['hg_lb_logits'], 'loss_target': out['loss_target'], 'm_meta_tokens': out['m_meta_tokens'], 'm_w_in': out['m_w_in'], 'm_w_na_out': out['m_w_na_out'], 'm_w_hg_out': out['m_w_hg_out'], 'm_w_o': out['m_w_o'], 'm_w_up': out['m_w_up'], 'm_w_down': out['m_w_down'], 'm_norm_mix': out['m_norm_mix'], 'm_norm_mlp': out['m_norm_mlp'], 'm_norm_final': out['m_norm_final'], 'm_hg_norm': out['m_hg_norm'], 'm_na_rpb': out['m_na_rpb'], 'm_hg_lb_logits': out['m_hg_lb_logits'], 'v_meta_tokens': out['v_meta_tokens'], 'v_w_in': out['v_w_in'], 'v_w_na_out': out['v_w_na_out'], 'v_w_hg_out': out['v_w_hg_out'], 'v_w_o': out['v_w_o'], 'v_w_up': out['v_w_up'], 'v_w_down': out['v_w_down'], 'v_norm_mix': out['v_norm_mix'], 'v_norm_mlp': out['v_norm_mlp'], 'v_norm_final': out['v_norm_final'], 'v_hg_norm': out['v_hg_norm'], 'v_na_rpb': out['v_na_rpb'], 'v_hg_lb_logits': out['v_hg_lb_logits']}


def _loss(weights, diff, rest, loss_target):
    with _jax.named_scope("forward"):
        args = {**rest, TWIN_DIFF_INPUT: diff, **{k: w.astype(_WEIGHT_DTYPES[k]) for k, w in weights.items()}}
        y = _forward(args)
    with _jax.named_scope("loss_head"):
        err = _jnp.square(y.astype(_jnp.float32) - loss_target)
        return 0.5 * _jnp.sum(_jnp.mean(err, axis=-1)) if err.ndim else 0.5 * err


def _adamw(w, g, m, v):
    m = ADAM_B1 * m + (1.0 - ADAM_B1) * g
    v = ADAM_B2 * v + (1.0 - ADAM_B2) * _jnp.square(g)
    m_hat = m / (1.0 - ADAM_B1 ** ADAM_STEP)
    v_hat = v / (1.0 - ADAM_B2 ** ADAM_STEP)
    delta = -ADAM_LR * (m_hat / (_jnp.sqrt(v_hat) + ADAM_EPS) + ADAM_WD * w)
    return delta, m, v


def reference(x, meta_tokens, w_in, w_na_out, w_hg_out, w_o, w_up, w_down, norm_mix, norm_mlp, norm_final, hg_norm, na_rpb, hg_lb_logits, loss_target, m_meta_tokens, m_w_in, m_w_na_out, m_w_hg_out, m_w_o, m_w_up, m_w_down, m_norm_mix, m_norm_mlp, m_norm_final, m_hg_norm, m_na_rpb, m_hg_lb_logits, v_meta_tokens, v_w_in, v_w_na_out, v_w_hg_out, v_w_o, v_w_up, v_w_down, v_norm_mix, v_norm_mlp, v_norm_final, v_hg_norm, v_na_rpb, v_hg_lb_logits):
    given = dict(x=x, meta_tokens=meta_tokens, w_in=w_in, w_na_out=w_na_out, w_hg_out=w_hg_out, w_o=w_o, w_up=w_up, w_down=w_down, norm_mix=norm_mix, norm_mlp=norm_mlp, norm_final=norm_final, hg_norm=hg_norm, na_rpb=na_rpb, hg_lb_logits=hg_lb_logits, loss_target=loss_target, m_meta_tokens=m_meta_tokens, m_w_in=m_w_in, m_w_na_out=m_w_na_out, m_w_hg_out=m_w_hg_out, m_w_o=m_w_o, m_w_up=m_w_up, m_w_down=m_w_down, m_norm_mix=m_norm_mix, m_norm_mlp=m_norm_mlp, m_norm_final=m_norm_final, m_hg_norm=m_hg_norm, m_na_rpb=m_na_rpb, m_hg_lb_logits=m_hg_lb_logits, v_meta_tokens=v_meta_tokens, v_w_in=v_w_in, v_w_na_out=v_w_na_out, v_w_hg_out=v_w_hg_out, v_w_o=v_w_o, v_w_up=v_w_up, v_w_down=v_w_down, v_norm_mix=v_norm_mix, v_norm_mlp=v_norm_mlp, v_norm_final=v_norm_final, v_hg_norm=v_hg_norm, v_na_rpb=v_na_rpb, v_hg_lb_logits=v_hg_lb_logits)
    weights = {n: given[n] for n in TWIN_WEIGHTS}
    shared = {n: given[n] for n in SHARED_INPUTS}
    per_example = {n: given[n] for n in ['x']}
    grad_fn = _jax.value_and_grad(_loss, argnums=(0, 1))

    def one_microbatch(ex, loss_target):
        ex = dict(ex)
        diff = ex.pop(TWIN_DIFF_INPUT)
        return grad_fn(weights, diff, {**shared, **ex}, loss_target)

    if N_MICROBATCH == 1:
        loss, (grad_w, grad_x) = one_microbatch(per_example, given["loss_target"])
    else:
        def body(carry, xs):
            loss_sum, grad_sum = carry
            l_k, (gw_k, gx_k) = one_microbatch(xs[0], xs[1])
            with _jax.named_scope("update"):
                return (loss_sum + l_k, _jax.tree.map(_jnp.add, grad_sum, gw_k)), gx_k

        init = (_jnp.zeros((), _jnp.float32), _jax.tree.map(_jnp.zeros_like, weights))
        (loss, grad_w), grad_x = _jax.lax.scan(body, init, (per_example, given["loss_target"]))
    with _jax.named_scope("update"):
        delta_w, new_m, new_v = {}, {}, {}
        for n in TWIN_WEIGHTS:
            delta_w[n], new_m[n], new_v[n] = _adamw(weights[n], grad_w[n], given["m_" + n], given["v_" + n])
    return (loss, grad_x, *[grad_w[n] for n in TWIN_WEIGHTS], *[delta_w[n] for n in TWIN_WEIGHTS],
            *[new_m[n] for n in TWIN_WEIGHTS], *[new_v[n] for n in TWIN_WEIGHTS])
```

```python
import functools

import numpy as np
import jax
import jax.numpy as jnp
from jax import lax
from jax.experimental import pallas as pl
from jax.experimental.pallas import tpu as pltpu

F32 = jnp.float32
BF16 = jnp.bfloat16
HIGHEST = lax.Precision.HIGHEST

GRID_W = 64
N_META = 16
EPS = 1e-6
NA_HEAD_DIM = 64
NA_WIN_H = 8
NA_WIN_W = 16
HG_DK = 128
HG_CHUNK = 16
LANES = 128
ROW_ALIGN = 128
VMEM_LIMIT = 48 * 1024 * 1024

ADAM_LR = 0.001
ADAM_B1 = 0.9
ADAM_B2 = 0.999
ADAM_EPS = 1e-08
ADAM_WD = 0.01
ADAM_STEP = 10

MESH = pl.DeviceIdType.MESH


def _cp(*sem):
    return pltpu.CompilerParams(dimension_semantics=sem, vmem_limit_bytes=VMEM_LIMIT)


def _sigmoid(x):
    return 1.0 / (1.0 + jnp.exp(-x))


def _dot(a, b, dims, precision=None):
    return lax.dot_general(a, b, (dims, ((), ())), preferred_element_type=F32, precision=precision)


def _nn(a, b, **kw):
    return _dot(a, b, ((1,), (0,)), **kw)


def _nt(a, b, **kw):
    return _dot(a, b, ((1,), (1,)), **kw)


def _tn(a, b, **kw):
    return _dot(a, b, ((0,), (0,)), **kw)


def _matmul(a, b, *, ta=False, tb=False, tm=None, tn=None, tk=None, out_dtype=F32, name,
            precision=None):
    if ta:
        kdim, m = a.shape
    else:
        m, kdim = a.shape
    if tb:
        n, k2 = b.shape
    else:
        k2, n = b.shape
    assert kdim == k2, (a.shape, b.shape, ta, tb)
    if tm is None:
        if ta:
            tm = next(t for t in (512, 256, 128, m) if m % t == 0)
        else:
            tm = m // 2 if (m // 2) % 16 == 0 and m > 512 else m
    if tn is None:
        tn = next(t for t in (512, 256, 128, n) if n % t == 0)
    if tk is None:
        tk = kdim if ta else next(t for t in (1024, 512, 256, 128, kdim) if kdim % t == 0)
    assert m % tm == 0 and n % tn == 0 and kdim % tk == 0, (m, n, kdim, tm, tn, tk)
    nk = kdim // tk
    op_dtype = F32 if precision is not None else BF16

    def body(a_ref, b_ref, o_ref, acc_ref):
        kk = pl.program_id(2)

        @pl.when(kk == 0)
        def _():
            acc_ref[...] = jnp.zeros_like(acc_ref)

        av = a_ref[...].astype(op_dtype)
        bv = b_ref[...].astype(op_dtype)
        dims = ((0 if ta else 1,), (1 if tb else 0,))
        acc_ref[...] += _dot(av, bv, dims, precision=precision)

        @pl.when(kk == nk - 1)
        def _():
            o_ref[...] = acc_ref[...].astype(out_dtype)

    a_spec = (pl.BlockSpec((tk, tm), lambda i, j, k: (k, i)) if ta
              else pl.BlockSpec((tm, tk), lambda i, j, k: (i, k)))
    b_spec = (pl.BlockSpec((tn, tk), lambda i, j, k: (j, k)) if tb
              else pl.BlockSpec((tk, tn), lambda i, j, k: (k, j)))
    return pl.pallas_call(
        body,
        grid=(m // tm, n // tn, nk),
        in_specs=[a_spec, b_spec],
        out_specs=pl.BlockSpec((tm, tn), lambda i, j, k: (i, j)),
        out_shape=jax.ShapeDtypeStruct((m, n), out_dtype),
        scratch_shapes=[pltpu.VMEM((tm, tn), F32)],
        compiler_params=_cp("parallel", "parallel", "arbitrary"),
        name=name,
    )(a, b)


def _rspec(tr, w, cb=0):
    return pl.BlockSpec((tr, w), lambda i: (i, cb))


def _fspec(shape):
    nd = len(shape)
    return pl.BlockSpec(shape, lambda i: (0,) * nd)


def _row_tile(lp):
    return ROW_ALIGN if lp % ROW_ALIGN == 0 else lp


def _rmsnorm_fwd(x, g, *, name):
    lp, d = x.shape
    tr = _row_tile(lp)

    def body(x_ref, g_ref, o_ref):
        xv = x_ref[...]
        r = lax.rsqrt(jnp.mean(xv * xv, axis=-1, keepdims=True) + EPS)
        o_ref[...] = (xv * r * g_ref[...]).astype(BF16)

    return pl.pallas_call(
        body, grid=(lp // tr,),
        in_specs=[_rspec(tr, d), _fspec((1, d))],
        out_specs=_rspec(tr, d),
        out_shape=jax.ShapeDtypeStruct((lp, d), BF16),
        compiler_params=_cp("parallel"), name=name)(x, g)


def _residual_norm(h, t, g, *, name):
    lp, d = h.shape
    tr = _row_tile(lp)

    def body(h_ref, t_ref, g_ref, h1_ref, m_ref):
        xv = h_ref[...] + t_ref[...]
        h1_ref[...] = xv
        r = lax.rsqrt(jnp.mean(xv * xv, axis=-1, keepdims=True) + EPS)
        m_ref[...] = (xv * r * g_ref[...]).astype(BF16)

    return pl.pallas_call(
        body, grid=(lp // tr,),
        in_specs=[_rspec(tr, d), _rspec(tr, d), _fspec((1, d))],
        out_specs=[_rspec(tr, d), _rspec(tr, d)],
        out_shape=[jax.ShapeDtypeStruct((lp, d), F32), jax.ShapeDtypeStruct((lp, d), BF16)],
        compiler_params=_cp("parallel"), name=name)(h, t, g)


def _rmsnorm_bwd_add(x, g, dy, dres, *, name):
    lp, d = x.shape
    tr = _row_tile(lp)

    def body(x_ref, g_ref, dy_ref, dr_ref, dx_ref, dg_ref):
        @pl.when(pl.program_id(0) == 0)
        def _():
            dg_ref[...] = jnp.zeros_like(dg_ref)

        xv = x_ref[...]
        r = lax.rsqrt(jnp.mean(xv * xv, axis=-1, keepdims=True) + EPS)
        xh = xv * r
        dyv = dy_ref[...]
        dg_ref[...] += jnp.sum(dyv * xh, axis=0, keepdims=True)
        dxh = dyv * g_ref[...]
        dx_ref[...] = dr_ref[...] + r * (dxh - xh * jnp.mean(dxh * xh, axis=-1, keepdims=True))

    return pl.pallas_call(
        body, grid=(lp // tr,),
        in_specs=[_rspec(tr, d), _fspec((1, d)), _rspec(tr, d), _rspec(tr, d)],
        out_specs=[_rspec(tr, d), _fspec((1, d))],
        out_shape=[jax.ShapeDtypeStruct((lp, d), F32), jax.ShapeDtypeStruct((1, d), F32)],
        compiler_params=_cp("arbitrary"), name=name)(x, g, dy, dres)


def _final_loss(h1, t2, g, tgt, *, n_tok, name):
    lp, d = h1.shape
    tr = _row_tile(lp)

    def body(h_ref, t_ref, g_ref, tg_ref, dh_ref, loss_ref, dg_ref):
        i = pl.program_id(0)

        @pl.when(i == 0)
        def _():
            loss_ref[...] = jnp.zeros_like(loss_ref)
            dg_ref[...] = jnp.zeros_like(dg_ref)

        xv = h_ref[...] + t_ref[...]
        r = lax.rsqrt(jnp.mean(xv * xv, axis=-1, keepdims=True) + EPS)
        xh = xv * r
        gv = g_ref[...]
        row = i * tr + lax.broadcasted_iota(jnp.int32, (tr, 1), 0)
        valid = (row >= N_META) & (row < N_META + n_tok)
        err = jnp.where(valid, xh * gv - tg_ref[...], 0.0)
        loss_ref[...] += jnp.sum(0.5 * err * err) / d
        dy = err / d
        dg_ref[...] += jnp.sum(dy * xh, axis=0, keepdims=True)
        dxh = dy * gv
        dh_ref[...] = r * (dxh - xh * jnp.mean(dxh * xh, axis=-1, keepdims=True))

    return pl.pallas_call(
        body, grid=(lp // tr,),
        in_specs=[_rspec(tr, d), _rspec(tr, d), _fspec((1, d)), _rspec(tr, d)],
        out_specs=[_rspec(tr, d), _fspec((1, LANES)), _fspec((1, d))],
        out_shape=[jax.ShapeDtypeStruct((lp, d), F32), jax.ShapeDtypeStruct((1, LANES), F32),
                   jax.ShapeDtypeStruct((1, d), F32)],
        compiler_params=_cp("arbitrary"), name=name)(h1, t2, g, tgt)


def _relu2(u, *, name):
    lp, f = u.shape
    tr = _row_tile(lp)

    def body(u_ref, o_ref):
        rv = jnp.maximum(u_ref[...], 0.0)
        o_ref[...] = (rv * rv).astype(BF16)

    return pl.pallas_call(
        body, grid=(lp // tr,), in_specs=[_rspec(tr, f)], out_specs=_rspec(tr, f),
        out_shape=jax.ShapeDtypeStruct((lp, f), BF16),
        compiler_params=_cp("parallel"), name=name)(u)


def _relu2_bwd(dact, u, *, name):
    lp, f = u.shape
    tr = _row_tile(lp)

    def body(d_ref, u_ref, o_ref):
        o_ref[...] = (d_ref[...] * 2.0 * jnp.maximum(u_ref[...], 0.0)).astype(BF16)

    return pl.pallas_call(
        body, grid=(lp // tr,), in_specs=[_rspec(tr, f), _rspec(tr, f)], out_specs=_rspec(tr, f),
        out_shape=jax.ShapeDtypeStruct((lp, f), BF16),
        compiler_params=_cp("parallel"), name=name)(dact, u)


def _gate_mix(proj, y_na, y_hg, *, col_gate, name):
    lp, d = y_na.shape
    tr = _row_tile(lp)
    cb = col_gate // d

    def body(gn_ref, gh_ref, yn_ref, yh_ref, o_ref):
        o_ref[...] = (_sigmoid(gn_ref[...]) * yn_ref[...]
                      + _sigmoid(gh_ref[...]) * yh_ref[...]).astype(BF16)

    return pl.pallas_call(
        body, grid=(lp // tr,),
        in_specs=[_rspec(tr, d, cb), _rspec(tr, d, cb + 1), _rspec(tr, d), _rspec(tr, d)],
        out_specs=_rspec(tr, d),
        out_shape=jax.ShapeDtypeStruct((lp, d), BF16),
        compiler_params=_cp("parallel"), name=name)(proj, proj, y_na, y_hg)


def _gate_mix_bwd(proj, y_na, y_hg, dmix, *, col_gate, name):
    lp, d = y_na.shape
    tr = _row_tile(lp)
    cb = col_gate // d

    def body(gn_ref, gh_ref, yn_ref, yh_ref, dm_ref, dyn_ref, dyh_ref, dgn_ref, dgh_ref):
        dm = dm_ref[...]
        sn = _sigmoid(gn_ref[...])
        sh = _sigmoid(gh_ref[...])
        dyn_ref[...] = (dm * sn).astype(BF16)
        dyh_ref[...] = (dm * sh).astype(BF16)
        dgn_ref[...] = (dm * yn_ref[...] * sn * (1.0 - sn)).astype(BF16)
        dgh_ref[...] = (dm * yh_ref[...] * sh * (1.0 - sh)).astype(BF16)

    sds = jax.ShapeDtypeStruct((lp, d), BF16)
    return pl.pallas_call(
        body, grid=(lp // tr,),
        in_specs=[_rspec(tr, d, cb), _rspec(tr, d, cb + 1), _rspec(tr, d), _rspec(tr, d),
                  _rspec(tr, d)],
        out_specs=[_rspec(tr, d)] * 4, out_shape=[sds] * 4,
        compiler_params=_cp("parallel"), name=name)(proj, proj, y_na, y_hg, dmix)


def _hg_out(o_f, o_b, proj, gain, *, col_g, name):
    lp, w = o_f.shape
    tr = _row_tile(lp)
    hh = w // HG_DK

    def body(of_ref, ob_ref, g_ref, gain_ref, y_ref):
        gv = g_ref[...]
        sg = gv * _sigmoid(gv)
        for h in range(hh):
            sl = slice(h * HG_DK, (h + 1) * HG_DK)
            o = of_ref[:, sl] + ob_ref[:, sl]
            r = lax.rsqrt(jnp.mean(o * o, axis=-1, keepdims=True) + EPS)
            y_ref[:, sl] = (o * r * gain_ref[:, sl] * sg[:, sl]).astype(BF16)

    return pl.pallas_call(
        body, grid=(lp // tr,),
        in_specs=[_rspec(tr, w), _rspec(tr, w), _rspec(tr, w, col_g // w), _fspec((1, w))],
        out_specs=_rspec(tr, w),
        out_shape=jax.ShapeDtypeStruct((lp, w), BF16),
        compiler_params=_cp("parallel"), name=name)(o_f, o_b, proj, gain)


def _hg_out_bwd(o_f, o_b, proj, gain, dy, *, col_g, name):
    lp, w = o_f.shape
    tr = _row_tile(lp)
    hh = w // HG_DK

    def body(of_ref, ob_ref, g_ref, gain_ref, dy_ref, do_ref, dg_ref, dgain_ref):
        @pl.when(pl.program_id(0) == 0)
        def _():
            dgain_ref[...] = jnp.zeros_like(dgain_ref)

        for h in range(hh):
            sl = slice(h * HG_DK, (h + 1) * HG_DK)
            gv = g_ref[:, sl]
            s = _sigmoid(gv)
            sg = gv * s
            dsg = s + gv * s * (1.0 - s)
            o = of_ref[:, sl] + ob_ref[:, sl]
            r = lax.rsqrt(jnp.mean(o * o, axis=-1, keepdims=True) + EPS)
            on = o * r
            dyv = dy_ref[:, sl]
            gn = gain_ref[:, sl]
            dgain_ref[:, sl] += jnp.sum(dyv * on * sg, axis=0, keepdims=True)
            dg_ref[:, sl] = (dyv * on * gn * dsg).astype(BF16)
            don = dyv * gn * sg
            do_ref[:, sl] = r * (don - on * jnp.mean(don * on, axis=-1, keepdims=True))

    return pl.pallas_call(
        body, grid=(lp // tr,),
        in_specs=[_rspec(tr, w), _rspec(tr, w), _rspec(tr, w, col_g // w), _fspec((1, w)),
                  _rspec(tr, w)],
        out_specs=[_rspec(tr, w), _rspec(tr, w), _fspec((1, w))],
        out_shape=[jax.ShapeDtypeStruct((lp, w), F32), jax.ShapeDtypeStruct((lp, w), BF16),
                   jax.ShapeDtypeStruct((1, w), F32)],
        compiler_params=_cp("arbitrary"), name=name)(o_f, o_b, proj, gain, dy)


def _hg_gates(zq, z, lbv):
    qh = zq * _sigmoid(zq)
    s = _sigmoid(z)
    f = lbv + (1.0 - lbv) * s
    kk = (1.0 - lbv) * _sigmoid(-z)
    return qh, s, f, jnp.log(f), kk


def _tri(reverse):
    r = lax.broadcasted_iota(jnp.int32, (HG_CHUNK, HG_CHUNK), 0)
    c = lax.broadcasted_iota(jnp.int32, (HG_CHUNK, HG_CHUNK), 1)
    return jnp.where((c >= r) if reverse else (c <= r), 1.0, 0.0).astype(F32)


def _hg_scan_fwd(proj, lb, *, reverse, n_chunks, col_q, col_z, col_i, hh, name):
    lp = proj.shape[0]
    c16 = HG_CHUNK
    last = 0 if reverse else c16 - 1

    def body(q_ref, z_ref, i_ref, lb_ref, o_ref, st_ref):
        o_ref[...] = jnp.zeros_like(o_ref)
        lbv = lb_ref[...]
        tri = _tri(reverse)
        row = lax.broadcasted_iota(jnp.int32, (c16, HG_DK), 0)

        def step(j, st):
            jj = (n_chunks - 1 - j) if reverse else j
            t0 = pl.multiple_of(jj * c16, c16)
            v = i_ref[pl.ds(t0, c16), :]
            qh, _, _, g, kk = _hg_gates(q_ref[pl.ds(t0, c16), :], z_ref[pl.ds(t0, c16), :], lbv)
            b = _nn(tri, g, precision=HIGHEST)
            st_ref[jj] = st
            bl = b[last:last + 1, :]
            qe = qh * jnp.exp(b)
            kd = kk * jnp.exp(bl - b)
            o = _nt(qe.astype(BF16), st.astype(BF16))
            for t in range(c16):
                mask = (row >= t) if reverse else (row <= t)
                e = jnp.where(mask, jnp.exp(jnp.minimum(b[t:t + 1, :] - b, 0.0)), 0.0)
                a_col = jnp.sum(qh[t:t + 1, :] * kk * e, axis=1, keepdims=True)
                o_t = jnp.sum(a_col * v, axis=0, keepdims=True)
                o = o + jnp.where(row == t, o_t, 0.0)
            o_ref[pl.ds(t0, c16), :] = o
            return jnp.exp(bl) * st + _tn(v.astype(BF16), kd.astype(BF16))

        lax.fori_loop(0, n_chunks, step, jnp.zeros((HG_DK, HG_DK), F32))

    cspec = lambda col: pl.BlockSpec((lp, HG_DK), lambda h: (0, col // HG_DK + h))
    return pl.pallas_call(
        body, grid=(hh,),
        in_specs=[cspec(col_q), cspec(col_z), cspec(col_i),
                  pl.BlockSpec((None, 1, HG_DK), lambda h: (h, 0, 0))],
        out_specs=[pl.BlockSpec((lp, HG_DK), lambda h: (0, h)),
                   pl.BlockSpec((None, n_chunks, HG_DK, HG_DK), lambda h: (h, 0, 0, 0))],
        out_shape=[jax.ShapeDtypeStruct((lp, hh * HG_DK), F32),
                   jax.ShapeDtypeStruct((hh, n_chunks, HG_DK, HG_DK), F32)],
        compiler_params=_cp("parallel"), name=name)(proj, proj, proj, lb)


def _hg_scan_bwd(proj, lb, states, do, *, reverse, n_chunks, col_q, col_z, col_i, hh, name):
    lp = proj.shape[0]
    c16 = HG_CHUNK
    last = 0 if reverse else c16 - 1

    def body(q_ref, z_ref, i_ref, lb_ref, st_ref, do_ref, dq_ref, dz_ref, dv_ref, dlb_ref):
        dq_ref[...] = jnp.zeros_like(dq_ref)
        dz_ref[...] = jnp.zeros_like(dz_ref)
        dv_ref[...] = jnp.zeros_like(dv_ref)
        lbv = lb_ref[...]
        tri = _tri(reverse)
        tri_t = _tri(not reverse)
        row = lax.broadcasted_iota(jnp.int32, (c16, HG_DK), 0)

        def step(j, carry):
            dst, dlb = carry
            jj = j if reverse else (n_chunks - 1 - j)
            t0 = pl.multiple_of(jj * c16, c16)
            zq = q_ref[pl.ds(t0, c16), :]
            v = i_ref[pl.ds(t0, c16), :]
            dov = do_ref[pl.ds(t0, c16), :]
            qh, s, f, g, kk = _hg_gates(zq, z_ref[pl.ds(t0, c16), :], lbv)
            b = _nn(tri, g, precision=HIGHEST)
            bl = b[last:last + 1, :]
            eb = jnp.exp(b)
            ebl = jnp.exp(bl - b)
            decay = jnp.exp(bl)
            qe = qh * eb
            kd = kk * ebl
            st = st_ref[jj]
            v16, do16 = v.astype(BF16), dov.astype(BF16)
            st16, dst16 = st.astype(BF16), dst.astype(BF16)
            do_s = _nn(do16, st16)
            da_t = _nt(v16, do16)
            v_ds = _nn(v16, dst16)
            dv = _nt(kd.astype(BF16), dst16)
            dq = eb * do_s
            dk_state = ebl * v_ds
            dk = dk_state
            for t in range(c16):
                mask = (row >= t) if reverse else (row <= t)
                e = jnp.where(mask, jnp.exp(jnp.minimum(b[t:t + 1, :] - b, 0.0)), 0.0)
                qt = qh[t:t + 1, :]
                a_col = jnp.sum(qt * kk * e, axis=1, keepdims=True)
                c_t = da_t[:, t:t + 1]
                ce = c_t * e
                dq_t = jnp.sum(ce * kk, axis=0, keepdims=True)
                dq = dq + jnp.where(row == t, dq_t, 0.0)
                dk = dk + ce * qt
                dv = dv + a_col * dov[t:t + 1, :]
            dbl = (decay * jnp.sum(st * dst, axis=0, keepdims=True)
                   + jnp.sum(kk * dk_state, axis=0, keepdims=True))
            db = qh * dq - kk * dk + jnp.where(row == last, dbl, 0.0)
            dg = _nn(tri_t, db, precision=HIGHEST)
            df = dg / f - dk
            sq = _sigmoid(zq)
            dq_ref[pl.ds(t0, c16), :] = dq * (sq + zq * sq * (1.0 - sq))
            dz_ref[pl.ds(t0, c16), :] = df * (1.0 - lbv) * s * (1.0 - s)
            dv_ref[pl.ds(t0, c16), :] = dv
            dlb = dlb + jnp.sum(df * (1.0 - s), axis=0, keepdims=True)
            dst_new = decay * dst + _tn(do16, qe.astype(BF16))
            return dst_new, dlb

        _, dlb = lax.fori_loop(0, n_chunks, step,
                               (jnp.zeros((HG_DK, HG_DK), F32), jnp.zeros((1, HG_DK), F32)))
        dlb_ref[...] = dlb

    cspec = lambda col: pl.BlockSpec((lp, HG_DK), lambda h: (0, col // HG_DK + h))
    ospec = pl.BlockSpec((lp, HG_DK), lambda h: (0, h))
    sds = jax.ShapeDtypeStruct((lp, hh * HG_DK), F32)
    return pl.pallas_call(
        body, grid=(hh,),
        in_specs=[cspec(col_q), cspec(col_z), cspec(col_i),
                  pl.BlockSpec((None, 1, HG_DK), lambda h: (h, 0, 0)),
                  pl.BlockSpec((None, n_chunks, HG_DK, HG_DK), lambda h: (h, 0, 0, 0)),
                  ospec],
        out_specs=[ospec, ospec, ospec, pl.BlockSpec((None, 1, HG_DK), lambda h: (h, 0, 0))],
        out_shape=[sds, sds, sds, jax.ShapeDtypeStruct((hh, 1, HG_DK), F32)],
        compiler_params=_cp("parallel"), name=name)(proj, proj, proj, lb, states, do)


def _na_rows(r, rows):
    rs = jnp.clip(r - NA_WIN_H // 2, 0, rows - NA_WIN_H)
    i0 = rs - r + (NA_WIN_H - 1)
    q0 = pl.multiple_of(N_META + GRID_W * r, 16)
    k0 = pl.multiple_of(N_META + GRID_W * rs, 16)
    return i0, q0, k0


def _na_scores(q16, k16, km16, tb_ref, i0, scale):
    s = _nt(q16, k16) * scale
    bias = jnp.concatenate([tb_ref[i0 + j] for j in range(NA_WIN_H)], axis=1)
    return s + bias, _nt(q16, km16) * scale


def _na_fwd(qkv, tb, *, n_tok, name):
    _, nh, lp, dh = qkv.shape
    rows = n_tok // GRID_W
    scale = dh ** -0.5
    kw = NA_WIN_H * GRID_W

    def body(q_ref, k_ref, v_ref, tb_ref, o_ref, lse_ref):
        o_ref[...] = jnp.zeros_like(o_ref)
        lse_ref[...] = jnp.zeros_like(lse_ref)
        km = k_ref[0:N_META, :].astype(BF16)
        vm = v_ref[0:N_META, :].astype(BF16)
        s = _nt(q_ref[0:N_META, :].astype(BF16), km) * scale
        m = jnp.max(s, axis=1, keepdims=True)
        p = jnp.exp(s - m)
        l = jnp.sum(p, axis=1, keepdims=True)
        o_ref[0:N_META, :] = _nn(p.astype(BF16), vm) / l
        lse_ref[0:N_META, :] = m + jnp.log(l)

        def step(r, carry):
            i0, q0, k0 = _na_rows(r, rows)
            q16 = q_ref[pl.ds(q0, GRID_W), :].astype(BF16)
            k16 = k_ref[pl.ds(k0, kw), :].astype(BF16)
            v16 = v_ref[pl.ds(k0, kw), :].astype(BF16)
            s, sm = _na_scores(q16, k16, km, tb_ref, i0, scale)
            m = jnp.maximum(jnp.max(s, axis=1, keepdims=True), jnp.max(sm, axis=1, keepdims=True))
            p = jnp.exp(s - m)
            pm = jnp.exp(sm - m)
            l = jnp.sum(p, axis=1, keepdims=True) + jnp.sum(pm, axis=1, keepdims=True)
            o = _nn(p.astype(BF16), v16) + _nn(pm.astype(BF16), vm)
            o_ref[pl.ds(q0, GRID_W), :] = o / l
            lse_ref[pl.ds(q0, GRID_W), :] = m + jnp.log(l)
            return carry

        lax.fori_loop(0, rows, step, 0)

    hspec = lambda which: pl.BlockSpec((None, None, lp, dh), lambda h: (which, h, 0, 0))
    return pl.pallas_call(
        body, grid=(nh,),
        in_specs=[hspec(0), hspec(1), hspec(2),
                  pl.BlockSpec((None, 2 * NA_WIN_H - 1, GRID_W, GRID_W), lambda h: (h, 0, 0, 0))],
        out_specs=[pl.BlockSpec((None, lp, dh), lambda h: (h, 0, 0)),
                   pl.BlockSpec((None, lp, 1), lambda h: (h, 0, 0))],
        out_shape=[jax.ShapeDtypeStruct((nh, lp, dh), F32), jax.ShapeDtypeStruct((nh, lp, 1), F32)],
        compiler_params=_cp("parallel"), name=name)(qkv, qkv, qkv, tb)


def _na_bwd(qkv, tb, o, lse, do, *, n_tok, name):
    _, nh, lp, dh = qkv.shape
    rows = n_tok // GRID_W
    scale = dh ** -0.5
    kw = NA_WIN_H * GRID_W

    def body(q_ref, k_ref, v_ref, tb_ref, o_ref, lse_ref, do_ref, dq_ref, dk_ref, dv_ref, dtb_ref):
        dq_ref[...] = jnp.zeros_like(dq_ref)
        dk_ref[...] = jnp.zeros_like(dk_ref)
        dv_ref[...] = jnp.zeros_like(dv_ref)
        dtb_ref[...] = jnp.zeros_like(dtb_ref)
        km = k_ref[0:N_META, :].astype(BF16)
        vm = v_ref[0:N_META, :].astype(BF16)
        qm = q_ref[0:N_META, :].astype(BF16)
        dom = do_ref[0:N_META, :]
        p = jnp.exp(_nt(qm, km) * scale - lse_ref[0:N_META, :])
        dp = _nt(dom.astype(BF16), vm)
        delta = jnp.sum(dom * o_ref[0:N_META, :], axis=1, keepdims=True)
        ds = (p * (dp - delta)).astype(BF16)
        dq_ref[0:N_META, :] = _nn(ds, km) * scale
        dkm0 = _tn(ds, qm) * scale
        dvm0 = _tn(p.astype(BF16), dom.astype(BF16))

        def step(r, carry):
            dkm, dvm = carry
            i0, q0, k0 = _na_rows(r, rows)
            q16 = q_ref[pl.ds(q0, GRID_W), :].astype(BF16)
            k16 = k_ref[pl.ds(k0, kw), :].astype(BF16)
            v16 = v_ref[pl.ds(k0, kw), :].astype(BF16)
            s, sm = _na_scores(q16, k16, km, tb_ref, i0, scale)
            lse = lse_ref[pl.ds(q0, GRID_W), :]
            p = jnp.exp(s - lse)
            pm = jnp.exp(sm - lse)
            dov = do_ref[pl.ds(q0, GRID_W), :]
            do16 = dov.astype(BF16)
            delta = jnp.sum(dov * o_ref[pl.ds(q0, GRID_W), :], axis=1, keepdims=True)
            ds = p * (_nt(do16, v16) - delta)
            dsm = (pm * (_nt(do16, vm) - delta)).astype(BF16)
            ds16 = ds.astype(BF16)
            dq_ref[pl.ds(q0, GRID_W), :] = (_nn(ds16, k16) + _nn(dsm, km)) * scale
            dk_ref[pl.ds(k0, kw), :] += _tn(ds16, q16) * scale
            dv_ref[pl.ds(k0, kw), :] += _tn(p.astype(BF16), do16)
            for j in range(NA_WIN_H):
                dtb_ref[i0 + j] += ds[:, j * GRID_W:(j + 1) * GRID_W]
            return (dkm + _tn(dsm, q16) * scale, dvm + _tn(pm.astype(BF16), do16))

        dkm, dvm = lax.fori_loop(0, rows, step, (dkm0, dvm0))
        dk_ref[0:N_META, :] += dkm
        dv_ref[0:N_META, :] += dvm

    hspec = lambda which: pl.BlockSpec((None, None, lp, dh), lambda h: (which, h, 0, 0))
    h3 = pl.BlockSpec((None, lp, dh), lambda h: (h, 0, 0))
    tbs = pl.BlockSpec((None, 2 * NA_WIN_H - 1, GRID_W, GRID_W), lambda h: (h, 0, 0, 0))
    sds = jax.ShapeDtypeStruct((nh, lp, dh), F32)
    return pl.pallas_call(
        body, grid=(nh,),
        in_specs=[hspec(0), hspec(1), hspec(2), tbs, h3,
                  pl.BlockSpec((None, lp, 1), lambda h: (h, 0, 0)), h3],
        out_specs=[h3, h3, h3, tbs],
        out_shape=[sds, sds, sds, jax.ShapeDtypeStruct(tb.shape, F32)],
        compiler_params=_cp("parallel"), name=name)(qkv, qkv, qkv, tb, o, lse, do)


def _rpb_onehot():
    c = np.arange(GRID_W)[:, None]
    w = np.arange(GRID_W)[None, :]
    cs = np.clip(c - NA_WIN_W // 2, 0, GRID_W - NA_WIN_W)
    in_win = (w >= cs) & (w < cs + NA_WIN_W)
    dc = np.clip(w - c, -(NA_WIN_W - 1), NA_WIN_W - 1) + NA_WIN_W - 1
    oh = np.zeros((LANES, GRID_W * GRID_W), np.float32)
    flat = np.arange(GRID_W * GRID_W).reshape(GRID_W, GRID_W)
    oh[dc[in_win], flat[in_win]] = 1.0
    neg = np.where(in_win, 0.0, -1e30).astype(np.float32).reshape(1, -1)
    return oh, neg


def _assemble_dproj(dqkv, dq_f, dq_b, dz_f, dz_b, dv_f, dv_b, dg, dgn, dgh, *, name):
    _, nh, lp, dh = dqkv.shape
    naw = nh * dh
    hgw = dq_f.shape[1]
    d = dgn.shape[1]
    cols = 3 * naw + 5 * hgw + 2 * d
    tr = _row_tile(lp)

    def body(na_ref, qf_ref, qb_ref, zf_ref, zb_ref, vf_ref, vb_ref, g_ref, gn_ref, gh_ref, o_ref):
        for which in range(3):
            for h in range(nh):
                c0 = which * naw + h * dh
                o_ref[:, c0:c0 + dh] = na_ref[which, h].astype(BF16)
        c0 = 3 * naw
        o_ref[:, c0:c0 + hgw] = (qf_ref[...] + qb_ref[...]).astype(BF16)
        o_ref[:, c0 + hgw:c0 + 2 * hgw] = zf_ref[...].astype(BF16)
        o_ref[:, c0 + 2 * hgw:c0 + 3 * hgw] = zb_ref[...].astype(BF16)
        o_ref[:, c0 + 3 * hgw:c0 + 4 * hgw] = (vf_ref[...] + vb_ref[...]).astype(BF16)
        o_ref[:, c0 + 4 * hgw:c0 + 5 * hgw] = g_ref[...]
        o_ref[:, c0 + 5 * hgw:c0 + 5 * hgw + d] = gn_ref[...]
        o_ref[:, c0 + 5 * hgw + d:] = gh_ref[...]

    hg = _rspec(tr, hgw)
    return pl.pallas_call(
        body, grid=(lp // tr,),
        in_specs=[pl.BlockSpec((3, nh, tr, dh), lambda i: (0, 0, i, 0)),
                  hg, hg, hg, hg, hg, hg, hg, _rspec(tr, d), _rspec(tr, d)],
        out_specs=_rspec(tr, cols),
        out_shape=jax.ShapeDtypeStruct((lp, cols), BF16),
        compiler_params=_cp("parallel"), name=name)(dqkv, dq_f, dq_b, dz_f, dz_b, dv_f, dv_b, dg,
                                                    dgn, dgh)


def _adamw(w, g, m, v, *, name):
    rows, cols = w.shape
    tr = 256 if rows % 256 == 0 else rows

    def body(w_ref, g_ref, m_ref, v_ref, d_ref, mo_ref, vo_ref):
        gv = g_ref[...]
        mn = ADAM_B1 * m_ref[...] + (1.0 - ADAM_B1) * gv
        vn = ADAM_B2 * v_ref[...] + (1.0 - ADAM_B2) * (gv * gv)
        m_hat = mn / (1.0 - ADAM_B1 ** ADAM_STEP)
        v_hat = vn / (1.0 - ADAM_B2 ** ADAM_STEP)
        d_ref[...] = -ADAM_LR * (m_hat / (jnp.sqrt(v_hat) + ADAM_EPS) + ADAM_WD * w_ref[...])
        mo_ref[...] = mn
        vo_ref[...] = vn

    spec = _rspec(tr, cols)
    sds = jax.ShapeDtypeStruct((rows, cols), F32)
    return pl.pallas_call(
        body, grid=(rows // tr,), in_specs=[spec] * 4, out_specs=[spec] * 3, out_shape=[sds] * 3,
        compiler_params=_cp("parallel"), name=name)(w, g, m, v)


def _local_step(x, tgt, meta, w_in, w_na, w_hg, w_o, w_up, w_down, g_mix, g_mlp, g_fin, hg_gain,
                rpb, lb):
    n_tok, d = x.shape
    naw, hgw = w_na.shape[0], w_hg.shape[0]
    nh, hh = naw // NA_HEAD_DIM, hgw // HG_DK
    l_real = N_META + n_tok
    lp = -(-l_real // ROW_ALIGN) * ROW_ALIGN
    n_chunks = l_real // HG_CHUNK
    pad = lp - l_real
    col_qhg = 3 * naw
    col_zf, col_zb, col_i, col_g = (col_qhg + hgw, col_qhg + 2 * hgw, col_qhg + 3 * hgw,
                                    col_qhg + 4 * hgw)
    col_gate = col_qhg + 5 * hgw

    zpad = jnp.zeros((pad, d), F32)
    h0 = jnp.concatenate([meta, x, zpad], axis=0)
    tgt_p = jnp.concatenate([jnp.zeros((N_META, d), F32), tgt, zpad], axis=0)

    oh_np, neg_np = _rpb_onehot()
    oh = jnp.asarray(oh_np)
    rpb_p = jnp.pad(rpb.reshape(nh * (2 * NA_WIN_H - 1), 2 * NA_WIN_W - 1),
                    ((0, 0), (0, LANES - (2 * NA_WIN_W - 1))))
    tb = _matmul(rpb_p, oh, tm=rpb_p.shape[0], tn=512, tk=LANES, precision=HIGHEST,
                 name="rpb_expand")
    tb = (tb + jnp.asarray(neg_np)).reshape(nh, 2 * NA_WIN_H - 1, GRID_W, GRID_W)

    a = _rmsnorm_fwd(h0, g_mix, name="norm_mix")
    proj = _matmul(a, w_in, name="mm_in")
    qkv = proj[:, :3 * naw].reshape(lp, 3, nh, NA_HEAD_DIM).transpose(1, 2, 0, 3)
    o_na_hm, lse = _na_fwd(qkv, tb, n_tok=n_tok, name="na_fwd")
    o_na = o_na_hm.transpose(1, 0, 2).reshape(lp, naw)
    lb_f = lb[0].reshape(hh, 1, HG_DK)
    lb_b = lb[1].reshape(hh, 1, HG_DK)
    scan_kw = dict(n_chunks=n_chunks, col_q=col_qhg, col_i=col_i, hh=hh)
    o_f, st_f = _hg_scan_fwd(proj, lb_f, reverse=False, col_z=col_zf, name="hg_scan_f", **scan_kw)
    o_b, st_b = _hg_scan_fwd(proj, lb_b, reverse=True, col_z=col_zb, name="hg_scan_b", **scan_kw)
    o_hg = _hg_out(o_f, o_b, proj, hg_gain, col_g=col_g, name="hg_out")
    y_na = _matmul(o_na, w_na, name="mm_na_out")
    y_hg = _matmul(o_hg, w_hg, name="mm_hg_out")
    mix = _gate_mix(proj, y_na, y_hg, col_gate=col_gate, name="gate_mix")
    t1 = _matmul(mix, w_o, name="mm_o")
    h1, mlp_in = _residual_norm(h0, t1, g_mlp, name="resid_norm_mlp")
    u = _matmul(mlp_in, w_up, name="mm_up")
    act = _relu2(u, name="relu2")
    t2 = _matmul(act, w_down, name="mm_down")
    dh2, loss, dg_fin = _final_loss(h1, t2, g_fin, tgt_p, n_tok=n_tok, name="final_loss")

    dact = _matmul(dh2, w_down, tb=True, name="mm_down_dx")
    dw_down = _matmul(act, dh2, ta=True, name="mm_down_dw")
    du = _relu2_bwd(dact, u, name="relu2_bwd")
    dm = _matmul(du, w_up, tb=True, name="mm_up_dx")
    dw_up = _matmul(mlp_in, du, ta=True, name="mm_up_dw")
    dh1, dg_mlp = _rmsnorm_bwd_add(h1, g_mlp, dm, dh2, name="norm_mlp_bwd")
    dmix = _matmul(dh1, w_o, tb=True, name="mm_o_dx")
    dw_o = _matmul(mix, dh1, ta=True, name="mm_o_dw")
    dy_na, dy_hg, dgn, dgh = _gate_mix_bwd(proj, y_na, y_hg, dmix, col_gate=col_gate,
                                           name="gate_mix_bwd")
    do_na = _matmul(dy_na, w_na, tb=True, name="mm_na_out_dx")
    dw_na = _matmul(o_na, dy_na, ta=True, name="mm_na_out_dw")
    do_hg = _matmul(dy_hg, w_hg, tb=True, name="mm_hg_out_dx")
    dw_hg = _matmul(o_hg, dy_hg, ta=True, name="mm_hg_out_dw")
    d_o, dg_hg, d_gain = _hg_out_bwd(o_f, o_b, proj, hg_gain, do_hg, col_g=col_g, name="hg_out_bwd")
    dq_f, dz_f, dv_f, dlb_f = _hg_scan_bwd(proj, lb_f, st_f, d_o, reverse=False, col_z=col_zf,
                                           name="hg_scan_f_bwd", **scan_kw)
    dq_b, dz_b, dv_b, dlb_b = _hg_scan_bwd(proj, lb_b, st_b, d_o, reverse=True, col_z=col_zb,
                                           name="hg_scan_b_bwd", **scan_kw)
    do_na_hm = do_na.reshape(lp, nh, NA_HEAD_DIM).transpose(1, 0, 2)
    dq_na, dk_na, dv_na, dtb = _na_bwd(qkv, tb, o_na_hm, lse, do_na_hm, n_tok=n_tok, name="na_bwd")
    dqkv = jnp.stack([dq_na, dk_na, dv_na], axis=0)
    dproj = _assemble_dproj(dqkv, dq_f, dq_b, dz_f, dz_b, dv_f, dv_b, dg_hg, dgn, dgh,
                            name="assemble_dproj")
    da = _matmul(dproj, w_in, tb=True, name="mm_in_dx")
    dw_in = _matmul(a, dproj, ta=True, name="mm_in_dw")
    dh0, dg_mix = _rmsnorm_bwd_add(h0, g_mix, da, dh1, name="norm_mix_bwd")
    d_rpb = _matmul(dtb.reshape(nh * (2 * NA_WIN_H - 1), GRID_W * GRID_W), oh, tb=True,
                    tm=nh * (2 * NA_WIN_H - 1), tn=LANES, tk=1024, precision=HIGHEST,
                    name="rpb_reduce")
    d_lb = jnp.concatenate([dlb_f.reshape(1, hgw), dlb_b.reshape(1, hgw)], axis=0)
    return (loss, dh0[N_META:l_real], dh0[:N_META], dw_in, dw_na, dw_hg, dw_o, dw_up, dw_down,
            dg_mix, dg_mlp, dg_fin, d_gain, d_rpb, d_lb)


N_CHIPS = 4
N_DEV = 8
ANY = pl.BlockSpec(memory_space=pl.ANY)


def _place():
    x, y, c = lax.axis_index("x"), lax.axis_index("y"), lax.axis_index("c")
    others = []
    for j in (1, 2, 3):
        tx = (1 - x) if (j >> 1) else x
        ty = (1 - y) if (j & 1) else y
        others.append((tx, ty))
    return x, y, c, others


def _piece(ref, axis, k, half, rh, cs):
    if axis == 1:
        return ref.at[pl.ds(pl.multiple_of(half * rh, 16), rh), pl.ds(pl.multiple_of(k * cs, LANES), cs)]
    return ref.at[pl.ds(pl.multiple_of(k * 2 * rh + half * rh, 16), rh), :]


def _weight_allgather(shards, axes, *, name):
    n = len(shards)
    geo = []
    out_shape = []
    for s, ax in zip(shards, axes):
        r, cs = s.shape
        full = (r, cs * N_CHIPS) if ax == 1 else (r * N_CHIPS, cs)
        geo.append((ax, r // 2, cs))
        out_shape.append(jax.ShapeDtypeStruct(full, s.dtype))

    def body(*refs):
        s_refs, o_refs = refs[:n], refs[n:2 * n]
        send_sems, recv_sems, local_sems = refs[2 * n:]
        x, y, c, others = _place()
        chip = 2 * x + y
        sib = (x, y, 1 - c)

        def rcopy(i, slot, src, dst, to):
            return pltpu.make_async_remote_copy(
                src_ref=src, dst_ref=dst, send_sem=send_sems.at[6 * i + slot],
                recv_sem=recv_sems.at[6 * i + slot], device_id=to, device_id_type=MESH)

        local = []
        for i in range(n):
            ax, rh, cs = geo[i]
            for half in range(2):
                cp = pltpu.make_async_copy(s_refs[i].at[pl.ds(half * rh, rh), :],
                                           _piece(o_refs[i], ax, chip, half, rh, cs),
                                           local_sems.at[2 * i + half])
                cp.start()
                local.append(cp)
        started = []
        for i in range(n):
            ax, rh, cs = geo[i]
            mine = _piece(o_refs[i], ax, chip, c, rh, cs)
            for j, (tx, ty) in enumerate(others):
                cp = rcopy(i, j, s_refs[i].at[pl.ds(pl.multiple_of(c * rh, 16), rh), :], mine,
                           (tx, ty, c))
                cp.start()
                started.append(cp)
        for i in range(n):
            ax, rh, cs = geo[i]
            for j, (tx, ty) in enumerate(others):
                got = _piece(o_refs[i], ax, 2 * tx + ty, c, rh, cs)
                rcopy(i, j, got, got, (x, y, c)).wait_recv()
                cp = rcopy(i, 3 + j, got, got, sib)
                cp.start()
                started.append(cp)
        for i in range(n):
            ax, rh, cs = geo[i]
            for j, (tx, ty) in enumerate(others):
                got = _piece(o_refs[i], ax, 2 * tx + ty, 1 - c, rh, cs)
                rcopy(i, 3 + j, got, got, (x, y, c)).wait_recv()
        for cp in started:
            cp.wait_send()
        for cp in local:
            cp.wait()

    return pl.pallas_call(
        body, in_specs=[ANY] * n, out_specs=[ANY] * n, out_shape=out_shape,
        scratch_shapes=[pltpu.SemaphoreType.DMA((6 * n,)), pltpu.SemaphoreType.DMA((6 * n,)),
                        pltpu.SemaphoreType.DMA((2 * n,))],
        name=name)(*shards)


def _sibling_swap(grads, *, name):
    n = len(grads)
    out_shape = [jax.ShapeDtypeStruct((g.shape[0], g.shape[1] // 2, g.shape[2]), g.dtype)
                 for g in grads]

    def body(*refs):
        g_refs, o_refs = refs[:n], refs[n:2 * n]
        send_sems, recv_sems = refs[2 * n:]
        x, y, c, _ = _place()
        cps = []
        for i in range(n):
            rh = grads[i].shape[1] // 2
            src = g_refs[i].at[:, pl.ds(pl.multiple_of((1 - c) * rh, 16), rh), :]
            cp = pltpu.make_async_remote_copy(
                src_ref=src, dst_ref=o_refs[i], send_sem=send_sems.at[i], recv_sem=recv_sems.at[i],
                device_id=(x, y, 1 - c), device_id_type=MESH)
            cp.start()
            cps.append(cp)
        for cp in cps:
            cp.wait()

    return pl.pallas_call(
        body, in_specs=[ANY] * n, out_specs=[ANY] * n, out_shape=out_shape,
        scratch_shapes=[pltpu.SemaphoreType.DMA((n,)), pltpu.SemaphoreType.DMA((n,))],
        name=name)(*grads)


def _pair_add(g3, rx, c_arr, *, out_dtype, name):
    nb, rows, cols = g3.shape
    rh = rows // 2
    tr = next(t for t in (128, 64, 32, 16) if rh % t == 0)
    nt = rh // tr

    def body(c_ref, g_ref, r_ref, o_ref):
        o_ref[...] = (g_ref[...] + r_ref[...]).astype(out_dtype)

    return pl.pallas_call(
        body,
        grid_spec=pltpu.PrefetchScalarGridSpec(
            num_scalar_prefetch=1, grid=(nb, nt),
            in_specs=[pl.BlockSpec((None, tr, cols), lambda b, i, c_ref: (b, c_ref[0] * nt + i, 0)),
                      pl.BlockSpec((None, tr, cols), lambda b, i, c_ref: (b, i, 0))],
            out_specs=pl.BlockSpec((None, tr, cols), lambda b, i, c_ref: (b, i, 0))),
        out_shape=jax.ShapeDtypeStruct((nb, rh, cols), out_dtype),
        compiler_params=_cp("parallel", "parallel"), name=name)(c_arr, g3, rx)


def _chip_scatter(parts, axes, *, name):
    n = len(parts)
    geo, out_shape = [], []
    for p, ax in zip(parts, axes):
        nb, rh, cols = p.shape
        cw = cols // N_CHIPS if ax == 1 else cols
        geo.append((ax, rh, cw))
        out_shape.append(jax.ShapeDtypeStruct((N_CHIPS, rh, cw), p.dtype))

    def body(*refs):
        p_refs, o_refs = refs[:n], refs[n:2 * n]
        send_sems, recv_sems, local_sems = refs[2 * n:]
        x, y, c, others = _place()
        chip = 2 * x + y

        def block(i, k):
            ax, rh, cw = geo[i]
            if ax == 1:
                return p_refs[i].at[0, :, pl.ds(pl.multiple_of(k * cw, LANES), cw)]
            return p_refs[i].at[k]

        cps, local = [], []
        for i in range(n):
            cp = pltpu.make_async_copy(block(i, chip), o_refs[i].at[chip], local_sems.at[i])
            cp.start()
            local.append(cp)
            for j, (tx, ty) in enumerate(others):
                cp = pltpu.make_async_remote_copy(
                    src_ref=block(i, 2 * tx + ty), dst_ref=o_refs[i].at[chip],
                    send_sem=send_sems.at[3 * i + j], recv_sem=recv_sems.at[3 * i + j],
                    device_id=(tx, ty, c), device_id_type=MESH)
                cp.start()
                cps.append(cp)
        for i in range(n):
            for j, (tx, ty) in enumerate(others):
                got = o_refs[i].at[2 * tx + ty]
                pltpu.make_async_remote_copy(
                    src_ref=got, dst_ref=got, send_sem=send_sems.at[3 * i + j],
                    recv_sem=recv_sems.at[3 * i + j], device_id=(x, y, c),
                    device_id_type=MESH).wait_recv()
        for cp in cps:
            cp.wait_send()
        for cp in local:
            cp.wait()

    return pl.pallas_call(
        body, in_specs=[ANY] * n, out_specs=[ANY] * n, out_shape=out_shape,
        scratch_shapes=[pltpu.SemaphoreType.DMA((3 * n,)), pltpu.SemaphoreType.DMA((3 * n,)),
                        pltpu.SemaphoreType.DMA((n,))],
        name=name)(*parts)


def _sum_slots(q, *, name):
    ns, rows, cols = q.shape
    tr = next(t for t in (128, 64, 32, 16, 8) if rows % t == 0)

    def body(q_ref, o_ref):
        acc = q_ref[0].astype(F32)
        for k in range(1, ns):
            acc = acc + q_ref[k].astype(F32)
        o_ref[...] = acc

    return pl.pallas_call(
        body, grid=(rows // tr,),
        in_specs=[pl.BlockSpec((ns, tr, cols), lambda i: (0, i, 0))],
        out_specs=_rspec(tr, cols),
        out_shape=jax.ShapeDtypeStruct((rows, cols), F32),
        compiler_params=_cp("parallel"), name=name)(q)


def _sibling_share(halves, *, name):
    n = len(halves)
    out_shape = [jax.ShapeDtypeStruct((2 * h.shape[0], h.shape[1]), h.dtype) for h in halves]

    def body(*refs):
        h_refs, o_refs = refs[:n], refs[n:2 * n]
        send_sems, recv_sems, local_sems = refs[2 * n:]
        x, y, c, _ = _place()
        cps, local = [], []
        for i in range(n):
            rh = halves[i].shape[0]
            mine = o_refs[i].at[pl.ds(pl.multiple_of(c * rh, 8), rh), :]
            cp = pltpu.make_async_copy(h_refs[i], mine, local_sems.at[i])
            cp.start()
            local.append(cp)
            cp = pltpu.make_async_remote_copy(
                src_ref=h_refs[i], dst_ref=mine, send_sem=send_sems.at[i], recv_sem=recv_sems.at[i],
                device_id=(x, y, 1 - c), device_id_type=MESH)
            cp.start()
            cps.append(cp)
        for i in range(n):
            rh = halves[i].shape[0]
            theirs = o_refs[i].at[pl.ds(pl.multiple_of((1 - c) * rh, 8), rh), :]
            pltpu.make_async_remote_copy(
                src_ref=theirs, dst_ref=theirs, send_sem=send_sems.at[i], recv_sem=recv_sems.at[i],
                device_id=(x, y, c), device_id_type=MESH).wait_recv()
        for cp in cps:
            cp.wait_send()
        for cp in local:
            cp.wait()

    return pl.pallas_call(
        body, in_specs=[ANY] * n, out_specs=[ANY] * n, out_shape=out_shape,
        scratch_shapes=[pltpu.SemaphoreType.DMA((n,)), pltpu.SemaphoreType.DMA((n,)),
                        pltpu.SemaphoreType.DMA((n,))],
        name=name)(*halves)


def _gather_all(blk, *, name):
    rows, cols = blk.shape

    def body(x_ref, out_ref, send_sems, recv_sems, local_sem):
        x, y, c = lax.axis_index("x"), lax.axis_index("y"), lax.axis_index("c")
        me = 4 * x + 2 * y + c
        mine = pltpu.make_async_copy(x_ref, out_ref.at[me], local_sem)
        mine.start()
        cps = []
        for k in range(1, N_DEV):
            tx = (1 - x) if (k >> 2) & 1 else x
            ty = (1 - y) if (k >> 1) & 1 else y
            tc = (1 - c) if k & 1 else c
            cp = pltpu.make_async_remote_copy(
                src_ref=x_ref, dst_ref=out_ref.at[me], send_sem=send_sems.at[k - 1],
                recv_sem=recv_sems.at[k - 1], device_id=(tx, ty, tc), device_id_type=MESH)
            cp.start()
            cps.append(cp)
        for k in range(1, N_DEV):
            tx = (1 - x) if (k >> 2) & 1 else x
            ty = (1 - y) if (k >> 1) & 1 else y
            tc = (1 - c) if k & 1 else c
            got = out_ref.at[4 * tx + 2 * ty + tc]
            pltpu.make_async_remote_copy(
                src_ref=got, dst_ref=got, send_sem=send_sems.at[k - 1], recv_sem=recv_sems.at[k - 1],
                device_id=(x, y, c), device_id_type=MESH).wait_recv()
        for cp in cps:
            cp.wait_send()
        mine.wait()

    vm = pl.BlockSpec(memory_space=pltpu.VMEM)
    return pl.pallas_call(
        body, in_specs=[vm], out_specs=vm,
        out_shape=jax.ShapeDtypeStruct((N_DEV, rows, cols), blk.dtype),
        scratch_shapes=[pltpu.SemaphoreType.DMA((N_DEV - 1,)), pltpu.SemaphoreType.DMA((N_DEV - 1,)),
                        pltpu.SemaphoreType.DMA],
        name=name)(blk)


def _as_rows(a):
    flat = a.reshape(-1)
    n = flat.shape[0]
    rows = -(-n // (8 * LANES)) * 8
    return jnp.pad(flat, (0, rows * LANES - n)).reshape(rows, LANES)


def _from_rows(p, shape):
    n = int(np.prod(shape))
    return p.reshape(-1)[:n].reshape(shape)


WEIGHT_AXES = (1, 1, 1, 0, 1, 0)
WIRE = BF16


def kernel(x, meta_tokens, w_in, w_na_out, w_hg_out, w_o, w_up, w_down, norm_mix, norm_mlp, norm_final, hg_norm, na_rpb, hg_lb_logits, loss_target, m_meta_tokens, m_w_in, m_w_na_out, m_w_hg_out, m_w_o, m_w_up, m_w_down, m_norm_mix, m_norm_mlp, m_norm_final, m_hg_norm, m_na_rpb, m_hg_lb_logits, v_meta_tokens, v_w_in, v_w_na_out, v_w_hg_out, v_w_o, v_w_up, v_w_down, v_norm_mix, v_norm_mlp, v_norm_final, v_hg_norm, v_na_rpb, v_hg_lb_logits):
    xi, yi, ci = lax.axis_index("x"), lax.axis_index("y"), lax.axis_index("c")
    chip = 2 * xi + yi
    d = x.shape[-1]
    dshard = meta_tokens.shape[1]
    hgw = hg_norm.shape[1]
    lbs = hg_lb_logits.shape[2]
    big = [w_in[0], w_na_out[0], w_hg_out[0], w_o[0], w_up[0], w_down[0]]
    big_m = [m_w_in[0], m_w_na_out[0], m_w_hg_out[0], m_w_o[0], m_w_up[0], m_w_down[0]]
    big_v = [v_w_in[0], v_w_na_out[0], v_w_hg_out[0], v_w_o[0], v_w_up[0], v_w_down[0]]

    full_w = _weight_allgather([w.astype(BF16) for w in big], WEIGHT_AXES, name="weight_allgather")
    small_in = jnp.concatenate([_as_rows(meta_tokens), _as_rows(hg_lb_logits)], axis=0)
    small_all = _gather_all(small_in, name="gather_small_params")[0::2]
    n_meta_rows = N_META * dshard // LANES
    meta_full = (small_all[:, :n_meta_rows].reshape(N_CHIPS, N_META, dshard)
                 .transpose(1, 0, 2).reshape(N_META, d))
    lbl_full = (small_all[:, n_meta_rows:].reshape(N_CHIPS, -1)[:, :4 * lbs]
                .reshape(N_CHIPS, 2, 2, lbs).transpose(1, 2, 0, 3).reshape(2, 2, N_CHIPS * lbs))
    lb = jax.nn.softmax(lbl_full, axis=1)[:, 0]

    (loss, dx, dmeta, *dws, dg_mix, dg_mlp, dg_fin, d_gain, d_rpb, d_lb) = _local_step(
        x[0], loss_target[0], meta_full, *full_w, norm_mix, norm_mlp, norm_final.reshape(1, d),
        hg_norm, na_rpb[0], lb)

    c_arr = ci.reshape(1).astype(jnp.int32)
    g3 = [g.reshape(1, *g.shape) if ax == 1 else g.reshape(N_CHIPS, g.shape[0] // N_CHIPS, g.shape[1])
          for g, ax in zip(dws, WEIGHT_AXES)]
    rx = _sibling_swap(g3, name="grad_sibling_swap")
    parts = [_pair_add(g, r, c_arr, out_dtype=WIRE, name=f"grad_pair_add_{i}")
             for i, (g, r) in enumerate(zip(g3, rx))]
    slots = _chip_scatter(parts, WEIGHT_AXES, name="grad_chip_scatter")
    halves = [_sum_slots(q, name=f"grad_sum_chips_{i}") for i, q in enumerate(slots)]
    g_big = _sibling_share(halves, name="grad_sibling_share")

    nrpb = int(np.prod(na_rpb.shape))
    d_rpb_c = d_rpb[:, :2 * NA_WIN_W - 1]
    small_g = [dmeta, dg_mix, dg_mlp, dg_fin, d_gain, d_rpb_c, d_lb, loss]
    packed = jnp.concatenate([_as_rows(a) for a in small_g], axis=0)
    total = _sum_slots(_gather_all(packed, name="gather_small_grads"), name="sum_small_grads")
    offs = np.cumsum([0] + [_as_rows(a).shape[0] for a in small_g])
    take = lambda i, shape: _from_rows(total[offs[i]:offs[i + 1]], shape)
    g_meta_full = take(0, (N_META, d))
    g_norm_mix, g_norm_mlp = take(1, (1, d)), take(2, (1, d))
    g_norm_final = take(3, (d,))
    g_hg_norm = take(4, (1, hgw))
    g_rpb = take(5, na_rpb.shape)
    g_lb = take(6, (2, hgw))
    loss_total = take(7, (1, LANES))[0, 0]
    g_meta = lax.dynamic_slice_in_dim(g_meta_full, chip * dshard, dshard, axis=1)
    dl0 = lb * (1.0 - lb) * g_lb
    g_lbl_full = jnp.stack([dl0, -dl0], axis=1)
    g_lbl = lax.dynamic_slice_in_dim(g_lbl_full, chip * lbs, lbs, axis=2)

    big_out = [_adamw(w, g, m, v, name=f"adamw_{i}")
               for i, (w, g, m, v) in enumerate(zip(big, g_big, big_m, big_v))]
    small_w = [meta_tokens, norm_mix, norm_mlp, norm_final, hg_norm, na_rpb, hg_lb_logits]
    small_gr = [g_meta, g_norm_mix, g_norm_mlp, g_norm_final, g_hg_norm, g_rpb, g_lbl]
    small_m = [m_meta_tokens, m_norm_mix, m_norm_mlp, m_norm_final, m_hg_norm, m_na_rpb, m_hg_lb_logits]
    small_v = [v_meta_tokens, v_norm_mix, v_norm_mlp, v_norm_final, v_hg_norm, v_na_rpb, v_hg_lb_logits]
    pk = lambda lst: jnp.concatenate([_as_rows(a) for a in lst], axis=0)
    sd, sm, sv = _adamw(pk(small_w), pk(small_gr), pk(small_m), pk(small_v), name="adamw_small")
    soffs = np.cumsum([0] + [_as_rows(a).shape[0] for a in small_w])
    unpk = lambda p: [_from_rows(p[soffs[i]:soffs[i + 1]], small_w[i].shape) for i in range(len(small_w))]
    sd, sm, sv = unpk(sd), unpk(sm), unpk(sv)

    def order(bigs, smalls):
        return [smalls[0]] + [b.reshape(1, *b.shape) for b in bigs] + smalls[1:]

    grads = order(g_big, small_gr)
    deltas = order([o[0] for o in big_out], sd)
    new_m = order([o[1] for o in big_out], sm)
    new_v = order([o[2] for o in big_out], sv)
    return (loss_total, dx.reshape(1, *dx.shape), *grads, *deltas, *new_m, *new_v)
```

```python
import functools

import numpy as np
import jax
import jax.numpy as jnp
from jax import lax
from jax.experimental import pallas as pl
from jax.experimental.pallas import tpu as pltpu

F32 = jnp.float32
BF16 = jnp.bfloat16
HIGHEST = lax.Precision.HIGHEST

GRID_W = 64
N_META = 16
EPS = 1e-6
NA_HEAD_DIM = 64
NA_WIN_H = 8
NA_WIN_W = 16
HG_DK = 128
HG_CHUNK = 16
LANES = 128
ROW_ALIGN = 128
VMEM_LIMIT = 48 * 1024 * 1024

ADAM_LR = 0.001
ADAM_B1 = 0.9
ADAM_B2 = 0.999
ADAM_EPS = 1e-08
ADAM_WD = 0.01
ADAM_STEP = 10

MESH = pl.DeviceIdType.MESH


def _cp(*sem):
    return pltpu.CompilerParams(dimension_semantics=sem, vmem_limit_bytes=VMEM_LIMIT)


def _sigmoid(x):
    return 1.0 / (1.0 + jnp.exp(-x))


def _dot(a, b, dims, precision=None):
    return lax.dot_general(a, b, (dims, ((), ())), preferred_element_type=F32, precision=precision)


def _nn(a, b, **kw):
    return _dot(a, b, ((1,), (0,)), **kw)


def _nt(a, b, **kw):
    return _dot(a, b, ((1,), (1,)), **kw)


def _tn(a, b, **kw):
    return _dot(a, b, ((0,), (0,)), **kw)


def _matmul(a, b, *, ta=False, tb=False, tm=None, tn=None, tk=None, out_dtype=F32, name,
            precision=None):
    if ta:
        kdim, m = a.shape
    else:
        m, kdim = a.shape
    if tb:
        n, k2 = b.shape
    else:
        k2, n = b.shape
    assert kdim == k2, (a.shape, b.shape, ta, tb)
    if tm is None:
        if ta:
            tm = next(t for t in (512, 256, 128, m) if m % t == 0)
        else:
            tm = m // 2 if (m // 2) % 16 == 0 and m > 512 else m
    if tn is None:
        tn = next(t for t in (512, 256, 128, n) if n % t == 0)
    if tk is None:
        tk = kdim if ta else next(t for t in (1024, 512, 256, 128, kdim) if kdim % t == 0)
    assert m % tm == 0 and n % tn == 0 and kdim % tk == 0, (m, n, kdim, tm, tn, tk)
    nk = kdim // tk
    op_dtype = F32 if precision is not None else BF16

    def body(a_ref, b_ref, o_ref, acc_ref):
        kk = pl.program_id(2)

        @pl.when(kk == 0)
        def _():
            acc_ref[...] = jnp.zeros_like(acc_ref)

        av = a_ref[...].astype(op_dtype)
        bv = b_ref[...].astype(op_dtype)
        dims = ((0 if ta else 1,), (1 if tb else 0,))
        acc_ref[...] += _dot(av, bv, dims, precision=precision)

        @pl.when(kk == nk - 1)
        def _():
            o_ref[...] = acc_ref[...].astype(out_dtype)

    a_spec = (pl.BlockSpec((tk, tm), lambda i, j, k: (k, i)) if ta
              else pl.BlockSpec((tm, tk), lambda i, j, k: (i, k)))
    b_spec = (pl.BlockSpec((tn, tk), lambda i, j, k: (j, k)) if tb
              else pl.BlockSpec((tk, tn), lambda i, j, k: (k, j)))
    return pl.pallas_call(
        body,
        grid=(m // tm, n // tn, nk),
        in_specs=[a_spec, b_spec],
        out_specs=pl.BlockSpec((tm, tn), lambda i, j, k: (i, j)),
        out_shape=jax.ShapeDtypeStruct((m, n), out_dtype),
        scratch_shapes=[pltpu.VMEM((tm, tn), F32)],
        compiler_params=_cp("parallel", "parallel", "arbitrary"),
        name=name,
    )(a, b)


def _rspec(tr, w, cb=0):
    return pl.BlockSpec((tr, w), lambda i: (i, cb))


def _fspec(shape):
    nd = len(shape)
    return pl.BlockSpec(shape, lambda i: (0,) * nd)


def _row_tile(lp):
    return ROW_ALIGN if lp % ROW_ALIGN == 0 else lp


def _rmsnorm_fwd(x, g, *, name):
    lp, d = x.shape
    tr = _row_tile(lp)

    def body(x_ref, g_ref, o_ref):
        xv = x_ref[...]
        r = lax.rsqrt(jnp.mean(xv * xv, axis=-1, keepdims=True) + EPS)
        o_ref[...] = (xv * r * g_ref[...]).astype(BF16)

    return pl.pallas_call(
        body, grid=(lp // tr,),
        in_specs=[_rspec(tr, d), _fspec((1, d))],
        out_specs=_rspec(tr, d),
        out_shape=jax.ShapeDtypeStruct((lp, d), BF16),
        compiler_params=_cp("parallel"), name=name)(x, g)


def _residual_norm(h, t, g, *, name):
    lp, d = h.shape
    tr = _row_tile(lp)

    def body(h_ref, t_ref, g_ref, h1_ref, m_ref):
        xv = h_ref[...] + t_ref[...]
        h1_ref[...] = xv
        r = lax.rsqrt(jnp.mean(xv * xv, axis=-1, keepdims=True) + EPS)
        m_ref[...] = (xv * r * g_ref[...]).astype(BF16)

    return pl.pallas_call(
        body, grid=(lp // tr,),
        in_specs=[_rspec(tr, d), _rspec(tr, d), _fspec((1, d))],
        out_specs=[_rspec(tr, d), _rspec(tr, d)],
        out_shape=[jax.ShapeDtypeStruct((lp, d), F32), jax.ShapeDtypeStruct((lp, d), BF16)],
        compiler_params=_cp("parallel"), name=name)(h, t, g)


def _rmsnorm_bwd_add(x, g, dy, dres, *, name):
    lp, d = x.shape
    tr = _row_tile(lp)

    def body(x_ref, g_ref, dy_ref, dr_ref, dx_ref, dg_ref):
        @pl.when(pl.program_id(0) == 0)
        def _():
            dg_ref[...] = jnp.zeros_like(dg_ref)

        xv = x_ref[...]
        r = lax.rsqrt(jnp.mean(xv * xv, axis=-1, keepdims=True) + EPS)
        xh = xv * r
        dyv = dy_ref[...]
        dg_ref[...] += jnp.sum(dyv * xh, axis=0, keepdims=True)
        dxh = dyv * g_ref[...]
        dx_ref[...] = dr_ref[...] + r * (dxh - xh * jnp.mean(dxh * xh, axis=-1, keepdims=True))

    return pl.pallas_call(
        body, grid=(lp // tr,),
        in_specs=[_rspec(tr, d), _fspec((1, d)), _rspec(tr, d), _rspec(tr, d)],
        out_specs=[_rspec(tr, d), _fspec((1, d))],
        out_shape=[jax.ShapeDtypeStruct((lp, d), F32), jax.ShapeDtypeStruct((1, d), F32)],
        compiler_params=_cp("arbitrary"), name=name)(x, g, dy, dres)


def _final_loss(h1, t2, g, tgt, *, n_tok, name):
    lp, d = h1.shape
    tr = _row_tile(lp)

    def body(h_ref, t_ref, g_ref, tg_ref, dh_ref, loss_ref, dg_ref):
        i = pl.program_id(0)

        @pl.when(i == 0)
        def _():
            loss_ref[...] = jnp.zeros_like(loss_ref)
            dg_ref[...] = jnp.zeros_like(dg_ref)

        xv = h_ref[...] + t_ref[...]
        r = lax.rsqrt(jnp.mean(xv * xv, axis=-1, keepdims=True) + EPS)
        xh = xv * r
        gv = g_ref[...]
        row = i * tr + lax.broadcasted_iota(jnp.int32, (tr, 1), 0)
        valid = (row >= N_META) & (row < N_META + n_tok)
        err = jnp.where(valid, xh * gv - tg_ref[...], 0.0)
        loss_ref[...] += jnp.sum(0.5 * err * err) / d
        dy = err / d
        dg_ref[...] += jnp.sum(dy * xh, axis=0, keepdims=True)
        dxh = dy * gv
        dh_ref[...] = r * (dxh - xh * jnp.mean(dxh * xh, axis=-1, keepdims=True))

    return pl.pallas_call(
        body, grid=(lp // tr,),
        in_specs=[_rspec(tr, d), _rspec(tr, d), _fspec((1, d)), _rspec(tr, d)],
        out_specs=[_rspec(tr, d), _fspec((1, LANES)), _fspec((1, d))],
        out_shape=[jax.ShapeDtypeStruct((lp, d), F32), jax.ShapeDtypeStruct((1, LANES), F32),
                   jax.ShapeDtypeStruct((1, d), F32)],
        compiler_params=_cp("arbitrary"), name=name)(h1, t2, g, tgt)


def _relu2(u, *, name):
    lp, f = u.shape
    tr = _row_tile(lp)

    def body(u_ref, o_ref):
        rv = jnp.maximum(u_ref[...], 0.0)
        o_ref[...] = (rv * rv).astype(BF16)

    return pl.pallas_call(
        body, grid=(lp // tr,), in_specs=[_rspec(tr, f)], out_specs=_rspec(tr, f),
        out_shape=jax.ShapeDtypeStruct((lp, f), BF16),
        compiler_params=_cp("parallel"), name=name)(u)


def _relu2_bwd(dact, u, *, name):
    lp, f = u.shape
    tr = _row_tile(lp)

    def body(d_ref, u_ref, o_ref):
        o_ref[...] = (d_ref[...] * 2.0 * jnp.maximum(u_ref[...], 0.0)).astype(BF16)

    return pl.pallas_call(
        body, grid=(lp // tr,), in_specs=[_rspec(tr, f), _rspec(tr, f)], out_specs=_rspec(tr, f),
        out_shape=jax.ShapeDtypeStruct((lp, f), BF16),
        compiler_params=_cp("parallel"), name=name)(dact, u)


def _gate_mix(proj, y_na, y_hg, *, col_gate, name):
    lp, d = y_na.shape
    tr = _row_tile(lp)
    cb = col_gate // d

    def body(gn_ref, gh_ref, yn_ref, yh_ref, o_ref):
        o_ref[...] = (_sigmoid(gn_ref[...]) * yn_ref[...]
                      + _sigmoid(gh_ref[...]) * yh_ref[...]).astype(BF16)

    return pl.pallas_call(
        body, grid=(lp // tr,),
        in_specs=[_rspec(tr, d, cb), _rspec(tr, d, cb + 1), _rspec(tr, d), _rspec(tr, d)],
        out_specs=_rspec(tr, d),
        out_shape=jax.ShapeDtypeStruct((lp, d), BF16),
        compiler_params=_cp("parallel"), name=name)(proj, proj, y_na, y_hg)


def _gate_mix_bwd(proj, y_na, y_hg, dmix, *, col_gate, name):
    lp, d = y_na.shape
    tr = _row_tile(lp)
    cb = col_gate // d

    def body(gn_ref, gh_ref, yn_ref, yh_ref, dm_ref, dyn_ref, dyh_ref, dgn_ref, dgh_ref):
        dm = dm_ref[...]
        sn = _sigmoid(gn_ref[...])
        sh = _sigmoid(gh_ref[...])
        dyn_ref[...] = (dm * sn).astype(BF16)
        dyh_ref[...] = (dm * sh).astype(BF16)
        dgn_ref[...] = (dm * yn_ref[...] * sn * (1.0 - sn)).astype(BF16)
        dgh_ref[...] = (dm * yh_ref[...] * sh * (1.0 - sh)).astype(BF16)

    sds = jax.ShapeDtypeStruct((lp, d), BF16)
    return pl.pallas_call(
        body, grid=(lp // tr,),
        in_specs=[_rspec(tr, d, cb), _rspec(tr, d, cb + 1), _rspec(tr, d), _rspec(tr, d),
                  _rspec(tr, d)],
        out_specs=[_rspec(tr, d)] * 4, out_shape=[sds] * 4,
        compiler_params=_cp("parallel"), name=name)(proj, proj, y_na, y_hg, dmix)


def _hg_out(o_f, o_b, proj, gain, *, col_g, name):
    lp, w = o_f.shape
    tr = _row_tile(lp)
    hh = w // HG_DK

    def body(of_ref, ob_ref, g_ref, gain_ref, y_ref):
        gv = g_ref[...]
        sg = gv * _sigmoid(gv)
        for h in range(hh):
            sl = slice(h * HG_DK, (h + 1) * HG_DK)
            o = of_ref[:, sl] + ob_ref[:, sl]
            r = lax.rsqrt(jnp.mean(o * o, axis=-1, keepdims=True) + EPS)
            y_ref[:, sl] = (o * r * gain_ref[:, sl] * sg[:, sl]).astype(BF16)

    return pl.pallas_call(
        body, grid=(lp // tr,),
        in_specs=[_rspec(tr, w), _rspec(tr, w), _rspec(tr, w, col_g // w), _fspec((1, w))],
        out_specs=_rspec(tr, w),
        out_shape=jax.ShapeDtypeStruct((lp, w), BF16),
        compiler_params=_cp("parallel"), name=name)(o_f, o_b, proj, gain)


def _hg_out_bwd(o_f, o_b, proj, gain, dy, *, col_g, name):
    lp, w = o_f.shape
    tr = _row_tile(lp)
    hh = w // HG_DK

    def body(of_ref, ob_ref, g_ref, gain_ref, dy_ref, do_ref, dg_ref, dgain_ref):
        @pl.when(pl.program_id(0) == 0)
        def _():
            dgain_ref[...] = jnp.zeros_like(dgain_ref)

        for h in range(hh):
            sl = slice(h * HG_DK, (h + 1) * HG_DK)
            gv = g_ref[:, sl]
            s = _sigmoid(gv)
            sg = gv * s
            dsg = s + gv * s * (1.0 - s)
            o = of_ref[:, sl] + ob_ref[:, sl]
            r = lax.rsqrt(jnp.mean(o * o, axis=-1, keepdims=True) + EPS)
            on = o * r
            dyv = dy_ref[:, sl]
            gn = gain_ref[:, sl]
            dgain_ref[:, sl] += jnp.sum(dyv * on * sg, axis=0, keepdims=True)
            dg_ref[:, sl] = (dyv * on * gn * dsg).astype(BF16)
            don = dyv * gn * sg
            do_ref[:, sl] = r * (don - on * jnp.mean(don * on, axis=-1, keepdims=True))

    return pl.pallas_call(
        body, grid=(lp // tr,),
        in_specs=[_rspec(tr, w), _rspec(tr, w), _rspec(tr, w, col_g // w), _fspec((1, w)),
                  _rspec(tr, w)],
        out_specs=[_rspec(tr, w), _rspec(tr, w), _fspec((1, w))],
        out_shape=[jax.ShapeDtypeStruct((lp, w), F32), jax.ShapeDtypeStruct((lp, w), BF16),
                   jax.ShapeDtypeStruct((1, w), F32)],
        compiler_params=_cp("arbitrary"), name=name)(o_f, o_b, proj, gain, dy)


def _hg_gates(zq, z, lbv):
    qh = zq * _sigmoid(zq)
    s = _sigmoid(z)
    f = lbv + (1.0 - lbv) * s
    kk = (1.0 - lbv) * _sigmoid(-z)
    return qh, s, f, jnp.log(f), kk


def _tri(reverse):
    r = lax.broadcasted_iota(jnp.int32, (HG_CHUNK, HG_CHUNK), 0)
    c = lax.broadcasted_iota(jnp.int32, (HG_CHUNK, HG_CHUNK), 1)
    return jnp.where((c >= r) if reverse else (c <= r), 1.0, 0.0).astype(F32)


def _hg_scan_fwd(proj, lb, *, reverse, n_chunks, col_q, col_z, col_i, hh, name):
    lp = proj.shape[0]
    c16 = HG_CHUNK
    last = 0 if reverse else c16 - 1

    def body(q_ref, z_ref, i_ref, lb_ref, o_ref, st_ref):
        o_ref[...] = jnp.zeros_like(o_ref)
        lbv = lb_ref[...]
        tri = _tri(reverse)
        row = lax.broadcasted_iota(jnp.int32, (c16, HG_DK), 0)

        def step(j, st):
            jj = (n_chunks - 1 - j) if reverse else j
            t0 = pl.multiple_of(jj * c16, c16)
            v = i_ref[pl.ds(t0, c16), :]
            qh, _, _, g, kk = _hg_gates(q_ref[pl.ds(t0, c16), :], z_ref[pl.ds(t0, c16), :], lbv)
            b = _nn(tri, g, precision=HIGHEST)
            st_ref[jj] = st
            bl = b[last:last + 1, :]
            qe = qh * jnp.exp(b)
            kd = kk * jnp.exp(bl - b)
            o = _nt(qe.astype(BF16), st.astype(BF16))
            for t in range(c16):
                mask = (row >= t) if reverse else (row <= t)
                e = jnp.where(mask, jnp.exp(jnp.minimum(b[t:t + 1, :] - b, 0.0)), 0.0)
                a_col = jnp.sum(qh[t:t + 1, :] * kk * e, axis=1, keepdims=True)
                o_t = jnp.sum(a_col * v, axis=0, keepdims=True)
                o = o + jnp.where(row == t, o_t, 0.0)
            o_ref[pl.ds(t0, c16), :] = o
            return jnp.exp(bl) * st + _tn(v.astype(BF16), kd.astype(BF16))

        lax.fori_loop(0, n_chunks, step, jnp.zeros((HG_DK, HG_DK), F32))

    cspec = lambda col: pl.BlockSpec((lp, HG_DK), lambda h: (0, col // HG_DK + h))
    return pl.pallas_call(
        body, grid=(hh,),
        in_specs=[cspec(col_q), cspec(col_z), cspec(col_i),
                  pl.BlockSpec((None, 1, HG_DK), lambda h: (h, 0, 0))],
        out_specs=[pl.BlockSpec((lp, HG_DK), lambda h: (0, h)),
                   pl.BlockSpec((None, n_chunks, HG_DK, HG_DK), lambda h: (h, 0, 0, 0))],
        out_shape=[jax.ShapeDtypeStruct((lp, hh * HG_DK), F32),
                   jax.ShapeDtypeStruct((hh, n_chunks, HG_DK, HG_DK), F32)],
        compiler_params=_cp("parallel"), name=name)(proj, proj, proj, lb)


def _hg_scan_bwd(proj, lb, states, do, *, reverse, n_chunks, col_q, col_z, col_i, hh, name):
    lp = proj.shape[0]
    c16 = HG_CHUNK
    last = 0 if reverse else c16 - 1

    def body(q_ref, z_ref, i_ref, lb_ref, st_ref, do_ref, dq_ref, dz_ref, dv_ref, dlb_ref):
        dq_ref[...] = jnp.zeros_like(dq_ref)
        dz_ref[...] = jnp.zeros_like(dz_ref)
        dv_ref[...] = jnp.zeros_like(dv_ref)
        lbv = lb_ref[...]
        tri = _tri(reverse)
        tri_t = _tri(not reverse)
        row = lax.broadcasted_iota(jnp.int32, (c16, HG_DK), 0)

        def step(j, carry):
            dst, dlb = carry
            jj = j if reverse else (n_chunks - 1 - j)
            t0 = pl.multiple_of(jj * c16, c16)
            zq = q_ref[pl.ds(t0, c16), :]
            v = i_ref[pl.ds(t0, c16), :]
            dov = do_ref[pl.ds(t0, c16), :]
            qh, s, f, g, kk = _hg_gates(zq, z_ref[pl.ds(t0, c16), :], lbv)
            b = _nn(tri, g, precision=HIGHEST)
            bl = b[last:last + 1, :]
            eb = jnp.exp(b)
            ebl = jnp.exp(bl - b)
            decay = jnp.exp(bl)
            qe = qh * eb
            kd = kk * ebl
            st = st_ref[jj]
            v16, do16 = v.astype(BF16), dov.astype(BF16)
            st16, dst16 = st.astype(BF16), dst.astype(BF16)
            do_s = _nn(do16, st16)
            da_t = _nt(v16, do16)
            v_ds = _nn(v16, dst16)
            dv = _nt(kd.astype(BF16), dst16)
            dq = eb * do_s
            dk_state = ebl * v_ds
            dk = dk_state
            for t in range(c16):
                mask = (row >= t) if reverse else (row <= t)
                e = jnp.where(mask, jnp.exp(jnp.minimum(b[t:t + 1, :] - b, 0.0)), 0.0)
                qt = qh[t:t + 1, :]
                a_col = jnp.sum(qt * kk * e, axis=1, keepdims=True)
                c_t = da_t[:, t:t + 1]
                ce = c_t * e
                dq_t = jnp.sum(ce * kk, axis=0, keepdims=True)
                dq = dq + jnp.where(row == t, dq_t, 0.0)
                dk = dk + ce * qt
                dv = dv + a_col * dov[t:t + 1, :]
            dbl = (decay * jnp.sum(st * dst, axis=0, keepdims=True)
                   + jnp.sum(kk * dk_state, axis=0, keepdims=True))
            db = qh * dq - kk * dk + jnp.where(row == last, dbl, 0.0)
            dg = _nn(tri_t, db, precision=HIGHEST)
            df = dg / f - dk
            sq = _sigmoid(zq)
            dq_ref[pl.ds(t0, c16), :] = dq * (sq + zq * sq * (1.0 - sq))
            dz_ref[pl.ds(t0, c16), :] = df * (1.0 - lbv) * s * (1.0 - s)
            dv_ref[pl.ds(t0, c16), :] = dv
            dlb = dlb + jnp.sum(df * (1.0 - s), axis=0, keepdims=True)
            dst_new = decay * dst + _tn(do16, qe.astype(BF16))
            return dst_new, dlb

        _, dlb = lax.fori_loop(0, n_chunks, step,
                               (jnp.zeros((HG_DK, HG_DK), F32), jnp.zeros((1, HG_DK), F32)))
        dlb_ref[...] = dlb

    cspec = lambda col: pl.BlockSpec((lp, HG_DK), lambda h: (0, col // HG_DK + h))
    ospec = pl.BlockSpec((lp, HG_DK), lambda h: (0, h))
    sds = jax.ShapeDtypeStruct((lp, hh * HG_DK), F32)
    return pl.pallas_call(
        body, grid=(hh,),
        in_specs=[cspec(col_q), cspec(col_z), cspec(col_i),
                  pl.BlockSpec((None, 1, HG_DK), lambda h: (h, 0, 0)),
                  pl.BlockSpec((None, n_chunks, HG_DK, HG_DK), lambda h: (h, 0, 0, 0)),
                  ospec],
        out_specs=[ospec, ospec, ospec, pl.BlockSpec((None, 1, HG_DK), lambda h: (h, 0, 0))],
        out_shape=[sds, sds, sds, jax.ShapeDtypeStruct((hh, 1, HG_DK), F32)],
        compiler_params=_cp("parallel"), name=name)(proj, proj, proj, lb, states, do)


def _na_rows(r, rows):
    rs = jnp.clip(r - NA_WIN_H // 2, 0, rows - NA_WIN_H)
    i0 = rs - r + (NA_WIN_H - 1)
    q0 = pl.multiple_of(N_META + GRID_W * r, 16)
    k0 = pl.multiple_of(N_META + GRID_W * rs, 16)
    return i0, q0, k0


def _na_scores(q16, k16, km16, tb_ref, i0, scale):
    s = _nt(q16, k16) * scale
    bias = jnp.concatenate([tb_ref[i0 + j] for j in range(NA_WIN_H)], axis=1)
    return s + bias, _nt(q16, km16) * scale


def _na_fwd(qkv, tb, *, n_tok, name):
    _, nh, lp, dh = qkv.shape
    rows = n_tok // GRID_W
    scale = dh ** -0.5
    kw = NA_WIN_H * GRID_W

    def body(q_ref, k_ref, v_ref, tb_ref, o_ref, lse_ref):
        o_ref[...] = jnp.zeros_like(o_ref)
        lse_ref[...] = jnp.zeros_like(lse_ref)
        km = k_ref[0:N_META, :].astype(BF16)
        vm = v_ref[0:N_META, :].astype(BF16)
        s = _nt(q_ref[0:N_META, :].astype(BF16), km) * scale
        m = jnp.max(s, axis=1, keepdims=True)
        p = jnp.exp(s - m)
        l = jnp.sum(p, axis=1, keepdims=True)
        o_ref[0:N_META, :] = _nn(p.astype(BF16), vm) / l
        lse_ref[0:N_META, :] = m + jnp.log(l)

        def step(r, carry):
            i0, q0, k0 = _na_rows(r, rows)
            q16 = q_ref[pl.ds(q0, GRID_W), :].astype(BF16)
            k16 = k_ref[pl.ds(k0, kw), :].astype(BF16)
            v16 = v_ref[pl.ds(k0, kw), :].astype(BF16)
            s, sm = _na_scores(q16, k16, km, tb_ref, i0, scale)
            m = jnp.maximum(jnp.max(s, axis=1, keepdims=True), jnp.max(sm, axis=1, keepdims=True))
            p = jnp.exp(s - m)
            pm = jnp.exp(sm - m)
            l = jnp.sum(p, axis=1, keepdims=True) + jnp.sum(pm, axis=1, keepdims=True)
            o = _nn(p.astype(BF16), v16) + _nn(pm.astype(BF16), vm)
            o_ref[pl.ds(q0, GRID_W), :] = o / l
            lse_ref[pl.ds(q0, GRID_W), :] = m + jnp.log(l)
            return carry

        lax.fori_loop(0, rows, step, 0)

    hspec = lambda which: pl.BlockSpec((None, None, lp, dh), lambda h: (which, h, 0, 0))
    return pl.pallas_call(
        body, grid=(nh,),
        in_specs=[hspec(0), hspec(1), hspec(2),
                  pl.BlockSpec((None, 2 * NA_WIN_H - 1, GRID_W, GRID_W), lambda h: (h, 0, 0, 0))],
        out_specs=[pl.BlockSpec((None, lp, dh), lambda h: (h, 0, 0)),
                   pl.BlockSpec((None, lp, 1), lambda h: (h, 0, 0))],
        out_shape=[jax.ShapeDtypeStruct((nh, lp, dh), F32), jax.ShapeDtypeStruct((nh, lp, 1), F32)],
        compiler_params=_cp("parallel"), name=name)(qkv, qkv, qkv, tb)


def _na_bwd(qkv, tb, o, lse, do, *, n_tok, name):
    _, nh, lp, dh = qkv.shape
    rows = n_tok // GRID_W
    scale = dh ** -0.5
    kw = NA_WIN_H * GRID_W

    def body(q_ref, k_ref, v_ref, tb_ref, o_ref, lse_ref, do_ref, dq_ref, dk_ref, dv_ref, dtb_ref):
        dq_ref[...] = jnp.zeros_like(dq_ref)
        dk_ref[...] = jnp.zeros_like(dk_ref)
        dv_ref[...] = jnp.zeros_like(dv_ref)
        dtb_ref[...] = jnp.zeros_like(dtb_ref)
        km = k_ref[0:N_META, :].astype(BF16)
        vm = v_ref[0:N_META, :].astype(BF16)
        qm = q_ref[0:N_META, :].astype(BF16)
        dom = do_ref[0:N_META, :]
        p = jnp.exp(_nt(qm, km) * scale - lse_ref[0:N_META, :])
        dp = _nt(dom.astype(BF16), vm)
        delta = jnp.sum(dom * o_ref[0:N_META, :], axis=1, keepdims=True)
        ds = (p * (dp - delta)).astype(BF16)
        dq_ref[0:N_META, :] = _nn(ds, km) * scale
        dkm0 = _tn(ds, qm) * scale
        dvm0 = _tn(p.astype(BF16), dom.astype(BF16))

        def step(r, carry):
            dkm, dvm = carry
            i0, q0, k0 = _na_rows(r, rows)
            q16 = q_ref[pl.ds(q0, GRID_W), :].astype(BF16)
            k16 = k_ref[pl.ds(k0, kw), :].astype(BF16)
            v16 = v_ref[pl.ds(k0, kw), :].astype(BF16)
            s, sm = _na_scores(q16, k16, km, tb_ref, i0, scale)
            lse = lse_ref[pl.ds(q0, GRID_W), :]
            p = jnp.exp(s - lse)
            pm = jnp.exp(sm - lse)
            dov = do_ref[pl.ds(q0, GRID_W), :]
            do16 = dov.astype(BF16)
            delta = jnp.sum(dov * o_ref[pl.ds(q0, GRID_W), :], axis=1, keepdims=True)
            ds = p * (_nt(do16, v16) - delta)
            dsm = (pm * (_nt(do16, vm) - delta)).astype(BF16)
            ds16 = ds.astype(BF16)
            dq_ref[pl.ds(q0, GRID_W), :] = (_nn(ds16, k16) + _nn(dsm, km)) * scale
            dk_ref[pl.ds(k0, kw), :] += _tn(ds16, q16) * scale
            dv_ref[pl.ds(k0, kw), :] += _tn(p.astype(BF16), do16)
            for j in range(NA_WIN_H):
                dtb_ref[i0 + j] += ds[:, j * GRID_W:(j + 1) * GRID_W]
            return (dkm + _tn(dsm, q16) * scale, dvm + _tn(pm.astype(BF16), do16))

        dkm, dvm = lax.fori_loop(0, rows, step, (dkm0, dvm0))
        dk_ref[0:N_META, :] += dkm
        dv_ref[0:N_META, :] += dvm

    hspec = lambda which: pl.BlockSpec((None, None, lp, dh), lambda h: (which, h, 0, 0))
    h3 = pl.BlockSpec((None, lp, dh), lambda h: (h, 0, 0))
    tbs = pl.BlockSpec((None, 2 * NA_WIN_H - 1, GRID_W, GRID_W), lambda h: (h, 0, 0, 0))
    sds = jax.ShapeDtypeStruct((nh, lp, dh), F32)
    return pl.pallas_call(
        body, grid=(nh,),
        in_specs=[hspec(0), hspec(1), hspec(2), tbs, h3,
                  pl.BlockSpec((None, lp, 1), lambda h: (h, 0, 0)), h3],
        out_specs=[h3, h3, h3, tbs],
        out_shape=[sds, sds, sds, jax.ShapeDtypeStruct(tb.shape, F32)],
        compiler_params=_cp("parallel"), name=name)(qkv, qkv, qkv, tb, o, lse, do)


def _rpb_onehot():
    c = np.arange(GRID_W)[:, None]
    w = np.arange(GRID_W)[None, :]
    cs = np.clip(c - NA_WIN_W // 2, 0, GRID_W - NA_WIN_W)
    in_win = (w >= cs) & (w < cs + NA_WIN_W)
    dc = np.clip(w - c, -(NA_WIN_W - 1), NA_WIN_W - 1) + NA_WIN_W - 1
    oh = np.zeros((LANES, GRID_W * GRID_W), np.float32)
    flat = np.arange(GRID_W * GRID_W).reshape(GRID_W, GRID_W)
    oh[dc[in_win], flat[in_win]] = 1.0
    neg = np.where(in_win, 0.0, -1e30).astype(np.float32).reshape(1, -1)
    return oh, neg


def _assemble_dproj(dqkv, dq_f, dq_b, dz_f, dz_b, dv_f, dv_b, dg, dgn, dgh, *, name):
    _, nh, lp, dh = dqkv.shape
    naw = nh * dh
    hgw = dq_f.shape[1]
    d = dgn.shape[1]
    cols = 3 * naw + 5 * hgw + 2 * d
    tr = _row_tile(lp)

    def body(na_ref, qf_ref, qb_ref, zf_ref, zb_ref, vf_ref, vb_ref, g_ref, gn_ref, gh_ref, o_ref):
        for which in range(3):
            for h in range(nh):
                c0 = which * naw + h * dh
                o_ref[:, c0:c0 + dh] = na_ref[which, h].astype(BF16)
        c0 = 3 * naw
        o_ref[:, c0:c0 + hgw] = (qf_ref[...] + qb_ref[...]).astype(BF16)
        o_ref[:, c0 + hgw:c0 + 2 * hgw] = zf_ref[...].astype(BF16)
        o_ref[:, c0 + 2 * hgw:c0 + 3 * hgw] = zb_ref[...].astype(BF16)
        o_ref[:, c0 + 3 * hgw:c0 + 4 * hgw] = (vf_ref[...] + vb_ref[...]).astype(BF16)
        o_ref[:, c0 + 4 * hgw:c0 + 5 * hgw] = g_ref[...]
        o_ref[:, c0 + 5 * hgw:c0 + 5 * hgw + d] = gn_ref[...]
        o_ref[:, c0 + 5 * hgw + d:] = gh_ref[...]

    hg = _rspec(tr, hgw)
    return pl.pallas_call(
        body, grid=(lp // tr,),
        in_specs=[pl.BlockSpec((3, nh, tr, dh), lambda i: (0, 0, i, 0)),
                  hg, hg, hg, hg, hg, hg, hg, _rspec(tr, d), _rspec(tr, d)],
        out_specs=_rspec(tr, cols),
        out_shape=jax.ShapeDtypeStruct((lp, cols), BF16),
        compiler_params=_cp("parallel"), name=name)(dqkv, dq_f, dq_b, dz_f, dz_b, dv_f, dv_b, dg,
                                                    dgn, dgh)


def _adamw(w, g, m, v, *, name):
    rows, cols = w.shape
    tr = 256 if rows % 256 == 0 else rows

    def body(w_ref, g_ref, m_ref, v_ref, d_ref, mo_ref, vo_ref):
        gv = g_ref[...]
        mn = ADAM_B1 * m_ref[...] + (1.0 - ADAM_B1) * gv
        vn = ADAM_B2 * v_ref[...] + (1.0 - ADAM_B2) * (gv * gv)
        m_hat = mn / (1.0 - ADAM_B1 ** ADAM_STEP)
        v_hat = vn / (1.0 - ADAM_B2 ** ADAM_STEP)
        d_ref[...] = -ADAM_LR * (m_hat / (jnp.sqrt(v_hat) + ADAM_EPS) + ADAM_WD * w_ref[...])
        mo_ref[...] = mn
        vo_ref[...] = vn

    spec = _rspec(tr, cols)
    sds = jax.ShapeDtypeStruct((rows, cols), F32)
    return pl.pallas_call(
        body, grid=(rows // tr,), in_specs=[spec] * 4, out_specs=[spec] * 3, out_shape=[sds] * 3,
        compiler_params=_cp("parallel"), name=name)(w, g, m, v)


def _local_step(x, tgt, meta, w_in, w_na, w_hg, w_o, w_up, w_down, g_mix, g_mlp, g_fin, hg_gain,
                rpb, lb):
    n_tok, d = x.shape
    naw, hgw = w_na.shape[0], w_hg.shape[0]
    nh, hh = naw // NA_HEAD_DIM, hgw // HG_DK
    l_real = N_META + n_tok
    lp = -(-l_real // ROW_ALIGN) * ROW_ALIGN
    n_chunks = l_real // HG_CHUNK
    pad = lp - l_real
    col_qhg = 3 * naw
    col_zf, col_zb, col_i, col_g = (col_qhg + hgw, col_qhg + 2 * hgw, col_qhg + 3 * hgw,
                                    col_qhg + 4 * hgw)
    col_gate = col_qhg + 5 * hgw

    zpad = jnp.zeros((pad, d), F32)
    h0 = jnp.concatenate([meta, x, zpad], axis=0)
    tgt_p = jnp.concatenate([jnp.zeros((N_META, d), F32), tgt, zpad], axis=0)

    oh_np, neg_np = _rpb_onehot()
    oh = jnp.asarray(oh_np)
    rpb_p = jnp.pad(rpb.reshape(nh * (2 * NA_WIN_H - 1), 2 * NA_WIN_W - 1),
                    ((0, 0), (0, LANES - (2 * NA_WIN_W - 1))))
    tb = _matmul(rpb_p, oh, tm=rpb_p.shape[0], tn=512, tk=LANES, precision=HIGHEST,
                 name="rpb_expand")
    tb = (tb + jnp.asarray(neg_np)).reshape(nh, 2 * NA_WIN_H - 1, GRID_W, GRID_W)

    a = _rmsnorm_fwd(h0, g_mix, name="norm_mix")
    proj = _matmul(a, w_in, name="mm_in")
    qkv = proj[:, :3 * naw].reshape(lp, 3, nh, NA_HEAD_DIM).transpose(1, 2, 0, 3)
    o_na_hm, lse = _na_fwd(qkv, tb, n_tok=n_tok, name="na_fwd")
    o_na = o_na_hm.transpose(1, 0, 2).reshape(lp, naw)
    lb_f = lb[0].reshape(hh, 1, HG_DK)
    lb_b = lb[1].reshape(hh, 1, HG_DK)
    scan_kw = dict(n_chunks=n_chunks, col_q=col_qhg, col_i=col_i, hh=hh)
    o_f, st_f = _hg_scan_fwd(proj, lb_f, reverse=False, col_z=col_zf, name="hg_scan_f", **scan_kw)
    o_b, st_b = _hg_scan_fwd(proj, lb_b, reverse=True, col_z=col_zb, name="hg_scan_b", **scan_kw)
    o_hg = _hg_out(o_f, o_b, proj, hg_gain, col_g=col_g, name="hg_out")
    y_na = _matmul(o_na, w_na, name="mm_na_out")
    y_hg = _matmul(o_hg, w_hg, name="mm_hg_out")
    mix = _gate_mix(proj, y_na, y_hg, col_gate=col_gate, name="gate_mix")
    t1 = _matmul(mix, w_o, name="mm_o")
    h1, mlp_in = _residual_norm(h0, t1, g_mlp, name="resid_norm_mlp")
    u = _matmul(mlp_in, w_up, name="mm_up")
    act = _relu2(u, name="relu2")
    t2 = _matmul(act, w_down, name="mm_down")
    dh2, loss, dg_fin = _final_loss(h1, t2, g_fin, tgt_p, n_tok=n_tok, name="final_loss")

    dact = _matmul(dh2, w_down, tb=True, name="mm_down_dx")
    dw_down = _matmul(act, dh2, ta=True, name="mm_down_dw")
    du = _relu2_bwd(dact, u, name="relu2_bwd")
    dm = _matmul(du, w_up, tb=True, name="mm_up_dx")
    dw_up = _matmul(mlp_in, du, ta=True, name="mm_up_dw")
    dh1, dg_mlp = _rmsnorm_bwd_add(h1, g_mlp, dm, dh2, name="norm_mlp_bwd")
    dmix = _matmul(dh1, w_o, tb=True, name="mm_o_dx")
    dw_o = _matmul(mix, dh1, ta=True, name="mm_o_dw")
    dy_na, dy_hg, dgn, dgh = _gate_mix_bwd(proj, y_na, y_hg, dmix, col_gate=col_gate,
                                           name="gate_mix_bwd")
    do_na = _matmul(dy_na, w_na, tb=True, name="mm_na_out_dx")
    dw_na = _matmul(o_na, dy_na, ta=True, name="mm_na_out_dw")
    do_hg = _matmul(dy_hg, w_hg, tb=True, name="mm_hg_out_dx")
    dw_hg = _matmul(o_hg, dy_hg, ta=True, name="mm_hg_out_dw")
    d_o, dg_hg, d_gain = _hg_out_bwd(o_f, o_b, proj, hg_gain, do_hg, col_g=col_g, name="hg_out_bwd")
    dq_f, dz_f, dv_f, dlb_f = _hg_scan_bwd(proj, lb_f, st_f, d_o, reverse=False, col_z=col_zf,
                                           name="hg_scan_f_bwd", **scan_kw)
    dq_b, dz_b, dv_b, dlb_b = _hg_scan_bwd(proj, lb_b, st_b, d_o, reverse=True, col_z=col_zb,
                                           name="hg_scan_b_bwd", **scan_kw)
    do_na_hm = do_na.reshape(lp, nh, NA_HEAD_DIM).transpose(1, 0, 2)
    dq_na, dk_na, dv_na, dtb = _na_bwd(qkv, tb, o_na_hm, lse, do_na_hm, n_tok=n_tok, name="na_bwd")
    dqkv = jnp.stack([dq_na, dk_na, dv_na], axis=0)
    dproj = _assemble_dproj(dqkv, dq_f, dq_b, dz_f, dz_b, dv_f, dv_b, dg_hg, dgn, dgh,
                            name="assemble_dproj")
    da = _matmul(dproj, w_in, tb=True, name="mm_in_dx")
    dw_in = _matmul(a, dproj, ta=True, name="mm_in_dw")
    dh0, dg_mix = _rmsnorm_bwd_add(h0, g_mix, da, dh1, name="norm_mix_bwd")
    d_rpb = _matmul(dtb.reshape(nh * (2 * NA_WIN_H - 1), GRID_W * GRID_W), oh, tb=True,
                    tm=nh * (2 * NA_WIN_H - 1), tn=LANES, tk=1024, precision=HIGHEST,
                    name="rpb_reduce")
    d_lb = jnp.concatenate([dlb_f.reshape(1, hgw), dlb_b.reshape(1, hgw)], axis=0)
    return (loss, dh0[N_META:l_real], dh0[:N_META], dw_in, dw_na, dw_hg, dw_o, dw_up, dw_down,
            dg_mix, dg_mlp, dg_fin, d_gain, d_rpb, d_lb)


N_CHIPS = 4
N_DEV = 8
ANY = pl.BlockSpec(memory_space=pl.ANY)


def _place():
    x, y, c = lax.axis_index("x"), lax.axis_index("y"), lax.axis_index("c")
    others = []
    for j in (1, 2, 3):
        tx = (1 - x) if (j >> 1) else x
        ty = (1 - y) if (j & 1) else y
        others.append((tx, ty))
    return x, y, c, others


def _piece(ref, axis, k, half, rh, cs):
    if axis == 1:
        return ref.at[pl.ds(pl.multiple_of(half * rh, 16), rh), pl.ds(pl.multiple_of(k * cs, LANES), cs)]
    return ref.at[pl.ds(pl.multiple_of(k * 2 * rh + half * rh, 16), rh), :]


def _cast_into_full(shard, axis, place, *, name):
    r, cs = shard.shape
    full = (r, cs * N_CHIPS) if axis == 1 else (r * N_CHIPS, cs)
    tr = next(t for t in (256, 128, 64, 32, 16) if r % t == 0)
    nt = r // tr

    def body(p_ref, s_ref, o_ref):
        o_ref[...] = s_ref[...].astype(BF16)

    if axis == 1:
        omap = lambda i, p_ref: (i, p_ref[0])
    else:
        omap = lambda i, p_ref: (p_ref[0] * nt + i, 0)
    return pl.pallas_call(
        body,
        grid_spec=pltpu.PrefetchScalarGridSpec(
            num_scalar_prefetch=1, grid=(nt,),
            in_specs=[pl.BlockSpec((tr, cs), lambda i, p_ref: (i, 0))],
            out_specs=pl.BlockSpec((tr, cs), omap)),
        out_shape=jax.ShapeDtypeStruct(full, BF16),
        compiler_params=_cp("parallel"), name=name)(place, shard)


def _weight_allgather(fulls, axes, *, name):
    n = len(fulls)
    geo = []
    for f, ax in zip(fulls, axes):
        r, cs = (f.shape[0], f.shape[1] // N_CHIPS) if ax == 1 else (f.shape[0] // N_CHIPS, f.shape[1])
        geo.append((ax, r // 2, cs))

    def body(*refs):
        o_refs = refs[n:2 * n]
        send_sems, recv_sems = refs[2 * n:]
        x, y, c, others = _place()
        chip = 2 * x + y
        sib = (x, y, 1 - c)

        def rcopy(i, slot, ref, to):
            return pltpu.make_async_remote_copy(
                src_ref=ref, dst_ref=ref, send_sem=send_sems.at[6 * i + slot],
                recv_sem=recv_sems.at[6 * i + slot], device_id=to, device_id_type=MESH)

        started = []
        for i in range(n):
            ax, rh, cs = geo[i]
            mine = _piece(o_refs[i], ax, chip, c, rh, cs)
            for j, (tx, ty) in enumerate(others):
                cp = rcopy(i, j, mine, (tx, ty, c))
                cp.start()
                started.append(cp)
        for i in range(n):
            ax, rh, cs = geo[i]
            for j, (tx, ty) in enumerate(others):
                got = _piece(o_refs[i], ax, 2 * tx + ty, c, rh, cs)
                rcopy(i, j, got, (x, y, c)).wait_recv()
                cp = rcopy(i, 3 + j, got, sib)
                cp.start()
                started.append(cp)
        for i in range(n):
            ax, rh, cs = geo[i]
            for j, (tx, ty) in enumerate(others):
                got = _piece(o_refs[i], ax, 2 * tx + ty, 1 - c, rh, cs)
                rcopy(i, 3 + j, got, (x, y, c)).wait_recv()
        for cp in started:
            cp.wait_send()

    return pl.pallas_call(
        body, in_specs=[ANY] * n, out_specs=[ANY] * n,
        out_shape=[jax.ShapeDtypeStruct(f.shape, f.dtype) for f in fulls],
        input_output_aliases={i: i for i in range(n)},
        scratch_shapes=[pltpu.SemaphoreType.DMA((6 * n,)), pltpu.SemaphoreType.DMA((6 * n,))],
        name=name)(*fulls)


def _sibling_swap(grads, *, name):
    n = len(grads)
    out_shape = [jax.ShapeDtypeStruct((g.shape[0], g.shape[1] // 2, g.shape[2]), g.dtype)
                 for g in grads]

    def body(*refs):
        g_refs, o_refs = refs[:n], refs[n:2 * n]
        send_sems, recv_sems = refs[2 * n:]
        x, y, c, _ = _place()
        cps = []
        for i in range(n):
            rh = grads[i].shape[1] // 2
            src = g_refs[i].at[:, pl.ds(pl.multiple_of((1 - c) * rh, 16), rh), :]
            cp = pltpu.make_async_remote_copy(
                src_ref=src, dst_ref=o_refs[i], send_sem=send_sems.at[i], recv_sem=recv_sems.at[i],
                device_id=(x, y, 1 - c), device_id_type=MESH)
            cp.start()
            cps.append(cp)
        for cp in cps:
            cp.wait()

    return pl.pallas_call(
        body, in_specs=[ANY] * n, out_specs=[ANY] * n, out_shape=out_shape,
        scratch_shapes=[pltpu.SemaphoreType.DMA((n,)), pltpu.SemaphoreType.DMA((n,))],
        name=name)(*grads)


def _pair_add(g3, rx, c_arr, *, out_dtype, name):
    nb, rows, cols = g3.shape
    rh = rows // 2
    tr = next(t for t in (128, 64, 32, 16) if rh % t == 0)
    nt = rh // tr

    def body(c_ref, g_ref, r_ref, o_ref):
        o_ref[...] = (g_ref[...] + r_ref[...]).astype(out_dtype)

    return pl.pallas_call(
        body,
        grid_spec=pltpu.PrefetchScalarGridSpec(
            num_scalar_prefetch=1, grid=(nb, nt),
            in_specs=[pl.BlockSpec((None, tr, cols), lambda b, i, c_ref: (b, c_ref[0] * nt + i, 0)),
                      pl.BlockSpec((None, tr, cols), lambda b, i, c_ref: (b, i, 0))],
            out_specs=pl.BlockSpec((None, tr, cols), lambda b, i, c_ref: (b, i, 0))),
        out_shape=jax.ShapeDtypeStruct((nb, rh, cols), out_dtype),
        compiler_params=_cp("parallel", "parallel"), name=name)(c_arr, g3, rx)


def _chip_scatter(parts, axes, *, name):
    n = len(parts)
    geo, out_shape = [], []
    for p, ax in zip(parts, axes):
        nb, rh, cols = p.shape
        cw = cols // N_CHIPS if ax == 1 else cols
        geo.append((ax, rh, cw))
        out_shape.append(jax.ShapeDtypeStruct((N_CHIPS, rh, cw), p.dtype))

    def body(*refs):
        p_refs, o_refs = refs[:n], refs[n:2 * n]
        send_sems, recv_sems = refs[2 * n:]
        x, y, c, others = _place()
        chip = 2 * x + y

        def block(i, k):
            ax, rh, cw = geo[i]
            if ax == 1:
                return p_refs[i].at[0, :, pl.ds(pl.multiple_of(k * cw, LANES), cw)]
            return p_refs[i].at[k]

        cps = []
        for i in range(n):
            for j, (tx, ty) in enumerate(others):
                cp = pltpu.make_async_remote_copy(
                    src_ref=block(i, 2 * tx + ty), dst_ref=o_refs[i].at[chip],
                    send_sem=send_sems.at[3 * i + j], recv_sem=recv_sems.at[3 * i + j],
                    device_id=(tx, ty, c), device_id_type=MESH)
                cp.start()
                cps.append(cp)
        for i in range(n):
            for j, (tx, ty) in enumerate(others):
                got = o_refs[i].at[2 * tx + ty]
                pltpu.make_async_remote_copy(
                    src_ref=got, dst_ref=got, send_sem=send_sems.at[3 * i + j],
                    recv_sem=recv_sems.at[3 * i + j], device_id=(x, y, c),
                    device_id_type=MESH).wait_recv()
        for cp in cps:
            cp.wait_send()

    return pl.pallas_call(
        body, in_specs=[ANY] * n, out_specs=[ANY] * n, out_shape=out_shape,
        scratch_shapes=[pltpu.SemaphoreType.DMA((3 * n,)), pltpu.SemaphoreType.DMA((3 * n,))],
        name=name)(*parts)


def _sum_slots(q, *, name):
    ns, rows, cols = q.shape
    tr = next(t for t in (128, 64, 32, 16, 8) if rows % t == 0)

    def body(q_ref, o_ref):
        acc = q_ref[0].astype(F32)
        for k in range(1, ns):
            acc = acc + q_ref[k].astype(F32)
        o_ref[...] = acc

    return pl.pallas_call(
        body, grid=(rows // tr,),
        in_specs=[pl.BlockSpec((ns, tr, cols), lambda i: (0, i, 0))],
        out_specs=_rspec(tr, cols),
        out_shape=jax.ShapeDtypeStruct((rows, cols), F32),
        compiler_params=_cp("parallel"), name=name)(q)


def _sum_chips(q, p, place, axis, *, name):
    _, rh, cw = q.shape
    tr = next(t for t in (128, 64, 32, 16) if rh % t == 0)
    nt = rh // tr

    def body(p_ref, *refs):
        q_refs, own_ref, o_ref = refs[:N_CHIPS], refs[N_CHIPS], refs[N_CHIPS + 1]
        chip = p_ref[0]
        acc = jnp.where(chip == 0, own_ref[...], q_refs[0][...]).astype(F32)
        for k in range(1, N_CHIPS):
            acc = acc + jnp.where(chip == k, own_ref[...], q_refs[k][...]).astype(F32)
        o_ref[...] = acc

    def slot_spec(k):
        return pl.BlockSpec((None, tr, cw),
                            lambda i, p_ref: (jnp.where(p_ref[0] == k, (k + 1) % N_CHIPS, k), i, 0))

    if axis == 1:
        own_spec = pl.BlockSpec((None, tr, cw), lambda i, p_ref: (0, i, p_ref[0]))
    else:
        own_spec = pl.BlockSpec((None, tr, cw), lambda i, p_ref: (p_ref[0], i, 0))
    return pl.pallas_call(
        body,
        grid_spec=pltpu.PrefetchScalarGridSpec(
            num_scalar_prefetch=1, grid=(nt,),
            in_specs=[slot_spec(k) for k in range(N_CHIPS)] + [own_spec],
            out_specs=pl.BlockSpec((tr, cw), lambda i, p_ref: (p_ref[1] * nt + i, 0))),
        out_shape=jax.ShapeDtypeStruct((2 * rh, cw), F32),
        compiler_params=_cp("parallel"), name=name)(place, *([q] * N_CHIPS), p)


def _sibling_share(shards, *, name):
    n = len(shards)

    def body(*refs):
        o_refs = refs[n:2 * n]
        send_sems, recv_sems = refs[2 * n:]
        x, y, c, _ = _place()
        cps = []
        for i in range(n):
            rh = shards[i].shape[0] // 2
            mine = o_refs[i].at[pl.ds(pl.multiple_of(c * rh, 8), rh), :]
            cp = pltpu.make_async_remote_copy(
                src_ref=mine, dst_ref=mine, send_sem=send_sems.at[i], recv_sem=recv_sems.at[i],
                device_id=(x, y, 1 - c), device_id_type=MESH)
            cp.start()
            cps.append(cp)
        for i in range(n):
            rh = shards[i].shape[0] // 2
            theirs = o_refs[i].at[pl.ds(pl.multiple_of((1 - c) * rh, 8), rh), :]
            pltpu.make_async_remote_copy(
                src_ref=theirs, dst_ref=theirs, send_sem=send_sems.at[i], recv_sem=recv_sems.at[i],
                device_id=(x, y, c), device_id_type=MESH).wait_recv()
        for cp in cps:
            cp.wait_send()

    return pl.pallas_call(
        body, in_specs=[ANY] * n, out_specs=[ANY] * n,
        out_shape=[jax.ShapeDtypeStruct(h.shape, h.dtype) for h in shards],
        input_output_aliases={i: i for i in range(n)},
        scratch_shapes=[pltpu.SemaphoreType.DMA((n,)), pltpu.SemaphoreType.DMA((n,))],
        name=name)(*shards)


def _gather_all(blk, *, name):
    rows, cols = blk.shape

    def body(x_ref, out_ref, send_sems, recv_sems, local_sem):
        x, y, c = lax.axis_index("x"), lax.axis_index("y"), lax.axis_index("c")
        me = 4 * x + 2 * y + c
        mine = pltpu.make_async_copy(x_ref, out_ref.at[me], local_sem)
        mine.start()
        cps = []
        for k in range(1, N_DEV):
            tx = (1 - x) if (k >> 2) & 1 else x
            ty = (1 - y) if (k >> 1) & 1 else y
            tc = (1 - c) if k & 1 else c
            cp = pltpu.make_async_remote_copy(
                src_ref=x_ref, dst_ref=out_ref.at[me], send_sem=send_sems.at[k - 1],
                recv_sem=recv_sems.at[k - 1], device_id=(tx, ty, tc), device_id_type=MESH)
            cp.start()
            cps.append(cp)
        for k in range(1, N_DEV):
            tx = (1 - x) if (k >> 2) & 1 else x
            ty = (1 - y) if (k >> 1) & 1 else y
            tc = (1 - c) if k & 1 else c
            got = out_ref.at[4 * tx + 2 * ty + tc]
            pltpu.make_async_remote_copy(
                src_ref=got, dst_ref=got, send_sem=send_sems.at[k - 1], recv_sem=recv_sems.at[k - 1],
                device_id=(x, y, c), device_id_type=MESH).wait_recv()
        for cp in cps:
            cp.wait_send()
        mine.wait()

    vm = pl.BlockSpec(memory_space=pltpu.VMEM)
    return pl.pallas_call(
        body, in_specs=[vm], out_specs=vm,
        out_shape=jax.ShapeDtypeStruct((N_DEV, rows, cols), blk.dtype),
        scratch_shapes=[pltpu.SemaphoreType.DMA((N_DEV - 1,)), pltpu.SemaphoreType.DMA((N_DEV - 1,)),
                        pltpu.SemaphoreType.DMA],
        name=name)(blk)


def _as_rows(a):
    flat = a.reshape(-1)
    n = flat.shape[0]
    rows = -(-n // (8 * LANES)) * 8
    return jnp.pad(flat, (0, rows * LANES - n)).reshape(rows, LANES)


def _from_rows(p, shape):
    n = int(np.prod(shape))
    return p.reshape(-1)[:n].reshape(shape)


WEIGHT_AXES = (1, 1, 1, 0, 1, 0)
WIRE = BF16


def kernel(x, meta_tokens, w_in, w_na_out, w_hg_out, w_o, w_up, w_down, norm_mix, norm_mlp, norm_final, hg_norm, na_rpb, hg_lb_logits, loss_target, m_meta_tokens, m_w_in, m_w_na_out, m_w_hg_out, m_w_o, m_w_up, m_w_down, m_norm_mix, m_norm_mlp, m_norm_final, m_hg_norm, m_na_rpb, m_hg_lb_logits, v_meta_tokens, v_w_in, v_w_na_out, v_w_hg_out, v_w_o, v_w_up, v_w_down, v_norm_mix, v_norm_mlp, v_norm_final, v_hg_norm, v_na_rpb, v_hg_lb_logits):
    xi, yi, ci = lax.axis_index("x"), lax.axis_index("y"), lax.axis_index("c")
    chip = 2 * xi + yi
    d = x.shape[-1]
    dshard = meta_tokens.shape[1]
    hgw = hg_norm.shape[1]
    lbs = hg_lb_logits.shape[2]
    big = [w_in[0], w_na_out[0], w_hg_out[0], w_o[0], w_up[0], w_down[0]]
    big_m = [m_w_in[0], m_w_na_out[0], m_w_hg_out[0], m_w_o[0], m_w_up[0], m_w_down[0]]
    big_v = [v_w_in[0], v_w_na_out[0], v_w_hg_out[0], v_w_o[0], v_w_up[0], v_w_down[0]]

    place = jnp.stack([chip, ci]).astype(jnp.int32)
    own_w = [_cast_into_full(w, ax, place, name=f"cast_shard_{i}")
             for i, (w, ax) in enumerate(zip(big, WEIGHT_AXES))]
    full_w = _weight_allgather(own_w, WEIGHT_AXES, name="weight_allgather")
    small_in = jnp.concatenate([_as_rows(meta_tokens), _as_rows(hg_lb_logits)], axis=0)
    small_all = _gather_all(small_in, name="gather_small_params")[0::2]
    n_meta_rows = N_META * dshard // LANES
    meta_full = (small_all[:, :n_meta_rows].reshape(N_CHIPS, N_META, dshard)
                 .transpose(1, 0, 2).reshape(N_META, d))
    lbl_full = (small_all[:, n_meta_rows:].reshape(N_CHIPS, -1)[:, :4 * lbs]
                .reshape(N_CHIPS, 2, 2, lbs).transpose(1, 2, 0, 3).reshape(2, 2, N_CHIPS * lbs))
    lb = jax.nn.softmax(lbl_full, axis=1)[:, 0]

    (loss, dx, dmeta, *dws, dg_mix, dg_mlp, dg_fin, d_gain, d_rpb, d_lb) = _local_step(
        x[0], loss_target[0], meta_full, *full_w, norm_mix, norm_mlp, norm_final.reshape(1, d),
        hg_norm, na_rpb[0], lb)

    c_arr = ci.reshape(1).astype(jnp.int32)
    g3 = [g.reshape(1, *g.shape) if ax == 1 else g.reshape(N_CHIPS, g.shape[0] // N_CHIPS, g.shape[1])
          for g, ax in zip(dws, WEIGHT_AXES)]
    rx = _sibling_swap(g3, name="grad_sibling_swap")
    parts = [_pair_add(g, r, c_arr, out_dtype=WIRE, name=f"grad_pair_add_{i}")
             for i, (g, r) in enumerate(zip(g3, rx))]
    slots = _chip_scatter(parts, WEIGHT_AXES, name="grad_chip_scatter")
    halves = [_sum_chips(q, p, place, ax, name=f"grad_sum_chips_{i}")
              for i, (q, p, ax) in enumerate(zip(slots, parts, WEIGHT_AXES))]
    g_big = _sibling_share(halves, name="grad_sibling_share")

    d_rpb_c = d_rpb[:, :2 * NA_WIN_W - 1]
    small_g = [dmeta, dg_mix, dg_mlp, dg_fin, d_gain, d_rpb_c, d_lb, loss]
    packed = jnp.concatenate([_as_rows(a) for a in small_g], axis=0)
    total = _sum_slots(_gather_all(packed, name="gather_small_grads"), name="sum_small_grads")
    offs = np.cumsum([0] + [_as_rows(a).shape[0] for a in small_g])
    take = lambda i, shape: _from_rows(total[offs[i]:offs[i + 1]], shape)
    g_meta_full = take(0, (N_META, d))
    g_norm_mix, g_norm_mlp = take(1, (1, d)), take(2, (1, d))
    g_norm_final = take(3, (d,))
    g_hg_norm = take(4, (1, hgw))
    g_rpb = take(5, na_rpb.shape)
    g_lb = take(6, (2, hgw))
    loss_total = take(7, (1, LANES))[0, 0]
    g_meta = lax.dynamic_slice_in_dim(g_meta_full, chip * dshard, dshard, axis=1)
    dl0 = lb * (1.0 - lb) * g_lb
    g_lbl_full = jnp.stack([dl0, -dl0], axis=1)
    g_lbl = lax.dynamic_slice_in_dim(g_lbl_full, chip * lbs, lbs, axis=2)

    big_out = [_adamw(w, g, m, v, name=f"adamw_{i}")
               for i, (w, g, m, v) in enumerate(zip(big, g_big, big_m, big_v))]
    small_w = [meta_tokens, norm_mix, norm_mlp, norm_final, hg_norm, na_rpb, hg_lb_logits]
    small_gr = [g_meta, g_norm_mix, g_norm_mlp, g_norm_final, g_hg_norm, g_rpb, g_lbl]
    small_m = [m_meta_tokens, m_norm_mix, m_norm_mlp, m_norm_final, m_hg_norm, m_na_rpb, m_hg_lb_logits]
    small_v = [v_meta_tokens, v_norm_mix, v_norm_mlp, v_norm_final, v_hg_norm, v_na_rpb, v_hg_lb_logits]
    pk = lambda lst: jnp.concatenate([_as_rows(a) for a in lst], axis=0)
    sd, sm, sv = _adamw(pk(small_w), pk(small_gr), pk(small_m), pk(small_v), name="adamw_small")
    soffs = np.cumsum([0] + [_as_rows(a).shape[0] for a in small_w])
    unpk = lambda p: [_from_rows(p[soffs[i]:soffs[i + 1]], small_w[i].shape) for i in range(len(small_w))]
    sd, sm, sv = unpk(sd), unpk(sm), unpk(sv)

    def order(bigs, smalls):
        return [smalls[0]] + [b.reshape(1, *b.shape) for b in bigs] + smalls[1:]

    grads = order(g_big, small_gr)
    deltas = order([o[0] for o in big_out], sd)
    new_m = order([o[1] for o in big_out], sm)
    new_v = order([o[2] for o in big_out], sv)
    return (loss_total, dx.reshape(1, *dx.shape), *grads, *deltas, *new_m, *new_v)
```

```python
import functools

import numpy as np
import jax
import jax.numpy as jnp
from jax import lax
from jax.experimental import pallas as pl
from jax.experimental.pallas import tpu as pltpu

F32 = jnp.float32
BF16 = jnp.bfloat16
HIGHEST = lax.Precision.HIGHEST

GRID_W = 64
N_META = 16
EPS = 1e-6
NA_HEAD_DIM = 64
NA_WIN_H = 8
NA_WIN_W = 16
HG_DK = 128
HG_CHUNK = 16
LANES = 128
ROW_ALIGN = 128
VMEM_LIMIT = 48 * 1024 * 1024

ADAM_LR = 0.001
ADAM_B1 = 0.9
ADAM_B2 = 0.999
ADAM_EPS = 1e-08
ADAM_WD = 0.01
ADAM_STEP = 10

MESH = pl.DeviceIdType.MESH


def _cp(*sem):
    return pltpu.CompilerParams(dimension_semantics=sem, vmem_limit_bytes=VMEM_LIMIT)


def _sigmoid(x):
    return 1.0 / (1.0 + jnp.exp(-x))


def _dot(a, b, dims, precision=None):
    return lax.dot_general(a, b, (dims, ((), ())), preferred_element_type=F32, precision=precision)


def _nn(a, b, **kw):
    return _dot(a, b, ((1,), (0,)), **kw)


def _nt(a, b, **kw):
    return _dot(a, b, ((1,), (1,)), **kw)


def _tn(a, b, **kw):
    return _dot(a, b, ((0,), (0,)), **kw)


def _matmul(a, b, *, ta=False, tb=False, tm=None, tn=None, tk=None, out_dtype=F32, name,
            precision=None):
    if ta:
        kdim, m = a.shape
    else:
        m, kdim = a.shape
    if tb:
        n, k2 = b.shape
    else:
        k2, n = b.shape
    assert kdim == k2, (a.shape, b.shape, ta, tb)
    if tm is None:
        if ta:
            tm = next(t for t in (512, 256, 128, m) if m % t == 0)
        else:
            tm = m // 2 if (m // 2) % 16 == 0 and m > 512 else m
    if tn is None:
        tn = next(t for t in (512, 256, 128, n) if n % t == 0)
    if tk is None:
        tk = kdim if ta else next(t for t in (1024, 512, 256, 128, kdim) if kdim % t == 0)
    assert m % tm == 0 and n % tn == 0 and kdim % tk == 0, (m, n, kdim, tm, tn, tk)
    nk = kdim // tk
    op_dtype = F32 if precision is not None else BF16

    def body(a_ref, b_ref, o_ref, acc_ref):
        kk = pl.program_id(2)

        @pl.when(kk == 0)
        def _():
            acc_ref[...] = jnp.zeros_like(acc_ref)

        av = a_ref[...].astype(op_dtype)
        bv = b_ref[...].astype(op_dtype)
        dims = ((0 if ta else 1,), (1 if tb else 0,))
        acc_ref[...] += _dot(av, bv, dims, precision=precision)

        @pl.when(kk == nk - 1)
        def _():
            o_ref[...] = acc_ref[...].astype(out_dtype)

    a_spec = (pl.BlockSpec((tk, tm), lambda i, j, k: (k, i)) if ta
              else pl.BlockSpec((tm, tk), lambda i, j, k: (i, k)))
    b_spec = (pl.BlockSpec((tn, tk), lambda i, j, k: (j, k)) if tb
              else pl.BlockSpec((tk, tn), lambda i, j, k: (k, j)))
    return pl.pallas_call(
        body,
        grid=(m // tm, n // tn, nk),
        in_specs=[a_spec, b_spec],
        out_specs=pl.BlockSpec((tm, tn), lambda i, j, k: (i, j)),
        out_shape=jax.ShapeDtypeStruct((m, n), out_dtype),
        scratch_shapes=[pltpu.VMEM((tm, tn), F32)],
        compiler_params=_cp("parallel", "parallel", "arbitrary"),
        name=name,
    )(a, b)


def _rspec(tr, w, cb=0):
    return pl.BlockSpec((tr, w), lambda i: (i, cb))


def _fspec(shape):
    nd = len(shape)
    return pl.BlockSpec(shape, lambda i: (0,) * nd)


def _row_tile(lp):
    return ROW_ALIGN if lp % ROW_ALIGN == 0 else lp


def _rmsnorm_fwd(x, g, *, name):
    lp, d = x.shape
    tr = _row_tile(lp)

    def body(x_ref, g_ref, o_ref):
        xv = x_ref[...]
        r = lax.rsqrt(jnp.mean(xv * xv, axis=-1, keepdims=True) + EPS)
        o_ref[...] = (xv * r * g_ref[...]).astype(BF16)

    return pl.pallas_call(
        body, grid=(lp // tr,),
        in_specs=[_rspec(tr, d), _fspec((1, d))],
        out_specs=_rspec(tr, d),
        out_shape=jax.ShapeDtypeStruct((lp, d), BF16),
        compiler_params=_cp("parallel"), name=name)(x, g)


def _residual_norm(h, t, g, *, name):
    lp, d = h.shape
    tr = _row_tile(lp)

    def body(h_ref, t_ref, g_ref, h1_ref, m_ref):
        xv = h_ref[...] + t_ref[...]
        h1_ref[...] = xv
        r = lax.rsqrt(jnp.mean(xv * xv, axis=-1, keepdims=True) + EPS)
        m_ref[...] = (xv * r * g_ref[...]).astype(BF16)

    return pl.pallas_call(
        body, grid=(lp // tr,),
        in_specs=[_rspec(tr, d), _rspec(tr, d), _fspec((1, d))],
        out_specs=[_rspec(tr, d), _rspec(tr, d)],
        out_shape=[jax.ShapeDtypeStruct((lp, d), F32), jax.ShapeDtypeStruct((lp, d), BF16)],
        compiler_params=_cp("parallel"), name=name)(h, t, g)


def _rmsnorm_bwd_add(x, g, dy, dres, *, name):
    lp, d = x.shape
    tr = _row_tile(lp)

    def body(x_ref, g_ref, dy_ref, dr_ref, dx_ref, dg_ref):
        @pl.when(pl.program_id(0) == 0)
        def _():
            dg_ref[...] = jnp.zeros_like(dg_ref)

        xv = x_ref[...]
        r = lax.rsqrt(jnp.mean(xv * xv, axis=-1, keepdims=True) + EPS)
        xh = xv * r
        dyv = dy_ref[...]
        dg_ref[...] += jnp.sum(dyv * xh, axis=0, keepdims=True)
        dxh = dyv * g_ref[...]
        dx_ref[...] = dr_ref[...] + r * (dxh - xh * jnp.mean(dxh * xh, axis=-1, keepdims=True))

    return pl.pallas_call(
        body, grid=(lp // tr,),
        in_specs=[_rspec(tr, d), _fspec((1, d)), _rspec(tr, d), _rspec(tr, d)],
        out_specs=[_rspec(tr, d), _fspec((1, d))],
        out_shape=[jax.ShapeDtypeStruct((lp, d), F32), jax.ShapeDtypeStruct((1, d), F32)],
        compiler_params=_cp("arbitrary"), name=name)(x, g, dy, dres)


def _final_loss(h1, t2, g, tgt, *, n_tok, name):
    lp, d = h1.shape
    tr = _row_tile(lp)

    def body(h_ref, t_ref, g_ref, tg_ref, dh_ref, loss_ref, dg_ref):
        i = pl.program_id(0)

        @pl.when(i == 0)
        def _():
            loss_ref[...] = jnp.zeros_like(loss_ref)
            dg_ref[...] = jnp.zeros_like(dg_ref)

        xv = h_ref[...] + t_ref[...]
        r = lax.rsqrt(jnp.mean(xv * xv, axis=-1, keepdims=True) + EPS)
        xh = xv * r
        gv = g_ref[...]
        row = i * tr + lax.broadcasted_iota(jnp.int32, (tr, 1), 0)
        valid = (row >= N_META) & (row < N_META + n_tok)
        err = jnp.where(valid, xh * gv - tg_ref[...], 0.0)
        loss_ref[...] += jnp.sum(0.5 * err * err) / d
        dy = err / d
        dg_ref[...] += jnp.sum(dy * xh, axis=0, keepdims=True)
        dxh = dy * gv
        dh_ref[...] = r * (dxh - xh * jnp.mean(dxh * xh, axis=-1, keepdims=True))

    return pl.pallas_call(
        body, grid=(lp // tr,),
        in_specs=[_rspec(tr, d), _rspec(tr, d), _fspec((1, d)), _rspec(tr, d)],
        out_specs=[_rspec(tr, d), _fspec((1, LANES)), _fspec((1, d))],
        out_shape=[jax.ShapeDtypeStruct((lp, d), F32), jax.ShapeDtypeStruct((1, LANES), F32),
                   jax.ShapeDtypeStruct((1, d), F32)],
        compiler_params=_cp("arbitrary"), name=name)(h1, t2, g, tgt)


def _relu2(u, *, name):
    lp, f = u.shape
    tr = _row_tile(lp)

    def body(u_ref, o_ref):
        rv = jnp.maximum(u_ref[...], 0.0)
        o_ref[...] = (rv * rv).astype(BF16)

    return pl.pallas_call(
        body, grid=(lp // tr,), in_specs=[_rspec(tr, f)], out_specs=_rspec(tr, f),
        out_shape=jax.ShapeDtypeStruct((lp, f), BF16),
        compiler_params=_cp("parallel"), name=name)(u)


def _relu2_bwd(dact, u, *, name):
    lp, f = u.shape
    tr = _row_tile(lp)

    def body(d_ref, u_ref, o_ref):
        o_ref[...] = (d_ref[...] * 2.0 * jnp.maximum(u_ref[...], 0.0)).astype(BF16)

    return pl.pallas_call(
        body, grid=(lp // tr,), in_specs=[_rspec(tr, f), _rspec(tr, f)], out_specs=_rspec(tr, f),
        out_shape=jax.ShapeDtypeStruct((lp, f), BF16),
        compiler_params=_cp("parallel"), name=name)(dact, u)


def _gate_mix(proj, y_na, y_hg, *, col_gate, name):
    lp, d = y_na.shape
    tr = _row_tile(lp)
    cb = col_gate // d

    def body(gn_ref, gh_ref, yn_ref, yh_ref, o_ref):
        o_ref[...] = (_sigmoid(gn_ref[...]) * yn_ref[...]
                      + _sigmoid(gh_ref[...]) * yh_ref[...]).astype(BF16)

    return pl.pallas_call(
        body, grid=(lp // tr,),
        in_specs=[_rspec(tr, d, cb), _rspec(tr, d, cb + 1), _rspec(tr, d), _rspec(tr, d)],
        out_specs=_rspec(tr, d),
        out_shape=jax.ShapeDtypeStruct((lp, d), BF16),
        compiler_params=_cp("parallel"), name=name)(proj, proj, y_na, y_hg)


def _gate_mix_bwd(proj, y_na, y_hg, dmix, *, col_gate, name):
    lp, d = y_na.shape
    tr = _row_tile(lp)
    cb = col_gate // d

    def body(gn_ref, gh_ref, yn_ref, yh_ref, dm_ref, dyn_ref, dyh_ref, dgn_ref, dgh_ref):
        dm = dm_ref[...]
        sn = _sigmoid(gn_ref[...])
        sh = _sigmoid(gh_ref[...])
        dyn_ref[...] = (dm * sn).astype(BF16)
        dyh_ref[...] = (dm * sh).astype(BF16)
        dgn_ref[...] = (dm * yn_ref[...] * sn * (1.0 - sn)).astype(BF16)
        dgh_ref[...] = (dm * yh_ref[...] * sh * (1.0 - sh)).astype(BF16)

    sds = jax.ShapeDtypeStruct((lp, d), BF16)
    return pl.pallas_call(
        body, grid=(lp // tr,),
        in_specs=[_rspec(tr, d, cb), _rspec(tr, d, cb + 1), _rspec(tr, d), _rspec(tr, d),
                  _rspec(tr, d)],
        out_specs=[_rspec(tr, d)] * 4, out_shape=[sds] * 4,
        compiler_params=_cp("parallel"), name=name)(proj, proj, y_na, y_hg, dmix)


def _hg_out(o_f, o_b, proj, gain, *, col_g, name):
    lp, w = o_f.shape
    tr = _row_tile(lp)
    hh = w // HG_DK

    def body(of_ref, ob_ref, g_ref, gain_ref, y_ref):
        gv = g_ref[...]
        sg = gv * _sigmoid(gv)
        for h in range(hh):
            sl = slice(h * HG_DK, (h + 1) * HG_DK)
            o = of_ref[:, sl] + ob_ref[:, sl]
            r = lax.rsqrt(jnp.mean(o * o, axis=-1, keepdims=True) + EPS)
            y_ref[:, sl] = (o * r * gain_ref[:, sl] * sg[:, sl]).astype(BF16)

    return pl.pallas_call(
        body, grid=(lp // tr,),
        in_specs=[_rspec(tr, w), _rspec(tr, w), _rspec(tr, w, col_g // w), _fspec((1, w))],
        out_specs=_rspec(tr, w),
        out_shape=jax.ShapeDtypeStruct((lp, w), BF16),
        compiler_params=_cp("parallel"), name=name)(o_f, o_b, proj, gain)


def _hg_out_bwd(o_f, o_b, proj, gain, dy, *, col_g, name):
    lp, w = o_f.shape
    tr = _row_tile(lp)
    hh = w // HG_DK

    def body(of_ref, ob_ref, g_ref, gain_ref, dy_ref, do_ref, dg_ref, dgain_ref):
        @pl.when(pl.program_id(0) == 0)
        def _():
            dgain_ref[...] = jnp.zeros_like(dgain_ref)

        for h in range(hh):
            sl = slice(h * HG_DK, (h + 1) * HG_DK)
            gv = g_ref[:, sl]
            s = _sigmoid(gv)
            sg = gv * s
            dsg = s + gv * s * (1.0 - s)
            o = of_ref[:, sl] + ob_ref[:, sl]
            r = lax.rsqrt(jnp.mean(o * o, axis=-1, keepdims=True) + EPS)
            on = o * r
            dyv = dy_ref[:, sl]
            gn = gain_ref[:, sl]
            dgain_ref[:, sl] += jnp.sum(dyv * on * sg, axis=0, keepdims=True)
            dg_ref[:, sl] = (dyv * on * gn * dsg).astype(BF16)
            don = dyv * gn * sg
            do_ref[:, sl] = r * (don - on * jnp.mean(don * on, axis=-1, keepdims=True))

    return pl.pallas_call(
        body, grid=(lp // tr,),
        in_specs=[_rspec(tr, w), _rspec(tr, w), _rspec(tr, w, col_g // w), _fspec((1, w)),
                  _rspec(tr, w)],
        out_specs=[_rspec(tr, w), _rspec(tr, w), _fspec((1, w))],
        out_shape=[jax.ShapeDtypeStruct((lp, w), F32), jax.ShapeDtypeStruct((lp, w), BF16),
                   jax.ShapeDtypeStruct((1, w), F32)],
        compiler_params=_cp("arbitrary"), name=name)(o_f, o_b, proj, gain, dy)


HG_GROUP = 8
HG_ROWS = HG_GROUP * HG_CHUNK


def _hg_gates(zq, z, lbv):
    qh = zq * _sigmoid(zq)
    s = _sigmoid(z)
    f = lbv + (1.0 - lbv) * s
    kk = (1.0 - lbv) * _sigmoid(-z)
    return qh, s, f, jnp.log(f), kk


def _tri_blocks(upper):
    r = lax.broadcasted_iota(jnp.int32, (HG_ROWS, HG_ROWS), 0)
    c = lax.broadcasted_iota(jnp.int32, (HG_ROWS, HG_ROWS), 1)
    same = (r // HG_CHUNK) == (c // HG_CHUNK)
    return jnp.where(same & ((c >= r) if upper else (c <= r)), 1.0, 0.0).astype(F32)


def _g3(a):
    return a.reshape(HG_GROUP, HG_CHUNK, a.shape[-1])


def _hg_scan_fwd(proj, lb, *, reverse, col_q, col_z, col_i, hh, name):
    lp = proj.shape[0]
    n_groups = lp // HG_ROWS
    n_chunks = lp // HG_CHUNK
    c16 = HG_CHUNK
    last = 0 if reverse else c16 - 1

    def body(q_ref, z_ref, i_ref, lb_ref, o_ref, st_ref):
        lbv = lb_ref[...]
        tri = _tri_blocks(reverse)
        row = lax.broadcasted_iota(jnp.int32, (1, c16, HG_DK), 1)

        def group(gi, st):
            gg = (n_groups - 1 - gi) if reverse else gi
            r0 = pl.multiple_of(gg * HG_ROWS, HG_ROWS)
            v = i_ref[pl.ds(r0, HG_ROWS), :]
            qh, _, _, g, kk = _hg_gates(q_ref[pl.ds(r0, HG_ROWS), :], z_ref[pl.ds(r0, HG_ROWS), :],
                                        lbv)
            b = _nn(tri, g, precision=HIGHEST)
            b3, qh3, kk3, v3 = _g3(b), _g3(qh), _g3(kk), _g3(v)
            bl3 = b3[:, last:last + 1, :]
            qe = (qh * jnp.exp(b)).astype(BF16)
            kd = (kk3 * jnp.exp(bl3 - b3)).reshape(HG_ROWS, HG_DK).astype(BF16)
            decay3 = jnp.exp(bl3)
            v16 = v.astype(BF16)
            o3 = jnp.zeros((HG_GROUP, c16, HG_DK), F32)
            for t in range(c16):
                mask = (row >= t) if reverse else (row <= t)
                e = jnp.where(mask, jnp.exp(jnp.minimum(b3[:, t:t + 1, :] - b3, 0.0)), 0.0)
                a_col = jnp.sum(qh3[:, t:t + 1, :] * kk3 * e, axis=2, keepdims=True)
                o_t = jnp.sum(a_col * v3, axis=1, keepdims=True)
                o3 = o3 + jnp.where(row == t, o_t, 0.0)
            inter = [None] * HG_GROUP
            for c in (reversed(range(HG_GROUP)) if reverse else range(HG_GROUP)):
                sl = slice(c * c16, (c + 1) * c16)
                st_ref[gg * HG_GROUP + c] = st
                inter[c] = _nt(qe[sl], st.astype(BF16))
                st = decay3[c] * st + _tn(v16[sl], kd[sl])
            o_ref[pl.ds(r0, HG_ROWS), :] = (o3.reshape(HG_ROWS, HG_DK)
                                            + jnp.concatenate(inter, axis=0))
            return st

        lax.fori_loop(0, n_groups, group, jnp.zeros((HG_DK, HG_DK), F32))

    cspec = lambda col: pl.BlockSpec((lp, HG_DK), lambda h: (0, col // HG_DK + h))
    return pl.pallas_call(
        body, grid=(hh,),
        in_specs=[cspec(col_q), cspec(col_z), cspec(col_i),
                  pl.BlockSpec((None, 1, HG_DK), lambda h: (h, 0, 0))],
        out_specs=[pl.BlockSpec((lp, HG_DK), lambda h: (0, h)),
                   pl.BlockSpec((None, n_chunks, HG_DK, HG_DK), lambda h: (h, 0, 0, 0))],
        out_shape=[jax.ShapeDtypeStruct((lp, hh * HG_DK), F32),
                   jax.ShapeDtypeStruct((hh, n_chunks, HG_DK, HG_DK), F32)],
        compiler_params=_cp("parallel"), name=name)(proj, proj, proj, lb)


def _hg_scan_bwd(proj, lb, states, do, *, reverse, col_q, col_z, col_i, hh, name):
    lp = proj.shape[0]
    n_groups = lp // HG_ROWS
    n_chunks = lp // HG_CHUNK
    c16 = HG_CHUNK
    last = 0 if reverse else c16 - 1

    def body(q_ref, z_ref, i_ref, lb_ref, st_ref, do_ref, dq_ref, dz_ref, dv_ref, dlb_ref):
        lbv = lb_ref[...]
        tri = _tri_blocks(reverse)
        tri_t = _tri_blocks(not reverse)
        row = lax.broadcasted_iota(jnp.int32, (1, c16, HG_DK), 1)

        def group(gi, carry):
            dst, dlb = carry
            gg = gi if reverse else (n_groups - 1 - gi)
            r0 = pl.multiple_of(gg * HG_ROWS, HG_ROWS)
            zq = q_ref[pl.ds(r0, HG_ROWS), :]
            v = i_ref[pl.ds(r0, HG_ROWS), :]
            dov = do_ref[pl.ds(r0, HG_ROWS), :]
            qh, s, f, g, kk = _hg_gates(zq, z_ref[pl.ds(r0, HG_ROWS), :], lbv)
            b = _nn(tri, g, precision=HIGHEST)
            b3, qh3, kk3, do3 = _g3(b), _g3(qh), _g3(kk), _g3(dov)
            bl3 = b3[:, last:last + 1, :]
            eb = jnp.exp(b)
            ebl3 = jnp.exp(bl3 - b3)
            decay3 = jnp.exp(bl3)
            qe16 = (qh * eb).astype(BF16)
            kd16 = (kk3 * ebl3).reshape(HG_ROWS, HG_DK).astype(BF16)
            v16, do16 = v.astype(BF16), dov.astype(BF16)
            do_s, da_t, v_ds, dv_st, dbl_st = ([None] * HG_GROUP for _ in range(5))
            for c in (range(HG_GROUP) if reverse else reversed(range(HG_GROUP))):
                sl = slice(c * c16, (c + 1) * c16)
                st = st_ref[gg * HG_GROUP + c]
                st16, dst16 = st.astype(BF16), dst.astype(BF16)
                do_s[c] = _nn(do16[sl], st16)
                da_t[c] = _nt(v16[sl], do16[sl])
                v_ds[c] = _nn(v16[sl], dst16)
                dv_st[c] = _nt(kd16[sl], dst16)
                dbl_st[c] = decay3[c] * jnp.sum(st * dst, axis=0, keepdims=True)
                dst = decay3[c] * dst + _tn(do16[sl], qe16[sl])
            cat3 = lambda parts: _g3(jnp.concatenate(parts, axis=0))
            dq3 = _g3(eb) * cat3(do_s)
            dk_state3 = ebl3 * cat3(v_ds)
            dk3 = dk_state3
            dv3 = cat3(dv_st)
            da3 = cat3(da_t)
            for t in range(c16):
                mask = (row >= t) if reverse else (row <= t)
                e = jnp.where(mask, jnp.exp(jnp.minimum(b3[:, t:t + 1, :] - b3, 0.0)), 0.0)
                qt = qh3[:, t:t + 1, :]
                a_col = jnp.sum(qt * kk3 * e, axis=2, keepdims=True)
                ce = da3[:, :, t:t + 1] * e
                dq_t = jnp.sum(ce * kk3, axis=1, keepdims=True)
                dq3 = dq3 + jnp.where(row == t, dq_t, 0.0)
                dk3 = dk3 + ce * qt
                dv3 = dv3 + a_col * do3[:, t:t + 1, :]
            dbl3 = (jnp.concatenate([d[None] for d in dbl_st], axis=0)
                    + jnp.sum(kk3 * dk_state3, axis=1, keepdims=True))
            db3 = qh3 * dq3 - kk3 * dk3 + jnp.where(row == last, dbl3, 0.0)
            dg = _nn(tri_t, db3.reshape(HG_ROWS, HG_DK), precision=HIGHEST)
            df = dg / f - dk3.reshape(HG_ROWS, HG_DK)
            sq = _sigmoid(zq)
            dq_ref[pl.ds(r0, HG_ROWS), :] = (dq3.reshape(HG_ROWS, HG_DK)
                                             * (sq + zq * sq * (1.0 - sq)))
            dz_ref[pl.ds(r0, HG_ROWS), :] = df * (1.0 - lbv) * s * (1.0 - s)
            dv_ref[pl.ds(r0, HG_ROWS), :] = dv3.reshape(HG_ROWS, HG_DK)
            return dst, dlb + jnp.sum(df * (1.0 - s), axis=0, keepdims=True)

        _, dlb = lax.fori_loop(0, n_groups, group,
                               (jnp.zeros((HG_DK, HG_DK), F32), jnp.zeros((1, HG_DK), F32)))
        dlb_ref[...] = dlb

    cspec = lambda col: pl.BlockSpec((lp, HG_DK), lambda h: (0, col // HG_DK + h))
    ospec = pl.BlockSpec((lp, HG_DK), lambda h: (0, h))
    sds = jax.ShapeDtypeStruct((lp, hh * HG_DK), F32)
    return pl.pallas_call(
        body, grid=(hh,),
        in_specs=[cspec(col_q), cspec(col_z), cspec(col_i),
                  pl.BlockSpec((None, 1, HG_DK), lambda h: (h, 0, 0)),
                  pl.BlockSpec((None, n_chunks, HG_DK, HG_DK), lambda h: (h, 0, 0, 0)),
                  ospec],
        out_specs=[ospec, ospec, ospec, pl.BlockSpec((None, 1, HG_DK), lambda h: (h, 0, 0))],
        out_shape=[sds, sds, sds, jax.ShapeDtypeStruct((hh, 1, HG_DK), F32)],
        compiler_params=_cp("parallel"), name=name)(proj, proj, proj, lb, states, do)


def _na_rows(r, rows):
    rs = jnp.clip(r - NA_WIN_H // 2, 0, rows - NA_WIN_H)
    i0 = rs - r + (NA_WIN_H - 1)
    q0 = pl.multiple_of(N_META + GRID_W * r, 16)
    k0 = pl.multiple_of(N_META + GRID_W * rs, 16)
    return i0, q0, k0


def _na_scores(q16, k16, km16, tb_ref, i0, scale):
    s = _nt(q16, k16) * scale
    bias = jnp.concatenate([tb_ref[i0 + j] for j in range(NA_WIN_H)], axis=1)
    return s + bias, _nt(q16, km16) * scale


def _na_fwd(qkv, tb, *, n_tok, name):
    _, nh, lp, dh = qkv.shape
    rows = n_tok // GRID_W
    scale = dh ** -0.5
    kw = NA_WIN_H * GRID_W

    def body(q_ref, k_ref, v_ref, tb_ref, o_ref, lse_ref):
        o_ref[...] = jnp.zeros_like(o_ref)
        lse_ref[...] = jnp.zeros_like(lse_ref)
        km = k_ref[0:N_META, :].astype(BF16)
        vm = v_ref[0:N_META, :].astype(BF16)
        s = _nt(q_ref[0:N_META, :].astype(BF16), km) * scale
        m = jnp.max(s, axis=1, keepdims=True)
        p = jnp.exp(s - m)
        l = jnp.sum(p, axis=1, keepdims=True)
        o_ref[0:N_META, :] = _nn(p.astype(BF16), vm) / l
        lse_ref[0:N_META, :] = m + jnp.log(l)

        def step(r, carry):
            i0, q0, k0 = _na_rows(r, rows)
            q16 = q_ref[pl.ds(q0, GRID_W), :].astype(BF16)
            k16 = k_ref[pl.ds(k0, kw), :].astype(BF16)
            v16 = v_ref[pl.ds(k0, kw), :].astype(BF16)
            s, sm = _na_scores(q16, k16, km, tb_ref, i0, scale)
            m = jnp.maximum(jnp.max(s, axis=1, keepdims=True), jnp.max(sm, axis=1, keepdims=True))
            p = jnp.exp(s - m)
            pm = jnp.exp(sm - m)
            l = jnp.sum(p, axis=1, keepdims=True) + jnp.sum(pm, axis=1, keepdims=True)
            o = _nn(p.astype(BF16), v16) + _nn(pm.astype(BF16), vm)
            o_ref[pl.ds(q0, GRID_W), :] = o / l
            lse_ref[pl.ds(q0, GRID_W), :] = m + jnp.log(l)
            return carry

        lax.fori_loop(0, rows, step, 0)

    hspec = lambda which: pl.BlockSpec((None, None, lp, dh), lambda h: (which, h, 0, 0))
    return pl.pallas_call(
        body, grid=(nh,),
        in_specs=[hspec(0), hspec(1), hspec(2),
                  pl.BlockSpec((None, 2 * NA_WIN_H - 1, GRID_W, GRID_W), lambda h: (h, 0, 0, 0))],
        out_specs=[pl.BlockSpec((None, lp, dh), lambda h: (h, 0, 0)),
                   pl.BlockSpec((None, lp, 1), lambda h: (h, 0, 0))],
        out_shape=[jax.ShapeDtypeStruct((nh, lp, dh), F32), jax.ShapeDtypeStruct((nh, lp, 1), F32)],
        compiler_params=_cp("parallel"), name=name)(qkv, qkv, qkv, tb)


def _na_bwd(qkv, tb, o, lse, do, *, n_tok, name):
    _, nh, lp, dh = qkv.shape
    rows = n_tok // GRID_W
    scale = dh ** -0.5
    kw = NA_WIN_H * GRID_W

    def body(q_ref, k_ref, v_ref, tb_ref, o_ref, lse_ref, do_ref, dq_ref, dk_ref, dv_ref, dtb_ref):
        dq_ref[...] = jnp.zeros_like(dq_ref)
        dk_ref[...] = jnp.zeros_like(dk_ref)
        dv_ref[...] = jnp.zeros_like(dv_ref)
        dtb_ref[...] = jnp.zeros_like(dtb_ref)
        km = k_ref[0:N_META, :].astype(BF16)
        vm = v_ref[0:N_META, :].astype(BF16)
        qm = q_ref[0:N_META, :].astype(BF16)
        dom = do_ref[0:N_META, :]
        p = jnp.exp(_nt(qm, km) * scale - lse_ref[0:N_META, :])
        dp = _nt(dom.astype(BF16), vm)
        delta = jnp.sum(dom * o_ref[0:N_META, :], axis=1, keepdims=True)
        ds = (p * (dp - delta)).astype(BF16)
        dq_ref[0:N_META, :] = _nn(ds, km) * scale
        dkm0 = _tn(ds, qm) * scale
        dvm0 = _tn(p.astype(BF16), dom.astype(BF16))

        def step(r, carry):
            dkm, dvm = carry
            i0, q0, k0 = _na_rows(r, rows)
            q16 = q_ref[pl.ds(q0, GRID_W), :].astype(BF16)
            k16 = k_ref[pl.ds(k0, kw), :].astype(BF16)
            v16 = v_ref[pl.ds(k0, kw), :].astype(BF16)
            s, sm = _na_scores(q16, k16, km, tb_ref, i0, scale)
            lse = lse_ref[pl.ds(q0, GRID_W), :]
            p = jnp.exp(s - lse)
            pm = jnp.exp(sm - lse)
            dov = do_ref[pl.ds(q0, GRID_W), :]
            do16 = dov.astype(BF16)
            delta = jnp.sum(dov * o_ref[pl.ds(q0, GRID_W), :], axis=1, keepdims=True)
            ds = p * (_nt(do16, v16) - delta)
            dsm = (pm * (_nt(do16, vm) - delta)).astype(BF16)
            ds16 = ds.astype(BF16)
            dq_ref[pl.ds(q0, GRID_W), :] = (_nn(ds16, k16) + _nn(dsm, km)) * scale
            dk_ref[pl.ds(k0, kw), :] += _tn(ds16, q16) * scale
            dv_ref[pl.ds(k0, kw), :] += _tn(p.astype(BF16), do16)
            for j in range(NA_WIN_H):
                dtb_ref[i0 + j] += ds[:, j * GRID_W:(j + 1) * GRID_W]
            return (dkm + _tn(dsm, q16) * scale, dvm + _tn(pm.astype(BF16), do16))

        dkm, dvm = lax.fori_loop(0, rows, step, (dkm0, dvm0))
        dk_ref[0:N_META, :] += dkm
        dv_ref[0:N_META, :] += dvm

    hspec = lambda which: pl.BlockSpec((None, None, lp, dh), lambda h: (which, h, 0, 0))
    h3 = pl.BlockSpec((None, lp, dh), lambda h: (h, 0, 0))
    tbs = pl.BlockSpec((None, 2 * NA_WIN_H - 1, GRID_W, GRID_W), lambda h: (h, 0, 0, 0))
    sds = jax.ShapeDtypeStruct((nh, lp, dh), F32)
    return pl.pallas_call(
        body, grid=(nh,),
        in_specs=[hspec(0), hspec(1), hspec(2), tbs, h3,
                  pl.BlockSpec((None, lp, 1), lambda h: (h, 0, 0)), h3],
        out_specs=[h3, h3, h3, tbs],
        out_shape=[sds, sds, sds, jax.ShapeDtypeStruct(tb.shape, F32)],
        compiler_params=_cp("parallel"), name=name)(qkv, qkv, qkv, tb, o, lse, do)


def _rpb_onehot():
    c = np.arange(GRID_W)[:, None]
    w = np.arange(GRID_W)[None, :]
    cs = np.clip(c - NA_WIN_W // 2, 0, GRID_W - NA_WIN_W)
    in_win = (w >= cs) & (w < cs + NA_WIN_W)
    dc = np.clip(w - c, -(NA_WIN_W - 1), NA_WIN_W - 1) + NA_WIN_W - 1
    oh = np.zeros((LANES, GRID_W * GRID_W), np.float32)
    flat = np.arange(GRID_W * GRID_W).reshape(GRID_W, GRID_W)
    oh[dc[in_win], flat[in_win]] = 1.0
    neg = np.where(in_win, 0.0, -1e30).astype(np.float32).reshape(1, -1)
    return oh, neg


def _assemble_dproj(dqkv, dq_f, dq_b, dz_f, dz_b, dv_f, dv_b, dg, dgn, dgh, *, name):
    _, nh, lp, dh = dqkv.shape
    naw = nh * dh
    hgw = dq_f.shape[1]
    d = dgn.shape[1]
    cols = 3 * naw + 5 * hgw + 2 * d
    tr = _row_tile(lp)

    def body(na_ref, qf_ref, qb_ref, zf_ref, zb_ref, vf_ref, vb_ref, g_ref, gn_ref, gh_ref, o_ref):
        for which in range(3):
            for h in range(nh):
                c0 = which * naw + h * dh
                o_ref[:, c0:c0 + dh] = na_ref[which, h].astype(BF16)
        c0 = 3 * naw
        o_ref[:, c0:c0 + hgw] = (qf_ref[...] + qb_ref[...]).astype(BF16)
        o_ref[:, c0 + hgw:c0 + 2 * hgw] = zf_ref[...].astype(BF16)
        o_ref[:, c0 + 2 * hgw:c0 + 3 * hgw] = zb_ref[...].astype(BF16)
        o_ref[:, c0 + 3 * hgw:c0 + 4 * hgw] = (vf_ref[...] + vb_ref[...]).astype(BF16)
        o_ref[:, c0 + 4 * hgw:c0 + 5 * hgw] = g_ref[...]
        o_ref[:, c0 + 5 * hgw:c0 + 5 * hgw + d] = gn_ref[...]
        o_ref[:, c0 + 5 * hgw + d:] = gh_ref[...]

    hg = _rspec(tr, hgw)
    return pl.pallas_call(
        body, grid=(lp // tr,),
        in_specs=[pl.BlockSpec((3, nh, tr, dh), lambda i: (0, 0, i, 0)),
                  hg, hg, hg, hg, hg, hg, hg, _rspec(tr, d), _rspec(tr, d)],
        out_specs=_rspec(tr, cols),
        out_shape=jax.ShapeDtypeStruct((lp, cols), BF16),
        compiler_params=_cp("parallel"), name=name)(dqkv, dq_f, dq_b, dz_f, dz_b, dv_f, dv_b, dg,
                                                    dgn, dgh)


def _adamw(w, g, m, v, *, name):
    rows, cols = w.shape
    tr = 256 if rows % 256 == 0 else rows

    def body(w_ref, g_ref, m_ref, v_ref, d_ref, mo_ref, vo_ref):
        gv = g_ref[...]
        mn = ADAM_B1 * m_ref[...] + (1.0 - ADAM_B1) * gv
        vn = ADAM_B2 * v_ref[...] + (1.0 - ADAM_B2) * (gv * gv)
        m_hat = mn / (1.0 - ADAM_B1 ** ADAM_STEP)
        v_hat = vn / (1.0 - ADAM_B2 ** ADAM_STEP)
        d_ref[...] = -ADAM_LR * (m_hat / (jnp.sqrt(v_hat) + ADAM_EPS) + ADAM_WD * w_ref[...])
        mo_ref[...] = mn
        vo_ref[...] = vn

    spec = _rspec(tr, cols)
    sds = jax.ShapeDtypeStruct((rows, cols), F32)
    return pl.pallas_call(
        body, grid=(rows // tr,), in_specs=[spec] * 4, out_specs=[spec] * 3, out_shape=[sds] * 3,
        compiler_params=_cp("parallel"), name=name)(w, g, m, v)


def _local_step(x, tgt, meta, w_in, w_na, w_hg, w_o, w_up, w_down, g_mix, g_mlp, g_fin, hg_gain,
                rpb, lb):
    n_tok, d = x.shape
    naw, hgw = w_na.shape[0], w_hg.shape[0]
    nh, hh = naw // NA_HEAD_DIM, hgw // HG_DK
    l_real = N_META + n_tok
    lp = -(-l_real // ROW_ALIGN) * ROW_ALIGN
    n_chunks = l_real // HG_CHUNK
    pad = lp - l_real
    col_qhg = 3 * naw
    col_zf, col_zb, col_i, col_g = (col_qhg + hgw, col_qhg + 2 * hgw, col_qhg + 3 * hgw,
                                    col_qhg + 4 * hgw)
    col_gate = col_qhg + 5 * hgw

    zpad = jnp.zeros((pad, d), F32)
    h0 = jnp.concatenate([meta, x, zpad], axis=0)
    tgt_p = jnp.concatenate([jnp.zeros((N_META, d), F32), tgt, zpad], axis=0)

    oh_np, neg_np = _rpb_onehot()
    oh = jnp.asarray(oh_np)
    rpb_p = jnp.pad(rpb.reshape(nh * (2 * NA_WIN_H - 1), 2 * NA_WIN_W - 1),
                    ((0, 0), (0, LANES - (2 * NA_WIN_W - 1))))
    tb = _matmul(rpb_p, oh, tm=rpb_p.shape[0], tn=512, tk=LANES, precision=HIGHEST,
                 name="rpb_expand")
    tb = (tb + jnp.asarray(neg_np)).reshape(nh, 2 * NA_WIN_H - 1, GRID_W, GRID_W)

    a = _rmsnorm_fwd(h0, g_mix, name="norm_mix")
    proj = _matmul(a, w_in, name="mm_in")
    qkv = proj[:, :3 * naw].reshape(lp, 3, nh, NA_HEAD_DIM).transpose(1, 2, 0, 3)
    o_na_hm, lse = _na_fwd(qkv, tb, n_tok=n_tok, name="na_fwd")
    o_na = o_na_hm.transpose(1, 0, 2).reshape(lp, naw)
    lb_f = lb[0].reshape(hh, 1, HG_DK)
    lb_b = lb[1].reshape(hh, 1, HG_DK)
    scan_kw = dict(col_q=col_qhg, col_i=col_i, hh=hh)
    o_f, st_f = _hg_scan_fwd(proj, lb_f, reverse=False, col_z=col_zf, name="hg_scan_f", **scan_kw)
    o_b, st_b = _hg_scan_fwd(proj, lb_b, reverse=True, col_z=col_zb, name="hg_scan_b", **scan_kw)
    o_hg = _hg_out(o_f, o_b, proj, hg_gain, col_g=col_g, name="hg_out")
    y_na = _matmul(o_na, w_na, name="mm_na_out")
    y_hg = _matmul(o_hg, w_hg, name="mm_hg_out")
    mix = _gate_mix(proj, y_na, y_hg, col_gate=col_gate, name="gate_mix")
    t1 = _matmul(mix, w_o, name="mm_o")
    h1, mlp_in = _residual_norm(h0, t1, g_mlp, name="resid_norm_mlp")
    u = _matmul(mlp_in, w_up, name="mm_up")
    act = _relu2(u, name="relu2")
    t2 = _matmul(act, w_down, name="mm_down")
    dh2, loss, dg_fin = _final_loss(h1, t2, g_fin, tgt_p, n_tok=n_tok, name="final_loss")

    dact = _matmul(dh2, w_down, tb=True, name="mm_down_dx")
    dw_down = _matmul(act, dh2, ta=True, name="mm_down_dw")
    du = _relu2_bwd(dact, u, name="relu2_bwd")
    dm = _matmul(du, w_up, tb=True, name="mm_up_dx")
    dw_up = _matmul(mlp_in, du, ta=True, name="mm_up_dw")
    dh1, dg_mlp = _rmsnorm_bwd_add(h1, g_mlp, dm, dh2, name="norm_mlp_bwd")
    dmix = _matmul(dh1, w_o, tb=True, name="mm_o_dx")
    dw_o = _matmul(mix, dh1, ta=True, name="mm_o_dw")
    dy_na, dy_hg, dgn, dgh = _gate_mix_bwd(proj, y_na, y_hg, dmix, col_gate=col_gate,
                                           name="gate_mix_bwd")
    do_na = _matmul(dy_na, w_na, tb=True, name="mm_na_out_dx")
    dw_na = _matmul(o_na, dy_na, ta=True, name="mm_na_out_dw")
    do_hg = _matmul(dy_hg, w_hg, tb=True, name="mm_hg_out_dx")
    dw_hg = _matmul(o_hg, dy_hg, ta=True, name="mm_hg_out_dw")
    d_o, dg_hg, d_gain = _hg_out_bwd(o_f, o_b, proj, hg_gain, do_hg, col_g=col_g, name="hg_out_bwd")
    dq_f, dz_f, dv_f, dlb_f = _hg_scan_bwd(proj, lb_f, st_f, d_o, reverse=False, col_z=col_zf,
                                           name="hg_scan_f_bwd", **scan_kw)
    dq_b, dz_b, dv_b, dlb_b = _hg_scan_bwd(proj, lb_b, st_b, d_o, reverse=True, col_z=col_zb,
                                           name="hg_scan_b_bwd", **scan_kw)
    do_na_hm = do_na.reshape(lp, nh, NA_HEAD_DIM).transpose(1, 0, 2)
    dq_na, dk_na, dv_na, dtb = _na_bwd(qkv, tb, o_na_hm, lse, do_na_hm, n_tok=n_tok, name="na_bwd")
    dqkv = jnp.stack([dq_na, dk_na, dv_na], axis=0)
    dproj = _assemble_dproj(dqkv, dq_f, dq_b, dz_f, dz_b, dv_f, dv_b, dg_hg, dgn, dgh,
                            name="assemble_dproj")
    da = _matmul(dproj, w_in, tb=True, name="mm_in_dx")
    dw_in = _matmul(a, dproj, ta=True, name="mm_in_dw")
    dh0, dg_mix = _rmsnorm_bwd_add(h0, g_mix, da, dh1, name="norm_mix_bwd")
    d_rpb = _matmul(dtb.reshape(nh * (2 * NA_WIN_H - 1), GRID_W * GRID_W), oh, tb=True,
                    tm=nh * (2 * NA_WIN_H - 1), tn=LANES, tk=1024, precision=HIGHEST,
                    name="rpb_reduce")
    d_lb = jnp.concatenate([dlb_f.reshape(1, hgw), dlb_b.reshape(1, hgw)], axis=0)
    return (loss, dh0[N_META:l_real], dh0[:N_META], dw_in, dw_na, dw_hg, dw_o, dw_up, dw_down,
            dg_mix, dg_mlp, dg_fin, d_gain, d_rpb, d_lb)


N_CHIPS = 4
N_DEV = 8
ANY = pl.BlockSpec(memory_space=pl.ANY)


def _place():
    x, y, c = lax.axis_index("x"), lax.axis_index("y"), lax.axis_index("c")
    others = []
    for j in (1, 2, 3):
        tx = (1 - x) if (j >> 1) else x
        ty = (1 - y) if (j & 1) else y
        others.append((tx, ty))
    return x, y, c, others


def _piece(ref, axis, k, half, rh, cs):
    if axis == 1:
        return ref.at[pl.ds(pl.multiple_of(half * rh, 16), rh), pl.ds(pl.multiple_of(k * cs, LANES), cs)]
    return ref.at[pl.ds(pl.multiple_of(k * 2 * rh + half * rh, 16), rh), :]


def _cast_into_full(shard, axis, place, *, name):
    r, cs = shard.shape
    full = (r, cs * N_CHIPS) if axis == 1 else (r * N_CHIPS, cs)
    tr = next(t for t in (256, 128, 64, 32, 16) if r % t == 0)
    nt = r // tr

    def body(p_ref, s_ref, o_ref):
        o_ref[...] = s_ref[...].astype(BF16)

    if axis == 1:
        omap = lambda i, p_ref: (i, p_ref[0])
    else:
        omap = lambda i, p_ref: (p_ref[0] * nt + i, 0)
    return pl.pallas_call(
        body,
        grid_spec=pltpu.PrefetchScalarGridSpec(
            num_scalar_prefetch=1, grid=(nt,),
            in_specs=[pl.BlockSpec((tr, cs), lambda i, p_ref: (i, 0))],
            out_specs=pl.BlockSpec((tr, cs), omap)),
        out_shape=jax.ShapeDtypeStruct(full, BF16),
        compiler_params=_cp("parallel"), name=name)(place, shard)


def _weight_allgather(fulls, axes, *, name):
    n = len(fulls)
    geo = []
    for f, ax in zip(fulls, axes):
        r, cs = (f.shape[0], f.shape[1] // N_CHIPS) if ax == 1 else (f.shape[0] // N_CHIPS, f.shape[1])
        geo.append((ax, r // 2, cs))

    def body(*refs):
        o_refs = refs[n:2 * n]
        send_sems, recv_sems = refs[2 * n:]
        x, y, c, others = _place()
        chip = 2 * x + y
        sib = (x, y, 1 - c)

        def rcopy(i, slot, ref, to):
            return pltpu.make_async_remote_copy(
                src_ref=ref, dst_ref=ref, send_sem=send_sems.at[6 * i + slot],
                recv_sem=recv_sems.at[6 * i + slot], device_id=to, device_id_type=MESH)

        started = []
        for i in range(n):
            ax, rh, cs = geo[i]
            mine = _piece(o_refs[i], ax, chip, c, rh, cs)
            for j, (tx, ty) in enumerate(others):
                cp = rcopy(i, j, mine, (tx, ty, c))
                cp.start()
                started.append(cp)
        for i in range(n):
            ax, rh, cs = geo[i]
            for j, (tx, ty) in enumerate(others):
                got = _piece(o_refs[i], ax, 2 * tx + ty, c, rh, cs)
                rcopy(i, j, got, (x, y, c)).wait_recv()
                cp = rcopy(i, 3 + j, got, sib)
                cp.start()
                started.append(cp)
        for i in range(n):
            ax, rh, cs = geo[i]
            for j, (tx, ty) in enumerate(others):
                got = _piece(o_refs[i], ax, 2 * tx + ty, 1 - c, rh, cs)
                rcopy(i, 3 + j, got, (x, y, c)).wait_recv()
        for cp in started:
            cp.wait_send()

    return pl.pallas_call(
        body, in_specs=[ANY] * n, out_specs=[ANY] * n,
        out_shape=[jax.ShapeDtypeStruct(f.shape, f.dtype) for f in fulls],
        input_output_aliases={i: i for i in range(n)},
        scratch_shapes=[pltpu.SemaphoreType.DMA((6 * n,)), pltpu.SemaphoreType.DMA((6 * n,))],
        name=name)(*fulls)


def _sibling_swap(grads, *, name):
    n = len(grads)
    out_shape = [jax.ShapeDtypeStruct((g.shape[0], g.shape[1] // 2, g.shape[2]), g.dtype)
                 for g in grads]

    def body(*refs):
        g_refs, o_refs = refs[:n], refs[n:2 * n]
        send_sems, recv_sems = refs[2 * n:]
        x, y, c, _ = _place()
        cps = []
        for i in range(n):
            rh = grads[i].shape[1] // 2
            src = g_refs[i].at[:, pl.ds(pl.multiple_of((1 - c) * rh, 16), rh), :]
            cp = pltpu.make_async_remote_copy(
                src_ref=src, dst_ref=o_refs[i], send_sem=send_sems.at[i], recv_sem=recv_sems.at[i],
                device_id=(x, y, 1 - c), device_id_type=MESH)
            cp.start()
            cps.append(cp)
        for cp in cps:
            cp.wait()

    return pl.pallas_call(
        body, in_specs=[ANY] * n, out_specs=[ANY] * n, out_shape=out_shape,
        scratch_shapes=[pltpu.SemaphoreType.DMA((n,)), pltpu.SemaphoreType.DMA((n,))],
        name=name)(*grads)


def _pair_add(g3, rx, c_arr, *, out_dtype, name):
    nb, rows, cols = g3.shape
    rh = rows // 2
    tr = next(t for t in (128, 64, 32, 16) if rh % t == 0)
    nt = rh // tr

    def body(c_ref, g_ref, r_ref, o_ref):
        o_ref[...] = (g_ref[...] + r_ref[...]).astype(out_dtype)

    return pl.pallas_call(
        body,
        grid_spec=pltpu.PrefetchScalarGridSpec(
            num_scalar_prefetch=1, grid=(nb, nt),
            in_specs=[pl.BlockSpec((None, tr, cols), lambda b, i, c_ref: (b, c_ref[0] * nt + i, 0)),
                      pl.BlockSpec((None, tr, cols), lambda b, i, c_ref: (b, i, 0))],
            out_specs=pl.BlockSpec((None, tr, cols), lambda b, i, c_ref: (b, i, 0))),
        out_shape=jax.ShapeDtypeStruct((nb, rh, cols), out_dtype),
        compiler_params=_cp("parallel", "parallel"), name=name)(c_arr, g3, rx)


def _chip_scatter(parts, axes, *, name):
    n = len(parts)
    geo, out_shape = [], []
    for p, ax in zip(parts, axes):
        nb, rh, cols = p.shape
        cw = cols // N_CHIPS if ax == 1 else cols
        geo.append((ax, rh, cw))
        out_shape.append(jax.ShapeDtypeStruct((N_CHIPS, rh, cw), p.dtype))

    def body(*refs):
        p_refs, o_refs = refs[:n], refs[n:2 * n]
        send_sems, recv_sems = refs[2 * n:]
        x, y, c, others = _place()
        chip = 2 * x + y

        def block(i, k):
            ax, rh, cw = geo[i]
            if ax == 1:
                return p_refs[i].at[0, :, pl.ds(pl.multiple_of(k * cw, LANES), cw)]
            return p_refs[i].at[k]

        cps = []
        for i in range(n):
            for j, (tx, ty) in enumerate(others):
                cp = pltpu.make_async_remote_copy(
                    src_ref=block(i, 2 * tx + ty), dst_ref=o_refs[i].at[chip],
                    send_sem=send_sems.at[3 * i + j], recv_sem=recv_sems.at[3 * i + j],
                    device_id=(tx, ty, c), device_id_type=MESH)
                cp.start()
                cps.append(cp)
        for i in range(n):
            for j, (tx, ty) in enumerate(others):
                got = o_refs[i].at[2 * tx + ty]
                pltpu.make_async_remote_copy(
                    src_ref=got, dst_ref=got, send_sem=send_sems.at[3 * i + j],
                    recv_sem=recv_sems.at[3 * i + j], device_id=(x, y, c),
                    device_id_type=MESH).wait_recv()
        for cp in cps:
            cp.wait_send()

    return pl.pallas_call(
        body, in_specs=[ANY] * n, out_specs=[ANY] * n, out_shape=out_shape,
        scratch_shapes=[pltpu.SemaphoreType.DMA((3 * n,)), pltpu.SemaphoreType.DMA((3 * n,))],
        name=name)(*parts)


def _sum_slots(q, *, name):
    ns, rows, cols = q.shape
    tr = next(t for t in (128, 64, 32, 16, 8) if rows % t == 0)

    def body(q_ref, o_ref):
        acc = q_ref[0].astype(F32)
        for k in range(1, ns):
            acc = acc + q_ref[k].astype(F32)
        o_ref[...] = acc

    return pl.pallas_call(
        body, grid=(rows // tr,),
        in_specs=[pl.BlockSpec((ns, tr, cols), lambda i: (0, i, 0))],
        out_specs=_rspec(tr, cols),
        out_shape=jax.ShapeDtypeStruct((rows, cols), F32),
        compiler_params=_cp("parallel"), name=name)(q)


def _sum_chips(q, p, place, axis, *, name):
    _, rh, cw = q.shape
    tr = next(t for t in (128, 64, 32, 16) if rh % t == 0)
    nt = rh // tr

    def body(p_ref, *refs):
        q_refs, own_ref, o_ref = refs[:N_CHIPS], refs[N_CHIPS], refs[N_CHIPS + 1]
        chip = p_ref[0]
        acc = jnp.where(chip == 0, own_ref[...], q_refs[0][...]).astype(F32)
        for k in range(1, N_CHIPS):
            acc = acc + jnp.where(chip == k, own_ref[...], q_refs[k][...]).astype(F32)
        o_ref[...] = acc

    def slot_spec(k):
        return pl.BlockSpec((None, tr, cw),
                            lambda i, p_ref: (jnp.where(p_ref[0] == k, (k + 1) % N_CHIPS, k), i, 0))

    if axis == 1:
        own_spec = pl.BlockSpec((None, tr, cw), lambda i, p_ref: (0, i, p_ref[0]))
    else:
        own_spec = pl.BlockSpec((None, tr, cw), lambda i, p_ref: (p_ref[0], i, 0))
    return pl.pallas_call(
        body,
        grid_spec=pltpu.PrefetchScalarGridSpec(
            num_scalar_prefetch=1, grid=(nt,),
            in_specs=[slot_spec(k) for k in range(N_CHIPS)] + [own_spec],
            out_specs=pl.BlockSpec((tr, cw), lambda i, p_ref: (p_ref[1] * nt + i, 0))),
        out_shape=jax.ShapeDtypeStruct((2 * rh, cw), F32),
        compiler_params=_cp("parallel"), name=name)(place, *([q] * N_CHIPS), p)


def _sibling_share(shards, *, name):
    n = len(shards)

    def body(*refs):
        o_refs = refs[n:2 * n]
        send_sems, recv_sems = refs[2 * n:]
        x, y, c, _ = _place()
        cps = []
        for i in range(n):
            rh = shards[i].shape[0] // 2
            mine = o_refs[i].at[pl.ds(pl.multiple_of(c * rh, 8), rh), :]
            cp = pltpu.make_async_remote_copy(
                src_ref=mine, dst_ref=mine, send_sem=send_sems.at[i], recv_sem=recv_sems.at[i],
                device_id=(x, y, 1 - c), device_id_type=MESH)
            cp.start()
            cps.append(cp)
        for i in range(n):
            rh = shards[i].shape[0] // 2
            theirs = o_refs[i].at[pl.ds(pl.multiple_of((1 - c) * rh, 8), rh), :]
            pltpu.make_async_remote_copy(
                src_ref=theirs, dst_ref=theirs, send_sem=send_sems.at[i], recv_sem=recv_sems.at[i],
                device_id=(x, y, c), device_id_type=MESH).wait_recv()
        for cp in cps:
            cp.wait_send()

    return pl.pallas_call(
        body, in_specs=[ANY] * n, out_specs=[ANY] * n,
        out_shape=[jax.ShapeDtypeStruct(h.shape, h.dtype) for h in shards],
        input_output_aliases={i: i for i in range(n)},
        scratch_shapes=[pltpu.SemaphoreType.DMA((n,)), pltpu.SemaphoreType.DMA((n,))],
        name=name)(*shards)


def _gather_all(blk, *, name):
    rows, cols = blk.shape

    def body(x_ref, out_ref, send_sems, recv_sems, local_sem):
        x, y, c = lax.axis_index("x"), lax.axis_index("y"), lax.axis_index("c")
        me = 4 * x + 2 * y + c
        mine = pltpu.make_async_copy(x_ref, out_ref.at[me], local_sem)
        mine.start()
        cps = []
        for k in range(1, N_DEV):
            tx = (1 - x) if (k >> 2) & 1 else x
            ty = (1 - y) if (k >> 1) & 1 else y
            tc = (1 - c) if k & 1 else c
            cp = pltpu.make_async_remote_copy(
                src_ref=x_ref, dst_ref=out_ref.at[me], send_sem=send_sems.at[k - 1],
                recv_sem=recv_sems.at[k - 1], device_id=(tx, ty, tc), device_id_type=MESH)
            cp.start()
            cps.append(cp)
        for k in range(1, N_DEV):
            tx = (1 - x) if (k >> 2) & 1 else x
            ty = (1 - y) if (k >> 1) & 1 else y
            tc = (1 - c) if k & 1 else c
            got = out_ref.at[4 * tx + 2 * ty + tc]
            pltpu.make_async_remote_copy(
                src_ref=got, dst_ref=got, send_sem=send_sems.at[k - 1], recv_sem=recv_sems.at[k - 1],
                device_id=(x, y, c), device_id_type=MESH).wait_recv()
        for cp in cps:
            cp.wait_send()
        mine.wait()

    vm = pl.BlockSpec(memory_space=pltpu.VMEM)
    return pl.pallas_call(
        body, in_specs=[vm], out_specs=vm,
        out_shape=jax.ShapeDtypeStruct((N_DEV, rows, cols), blk.dtype),
        scratch_shapes=[pltpu.SemaphoreType.DMA((N_DEV - 1,)), pltpu.SemaphoreType.DMA((N_DEV - 1,)),
                        pltpu.SemaphoreType.DMA],
        name=name)(blk)


def _as_rows(a):
    flat = a.reshape(-1)
    n = flat.shape[0]
    rows = -(-n // (8 * LANES)) * 8
    return jnp.pad(flat, (0, rows * LANES - n)).reshape(rows, LANES)


def _from_rows(p, shape):
    n = int(np.prod(shape))
    return p.reshape(-1)[:n].reshape(shape)


WEIGHT_AXES = (1, 1, 1, 0, 1, 0)
WIRE = BF16


def kernel(x, meta_tokens, w_in, w_na_out, w_hg_out, w_o, w_up, w_down, norm_mix, norm_mlp, norm_final, hg_norm, na_rpb, hg_lb_logits, loss_target, m_meta_tokens, m_w_in, m_w_na_out, m_w_hg_out, m_w_o, m_w_up, m_w_down, m_norm_mix, m_norm_mlp, m_norm_final, m_hg_norm, m_na_rpb, m_hg_lb_logits, v_meta_tokens, v_w_in, v_w_na_out, v_w_hg_out, v_w_o, v_w_up, v_w_down, v_norm_mix, v_norm_mlp, v_norm_final, v_hg_norm, v_na_rpb, v_hg_lb_logits):
    xi, yi, ci = lax.axis_index("x"), lax.axis_index("y"), lax.axis_index("c")
    chip = 2 * xi + yi
    d = x.shape[-1]
    dshard = meta_tokens.shape[1]
    hgw = hg_norm.shape[1]
    lbs = hg_lb_logits.shape[2]
    big = [w_in[0], w_na_out[0], w_hg_out[0], w_o[0], w_up[0], w_down[0]]
    big_m = [m_w_in[0], m_w_na_out[0], m_w_hg_out[0], m_w_o[0], m_w_up[0], m_w_down[0]]
    big_v = [v_w_in[0], v_w_na_out[0], v_w_hg_out[0], v_w_o[0], v_w_up[0], v_w_down[0]]

    place = jnp.stack([chip, ci]).astype(jnp.int32)
    own_w = [_cast_into_full(w, ax, place, name=f"cast_shard_{i}")
             for i, (w, ax) in enumerate(zip(big, WEIGHT_AXES))]
    full_w = _weight_allgather(own_w, WEIGHT_AXES, name="weight_allgather")
    small_in = jnp.concatenate([_as_rows(meta_tokens), _as_rows(hg_lb_logits)], axis=0)
    small_all = _gather_all(small_in, name="gather_small_params")[0::2]
    n_meta_rows = N_META * dshard // LANES
    meta_full = (small_all[:, :n_meta_rows].reshape(N_CHIPS, N_META, dshard)
                 .transpose(1, 0, 2).reshape(N_META, d))
    lbl_full = (small_all[:, n_meta_rows:].reshape(N_CHIPS, -1)[:, :4 * lbs]
                .reshape(N_CHIPS, 2, 2, lbs).transpose(1, 2, 0, 3).reshape(2, 2, N_CHIPS * lbs))
    lb = jax.nn.softmax(lbl_full, axis=1)[:, 0]

    (loss, dx, dmeta, *dws, dg_mix, dg_mlp, dg_fin, d_gain, d_rpb, d_lb) = _local_step(
        x[0], loss_target[0], meta_full, *full_w, norm_mix, norm_mlp, norm_final.reshape(1, d),
        hg_norm, na_rpb[0], lb)

    c_arr = ci.reshape(1).astype(jnp.int32)
    g3 = [g.reshape(1, *g.shape) if ax == 1 else g.reshape(N_CHIPS, g.shape[0] // N_CHIPS, g.shape[1])
          for g, ax in zip(dws, WEIGHT_AXES)]
    rx = _sibling_swap(g3, name="grad_sibling_swap")
    parts = [_pair_add(g, r, c_arr, out_dtype=WIRE, name=f"grad_pair_add_{i}")
             for i, (g, r) in enumerate(zip(g3, rx))]
    slots = _chip_scatter(parts, WEIGHT_AXES, name="grad_chip_scatter")
    halves = [_sum_chips(q, p, place, ax, name=f"grad_sum_chips_{i}")
              for i, (q, p, ax) in enumerate(zip(slots, parts, WEIGHT_AXES))]
    g_big = _sibling_share(halves, name="grad_sibling_share")

    d_rpb_c = d_rpb[:, :2 * NA_WIN_W - 1]
    small_g = [dmeta, dg_mix, dg_mlp, dg_fin, d_gain, d_rpb_c, d_lb, loss]
    packed = jnp.concatenate([_as_rows(a) for a in small_g], axis=0)
    total = _sum_slots(_gather_all(packed, name="gather_small_grads"), name="sum_small_grads")
    offs = np.cumsum([0] + [_as_rows(a).shape[0] for a in small_g])
    take = lambda i, shape: _from_rows(total[offs[i]:offs[i + 1]], shape)
    g_meta_full = take(0, (N_META, d))
    g_norm_mix, g_norm_mlp = take(1, (1, d)), take(2, (1, d))
    g_norm_final = take(3, (d,))
    g_hg_norm = take(4, (1, hgw))
    g_rpb = take(5, na_rpb.shape)
    g_lb = take(6, (2, hgw))
    loss_total = take(7, (1, LANES))[0, 0]
    g_meta = lax.dynamic_slice_in_dim(g_meta_full, chip * dshard, dshard, axis=1)
    dl0 = lb * (1.0 - lb) * g_lb
    g_lbl_full = jnp.stack([dl0, -dl0], axis=1)
    g_lbl = lax.dynamic_slice_in_dim(g_lbl_full, chip * lbs, lbs, axis=2)

    big_out = [_adamw(w, g, m, v, name=f"adamw_{i}")
               for i, (w, g, m, v) in enumerate(zip(big, g_big, big_m, big_v))]
    small_w = [meta_tokens, norm_mix, norm_mlp, norm_final, hg_norm, na_rpb, hg_lb_logits]
    small_gr = [g_meta, g_norm_mix, g_norm_mlp, g_norm_final, g_hg_norm, g_rpb, g_lbl]
    small_m = [m_meta_tokens, m_norm_mix, m_norm_mlp, m_norm_final, m_hg_norm, m_na_rpb, m_hg_lb_logits]
    small_v = [v_meta_tokens, v_norm_mix, v_norm_mlp, v_norm_final, v_hg_norm, v_na_rpb, v_hg_lb_logits]
    pk = lambda lst: jnp.concatenate([_as_rows(a) for a in lst], axis=0)
    sd, sm, sv = _adamw(pk(small_w), pk(small_gr), pk(small_m), pk(small_v), name="adamw_small")
    soffs = np.cumsum([0] + [_as_rows(a).shape[0] for a in small_w])
    unpk = lambda p: [_from_rows(p[soffs[i]:soffs[i + 1]], small_w[i].shape) for i in range(len(small_w))]
    sd, sm, sv = unpk(sd), unpk(sm), unpk(sv)

    def order(bigs, smalls):
        return [smalls[0]] + [b.reshape(1, *b.shape) for b in bigs] + smalls[1:]

    grads = order(g_big, small_gr)
    deltas = order([o[0] for o in big_out], sd)
    new_m = order([o[1] for o in big_out], sm)
    new_v = order([o[2] for o in big_out], sv)
    return (loss_total, dx.reshape(1, *dx.shape), *grads, *deltas, *new_m, *new_v)
```

```python
import functools

import numpy as np
import jax
import jax.numpy as jnp
from jax import lax
from jax.experimental import pallas as pl
from jax.experimental.pallas import tpu as pltpu

F32 = jnp.float32
BF16 = jnp.bfloat16
HIGHEST = lax.Precision.HIGHEST

GRID_W = 64
N_META = 16
EPS = 1e-6
NA_HEAD_DIM = 64
NA_WIN_H = 8
NA_WIN_W = 16
HG_DK = 128
HG_CHUNK = 16
LANES = 128
ROW_ALIGN = 128
VMEM_LIMIT = 48 * 1024 * 1024

ADAM_LR = 0.001
ADAM_B1 = 0.9
ADAM_B2 = 0.999
ADAM_EPS = 1e-08
ADAM_WD = 0.01
ADAM_STEP = 10

MESH = pl.DeviceIdType.MESH


def _cp(*sem):
    return pltpu.CompilerParams(dimension_semantics=sem, vmem_limit_bytes=VMEM_LIMIT)


def _sigmoid(x):
    return 1.0 / (1.0 + jnp.exp(-x))


def _dot(a, b, dims, precision=None):
    return lax.dot_general(a, b, (dims, ((), ())), preferred_element_type=F32, precision=precision)


def _nn(a, b, **kw):
    return _dot(a, b, ((1,), (0,)), **kw)


def _nt(a, b, **kw):
    return _dot(a, b, ((1,), (1,)), **kw)


def _tn(a, b, **kw):
    return _dot(a, b, ((0,), (0,)), **kw)


def _matmul(a, b, *, ta=False, tb=False, tm=None, tn=None, tk=None, out_dtype=F32, name,
            precision=None):
    if ta:
        kdim, m = a.shape
    else:
        m, kdim = a.shape
    if tb:
        n, k2 = b.shape
    else:
        k2, n = b.shape
    assert kdim == k2, (a.shape, b.shape, ta, tb)
    if tm is None:
        if ta:
            tm = next(t for t in (512, 256, 128, m) if m % t == 0)
        else:
            tm = m // 2 if (m // 2) % 16 == 0 and m > 512 else m
    if tn is None:
        tn = next(t for t in (512, 256, 128, n) if n % t == 0)
    if tk is None:
        tk = kdim if ta else next(t for t in (1024, 512, 256, 128, kdim) if kdim % t == 0)
    assert m % tm == 0 and n % tn == 0 and kdim % tk == 0, (m, n, kdim, tm, tn, tk)
    nk = kdim // tk
    op_dtype = F32 if precision is not None else BF16

    def body(a_ref, b_ref, o_ref, acc_ref):
        kk = pl.program_id(2)

        @pl.when(kk == 0)
        def _():
            acc_ref[...] = jnp.zeros_like(acc_ref)

        av = a_ref[...].astype(op_dtype)
        bv = b_ref[...].astype(op_dtype)
        dims = ((0 if ta else 1,), (1 if tb else 0,))
        acc_ref[...] += _dot(av, bv, dims, precision=precision)

        @pl.when(kk == nk - 1)
        def _():
            o_ref[...] = acc_ref[...].astype(out_dtype)

    a_spec = (pl.BlockSpec((tk, tm), lambda i, j, k: (k, i)) if ta
              else pl.BlockSpec((tm, tk), lambda i, j, k: (i, k)))
    b_spec = (pl.BlockSpec((tn, tk), lambda i, j, k: (j, k)) if tb
              else pl.BlockSpec((tk, tn), lambda i, j, k: (k, j)))
    return pl.pallas_call(
        body,
        grid=(m // tm, n // tn, nk),
        in_specs=[a_spec, b_spec],
        out_specs=pl.BlockSpec((tm, tn), lambda i, j, k: (i, j)),
        out_shape=jax.ShapeDtypeStruct((m, n), out_dtype),
        scratch_shapes=[pltpu.VMEM((tm, tn), F32)],
        compiler_params=_cp("parallel", "parallel", "arbitrary"),
        name=name,
    )(a, b)


def _rspec(tr, w, cb=0):
    return pl.BlockSpec((tr, w), lambda i: (i, cb))


def _fspec(shape):
    nd = len(shape)
    return pl.BlockSpec(shape, lambda i: (0,) * nd)


def _row_tile(lp):
    return ROW_ALIGN if lp % ROW_ALIGN == 0 else lp


def _rmsnorm_fwd(x, g, *, name):
    lp, d = x.shape
    tr = _row_tile(lp)

    def body(x_ref, g_ref, o_ref):
        xv = x_ref[...]
        r = lax.rsqrt(jnp.mean(xv * xv, axis=-1, keepdims=True) + EPS)
        o_ref[...] = (xv * r * g_ref[...]).astype(BF16)

    return pl.pallas_call(
        body, grid=(lp // tr,),
        in_specs=[_rspec(tr, d), _fspec((1, d))],
        out_specs=_rspec(tr, d),
        out_shape=jax.ShapeDtypeStruct((lp, d), BF16),
        compiler_params=_cp("parallel"), name=name)(x, g)


def _residual_norm(h, t, g, *, name):
    lp, d = h.shape
    tr = _row_tile(lp)

    def body(h_ref, t_ref, g_ref, h1_ref, m_ref):
        xv = h_ref[...] + t_ref[...]
        h1_ref[...] = xv
        r = lax.rsqrt(jnp.mean(xv * xv, axis=-1, keepdims=True) + EPS)
        m_ref[...] = (xv * r * g_ref[...]).astype(BF16)

    return pl.pallas_call(
        body, grid=(lp // tr,),
        in_specs=[_rspec(tr, d), _rspec(tr, d), _fspec((1, d))],
        out_specs=[_rspec(tr, d), _rspec(tr, d)],
        out_shape=[jax.ShapeDtypeStruct((lp, d), F32), jax.ShapeDtypeStruct((lp, d), BF16)],
        compiler_params=_cp("parallel"), name=name)(h, t, g)


def _rmsnorm_bwd_add(x, g, dy, dres, *, name):
    lp, d = x.shape
    tr = _row_tile(lp)

    def body(x_ref, g_ref, dy_ref, dr_ref, dx_ref, dg_ref):
        @pl.when(pl.program_id(0) == 0)
        def _():
            dg_ref[...] = jnp.zeros_like(dg_ref)

        xv = x_ref[...]
        r = lax.rsqrt(jnp.mean(xv * xv, axis=-1, keepdims=True) + EPS)
        xh = xv * r
        dyv = dy_ref[...]
        dg_ref[...] += jnp.sum(dyv * xh, axis=0, keepdims=True)
        dxh = dyv * g_ref[...]
        dx_ref[...] = dr_ref[...] + r * (dxh - xh * jnp.mean(dxh * xh, axis=-1, keepdims=True))

    return pl.pallas_call(
        body, grid=(lp // tr,),
        in_specs=[_rspec(tr, d), _fspec((1, d)), _rspec(tr, d), _rspec(tr, d)],
        out_specs=[_rspec(tr, d), _fspec((1, d))],
        out_shape=[jax.ShapeDtypeStruct((lp, d), F32), jax.ShapeDtypeStruct((1, d), F32)],
        compiler_params=_cp("arbitrary"), name=name)(x, g, dy, dres)


def _final_loss(h1, t2, g, tgt, *, n_tok, name):
    lp, d = h1.shape
    tr = _row_tile(lp)

    def body(h_ref, t_ref, g_ref, tg_ref, dh_ref, loss_ref, dg_ref):
        i = pl.program_id(0)

        @pl.when(i == 0)
        def _():
            loss_ref[...] = jnp.zeros_like(loss_ref)
            dg_ref[...] = jnp.zeros_like(dg_ref)

        xv = h_ref[...] + t_ref[...]
        r = lax.rsqrt(jnp.mean(xv * xv, axis=-1, keepdims=True) + EPS)
        xh = xv * r
        gv = g_ref[...]
        row = i * tr + lax.broadcasted_iota(jnp.int32, (tr, 1), 0)
        valid = (row >= N_META) & (row < N_META + n_tok)
        err = jnp.where(valid, xh * gv - tg_ref[...], 0.0)
        loss_ref[...] += jnp.sum(0.5 * err * err) / d
        dy = err / d
        dg_ref[...] += jnp.sum(dy * xh, axis=0, keepdims=True)
        dxh = dy * gv
        dh_ref[...] = r * (dxh - xh * jnp.mean(dxh * xh, axis=-1, keepdims=True))

    return pl.pallas_call(
        body, grid=(lp // tr,),
        in_specs=[_rspec(tr, d), _rspec(tr, d), _fspec((1, d)), _rspec(tr, d)],
        out_specs=[_rspec(tr, d), _fspec((1, LANES)), _fspec((1, d))],
        out_shape=[jax.ShapeDtypeStruct((lp, d), F32), jax.ShapeDtypeStruct((1, LANES), F32),
                   jax.ShapeDtypeStruct((1, d), F32)],
        compiler_params=_cp("arbitrary"), name=name)(h1, t2, g, tgt)


def _relu2(u, *, name):
    lp, f = u.shape
    tr = _row_tile(lp)

    def body(u_ref, o_ref):
        rv = jnp.maximum(u_ref[...], 0.0)
        o_ref[...] = (rv * rv).astype(BF16)

    return pl.pallas_call(
        body, grid=(lp // tr,), in_specs=[_rspec(tr, f)], out_specs=_rspec(tr, f),
        out_shape=jax.ShapeDtypeStruct((lp, f), BF16),
        compiler_params=_cp("parallel"), name=name)(u)


def _relu2_bwd(dact, u, *, name):
    lp, f = u.shape
    tr = _row_tile(lp)

    def body(d_ref, u_ref, o_ref):
        o_ref[...] = (d_ref[...] * 2.0 * jnp.maximum(u_ref[...], 0.0)).astype(BF16)

    return pl.pallas_call(
        body, grid=(lp // tr,), in_specs=[_rspec(tr, f), _rspec(tr, f)], out_specs=_rspec(tr, f),
        out_shape=jax.ShapeDtypeStruct((lp, f), BF16),
        compiler_params=_cp("parallel"), name=name)(dact, u)


def _gate_mix(proj, y_na, y_hg, *, col_gate, name):
    lp, d = y_na.shape
    tr = _row_tile(lp)
    cb = col_gate // d

    def body(gn_ref, gh_ref, yn_ref, yh_ref, o_ref):
        o_ref[...] = (_sigmoid(gn_ref[...]) * yn_ref[...]
                      + _sigmoid(gh_ref[...]) * yh_ref[...]).astype(BF16)

    return pl.pallas_call(
        body, grid=(lp // tr,),
        in_specs=[_rspec(tr, d, cb), _rspec(tr, d, cb + 1), _rspec(tr, d), _rspec(tr, d)],
        out_specs=_rspec(tr, d),
        out_shape=jax.ShapeDtypeStruct((lp, d), BF16),
        compiler_params=_cp("parallel"), name=name)(proj, proj, y_na, y_hg)


def _gate_mix_bwd(proj, y_na, y_hg, dmix, *, col_gate, name):
    lp, d = y_na.shape
    tr = _row_tile(lp)
    cb = col_gate // d

    def body(gn_ref, gh_ref, yn_ref, yh_ref, dm_ref, dyn_ref, dyh_ref, dgn_ref, dgh_ref):
        dm = dm_ref[...]
        sn = _sigmoid(gn_ref[...])
        sh = _sigmoid(gh_ref[...])
        dyn_ref[...] = (dm * sn).astype(BF16)
        dyh_ref[...] = (dm * sh).astype(BF16)
        dgn_ref[...] = (dm * yn_ref[...] * sn * (1.0 - sn)).astype(BF16)
        dgh_ref[...] = (dm * yh_ref[...] * sh * (1.0 - sh)).astype(BF16)

    sds = jax.ShapeDtypeStruct((lp, d), BF16)
    return pl.pallas_call(
        body, grid=(lp // tr,),
        in_specs=[_rspec(tr, d, cb), _rspec(tr, d, cb + 1), _rspec(tr, d), _rspec(tr, d),
                  _rspec(tr, d)],
        out_specs=[_rspec(tr, d)] * 4, out_shape=[sds] * 4,
        compiler_params=_cp("parallel"), name=name)(proj, proj, y_na, y_hg, dmix)


def _hg_out(o_f, o_b, proj, gain, *, col_g, name):
    lp, w = o_f.shape
    tr = _row_tile(lp)
    hh = w // HG_DK

    def body(of_ref, ob_ref, g_ref, gain_ref, y_ref):
        gv = g_ref[...]
        sg = gv * _sigmoid(gv)
        for h in range(hh):
            sl = slice(h * HG_DK, (h + 1) * HG_DK)
            o = of_ref[:, sl] + ob_ref[:, sl]
            r = lax.rsqrt(jnp.mean(o * o, axis=-1, keepdims=True) + EPS)
            y_ref[:, sl] = (o * r * gain_ref[:, sl] * sg[:, sl]).astype(BF16)

    return pl.pallas_call(
        body, grid=(lp // tr,),
        in_specs=[_rspec(tr, w), _rspec(tr, w), _rspec(tr, w, col_g // w), _fspec((1, w))],
        out_specs=_rspec(tr, w),
        out_shape=jax.ShapeDtypeStruct((lp, w), BF16),
        compiler_params=_cp("parallel"), name=name)(o_f, o_b, proj, gain)


def _hg_out_bwd(o_f, o_b, proj, gain, dy, *, col_g, name):
    lp, w = o_f.shape
    tr = _row_tile(lp)
    hh = w // HG_DK

    def body(of_ref, ob_ref, g_ref, gain_ref, dy_ref, do_ref, dg_ref, dgain_ref):
        @pl.when(pl.program_id(0) == 0)
        def _():
            dgain_ref[...] = jnp.zeros_like(dgain_ref)

        for h in range(hh):
            sl = slice(h * HG_DK, (h + 1) * HG_DK)
            gv = g_ref[:, sl]
            s = _sigmoid(gv)
            sg = gv * s
            dsg = s + gv * s * (1.0 - s)
            o = of_ref[:, sl] + ob_ref[:, sl]
            r = lax.rsqrt(jnp.mean(o * o, axis=-1, keepdims=True) + EPS)
            on = o * r
            dyv = dy_ref[:, sl]
            gn = gain_ref[:, sl]
            dgain_ref[:, sl] += jnp.sum(dyv * on * sg, axis=0, keepdims=True)
            dg_ref[:, sl] = (dyv * on * gn * dsg).astype(BF16)
            don = dyv * gn * sg
            do_ref[:, sl] = r * (don - on * jnp.mean(don * on, axis=-1, keepdims=True))

    return pl.pallas_call(
        body, grid=(lp // tr,),
        in_specs=[_rspec(tr, w), _rspec(tr, w), _rspec(tr, w, col_g // w), _fspec((1, w)),
                  _rspec(tr, w)],
        out_specs=[_rspec(tr, w), _rspec(tr, w), _fspec((1, w))],
        out_shape=[jax.ShapeDtypeStruct((lp, w), F32), jax.ShapeDtypeStruct((lp, w), BF16),
                   jax.ShapeDtypeStruct((1, w), F32)],
        compiler_params=_cp("arbitrary"), name=name)(o_f, o_b, proj, gain, dy)


HG_GROUP = 8
HG_ROWS = HG_GROUP * HG_CHUNK


def _hg_gates(zq, z, lbv):
    qh = zq * _sigmoid(zq)
    s = _sigmoid(z)
    f = lbv + (1.0 - lbv) * s
    kk = (1.0 - lbv) * _sigmoid(-z)
    return qh, s, f, jnp.log(f), kk


def _tri_blocks(upper):
    r = lax.broadcasted_iota(jnp.int32, (HG_ROWS, HG_ROWS), 0)
    c = lax.broadcasted_iota(jnp.int32, (HG_ROWS, HG_ROWS), 1)
    same = (r // HG_CHUNK) == (c // HG_CHUNK)
    return jnp.where(same & ((c >= r) if upper else (c <= r)), 1.0, 0.0).astype(F32)


def _g3(a):
    return a.reshape(HG_GROUP, HG_CHUNK, a.shape[-1])


def _hg_scan_fwd(proj, lb, *, reverse, col_q, col_z, col_i, hh, name):
    lp = proj.shape[0]
    n_groups = lp // HG_ROWS
    n_chunks = lp // HG_CHUNK
    c16 = HG_CHUNK
    last = 0 if reverse else c16 - 1

    def body(q_ref, z_ref, i_ref, lb_ref, o_ref, st_ref):
        lbv = lb_ref[...]
        tri = _tri_blocks(reverse)
        row = lax.broadcasted_iota(jnp.int32, (1, c16, HG_DK), 1)

        def group(gi, st):
            gg = (n_groups - 1 - gi) if reverse else gi
            r0 = pl.multiple_of(gg * HG_ROWS, HG_ROWS)
            v = i_ref[pl.ds(r0, HG_ROWS), :]
            qh, _, _, g, kk = _hg_gates(q_ref[pl.ds(r0, HG_ROWS), :], z_ref[pl.ds(r0, HG_ROWS), :],
                                        lbv)
            b = _nn(tri, g, precision=HIGHEST)
            b3, qh3, kk3, v3 = _g3(b), _g3(qh), _g3(kk), _g3(v)
            bl3 = b3[:, last:last + 1, :]
            qe = (qh * jnp.exp(b)).astype(BF16)
            kd = (kk3 * jnp.exp(bl3 - b3)).reshape(HG_ROWS, HG_DK).astype(BF16)
            decay3 = jnp.exp(bl3)
            v16 = v.astype(BF16)
            o3 = jnp.zeros((HG_GROUP, c16, HG_DK), F32)
            for t in range(c16):
                mask = (row >= t) if reverse else (row <= t)
                e = jnp.where(mask, jnp.exp(jnp.minimum(b3[:, t:t + 1, :] - b3, 0.0)), 0.0)
                a_col = jnp.sum(qh3[:, t:t + 1, :] * kk3 * e, axis=2, keepdims=True)
                o_t = jnp.sum(a_col * v3, axis=1, keepdims=True)
                o3 = o3 + jnp.where(row == t, o_t, 0.0)
            inter = [None] * HG_GROUP
            for c in (reversed(range(HG_GROUP)) if reverse else range(HG_GROUP)):
                sl = slice(c * c16, (c + 1) * c16)
                st_ref[gg * HG_GROUP + c] = st
                inter[c] = _nt(qe[sl], st.astype(BF16))
                st = decay3[c] * st + _tn(v16[sl], kd[sl])
            o_ref[pl.ds(r0, HG_ROWS), :] = (o3.reshape(HG_ROWS, HG_DK)
                                            + jnp.concatenate(inter, axis=0))
            return st

        lax.fori_loop(0, n_groups, group, jnp.zeros((HG_DK, HG_DK), F32))

    cspec = lambda col: pl.BlockSpec((lp, HG_DK), lambda h: (0, col // HG_DK + h))
    return pl.pallas_call(
        body, grid=(hh,),
        in_specs=[cspec(col_q), cspec(col_z), cspec(col_i),
                  pl.BlockSpec((None, 1, HG_DK), lambda h: (h, 0, 0))],
        out_specs=[pl.BlockSpec((lp, HG_DK), lambda h: (0, h)),
                   pl.BlockSpec((None, n_chunks, HG_DK, HG_DK), lambda h: (h, 0, 0, 0))],
        out_shape=[jax.ShapeDtypeStruct((lp, hh * HG_DK), F32),
                   jax.ShapeDtypeStruct((hh, n_chunks, HG_DK, HG_DK), F32)],
        compiler_params=_cp("parallel"), name=name)(proj, proj, proj, lb)


def _hg_scan_bwd(proj, lb, states, do, *, reverse, col_q, col_z, col_i, hh, name):
    lp = proj.shape[0]
    n_groups = lp // HG_ROWS
    n_chunks = lp // HG_CHUNK
    c16 = HG_CHUNK
    last = 0 if reverse else c16 - 1

    def body(q_ref, z_ref, i_ref, lb_ref, st_ref, do_ref, dq_ref, dz_ref, dv_ref, dlb_ref):
        lbv = lb_ref[...]
        tri = _tri_blocks(reverse)
        tri_t = _tri_blocks(not reverse)
        row = lax.broadcasted_iota(jnp.int32, (1, c16, HG_DK), 1)

        def group(gi, carry):
            dst, dlb = carry
            gg = gi if reverse else (n_groups - 1 - gi)
            r0 = pl.multiple_of(gg * HG_ROWS, HG_ROWS)
            zq = q_ref[pl.ds(r0, HG_ROWS), :]
            v = i_ref[pl.ds(r0, HG_ROWS), :]
            dov = do_ref[pl.ds(r0, HG_ROWS), :]
            qh, s, f, g, kk = _hg_gates(zq, z_ref[pl.ds(r0, HG_ROWS), :], lbv)
            b = _nn(tri, g, precision=HIGHEST)
            b3, qh3, kk3, do3 = _g3(b), _g3(qh), _g3(kk), _g3(dov)
            bl3 = b3[:, last:last + 1, :]
            eb = jnp.exp(b)
            ebl3 = jnp.exp(bl3 - b3)
            decay3 = jnp.exp(bl3)
            qe16 = (qh * eb).astype(BF16)
            kd16 = (kk3 * ebl3).reshape(HG_ROWS, HG_DK).astype(BF16)
            v16, do16 = v.astype(BF16), dov.astype(BF16)
            do_s, da_t, v_ds, dv_st, dbl_st = ([None] * HG_GROUP for _ in range(5))
            for c in (range(HG_GROUP) if reverse else reversed(range(HG_GROUP))):
                sl = slice(c * c16, (c + 1) * c16)
                st = st_ref[gg * HG_GROUP + c]
                st16, dst16 = st.astype(BF16), dst.astype(BF16)
                do_s[c] = _nn(do16[sl], st16)
                da_t[c] = _nt(v16[sl], do16[sl])
                v_ds[c] = _nn(v16[sl], dst16)
                dv_st[c] = _nt(kd16[sl], dst16)
                dbl_st[c] = decay3[c] * jnp.sum(st * dst, axis=0, keepdims=True)
                dst = decay3[c] * dst + _tn(do16[sl], qe16[sl])
            cat3 = lambda parts: _g3(jnp.concatenate(parts, axis=0))
            dq3 = _g3(eb) * cat3(do_s)
            dk_state3 = ebl3 * cat3(v_ds)
            dk3 = dk_state3
            dv3 = cat3(dv_st)
            da3 = cat3(da_t)
            for t in range(c16):
                mask = (row >= t) if reverse else (row <= t)
                e = jnp.where(mask, jnp.exp(jnp.minimum(b3[:, t:t + 1, :] - b3, 0.0)), 0.0)
                qt = qh3[:, t:t + 1, :]
                a_col = jnp.sum(qt * kk3 * e, axis=2, keepdims=True)
                ce = da3[:, :, t:t + 1] * e
                dq_t = jnp.sum(ce * kk3, axis=1, keepdims=True)
                dq3 = dq3 + jnp.where(row == t, dq_t, 0.0)
                dk3 = dk3 + ce * qt
                dv3 = dv3 + a_col * do3[:, t:t + 1, :]
            dbl3 = (jnp.concatenate([d[None] for d in dbl_st], axis=0)
                    + jnp.sum(kk3 * dk_state3, axis=1, keepdims=True))
            db3 = qh3 * dq3 - kk3 * dk3 + jnp.where(row == last, dbl3, 0.0)
            dg = _nn(tri_t, db3.reshape(HG_ROWS, HG_DK), precision=HIGHEST)
            df = dg / f - dk3.reshape(HG_ROWS, HG_DK)
            sq = _sigmoid(zq)
            dq_ref[pl.ds(r0, HG_ROWS), :] = (dq3.reshape(HG_ROWS, HG_DK)
                                             * (sq + zq * sq * (1.0 - sq)))
            dz_ref[pl.ds(r0, HG_ROWS), :] = df * (1.0 - lbv) * s * (1.0 - s)
            dv_ref[pl.ds(r0, HG_ROWS), :] = dv3.reshape(HG_ROWS, HG_DK)
            return dst, dlb + jnp.sum(df * (1.0 - s), axis=0, keepdims=True)

        _, dlb = lax.fori_loop(0, n_groups, group,
                               (jnp.zeros((HG_DK, HG_DK), F32), jnp.zeros((1, HG_DK), F32)))
        dlb_ref[...] = dlb

    cspec = lambda col: pl.BlockSpec((lp, HG_DK), lambda h: (0, col // HG_DK + h))
    ospec = pl.BlockSpec((lp, HG_DK), lambda h: (0, h))
    sds = jax.ShapeDtypeStruct((lp, hh * HG_DK), F32)
    return pl.pallas_call(
        body, grid=(hh,),
        in_specs=[cspec(col_q), cspec(col_z), cspec(col_i),
                  pl.BlockSpec((None, 1, HG_DK), lambda h: (h, 0, 0)),
                  pl.BlockSpec((None, n_chunks, HG_DK, HG_DK), lambda h: (h, 0, 0, 0)),
                  ospec],
        out_specs=[ospec, ospec, ospec, pl.BlockSpec((None, 1, HG_DK), lambda h: (h, 0, 0))],
        out_shape=[sds, sds, sds, jax.ShapeDtypeStruct((hh, 1, HG_DK), F32)],
        compiler_params=_cp("parallel"), name=name)(proj, proj, proj, lb, states, do)


def _na_rows(r, rows):
    rs = jnp.clip(r - NA_WIN_H // 2, 0, rows - NA_WIN_H)
    i0 = rs - r + (NA_WIN_H - 1)
    q0 = pl.multiple_of(N_META + GRID_W * r, 16)
    k0 = pl.multiple_of(N_META + GRID_W * rs, 16)
    return i0, q0, k0


def _na_scores(q16, k16, km16, tb_ref, i0, scale):
    s = _nt(q16, k16) * scale
    bias = jnp.concatenate([tb_ref[i0 + j] for j in range(NA_WIN_H)], axis=1)
    return s + bias, _nt(q16, km16) * scale


def _na_fwd(qkv, tb, *, n_tok, name):
    _, nh, lp, dh = qkv.shape
    rows = n_tok // GRID_W
    scale = dh ** -0.5
    kw = NA_WIN_H * GRID_W

    def body(q_ref, k_ref, v_ref, tb_ref, o_ref, lse_ref):
        o_ref[...] = jnp.zeros_like(o_ref)
        lse_ref[...] = jnp.zeros_like(lse_ref)
        km = k_ref[0:N_META, :].astype(BF16)
        vm = v_ref[0:N_META, :].astype(BF16)
        s = _nt(q_ref[0:N_META, :].astype(BF16), km) * scale
        m = jnp.max(s, axis=1, keepdims=True)
        p = jnp.exp(s - m)
        l = jnp.sum(p, axis=1, keepdims=True)
        o_ref[0:N_META, :] = _nn(p.astype(BF16), vm) / l
        lse_ref[0:N_META, :] = m + jnp.log(l)

        def step(r, carry):
            i0, q0, k0 = _na_rows(r, rows)
            q16 = q_ref[pl.ds(q0, GRID_W), :].astype(BF16)
            k16 = k_ref[pl.ds(k0, kw), :].astype(BF16)
            v16 = v_ref[pl.ds(k0, kw), :].astype(BF16)
            s, sm = _na_scores(q16, k16, km, tb_ref, i0, scale)
            m = jnp.maximum(jnp.max(s, axis=1, keepdims=True), jnp.max(sm, axis=1, keepdims=True))
            p = jnp.exp(s - m)
            pm = jnp.exp(sm - m)
            l = jnp.sum(p, axis=1, keepdims=True) + jnp.sum(pm, axis=1, keepdims=True)
            o = _nn(p.astype(BF16), v16) + _nn(pm.astype(BF16), vm)
            o_ref[pl.ds(q0, GRID_W), :] = o / l
            lse_ref[pl.ds(q0, GRID_W), :] = m + jnp.log(l)
            return carry

        lax.fori_loop(0, rows, step, 0)

    hspec = lambda which: pl.BlockSpec((None, None, lp, dh), lambda h: (which, h, 0, 0))
    return pl.pallas_call(
        body, grid=(nh,),
        in_specs=[hspec(0), hspec(1), hspec(2),
                  pl.BlockSpec((None, 2 * NA_WIN_H - 1, GRID_W, GRID_W), lambda h: (h, 0, 0, 0))],
        out_specs=[pl.BlockSpec((None, lp, dh), lambda h: (h, 0, 0)),
                   pl.BlockSpec((None, lp, 1), lambda h: (h, 0, 0))],
        out_shape=[jax.ShapeDtypeStruct((nh, lp, dh), F32), jax.ShapeDtypeStruct((nh, lp, 1), F32)],
        compiler_params=_cp("parallel"), name=name)(qkv, qkv, qkv, tb)


def _na_bwd(qkv, tb, o, lse, do, *, n_tok, name):
    _, nh, lp, dh = qkv.shape
    rows = n_tok // GRID_W
    scale = dh ** -0.5
    kw = NA_WIN_H * GRID_W

    def body(q_ref, k_ref, v_ref, tb_ref, o_ref, lse_ref, do_ref, dq_ref, dk_ref, dv_ref, dtb_ref):
        dq_ref[...] = jnp.zeros_like(dq_ref)
        dk_ref[...] = jnp.zeros_like(dk_ref)
        dv_ref[...] = jnp.zeros_like(dv_ref)
        dtb_ref[...] = jnp.zeros_like(dtb_ref)
        km = k_ref[0:N_META, :].astype(BF16)
        vm = v_ref[0:N_META, :].astype(BF16)
        qm = q_ref[0:N_META, :].astype(BF16)
        dom = do_ref[0:N_META, :]
        p = jnp.exp(_nt(qm, km) * scale - lse_ref[0:N_META, :])
        dp = _nt(dom.astype(BF16), vm)
        delta = jnp.sum(dom * o_ref[0:N_META, :], axis=1, keepdims=True)
        ds = (p * (dp - delta)).astype(BF16)
        dq_ref[0:N_META, :] = _nn(ds, km) * scale
        dkm0 = _tn(ds, qm) * scale
        dvm0 = _tn(p.astype(BF16), dom.astype(BF16))

        def step(r, carry):
            dkm, dvm = carry
            i0, q0, k0 = _na_rows(r, rows)
            q16 = q_ref[pl.ds(q0, GRID_W), :].astype(BF16)
            k16 = k_ref[pl.ds(k0, kw), :].astype(BF16)
            v16 = v_ref[pl.ds(k0, kw), :].astype(BF16)
            s, sm = _na_scores(q16, k16, km, tb_ref, i0, scale)
            lse = lse_ref[pl.ds(q0, GRID_W), :]
            p = jnp.exp(s - lse)
            pm = jnp.exp(sm - lse)
            dov = do_ref[pl.ds(q0, GRID_W), :]
            do16 = dov.astype(BF16)
            delta = jnp.sum(dov * o_ref[pl.ds(q0, GRID_W), :], axis=1, keepdims=True)
            ds = p * (_nt(do16, v16) - delta)
            dsm = (pm * (_nt(do16, vm) - delta)).astype(BF16)
            ds16 = ds.astype(BF16)
            dq_ref[pl.ds(q0, GRID_W), :] = (_nn(ds16, k16) + _nn(dsm, km)) * scale
            dk_ref[pl.ds(k0, kw), :] += _tn(ds16, q16) * scale
            dv_ref[pl.ds(k0, kw), :] += _tn(p.astype(BF16), do16)
            for j in range(NA_WIN_H):
                dtb_ref[i0 + j] += ds[:, j * GRID_W:(j + 1) * GRID_W]
            return (dkm + _tn(dsm, q16) * scale, dvm + _tn(pm.astype(BF16), do16))

        dkm, dvm = lax.fori_loop(0, rows, step, (dkm0, dvm0))
        dk_ref[0:N_META, :] += dkm
        dv_ref[0:N_META, :] += dvm

    hspec = lambda which: pl.BlockSpec((None, None, lp, dh), lambda h: (which, h, 0, 0))
    h3 = pl.BlockSpec((None, lp, dh), lambda h: (h, 0, 0))
    tbs = pl.BlockSpec((None, 2 * NA_WIN_H - 1, GRID_W, GRID_W), lambda h: (h, 0, 0, 0))
    sds = jax.ShapeDtypeStruct((nh, lp, dh), F32)
    return pl.pallas_call(
        body, grid=(nh,),
        in_specs=[hspec(0), hspec(1), hspec(2), tbs, h3,
                  pl.BlockSpec((None, lp, 1), lambda h: (h, 0, 0)), h3],
        out_specs=[h3, h3, h3, tbs],
        out_shape=[sds, sds, sds, jax.ShapeDtypeStruct(tb.shape, F32)],
        compiler_params=_cp("parallel"), name=name)(qkv, qkv, qkv, tb, o, lse, do)


def _rpb_onehot():
    c = np.arange(GRID_W)[:, None]
    w = np.arange(GRID_W)[None, :]
    cs = np.clip(c - NA_WIN_W // 2, 0, GRID_W - NA_WIN_W)
    in_win = (w >= cs) & (w < cs + NA_WIN_W)
    dc = np.clip(w - c, -(NA_WIN_W - 1), NA_WIN_W - 1) + NA_WIN_W - 1
    oh = np.zeros((LANES, GRID_W * GRID_W), np.float32)
    flat = np.arange(GRID_W * GRID_W).reshape(GRID_W, GRID_W)
    oh[dc[in_win], flat[in_win]] = 1.0
    neg = np.where(in_win, 0.0, -1e30).astype(np.float32).reshape(1, -1)
    return oh, neg


def _assemble_dproj(dqkv, dq_f, dq_b, dz_f, dz_b, dv_f, dv_b, dg, dgn, dgh, *, name):
    _, nh, lp, dh = dqkv.shape
    naw = nh * dh
    hgw = dq_f.shape[1]
    d = dgn.shape[1]
    cols = 3 * naw + 5 * hgw + 2 * d
    tr = _row_tile(lp)

    def body(na_ref, qf_ref, qb_ref, zf_ref, zb_ref, vf_ref, vb_ref, g_ref, gn_ref, gh_ref, o_ref):
        for which in range(3):
            for h in range(nh):
                c0 = which * naw + h * dh
                o_ref[:, c0:c0 + dh] = na_ref[which, h].astype(BF16)
        c0 = 3 * naw
        o_ref[:, c0:c0 + hgw] = (qf_ref[...] + qb_ref[...]).astype(BF16)
        o_ref[:, c0 + hgw:c0 + 2 * hgw] = zf_ref[...].astype(BF16)
        o_ref[:, c0 + 2 * hgw:c0 + 3 * hgw] = zb_ref[...].astype(BF16)
        o_ref[:, c0 + 3 * hgw:c0 + 4 * hgw] = (vf_ref[...] + vb_ref[...]).astype(BF16)
        o_ref[:, c0 + 4 * hgw:c0 + 5 * hgw] = g_ref[...]
        o_ref[:, c0 + 5 * hgw:c0 + 5 * hgw + d] = gn_ref[...]
        o_ref[:, c0 + 5 * hgw + d:] = gh_ref[...]

    hg = _rspec(tr, hgw)
    return pl.pallas_call(
        body, grid=(lp // tr,),
        in_specs=[pl.BlockSpec((3, nh, tr, dh), lambda i: (0, 0, i, 0)),
                  hg, hg, hg, hg, hg, hg, hg, _rspec(tr, d), _rspec(tr, d)],
        out_specs=_rspec(tr, cols),
        out_shape=jax.ShapeDtypeStruct((lp, cols), BF16),
        compiler_params=_cp("parallel"), name=name)(dqkv, dq_f, dq_b, dz_f, dz_b, dv_f, dv_b, dg,
                                                    dgn, dgh)


def _adamw(w, g, m, v, *, name):
    rows, cols = w.shape
    tr = 256 if rows % 256 == 0 else rows

    def body(w_ref, g_ref, m_ref, v_ref, d_ref, mo_ref, vo_ref):
        gv = g_ref[...]
        mn = ADAM_B1 * m_ref[...] + (1.0 - ADAM_B1) * gv
        vn = ADAM_B2 * v_ref[...] + (1.0 - ADAM_B2) * (gv * gv)
        m_hat = mn / (1.0 - ADAM_B1 ** ADAM_STEP)
        v_hat = vn / (1.0 - ADAM_B2 ** ADAM_STEP)
        d_ref[...] = -ADAM_LR * (m_hat / (jnp.sqrt(v_hat) + ADAM_EPS) + ADAM_WD * w_ref[...])
        mo_ref[...] = mn
        vo_ref[...] = vn

    spec = _rspec(tr, cols)
    sds = jax.ShapeDtypeStruct((rows, cols), F32)
    return pl.pallas_call(
        body, grid=(rows // tr,), in_specs=[spec] * 4, out_specs=[spec] * 3, out_shape=[sds] * 3,
        compiler_params=_cp("parallel"), name=name)(w, g, m, v)


def _local_step(x, tgt, meta, w_in, rest_weights, g_mix, g_mlp, g_fin, hg_gain, rpb, lb,
                early_grads=None):
    n_tok, d = x.shape
    hgw = hg_gain.shape[1]
    naw = (w_in.shape[1] - 5 * hgw - 2 * d) // 3
    nh, hh = naw // NA_HEAD_DIM, hgw // HG_DK
    l_real = N_META + n_tok
    lp = -(-l_real // ROW_ALIGN) * ROW_ALIGN
    n_chunks = l_real // HG_CHUNK
    pad = lp - l_real
    col_qhg = 3 * naw
    col_zf, col_zb, col_i, col_g = (col_qhg + hgw, col_qhg + 2 * hgw, col_qhg + 3 * hgw,
                                    col_qhg + 4 * hgw)
    col_gate = col_qhg + 5 * hgw

    zpad = jnp.zeros((pad, d), F32)
    h0 = jnp.concatenate([meta, x, zpad], axis=0)
    tgt_p = jnp.concatenate([jnp.zeros((N_META, d), F32), tgt, zpad], axis=0)

    oh_np, neg_np = _rpb_onehot()
    oh = jnp.asarray(oh_np)
    rpb_p = jnp.pad(rpb.reshape(nh * (2 * NA_WIN_H - 1), 2 * NA_WIN_W - 1),
                    ((0, 0), (0, LANES - (2 * NA_WIN_W - 1))))
    tb = _matmul(rpb_p, oh, tm=rpb_p.shape[0], tn=512, tk=LANES, precision=HIGHEST,
                 name="rpb_expand")
    tb = (tb + jnp.asarray(neg_np)).reshape(nh, 2 * NA_WIN_H - 1, GRID_W, GRID_W)

    a = _rmsnorm_fwd(h0, g_mix, name="norm_mix")
    proj = _matmul(a, w_in, name="mm_in")
    qkv = proj[:, :3 * naw].reshape(lp, 3, nh, NA_HEAD_DIM).transpose(1, 2, 0, 3)
    o_na_hm, lse = _na_fwd(qkv, tb, n_tok=n_tok, name="na_fwd")
    o_na = o_na_hm.transpose(1, 0, 2).reshape(lp, naw)
    lb_f = lb[0].reshape(hh, 1, HG_DK)
    lb_b = lb[1].reshape(hh, 1, HG_DK)
    scan_kw = dict(col_q=col_qhg, col_i=col_i, hh=hh)
    o_f, st_f = _hg_scan_fwd(proj, lb_f, reverse=False, col_z=col_zf, name="hg_scan_f", **scan_kw)
    o_b, st_b = _hg_scan_fwd(proj, lb_b, reverse=True, col_z=col_zb, name="hg_scan_b", **scan_kw)
    o_hg = _hg_out(o_f, o_b, proj, hg_gain, col_g=col_g, name="hg_out")
    w_na, w_hg, w_o, w_up, w_down = rest_weights(o_hg)
    y_na = _matmul(o_na, w_na, name="mm_na_out")
    y_hg = _matmul(o_hg, w_hg, name="mm_hg_out")
    mix = _gate_mix(proj, y_na, y_hg, col_gate=col_gate, name="gate_mix")
    t1 = _matmul(mix, w_o, name="mm_o")
    h1, mlp_in = _residual_norm(h0, t1, g_mlp, name="resid_norm_mlp")
    u = _matmul(mlp_in, w_up, name="mm_up")
    act = _relu2(u, name="relu2")
    t2 = _matmul(act, w_down, name="mm_down")
    dh2, loss, dg_fin = _final_loss(h1, t2, g_fin, tgt_p, n_tok=n_tok, name="final_loss")

    dact = _matmul(dh2, w_down, tb=True, name="mm_down_dx")
    dw_down = _matmul(act, dh2, ta=True, name="mm_down_dw")
    du = _relu2_bwd(dact, u, name="relu2_bwd")
    dm = _matmul(du, w_up, tb=True, name="mm_up_dx")
    dw_up = _matmul(mlp_in, du, ta=True, name="mm_up_dw")
    dh1, dg_mlp = _rmsnorm_bwd_add(h1, g_mlp, dm, dh2, name="norm_mlp_bwd")
    dmix = _matmul(dh1, w_o, tb=True, name="mm_o_dx")
    dw_o = _matmul(mix, dh1, ta=True, name="mm_o_dw")
    dy_na, dy_hg, dgn, dgh = _gate_mix_bwd(proj, y_na, y_hg, dmix, col_gate=col_gate,
                                           name="gate_mix_bwd")
    do_na = _matmul(dy_na, w_na, tb=True, name="mm_na_out_dx")
    dw_na = _matmul(o_na, dy_na, ta=True, name="mm_na_out_dw")
    do_hg = _matmul(dy_hg, w_hg, tb=True, name="mm_hg_out_dx")
    dw_hg = _matmul(o_hg, dy_hg, ta=True, name="mm_hg_out_dw")
    token = early_grads([dw_na, dw_hg, dw_o, dw_up, dw_down]) if early_grads else None
    if token is not None:
        hg_gain = hg_gain + token[0:1, 0:1]
    d_o, dg_hg, d_gain = _hg_out_bwd(o_f, o_b, proj, hg_gain, do_hg, col_g=col_g, name="hg_out_bwd")
    dq_f, dz_f, dv_f, dlb_f = _hg_scan_bwd(proj, lb_f, st_f, d_o, reverse=False, col_z=col_zf,
                                           name="hg_scan_f_bwd", **scan_kw)
    dq_b, dz_b, dv_b, dlb_b = _hg_scan_bwd(proj, lb_b, st_b, d_o, reverse=True, col_z=col_zb,
                                           name="hg_scan_b_bwd", **scan_kw)
    do_na_hm = do_na.reshape(lp, nh, NA_HEAD_DIM).transpose(1, 0, 2)
    dq_na, dk_na, dv_na, dtb = _na_bwd(qkv, tb, o_na_hm, lse, do_na_hm, n_tok=n_tok, name="na_bwd")
    dqkv = jnp.stack([dq_na, dk_na, dv_na], axis=0)
    dproj = _assemble_dproj(dqkv, dq_f, dq_b, dz_f, dz_b, dv_f, dv_b, dg_hg, dgn, dgh,
                            name="assemble_dproj")
    da = _matmul(dproj, w_in, tb=True, name="mm_in_dx")
    dw_in = _matmul(a, dproj, ta=True, name="mm_in_dw")
    dh0, dg_mix = _rmsnorm_bwd_add(h0, g_mix, da, dh1, name="norm_mix_bwd")
    d_rpb = _matmul(dtb.reshape(nh * (2 * NA_WIN_H - 1), GRID_W * GRID_W), oh, tb=True,
                    tm=nh * (2 * NA_WIN_H - 1), tn=LANES, tk=1024, precision=HIGHEST,
                    name="rpb_reduce")
    d_lb = jnp.concatenate([dlb_f.reshape(1, hgw), dlb_b.reshape(1, hgw)], axis=0)
    return (loss, dh0[N_META:l_real], dh0[:N_META], dw_in, dw_na, dw_hg, dw_o, dw_up, dw_down,
            dg_mix, dg_mlp, dg_fin, d_gain, d_rpb, d_lb)


N_CHIPS = 4
N_DEV = 8
ANY = pl.BlockSpec(memory_space=pl.ANY)


def _place():
    x, y, c = lax.axis_index("x"), lax.axis_index("y"), lax.axis_index("c")
    others = []
    for j in (1, 2, 3):
        tx = (1 - x) if (j >> 1) else x
        ty = (1 - y) if (j & 1) else y
        others.append((tx, ty))
    return x, y, c, others


def _piece(ref, axis, k, half, rh, cs):
    if axis == 1:
        return ref.at[pl.ds(pl.multiple_of(half * rh, 16), rh), pl.ds(pl.multiple_of(k * cs, LANES), cs)]
    return ref.at[pl.ds(pl.multiple_of(k * 2 * rh + half * rh, 16), rh), :]


def _cast_into_full(shard, axis, place, *, name):
    r, cs = shard.shape
    full = (r, cs * N_CHIPS) if axis == 1 else (r * N_CHIPS, cs)
    tr = next(t for t in (256, 128, 64, 32, 16) if r % t == 0)
    nt = r // tr

    def body(p_ref, s_ref, o_ref):
        o_ref[...] = s_ref[...].astype(BF16)

    if axis == 1:
        omap = lambda i, p_ref: (i, p_ref[0])
    else:
        omap = lambda i, p_ref: (p_ref[0] * nt + i, 0)
    return pl.pallas_call(
        body,
        grid_spec=pltpu.PrefetchScalarGridSpec(
            num_scalar_prefetch=1, grid=(nt,),
            in_specs=[pl.BlockSpec((tr, cs), lambda i, p_ref: (i, 0))],
            out_specs=pl.BlockSpec((tr, cs), omap)),
        out_shape=jax.ShapeDtypeStruct(full, BF16),
        compiler_params=_cp("parallel"), name=name)(place, shard)


def _weight_allgather(fulls, axes, *, name):
    n = len(fulls)
    geo = []
    for f, ax in zip(fulls, axes):
        r, cs = (f.shape[0], f.shape[1] // N_CHIPS) if ax == 1 else (f.shape[0] // N_CHIPS, f.shape[1])
        geo.append((ax, r // 2, cs))

    def body(*refs):
        o_refs = refs[n:2 * n]
        send_sems, recv_sems = refs[2 * n:]
        x, y, c, others = _place()
        chip = 2 * x + y
        sib = (x, y, 1 - c)

        def rcopy(i, slot, ref, to):
            return pltpu.make_async_remote_copy(
                src_ref=ref, dst_ref=ref, send_sem=send_sems.at[6 * i + slot],
                recv_sem=recv_sems.at[6 * i + slot], device_id=to, device_id_type=MESH)

        started = []
        for i in range(n):
            ax, rh, cs = geo[i]
            mine = _piece(o_refs[i], ax, chip, c, rh, cs)
            for j, (tx, ty) in enumerate(others):
                cp = rcopy(i, j, mine, (tx, ty, c))
                cp.start()
                started.append(cp)
        for i in range(n):
            ax, rh, cs = geo[i]
            for j, (tx, ty) in enumerate(others):
                got = _piece(o_refs[i], ax, 2 * tx + ty, c, rh, cs)
                rcopy(i, j, got, (x, y, c)).wait_recv()
                cp = rcopy(i, 3 + j, got, sib)
                cp.start()
                started.append(cp)
        for i in range(n):
            ax, rh, cs = geo[i]
            for j, (tx, ty) in enumerate(others):
                got = _piece(o_refs[i], ax, 2 * tx + ty, 1 - c, rh, cs)
                rcopy(i, 3 + j, got, (x, y, c)).wait_recv()
        for cp in started:
            cp.wait_send()

    return pl.pallas_call(
        body, in_specs=[ANY] * n, out_specs=[ANY] * n,
        out_shape=[jax.ShapeDtypeStruct(f.shape, f.dtype) for f in fulls],
        input_output_aliases={i: i for i in range(n)},
        scratch_shapes=[pltpu.SemaphoreType.DMA((6 * n,)), pltpu.SemaphoreType.DMA((6 * n,))],
        name=name)(*fulls)


HBM_SPEC = pl.BlockSpec(memory_space=pltpu.HBM)
SEM_SPEC = pl.BlockSpec(memory_space=pltpu.SEMAPHORE)
SPLIT_COPY = pltpu.CompilerParams(has_side_effects=pltpu.SideEffectType.DATAFLOW_SIDE_EFFECTING)
TOKEN = jax.ShapeDtypeStruct((8, LANES), F32)


def _geo(fulls, axes):
    out = []
    for f, ax in zip(fulls, axes):
        r, cs = (f.shape[0], f.shape[1] // N_CHIPS) if ax == 1 else (f.shape[0] // N_CHIPS, f.shape[1])
        out.append((ax, r // 2, cs))
    return out


def _gather_copies(refs, geo, send_sems, recv_sems):
    x, y, c, others = _place()
    chip = 2 * x + y
    cps = []
    for i, (ax, rh, cs) in enumerate(geo):
        mine = _piece(refs[i], ax, chip, c, rh, cs)
        for j, (tx, ty) in enumerate(others):
            cps.append(pltpu.make_async_remote_copy(
                src_ref=mine, dst_ref=mine, send_sem=send_sems.at[3 * i + j],
                recv_sem=recv_sems.at[3 * i + j], device_id=(tx, ty, c), device_id_type=MESH))
    return cps


def _allgather_start(fulls, axes, after, *, name):
    n = len(fulls)
    geo = _geo(fulls, axes)

    def body(*refs):
        w_refs = refs[:n]
        send_sems, recv_sems = refs[n + 1], refs[n + 2]
        token = refs[2 * n + 3]
        for cp in _gather_copies(w_refs, geo, send_sems, recv_sems):
            cp.start()
        token[...] = jnp.zeros_like(token)

    out = pl.pallas_call(
        body, name=name,
        out_shape=(pltpu.SemaphoreType.DMA((3 * n,)), pltpu.SemaphoreType.DMA((3 * n,)),
                   *[pltpu.HBM(f.shape, f.dtype) for f in fulls], TOKEN),
        in_specs=[HBM_SPEC] * n + [ANY],
        out_specs=(SEM_SPEC, SEM_SPEC, *[HBM_SPEC] * n, pl.BlockSpec(memory_space=pltpu.VMEM)),
        input_output_aliases={i: 2 + i for i in range(n)},
        compiler_params=SPLIT_COPY,
    )(*[pltpu.with_memory_space_constraint(f, pltpu.HBM) for f in fulls], after)
    return out[0], out[1], list(out[2:2 + n]), out[2 + n]


def _allgather_wait(send_sems, recv_sems, fulls, axes, after, *, name):
    n = len(fulls)
    geo = _geo(fulls, axes)

    def body(*refs):
        w_refs = refs[:n]
        for cp in _gather_copies(w_refs, geo, refs[n], refs[n + 1]):
            cp.wait_send()
            cp.wait_recv()

    return list(pl.pallas_call(
        body, name=name,
        out_shape=[pltpu.HBM(f.shape, f.dtype) for f in fulls],
        in_specs=[HBM_SPEC] * n + [SEM_SPEC, SEM_SPEC, ANY],
        out_specs=[HBM_SPEC] * n,
        input_output_aliases={i: i for i in range(n)},
        compiler_params=SPLIT_COPY,
    )(*fulls, send_sems, recv_sems, after))


def _allgather_forward(fulls, axes, *, name):
    n = len(fulls)
    geo = _geo(fulls, axes)

    def body(*refs):
        o_refs = refs[n:2 * n]
        send_sems, recv_sems = refs[2 * n:]
        x, y, c, others = _place()

        def rcopy(i, j, half, to):
            ax, rh, cs = geo[i]
            ref = _piece(o_refs[i], ax, 2 * others[j][0] + others[j][1], half, rh, cs)
            return pltpu.make_async_remote_copy(
                src_ref=ref, dst_ref=ref, send_sem=send_sems.at[3 * i + j],
                recv_sem=recv_sems.at[3 * i + j], device_id=to, device_id_type=MESH)

        cps = [rcopy(i, j, c, (x, y, 1 - c)) for i in range(n) for j in range(3)]
        for cp in cps:
            cp.start()
        for i in range(n):
            for j in range(3):
                rcopy(i, j, 1 - c, (x, y, c)).wait_recv()
        for cp in cps:
            cp.wait_send()

    return list(pl.pallas_call(
        body, in_specs=[ANY] * n, out_specs=[ANY] * n,
        out_shape=[jax.ShapeDtypeStruct(f.shape, f.dtype) for f in fulls],
        input_output_aliases={i: i for i in range(n)},
        scratch_shapes=[pltpu.SemaphoreType.DMA((3 * n,)), pltpu.SemaphoreType.DMA((3 * n,))],
        name=name)(*fulls))


def _scatter_geo(parts, axes):
    out = []
    for p, ax in zip(parts, axes):
        _, rh, cols = p.shape
        out.append((ax, rh, cols // N_CHIPS if ax == 1 else cols))
    return out


def _scatter_copies(p_refs, q_refs, geo, send_sems, recv_sems):
    x, y, c, others = _place()
    chip = 2 * x + y
    cps = []
    for i, (ax, rh, cw) in enumerate(geo):
        for j, (tx, ty) in enumerate(others):
            k = 2 * tx + ty
            src = (p_refs[i].at[0, :, pl.ds(pl.multiple_of(k * cw, LANES), cw)] if ax == 1
                   else p_refs[i].at[k])
            cps.append(pltpu.make_async_remote_copy(
                src_ref=src, dst_ref=q_refs[i].at[chip], send_sem=send_sems.at[3 * i + j],
                recv_sem=recv_sems.at[3 * i + j], device_id=(tx, ty, c), device_id_type=MESH))
    return cps


def _scatter_start(parts, axes, *, name):
    n = len(parts)
    geo = _scatter_geo(parts, axes)
    slots = [pltpu.HBM((N_CHIPS, rh, cw), p.dtype) for p, (_, rh, cw) in zip(parts, geo)]

    def body(*refs):
        p_refs, q_refs = refs[:n], refs[n:2 * n]
        send_sems, recv_sems = refs[2 * n], refs[2 * n + 1]
        token = refs[4 * n + 2]
        for cp in _scatter_copies(p_refs, q_refs, geo, send_sems, recv_sems):
            cp.start()
        token[...] = jnp.zeros_like(token)

    land = [pltpu.with_memory_space_constraint(lax.empty(s.inner_aval.shape, s.inner_aval.dtype), pltpu.HBM)
            for s in slots]
    out = pl.pallas_call(
        body, name=name,
        out_shape=(pltpu.SemaphoreType.DMA((3 * n,)), pltpu.SemaphoreType.DMA((3 * n,)),
                   *[pltpu.HBM(p.shape, p.dtype) for p in parts], *slots, TOKEN),
        in_specs=[HBM_SPEC] * (2 * n),
        out_specs=(SEM_SPEC, SEM_SPEC, *[HBM_SPEC] * (2 * n), pl.BlockSpec(memory_space=pltpu.VMEM)),
        input_output_aliases={i: 2 + i for i in range(2 * n)},
        compiler_params=SPLIT_COPY,
    )(*[pltpu.with_memory_space_constraint(p, pltpu.HBM) for p in parts], *land)
    return out[0], out[1], list(out[2:2 + n]), list(out[2 + n:2 + 2 * n]), out[2 + 2 * n]


def _scatter_wait(send_sems, recv_sems, parts, slots, axes, after, *, name):
    n = len(parts)
    geo = _scatter_geo(parts, axes)

    def body(*refs):
        p_refs, q_refs = refs[:n], refs[n:2 * n]
        for cp in _scatter_copies(p_refs, q_refs, geo, refs[2 * n], refs[2 * n + 1]):
            cp.wait_send()
            cp.wait_recv()

    out = pl.pallas_call(
        body, name=name,
        out_shape=[pltpu.HBM(a.shape, a.dtype) for a in (*parts, *slots)],
        in_specs=[HBM_SPEC] * (2 * n) + [SEM_SPEC, SEM_SPEC, ANY],
        out_specs=[HBM_SPEC] * (2 * n),
        input_output_aliases={i: i for i in range(2 * n)},
        compiler_params=SPLIT_COPY,
    )(*parts, *slots, send_sems, recv_sems, after)
    return list(out[:n]), list(out[n:])


def _sibling_swap(grads, *, name):
    n = len(grads)
    out_shape = [jax.ShapeDtypeStruct((g.shape[0], g.shape[1] // 2, g.shape[2]), g.dtype)
                 for g in grads]

    def body(*refs):
        g_refs, o_refs = refs[:n], refs[n:2 * n]
        send_sems, recv_sems = refs[2 * n:]
        x, y, c, _ = _place()
        cps = []
        for i in range(n):
            rh = grads[i].shape[1] // 2
            src = g_refs[i].at[:, pl.ds(pl.multiple_of((1 - c) * rh, 16), rh), :]
            cp = pltpu.make_async_remote_copy(
                src_ref=src, dst_ref=o_refs[i], send_sem=send_sems.at[i], recv_sem=recv_sems.at[i],
                device_id=(x, y, 1 - c), device_id_type=MESH)
            cp.start()
            cps.append(cp)
        for cp in cps:
            cp.wait()

    return pl.pallas_call(
        body, in_specs=[ANY] * n, out_specs=[ANY] * n, out_shape=out_shape,
        scratch_shapes=[pltpu.SemaphoreType.DMA((n,)), pltpu.SemaphoreType.DMA((n,))],
        name=name)(*grads)


def _pair_add(g3, rx, c_arr, *, out_dtype, name):
    nb, rows, cols = g3.shape
    rh = rows // 2
    tr = next(t for t in (128, 64, 32, 16) if rh % t == 0)
    nt = rh // tr

    def body(c_ref, g_ref, r_ref, o_ref):
        o_ref[...] = (g_ref[...] + r_ref[...]).astype(out_dtype)

    return pl.pallas_call(
        body,
        grid_spec=pltpu.PrefetchScalarGridSpec(
            num_scalar_prefetch=1, grid=(nb, nt),
            in_specs=[pl.BlockSpec((None, tr, cols), lambda b, i, c_ref: (b, c_ref[0] * nt + i, 0)),
                      pl.BlockSpec((None, tr, cols), lambda b, i, c_ref: (b, i, 0))],
            out_specs=pl.BlockSpec((None, tr, cols), lambda b, i, c_ref: (b, i, 0))),
        out_shape=jax.ShapeDtypeStruct((nb, rh, cols), out_dtype),
        compiler_params=_cp("parallel", "parallel"), name=name)(c_arr, g3, rx)


def _chip_scatter(parts, axes, *, name):
    n = len(parts)
    geo, out_shape = [], []
    for p, ax in zip(parts, axes):
        nb, rh, cols = p.shape
        cw = cols // N_CHIPS if ax == 1 else cols
        geo.append((ax, rh, cw))
        out_shape.append(jax.ShapeDtypeStruct((N_CHIPS, rh, cw), p.dtype))

    def body(*refs):
        p_refs, o_refs = refs[:n], refs[n:2 * n]
        send_sems, recv_sems = refs[2 * n:]
        x, y, c, others = _place()
        chip = 2 * x + y

        def block(i, k):
            ax, rh, cw = geo[i]
            if ax == 1:
                return p_refs[i].at[0, :, pl.ds(pl.multiple_of(k * cw, LANES), cw)]
            return p_refs[i].at[k]

        cps = []
        for i in range(n):
            for j, (tx, ty) in enumerate(others):
                cp = pltpu.make_async_remote_copy(
                    src_ref=block(i, 2 * tx + ty), dst_ref=o_refs[i].at[chip],
                    send_sem=send_sems.at[3 * i + j], recv_sem=recv_sems.at[3 * i + j],
                    device_id=(tx, ty, c), device_id_type=MESH)
                cp.start()
                cps.append(cp)
        for i in range(n):
            for j, (tx, ty) in enumerate(others):
                got = o_refs[i].at[2 * tx + ty]
                pltpu.make_async_remote_copy(
                    src_ref=got, dst_ref=got, send_sem=send_sems.at[3 * i + j],
                    recv_sem=recv_sems.at[3 * i + j], device_id=(x, y, c),
                    device_id_type=MESH).wait_recv()
        for cp in cps:
            cp.wait_send()

    return pl.pallas_call(
        body, in_specs=[ANY] * n, out_specs=[ANY] * n, out_shape=out_shape,
        scratch_shapes=[pltpu.SemaphoreType.DMA((3 * n,)), pltpu.SemaphoreType.DMA((3 * n,))],
        name=name)(*parts)


def _sum_slots(q, *, name):
    ns, rows, cols = q.shape
    tr = next(t for t in (128, 64, 32, 16, 8) if rows % t == 0)

    def body(q_ref, o_ref):
        acc = q_ref[0].astype(F32)
        for k in range(1, ns):
            acc = acc + q_ref[k].astype(F32)
        o_ref[...] = acc

    return pl.pallas_call(
        body, grid=(rows // tr,),
        in_specs=[pl.BlockSpec((ns, tr, cols), lambda i: (0, i, 0))],
        out_specs=_rspec(tr, cols),
        out_shape=jax.ShapeDtypeStruct((rows, cols), F32),
        compiler_params=_cp("parallel"), name=name)(q)


def _sum_chips(q, p, place, axis, *, name):
    _, rh, cw = q.shape
    tr = next(t for t in (128, 64, 32, 16) if rh % t == 0)
    nt = rh // tr

    def body(p_ref, *refs):
        q_refs, own_ref, o_ref = refs[:N_CHIPS], refs[N_CHIPS], refs[N_CHIPS + 1]
        chip = p_ref[0]
        acc = jnp.where(chip == 0, own_ref[...], q_refs[0][...]).astype(F32)
        for k in range(1, N_CHIPS):
            acc = acc + jnp.where(chip == k, own_ref[...], q_refs[k][...]).astype(F32)
        o_ref[...] = acc

    def slot_spec(k):
        return pl.BlockSpec((None, tr, cw),
                            lambda i, p_ref: (jnp.where(p_ref[0] == k, (k + 1) % N_CHIPS, k), i, 0))

    if axis == 1:
        own_spec = pl.BlockSpec((None, tr, cw), lambda i, p_ref: (0, i, p_ref[0]))
    else:
        own_spec = pl.BlockSpec((None, tr, cw), lambda i, p_ref: (p_ref[0], i, 0))
    return pl.pallas_call(
        body,
        grid_spec=pltpu.PrefetchScalarGridSpec(
            num_scalar_prefetch=1, grid=(nt,),
            in_specs=[slot_spec(k) for k in range(N_CHIPS)] + [own_spec],
            out_specs=pl.BlockSpec((tr, cw), lambda i, p_ref: (p_ref[1] * nt + i, 0))),
        out_shape=jax.ShapeDtypeStruct((2 * rh, cw), F32),
        compiler_params=_cp("parallel"), name=name)(place, *([q] * N_CHIPS), p)


def _sibling_share(shards, *, name):
    n = len(shards)

    def body(*refs):
        o_refs = refs[n:2 * n]
        send_sems, recv_sems = refs[2 * n:]
        x, y, c, _ = _place()
        cps = []
        for i in range(n):
            rh = shards[i].shape[0] // 2
            mine = o_refs[i].at[pl.ds(pl.multiple_of(c * rh, 8), rh), :]
            cp = pltpu.make_async_remote_copy(
                src_ref=mine, dst_ref=mine, send_sem=send_sems.at[i], recv_sem=recv_sems.at[i],
                device_id=(x, y, 1 - c), device_id_type=MESH)
            cp.start()
            cps.append(cp)
        for i in range(n):
            rh = shards[i].shape[0] // 2
            theirs = o_refs[i].at[pl.ds(pl.multiple_of((1 - c) * rh, 8), rh), :]
            pltpu.make_async_remote_copy(
                src_ref=theirs, dst_ref=theirs, send_sem=send_sems.at[i], recv_sem=recv_sems.at[i],
                device_id=(x, y, c), device_id_type=MESH).wait_recv()
        for cp in cps:
            cp.wait_send()

    return pl.pallas_call(
        body, in_specs=[ANY] * n, out_specs=[ANY] * n,
        out_shape=[jax.ShapeDtypeStruct(h.shape, h.dtype) for h in shards],
        input_output_aliases={i: i for i in range(n)},
        scratch_shapes=[pltpu.SemaphoreType.DMA((n,)), pltpu.SemaphoreType.DMA((n,))],
        name=name)(*shards)


def _gather_all(blk, *, name):
    rows, cols = blk.shape

    def body(x_ref, out_ref, send_sems, recv_sems, local_sem):
        x, y, c = lax.axis_index("x"), lax.axis_index("y"), lax.axis_index("c")
        me = 4 * x + 2 * y + c
        mine = pltpu.make_async_copy(x_ref, out_ref.at[me], local_sem)
        mine.start()
        cps = []
        for k in range(1, N_DEV):
            tx = (1 - x) if (k >> 2) & 1 else x
            ty = (1 - y) if (k >> 1) & 1 else y
            tc = (1 - c) if k & 1 else c
            cp = pltpu.make_async_remote_copy(
                src_ref=x_ref, dst_ref=out_ref.at[me], send_sem=send_sems.at[k - 1],
                recv_sem=recv_sems.at[k - 1], device_id=(tx, ty, tc), device_id_type=MESH)
            cp.start()
            cps.append(cp)
        for k in range(1, N_DEV):
            tx = (1 - x) if (k >> 2) & 1 else x
            ty = (1 - y) if (k >> 1) & 1 else y
            tc = (1 - c) if k & 1 else c
            got = out_ref.at[4 * tx + 2 * ty + tc]
            pltpu.make_async_remote_copy(
                src_ref=got, dst_ref=got, send_sem=send_sems.at[k - 1], recv_sem=recv_sems.at[k - 1],
                device_id=(x, y, c), device_id_type=MESH).wait_recv()
        for cp in cps:
            cp.wait_send()
        mine.wait()

    vm = pl.BlockSpec(memory_space=pltpu.VMEM)
    return pl.pallas_call(
        body, in_specs=[vm], out_specs=vm,
        out_shape=jax.ShapeDtypeStruct((N_DEV, rows, cols), blk.dtype),
        scratch_shapes=[pltpu.SemaphoreType.DMA((N_DEV - 1,)), pltpu.SemaphoreType.DMA((N_DEV - 1,)),
                        pltpu.SemaphoreType.DMA],
        name=name)(blk)


def _as_rows(a):
    flat = a.reshape(-1)
    n = flat.shape[0]
    rows = -(-n // (8 * LANES)) * 8
    return jnp.pad(flat, (0, rows * LANES - n)).reshape(rows, LANES)


def _from_rows(p, shape):
    n = int(np.prod(shape))
    return p.reshape(-1)[:n].reshape(shape)


WEIGHT_AXES = (1, 1, 1, 0, 1, 0)
WIRE = BF16


def kernel(x, meta_tokens, w_in, w_na_out, w_hg_out, w_o, w_up, w_down, norm_mix, norm_mlp, norm_final, hg_norm, na_rpb, hg_lb_logits, loss_target, m_meta_tokens, m_w_in, m_w_na_out, m_w_hg_out, m_w_o, m_w_up, m_w_down, m_norm_mix, m_norm_mlp, m_norm_final, m_hg_norm, m_na_rpb, m_hg_lb_logits, v_meta_tokens, v_w_in, v_w_na_out, v_w_hg_out, v_w_o, v_w_up, v_w_down, v_norm_mix, v_norm_mlp, v_norm_final, v_hg_norm, v_na_rpb, v_hg_lb_logits):
    xi, yi, ci = lax.axis_index("x"), lax.axis_index("y"), lax.axis_index("c")
    chip = 2 * xi + yi
    d = x.shape[-1]
    dshard = meta_tokens.shape[1]
    hgw = hg_norm.shape[1]
    lbs = hg_lb_logits.shape[2]
    big = [w_in[0], w_na_out[0], w_hg_out[0], w_o[0], w_up[0], w_down[0]]
    big_m = [m_w_in[0], m_w_na_out[0], m_w_hg_out[0], m_w_o[0], m_w_up[0], m_w_down[0]]
    big_v = [v_w_in[0], v_w_na_out[0], v_w_hg_out[0], v_w_o[0], v_w_up[0], v_w_down[0]]

    place = jnp.stack([chip, ci]).astype(jnp.int32)
    own_w = [_cast_into_full(w, ax, place, name=f"cast_shard_{i}")
             for i, (w, ax) in enumerate(zip(big, WEIGHT_AXES))]
    rest_axes = WEIGHT_AXES[1:]
    (w_in_full,) = _weight_allgather(own_w[:1], WEIGHT_AXES[:1], name="weight_allgather_in")
    ag_send, ag_recv, ag_bufs, ag_token = _allgather_start(own_w[1:], rest_axes, w_in_full,
                                                           name="weight_allgather_rest_start")

    def rest_weights(after):
        got = _allgather_wait(ag_send, ag_recv, ag_bufs, rest_axes, after,
                              name="weight_allgather_rest_wait")
        return _allgather_forward(got, rest_axes, name="weight_allgather_rest_forward")

    small_in = jnp.concatenate([_as_rows(meta_tokens), _as_rows(hg_lb_logits)], axis=0)
    small_all = _gather_all(small_in, name="gather_small_params")[0::2]
    n_meta_rows = N_META * dshard // LANES
    meta_full = (small_all[:, :n_meta_rows].reshape(N_CHIPS, N_META, dshard)
                 .transpose(1, 0, 2).reshape(N_META, d))
    lbl_full = (small_all[:, n_meta_rows:].reshape(N_CHIPS, -1)[:, :4 * lbs]
                .reshape(N_CHIPS, 2, 2, lbs).transpose(1, 2, 0, 3).reshape(2, 2, N_CHIPS * lbs))
    lb = jax.nn.softmax(lbl_full, axis=1)[:, 0]

    c_arr = ci.reshape(1).astype(jnp.int32)

    def chip_partials(dws, axes, tag):
        g3 = [g.reshape(1, *g.shape) if ax == 1
              else g.reshape(N_CHIPS, g.shape[0] // N_CHIPS, g.shape[1]) for g, ax in zip(dws, axes)]
        rx = _sibling_swap(g3, name=f"grad_sibling_swap_{tag}")
        return [_pair_add(g, r, c_arr, out_dtype=WIRE, name=f"grad_pair_add_{tag}_{i}")
                for i, (g, r) in enumerate(zip(g3, rx))]

    in_flight = {}

    def early_grads(dws):
        send, recv, parts, slots, token = _scatter_start(chip_partials(dws, rest_axes, "rest"),
                                                         rest_axes, name="grad_scatter_rest_start")
        in_flight.update(send=send, recv=recv, parts=parts, slots=slots)
        return token

    (loss, dx, dmeta, dw_in, *_, dg_mix, dg_mlp, dg_fin, d_gain, d_rpb, d_lb) = _local_step(
        x[0], loss_target[0], meta_full, w_in_full, rest_weights, norm_mix + ag_token[0:1, 0:1],
        norm_mlp, norm_final.reshape(1, d), hg_norm, na_rpb[0], lb, early_grads)

    parts_in = chip_partials([dw_in], WEIGHT_AXES[:1], "in")
    slots_in = _chip_scatter(parts_in, WEIGHT_AXES[:1], name="grad_chip_scatter_in")
    parts_rest, slots_rest = _scatter_wait(in_flight["send"], in_flight["recv"], in_flight["parts"],
                                           in_flight["slots"], rest_axes, slots_in[0],
                                           name="grad_scatter_rest_wait")
    halves = [_sum_chips(q, p, place, ax, name=f"grad_sum_chips_{i}")
              for i, (q, p, ax) in enumerate(zip(list(slots_in) + slots_rest,
                                                 list(parts_in) + parts_rest, WEIGHT_AXES))]
    g_big = _sibling_share(halves, name="grad_sibling_share")

    d_rpb_c = d_rpb[:, :2 * NA_WIN_W - 1]
    small_g = [dmeta, dg_mix, dg_mlp, dg_fin, d_gain, d_rpb_c, d_lb, loss]
    packed = jnp.concatenate([_as_rows(a) for a in small_g], axis=0)
    total = _sum_slots(_gather_all(packed, name="gather_small_grads"), name="sum_small_grads")
    offs = np.cumsum([0] + [_as_rows(a).shape[0] for a in small_g])
    take = lambda i, shape: _from_rows(total[offs[i]:offs[i + 1]], shape)
    g_meta_full = take(0, (N_META, d))
    g_norm_mix, g_norm_mlp = take(1, (1, d)), take(2, (1, d))
    g_norm_final = take(3, (d,))
    g_hg_norm = take(4, (1, hgw))
    g_rpb = take(5, na_rpb.shape)
    g_lb = take(6, (2, hgw))
    loss_total = take(7, (1, LANES))[0, 0]
    g_meta = lax.dynamic_slice_in_dim(g_meta_full, chip * dshard, dshard, axis=1)
    dl0 = lb * (1.0 - lb) * g_lb
    g_lbl_full = jnp.stack([dl0, -dl0], axis=1)
    g_lbl = lax.dynamic_slice_in_dim(g_lbl_full, chip * lbs, lbs, axis=2)

    big_out = [_adamw(w, g, m, v, name=f"adamw_{i}")
               for i, (w, g, m, v) in enumerate(zip(big, g_big, big_m, big_v))]
    small_w = [meta_tokens, norm_mix, norm_mlp, norm_final, hg_norm, na_rpb, hg_lb_logits]
    small_gr = [g_meta, g_norm_mix, g_norm_mlp, g_norm_final, g_hg_norm, g_rpb, g_lbl]
    small_m = [m_meta_tokens, m_norm_mix, m_norm_mlp, m_norm_final, m_hg_norm, m_na_rpb, m_hg_lb_logits]
    small_v = [v_meta_tokens, v_norm_mix, v_norm_mlp, v_norm_final, v_hg_norm, v_na_rpb, v_hg_lb_logits]
    pk = lambda lst: jnp.concatenate([_as_rows(a) for a in lst], axis=0)
    sd, sm, sv = _adamw(pk(small_w), pk(small_gr), pk(small_m), pk(small_v), name="adamw_small")
    soffs = np.cumsum([0] + [_as_rows(a).shape[0] for a in small_w])
    unpk = lambda p: [_from_rows(p[soffs[i]:soffs[i + 1]], small_w[i].shape) for i in range(len(small_w))]
    sd, sm, sv = unpk(sd), unpk(sm), unpk(sv)

    def order(bigs, smalls):
        return [smalls[0]] + [b.reshape(1, *b.shape) for b in bigs] + smalls[1:]

    grads = order(g_big, small_gr)
    deltas = order([o[0] for o in big_out], sd)
    new_m = order([o[1] for o in big_out], sm)
    new_v = order([o[2] for o in big_out], sv)
    return (loss_total, dx.reshape(1, *dx.shape), *grads, *deltas, *new_m, *new_v)
```

```python
import functools

import numpy as np
import jax
import jax.numpy as jnp
from jax import lax
from jax.experimental import pallas as pl
from jax.experimental.pallas import tpu as pltpu

F32 = jnp.float32
BF16 = jnp.bfloat16
HIGHEST = lax.Precision.HIGHEST

GRID_W = 64
N_META = 16
EPS = 1e-6
NA_HEAD_DIM = 64
NA_WIN_H = 8
NA_WIN_W = 16
HG_DK = 128
HG_CHUNK = 16
LANES = 128
ROW_ALIGN = 128
VMEM_LIMIT = 48 * 1024 * 1024

ADAM_LR = 0.001
ADAM_B1 = 0.9
ADAM_B2 = 0.999
ADAM_EPS = 1e-08
ADAM_WD = 0.01
ADAM_STEP = 10

MESH = pl.DeviceIdType.MESH


def _cp(*sem):
    return pltpu.CompilerParams(dimension_semantics=sem, vmem_limit_bytes=VMEM_LIMIT)


def _sigmoid(x):
    return 1.0 / (1.0 + jnp.exp(-x))


def _dot(a, b, dims, precision=None):
    return lax.dot_general(a, b, (dims, ((), ())), preferred_element_type=F32, precision=precision)


def _nn(a, b, **kw):
    return _dot(a, b, ((1,), (0,)), **kw)


def _nt(a, b, **kw):
    return _dot(a, b, ((1,), (1,)), **kw)


def _tn(a, b, **kw):
    return _dot(a, b, ((0,), (0,)), **kw)


def _matmul(a, b, *, ta=False, tb=False, tm=None, tn=None, tk=None, out_dtype=F32, name,
            precision=None):
    if ta:
        kdim, m = a.shape
    else:
        m, kdim = a.shape
    if tb:
        n, k2 = b.shape
    else:
        k2, n = b.shape
    assert kdim == k2, (a.shape, b.shape, ta, tb)
    if tm is None:
        if ta:
            tm = next(t for t in (512, 256, 128, m) if m % t == 0)
        else:
            tm = m // 2 if (m // 2) % 16 == 0 and m > 512 else m
    if tn is None:
        tn = next(t for t in (512, 256, 128, n) if n % t == 0)
    if tk is None:
        tk = kdim if ta else next(t for t in (1024, 512, 256, 128, kdim) if kdim % t == 0)
    assert m % tm == 0 and n % tn == 0 and kdim % tk == 0, (m, n, kdim, tm, tn, tk)
    nk = kdim // tk
    op_dtype = F32 if precision is not None else BF16

    def body(a_ref, b_ref, o_ref, acc_ref):
        kk = pl.program_id(2)

        @pl.when(kk == 0)
        def _():
            acc_ref[...] = jnp.zeros_like(acc_ref)

        av = a_ref[...].astype(op_dtype)
        bv = b_ref[...].astype(op_dtype)
        dims = ((0 if ta else 1,), (1 if tb else 0,))
        acc_ref[...] += _dot(av, bv, dims, precision=precision)

        @pl.when(kk == nk - 1)
        def _():
            o_ref[...] = acc_ref[...].astype(out_dtype)

    a_spec = (pl.BlockSpec((tk, tm), lambda i, j, k: (k, i)) if ta
              else pl.BlockSpec((tm, tk), lambda i, j, k: (i, k)))
    b_spec = (pl.BlockSpec((tn, tk), lambda i, j, k: (j, k)) if tb
              else pl.BlockSpec((tk, tn), lambda i, j, k: (k, j)))
    return pl.pallas_call(
        body,
        grid=(m // tm, n // tn, nk),
        in_specs=[a_spec, b_spec],
        out_specs=pl.BlockSpec((tm, tn), lambda i, j, k: (i, j)),
        out_shape=jax.ShapeDtypeStruct((m, n), out_dtype),
        scratch_shapes=[pltpu.VMEM((tm, tn), F32)],
        compiler_params=_cp("parallel", "parallel", "arbitrary"),
        name=name,
    )(a, b)


def _rspec(tr, w, cb=0):
    return pl.BlockSpec((tr, w), lambda i: (i, cb))


def _fspec(shape):
    nd = len(shape)
    return pl.BlockSpec(shape, lambda i: (0,) * nd)


def _row_tile(lp):
    return ROW_ALIGN if lp % ROW_ALIGN == 0 else lp


def _rmsnorm_fwd(x, g, *, name):
    lp, d = x.shape
    tr = _row_tile(lp)

    def body(x_ref, g_ref, o_ref):
        xv = x_ref[...]
        r = lax.rsqrt(jnp.mean(xv * xv, axis=-1, keepdims=True) + EPS)
        o_ref[...] = (xv * r * g_ref[...]).astype(BF16)

    return pl.pallas_call(
        body, grid=(lp // tr,),
        in_specs=[_rspec(tr, d), _fspec((1, d))],
        out_specs=_rspec(tr, d),
        out_shape=jax.ShapeDtypeStruct((lp, d), BF16),
        compiler_params=_cp("parallel"), name=name)(x, g)


def _residual_norm(h, t, g, *, name):
    lp, d = h.shape
    tr = _row_tile(lp)

    def body(h_ref, t_ref, g_ref, h1_ref, m_ref):
        xv = h_ref[...] + t_ref[...]
        h1_ref[...] = xv
        r = lax.rsqrt(jnp.mean(xv * xv, axis=-1, keepdims=True) + EPS)
        m_ref[...] = (xv * r * g_ref[...]).astype(BF16)

    return pl.pallas_call(
        body, grid=(lp // tr,),
        in_specs=[_rspec(tr, d), _rspec(tr, d), _fspec((1, d))],
        out_specs=[_rspec(tr, d), _rspec(tr, d)],
        out_shape=[jax.ShapeDtypeStruct((lp, d), F32), jax.ShapeDtypeStruct((lp, d), BF16)],
        compiler_params=_cp("parallel"), name=name)(h, t, g)


def _rmsnorm_bwd_add(x, g, dy, dres, *, name):
    lp, d = x.shape
    tr = _row_tile(lp)

    def body(x_ref, g_ref, dy_ref, dr_ref, dx_ref, dg_ref):
        @pl.when(pl.program_id(0) == 0)
        def _():
            dg_ref[...] = jnp.zeros_like(dg_ref)

        xv = x_ref[...]
        r = lax.rsqrt(jnp.mean(xv * xv, axis=-1, keepdims=True) + EPS)
        xh = xv * r
        dyv = dy_ref[...]
        dg_ref[...] += jnp.sum(dyv * xh, axis=0, keepdims=True)
        dxh = dyv * g_ref[...]
        dx_ref[...] = dr_ref[...] + r * (dxh - xh * jnp.mean(dxh * xh, axis=-1, keepdims=True))

    return pl.pallas_call(
        body, grid=(lp // tr,),
        in_specs=[_rspec(tr, d), _fspec((1, d)), _rspec(tr, d), _rspec(tr, d)],
        out_specs=[_rspec(tr, d), _fspec((1, d))],
        out_shape=[jax.ShapeDtypeStruct((lp, d), F32), jax.ShapeDtypeStruct((1, d), F32)],
        compiler_params=_cp("arbitrary"), name=name)(x, g, dy, dres)


def _final_loss(h1, t2, g, tgt, *, n_tok, name):
    lp, d = h1.shape
    tr = _row_tile(lp)

    def body(h_ref, t_ref, g_ref, tg_ref, dh_ref, loss_ref, dg_ref):
        i = pl.program_id(0)

        @pl.when(i == 0)
        def _():
            loss_ref[...] = jnp.zeros_like(loss_ref)
            dg_ref[...] = jnp.zeros_like(dg_ref)

        xv = h_ref[...] + t_ref[...]
        r = lax.rsqrt(jnp.mean(xv * xv, axis=-1, keepdims=True) + EPS)
        xh = xv * r
        gv = g_ref[...]
        row = i * tr + lax.broadcasted_iota(jnp.int32, (tr, 1), 0)
        valid = (row >= N_META) & (row < N_META + n_tok)
        err = jnp.where(valid, xh * gv - tg_ref[...], 0.0)
        loss_ref[...] += jnp.sum(0.5 * err * err) / d
        dy = err / d
        dg_ref[...] += jnp.sum(dy * xh, axis=0, keepdims=True)
        dxh = dy * gv
        dh_ref[...] = r * (dxh - xh * jnp.mean(dxh * xh, axis=-1, keepdims=True))

    return pl.pallas_call(
        body, grid=(lp // tr,),
        in_specs=[_rspec(tr, d), _rspec(tr, d), _fspec((1, d)), _rspec(tr, d)],
        out_specs=[_rspec(tr, d), _fspec((1, LANES)), _fspec((1, d))],
        out_shape=[jax.ShapeDtypeStruct((lp, d), F32), jax.ShapeDtypeStruct((1, LANES), F32),
                   jax.ShapeDtypeStruct((1, d), F32)],
        compiler_params=_cp("arbitrary"), name=name)(h1, t2, g, tgt)


def _relu2(u, *, name):
    lp, f = u.shape
    tr = _row_tile(lp)

    def body(u_ref, o_ref):
        rv = jnp.maximum(u_ref[...], 0.0)
        o_ref[...] = (rv * rv).astype(BF16)

    return pl.pallas_call(
        body, grid=(lp // tr,), in_specs=[_rspec(tr, f)], out_specs=_rspec(tr, f),
        out_shape=jax.ShapeDtypeStruct((lp, f), BF16),
        compiler_params=_cp("parallel"), name=name)(u)


def _relu2_bwd(dact, u, *, name):
    lp, f = u.shape
    tr = _row_tile(lp)

    def body(d_ref, u_ref, o_ref):
        o_ref[...] = (d_ref[...] * 2.0 * jnp.maximum(u_ref[...], 0.0)).astype(BF16)

    return pl.pallas_call(
        body, grid=(lp // tr,), in_specs=[_rspec(tr, f), _rspec(tr, f)], out_specs=_rspec(tr, f),
        out_shape=jax.ShapeDtypeStruct((lp, f), BF16),
        compiler_params=_cp("parallel"), name=name)(dact, u)


def _gate_mix(proj, y_na, y_hg, *, col_gate, name):
    lp, d = y_na.shape
    tr = _row_tile(lp)
    cb = col_gate // d

    def body(gn_ref, gh_ref, yn_ref, yh_ref, o_ref):
        o_ref[...] = (_sigmoid(gn_ref[...]) * yn_ref[...]
                      + _sigmoid(gh_ref[...]) * yh_ref[...]).astype(BF16)

    return pl.pallas_call(
        body, grid=(lp // tr,),
        in_specs=[_rspec(tr, d, cb), _rspec(tr, d, cb + 1), _rspec(tr, d), _rspec(tr, d)],
        out_specs=_rspec(tr, d),
        out_shape=jax.ShapeDtypeStruct((lp, d), BF16),
        compiler_params=_cp("parallel"), name=name)(proj, proj, y_na, y_hg)


def _gate_mix_bwd(proj, y_na, y_hg, dmix, *, col_gate, name):
    lp, d = y_na.shape
    tr = _row_tile(lp)
    cb = col_gate // d

    def body(gn_ref, gh_ref, yn_ref, yh_ref, dm_ref, dyn_ref, dyh_ref, dgn_ref, dgh_ref):
        dm = dm_ref[...]
        sn = _sigmoid(gn_ref[...])
        sh = _sigmoid(gh_ref[...])
        dyn_ref[...] = (dm * sn).astype(BF16)
        dyh_ref[...] = (dm * sh).astype(BF16)
        dgn_ref[...] = (dm * yn_ref[...] * sn * (1.0 - sn)).astype(BF16)
        dgh_ref[...] = (dm * yh_ref[...] * sh * (1.0 - sh)).astype(BF16)

    sds = jax.ShapeDtypeStruct((lp, d), BF16)
    return pl.pallas_call(
        body, grid=(lp // tr,),
        in_specs=[_rspec(tr, d, cb), _rspec(tr, d, cb + 1), _rspec(tr, d), _rspec(tr, d),
                  _rspec(tr, d)],
        out_specs=[_rspec(tr, d)] * 4, out_shape=[sds] * 4,
        compiler_params=_cp("parallel"), name=name)(proj, proj, y_na, y_hg, dmix)


def _hg_out(o_f, o_b, proj, gain, *, col_g, name):
    lp, w = o_f.shape
    tr = _row_tile(lp)
    hh = w // HG_DK

    def body(of_ref, ob_ref, g_ref, gain_ref, y_ref):
        gv = g_ref[...]
        sg = gv * _sigmoid(gv)
        for h in range(hh):
            sl = slice(h * HG_DK, (h + 1) * HG_DK)
            o = of_ref[:, sl] + ob_ref[:, sl]
            r = lax.rsqrt(jnp.mean(o * o, axis=-1, keepdims=True) + EPS)
            y_ref[:, sl] = (o * r * gain_ref[:, sl] * sg[:, sl]).astype(BF16)

    return pl.pallas_call(
        body, grid=(lp // tr,),
        in_specs=[_rspec(tr, w), _rspec(tr, w), _rspec(tr, w, col_g // w), _fspec((1, w))],
        out_specs=_rspec(tr, w),
        out_shape=jax.ShapeDtypeStruct((lp, w), BF16),
        compiler_params=_cp("parallel"), name=name)(o_f, o_b, proj, gain)


def _hg_out_bwd(o_f, o_b, proj, gain, dy, *, col_g, name):
    lp, w = o_f.shape
    tr = _row_tile(lp)
    hh = w // HG_DK

    def body(of_ref, ob_ref, g_ref, gain_ref, dy_ref, do_ref, dg_ref, dgain_ref):
        @pl.when(pl.program_id(0) == 0)
        def _():
            dgain_ref[...] = jnp.zeros_like(dgain_ref)

        for h in range(hh):
            sl = slice(h * HG_DK, (h + 1) * HG_DK)
            gv = g_ref[:, sl]
            s = _sigmoid(gv)
            sg = gv * s
            dsg = s + gv * s * (1.0 - s)
            o = of_ref[:, sl] + ob_ref[:, sl]
            r = lax.rsqrt(jnp.mean(o * o, axis=-1, keepdims=True) + EPS)
            on = o * r
            dyv = dy_ref[:, sl]
            gn = gain_ref[:, sl]
            dgain_ref[:, sl] += jnp.sum(dyv * on * sg, axis=0, keepdims=True)
            dg_ref[:, sl] = (dyv * on * gn * dsg).astype(BF16)
            don = dyv * gn * sg
            do_ref[:, sl] = r * (don - on * jnp.mean(don * on, axis=-1, keepdims=True))

    return pl.pallas_call(
        body, grid=(lp // tr,),
        in_specs=[_rspec(tr, w), _rspec(tr, w), _rspec(tr, w, col_g // w), _fspec((1, w)),
                  _rspec(tr, w)],
        out_specs=[_rspec(tr, w), _rspec(tr, w), _fspec((1, w))],
        out_shape=[jax.ShapeDtypeStruct((lp, w), F32), jax.ShapeDtypeStruct((lp, w), BF16),
                   jax.ShapeDtypeStruct((1, w), F32)],
        compiler_params=_cp("arbitrary"), name=name)(o_f, o_b, proj, gain, dy)


HG_GROUP = 8
HG_ROWS = HG_GROUP * HG_CHUNK


def _hg_gates(zq, z, lbv):
    qh = zq * _sigmoid(zq)
    s = _sigmoid(z)
    f = lbv + (1.0 - lbv) * s
    kk = (1.0 - lbv) * _sigmoid(-z)
    return qh, s, f, jnp.log(f), kk


def _tri_blocks(upper):
    r = lax.broadcasted_iota(jnp.int32, (HG_ROWS, HG_ROWS), 0)
    c = lax.broadcasted_iota(jnp.int32, (HG_ROWS, HG_ROWS), 1)
    same = (r // HG_CHUNK) == (c // HG_CHUNK)
    return jnp.where(same & ((c >= r) if upper else (c <= r)), 1.0, 0.0).astype(F32)


def _g3(a):
    return a.reshape(HG_GROUP, HG_CHUNK, a.shape[-1])


def _hg_scan_fwd(proj, lb, *, reverse, col_q, col_z, col_i, hh, name):
    lp = proj.shape[0]
    n_groups = lp // HG_ROWS
    n_chunks = lp // HG_CHUNK
    c16 = HG_CHUNK
    last = 0 if reverse else c16 - 1

    def body(q_ref, z_ref, i_ref, lb_ref, o_ref, st_ref):
        lbv = lb_ref[...]
        tri = _tri_blocks(reverse)
        row = lax.broadcasted_iota(jnp.int32, (1, c16, HG_DK), 1)

        def group(gi, st):
            gg = (n_groups - 1 - gi) if reverse else gi
            r0 = pl.multiple_of(gg * HG_ROWS, HG_ROWS)
            v = i_ref[pl.ds(r0, HG_ROWS), :]
            qh, _, _, g, kk = _hg_gates(q_ref[pl.ds(r0, HG_ROWS), :], z_ref[pl.ds(r0, HG_ROWS), :],
                                        lbv)
            b = _nn(tri, g, precision=HIGHEST)
            b3, qh3, kk3, v3 = _g3(b), _g3(qh), _g3(kk), _g3(v)
            bl3 = b3[:, last:last + 1, :]
            qe = (qh * jnp.exp(b)).astype(BF16)
            kd = (kk3 * jnp.exp(bl3 - b3)).reshape(HG_ROWS, HG_DK).astype(BF16)
            decay3 = jnp.exp(bl3)
            v16 = v.astype(BF16)
            o3 = jnp.zeros((HG_GROUP, c16, HG_DK), F32)
            for t in range(c16):
                mask = (row >= t) if reverse else (row <= t)
                e = jnp.where(mask, jnp.exp(jnp.minimum(b3[:, t:t + 1, :] - b3, 0.0)), 0.0)
                a_col = jnp.sum(qh3[:, t:t + 1, :] * kk3 * e, axis=2, keepdims=True)
                o_t = jnp.sum(a_col * v3, axis=1, keepdims=True)
                o3 = o3 + jnp.where(row == t, o_t, 0.0)
            inter = [None] * HG_GROUP
            for c in (reversed(range(HG_GROUP)) if reverse else range(HG_GROUP)):
                sl = slice(c * c16, (c + 1) * c16)
                st_ref[gg * HG_GROUP + c] = st
                inter[c] = _nt(qe[sl], st.astype(BF16))
                st = decay3[c] * st + _tn(v16[sl], kd[sl])
            o_ref[pl.ds(r0, HG_ROWS), :] = (o3.reshape(HG_ROWS, HG_DK)
                                            + jnp.concatenate(inter, axis=0))
            return st

        lax.fori_loop(0, n_groups, group, jnp.zeros((HG_DK, HG_DK), F32))

    cspec = lambda col: pl.BlockSpec((lp, HG_DK), lambda h: (0, col // HG_DK + h))
    return pl.pallas_call(
        body, grid=(hh,),
        in_specs=[cspec(col_q), cspec(col_z), cspec(col_i),
                  pl.BlockSpec((None, 1, HG_DK), lambda h: (h, 0, 0))],
        out_specs=[pl.BlockSpec((lp, HG_DK), lambda h: (0, h)),
                   pl.BlockSpec((None, n_chunks, HG_DK, HG_DK), lambda h: (h, 0, 0, 0))],
        out_shape=[jax.ShapeDtypeStruct((lp, hh * HG_DK), F32),
                   jax.ShapeDtypeStruct((hh, n_chunks, HG_DK, HG_DK), F32)],
        compiler_params=_cp("parallel"), name=name)(proj, proj, proj, lb)


def _hg_scan_bwd(proj, lb, states, do, *, reverse, col_q, col_z, col_i, hh, name):
    lp = proj.shape[0]
    n_groups = lp // HG_ROWS
    n_chunks = lp // HG_CHUNK
    c16 = HG_CHUNK
    last = 0 if reverse else c16 - 1

    def body(q_ref, z_ref, i_ref, lb_ref, st_ref, do_ref, dq_ref, dz_ref, dv_ref, dlb_ref):
        lbv = lb_ref[...]
        tri = _tri_blocks(reverse)
        tri_t = _tri_blocks(not reverse)
        row = lax.broadcasted_iota(jnp.int32, (1, c16, HG_DK), 1)

        def group(gi, carry):
            dst, dlb = carry
            gg = gi if reverse else (n_groups - 1 - gi)
            r0 = pl.multiple_of(gg * HG_ROWS, HG_ROWS)
            zq = q_ref[pl.ds(r0, HG_ROWS), :]
            v = i_ref[pl.ds(r0, HG_ROWS), :]
            dov = do_ref[pl.ds(r0, HG_ROWS), :]
            qh, s, f, g, kk = _hg_gates(zq, z_ref[pl.ds(r0, HG_ROWS), :], lbv)
            b = _nn(tri, g, precision=HIGHEST)
            b3, qh3, kk3, do3 = _g3(b), _g3(qh), _g3(kk), _g3(dov)
            bl3 = b3[:, last:last + 1, :]
            eb = jnp.exp(b)
            ebl3 = jnp.exp(bl3 - b3)
            decay3 = jnp.exp(bl3)
            qe16 = (qh * eb).astype(BF16)
            kd16 = (kk3 * ebl3).reshape(HG_ROWS, HG_DK).astype(BF16)
            v16, do16 = v.astype(BF16), dov.astype(BF16)
            do_s, da_t, v_ds, dv_st, dbl_st = ([None] * HG_GROUP for _ in range(5))
            for c in (range(HG_GROUP) if reverse else reversed(range(HG_GROUP))):
                sl = slice(c * c16, (c + 1) * c16)
                st = st_ref[gg * HG_GROUP + c]
                st16, dst16 = st.astype(BF16), dst.astype(BF16)
                do_s[c] = _nn(do16[sl], st16)
                da_t[c] = _nt(v16[sl], do16[sl])
                v_ds[c] = _nn(v16[sl], dst16)
                dv_st[c] = _nt(kd16[sl], dst16)
                dbl_st[c] = decay3[c] * jnp.sum(st * dst, axis=0, keepdims=True)
                dst = decay3[c] * dst + _tn(do16[sl], qe16[sl])
            cat3 = lambda parts: _g3(jnp.concatenate(parts, axis=0))
            dq3 = _g3(eb) * cat3(do_s)
            dk_state3 = ebl3 * cat3(v_ds)
            dk3 = dk_state3
            dv3 = cat3(dv_st)
            da3 = cat3(da_t)
            for t in range(c16):
                mask = (row >= t) if reverse else (row <= t)
                e = jnp.where(mask, jnp.exp(jnp.minimum(b3[:, t:t + 1, :] - b3, 0.0)), 0.0)
                qt = qh3[:, t:t + 1, :]
                a_col = jnp.sum(qt * kk3 * e, axis=2, keepdims=True)
                ce = da3[:, :, t:t + 1] * e
                dq_t = jnp.sum(ce * kk3, axis=1, keepdims=True)
                dq3 = dq3 + jnp.where(row == t, dq_t, 0.0)
                dk3 = dk3 + ce * qt
                dv3 = dv3 + a_col * do3[:, t:t + 1, :]
            dbl3 = (jnp.concatenate([d[None] for d in dbl_st], axis=0)
                    + jnp.sum(kk3 * dk_state3, axis=1, keepdims=True))
            db3 = qh3 * dq3 - kk3 * dk3 + jnp.where(row == last, dbl3, 0.0)
            dg = _nn(tri_t, db3.reshape(HG_ROWS, HG_DK), precision=HIGHEST)
            df = dg / f - dk3.reshape(HG_ROWS, HG_DK)
            sq = _sigmoid(zq)
            dq_ref[pl.ds(r0, HG_ROWS), :] = (dq3.reshape(HG_ROWS, HG_DK)
                                             * (sq + zq * sq * (1.0 - sq)))
            dz_ref[pl.ds(r0, HG_ROWS), :] = df * (1.0 - lbv) * s * (1.0 - s)
            dv_ref[pl.ds(r0, HG_ROWS), :] = dv3.reshape(HG_ROWS, HG_DK)
            return dst, dlb + jnp.sum(df * (1.0 - s), axis=0, keepdims=True)

        _, dlb = lax.fori_loop(0, n_groups, group,
                               (jnp.zeros((HG_DK, HG_DK), F32), jnp.zeros((1, HG_DK), F32)))
        dlb_ref[...] = dlb

    cspec = lambda col: pl.BlockSpec((lp, HG_DK), lambda h: (0, col // HG_DK + h))
    ospec = pl.BlockSpec((lp, HG_DK), lambda h: (0, h))
    sds = jax.ShapeDtypeStruct((lp, hh * HG_DK), F32)
    return pl.pallas_call(
        body, grid=(hh,),
        in_specs=[cspec(col_q), cspec(col_z), cspec(col_i),
                  pl.BlockSpec((None, 1, HG_DK), lambda h: (h, 0, 0)),
                  pl.BlockSpec((None, n_chunks, HG_DK, HG_DK), lambda h: (h, 0, 0, 0)),
                  ospec],
        out_specs=[ospec, ospec, ospec, pl.BlockSpec((None, 1, HG_DK), lambda h: (h, 0, 0))],
        out_shape=[sds, sds, sds, jax.ShapeDtypeStruct((hh, 1, HG_DK), F32)],
        compiler_params=_cp("parallel"), name=name)(proj, proj, proj, lb, states, do)


def _na_rows(r, rows):
    rs = jnp.clip(r - NA_WIN_H // 2, 0, rows - NA_WIN_H)
    i0 = rs - r + (NA_WIN_H - 1)
    q0 = pl.multiple_of(N_META + GRID_W * r, 16)
    k0 = pl.multiple_of(N_META + GRID_W * rs, 16)
    return i0, q0, k0


def _na_scores(q16, k16, km16, tb_ref, i0, scale):
    s = _nt(q16, k16) * scale
    bias = jnp.concatenate([tb_ref[i0 + j] for j in range(NA_WIN_H)], axis=1)
    return s + bias, _nt(q16, km16) * scale


def _na_fwd(qkv, tb, *, n_tok, name):
    _, nh, lp, dh = qkv.shape
    rows = n_tok // GRID_W
    scale = dh ** -0.5
    kw = NA_WIN_H * GRID_W

    def body(q_ref, k_ref, v_ref, tb_ref, o_ref, lse_ref):
        o_ref[...] = jnp.zeros_like(o_ref)
        lse_ref[...] = jnp.zeros_like(lse_ref)
        km = k_ref[0:N_META, :].astype(BF16)
        vm = v_ref[0:N_META, :].astype(BF16)
        s = _nt(q_ref[0:N_META, :].astype(BF16), km) * scale
        m = jnp.max(s, axis=1, keepdims=True)
        p = jnp.exp(s - m)
        l = jnp.sum(p, axis=1, keepdims=True)
        o_ref[0:N_META, :] = _nn(p.astype(BF16), vm) / l
        lse_ref[0:N_META, :] = m + jnp.log(l)

        def step(r, carry):
            i0, q0, k0 = _na_rows(r, rows)
            q16 = q_ref[pl.ds(q0, GRID_W), :].astype(BF16)
            k16 = k_ref[pl.ds(k0, kw), :].astype(BF16)
            v16 = v_ref[pl.ds(k0, kw), :].astype(BF16)
            s, sm = _na_scores(q16, k16, km, tb_ref, i0, scale)
            m = jnp.maximum(jnp.max(s, axis=1, keepdims=True), jnp.max(sm, axis=1, keepdims=True))
            p = jnp.exp(s - m)
            pm = jnp.exp(sm - m)
            l = jnp.sum(p, axis=1, keepdims=True) + jnp.sum(pm, axis=1, keepdims=True)
            o = _nn(p.astype(BF16), v16) + _nn(pm.astype(BF16), vm)
            o_ref[pl.ds(q0, GRID_W), :] = o / l
            lse_ref[pl.ds(q0, GRID_W), :] = m + jnp.log(l)
            return carry

        lax.fori_loop(0, rows, step, 0)

    hspec = lambda which: pl.BlockSpec((None, None, lp, dh), lambda h: (which, h, 0, 0))
    return pl.pallas_call(
        body, grid=(nh,),
        in_specs=[hspec(0), hspec(1), hspec(2),
                  pl.BlockSpec((None, 2 * NA_WIN_H - 1, GRID_W, GRID_W), lambda h: (h, 0, 0, 0))],
        out_specs=[pl.BlockSpec((None, lp, dh), lambda h: (h, 0, 0)),
                   pl.BlockSpec((None, lp, 1), lambda h: (h, 0, 0))],
        out_shape=[jax.ShapeDtypeStruct((nh, lp, dh), F32), jax.ShapeDtypeStruct((nh, lp, 1), F32)],
        compiler_params=_cp("parallel"), name=name)(qkv, qkv, qkv, tb)


def _na_bwd(qkv, tb, o, lse, do, *, n_tok, name):
    _, nh, lp, dh = qkv.shape
    rows = n_tok // GRID_W
    scale = dh ** -0.5
    kw = NA_WIN_H * GRID_W

    def body(q_ref, k_ref, v_ref, tb_ref, o_ref, lse_ref, do_ref, dq_ref, dk_ref, dv_ref, dtb_ref):
        dq_ref[...] = jnp.zeros_like(dq_ref)
        dk_ref[...] = jnp.zeros_like(dk_ref)
        dv_ref[...] = jnp.zeros_like(dv_ref)
        dtb_ref[...] = jnp.zeros_like(dtb_ref)
        km = k_ref[0:N_META, :].astype(BF16)
        vm = v_ref[0:N_META, :].astype(BF16)
        qm = q_ref[0:N_META, :].astype(BF16)
        dom = do_ref[0:N_META, :]
        p = jnp.exp(_nt(qm, km) * scale - lse_ref[0:N_META, :])
        dp = _nt(dom.astype(BF16), vm)
        delta = jnp.sum(dom * o_ref[0:N_META, :], axis=1, keepdims=True)
        ds = (p * (dp - delta)).astype(BF16)
        dq_ref[0:N_META, :] = _nn(ds, km) * scale
        dkm0 = _tn(ds, qm) * scale
        dvm0 = _tn(p.astype(BF16), dom.astype(BF16))

        def step(r, carry):
            dkm, dvm = carry
            i0, q0, k0 = _na_rows(r, rows)
            q16 = q_ref[pl.ds(q0, GRID_W), :].astype(BF16)
            k16 = k_ref[pl.ds(k0, kw), :].astype(BF16)
            v16 = v_ref[pl.ds(k0, kw), :].astype(BF16)
            s, sm = _na_scores(q16, k16, km, tb_ref, i0, scale)
            lse = lse_ref[pl.ds(q0, GRID_W), :]
            p = jnp.exp(s - lse)
            pm = jnp.exp(sm - lse)
            dov = do_ref[pl.ds(q0, GRID_W), :]
            do16 = dov.astype(BF16)
            delta = jnp.sum(dov * o_ref[pl.ds(q0, GRID_W), :], axis=1, keepdims=True)
            ds = p * (_nt(do16, v16) - delta)
            dsm = (pm * (_nt(do16, vm) - delta)).astype(BF16)
            ds16 = ds.astype(BF16)
            dq_ref[pl.ds(q0, GRID_W), :] = (_nn(ds16, k16) + _nn(dsm, km)) * scale
            dk_ref[pl.ds(k0, kw), :] += _tn(ds16, q16) * scale
            dv_ref[pl.ds(k0, kw), :] += _tn(p.astype(BF16), do16)
            for j in range(NA_WIN_H):
                dtb_ref[i0 + j] += ds[:, j * GRID_W:(j + 1) * GRID_W]
            return (dkm + _tn(dsm, q16) * scale, dvm + _tn(pm.astype(BF16), do16))

        dkm, dvm = lax.fori_loop(0, rows, step, (dkm0, dvm0))
        dk_ref[0:N_META, :] += dkm
        dv_ref[0:N_META, :] += dvm

    hspec = lambda which: pl.BlockSpec((None, None, lp, dh), lambda h: (which, h, 0, 0))
    h3 = pl.BlockSpec((None, lp, dh), lambda h: (h, 0, 0))
    tbs = pl.BlockSpec((None, 2 * NA_WIN_H - 1, GRID_W, GRID_W), lambda h: (h, 0, 0, 0))
    sds = jax.ShapeDtypeStruct((nh, lp, dh), F32)
    return pl.pallas_call(
        body, grid=(nh,),
        in_specs=[hspec(0), hspec(1), hspec(2), tbs, h3,
                  pl.BlockSpec((None, lp, 1), lambda h: (h, 0, 0)), h3],
        out_specs=[h3, h3, h3, tbs],
        out_shape=[sds, sds, sds, jax.ShapeDtypeStruct(tb.shape, F32)],
        compiler_params=_cp("parallel"), name=name)(qkv, qkv, qkv, tb, o, lse, do)


def _rpb_onehot():
    c = np.arange(GRID_W)[:, None]
    w = np.arange(GRID_W)[None, :]
    cs = np.clip(c - NA_WIN_W // 2, 0, GRID_W - NA_WIN_W)
    in_win = (w >= cs) & (w < cs + NA_WIN_W)
    dc = np.clip(w - c, -(NA_WIN_W - 1), NA_WIN_W - 1) + NA_WIN_W - 1
    oh = np.zeros((LANES, GRID_W * GRID_W), np.float32)
    flat = np.arange(GRID_W * GRID_W).reshape(GRID_W, GRID_W)
    oh[dc[in_win], flat[in_win]] = 1.0
    neg = np.where(in_win, 0.0, -1e30).astype(np.float32).reshape(1, -1)
    return oh, neg


def _assemble_dproj(dqkv, dq_f, dq_b, dz_f, dz_b, dv_f, dv_b, dg, dgn, dgh, *, name):
    _, nh, lp, dh = dqkv.shape
    naw = nh * dh
    hgw = dq_f.shape[1]
    d = dgn.shape[1]
    cols = 3 * naw + 5 * hgw + 2 * d
    tr = _row_tile(lp)

    def body(na_ref, qf_ref, qb_ref, zf_ref, zb_ref, vf_ref, vb_ref, g_ref, gn_ref, gh_ref, o_ref):
        for which in range(3):
            for h in range(nh):
                c0 = which * naw + h * dh
                o_ref[:, c0:c0 + dh] = na_ref[which, h].astype(BF16)
        c0 = 3 * naw
        o_ref[:, c0:c0 + hgw] = (qf_ref[...] + qb_ref[...]).astype(BF16)
        o_ref[:, c0 + hgw:c0 + 2 * hgw] = zf_ref[...].astype(BF16)
        o_ref[:, c0 + 2 * hgw:c0 + 3 * hgw] = zb_ref[...].astype(BF16)
        o_ref[:, c0 + 3 * hgw:c0 + 4 * hgw] = (vf_ref[...] + vb_ref[...]).astype(BF16)
        o_ref[:, c0 + 4 * hgw:c0 + 5 * hgw] = g_ref[...]
        o_ref[:, c0 + 5 * hgw:c0 + 5 * hgw + d] = gn_ref[...]
        o_ref[:, c0 + 5 * hgw + d:] = gh_ref[...]

    hg = _rspec(tr, hgw)
    return pl.pallas_call(
        body, grid=(lp // tr,),
        in_specs=[pl.BlockSpec((3, nh, tr, dh), lambda i: (0, 0, i, 0)),
                  hg, hg, hg, hg, hg, hg, hg, _rspec(tr, d), _rspec(tr, d)],
        out_specs=_rspec(tr, cols),
        out_shape=jax.ShapeDtypeStruct((lp, cols), BF16),
        compiler_params=_cp("parallel"), name=name)(dqkv, dq_f, dq_b, dz_f, dz_b, dv_f, dv_b, dg,
                                                    dgn, dgh)


def _adamw(w, g, m, v, *, name):
    rows, cols = w.shape
    tr = 256 if rows % 256 == 0 else rows

    def body(w_ref, g_ref, m_ref, v_ref, d_ref, mo_ref, vo_ref):
        gv = g_ref[...]
        mn = ADAM_B1 * m_ref[...] + (1.0 - ADAM_B1) * gv
        vn = ADAM_B2 * v_ref[...] + (1.0 - ADAM_B2) * (gv * gv)
        m_hat = mn / (1.0 - ADAM_B1 ** ADAM_STEP)
        v_hat = vn / (1.0 - ADAM_B2 ** ADAM_STEP)
        d_ref[...] = -ADAM_LR * (m_hat / (jnp.sqrt(v_hat) + ADAM_EPS) + ADAM_WD * w_ref[...])
        mo_ref[...] = mn
        vo_ref[...] = vn

    spec = _rspec(tr, cols)
    sds = jax.ShapeDtypeStruct((rows, cols), F32)
    return pl.pallas_call(
        body, grid=(rows // tr,), in_specs=[spec] * 4, out_specs=[spec] * 3, out_shape=[sds] * 3,
        compiler_params=_cp("parallel"), name=name)(w, g, m, v)


def _local_step(x, tgt, meta, first_weight, rest_weights, g_mix, g_mlp, g_fin, hg_gain, rpb, lb,
                early_grads=None, late_grad=None):
    n_tok, d = x.shape
    hgw = hg_gain.shape[1]
    nh, hh = rpb.shape[0], hgw // HG_DK
    naw = nh * NA_HEAD_DIM
    l_real = N_META + n_tok
    lp = -(-l_real // ROW_ALIGN) * ROW_ALIGN
    n_chunks = l_real // HG_CHUNK
    pad = lp - l_real
    col_qhg = 3 * naw
    col_zf, col_zb, col_i, col_g = (col_qhg + hgw, col_qhg + 2 * hgw, col_qhg + 3 * hgw,
                                    col_qhg + 4 * hgw)
    col_gate = col_qhg + 5 * hgw

    zpad = jnp.zeros((pad, d), F32)
    h0 = jnp.concatenate([meta, x, zpad], axis=0)
    tgt_p = jnp.concatenate([jnp.zeros((N_META, d), F32), tgt, zpad], axis=0)

    oh_np, neg_np = _rpb_onehot()
    oh = jnp.asarray(oh_np)
    rpb_p = jnp.pad(rpb.reshape(nh * (2 * NA_WIN_H - 1), 2 * NA_WIN_W - 1),
                    ((0, 0), (0, LANES - (2 * NA_WIN_W - 1))))
    tb = _matmul(rpb_p, oh, tm=rpb_p.shape[0], tn=512, tk=LANES, precision=HIGHEST,
                 name="rpb_expand")
    tb = (tb + jnp.asarray(neg_np)).reshape(nh, 2 * NA_WIN_H - 1, GRID_W, GRID_W)

    a = _rmsnorm_fwd(h0, g_mix, name="norm_mix")
    w_in = first_weight(a)
    proj = _matmul(a, w_in, name="mm_in")
    qkv = proj[:, :3 * naw].reshape(lp, 3, nh, NA_HEAD_DIM).transpose(1, 2, 0, 3)
    o_na_hm, lse = _na_fwd(qkv, tb, n_tok=n_tok, name="na_fwd")
    o_na = o_na_hm.transpose(1, 0, 2).reshape(lp, naw)
    lb_f = lb[0].reshape(hh, 1, HG_DK)
    lb_b = lb[1].reshape(hh, 1, HG_DK)
    scan_kw = dict(col_q=col_qhg, col_i=col_i, hh=hh)
    o_f, st_f = _hg_scan_fwd(proj, lb_f, reverse=False, col_z=col_zf, name="hg_scan_f", **scan_kw)
    o_b, st_b = _hg_scan_fwd(proj, lb_b, reverse=True, col_z=col_zb, name="hg_scan_b", **scan_kw)
    o_hg = _hg_out(o_f, o_b, proj, hg_gain, col_g=col_g, name="hg_out")
    w_na, w_hg, w_o, w_up, w_down = rest_weights(o_hg)
    y_na = _matmul(o_na, w_na, name="mm_na_out")
    y_hg = _matmul(o_hg, w_hg, name="mm_hg_out")
    mix = _gate_mix(proj, y_na, y_hg, col_gate=col_gate, name="gate_mix")
    t1 = _matmul(mix, w_o, name="mm_o")
    h1, mlp_in = _residual_norm(h0, t1, g_mlp, name="resid_norm_mlp")
    u = _matmul(mlp_in, w_up, name="mm_up")
    act = _relu2(u, name="relu2")
    t2 = _matmul(act, w_down, name="mm_down")
    dh2, loss, dg_fin = _final_loss(h1, t2, g_fin, tgt_p, n_tok=n_tok, name="final_loss")

    dact = _matmul(dh2, w_down, tb=True, name="mm_down_dx")
    dw_down = _matmul(act, dh2, ta=True, name="mm_down_dw")
    du = _relu2_bwd(dact, u, name="relu2_bwd")
    dm = _matmul(du, w_up, tb=True, name="mm_up_dx")
    dw_up = _matmul(mlp_in, du, ta=True, name="mm_up_dw")
    dh1, dg_mlp = _rmsnorm_bwd_add(h1, g_mlp, dm, dh2, name="norm_mlp_bwd")
    dmix = _matmul(dh1, w_o, tb=True, name="mm_o_dx")
    dw_o = _matmul(mix, dh1, ta=True, name="mm_o_dw")
    dy_na, dy_hg, dgn, dgh = _gate_mix_bwd(proj, y_na, y_hg, dmix, col_gate=col_gate,
                                           name="gate_mix_bwd")
    do_na = _matmul(dy_na, w_na, tb=True, name="mm_na_out_dx")
    dw_na = _matmul(o_na, dy_na, ta=True, name="mm_na_out_dw")
    do_hg = _matmul(dy_hg, w_hg, tb=True, name="mm_hg_out_dx")
    dw_hg = _matmul(o_hg, dy_hg, ta=True, name="mm_hg_out_dw")
    token = early_grads([dw_na, dw_hg, dw_o, dw_up, dw_down]) if early_grads else None
    if token is not None:
        hg_gain = hg_gain + token[0:1, 0:1]
    d_o, dg_hg, d_gain = _hg_out_bwd(o_f, o_b, proj, hg_gain, do_hg, col_g=col_g, name="hg_out_bwd")
    dq_f, dz_f, dv_f, dlb_f = _hg_scan_bwd(proj, lb_f, st_f, d_o, reverse=False, col_z=col_zf,
                                           name="hg_scan_f_bwd", **scan_kw)
    dq_b, dz_b, dv_b, dlb_b = _hg_scan_bwd(proj, lb_b, st_b, d_o, reverse=True, col_z=col_zb,
                                           name="hg_scan_b_bwd", **scan_kw)
    do_na_hm = do_na.reshape(lp, nh, NA_HEAD_DIM).transpose(1, 0, 2)
    dq_na, dk_na, dv_na, dtb = _na_bwd(qkv, tb, o_na_hm, lse, do_na_hm, n_tok=n_tok, name="na_bwd")
    dqkv = jnp.stack([dq_na, dk_na, dv_na], axis=0)
    dproj = _assemble_dproj(dqkv, dq_f, dq_b, dz_f, dz_b, dv_f, dv_b, dg_hg, dgn, dgh,
                            name="assemble_dproj")
    dw_in = _matmul(a, dproj, ta=True, name="mm_in_dw")
    token = late_grad(dw_in) if late_grad else None
    if token is not None:
        g_mix = g_mix + token[0:1, 0:1]
    da = _matmul(dproj, w_in, tb=True, name="mm_in_dx")
    dh0, dg_mix = _rmsnorm_bwd_add(h0, g_mix, da, dh1, name="norm_mix_bwd")
    d_rpb = _matmul(dtb.reshape(nh * (2 * NA_WIN_H - 1), GRID_W * GRID_W), oh, tb=True,
                    tm=nh * (2 * NA_WIN_H - 1), tn=LANES, tk=1024, precision=HIGHEST,
                    name="rpb_reduce")
    d_lb = jnp.concatenate([dlb_f.reshape(1, hgw), dlb_b.reshape(1, hgw)], axis=0)
    return (loss, dh0[N_META:l_real], dh0[:N_META], dw_in, dw_na, dw_hg, dw_o, dw_up, dw_down,
            dg_mix, dg_mlp, dg_fin, d_gain, d_rpb, d_lb)


N_CHIPS = 4
N_DEV = 8
ANY = pl.BlockSpec(memory_space=pl.ANY)


def _place():
    x, y, c = lax.axis_index("x"), lax.axis_index("y"), lax.axis_index("c")
    others = []
    for j in (1, 2, 3):
        tx = (1 - x) if (j >> 1) else x
        ty = (1 - y) if (j & 1) else y
        others.append((tx, ty))
    return x, y, c, others


def _piece(ref, axis, k, half, rh, cs):
    if axis == 1:
        return ref.at[pl.ds(pl.multiple_of(half * rh, 16), rh), pl.ds(pl.multiple_of(k * cs, LANES), cs)]
    return ref.at[pl.ds(pl.multiple_of(k * 2 * rh + half * rh, 16), rh), :]


def _cast_into_full(shard, axis, place, *, name):
    r, cs = shard.shape
    full = (r, cs * N_CHIPS) if axis == 1 else (r * N_CHIPS, cs)
    tr = next(t for t in (256, 128, 64, 32, 16) if r % t == 0)
    nt = r // tr

    def body(p_ref, s_ref, o_ref):
        o_ref[...] = s_ref[...].astype(BF16)

    if axis == 1:
        omap = lambda i, p_ref: (i, p_ref[0])
    else:
        omap = lambda i, p_ref: (p_ref[0] * nt + i, 0)
    return pl.pallas_call(
        body,
        grid_spec=pltpu.PrefetchScalarGridSpec(
            num_scalar_prefetch=1, grid=(nt,),
            in_specs=[pl.BlockSpec((tr, cs), lambda i, p_ref: (i, 0))],
            out_specs=pl.BlockSpec((tr, cs), omap)),
        out_shape=jax.ShapeDtypeStruct(full, BF16),
        compiler_params=_cp("parallel"), name=name)(place, shard)


def _weight_allgather(fulls, axes, *, name):
    n = len(fulls)
    geo = []
    for f, ax in zip(fulls, axes):
        r, cs = (f.shape[0], f.shape[1] // N_CHIPS) if ax == 1 else (f.shape[0] // N_CHIPS, f.shape[1])
        geo.append((ax, r // 2, cs))

    def body(*refs):
        o_refs = refs[n:2 * n]
        send_sems, recv_sems = refs[2 * n:]
        x, y, c, others = _place()
        chip = 2 * x + y
        sib = (x, y, 1 - c)

        def rcopy(i, slot, ref, to):
            return pltpu.make_async_remote_copy(
                src_ref=ref, dst_ref=ref, send_sem=send_sems.at[6 * i + slot],
                recv_sem=recv_sems.at[6 * i + slot], device_id=to, device_id_type=MESH)

        started = []
        for i in range(n):
            ax, rh, cs = geo[i]
            mine = _piece(o_refs[i], ax, chip, c, rh, cs)
            for j, (tx, ty) in enumerate(others):
                cp = rcopy(i, j, mine, (tx, ty, c))
                cp.start()
                started.append(cp)
        for i in range(n):
            ax, rh, cs = geo[i]
            for j, (tx, ty) in enumerate(others):
                got = _piece(o_refs[i], ax, 2 * tx + ty, c, rh, cs)
                rcopy(i, j, got, (x, y, c)).wait_recv()
                cp = rcopy(i, 3 + j, got, sib)
                cp.start()
                started.append(cp)
        for i in range(n):
            ax, rh, cs = geo[i]
            for j, (tx, ty) in enumerate(others):
                got = _piece(o_refs[i], ax, 2 * tx + ty, 1 - c, rh, cs)
                rcopy(i, 3 + j, got, (x, y, c)).wait_recv()
        for cp in started:
            cp.wait_send()

    return pl.pallas_call(
        body, in_specs=[ANY] * n, out_specs=[ANY] * n,
        out_shape=[jax.ShapeDtypeStruct(f.shape, f.dtype) for f in fulls],
        input_output_aliases={i: i for i in range(n)},
        scratch_shapes=[pltpu.SemaphoreType.DMA((6 * n,)), pltpu.SemaphoreType.DMA((6 * n,))],
        name=name)(*fulls)


HBM_SPEC = pl.BlockSpec(memory_space=pltpu.HBM)
SEM_SPEC = pl.BlockSpec(memory_space=pltpu.SEMAPHORE)
SPLIT_COPY = pltpu.CompilerParams(has_side_effects=pltpu.SideEffectType.DATAFLOW_SIDE_EFFECTING)
TOKEN = jax.ShapeDtypeStruct((8, LANES), F32)


def _geo(fulls, axes):
    out = []
    for f, ax in zip(fulls, axes):
        r, cs = (f.shape[0], f.shape[1] // N_CHIPS) if ax == 1 else (f.shape[0] // N_CHIPS, f.shape[1])
        out.append((ax, r // 2, cs))
    return out


def _gather_copies(refs, geo, send_sems, recv_sems):
    x, y, c, others = _place()
    chip = 2 * x + y
    cps = []
    for i, (ax, rh, cs) in enumerate(geo):
        mine = _piece(refs[i], ax, chip, c, rh, cs)
        for j, (tx, ty) in enumerate(others):
            cps.append(pltpu.make_async_remote_copy(
                src_ref=mine, dst_ref=mine, send_sem=send_sems.at[3 * i + j],
                recv_sem=recv_sems.at[3 * i + j], device_id=(tx, ty, c), device_id_type=MESH))
    return cps


def _allgather_start(fulls, axes, after, *, name):
    n = len(fulls)
    geo = _geo(fulls, axes)

    def body(*refs):
        w_refs = refs[:n]
        send_sems, recv_sems = refs[n + 1], refs[n + 2]
        token = refs[2 * n + 3]
        for cp in _gather_copies(w_refs, geo, send_sems, recv_sems):
            cp.start()
        token[...] = jnp.zeros_like(token)

    out = pl.pallas_call(
        body, name=name,
        out_shape=(pltpu.SemaphoreType.DMA((3 * n,)), pltpu.SemaphoreType.DMA((3 * n,)),
                   *[pltpu.HBM(f.shape, f.dtype) for f in fulls], TOKEN),
        in_specs=[HBM_SPEC] * n + [ANY],
        out_specs=(SEM_SPEC, SEM_SPEC, *[HBM_SPEC] * n, pl.BlockSpec(memory_space=pltpu.VMEM)),
        input_output_aliases={i: 2 + i for i in range(n)},
        compiler_params=SPLIT_COPY,
    )(*[pltpu.with_memory_space_constraint(f, pltpu.HBM) for f in fulls], after)
    return out[0], out[1], list(out[2:2 + n]), out[2 + n]


def _allgather_wait(send_sems, recv_sems, fulls, axes, after, *, name):
    n = len(fulls)
    geo = _geo(fulls, axes)

    def body(*refs):
        w_refs = refs[:n]
        for cp in _gather_copies(w_refs, geo, refs[n], refs[n + 1]):
            cp.wait_send()
            cp.wait_recv()

    return list(pl.pallas_call(
        body, name=name,
        out_shape=[pltpu.HBM(f.shape, f.dtype) for f in fulls],
        in_specs=[HBM_SPEC] * n + [SEM_SPEC, SEM_SPEC, ANY],
        out_specs=[HBM_SPEC] * n,
        input_output_aliases={i: i for i in range(n)},
        compiler_params=SPLIT_COPY,
    )(*fulls, send_sems, recv_sems, after))


def _allgather_forward(fulls, axes, *, name):
    n = len(fulls)
    geo = _geo(fulls, axes)

    def body(*refs):
        o_refs = refs[n:2 * n]
        send_sems, recv_sems = refs[2 * n:]
        x, y, c, others = _place()

        def rcopy(i, j, half, to):
            ax, rh, cs = geo[i]
            ref = _piece(o_refs[i], ax, 2 * others[j][0] + others[j][1], half, rh, cs)
            return pltpu.make_async_remote_copy(
                src_ref=ref, dst_ref=ref, send_sem=send_sems.at[3 * i + j],
                recv_sem=recv_sems.at[3 * i + j], device_id=to, device_id_type=MESH)

        cps = [rcopy(i, j, c, (x, y, 1 - c)) for i in range(n) for j in range(3)]
        for cp in cps:
            cp.start()
        for i in range(n):
            for j in range(3):
                rcopy(i, j, 1 - c, (x, y, c)).wait_recv()
        for cp in cps:
            cp.wait_send()

    return list(pl.pallas_call(
        body, in_specs=[ANY] * n, out_specs=[ANY] * n,
        out_shape=[jax.ShapeDtypeStruct(f.shape, f.dtype) for f in fulls],
        input_output_aliases={i: i for i in range(n)},
        scratch_shapes=[pltpu.SemaphoreType.DMA((3 * n,)), pltpu.SemaphoreType.DMA((3 * n,))],
        name=name)(*fulls))


def _scatter_geo(parts, axes):
    out = []
    for p, ax in zip(parts, axes):
        _, rh, cols = p.shape
        out.append((ax, rh, cols // N_CHIPS if ax == 1 else cols))
    return out


def _scatter_copies(p_refs, q_refs, geo, send_sems, recv_sems):
    x, y, c, others = _place()
    chip = 2 * x + y
    cps = []
    for i, (ax, rh, cw) in enumerate(geo):
        for j, (tx, ty) in enumerate(others):
            k = 2 * tx + ty
            src = (p_refs[i].at[0, :, pl.ds(pl.multiple_of(k * cw, LANES), cw)] if ax == 1
                   else p_refs[i].at[k])
            cps.append(pltpu.make_async_remote_copy(
                src_ref=src, dst_ref=q_refs[i].at[chip], send_sem=send_sems.at[3 * i + j],
                recv_sem=recv_sems.at[3 * i + j], device_id=(tx, ty, c), device_id_type=MESH))
    return cps


def _scatter_start(parts, axes, *, name):
    n = len(parts)
    geo = _scatter_geo(parts, axes)
    slots = [pltpu.HBM((N_CHIPS, rh, cw), p.dtype) for p, (_, rh, cw) in zip(parts, geo)]

    def body(*refs):
        p_refs, q_refs = refs[:n], refs[n:2 * n]
        send_sems, recv_sems = refs[2 * n], refs[2 * n + 1]
        token = refs[4 * n + 2]
        for cp in _scatter_copies(p_refs, q_refs, geo, send_sems, recv_sems):
            cp.start()
        token[...] = jnp.zeros_like(token)

    land = [pltpu.with_memory_space_constraint(lax.empty(s.inner_aval.shape, s.inner_aval.dtype), pltpu.HBM)
            for s in slots]
    out = pl.pallas_call(
        body, name=name,
        out_shape=(pltpu.SemaphoreType.DMA((3 * n,)), pltpu.SemaphoreType.DMA((3 * n,)),
                   *[pltpu.HBM(p.shape, p.dtype) for p in parts], *slots, TOKEN),
        in_specs=[HBM_SPEC] * (2 * n),
        out_specs=(SEM_SPEC, SEM_SPEC, *[HBM_SPEC] * (2 * n), pl.BlockSpec(memory_space=pltpu.VMEM)),
        input_output_aliases={i: 2 + i for i in range(2 * n)},
        compiler_params=SPLIT_COPY,
    )(*[pltpu.with_memory_space_constraint(p, pltpu.HBM) for p in parts], *land)
    return out[0], out[1], list(out[2:2 + n]), list(out[2 + n:2 + 2 * n]), out[2 + 2 * n]


def _scatter_wait(send_sems, recv_sems, parts, slots, axes, after, *, name):
    n = len(parts)
    geo = _scatter_geo(parts, axes)

    def body(*refs):
        p_refs, q_refs = refs[:n], refs[n:2 * n]
        for cp in _scatter_copies(p_refs, q_refs, geo, refs[2 * n], refs[2 * n + 1]):
            cp.wait_send()
            cp.wait_recv()

    out = pl.pallas_call(
        body, name=name,
        out_shape=[pltpu.HBM(a.shape, a.dtype) for a in (*parts, *slots)],
        in_specs=[HBM_SPEC] * (2 * n) + [SEM_SPEC, SEM_SPEC, ANY],
        out_specs=[HBM_SPEC] * (2 * n),
        input_output_aliases={i: i for i in range(2 * n)},
        compiler_params=SPLIT_COPY,
    )(*parts, *slots, send_sems, recv_sems, after)
    return list(out[:n]), list(out[n:])


def _sibling_swap(grads, *, name):
    n = len(grads)
    out_shape = [jax.ShapeDtypeStruct((g.shape[0], g.shape[1] // 2, g.shape[2]), g.dtype)
                 for g in grads]

    def body(*refs):
        g_refs, o_refs = refs[:n], refs[n:2 * n]
        send_sems, recv_sems = refs[2 * n:]
        x, y, c, _ = _place()
        cps = []
        for i in range(n):
            rh = grads[i].shape[1] // 2
            src = g_refs[i].at[:, pl.ds(pl.multiple_of((1 - c) * rh, 16), rh), :]
            cp = pltpu.make_async_remote_copy(
                src_ref=src, dst_ref=o_refs[i], send_sem=send_sems.at[i], recv_sem=recv_sems.at[i],
                device_id=(x, y, 1 - c), device_id_type=MESH)
            cp.start()
            cps.append(cp)
        for cp in cps:
            cp.wait()

    return pl.pallas_call(
        body, in_specs=[ANY] * n, out_specs=[ANY] * n, out_shape=out_shape,
        scratch_shapes=[pltpu.SemaphoreType.DMA((n,)), pltpu.SemaphoreType.DMA((n,))],
        name=name)(*grads)


def _pair_add(g3, rx, c_arr, *, out_dtype, name):
    nb, rows, cols = g3.shape
    rh = rows // 2
    tr = next(t for t in (128, 64, 32, 16) if rh % t == 0)
    nt = rh // tr

    def body(c_ref, g_ref, r_ref, o_ref):
        o_ref[...] = (g_ref[...] + r_ref[...]).astype(out_dtype)

    return pl.pallas_call(
        body,
        grid_spec=pltpu.PrefetchScalarGridSpec(
            num_scalar_prefetch=1, grid=(nb, nt),
            in_specs=[pl.BlockSpec((None, tr, cols), lambda b, i, c_ref: (b, c_ref[0] * nt + i, 0)),
                      pl.BlockSpec((None, tr, cols), lambda b, i, c_ref: (b, i, 0))],
            out_specs=pl.BlockSpec((None, tr, cols), lambda b, i, c_ref: (b, i, 0))),
        out_shape=jax.ShapeDtypeStruct((nb, rh, cols), out_dtype),
        compiler_params=_cp("parallel", "parallel"), name=name)(c_arr, g3, rx)


def _chip_scatter(parts, axes, *, name):
    n = len(parts)
    geo, out_shape = [], []
    for p, ax in zip(parts, axes):
        nb, rh, cols = p.shape
        cw = cols // N_CHIPS if ax == 1 else cols
        geo.append((ax, rh, cw))
        out_shape.append(jax.ShapeDtypeStruct((N_CHIPS, rh, cw), p.dtype))

    def body(*refs):
        p_refs, o_refs = refs[:n], refs[n:2 * n]
        send_sems, recv_sems = refs[2 * n:]
        x, y, c, others = _place()
        chip = 2 * x + y

        def block(i, k):
            ax, rh, cw = geo[i]
            if ax == 1:
                return p_refs[i].at[0, :, pl.ds(pl.multiple_of(k * cw, LANES), cw)]
            return p_refs[i].at[k]

        cps = []
        for i in range(n):
            for j, (tx, ty) in enumerate(others):
                cp = pltpu.make_async_remote_copy(
                    src_ref=block(i, 2 * tx + ty), dst_ref=o_refs[i].at[chip],
                    send_sem=send_sems.at[3 * i + j], recv_sem=recv_sems.at[3 * i + j],
                    device_id=(tx, ty, c), device_id_type=MESH)
                cp.start()
                cps.append(cp)
        for i in range(n):
            for j, (tx, ty) in enumerate(others):
                got = o_refs[i].at[2 * tx + ty]
                pltpu.make_async_remote_copy(
                    src_ref=got, dst_ref=got, send_sem=send_sems.at[3 * i + j],
                    recv_sem=recv_sems.at[3 * i + j], device_id=(x, y, c),
                    device_id_type=MESH).wait_recv()
        for cp in cps:
            cp.wait_send()

    return pl.pallas_call(
        body, in_specs=[ANY] * n, out_specs=[ANY] * n, out_shape=out_shape,
        scratch_shapes=[pltpu.SemaphoreType.DMA((3 * n,)), pltpu.SemaphoreType.DMA((3 * n,))],
        name=name)(*parts)


def _sum_slots(q, *, name):
    ns, rows, cols = q.shape
    tr = next(t for t in (128, 64, 32, 16, 8) if rows % t == 0)

    def body(q_ref, o_ref):
        acc = q_ref[0].astype(F32)
        for k in range(1, ns):
            acc = acc + q_ref[k].astype(F32)
        o_ref[...] = acc

    return pl.pallas_call(
        body, grid=(rows // tr,),
        in_specs=[pl.BlockSpec((ns, tr, cols), lambda i: (0, i, 0))],
        out_specs=_rspec(tr, cols),
        out_shape=jax.ShapeDtypeStruct((rows, cols), F32),
        compiler_params=_cp("parallel"), name=name)(q)


def _sum_chips(q, p, place, axis, *, name):
    _, rh, cw = q.shape
    tr = next(t for t in (128, 64, 32, 16) if rh % t == 0)
    nt = rh // tr

    def body(p_ref, *refs):
        q_refs, own_ref, o_ref = refs[:N_CHIPS], refs[N_CHIPS], refs[N_CHIPS + 1]
        chip = p_ref[0]
        acc = jnp.where(chip == 0, own_ref[...], q_refs[0][...]).astype(F32)
        for k in range(1, N_CHIPS):
            acc = acc + jnp.where(chip == k, own_ref[...], q_refs[k][...]).astype(F32)
        o_ref[...] = acc

    def slot_spec(k):
        return pl.BlockSpec((None, tr, cw),
                            lambda i, p_ref: (jnp.where(p_ref[0] == k, (k + 1) % N_CHIPS, k), i, 0))

    if axis == 1:
        own_spec = pl.BlockSpec((None, tr, cw), lambda i, p_ref: (0, i, p_ref[0]))
    else:
        own_spec = pl.BlockSpec((None, tr, cw), lambda i, p_ref: (p_ref[0], i, 0))
    return pl.pallas_call(
        body,
        grid_spec=pltpu.PrefetchScalarGridSpec(
            num_scalar_prefetch=1, grid=(nt,),
            in_specs=[slot_spec(k) for k in range(N_CHIPS)] + [own_spec],
            out_specs=pl.BlockSpec((tr, cw), lambda i, p_ref: (p_ref[1] * nt + i, 0))),
        out_shape=jax.ShapeDtypeStruct((2 * rh, cw), F32),
        compiler_params=_cp("parallel"), name=name)(place, *([q] * N_CHIPS), p)


def _sibling_share(shards, *, name):
    n = len(shards)

    def body(*refs):
        o_refs = refs[n:2 * n]
        send_sems, recv_sems = refs[2 * n:]
        x, y, c, _ = _place()
        cps = []
        for i in range(n):
            rh = shards[i].shape[0] // 2
            mine = o_refs[i].at[pl.ds(pl.multiple_of(c * rh, 8), rh), :]
            cp = pltpu.make_async_remote_copy(
                src_ref=mine, dst_ref=mine, send_sem=send_sems.at[i], recv_sem=recv_sems.at[i],
                device_id=(x, y, 1 - c), device_id_type=MESH)
            cp.start()
            cps.append(cp)
        for i in range(n):
            rh = shards[i].shape[0] // 2
            theirs = o_refs[i].at[pl.ds(pl.multiple_of((1 - c) * rh, 8), rh), :]
            pltpu.make_async_remote_copy(
                src_ref=theirs, dst_ref=theirs, send_sem=send_sems.at[i], recv_sem=recv_sems.at[i],
                device_id=(x, y, c), device_id_type=MESH).wait_recv()
        for cp in cps:
            cp.wait_send()

    return pl.pallas_call(
        body, in_specs=[ANY] * n, out_specs=[ANY] * n,
        out_shape=[jax.ShapeDtypeStruct(h.shape, h.dtype) for h in shards],
        input_output_aliases={i: i for i in range(n)},
        scratch_shapes=[pltpu.SemaphoreType.DMA((n,)), pltpu.SemaphoreType.DMA((n,))],
        name=name)(*shards)


def _gather_all(blk, *, name, after=None):
    rows, cols = blk.shape
    extra = [] if after is None else [after]

    def body(x_ref, *refs):
        out_ref, send_sems, recv_sems, local_sem = refs[len(extra):]
        x, y, c = lax.axis_index("x"), lax.axis_index("y"), lax.axis_index("c")
        me = 4 * x + 2 * y + c
        mine = pltpu.make_async_copy(x_ref, out_ref.at[me], local_sem)
        mine.start()
        cps = []
        for k in range(1, N_DEV):
            tx = (1 - x) if (k >> 2) & 1 else x
            ty = (1 - y) if (k >> 1) & 1 else y
            tc = (1 - c) if k & 1 else c
            cp = pltpu.make_async_remote_copy(
                src_ref=x_ref, dst_ref=out_ref.at[me], send_sem=send_sems.at[k - 1],
                recv_sem=recv_sems.at[k - 1], device_id=(tx, ty, tc), device_id_type=MESH)
            cp.start()
            cps.append(cp)
        for k in range(1, N_DEV):
            tx = (1 - x) if (k >> 2) & 1 else x
            ty = (1 - y) if (k >> 1) & 1 else y
            tc = (1 - c) if k & 1 else c
            got = out_ref.at[4 * tx + 2 * ty + tc]
            pltpu.make_async_remote_copy(
                src_ref=got, dst_ref=got, send_sem=send_sems.at[k - 1], recv_sem=recv_sems.at[k - 1],
                device_id=(x, y, c), device_id_type=MESH).wait_recv()
        for cp in cps:
            cp.wait_send()
        mine.wait()

    vm = pl.BlockSpec(memory_space=pltpu.VMEM)
    return pl.pallas_call(
        body, in_specs=[vm] + [ANY] * len(extra), out_specs=vm,
        out_shape=jax.ShapeDtypeStruct((N_DEV, rows, cols), blk.dtype),
        scratch_shapes=[pltpu.SemaphoreType.DMA((N_DEV - 1,)), pltpu.SemaphoreType.DMA((N_DEV - 1,)),
                        pltpu.SemaphoreType.DMA],
        name=name)(blk, *extra)


def _as_rows(a):
    flat = a.reshape(-1)
    n = flat.shape[0]
    rows = -(-n // (8 * LANES)) * 8
    return jnp.pad(flat, (0, rows * LANES - n)).reshape(rows, LANES)


def _from_rows(p, shape):
    n = int(np.prod(shape))
    return p.reshape(-1)[:n].reshape(shape)


WEIGHT_AXES = (1, 1, 1, 0, 1, 0)
WIRE = BF16


def kernel(x, meta_tokens, w_in, w_na_out, w_hg_out, w_o, w_up, w_down, norm_mix, norm_mlp, norm_final, hg_norm, na_rpb, hg_lb_logits, loss_target, m_meta_tokens, m_w_in, m_w_na_out, m_w_hg_out, m_w_o, m_w_up, m_w_down, m_norm_mix, m_norm_mlp, m_norm_final, m_hg_norm, m_na_rpb, m_hg_lb_logits, v_meta_tokens, v_w_in, v_w_na_out, v_w_hg_out, v_w_o, v_w_up, v_w_down, v_norm_mix, v_norm_mlp, v_norm_final, v_hg_norm, v_na_rpb, v_hg_lb_logits):
    xi, yi, ci = lax.axis_index("x"), lax.axis_index("y"), lax.axis_index("c")
    chip = 2 * xi + yi
    d = x.shape[-1]
    dshard = meta_tokens.shape[1]
    hgw = hg_norm.shape[1]
    lbs = hg_lb_logits.shape[2]
    big = [w_in[0], w_na_out[0], w_hg_out[0], w_o[0], w_up[0], w_down[0]]
    big_m = [m_w_in[0], m_w_na_out[0], m_w_hg_out[0], m_w_o[0], m_w_up[0], m_w_down[0]]
    big_v = [v_w_in[0], v_w_na_out[0], v_w_hg_out[0], v_w_o[0], v_w_up[0], v_w_down[0]]

    place = jnp.stack([chip, ci]).astype(jnp.int32)
    own_w = [_cast_into_full(w, ax, place, name=f"cast_shard_{i}")
             for i, (w, ax) in enumerate(zip(big, WEIGHT_AXES))]
    in_axes, rest_axes = WEIGHT_AXES[:1], WEIGHT_AXES[1:]
    small_in = jnp.concatenate([_as_rows(meta_tokens), _as_rows(hg_lb_logits)], axis=0)
    small_all = _gather_all(small_in, name="gather_small_params")[0::2]
    in_send, in_recv, in_bufs, in_token = _allgather_start(own_w[:1], in_axes, small_all,
                                                           name="weight_allgather_in_start")
    ag_send, ag_recv, ag_bufs, ag_token = _allgather_start(own_w[1:], rest_axes, in_token,
                                                           name="weight_allgather_rest_start")

    def first_weight(after):
        got = _allgather_wait(in_send, in_recv, in_bufs, in_axes, after,
                              name="weight_allgather_in_wait")
        return _allgather_forward(got, in_axes, name="weight_allgather_in_forward")[0]

    def rest_weights(after):
        got = _allgather_wait(ag_send, ag_recv, ag_bufs, rest_axes, after,
                              name="weight_allgather_rest_wait")
        return _allgather_forward(got, rest_axes, name="weight_allgather_rest_forward")

    n_meta_rows = N_META * dshard // LANES
    meta_full = (small_all[:, :n_meta_rows].reshape(N_CHIPS, N_META, dshard)
                 .transpose(1, 0, 2).reshape(N_META, d))
    lbl_full = (small_all[:, n_meta_rows:].reshape(N_CHIPS, -1)[:, :4 * lbs]
                .reshape(N_CHIPS, 2, 2, lbs).transpose(1, 2, 0, 3).reshape(2, 2, N_CHIPS * lbs))
    lb = jax.nn.softmax(lbl_full, axis=1)[:, 0]

    c_arr = ci.reshape(1).astype(jnp.int32)

    def chip_partials(dws, axes, tag):
        g3 = [g.reshape(1, *g.shape) if ax == 1
              else g.reshape(N_CHIPS, g.shape[0] // N_CHIPS, g.shape[1]) for g, ax in zip(dws, axes)]
        rx = _sibling_swap(g3, name=f"grad_sibling_swap_{tag}")
        return [_pair_add(g, r, c_arr, out_dtype=WIRE, name=f"grad_pair_add_{tag}_{i}")
                for i, (g, r) in enumerate(zip(g3, rx))]

    flying = {}

    def scatter_behind(tag, axes):
        def hook(dws):
            send, recv, parts, slots, token = _scatter_start(
                chip_partials(dws, axes, tag), axes, name=f"grad_scatter_{tag}_start")
            flying[tag] = (send, recv, parts, slots)
            return token
        return hook

    def landed(tag, axes, after):
        parts, slots = _scatter_wait(*flying[tag], axes, after, name=f"grad_scatter_{tag}_wait")
        return [_sum_chips(q, p, place, ax, name=f"grad_sum_chips_{tag}_{i}")
                for i, (q, p, ax) in enumerate(zip(slots, parts, axes))]

    (loss, dx, dmeta, *_, dg_mix, dg_mlp, dg_fin, d_gain, d_rpb, d_lb) = _local_step(
        x[0], loss_target[0], meta_full, first_weight, rest_weights,
        norm_mix + ag_token[0:1, 0:1], norm_mlp, norm_final.reshape(1, d), hg_norm, na_rpb[0], lb,
        scatter_behind("rest", rest_axes), lambda dw_in: scatter_behind("in", in_axes)([dw_in]))

    halves_rest = landed("rest", rest_axes, dx)
    halves_in = landed("in", in_axes, halves_rest[-1])
    g_big = _sibling_share(halves_in + halves_rest, name="grad_sibling_share")

    d_rpb_c = d_rpb[:, :2 * NA_WIN_W - 1]
    small_g = [dmeta, dg_mix, dg_mlp, dg_fin, d_gain, d_rpb_c, d_lb, loss]
    packed = jnp.concatenate([_as_rows(a) for a in small_g], axis=0)
    total = _sum_slots(_gather_all(packed, after=halves_in[0], name="gather_small_grads"),
                       name="sum_small_grads")
    offs = np.cumsum([0] + [_as_rows(a).shape[0] for a in small_g])
    take = lambda i, shape: _from_rows(total[offs[i]:offs[i + 1]], shape)
    g_meta_full = take(0, (N_META, d))
    g_norm_mix, g_norm_mlp = take(1, (1, d)), take(2, (1, d))
    g_norm_final = take(3, (d,))
    g_hg_norm = take(4, (1, hgw))
    g_rpb = take(5, na_rpb.shape)
    g_lb = take(6, (2, hgw))
    loss_total = take(7, (1, LANES))[0, 0]
    g_meta = lax.dynamic_slice_in_dim(g_meta_full, chip * dshard, dshard, axis=1)
    dl0 = lb * (1.0 - lb) * g_lb
    g_lbl_full = jnp.stack([dl0, -dl0], axis=1)
    g_lbl = lax.dynamic_slice_in_dim(g_lbl_full, chip * lbs, lbs, axis=2)

    big_out = [_adamw(w, g, m, v, name=f"adamw_{i}")
               for i, (w, g, m, v) in enumerate(zip(big, g_big, big_m, big_v))]
    small_w = [meta_tokens, norm_mix, norm_mlp, norm_final, hg_norm, na_rpb, hg_lb_logits]
    small_gr = [g_meta, g_norm_mix, g_norm_mlp, g_norm_final, g_hg_norm, g_rpb, g_lbl]
    small_m = [m_meta_tokens, m_norm_mix, m_norm_mlp, m_norm_final, m_hg_norm, m_na_rpb, m_hg_lb_logits]
    small_v = [v_meta_tokens, v_norm_mix, v_norm_mlp, v_norm_final, v_hg_norm, v_na_rpb, v_hg_lb_logits]
    pk = lambda lst: jnp.concatenate([_as_rows(a) for a in lst], axis=0)
    sd, sm, sv = _adamw(pk(small_w), pk(small_gr), pk(small_m), pk(small_v), name="adamw_small")
    soffs = np.cumsum([0] + [_as_rows(a).shape[0] for a in small_w])
    unpk = lambda p: [_from_rows(p[soffs[i]:soffs[i + 1]], small_w[i].shape) for i in range(len(small_w))]
    sd, sm, sv = unpk(sd), unpk(sm), unpk(sv)

    def order(bigs, smalls):
        return [smalls[0]] + [b.reshape(1, *b.shape) for b in bigs] + smalls[1:]

    grads = order(g_big, small_gr)
    deltas = order([o[0] for o in big_out], sd)
    new_m = order([o[1] for o in big_out], sm)
    new_v = order([o[2] for o in big_out], sv)
    return (loss_total, dx.reshape(1, *dx.shape), *grads, *deltas, *new_m, *new_v)
```

```python
import functools

import numpy as np
import jax
import jax.numpy as jnp
from jax import lax
from jax.experimental import pallas as pl
from jax.experimental.pallas import tpu as pltpu

F32 = jnp.float32
BF16 = jnp.bfloat16
HIGHEST = lax.Precision.HIGHEST

GRID_W = 64
N_META = 16
EPS = 1e-6
NA_HEAD_DIM = 64
NA_WIN_H = 8
NA_WIN_W = 16
HG_DK = 128
HG_CHUNK = 16
LANES = 128
ROW_ALIGN = 128
VMEM_LIMIT = 48 * 1024 * 1024

ADAM_LR = 0.001
ADAM_B1 = 0.9
ADAM_B2 = 0.999
ADAM_EPS = 1e-08
ADAM_WD = 0.01
ADAM_STEP = 10

MESH = pl.DeviceIdType.MESH


def _cp(*sem):
    return pltpu.CompilerParams(dimension_semantics=sem, vmem_limit_bytes=VMEM_LIMIT)


def _sigmoid(x):
    return 1.0 / (1.0 + jnp.exp(-x))


def _dot(a, b, dims, precision=None):
    return lax.dot_general(a, b, (dims, ((), ())), preferred_element_type=F32, precision=precision)


def _nn(a, b, **kw):
    return _dot(a, b, ((1,), (0,)), **kw)


def _nt(a, b, **kw):
    return _dot(a, b, ((1,), (1,)), **kw)


def _tn(a, b, **kw):
    return _dot(a, b, ((0,), (0,)), **kw)


def _matmul(a, b, *, ta=False, tb=False, tm=None, tn=None, tk=None, out_dtype=F32, name,
            precision=None, after=None):
    extra = [] if after is None else [after]
    if ta:
        kdim, m = a.shape
    else:
        m, kdim = a.shape
    if tb:
        n, k2 = b.shape
    else:
        k2, n = b.shape
    assert kdim == k2, (a.shape, b.shape, ta, tb)
    if tm is None:
        if ta:
            tm = next(t for t in (512, 256, 128, m) if m % t == 0)
        else:
            tm = m // 2 if (m // 2) % 16 == 0 and m > 512 else m
    if tn is None:
        tn = next(t for t in (512, 256, 128, n) if n % t == 0)
    if tk is None:
        tk = kdim if ta else next(t for t in (1024, 512, 256, 128, kdim) if kdim % t == 0)
    assert m % tm == 0 and n % tn == 0 and kdim % tk == 0, (m, n, kdim, tm, tn, tk)
    nk = kdim // tk
    op_dtype = F32 if precision is not None else BF16

    def body(a_ref, b_ref, *refs):
        o_ref, acc_ref = refs[len(extra):]
        kk = pl.program_id(2)

        @pl.when(kk == 0)
        def _():
            acc_ref[...] = jnp.zeros_like(acc_ref)

        av = a_ref[...].astype(op_dtype)
        bv = b_ref[...].astype(op_dtype)
        dims = ((0 if ta else 1,), (1 if tb else 0,))
        acc_ref[...] += _dot(av, bv, dims, precision=precision)

        @pl.when(kk == nk - 1)
        def _():
            o_ref[...] = acc_ref[...].astype(out_dtype)

    a_spec = (pl.BlockSpec((tk, tm), lambda i, j, k: (k, i)) if ta
              else pl.BlockSpec((tm, tk), lambda i, j, k: (i, k)))
    b_spec = (pl.BlockSpec((tn, tk), lambda i, j, k: (j, k)) if tb
              else pl.BlockSpec((tk, tn), lambda i, j, k: (k, j)))
    return pl.pallas_call(
        body,
        grid=(m // tm, n // tn, nk),
        in_specs=[a_spec, b_spec] + [pl.BlockSpec(memory_space=pl.ANY)] * len(extra),
        out_specs=pl.BlockSpec((tm, tn), lambda i, j, k: (i, j)),
        out_shape=jax.ShapeDtypeStruct((m, n), out_dtype),
        scratch_shapes=[pltpu.VMEM((tm, tn), F32)],
        compiler_params=_cp("parallel", "parallel", "arbitrary"),
        name=name,
    )(a, b, *extra)


def _rspec(tr, w, cb=0):
    return pl.BlockSpec((tr, w), lambda i: (i, cb))


def _fspec(shape):
    nd = len(shape)
    return pl.BlockSpec(shape, lambda i: (0,) * nd)


def _row_tile(lp):
    return ROW_ALIGN if lp % ROW_ALIGN == 0 else lp


def _rmsnorm_fwd(x, g, *, name):
    lp, d = x.shape
    tr = _row_tile(lp)

    def body(x_ref, g_ref, o_ref):
        xv = x_ref[...]
        r = lax.rsqrt(jnp.mean(xv * xv, axis=-1, keepdims=True) + EPS)
        o_ref[...] = (xv * r * g_ref[...]).astype(BF16)

    return pl.pallas_call(
        body, grid=(lp // tr,),
        in_specs=[_rspec(tr, d), _fspec((1, d))],
        out_specs=_rspec(tr, d),
        out_shape=jax.ShapeDtypeStruct((lp, d), BF16),
        compiler_params=_cp("parallel"), name=name)(x, g)


def _residual_norm(h, t, g, *, name):
    lp, d = h.shape
    tr = _row_tile(lp)

    def body(h_ref, t_ref, g_ref, h1_ref, m_ref):
        xv = h_ref[...] + t_ref[...]
        h1_ref[...] = xv
        r = lax.rsqrt(jnp.mean(xv * xv, axis=-1, keepdims=True) + EPS)
        m_ref[...] = (xv * r * g_ref[...]).astype(BF16)

    return pl.pallas_call(
        body, grid=(lp // tr,),
        in_specs=[_rspec(tr, d), _rspec(tr, d), _fspec((1, d))],
        out_specs=[_rspec(tr, d), _rspec(tr, d)],
        out_shape=[jax.ShapeDtypeStruct((lp, d), F32), jax.ShapeDtypeStruct((lp, d), BF16)],
        compiler_params=_cp("parallel"), name=name)(h, t, g)


def _rmsnorm_bwd_add(x, g, dy, dres, *, name):
    lp, d = x.shape
    tr = _row_tile(lp)

    def body(x_ref, g_ref, dy_ref, dr_ref, dx_ref, dg_ref):
        @pl.when(pl.program_id(0) == 0)
        def _():
            dg_ref[...] = jnp.zeros_like(dg_ref)

        xv = x_ref[...]
        r = lax.rsqrt(jnp.mean(xv * xv, axis=-1, keepdims=True) + EPS)
        xh = xv * r
        dyv = dy_ref[...]
        dg_ref[...] += jnp.sum(dyv * xh, axis=0, keepdims=True)
        dxh = dyv * g_ref[...]
        dx_ref[...] = dr_ref[...] + r * (dxh - xh * jnp.mean(dxh * xh, axis=-1, keepdims=True))

    return pl.pallas_call(
        body, grid=(lp // tr,),
        in_specs=[_rspec(tr, d), _fspec((1, d)), _rspec(tr, d), _rspec(tr, d)],
        out_specs=[_rspec(tr, d), _fspec((1, d))],
        out_shape=[jax.ShapeDtypeStruct((lp, d), F32), jax.ShapeDtypeStruct((1, d), F32)],
        compiler_params=_cp("arbitrary"), name=name)(x, g, dy, dres)


def _final_loss(h1, t2, g, tgt, *, n_tok, name):
    lp, d = h1.shape
    tr = _row_tile(lp)

    def body(h_ref, t_ref, g_ref, tg_ref, dh_ref, loss_ref, dg_ref):
        i = pl.program_id(0)

        @pl.when(i == 0)
        def _():
            loss_ref[...] = jnp.zeros_like(loss_ref)
            dg_ref[...] = jnp.zeros_like(dg_ref)

        xv = h_ref[...] + t_ref[...]
        r = lax.rsqrt(jnp.mean(xv * xv, axis=-1, keepdims=True) + EPS)
        xh = xv * r
        gv = g_ref[...]
        row = i * tr + lax.broadcasted_iota(jnp.int32, (tr, 1), 0)
        valid = (row >= N_META) & (row < N_META + n_tok)
        err = jnp.where(valid, xh * gv - tg_ref[...], 0.0)
        loss_ref[...] += jnp.sum(0.5 * err * err) / d
        dy = err / d
        dg_ref[...] += jnp.sum(dy * xh, axis=0, keepdims=True)
        dxh = dy * gv
        dh_ref[...] = r * (dxh - xh * jnp.mean(dxh * xh, axis=-1, keepdims=True))

    return pl.pallas_call(
        body, grid=(lp // tr,),
        in_specs=[_rspec(tr, d), _rspec(tr, d), _fspec((1, d)), _rspec(tr, d)],
        out_specs=[_rspec(tr, d), _fspec((1, LANES)), _fspec((1, d))],
        out_shape=[jax.ShapeDtypeStruct((lp, d), F32), jax.ShapeDtypeStruct((1, LANES), F32),
                   jax.ShapeDtypeStruct((1, d), F32)],
        compiler_params=_cp("arbitrary"), name=name)(h1, t2, g, tgt)


def _relu2(u, *, name):
    lp, f = u.shape
    tr = _row_tile(lp)

    def body(u_ref, o_ref):
        rv = jnp.maximum(u_ref[...], 0.0)
        o_ref[...] = (rv * rv).astype(BF16)

    return pl.pallas_call(
        body, grid=(lp // tr,), in_specs=[_rspec(tr, f)], out_specs=_rspec(tr, f),
        out_shape=jax.ShapeDtypeStruct((lp, f), BF16),
        compiler_params=_cp("parallel"), name=name)(u)


def _relu2_bwd(dact, u, *, name):
    lp, f = u.shape
    tr = _row_tile(lp)

    def body(d_ref, u_ref, o_ref):
        o_ref[...] = (d_ref[...] * 2.0 * jnp.maximum(u_ref[...], 0.0)).astype(BF16)

    return pl.pallas_call(
        body, grid=(lp // tr,), in_specs=[_rspec(tr, f), _rspec(tr, f)], out_specs=_rspec(tr, f),
        out_shape=jax.ShapeDtypeStruct((lp, f), BF16),
        compiler_params=_cp("parallel"), name=name)(dact, u)


def _gate_mix(proj, y_na, y_hg, *, col_gate, name):
    lp, d = y_na.shape
    tr = _row_tile(lp)
    cb = col_gate // d

    def body(gn_ref, gh_ref, yn_ref, yh_ref, o_ref):
        o_ref[...] = (_sigmoid(gn_ref[...]) * yn_ref[...]
                      + _sigmoid(gh_ref[...]) * yh_ref[...]).astype(BF16)

    return pl.pallas_call(
        body, grid=(lp // tr,),
        in_specs=[_rspec(tr, d, cb), _rspec(tr, d, cb + 1), _rspec(tr, d), _rspec(tr, d)],
        out_specs=_rspec(tr, d),
        out_shape=jax.ShapeDtypeStruct((lp, d), BF16),
        compiler_params=_cp("parallel"), name=name)(proj, proj, y_na, y_hg)


def _gate_mix_bwd(proj, y_na, y_hg, dmix, *, col_gate, name):
    lp, d = y_na.shape
    tr = _row_tile(lp)
    cb = col_gate // d

    def body(gn_ref, gh_ref, yn_ref, yh_ref, dm_ref, dyn_ref, dyh_ref, dgn_ref, dgh_ref):
        dm = dm_ref[...]
        sn = _sigmoid(gn_ref[...])
        sh = _sigmoid(gh_ref[...])
        dyn_ref[...] = (dm * sn).astype(BF16)
        dyh_ref[...] = (dm * sh).astype(BF16)
        dgn_ref[...] = (dm * yn_ref[...] * sn * (1.0 - sn)).astype(BF16)
        dgh_ref[...] = (dm * yh_ref[...] * sh * (1.0 - sh)).astype(BF16)

    sds = jax.ShapeDtypeStruct((lp, d), BF16)
    return pl.pallas_call(
        body, grid=(lp // tr,),
        in_specs=[_rspec(tr, d, cb), _rspec(tr, d, cb + 1), _rspec(tr, d), _rspec(tr, d),
                  _rspec(tr, d)],
        out_specs=[_rspec(tr, d)] * 4, out_shape=[sds] * 4,
        compiler_params=_cp("parallel"), name=name)(proj, proj, y_na, y_hg, dmix)


def _hg_out(o_f, o_b, proj, gain, *, col_g, name):
    lp, w = o_f.shape
    tr = _row_tile(lp)
    hh = w // HG_DK

    def body(of_ref, ob_ref, g_ref, gain_ref, y_ref):
        gv = g_ref[...]
        sg = gv * _sigmoid(gv)
        for h in range(hh):
            sl = slice(h * HG_DK, (h + 1) * HG_DK)
            o = of_ref[:, sl] + ob_ref[:, sl]
            r = lax.rsqrt(jnp.mean(o * o, axis=-1, keepdims=True) + EPS)
            y_ref[:, sl] = (o * r * gain_ref[:, sl] * sg[:, sl]).astype(BF16)

    return pl.pallas_call(
        body, grid=(lp // tr,),
        in_specs=[_rspec(tr, w), _rspec(tr, w), _rspec(tr, w, col_g // w), _fspec((1, w))],
        out_specs=_rspec(tr, w),
        out_shape=jax.ShapeDtypeStruct((lp, w), BF16),
        compiler_params=_cp("parallel"), name=name)(o_f, o_b, proj, gain)


def _hg_out_bwd(o_f, o_b, proj, gain, dy, *, col_g, name):
    lp, w = o_f.shape
    tr = _row_tile(lp)
    hh = w // HG_DK

    def body(of_ref, ob_ref, g_ref, gain_ref, dy_ref, do_ref, dg_ref, dgain_ref):
        @pl.when(pl.program_id(0) == 0)
        def _():
            dgain_ref[...] = jnp.zeros_like(dgain_ref)

        for h in range(hh):
            sl = slice(h * HG_DK, (h + 1) * HG_DK)
            gv = g_ref[:, sl]
            s = _sigmoid(gv)
            sg = gv * s
            dsg = s + gv * s * (1.0 - s)
            o = of_ref[:, sl] + ob_ref[:, sl]
            r = lax.rsqrt(jnp.mean(o * o, axis=-1, keepdims=True) + EPS)
            on = o * r
            dyv = dy_ref[:, sl]
            gn = gain_ref[:, sl]
            dgain_ref[:, sl] += jnp.sum(dyv * on * sg, axis=0, keepdims=True)
            dg_ref[:, sl] = (dyv * on * gn * dsg).astype(BF16)
            don = dyv * gn * sg
            do_ref[:, sl] = r * (don - on * jnp.mean(don * on, axis=-1, keepdims=True))

    return pl.pallas_call(
        body, grid=(lp // tr,),
        in_specs=[_rspec(tr, w), _rspec(tr, w), _rspec(tr, w, col_g // w), _fspec((1, w)),
                  _rspec(tr, w)],
        out_specs=[_rspec(tr, w), _rspec(tr, w), _fspec((1, w))],
        out_shape=[jax.ShapeDtypeStruct((lp, w), F32), jax.ShapeDtypeStruct((lp, w), BF16),
                   jax.ShapeDtypeStruct((1, w), F32)],
        compiler_params=_cp("arbitrary"), name=name)(o_f, o_b, proj, gain, dy)


HG_GROUP = 8
HG_ROWS = HG_GROUP * HG_CHUNK


def _hg_gates(zq, z, lbv):
    qh = zq * _sigmoid(zq)
    s = _sigmoid(z)
    f = lbv + (1.0 - lbv) * s
    kk = (1.0 - lbv) * _sigmoid(-z)
    return qh, s, f, jnp.log(f), kk


def _tri_blocks(upper):
    r = lax.broadcasted_iota(jnp.int32, (HG_ROWS, HG_ROWS), 0)
    c = lax.broadcasted_iota(jnp.int32, (HG_ROWS, HG_ROWS), 1)
    same = (r // HG_CHUNK) == (c // HG_CHUNK)
    return jnp.where(same & ((c >= r) if upper else (c <= r)), 1.0, 0.0).astype(F32)


def _g3(a):
    return a.reshape(HG_GROUP, HG_CHUNK, a.shape[-1])


def _hg_scan_fwd(proj, lb, *, reverse, col_q, col_z, col_i, hh, name):
    lp = proj.shape[0]
    n_groups = lp // HG_ROWS
    n_chunks = lp // HG_CHUNK
    c16 = HG_CHUNK
    last = 0 if reverse else c16 - 1

    def body(q_ref, z_ref, i_ref, lb_ref, o_ref, st_ref):
        lbv = lb_ref[...]
        tri = _tri_blocks(reverse)
        row = lax.broadcasted_iota(jnp.int32, (1, c16, HG_DK), 1)

        def group(gi, st):
            gg = (n_groups - 1 - gi) if reverse else gi
            r0 = pl.multiple_of(gg * HG_ROWS, HG_ROWS)
            v = i_ref[pl.ds(r0, HG_ROWS), :]
            qh, _, _, g, kk = _hg_gates(q_ref[pl.ds(r0, HG_ROWS), :], z_ref[pl.ds(r0, HG_ROWS), :],
                                        lbv)
            b = _nn(tri, g, precision=HIGHEST)
            b3, qh3, kk3, v3 = _g3(b), _g3(qh), _g3(kk), _g3(v)
            bl3 = b3[:, last:last + 1, :]
            qe = (qh * jnp.exp(b)).astype(BF16)
            kd = (kk3 * jnp.exp(bl3 - b3)).reshape(HG_ROWS, HG_DK).astype(BF16)
            decay3 = jnp.exp(bl3)
            v16 = v.astype(BF16)
            o3 = jnp.zeros((HG_GROUP, c16, HG_DK), F32)
            for t in range(c16):
                mask = (row >= t) if reverse else (row <= t)
                e = jnp.where(mask, jnp.exp(jnp.minimum(b3[:, t:t + 1, :] - b3, 0.0)), 0.0)
                a_col = jnp.sum(qh3[:, t:t + 1, :] * kk3 * e, axis=2, keepdims=True)
                o_t = jnp.sum(a_col * v3, axis=1, keepdims=True)
                o3 = o3 + jnp.where(row == t, o_t, 0.0)
            inter = [None] * HG_GROUP
            for c in (reversed(range(HG_GROUP)) if reverse else range(HG_GROUP)):
                sl = slice(c * c16, (c + 1) * c16)
                st_ref[gg * HG_GROUP + c] = st
                inter[c] = _nt(qe[sl], st.astype(BF16))
                st = decay3[c] * st + _tn(v16[sl], kd[sl])
            o_ref[pl.ds(r0, HG_ROWS), :] = (o3.reshape(HG_ROWS, HG_DK)
                                            + jnp.concatenate(inter, axis=0))
            return st

        lax.fori_loop(0, n_groups, group, jnp.zeros((HG_DK, HG_DK), F32))

    cspec = lambda col: pl.BlockSpec((lp, HG_DK), lambda h: (0, col // HG_DK + h))
    return pl.pallas_call(
        body, grid=(hh,),
        in_specs=[cspec(col_q), cspec(col_z), cspec(col_i),
                  pl.BlockSpec((None, 1, HG_DK), lambda h: (h, 0, 0))],
        out_specs=[pl.BlockSpec((lp, HG_DK), lambda h: (0, h)),
                   pl.BlockSpec((None, n_chunks, HG_DK, HG_DK), lambda h: (h, 0, 0, 0))],
        out_shape=[jax.ShapeDtypeStruct((lp, hh * HG_DK), F32),
                   jax.ShapeDtypeStruct((hh, n_chunks, HG_DK, HG_DK), F32)],
        compiler_params=_cp("parallel"), name=name)(proj, proj, proj, lb)


def _hg_scan_bwd(proj, lb, states, do, *, reverse, col_q, col_z, col_i, hh, name):
    lp = proj.shape[0]
    n_groups = lp // HG_ROWS
    n_chunks = lp // HG_CHUNK
    c16 = HG_CHUNK
    last = 0 if reverse else c16 - 1

    def body(q_ref, z_ref, i_ref, lb_ref, st_ref, do_ref, dq_ref, dz_ref, dv_ref, dlb_ref):
        lbv = lb_ref[...]
        tri = _tri_blocks(reverse)
        tri_t = _tri_blocks(not reverse)
        row = lax.broadcasted_iota(jnp.int32, (1, c16, HG_DK), 1)

        def group(gi, carry):
            dst, dlb = carry
            gg = gi if reverse else (n_groups - 1 - gi)
            r0 = pl.multiple_of(gg * HG_ROWS, HG_ROWS)
            zq = q_ref[pl.ds(r0, HG_ROWS), :]
            v = i_ref[pl.ds(r0, HG_ROWS), :]
            dov = do_ref[pl.ds(r0, HG_ROWS), :]
            qh, s, f, g, kk = _hg_gates(zq, z_ref[pl.ds(r0, HG_ROWS), :], lbv)
            b = _nn(tri, g, precision=HIGHEST)
            b3, qh3, kk3, do3 = _g3(b), _g3(qh), _g3(kk), _g3(dov)
            bl3 = b3[:, last:last + 1, :]
            eb = jnp.exp(b)
            ebl3 = jnp.exp(bl3 - b3)
            decay3 = jnp.exp(bl3)
            qe16 = (qh * eb).astype(BF16)
            kd16 = (kk3 * ebl3).reshape(HG_ROWS, HG_DK).astype(BF16)
            v16, do16 = v.astype(BF16), dov.astype(BF16)
            do_s, da_t, v_ds, dv_st, dbl_st = ([None] * HG_GROUP for _ in range(5))
            for c in (range(HG_GROUP) if reverse else reversed(range(HG_GROUP))):
                sl = slice(c * c16, (c + 1) * c16)
                st = st_ref[gg * HG_GROUP + c]
                st16, dst16 = st.astype(BF16), dst.astype(BF16)
                do_s[c] = _nn(do16[sl], st16)
                da_t[c] = _nt(v16[sl], do16[sl])
                v_ds[c] = _nn(v16[sl], dst16)
                dv_st[c] = _nt(kd16[sl], dst16)
                dbl_st[c] = decay3[c] * jnp.sum(st * dst, axis=0, keepdims=True)
                dst = decay3[c] * dst + _tn(do16[sl], qe16[sl])
            cat3 = lambda parts: _g3(jnp.concatenate(parts, axis=0))
            dq3 = _g3(eb) * cat3(do_s)
            dk_state3 = ebl3 * cat3(v_ds)
            dk3 = dk_state3
            dv3 = cat3(dv_st)
            da3 = cat3(da_t)
            for t in range(c16):
                mask = (row >= t) if reverse else (row <= t)
                e = jnp.where(mask, jnp.exp(jnp.minimum(b3[:, t:t + 1, :] - b3, 0.0)), 0.0)
                qt = qh3[:, t:t + 1, :]
                a_col = jnp.sum(qt * kk3 * e, axis=2, keepdims=True)
                ce = da3[:, :, t:t + 1] * e
                dq_t = jnp.sum(ce * kk3, axis=1, keepdims=True)
                dq3 = dq3 + jnp.where(row == t, dq_t, 0.0)
                dk3 = dk3 + ce * qt
                dv3 = dv3 + a_col * do3[:, t:t + 1, :]
            dbl3 = (jnp.concatenate([d[None] for d in dbl_st], axis=0)
                    + jnp.sum(kk3 * dk_state3, axis=1, keepdims=True))
            db3 = qh3 * dq3 - kk3 * dk3 + jnp.where(row == last, dbl3, 0.0)
            dg = _nn(tri_t, db3.reshape(HG_ROWS, HG_DK), precision=HIGHEST)
            df = dg / f - dk3.reshape(HG_ROWS, HG_DK)
            sq = _sigmoid(zq)
            dq_ref[pl.ds(r0, HG_ROWS), :] = (dq3.reshape(HG_ROWS, HG_DK)
                                             * (sq + zq * sq * (1.0 - sq)))
            dz_ref[pl.ds(r0, HG_ROWS), :] = df * (1.0 - lbv) * s * (1.0 - s)
            dv_ref[pl.ds(r0, HG_ROWS), :] = dv3.reshape(HG_ROWS, HG_DK)
            return dst, dlb + jnp.sum(df * (1.0 - s), axis=0, keepdims=True)

        _, dlb = lax.fori_loop(0, n_groups, group,
                               (jnp.zeros((HG_DK, HG_DK), F32), jnp.zeros((1, HG_DK), F32)))
        dlb_ref[...] = dlb

    cspec = lambda col: pl.BlockSpec((lp, HG_DK), lambda h: (0, col // HG_DK + h))
    ospec = pl.BlockSpec((lp, HG_DK), lambda h: (0, h))
    sds = jax.ShapeDtypeStruct((lp, hh * HG_DK), F32)
    return pl.pallas_call(
        body, grid=(hh,),
        in_specs=[cspec(col_q), cspec(col_z), cspec(col_i),
                  pl.BlockSpec((None, 1, HG_DK), lambda h: (h, 0, 0)),
                  pl.BlockSpec((None, n_chunks, HG_DK, HG_DK), lambda h: (h, 0, 0, 0)),
                  ospec],
        out_specs=[ospec, ospec, ospec, pl.BlockSpec((None, 1, HG_DK), lambda h: (h, 0, 0))],
        out_shape=[sds, sds, sds, jax.ShapeDtypeStruct((hh, 1, HG_DK), F32)],
        compiler_params=_cp("parallel"), name=name)(proj, proj, proj, lb, states, do)


def _na_rows(r, rows):
    rs = jnp.clip(r - NA_WIN_H // 2, 0, rows - NA_WIN_H)
    i0 = rs - r + (NA_WIN_H - 1)
    q0 = pl.multiple_of(N_META + GRID_W * r, 16)
    k0 = pl.multiple_of(N_META + GRID_W * rs, 16)
    return i0, q0, k0


def _na_scores(q16, k16, km16, tb_ref, i0, scale):
    s = _nt(q16, k16) * scale
    bias = jnp.concatenate([tb_ref[i0 + j] for j in range(NA_WIN_H)], axis=1)
    return s + bias, _nt(q16, km16) * scale


NA_HB = 2


def _na_fwd(qkv, tb, *, n_tok, name):
    _, nh, lp, dh = qkv.shape
    rows = n_tok // GRID_W
    scale = dh ** -0.5
    kw = NA_WIN_H * GRID_W
    hb = NA_HB

    def body(q_ref, k_ref, v_ref, tb_ref, o_ref, lse_ref):
        o_ref[...] = jnp.zeros_like(o_ref)
        lse_ref[...] = jnp.zeros_like(lse_ref)
        kms, vms = [], []
        for h in range(hb):
            km = k_ref[h, 0:N_META, :]
            vm = v_ref[h, 0:N_META, :]
            s = _nt(q_ref[h, 0:N_META, :], km) * scale
            m = jnp.max(s, axis=1, keepdims=True)
            p = jnp.exp(s - m)
            l = jnp.sum(p, axis=1, keepdims=True)
            o_ref[h, 0:N_META, :] = _nn(p.astype(BF16), vm) / l
            lse_ref[h, 0:N_META, :] = m + jnp.log(l)
            kms.append(km)
            vms.append(vm)

        def step(r, carry):
            i0, q0, k0 = _na_rows(r, rows)
            for h in range(hb):
                q16 = q_ref[h, pl.ds(q0, GRID_W), :]
                k16 = k_ref[h, pl.ds(k0, kw), :]
                v16 = v_ref[h, pl.ds(k0, kw), :]
                s, sm = _na_scores(q16, k16, kms[h], tb_ref.at[h], i0, scale)
                m = jnp.maximum(jnp.max(s, axis=1, keepdims=True),
                                jnp.max(sm, axis=1, keepdims=True))
                p = jnp.exp(s - m)
                pm = jnp.exp(sm - m)
                l = jnp.sum(p, axis=1, keepdims=True) + jnp.sum(pm, axis=1, keepdims=True)
                o = _nn(p.astype(BF16), v16) + _nn(pm.astype(BF16), vms[h])
                o_ref[h, pl.ds(q0, GRID_W), :] = o / l
                lse_ref[h, pl.ds(q0, GRID_W), :] = m + jnp.log(l)
            return carry

        lax.fori_loop(0, rows, step, 0, unroll=2)

    hspec = lambda which: pl.BlockSpec((None, hb, lp, dh), lambda g: (which, g, 0, 0))
    return pl.pallas_call(
        body, grid=(nh // hb,),
        in_specs=[hspec(0), hspec(1), hspec(2),
                  pl.BlockSpec((hb, 2 * NA_WIN_H - 1, GRID_W, GRID_W), lambda g: (g, 0, 0, 0))],
        out_specs=[pl.BlockSpec((hb, lp, dh), lambda g: (g, 0, 0)),
                   pl.BlockSpec((hb, lp, 1), lambda g: (g, 0, 0))],
        out_shape=[jax.ShapeDtypeStruct((nh, lp, dh), F32), jax.ShapeDtypeStruct((nh, lp, 1), F32)],
        compiler_params=_cp("parallel"), name=name)(qkv, qkv, qkv, tb)


def _na_bwd(qkv, tb, o, lse, do, *, n_tok, name):
    _, nh, lp, dh = qkv.shape
    rows = n_tok // GRID_W
    scale = dh ** -0.5
    kw = NA_WIN_H * GRID_W
    hb = NA_HB

    def body(q_ref, k_ref, v_ref, tb_ref, o_ref, lse_ref, do_ref, dq_ref, dk_ref, dv_ref, dtb_ref):
        dq_ref[...] = jnp.zeros_like(dq_ref)
        dk_ref[...] = jnp.zeros_like(dk_ref)
        dv_ref[...] = jnp.zeros_like(dv_ref)
        dtb_ref[...] = jnp.zeros_like(dtb_ref)
        kms, vms, init = [], [], []
        for h in range(hb):
            km = k_ref[h, 0:N_META, :]
            vm = v_ref[h, 0:N_META, :]
            qm = q_ref[h, 0:N_META, :]
            dom = do_ref[h, 0:N_META, :]
            p = jnp.exp(_nt(qm, km) * scale - lse_ref[h, 0:N_META, :])
            dp = _nt(dom.astype(BF16), vm)
            delta = jnp.sum(dom * o_ref[h, 0:N_META, :], axis=1, keepdims=True)
            ds = (p * (dp - delta)).astype(BF16)
            dq_ref[h, 0:N_META, :] = _nn(ds, km) * scale
            init += [_tn(ds, qm) * scale, _tn(p.astype(BF16), dom.astype(BF16))]
            kms.append(km)
            vms.append(vm)

        def step(r, carry):
            i0, q0, k0 = _na_rows(r, rows)
            out = []
            for h in range(hb):
                q16 = q_ref[h, pl.ds(q0, GRID_W), :]
                k16 = k_ref[h, pl.ds(k0, kw), :]
                v16 = v_ref[h, pl.ds(k0, kw), :]
                s, sm = _na_scores(q16, k16, kms[h], tb_ref.at[h], i0, scale)
                lse = lse_ref[h, pl.ds(q0, GRID_W), :]
                p = jnp.exp(s - lse)
                pm = jnp.exp(sm - lse)
                dov = do_ref[h, pl.ds(q0, GRID_W), :]
                do16 = dov.astype(BF16)
                delta = jnp.sum(dov * o_ref[h, pl.ds(q0, GRID_W), :], axis=1, keepdims=True)
                ds = p * (_nt(do16, v16) - delta)
                dsm = (pm * (_nt(do16, vms[h]) - delta)).astype(BF16)
                ds16 = ds.astype(BF16)
                dq_ref[h, pl.ds(q0, GRID_W), :] = (_nn(ds16, k16) + _nn(dsm, kms[h])) * scale
                dk_ref[h, pl.ds(k0, kw), :] += _tn(ds16, q16) * scale
                dv_ref[h, pl.ds(k0, kw), :] += _tn(p.astype(BF16), do16)
                for j in range(NA_WIN_H):
                    dtb_ref[h, i0 + j] += ds[:, j * GRID_W:(j + 1) * GRID_W]
                out += [carry[2 * h] + _tn(dsm, q16) * scale,
                        carry[2 * h + 1] + _tn(pm.astype(BF16), do16)]
            return tuple(out)

        fin = lax.fori_loop(0, rows, step, tuple(init))
        for h in range(hb):
            dk_ref[h, 0:N_META, :] += fin[2 * h]
            dv_ref[h, 0:N_META, :] += fin[2 * h + 1]

    hspec = lambda which: pl.BlockSpec((None, hb, lp, dh), lambda g: (which, g, 0, 0))
    h3 = pl.BlockSpec((hb, lp, dh), lambda g: (g, 0, 0))
    tbs = pl.BlockSpec((hb, 2 * NA_WIN_H - 1, GRID_W, GRID_W), lambda g: (g, 0, 0, 0))
    sds = jax.ShapeDtypeStruct((nh, lp, dh), F32)
    return pl.pallas_call(
        body, grid=(nh // hb,),
        in_specs=[hspec(0), hspec(1), hspec(2), tbs, h3,
                  pl.BlockSpec((hb, lp, 1), lambda g: (g, 0, 0)), h3],
        out_specs=[h3, h3, h3, tbs],
        out_shape=[sds, sds, sds, jax.ShapeDtypeStruct(tb.shape, F32)],
        compiler_params=_cp("parallel"), name=name)(qkv, qkv, qkv, tb, o, lse, do)


def _rpb_onehot():
    c = np.arange(GRID_W)[:, None]
    w = np.arange(GRID_W)[None, :]
    cs = np.clip(c - NA_WIN_W // 2, 0, GRID_W - NA_WIN_W)
    in_win = (w >= cs) & (w < cs + NA_WIN_W)
    dc = np.clip(w - c, -(NA_WIN_W - 1), NA_WIN_W - 1) + NA_WIN_W - 1
    oh = np.zeros((LANES, GRID_W * GRID_W), np.float32)
    flat = np.arange(GRID_W * GRID_W).reshape(GRID_W, GRID_W)
    oh[dc[in_win], flat[in_win]] = 1.0
    neg = np.where(in_win, 0.0, -1e30).astype(np.float32).reshape(1, -1)
    return oh, neg


def _assemble_dproj(dqkv, dq_f, dq_b, dz_f, dz_b, dv_f, dv_b, dg, dgn, dgh, *, name):
    _, nh, lp, dh = dqkv.shape
    naw = nh * dh
    hgw = dq_f.shape[1]
    d = dgn.shape[1]
    cols = 3 * naw + 5 * hgw + 2 * d
    tr = _row_tile(lp)

    def body(na_ref, qf_ref, qb_ref, zf_ref, zb_ref, vf_ref, vb_ref, g_ref, gn_ref, gh_ref, o_ref):
        for which in range(3):
            for h in range(nh):
                c0 = which * naw + h * dh
                o_ref[:, c0:c0 + dh] = na_ref[which, h].astype(BF16)
        c0 = 3 * naw
        o_ref[:, c0:c0 + hgw] = (qf_ref[...] + qb_ref[...]).astype(BF16)
        o_ref[:, c0 + hgw:c0 + 2 * hgw] = zf_ref[...].astype(BF16)
        o_ref[:, c0 + 2 * hgw:c0 + 3 * hgw] = zb_ref[...].astype(BF16)
        o_ref[:, c0 + 3 * hgw:c0 + 4 * hgw] = (vf_ref[...] + vb_ref[...]).astype(BF16)
        o_ref[:, c0 + 4 * hgw:c0 + 5 * hgw] = g_ref[...]
        o_ref[:, c0 + 5 * hgw:c0 + 5 * hgw + d] = gn_ref[...]
        o_ref[:, c0 + 5 * hgw + d:] = gh_ref[...]

    hg = _rspec(tr, hgw)
    return pl.pallas_call(
        body, grid=(lp // tr,),
        in_specs=[pl.BlockSpec((3, nh, tr, dh), lambda i: (0, 0, i, 0)),
                  hg, hg, hg, hg, hg, hg, hg, _rspec(tr, d), _rspec(tr, d)],
        out_specs=_rspec(tr, cols),
        out_shape=jax.ShapeDtypeStruct((lp, cols), BF16),
        compiler_params=_cp("parallel"), name=name)(dqkv, dq_f, dq_b, dz_f, dz_b, dv_f, dv_b, dg,
                                                    dgn, dgh)


def _adamw(w, g, m, v, *, name):
    rows, cols = w.shape
    tr = 256 if rows % 256 == 0 else rows

    def body(w_ref, g_ref, m_ref, v_ref, d_ref, mo_ref, vo_ref):
        gv = g_ref[...]
        mn = ADAM_B1 * m_ref[...] + (1.0 - ADAM_B1) * gv
        vn = ADAM_B2 * v_ref[...] + (1.0 - ADAM_B2) * (gv * gv)
        m_hat = mn / (1.0 - ADAM_B1 ** ADAM_STEP)
        v_hat = vn / (1.0 - ADAM_B2 ** ADAM_STEP)
        d_ref[...] = -ADAM_LR * (m_hat / (jnp.sqrt(v_hat) + ADAM_EPS) + ADAM_WD * w_ref[...])
        mo_ref[...] = mn
        vo_ref[...] = vn

    spec = _rspec(tr, cols)
    sds = jax.ShapeDtypeStruct((rows, cols), F32)
    return pl.pallas_call(
        body, grid=(rows // tr,), in_specs=[spec] * 4, out_specs=[spec] * 3, out_shape=[sds] * 3,
        compiler_params=_cp("parallel"), name=name)(w, g, m, v)


def _local_step(x, tgt, meta, first_weight, rest_weights, g_mix, g_mlp, g_fin, hg_gain, rpb, lb,
                early_grads=None, late_grad=None):
    n_tok, d = x.shape
    hgw = hg_gain.shape[1]
    nh, hh = rpb.shape[0], hgw // HG_DK
    naw = nh * NA_HEAD_DIM
    l_real = N_META + n_tok
    lp = -(-l_real // ROW_ALIGN) * ROW_ALIGN
    n_chunks = l_real // HG_CHUNK
    pad = lp - l_real
    col_qhg = 3 * naw
    col_zf, col_zb, col_i, col_g = (col_qhg + hgw, col_qhg + 2 * hgw, col_qhg + 3 * hgw,
                                    col_qhg + 4 * hgw)
    col_gate = col_qhg + 5 * hgw

    zpad = jnp.zeros((pad, d), F32)
    h0 = jnp.concatenate([meta, x, zpad], axis=0)
    tgt_p = jnp.concatenate([jnp.zeros((N_META, d), F32), tgt, zpad], axis=0)

    oh_np, neg_np = _rpb_onehot()
    oh = jnp.asarray(oh_np)
    rpb_p = jnp.pad(rpb.reshape(nh * (2 * NA_WIN_H - 1), 2 * NA_WIN_W - 1),
                    ((0, 0), (0, LANES - (2 * NA_WIN_W - 1))))
    tb = _matmul(rpb_p, oh, tm=rpb_p.shape[0], tn=512, tk=LANES, precision=HIGHEST,
                 name="rpb_expand")
    tb = (tb + jnp.asarray(neg_np)).reshape(nh, 2 * NA_WIN_H - 1, GRID_W, GRID_W)

    a = _rmsnorm_fwd(h0, g_mix, name="norm_mix")
    w_in = first_weight(a)
    proj = _matmul(a, w_in, name="mm_in")
    qkv = proj[:, :3 * naw].astype(BF16).reshape(lp, 3, nh, NA_HEAD_DIM).transpose(1, 2, 0, 3)
    o_na_hm, lse = _na_fwd(qkv, tb, n_tok=n_tok, name="na_fwd")
    o_na = o_na_hm.transpose(1, 0, 2).reshape(lp, naw)
    lb_f = lb[0].reshape(hh, 1, HG_DK)
    lb_b = lb[1].reshape(hh, 1, HG_DK)
    scan_kw = dict(col_q=col_qhg, col_i=col_i, hh=hh)
    o_f, st_f = _hg_scan_fwd(proj, lb_f, reverse=False, col_z=col_zf, name="hg_scan_f", **scan_kw)
    o_b, st_b = _hg_scan_fwd(proj, lb_b, reverse=True, col_z=col_zb, name="hg_scan_b", **scan_kw)
    o_hg = _hg_out(o_f, o_b, proj, hg_gain, col_g=col_g, name="hg_out")
    w_na, w_hg, w_o, w_up, w_down = rest_weights(o_hg)
    y_na = _matmul(o_na, w_na, name="mm_na_out")
    y_hg = _matmul(o_hg, w_hg, name="mm_hg_out")
    mix = _gate_mix(proj, y_na, y_hg, col_gate=col_gate, name="gate_mix")
    t1 = _matmul(mix, w_o, name="mm_o")
    h1, mlp_in = _residual_norm(h0, t1, g_mlp, name="resid_norm_mlp")
    u = _matmul(mlp_in, w_up, name="mm_up")
    act = _relu2(u, name="relu2")
    t2 = _matmul(act, w_down, name="mm_down")
    dh2, loss, dg_fin = _final_loss(h1, t2, g_fin, tgt_p, n_tok=n_tok, name="final_loss")

    dact = _matmul(dh2, w_down, tb=True, name="mm_down_dx")
    dw_down = _matmul(act, dh2, ta=True, name="mm_down_dw")
    du = _relu2_bwd(dact, u, name="relu2_bwd")
    dm = _matmul(du, w_up, tb=True, name="mm_up_dx")
    dw_up = _matmul(mlp_in, du, ta=True, name="mm_up_dw")
    dh1, dg_mlp = _rmsnorm_bwd_add(h1, g_mlp, dm, dh2, name="norm_mlp_bwd")
    dmix = _matmul(dh1, w_o, tb=True, name="mm_o_dx")
    dw_o = _matmul(mix, dh1, ta=True, name="mm_o_dw")
    dy_na, dy_hg, dgn, dgh = _gate_mix_bwd(proj, y_na, y_hg, dmix, col_gate=col_gate,
                                           name="gate_mix_bwd")
    do_na = _matmul(dy_na, w_na, tb=True, name="mm_na_out_dx")
    dw_na = _matmul(o_na, dy_na, ta=True, name="mm_na_out_dw")
    do_hg = _matmul(dy_hg, w_hg, tb=True, name="mm_hg_out_dx")
    dw_hg = _matmul(o_hg, dy_hg, ta=True, name="mm_hg_out_dw")
    token = early_grads([dw_na, dw_hg, dw_o, dw_up, dw_down]) if early_grads else None
    if token is not None:
        hg_gain = hg_gain + token[0:1, 0:1]
    d_o, dg_hg, d_gain = _hg_out_bwd(o_f, o_b, proj, hg_gain, do_hg, col_g=col_g, name="hg_out_bwd")
    dq_f, dz_f, dv_f, dlb_f = _hg_scan_bwd(proj, lb_f, st_f, d_o, reverse=False, col_z=col_zf,
                                           name="hg_scan_f_bwd", **scan_kw)
    dq_b, dz_b, dv_b, dlb_b = _hg_scan_bwd(proj, lb_b, st_b, d_o, reverse=True, col_z=col_zb,
                                           name="hg_scan_b_bwd", **scan_kw)
    do_na_hm = do_na.reshape(lp, nh, NA_HEAD_DIM).transpose(1, 0, 2)
    dq_na, dk_na, dv_na, dtb = _na_bwd(qkv, tb, o_na_hm, lse, do_na_hm, n_tok=n_tok, name="na_bwd")
    dqkv = jnp.stack([dq_na, dk_na, dv_na], axis=0)
    dproj = _assemble_dproj(dqkv, dq_f, dq_b, dz_f, dz_b, dv_f, dv_b, dg_hg, dgn, dgh,
                            name="assemble_dproj")
    dw_in = _matmul(a, dproj, ta=True, name="mm_in_dw")
    token = late_grad(dw_in) if late_grad else None
    da = _matmul(dproj, w_in, tb=True, name="mm_in_dx", after=token)
    dh0, dg_mix = _rmsnorm_bwd_add(h0, g_mix, da, dh1, name="norm_mix_bwd")
    d_rpb = _matmul(dtb.reshape(nh * (2 * NA_WIN_H - 1), GRID_W * GRID_W), oh, tb=True,
                    tm=nh * (2 * NA_WIN_H - 1), tn=LANES, tk=1024, precision=HIGHEST,
                    name="rpb_reduce")
    d_lb = jnp.concatenate([dlb_f.reshape(1, hgw), dlb_b.reshape(1, hgw)], axis=0)
    return (loss, dh0[N_META:l_real], dh0[:N_META], dw_in, dw_na, dw_hg, dw_o, dw_up, dw_down,
            dg_mix, dg_mlp, dg_fin, d_gain, d_rpb, d_lb)


N_CHIPS = 4
N_DEV = 8
ANY = pl.BlockSpec(memory_space=pl.ANY)


def _place():
    x, y, c = lax.axis_index("x"), lax.axis_index("y"), lax.axis_index("c")
    others = []
    for j in (1, 2, 3):
        tx = (1 - x) if (j >> 1) else x
        ty = (1 - y) if (j & 1) else y
        others.append((tx, ty))
    return x, y, c, others


def _piece(ref, axis, k, half, rh, cs):
    if axis == 1:
        return ref.at[pl.ds(pl.multiple_of(half * rh, 16), rh), pl.ds(pl.multiple_of(k * cs, LANES), cs)]
    return ref.at[pl.ds(pl.multiple_of(k * 2 * rh + half * rh, 16), rh), :]


def _cast_into_full(shard, axis, place, *, name):
    r, cs = shard.shape
    full = (r, cs * N_CHIPS) if axis == 1 else (r * N_CHIPS, cs)
    tr = next(t for t in (256, 128, 64, 32, 16) if r % t == 0)
    nt = r // tr

    def body(p_ref, s_ref, o_ref):
        o_ref[...] = s_ref[...].astype(BF16)

    if axis == 1:
        omap = lambda i, p_ref: (i, p_ref[0])
    else:
        omap = lambda i, p_ref: (p_ref[0] * nt + i, 0)
    return pl.pallas_call(
        body,
        grid_spec=pltpu.PrefetchScalarGridSpec(
            num_scalar_prefetch=1, grid=(nt,),
            in_specs=[pl.BlockSpec((tr, cs), lambda i, p_ref: (i, 0))],
            out_specs=pl.BlockSpec((tr, cs), omap)),
        out_shape=jax.ShapeDtypeStruct(full, BF16),
        compiler_params=_cp("parallel"), name=name)(place, shard)


HBM_SPEC = pl.BlockSpec(memory_space=pltpu.HBM)
SEM_SPEC = pl.BlockSpec(memory_space=pltpu.SEMAPHORE)
SPLIT_COPY = pltpu.CompilerParams(has_side_effects=pltpu.SideEffectType.DATAFLOW_SIDE_EFFECTING)
TOKEN = jax.ShapeDtypeStruct((8, LANES), F32)


def _geo(fulls, axes):
    out = []
    for f, ax in zip(fulls, axes):
        r, cs = (f.shape[0], f.shape[1] // N_CHIPS) if ax == 1 else (f.shape[0] // N_CHIPS, f.shape[1])
        out.append((ax, r // 2, cs))
    return out


def _gather_copies(refs, geo, send_sems, recv_sems):
    x, y, c, others = _place()
    chip = 2 * x + y
    cps = []
    for i, (ax, rh, cs) in enumerate(geo):
        mine = _piece(refs[i], ax, chip, c, rh, cs)
        for j, (tx, ty) in enumerate(others):
            cps.append(pltpu.make_async_remote_copy(
                src_ref=mine, dst_ref=mine, send_sem=send_sems.at[3 * i + j],
                recv_sem=recv_sems.at[3 * i + j], device_id=(tx, ty, c), device_id_type=MESH))
    return cps


def _allgather_start(fulls, axes, after, *, name):
    n = len(fulls)
    geo = _geo(fulls, axes)

    def body(*refs):
        w_refs = refs[:n]
        send_sems, recv_sems = refs[n + 1], refs[n + 2]
        token = refs[2 * n + 3]
        for cp in _gather_copies(w_refs, geo, send_sems, recv_sems):
            cp.start()
        token[...] = jnp.zeros_like(token)

    out = pl.pallas_call(
        body, name=name,
        out_shape=(pltpu.SemaphoreType.DMA((3 * n,)), pltpu.SemaphoreType.DMA((3 * n,)),
                   *[pltpu.HBM(f.shape, f.dtype) for f in fulls], TOKEN),
        in_specs=[HBM_SPEC] * n + [ANY],
        out_specs=(SEM_SPEC, SEM_SPEC, *[HBM_SPEC] * n, pl.BlockSpec(memory_space=pltpu.VMEM)),
        input_output_aliases={i: 2 + i for i in range(n)},
        compiler_params=SPLIT_COPY,
    )(*[pltpu.with_memory_space_constraint(f, pltpu.HBM) for f in fulls], after)
    return out[0], out[1], list(out[2:2 + n]), out[2 + n]


def _allgather_wait(send_sems, recv_sems, fulls, axes, after, *, name):
    n = len(fulls)
    geo = _geo(fulls, axes)

    def body(*refs):
        w_refs = refs[:n]
        for cp in _gather_copies(w_refs, geo, refs[n], refs[n + 1]):
            cp.wait_send()
            cp.wait_recv()

    return list(pl.pallas_call(
        body, name=name,
        out_shape=[pltpu.HBM(f.shape, f.dtype) for f in fulls],
        in_specs=[HBM_SPEC] * n + [SEM_SPEC, SEM_SPEC, ANY],
        out_specs=[HBM_SPEC] * n,
        input_output_aliases={i: i for i in range(n)},
        compiler_params=SPLIT_COPY,
    )(*fulls, send_sems, recv_sems, after))


def _allgather_forward(fulls, axes, *, name):
    n = len(fulls)
    geo = _geo(fulls, axes)

    def body(*refs):
        o_refs = refs[n:2 * n]
        send_sems, recv_sems = refs[2 * n:]
        x, y, c, others = _place()

        def rcopy(i, j, half, to):
            ax, rh, cs = geo[i]
            ref = _piece(o_refs[i], ax, 2 * others[j][0] + others[j][1], half, rh, cs)
            return pltpu.make_async_remote_copy(
                src_ref=ref, dst_ref=ref, send_sem=send_sems.at[3 * i + j],
                recv_sem=recv_sems.at[3 * i + j], device_id=to, device_id_type=MESH)

        cps = [rcopy(i, j, c, (x, y, 1 - c)) for i in range(n) for j in range(3)]
        for cp in cps:
            cp.start()
        for i in range(n):
            for j in range(3):
                rcopy(i, j, 1 - c, (x, y, c)).wait_recv()
        for cp in cps:
            cp.wait_send()

    return list(pl.pallas_call(
        body, in_specs=[ANY] * n, out_specs=[ANY] * n,
        out_shape=[jax.ShapeDtypeStruct(f.shape, f.dtype) for f in fulls],
        input_output_aliases={i: i for i in range(n)},
        scratch_shapes=[pltpu.SemaphoreType.DMA((3 * n,)), pltpu.SemaphoreType.DMA((3 * n,))],
        name=name)(*fulls))


def _scatter_geo(parts, axes):
    out = []
    for p, ax in zip(parts, axes):
        _, rh, cols = p.shape
        out.append((ax, rh, cols // N_CHIPS if ax == 1 else cols))
    return out


def _scatter_copies(p_refs, q_refs, geo, send_sems, recv_sems):
    x, y, c, others = _place()
    chip = 2 * x + y
    cps = []
    for i, (ax, rh, cw) in enumerate(geo):
        for j, (tx, ty) in enumerate(others):
            k = 2 * tx + ty
            src = (p_refs[i].at[0, :, pl.ds(pl.multiple_of(k * cw, LANES), cw)] if ax == 1
                   else p_refs[i].at[k])
            cps.append(pltpu.make_async_remote_copy(
                src_ref=src, dst_ref=q_refs[i].at[chip], send_sem=send_sems.at[3 * i + j],
                recv_sem=recv_sems.at[3 * i + j], device_id=(tx, ty, c), device_id_type=MESH))
    return cps


def _scatter_start(parts, axes, *, name):
    n = len(parts)
    geo = _scatter_geo(parts, axes)
    slots = [pltpu.HBM((N_CHIPS, rh, cw), p.dtype) for p, (_, rh, cw) in zip(parts, geo)]

    def body(*refs):
        p_refs, q_refs = refs[:n], refs[n:2 * n]
        send_sems, recv_sems = refs[2 * n], refs[2 * n + 1]
        token = refs[4 * n + 2]
        for cp in _scatter_copies(p_refs, q_refs, geo, send_sems, recv_sems):
            cp.start()
        token[...] = jnp.zeros_like(token)

    land = [pltpu.with_memory_space_constraint(lax.empty(s.inner_aval.shape, s.inner_aval.dtype), pltpu.HBM)
            for s in slots]
    out = pl.pallas_call(
        body, name=name,
        out_shape=(pltpu.SemaphoreType.DMA((3 * n,)), pltpu.SemaphoreType.DMA((3 * n,)),
                   *[pltpu.HBM(p.shape, p.dtype) for p in parts], *slots, TOKEN),
        in_specs=[HBM_SPEC] * (2 * n),
        out_specs=(SEM_SPEC, SEM_SPEC, *[HBM_SPEC] * (2 * n), pl.BlockSpec(memory_space=pltpu.VMEM)),
        input_output_aliases={i: 2 + i for i in range(2 * n)},
        compiler_params=SPLIT_COPY,
    )(*[pltpu.with_memory_space_constraint(p, pltpu.HBM) for p in parts], *land)
    return out[0], out[1], list(out[2:2 + n]), list(out[2 + n:2 + 2 * n]), out[2 + 2 * n]


def _scatter_wait(send_sems, recv_sems, parts, slots, axes, after, *, name):
    n = len(parts)
    geo = _scatter_geo(parts, axes)

    def body(*refs):
        p_refs, q_refs = refs[:n], refs[n:2 * n]
        for cp in _scatter_copies(p_refs, q_refs, geo, refs[2 * n], refs[2 * n + 1]):
            cp.wait_send()
            cp.wait_recv()

    out = pl.pallas_call(
        body, name=name,
        out_shape=[pltpu.HBM(a.shape, a.dtype) for a in (*parts, *slots)],
        in_specs=[HBM_SPEC] * (2 * n) + [SEM_SPEC, SEM_SPEC, ANY],
        out_specs=[HBM_SPEC] * (2 * n),
        input_output_aliases={i: i for i in range(2 * n)},
        compiler_params=SPLIT_COPY,
    )(*parts, *slots, send_sems, recv_sems, after)
    return list(out[:n]), list(out[n:])


def _sibling_swap(grads, *, name):
    n = len(grads)
    out_shape = [jax.ShapeDtypeStruct((g.shape[0], g.shape[1] // 2, g.shape[2]), g.dtype)
                 for g in grads]

    def body(*refs):
        g_refs, o_refs = refs[:n], refs[n:2 * n]
        send_sems, recv_sems = refs[2 * n:]
        x, y, c, _ = _place()
        cps = []
        for i in range(n):
            rh = grads[i].shape[1] // 2
            src = g_refs[i].at[:, pl.ds(pl.multiple_of((1 - c) * rh, 16), rh), :]
            cp = pltpu.make_async_remote_copy(
                src_ref=src, dst_ref=o_refs[i], send_sem=send_sems.at[i], recv_sem=recv_sems.at[i],
                device_id=(x, y, 1 - c), device_id_type=MESH)
            cp.start()
            cps.append(cp)
        for cp in cps:
            cp.wait()

    return pl.pallas_call(
        body, in_specs=[ANY] * n, out_specs=[ANY] * n, out_shape=out_shape,
        scratch_shapes=[pltpu.SemaphoreType.DMA((n,)), pltpu.SemaphoreType.DMA((n,))],
        name=name)(*grads)


def _pair_add(g3, rx, c_arr, *, out_dtype, name):
    nb, rows, cols = g3.shape
    rh = rows // 2
    tr = next(t for t in (128, 64, 32, 16) if rh % t == 0)
    nt = rh // tr

    def body(c_ref, g_ref, r_ref, o_ref):
        o_ref[...] = (g_ref[...] + r_ref[...]).astype(out_dtype)

    return pl.pallas_call(
        body,
        grid_spec=pltpu.PrefetchScalarGridSpec(
            num_scalar_prefetch=1, grid=(nb, nt),
            in_specs=[pl.BlockSpec((None, tr, cols), lambda b, i, c_ref: (b, c_ref[0] * nt + i, 0)),
                      pl.BlockSpec((None, tr, cols), lambda b, i, c_ref: (b, i, 0))],
            out_specs=pl.BlockSpec((None, tr, cols), lambda b, i, c_ref: (b, i, 0))),
        out_shape=jax.ShapeDtypeStruct((nb, rh, cols), out_dtype),
        compiler_params=_cp("parallel", "parallel"), name=name)(c_arr, g3, rx)


def _sum_slots(q, *, name):
    ns, rows, cols = q.shape
    tr = next(t for t in (128, 64, 32, 16, 8) if rows % t == 0)

    def body(q_ref, o_ref):
        acc = q_ref[0].astype(F32)
        for k in range(1, ns):
            acc = acc + q_ref[k].astype(F32)
        o_ref[...] = acc

    return pl.pallas_call(
        body, grid=(rows // tr,),
        in_specs=[pl.BlockSpec((ns, tr, cols), lambda i: (0, i, 0))],
        out_specs=_rspec(tr, cols),
        out_shape=jax.ShapeDtypeStruct((rows, cols), F32),
        compiler_params=_cp("parallel"), name=name)(q)


def _sum_chips(q, p, place, axis, *, name):
    _, rh, cw = q.shape
    tr = next(t for t in (128, 64, 32, 16) if rh % t == 0)
    nt = rh // tr

    def body(p_ref, *refs):
        q_refs, own_ref, o_ref = refs[:N_CHIPS], refs[N_CHIPS], refs[N_CHIPS + 1]
        chip = p_ref[0]
        acc = jnp.where(chip == 0, own_ref[...], q_refs[0][...]).astype(F32)
        for k in range(1, N_CHIPS):
            acc = acc + jnp.where(chip == k, own_ref[...], q_refs[k][...]).astype(F32)
        o_ref[...] = acc

    def slot_spec(k):
        return pl.BlockSpec((None, tr, cw),
                            lambda i, p_ref: (jnp.where(p_ref[0] == k, (k + 1) % N_CHIPS, k), i, 0))

    if axis == 1:
        own_spec = pl.BlockSpec((None, tr, cw), lambda i, p_ref: (0, i, p_ref[0]))
    else:
        own_spec = pl.BlockSpec((None, tr, cw), lambda i, p_ref: (p_ref[0], i, 0))
    return pl.pallas_call(
        body,
        grid_spec=pltpu.PrefetchScalarGridSpec(
            num_scalar_prefetch=1, grid=(nt,),
            in_specs=[slot_spec(k) for k in range(N_CHIPS)] + [own_spec],
            out_specs=pl.BlockSpec((tr, cw), lambda i, p_ref: (p_ref[1] * nt + i, 0))),
        out_shape=jax.ShapeDtypeStruct((2 * rh, cw), F32),
        compiler_params=_cp("parallel"), name=name)(place, *([q] * N_CHIPS), p)


def _sibling_share(shards, *, name):
    n = len(shards)

    def body(*refs):
        o_refs = refs[n:2 * n]
        send_sems, recv_sems = refs[2 * n:]
        x, y, c, _ = _place()
        cps = []
        for i in range(n):
            rh = shards[i].shape[0] // 2
            mine = o_refs[i].at[pl.ds(pl.multiple_of(c * rh, 8), rh), :]
            cp = pltpu.make_async_remote_copy(
                src_ref=mine, dst_ref=mine, send_sem=send_sems.at[i], recv_sem=recv_sems.at[i],
                device_id=(x, y, 1 - c), device_id_type=MESH)
            cp.start()
            cps.append(cp)
        for i in range(n):
            rh = shards[i].shape[0] // 2
            theirs = o_refs[i].at[pl.ds(pl.multiple_of((1 - c) * rh, 8), rh), :]
            pltpu.make_async_remote_copy(
                src_ref=theirs, dst_ref=theirs, send_sem=send_sems.at[i], recv_sem=recv_sems.at[i],
                device_id=(x, y, c), device_id_type=MESH).wait_recv()
        for cp in cps:
            cp.wait_send()

    return pl.pallas_call(
        body, in_specs=[ANY] * n, out_specs=[ANY] * n,
        out_shape=[jax.ShapeDtypeStruct(h.shape, h.dtype) for h in shards],
        input_output_aliases={i: i for i in range(n)},
        scratch_shapes=[pltpu.SemaphoreType.DMA((n,)), pltpu.SemaphoreType.DMA((n,))],
        name=name)(*shards)


def _gather_all(blk, *, name, after=None):
    rows, cols = blk.shape
    extra = [] if after is None else [after]

    def body(x_ref, *refs):
        out_ref, send_sems, recv_sems, local_sem = refs[len(extra):]
        x, y, c = lax.axis_index("x"), lax.axis_index("y"), lax.axis_index("c")
        me = 4 * x + 2 * y + c
        mine = pltpu.make_async_copy(x_ref, out_ref.at[me], local_sem)
        mine.start()
        cps = []
        for k in range(1, N_DEV):
            tx = (1 - x) if (k >> 2) & 1 else x
            ty = (1 - y) if (k >> 1) & 1 else y
            tc = (1 - c) if k & 1 else c
            cp = pltpu.make_async_remote_copy(
                src_ref=x_ref, dst_ref=out_ref.at[me], send_sem=send_sems.at[k - 1],
                recv_sem=recv_sems.at[k - 1], device_id=(tx, ty, tc), device_id_type=MESH)
            cp.start()
            cps.append(cp)
        for k in range(1, N_DEV):
            tx = (1 - x) if (k >> 2) & 1 else x
            ty = (1 - y) if (k >> 1) & 1 else y
            tc = (1 - c) if k & 1 else c
            got = out_ref.at[4 * tx + 2 * ty + tc]
            pltpu.make_async_remote_copy(
                src_ref=got, dst_ref=got, send_sem=send_sems.at[k - 1], recv_sem=recv_sems.at[k - 1],
                device_id=(x, y, c), device_id_type=MESH).wait_recv()
        for cp in cps:
            cp.wait_send()
        mine.wait()

    vm = pl.BlockSpec(memory_space=pltpu.VMEM)
    return pl.pallas_call(
        body, in_specs=[vm] + [ANY] * len(extra), out_specs=vm,
        out_shape=jax.ShapeDtypeStruct((N_DEV, rows, cols), blk.dtype),
        scratch_shapes=[pltpu.SemaphoreType.DMA((N_DEV - 1,)), pltpu.SemaphoreType.DMA((N_DEV - 1,)),
                        pltpu.SemaphoreType.DMA],
        name=name)(blk, *extra)


def _as_rows(a):
    flat = a.reshape(-1)
    n = flat.shape[0]
    rows = -(-n // (8 * LANES)) * 8
    return jnp.pad(flat, (0, rows * LANES - n)).reshape(rows, LANES)


def _from_rows(p, shape):
    n = int(np.prod(shape))
    return p.reshape(-1)[:n].reshape(shape)


WEIGHT_AXES = (1, 1, 1, 0, 1, 0)
WIRE = BF16


def kernel(x, meta_tokens, w_in, w_na_out, w_hg_out, w_o, w_up, w_down, norm_mix, norm_mlp, norm_final, hg_norm, na_rpb, hg_lb_logits, loss_target, m_meta_tokens, m_w_in, m_w_na_out, m_w_hg_out, m_w_o, m_w_up, m_w_down, m_norm_mix, m_norm_mlp, m_norm_final, m_hg_norm, m_na_rpb, m_hg_lb_logits, v_meta_tokens, v_w_in, v_w_na_out, v_w_hg_out, v_w_o, v_w_up, v_w_down, v_norm_mix, v_norm_mlp, v_norm_final, v_hg_norm, v_na_rpb, v_hg_lb_logits):
    xi, yi, ci = lax.axis_index("x"), lax.axis_index("y"), lax.axis_index("c")
    chip = 2 * xi + yi
    d = x.shape[-1]
    dshard = meta_tokens.shape[1]
    hgw = hg_norm.shape[1]
    lbs = hg_lb_logits.shape[2]
    big = [w_in[0], w_na_out[0], w_hg_out[0], w_o[0], w_up[0], w_down[0]]
    big_m = [m_w_in[0], m_w_na_out[0], m_w_hg_out[0], m_w_o[0], m_w_up[0], m_w_down[0]]
    big_v = [v_w_in[0], v_w_na_out[0], v_w_hg_out[0], v_w_o[0], v_w_up[0], v_w_down[0]]

    place = jnp.stack([chip, ci]).astype(jnp.int32)
    own_w = [_cast_into_full(w, ax, place, name=f"cast_shard_{i}")
             for i, (w, ax) in enumerate(zip(big, WEIGHT_AXES))]
    in_axes, rest_axes = WEIGHT_AXES[:1], WEIGHT_AXES[1:]
    small_in = jnp.concatenate([_as_rows(meta_tokens), _as_rows(hg_lb_logits)], axis=0)
    small_all = _gather_all(small_in, name="gather_small_params")[0::2]
    in_send, in_recv, in_bufs, in_token = _allgather_start(own_w[:1], in_axes, small_all,
                                                           name="weight_allgather_in_start")
    ag_send, ag_recv, ag_bufs, ag_token = _allgather_start(own_w[1:], rest_axes, in_token,
                                                           name="weight_allgather_rest_start")

    def first_weight(after):
        got = _allgather_wait(in_send, in_recv, in_bufs, in_axes, after,
                              name="weight_allgather_in_wait")
        return _allgather_forward(got, in_axes, name="weight_allgather_in_forward")[0]

    def rest_weights(after):
        got = _allgather_wait(ag_send, ag_recv, ag_bufs, rest_axes, after,
                              name="weight_allgather_rest_wait")
        return _allgather_forward(got, rest_axes, name="weight_allgather_rest_forward")

    n_meta_rows = N_META * dshard // LANES
    meta_full = (small_all[:, :n_meta_rows].reshape(N_CHIPS, N_META, dshard)
                 .transpose(1, 0, 2).reshape(N_META, d))
    lbl_full = (small_all[:, n_meta_rows:].reshape(N_CHIPS, -1)[:, :4 * lbs]
                .reshape(N_CHIPS, 2, 2, lbs).transpose(1, 2, 0, 3).reshape(2, 2, N_CHIPS * lbs))
    lb = jax.nn.softmax(lbl_full, axis=1)[:, 0]

    c_arr = ci.reshape(1).astype(jnp.int32)

    def chip_partials(dws, axes, tag):
        g3 = [g.reshape(1, *g.shape) if ax == 1
              else g.reshape(N_CHIPS, g.shape[0] // N_CHIPS, g.shape[1]) for g, ax in zip(dws, axes)]
        rx = _sibling_swap(g3, name=f"grad_sibling_swap_{tag}")
        return [_pair_add(g, r, c_arr, out_dtype=WIRE, name=f"grad_pair_add_{tag}_{i}")
                for i, (g, r) in enumerate(zip(g3, rx))]

    flying = {}

    def scatter_behind(tag, axes):
        def hook(dws):
            send, recv, parts, slots, token = _scatter_start(
                chip_partials(dws, axes, tag), axes, name=f"grad_scatter_{tag}_start")
            flying[tag] = (send, recv, parts, slots)
            return token
        return hook

    def landed(tag, axes, after):
        parts, slots = _scatter_wait(*flying[tag], axes, after, name=f"grad_scatter_{tag}_wait")
        return [_sum_chips(q, p, place, ax, name=f"grad_sum_chips_{tag}_{i}")
                for i, (q, p, ax) in enumerate(zip(slots, parts, axes))]

    (loss, dx, dmeta, *_, dg_mix, dg_mlp, dg_fin, d_gain, d_rpb, d_lb) = _local_step(
        x[0], loss_target[0], meta_full, first_weight, rest_weights,
        norm_mix + ag_token[0:1, 0:1], norm_mlp, norm_final.reshape(1, d), hg_norm, na_rpb[0], lb,
        scatter_behind("rest", rest_axes), lambda dw_in: scatter_behind("in", in_axes)([dw_in]))

    halves_rest = landed("rest", rest_axes, dx)
    halves_in = landed("in", in_axes, halves_rest[-1])
    g_big = _sibling_share(halves_in + halves_rest, name="grad_sibling_share")

    d_rpb_c = d_rpb[:, :2 * NA_WIN_W - 1]
    small_g = [dmeta, dg_mix, dg_mlp, dg_fin, d_gain, d_rpb_c, d_lb, loss]
    packed = jnp.concatenate([_as_rows(a) for a in small_g], axis=0)
    total = _sum_slots(_gather_all(packed, after=halves_in[0], name="gather_small_grads"),
                       name="sum_small_grads")
    offs = np.cumsum([0] + [_as_rows(a).shape[0] for a in small_g])
    take = lambda i, shape: _from_rows(total[offs[i]:offs[i + 1]], shape)
    g_meta_full = take(0, (N_META, d))
    g_norm_mix, g_norm_mlp = take(1, (1, d)), take(2, (1, d))
    g_norm_final = take(3, (d,))
    g_hg_norm = take(4, (1, hgw))
    g_rpb = take(5, na_rpb.shape)
    g_lb = take(6, (2, hgw))
    loss_total = take(7, (1, LANES))[0, 0]
    g_meta = lax.dynamic_slice_in_dim(g_meta_full, chip * dshard, dshard, axis=1)
    dl0 = lb * (1.0 - lb) * g_lb
    g_lbl_full = jnp.stack([dl0, -dl0], axis=1)
    g_lbl = lax.dynamic_slice_in_dim(g_lbl_full, chip * lbs, lbs, axis=2)

    big_out = [_adamw(w, g, m, v, name=f"adamw_{i}")
               for i, (w, g, m, v) in enumerate(zip(big, g_big, big_m, big_v))]
    small_w = [meta_tokens, norm_mix, norm_mlp, norm_final, hg_norm, na_rpb, hg_lb_logits]
    small_gr = [g_meta, g_norm_mix, g_norm_mlp, g_norm_final, g_hg_norm, g_rpb, g_lbl]
    small_m = [m_meta_tokens, m_norm_mix, m_norm_mlp, m_norm_final, m_hg_norm, m_na_rpb, m_hg_lb_logits]
    small_v = [v_meta_tokens, v_norm_mix, v_norm_mlp, v_norm_final, v_hg_norm, v_na_rpb, v_hg_lb_logits]
    pk = lambda lst: jnp.concatenate([_as_rows(a) for a in lst], axis=0)
    sd, sm, sv = _adamw(pk(small_w), pk(small_gr), pk(small_m), pk(small_v), name="adamw_small")
    soffs = np.cumsum([0] + [_as_rows(a).shape[0] for a in small_w])
    unpk = lambda p: [_from_rows(p[soffs[i]:soffs[i + 1]], small_w[i].shape) for i in range(len(small_w))]
    sd, sm, sv = unpk(sd), unpk(sm), unpk(sv)

    def order(bigs, smalls):
        return [smalls[0]] + [b.reshape(1, *b.shape) for b in bigs] + smalls[1:]

    grads = order(g_big, small_gr)
    deltas = order([o[0] for o in big_out], sd)
    new_m = order([o[1] for o in big_out], sm)
    new_v = order([o[2] for o in big_out], sv)
    return (loss_total, dx.reshape(1, *dx.shape), *grads, *deltas, *new_m, *new_v)
```

```python
import functools

import numpy as np
import jax
import jax.numpy as jnp
from jax import lax
from jax.experimental import pallas as pl
from jax.experimental.pallas import tpu as pltpu

F32 = jnp.float32
BF16 = jnp.bfloat16
HIGHEST = lax.Precision.HIGHEST

GRID_W = 64
N_META = 16
EPS = 1e-6
NA_HEAD_DIM = 64
NA_WIN_H = 8
NA_WIN_W = 16
HG_DK = 128
HG_CHUNK = 16
LANES = 128
ROW_ALIGN = 128
VMEM_LIMIT = 48 * 1024 * 1024

ADAM_LR = 0.001
ADAM_B1 = 0.9
ADAM_B2 = 0.999
ADAM_EPS = 1e-08
ADAM_WD = 0.01
ADAM_STEP = 10

MESH = pl.DeviceIdType.MESH


def _cp(*sem):
    return pltpu.CompilerParams(dimension_semantics=sem, vmem_limit_bytes=VMEM_LIMIT)


def _sigmoid(x):
    return 1.0 / (1.0 + jnp.exp(-x))


def _dot(a, b, dims, precision=None):
    return lax.dot_general(a, b, (dims, ((), ())), preferred_element_type=F32, precision=precision)


def _nn(a, b, **kw):
    return _dot(a, b, ((1,), (0,)), **kw)


def _nt(a, b, **kw):
    return _dot(a, b, ((1,), (1,)), **kw)


def _tn(a, b, **kw):
    return _dot(a, b, ((0,), (0,)), **kw)


def _matmul(a, b, *, ta=False, tb=False, tm=None, tn=None, tk=None, out_dtype=F32, name,
            precision=None, after=None):
    extra = [] if after is None else [after]
    if ta:
        kdim, m = a.shape
    else:
        m, kdim = a.shape
    if tb:
        n, k2 = b.shape
    else:
        k2, n = b.shape
    assert kdim == k2, (a.shape, b.shape, ta, tb)
    if tm is None:
        if ta:
            tm = next(t for t in (512, 256, 128, m) if m % t == 0)
        else:
            tm = m // 2 if (m // 2) % 16 == 0 and m > 512 else m
    if tn is None:
        tn = next(t for t in (512, 256, 128, n) if n % t == 0)
    if tk is None:
        tk = kdim if ta else next(t for t in (1024, 512, 256, 128, kdim) if kdim % t == 0)
    assert m % tm == 0 and n % tn == 0 and kdim % tk == 0, (m, n, kdim, tm, tn, tk)
    nk = kdim // tk
    op_dtype = F32 if precision is not None else BF16

    def body(a_ref, b_ref, *refs):
        o_ref, acc_ref = refs[len(extra):]
        kk = pl.program_id(2)

        @pl.when(kk == 0)
        def _():
            acc_ref[...] = jnp.zeros_like(acc_ref)

        av = a_ref[...].astype(op_dtype)
        bv = b_ref[...].astype(op_dtype)
        dims = ((0 if ta else 1,), (1 if tb else 0,))
        acc_ref[...] += _dot(av, bv, dims, precision=precision)

        @pl.when(kk == nk - 1)
        def _():
            o_ref[...] = acc_ref[...].astype(out_dtype)

    a_spec = (pl.BlockSpec((tk, tm), lambda i, j, k: (k, i)) if ta
              else pl.BlockSpec((tm, tk), lambda i, j, k: (i, k)))
    b_spec = (pl.BlockSpec((tn, tk), lambda i, j, k: (j, k)) if tb
              else pl.BlockSpec((tk, tn), lambda i, j, k: (k, j)))
    return pl.pallas_call(
        body,
        grid=(m // tm, n // tn, nk),
        in_specs=[a_spec, b_spec] + [pl.BlockSpec(memory_space=pl.ANY)] * len(extra),
        out_specs=pl.BlockSpec((tm, tn), lambda i, j, k: (i, j)),
        out_shape=jax.ShapeDtypeStruct((m, n), out_dtype),
        scratch_shapes=[pltpu.VMEM((tm, tn), F32)],
        compiler_params=_cp("parallel", "parallel", "arbitrary"),
        name=name,
    )(a, b, *extra)


def _rspec(tr, w, cb=0):
    return pl.BlockSpec((tr, w), lambda i: (i, cb))


def _fspec(shape):
    nd = len(shape)
    return pl.BlockSpec(shape, lambda i: (0,) * nd)


def _row_tile(lp):
    return ROW_ALIGN if lp % ROW_ALIGN == 0 else lp


def _rmsnorm_fwd(x, g, *, name):
    lp, d = x.shape
    tr = _row_tile(lp)

    def body(x_ref, g_ref, o_ref):
        xv = x_ref[...]
        r = lax.rsqrt(jnp.mean(xv * xv, axis=-1, keepdims=True) + EPS)
        o_ref[...] = (xv * r * g_ref[...]).astype(BF16)

    return pl.pallas_call(
        body, grid=(lp // tr,),
        in_specs=[_rspec(tr, d), _fspec((1, d))],
        out_specs=_rspec(tr, d),
        out_shape=jax.ShapeDtypeStruct((lp, d), BF16),
        compiler_params=_cp("parallel"), name=name)(x, g)


def _residual_norm(h, t, g, *, name):
    lp, d = h.shape
    tr = _row_tile(lp)

    def body(h_ref, t_ref, g_ref, h1_ref, m_ref):
        xv = h_ref[...] + t_ref[...]
        h1_ref[...] = xv
        r = lax.rsqrt(jnp.mean(xv * xv, axis=-1, keepdims=True) + EPS)
        m_ref[...] = (xv * r * g_ref[...]).astype(BF16)

    return pl.pallas_call(
        body, grid=(lp // tr,),
        in_specs=[_rspec(tr, d), _rspec(tr, d), _fspec((1, d))],
        out_specs=[_rspec(tr, d), _rspec(tr, d)],
        out_shape=[jax.ShapeDtypeStruct((lp, d), F32), jax.ShapeDtypeStruct((lp, d), BF16)],
        compiler_params=_cp("parallel"), name=name)(h, t, g)


def _rmsnorm_bwd_add(x, g, dy, dres, *, name):
    lp, d = x.shape
    tr = _row_tile(lp)

    def body(x_ref, g_ref, dy_ref, dr_ref, dx_ref, dg_ref):
        @pl.when(pl.program_id(0) == 0)
        def _():
            dg_ref[...] = jnp.zeros_like(dg_ref)

        xv = x_ref[...]
        r = lax.rsqrt(jnp.mean(xv * xv, axis=-1, keepdims=True) + EPS)
        xh = xv * r
        dyv = dy_ref[...]
        dg_ref[...] += jnp.sum(dyv * xh, axis=0, keepdims=True)
        dxh = dyv * g_ref[...]
        dx_ref[...] = dr_ref[...] + r * (dxh - xh * jnp.mean(dxh * xh, axis=-1, keepdims=True))

    return pl.pallas_call(
        body, grid=(lp // tr,),
        in_specs=[_rspec(tr, d), _fspec((1, d)), _rspec(tr, d), _rspec(tr, d)],
        out_specs=[_rspec(tr, d), _fspec((1, d))],
        out_shape=[jax.ShapeDtypeStruct((lp, d), F32), jax.ShapeDtypeStruct((1, d), F32)],
        compiler_params=_cp("arbitrary"), name=name)(x, g, dy, dres)


def _final_loss(h1, t2, g, tgt, *, n_tok, name):
    lp, d = h1.shape
    tr = _row_tile(lp)

    def body(h_ref, t_ref, g_ref, tg_ref, dh_ref, loss_ref, dg_ref):
        i = pl.program_id(0)

        @pl.when(i == 0)
        def _():
            loss_ref[...] = jnp.zeros_like(loss_ref)
            dg_ref[...] = jnp.zeros_like(dg_ref)

        xv = h_ref[...] + t_ref[...]
        r = lax.rsqrt(jnp.mean(xv * xv, axis=-1, keepdims=True) + EPS)
        xh = xv * r
        gv = g_ref[...]
        row = i * tr + lax.broadcasted_iota(jnp.int32, (tr, 1), 0)
        valid = (row >= N_META) & (row < N_META + n_tok)
        err = jnp.where(valid, xh * gv - tg_ref[...], 0.0)
        loss_ref[...] += jnp.sum(0.5 * err * err) / d
        dy = err / d
        dg_ref[...] += jnp.sum(dy * xh, axis=0, keepdims=True)
        dxh = dy * gv
        dh_ref[...] = r * (dxh - xh * jnp.mean(dxh * xh, axis=-1, keepdims=True))

    return pl.pallas_call(
        body, grid=(lp // tr,),
        in_specs=[_rspec(tr, d), _rspec(tr, d), _fspec((1, d)), _rspec(tr, d)],
        out_specs=[_rspec(tr, d), _fspec((1, LANES)), _fspec((1, d))],
        out_shape=[jax.ShapeDtypeStruct((lp, d), F32), jax.ShapeDtypeStruct((1, LANES), F32),
                   jax.ShapeDtypeStruct((1, d), F32)],
        compiler_params=_cp("arbitrary"), name=name)(h1, t2, g, tgt)


def _relu2(u, *, name):
    lp, f = u.shape
    tr = _row_tile(lp)

    def body(u_ref, o_ref):
        rv = jnp.maximum(u_ref[...], 0.0)
        o_ref[...] = (rv * rv).astype(BF16)

    return pl.pallas_call(
        body, grid=(lp // tr,), in_specs=[_rspec(tr, f)], out_specs=_rspec(tr, f),
        out_shape=jax.ShapeDtypeStruct((lp, f), BF16),
        compiler_params=_cp("parallel"), name=name)(u)


def _relu2_bwd(dact, u, *, name):
    lp, f = u.shape
    tr = _row_tile(lp)

    def body(d_ref, u_ref, o_ref):
        o_ref[...] = (d_ref[...] * 2.0 * jnp.maximum(u_ref[...], 0.0)).astype(BF16)

    return pl.pallas_call(
        body, grid=(lp // tr,), in_specs=[_rspec(tr, f), _rspec(tr, f)], out_specs=_rspec(tr, f),
        out_shape=jax.ShapeDtypeStruct((lp, f), BF16),
        compiler_params=_cp("parallel"), name=name)(dact, u)


def _gate_mix(proj, y_na, y_hg, *, col_gate, name):
    lp, d = y_na.shape
    tr = _row_tile(lp)
    cb = col_gate // d

    def body(gn_ref, gh_ref, yn_ref, yh_ref, o_ref):
        o_ref[...] = (_sigmoid(gn_ref[...]) * yn_ref[...]
                      + _sigmoid(gh_ref[...]) * yh_ref[...]).astype(BF16)

    return pl.pallas_call(
        body, grid=(lp // tr,),
        in_specs=[_rspec(tr, d, cb), _rspec(tr, d, cb + 1), _rspec(tr, d), _rspec(tr, d)],
        out_specs=_rspec(tr, d),
        out_shape=jax.ShapeDtypeStruct((lp, d), BF16),
        compiler_params=_cp("parallel"), name=name)(proj, proj, y_na, y_hg)


def _gate_mix_bwd(proj, y_na, y_hg, dmix, *, col_gate, name):
    lp, d = y_na.shape
    tr = _row_tile(lp)
    cb = col_gate // d

    def body(gn_ref, gh_ref, yn_ref, yh_ref, dm_ref, dyn_ref, dyh_ref, dgn_ref, dgh_ref):
        dm = dm_ref[...]
        sn = _sigmoid(gn_ref[...])
        sh = _sigmoid(gh_ref[...])
        dyn_ref[...] = (dm * sn).astype(BF16)
        dyh_ref[...] = (dm * sh).astype(BF16)
        dgn_ref[...] = (dm * yn_ref[...] * sn * (1.0 - sn)).astype(BF16)
        dgh_ref[...] = (dm * yh_ref[...] * sh * (1.0 - sh)).astype(BF16)

    sds = jax.ShapeDtypeStruct((lp, d), BF16)
    return pl.pallas_call(
        body, grid=(lp // tr,),
        in_specs=[_rspec(tr, d, cb), _rspec(tr, d, cb + 1), _rspec(tr, d), _rspec(tr, d),
                  _rspec(tr, d)],
        out_specs=[_rspec(tr, d)] * 4, out_shape=[sds] * 4,
        compiler_params=_cp("parallel"), name=name)(proj, proj, y_na, y_hg, dmix)


def _hg_out(o_f, o_b, proj, gain, *, col_g, name):
    lp, w = o_f.shape
    tr = _row_tile(lp)
    hh = w // HG_DK

    def body(of_ref, ob_ref, g_ref, gain_ref, y_ref):
        gv = g_ref[...]
        sg = gv * _sigmoid(gv)
        for h in range(hh):
            sl = slice(h * HG_DK, (h + 1) * HG_DK)
            o = of_ref[:, sl] + ob_ref[:, sl]
            r = lax.rsqrt(jnp.mean(o * o, axis=-1, keepdims=True) + EPS)
            y_ref[:, sl] = (o * r * gain_ref[:, sl] * sg[:, sl]).astype(BF16)

    return pl.pallas_call(
        body, grid=(lp // tr,),
        in_specs=[_rspec(tr, w), _rspec(tr, w), _rspec(tr, w, col_g // w), _fspec((1, w))],
        out_specs=_rspec(tr, w),
        out_shape=jax.ShapeDtypeStruct((lp, w), BF16),
        compiler_params=_cp("parallel"), name=name)(o_f, o_b, proj, gain)


def _hg_out_bwd(o_f, o_b, proj, gain, dy, *, col_g, name):
    lp, w = o_f.shape
    tr = _row_tile(lp)
    hh = w // HG_DK

    def body(of_ref, ob_ref, g_ref, gain_ref, dy_ref, do_ref, dg_ref, dgain_ref):
        @pl.when(pl.program_id(0) == 0)
        def _():
            dgain_ref[...] = jnp.zeros_like(dgain_ref)

        for h in range(hh):
            sl = slice(h * HG_DK, (h + 1) * HG_DK)
            gv = g_ref[:, sl]
            s = _sigmoid(gv)
            sg = gv * s
            dsg = s + gv * s * (1.0 - s)
            o = of_ref[:, sl] + ob_ref[:, sl]
            r = lax.rsqrt(jnp.mean(o * o, axis=-1, keepdims=True) + EPS)
            on = o * r
            dyv = dy_ref[:, sl]
            gn = gain_ref[:, sl]
            dgain_ref[:, sl] += jnp.sum(dyv * on * sg, axis=0, keepdims=True)
            dg_ref[:, sl] = (dyv * on * gn * dsg).astype(BF16)
            don = dyv * gn * sg
            do_ref[:, sl] = r * (don - on * jnp.mean(don * on, axis=-1, keepdims=True))

    return pl.pallas_call(
        body, grid=(lp // tr,),
        in_specs=[_rspec(tr, w), _rspec(tr, w), _rspec(tr, w, col_g // w), _fspec((1, w)),
                  _rspec(tr, w)],
        out_specs=[_rspec(tr, w), _rspec(tr, w), _fspec((1, w))],
        out_shape=[jax.ShapeDtypeStruct((lp, w), F32), jax.ShapeDtypeStruct((lp, w), BF16),
                   jax.ShapeDtypeStruct((1, w), F32)],
        compiler_params=_cp("arbitrary"), name=name)(o_f, o_b, proj, gain, dy)


HG_GROUP = 8
HG_ROWS = HG_GROUP * HG_CHUNK


def _hg_gates(zq, z, lbv):
    qh = zq * _sigmoid(zq)
    s = _sigmoid(z)
    f = lbv + (1.0 - lbv) * s
    kk = (1.0 - lbv) * _sigmoid(-z)
    return qh, s, f, jnp.log(f), kk


def _chunk_cumsum(g, pos, suffix):
    x = g
    for k in (1, 2, 4, 8):
        if suffix:
            x = x + jnp.where(pos < HG_CHUNK - k, pltpu.roll(x, HG_ROWS - k, 0), 0.0)
        else:
            x = x + jnp.where(pos >= k, pltpu.roll(x, k, 0), 0.0)
    return x


def _pair_levels(b, pos, reverse):
    out = []
    first = b
    for m in (1, 2, 4, 8):
        if m > 1:
            first = jnp.where((pos & (m - 1)) >= m // 2, pltpu.roll(first, m // 2, 0), first)
        nxt = pltpu.roll(first, HG_ROWS - m, 0)
        upper = (pos & (2 * m - 1)) >= m
        if reverse:
            eq = jnp.where(upper, 0.0, jnp.exp(jnp.minimum(b - nxt, 0.0)))
            ek = jnp.where(upper, jnp.exp(jnp.minimum(first - b, 0.0)), 0.0)
        else:
            eq = jnp.where(upper, jnp.exp(jnp.minimum(b - first, 0.0)), 0.0)
            ek = jnp.where(upper, 0.0, jnp.exp(jnp.minimum(nxt - b, 0.0)))
        out.append((m.bit_length(), eq, ek))
    return out


def _g3(a):
    return a.reshape(HG_GROUP, HG_CHUNK, a.shape[-1])


def _hg_scan_fwd(proj, lb, *, reverse, col_q, col_z, col_i, hh, name):
    lp = proj.shape[0]
    n_groups = lp // HG_ROWS
    n_chunks = lp // HG_CHUNK
    c16 = HG_CHUNK
    last = 0 if reverse else c16 - 1

    def body(q_ref, z_ref, i_ref, lb_ref, o_ref, st_ref):
        lbv = lb_ref[...]
        pos = lax.broadcasted_iota(jnp.int32, (HG_ROWS, 1), 0) & (c16 - 1)
        ri = lax.broadcasted_iota(jnp.int32, (HG_ROWS, HG_ROWS), 0)
        ci = lax.broadcasted_iota(jnp.int32, (HG_ROWS, HG_ROWS), 1)

        def group(gi, st):
            gg = (n_groups - 1 - gi) if reverse else gi
            r0 = pl.multiple_of(gg * HG_ROWS, HG_ROWS)
            v = i_ref[pl.ds(r0, HG_ROWS), :]
            qh, _, _, g, kk = _hg_gates(q_ref[pl.ds(r0, HG_ROWS), :], z_ref[pl.ds(r0, HG_ROWS), :],
                                        lbv)
            b = _chunk_cumsum(g, pos, reverse)
            b3, kk3 = _g3(b), _g3(kk)
            bl3 = b3[:, last:last + 1, :]
            qe = (qh * jnp.exp(b)).astype(BF16)
            kd = (kk3 * jnp.exp(bl3 - b3)).reshape(HG_ROWS, HG_DK).astype(BF16)
            decay3 = jnp.exp(bl3)
            v16 = v.astype(BF16)
            a = jnp.where(ri == ci, jnp.sum(qh * kk, axis=1, keepdims=True), 0.0)
            for sh, eq, ek in _pair_levels(b, pos, reverse):
                a = a + jnp.where((ri >> sh) == (ci >> sh),
                                  _nt((qh * eq).astype(BF16), (kk * ek).astype(BF16)), 0.0)
            intra = _nn(a.astype(BF16), v16)
            inter = [None] * HG_GROUP
            for c in (reversed(range(HG_GROUP)) if reverse else range(HG_GROUP)):
                sl = slice(c * c16, (c + 1) * c16)
                st_ref[gg * HG_GROUP + c] = st
                inter[c] = _nt(qe[sl], st.astype(BF16))
                st = decay3[c] * st + _tn(v16[sl], kd[sl])
            o_ref[pl.ds(r0, HG_ROWS), :] = intra + jnp.concatenate(inter, axis=0)
            return st

        lax.fori_loop(0, n_groups, group, jnp.zeros((HG_DK, HG_DK), F32))

    cspec = lambda col: pl.BlockSpec((lp, HG_DK), lambda h: (0, col // HG_DK + h))
    return pl.pallas_call(
        body, grid=(hh,),
        in_specs=[cspec(col_q), cspec(col_z), cspec(col_i),
                  pl.BlockSpec((None, 1, HG_DK), lambda h: (h, 0, 0))],
        out_specs=[pl.BlockSpec((lp, HG_DK), lambda h: (0, h)),
                   pl.BlockSpec((None, n_chunks, HG_DK, HG_DK), lambda h: (h, 0, 0, 0))],
        out_shape=[jax.ShapeDtypeStruct((lp, hh * HG_DK), F32),
                   jax.ShapeDtypeStruct((hh, n_chunks, HG_DK, HG_DK), F32)],
        compiler_params=_cp("parallel"), name=name)(proj, proj, proj, lb)


def _hg_scan_bwd(proj, lb, states, do, *, reverse, col_q, col_z, col_i, hh, name):
    lp = proj.shape[0]
    n_groups = lp // HG_ROWS
    n_chunks = lp // HG_CHUNK
    c16 = HG_CHUNK
    last = 0 if reverse else c16 - 1

    def body(q_ref, z_ref, i_ref, lb_ref, st_ref, do_ref, dq_ref, dz_ref, dv_ref, dlb_ref):
        lbv = lb_ref[...]
        pos = lax.broadcasted_iota(jnp.int32, (HG_ROWS, 1), 0) & (c16 - 1)
        ri = lax.broadcasted_iota(jnp.int32, (HG_ROWS, HG_ROWS), 0)
        ci = lax.broadcasted_iota(jnp.int32, (HG_ROWS, HG_ROWS), 1)

        def group(gi, carry):
            dst, dlb = carry
            gg = gi if reverse else (n_groups - 1 - gi)
            r0 = pl.multiple_of(gg * HG_ROWS, HG_ROWS)
            zq = q_ref[pl.ds(r0, HG_ROWS), :]
            v = i_ref[pl.ds(r0, HG_ROWS), :]
            dov = do_ref[pl.ds(r0, HG_ROWS), :]
            qh, s, f, g, kk = _hg_gates(zq, z_ref[pl.ds(r0, HG_ROWS), :], lbv)
            b = _chunk_cumsum(g, pos, reverse)
            b3, kk3 = _g3(b), _g3(kk)
            bl3 = b3[:, last:last + 1, :]
            eb = jnp.exp(b)
            ebl3 = jnp.exp(bl3 - b3)
            decay3 = jnp.exp(bl3)
            qe16 = (qh * eb).astype(BF16)
            kd16 = (kk3 * ebl3).reshape(HG_ROWS, HG_DK).astype(BF16)
            v16, do16 = v.astype(BF16), dov.astype(BF16)
            do_s, v_ds, dv_st, dbl_st = ([None] * HG_GROUP for _ in range(4))
            for c in (range(HG_GROUP) if reverse else reversed(range(HG_GROUP))):
                sl = slice(c * c16, (c + 1) * c16)
                st = st_ref[gg * HG_GROUP + c]
                st16, dst16 = st.astype(BF16), dst.astype(BF16)
                do_s[c] = _nn(do16[sl], st16)
                v_ds[c] = _nn(v16[sl], dst16)
                dv_st[c] = _nt(kd16[sl], dst16)
                dbl_st[c] = decay3[c] * jnp.sum(st * dst, axis=0, keepdims=True)
                dst = decay3[c] * dst + _tn(do16[sl], qe16[sl])
            cat = lambda parts: jnp.concatenate(parts, axis=0)
            same_row = ri == ci
            da = _nt(do16, v16)
            da_diag = jnp.sum(jnp.where(same_row, da, 0.0), axis=1, keepdims=True)
            a = jnp.where(same_row, jnp.sum(qh * kk, axis=1, keepdims=True), 0.0)
            dq = eb * cat(do_s) + da_diag * kk
            dk_state = ebl3.reshape(HG_ROWS, HG_DK) * cat(v_ds)
            dk = dk_state + da_diag * qh
            for sh, eq, ek in _pair_levels(b, pos, reverse):
                same = (ri >> sh) == (ci >> sh)
                q16, k16 = (qh * eq).astype(BF16), (kk * ek).astype(BF16)
                a = a + jnp.where(same, _nt(q16, k16), 0.0)
                da16 = jnp.where(same, da, 0.0).astype(BF16)
                dq = dq + eq * _nn(da16, k16)
                dk = dk + ek * _tn(da16, q16)
            dv = cat(dv_st) + _tn(a.astype(BF16), do16)
            dbl3 = (jnp.concatenate([d[None] for d in dbl_st], axis=0)
                    + jnp.sum(_g3(kk * dk_state), axis=1, keepdims=True))
            dbl = jnp.broadcast_to(dbl3, (HG_GROUP, c16, HG_DK)).reshape(HG_ROWS, HG_DK)
            db = qh * dq - kk * dk + jnp.where(pos == last, dbl, 0.0)
            dg = _chunk_cumsum(db, pos, not reverse)
            df = dg / f - dk
            sq = _sigmoid(zq)
            dq_ref[pl.ds(r0, HG_ROWS), :] = dq * (sq + zq * sq * (1.0 - sq))
            dz_ref[pl.ds(r0, HG_ROWS), :] = df * (1.0 - lbv) * s * (1.0 - s)
            dv_ref[pl.ds(r0, HG_ROWS), :] = dv
            return dst, dlb + jnp.sum(df * (1.0 - s), axis=0, keepdims=True)

        _, dlb = lax.fori_loop(0, n_groups, group,
                               (jnp.zeros((HG_DK, HG_DK), F32), jnp.zeros((1, HG_DK), F32)))
        dlb_ref[...] = dlb

    cspec = lambda col: pl.BlockSpec((lp, HG_DK), lambda h: (0, col // HG_DK + h))
    ospec = pl.BlockSpec((lp, HG_DK), lambda h: (0, h))
    sds = jax.ShapeDtypeStruct((lp, hh * HG_DK), F32)
    return pl.pallas_call(
        body, grid=(hh,),
        in_specs=[cspec(col_q), cspec(col_z), cspec(col_i),
                  pl.BlockSpec((None, 1, HG_DK), lambda h: (h, 0, 0)),
                  pl.BlockSpec((None, n_chunks, HG_DK, HG_DK), lambda h: (h, 0, 0, 0)),
                  ospec],
        out_specs=[ospec, ospec, ospec, pl.BlockSpec((None, 1, HG_DK), lambda h: (h, 0, 0))],
        out_shape=[sds, sds, sds, jax.ShapeDtypeStruct((hh, 1, HG_DK), F32)],
        compiler_params=_cp("parallel"), name=name)(proj, proj, proj, lb, states, do)


def _na_rows(r, rows):
    rs = jnp.clip(r - NA_WIN_H // 2, 0, rows - NA_WIN_H)
    i0 = rs - r + (NA_WIN_H - 1)
    q0 = pl.multiple_of(N_META + GRID_W * r, 16)
    k0 = pl.multiple_of(N_META + GRID_W * rs, 16)
    return i0, q0, k0


def _na_scores(q16, k16, km16, tb_ref, i0, scale):
    s = _nt(q16, k16) * scale
    bias = jnp.concatenate([tb_ref[i0 + j] for j in range(NA_WIN_H)], axis=1)
    return s + bias, _nt(q16, km16) * scale


NA_HB = 2


def _na_fwd(qkv, tb, *, n_tok, name):
    _, nh, lp, dh = qkv.shape
    rows = n_tok // GRID_W
    scale = dh ** -0.5
    kw = NA_WIN_H * GRID_W
    hb = NA_HB

    def body(q_ref, k_ref, v_ref, tb_ref, o_ref, lse_ref):
        o_ref[...] = jnp.zeros_like(o_ref)
        lse_ref[...] = jnp.zeros_like(lse_ref)
        kms, vms = [], []
        for h in range(hb):
            km = k_ref[h, 0:N_META, :]
            vm = v_ref[h, 0:N_META, :]
            s = _nt(q_ref[h, 0:N_META, :], km) * scale
            m = jnp.max(s, axis=1, keepdims=True)
            p = jnp.exp(s - m)
            l = jnp.sum(p, axis=1, keepdims=True)
            o_ref[h, 0:N_META, :] = _nn(p.astype(BF16), vm) / l
            lse_ref[h, 0:N_META, :] = m + jnp.log(l)
            kms.append(km)
            vms.append(vm)

        def step(r, carry):
            i0, q0, k0 = _na_rows(r, rows)
            for h in range(hb):
                q16 = q_ref[h, pl.ds(q0, GRID_W), :]
                k16 = k_ref[h, pl.ds(k0, kw), :]
                v16 = v_ref[h, pl.ds(k0, kw), :]
                s, sm = _na_scores(q16, k16, kms[h], tb_ref.at[h], i0, scale)
                m = jnp.maximum(jnp.max(s, axis=1, keepdims=True),
                                jnp.max(sm, axis=1, keepdims=True))
                p = jnp.exp(s - m)
                pm = jnp.exp(sm - m)
                l = jnp.sum(p, axis=1, keepdims=True) + jnp.sum(pm, axis=1, keepdims=True)
                o = _nn(p.astype(BF16), v16) + _nn(pm.astype(BF16), vms[h])
                o_ref[h, pl.ds(q0, GRID_W), :] = o / l
                lse_ref[h, pl.ds(q0, GRID_W), :] = m + jnp.log(l)
            return carry

        lax.fori_loop(0, rows, step, 0, unroll=2)

    hspec = lambda which: pl.BlockSpec((None, hb, lp, dh), lambda g: (which, g, 0, 0))
    return pl.pallas_call(
        body, grid=(nh // hb,),
        in_specs=[hspec(0), hspec(1), hspec(2),
                  pl.BlockSpec((hb, 2 * NA_WIN_H - 1, GRID_W, GRID_W), lambda g: (g, 0, 0, 0))],
        out_specs=[pl.BlockSpec((hb, lp, dh), lambda g: (g, 0, 0)),
                   pl.BlockSpec((hb, lp, 1), lambda g: (g, 0, 0))],
        out_shape=[jax.ShapeDtypeStruct((nh, lp, dh), F32), jax.ShapeDtypeStruct((nh, lp, 1), F32)],
        compiler_params=_cp("parallel"), name=name)(qkv, qkv, qkv, tb)


def _na_bwd(qkv, tb, o, lse, do, *, n_tok, name):
    _, nh, lp, dh = qkv.shape
    rows = n_tok // GRID_W
    scale = dh ** -0.5
    kw = NA_WIN_H * GRID_W
    hb = NA_HB

    def body(q_ref, k_ref, v_ref, tb_ref, o_ref, lse_ref, do_ref, dq_ref, dk_ref, dv_ref, dtb_ref):
        dq_ref[...] = jnp.zeros_like(dq_ref)
        dk_ref[...] = jnp.zeros_like(dk_ref)
        dv_ref[...] = jnp.zeros_like(dv_ref)
        dtb_ref[...] = jnp.zeros_like(dtb_ref)
        kms, vms, init = [], [], []
        for h in range(hb):
            km = k_ref[h, 0:N_META, :]
            vm = v_ref[h, 0:N_META, :]
            qm = q_ref[h, 0:N_META, :]
            dom = do_ref[h, 0:N_META, :]
            p = jnp.exp(_nt(qm, km) * scale - lse_ref[h, 0:N_META, :])
            dp = _nt(dom.astype(BF16), vm)
            delta = jnp.sum(dom * o_ref[h, 0:N_META, :], axis=1, keepdims=True)
            ds = (p * (dp - delta)).astype(BF16)
            dq_ref[h, 0:N_META, :] = _nn(ds, km) * scale
            init += [_tn(ds, qm) * scale, _tn(p.astype(BF16), dom.astype(BF16))]
            kms.append(km)
            vms.append(vm)

        def step(r, carry):
            i0, q0, k0 = _na_rows(r, rows)
            out = []
            for h in range(hb):
                q16 = q_ref[h, pl.ds(q0, GRID_W), :]
                k16 = k_ref[h, pl.ds(k0, kw), :]
                v16 = v_ref[h, pl.ds(k0, kw), :]
                s, sm = _na_scores(q16, k16, kms[h], tb_ref.at[h], i0, scale)
                lse = lse_ref[h, pl.ds(q0, GRID_W), :]
                p = jnp.exp(s - lse)
                pm = jnp.exp(sm - lse)
                dov = do_ref[h, pl.ds(q0, GRID_W), :]
                do16 = dov.astype(BF16)
                delta = jnp.sum(dov * o_ref[h, pl.ds(q0, GRID_W), :], axis=1, keepdims=True)
                ds = p * (_nt(do16, v16) - delta)
                dsm = (pm * (_nt(do16, vms[h]) - delta)).astype(BF16)
                ds16 = ds.astype(BF16)
                dq_ref[h, pl.ds(q0, GRID_W), :] = (_nn(ds16, k16) + _nn(dsm, kms[h])) * scale
                dk_ref[h, pl.ds(k0, kw), :] += _tn(ds16, q16) * scale
                dv_ref[h, pl.ds(k0, kw), :] += _tn(p.astype(BF16), do16)
                for j in range(NA_WIN_H):
                    dtb_ref[h, i0 + j] += ds[:, j * GRID_W:(j + 1) * GRID_W]
                out += [carry[2 * h] + _tn(dsm, q16) * scale,
                        carry[2 * h + 1] + _tn(pm.astype(BF16), do16)]
            return tuple(out)

        fin = lax.fori_loop(0, rows, step, tuple(init))
        for h in range(hb):
            dk_ref[h, 0:N_META, :] += fin[2 * h]
            dv_ref[h, 0:N_META, :] += fin[2 * h + 1]

    hspec = lambda which: pl.BlockSpec((None, hb, lp, dh), lambda g: (which, g, 0, 0))
    h3 = pl.BlockSpec((hb, lp, dh), lambda g: (g, 0, 0))
    tbs = pl.BlockSpec((hb, 2 * NA_WIN_H - 1, GRID_W, GRID_W), lambda g: (g, 0, 0, 0))
    sds = jax.ShapeDtypeStruct((nh, lp, dh), F32)
    return pl.pallas_call(
        body, grid=(nh // hb,),
        in_specs=[hspec(0), hspec(1), hspec(2), tbs, h3,
                  pl.BlockSpec((hb, lp, 1), lambda g: (g, 0, 0)), h3],
        out_specs=[h3, h3, h3, tbs],
        out_shape=[sds, sds, sds, jax.ShapeDtypeStruct(tb.shape, F32)],
        compiler_params=_cp("parallel"), name=name)(qkv, qkv, qkv, tb, o, lse, do)


def _rpb_onehot():
    c = np.arange(GRID_W)[:, None]
    w = np.arange(GRID_W)[None, :]
    cs = np.clip(c - NA_WIN_W // 2, 0, GRID_W - NA_WIN_W)
    in_win = (w >= cs) & (w < cs + NA_WIN_W)
    dc = np.clip(w - c, -(NA_WIN_W - 1), NA_WIN_W - 1) + NA_WIN_W - 1
    oh = np.zeros((LANES, GRID_W * GRID_W), np.float32)
    flat = np.arange(GRID_W * GRID_W).reshape(GRID_W, GRID_W)
    oh[dc[in_win], flat[in_win]] = 1.0
    neg = np.where(in_win, 0.0, -1e30).astype(np.float32).reshape(1, -1)
    return oh, neg


def _assemble_dproj(dqkv, dq_f, dq_b, dz_f, dz_b, dv_f, dv_b, dg, dgn, dgh, *, name):
    _, nh, lp, dh = dqkv.shape
    naw = nh * dh
    hgw = dq_f.shape[1]
    d = dgn.shape[1]
    cols = 3 * naw + 5 * hgw + 2 * d
    tr = _row_tile(lp)

    def body(na_ref, qf_ref, qb_ref, zf_ref, zb_ref, vf_ref, vb_ref, g_ref, gn_ref, gh_ref, o_ref):
        for which in range(3):
            for h in range(nh):
                c0 = which * naw + h * dh
                o_ref[:, c0:c0 + dh] = na_ref[which, h].astype(BF16)
        c0 = 3 * naw
        o_ref[:, c0:c0 + hgw] = (qf_ref[...] + qb_ref[...]).astype(BF16)
        o_ref[:, c0 + hgw:c0 + 2 * hgw] = zf_ref[...].astype(BF16)
        o_ref[:, c0 + 2 * hgw:c0 + 3 * hgw] = zb_ref[...].astype(BF16)
        o_ref[:, c0 + 3 * hgw:c0 + 4 * hgw] = (vf_ref[...] + vb_ref[...]).astype(BF16)
        o_ref[:, c0 + 4 * hgw:c0 + 5 * hgw] = g_ref[...]
        o_ref[:, c0 + 5 * hgw:c0 + 5 * hgw + d] = gn_ref[...]
        o_ref[:, c0 + 5 * hgw + d:] = gh_ref[...]

    hg = _rspec(tr, hgw)
    return pl.pallas_call(
        body, grid=(lp // tr,),
        in_specs=[pl.BlockSpec((3, nh, tr, dh), lambda i: (0, 0, i, 0)),
                  hg, hg, hg, hg, hg, hg, hg, _rspec(tr, d), _rspec(tr, d)],
        out_specs=_rspec(tr, cols),
        out_shape=jax.ShapeDtypeStruct((lp, cols), BF16),
        compiler_params=_cp("parallel"), name=name)(dqkv, dq_f, dq_b, dz_f, dz_b, dv_f, dv_b, dg,
                                                    dgn, dgh)


def _adamw(w, g, m, v, *, name):
    rows, cols = w.shape
    tr = 256 if rows % 256 == 0 else rows

    def body(w_ref, g_ref, m_ref, v_ref, d_ref, mo_ref, vo_ref):
        gv = g_ref[...]
        mn = ADAM_B1 * m_ref[...] + (1.0 - ADAM_B1) * gv
        vn = ADAM_B2 * v_ref[...] + (1.0 - ADAM_B2) * (gv * gv)
        m_hat = mn / (1.0 - ADAM_B1 ** ADAM_STEP)
        v_hat = vn / (1.0 - ADAM_B2 ** ADAM_STEP)
        d_ref[...] = -ADAM_LR * (m_hat / (jnp.sqrt(v_hat) + ADAM_EPS) + ADAM_WD * w_ref[...])
        mo_ref[...] = mn
        vo_ref[...] = vn

    spec = _rspec(tr, cols)
    sds = jax.ShapeDtypeStruct((rows, cols), F32)
    return pl.pallas_call(
        body, grid=(rows // tr,), in_specs=[spec] * 4, out_specs=[spec] * 3, out_shape=[sds] * 3,
        compiler_params=_cp("parallel"), name=name)(w, g, m, v)


def _local_step(x, tgt, meta, first_weight, rest_weights, g_mix, g_mlp, g_fin, hg_gain, rpb, lb,
                early_grads=None, late_grad=None):
    n_tok, d = x.shape
    hgw = hg_gain.shape[1]
    nh, hh = rpb.shape[0], hgw // HG_DK
    naw = nh * NA_HEAD_DIM
    l_real = N_META + n_tok
    lp = -(-l_real // ROW_ALIGN) * ROW_ALIGN
    n_chunks = l_real // HG_CHUNK
    pad = lp - l_real
    col_qhg = 3 * naw
    col_zf, col_zb, col_i, col_g = (col_qhg + hgw, col_qhg + 2 * hgw, col_qhg + 3 * hgw,
                                    col_qhg + 4 * hgw)
    col_gate = col_qhg + 5 * hgw

    zpad = jnp.zeros((pad, d), F32)
    h0 = jnp.concatenate([meta, x, zpad], axis=0)
    tgt_p = jnp.concatenate([jnp.zeros((N_META, d), F32), tgt, zpad], axis=0)

    oh_np, neg_np = _rpb_onehot()
    oh = jnp.asarray(oh_np)
    rpb_p = jnp.pad(rpb.reshape(nh * (2 * NA_WIN_H - 1), 2 * NA_WIN_W - 1),
                    ((0, 0), (0, LANES - (2 * NA_WIN_W - 1))))
    tb = _matmul(rpb_p, oh, tm=rpb_p.shape[0], tn=512, tk=LANES, precision=HIGHEST,
                 name="rpb_expand")
    tb = (tb + jnp.asarray(neg_np)).reshape(nh, 2 * NA_WIN_H - 1, GRID_W, GRID_W)

    a = _rmsnorm_fwd(h0, g_mix, name="norm_mix")
    w_in = first_weight(a)
    proj = _matmul(a, w_in, name="mm_in")
    qkv = proj[:, :3 * naw].astype(BF16).reshape(lp, 3, nh, NA_HEAD_DIM).transpose(1, 2, 0, 3)
    o_na_hm, lse = _na_fwd(qkv, tb, n_tok=n_tok, name="na_fwd")
    o_na = o_na_hm.transpose(1, 0, 2).reshape(lp, naw)
    lb_f = lb[0].reshape(hh, 1, HG_DK)
    lb_b = lb[1].reshape(hh, 1, HG_DK)
    scan_kw = dict(col_q=col_qhg, col_i=col_i, hh=hh)
    o_f, st_f = _hg_scan_fwd(proj, lb_f, reverse=False, col_z=col_zf, name="hg_scan_f", **scan_kw)
    o_b, st_b = _hg_scan_fwd(proj, lb_b, reverse=True, col_z=col_zb, name="hg_scan_b", **scan_kw)
    o_hg = _hg_out(o_f, o_b, proj, hg_gain, col_g=col_g, name="hg_out")
    w_na, w_hg, w_o, w_up, w_down = rest_weights(o_hg)
    y_na = _matmul(o_na, w_na, name="mm_na_out")
    y_hg = _matmul(o_hg, w_hg, name="mm_hg_out")
    mix = _gate_mix(proj, y_na, y_hg, col_gate=col_gate, name="gate_mix")
    t1 = _matmul(mix, w_o, name="mm_o")
    h1, mlp_in = _residual_norm(h0, t1, g_mlp, name="resid_norm_mlp")
    u = _matmul(mlp_in, w_up, name="mm_up")
    act = _relu2(u, name="relu2")
    t2 = _matmul(act, w_down, name="mm_down")
    dh2, loss, dg_fin = _final_loss(h1, t2, g_fin, tgt_p, n_tok=n_tok, name="final_loss")

    dact = _matmul(dh2, w_down, tb=True, name="mm_down_dx")
    dw_down = _matmul(act, dh2, ta=True, name="mm_down_dw")
    du = _relu2_bwd(dact, u, name="relu2_bwd")
    dm = _matmul(du, w_up, tb=True, name="mm_up_dx")
    dw_up = _matmul(mlp_in, du, ta=True, name="mm_up_dw")
    dh1, dg_mlp = _rmsnorm_bwd_add(h1, g_mlp, dm, dh2, name="norm_mlp_bwd")
    dmix = _matmul(dh1, w_o, tb=True, name="mm_o_dx")
    dw_o = _matmul(mix, dh1, ta=True, name="mm_o_dw")
    dy_na, dy_hg, dgn, dgh = _gate_mix_bwd(proj, y_na, y_hg, dmix, col_gate=col_gate,
                                           name="gate_mix_bwd")
    do_na = _matmul(dy_na, w_na, tb=True, name="mm_na_out_dx")
    dw_na = _matmul(o_na, dy_na, ta=True, name="mm_na_out_dw")
    do_hg = _matmul(dy_hg, w_hg, tb=True, name="mm_hg_out_dx")
    dw_hg = _matmul(o_hg, dy_hg, ta=True, name="mm_hg_out_dw")
    token = early_grads([dw_na, dw_hg, dw_o, dw_up, dw_down]) if early_grads else None
    if token is not None:
        hg_gain = hg_gain + token[0:1, 0:1]
    d_o, dg_hg, d_gain = _hg_out_bwd(o_f, o_b, proj, hg_gain, do_hg, col_g=col_g, name="hg_out_bwd")
    dq_f, dz_f, dv_f, dlb_f = _hg_scan_bwd(proj, lb_f, st_f, d_o, reverse=False, col_z=col_zf,
                                           name="hg_scan_f_bwd", **scan_kw)
    dq_b, dz_b, dv_b, dlb_b = _hg_scan_bwd(proj, lb_b, st_b, d_o, reverse=True, col_z=col_zb,
                                           name="hg_scan_b_bwd", **scan_kw)
    do_na_hm = do_na.reshape(lp, nh, NA_HEAD_DIM).transpose(1, 0, 2)
    dq_na, dk_na, dv_na, dtb = _na_bwd(qkv, tb, o_na_hm, lse, do_na_hm, n_tok=n_tok, name="na_bwd")
    dqkv = jnp.stack([dq_na, dk_na, dv_na], axis=0)
    dproj = _assemble_dproj(dqkv, dq_f, dq_b, dz_f, dz_b, dv_f, dv_b, dg_hg, dgn, dgh,
                            name="assemble_dproj")
    dw_in = _matmul(a, dproj, ta=True, name="mm_in_dw")
    token = late_grad(dw_in) if late_grad else None
    da = _matmul(dproj, w_in, tb=True, name="mm_in_dx", after=token)
    dh0, dg_mix = _rmsnorm_bwd_add(h0, g_mix, da, dh1, name="norm_mix_bwd")
    d_rpb = _matmul(dtb.reshape(nh * (2 * NA_WIN_H - 1), GRID_W * GRID_W), oh, tb=True,
                    tm=nh * (2 * NA_WIN_H - 1), tn=LANES, tk=1024, precision=HIGHEST,
                    name="rpb_reduce")
    d_lb = jnp.concatenate([dlb_f.reshape(1, hgw), dlb_b.reshape(1, hgw)], axis=0)
    return (loss, dh0[N_META:l_real], dh0[:N_META], dw_in, dw_na, dw_hg, dw_o, dw_up, dw_down,
            dg_mix, dg_mlp, dg_fin, d_gain, d_rpb, d_lb)


N_CHIPS = 4
N_DEV = 8
ANY = pl.BlockSpec(memory_space=pl.ANY)


def _place():
    x, y, c = lax.axis_index("x"), lax.axis_index("y"), lax.axis_index("c")
    others = []
    for j in (1, 2, 3):
        tx = (1 - x) if (j >> 1) else x
        ty = (1 - y) if (j & 1) else y
        others.append((tx, ty))
    return x, y, c, others


def _piece(ref, axis, k, half, rh, cs):
    if axis == 1:
        return ref.at[pl.ds(pl.multiple_of(half * rh, 16), rh), pl.ds(pl.multiple_of(k * cs, LANES), cs)]
    return ref.at[pl.ds(pl.multiple_of(k * 2 * rh + half * rh, 16), rh), :]


def _cast_into_full(shard, axis, place, *, name):
    r, cs = shard.shape
    full = (r, cs * N_CHIPS) if axis == 1 else (r * N_CHIPS, cs)
    tr = next(t for t in (256, 128, 64, 32, 16) if r % t == 0)
    nt = r // tr

    def body(p_ref, s_ref, o_ref):
        o_ref[...] = s_ref[...].astype(BF16)

    if axis == 1:
        omap = lambda i, p_ref: (i, p_ref[0])
    else:
        omap = lambda i, p_ref: (p_ref[0] * nt + i, 0)
    return pl.pallas_call(
        body,
        grid_spec=pltpu.PrefetchScalarGridSpec(
            num_scalar_prefetch=1, grid=(nt,),
            in_specs=[pl.BlockSpec((tr, cs), lambda i, p_ref: (i, 0))],
            out_specs=pl.BlockSpec((tr, cs), omap)),
        out_shape=jax.ShapeDtypeStruct(full, BF16),
        compiler_params=_cp("parallel"), name=name)(place, shard)


HBM_SPEC = pl.BlockSpec(memory_space=pltpu.HBM)
SEM_SPEC = pl.BlockSpec(memory_space=pltpu.SEMAPHORE)
SPLIT_COPY = pltpu.CompilerParams(has_side_effects=pltpu.SideEffectType.DATAFLOW_SIDE_EFFECTING)
TOKEN = jax.ShapeDtypeStruct((8, LANES), F32)


def _geo(fulls, axes):
    out = []
    for f, ax in zip(fulls, axes):
        r, cs = (f.shape[0], f.shape[1] // N_CHIPS) if ax == 1 else (f.shape[0] // N_CHIPS, f.shape[1])
        out.append((ax, r // 2, cs))
    return out


def _gather_copies(refs, geo, send_sems, recv_sems):
    x, y, c, others = _place()
    chip = 2 * x + y
    cps = []
    for i, (ax, rh, cs) in enumerate(geo):
        mine = _piece(refs[i], ax, chip, c, rh, cs)
        for j, (tx, ty) in enumerate(others):
            cps.append(pltpu.make_async_remote_copy(
                src_ref=mine, dst_ref=mine, send_sem=send_sems.at[3 * i + j],
                recv_sem=recv_sems.at[3 * i + j], device_id=(tx, ty, c), device_id_type=MESH))
    return cps


def _allgather_start(fulls, axes, after, *, name):
    n = len(fulls)
    geo = _geo(fulls, axes)

    def body(*refs):
        w_refs = refs[:n]
        send_sems, recv_sems = refs[n + 1], refs[n + 2]
        token = refs[2 * n + 3]
        for cp in _gather_copies(w_refs, geo, send_sems, recv_sems):
            cp.start()
        token[...] = jnp.zeros_like(token)

    out = pl.pallas_call(
        body, name=name,
        out_shape=(pltpu.SemaphoreType.DMA((3 * n,)), pltpu.SemaphoreType.DMA((3 * n,)),
                   *[pltpu.HBM(f.shape, f.dtype) for f in fulls], TOKEN),
        in_specs=[HBM_SPEC] * n + [ANY],
        out_specs=(SEM_SPEC, SEM_SPEC, *[HBM_SPEC] * n, pl.BlockSpec(memory_space=pltpu.VMEM)),
        input_output_aliases={i: 2 + i for i in range(n)},
        compiler_params=SPLIT_COPY,
    )(*[pltpu.with_memory_space_constraint(f, pltpu.HBM) for f in fulls], after)
    return out[0], out[1], list(out[2:2 + n]), out[2 + n]


def _allgather_wait(send_sems, recv_sems, fulls, axes, after, *, name):
    n = len(fulls)
    geo = _geo(fulls, axes)

    def body(*refs):
        w_refs = refs[:n]
        for cp in _gather_copies(w_refs, geo, refs[n], refs[n + 1]):
            cp.wait_send()
            cp.wait_recv()

    return list(pl.pallas_call(
        body, name=name,
        out_shape=[pltpu.HBM(f.shape, f.dtype) for f in fulls],
        in_specs=[HBM_SPEC] * n + [SEM_SPEC, SEM_SPEC, ANY],
        out_specs=[HBM_SPEC] * n,
        input_output_aliases={i: i for i in range(n)},
        compiler_params=SPLIT_COPY,
    )(*fulls, send_sems, recv_sems, after))


def _allgather_forward(fulls, axes, *, name):
    n = len(fulls)
    geo = _geo(fulls, axes)

    def body(*refs):
        o_refs = refs[n:2 * n]
        send_sems, recv_sems = refs[2 * n:]
        x, y, c, others = _place()

        def rcopy(i, j, half, to):
            ax, rh, cs = geo[i]
            ref = _piece(o_refs[i], ax, 2 * others[j][0] + others[j][1], half, rh, cs)
            return pltpu.make_async_remote_copy(
                src_ref=ref, dst_ref=ref, send_sem=send_sems.at[3 * i + j],
                recv_sem=recv_sems.at[3 * i + j], device_id=to, device_id_type=MESH)

        cps = [rcopy(i, j, c, (x, y, 1 - c)) for i in range(n) for j in range(3)]
        for cp in cps:
            cp.start()
        for i in range(n):
            for j in range(3):
                rcopy(i, j, 1 - c, (x, y, c)).wait_recv()
        for cp in cps:
            cp.wait_send()

    return list(pl.pallas_call(
        body, in_specs=[ANY] * n, out_specs=[ANY] * n,
        out_shape=[jax.ShapeDtypeStruct(f.shape, f.dtype) for f in fulls],
        input_output_aliases={i: i for i in range(n)},
        scratch_shapes=[pltpu.SemaphoreType.DMA((3 * n,)), pltpu.SemaphoreType.DMA((3 * n,))],
        name=name)(*fulls))


def _scatter_geo(parts, axes):
    out = []
    for p, ax in zip(parts, axes):
        _, rh, cols = p.shape
        out.append((ax, rh, cols // N_CHIPS if ax == 1 else cols))
    return out


def _scatter_copies(p_refs, q_refs, geo, send_sems, recv_sems):
    x, y, c, others = _place()
    chip = 2 * x + y
    cps = []
    for i, (ax, rh, cw) in enumerate(geo):
        for j, (tx, ty) in enumerate(others):
            k = 2 * tx + ty
            src = (p_refs[i].at[0, :, pl.ds(pl.multiple_of(k * cw, LANES), cw)] if ax == 1
                   else p_refs[i].at[k])
            cps.append(pltpu.make_async_remote_copy(
                src_ref=src, dst_ref=q_refs[i].at[chip], send_sem=send_sems.at[3 * i + j],
                recv_sem=recv_sems.at[3 * i + j], device_id=(tx, ty, c), device_id_type=MESH))
    return cps


def _scatter_start(parts, axes, *, name):
    n = len(parts)
    geo = _scatter_geo(parts, axes)
    slots = [pltpu.HBM((N_CHIPS, rh, cw), p.dtype) for p, (_, rh, cw) in zip(parts, geo)]

    def body(*refs):
        p_refs, q_refs = refs[:n], refs[n:2 * n]
        send_sems, recv_sems = refs[2 * n], refs[2 * n + 1]
        token = refs[4 * n + 2]
        for cp in _scatter_copies(p_refs, q_refs, geo, send_sems, recv_sems):
            cp.start()
        token[...] = jnp.zeros_like(token)

    land = [pltpu.with_memory_space_constraint(lax.empty(s.inner_aval.shape, s.inner_aval.dtype), pltpu.HBM)
            for s in slots]
    out = pl.pallas_call(
        body, name=name,
        out_shape=(pltpu.SemaphoreType.DMA((3 * n,)), pltpu.SemaphoreType.DMA((3 * n,)),
                   *[pltpu.HBM(p.shape, p.dtype) for p in parts], *slots, TOKEN),
        in_specs=[HBM_SPEC] * (2 * n),
        out_specs=(SEM_SPEC, SEM_SPEC, *[HBM_SPEC] * (2 * n), pl.BlockSpec(memory_space=pltpu.VMEM)),
        input_output_aliases={i: 2 + i for i in range(2 * n)},
        compiler_params=SPLIT_COPY,
    )(*[pltpu.with_memory_space_constraint(p, pltpu.HBM) for p in parts], *land)
    return out[0], out[1], list(out[2:2 + n]), list(out[2 + n:2 + 2 * n]), out[2 + 2 * n]


def _scatter_wait(send_sems, recv_sems, parts, slots, axes, after, *, name):
    n = len(parts)
    geo = _scatter_geo(parts, axes)

    def body(*refs):
        p_refs, q_refs = refs[:n], refs[n:2 * n]
        for cp in _scatter_copies(p_refs, q_refs, geo, refs[2 * n], refs[2 * n + 1]):
            cp.wait_send()
            cp.wait_recv()

    out = pl.pallas_call(
        body, name=name,
        out_shape=[pltpu.HBM(a.shape, a.dtype) for a in (*parts, *slots)],
        in_specs=[HBM_SPEC] * (2 * n) + [SEM_SPEC, SEM_SPEC, ANY],
        out_specs=[HBM_SPEC] * (2 * n),
        input_output_aliases={i: i for i in range(2 * n)},
        compiler_params=SPLIT_COPY,
    )(*parts, *slots, send_sems, recv_sems, after)
    return list(out[:n]), list(out[n:])


def _sibling_swap(grads, *, name):
    n = len(grads)
    out_shape = [jax.ShapeDtypeStruct((g.shape[0], g.shape[1] // 2, g.shape[2]), g.dtype)
                 for g in grads]

    def body(*refs):
        g_refs, o_refs = refs[:n], refs[n:2 * n]
        send_sems, recv_sems = refs[2 * n:]
        x, y, c, _ = _place()
        cps = []
        for i in range(n):
            rh = grads[i].shape[1] // 2
            src = g_refs[i].at[:, pl.ds(pl.multiple_of((1 - c) * rh, 16), rh), :]
            cp = pltpu.make_async_remote_copy(
                src_ref=src, dst_ref=o_refs[i], send_sem=send_sems.at[i], recv_sem=recv_sems.at[i],
                device_id=(x, y, 1 - c), device_id_type=MESH)
            cp.start()
            cps.append(cp)
        for cp in cps:
            cp.wait()

    return pl.pallas_call(
        body, in_specs=[ANY] * n, out_specs=[ANY] * n, out_shape=out_shape,
        scratch_shapes=[pltpu.SemaphoreType.DMA((n,)), pltpu.SemaphoreType.DMA((n,))],
        name=name)(*grads)


def _pair_add(g3, rx, c_arr, *, out_dtype, name):
    nb, rows, cols = g3.shape
    rh = rows // 2
    tr = next(t for t in (128, 64, 32, 16) if rh % t == 0)
    nt = rh // tr

    def body(c_ref, g_ref, r_ref, o_ref):
        o_ref[...] = (g_ref[...] + r_ref[...]).astype(out_dtype)

    return pl.pallas_call(
        body,
        grid_spec=pltpu.PrefetchScalarGridSpec(
            num_scalar_prefetch=1, grid=(nb, nt),
            in_specs=[pl.BlockSpec((None, tr, cols), lambda b, i, c_ref: (b, c_ref[0] * nt + i, 0)),
                      pl.BlockSpec((None, tr, cols), lambda b, i, c_ref: (b, i, 0))],
            out_specs=pl.BlockSpec((None, tr, cols), lambda b, i, c_ref: (b, i, 0))),
        out_shape=jax.ShapeDtypeStruct((nb, rh, cols), out_dtype),
        compiler_params=_cp("parallel", "parallel"), name=name)(c_arr, g3, rx)


def _sum_slots(q, *, name):
    ns, rows, cols = q.shape
    tr = next(t for t in (128, 64, 32, 16, 8) if rows % t == 0)

    def body(q_ref, o_ref):
        acc = q_ref[0].astype(F32)
        for k in range(1, ns):
            acc = acc + q_ref[k].astype(F32)
        o_ref[...] = acc

    return pl.pallas_call(
        body, grid=(rows // tr,),
        in_specs=[pl.BlockSpec((ns, tr, cols), lambda i: (0, i, 0))],
        out_specs=_rspec(tr, cols),
        out_shape=jax.ShapeDtypeStruct((rows, cols), F32),
        compiler_params=_cp("parallel"), name=name)(q)


def _sum_chips(q, p, place, axis, *, name):
    _, rh, cw = q.shape
    tr = next(t for t in (128, 64, 32, 16) if rh % t == 0)
    nt = rh // tr

    def body(p_ref, *refs):
        q_refs, own_ref, o_ref = refs[:N_CHIPS], refs[N_CHIPS], refs[N_CHIPS + 1]
        chip = p_ref[0]
        acc = jnp.where(chip == 0, own_ref[...], q_refs[0][...]).astype(F32)
        for k in range(1, N_CHIPS):
            acc = acc + jnp.where(chip == k, own_ref[...], q_refs[k][...]).astype(F32)
        o_ref[...] = acc

    def slot_spec(k):
        return pl.BlockSpec((None, tr, cw),
                            lambda i, p_ref: (jnp.where(p_ref[0] == k, (k + 1) % N_CHIPS, k), i, 0))

    if axis == 1:
        own_spec = pl.BlockSpec((None, tr, cw), lambda i, p_ref: (0, i, p_ref[0]))
    else:
        own_spec = pl.BlockSpec((None, tr, cw), lambda i, p_ref: (p_ref[0], i, 0))
    return pl.pallas_call(
        body,
        grid_spec=pltpu.PrefetchScalarGridSpec(
            num_scalar_prefetch=1, grid=(nt,),
            in_specs=[slot_spec(k) for k in range(N_CHIPS)] + [own_spec],
            out_specs=pl.BlockSpec((tr, cw), lambda i, p_ref: (p_ref[1] * nt + i, 0))),
        out_shape=jax.ShapeDtypeStruct((2 * rh, cw), F32),
        compiler_params=_cp("parallel"), name=name)(place, *([q] * N_CHIPS), p)


def _sibling_share(shards, *, name):
    n = len(shards)

    def body(*refs):
        o_refs = refs[n:2 * n]
        send_sems, recv_sems = refs[2 * n:]
        x, y, c, _ = _place()
        cps = []
        for i in range(n):
            rh = shards[i].shape[0] // 2
            mine = o_refs[i].at[pl.ds(pl.multiple_of(c * rh, 8), rh), :]
            cp = pltpu.make_async_remote_copy(
                src_ref=mine, dst_ref=mine, send_sem=send_sems.at[i], recv_sem=recv_sems.at[i],
                device_id=(x, y, 1 - c), device_id_type=MESH)
            cp.start()
            cps.append(cp)
        for i in range(n):
            rh = shards[i].shape[0] // 2
            theirs = o_refs[i].at[pl.ds(pl.multiple_of((1 - c) * rh, 8), rh), :]
            pltpu.make_async_remote_copy(
                src_ref=theirs, dst_ref=theirs, send_sem=send_sems.at[i], recv_sem=recv_sems.at[i],
                device_id=(x, y, c), device_id_type=MESH).wait_recv()
        for cp in cps:
            cp.wait_send()

    return pl.pallas_call(
        body, in_specs=[ANY] * n, out_specs=[ANY] * n,
        out_shape=[jax.ShapeDtypeStruct(h.shape, h.dtype) for h in shards],
        input_output_aliases={i: i for i in range(n)},
        scratch_shapes=[pltpu.SemaphoreType.DMA((n,)), pltpu.SemaphoreType.DMA((n,))],
        name=name)(*shards)


def _gather_all(blk, *, name, after=None):
    rows, cols = blk.shape
    extra = [] if after is None else [after]

    def body(x_ref, *refs):
        out_ref, send_sems, recv_sems, local_sem = refs[len(extra):]
        x, y, c = lax.axis_index("x"), lax.axis_index("y"), lax.axis_index("c")
        me = 4 * x + 2 * y + c
        mine = pltpu.make_async_copy(x_ref, out_ref.at[me], local_sem)
        mine.start()
        cps = []
        for k in range(1, N_DEV):
            tx = (1 - x) if (k >> 2) & 1 else x
            ty = (1 - y) if (k >> 1) & 1 else y
            tc = (1 - c) if k & 1 else c
            cp = pltpu.make_async_remote_copy(
                src_ref=x_ref, dst_ref=out_ref.at[me], send_sem=send_sems.at[k - 1],
                recv_sem=recv_sems.at[k - 1], device_id=(tx, ty, tc), device_id_type=MESH)
            cp.start()
            cps.append(cp)
        for k in range(1, N_DEV):
            tx = (1 - x) if (k >> 2) & 1 else x
            ty = (1 - y) if (k >> 1) & 1 else y
            tc = (1 - c) if k & 1 else c
            got = out_ref.at[4 * tx + 2 * ty + tc]
            pltpu.make_async_remote_copy(
                src_ref=got, dst_ref=got, send_sem=send_sems.at[k - 1], recv_sem=recv_sems.at[k - 1],
                device_id=(x, y, c), device_id_type=MESH).wait_recv()
        for cp in cps:
            cp.wait_send()
        mine.wait()

    vm = pl.BlockSpec(memory_space=pltpu.VMEM)
    return pl.pallas_call(
        body, in_specs=[vm] + [ANY] * len(extra), out_specs=vm,
        out_shape=jax.ShapeDtypeStruct((N_DEV, rows, cols), blk.dtype),
        scratch_shapes=[pltpu.SemaphoreType.DMA((N_DEV - 1,)), pltpu.SemaphoreType.DMA((N_DEV - 1,)),
                        pltpu.SemaphoreType.DMA],
        name=name)(blk, *extra)


def _as_rows(a):
    flat = a.reshape(-1)
    n = flat.shape[0]
    rows = -(-n // (8 * LANES)) * 8
    return jnp.pad(flat, (0, rows * LANES - n)).reshape(rows, LANES)


def _from_rows(p, shape):
    n = int(np.prod(shape))
    return p.reshape(-1)[:n].reshape(shape)


WEIGHT_AXES = (1, 1, 1, 0, 1, 0)
WIRE = BF16


def kernel(x, meta_tokens, w_in, w_na_out, w_hg_out, w_o, w_up, w_down, norm_mix, norm_mlp, norm_final, hg_norm, na_rpb, hg_lb_logits, loss_target, m_meta_tokens, m_w_in, m_w_na_out, m_w_hg_out, m_w_o, m_w_up, m_w_down, m_norm_mix, m_norm_mlp, m_norm_final, m_hg_norm, m_na_rpb, m_hg_lb_logits, v_meta_tokens, v_w_in, v_w_na_out, v_w_hg_out, v_w_o, v_w_up, v_w_down, v_norm_mix, v_norm_mlp, v_norm_final, v_hg_norm, v_na_rpb, v_hg_lb_logits):
    xi, yi, ci = lax.axis_index("x"), lax.axis_index("y"), lax.axis_index("c")
    chip = 2 * xi + yi
    d = x.shape[-1]
    dshard = meta_tokens.shape[1]
    hgw = hg_norm.shape[1]
    lbs = hg_lb_logits.shape[2]
    big = [w_in[0], w_na_out[0], w_hg_out[0], w_o[0], w_up[0], w_down[0]]
    big_m = [m_w_in[0], m_w_na_out[0], m_w_hg_out[0], m_w_o[0], m_w_up[0], m_w_down[0]]
    big_v = [v_w_in[0], v_w_na_out[0], v_w_hg_out[0], v_w_o[0], v_w_up[0], v_w_down[0]]

    place = jnp.stack([chip, ci]).astype(jnp.int32)
    own_w = [_cast_into_full(w, ax, place, name=f"cast_shard_{i}")
             for i, (w, ax) in enumerate(zip(big, WEIGHT_AXES))]
    in_axes, rest_axes = WEIGHT_AXES[:1], WEIGHT_AXES[1:]
    small_in = jnp.concatenate([_as_rows(meta_tokens), _as_rows(hg_lb_logits)], axis=0)
    small_all = _gather_all(small_in, name="gather_small_params")[0::2]
    in_send, in_recv, in_bufs, in_token = _allgather_start(own_w[:1], in_axes, small_all,
                                                           name="weight_allgather_in_start")
    ag_send, ag_recv, ag_bufs, ag_token = _allgather_start(own_w[1:], rest_axes, in_token,
                                                           name="weight_allgather_rest_start")

    def first_weight(after):
        got = _allgather_wait(in_send, in_recv, in_bufs, in_axes, after,
                              name="weight_allgather_in_wait")
        return _allgather_forward(got, in_axes, name="weight_allgather_in_forward")[0]

    def rest_weights(after):
        got = _allgather_wait(ag_send, ag_recv, ag_bufs, rest_axes, after,
                              name="weight_allgather_rest_wait")
        return _allgather_forward(got, rest_axes, name="weight_allgather_rest_forward")

    n_meta_rows = N_META * dshard // LANES
    meta_full = (small_all[:, :n_meta_rows].reshape(N_CHIPS, N_META, dshard)
                 .transpose(1, 0, 2).reshape(N_META, d))
    lbl_full = (small_all[:, n_meta_rows:].reshape(N_CHIPS, -1)[:, :4 * lbs]
                .reshape(N_CHIPS, 2, 2, lbs).transpose(1, 2, 0, 3).reshape(2, 2, N_CHIPS * lbs))
    lb = jax.nn.softmax(lbl_full, axis=1)[:, 0]

    c_arr = ci.reshape(1).astype(jnp.int32)

    def chip_partials(dws, axes, tag):
        g3 = [g.reshape(1, *g.shape) if ax == 1
              else g.reshape(N_CHIPS, g.shape[0] // N_CHIPS, g.shape[1]) for g, ax in zip(dws, axes)]
        rx = _sibling_swap(g3, name=f"grad_sibling_swap_{tag}")
        return [_pair_add(g, r, c_arr, out_dtype=WIRE, name=f"grad_pair_add_{tag}_{i}")
                for i, (g, r) in enumerate(zip(g3, rx))]

    flying = {}

    def scatter_behind(tag, axes):
        def hook(dws):
            send, recv, parts, slots, token = _scatter_start(
                chip_partials(dws, axes, tag), axes, name=f"grad_scatter_{tag}_start")
            flying[tag] = (send, recv, parts, slots)
            return token
        return hook

    def landed(tag, axes, after):
        parts, slots = _scatter_wait(*flying[tag], axes, after, name=f"grad_scatter_{tag}_wait")
        return [_sum_chips(q, p, place, ax, name=f"grad_sum_chips_{tag}_{i}")
                for i, (q, p, ax) in enumerate(zip(slots, parts, axes))]

    (loss, dx, dmeta, *_, dg_mix, dg_mlp, dg_fin, d_gain, d_rpb, d_lb) = _local_step(
        x[0], loss_target[0], meta_full, first_weight, rest_weights,
        norm_mix + ag_token[0:1, 0:1], norm_mlp, norm_final.reshape(1, d), hg_norm, na_rpb[0], lb,
        scatter_behind("rest", rest_axes), lambda dw_in: scatter_behind("in", in_axes)([dw_in]))

    halves_rest = landed("rest", rest_axes, dx)
    halves_in = landed("in", in_axes, halves_rest[-1])
    g_big = _sibling_share(halves_in + halves_rest, name="grad_sibling_share")

    d_rpb_c = d_rpb[:, :2 * NA_WIN_W - 1]
    small_g = [dmeta, dg_mix, dg_mlp, dg_fin, d_gain, d_rpb_c, d_lb, loss]
    packed = jnp.concatenate([_as_rows(a) for a in small_g], axis=0)
    total = _sum_slots(_gather_all(packed, after=halves_in[0], name="gather_small_grads"),
                       name="sum_small_grads")
    offs = np.cumsum([0] + [_as_rows(a).shape[0] for a in small_g])
    take = lambda i, shape: _from_rows(total[offs[i]:offs[i + 1]], shape)
    g_meta_full = take(0, (N_META, d))
    g_norm_mix, g_norm_mlp = take(1, (1, d)), take(2, (1, d))
    g_norm_final = take(3, (d,))
    g_hg_norm = take(4, (1, hgw))
    g_rpb = take(5, na_rpb.shape)
    g_lb = take(6, (2, hgw))
    loss_total = take(7, (1, LANES))[0, 0]
    g_meta = lax.dynamic_slice_in_dim(g_meta_full, chip * dshard, dshard, axis=1)
    dl0 = lb * (1.0 - lb) * g_lb
    g_lbl_full = jnp.stack([dl0, -dl0], axis=1)
    g_lbl = lax.dynamic_slice_in_dim(g_lbl_full, chip * lbs, lbs, axis=2)

    big_out = [_adamw(w, g, m, v, name=f"adamw_{i}")
               for i, (w, g, m, v) in enumerate(zip(big, g_big, big_m, big_v))]
    small_w = [meta_tokens, norm_mix, norm_mlp, norm_final, hg_norm, na_rpb, hg_lb_logits]
    small_gr = [g_meta, g_norm_mix, g_norm_mlp, g_norm_final, g_hg_norm, g_rpb, g_lbl]
    small_m = [m_meta_tokens, m_norm_mix, m_norm_mlp, m_norm_final, m_hg_norm, m_na_rpb, m_hg_lb_logits]
    small_v = [v_meta_tokens, v_norm_mix, v_norm_mlp, v_norm_final, v_hg_norm, v_na_rpb, v_hg_lb_logits]
    pk = lambda lst: jnp.concatenate([_as_rows(a) for a in lst], axis=0)
    sd, sm, sv = _adamw(pk(small_w), pk(small_gr), pk(small_m), pk(small_v), name="adamw_small")
    soffs = np.cumsum([0] + [_as_rows(a).shape[0] for a in small_w])
    unpk = lambda p: [_from_rows(p[soffs[i]:soffs[i + 1]], small_w[i].shape) for i in range(len(small_w))]
    sd, sm, sv = unpk(sd), unpk(sm), unpk(sv)

    def order(bigs, smalls):
        return [smalls[0]] + [b.reshape(1, *b.shape) for b in bigs] + smalls[1:]

    grads = order(g_big, small_gr)
    deltas = order([o[0] for o in big_out], sd)
    new_m = order([o[1] for o in big_out], sm)
    new_v = order([o[2] for o in big_out], sv)
    return (loss_total, dx.reshape(1, *dx.shape), *grads, *deltas, *new_m, *new_v)
```

```python
import functools

import numpy as np
import jax
import jax.numpy as jnp
from jax import lax
from jax.experimental import pallas as pl
from jax.experimental.pallas import tpu as pltpu

F32 = jnp.float32
BF16 = jnp.bfloat16
HIGHEST = lax.Precision.HIGHEST

GRID_W = 64
N_META = 16
EPS = 1e-6
NA_HEAD_DIM = 64
NA_WIN_H = 8
NA_WIN_W = 16
HG_DK = 128
HG_CHUNK = 16
LANES = 128
ROW_ALIGN = 128
VMEM_LIMIT = 48 * 1024 * 1024

ADAM_LR = 0.001
ADAM_B1 = 0.9
ADAM_B2 = 0.999
ADAM_EPS = 1e-08
ADAM_WD = 0.01
ADAM_STEP = 10

MESH = pl.DeviceIdType.MESH


def _cp(*sem):
    return pltpu.CompilerParams(dimension_semantics=sem, vmem_limit_bytes=VMEM_LIMIT)


def _sigmoid(x):
    return 1.0 / (1.0 + jnp.exp(-x))


def _dot(a, b, dims, precision=None):
    return lax.dot_general(a, b, (dims, ((), ())), preferred_element_type=F32, precision=precision)


def _nn(a, b, **kw):
    return _dot(a, b, ((1,), (0,)), **kw)


def _nt(a, b, **kw):
    return _dot(a, b, ((1,), (1,)), **kw)


def _tn(a, b, **kw):
    return _dot(a, b, ((0,), (0,)), **kw)


def _matmul(a, b, *, ta=False, tb=False, tm=None, tn=None, tk=None, out_dtype=F32, name,
            precision=None, after=None, epilogue=None, tiles=(), out_dtypes=None):
    extra = [] if after is None else [after]
    single = out_dtypes is None
    if single:
        out_dtypes = (out_dtype,)
    n_t, n_o = len(tiles), len(out_dtypes)
    if ta:
        kdim, m = a.shape
    else:
        m, kdim = a.shape
    if tb:
        n, k2 = b.shape
    else:
        k2, n = b.shape
    assert kdim == k2, (a.shape, b.shape, ta, tb)
    if tm is None:
        if ta:
            tm = next(t for t in (512, 256, 128, m) if m % t == 0)
        else:
            tm = m // 2 if (m // 2) % 16 == 0 and m > 512 else m
    if tn is None:
        tn = next(t for t in (512, 256, 128, n) if n % t == 0)
    if tk is None:
        tk = kdim if ta else next(t for t in (1024, 512, 256, 128, kdim) if kdim % t == 0)
    assert m % tm == 0 and n % tn == 0 and kdim % tk == 0, (m, n, kdim, tm, tn, tk)
    nk = kdim // tk
    op_dtype = F32 if precision is not None else BF16

    def body(a_ref, b_ref, *refs):
        t_refs = refs[:n_t]
        o_refs = refs[n_t + len(extra):n_t + len(extra) + n_o]
        acc_ref = refs[-1]
        kk = pl.program_id(2)

        @pl.when(kk == 0)
        def _():
            acc_ref[...] = jnp.zeros_like(acc_ref)

        av = a_ref[...].astype(op_dtype)
        bv = b_ref[...].astype(op_dtype)
        dims = ((0 if ta else 1,), (1 if tb else 0,))
        acc_ref[...] += _dot(av, bv, dims, precision=precision)

        @pl.when(kk == nk - 1)
        def _():
            acc = acc_ref[...]
            outs = (acc,) if epilogue is None else epilogue(acc, *[t[...] for t in t_refs])
            for o_ref, val in zip(o_refs, outs):
                o_ref[...] = val.astype(o_ref.dtype)

    a_spec = (pl.BlockSpec((tk, tm), lambda i, j, k: (k, i)) if ta
              else pl.BlockSpec((tm, tk), lambda i, j, k: (i, k)))
    b_spec = (pl.BlockSpec((tn, tk), lambda i, j, k: (j, k)) if tb
              else pl.BlockSpec((tk, tn), lambda i, j, k: (k, j)))
    for _, off in tiles:
        assert off % tn == 0, (off, tn)
    t_specs = [pl.BlockSpec((tm, tn), functools.partial(lambda i, j, k, o: (i, o + j), o=off // tn))
               for _, off in tiles]
    o_spec = pl.BlockSpec((tm, tn), lambda i, j, k: (i, j))
    outs = pl.pallas_call(
        body,
        grid=(m // tm, n // tn, nk),
        in_specs=[a_spec, b_spec] + t_specs + [pl.BlockSpec(memory_space=pl.ANY)] * len(extra),
        out_specs=[o_spec] * n_o,
        out_shape=[jax.ShapeDtypeStruct((m, n), dt) for dt in out_dtypes],
        scratch_shapes=[pltpu.VMEM((tm, tn), F32)],
        compiler_params=_cp("parallel", "parallel", "arbitrary"),
        name=name,
    )(a, b, *[t for t, _ in tiles], *extra)
    return outs[0] if single else outs


def _rspec(tr, w, cb=0):
    return pl.BlockSpec((tr, w), lambda i: (i, cb))


def _fspec(shape):
    nd = len(shape)
    return pl.BlockSpec(shape, lambda i: (0,) * nd)


def _row_tile(lp):
    return ROW_ALIGN if lp % ROW_ALIGN == 0 else lp


def _rmsnorm_fwd(x, g, *, name):
    lp, d = x.shape
    tr = _row_tile(lp)

    def body(x_ref, g_ref, o_ref):
        xv = x_ref[...]
        r = lax.rsqrt(jnp.mean(xv * xv, axis=-1, keepdims=True) + EPS)
        o_ref[...] = (xv * r * g_ref[...]).astype(BF16)

    return pl.pallas_call(
        body, grid=(lp // tr,),
        in_specs=[_rspec(tr, d), _fspec((1, d))],
        out_specs=_rspec(tr, d),
        out_shape=jax.ShapeDtypeStruct((lp, d), BF16),
        compiler_params=_cp("parallel"), name=name)(x, g)


def _residual_norm(h, t, g, *, name):
    lp, d = h.shape
    tr = _row_tile(lp)

    def body(h_ref, t_ref, g_ref, h1_ref, m_ref):
        xv = h_ref[...] + t_ref[...]
        h1_ref[...] = xv
        r = lax.rsqrt(jnp.mean(xv * xv, axis=-1, keepdims=True) + EPS)
        m_ref[...] = (xv * r * g_ref[...]).astype(BF16)

    return pl.pallas_call(
        body, grid=(lp // tr,),
        in_specs=[_rspec(tr, d), _rspec(tr, d), _fspec((1, d))],
        out_specs=[_rspec(tr, d), _rspec(tr, d)],
        out_shape=[jax.ShapeDtypeStruct((lp, d), F32), jax.ShapeDtypeStruct((lp, d), BF16)],
        compiler_params=_cp("parallel"), name=name)(h, t, g)


def _rmsnorm_bwd_add(x, g, dy, dres, *, name):
    lp, d = x.shape
    tr = _row_tile(lp)

    def body(x_ref, g_ref, dy_ref, dr_ref, dx_ref, dg_ref):
        @pl.when(pl.program_id(0) == 0)
        def _():
            dg_ref[...] = jnp.zeros_like(dg_ref)

        xv = x_ref[...]
        r = lax.rsqrt(jnp.mean(xv * xv, axis=-1, keepdims=True) + EPS)
        xh = xv * r
        dyv = dy_ref[...]
        dg_ref[...] += jnp.sum(dyv * xh, axis=0, keepdims=True)
        dxh = dyv * g_ref[...]
        dx_ref[...] = dr_ref[...] + r * (dxh - xh * jnp.mean(dxh * xh, axis=-1, keepdims=True))

    return pl.pallas_call(
        body, grid=(lp // tr,),
        in_specs=[_rspec(tr, d), _fspec((1, d)), _rspec(tr, d), _rspec(tr, d)],
        out_specs=[_rspec(tr, d), _fspec((1, d))],
        out_shape=[jax.ShapeDtypeStruct((lp, d), F32), jax.ShapeDtypeStruct((1, d), F32)],
        compiler_params=_cp("arbitrary"), name=name)(x, g, dy, dres)


def _final_loss(h1, t2, g, tgt, *, n_tok, name):
    lp, d = h1.shape
    tr = _row_tile(lp)

    def body(h_ref, t_ref, g_ref, tg_ref, dh_ref, loss_ref, dg_ref):
        i = pl.program_id(0)

        @pl.when(i == 0)
        def _():
            loss_ref[...] = jnp.zeros_like(loss_ref)
            dg_ref[...] = jnp.zeros_like(dg_ref)

        xv = h_ref[...] + t_ref[...]
        r = lax.rsqrt(jnp.mean(xv * xv, axis=-1, keepdims=True) + EPS)
        xh = xv * r
        gv = g_ref[...]
        row = i * tr + lax.broadcasted_iota(jnp.int32, (tr, 1), 0)
        valid = (row >= N_META) & (row < N_META + n_tok)
        err = jnp.where(valid, xh * gv - tg_ref[...], 0.0)
        loss_ref[...] += jnp.sum(0.5 * err * err) / d
        dy = err / d
        dg_ref[...] += jnp.sum(dy * xh, axis=0, keepdims=True)
        dxh = dy * gv
        dh_ref[...] = r * (dxh - xh * jnp.mean(dxh * xh, axis=-1, keepdims=True))

    return pl.pallas_call(
        body, grid=(lp // tr,),
        in_specs=[_rspec(tr, d), _rspec(tr, d), _fspec((1, d)), _rspec(tr, d)],
        out_specs=[_rspec(tr, d), _fspec((1, LANES)), _fspec((1, d))],
        out_shape=[jax.ShapeDtypeStruct((lp, d), F32), jax.ShapeDtypeStruct((1, LANES), F32),
                   jax.ShapeDtypeStruct((1, d), F32)],
        compiler_params=_cp("arbitrary"), name=name)(h1, t2, g, tgt)


def _hg_out(o_f, o_b, proj, gain, *, col_g, name):
    lp, w = o_f.shape
    tr = _row_tile(lp)
    hh = w // HG_DK

    def body(of_ref, ob_ref, g_ref, gain_ref, y_ref):
        gv = g_ref[...]
        sg = gv * _sigmoid(gv)
        for h in range(hh):
            sl = slice(h * HG_DK, (h + 1) * HG_DK)
            o = of_ref[:, sl] + ob_ref[:, sl]
            r = lax.rsqrt(jnp.mean(o * o, axis=-1, keepdims=True) + EPS)
            y_ref[:, sl] = (o * r * gain_ref[:, sl] * sg[:, sl]).astype(BF16)

    return pl.pallas_call(
        body, grid=(lp // tr,),
        in_specs=[_rspec(tr, w), _rspec(tr, w), _rspec(tr, w, col_g // w), _fspec((1, w))],
        out_specs=_rspec(tr, w),
        out_shape=jax.ShapeDtypeStruct((lp, w), BF16),
        compiler_params=_cp("parallel"), name=name)(o_f, o_b, proj, gain)


def _hg_out_bwd(o_f, o_b, proj, gain, dy, *, col_g, name):
    lp, w = o_f.shape
    tr = _row_tile(lp)
    hh = w // HG_DK

    def body(of_ref, ob_ref, g_ref, gain_ref, dy_ref, do_ref, dg_ref, dgain_ref):
        @pl.when(pl.program_id(0) == 0)
        def _():
            dgain_ref[...] = jnp.zeros_like(dgain_ref)

        for h in range(hh):
            sl = slice(h * HG_DK, (h + 1) * HG_DK)
            gv = g_ref[:, sl]
            s = _sigmoid(gv)
            sg = gv * s
            dsg = s + gv * s * (1.0 - s)
            o = of_ref[:, sl] + ob_ref[:, sl]
            r = lax.rsqrt(jnp.mean(o * o, axis=-1, keepdims=True) + EPS)
            on = o * r
            dyv = dy_ref[:, sl]
            gn = gain_ref[:, sl]
            dgain_ref[:, sl] += jnp.sum(dyv * on * sg, axis=0, keepdims=True)
            dg_ref[:, sl] = (dyv * on * gn * dsg).astype(BF16)
            don = dyv * gn * sg
            do_ref[:, sl] = r * (don - on * jnp.mean(don * on, axis=-1, keepdims=True))

    return pl.pallas_call(
        body, grid=(lp // tr,),
        in_specs=[_rspec(tr, w), _rspec(tr, w), _rspec(tr, w, col_g // w), _fspec((1, w)),
                  _rspec(tr, w)],
        out_specs=[_rspec(tr, w), _rspec(tr, w), _fspec((1, w))],
        out_shape=[jax.ShapeDtypeStruct((lp, w), F32), jax.ShapeDtypeStruct((lp, w), BF16),
                   jax.ShapeDtypeStruct((1, w), F32)],
        compiler_params=_cp("arbitrary"), name=name)(o_f, o_b, proj, gain, dy)


HG_GROUP = 8
HG_ROWS = HG_GROUP * HG_CHUNK


def _hg_gates(zq, z, lbv):
    qh = zq * _sigmoid(zq)
    s = _sigmoid(z)
    f = lbv + (1.0 - lbv) * s
    kk = (1.0 - lbv) * _sigmoid(-z)
    return qh, s, f, jnp.log(f), kk


def _chunk_cumsum(g, pos, suffix):
    x = g
    for k in (1, 2, 4, 8):
        if suffix:
            x = x + jnp.where(pos < HG_CHUNK - k, pltpu.roll(x, HG_ROWS - k, 0), 0.0)
        else:
            x = x + jnp.where(pos >= k, pltpu.roll(x, k, 0), 0.0)
    return x


def _pair_levels(b, pos, reverse):
    out = []
    first = b
    for m in (1, 2, 4, 8):
        if m > 1:
            first = jnp.where((pos & (m - 1)) >= m // 2, pltpu.roll(first, m // 2, 0), first)
        nxt = pltpu.roll(first, HG_ROWS - m, 0)
        upper = (pos & (2 * m - 1)) >= m
        if reverse:
            eq = jnp.where(upper, 0.0, jnp.exp(jnp.minimum(b - nxt, 0.0)))
            ek = jnp.where(upper, jnp.exp(jnp.minimum(first - b, 0.0)), 0.0)
        else:
            eq = jnp.where(upper, jnp.exp(jnp.minimum(b - first, 0.0)), 0.0)
            ek = jnp.where(upper, 0.0, jnp.exp(jnp.minimum(nxt - b, 0.0)))
        out.append((m.bit_length(), eq, ek))
    return out


def _g3(a):
    return a.reshape(HG_GROUP, HG_CHUNK, a.shape[-1])


def _hg_scan_fwd(proj, lb, *, reverse, col_q, col_z, col_i, hh, name):
    lp = proj.shape[0]
    n_groups = lp // HG_ROWS
    n_chunks = lp // HG_CHUNK
    c16 = HG_CHUNK
    last = 0 if reverse else c16 - 1

    def body(q_ref, z_ref, i_ref, lb_ref, o_ref, st_ref):
        lbv = lb_ref[...]
        pos = lax.broadcasted_iota(jnp.int32, (HG_ROWS, 1), 0) & (c16 - 1)
        ri = lax.broadcasted_iota(jnp.int32, (HG_ROWS, HG_ROWS), 0)
        ci = lax.broadcasted_iota(jnp.int32, (HG_ROWS, HG_ROWS), 1)

        def group(gi, st):
            gg = (n_groups - 1 - gi) if reverse else gi
            r0 = pl.multiple_of(gg * HG_ROWS, HG_ROWS)
            v = i_ref[pl.ds(r0, HG_ROWS), :]
            qh, _, _, g, kk = _hg_gates(q_ref[pl.ds(r0, HG_ROWS), :], z_ref[pl.ds(r0, HG_ROWS), :],
                                        lbv)
            b = _chunk_cumsum(g, pos, reverse)
            b3, kk3 = _g3(b), _g3(kk)
            bl3 = b3[:, last:last + 1, :]
            qe = (qh * jnp.exp(b)).astype(BF16)
            kd = (kk3 * jnp.exp(bl3 - b3)).reshape(HG_ROWS, HG_DK).astype(BF16)
            decay3 = jnp.exp(bl3)
            v16 = v.astype(BF16)
            a = jnp.where(ri == ci, jnp.sum(qh * kk, axis=1, keepdims=True), 0.0)
            for sh, eq, ek in _pair_levels(b, pos, reverse):
                a = a + jnp.where((ri >> sh) == (ci >> sh),
                                  _nt((qh * eq).astype(BF16), (kk * ek).astype(BF16)), 0.0)
            intra = _nn(a.astype(BF16), v16)
            inter = [None] * HG_GROUP
            for c in (reversed(range(HG_GROUP)) if reverse else range(HG_GROUP)):
                sl = slice(c * c16, (c + 1) * c16)
                st_ref[gg * HG_GROUP + c] = st
                inter[c] = _nt(qe[sl], st.astype(BF16))
                st = decay3[c] * st + _tn(v16[sl], kd[sl])
            o_ref[pl.ds(r0, HG_ROWS), :] = intra + jnp.concatenate(inter, axis=0)
            return st

        lax.fori_loop(0, n_groups, group, jnp.zeros((HG_DK, HG_DK), F32))

    cspec = lambda col: pl.BlockSpec((lp, HG_DK), lambda h: (0, col // HG_DK + h))
    return pl.pallas_call(
        body, grid=(hh,),
        in_specs=[cspec(col_q), cspec(col_z), cspec(col_i),
                  pl.BlockSpec((None, 1, HG_DK), lambda h: (h, 0, 0))],
        out_specs=[pl.BlockSpec((lp, HG_DK), lambda h: (0, h)),
                   pl.BlockSpec((None, n_chunks, HG_DK, HG_DK), lambda h: (h, 0, 0, 0))],
        out_shape=[jax.ShapeDtypeStruct((lp, hh * HG_DK), F32),
                   jax.ShapeDtypeStruct((hh, n_chunks, HG_DK, HG_DK), F32)],
        compiler_params=_cp("parallel"), name=name)(proj, proj, proj, lb)


def _hg_scan_bwd(proj, lb, states, do, *, reverse, col_q, col_z, col_i, hh, name):
    lp = proj.shape[0]
    n_groups = lp // HG_ROWS
    n_chunks = lp // HG_CHUNK
    c16 = HG_CHUNK
    last = 0 if reverse else c16 - 1

    def body(q_ref, z_ref, i_ref, lb_ref, st_ref, do_ref, dq_ref, dz_ref, dv_ref, dlb_ref):
        lbv = lb_ref[...]
        pos = lax.broadcasted_iota(jnp.int32, (HG_ROWS, 1), 0) & (c16 - 1)
        ri = lax.broadcasted_iota(jnp.int32, (HG_ROWS, HG_ROWS), 0)
        ci = lax.broadcasted_iota(jnp.int32, (HG_ROWS, HG_ROWS), 1)

        def group(gi, carry):
            dst, dlb = carry
            gg = gi if reverse else (n_groups - 1 - gi)
            r0 = pl.multiple_of(gg * HG_ROWS, HG_ROWS)
            zq = q_ref[pl.ds(r0, HG_ROWS), :]
            v = i_ref[pl.ds(r0, HG_ROWS), :]
            dov = do_ref[pl.ds(r0, HG_ROWS), :]
            qh, s, f, g, kk = _hg_gates(zq, z_ref[pl.ds(r0, HG_ROWS), :], lbv)
            b = _chunk_cumsum(g, pos, reverse)
            b3, kk3 = _g3(b), _g3(kk)
            bl3 = b3[:, last:last + 1, :]
            eb = jnp.exp(b)
            ebl3 = jnp.exp(bl3 - b3)
            decay3 = jnp.exp(bl3)
            qe16 = (qh * eb).astype(BF16)
            kd16 = (kk3 * ebl3).reshape(HG_ROWS, HG_DK).astype(BF16)
            v16, do16 = v.astype(BF16), dov.astype(BF16)
            do_s, v_ds, dv_st, dbl_st = ([None] * HG_GROUP for _ in range(4))
            for c in (range(HG_GROUP) if reverse else reversed(range(HG_GROUP))):
                sl = slice(c * c16, (c + 1) * c16)
                st = st_ref[gg * HG_GROUP + c]
                st16, dst16 = st.astype(BF16), dst.astype(BF16)
                do_s[c] = _nn(do16[sl], st16)
                v_ds[c] = _nn(v16[sl], dst16)
                dv_st[c] = _nt(kd16[sl], dst16)
                dbl_st[c] = decay3[c] * jnp.sum(st * dst, axis=0, keepdims=True)
                dst = decay3[c] * dst + _tn(do16[sl], qe16[sl])
            cat = lambda parts: jnp.concatenate(parts, axis=0)
            same_row = ri == ci
            da = _nt(do16, v16)
            da_diag = jnp.sum(jnp.where(same_row, da, 0.0), axis=1, keepdims=True)
            a = jnp.where(same_row, jnp.sum(qh * kk, axis=1, keepdims=True), 0.0)
            dq = eb * cat(do_s) + da_diag * kk
            dk_state = ebl3.reshape(HG_ROWS, HG_DK) * cat(v_ds)
            dk = dk_state + da_diag * qh
            for sh, eq, ek in _pair_levels(b, pos, reverse):
                same = (ri >> sh) == (ci >> sh)
                q16, k16 = (qh * eq).astype(BF16), (kk * ek).astype(BF16)
                a = a + jnp.where(same, _nt(q16, k16), 0.0)
                da16 = jnp.where(same, da, 0.0).astype(BF16)
                dq = dq + eq * _nn(da16, k16)
                dk = dk + ek * _tn(da16, q16)
            dv = cat(dv_st) + _tn(a.astype(BF16), do16)
            dbl3 = (jnp.concatenate([d[None] for d in dbl_st], axis=0)
                    + jnp.sum(_g3(kk * dk_state), axis=1, keepdims=True))
            dbl = jnp.broadcast_to(dbl3, (HG_GROUP, c16, HG_DK)).reshape(HG_ROWS, HG_DK)
            db = qh * dq - kk * dk + jnp.where(pos == last, dbl, 0.0)
            dg = _chunk_cumsum(db, pos, not reverse)
            df = dg / f - dk
            sq = _sigmoid(zq)
            dq_ref[pl.ds(r0, HG_ROWS), :] = dq * (sq + zq * sq * (1.0 - sq))
            dz_ref[pl.ds(r0, HG_ROWS), :] = df * (1.0 - lbv) * s * (1.0 - s)
            dv_ref[pl.ds(r0, HG_ROWS), :] = dv
            return dst, dlb + jnp.sum(df * (1.0 - s), axis=0, keepdims=True)

        _, dlb = lax.fori_loop(0, n_groups, group,
                               (jnp.zeros((HG_DK, HG_DK), F32), jnp.zeros((1, HG_DK), F32)))
        dlb_ref[...] = dlb

    cspec = lambda col: pl.BlockSpec((lp, HG_DK), lambda h: (0, col // HG_DK + h))
    ospec = pl.BlockSpec((lp, HG_DK), lambda h: (0, h))
    sds = jax.ShapeDtypeStruct((lp, hh * HG_DK), F32)
    return pl.pallas_call(
        body, grid=(hh,),
        in_specs=[cspec(col_q), cspec(col_z), cspec(col_i),
                  pl.BlockSpec((None, 1, HG_DK), lambda h: (h, 0, 0)),
                  pl.BlockSpec((None, n_chunks, HG_DK, HG_DK), lambda h: (h, 0, 0, 0)),
                  ospec],
        out_specs=[ospec, ospec, ospec, pl.BlockSpec((None, 1, HG_DK), lambda h: (h, 0, 0))],
        out_shape=[sds, sds, sds, jax.ShapeDtypeStruct((hh, 1, HG_DK), F32)],
        compiler_params=_cp("parallel"), name=name)(proj, proj, proj, lb, states, do)


def _na_rows(r, rows):
    rs = jnp.clip(r - NA_WIN_H // 2, 0, rows - NA_WIN_H)
    i0 = rs - r + (NA_WIN_H - 1)
    q0 = pl.multiple_of(N_META + GRID_W * r, 16)
    k0 = pl.multiple_of(N_META + GRID_W * rs, 16)
    return i0, q0, k0


def _na_scores(q16, k16, km16, tb_ref, i0, scale):
    s = _nt(q16, k16) * scale
    bias = jnp.concatenate([tb_ref[i0 + j] for j in range(NA_WIN_H)], axis=1)
    return s + bias, _nt(q16, km16) * scale


NA_HB = LANES // NA_HEAD_DIM


def _na_cols(h):
    return slice(h * NA_HEAD_DIM, (h + 1) * NA_HEAD_DIM)


def _na_fwd(proj, tb, *, n_tok, nh, name):
    lp = proj.shape[0]
    dh, hb = NA_HEAD_DIM, NA_HB
    naw = nh * dh
    rows = n_tok // GRID_W
    scale = dh ** -0.5
    kw = NA_WIN_H * GRID_W

    def body(q_ref, k_ref, v_ref, tb_ref, o_ref, lse_ref, q16_ref, k16_ref, v16_ref):
        o_ref[...] = jnp.zeros_like(o_ref)
        lse_ref[...] = jnp.zeros_like(lse_ref)
        q16_ref[...] = q_ref[...].astype(BF16)
        k16_ref[...] = k_ref[...].astype(BF16)
        v16_ref[...] = v_ref[...].astype(BF16)
        kms, vms = [], []
        for h in range(hb):
            km = k16_ref[0:N_META, _na_cols(h)]
            vm = v16_ref[0:N_META, _na_cols(h)]
            s = _nt(q16_ref[0:N_META, _na_cols(h)], km) * scale
            m = jnp.max(s, axis=1, keepdims=True)
            p = jnp.exp(s - m)
            l = jnp.sum(p, axis=1, keepdims=True)
            o_ref[0:N_META, _na_cols(h)] = _nn(p.astype(BF16), vm) / l
            lse_ref[h, 0:N_META, :] = m + jnp.log(l)
            kms.append(km)
            vms.append(vm)

        def step(r, carry):
            i0, q0, k0 = _na_rows(r, rows)
            for h in range(hb):
                q16 = q16_ref[pl.ds(q0, GRID_W), _na_cols(h)]
                k16 = k16_ref[pl.ds(k0, kw), _na_cols(h)]
                v16 = v16_ref[pl.ds(k0, kw), _na_cols(h)]
                s, sm = _na_scores(q16, k16, kms[h], tb_ref.at[h], i0, scale)
                m = jnp.maximum(jnp.max(s, axis=1, keepdims=True),
                                jnp.max(sm, axis=1, keepdims=True))
                p = jnp.exp(s - m)
                pm = jnp.exp(sm - m)
                l = jnp.sum(p, axis=1, keepdims=True) + jnp.sum(pm, axis=1, keepdims=True)
                o = _nn(p.astype(BF16), v16) + _nn(pm.astype(BF16), vms[h])
                o_ref[pl.ds(q0, GRID_W), _na_cols(h)] = o / l
                lse_ref[h, pl.ds(q0, GRID_W), :] = m + jnp.log(l)
            return carry

        lax.fori_loop(0, rows, step, 0, unroll=2)

    cblk = lambda col: pl.BlockSpec((lp, LANES), lambda g: (0, col // LANES + g))
    return pl.pallas_call(
        body, grid=(nh // hb,),
        in_specs=[cblk(0), cblk(naw), cblk(2 * naw),
                  pl.BlockSpec((hb, 2 * NA_WIN_H - 1, GRID_W, GRID_W), lambda g: (g, 0, 0, 0))],
        out_specs=[cblk(0), pl.BlockSpec((hb, lp, 1), lambda g: (g, 0, 0))],
        out_shape=[jax.ShapeDtypeStruct((lp, naw), F32), jax.ShapeDtypeStruct((nh, lp, 1), F32)],
        scratch_shapes=[pltpu.VMEM((lp, LANES), BF16)] * 3,
        compiler_params=_cp("parallel"), name=name)(proj, proj, proj, tb)


def _na_bwd(proj, tb, o, lse, do, *, n_tok, nh, name):
    lp = proj.shape[0]
    dh, hb = NA_HEAD_DIM, NA_HB
    naw = nh * dh
    rows = n_tok // GRID_W
    scale = dh ** -0.5
    kw = NA_WIN_H * GRID_W

    def body(q_ref, k_ref, v_ref, tb_ref, o_ref, lse_ref, do_ref, dq_ref, dk_ref, dv_ref, dtb_ref,
             q16_ref, k16_ref, v16_ref):
        dq_ref[...] = jnp.zeros_like(dq_ref)
        dk_ref[...] = jnp.zeros_like(dk_ref)
        dv_ref[...] = jnp.zeros_like(dv_ref)
        dtb_ref[...] = jnp.zeros_like(dtb_ref)
        q16_ref[...] = q_ref[...].astype(BF16)
        k16_ref[...] = k_ref[...].astype(BF16)
        v16_ref[...] = v_ref[...].astype(BF16)
        kms, vms, init = [], [], []
        for h in range(hb):
            km = k16_ref[0:N_META, _na_cols(h)]
            vm = v16_ref[0:N_META, _na_cols(h)]
            qm = q16_ref[0:N_META, _na_cols(h)]
            dom = do_ref[0:N_META, _na_cols(h)]
            p = jnp.exp(_nt(qm, km) * scale - lse_ref[h, 0:N_META, :])
            dp = _nt(dom.astype(BF16), vm)
            delta = jnp.sum(dom * o_ref[0:N_META, _na_cols(h)], axis=1, keepdims=True)
            ds = (p * (dp - delta)).astype(BF16)
            dq_ref[0:N_META, _na_cols(h)] = _nn(ds, km) * scale
            init += [_tn(ds, qm) * scale, _tn(p.astype(BF16), dom.astype(BF16))]
            kms.append(km)
            vms.append(vm)

        def step(r, carry):
            i0, q0, k0 = _na_rows(r, rows)
            out = []
            for h in range(hb):
                q16 = q16_ref[pl.ds(q0, GRID_W), _na_cols(h)]
                k16 = k16_ref[pl.ds(k0, kw), _na_cols(h)]
                v16 = v16_ref[pl.ds(k0, kw), _na_cols(h)]
                s, sm = _na_scores(q16, k16, kms[h], tb_ref.at[h], i0, scale)
                lse = lse_ref[h, pl.ds(q0, GRID_W), :]
                p = jnp.exp(s - lse)
                pm = jnp.exp(sm - lse)
                dov = do_ref[pl.ds(q0, GRID_W), _na_cols(h)]
                do16 = dov.astype(BF16)
                delta = jnp.sum(dov * o_ref[pl.ds(q0, GRID_W), _na_cols(h)], axis=1, keepdims=True)
                ds = p * (_nt(do16, v16) - delta)
                dsm = (pm * (_nt(do16, vms[h]) - delta)).astype(BF16)
                ds16 = ds.astype(BF16)
                dq_ref[pl.ds(q0, GRID_W), _na_cols(h)] = (_nn(ds16, k16) + _nn(dsm, kms[h])) * scale
                dk_ref[pl.ds(k0, kw), _na_cols(h)] += _tn(ds16, q16) * scale
                dv_ref[pl.ds(k0, kw), _na_cols(h)] += _tn(p.astype(BF16), do16)
                for j in range(NA_WIN_H):
                    dtb_ref[h, i0 + j] += ds[:, j * GRID_W:(j + 1) * GRID_W]
                out += [carry[2 * h] + _tn(dsm, q16) * scale,
                        carry[2 * h + 1] + _tn(pm.astype(BF16), do16)]
            return tuple(out)

        fin = lax.fori_loop(0, rows, step, tuple(init))
        for h in range(hb):
            dk_ref[0:N_META, _na_cols(h)] += fin[2 * h]
            dv_ref[0:N_META, _na_cols(h)] += fin[2 * h + 1]

    cblk = lambda col: pl.BlockSpec((lp, LANES), lambda g: (0, col // LANES + g))
    tbs = pl.BlockSpec((hb, 2 * NA_WIN_H - 1, GRID_W, GRID_W), lambda g: (g, 0, 0, 0))
    sds = jax.ShapeDtypeStruct((lp, naw), F32)
    return pl.pallas_call(
        body, grid=(nh // hb,),
        in_specs=[cblk(0), cblk(naw), cblk(2 * naw), tbs, cblk(0),
                  pl.BlockSpec((hb, lp, 1), lambda g: (g, 0, 0)), cblk(0)],
        out_specs=[cblk(0), cblk(0), cblk(0), tbs],
        out_shape=[sds, sds, sds, jax.ShapeDtypeStruct(tb.shape, F32)],
        scratch_shapes=[pltpu.VMEM((lp, LANES), BF16)] * 3,
        compiler_params=_cp("parallel"), name=name)(proj, proj, proj, tb, o, lse, do)


def _rpb_onehot():
    c = np.arange(GRID_W)[:, None]
    w = np.arange(GRID_W)[None, :]
    cs = np.clip(c - NA_WIN_W // 2, 0, GRID_W - NA_WIN_W)
    in_win = (w >= cs) & (w < cs + NA_WIN_W)
    dc = np.clip(w - c, -(NA_WIN_W - 1), NA_WIN_W - 1) + NA_WIN_W - 1
    oh = np.zeros((LANES, GRID_W * GRID_W), np.float32)
    flat = np.arange(GRID_W * GRID_W).reshape(GRID_W, GRID_W)
    oh[dc[in_win], flat[in_win]] = 1.0
    neg = np.where(in_win, 0.0, -1e30).astype(np.float32).reshape(1, -1)
    return oh, neg


def _assemble_dproj(dq_na, dk_na, dv_na, dq_f, dq_b, dz_f, dz_b, dv_f, dv_b, dg, dgn, dgh, *, name):
    lp, naw = dq_na.shape
    hgw = dq_f.shape[1]
    d = dgn.shape[1]
    cols = 3 * naw + 5 * hgw + 2 * d
    tr = _row_tile(lp)

    def body(nq_ref, nk_ref, nv_ref, qf_ref, qb_ref, zf_ref, zb_ref, vf_ref, vb_ref, g_ref, gn_ref,
             gh_ref, o_ref):
        o_ref[:, 0:naw] = nq_ref[...].astype(BF16)
        o_ref[:, naw:2 * naw] = nk_ref[...].astype(BF16)
        o_ref[:, 2 * naw:3 * naw] = nv_ref[...].astype(BF16)
        c0 = 3 * naw
        o_ref[:, c0:c0 + hgw] = (qf_ref[...] + qb_ref[...]).astype(BF16)
        o_ref[:, c0 + hgw:c0 + 2 * hgw] = zf_ref[...].astype(BF16)
        o_ref[:, c0 + 2 * hgw:c0 + 3 * hgw] = zb_ref[...].astype(BF16)
        o_ref[:, c0 + 3 * hgw:c0 + 4 * hgw] = (vf_ref[...] + vb_ref[...]).astype(BF16)
        o_ref[:, c0 + 4 * hgw:c0 + 5 * hgw] = g_ref[...]
        o_ref[:, c0 + 5 * hgw:c0 + 5 * hgw + d] = gn_ref[...]
        o_ref[:, c0 + 5 * hgw + d:] = gh_ref[...]

    hg, na = _rspec(tr, hgw), _rspec(tr, naw)
    return pl.pallas_call(
        body, grid=(lp // tr,),
        in_specs=[na, na, na, hg, hg, hg, hg, hg, hg, hg, _rspec(tr, d), _rspec(tr, d)],
        out_specs=_rspec(tr, cols),
        out_shape=jax.ShapeDtypeStruct((lp, cols), BF16),
        compiler_params=_cp("parallel"), name=name)(dq_na, dk_na, dv_na, dq_f, dq_b, dz_f, dz_b,
                                                    dv_f, dv_b, dg, dgn, dgh)


def _adamw(w, g, m, v, *, name):
    rows, cols = w.shape
    tr = 256 if rows % 256 == 0 else rows

    def body(w_ref, g_ref, m_ref, v_ref, d_ref, mo_ref, vo_ref):
        gv = g_ref[...]
        mn = ADAM_B1 * m_ref[...] + (1.0 - ADAM_B1) * gv
        vn = ADAM_B2 * v_ref[...] + (1.0 - ADAM_B2) * (gv * gv)
        m_hat = mn / (1.0 - ADAM_B1 ** ADAM_STEP)
        v_hat = vn / (1.0 - ADAM_B2 ** ADAM_STEP)
        d_ref[...] = -ADAM_LR * (m_hat / (jnp.sqrt(v_hat) + ADAM_EPS) + ADAM_WD * w_ref[...])
        mo_ref[...] = mn
        vo_ref[...] = vn

    spec = _rspec(tr, cols)
    sds = jax.ShapeDtypeStruct((rows, cols), F32)
    return pl.pallas_call(
        body, grid=(rows // tr,), in_specs=[spec] * 4, out_specs=[spec] * 3, out_shape=[sds] * 3,
        compiler_params=_cp("parallel"), name=name)(w, g, m, v)


def _local_step(x, tgt, meta, first_weight, rest_weights, g_mix, g_mlp, g_fin, hg_gain, rpb, lb,
                early_grads=None, late_grad=None):
    n_tok, d = x.shape
    hgw = hg_gain.shape[1]
    nh, hh = rpb.shape[0], hgw // HG_DK
    naw = nh * NA_HEAD_DIM
    l_real = N_META + n_tok
    lp = -(-l_real // ROW_ALIGN) * ROW_ALIGN
    n_chunks = l_real // HG_CHUNK
    pad = lp - l_real
    col_qhg = 3 * naw
    col_zf, col_zb, col_i, col_g = (col_qhg + hgw, col_qhg + 2 * hgw, col_qhg + 3 * hgw,
                                    col_qhg + 4 * hgw)
    col_gate = col_qhg + 5 * hgw

    zpad = jnp.zeros((pad, d), F32)
    h0 = jnp.concatenate([meta, x, zpad], axis=0)
    tgt_p = jnp.concatenate([jnp.zeros((N_META, d), F32), tgt, zpad], axis=0)

    oh_np, neg_np = _rpb_onehot()
    oh = jnp.asarray(oh_np)
    rpb_p = jnp.pad(rpb.reshape(nh * (2 * NA_WIN_H - 1), 2 * NA_WIN_W - 1),
                    ((0, 0), (0, LANES - (2 * NA_WIN_W - 1))))
    tb = _matmul(rpb_p, oh, tm=rpb_p.shape[0], tn=512, tk=LANES, precision=HIGHEST,
                 name="rpb_expand")
    tb = (tb + jnp.asarray(neg_np)).reshape(nh, 2 * NA_WIN_H - 1, GRID_W, GRID_W)

    a = _rmsnorm_fwd(h0, g_mix, name="norm_mix")
    w_in = first_weight(a)
    proj = _matmul(a, w_in, name="mm_in")
    o_na, lse = _na_fwd(proj, tb, n_tok=n_tok, nh=nh, name="na_fwd")
    lb_f = lb[0].reshape(hh, 1, HG_DK)
    lb_b = lb[1].reshape(hh, 1, HG_DK)
    scan_kw = dict(col_q=col_qhg, col_i=col_i, hh=hh)
    o_f, st_f = _hg_scan_fwd(proj, lb_f, reverse=False, col_z=col_zf, name="hg_scan_f", **scan_kw)
    o_b, st_b = _hg_scan_fwd(proj, lb_b, reverse=True, col_z=col_zb, name="hg_scan_b", **scan_kw)
    o_hg = _hg_out(o_f, o_b, proj, hg_gain, col_g=col_g, name="hg_out")
    w_na, w_hg, w_o, w_up, w_down = rest_weights(o_hg)
    y_na = _matmul(o_na, w_na, name="mm_na_out")
    gates = ((proj, col_gate), (proj, col_gate + d))

    def mix_gates(acc, gn, gh, yn):
        return acc, _sigmoid(gn) * yn + _sigmoid(gh) * acc

    def mix_gates_bwd(dmix, gn, gh, yn, yh):
        sn, sh = _sigmoid(gn), _sigmoid(gh)
        return dmix * sn, dmix * sh, dmix * yn * sn * (1.0 - sn), dmix * yh * sh * (1.0 - sh)

    y_hg, mix = _matmul(o_hg, w_hg, name="mm_hg_out", epilogue=mix_gates,
                        tiles=(*gates, (y_na, 0)), out_dtypes=(F32, BF16))
    t1 = _matmul(mix, w_o, name="mm_o")
    h1, mlp_in = _residual_norm(h0, t1, g_mlp, name="resid_norm_mlp")
    u, act = _matmul(mlp_in, w_up, name="mm_up", out_dtypes=(F32, BF16),
                     epilogue=lambda acc: (acc, jnp.square(jnp.maximum(acc, 0.0))))
    t2 = _matmul(act, w_down, name="mm_down")
    dh2, loss, dg_fin = _final_loss(h1, t2, g_fin, tgt_p, n_tok=n_tok, name="final_loss")

    (du,) = _matmul(dh2, w_down, tb=True, name="mm_down_dx", tiles=((u, 0),), out_dtypes=(BF16,),
                    epilogue=lambda acc, uv: (acc * 2.0 * jnp.maximum(uv, 0.0),))
    dw_down = _matmul(act, dh2, ta=True, name="mm_down_dw")
    dm = _matmul(du, w_up, tb=True, name="mm_up_dx")
    dw_up = _matmul(mlp_in, du, ta=True, name="mm_up_dw")
    dh1, dg_mlp = _rmsnorm_bwd_add(h1, g_mlp, dm, dh2, name="norm_mlp_bwd")
    dy_na, dy_hg, dgn, dgh = _matmul(dh1, w_o, tb=True, name="mm_o_dx", epilogue=mix_gates_bwd,
                                     tiles=(*gates, (y_na, 0), (y_hg, 0)), out_dtypes=(BF16,) * 4)
    dw_o = _matmul(mix, dh1, ta=True, name="mm_o_dw")
    do_na = _matmul(dy_na, w_na, tb=True, name="mm_na_out_dx")
    dw_na = _matmul(o_na, dy_na, ta=True, name="mm_na_out_dw")
    do_hg = _matmul(dy_hg, w_hg, tb=True, name="mm_hg_out_dx")
    dw_hg = _matmul(o_hg, dy_hg, ta=True, name="mm_hg_out_dw")
    token = early_grads([dw_na, dw_hg, dw_o, dw_up, dw_down]) if early_grads else None
    if token is not None:
        hg_gain = hg_gain + token[0:1, 0:1]
    d_o, dg_hg, d_gain = _hg_out_bwd(o_f, o_b, proj, hg_gain, do_hg, col_g=col_g, name="hg_out_bwd")
    dq_f, dz_f, dv_f, dlb_f = _hg_scan_bwd(proj, lb_f, st_f, d_o, reverse=False, col_z=col_zf,
                                           name="hg_scan_f_bwd", **scan_kw)
    dq_b, dz_b, dv_b, dlb_b = _hg_scan_bwd(proj, lb_b, st_b, d_o, reverse=True, col_z=col_zb,
                                           name="hg_scan_b_bwd", **scan_kw)
    dq_na, dk_na, dv_na, dtb = _na_bwd(proj, tb, o_na, lse, do_na, n_tok=n_tok, nh=nh, name="na_bwd")
    dproj = _assemble_dproj(dq_na, dk_na, dv_na, dq_f, dq_b, dz_f, dz_b, dv_f, dv_b, dg_hg, dgn,
                            dgh, name="assemble_dproj")
    dw_in = _matmul(a, dproj, ta=True, name="mm_in_dw")
    token = late_grad(dw_in) if late_grad else None
    da = _matmul(dproj, w_in, tb=True, name="mm_in_dx", after=token)
    dh0, dg_mix = _rmsnorm_bwd_add(h0, g_mix, da, dh1, name="norm_mix_bwd")
    d_rpb = _matmul(dtb.reshape(nh * (2 * NA_WIN_H - 1), GRID_W * GRID_W), oh, tb=True,
                    tm=nh * (2 * NA_WIN_H - 1), tn=LANES, tk=1024, precision=HIGHEST,
                    name="rpb_reduce")
    d_lb = jnp.concatenate([dlb_f.reshape(1, hgw), dlb_b.reshape(1, hgw)], axis=0)
    return (loss, dh0[N_META:l_real], dh0[:N_META], dw_in, dw_na, dw_hg, dw_o, dw_up, dw_down,
            dg_mix, dg_mlp, dg_fin, d_gain, d_rpb, d_lb)


N_CHIPS = 4
N_DEV = 8
ANY = pl.BlockSpec(memory_space=pl.ANY)


def _place():
    x, y, c = lax.axis_index("x"), lax.axis_index("y"), lax.axis_index("c")
    others = []
    for j in (1, 2, 3):
        tx = (1 - x) if (j >> 1) else x
        ty = (1 - y) if (j & 1) else y
        others.append((tx, ty))
    return x, y, c, others


def _piece(ref, axis, k, half, rh, cs):
    if axis == 1:
        return ref.at[pl.ds(pl.multiple_of(half * rh, 16), rh), pl.ds(pl.multiple_of(k * cs, LANES), cs)]
    return ref.at[pl.ds(pl.multiple_of(k * 2 * rh + half * rh, 16), rh), :]


def _cast_into_full(shard, axis, place, *, name):
    r, cs = shard.shape
    full = (r, cs * N_CHIPS) if axis == 1 else (r * N_CHIPS, cs)
    tr = next(t for t in (256, 128, 64, 32, 16) if r % t == 0)
    nt = r // tr

    def body(p_ref, s_ref, o_ref):
        o_ref[...] = s_ref[...].astype(BF16)

    if axis == 1:
        omap = lambda i, p_ref: (i, p_ref[0])
    else:
        omap = lambda i, p_ref: (p_ref[0] * nt + i, 0)
    return pl.pallas_call(
        body,
        grid_spec=pltpu.PrefetchScalarGridSpec(
            num_scalar_prefetch=1, grid=(nt,),
            in_specs=[pl.BlockSpec((tr, cs), lambda i, p_ref: (i, 0))],
            out_specs=pl.BlockSpec((tr, cs), omap)),
        out_shape=jax.ShapeDtypeStruct(full, BF16),
        compiler_params=_cp("parallel"), name=name)(place, shard)


HBM_SPEC = pl.BlockSpec(memory_space=pltpu.HBM)
SEM_SPEC = pl.BlockSpec(memory_space=pltpu.SEMAPHORE)
SPLIT_COPY = pltpu.CompilerParams(has_side_effects=pltpu.SideEffectType.DATAFLOW_SIDE_EFFECTING)
TOKEN = jax.ShapeDtypeStruct((8, LANES), F32)


def _geo(fulls, axes):
    out = []
    for f, ax in zip(fulls, axes):
        r, cs = (f.shape[0], f.shape[1] // N_CHIPS) if ax == 1 else (f.shape[0] // N_CHIPS, f.shape[1])
        out.append((ax, r // 2, cs))
    return out


def _gather_copies(refs, geo, send_sems, recv_sems):
    x, y, c, others = _place()
    chip = 2 * x + y
    cps = []
    for i, (ax, rh, cs) in enumerate(geo):
        mine = _piece(refs[i], ax, chip, c, rh, cs)
        for j, (tx, ty) in enumerate(others):
            cps.append(pltpu.make_async_remote_copy(
                src_ref=mine, dst_ref=mine, send_sem=send_sems.at[3 * i + j],
                recv_sem=recv_sems.at[3 * i + j], device_id=(tx, ty, c), device_id_type=MESH))
    return cps


def _allgather_start(fulls, axes, after, *, name):
    n = len(fulls)
    geo = _geo(fulls, axes)

    def body(*refs):
        w_refs = refs[:n]
        send_sems, recv_sems = refs[n + 1], refs[n + 2]
        token = refs[2 * n + 3]
        for cp in _gather_copies(w_refs, geo, send_sems, recv_sems):
            cp.start()
        token[...] = jnp.zeros_like(token)

    out = pl.pallas_call(
        body, name=name,
        out_shape=(pltpu.SemaphoreType.DMA((3 * n,)), pltpu.SemaphoreType.DMA((3 * n,)),
                   *[pltpu.HBM(f.shape, f.dtype) for f in fulls], TOKEN),
        in_specs=[HBM_SPEC] * n + [ANY],
        out_specs=(SEM_SPEC, SEM_SPEC, *[HBM_SPEC] * n, pl.BlockSpec(memory_space=pltpu.VMEM)),
        input_output_aliases={i: 2 + i for i in range(n)},
        compiler_params=SPLIT_COPY,
    )(*[pltpu.with_memory_space_constraint(f, pltpu.HBM) for f in fulls], after)
    return out[0], out[1], list(out[2:2 + n]), out[2 + n]


def _allgather_wait(send_sems, recv_sems, fulls, axes, after, *, name):
    n = len(fulls)
    geo = _geo(fulls, axes)

    def body(*refs):
        w_refs = refs[:n]
        for cp in _gather_copies(w_refs, geo, refs[n], refs[n + 1]):
            cp.wait_send()
            cp.wait_recv()

    return list(pl.pallas_call(
        body, name=name,
        out_shape=[pltpu.HBM(f.shape, f.dtype) for f in fulls],
        in_specs=[HBM_SPEC] * n + [SEM_SPEC, SEM_SPEC, ANY],
        out_specs=[HBM_SPEC] * n,
        input_output_aliases={i: i for i in range(n)},
        compiler_params=SPLIT_COPY,
    )(*fulls, send_sems, recv_sems, after))


def _allgather_forward(fulls, axes, *, name):
    n = len(fulls)
    geo = _geo(fulls, axes)

    def body(*refs):
        o_refs = refs[n:2 * n]
        send_sems, recv_sems = refs[2 * n:]
        x, y, c, others = _place()

        def rcopy(i, j, half, to):
            ax, rh, cs = geo[i]
            ref = _piece(o_refs[i], ax, 2 * others[j][0] + others[j][1], half, rh, cs)
            return pltpu.make_async_remote_copy(
                src_ref=ref, dst_ref=ref, send_sem=send_sems.at[3 * i + j],
                recv_sem=recv_sems.at[3 * i + j], device_id=to, device_id_type=MESH)

        cps = [rcopy(i, j, c, (x, y, 1 - c)) for i in range(n) for j in range(3)]
        for cp in cps:
            cp.start()
        for i in range(n):
            for j in range(3):
                rcopy(i, j, 1 - c, (x, y, c)).wait_recv()
        for cp in cps:
            cp.wait_send()

    return list(pl.pallas_call(
        body, in_specs=[ANY] * n, out_specs=[ANY] * n,
        out_shape=[jax.ShapeDtypeStruct(f.shape, f.dtype) for f in fulls],
        input_output_aliases={i: i for i in range(n)},
        scratch_shapes=[pltpu.SemaphoreType.DMA((3 * n,)), pltpu.SemaphoreType.DMA((3 * n,))],
        name=name)(*fulls))


def _scatter_geo(parts, axes):
    out = []
    for p, ax in zip(parts, axes):
        _, rh, cols = p.shape
        out.append((ax, rh, cols // N_CHIPS if ax == 1 else cols))
    return out


def _scatter_copies(p_refs, q_refs, geo, send_sems, recv_sems):
    x, y, c, others = _place()
    chip = 2 * x + y
    cps = []
    for i, (ax, rh, cw) in enumerate(geo):
        for j, (tx, ty) in enumerate(others):
            k = 2 * tx + ty
            src = (p_refs[i].at[0, :, pl.ds(pl.multiple_of(k * cw, LANES), cw)] if ax == 1
                   else p_refs[i].at[k])
            cps.append(pltpu.make_async_remote_copy(
                src_ref=src, dst_ref=q_refs[i].at[chip], send_sem=send_sems.at[3 * i + j],
                recv_sem=recv_sems.at[3 * i + j], device_id=(tx, ty, c), device_id_type=MESH))
    return cps


def _scatter_start(parts, axes, *, name):
    n = len(parts)
    geo = _scatter_geo(parts, axes)
    slots = [pltpu.HBM((N_CHIPS, rh, cw), p.dtype) for p, (_, rh, cw) in zip(parts, geo)]

    def body(*refs):
        p_refs, q_refs = refs[:n], refs[n:2 * n]
        send_sems, recv_sems = refs[2 * n], refs[2 * n + 1]
        token = refs[4 * n + 2]
        for cp in _scatter_copies(p_refs, q_refs, geo, send_sems, recv_sems):
            cp.start()
        token[...] = jnp.zeros_like(token)

    land = [pltpu.with_memory_space_constraint(lax.empty(s.inner_aval.shape, s.inner_aval.dtype), pltpu.HBM)
            for s in slots]
    out = pl.pallas_call(
        body, name=name,
        out_shape=(pltpu.SemaphoreType.DMA((3 * n,)), pltpu.SemaphoreType.DMA((3 * n,)),
                   *[pltpu.HBM(p.shape, p.dtype) for p in parts], *slots, TOKEN),
        in_specs=[HBM_SPEC] * (2 * n),
        out_specs=(SEM_SPEC, SEM_SPEC, *[HBM_SPEC] * (2 * n), pl.BlockSpec(memory_space=pltpu.VMEM)),
        input_output_aliases={i: 2 + i for i in range(2 * n)},
        compiler_params=SPLIT_COPY,
    )(*[pltpu.with_memory_space_constraint(p, pltpu.HBM) for p in parts], *land)
    return out[0], out[1], list(out[2:2 + n]), list(out[2 + n:2 + 2 * n]), out[2 + 2 * n]


def _scatter_wait(send_sems, recv_sems, parts, slots, axes, after, *, name):
    n = len(parts)
    geo = _scatter_geo(parts, axes)

    def body(*refs):
        p_refs, q_refs = refs[:n], refs[n:2 * n]
        for cp in _scatter_copies(p_refs, q_refs, geo, refs[2 * n], refs[2 * n + 1]):
            cp.wait_send()
            cp.wait_recv()

    out = pl.pallas_call(
        body, name=name,
        out_shape=[pltpu.HBM(a.shape, a.dtype) for a in (*parts, *slots)],
        in_specs=[HBM_SPEC] * (2 * n) + [SEM_SPEC, SEM_SPEC, ANY],
        out_specs=[HBM_SPEC] * (2 * n),
        input_output_aliases={i: i for i in range(2 * n)},
        compiler_params=SPLIT_COPY,
    )(*parts, *slots, send_sems, recv_sems, after)
    return list(out[:n]), list(out[n:])


def _sibling_swap(grads, *, name):
    n = len(grads)
    out_shape = [jax.ShapeDtypeStruct((g.shape[0], g.shape[1] // 2, g.shape[2]), g.dtype)
                 for g in grads]

    def body(*refs):
        g_refs, o_refs = refs[:n], refs[n:2 * n]
        send_sems, recv_sems = refs[2 * n:]
        x, y, c, _ = _place()
        cps = []
        for i in range(n):
            rh = grads[i].shape[1] // 2
            src = g_refs[i].at[:, pl.ds(pl.multiple_of((1 - c) * rh, 16), rh), :]
            cp = pltpu.make_async_remote_copy(
                src_ref=src, dst_ref=o_refs[i], send_sem=send_sems.at[i], recv_sem=recv_sems.at[i],
                device_id=(x, y, 1 - c), device_id_type=MESH)
            cp.start()
            cps.append(cp)
        for cp in cps:
            cp.wait()

    return pl.pallas_call(
        body, in_specs=[ANY] * n, out_specs=[ANY] * n, out_shape=out_shape,
        scratch_shapes=[pltpu.SemaphoreType.DMA((n,)), pltpu.SemaphoreType.DMA((n,))],
        name=name)(*grads)


def _pair_add(g3, rx, c_arr, *, out_dtype, name):
    nb, rows, cols = g3.shape
    rh = rows // 2
    tr = next(t for t in (128, 64, 32, 16) if rh % t == 0)
    nt = rh // tr

    def body(c_ref, g_ref, r_ref, o_ref):
        o_ref[...] = (g_ref[...] + r_ref[...]).astype(out_dtype)

    return pl.pallas_call(
        body,
        grid_spec=pltpu.PrefetchScalarGridSpec(
            num_scalar_prefetch=1, grid=(nb, nt),
            in_specs=[pl.BlockSpec((None, tr, cols), lambda b, i, c_ref: (b, c_ref[0] * nt + i, 0)),
                      pl.BlockSpec((None, tr, cols), lambda b, i, c_ref: (b, i, 0))],
            out_specs=pl.BlockSpec((None, tr, cols), lambda b, i, c_ref: (b, i, 0))),
        out_shape=jax.ShapeDtypeStruct((nb, rh, cols), out_dtype),
        compiler_params=_cp("parallel", "parallel"), name=name)(c_arr, g3, rx)


def _sum_slots(q, *, name):
    ns, rows, cols = q.shape
    tr = next(t for t in (128, 64, 32, 16, 8) if rows % t == 0)

    def body(q_ref, o_ref):
        acc = q_ref[0].astype(F32)
        for k in range(1, ns):
            acc = acc + q_ref[k].astype(F32)
        o_ref[...] = acc

    return pl.pallas_call(
        body, grid=(rows // tr,),
        in_specs=[pl.BlockSpec((ns, tr, cols), lambda i: (0, i, 0))],
        out_specs=_rspec(tr, cols),
        out_shape=jax.ShapeDtypeStruct((rows, cols), F32),
        compiler_params=_cp("parallel"), name=name)(q)


def _sum_chips(q, p, place, axis, *, name):
    _, rh, cw = q.shape
    tr = next(t for t in (128, 64, 32, 16) if rh % t == 0)
    nt = rh // tr

    def body(p_ref, *refs):
        q_refs, own_ref, o_ref = refs[:N_CHIPS], refs[N_CHIPS], refs[N_CHIPS + 1]
        chip = p_ref[0]
        acc = jnp.where(chip == 0, own_ref[...], q_refs[0][...]).astype(F32)
        for k in range(1, N_CHIPS):
            acc = acc + jnp.where(chip == k, own_ref[...], q_refs[k][...]).astype(F32)
        o_ref[...] = acc

    def slot_spec(k):
        return pl.BlockSpec((None, tr, cw),
                            lambda i, p_ref: (jnp.where(p_ref[0] == k, (k + 1) % N_CHIPS, k), i, 0))

    if axis == 1:
        own_spec = pl.BlockSpec((None, tr, cw), lambda i, p_ref: (0, i, p_ref[0]))
    else:
        own_spec = pl.BlockSpec((None, tr, cw), lambda i, p_ref: (p_ref[0], i, 0))
    return pl.pallas_call(
        body,
        grid_spec=pltpu.PrefetchScalarGridSpec(
            num_scalar_prefetch=1, grid=(nt,),
            in_specs=[slot_spec(k) for k in range(N_CHIPS)] + [own_spec],
            out_specs=pl.BlockSpec((tr, cw), lambda i, p_ref: (p_ref[1] * nt + i, 0))),
        out_shape=jax.ShapeDtypeStruct((2 * rh, cw), F32),
        compiler_params=_cp("parallel"), name=name)(place, *([q] * N_CHIPS), p)


def _sibling_share(shards, *, name):
    n = len(shards)

    def body(*refs):
        o_refs = refs[n:2 * n]
        send_sems, recv_sems = refs[2 * n:]
        x, y, c, _ = _place()
        cps = []
        for i in range(n):
            rh = shards[i].shape[0] // 2
            mine = o_refs[i].at[pl.ds(pl.multiple_of(c * rh, 8), rh), :]
            cp = pltpu.make_async_remote_copy(
                src_ref=mine, dst_ref=mine, send_sem=send_sems.at[i], recv_sem=recv_sems.at[i],
                device_id=(x, y, 1 - c), device_id_type=MESH)
            cp.start()
            cps.append(cp)
        for i in range(n):
            rh = shards[i].shape[0] // 2
            theirs = o_refs[i].at[pl.ds(pl.multiple_of((1 - c) * rh, 8), rh), :]
            pltpu.make_async_remote_copy(
                src_ref=theirs, dst_ref=theirs, send_sem=send_sems.at[i], recv_sem=recv_sems.at[i],
                device_id=(x, y, c), device_id_type=MESH).wait_recv()
        for cp in cps:
            cp.wait_send()

    return pl.pallas_call(
        body, in_specs=[ANY] * n, out_specs=[ANY] * n,
        out_shape=[jax.ShapeDtypeStruct(h.shape, h.dtype) for h in shards],
        input_output_aliases={i: i for i in range(n)},
        scratch_shapes=[pltpu.SemaphoreType.DMA((n,)), pltpu.SemaphoreType.DMA((n,))],
        name=name)(*shards)


def _gather_all(blk, *, name, after=None):
    rows, cols = blk.shape
    extra = [] if after is None else [after]

    def body(x_ref, *refs):
        out_ref, send_sems, recv_sems, local_sem = refs[len(extra):]
        x, y, c = lax.axis_index("x"), lax.axis_index("y"), lax.axis_index("c")
        me = 4 * x + 2 * y + c
        mine = pltpu.make_async_copy(x_ref, out_ref.at[me], local_sem)
        mine.start()
        cps = []
        for k in range(1, N_DEV):
            tx = (1 - x) if (k >> 2) & 1 else x
            ty = (1 - y) if (k >> 1) & 1 else y
            tc = (1 - c) if k & 1 else c
            cp = pltpu.make_async_remote_copy(
                src_ref=x_ref, dst_ref=out_ref.at[me], send_sem=send_sems.at[k - 1],
                recv_sem=recv_sems.at[k - 1], device_id=(tx, ty, tc), device_id_type=MESH)
            cp.start()
            cps.append(cp)
        for k in range(1, N_DEV):
            tx = (1 - x) if (k >> 2) & 1 else x
            ty = (1 - y) if (k >> 1) & 1 else y
            tc = (1 - c) if k & 1 else c
            got = out_ref.at[4 * tx + 2 * ty + tc]
            pltpu.make_async_remote_copy(
                src_ref=got, dst_ref=got, send_sem=send_sems.at[k - 1], recv_sem=recv_sems.at[k - 1],
                device_id=(x, y, c), device_id_type=MESH).wait_recv()
        for cp in cps:
            cp.wait_send()
        mine.wait()

    vm = pl.BlockSpec(memory_space=pltpu.VMEM)
    return pl.pallas_call(
        body, in_specs=[vm] + [ANY] * len(extra), out_specs=vm,
        out_shape=jax.ShapeDtypeStruct((N_DEV, rows, cols), blk.dtype),
        scratch_shapes=[pltpu.SemaphoreType.DMA((N_DEV - 1,)), pltpu.SemaphoreType.DMA((N_DEV - 1,)),
                        pltpu.SemaphoreType.DMA],
        name=name)(blk, *extra)


def _as_rows(a):
    flat = a.reshape(-1)
    n = flat.shape[0]
    rows = -(-n // (8 * LANES)) * 8
    return jnp.pad(flat, (0, rows * LANES - n)).reshape(rows, LANES)


def _from_rows(p, shape):
    n = int(np.prod(shape))
    return p.reshape(-1)[:n].reshape(shape)


WEIGHT_AXES = (1, 1, 1, 0, 1, 0)
WIRE = BF16


def kernel(x, meta_tokens, w_in, w_na_out, w_hg_out, w_o, w_up, w_down, norm_mix, norm_mlp, norm_final, hg_norm, na_rpb, hg_lb_logits, loss_target, m_meta_tokens, m_w_in, m_w_na_out, m_w_hg_out, m_w_o, m_w_up, m_w_down, m_norm_mix, m_norm_mlp, m_norm_final, m_hg_norm, m_na_rpb, m_hg_lb_logits, v_meta_tokens, v_w_in, v_w_na_out, v_w_hg_out, v_w_o, v_w_up, v_w_down, v_norm_mix, v_norm_mlp, v_norm_final, v_hg_norm, v_na_rpb, v_hg_lb_logits):
    xi, yi, ci = lax.axis_index("x"), lax.axis_index("y"), lax.axis_index("c")
    chip = 2 * xi + yi
    d = x.shape[-1]
    dshard = meta_tokens.shape[1]
    hgw = hg_norm.shape[1]
    lbs = hg_lb_logits.shape[2]
    big = [w_in[0], w_na_out[0], w_hg_out[0], w_o[0], w_up[0], w_down[0]]
    big_m = [m_w_in[0], m_w_na_out[0], m_w_hg_out[0], m_w_o[0], m_w_up[0], m_w_down[0]]
    big_v = [v_w_in[0], v_w_na_out[0], v_w_hg_out[0], v_w_o[0], v_w_up[0], v_w_down[0]]

    place = jnp.stack([chip, ci]).astype(jnp.int32)
    own_w = [_cast_into_full(w, ax, place, name=f"cast_shard_{i}")
             for i, (w, ax) in enumerate(zip(big, WEIGHT_AXES))]
    in_axes, rest_axes = WEIGHT_AXES[:1], WEIGHT_AXES[1:]
    small_in = jnp.concatenate([_as_rows(meta_tokens), _as_rows(hg_lb_logits)], axis=0)
    small_all = _gather_all(small_in, name="gather_small_params")[0::2]
    in_send, in_recv, in_bufs, in_token = _allgather_start(own_w[:1], in_axes, small_all,
                                                           name="weight_allgather_in_start")
    ag_send, ag_recv, ag_bufs, ag_token = _allgather_start(own_w[1:], rest_axes, in_token,
                                                           name="weight_allgather_rest_start")

    def first_weight(after):
        got = _allgather_wait(in_send, in_recv, in_bufs, in_axes, after,
                              name="weight_allgather_in_wait")
        return _allgather_forward(got, in_axes, name="weight_allgather_in_forward")[0]

    def rest_weights(after):
        got = _allgather_wait(ag_send, ag_recv, ag_bufs, rest_axes, after,
                              name="weight_allgather_rest_wait")
        return _allgather_forward(got, rest_axes, name="weight_allgather_rest_forward")

    n_meta_rows = N_META * dshard // LANES
    meta_full = (small_all[:, :n_meta_rows].reshape(N_CHIPS, N_META, dshard)
                 .transpose(1, 0, 2).reshape(N_META, d))
    lbl_full = (small_all[:, n_meta_rows:].reshape(N_CHIPS, -1)[:, :4 * lbs]
                .reshape(N_CHIPS, 2, 2, lbs).transpose(1, 2, 0, 3).reshape(2, 2, N_CHIPS * lbs))
    lb = jax.nn.softmax(lbl_full, axis=1)[:, 0]

    c_arr = ci.reshape(1).astype(jnp.int32)

    def chip_partials(dws, axes, tag):
        g3 = [g.reshape(1, *g.shape) if ax == 1
              else g.reshape(N_CHIPS, g.shape[0] // N_CHIPS, g.shape[1]) for g, ax in zip(dws, axes)]
        rx = _sibling_swap(g3, name=f"grad_sibling_swap_{tag}")
        return [_pair_add(g, r, c_arr, out_dtype=WIRE, name=f"grad_pair_add_{tag}_{i}")
                for i, (g, r) in enumerate(zip(g3, rx))]

    flying = {}

    def scatter_behind(tag, axes):
        def hook(dws):
            send, recv, parts, slots, token = _scatter_start(
                chip_partials(dws, axes, tag), axes, name=f"grad_scatter_{tag}_start")
            flying[tag] = (send, recv, parts, slots)
            return token
        return hook

    def landed(tag, axes, after):
        parts, slots = _scatter_wait(*flying[tag], axes, after, name=f"grad_scatter_{tag}_wait")
        return [_sum_chips(q, p, place, ax, name=f"grad_sum_chips_{tag}_{i}")
                for i, (q, p, ax) in enumerate(zip(slots, parts, axes))]

    (loss, dx, dmeta, *_, dg_mix, dg_mlp, dg_fin, d_gain, d_rpb, d_lb) = _local_step(
        x[0], loss_target[0], meta_full, first_weight, rest_weights,
        norm_mix + ag_token[0:1, 0:1], norm_mlp, norm_final.reshape(1, d), hg_norm, na_rpb[0], lb,
        scatter_behind("rest", rest_axes), lambda dw_in: scatter_behind("in", in_axes)([dw_in]))

    halves_rest = landed("rest", rest_axes, dx)
    halves_in = landed("in", in_axes, halves_rest[-1])
    g_big = _sibling_share(halves_in + halves_rest, name="grad_sibling_share")

    d_rpb_c = d_rpb[:, :2 * NA_WIN_W - 1]
    small_g = [dmeta, dg_mix, dg_mlp, dg_fin, d_gain, d_rpb_c, d_lb, loss]
    packed = jnp.concatenate([_as_rows(a) for a in small_g], axis=0)
    total = _sum_slots(_gather_all(packed, after=halves_in[0], name="gather_small_grads"),
                       name="sum_small_grads")
    offs = np.cumsum([0] + [_as_rows(a).shape[0] for a in small_g])
    take = lambda i, shape: _from_rows(total[offs[i]:offs[i + 1]], shape)
    g_meta_full = take(0, (N_META, d))
    g_norm_mix, g_norm_mlp = take(1, (1, d)), take(2, (1, d))
    g_norm_final = take(3, (d,))
    g_hg_norm = take(4, (1, hgw))
    g_rpb = take(5, na_rpb.shape)
    g_lb = take(6, (2, hgw))
    loss_total = take(7, (1, LANES))[0, 0]
    g_meta = lax.dynamic_slice_in_dim(g_meta_full, chip * dshard, dshard, axis=1)
    dl0 = lb * (1.0 - lb) * g_lb
    g_lbl_full = jnp.stack([dl0, -dl0], axis=1)
    g_lbl = lax.dynamic_slice_in_dim(g_lbl_full, chip * lbs, lbs, axis=2)

    big_out = [_adamw(w, g, m, v, name=f"adamw_{i}")
               for i, (w, g, m, v) in enumerate(zip(big, g_big, big_m, big_v))]
    small_w = [meta_tokens, norm_mix, norm_mlp, norm_final, hg_norm, na_rpb, hg_lb_logits]
    small_gr = [g_meta, g_norm_mix, g_norm_mlp, g_norm_final, g_hg_norm, g_rpb, g_lbl]
    small_m = [m_meta_tokens, m_norm_mix, m_norm_mlp, m_norm_final, m_hg_norm, m_na_rpb, m_hg_lb_logits]
    small_v = [v_meta_tokens, v_norm_mix, v_norm_mlp, v_norm_final, v_hg_norm, v_na_rpb, v_hg_lb_logits]
    pk = lambda lst: jnp.concatenate([_as_rows(a) for a in lst], axis=0)
    sd, sm, sv = _adamw(pk(small_w), pk(small_gr), pk(small_m), pk(small_v), name="adamw_small")
    soffs = np.cumsum([0] + [_as_rows(a).shape[0] for a in small_w])
    unpk = lambda p: [_from_rows(p[soffs[i]:soffs[i + 1]], small_w[i].shape) for i in range(len(small_w))]
    sd, sm, sv = unpk(sd), unpk(sm), unpk(sv)

    def order(bigs, smalls):
        return [smalls[0]] + [b.reshape(1, *b.shape) for b in bigs] + smalls[1:]

    grads = order(g_big, small_gr)
    deltas = order([o[0] for o in big_out], sd)
    new_m = order([o[1] for o in big_out], sm)
    new_v = order([o[2] for o in big_out], sv)
    return (loss_total, dx.reshape(1, *dx.shape), *grads, *deltas, *new_m, *new_v)
```

```python
import functools

import numpy as np
import jax
import jax.numpy as jnp
from jax import lax
from jax.experimental import pallas as pl
from jax.experimental.pallas import tpu as pltpu

F32 = jnp.float32
BF16 = jnp.bfloat16
HIGHEST = lax.Precision.HIGHEST

GRID_W = 64
N_META = 16
EPS = 1e-6
NA_HEAD_DIM = 64
NA_WIN_H = 8
NA_WIN_W = 16
HG_DK = 128
HG_CHUNK = 16
LANES = 128
ROW_ALIGN = 128
VMEM_LIMIT = 48 * 1024 * 1024

ADAM_LR = 0.001
ADAM_B1 = 0.9
ADAM_B2 = 0.999
ADAM_EPS = 1e-08
ADAM_WD = 0.01
ADAM_STEP = 10

MESH = pl.DeviceIdType.MESH


def _cp(*sem):
    return pltpu.CompilerParams(dimension_semantics=sem, vmem_limit_bytes=VMEM_LIMIT)


def _sigmoid(x):
    return 1.0 / (1.0 + jnp.exp(-x))


def _dot(a, b, dims, precision=None):
    return lax.dot_general(a, b, (dims, ((), ())), preferred_element_type=F32, precision=precision)


def _nn(a, b, **kw):
    return _dot(a, b, ((1,), (0,)), **kw)


def _nt(a, b, **kw):
    return _dot(a, b, ((1,), (1,)), **kw)


def _tn(a, b, **kw):
    return _dot(a, b, ((0,), (0,)), **kw)


def _matmul(a, b, *, ta=False, tb=False, tm=None, tn=None, tk=None, out_dtype=F32, name,
            precision=None, after=None, epilogue=None, tiles=(), out_dtypes=None):
    extra = [] if after is None else [after]
    single = out_dtypes is None
    if single:
        out_dtypes = (out_dtype,)
    n_t, n_o = len(tiles), len(out_dtypes)
    if ta:
        kdim, m = a.shape
    else:
        m, kdim = a.shape
    if tb:
        n, k2 = b.shape
    else:
        k2, n = b.shape
    assert kdim == k2, (a.shape, b.shape, ta, tb)
    if tm is None:
        if ta:
            tm = next(t for t in (512, 256, 128, m) if m % t == 0)
        else:
            tm = m // 2 if (m // 2) % 16 == 0 and m > 512 else m
    if tn is None:
        tn = next(t for t in (512, 256, 128, n) if n % t == 0)
    if tk is None:
        tk = kdim if ta else next(t for t in (1024, 512, 256, 128, kdim) if kdim % t == 0)
    assert m % tm == 0 and n % tn == 0 and kdim % tk == 0, (m, n, kdim, tm, tn, tk)
    nk = kdim // tk
    op_dtype = F32 if precision is not None else BF16

    def body(a_ref, b_ref, *refs):
        t_refs = refs[:n_t]
        o_refs = refs[n_t + len(extra):n_t + len(extra) + n_o]
        acc_ref = refs[-1]
        kk = pl.program_id(2)

        @pl.when(kk == 0)
        def _():
            acc_ref[...] = jnp.zeros_like(acc_ref)

        av = a_ref[...].astype(op_dtype)
        bv = b_ref[...].astype(op_dtype)
        dims = ((0 if ta else 1,), (1 if tb else 0,))
        acc_ref[...] += _dot(av, bv, dims, precision=precision)

        @pl.when(kk == nk - 1)
        def _():
            acc = acc_ref[...]
            outs = (acc,) if epilogue is None else epilogue(acc, *[t[...] for t in t_refs])
            for o_ref, val in zip(o_refs, outs):
                o_ref[...] = val.astype(o_ref.dtype)

    a_spec = (pl.BlockSpec((tk, tm), lambda i, j, k: (k, i)) if ta
              else pl.BlockSpec((tm, tk), lambda i, j, k: (i, k)))
    b_spec = (pl.BlockSpec((tn, tk), lambda i, j, k: (j, k)) if tb
              else pl.BlockSpec((tk, tn), lambda i, j, k: (k, j)))
    for _, off in tiles:
        assert off % tn == 0, (off, tn)
    t_specs = [pl.BlockSpec((tm, tn), functools.partial(lambda i, j, k, o: (i, o + j), o=off // tn))
               for _, off in tiles]
    o_spec = pl.BlockSpec((tm, tn), lambda i, j, k: (i, j))
    outs = pl.pallas_call(
        body,
        grid=(m // tm, n // tn, nk),
        in_specs=[a_spec, b_spec] + t_specs + [pl.BlockSpec(memory_space=pl.ANY)] * len(extra),
        out_specs=[o_spec] * n_o,
        out_shape=[jax.ShapeDtypeStruct((m, n), dt) for dt in out_dtypes],
        scratch_shapes=[pltpu.VMEM((tm, tn), F32)],
        compiler_params=_cp("parallel", "parallel", "arbitrary"),
        name=name,
    )(a, b, *[t for t, _ in tiles], *extra)
    return outs[0] if single else outs


def _rspec(tr, w, cb=0):
    return pl.BlockSpec((tr, w), lambda i: (i, cb))


def _fspec(shape):
    nd = len(shape)
    return pl.BlockSpec(shape, lambda i: (0,) * nd)


def _row_tile(lp):
    return ROW_ALIGN if lp % ROW_ALIGN == 0 else lp


def _rmsnorm_fwd(x, g, *, name):
    lp, d = x.shape
    tr = _row_tile(lp)

    def body(x_ref, g_ref, o_ref):
        xv = x_ref[...]
        r = lax.rsqrt(jnp.mean(xv * xv, axis=-1, keepdims=True) + EPS)
        o_ref[...] = (xv * r * g_ref[...]).astype(BF16)

    return pl.pallas_call(
        body, grid=(lp // tr,),
        in_specs=[_rspec(tr, d), _fspec((1, d))],
        out_specs=_rspec(tr, d),
        out_shape=jax.ShapeDtypeStruct((lp, d), BF16),
        compiler_params=_cp("parallel"), name=name)(x, g)


def _residual_norm(h, t, g, *, name):
    lp, d = h.shape
    tr = _row_tile(lp)

    def body(h_ref, t_ref, g_ref, h1_ref, m_ref):
        xv = h_ref[...] + t_ref[...]
        h1_ref[...] = xv
        r = lax.rsqrt(jnp.mean(xv * xv, axis=-1, keepdims=True) + EPS)
        m_ref[...] = (xv * r * g_ref[...]).astype(BF16)

    return pl.pallas_call(
        body, grid=(lp // tr,),
        in_specs=[_rspec(tr, d), _rspec(tr, d), _fspec((1, d))],
        out_specs=[_rspec(tr, d), _rspec(tr, d)],
        out_shape=[jax.ShapeDtypeStruct((lp, d), F32), jax.ShapeDtypeStruct((lp, d), BF16)],
        compiler_params=_cp("parallel"), name=name)(h, t, g)


def _rmsnorm_bwd_add(x, g, dy, dres, *, name):
    lp, d = x.shape
    tr = _row_tile(lp)

    def body(x_ref, g_ref, dy_ref, dr_ref, dx_ref, dg_ref):
        @pl.when(pl.program_id(0) == 0)
        def _():
            dg_ref[...] = jnp.zeros_like(dg_ref)

        xv = x_ref[...]
        r = lax.rsqrt(jnp.mean(xv * xv, axis=-1, keepdims=True) + EPS)
        xh = xv * r
        dyv = dy_ref[...]
        dg_ref[...] += jnp.sum(dyv * xh, axis=0, keepdims=True)
        dxh = dyv * g_ref[...]
        dx_ref[...] = dr_ref[...] + r * (dxh - xh * jnp.mean(dxh * xh, axis=-1, keepdims=True))

    return pl.pallas_call(
        body, grid=(lp // tr,),
        in_specs=[_rspec(tr, d), _fspec((1, d)), _rspec(tr, d), _rspec(tr, d)],
        out_specs=[_rspec(tr, d), _fspec((1, d))],
        out_shape=[jax.ShapeDtypeStruct((lp, d), F32), jax.ShapeDtypeStruct((1, d), F32)],
        compiler_params=_cp("arbitrary"), name=name)(x, g, dy, dres)


def _final_loss(h1, t2, g, tgt, *, n_tok, name):
    lp, d = h1.shape
    tr = _row_tile(lp)

    def body(h_ref, t_ref, g_ref, tg_ref, dh_ref, loss_ref, dg_ref):
        i = pl.program_id(0)

        @pl.when(i == 0)
        def _():
            loss_ref[...] = jnp.zeros_like(loss_ref)
            dg_ref[...] = jnp.zeros_like(dg_ref)

        xv = h_ref[...] + t_ref[...]
        r = lax.rsqrt(jnp.mean(xv * xv, axis=-1, keepdims=True) + EPS)
        xh = xv * r
        gv = g_ref[...]
        row = i * tr + lax.broadcasted_iota(jnp.int32, (tr, 1), 0)
        valid = (row >= N_META) & (row < N_META + n_tok)
        err = jnp.where(valid, xh * gv - tg_ref[...], 0.0)
        loss_ref[...] += jnp.sum(0.5 * err * err) / d
        dy = err / d
        dg_ref[...] += jnp.sum(dy * xh, axis=0, keepdims=True)
        dxh = dy * gv
        dh_ref[...] = r * (dxh - xh * jnp.mean(dxh * xh, axis=-1, keepdims=True))

    return pl.pallas_call(
        body, grid=(lp // tr,),
        in_specs=[_rspec(tr, d), _rspec(tr, d), _fspec((1, d)), _rspec(tr, d)],
        out_specs=[_rspec(tr, d), _fspec((1, LANES)), _fspec((1, d))],
        out_shape=[jax.ShapeDtypeStruct((lp, d), F32), jax.ShapeDtypeStruct((1, LANES), F32),
                   jax.ShapeDtypeStruct((1, d), F32)],
        compiler_params=_cp("arbitrary"), name=name)(h1, t2, g, tgt)


def _hg_out(o_f, o_b, proj, gain, *, col_g, name):
    lp, w = o_f.shape
    tr = _row_tile(lp)
    hh = w // HG_DK

    def body(of_ref, ob_ref, g_ref, gain_ref, y_ref):
        gv = g_ref[...]
        sg = gv * _sigmoid(gv)
        for h in range(hh):
            sl = slice(h * HG_DK, (h + 1) * HG_DK)
            o = of_ref[:, sl] + ob_ref[:, sl]
            r = lax.rsqrt(jnp.mean(o * o, axis=-1, keepdims=True) + EPS)
            y_ref[:, sl] = (o * r * gain_ref[:, sl] * sg[:, sl]).astype(BF16)

    return pl.pallas_call(
        body, grid=(lp // tr,),
        in_specs=[_rspec(tr, w), _rspec(tr, w), _rspec(tr, w, col_g // w), _fspec((1, w))],
        out_specs=_rspec(tr, w),
        out_shape=jax.ShapeDtypeStruct((lp, w), BF16),
        compiler_params=_cp("parallel"), name=name)(o_f, o_b, proj, gain)


def _hg_out_bwd(o_f, o_b, proj, gain, dy, *, col_g, name):
    lp, w = o_f.shape
    tr = _row_tile(lp)
    hh = w // HG_DK

    def body(of_ref, ob_ref, g_ref, gain_ref, dy_ref, do_ref, dg_ref, dgain_ref):
        @pl.when(pl.program_id(0) == 0)
        def _():
            dgain_ref[...] = jnp.zeros_like(dgain_ref)

        for h in range(hh):
            sl = slice(h * HG_DK, (h + 1) * HG_DK)
            gv = g_ref[:, sl]
            s = _sigmoid(gv)
            sg = gv * s
            dsg = s + gv * s * (1.0 - s)
            o = of_ref[:, sl] + ob_ref[:, sl]
            r = lax.rsqrt(jnp.mean(o * o, axis=-1, keepdims=True) + EPS)
            on = o * r
            dyv = dy_ref[:, sl]
            gn = gain_ref[:, sl]
            dgain_ref[:, sl] += jnp.sum(dyv * on * sg, axis=0, keepdims=True)
            dg_ref[:, sl] = (dyv * on * gn * dsg).astype(BF16)
            don = dyv * gn * sg
            do_ref[:, sl] = r * (don - on * jnp.mean(don * on, axis=-1, keepdims=True))

    return pl.pallas_call(
        body, grid=(lp // tr,),
        in_specs=[_rspec(tr, w), _rspec(tr, w), _rspec(tr, w, col_g // w), _fspec((1, w)),
                  _rspec(tr, w)],
        out_specs=[_rspec(tr, w), _rspec(tr, w), _fspec((1, w))],
        out_shape=[jax.ShapeDtypeStruct((lp, w), F32), jax.ShapeDtypeStruct((lp, w), BF16),
                   jax.ShapeDtypeStruct((1, w), F32)],
        compiler_params=_cp("arbitrary"), name=name)(o_f, o_b, proj, gain, dy)


HG_ROWS = 128
HG_HALVES = (1, 2, 4, 8, 16, 32, 64)


def _hg_gates(zq, z, lbv):
    qh = zq * _sigmoid(zq)
    s = _sigmoid(z)
    f = lbv + (1.0 - lbv) * s
    kk = (1.0 - lbv) * _sigmoid(-z)
    return qh, s, f, jnp.log(f), kk


def _block_cumsum(g, pos, suffix):
    x = g
    for k in HG_HALVES:
        if suffix:
            x = x + jnp.where(pos < HG_ROWS - k, pltpu.roll(x, HG_ROWS - k, 0), 0.0)
        else:
            x = x + jnp.where(pos >= k, pltpu.roll(x, k, 0), 0.0)
    return x


def _pair_levels(b, pos, reverse):
    out = []
    first = b
    for m in HG_HALVES:
        if m > 1:
            first = jnp.where((pos & (m - 1)) >= m // 2, pltpu.roll(first, m // 2, 0), first)
        nxt = pltpu.roll(first, HG_ROWS - m, 0)
        upper = (pos & (2 * m - 1)) >= m
        if reverse:
            eq = jnp.where(upper, 0.0, jnp.exp(jnp.minimum(b - nxt, 0.0)))
            ek = jnp.where(upper, jnp.exp(jnp.minimum(first - b, 0.0)), 0.0)
        else:
            eq = jnp.where(upper, jnp.exp(jnp.minimum(b - first, 0.0)), 0.0)
            ek = jnp.where(upper, 0.0, jnp.exp(jnp.minimum(nxt - b, 0.0)))
        out.append((m.bit_length(), eq, ek))
    return out


def _hg_scan_fwd(proj, lb, *, reverse, col_q, col_z, col_i, hh, name):
    lp = proj.shape[0]
    n_blocks = lp // HG_ROWS
    last = 0 if reverse else HG_ROWS - 1

    def body(q_ref, z_ref, i_ref, lb_ref, o_ref, st_ref):
        lbv = lb_ref[...]
        pos = lax.broadcasted_iota(jnp.int32, (HG_ROWS, 1), 0)
        ri = lax.broadcasted_iota(jnp.int32, (HG_ROWS, HG_ROWS), 0)
        ci = lax.broadcasted_iota(jnp.int32, (HG_ROWS, HG_ROWS), 1)

        def block(bi, st):
            bb = (n_blocks - 1 - bi) if reverse else bi
            r0 = pl.multiple_of(bb * HG_ROWS, HG_ROWS)
            v16 = i_ref[pl.ds(r0, HG_ROWS), :].astype(BF16)
            qh, _, _, g, kk = _hg_gates(q_ref[pl.ds(r0, HG_ROWS), :], z_ref[pl.ds(r0, HG_ROWS), :],
                                        lbv)
            b = _block_cumsum(g, pos, reverse)
            bl = b[last:last + 1, :]
            qe = (qh * jnp.exp(b)).astype(BF16)
            kd = (kk * jnp.exp(bl - b)).astype(BF16)
            a = jnp.where(ri == ci, jnp.sum(qh * kk, axis=1, keepdims=True), 0.0)
            for sh, eq, ek in _pair_levels(b, pos, reverse):
                a = a + jnp.where((ri >> sh) == (ci >> sh),
                                  _nt((qh * eq).astype(BF16), (kk * ek).astype(BF16)), 0.0)
            st_ref[bb] = st
            o_ref[pl.ds(r0, HG_ROWS), :] = _nn(a.astype(BF16), v16) + _nt(qe, st.astype(BF16))
            return jnp.exp(bl) * st + _tn(v16, kd)

        lax.fori_loop(0, n_blocks, block, jnp.zeros((HG_DK, HG_DK), F32))

    cspec = lambda col: pl.BlockSpec((lp, HG_DK), lambda h: (0, col // HG_DK + h))
    return pl.pallas_call(
        body, grid=(hh,),
        in_specs=[cspec(col_q), cspec(col_z), cspec(col_i),
                  pl.BlockSpec((None, 1, HG_DK), lambda h: (h, 0, 0))],
        out_specs=[pl.BlockSpec((lp, HG_DK), lambda h: (0, h)),
                   pl.BlockSpec((None, n_blocks, HG_DK, HG_DK), lambda h: (h, 0, 0, 0))],
        out_shape=[jax.ShapeDtypeStruct((lp, hh * HG_DK), F32),
                   jax.ShapeDtypeStruct((hh, n_blocks, HG_DK, HG_DK), F32)],
        compiler_params=_cp("parallel"), name=name)(proj, proj, proj, lb)


def _hg_scan_bwd(proj, lb, states, do, *, reverse, col_q, col_z, col_i, hh, name):
    lp = proj.shape[0]
    n_blocks = lp // HG_ROWS
    last = 0 if reverse else HG_ROWS - 1

    def body(q_ref, z_ref, i_ref, lb_ref, st_ref, do_ref, dq_ref, dz_ref, dv_ref, dlb_ref):
        lbv = lb_ref[...]
        pos = lax.broadcasted_iota(jnp.int32, (HG_ROWS, 1), 0)
        ri = lax.broadcasted_iota(jnp.int32, (HG_ROWS, HG_ROWS), 0)
        ci = lax.broadcasted_iota(jnp.int32, (HG_ROWS, HG_ROWS), 1)

        def block(bi, carry):
            dst, dlb = carry
            bb = bi if reverse else (n_blocks - 1 - bi)
            r0 = pl.multiple_of(bb * HG_ROWS, HG_ROWS)
            zq = q_ref[pl.ds(r0, HG_ROWS), :]
            v16 = i_ref[pl.ds(r0, HG_ROWS), :].astype(BF16)
            do16 = do_ref[pl.ds(r0, HG_ROWS), :].astype(BF16)
            qh, s, f, g, kk = _hg_gates(zq, z_ref[pl.ds(r0, HG_ROWS), :], lbv)
            b = _block_cumsum(g, pos, reverse)
            bl = b[last:last + 1, :]
            eb = jnp.exp(b)
            ebl = jnp.exp(bl - b)
            decay = jnp.exp(bl)
            qe16 = (qh * eb).astype(BF16)
            kd16 = (kk * ebl).astype(BF16)
            st = st_ref[bb]
            st16, dst16 = st.astype(BF16), dst.astype(BF16)
            same_row = ri == ci
            da = _nt(do16, v16)
            da_diag = jnp.sum(jnp.where(same_row, da, 0.0), axis=1, keepdims=True)
            dq_state = eb * _nn(do16, st16)
            dk_state = ebl * _nn(v16, dst16)
            dq = dq_state + da_diag * kk
            dk = dk_state + da_diag * qh
            dbl = (decay * jnp.sum(st * dst, axis=0, keepdims=True)
                   + jnp.sum(kk * dk_state, axis=0, keepdims=True))
            db = qh * dq_state - kk * dk_state + jnp.where(pos == last, dbl, 0.0)
            a = jnp.where(same_row, jnp.sum(qh * kk, axis=1, keepdims=True), 0.0)
            for sh, eq, ek in _pair_levels(b, pos, reverse):
                same = (ri >> sh) == (ci >> sh)
                q16, k16 = (qh * eq).astype(BF16), (kk * ek).astype(BF16)
                a = a + jnp.where(same, _nt(q16, k16), 0.0)
                da16 = jnp.where(same, da, 0.0).astype(BF16)
                gq, gk = _nn(da16, k16), _tn(da16, q16)
                dq = dq + eq * gq
                dk = dk + ek * gk
                db = db + (q16.astype(F32) * gq - k16.astype(F32) * gk)
            dg = _block_cumsum(db, pos, not reverse)
            df = dg / f - dk
            sq = _sigmoid(zq)
            dq_ref[pl.ds(r0, HG_ROWS), :] = dq * (sq + zq * sq * (1.0 - sq))
            dz_ref[pl.ds(r0, HG_ROWS), :] = df * (1.0 - lbv) * s * (1.0 - s)
            dv_ref[pl.ds(r0, HG_ROWS), :] = _nt(kd16, dst16) + _tn(a.astype(BF16), do16)
            return (decay * dst + _tn(do16, qe16),
                    dlb + jnp.sum(df * (1.0 - s), axis=0, keepdims=True))

        _, dlb = lax.fori_loop(0, n_blocks, block,
                               (jnp.zeros((HG_DK, HG_DK), F32), jnp.zeros((1, HG_DK), F32)))
        dlb_ref[...] = dlb

    cspec = lambda col: pl.BlockSpec((lp, HG_DK), lambda h: (0, col // HG_DK + h))
    ospec = pl.BlockSpec((lp, HG_DK), lambda h: (0, h))
    sds = jax.ShapeDtypeStruct((lp, hh * HG_DK), F32)
    return pl.pallas_call(
        body, grid=(hh,),
        in_specs=[cspec(col_q), cspec(col_z), cspec(col_i),
                  pl.BlockSpec((None, 1, HG_DK), lambda h: (h, 0, 0)),
                  pl.BlockSpec((None, n_blocks, HG_DK, HG_DK), lambda h: (h, 0, 0, 0)),
                  ospec],
        out_specs=[ospec, ospec, ospec, pl.BlockSpec((None, 1, HG_DK), lambda h: (h, 0, 0))],
        out_shape=[sds, sds, sds, jax.ShapeDtypeStruct((hh, 1, HG_DK), F32)],
        compiler_params=_cp("parallel"), name=name)(proj, proj, proj, lb, states, do)


def _na_rows(r, rows):
    rs = jnp.clip(r - NA_WIN_H // 2, 0, rows - NA_WIN_H)
    i0 = rs - r + (NA_WIN_H - 1)
    q0 = pl.multiple_of(N_META + GRID_W * r, 16)
    k0 = pl.multiple_of(N_META + GRID_W * rs, 16)
    return i0, q0, k0


def _na_scores(q16, k16, km16, tb_ref, i0, scale):
    s = _nt(q16, k16) * scale
    bias = jnp.concatenate([tb_ref[i0 + j] for j in range(NA_WIN_H)], axis=1)
    return s + bias, _nt(q16, km16) * scale


NA_HB = LANES // NA_HEAD_DIM


def _na_cols(h):
    return slice(h * NA_HEAD_DIM, (h + 1) * NA_HEAD_DIM)


def _na_fwd(proj, tb, *, n_tok, nh, name):
    lp = proj.shape[0]
    dh, hb = NA_HEAD_DIM, NA_HB
    naw = nh * dh
    rows = n_tok // GRID_W
    scale = dh ** -0.5
    kw = NA_WIN_H * GRID_W

    def body(q_ref, k_ref, v_ref, tb_ref, o_ref, lse_ref, q16_ref, k16_ref, v16_ref):
        o_ref[...] = jnp.zeros_like(o_ref)
        lse_ref[...] = jnp.zeros_like(lse_ref)
        q16_ref[...] = q_ref[...].astype(BF16)
        k16_ref[...] = k_ref[...].astype(BF16)
        v16_ref[...] = v_ref[...].astype(BF16)
        kms, vms = [], []
        for h in range(hb):
            km = k16_ref[0:N_META, _na_cols(h)]
            vm = v16_ref[0:N_META, _na_cols(h)]
            s = _nt(q16_ref[0:N_META, _na_cols(h)], km) * scale
            m = jnp.max(s, axis=1, keepdims=True)
            p = jnp.exp(s - m)
            l = jnp.sum(p, axis=1, keepdims=True)
            o_ref[0:N_META, _na_cols(h)] = _nn(p.astype(BF16), vm) / l
            lse_ref[h, 0:N_META, :] = m + jnp.log(l)
            kms.append(km)
            vms.append(vm)

        def step(r, carry):
            i0, q0, k0 = _na_rows(r, rows)
            for h in range(hb):
                q16 = q16_ref[pl.ds(q0, GRID_W), _na_cols(h)]
                k16 = k16_ref[pl.ds(k0, kw), _na_cols(h)]
                v16 = v16_ref[pl.ds(k0, kw), _na_cols(h)]
                s, sm = _na_scores(q16, k16, kms[h], tb_ref.at[h], i0, scale)
                m = jnp.maximum(jnp.max(s, axis=1, keepdims=True),
                                jnp.max(sm, axis=1, keepdims=True))
                p = jnp.exp(s - m)
                pm = jnp.exp(sm - m)
                l = jnp.sum(p, axis=1, keepdims=True) + jnp.sum(pm, axis=1, keepdims=True)
                o = _nn(p.astype(BF16), v16) + _nn(pm.astype(BF16), vms[h])
                o_ref[pl.ds(q0, GRID_W), _na_cols(h)] = o / l
                lse_ref[h, pl.ds(q0, GRID_W), :] = m + jnp.log(l)
            return carry

        lax.fori_loop(0, rows, step, 0, unroll=2)

    cblk = lambda col: pl.BlockSpec((lp, LANES), lambda g: (0, col // LANES + g))
    return pl.pallas_call(
        body, grid=(nh // hb,),
        in_specs=[cblk(0), cblk(naw), cblk(2 * naw),
                  pl.BlockSpec((hb, 2 * NA_WIN_H - 1, GRID_W, GRID_W), lambda g: (g, 0, 0, 0))],
        out_specs=[cblk(0), pl.BlockSpec((hb, lp, 1), lambda g: (g, 0, 0))],
        out_shape=[jax.ShapeDtypeStruct((lp, naw), F32), jax.ShapeDtypeStruct((nh, lp, 1), F32)],
        scratch_shapes=[pltpu.VMEM((lp, LANES), BF16)] * 3,
        compiler_params=_cp("parallel"), name=name)(proj, proj, proj, tb)


def _na_bwd(proj, tb, o, lse, do, *, n_tok, nh, name):
    lp = proj.shape[0]
    dh, hb = NA_HEAD_DIM, NA_HB
    naw = nh * dh
    rows = n_tok // GRID_W
    scale = dh ** -0.5
    kw = NA_WIN_H * GRID_W

    def body(q_ref, k_ref, v_ref, tb_ref, o_ref, lse_ref, do_ref, dq_ref, dk_ref, dv_ref, dtb_ref,
             q16_ref, k16_ref, v16_ref):
        dq_ref[...] = jnp.zeros_like(dq_ref)
        dk_ref[...] = jnp.zeros_like(dk_ref)
        dv_ref[...] = jnp.zeros_like(dv_ref)
        dtb_ref[...] = jnp.zeros_like(dtb_ref)
        q16_ref[...] = q_ref[...].astype(BF16)
        k16_ref[...] = k_ref[...].astype(BF16)
        v16_ref[...] = v_ref[...].astype(BF16)
        kms, vms, init = [], [], []
        for h in range(hb):
            km = k16_ref[0:N_META, _na_cols(h)]
            vm = v16_ref[0:N_META, _na_cols(h)]
            qm = q16_ref[0:N_META, _na_cols(h)]
            dom = do_ref[0:N_META, _na_cols(h)]
            p = jnp.exp(_nt(qm, km) * scale - lse_ref[h, 0:N_META, :])
            dp = _nt(dom.astype(BF16), vm)
            delta = jnp.sum(dom * o_ref[0:N_META, _na_cols(h)], axis=1, keepdims=True)
            ds = (p * (dp - delta)).astype(BF16)
            dq_ref[0:N_META, _na_cols(h)] = _nn(ds, km) * scale
            init += [_tn(ds, qm) * scale, _tn(p.astype(BF16), dom.astype(BF16))]
            kms.append(km)
            vms.append(vm)

        def step(r, carry):
            i0, q0, k0 = _na_rows(r, rows)
            out = []
            for h in range(hb):
                q16 = q16_ref[pl.ds(q0, GRID_W), _na_cols(h)]
                k16 = k16_ref[pl.ds(k0, kw), _na_cols(h)]
                v16 = v16_ref[pl.ds(k0, kw), _na_cols(h)]
                s, sm = _na_scores(q16, k16, kms[h], tb_ref.at[h], i0, scale)
                lse = lse_ref[h, pl.ds(q0, GRID_W), :]
                p = jnp.exp(s - lse)
                pm = jnp.exp(sm - lse)
                dov = do_ref[pl.ds(q0, GRID_W), _na_cols(h)]
                do16 = dov.astype(BF16)
                delta = jnp.sum(dov * o_ref[pl.ds(q0, GRID_W), _na_cols(h)], axis=1, keepdims=True)
                ds = p * (_nt(do16, v16) - delta)
                dsm = (pm * (_nt(do16, vms[h]) - delta)).astype(BF16)
                ds16 = ds.astype(BF16)
                dq_ref[pl.ds(q0, GRID_W), _na_cols(h)] = (_nn(ds16, k16) + _nn(dsm, kms[h])) * scale
                dk_ref[pl.ds(k0, kw), _na_cols(h)] += _tn(ds16, q16) * scale
                dv_ref[pl.ds(k0, kw), _na_cols(h)] += _tn(p.astype(BF16), do16)
                for j in range(NA_WIN_H):
                    dtb_ref[h, i0 + j] += ds[:, j * GRID_W:(j + 1) * GRID_W]
                out += [carry[2 * h] + _tn(dsm, q16) * scale,
                        carry[2 * h + 1] + _tn(pm.astype(BF16), do16)]
            return tuple(out)

        fin = lax.fori_loop(0, rows, step, tuple(init))
        for h in range(hb):
            dk_ref[0:N_META, _na_cols(h)] += fin[2 * h]
            dv_ref[0:N_META, _na_cols(h)] += fin[2 * h + 1]

    cblk = lambda col: pl.BlockSpec((lp, LANES), lambda g: (0, col // LANES + g))
    tbs = pl.BlockSpec((hb, 2 * NA_WIN_H - 1, GRID_W, GRID_W), lambda g: (g, 0, 0, 0))
    sds = jax.ShapeDtypeStruct((lp, naw), F32)
    return pl.pallas_call(
        body, grid=(nh // hb,),
        in_specs=[cblk(0), cblk(naw), cblk(2 * naw), tbs, cblk(0),
                  pl.BlockSpec((hb, lp, 1), lambda g: (g, 0, 0)), cblk(0)],
        out_specs=[cblk(0), cblk(0), cblk(0), tbs],
        out_shape=[sds, sds, sds, jax.ShapeDtypeStruct(tb.shape, F32)],
        scratch_shapes=[pltpu.VMEM((lp, LANES), BF16)] * 3,
        compiler_params=_cp("parallel"), name=name)(proj, proj, proj, tb, o, lse, do)


def _rpb_onehot():
    c = np.arange(GRID_W)[:, None]
    w = np.arange(GRID_W)[None, :]
    cs = np.clip(c - NA_WIN_W // 2, 0, GRID_W - NA_WIN_W)
    in_win = (w >= cs) & (w < cs + NA_WIN_W)
    dc = np.clip(w - c, -(NA_WIN_W - 1), NA_WIN_W - 1) + NA_WIN_W - 1
    oh = np.zeros((LANES, GRID_W * GRID_W), np.float32)
    flat = np.arange(GRID_W * GRID_W).reshape(GRID_W, GRID_W)
    oh[dc[in_win], flat[in_win]] = 1.0
    neg = np.where(in_win, 0.0, -1e30).astype(np.float32).reshape(1, -1)
    return oh, neg


def _assemble_dproj(dq_na, dk_na, dv_na, dq_f, dq_b, dz_f, dz_b, dv_f, dv_b, dg, dgn, dgh, *, name):
    lp, naw = dq_na.shape
    hgw = dq_f.shape[1]
    d = dgn.shape[1]
    cols = 3 * naw + 5 * hgw + 2 * d
    tr = _row_tile(lp)

    def body(nq_ref, nk_ref, nv_ref, qf_ref, qb_ref, zf_ref, zb_ref, vf_ref, vb_ref, g_ref, gn_ref,
             gh_ref, o_ref):
        o_ref[:, 0:naw] = nq_ref[...].astype(BF16)
        o_ref[:, naw:2 * naw] = nk_ref[...].astype(BF16)
        o_ref[:, 2 * naw:3 * naw] = nv_ref[...].astype(BF16)
        c0 = 3 * naw
        o_ref[:, c0:c0 + hgw] = (qf_ref[...] + qb_ref[...]).astype(BF16)
        o_ref[:, c0 + hgw:c0 + 2 * hgw] = zf_ref[...].astype(BF16)
        o_ref[:, c0 + 2 * hgw:c0 + 3 * hgw] = zb_ref[...].astype(BF16)
        o_ref[:, c0 + 3 * hgw:c0 + 4 * hgw] = (vf_ref[...] + vb_ref[...]).astype(BF16)
        o_ref[:, c0 + 4 * hgw:c0 + 5 * hgw] = g_ref[...]
        o_ref[:, c0 + 5 * hgw:c0 + 5 * hgw + d] = gn_ref[...]
        o_ref[:, c0 + 5 * hgw + d:] = gh_ref[...]

    hg, na = _rspec(tr, hgw), _rspec(tr, naw)
    return pl.pallas_call(
        body, grid=(lp // tr,),
        in_specs=[na, na, na, hg, hg, hg, hg, hg, hg, hg, _rspec(tr, d), _rspec(tr, d)],
        out_specs=_rspec(tr, cols),
        out_shape=jax.ShapeDtypeStruct((lp, cols), BF16),
        compiler_params=_cp("parallel"), name=name)(dq_na, dk_na, dv_na, dq_f, dq_b, dz_f, dz_b,
                                                    dv_f, dv_b, dg, dgn, dgh)


def _adamw(w, g, m, v, *, name):
    rows, cols = w.shape
    tr = 256 if rows % 256 == 0 else rows

    def body(w_ref, g_ref, m_ref, v_ref, d_ref, mo_ref, vo_ref):
        gv = g_ref[...]
        mn = ADAM_B1 * m_ref[...] + (1.0 - ADAM_B1) * gv
        vn = ADAM_B2 * v_ref[...] + (1.0 - ADAM_B2) * (gv * gv)
        m_hat = mn / (1.0 - ADAM_B1 ** ADAM_STEP)
        v_hat = vn / (1.0 - ADAM_B2 ** ADAM_STEP)
        d_ref[...] = -ADAM_LR * (m_hat / (jnp.sqrt(v_hat) + ADAM_EPS) + ADAM_WD * w_ref[...])
        mo_ref[...] = mn
        vo_ref[...] = vn

    spec = _rspec(tr, cols)
    sds = jax.ShapeDtypeStruct((rows, cols), F32)
    return pl.pallas_call(
        body, grid=(rows // tr,), in_specs=[spec] * 4, out_specs=[spec] * 3, out_shape=[sds] * 3,
        compiler_params=_cp("parallel"), name=name)(w, g, m, v)


def _local_step(x, tgt, meta, first_weight, rest_weights, g_mix, g_mlp, g_fin, hg_gain, rpb, lb,
                early_grads=None, late_grad=None):
    n_tok, d = x.shape
    hgw = hg_gain.shape[1]
    nh, hh = rpb.shape[0], hgw // HG_DK
    naw = nh * NA_HEAD_DIM
    l_real = N_META + n_tok
    lp = -(-l_real // ROW_ALIGN) * ROW_ALIGN
    n_chunks = l_real // HG_CHUNK
    pad = lp - l_real
    col_qhg = 3 * naw
    col_zf, col_zb, col_i, col_g = (col_qhg + hgw, col_qhg + 2 * hgw, col_qhg + 3 * hgw,
                                    col_qhg + 4 * hgw)
    col_gate = col_qhg + 5 * hgw

    zpad = jnp.zeros((pad, d), F32)
    h0 = jnp.concatenate([meta, x, zpad], axis=0)
    tgt_p = jnp.concatenate([jnp.zeros((N_META, d), F32), tgt, zpad], axis=0)

    oh_np, neg_np = _rpb_onehot()
    oh = jnp.asarray(oh_np)
    rpb_p = jnp.pad(rpb.reshape(nh * (2 * NA_WIN_H - 1), 2 * NA_WIN_W - 1),
                    ((0, 0), (0, LANES - (2 * NA_WIN_W - 1))))
    tb = _matmul(rpb_p, oh, tm=rpb_p.shape[0], tn=512, tk=LANES, precision=HIGHEST,
                 name="rpb_expand")
    tb = (tb + jnp.asarray(neg_np)).reshape(nh, 2 * NA_WIN_H - 1, GRID_W, GRID_W)

    a = _rmsnorm_fwd(h0, g_mix, name="norm_mix")
    w_in = first_weight(a)
    proj = _matmul(a, w_in, name="mm_in")
    o_na, lse = _na_fwd(proj, tb, n_tok=n_tok, nh=nh, name="na_fwd")
    lb_f = lb[0].reshape(hh, 1, HG_DK)
    lb_b = lb[1].reshape(hh, 1, HG_DK)
    scan_kw = dict(col_q=col_qhg, col_i=col_i, hh=hh)
    o_f, st_f = _hg_scan_fwd(proj, lb_f, reverse=False, col_z=col_zf, name="hg_scan_f", **scan_kw)
    o_b, st_b = _hg_scan_fwd(proj, lb_b, reverse=True, col_z=col_zb, name="hg_scan_b", **scan_kw)
    o_hg = _hg_out(o_f, o_b, proj, hg_gain, col_g=col_g, name="hg_out")
    w_na, w_hg, w_o, w_up, w_down = rest_weights(o_hg)
    y_na = _matmul(o_na, w_na, name="mm_na_out")
    gates = ((proj, col_gate), (proj, col_gate + d))

    def mix_gates(acc, gn, gh, yn):
        return acc, _sigmoid(gn) * yn + _sigmoid(gh) * acc

    def mix_gates_bwd(dmix, gn, gh, yn, yh):
        sn, sh = _sigmoid(gn), _sigmoid(gh)
        return dmix * sn, dmix * sh, dmix * yn * sn * (1.0 - sn), dmix * yh * sh * (1.0 - sh)

    y_hg, mix = _matmul(o_hg, w_hg, name="mm_hg_out", epilogue=mix_gates,
                        tiles=(*gates, (y_na, 0)), out_dtypes=(F32, BF16))
    t1 = _matmul(mix, w_o, name="mm_o")
    h1, mlp_in = _residual_norm(h0, t1, g_mlp, name="resid_norm_mlp")
    u, act = _matmul(mlp_in, w_up, name="mm_up", out_dtypes=(F32, BF16),
                     epilogue=lambda acc: (acc, jnp.square(jnp.maximum(acc, 0.0))))
    t2 = _matmul(act, w_down, name="mm_down")
    dh2, loss, dg_fin = _final_loss(h1, t2, g_fin, tgt_p, n_tok=n_tok, name="final_loss")

    (du,) = _matmul(dh2, w_down, tb=True, name="mm_down_dx", tiles=((u, 0),), out_dtypes=(BF16,),
                    epilogue=lambda acc, uv: (acc * 2.0 * jnp.maximum(uv, 0.0),))
    dw_down = _matmul(act, dh2, ta=True, name="mm_down_dw")
    dm = _matmul(du, w_up, tb=True, name="mm_up_dx")
    dw_up = _matmul(mlp_in, du, ta=True, name="mm_up_dw")
    dh1, dg_mlp = _rmsnorm_bwd_add(h1, g_mlp, dm, dh2, name="norm_mlp_bwd")
    dy_na, dy_hg, dgn, dgh = _matmul(dh1, w_o, tb=True, name="mm_o_dx", epilogue=mix_gates_bwd,
                                     tiles=(*gates, (y_na, 0), (y_hg, 0)), out_dtypes=(BF16,) * 4)
    dw_o = _matmul(mix, dh1, ta=True, name="mm_o_dw")
    do_na = _matmul(dy_na, w_na, tb=True, name="mm_na_out_dx")
    dw_na = _matmul(o_na, dy_na, ta=True, name="mm_na_out_dw")
    do_hg = _matmul(dy_hg, w_hg, tb=True, name="mm_hg_out_dx")
    dw_hg = _matmul(o_hg, dy_hg, ta=True, name="mm_hg_out_dw")
    token = early_grads([dw_na, dw_hg, dw_o, dw_up, dw_down]) if early_grads else None
    if token is not None:
        hg_gain = hg_gain + token[0:1, 0:1]
    d_o, dg_hg, d_gain = _hg_out_bwd(o_f, o_b, proj, hg_gain, do_hg, col_g=col_g, name="hg_out_bwd")
    dq_f, dz_f, dv_f, dlb_f = _hg_scan_bwd(proj, lb_f, st_f, d_o, reverse=False, col_z=col_zf,
                                           name="hg_scan_f_bwd", **scan_kw)
    dq_b, dz_b, dv_b, dlb_b = _hg_scan_bwd(proj, lb_b, st_b, d_o, reverse=True, col_z=col_zb,
                                           name="hg_scan_b_bwd", **scan_kw)
    dq_na, dk_na, dv_na, dtb = _na_bwd(proj, tb, o_na, lse, do_na, n_tok=n_tok, nh=nh, name="na_bwd")
    dproj = _assemble_dproj(dq_na, dk_na, dv_na, dq_f, dq_b, dz_f, dz_b, dv_f, dv_b, dg_hg, dgn,
                            dgh, name="assemble_dproj")
    dw_in = _matmul(a, dproj, ta=True, name="mm_in_dw")
    token = late_grad(dw_in) if late_grad else None
    da = _matmul(dproj, w_in, tb=True, name="mm_in_dx", after=token)
    dh0, dg_mix = _rmsnorm_bwd_add(h0, g_mix, da, dh1, name="norm_mix_bwd")
    d_rpb = _matmul(dtb.reshape(nh * (2 * NA_WIN_H - 1), GRID_W * GRID_W), oh, tb=True,
                    tm=nh * (2 * NA_WIN_H - 1), tn=LANES, tk=1024, precision=HIGHEST,
                    name="rpb_reduce")
    d_lb = jnp.concatenate([dlb_f.reshape(1, hgw), dlb_b.reshape(1, hgw)], axis=0)
    return (loss, dh0[N_META:l_real], dh0[:N_META], dw_in, dw_na, dw_hg, dw_o, dw_up, dw_down,
            dg_mix, dg_mlp, dg_fin, d_gain, d_rpb, d_lb)


N_CHIPS = 4
N_DEV = 8
ANY = pl.BlockSpec(memory_space=pl.ANY)


def _place():
    x, y, c = lax.axis_index("x"), lax.axis_index("y"), lax.axis_index("c")
    others = []
    for j in (1, 2, 3):
        tx = (1 - x) if (j >> 1) else x
        ty = (1 - y) if (j & 1) else y
        others.append((tx, ty))
    return x, y, c, others


def _piece(ref, axis, k, half, rh, cs):
    if axis == 1:
        return ref.at[pl.ds(pl.multiple_of(half * rh, 16), rh), pl.ds(pl.multiple_of(k * cs, LANES), cs)]
    return ref.at[pl.ds(pl.multiple_of(k * 2 * rh + half * rh, 16), rh), :]


def _cast_into_full(shard, axis, place, *, name):
    r, cs = shard.shape
    full = (r, cs * N_CHIPS) if axis == 1 else (r * N_CHIPS, cs)
    tr = next(t for t in (256, 128, 64, 32, 16) if r % t == 0)
    nt = r // tr

    def body(p_ref, s_ref, o_ref):
        o_ref[...] = s_ref[...].astype(BF16)

    if axis == 1:
        omap = lambda i, p_ref: (i, p_ref[0])
    else:
        omap = lambda i, p_ref: (p_ref[0] * nt + i, 0)
    return pl.pallas_call(
        body,
        grid_spec=pltpu.PrefetchScalarGridSpec(
            num_scalar_prefetch=1, grid=(nt,),
            in_specs=[pl.BlockSpec((tr, cs), lambda i, p_ref: (i, 0))],
            out_specs=pl.BlockSpec((tr, cs), omap)),
        out_shape=jax.ShapeDtypeStruct(full, BF16),
        compiler_params=_cp("parallel"), name=name)(place, shard)


HBM_SPEC = pl.BlockSpec(memory_space=pltpu.HBM)
SEM_SPEC = pl.BlockSpec(memory_space=pltpu.SEMAPHORE)
SPLIT_COPY = pltpu.CompilerParams(has_side_effects=pltpu.SideEffectType.DATAFLOW_SIDE_EFFECTING)
TOKEN = jax.ShapeDtypeStruct((8, LANES), F32)


def _geo(fulls, axes):
    out = []
    for f, ax in zip(fulls, axes):
        r, cs = (f.shape[0], f.shape[1] // N_CHIPS) if ax == 1 else (f.shape[0] // N_CHIPS, f.shape[1])
        out.append((ax, r // 2, cs))
    return out


def _gather_copies(refs, geo, send_sems, recv_sems):
    x, y, c, others = _place()
    chip = 2 * x + y
    cps = []
    for i, (ax, rh, cs) in enumerate(geo):
        mine = _piece(refs[i], ax, chip, c, rh, cs)
        for j, (tx, ty) in enumerate(others):
            cps.append(pltpu.make_async_remote_copy(
                src_ref=mine, dst_ref=mine, send_sem=send_sems.at[3 * i + j],
                recv_sem=recv_sems.at[3 * i + j], device_id=(tx, ty, c), device_id_type=MESH))
    return cps


def _allgather_start(fulls, axes, after, *, name):
    n = len(fulls)
    geo = _geo(fulls, axes)

    def body(*refs):
        w_refs = refs[:n]
        send_sems, recv_sems = refs[n + 1], refs[n + 2]
        token = refs[2 * n + 3]
        for cp in _gather_copies(w_refs, geo, send_sems, recv_sems):
            cp.start()
        token[...] = jnp.zeros_like(token)

    out = pl.pallas_call(
        body, name=name,
        out_shape=(pltpu.SemaphoreType.DMA((3 * n,)), pltpu.SemaphoreType.DMA((3 * n,)),
                   *[pltpu.HBM(f.shape, f.dtype) for f in fulls], TOKEN),
        in_specs=[HBM_SPEC] * n + [ANY],
        out_specs=(SEM_SPEC, SEM_SPEC, *[HBM_SPEC] * n, pl.BlockSpec(memory_space=pltpu.VMEM)),
        input_output_aliases={i: 2 + i for i in range(n)},
        compiler_params=SPLIT_COPY,
    )(*[pltpu.with_memory_space_constraint(f, pltpu.HBM) for f in fulls], after)
    return out[0], out[1], list(out[2:2 + n]), out[2 + n]


def _allgather_wait(send_sems, recv_sems, fulls, axes, after, *, name):
    n = len(fulls)
    geo = _geo(fulls, axes)

    def body(*refs):
        w_refs = refs[:n]
        for cp in _gather_copies(w_refs, geo, refs[n], refs[n + 1]):
            cp.wait_send()
            cp.wait_recv()

    return list(pl.pallas_call(
        body, name=name,
        out_shape=[pltpu.HBM(f.shape, f.dtype) for f in fulls],
        in_specs=[HBM_SPEC] * n + [SEM_SPEC, SEM_SPEC, ANY],
        out_specs=[HBM_SPEC] * n,
        input_output_aliases={i: i for i in range(n)},
        compiler_params=SPLIT_COPY,
    )(*fulls, send_sems, recv_sems, after))


def _allgather_forward(fulls, axes, *, name):
    n = len(fulls)
    geo = _geo(fulls, axes)

    def body(*refs):
        o_refs = refs[n:2 * n]
        send_sems, recv_sems = refs[2 * n:]
        x, y, c, others = _place()

        def rcopy(i, j, half, to):
            ax, rh, cs = geo[i]
            ref = _piece(o_refs[i], ax, 2 * others[j][0] + others[j][1], half, rh, cs)
            return pltpu.make_async_remote_copy(
                src_ref=ref, dst_ref=ref, send_sem=send_sems.at[3 * i + j],
                recv_sem=recv_sems.at[3 * i + j], device_id=to, device_id_type=MESH)

        cps = [rcopy(i, j, c, (x, y, 1 - c)) for i in range(n) for j in range(3)]
        for cp in cps:
            cp.start()
        for i in range(n):
            for j in range(3):
                rcopy(i, j, 1 - c, (x, y, c)).wait_recv()
        for cp in cps:
            cp.wait_send()

    return list(pl.pallas_call(
        body, in_specs=[ANY] * n, out_specs=[ANY] * n,
        out_shape=[jax.ShapeDtypeStruct(f.shape, f.dtype) for f in fulls],
        input_output_aliases={i: i for i in range(n)},
        scratch_shapes=[pltpu.SemaphoreType.DMA((3 * n,)), pltpu.SemaphoreType.DMA((3 * n,))],
        name=name)(*fulls))


def _scatter_geo(parts, axes):
    out = []
    for p, ax in zip(parts, axes):
        _, rh, cols = p.shape
        out.append((ax, rh, cols // N_CHIPS if ax == 1 else cols))
    return out


def _scatter_copies(p_refs, q_refs, geo, send_sems, recv_sems):
    x, y, c, others = _place()
    chip = 2 * x + y
    cps = []
    for i, (ax, rh, cw) in enumerate(geo):
        for j, (tx, ty) in enumerate(others):
            k = 2 * tx + ty
            src = (p_refs[i].at[0, :, pl.ds(pl.multiple_of(k * cw, LANES), cw)] if ax == 1
                   else p_refs[i].at[k])
            cps.append(pltpu.make_async_remote_copy(
                src_ref=src, dst_ref=q_refs[i].at[chip], send_sem=send_sems.at[3 * i + j],
                recv_sem=recv_sems.at[3 * i + j], device_id=(tx, ty, c), device_id_type=MESH))
    return cps


def _scatter_start(parts, axes, *, name):
    n = len(parts)
    geo = _scatter_geo(parts, axes)
    slots = [pltpu.HBM((N_CHIPS, rh, cw), p.dtype) for p, (_, rh, cw) in zip(parts, geo)]

    def body(*refs):
        p_refs, q_refs = refs[:n], refs[n:2 * n]
        send_sems, recv_sems = refs[2 * n], refs[2 * n + 1]
        token = refs[4 * n + 2]
        for cp in _scatter_copies(p_refs, q_refs, geo, send_sems, recv_sems):
            cp.start()
        token[...] = jnp.zeros_like(token)

    land = [pltpu.with_memory_space_constraint(lax.empty(s.inner_aval.shape, s.inner_aval.dtype), pltpu.HBM)
            for s in slots]
    out = pl.pallas_call(
        body, name=name,
        out_shape=(pltpu.SemaphoreType.DMA((3 * n,)), pltpu.SemaphoreType.DMA((3 * n,)),
                   *[pltpu.HBM(p.shape, p.dtype) for p in parts], *slots, TOKEN),
        in_specs=[HBM_SPEC] * (2 * n),
        out_specs=(SEM_SPEC, SEM_SPEC, *[HBM_SPEC] * (2 * n), pl.BlockSpec(memory_space=pltpu.VMEM)),
        input_output_aliases={i: 2 + i for i in range(2 * n)},
        compiler_params=SPLIT_COPY,
    )(*[pltpu.with_memory_space_constraint(p, pltpu.HBM) for p in parts], *land)
    return out[0], out[1], list(out[2:2 + n]), list(out[2 + n:2 + 2 * n]), out[2 + 2 * n]


def _scatter_wait(send_sems, recv_sems, parts, slots, axes, after, *, name):
    n = len(parts)
    geo = _scatter_geo(parts, axes)

    def body(*refs):
        p_refs, q_refs = refs[:n], refs[n:2 * n]
        for cp in _scatter_copies(p_refs, q_refs, geo, refs[2 * n], refs[2 * n + 1]):
            cp.wait_send()
            cp.wait_recv()

    out = pl.pallas_call(
        body, name=name,
        out_shape=[pltpu.HBM(a.shape, a.dtype) for a in (*parts, *slots)],
        in_specs=[HBM_SPEC] * (2 * n) + [SEM_SPEC, SEM_SPEC, ANY],
        out_specs=[HBM_SPEC] * (2 * n),
        input_output_aliases={i: i for i in range(2 * n)},
        compiler_params=SPLIT_COPY,
    )(*parts, *slots, send_sems, recv_sems, after)
    return list(out[:n]), list(out[n:])


def _sibling_swap(grads, *, name):
    n = len(grads)
    out_shape = [jax.ShapeDtypeStruct((g.shape[0], g.shape[1] // 2, g.shape[2]), g.dtype)
                 for g in grads]

    def body(*refs):
        g_refs, o_refs = refs[:n], refs[n:2 * n]
        send_sems, recv_sems = refs[2 * n:]
        x, y, c, _ = _place()
        cps = []
        for i in range(n):
            rh = grads[i].shape[1] // 2
            src = g_refs[i].at[:, pl.ds(pl.multiple_of((1 - c) * rh, 16), rh), :]
            cp = pltpu.make_async_remote_copy(
                src_ref=src, dst_ref=o_refs[i], send_sem=send_sems.at[i], recv_sem=recv_sems.at[i],
                device_id=(x, y, 1 - c), device_id_type=MESH)
            cp.start()
            cps.append(cp)
        for cp in cps:
            cp.wait()

    return pl.pallas_call(
        body, in_specs=[ANY] * n, out_specs=[ANY] * n, out_shape=out_shape,
        scratch_shapes=[pltpu.SemaphoreType.DMA((n,)), pltpu.SemaphoreType.DMA((n,))],
        name=name)(*grads)


def _pair_add(g3, rx, c_arr, *, out_dtype, name):
    nb, rows, cols = g3.shape
    rh = rows // 2
    tr = next(t for t in (128, 64, 32, 16) if rh % t == 0)
    nt = rh // tr

    def body(c_ref, g_ref, r_ref, o_ref):
        o_ref[...] = (g_ref[...] + r_ref[...]).astype(out_dtype)

    return pl.pallas_call(
        body,
        grid_spec=pltpu.PrefetchScalarGridSpec(
            num_scalar_prefetch=1, grid=(nb, nt),
            in_specs=[pl.BlockSpec((None, tr, cols), lambda b, i, c_ref: (b, c_ref[0] * nt + i, 0)),
                      pl.BlockSpec((None, tr, cols), lambda b, i, c_ref: (b, i, 0))],
            out_specs=pl.BlockSpec((None, tr, cols), lambda b, i, c_ref: (b, i, 0))),
        out_shape=jax.ShapeDtypeStruct((nb, rh, cols), out_dtype),
        compiler_params=_cp("parallel", "parallel"), name=name)(c_arr, g3, rx)


def _sum_slots(q, *, name):
    ns, rows, cols = q.shape
    tr = next(t for t in (128, 64, 32, 16, 8) if rows % t == 0)

    def body(q_ref, o_ref):
        acc = q_ref[0].astype(F32)
        for k in range(1, ns):
            acc = acc + q_ref[k].astype(F32)
        o_ref[...] = acc

    return pl.pallas_call(
        body, grid=(rows // tr,),
        in_specs=[pl.BlockSpec((ns, tr, cols), lambda i: (0, i, 0))],
        out_specs=_rspec(tr, cols),
        out_shape=jax.ShapeDtypeStruct((rows, cols), F32),
        compiler_params=_cp("parallel"), name=name)(q)


def _sum_chips(q, p, place, axis, *, name):
    _, rh, cw = q.shape
    tr = next(t for t in (128, 64, 32, 16) if rh % t == 0)
    nt = rh // tr

    def body(p_ref, *refs):
        q_refs, own_ref, o_ref = refs[:N_CHIPS], refs[N_CHIPS], refs[N_CHIPS + 1]
        chip = p_ref[0]
        acc = jnp.where(chip == 0, own_ref[...], q_refs[0][...]).astype(F32)
        for k in range(1, N_CHIPS):
            acc = acc + jnp.where(chip == k, own_ref[...], q_refs[k][...]).astype(F32)
        o_ref[...] = acc

    def slot_spec(k):
        return pl.BlockSpec((None, tr, cw),
                            lambda i, p_ref: (jnp.where(p_ref[0] == k, (k + 1) % N_CHIPS, k), i, 0))

    if axis == 1:
        own_spec = pl.BlockSpec((None, tr, cw), lambda i, p_ref: (0, i, p_ref[0]))
    else:
        own_spec = pl.BlockSpec((None, tr, cw), lambda i, p_ref: (p_ref[0], i, 0))
    return pl.pallas_call(
        body,
        grid_spec=pltpu.PrefetchScalarGridSpec(
            num_scalar_prefetch=1, grid=(nt,),
            in_specs=[slot_spec(k) for k in range(N_CHIPS)] + [own_spec],
            out_specs=pl.BlockSpec((tr, cw), lambda i, p_ref: (p_ref[1] * nt + i, 0))),
        out_shape=jax.ShapeDtypeStruct((2 * rh, cw), F32),
        compiler_params=_cp("parallel"), name=name)(place, *([q] * N_CHIPS), p)


def _sibling_share(shards, *, name):
    n = len(shards)

    def body(*refs):
        o_refs = refs[n:2 * n]
        send_sems, recv_sems = refs[2 * n:]
        x, y, c, _ = _place()
        cps = []
        for i in range(n):
            rh = shards[i].shape[0] // 2
            mine = o_refs[i].at[pl.ds(pl.multiple_of(c * rh, 8), rh), :]
            cp = pltpu.make_async_remote_copy(
                src_ref=mine, dst_ref=mine, send_sem=send_sems.at[i], recv_sem=recv_sems.at[i],
                device_id=(x, y, 1 - c), device_id_type=MESH)
            cp.start()
            cps.append(cp)
        for i in range(n):
            rh = shards[i].shape[0] // 2
            theirs = o_refs[i].at[pl.ds(pl.multiple_of((1 - c) * rh, 8), rh), :]
            pltpu.make_async_remote_copy(
                src_ref=theirs, dst_ref=theirs, send_sem=send_sems.at[i], recv_sem=recv_sems.at[i],
                device_id=(x, y, c), device_id_type=MESH).wait_recv()
        for cp in cps:
            cp.wait_send()

    return pl.pallas_call(
        body, in_specs=[ANY] * n, out_specs=[ANY] * n,
        out_shape=[jax.ShapeDtypeStruct(h.shape, h.dtype) for h in shards],
        input_output_aliases={i: i for i in range(n)},
        scratch_shapes=[pltpu.SemaphoreType.DMA((n,)), pltpu.SemaphoreType.DMA((n,))],
        name=name)(*shards)


def _gather_all(blk, *, name, after=None):
    rows, cols = blk.shape
    extra = [] if after is None else [after]

    def body(x_ref, *refs):
        out_ref, send_sems, recv_sems, local_sem = refs[len(extra):]
        x, y, c = lax.axis_index("x"), lax.axis_index("y"), lax.axis_index("c")
        me = 4 * x + 2 * y + c
        mine = pltpu.make_async_copy(x_ref, out_ref.at[me], local_sem)
        mine.start()
        cps = []
        for k in range(1, N_DEV):
            tx = (1 - x) if (k >> 2) & 1 else x
            ty = (1 - y) if (k >> 1) & 1 else y
            tc = (1 - c) if k & 1 else c
            cp = pltpu.make_async_remote_copy(
                src_ref=x_ref, dst_ref=out_ref.at[me], send_sem=send_sems.at[k - 1],
                recv_sem=recv_sems.at[k - 1], device_id=(tx, ty, tc), device_id_type=MESH)
            cp.start()
            cps.append(cp)
        for k in range(1, N_DEV):
            tx = (1 - x) if (k >> 2) & 1 else x
            ty = (1 - y) if (k >> 1) & 1 else y
            tc = (1 - c) if k & 1 else c
            got = out_ref.at[4 * tx + 2 * ty + tc]
            pltpu.make_async_remote_copy(
                src_ref=got, dst_ref=got, send_sem=send_sems.at[k - 1], recv_sem=recv_sems.at[k - 1],
                device_id=(x, y, c), device_id_type=MESH).wait_recv()
        for cp in cps:
            cp.wait_send()
        mine.wait()

    vm = pl.BlockSpec(memory_space=pltpu.VMEM)
    return pl.pallas_call(
        body, in_specs=[vm] + [ANY] * len(extra), out_specs=vm,
        out_shape=jax.ShapeDtypeStruct((N_DEV, rows, cols), blk.dtype),
        scratch_shapes=[pltpu.SemaphoreType.DMA((N_DEV - 1,)), pltpu.SemaphoreType.DMA((N_DEV - 1,)),
                        pltpu.SemaphoreType.DMA],
        name=name)(blk, *extra)


def _as_rows(a):
    flat = a.reshape(-1)
    n = flat.shape[0]
    rows = -(-n // (8 * LANES)) * 8
    return jnp.pad(flat, (0, rows * LANES - n)).reshape(rows, LANES)


def _from_rows(p, shape):
    n = int(np.prod(shape))
    return p.reshape(-1)[:n].reshape(shape)


WEIGHT_AXES = (1, 1, 1, 0, 1, 0)
WIRE = BF16


def kernel(x, meta_tokens, w_in, w_na_out, w_hg_out, w_o, w_up, w_down, norm_mix, norm_mlp, norm_final, hg_norm, na_rpb, hg_lb_logits, loss_target, m_meta_tokens, m_w_in, m_w_na_out, m_w_hg_out, m_w_o, m_w_up, m_w_down, m_norm_mix, m_norm_mlp, m_norm_final, m_hg_norm, m_na_rpb, m_hg_lb_logits, v_meta_tokens, v_w_in, v_w_na_out, v_w_hg_out, v_w_o, v_w_up, v_w_down, v_norm_mix, v_norm_mlp, v_norm_final, v_hg_norm, v_na_rpb, v_hg_lb_logits):
    xi, yi, ci = lax.axis_index("x"), lax.axis_index("y"), lax.axis_index("c")
    chip = 2 * xi + yi
    d = x.shape[-1]
    dshard = meta_tokens.shape[1]
    hgw = hg_norm.shape[1]
    lbs = hg_lb_logits.shape[2]
    big = [w_in[0], w_na_out[0], w_hg_out[0], w_o[0], w_up[0], w_down[0]]
    big_m = [m_w_in[0], m_w_na_out[0], m_w_hg_out[0], m_w_o[0], m_w_up[0], m_w_down[0]]
    big_v = [v_w_in[0], v_w_na_out[0], v_w_hg_out[0], v_w_o[0], v_w_up[0], v_w_down[0]]

    place = jnp.stack([chip, ci]).astype(jnp.int32)
    own_w = [_cast_into_full(w, ax, place, name=f"cast_shard_{i}")
             for i, (w, ax) in enumerate(zip(big, WEIGHT_AXES))]
    in_axes, rest_axes = WEIGHT_AXES[:1], WEIGHT_AXES[1:]
    small_in = jnp.concatenate([_as_rows(meta_tokens), _as_rows(hg_lb_logits)], axis=0)
    small_all = _gather_all(small_in, name="gather_small_params")[0::2]
    in_send, in_recv, in_bufs, in_token = _allgather_start(own_w[:1], in_axes, small_all,
                                                           name="weight_allgather_in_start")
    ag_send, ag_recv, ag_bufs, ag_token = _allgather_start(own_w[1:], rest_axes, in_token,
                                                           name="weight_allgather_rest_start")

    def first_weight(after):
        got = _allgather_wait(in_send, in_recv, in_bufs, in_axes, after,
                              name="weight_allgather_in_wait")
        return _allgather_forward(got, in_axes, name="weight_allgather_in_forward")[0]

    def rest_weights(after):
        got = _allgather_wait(ag_send, ag_recv, ag_bufs, rest_axes, after,
                              name="weight_allgather_rest_wait")
        return _allgather_forward(got, rest_axes, name="weight_allgather_rest_forward")

    n_meta_rows = N_META * dshard // LANES
    meta_full = (small_all[:, :n_meta_rows].reshape(N_CHIPS, N_META, dshard)
                 .transpose(1, 0, 2).reshape(N_META, d))
    lbl_full = (small_all[:, n_meta_rows:].reshape(N_CHIPS, -1)[:, :4 * lbs]
                .reshape(N_CHIPS, 2, 2, lbs).transpose(1, 2, 0, 3).reshape(2, 2, N_CHIPS * lbs))
    lb = jax.nn.softmax(lbl_full, axis=1)[:, 0]

    c_arr = ci.reshape(1).astype(jnp.int32)

    def chip_partials(dws, axes, tag):
        g3 = [g.reshape(1, *g.shape) if ax == 1
              else g.reshape(N_CHIPS, g.shape[0] // N_CHIPS, g.shape[1]) for g, ax in zip(dws, axes)]
        rx = _sibling_swap(g3, name=f"grad_sibling_swap_{tag}")
        return [_pair_add(g, r, c_arr, out_dtype=WIRE, name=f"grad_pair_add_{tag}_{i}")
                for i, (g, r) in enumerate(zip(g3, rx))]

    flying = {}

    def scatter_behind(tag, axes):
        def hook(dws):
            send, recv, parts, slots, token = _scatter_start(
                chip_partials(dws, axes, tag), axes, name=f"grad_scatter_{tag}_start")
            flying[tag] = (send, recv, parts, slots)
            return token
        return hook

    def landed(tag, axes, after):
        parts, slots = _scatter_wait(*flying[tag], axes, after, name=f"grad_scatter_{tag}_wait")
        return [_sum_chips(q, p, place, ax, name=f"grad_sum_chips_{tag}_{i}")
                for i, (q, p, ax) in enumerate(zip(slots, parts, axes))]

    (loss, dx, dmeta, *_, dg_mix, dg_mlp, dg_fin, d_gain, d_rpb, d_lb) = _local_step(
        x[0], loss_target[0], meta_full, first_weight, rest_weights,
        norm_mix + ag_token[0:1, 0:1], norm_mlp, norm_final.reshape(1, d), hg_norm, na_rpb[0], lb,
        scatter_behind("rest", rest_axes), lambda dw_in: scatter_behind("in", in_axes)([dw_in]))

    halves_rest = landed("rest", rest_axes, dx)
    halves_in = landed("in", in_axes, halves_rest[-1])
    g_big = _sibling_share(halves_in + halves_rest, name="grad_sibling_share")

    d_rpb_c = d_rpb[:, :2 * NA_WIN_W - 1]
    small_g = [dmeta, dg_mix, dg_mlp, dg_fin, d_gain, d_rpb_c, d_lb, loss]
    packed = jnp.concatenate([_as_rows(a) for a in small_g], axis=0)
    total = _sum_slots(_gather_all(packed, after=halves_in[0], name="gather_small_grads"),
                       name="sum_small_grads")
    offs = np.cumsum([0] + [_as_rows(a).shape[0] for a in small_g])
    take = lambda i, shape: _from_rows(total[offs[i]:offs[i + 1]], shape)
    g_meta_full = take(0, (N_META, d))
    g_norm_mix, g_norm_mlp = take(1, (1, d)), take(2, (1, d))
    g_norm_final = take(3, (d,))
    g_hg_norm = take(4, (1, hgw))
    g_rpb = take(5, na_rpb.shape)
    g_lb = take(6, (2, hgw))
    loss_total = take(7, (1, LANES))[0, 0]
    g_meta = lax.dynamic_slice_in_dim(g_meta_full, chip * dshard, dshard, axis=1)
    dl0 = lb * (1.0 - lb) * g_lb
    g_lbl_full = jnp.stack([dl0, -dl0], axis=1)
    g_lbl = lax.dynamic_slice_in_dim(g_lbl_full, chip * lbs, lbs, axis=2)

    big_out = [_adamw(w, g, m, v, name=f"adamw_{i}")
               for i, (w, g, m, v) in enumerate(zip(big, g_big, big_m, big_v))]
    small_w = [meta_tokens, norm_mix, norm_mlp, norm_final, hg_norm, na_rpb, hg_lb_logits]
    small_gr = [g_meta, g_norm_mix, g_norm_mlp, g_norm_final, g_hg_norm, g_rpb, g_lbl]
    small_m = [m_meta_tokens, m_norm_mix, m_norm_mlp, m_norm_final, m_hg_norm, m_na_rpb, m_hg_lb_logits]
    small_v = [v_meta_tokens, v_norm_mix, v_norm_mlp, v_norm_final, v_hg_norm, v_na_rpb, v_hg_lb_logits]
    pk = lambda lst: jnp.concatenate([_as_rows(a) for a in lst], axis=0)
    sd, sm, sv = _adamw(pk(small_w), pk(small_gr), pk(small_m), pk(small_v), name="adamw_small")
    soffs = np.cumsum([0] + [_as_rows(a).shape[0] for a in small_w])
    unpk = lambda p: [_from_rows(p[soffs[i]:soffs[i + 1]], small_w[i].shape) for i in range(len(small_w))]
    sd, sm, sv = unpk(sd), unpk(sm), unpk(sv)

    def order(bigs, smalls):
        return [smalls[0]] + [b.reshape(1, *b.shape) for b in bigs] + smalls[1:]

    grads = order(g_big, small_gr)
    deltas = order([o[0] for o in big_out], sd)
    new_m = order([o[1] for o in big_out], sm)
    new_v = order([o[2] for o in big_out], sv)
    return (loss_total, dx.reshape(1, *dx.shape), *grads, *deltas, *new_m, *new_v)
```

```python
import functools

import numpy as np
import jax
import jax.numpy as jnp
from jax import lax
from jax.experimental import pallas as pl
from jax.experimental.pallas import tpu as pltpu

F32 = jnp.float32
BF16 = jnp.bfloat16
HIGHEST = lax.Precision.HIGHEST

GRID_W = 64
N_META = 16
EPS = 1e-6
NA_HEAD_DIM = 64
NA_WIN_H = 8
NA_WIN_W = 16
HG_DK = 128
HG_CHUNK = 16
LANES = 128
ROW_ALIGN = 128
VMEM_LIMIT = 48 * 1024 * 1024

ADAM_LR = 0.001
ADAM_B1 = 0.9
ADAM_B2 = 0.999
ADAM_EPS = 1e-08
ADAM_WD = 0.01
ADAM_STEP = 10

MESH = pl.DeviceIdType.MESH


def _cp(*sem):
    return pltpu.CompilerParams(dimension_semantics=sem, vmem_limit_bytes=VMEM_LIMIT)


def _sigmoid(x):
    return 1.0 / (1.0 + jnp.exp(-x))


def _dot(a, b, dims, precision=None):
    return lax.dot_general(a, b, (dims, ((), ())), preferred_element_type=F32, precision=precision)


def _nn(a, b, **kw):
    return _dot(a, b, ((1,), (0,)), **kw)


def _nt(a, b, **kw):
    return _dot(a, b, ((1,), (1,)), **kw)


def _tn(a, b, **kw):
    return _dot(a, b, ((0,), (0,)), **kw)


def _matmul(a, b, *, ta=False, tb=False, tm=None, tn=None, tk=None, out_dtype=F32, name,
            precision=None, after=None, epilogue=None, tiles=(), out_dtypes=None):
    extra = [] if after is None else [after]
    single = out_dtypes is None
    if single:
        out_dtypes = (out_dtype,)
    n_t, n_o = len(tiles), len(out_dtypes)
    if ta:
        kdim, m = a.shape
    else:
        m, kdim = a.shape
    if tb:
        n, k2 = b.shape
    else:
        k2, n = b.shape
    assert kdim == k2, (a.shape, b.shape, ta, tb)
    if tm is None:
        if ta:
            tm = next(t for t in (512, 256, 128, m) if m % t == 0)
        else:
            tm = m // 2 if (m // 2) % 16 == 0 and m > 512 else m
    if tn is None:
        tn = next(t for t in (512, 256, 128, n) if n % t == 0)
    if tk is None:
        tk = kdim if ta else next(t for t in (1024, 512, 256, 128, kdim) if kdim % t == 0)
    assert m % tm == 0 and n % tn == 0 and kdim % tk == 0, (m, n, kdim, tm, tn, tk)
    nk = kdim // tk
    op_dtype = F32 if precision is not None else BF16

    def body(a_ref, b_ref, *refs):
        t_refs = refs[:n_t]
        o_refs = refs[n_t + len(extra):n_t + len(extra) + n_o]
        acc_ref = refs[-1]
        kk = pl.program_id(2)

        @pl.when(kk == 0)
        def _():
            acc_ref[...] = jnp.zeros_like(acc_ref)

        av = a_ref[...].astype(op_dtype)
        bv = b_ref[...].astype(op_dtype)
        dims = ((0 if ta else 1,), (1 if tb else 0,))
        acc_ref[...] += _dot(av, bv, dims, precision=precision)

        @pl.when(kk == nk - 1)
        def _():
            acc = acc_ref[...]
            outs = (acc,) if epilogue is None else epilogue(acc, *[t[...] for t in t_refs])
            for o_ref, val in zip(o_refs, outs):
                o_ref[...] = val.astype(o_ref.dtype)

    a_spec = (pl.BlockSpec((tk, tm), lambda i, j, k: (k, i)) if ta
              else pl.BlockSpec((tm, tk), lambda i, j, k: (i, k)))
    b_spec = (pl.BlockSpec((tn, tk), lambda i, j, k: (j, k)) if tb
              else pl.BlockSpec((tk, tn), lambda i, j, k: (k, j)))
    for _, off in tiles:
        assert off % tn == 0, (off, tn)
    t_specs = [pl.BlockSpec((tm, tn), functools.partial(lambda i, j, k, o: (i, o + j), o=off // tn))
               for _, off in tiles]
    o_spec = pl.BlockSpec((tm, tn), lambda i, j, k: (i, j))
    outs = pl.pallas_call(
        body,
        grid=(m // tm, n // tn, nk),
        in_specs=[a_spec, b_spec] + t_specs + [pl.BlockSpec(memory_space=pl.ANY)] * len(extra),
        out_specs=[o_spec] * n_o,
        out_shape=[jax.ShapeDtypeStruct((m, n), dt) for dt in out_dtypes],
        scratch_shapes=[pltpu.VMEM((tm, tn), F32)],
        compiler_params=_cp("parallel", "parallel", "arbitrary"),
        name=name,
    )(a, b, *[t for t, _ in tiles], *extra)
    return outs[0] if single else outs


def _rspec(tr, w, cb=0):
    return pl.BlockSpec((tr, w), lambda i: (i, cb))


def _fspec(shape):
    nd = len(shape)
    return pl.BlockSpec(shape, lambda i: (0,) * nd)


def _row_tile(lp):
    return ROW_ALIGN if lp % ROW_ALIGN == 0 else lp


def _rmsnorm_fwd(x, g, *, name):
    lp, d = x.shape
    tr = _row_tile(lp)

    def body(x_ref, g_ref, o_ref):
        xv = x_ref[...]
        r = lax.rsqrt(jnp.mean(xv * xv, axis=-1, keepdims=True) + EPS)
        o_ref[...] = (xv * r * g_ref[...]).astype(BF16)

    return pl.pallas_call(
        body, grid=(lp // tr,),
        in_specs=[_rspec(tr, d), _fspec((1, d))],
        out_specs=_rspec(tr, d),
        out_shape=jax.ShapeDtypeStruct((lp, d), BF16),
        compiler_params=_cp("parallel"), name=name)(x, g)


def _residual_norm(h, t, g, *, name):
    lp, d = h.shape
    tr = _row_tile(lp)

    def body(h_ref, t_ref, g_ref, h1_ref, m_ref):
        xv = h_ref[...] + t_ref[...]
        h1_ref[...] = xv
        r = lax.rsqrt(jnp.mean(xv * xv, axis=-1, keepdims=True) + EPS)
        m_ref[...] = (xv * r * g_ref[...]).astype(BF16)

    return pl.pallas_call(
        body, grid=(lp // tr,),
        in_specs=[_rspec(tr, d), _rspec(tr, d), _fspec((1, d))],
        out_specs=[_rspec(tr, d), _rspec(tr, d)],
        out_shape=[jax.ShapeDtypeStruct((lp, d), F32), jax.ShapeDtypeStruct((lp, d), BF16)],
        compiler_params=_cp("parallel"), name=name)(h, t, g)


def _rmsnorm_bwd_add(x, g, dy, dres, *, name):
    lp, d = x.shape
    tr = _row_tile(lp)

    def body(x_ref, g_ref, dy_ref, dr_ref, dx_ref, dg_ref):
        @pl.when(pl.program_id(0) == 0)
        def _():
            dg_ref[...] = jnp.zeros_like(dg_ref)

        xv = x_ref[...]
        r = lax.rsqrt(jnp.mean(xv * xv, axis=-1, keepdims=True) + EPS)
        xh = xv * r
        dyv = dy_ref[...]
        dg_ref[...] += jnp.sum(dyv * xh, axis=0, keepdims=True)
        dxh = dyv * g_ref[...]
        dx_ref[...] = dr_ref[...] + r * (dxh - xh * jnp.mean(dxh * xh, axis=-1, keepdims=True))

    return pl.pallas_call(
        body, grid=(lp // tr,),
        in_specs=[_rspec(tr, d), _fspec((1, d)), _rspec(tr, d), _rspec(tr, d)],
        out_specs=[_rspec(tr, d), _fspec((1, d))],
        out_shape=[jax.ShapeDtypeStruct((lp, d), F32), jax.ShapeDtypeStruct((1, d), F32)],
        compiler_params=_cp("arbitrary"), name=name)(x, g, dy, dres)


def _final_loss(h1, t2, g, tgt, *, n_tok, name):
    lp, d = h1.shape
    tr = _row_tile(lp)

    def body(h_ref, t_ref, g_ref, tg_ref, dh_ref, loss_ref, dg_ref):
        i = pl.program_id(0)

        @pl.when(i == 0)
        def _():
            loss_ref[...] = jnp.zeros_like(loss_ref)
            dg_ref[...] = jnp.zeros_like(dg_ref)

        xv = h_ref[...] + t_ref[...]
        r = lax.rsqrt(jnp.mean(xv * xv, axis=-1, keepdims=True) + EPS)
        xh = xv * r
        gv = g_ref[...]
        row = i * tr + lax.broadcasted_iota(jnp.int32, (tr, 1), 0)
        valid = (row >= N_META) & (row < N_META + n_tok)
        err = jnp.where(valid, xh * gv - tg_ref[...], 0.0)
        loss_ref[...] += jnp.sum(0.5 * err * err) / d
        dy = err / d
        dg_ref[...] += jnp.sum(dy * xh, axis=0, keepdims=True)
        dxh = dy * gv
        dh_ref[...] = r * (dxh - xh * jnp.mean(dxh * xh, axis=-1, keepdims=True))

    return pl.pallas_call(
        body, grid=(lp // tr,),
        in_specs=[_rspec(tr, d), _rspec(tr, d), _fspec((1, d)), _rspec(tr, d)],
        out_specs=[_rspec(tr, d), _fspec((1, LANES)), _fspec((1, d))],
        out_shape=[jax.ShapeDtypeStruct((lp, d), F32), jax.ShapeDtypeStruct((1, LANES), F32),
                   jax.ShapeDtypeStruct((1, d), F32)],
        compiler_params=_cp("arbitrary"), name=name)(h1, t2, g, tgt)


def _hg_out(o_f, o_b, proj, gain, *, col_g, name):
    lp, w = o_f.shape
    tr = _row_tile(lp)
    hh = w // HG_DK

    def body(of_ref, ob_ref, g_ref, gain_ref, y_ref):
        gv = g_ref[...]
        sg = gv * _sigmoid(gv)
        for h in range(hh):
            sl = slice(h * HG_DK, (h + 1) * HG_DK)
            o = of_ref[:, sl] + ob_ref[:, sl]
            r = lax.rsqrt(jnp.mean(o * o, axis=-1, keepdims=True) + EPS)
            y_ref[:, sl] = (o * r * gain_ref[:, sl] * sg[:, sl]).astype(BF16)

    return pl.pallas_call(
        body, grid=(lp // tr,),
        in_specs=[_rspec(tr, w), _rspec(tr, w), _rspec(tr, w, col_g // w), _fspec((1, w))],
        out_specs=_rspec(tr, w),
        out_shape=jax.ShapeDtypeStruct((lp, w), BF16),
        compiler_params=_cp("parallel"), name=name)(o_f, o_b, proj, gain)


def _hg_out_bwd(o_f, o_b, proj, gain, dy, *, col_g, name):
    lp, w = o_f.shape
    tr = _row_tile(lp)
    hh = w // HG_DK

    def body(of_ref, ob_ref, g_ref, gain_ref, dy_ref, do_ref, dg_ref, dgain_ref):
        @pl.when(pl.program_id(0) == 0)
        def _():
            dgain_ref[...] = jnp.zeros_like(dgain_ref)

        for h in range(hh):
            sl = slice(h * HG_DK, (h + 1) * HG_DK)
            gv = g_ref[:, sl]
            s = _sigmoid(gv)
            sg = gv * s
            dsg = s + gv * s * (1.0 - s)
            o = of_ref[:, sl] + ob_ref[:, sl]
            r = lax.rsqrt(jnp.mean(o * o, axis=-1, keepdims=True) + EPS)
            on = o * r
            dyv = dy_ref[:, sl]
            gn = gain_ref[:, sl]
            dgain_ref[:, sl] += jnp.sum(dyv * on * sg, axis=0, keepdims=True)
            dg_ref[:, sl] = (dyv * on * gn * dsg).astype(BF16)
            don = dyv * gn * sg
            do_ref[:, sl] = r * (don - on * jnp.mean(don * on, axis=-1, keepdims=True))

    return pl.pallas_call(
        body, grid=(lp // tr,),
        in_specs=[_rspec(tr, w), _rspec(tr, w), _rspec(tr, w, col_g // w), _fspec((1, w)),
                  _rspec(tr, w)],
        out_specs=[_rspec(tr, w), _rspec(tr, w), _fspec((1, w))],
        out_shape=[jax.ShapeDtypeStruct((lp, w), F32), jax.ShapeDtypeStruct((lp, w), BF16),
                   jax.ShapeDtypeStruct((1, w), F32)],
        compiler_params=_cp("arbitrary"), name=name)(o_f, o_b, proj, gain, dy)


HG_ROWS = 128
HG_HALVES = (1, 2, 4, 8, 16, 32, 64)


def _hg_gates(zq, z, lbv):
    qh = zq * _sigmoid(zq)
    s = _sigmoid(z)
    f = lbv + (1.0 - lbv) * s
    kk = (1.0 - lbv) * _sigmoid(-z)
    return qh, s, f, jnp.log(f), kk


def _block_cumsum(g, pos, suffix):
    x = g
    for k in HG_HALVES:
        if suffix:
            x = x + jnp.where(pos < HG_ROWS - k, pltpu.roll(x, HG_ROWS - k, 0), 0.0)
        else:
            x = x + jnp.where(pos >= k, pltpu.roll(x, k, 0), 0.0)
    return x


def _pair_levels(b, pos, reverse):
    out = []
    first = b
    for m in HG_HALVES:
        if m > 1:
            first = jnp.where((pos & (m - 1)) >= m // 2, pltpu.roll(first, m // 2, 0), first)
        nxt = pltpu.roll(first, HG_ROWS - m, 0)
        upper = (pos & (2 * m - 1)) >= m
        if reverse:
            eq = jnp.where(upper, 0.0, jnp.exp(jnp.minimum(b - nxt, 0.0)))
            ek = jnp.where(upper, jnp.exp(jnp.minimum(first - b, 0.0)), 0.0)
        else:
            eq = jnp.where(upper, jnp.exp(jnp.minimum(b - first, 0.0)), 0.0)
            ek = jnp.where(upper, 0.0, jnp.exp(jnp.minimum(nxt - b, 0.0)))
        out.append((m.bit_length(), eq, ek))
    return out


def _hg_scan_fwd(proj, lb, *, reverse, col_q, col_z, col_i, hh, name):
    lp = proj.shape[0]
    n_blocks = lp // HG_ROWS
    last = 0 if reverse else HG_ROWS - 1

    def body(q_ref, z_ref, i_ref, lb_ref, o_ref, st_ref):
        lbv = lb_ref[...]
        pos = lax.broadcasted_iota(jnp.int32, (HG_ROWS, 1), 0)
        ri = lax.broadcasted_iota(jnp.int32, (HG_ROWS, HG_ROWS), 0)
        ci = lax.broadcasted_iota(jnp.int32, (HG_ROWS, HG_ROWS), 1)

        def block(bi, st):
            bb = (n_blocks - 1 - bi) if reverse else bi
            r0 = pl.multiple_of(bb * HG_ROWS, HG_ROWS)
            v16 = i_ref[pl.ds(r0, HG_ROWS), :].astype(BF16)
            qh, _, _, g, kk = _hg_gates(q_ref[pl.ds(r0, HG_ROWS), :], z_ref[pl.ds(r0, HG_ROWS), :],
                                        lbv)
            b = _block_cumsum(g, pos, reverse)
            bl = b[last:last + 1, :]
            qe = (qh * jnp.exp(b)).astype(BF16)
            kd = (kk * jnp.exp(bl - b)).astype(BF16)
            a = jnp.where(ri == ci, jnp.sum(qh * kk, axis=1, keepdims=True), 0.0)
            for sh, eq, ek in _pair_levels(b, pos, reverse):
                a = a + jnp.where((ri >> sh) == (ci >> sh),
                                  _nt((qh * eq).astype(BF16), (kk * ek).astype(BF16)), 0.0)
            st_ref[bb] = st
            o_ref[pl.ds(r0, HG_ROWS), :] = _nn(a.astype(BF16), v16) + _nt(qe, st.astype(BF16))
            return jnp.exp(bl) * st + _tn(v16, kd)

        lax.fori_loop(0, n_blocks, block, jnp.zeros((HG_DK, HG_DK), F32))

    cspec = lambda col: pl.BlockSpec((lp, HG_DK), lambda h: (0, col // HG_DK + h))
    return pl.pallas_call(
        body, grid=(hh,),
        in_specs=[cspec(col_q), cspec(col_z), cspec(col_i),
                  pl.BlockSpec((None, 1, HG_DK), lambda h: (h, 0, 0))],
        out_specs=[pl.BlockSpec((lp, HG_DK), lambda h: (0, h)),
                   pl.BlockSpec((None, n_blocks, HG_DK, HG_DK), lambda h: (h, 0, 0, 0))],
        out_shape=[jax.ShapeDtypeStruct((lp, hh * HG_DK), F32),
                   jax.ShapeDtypeStruct((hh, n_blocks, HG_DK, HG_DK), F32)],
        compiler_params=_cp("parallel"), name=name)(proj, proj, proj, lb)


def _hg_scan_bwd(proj, lb, states, do, *, reverse, col_q, col_z, col_i, hh, name):
    lp = proj.shape[0]
    n_blocks = lp // HG_ROWS
    last = 0 if reverse else HG_ROWS - 1

    def body(q_ref, z_ref, i_ref, lb_ref, st_ref, do_ref, dq_ref, dz_ref, dv_ref, dlb_ref):
        lbv = lb_ref[...]
        pos = lax.broadcasted_iota(jnp.int32, (HG_ROWS, 1), 0)
        ri = lax.broadcasted_iota(jnp.int32, (HG_ROWS, HG_ROWS), 0)
        ci = lax.broadcasted_iota(jnp.int32, (HG_ROWS, HG_ROWS), 1)

        def block(bi, carry):
            dst, dlb = carry
            bb = bi if reverse else (n_blocks - 1 - bi)
            r0 = pl.multiple_of(bb * HG_ROWS, HG_ROWS)
            zq = q_ref[pl.ds(r0, HG_ROWS), :]
            v16 = i_ref[pl.ds(r0, HG_ROWS), :].astype(BF16)
            do16 = do_ref[pl.ds(r0, HG_ROWS), :].astype(BF16)
            qh, s, f, g, kk = _hg_gates(zq, z_ref[pl.ds(r0, HG_ROWS), :], lbv)
            b = _block_cumsum(g, pos, reverse)
            bl = b[last:last + 1, :]
            eb = jnp.exp(b)
            ebl = jnp.exp(bl - b)
            decay = jnp.exp(bl)
            qe16 = (qh * eb).astype(BF16)
            kd16 = (kk * ebl).astype(BF16)
            st = st_ref[bb]
            st16, dst16 = st.astype(BF16), dst.astype(BF16)
            same_row = ri == ci
            da = _nt(do16, v16)
            da_diag = jnp.sum(jnp.where(same_row, da, 0.0), axis=1, keepdims=True)
            dq_state = eb * _nn(do16, st16)
            dk_state = ebl * _nn(v16, dst16)
            dq = dq_state + da_diag * kk
            dk = dk_state + da_diag * qh
            dbl = (decay * jnp.sum(st * dst, axis=0, keepdims=True)
                   + jnp.sum(kk * dk_state, axis=0, keepdims=True))
            db = qh * dq_state - kk * dk_state + jnp.where(pos == last, dbl, 0.0)
            a = jnp.where(same_row, jnp.sum(qh * kk, axis=1, keepdims=True), 0.0)
            for sh, eq, ek in _pair_levels(b, pos, reverse):
                same = (ri >> sh) == (ci >> sh)
                q16, k16 = (qh * eq).astype(BF16), (kk * ek).astype(BF16)
                a = a + jnp.where(same, _nt(q16, k16), 0.0)
                da16 = jnp.where(same, da, 0.0).astype(BF16)
                gq, gk = _nn(da16, k16), _tn(da16, q16)
                dq = dq + eq * gq
                dk = dk + ek * gk
                db = db + (q16.astype(F32) * gq - k16.astype(F32) * gk)
            dg = _block_cumsum(db, pos, not reverse)
            df = dg / f - dk
            sq = _sigmoid(zq)
            dq_ref[pl.ds(r0, HG_ROWS), :] = dq * (sq + zq * sq * (1.0 - sq))
            dz_ref[pl.ds(r0, HG_ROWS), :] = df * (1.0 - lbv) * s * (1.0 - s)
            dv_ref[pl.ds(r0, HG_ROWS), :] = _nt(kd16, dst16) + _tn(a.astype(BF16), do16)
            return (decay * dst + _tn(do16, qe16),
                    dlb + jnp.sum(df * (1.0 - s), axis=0, keepdims=True))

        _, dlb = lax.fori_loop(0, n_blocks, block,
                               (jnp.zeros((HG_DK, HG_DK), F32), jnp.zeros((1, HG_DK), F32)))
        dlb_ref[...] = dlb

    cspec = lambda col: pl.BlockSpec((lp, HG_DK), lambda h: (0, col // HG_DK + h))
    ospec = pl.BlockSpec((lp, HG_DK), lambda h: (0, h))
    sds = jax.ShapeDtypeStruct((lp, hh * HG_DK), F32)
    return pl.pallas_call(
        body, grid=(hh,),
        in_specs=[cspec(col_q), cspec(col_z), cspec(col_i),
                  pl.BlockSpec((None, 1, HG_DK), lambda h: (h, 0, 0)),
                  pl.BlockSpec((None, n_blocks, HG_DK, HG_DK), lambda h: (h, 0, 0, 0)),
                  ospec],
        out_specs=[ospec, ospec, ospec, pl.BlockSpec((None, 1, HG_DK), lambda h: (h, 0, 0))],
        out_shape=[sds, sds, sds, jax.ShapeDtypeStruct((hh, 1, HG_DK), F32)],
        compiler_params=_cp("parallel"), name=name)(proj, proj, proj, lb, states, do)


def _na_rows(r, rows):
    rs = jnp.clip(r - NA_WIN_H // 2, 0, rows - NA_WIN_H)
    i0 = rs - r + (NA_WIN_H - 1)
    q0 = pl.multiple_of(N_META + GRID_W * r, 16)
    k0 = pl.multiple_of(N_META + GRID_W * rs, 16)
    return i0, q0, k0


def _na_scores(q16, k16, km16, tb_ref, i0, scale):
    s = _nt(q16, k16) * scale
    bias = jnp.concatenate([tb_ref[i0 + j] for j in range(NA_WIN_H)], axis=1)
    return s + bias, _nt(q16, km16) * scale


NA_HB = LANES // NA_HEAD_DIM


def _na_head_lanes():
    lane = lax.broadcasted_iota(jnp.int32, (1, LANES), 1)
    return [lane // NA_HEAD_DIM == h for h in range(NA_HB)]


def _na_only(mask, x):
    return jnp.where(mask, x, jnp.zeros_like(x))


def _na_fwd(proj, tb, *, n_tok, nh, name):
    lp = proj.shape[0]
    dh, hb = NA_HEAD_DIM, NA_HB
    naw = nh * dh
    rows = n_tok // GRID_W
    scale = dh ** -0.5
    kw = NA_WIN_H * GRID_W

    def body(q_ref, k_ref, v_ref, tb_ref, o_ref, lse_ref, q16_ref, k16_ref, v16_ref):
        o_ref[...] = jnp.zeros_like(o_ref)
        lse_ref[...] = jnp.zeros_like(lse_ref)
        q16_ref[...] = q_ref[...].astype(BF16)
        k16_ref[...] = k_ref[...].astype(BF16)
        v16_ref[...] = v_ref[...].astype(BF16)
        heads = _na_head_lanes()
        km = k16_ref[0:N_META, :]
        vm = v16_ref[0:N_META, :]
        qm = q16_ref[0:N_META, :]
        o_m = None
        for h in range(hb):
            s = _nt(_na_only(heads[h], qm), km) * scale
            m = jnp.max(s, axis=1, keepdims=True)
            p = jnp.exp(s - m)
            l = jnp.sum(p, axis=1, keepdims=True)
            o_h = _nn(p.astype(BF16), vm) / l
            o_m = o_h if o_m is None else jnp.where(heads[h], o_h, o_m)
            lse_ref[h, 0:N_META, :] = m + jnp.log(l)
        o_ref[0:N_META, :] = o_m

        def step(r, carry):
            i0, q0, k0 = _na_rows(r, rows)
            q16 = q16_ref[pl.ds(q0, GRID_W), :]
            k16 = k16_ref[pl.ds(k0, kw), :]
            v16 = v16_ref[pl.ds(k0, kw), :]
            o = None
            for h in range(hb):
                s, sm = _na_scores(_na_only(heads[h], q16), k16, km, tb_ref.at[h], i0, scale)
                m = jnp.maximum(jnp.max(s, axis=1, keepdims=True),
                                jnp.max(sm, axis=1, keepdims=True))
                p = jnp.exp(s - m)
                pm = jnp.exp(sm - m)
                l = jnp.sum(p, axis=1, keepdims=True) + jnp.sum(pm, axis=1, keepdims=True)
                o_h = (_nn(p.astype(BF16), v16) + _nn(pm.astype(BF16), vm)) / l
                o = o_h if o is None else jnp.where(heads[h], o_h, o)
                lse_ref[h, pl.ds(q0, GRID_W), :] = m + jnp.log(l)
            o_ref[pl.ds(q0, GRID_W), :] = o
            return carry

        lax.fori_loop(0, rows, step, 0, unroll=2)

    cblk = lambda col: pl.BlockSpec((lp, LANES), lambda g: (0, col // LANES + g))
    return pl.pallas_call(
        body, grid=(nh // hb,),
        in_specs=[cblk(0), cblk(naw), cblk(2 * naw),
                  pl.BlockSpec((hb, 2 * NA_WIN_H - 1, GRID_W, GRID_W), lambda g: (g, 0, 0, 0))],
        out_specs=[cblk(0), pl.BlockSpec((hb, lp, 1), lambda g: (g, 0, 0))],
        out_shape=[jax.ShapeDtypeStruct((lp, naw), F32), jax.ShapeDtypeStruct((nh, lp, 1), F32)],
        scratch_shapes=[pltpu.VMEM((lp, LANES), BF16)] * 3,
        compiler_params=_cp("parallel"), name=name)(proj, proj, proj, tb)


def _na_bwd(proj, tb, o, lse, do, *, n_tok, nh, name):
    lp = proj.shape[0]
    dh, hb = NA_HEAD_DIM, NA_HB
    naw = nh * dh
    rows = n_tok // GRID_W
    scale = dh ** -0.5
    kw = NA_WIN_H * GRID_W

    def body(q_ref, k_ref, v_ref, tb_ref, o_ref, lse_ref, do_ref, dq_ref, dk_ref, dv_ref, dtb_ref,
             q16_ref, k16_ref, v16_ref):
        dq_ref[...] = jnp.zeros_like(dq_ref)
        dk_ref[...] = jnp.zeros_like(dk_ref)
        dv_ref[...] = jnp.zeros_like(dv_ref)
        dtb_ref[...] = jnp.zeros_like(dtb_ref)
        q16_ref[...] = q_ref[...].astype(BF16)
        k16_ref[...] = k_ref[...].astype(BF16)
        v16_ref[...] = v_ref[...].astype(BF16)
        heads = _na_head_lanes()
        km = k16_ref[0:N_META, :]
        vm = v16_ref[0:N_META, :]
        qm = q16_ref[0:N_META, :]
        dom = do_ref[0:N_META, :]
        prod = dom * o_ref[0:N_META, :]
        dq_m = None
        dkm0 = jnp.zeros((N_META, LANES), F32)
        dvm0 = jnp.zeros((N_META, LANES), F32)
        for h in range(hb):
            q_h = _na_only(heads[h], qm)
            do_h = _na_only(heads[h], dom).astype(BF16)
            p = jnp.exp(_nt(q_h, km) * scale - lse_ref[h, 0:N_META, :])
            delta = jnp.sum(_na_only(heads[h], prod), axis=1, keepdims=True)
            ds = (p * (_nt(do_h, vm) - delta)).astype(BF16)
            dq_h = _nn(ds, km) * scale
            dq_m = dq_h if dq_m is None else jnp.where(heads[h], dq_h, dq_m)
            dkm0 = dkm0 + _tn(ds, q_h) * scale
            dvm0 = dvm0 + _tn(p.astype(BF16), do_h)
        dq_ref[0:N_META, :] = dq_m

        def step(r, carry):
            dkm, dvm = carry
            i0, q0, k0 = _na_rows(r, rows)
            q16 = q16_ref[pl.ds(q0, GRID_W), :]
            k16 = k16_ref[pl.ds(k0, kw), :]
            v16 = v16_ref[pl.ds(k0, kw), :]
            dov = do_ref[pl.ds(q0, GRID_W), :]
            prod = dov * o_ref[pl.ds(q0, GRID_W), :]
            dq = None
            dk = jnp.zeros((kw, LANES), F32)
            dv = jnp.zeros((kw, LANES), F32)
            for h in range(hb):
                q_h = _na_only(heads[h], q16)
                do_h = _na_only(heads[h], dov).astype(BF16)
                s, sm = _na_scores(q_h, k16, km, tb_ref.at[h], i0, scale)
                lse = lse_ref[h, pl.ds(q0, GRID_W), :]
                p = jnp.exp(s - lse)
                pm = jnp.exp(sm - lse)
                delta = jnp.sum(_na_only(heads[h], prod), axis=1, keepdims=True)
                ds = p * (_nt(do_h, v16) - delta)
                dsm = (pm * (_nt(do_h, vm) - delta)).astype(BF16)
                ds16 = ds.astype(BF16)
                dq_h = (_nn(ds16, k16) + _nn(dsm, km)) * scale
                dq = dq_h if dq is None else jnp.where(heads[h], dq_h, dq)
                dk = dk + _tn(ds16, q_h) * scale
                dv = dv + _tn(p.astype(BF16), do_h)
                dkm = dkm + _tn(dsm, q_h) * scale
                dvm = dvm + _tn(pm.astype(BF16), do_h)
                for j in range(NA_WIN_H):
                    dtb_ref[h, i0 + j] += ds[:, j * GRID_W:(j + 1) * GRID_W]
            dq_ref[pl.ds(q0, GRID_W), :] = dq
            dk_ref[pl.ds(k0, kw), :] += dk
            dv_ref[pl.ds(k0, kw), :] += dv
            return dkm, dvm

        dkm, dvm = lax.fori_loop(0, rows, step, (dkm0, dvm0))
        dk_ref[0:N_META, :] += dkm
        dv_ref[0:N_META, :] += dvm

    cblk = lambda col: pl.BlockSpec((lp, LANES), lambda g: (0, col // LANES + g))
    tbs = pl.BlockSpec((hb, 2 * NA_WIN_H - 1, GRID_W, GRID_W), lambda g: (g, 0, 0, 0))
    sds = jax.ShapeDtypeStruct((lp, naw), F32)
    return pl.pallas_call(
        body, grid=(nh // hb,),
        in_specs=[cblk(0), cblk(naw), cblk(2 * naw), tbs, cblk(0),
                  pl.BlockSpec((hb, lp, 1), lambda g: (g, 0, 0)), cblk(0)],
        out_specs=[cblk(0), cblk(0), cblk(0), tbs],
        out_shape=[sds, sds, sds, jax.ShapeDtypeStruct(tb.shape, F32)],
        scratch_shapes=[pltpu.VMEM((lp, LANES), BF16)] * 3,
        compiler_params=_cp("parallel"), name=name)(proj, proj, proj, tb, o, lse, do)


def _rpb_onehot():
    c = np.arange(GRID_W)[:, None]
    w = np.arange(GRID_W)[None, :]
    cs = np.clip(c - NA_WIN_W // 2, 0, GRID_W - NA_WIN_W)
    in_win = (w >= cs) & (w < cs + NA_WIN_W)
    dc = np.clip(w - c, -(NA_WIN_W - 1), NA_WIN_W - 1) + NA_WIN_W - 1
    oh = np.zeros((LANES, GRID_W * GRID_W), np.float32)
    flat = np.arange(GRID_W * GRID_W).reshape(GRID_W, GRID_W)
    oh[dc[in_win], flat[in_win]] = 1.0
    neg = np.where(in_win, 0.0, -1e30).astype(np.float32).reshape(1, -1)
    return oh, neg


def _assemble_dproj(dq_na, dk_na, dv_na, dq_f, dq_b, dz_f, dz_b, dv_f, dv_b, dg, dgn, dgh, *, name):
    lp, naw = dq_na.shape
    hgw = dq_f.shape[1]
    d = dgn.shape[1]
    cols = 3 * naw + 5 * hgw + 2 * d
    tr = _row_tile(lp)

    def body(nq_ref, nk_ref, nv_ref, qf_ref, qb_ref, zf_ref, zb_ref, vf_ref, vb_ref, g_ref, gn_ref,
             gh_ref, o_ref):
        o_ref[:, 0:naw] = nq_ref[...].astype(BF16)
        o_ref[:, naw:2 * naw] = nk_ref[...].astype(BF16)
        o_ref[:, 2 * naw:3 * naw] = nv_ref[...].astype(BF16)
        c0 = 3 * naw
        o_ref[:, c0:c0 + hgw] = (qf_ref[...] + qb_ref[...]).astype(BF16)
        o_ref[:, c0 + hgw:c0 + 2 * hgw] = zf_ref[...].astype(BF16)
        o_ref[:, c0 + 2 * hgw:c0 + 3 * hgw] = zb_ref[...].astype(BF16)
        o_ref[:, c0 + 3 * hgw:c0 + 4 * hgw] = (vf_ref[...] + vb_ref[...]).astype(BF16)
        o_ref[:, c0 + 4 * hgw:c0 + 5 * hgw] = g_ref[...]
        o_ref[:, c0 + 5 * hgw:c0 + 5 * hgw + d] = gn_ref[...]
        o_ref[:, c0 + 5 * hgw + d:] = gh_ref[...]

    hg, na = _rspec(tr, hgw), _rspec(tr, naw)
    return pl.pallas_call(
        body, grid=(lp // tr,),
        in_specs=[na, na, na, hg, hg, hg, hg, hg, hg, hg, _rspec(tr, d), _rspec(tr, d)],
        out_specs=_rspec(tr, cols),
        out_shape=jax.ShapeDtypeStruct((lp, cols), BF16),
        compiler_params=_cp("parallel"), name=name)(dq_na, dk_na, dv_na, dq_f, dq_b, dz_f, dz_b,
                                                    dv_f, dv_b, dg, dgn, dgh)


def _adamw(w, g, m, v, *, name):
    rows, cols = w.shape
    tr = 256 if rows % 256 == 0 else rows

    def body(w_ref, g_ref, m_ref, v_ref, d_ref, mo_ref, vo_ref):
        gv = g_ref[...]
        mn = ADAM_B1 * m_ref[...] + (1.0 - ADAM_B1) * gv
        vn = ADAM_B2 * v_ref[...] + (1.0 - ADAM_B2) * (gv * gv)
        m_hat = mn / (1.0 - ADAM_B1 ** ADAM_STEP)
        v_hat = vn / (1.0 - ADAM_B2 ** ADAM_STEP)
        d_ref[...] = -ADAM_LR * (m_hat / (jnp.sqrt(v_hat) + ADAM_EPS) + ADAM_WD * w_ref[...])
        mo_ref[...] = mn
        vo_ref[...] = vn

    spec = _rspec(tr, cols)
    sds = jax.ShapeDtypeStruct((rows, cols), F32)
    return pl.pallas_call(
        body, grid=(rows // tr,), in_specs=[spec] * 4, out_specs=[spec] * 3, out_shape=[sds] * 3,
        compiler_params=_cp("parallel"), name=name)(w, g, m, v)


def _local_step(x, tgt, meta, first_weight, rest_weights, g_mix, g_mlp, g_fin, hg_gain, rpb, lb,
                early_grads=None, late_grad=None):
    n_tok, d = x.shape
    hgw = hg_gain.shape[1]
    nh, hh = rpb.shape[0], hgw // HG_DK
    naw = nh * NA_HEAD_DIM
    l_real = N_META + n_tok
    lp = -(-l_real // ROW_ALIGN) * ROW_ALIGN
    n_chunks = l_real // HG_CHUNK
    pad = lp - l_real
    col_qhg = 3 * naw
    col_zf, col_zb, col_i, col_g = (col_qhg + hgw, col_qhg + 2 * hgw, col_qhg + 3 * hgw,
                                    col_qhg + 4 * hgw)
    col_gate = col_qhg + 5 * hgw

    zpad = jnp.zeros((pad, d), F32)
    h0 = jnp.concatenate([meta, x, zpad], axis=0)
    tgt_p = jnp.concatenate([jnp.zeros((N_META, d), F32), tgt, zpad], axis=0)

    oh_np, neg_np = _rpb_onehot()
    oh = jnp.asarray(oh_np)
    rpb_p = jnp.pad(rpb.reshape(nh * (2 * NA_WIN_H - 1), 2 * NA_WIN_W - 1),
                    ((0, 0), (0, LANES - (2 * NA_WIN_W - 1))))
    tb = _matmul(rpb_p, oh, tm=rpb_p.shape[0], tn=512, tk=LANES, precision=HIGHEST,
                 name="rpb_expand")
    tb = (tb + jnp.asarray(neg_np)).reshape(nh, 2 * NA_WIN_H - 1, GRID_W, GRID_W)

    a = _rmsnorm_fwd(h0, g_mix, name="norm_mix")
    w_in = first_weight(a)
    proj = _matmul(a, w_in, name="mm_in")
    o_na, lse = _na_fwd(proj, tb, n_tok=n_tok, nh=nh, name="na_fwd")
    lb_f = lb[0].reshape(hh, 1, HG_DK)
    lb_b = lb[1].reshape(hh, 1, HG_DK)
    scan_kw = dict(col_q=col_qhg, col_i=col_i, hh=hh)
    o_f, st_f = _hg_scan_fwd(proj, lb_f, reverse=False, col_z=col_zf, name="hg_scan_f", **scan_kw)
    o_b, st_b = _hg_scan_fwd(proj, lb_b, reverse=True, col_z=col_zb, name="hg_scan_b", **scan_kw)
    o_hg = _hg_out(o_f, o_b, proj, hg_gain, col_g=col_g, name="hg_out")
    w_na, w_hg, w_o, w_up, w_down = rest_weights(o_hg)
    y_na = _matmul(o_na, w_na, name="mm_na_out")
    gates = ((proj, col_gate), (proj, col_gate + d))

    def mix_gates(acc, gn, gh, yn):
        return acc, _sigmoid(gn) * yn + _sigmoid(gh) * acc

    def mix_gates_bwd(dmix, gn, gh, yn, yh):
        sn, sh = _sigmoid(gn), _sigmoid(gh)
        return dmix * sn, dmix * sh, dmix * yn * sn * (1.0 - sn), dmix * yh * sh * (1.0 - sh)

    y_hg, mix = _matmul(o_hg, w_hg, name="mm_hg_out", epilogue=mix_gates,
                        tiles=(*gates, (y_na, 0)), out_dtypes=(F32, BF16))
    t1 = _matmul(mix, w_o, name="mm_o")
    h1, mlp_in = _residual_norm(h0, t1, g_mlp, name="resid_norm_mlp")
    u, act = _matmul(mlp_in, w_up, name="mm_up", out_dtypes=(F32, BF16),
                     epilogue=lambda acc: (acc, jnp.square(jnp.maximum(acc, 0.0))))
    t2 = _matmul(act, w_down, name="mm_down")
    dh2, loss, dg_fin = _final_loss(h1, t2, g_fin, tgt_p, n_tok=n_tok, name="final_loss")

    (du,) = _matmul(dh2, w_down, tb=True, name="mm_down_dx", tiles=((u, 0),), out_dtypes=(BF16,),
                    epilogue=lambda acc, uv: (acc * 2.0 * jnp.maximum(uv, 0.0),))
    dw_down = _matmul(act, dh2, ta=True, name="mm_down_dw")
    dm = _matmul(du, w_up, tb=True, name="mm_up_dx")
    dw_up = _matmul(mlp_in, du, ta=True, name="mm_up_dw")
    dh1, dg_mlp = _rmsnorm_bwd_add(h1, g_mlp, dm, dh2, name="norm_mlp_bwd")
    dy_na, dy_hg, dgn, dgh = _matmul(dh1, w_o, tb=True, name="mm_o_dx", epilogue=mix_gates_bwd,
                                     tiles=(*gates, (y_na, 0), (y_hg, 0)), out_dtypes=(BF16,) * 4)
    dw_o = _matmul(mix, dh1, ta=True, name="mm_o_dw")
    do_na = _matmul(dy_na, w_na, tb=True, name="mm_na_out_dx")
    dw_na = _matmul(o_na, dy_na, ta=True, name="mm_na_out_dw")
    do_hg = _matmul(dy_hg, w_hg, tb=True, name="mm_hg_out_dx")
    dw_hg = _matmul(o_hg, dy_hg, ta=True, name="mm_hg_out_dw")
    token = early_grads([dw_na, dw_hg, dw_o, dw_up, dw_down]) if early_grads else None
    if token is not None:
        hg_gain = hg_gain + token[0:1, 0:1]
    d_o, dg_hg, d_gain = _hg_out_bwd(o_f, o_b, proj, hg_gain, do_hg, col_g=col_g, name="hg_out_bwd")
    dq_f, dz_f, dv_f, dlb_f = _hg_scan_bwd(proj, lb_f, st_f, d_o, reverse=False, col_z=col_zf,
                                           name="hg_scan_f_bwd", **scan_kw)
    dq_b, dz_b, dv_b, dlb_b = _hg_scan_bwd(proj, lb_b, st_b, d_o, reverse=True, col_z=col_zb,
                                           name="hg_scan_b_bwd", **scan_kw)
    dq_na, dk_na, dv_na, dtb = _na_bwd(proj, tb, o_na, lse, do_na, n_tok=n_tok, nh=nh, name="na_bwd")
    dproj = _assemble_dproj(dq_na, dk_na, dv_na, dq_f, dq_b, dz_f, dz_b, dv_f, dv_b, dg_hg, dgn,
                            dgh, name="assemble_dproj")
    dw_in = _matmul(a, dproj, ta=True, name="mm_in_dw")
    token = late_grad(dw_in) if late_grad else None
    da = _matmul(dproj, w_in, tb=True, name="mm_in_dx", after=token)
    dh0, dg_mix = _rmsnorm_bwd_add(h0, g_mix, da, dh1, name="norm_mix_bwd")
    d_rpb = _matmul(dtb.reshape(nh * (2 * NA_WIN_H - 1), GRID_W * GRID_W), oh, tb=True,
                    tm=nh * (2 * NA_WIN_H - 1), tn=LANES, tk=1024, precision=HIGHEST,
                    name="rpb_reduce")
    d_lb = jnp.concatenate([dlb_f.reshape(1, hgw), dlb_b.reshape(1, hgw)], axis=0)
    return (loss, dh0[N_META:l_real], dh0[:N_META], dw_in, dw_na, dw_hg, dw_o, dw_up, dw_down,
            dg_mix, dg_mlp, dg_fin, d_gain, d_rpb, d_lb)


N_CHIPS = 4
N_DEV = 8
ANY = pl.BlockSpec(memory_space=pl.ANY)


def _place():
    x, y, c = lax.axis_index("x"), lax.axis_index("y"), lax.axis_index("c")
    others = []
    for j in (1, 2, 3):
        tx = (1 - x) if (j >> 1) else x
        ty = (1 - y) if (j & 1) else y
        others.append((tx, ty))
    return x, y, c, others


def _piece(ref, axis, k, half, rh, cs):
    if axis == 1:
        return ref.at[pl.ds(pl.multiple_of(half * rh, 16), rh), pl.ds(pl.multiple_of(k * cs, LANES), cs)]
    return ref.at[pl.ds(pl.multiple_of(k * 2 * rh + half * rh, 16), rh), :]


def _cast_into_full(shard, axis, place, *, name):
    r, cs = shard.shape
    full = (r, cs * N_CHIPS) if axis == 1 else (r * N_CHIPS, cs)
    tr = next(t for t in (256, 128, 64, 32, 16) if r % t == 0)
    nt = r // tr

    def body(p_ref, s_ref, o_ref):
        o_ref[...] = s_ref[...].astype(BF16)

    if axis == 1:
        omap = lambda i, p_ref: (i, p_ref[0])
    else:
        omap = lambda i, p_ref: (p_ref[0] * nt + i, 0)
    return pl.pallas_call(
        body,
        grid_spec=pltpu.PrefetchScalarGridSpec(
            num_scalar_prefetch=1, grid=(nt,),
            in_specs=[pl.BlockSpec((tr, cs), lambda i, p_ref: (i, 0))],
            out_specs=pl.BlockSpec((tr, cs), omap)),
        out_shape=jax.ShapeDtypeStruct(full, BF16),
        compiler_params=_cp("parallel"), name=name)(place, shard)


HBM_SPEC = pl.BlockSpec(memory_space=pltpu.HBM)
SEM_SPEC = pl.BlockSpec(memory_space=pltpu.SEMAPHORE)
SPLIT_COPY = pltpu.CompilerParams(has_side_effects=pltpu.SideEffectType.DATAFLOW_SIDE_EFFECTING)
TOKEN = jax.ShapeDtypeStruct((8, LANES), F32)


def _geo(fulls, axes):
    out = []
    for f, ax in zip(fulls, axes):
        r, cs = (f.shape[0], f.shape[1] // N_CHIPS) if ax == 1 else (f.shape[0] // N_CHIPS, f.shape[1])
        out.append((ax, r // 2, cs))
    return out


def _gather_copies(refs, geo, send_sems, recv_sems):
    x, y, c, others = _place()
    chip = 2 * x + y
    cps = []
    for i, (ax, rh, cs) in enumerate(geo):
        mine = _piece(refs[i], ax, chip, c, rh, cs)
        for j, (tx, ty) in enumerate(others):
            cps.append(pltpu.make_async_remote_copy(
                src_ref=mine, dst_ref=mine, send_sem=send_sems.at[3 * i + j],
                recv_sem=recv_sems.at[3 * i + j], device_id=(tx, ty, c), device_id_type=MESH))
    return cps


def _allgather_start(fulls, axes, after, *, name):
    n = len(fulls)
    geo = _geo(fulls, axes)

    def body(*refs):
        w_refs = refs[:n]
        send_sems, recv_sems = refs[n + 1], refs[n + 2]
        token = refs[2 * n + 3]
        for cp in _gather_copies(w_refs, geo, send_sems, recv_sems):
            cp.start()
        token[...] = jnp.zeros_like(token)

    out = pl.pallas_call(
        body, name=name,
        out_shape=(pltpu.SemaphoreType.DMA((3 * n,)), pltpu.SemaphoreType.DMA((3 * n,)),
                   *[pltpu.HBM(f.shape, f.dtype) for f in fulls], TOKEN),
        in_specs=[HBM_SPEC] * n + [ANY],
        out_specs=(SEM_SPEC, SEM_SPEC, *[HBM_SPEC] * n, pl.BlockSpec(memory_space=pltpu.VMEM)),
        input_output_aliases={i: 2 + i for i in range(n)},
        compiler_params=SPLIT_COPY,
    )(*[pltpu.with_memory_space_constraint(f, pltpu.HBM) for f in fulls], after)
    return out[0], out[1], list(out[2:2 + n]), out[2 + n]


def _allgather_wait(send_sems, recv_sems, fulls, axes, after, *, name):
    n = len(fulls)
    geo = _geo(fulls, axes)

    def body(*refs):
        w_refs = refs[:n]
        for cp in _gather_copies(w_refs, geo, refs[n], refs[n + 1]):
            cp.wait_send()
            cp.wait_recv()

    return list(pl.pallas_call(
        body, name=name,
        out_shape=[pltpu.HBM(f.shape, f.dtype) for f in fulls],
        in_specs=[HBM_SPEC] * n + [SEM_SPEC, SEM_SPEC, ANY],
        out_specs=[HBM_SPEC] * n,
        input_output_aliases={i: i for i in range(n)},
        compiler_params=SPLIT_COPY,
    )(*fulls, send_sems, recv_sems, after))


def _allgather_forward(fulls, axes, *, name):
    n = len(fulls)
    geo = _geo(fulls, axes)

    def body(*refs):
        o_refs = refs[n:2 * n]
        send_sems, recv_sems = refs[2 * n:]
        x, y, c, others = _place()

        def rcopy(i, j, half, to):
            ax, rh, cs = geo[i]
            ref = _piece(o_refs[i], ax, 2 * others[j][0] + others[j][1], half, rh, cs)
            return pltpu.make_async_remote_copy(
                src_ref=ref, dst_ref=ref, send_sem=send_sems.at[3 * i + j],
                recv_sem=recv_sems.at[3 * i + j], device_id=to, device_id_type=MESH)

        cps = [rcopy(i, j, c, (x, y, 1 - c)) for i in range(n) for j in range(3)]
        for cp in cps:
            cp.start()
        for i in range(n):
            for j in range(3):
                rcopy(i, j, 1 - c, (x, y, c)).wait_recv()
        for cp in cps:
            cp.wait_send()

    return list(pl.pallas_call(
        body, in_specs=[ANY] * n, out_specs=[ANY] * n,
        out_shape=[jax.ShapeDtypeStruct(f.shape, f.dtype) for f in fulls],
        input_output_aliases={i: i for i in range(n)},
        scratch_shapes=[pltpu.SemaphoreType.DMA((3 * n,)), pltpu.SemaphoreType.DMA((3 * n,))],
        name=name)(*fulls))


def _scatter_geo(parts, axes):
    out = []
    for p, ax in zip(parts, axes):
        _, rh, cols = p.shape
        out.append((ax, rh, cols // N_CHIPS if ax == 1 else cols))
    return out


def _scatter_copies(p_refs, q_refs, geo, send_sems, recv_sems):
    x, y, c, others = _place()
    chip = 2 * x + y
    cps = []
    for i, (ax, rh, cw) in enumerate(geo):
        for j, (tx, ty) in enumerate(others):
            k = 2 * tx + ty
            src = (p_refs[i].at[0, :, pl.ds(pl.multiple_of(k * cw, LANES), cw)] if ax == 1
                   else p_refs[i].at[k])
            cps.append(pltpu.make_async_remote_copy(
                src_ref=src, dst_ref=q_refs[i].at[chip], send_sem=send_sems.at[3 * i + j],
                recv_sem=recv_sems.at[3 * i + j], device_id=(tx, ty, c), device_id_type=MESH))
    return cps


def _scatter_start(parts, axes, *, name):
    n = len(parts)
    geo = _scatter_geo(parts, axes)
    slots = [pltpu.HBM((N_CHIPS, rh, cw), p.dtype) for p, (_, rh, cw) in zip(parts, geo)]

    def body(*refs):
        p_refs, q_refs = refs[:n], refs[n:2 * n]
        send_sems, recv_sems = refs[2 * n], refs[2 * n + 1]
        token = refs[4 * n + 2]
        for cp in _scatter_copies(p_refs, q_refs, geo, send_sems, recv_sems):
            cp.start()
        token[...] = jnp.zeros_like(token)

    land = [pltpu.with_memory_space_constraint(lax.empty(s.inner_aval.shape, s.inner_aval.dtype), pltpu.HBM)
            for s in slots]
    out = pl.pallas_call(
        body, name=name,
        out_shape=(pltpu.SemaphoreType.DMA((3 * n,)), pltpu.SemaphoreType.DMA((3 * n,)),
                   *[pltpu.HBM(p.shape, p.dtype) for p in parts], *slots, TOKEN),
        in_specs=[HBM_SPEC] * (2 * n),
        out_specs=(SEM_SPEC, SEM_SPEC, *[HBM_SPEC] * (2 * n), pl.BlockSpec(memory_space=pltpu.VMEM)),
        input_output_aliases={i: 2 + i for i in range(2 * n)},
        compiler_params=SPLIT_COPY,
    )(*[pltpu.with_memory_space_constraint(p, pltpu.HBM) for p in parts], *land)
    return out[0], out[1], list(out[2:2 + n]), list(out[2 + n:2 + 2 * n]), out[2 + 2 * n]


def _scatter_wait(send_sems, recv_sems, parts, slots, axes, after, *, name):
    n = len(parts)
    geo = _scatter_geo(parts, axes)

    def body(*refs):
        p_refs, q_refs = refs[:n], refs[n:2 * n]
        for cp in _scatter_copies(p_refs, q_refs, geo, refs[2 * n], refs[2 * n + 1]):
            cp.wait_send()
            cp.wait_recv()

    out = pl.pallas_call(
        body, name=name,
        out_shape=[pltpu.HBM(a.shape, a.dtype) for a in (*parts, *slots)],
        in_specs=[HBM_SPEC] * (2 * n) + [SEM_SPEC, SEM_SPEC, ANY],
        out_specs=[HBM_SPEC] * (2 * n),
        input_output_aliases={i: i for i in range(2 * n)},
        compiler_params=SPLIT_COPY,
    )(*parts, *slots, send_sems, recv_sems, after)
    return list(out[:n]), list(out[n:])


def _sibling_swap(grads, *, name):
    n = len(grads)
    out_shape = [jax.ShapeDtypeStruct((g.shape[0], g.shape[1] // 2, g.shape[2]), g.dtype)
                 for g in grads]

    def body(*refs):
        g_refs, o_refs = refs[:n], refs[n:2 * n]
        send_sems, recv_sems = refs[2 * n:]
        x, y, c, _ = _place()
        cps = []
        for i in range(n):
            rh = grads[i].shape[1] // 2
            src = g_refs[i].at[:, pl.ds(pl.multiple_of((1 - c) * rh, 16), rh), :]
            cp = pltpu.make_async_remote_copy(
                src_ref=src, dst_ref=o_refs[i], send_sem=send_sems.at[i], recv_sem=recv_sems.at[i],
                device_id=(x, y, 1 - c), device_id_type=MESH)
            cp.start()
            cps.append(cp)
        for cp in cps:
            cp.wait()

    return pl.pallas_call(
        body, in_specs=[ANY] * n, out_specs=[ANY] * n, out_shape=out_shape,
        scratch_shapes=[pltpu.SemaphoreType.DMA((n,)), pltpu.SemaphoreType.DMA((n,))],
        name=name)(*grads)


def _pair_add(g3, rx, c_arr, *, out_dtype, name):
    nb, rows, cols = g3.shape
    rh = rows // 2
    tr = next(t for t in (128, 64, 32, 16) if rh % t == 0)
    nt = rh // tr

    def body(c_ref, g_ref, r_ref, o_ref):
        o_ref[...] = (g_ref[...] + r_ref[...]).astype(out_dtype)

    return pl.pallas_call(
        body,
        grid_spec=pltpu.PrefetchScalarGridSpec(
            num_scalar_prefetch=1, grid=(nb, nt),
            in_specs=[pl.BlockSpec((None, tr, cols), lambda b, i, c_ref: (b, c_ref[0] * nt + i, 0)),
                      pl.BlockSpec((None, tr, cols), lambda b, i, c_ref: (b, i, 0))],
            out_specs=pl.BlockSpec((None, tr, cols), lambda b, i, c_ref: (b, i, 0))),
        out_shape=jax.ShapeDtypeStruct((nb, rh, cols), out_dtype),
        compiler_params=_cp("parallel", "parallel"), name=name)(c_arr, g3, rx)


def _sum_slots(q, *, name):
    ns, rows, cols = q.shape
    tr = next(t for t in (128, 64, 32, 16, 8) if rows % t == 0)

    def body(q_ref, o_ref):
        acc = q_ref[0].astype(F32)
        for k in range(1, ns):
            acc = acc + q_ref[k].astype(F32)
        o_ref[...] = acc

    return pl.pallas_call(
        body, grid=(rows // tr,),
        in_specs=[pl.BlockSpec((ns, tr, cols), lambda i: (0, i, 0))],
        out_specs=_rspec(tr, cols),
        out_shape=jax.ShapeDtypeStruct((rows, cols), F32),
        compiler_params=_cp("parallel"), name=name)(q)


def _sum_chips(q, p, place, axis, *, name):
    _, rh, cw = q.shape
    tr = next(t for t in (128, 64, 32, 16) if rh % t == 0)
    nt = rh // tr

    def body(p_ref, *refs):
        q_refs, own_ref, o_ref = refs[:N_CHIPS], refs[N_CHIPS], refs[N_CHIPS + 1]
        chip = p_ref[0]
        acc = jnp.where(chip == 0, own_ref[...], q_refs[0][...]).astype(F32)
        for k in range(1, N_CHIPS):
            acc = acc + jnp.where(chip == k, own_ref[...], q_refs[k][...]).astype(F32)
        o_ref[...] = acc

    def slot_spec(k):
        return pl.BlockSpec((None, tr, cw),
                            lambda i, p_ref: (jnp.where(p_ref[0] == k, (k + 1) % N_CHIPS, k), i, 0))

    if axis == 1:
        own_spec = pl.BlockSpec((None, tr, cw), lambda i, p_ref: (0, i, p_ref[0]))
    else:
        own_spec = pl.BlockSpec((None, tr, cw), lambda i, p_ref: (p_ref[0], i, 0))
    return pl.pallas_call(
        body,
        grid_spec=pltpu.PrefetchScalarGridSpec(
            num_scalar_prefetch=1, grid=(nt,),
            in_specs=[slot_spec(k) for k in range(N_CHIPS)] + [own_spec],
            out_specs=pl.BlockSpec((tr, cw), lambda i, p_ref: (p_ref[1] * nt + i, 0))),
        out_shape=jax.ShapeDtypeStruct((2 * rh, cw), F32),
        compiler_params=_cp("parallel"), name=name)(place, *([q] * N_CHIPS), p)


def _sibling_share(shards, *, name):
    n = len(shards)

    def body(*refs):
        o_refs = refs[n:2 * n]
        send_sems, recv_sems = refs[2 * n:]
        x, y, c, _ = _place()
        cps = []
        for i in range(n):
            rh = shards[i].shape[0] // 2
            mine = o_refs[i].at[pl.ds(pl.multiple_of(c * rh, 8), rh), :]
            cp = pltpu.make_async_remote_copy(
                src_ref=mine, dst_ref=mine, send_sem=send_sems.at[i], recv_sem=recv_sems.at[i],
                device_id=(x, y, 1 - c), device_id_type=MESH)
            cp.start()
            cps.append(cp)
        for i in range(n):
            rh = shards[i].shape[0] // 2
            theirs = o_refs[i].at[pl.ds(pl.multiple_of((1 - c) * rh, 8), rh), :]
            pltpu.make_async_remote_copy(
                src_ref=theirs, dst_ref=theirs, send_sem=send_sems.at[i], recv_sem=recv_sems.at[i],
                device_id=(x, y, c), device_id_type=MESH).wait_recv()
        for cp in cps:
            cp.wait_send()

    return pl.pallas_call(
        body, in_specs=[ANY] * n, out_specs=[ANY] * n,
        out_shape=[jax.ShapeDtypeStruct(h.shape, h.dtype) for h in shards],
        input_output_aliases={i: i for i in range(n)},
        scratch_shapes=[pltpu.SemaphoreType.DMA((n,)), pltpu.SemaphoreType.DMA((n,))],
        name=name)(*shards)


def _gather_all(blk, *, name, after=None):
    rows, cols = blk.shape
    extra = [] if after is None else [after]

    def body(x_ref, *refs):
        out_ref, send_sems, recv_sems, local_sem = refs[len(extra):]
        x, y, c = lax.axis_index("x"), lax.axis_index("y"), lax.axis_index("c")
        me = 4 * x + 2 * y + c
        mine = pltpu.make_async_copy(x_ref, out_ref.at[me], local_sem)
        mine.start()
        cps = []
        for k in range(1, N_DEV):
            tx = (1 - x) if (k >> 2) & 1 else x
            ty = (1 - y) if (k >> 1) & 1 else y
            tc = (1 - c) if k & 1 else c
            cp = pltpu.make_async_remote_copy(
                src_ref=x_ref, dst_ref=out_ref.at[me], send_sem=send_sems.at[k - 1],
                recv_sem=recv_sems.at[k - 1], device_id=(tx, ty, tc), device_id_type=MESH)
            cp.start()
            cps.append(cp)
        for k in range(1, N_DEV):
            tx = (1 - x) if (k >> 2) & 1 else x
            ty = (1 - y) if (k >> 1) & 1 else y
            tc = (1 - c) if k & 1 else c
            got = out_ref.at[4 * tx + 2 * ty + tc]
            pltpu.make_async_remote_copy(
                src_ref=got, dst_ref=got, send_sem=send_sems.at[k - 1], recv_sem=recv_sems.at[k - 1],
                device_id=(x, y, c), device_id_type=MESH).wait_recv()
        for cp in cps:
            cp.wait_send()
        mine.wait()

    vm = pl.BlockSpec(memory_space=pltpu.VMEM)
    return pl.pallas_call(
        body, in_specs=[vm] + [ANY] * len(extra), out_specs=vm,
        out_shape=jax.ShapeDtypeStruct((N_DEV, rows, cols), blk.dtype),
        scratch_shapes=[pltpu.SemaphoreType.DMA((N_DEV - 1,)), pltpu.SemaphoreType.DMA((N_DEV - 1,)),
                        pltpu.SemaphoreType.DMA],
        name=name)(blk, *extra)


def _as_rows(a):
    flat = a.reshape(-1)
    n = flat.shape[0]
    rows = -(-n // (8 * LANES)) * 8
    return jnp.pad(flat, (0, rows * LANES - n)).reshape(rows, LANES)


def _from_rows(p, shape):
    n = int(np.prod(shape))
    return p.reshape(-1)[:n].reshape(shape)


WEIGHT_AXES = (1, 1, 1, 0, 1, 0)
WIRE = BF16


def kernel(x, meta_tokens, w_in, w_na_out, w_hg_out, w_o, w_up, w_down, norm_mix, norm_mlp, norm_final, hg_norm, na_rpb, hg_lb_logits, loss_target, m_meta_tokens, m_w_in, m_w_na_out, m_w_hg_out, m_w_o, m_w_up, m_w_down, m_norm_mix, m_norm_mlp, m_norm_final, m_hg_norm, m_na_rpb, m_hg_lb_logits, v_meta_tokens, v_w_in, v_w_na_out, v_w_hg_out, v_w_o, v_w_up, v_w_down, v_norm_mix, v_norm_mlp, v_norm_final, v_hg_norm, v_na_rpb, v_hg_lb_logits):
    xi, yi, ci = lax.axis_index("x"), lax.axis_index("y"), lax.axis_index("c")
    chip = 2 * xi + yi
    d = x.shape[-1]
    dshard = meta_tokens.shape[1]
    hgw = hg_norm.shape[1]
    lbs = hg_lb_logits.shape[2]
    big = [w_in[0], w_na_out[0], w_hg_out[0], w_o[0], w_up[0], w_down[0]]
    big_m = [m_w_in[0], m_w_na_out[0], m_w_hg_out[0], m_w_o[0], m_w_up[0], m_w_down[0]]
    big_v = [v_w_in[0], v_w_na_out[0], v_w_hg_out[0], v_w_o[0], v_w_up[0], v_w_down[0]]

    place = jnp.stack([chip, ci]).astype(jnp.int32)
    own_w = [_cast_into_full(w, ax, place, name=f"cast_shard_{i}")
             for i, (w, ax) in enumerate(zip(big, WEIGHT_AXES))]
    in_axes, rest_axes = WEIGHT_AXES[:1], WEIGHT_AXES[1:]
    small_in = jnp.concatenate([_as_rows(meta_tokens), _as_rows(hg_lb_logits)], axis=0)
    small_all = _gather_all(small_in, name="gather_small_params")[0::2]
    in_send, in_recv, in_bufs, in_token = _allgather_start(own_w[:1], in_axes, small_all,
                                                           name="weight_allgather_in_start")
    ag_send, ag_recv, ag_bufs, ag_token = _allgather_start(own_w[1:], rest_axes, in_token,
                                                           name="weight_allgather_rest_start")

    def first_weight(after):
        got = _allgather_wait(in_send, in_recv, in_bufs, in_axes, after,
                              name="weight_allgather_in_wait")
        return _allgather_forward(got, in_axes, name="weight_allgather_in_forward")[0]

    def rest_weights(after):
        got = _allgather_wait(ag_send, ag_recv, ag_bufs, rest_axes, after,
                              name="weight_allgather_rest_wait")
        return _allgather_forward(got, rest_axes, name="weight_allgather_rest_forward")

    n_meta_rows = N_META * dshard // LANES
    meta_full = (small_all[:, :n_meta_rows].reshape(N_CHIPS, N_META, dshard)
                 .transpose(1, 0, 2).reshape(N_META, d))
    lbl_full = (small_all[:, n_meta_rows:].reshape(N_CHIPS, -1)[:, :4 * lbs]
                .reshape(N_CHIPS, 2, 2, lbs).transpose(1, 2, 0, 3).reshape(2, 2, N_CHIPS * lbs))
    lb = jax.nn.softmax(lbl_full, axis=1)[:, 0]

    c_arr = ci.reshape(1).astype(jnp.int32)

    def chip_partials(dws, axes, tag):
        g3 = [g.reshape(1, *g.shape) if ax == 1
              else g.reshape(N_CHIPS, g.shape[0] // N_CHIPS, g.shape[1]) for g, ax in zip(dws, axes)]
        rx = _sibling_swap(g3, name=f"grad_sibling_swap_{tag}")
        return [_pair_add(g, r, c_arr, out_dtype=WIRE, name=f"grad_pair_add_{tag}_{i}")
                for i, (g, r) in enumerate(zip(g3, rx))]

    flying = {}

    def scatter_behind(tag, axes):
        def hook(dws):
            send, recv, parts, slots, token = _scatter_start(
                chip_partials(dws, axes, tag), axes, name=f"grad_scatter_{tag}_start")
            flying[tag] = (send, recv, parts, slots)
            return token
        return hook

    def landed(tag, axes, after):
        parts, slots = _scatter_wait(*flying[tag], axes, after, name=f"grad_scatter_{tag}_wait")
        return [_sum_chips(q, p, place, ax, name=f"grad_sum_chips_{tag}_{i}")
                for i, (q, p, ax) in enumerate(zip(slots, parts, axes))]

    (loss, dx, dmeta, *_, dg_mix, dg_mlp, dg_fin, d_gain, d_rpb, d_lb) = _local_step(
        x[0], loss_target[0], meta_full, first_weight, rest_weights,
        norm_mix + ag_token[0:1, 0:1], norm_mlp, norm_final.reshape(1, d), hg_norm, na_rpb[0], lb,
        scatter_behind("rest", rest_axes), lambda dw_in: scatter_behind("in", in_axes)([dw_in]))

    halves_rest = landed("rest", rest_axes, dx)
    halves_in = landed("in", in_axes, halves_rest[-1])
    g_big = _sibling_share(halves_in + halves_rest, name="grad_sibling_share")

    d_rpb_c = d_rpb[:, :2 * NA_WIN_W - 1]
    small_g = [dmeta, dg_mix, dg_mlp, dg_fin, d_gain, d_rpb_c, d_lb, loss]
    packed = jnp.concatenate([_as_rows(a) for a in small_g], axis=0)
    total = _sum_slots(_gather_all(packed, after=halves_in[0], name="gather_small_grads"),
                       name="sum_small_grads")
    offs = np.cumsum([0] + [_as_rows(a).shape[0] for a in small_g])
    take = lambda i, shape: _from_rows(total[offs[i]:offs[i + 1]], shape)
    g_meta_full = take(0, (N_META, d))
    g_norm_mix, g_norm_mlp = take(1, (1, d)), take(2, (1, d))
    g_norm_final = take(3, (d,))
    g_hg_norm = take(4, (1, hgw))
    g_rpb = take(5, na_rpb.shape)
    g_lb = take(6, (2, hgw))
    loss_total = take(7, (1, LANES))[0, 0]
    g_meta = lax.dynamic_slice_in_dim(g_meta_full, chip * dshard, dshard, axis=1)
    dl0 = lb * (1.0 - lb) * g_lb
    g_lbl_full = jnp.stack([dl0, -dl0], axis=1)
    g_lbl = lax.dynamic_slice_in_dim(g_lbl_full, chip * lbs, lbs, axis=2)

    big_out = [_adamw(w, g, m, v, name=f"adamw_{i}")
               for i, (w, g, m, v) in enumerate(zip(big, g_big, big_m, big_v))]
    small_w = [meta_tokens, norm_mix, norm_mlp, norm_final, hg_norm, na_rpb, hg_lb_logits]
    small_gr = [g_meta, g_norm_mix, g_norm_mlp, g_norm_final, g_hg_norm, g_rpb, g_lbl]
    small_m = [m_meta_tokens, m_norm_mix, m_norm_mlp, m_norm_final, m_hg_norm, m_na_rpb, m_hg_lb_logits]
    small_v = [v_meta_tokens, v_norm_mix, v_norm_mlp, v_norm_final, v_hg_norm, v_na_rpb, v_hg_lb_logits]
    pk = lambda lst: jnp.concatenate([_as_rows(a) for a in lst], axis=0)
    sd, sm, sv = _adamw(pk(small_w), pk(small_gr), pk(small_m), pk(small_v), name="adamw_small")
    soffs = np.cumsum([0] + [_as_rows(a).shape[0] for a in small_w])
    unpk = lambda p: [_from_rows(p[soffs[i]:soffs[i + 1]], small_w[i].shape) for i in range(len(small_w))]
    sd, sm, sv = unpk(sd), unpk(sm), unpk(sv)

    def order(bigs, smalls):
        return [smalls[0]] + [b.reshape(1, *b.shape) for b in bigs] + smalls[1:]

    grads = order(g_big, small_gr)
    deltas = order([o[0] for o in big_out], sd)
    new_m = order([o[1] for o in big_out], sm)
    new_v = order([o[2] for o in big_out], sv)
    return (loss_total, dx.reshape(1, *dx.shape), *grads, *deltas, *new_m, *new_v)
```

```python
import functools

import numpy as np
import jax
import jax.numpy as jnp
from jax import lax
from jax.experimental import pallas as pl
from jax.experimental.pallas import tpu as pltpu

F32 = jnp.float32
BF16 = jnp.bfloat16
HIGHEST = lax.Precision.HIGHEST

GRID_W = 64
N_META = 16
EPS = 1e-6
NA_HEAD_DIM = 64
NA_WIN_H = 8
NA_WIN_W = 16
HG_DK = 128
HG_CHUNK = 16
LANES = 128
ROW_ALIGN = 128
VMEM_LIMIT = 48 * 1024 * 1024

ADAM_LR = 0.001
ADAM_B1 = 0.9
ADAM_B2 = 0.999
ADAM_EPS = 1e-08
ADAM_WD = 0.01
ADAM_STEP = 10

MESH = pl.DeviceIdType.MESH


def _cp(*sem):
    return pltpu.CompilerParams(dimension_semantics=sem, vmem_limit_bytes=VMEM_LIMIT)


def _sigmoid(x):
    return 1.0 / (1.0 + jnp.exp(-x))


def _dot(a, b, dims, precision=None):
    return lax.dot_general(a, b, (dims, ((), ())), preferred_element_type=F32, precision=precision)


def _nn(a, b, **kw):
    return _dot(a, b, ((1,), (0,)), **kw)


def _nt(a, b, **kw):
    return _dot(a, b, ((1,), (1,)), **kw)


def _tn(a, b, **kw):
    return _dot(a, b, ((0,), (0,)), **kw)


def _matmul(a, b, *, ta=False, tb=False, tm=None, tn=None, tk=None, out_dtype=F32, name,
            precision=None, after=None, epilogue=None, tiles=(), out_dtypes=None):
    extra = [] if after is None else [after]
    single = out_dtypes is None
    if single:
        out_dtypes = (out_dtype,)
    n_t, n_o = len(tiles), len(out_dtypes)
    if ta:
        kdim, m = a.shape
    else:
        m, kdim = a.shape
    if tb:
        n, k2 = b.shape
    else:
        k2, n = b.shape
    assert kdim == k2, (a.shape, b.shape, ta, tb)
    if tm is None:
        if ta:
            tm = next(t for t in (512, 256, 128, m) if m % t == 0)
        else:
            tm = m // 2 if (m // 2) % 16 == 0 and m > 512 else m
    if tn is None:
        wide = (1024,) if not ta and not tiles and n_o == 1 else ()
        tn = next(t for t in (*wide, 512, 256, 128, n) if n % t == 0)
    if tk is None:
        tk = kdim if ta else next(t for t in (1024, 512, 256, 128, kdim) if kdim % t == 0)
    assert m % tm == 0 and n % tn == 0 and kdim % tk == 0, (m, n, kdim, tm, tn, tk)
    nk = kdim // tk
    op_dtype = F32 if precision is not None else BF16

    def body(a_ref, b_ref, *refs):
        t_refs = refs[:n_t]
        o_refs = refs[n_t + len(extra):n_t + len(extra) + n_o]
        av = a_ref[...].astype(op_dtype)
        bv = b_ref[...].astype(op_dtype)
        dims = ((0 if ta else 1,), (1 if tb else 0,))
        part = _dot(av, bv, dims, precision=precision)

        def finish(acc):
            outs = (acc,) if epilogue is None else epilogue(acc, *[t[...] for t in t_refs])
            for o_ref, val in zip(o_refs, outs):
                o_ref[...] = val.astype(o_ref.dtype)

        if nk == 1:
            finish(part)
            return
        acc_ref = refs[-1]
        kk = pl.program_id(2)

        @pl.when(kk == 0)
        def _():
            acc_ref[...] = part

        @pl.when((kk > 0) & (kk < nk - 1))
        def _():
            acc_ref[...] += part

        @pl.when(kk == nk - 1)
        def _():
            finish(acc_ref[...] + part)

    a_spec = (pl.BlockSpec((tk, tm), lambda i, j, k: (k, i)) if ta
              else pl.BlockSpec((tm, tk), lambda i, j, k: (i, k)))
    b_spec = (pl.BlockSpec((tn, tk), lambda i, j, k: (j, k)) if tb
              else pl.BlockSpec((tk, tn), lambda i, j, k: (k, j)))
    for _, off in tiles:
        assert off % tn == 0, (off, tn)
    t_specs = [pl.BlockSpec((tm, tn), functools.partial(lambda i, j, k, o: (i, o + j), o=off // tn))
               for _, off in tiles]
    o_spec = pl.BlockSpec((tm, tn), lambda i, j, k: (i, j))
    outs = pl.pallas_call(
        body,
        grid=(m // tm, n // tn, nk),
        in_specs=[a_spec, b_spec] + t_specs + [pl.BlockSpec(memory_space=pl.ANY)] * len(extra),
        out_specs=[o_spec] * n_o,
        out_shape=[jax.ShapeDtypeStruct((m, n), dt) for dt in out_dtypes],
        scratch_shapes=[pltpu.VMEM((tm, tn), F32)] if nk > 1 else [],
        compiler_params=_cp("parallel", "parallel", "arbitrary"),
        name=name,
    )(a, b, *[t for t, _ in tiles], *extra)
    return outs[0] if single else outs


def _rspec(tr, w, cb=0):
    return pl.BlockSpec((tr, w), lambda i: (i, cb))


def _fspec(shape):
    nd = len(shape)
    return pl.BlockSpec(shape, lambda i: (0,) * nd)


def _row_tile(lp):
    return ROW_ALIGN if lp % ROW_ALIGN == 0 else lp


def _rmsnorm_fwd(x, g, *, name):
    lp, d = x.shape
    tr = _row_tile(lp)

    def body(x_ref, g_ref, o_ref):
        xv = x_ref[...]
        r = lax.rsqrt(jnp.mean(xv * xv, axis=-1, keepdims=True) + EPS)
        o_ref[...] = (xv * r * g_ref[...]).astype(BF16)

    return pl.pallas_call(
        body, grid=(lp // tr,),
        in_specs=[_rspec(tr, d), _fspec((1, d))],
        out_specs=_rspec(tr, d),
        out_shape=jax.ShapeDtypeStruct((lp, d), BF16),
        compiler_params=_cp("parallel"), name=name)(x, g)


def _residual_norm(h, t, g, *, name):
    lp, d = h.shape
    tr = _row_tile(lp)

    def body(h_ref, t_ref, g_ref, h1_ref, m_ref):
        xv = h_ref[...] + t_ref[...]
        h1_ref[...] = xv
        r = lax.rsqrt(jnp.mean(xv * xv, axis=-1, keepdims=True) + EPS)
        m_ref[...] = (xv * r * g_ref[...]).astype(BF16)

    return pl.pallas_call(
        body, grid=(lp // tr,),
        in_specs=[_rspec(tr, d), _rspec(tr, d), _fspec((1, d))],
        out_specs=[_rspec(tr, d), _rspec(tr, d)],
        out_shape=[jax.ShapeDtypeStruct((lp, d), F32), jax.ShapeDtypeStruct((lp, d), BF16)],
        compiler_params=_cp("parallel"), name=name)(h, t, g)


def _rmsnorm_bwd_add(x, g, dy, dres, *, name):
    lp, d = x.shape
    tr = _row_tile(lp)

    def body(x_ref, g_ref, dy_ref, dr_ref, dx_ref, dg_ref):
        @pl.when(pl.program_id(0) == 0)
        def _():
            dg_ref[...] = jnp.zeros_like(dg_ref)

        xv = x_ref[...]
        r = lax.rsqrt(jnp.mean(xv * xv, axis=-1, keepdims=True) + EPS)
        xh = xv * r
        dyv = dy_ref[...]
        dg_ref[...] += jnp.sum(dyv * xh, axis=0, keepdims=True)
        dxh = dyv * g_ref[...]
        dx_ref[...] = dr_ref[...] + r * (dxh - xh * jnp.mean(dxh * xh, axis=-1, keepdims=True))

    return pl.pallas_call(
        body, grid=(lp // tr,),
        in_specs=[_rspec(tr, d), _fspec((1, d)), _rspec(tr, d), _rspec(tr, d)],
        out_specs=[_rspec(tr, d), _fspec((1, d))],
        out_shape=[jax.ShapeDtypeStruct((lp, d), F32), jax.ShapeDtypeStruct((1, d), F32)],
        compiler_params=_cp("arbitrary"), name=name)(x, g, dy, dres)


def _final_loss(h1, t2, g, tgt, *, n_tok, name):
    lp, d = h1.shape
    tr = _row_tile(lp)

    def body(h_ref, t_ref, g_ref, tg_ref, dh_ref, loss_ref, dg_ref):
        i = pl.program_id(0)

        @pl.when(i == 0)
        def _():
            loss_ref[...] = jnp.zeros_like(loss_ref)
            dg_ref[...] = jnp.zeros_like(dg_ref)

        xv = h_ref[...] + t_ref[...]
        r = lax.rsqrt(jnp.mean(xv * xv, axis=-1, keepdims=True) + EPS)
        xh = xv * r
        gv = g_ref[...]
        row = i * tr + lax.broadcasted_iota(jnp.int32, (tr, 1), 0)
        valid = (row >= N_META) & (row < N_META + n_tok)
        err = jnp.where(valid, xh * gv - tg_ref[...], 0.0)
        loss_ref[...] += jnp.sum(0.5 * err * err) / d
        dy = err / d
        dg_ref[...] += jnp.sum(dy * xh, axis=0, keepdims=True)
        dxh = dy * gv
        dh_ref[...] = r * (dxh - xh * jnp.mean(dxh * xh, axis=-1, keepdims=True))

    return pl.pallas_call(
        body, grid=(lp // tr,),
        in_specs=[_rspec(tr, d), _rspec(tr, d), _fspec((1, d)), _rspec(tr, d)],
        out_specs=[_rspec(tr, d), _fspec((1, LANES)), _fspec((1, d))],
        out_shape=[jax.ShapeDtypeStruct((lp, d), F32), jax.ShapeDtypeStruct((1, LANES), F32),
                   jax.ShapeDtypeStruct((1, d), F32)],
        compiler_params=_cp("arbitrary"), name=name)(h1, t2, g, tgt)


def _hg_out(o_f, o_b, proj, gain, *, col_g, name):
    lp, w = o_f.shape
    tr = _row_tile(lp)
    hh = w // HG_DK

    def body(of_ref, ob_ref, g_ref, gain_ref, y_ref):
        gv = g_ref[...]
        sg = gv * _sigmoid(gv)
        for h in range(hh):
            sl = slice(h * HG_DK, (h + 1) * HG_DK)
            o = of_ref[:, sl] + ob_ref[:, sl]
            r = lax.rsqrt(jnp.mean(o * o, axis=-1, keepdims=True) + EPS)
            y_ref[:, sl] = (o * r * gain_ref[:, sl] * sg[:, sl]).astype(BF16)

    return pl.pallas_call(
        body, grid=(lp // tr,),
        in_specs=[_rspec(tr, w), _rspec(tr, w), _rspec(tr, w, col_g // w), _fspec((1, w))],
        out_specs=_rspec(tr, w),
        out_shape=jax.ShapeDtypeStruct((lp, w), BF16),
        compiler_params=_cp("parallel"), name=name)(o_f, o_b, proj, gain)


def _hg_out_bwd(o_f, o_b, proj, gain, dy, *, col_g, name):
    lp, w = o_f.shape
    tr = _row_tile(lp)
    hh = w // HG_DK

    def body(of_ref, ob_ref, g_ref, gain_ref, dy_ref, do_ref, dg_ref, dgain_ref):
        @pl.when(pl.program_id(0) == 0)
        def _():
            dgain_ref[...] = jnp.zeros_like(dgain_ref)

        for h in range(hh):
            sl = slice(h * HG_DK, (h + 1) * HG_DK)
            gv = g_ref[:, sl]
            s = _sigmoid(gv)
            sg = gv * s
            dsg = s + gv * s * (1.0 - s)
            o = of_ref[:, sl] + ob_ref[:, sl]
            r = lax.rsqrt(jnp.mean(o * o, axis=-1, keepdims=True) + EPS)
            on = o * r
            dyv = dy_ref[:, sl]
            gn = gain_ref[:, sl]
            dgain_ref[:, sl] += jnp.sum(dyv * on * sg, axis=0, keepdims=True)
            dg_ref[:, sl] = (dyv * on * gn * dsg).astype(BF16)
            don = dyv * gn * sg
            do_ref[:, sl] = r * (don - on * jnp.mean(don * on, axis=-1, keepdims=True))

    return pl.pallas_call(
        body, grid=(lp // tr,),
        in_specs=[_rspec(tr, w), _rspec(tr, w), _rspec(tr, w, col_g // w), _fspec((1, w)),
                  _rspec(tr, w)],
        out_specs=[_rspec(tr, w), _rspec(tr, w), _fspec((1, w))],
        out_shape=[jax.ShapeDtypeStruct((lp, w), F32), jax.ShapeDtypeStruct((lp, w), BF16),
                   jax.ShapeDtypeStruct((1, w), F32)],
        compiler_params=_cp("arbitrary"), name=name)(o_f, o_b, proj, gain, dy)


HG_ROWS = 128
HG_HALVES = (1, 2, 4, 8, 16, 32, 64)


def _hg_gates(zq, z, lbv):
    qh = zq * _sigmoid(zq)
    s = _sigmoid(z)
    f = lbv + (1.0 - lbv) * s
    kk = (1.0 - lbv) * _sigmoid(-z)
    return qh, s, f, jnp.log(f), kk


def _block_cumsum(g, pos, suffix):
    x = g
    for k in HG_HALVES:
        if suffix:
            x = x + jnp.where(pos < HG_ROWS - k, pltpu.roll(x, HG_ROWS - k, 0), 0.0)
        else:
            x = x + jnp.where(pos >= k, pltpu.roll(x, k, 0), 0.0)
    return x


def _pair_levels(b, pos, reverse):
    out = []
    first = b
    for m in HG_HALVES:
        if m > 1:
            first = jnp.where((pos & (m - 1)) >= m // 2, pltpu.roll(first, m // 2, 0), first)
        nxt = pltpu.roll(first, HG_ROWS - m, 0)
        upper = (pos & (2 * m - 1)) >= m
        if reverse:
            eq = jnp.where(upper, 0.0, jnp.exp(jnp.minimum(b - nxt, 0.0)))
            ek = jnp.where(upper, jnp.exp(jnp.minimum(first - b, 0.0)), 0.0)
        else:
            eq = jnp.where(upper, jnp.exp(jnp.minimum(b - first, 0.0)), 0.0)
            ek = jnp.where(upper, 0.0, jnp.exp(jnp.minimum(nxt - b, 0.0)))
        out.append((m.bit_length(), eq, ek))
    return out


def _hg_scan_fwd(proj, lb, *, reverse, col_q, col_z, col_i, hh, name):
    lp = proj.shape[0]
    n_blocks = lp // HG_ROWS
    last = 0 if reverse else HG_ROWS - 1

    def body(q_ref, z_ref, i_ref, lb_ref, o_ref, st_ref):
        lbv = lb_ref[...]
        pos = lax.broadcasted_iota(jnp.int32, (HG_ROWS, 1), 0)
        ri = lax.broadcasted_iota(jnp.int32, (HG_ROWS, HG_ROWS), 0)
        ci = lax.broadcasted_iota(jnp.int32, (HG_ROWS, HG_ROWS), 1)

        def block(bi, st):
            bb = (n_blocks - 1 - bi) if reverse else bi
            r0 = pl.multiple_of(bb * HG_ROWS, HG_ROWS)
            v16 = i_ref[pl.ds(r0, HG_ROWS), :].astype(BF16)
            qh, _, _, g, kk = _hg_gates(q_ref[pl.ds(r0, HG_ROWS), :], z_ref[pl.ds(r0, HG_ROWS), :],
                                        lbv)
            b = _block_cumsum(g, pos, reverse)
            bl = b[last:last + 1, :]
            qe = (qh * jnp.exp(b)).astype(BF16)
            kd = (kk * jnp.exp(bl - b)).astype(BF16)
            a = jnp.where(ri == ci, jnp.sum(qh * kk, axis=1, keepdims=True), 0.0)
            for sh, eq, ek in _pair_levels(b, pos, reverse):
                a = a + jnp.where((ri >> sh) == (ci >> sh),
                                  _nt((qh * eq).astype(BF16), (kk * ek).astype(BF16)), 0.0)
            st_ref[bb] = st
            o_ref[pl.ds(r0, HG_ROWS), :] = _nn(a.astype(BF16), v16) + _nt(qe, st.astype(BF16))
            return jnp.exp(bl) * st + _tn(v16, kd)

        lax.fori_loop(0, n_blocks, block, jnp.zeros((HG_DK, HG_DK), F32))

    cspec = lambda col: pl.BlockSpec((lp, HG_DK), lambda h: (0, col // HG_DK + h))
    return pl.pallas_call(
        body, grid=(hh,),
        in_specs=[cspec(col_q), cspec(col_z), cspec(col_i),
                  pl.BlockSpec((None, 1, HG_DK), lambda h: (h, 0, 0))],
        out_specs=[pl.BlockSpec((lp, HG_DK), lambda h: (0, h)),
                   pl.BlockSpec((None, n_blocks, HG_DK, HG_DK), lambda h: (h, 0, 0, 0))],
        out_shape=[jax.ShapeDtypeStruct((lp, hh * HG_DK), F32),
                   jax.ShapeDtypeStruct((hh, n_blocks, HG_DK, HG_DK), F32)],
        compiler_params=_cp("parallel"), name=name)(proj, proj, proj, lb)


def _hg_scan_bwd(proj, lb, states, do, *, reverse, col_q, col_z, col_i, hh, name):
    lp = proj.shape[0]
    n_blocks = lp // HG_ROWS
    last = 0 if reverse else HG_ROWS - 1

    def body(q_ref, z_ref, i_ref, lb_ref, st_ref, do_ref, dq_ref, dz_ref, dv_ref, dlb_ref):
        lbv = lb_ref[...]
        pos = lax.broadcasted_iota(jnp.int32, (HG_ROWS, 1), 0)
        ri = lax.broadcasted_iota(jnp.int32, (HG_ROWS, HG_ROWS), 0)
        ci = lax.broadcasted_iota(jnp.int32, (HG_ROWS, HG_ROWS), 1)

        def block(bi, carry):
            dst, dlb = carry
            bb = bi if reverse else (n_blocks - 1 - bi)
            r0 = pl.multiple_of(bb * HG_ROWS, HG_ROWS)
            zq = q_ref[pl.ds(r0, HG_ROWS), :]
            v16 = i_ref[pl.ds(r0, HG_ROWS), :].astype(BF16)
            do16 = do_ref[pl.ds(r0, HG_ROWS), :].astype(BF16)
            qh, s, f, g, kk = _hg_gates(zq, z_ref[pl.ds(r0, HG_ROWS), :], lbv)
            b = _block_cumsum(g, pos, reverse)
            bl = b[last:last + 1, :]
            eb = jnp.exp(b)
            ebl = jnp.exp(bl - b)
            decay = jnp.exp(bl)
            qe16 = (qh * eb).astype(BF16)
            kd16 = (kk * ebl).astype(BF16)
            st = st_ref[bb]
            st16, dst16 = st.astype(BF16), dst.astype(BF16)
            same_row = ri == ci
            da = _nt(do16, v16)
            da_diag = jnp.sum(jnp.where(same_row, da, 0.0), axis=1, keepdims=True)
            dq_state = eb * _nn(do16, st16)
            dk_state = ebl * _nn(v16, dst16)
            dq = dq_state + da_diag * kk
            dk = dk_state + da_diag * qh
            dbl = (decay * jnp.sum(st * dst, axis=0, keepdims=True)
                   + jnp.sum(kk * dk_state, axis=0, keepdims=True))
            db = qh * dq_state - kk * dk_state + jnp.where(pos == last, dbl, 0.0)
            a = jnp.where(same_row, jnp.sum(qh * kk, axis=1, keepdims=True), 0.0)
            for sh, eq, ek in _pair_levels(b, pos, reverse):
                same = (ri >> sh) == (ci >> sh)
                q16, k16 = (qh * eq).astype(BF16), (kk * ek).astype(BF16)
                a = a + jnp.where(same, _nt(q16, k16), 0.0)
                da16 = jnp.where(same, da, 0.0).astype(BF16)
                gq, gk = _nn(da16, k16), _tn(da16, q16)
                dq = dq + eq * gq
                dk = dk + ek * gk
                db = db + (q16.astype(F32) * gq - k16.astype(F32) * gk)
            dg = _block_cumsum(db, pos, not reverse)
            df = dg / f - dk
            sq = _sigmoid(zq)
            dq_ref[pl.ds(r0, HG_ROWS), :] = dq * (sq + zq * sq * (1.0 - sq))
            dz_ref[pl.ds(r0, HG_ROWS), :] = df * (1.0 - lbv) * s * (1.0 - s)
            dv_ref[pl.ds(r0, HG_ROWS), :] = _nt(kd16, dst16) + _tn(a.astype(BF16), do16)
            return (decay * dst + _tn(do16, qe16),
                    dlb + jnp.sum(df * (1.0 - s), axis=0, keepdims=True))

        _, dlb = lax.fori_loop(0, n_blocks, block,
                               (jnp.zeros((HG_DK, HG_DK), F32), jnp.zeros((1, HG_DK), F32)))
        dlb_ref[...] = dlb

    cspec = lambda col: pl.BlockSpec((lp, HG_DK), lambda h: (0, col // HG_DK + h))
    ospec = pl.BlockSpec((lp, HG_DK), lambda h: (0, h))
    sds = jax.ShapeDtypeStruct((lp, hh * HG_DK), F32)
    return pl.pallas_call(
        body, grid=(hh,),
        in_specs=[cspec(col_q), cspec(col_z), cspec(col_i),
                  pl.BlockSpec((None, 1, HG_DK), lambda h: (h, 0, 0)),
                  pl.BlockSpec((None, n_blocks, HG_DK, HG_DK), lambda h: (h, 0, 0, 0)),
                  ospec],
        out_specs=[ospec, ospec, ospec, pl.BlockSpec((None, 1, HG_DK), lambda h: (h, 0, 0))],
        out_shape=[sds, sds, sds, jax.ShapeDtypeStruct((hh, 1, HG_DK), F32)],
        compiler_params=_cp("parallel"), name=name)(proj, proj, proj, lb, states, do)


def _na_rows(r, rows):
    rs = jnp.clip(r - NA_WIN_H // 2, 0, rows - NA_WIN_H)
    i0 = rs - r + (NA_WIN_H - 1)
    q0 = pl.multiple_of(N_META + GRID_W * r, 16)
    k0 = pl.multiple_of(N_META + GRID_W * rs, 16)
    return i0, q0, k0


def _na_scores(q16, k16, km16, tb_ref, i0, scale):
    s = _nt(q16, k16) * scale
    bias = jnp.concatenate([tb_ref[i0 + j] for j in range(NA_WIN_H)], axis=1)
    return s + bias, _nt(q16, km16) * scale


NA_HB = LANES // NA_HEAD_DIM


def _na_head_lanes():
    lane = lax.broadcasted_iota(jnp.int32, (1, LANES), 1)
    return [lane // NA_HEAD_DIM == h for h in range(NA_HB)]


def _na_only(mask, x):
    return jnp.where(mask, x, jnp.zeros_like(x))


def _na_fwd(proj, tb, *, n_tok, nh, name):
    lp = proj.shape[0]
    dh, hb = NA_HEAD_DIM, NA_HB
    naw = nh * dh
    rows = n_tok // GRID_W
    scale = dh ** -0.5
    kw = NA_WIN_H * GRID_W

    def body(q_ref, k_ref, v_ref, tb_ref, o_ref, lse_ref, q16_ref, k16_ref, v16_ref):
        o_ref[...] = jnp.zeros_like(o_ref)
        lse_ref[...] = jnp.zeros_like(lse_ref)
        q16_ref[...] = q_ref[...].astype(BF16)
        k16_ref[...] = k_ref[...].astype(BF16)
        v16_ref[...] = v_ref[...].astype(BF16)
        heads = _na_head_lanes()
        km = k16_ref[0:N_META, :]
        vm = v16_ref[0:N_META, :]
        qm = q16_ref[0:N_META, :]
        o_m = None
        for h in range(hb):
            s = _nt(_na_only(heads[h], qm), km) * scale
            m = jnp.max(s, axis=1, keepdims=True)
            p = jnp.exp(s - m)
            l = jnp.sum(p, axis=1, keepdims=True)
            o_h = _nn(p.astype(BF16), vm) / l
            o_m = o_h if o_m is None else jnp.where(heads[h], o_h, o_m)
            lse_ref[h, 0:N_META, :] = m + jnp.log(l)
        o_ref[0:N_META, :] = o_m

        def step(r, carry):
            i0, q0, k0 = _na_rows(r, rows)
            q16 = q16_ref[pl.ds(q0, GRID_W), :]
            k16 = k16_ref[pl.ds(k0, kw), :]
            v16 = v16_ref[pl.ds(k0, kw), :]
            o = None
            for h in range(hb):
                s, sm = _na_scores(_na_only(heads[h], q16), k16, km, tb_ref.at[h], i0, scale)
                m = jnp.maximum(jnp.max(s, axis=1, keepdims=True),
                                jnp.max(sm, axis=1, keepdims=True))
                p = jnp.exp(s - m)
                pm = jnp.exp(sm - m)
                l = jnp.sum(p, axis=1, keepdims=True) + jnp.sum(pm, axis=1, keepdims=True)
                o_h = (_nn(p.astype(BF16), v16) + _nn(pm.astype(BF16), vm)) / l
                o = o_h if o is None else jnp.where(heads[h], o_h, o)
                lse_ref[h, pl.ds(q0, GRID_W), :] = m + jnp.log(l)
            o_ref[pl.ds(q0, GRID_W), :] = o
            return carry

        lax.fori_loop(0, rows, step, 0, unroll=2)

    cblk = lambda col: pl.BlockSpec((lp, LANES), lambda g: (0, col // LANES + g))
    return pl.pallas_call(
        body, grid=(nh // hb,),
        in_specs=[cblk(0), cblk(naw), cblk(2 * naw),
                  pl.BlockSpec((hb, 2 * NA_WIN_H - 1, GRID_W, GRID_W), lambda g: (g, 0, 0, 0))],
        out_specs=[cblk(0), pl.BlockSpec((hb, lp, 1), lambda g: (g, 0, 0))],
        out_shape=[jax.ShapeDtypeStruct((lp, naw), F32), jax.ShapeDtypeStruct((nh, lp, 1), F32)],
        scratch_shapes=[pltpu.VMEM((lp, LANES), BF16)] * 3,
        compiler_params=_cp("parallel"), name=name)(proj, proj, proj, tb)


def _na_bwd(proj, tb, o, lse, do, *, n_tok, nh, name):
    lp = proj.shape[0]
    dh, hb = NA_HEAD_DIM, NA_HB
    naw = nh * dh
    rows = n_tok // GRID_W
    scale = dh ** -0.5
    kw = NA_WIN_H * GRID_W

    def body(q_ref, k_ref, v_ref, tb_ref, o_ref, lse_ref, do_ref, dq_ref, dk_ref, dv_ref, dtb_ref,
             q16_ref, k16_ref, v16_ref):
        dq_ref[...] = jnp.zeros_like(dq_ref)
        dk_ref[...] = jnp.zeros_like(dk_ref)
        dv_ref[...] = jnp.zeros_like(dv_ref)
        dtb_ref[...] = jnp.zeros_like(dtb_ref)
        q16_ref[...] = q_ref[...].astype(BF16)
        k16_ref[...] = k_ref[...].astype(BF16)
        v16_ref[...] = v_ref[...].astype(BF16)
        heads = _na_head_lanes()
        km = k16_ref[0:N_META, :]
        vm = v16_ref[0:N_META, :]
        qm = q16_ref[0:N_META, :]
        dom = do_ref[0:N_META, :]
        prod = dom * o_ref[0:N_META, :]
        dq_m = None
        dkm0 = jnp.zeros((N_META, LANES), F32)
        dvm0 = jnp.zeros((N_META, LANES), F32)
        for h in range(hb):
            q_h = _na_only(heads[h], qm)
            do_h = _na_only(heads[h], dom).astype(BF16)
            p = jnp.exp(_nt(q_h, km) * scale - lse_ref[h, 0:N_META, :])
            delta = jnp.sum(_na_only(heads[h], prod), axis=1, keepdims=True)
            ds = (p * (_nt(do_h, vm) - delta)).astype(BF16)
            dq_h = _nn(ds, km) * scale
            dq_m = dq_h if dq_m is None else jnp.where(heads[h], dq_h, dq_m)
            dkm0 = dkm0 + _tn(ds, q_h) * scale
            dvm0 = dvm0 + _tn(p.astype(BF16), do_h)
        dq_ref[0:N_META, :] = dq_m

        def step(r, carry):
            dkm, dvm = carry
            i0, q0, k0 = _na_rows(r, rows)
            q16 = q16_ref[pl.ds(q0, GRID_W), :]
            k16 = k16_ref[pl.ds(k0, kw), :]
            v16 = v16_ref[pl.ds(k0, kw), :]
            dov = do_ref[pl.ds(q0, GRID_W), :]
            prod = dov * o_ref[pl.ds(q0, GRID_W), :]
            dq = None
            dk = jnp.zeros((kw, LANES), F32)
            dv = jnp.zeros((kw, LANES), F32)
            for h in range(hb):
                q_h = _na_only(heads[h], q16)
                do_h = _na_only(heads[h], dov).astype(BF16)
                s, sm = _na_scores(q_h, k16, km, tb_ref.at[h], i0, scale)
                lse = lse_ref[h, pl.ds(q0, GRID_W), :]
                p = jnp.exp(s - lse)
                pm = jnp.exp(sm - lse)
                delta = jnp.sum(_na_only(heads[h], prod), axis=1, keepdims=True)
                ds = p * (_nt(do_h, v16) - delta)
                dsm = (pm * (_nt(do_h, vm) - delta)).astype(BF16)
                ds16 = ds.astype(BF16)
                dq_h = (_nn(ds16, k16) + _nn(dsm, km)) * scale
                dq = dq_h if dq is None else jnp.where(heads[h], dq_h, dq)
                dk = dk + _tn(ds16, q_h) * scale
                dv = dv + _tn(p.astype(BF16), do_h)
                dkm = dkm + _tn(dsm, q_h) * scale
                dvm = dvm + _tn(pm.astype(BF16), do_h)
                for j in range(NA_WIN_H):
                    dtb_ref[h, i0 + j] += ds[:, j * GRID_W:(j + 1) * GRID_W]
            dq_ref[pl.ds(q0, GRID_W), :] = dq
            dk_ref[pl.ds(k0, kw), :] += dk
            dv_ref[pl.ds(k0, kw), :] += dv
            return dkm, dvm

        dkm, dvm = lax.fori_loop(0, rows, step, (dkm0, dvm0))
        dk_ref[0:N_META, :] += dkm
        dv_ref[0:N_META, :] += dvm

    cblk = lambda col: pl.BlockSpec((lp, LANES), lambda g: (0, col // LANES + g))
    tbs = pl.BlockSpec((hb, 2 * NA_WIN_H - 1, GRID_W, GRID_W), lambda g: (g, 0, 0, 0))
    sds = jax.ShapeDtypeStruct((lp, naw), F32)
    return pl.pallas_call(
        body, grid=(nh // hb,),
        in_specs=[cblk(0), cblk(naw), cblk(2 * naw), tbs, cblk(0),
                  pl.BlockSpec((hb, lp, 1), lambda g: (g, 0, 0)), cblk(0)],
        out_specs=[cblk(0), cblk(0), cblk(0), tbs],
        out_shape=[sds, sds, sds, jax.ShapeDtypeStruct(tb.shape, F32)],
        scratch_shapes=[pltpu.VMEM((lp, LANES), BF16)] * 3,
        compiler_params=_cp("parallel"), name=name)(proj, proj, proj, tb, o, lse, do)


def _rpb_onehot():
    c = np.arange(GRID_W)[:, None]
    w = np.arange(GRID_W)[None, :]
    cs = np.clip(c - NA_WIN_W // 2, 0, GRID_W - NA_WIN_W)
    in_win = (w >= cs) & (w < cs + NA_WIN_W)
    dc = np.clip(w - c, -(NA_WIN_W - 1), NA_WIN_W - 1) + NA_WIN_W - 1
    oh = np.zeros((LANES, GRID_W * GRID_W), np.float32)
    flat = np.arange(GRID_W * GRID_W).reshape(GRID_W, GRID_W)
    oh[dc[in_win], flat[in_win]] = 1.0
    neg = np.where(in_win, 0.0, -1e30).astype(np.float32).reshape(1, -1)
    return oh, neg


def _assemble_dproj(dq_na, dk_na, dv_na, dq_f, dq_b, dz_f, dz_b, dv_f, dv_b, dg, dgn, dgh, *, name):
    lp, naw = dq_na.shape
    hgw = dq_f.shape[1]
    d = dgn.shape[1]
    cols = 3 * naw + 5 * hgw + 2 * d
    tr = _row_tile(lp)

    def body(nq_ref, nk_ref, nv_ref, qf_ref, qb_ref, zf_ref, zb_ref, vf_ref, vb_ref, g_ref, gn_ref,
             gh_ref, o_ref):
        o_ref[:, 0:naw] = nq_ref[...].astype(BF16)
        o_ref[:, naw:2 * naw] = nk_ref[...].astype(BF16)
        o_ref[:, 2 * naw:3 * naw] = nv_ref[...].astype(BF16)
        c0 = 3 * naw
        o_ref[:, c0:c0 + hgw] = (qf_ref[...] + qb_ref[...]).astype(BF16)
        o_ref[:, c0 + hgw:c0 + 2 * hgw] = zf_ref[...].astype(BF16)
        o_ref[:, c0 + 2 * hgw:c0 + 3 * hgw] = zb_ref[...].astype(BF16)
        o_ref[:, c0 + 3 * hgw:c0 + 4 * hgw] = (vf_ref[...] + vb_ref[...]).astype(BF16)
        o_ref[:, c0 + 4 * hgw:c0 + 5 * hgw] = g_ref[...]
        o_ref[:, c0 + 5 * hgw:c0 + 5 * hgw + d] = gn_ref[...]
        o_ref[:, c0 + 5 * hgw + d:] = gh_ref[...]

    hg, na = _rspec(tr, hgw), _rspec(tr, naw)
    return pl.pallas_call(
        body, grid=(lp // tr,),
        in_specs=[na, na, na, hg, hg, hg, hg, hg, hg, hg, _rspec(tr, d), _rspec(tr, d)],
        out_specs=_rspec(tr, cols),
        out_shape=jax.ShapeDtypeStruct((lp, cols), BF16),
        compiler_params=_cp("parallel"), name=name)(dq_na, dk_na, dv_na, dq_f, dq_b, dz_f, dz_b,
                                                    dv_f, dv_b, dg, dgn, dgh)


def _adamw(w, g, m, v, *, name):
    rows, cols = w.shape
    tr = 256 if rows % 256 == 0 else rows

    def body(w_ref, g_ref, m_ref, v_ref, d_ref, mo_ref, vo_ref):
        gv = g_ref[...]
        mn = ADAM_B1 * m_ref[...] + (1.0 - ADAM_B1) * gv
        vn = ADAM_B2 * v_ref[...] + (1.0 - ADAM_B2) * (gv * gv)
        m_hat = mn / (1.0 - ADAM_B1 ** ADAM_STEP)
        v_hat = vn / (1.0 - ADAM_B2 ** ADAM_STEP)
        d_ref[...] = -ADAM_LR * (m_hat / (jnp.sqrt(v_hat) + ADAM_EPS) + ADAM_WD * w_ref[...])
        mo_ref[...] = mn
        vo_ref[...] = vn

    spec = _rspec(tr, cols)
    sds = jax.ShapeDtypeStruct((rows, cols), F32)
    return pl.pallas_call(
        body, grid=(rows // tr,), in_specs=[spec] * 4, out_specs=[spec] * 3, out_shape=[sds] * 3,
        compiler_params=_cp("parallel"), name=name)(w, g, m, v)


def _local_step(x, tgt, meta, first_weight, rest_weights, g_mix, g_mlp, g_fin, hg_gain, rpb, lb,
                early_grads=None, late_grad=None):
    n_tok, d = x.shape
    hgw = hg_gain.shape[1]
    nh, hh = rpb.shape[0], hgw // HG_DK
    naw = nh * NA_HEAD_DIM
    l_real = N_META + n_tok
    lp = -(-l_real // ROW_ALIGN) * ROW_ALIGN
    n_chunks = l_real // HG_CHUNK
    pad = lp - l_real
    col_qhg = 3 * naw
    col_zf, col_zb, col_i, col_g = (col_qhg + hgw, col_qhg + 2 * hgw, col_qhg + 3 * hgw,
                                    col_qhg + 4 * hgw)
    col_gate = col_qhg + 5 * hgw

    zpad = jnp.zeros((pad, d), F32)
    h0 = jnp.concatenate([meta, x, zpad], axis=0)
    tgt_p = jnp.concatenate([jnp.zeros((N_META, d), F32), tgt, zpad], axis=0)

    oh_np, neg_np = _rpb_onehot()
    oh = jnp.asarray(oh_np)
    rpb_p = jnp.pad(rpb.reshape(nh * (2 * NA_WIN_H - 1), 2 * NA_WIN_W - 1),
                    ((0, 0), (0, LANES - (2 * NA_WIN_W - 1))))
    tb = _matmul(rpb_p, oh, tm=rpb_p.shape[0], tn=512, tk=LANES, precision=HIGHEST,
                 name="rpb_expand")
    tb = (tb + jnp.asarray(neg_np)).reshape(nh, 2 * NA_WIN_H - 1, GRID_W, GRID_W)

    a = _rmsnorm_fwd(h0, g_mix, name="norm_mix")
    w_in = first_weight(a)
    proj = _matmul(a, w_in, name="mm_in")
    o_na, lse = _na_fwd(proj, tb, n_tok=n_tok, nh=nh, name="na_fwd")
    lb_f = lb[0].reshape(hh, 1, HG_DK)
    lb_b = lb[1].reshape(hh, 1, HG_DK)
    scan_kw = dict(col_q=col_qhg, col_i=col_i, hh=hh)
    o_f, st_f = _hg_scan_fwd(proj, lb_f, reverse=False, col_z=col_zf, name="hg_scan_f", **scan_kw)
    o_b, st_b = _hg_scan_fwd(proj, lb_b, reverse=True, col_z=col_zb, name="hg_scan_b", **scan_kw)
    o_hg = _hg_out(o_f, o_b, proj, hg_gain, col_g=col_g, name="hg_out")
    w_na, w_hg, w_o, w_up, w_down = rest_weights(o_hg)
    y_na = _matmul(o_na, w_na, name="mm_na_out")
    gates = ((proj, col_gate), (proj, col_gate + d))

    def mix_gates(acc, gn, gh, yn):
        return acc, _sigmoid(gn) * yn + _sigmoid(gh) * acc

    def mix_gates_bwd(dmix, gn, gh, yn, yh):
        sn, sh = _sigmoid(gn), _sigmoid(gh)
        return dmix * sn, dmix * sh, dmix * yn * sn * (1.0 - sn), dmix * yh * sh * (1.0 - sh)

    y_hg, mix = _matmul(o_hg, w_hg, name="mm_hg_out", epilogue=mix_gates,
                        tiles=(*gates, (y_na, 0)), out_dtypes=(F32, BF16))
    t1 = _matmul(mix, w_o, name="mm_o")
    h1, mlp_in = _residual_norm(h0, t1, g_mlp, name="resid_norm_mlp")
    u, act = _matmul(mlp_in, w_up, name="mm_up", out_dtypes=(F32, BF16),
                     epilogue=lambda acc: (acc, jnp.square(jnp.maximum(acc, 0.0))))
    t2 = _matmul(act, w_down, name="mm_down")
    dh2, loss, dg_fin = _final_loss(h1, t2, g_fin, tgt_p, n_tok=n_tok, name="final_loss")

    (du,) = _matmul(dh2, w_down, tb=True, name="mm_down_dx", tiles=((u, 0),), out_dtypes=(BF16,),
                    epilogue=lambda acc, uv: (acc * 2.0 * jnp.maximum(uv, 0.0),))
    dw_down = _matmul(act, dh2, ta=True, name="mm_down_dw")
    dm = _matmul(du, w_up, tb=True, name="mm_up_dx")
    dw_up = _matmul(mlp_in, du, ta=True, name="mm_up_dw")
    dh1, dg_mlp = _rmsnorm_bwd_add(h1, g_mlp, dm, dh2, name="norm_mlp_bwd")
    dy_na, dy_hg, dgn, dgh = _matmul(dh1, w_o, tb=True, name="mm_o_dx", epilogue=mix_gates_bwd,
                                     tiles=(*gates, (y_na, 0), (y_hg, 0)), out_dtypes=(BF16,) * 4)
    dw_o = _matmul(mix, dh1, ta=True, name="mm_o_dw")
    do_na = _matmul(dy_na, w_na, tb=True, name="mm_na_out_dx")
    dw_na = _matmul(o_na, dy_na, ta=True, name="mm_na_out_dw")
    do_hg = _matmul(dy_hg, w_hg, tb=True, name="mm_hg_out_dx")
    dw_hg = _matmul(o_hg, dy_hg, ta=True, name="mm_hg_out_dw")
    token = early_grads([dw_na, dw_hg, dw_o, dw_up, dw_down]) if early_grads else None
    if token is not None:
        hg_gain = hg_gain + token[0:1, 0:1]
    d_o, dg_hg, d_gain = _hg_out_bwd(o_f, o_b, proj, hg_gain, do_hg, col_g=col_g, name="hg_out_bwd")
    dq_f, dz_f, dv_f, dlb_f = _hg_scan_bwd(proj, lb_f, st_f, d_o, reverse=False, col_z=col_zf,
                                           name="hg_scan_f_bwd", **scan_kw)
    dq_b, dz_b, dv_b, dlb_b = _hg_scan_bwd(proj, lb_b, st_b, d_o, reverse=True, col_z=col_zb,
                                           name="hg_scan_b_bwd", **scan_kw)
    dq_na, dk_na, dv_na, dtb = _na_bwd(proj, tb, o_na, lse, do_na, n_tok=n_tok, nh=nh, name="na_bwd")
    dproj = _assemble_dproj(dq_na, dk_na, dv_na, dq_f, dq_b, dz_f, dz_b, dv_f, dv_b, dg_hg, dgn,
                            dgh, name="assemble_dproj")
    dw_in = _matmul(a, dproj, ta=True, name="mm_in_dw")
    token = late_grad(dw_in) if late_grad else None
    da = _matmul(dproj, w_in, tb=True, name="mm_in_dx", after=token)
    dh0, dg_mix = _rmsnorm_bwd_add(h0, g_mix, da, dh1, name="norm_mix_bwd")
    d_rpb = _matmul(dtb.reshape(nh * (2 * NA_WIN_H - 1), GRID_W * GRID_W), oh, tb=True,
                    tm=nh * (2 * NA_WIN_H - 1), tn=LANES, tk=1024, precision=HIGHEST,
                    name="rpb_reduce")
    d_lb = jnp.concatenate([dlb_f.reshape(1, hgw), dlb_b.reshape(1, hgw)], axis=0)
    return (loss, dh0[N_META:l_real], dh0[:N_META], dw_in, dw_na, dw_hg, dw_o, dw_up, dw_down,
            dg_mix, dg_mlp, dg_fin, d_gain, d_rpb, d_lb)


N_CHIPS = 4
N_DEV = 8
ANY = pl.BlockSpec(memory_space=pl.ANY)


def _place():
    x, y, c = lax.axis_index("x"), lax.axis_index("y"), lax.axis_index("c")
    others = []
    for j in (1, 2, 3):
        tx = (1 - x) if (j >> 1) else x
        ty = (1 - y) if (j & 1) else y
        others.append((tx, ty))
    return x, y, c, others


def _piece(ref, axis, k, half, rh, cs):
    if axis == 1:
        return ref.at[pl.ds(pl.multiple_of(half * rh, 16), rh), pl.ds(pl.multiple_of(k * cs, LANES), cs)]
    return ref.at[pl.ds(pl.multiple_of(k * 2 * rh + half * rh, 16), rh), :]


def _cast_into_full(shard, axis, place, *, name):
    r, cs = shard.shape
    full = (r, cs * N_CHIPS) if axis == 1 else (r * N_CHIPS, cs)
    tr = next(t for t in (256, 128, 64, 32, 16) if r % t == 0)
    nt = r // tr

    def body(p_ref, s_ref, o_ref):
        o_ref[...] = s_ref[...].astype(BF16)

    if axis == 1:
        omap = lambda i, p_ref: (i, p_ref[0])
    else:
        omap = lambda i, p_ref: (p_ref[0] * nt + i, 0)
    return pl.pallas_call(
        body,
        grid_spec=pltpu.PrefetchScalarGridSpec(
            num_scalar_prefetch=1, grid=(nt,),
            in_specs=[pl.BlockSpec((tr, cs), lambda i, p_ref: (i, 0))],
            out_specs=pl.BlockSpec((tr, cs), omap)),
        out_shape=jax.ShapeDtypeStruct(full, BF16),
        compiler_params=_cp("parallel"), name=name)(place, shard)


HBM_SPEC = pl.BlockSpec(memory_space=pltpu.HBM)
SEM_SPEC = pl.BlockSpec(memory_space=pltpu.SEMAPHORE)
SPLIT_COPY = pltpu.CompilerParams(has_side_effects=pltpu.SideEffectType.DATAFLOW_SIDE_EFFECTING)
TOKEN = jax.ShapeDtypeStruct((8, LANES), F32)


def _geo(fulls, axes):
    out = []
    for f, ax in zip(fulls, axes):
        r, cs = (f.shape[0], f.shape[1] // N_CHIPS) if ax == 1 else (f.shape[0] // N_CHIPS, f.shape[1])
        out.append((ax, r // 2, cs))
    return out


def _gather_copies(refs, geo, send_sems, recv_sems):
    x, y, c, others = _place()
    chip = 2 * x + y
    cps = []
    for i, (ax, rh, cs) in enumerate(geo):
        mine = _piece(refs[i], ax, chip, c, rh, cs)
        for j, (tx, ty) in enumerate(others):
            cps.append(pltpu.make_async_remote_copy(
                src_ref=mine, dst_ref=mine, send_sem=send_sems.at[3 * i + j],
                recv_sem=recv_sems.at[3 * i + j], device_id=(tx, ty, c), device_id_type=MESH))
    return cps


def _allgather_start(fulls, axes, after, *, name):
    n = len(fulls)
    geo = _geo(fulls, axes)

    def body(*refs):
        w_refs = refs[:n]
        send_sems, recv_sems = refs[n + 1], refs[n + 2]
        token = refs[2 * n + 3]
        for cp in _gather_copies(w_refs, geo, send_sems, recv_sems):
            cp.start()
        token[...] = jnp.zeros_like(token)

    out = pl.pallas_call(
        body, name=name,
        out_shape=(pltpu.SemaphoreType.DMA((3 * n,)), pltpu.SemaphoreType.DMA((3 * n,)),
                   *[pltpu.HBM(f.shape, f.dtype) for f in fulls], TOKEN),
        in_specs=[HBM_SPEC] * n + [ANY],
        out_specs=(SEM_SPEC, SEM_SPEC, *[HBM_SPEC] * n, pl.BlockSpec(memory_space=pltpu.VMEM)),
        input_output_aliases={i: 2 + i for i in range(n)},
        compiler_params=SPLIT_COPY,
    )(*[pltpu.with_memory_space_constraint(f, pltpu.HBM) for f in fulls], after)
    return out[0], out[1], list(out[2:2 + n]), out[2 + n]


def _allgather_wait(send_sems, recv_sems, fulls, axes, after, *, name):
    n = len(fulls)
    geo = _geo(fulls, axes)

    def body(*refs):
        w_refs = refs[:n]
        for cp in _gather_copies(w_refs, geo, refs[n], refs[n + 1]):
            cp.wait_send()
            cp.wait_recv()

    return list(pl.pallas_call(
        body, name=name,
        out_shape=[pltpu.HBM(f.shape, f.dtype) for f in fulls],
        in_specs=[HBM_SPEC] * n + [SEM_SPEC, SEM_SPEC, ANY],
        out_specs=[HBM_SPEC] * n,
        input_output_aliases={i: i for i in range(n)},
        compiler_params=SPLIT_COPY,
    )(*fulls, send_sems, recv_sems, after))


def _allgather_forward(fulls, axes, *, name):
    n = len(fulls)
    geo = _geo(fulls, axes)

    def body(*refs):
        o_refs = refs[n:2 * n]
        send_sems, recv_sems = refs[2 * n:]
        x, y, c, others = _place()

        def rcopy(i, j, half, to):
            ax, rh, cs = geo[i]
            ref = _piece(o_refs[i], ax, 2 * others[j][0] + others[j][1], half, rh, cs)
            return pltpu.make_async_remote_copy(
                src_ref=ref, dst_ref=ref, send_sem=send_sems.at[3 * i + j],
                recv_sem=recv_sems.at[3 * i + j], device_id=to, device_id_type=MESH)

        cps = [rcopy(i, j, c, (x, y, 1 - c)) for i in range(n) for j in range(3)]
        for cp in cps:
            cp.start()
        for i in range(n):
            for j in range(3):
                rcopy(i, j, 1 - c, (x, y, c)).wait_recv()
        for cp in cps:
            cp.wait_send()

    return list(pl.pallas_call(
        body, in_specs=[ANY] * n, out_specs=[ANY] * n,
        out_shape=[jax.ShapeDtypeStruct(f.shape, f.dtype) for f in fulls],
        input_output_aliases={i: i for i in range(n)},
        scratch_shapes=[pltpu.SemaphoreType.DMA((3 * n,)), pltpu.SemaphoreType.DMA((3 * n,))],
        name=name)(*fulls))


def _scatter_geo(parts, axes):
    out = []
    for p, ax in zip(parts, axes):
        _, rh, cols = p.shape
        out.append((ax, rh, cols // N_CHIPS if ax == 1 else cols))
    return out


def _scatter_copies(p_refs, q_refs, geo, send_sems, recv_sems):
    x, y, c, others = _place()
    chip = 2 * x + y
    cps = []
    for i, (ax, rh, cw) in enumerate(geo):
        for j, (tx, ty) in enumerate(others):
            k = 2 * tx + ty
            src = (p_refs[i].at[0, :, pl.ds(pl.multiple_of(k * cw, LANES), cw)] if ax == 1
                   else p_refs[i].at[k])
            cps.append(pltpu.make_async_remote_copy(
                src_ref=src, dst_ref=q_refs[i].at[chip], send_sem=send_sems.at[3 * i + j],
                recv_sem=recv_sems.at[3 * i + j], device_id=(tx, ty, c), device_id_type=MESH))
    return cps


def _scatter_start(parts, axes, *, name):
    n = len(parts)
    geo = _scatter_geo(parts, axes)
    slots = [pltpu.HBM((N_CHIPS, rh, cw), p.dtype) for p, (_, rh, cw) in zip(parts, geo)]

    def body(*refs):
        p_refs, q_refs = refs[:n], refs[n:2 * n]
        send_sems, recv_sems = refs[2 * n], refs[2 * n + 1]
        token = refs[4 * n + 2]
        for cp in _scatter_copies(p_refs, q_refs, geo, send_sems, recv_sems):
            cp.start()
        token[...] = jnp.zeros_like(token)

    land = [pltpu.with_memory_space_constraint(lax.empty(s.inner_aval.shape, s.inner_aval.dtype), pltpu.HBM)
            for s in slots]
    out = pl.pallas_call(
        body, name=name,
        out_shape=(pltpu.SemaphoreType.DMA((3 * n,)), pltpu.SemaphoreType.DMA((3 * n,)),
                   *[pltpu.HBM(p.shape, p.dtype) for p in parts], *slots, TOKEN),
        in_specs=[HBM_SPEC] * (2 * n),
        out_specs=(SEM_SPEC, SEM_SPEC, *[HBM_SPEC] * (2 * n), pl.BlockSpec(memory_space=pltpu.VMEM)),
        input_output_aliases={i: 2 + i for i in range(2 * n)},
        compiler_params=SPLIT_COPY,
    )(*[pltpu.with_memory_space_constraint(p, pltpu.HBM) for p in parts], *land)
    return out[0], out[1], list(out[2:2 + n]), list(out[2 + n:2 + 2 * n]), out[2 + 2 * n]


def _scatter_wait(send_sems, recv_sems, parts, slots, axes, after, *, name):
    n = len(parts)
    geo = _scatter_geo(parts, axes)

    def body(*refs):
        p_refs, q_refs = refs[:n], refs[n:2 * n]
        for cp in _scatter_copies(p_refs, q_refs, geo, refs[2 * n], refs[2 * n + 1]):
            cp.wait_send()
            cp.wait_recv()

    out = pl.pallas_call(
        body, name=name,
        out_shape=[pltpu.HBM(a.shape, a.dtype) for a in (*parts, *slots)],
        in_specs=[HBM_SPEC] * (2 * n) + [SEM_SPEC, SEM_SPEC, ANY],
        out_specs=[HBM_SPEC] * (2 * n),
        input_output_aliases={i: i for i in range(2 * n)},
        compiler_params=SPLIT_COPY,
    )(*parts, *slots, send_sems, recv_sems, after)
    return list(out[:n]), list(out[n:])


def _sibling_swap(grads, *, name):
    n = len(grads)
    out_shape = [jax.ShapeDtypeStruct((g.shape[0], g.shape[1] // 2, g.shape[2]), g.dtype)
                 for g in grads]

    def body(*refs):
        g_refs, o_refs = refs[:n], refs[n:2 * n]
        send_sems, recv_sems = refs[2 * n:]
        x, y, c, _ = _place()
        cps = []
        for i in range(n):
            rh = grads[i].shape[1] // 2
            src = g_refs[i].at[:, pl.ds(pl.multiple_of((1 - c) * rh, 16), rh), :]
            cp = pltpu.make_async_remote_copy(
                src_ref=src, dst_ref=o_refs[i], send_sem=send_sems.at[i], recv_sem=recv_sems.at[i],
                device_id=(x, y, 1 - c), device_id_type=MESH)
            cp.start()
            cps.append(cp)
        for cp in cps:
            cp.wait()

    return pl.pallas_call(
        body, in_specs=[ANY] * n, out_specs=[ANY] * n, out_shape=out_shape,
        scratch_shapes=[pltpu.SemaphoreType.DMA((n,)), pltpu.SemaphoreType.DMA((n,))],
        name=name)(*grads)


def _pair_add(g3, rx, c_arr, *, out_dtype, name):
    nb, rows, cols = g3.shape
    rh = rows // 2
    tr = next(t for t in (128, 64, 32, 16) if rh % t == 0)
    nt = rh // tr

    def body(c_ref, g_ref, r_ref, o_ref):
        o_ref[...] = (g_ref[...] + r_ref[...]).astype(out_dtype)

    return pl.pallas_call(
        body,
        grid_spec=pltpu.PrefetchScalarGridSpec(
            num_scalar_prefetch=1, grid=(nb, nt),
            in_specs=[pl.BlockSpec((None, tr, cols), lambda b, i, c_ref: (b, c_ref[0] * nt + i, 0)),
                      pl.BlockSpec((None, tr, cols), lambda b, i, c_ref: (b, i, 0))],
            out_specs=pl.BlockSpec((None, tr, cols), lambda b, i, c_ref: (b, i, 0))),
        out_shape=jax.ShapeDtypeStruct((nb, rh, cols), out_dtype),
        compiler_params=_cp("parallel", "parallel"), name=name)(c_arr, g3, rx)


def _sum_slots(q, *, name):
    ns, rows, cols = q.shape
    tr = next(t for t in (128, 64, 32, 16, 8) if rows % t == 0)

    def body(q_ref, o_ref):
        acc = q_ref[0].astype(F32)
        for k in range(1, ns):
            acc = acc + q_ref[k].astype(F32)
        o_ref[...] = acc

    return pl.pallas_call(
        body, grid=(rows // tr,),
        in_specs=[pl.BlockSpec((ns, tr, cols), lambda i: (0, i, 0))],
        out_specs=_rspec(tr, cols),
        out_shape=jax.ShapeDtypeStruct((rows, cols), F32),
        compiler_params=_cp("parallel"), name=name)(q)


def _sum_chips(q, p, place, axis, *, name):
    _, rh, cw = q.shape
    tr = next(t for t in (128, 64, 32, 16) if rh % t == 0)
    nt = rh // tr

    def body(p_ref, *refs):
        q_refs, own_ref, o_ref = refs[:N_CHIPS], refs[N_CHIPS], refs[N_CHIPS + 1]
        chip = p_ref[0]
        acc = jnp.where(chip == 0, own_ref[...], q_refs[0][...]).astype(F32)
        for k in range(1, N_CHIPS):
            acc = acc + jnp.where(chip == k, own_ref[...], q_refs[k][...]).astype(F32)
        o_ref[...] = acc

    def slot_spec(k):
        return pl.BlockSpec((None, tr, cw),
                            lambda i, p_ref: (jnp.where(p_ref[0] == k, (k + 1) % N_CHIPS, k), i, 0))

    if axis == 1:
        own_spec = pl.BlockSpec((None, tr, cw), lambda i, p_ref: (0, i, p_ref[0]))
    else:
        own_spec = pl.BlockSpec((None, tr, cw), lambda i, p_ref: (p_ref[0], i, 0))
    return pl.pallas_call(
        body,
        grid_spec=pltpu.PrefetchScalarGridSpec(
            num_scalar_prefetch=1, grid=(nt,),
            in_specs=[slot_spec(k) for k in range(N_CHIPS)] + [own_spec],
            out_specs=pl.BlockSpec((tr, cw), lambda i, p_ref: (p_ref[1] * nt + i, 0))),
        out_shape=jax.ShapeDtypeStruct((2 * rh, cw), F32),
        compiler_params=_cp("parallel"), name=name)(place, *([q] * N_CHIPS), p)


def _sibling_share(shards, *, name):
    n = len(shards)

    def body(*refs):
        o_refs = refs[n:2 * n]
        send_sems, recv_sems = refs[2 * n:]
        x, y, c, _ = _place()
        cps = []
        for i in range(n):
            rh = shards[i].shape[0] // 2
            mine = o_refs[i].at[pl.ds(pl.multiple_of(c * rh, 8), rh), :]
            cp = pltpu.make_async_remote_copy(
                src_ref=mine, dst_ref=mine, send_sem=send_sems.at[i], recv_sem=recv_sems.at[i],
                device_id=(x, y, 1 - c), device_id_type=MESH)
            cp.start()
            cps.append(cp)
        for i in range(n):
            rh = shards[i].shape[0] // 2
            theirs = o_refs[i].at[pl.ds(pl.multiple_of((1 - c) * rh, 8), rh), :]
            pltpu.make_async_remote_copy(
                src_ref=theirs, dst_ref=theirs, send_sem=send_sems.at[i], recv_sem=recv_sems.at[i],
                device_id=(x, y, c), device_id_type=MESH).wait_recv()
        for cp in cps:
            cp.wait_send()

    return pl.pallas_call(
        body, in_specs=[ANY] * n, out_specs=[ANY] * n,
        out_shape=[jax.ShapeDtypeStruct(h.shape, h.dtype) for h in shards],
        input_output_aliases={i: i for i in range(n)},
        scratch_shapes=[pltpu.SemaphoreType.DMA((n,)), pltpu.SemaphoreType.DMA((n,))],
        name=name)(*shards)


def _gather_all(blk, *, name, after=None):
    rows, cols = blk.shape
    extra = [] if after is None else [after]

    def body(x_ref, *refs):
        out_ref, send_sems, recv_sems, local_sem = refs[len(extra):]
        x, y, c = lax.axis_index("x"), lax.axis_index("y"), lax.axis_index("c")
        me = 4 * x + 2 * y + c
        mine = pltpu.make_async_copy(x_ref, out_ref.at[me], local_sem)
        mine.start()
        cps = []
        for k in range(1, N_DEV):
            tx = (1 - x) if (k >> 2) & 1 else x
            ty = (1 - y) if (k >> 1) & 1 else y
            tc = (1 - c) if k & 1 else c
            cp = pltpu.make_async_remote_copy(
                src_ref=x_ref, dst_ref=out_ref.at[me], send_sem=send_sems.at[k - 1],
                recv_sem=recv_sems.at[k - 1], device_id=(tx, ty, tc), device_id_type=MESH)
            cp.start()
            cps.append(cp)
        for k in range(1, N_DEV):
            tx = (1 - x) if (k >> 2) & 1 else x
            ty = (1 - y) if (k >> 1) & 1 else y
            tc = (1 - c) if k & 1 else c
            got = out_ref.at[4 * tx + 2 * ty + tc]
            pltpu.make_async_remote_copy(
                src_ref=got, dst_ref=got, send_sem=send_sems.at[k - 1], recv_sem=recv_sems.at[k - 1],
                device_id=(x, y, c), device_id_type=MESH).wait_recv()
        for cp in cps:
            cp.wait_send()
        mine.wait()

    vm = pl.BlockSpec(memory_space=pltpu.VMEM)
    return pl.pallas_call(
        body, in_specs=[vm] + [ANY] * len(extra), out_specs=vm,
        out_shape=jax.ShapeDtypeStruct((N_DEV, rows, cols), blk.dtype),
        scratch_shapes=[pltpu.SemaphoreType.DMA((N_DEV - 1,)), pltpu.SemaphoreType.DMA((N_DEV - 1,)),
                        pltpu.SemaphoreType.DMA],
        name=name)(blk, *extra)


def _as_rows(a):
    flat = a.reshape(-1)
    n = flat.shape[0]
    rows = -(-n // (8 * LANES)) * 8
    return jnp.pad(flat, (0, rows * LANES - n)).reshape(rows, LANES)


def _from_rows(p, shape):
    n = int(np.prod(shape))
    return p.reshape(-1)[:n].reshape(shape)


WEIGHT_AXES = (1, 1, 1, 0, 1, 0)
WIRE = BF16


def kernel(x, meta_tokens, w_in, w_na_out, w_hg_out, w_o, w_up, w_down, norm_mix, norm_mlp, norm_final, hg_norm, na_rpb, hg_lb_logits, loss_target, m_meta_tokens, m_w_in, m_w_na_out, m_w_hg_out, m_w_o, m_w_up, m_w_down, m_norm_mix, m_norm_mlp, m_norm_final, m_hg_norm, m_na_rpb, m_hg_lb_logits, v_meta_tokens, v_w_in, v_w_na_out, v_w_hg_out, v_w_o, v_w_up, v_w_down, v_norm_mix, v_norm_mlp, v_norm_final, v_hg_norm, v_na_rpb, v_hg_lb_logits):
    xi, yi, ci = lax.axis_index("x"), lax.axis_index("y"), lax.axis_index("c")
    chip = 2 * xi + yi
    d = x.shape[-1]
    dshard = meta_tokens.shape[1]
    hgw = hg_norm.shape[1]
    lbs = hg_lb_logits.shape[2]
    big = [w_in[0], w_na_out[0], w_hg_out[0], w_o[0], w_up[0], w_down[0]]
    big_m = [m_w_in[0], m_w_na_out[0], m_w_hg_out[0], m_w_o[0], m_w_up[0], m_w_down[0]]
    big_v = [v_w_in[0], v_w_na_out[0], v_w_hg_out[0], v_w_o[0], v_w_up[0], v_w_down[0]]

    place = jnp.stack([chip, ci]).astype(jnp.int32)
    own_w = [_cast_into_full(w, ax, place, name=f"cast_shard_{i}")
             for i, (w, ax) in enumerate(zip(big, WEIGHT_AXES))]
    in_axes, rest_axes = WEIGHT_AXES[:1], WEIGHT_AXES[1:]
    small_in = jnp.concatenate([_as_rows(meta_tokens), _as_rows(hg_lb_logits)], axis=0)
    small_all = _gather_all(small_in, name="gather_small_params")[0::2]
    in_send, in_recv, in_bufs, in_token = _allgather_start(own_w[:1], in_axes, small_all,
                                                           name="weight_allgather_in_start")
    ag_send, ag_recv, ag_bufs, ag_token = _allgather_start(own_w[1:], rest_axes, in_token,
                                                           name="weight_allgather_rest_start")

    def first_weight(after):
        got = _allgather_wait(in_send, in_recv, in_bufs, in_axes, after,
                              name="weight_allgather_in_wait")
        return _allgather_forward(got, in_axes, name="weight_allgather_in_forward")[0]

    def rest_weights(after):
        got = _allgather_wait(ag_send, ag_recv, ag_bufs, rest_axes, after,
                              name="weight_allgather_rest_wait")
        return _allgather_forward(got, rest_axes, name="weight_allgather_rest_forward")

    n_meta_rows = N_META * dshard // LANES
    meta_full = (small_all[:, :n_meta_rows].reshape(N_CHIPS, N_META, dshard)
                 .transpose(1, 0, 2).reshape(N_META, d))
    lbl_full = (small_all[:, n_meta_rows:].reshape(N_CHIPS, -1)[:, :4 * lbs]
                .reshape(N_CHIPS, 2, 2, lbs).transpose(1, 2, 0, 3).reshape(2, 2, N_CHIPS * lbs))
    lb = jax.nn.softmax(lbl_full, axis=1)[:, 0]

    c_arr = ci.reshape(1).astype(jnp.int32)

    def chip_partials(dws, axes, tag):
        g3 = [g.reshape(1, *g.shape) if ax == 1
              else g.reshape(N_CHIPS, g.shape[0] // N_CHIPS, g.shape[1]) for g, ax in zip(dws, axes)]
        rx = _sibling_swap(g3, name=f"grad_sibling_swap_{tag}")
        return [_pair_add(g, r, c_arr, out_dtype=WIRE, name=f"grad_pair_add_{tag}_{i}")
                for i, (g, r) in enumerate(zip(g3, rx))]

    flying = {}

    def scatter_behind(tag, axes):
        def hook(dws):
            send, recv, parts, slots, token = _scatter_start(
                chip_partials(dws, axes, tag), axes, name=f"grad_scatter_{tag}_start")
            flying[tag] = (send, recv, parts, slots)
            return token
        return hook

    def landed(tag, axes, after):
        parts, slots = _scatter_wait(*flying[tag], axes, after, name=f"grad_scatter_{tag}_wait")
        return [_sum_chips(q, p, place, ax, name=f"grad_sum_chips_{tag}_{i}")
                for i, (q, p, ax) in enumerate(zip(slots, parts, axes))]

    (loss, dx, dmeta, *_, dg_mix, dg_mlp, dg_fin, d_gain, d_rpb, d_lb) = _local_step(
        x[0], loss_target[0], meta_full, first_weight, rest_weights,
        norm_mix + ag_token[0:1, 0:1], norm_mlp, norm_final.reshape(1, d), hg_norm, na_rpb[0], lb,
        scatter_behind("rest", rest_axes), lambda dw_in: scatter_behind("in", in_axes)([dw_in]))

    halves_rest = landed("rest", rest_axes, dx)
    halves_in = landed("in", in_axes, halves_rest[-1])
    g_big = _sibling_share(halves_in + halves_rest, name="grad_sibling_share")

    d_rpb_c = d_rpb[:, :2 * NA_WIN_W - 1]
    small_g = [dmeta, dg_mix, dg_mlp, dg_fin, d_gain, d_rpb_c, d_lb, loss]
    packed = jnp.concatenate([_as_rows(a) for a in small_g], axis=0)
    total = _sum_slots(_gather_all(packed, after=halves_in[0], name="gather_small_grads"),
                       name="sum_small_grads")
    offs = np.cumsum([0] + [_as_rows(a).shape[0] for a in small_g])
    take = lambda i, shape: _from_rows(total[offs[i]:offs[i + 1]], shape)
    g_meta_full = take(0, (N_META, d))
    g_norm_mix, g_norm_mlp = take(1, (1, d)), take(2, (1, d))
    g_norm_final = take(3, (d,))
    g_hg_norm = take(4, (1, hgw))
    g_rpb = take(5, na_rpb.shape)
    g_lb = take(6, (2, hgw))
    loss_total = take(7, (1, LANES))[0, 0]
    g_meta = lax.dynamic_slice_in_dim(g_meta_full, chip * dshard, dshard, axis=1)
    dl0 = lb * (1.0 - lb) * g_lb
    g_lbl_full = jnp.stack([dl0, -dl0], axis=1)
    g_lbl = lax.dynamic_slice_in_dim(g_lbl_full, chip * lbs, lbs, axis=2)

    big_out = [_adamw(w, g, m, v, name=f"adamw_{i}")
               for i, (w, g, m, v) in enumerate(zip(big, g_big, big_m, big_v))]
    small_w = [meta_tokens, norm_mix, norm_mlp, norm_final, hg_norm, na_rpb, hg_lb_logits]
    small_gr = [g_meta, g_norm_mix, g_norm_mlp, g_norm_final, g_hg_norm, g_rpb, g_lbl]
    small_m = [m_meta_tokens, m_norm_mix, m_norm_mlp, m_norm_final, m_hg_norm, m_na_rpb, m_hg_lb_logits]
    small_v = [v_meta_tokens, v_norm_mix, v_norm_mlp, v_norm_final, v_hg_norm, v_na_rpb, v_hg_lb_logits]
    pk = lambda lst: jnp.concatenate([_as_rows(a) for a in lst], axis=0)
    sd, sm, sv = _adamw(pk(small_w), pk(small_gr), pk(small_m), pk(small_v), name="adamw_small")
    soffs = np.cumsum([0] + [_as_rows(a).shape[0] for a in small_w])
    unpk = lambda p: [_from_rows(p[soffs[i]:soffs[i + 1]], small_w[i].shape) for i in range(len(small_w))]
    sd, sm, sv = unpk(sd), unpk(sm), unpk(sv)

    def order(bigs, smalls):
        return [smalls[0]] + [b.reshape(1, *b.shape) for b in bigs] + smalls[1:]

    grads = order(g_big, small_gr)
    deltas = order([o[0] for o in big_out], sd)
    new_m = order([o[1] for o in big_out], sm)
    new_v = order([o[2] for o in big_out], sv)
    return (loss_total, dx.reshape(1, *dx.shape), *grads, *deltas, *new_m, *new_v)
```

```python
import functools

import numpy as np
import jax
import jax.numpy as jnp
from jax import lax
from jax.experimental import pallas as pl
from jax.experimental.pallas import tpu as pltpu

F32 = jnp.float32
BF16 = jnp.bfloat16
HIGHEST = lax.Precision.HIGHEST

GRID_W = 64
N_META = 16
EPS = 1e-6
NA_HEAD_DIM = 64
NA_WIN_H = 8
NA_WIN_W = 16
HG_DK = 128
HG_CHUNK = 16
LANES = 128
ROW_ALIGN = 128
VMEM_LIMIT = 48 * 1024 * 1024

ADAM_LR = 0.001
ADAM_B1 = 0.9
ADAM_B2 = 0.999
ADAM_EPS = 1e-08
ADAM_WD = 0.01
ADAM_STEP = 10

MESH = pl.DeviceIdType.MESH


def _cp(*sem):
    return pltpu.CompilerParams(dimension_semantics=sem, vmem_limit_bytes=VMEM_LIMIT)


def _sigmoid(x):
    return 1.0 / (1.0 + jnp.exp(-x))


def _dot(a, b, dims, precision=None):
    return lax.dot_general(a, b, (dims, ((), ())), preferred_element_type=F32, precision=precision)


def _nn(a, b, **kw):
    return _dot(a, b, ((1,), (0,)), **kw)


def _nt(a, b, **kw):
    return _dot(a, b, ((1,), (1,)), **kw)


def _tn(a, b, **kw):
    return _dot(a, b, ((0,), (0,)), **kw)


def _matmul(a, b, *, ta=False, tb=False, tm=None, tn=None, tk=None, out_dtype=F32, name,
            precision=None, after=None, epilogue=None, tiles=(), out_dtypes=None):
    extra = [] if after is None else [after]
    single = out_dtypes is None
    if single:
        out_dtypes = (out_dtype,)
    n_t, n_o = len(tiles), len(out_dtypes)
    if ta:
        kdim, m = a.shape
    else:
        m, kdim = a.shape
    if tb:
        n, k2 = b.shape
    else:
        k2, n = b.shape
    assert kdim == k2, (a.shape, b.shape, ta, tb)
    if tm is None:
        if ta:
            tm = next(t for t in (1024, 512, 256, 128, m) if m % t == 0)
        else:
            tm = m // 2 if (m // 2) % 16 == 0 and m > 512 else m
    if tn is None:
        wide = (1024,) if not ta and len(tiles) <= 1 else ()
        tn = next(t for t in (*wide, 512, 256, 128, n) if n % t == 0)
    if tk is None:
        tk = kdim if ta else next(t for t in (1024, 512, 256, 128, kdim) if kdim % t == 0)
    assert m % tm == 0 and n % tn == 0 and kdim % tk == 0, (m, n, kdim, tm, tn, tk)
    nk = kdim // tk
    op_dtype = F32 if precision is not None else BF16

    def body(a_ref, b_ref, *refs):
        t_refs = refs[:n_t]
        o_refs = refs[n_t + len(extra):n_t + len(extra) + n_o]
        av = a_ref[...].astype(op_dtype)
        bv = b_ref[...].astype(op_dtype)
        dims = ((0 if ta else 1,), (1 if tb else 0,))
        part = _dot(av, bv, dims, precision=precision)

        def finish(acc):
            outs = (acc,) if epilogue is None else epilogue(acc, *[t[...] for t in t_refs])
            for o_ref, val in zip(o_refs, outs):
                o_ref[...] = val.astype(o_ref.dtype)

        if nk == 1:
            finish(part)
            return
        acc_ref = refs[-1]
        kk = pl.program_id(2)

        @pl.when(kk == 0)
        def _():
            acc_ref[...] = part

        @pl.when((kk > 0) & (kk < nk - 1))
        def _():
            acc_ref[...] += part

        @pl.when(kk == nk - 1)
        def _():
            finish(acc_ref[...] + part)

    a_spec = (pl.BlockSpec((tk, tm), lambda i, j, k: (k, i)) if ta
              else pl.BlockSpec((tm, tk), lambda i, j, k: (i, k)))
    b_spec = (pl.BlockSpec((tn, tk), lambda i, j, k: (j, k)) if tb
              else pl.BlockSpec((tk, tn), lambda i, j, k: (k, j)))
    for _, off in tiles:
        assert off % tn == 0, (off, tn)
    t_specs = [pl.BlockSpec((tm, tn), functools.partial(lambda i, j, k, o: (i, o + j), o=off // tn))
               for _, off in tiles]
    o_spec = pl.BlockSpec((tm, tn), lambda i, j, k: (i, j))
    outs = pl.pallas_call(
        body,
        grid=(m // tm, n // tn, nk),
        in_specs=[a_spec, b_spec] + t_specs + [pl.BlockSpec(memory_space=pl.ANY)] * len(extra),
        out_specs=[o_spec] * n_o,
        out_shape=[jax.ShapeDtypeStruct((m, n), dt) for dt in out_dtypes],
        scratch_shapes=[pltpu.VMEM((tm, tn), F32)] if nk > 1 else [],
        compiler_params=_cp("parallel", "parallel", "arbitrary"),
        name=name,
    )(a, b, *[t for t, _ in tiles], *extra)
    return outs[0] if single else outs


def _rspec(tr, w, cb=0):
    return pl.BlockSpec((tr, w), lambda i: (i, cb))


def _fspec(shape):
    nd = len(shape)
    return pl.BlockSpec(shape, lambda i: (0,) * nd)


def _row_tile(lp):
    return ROW_ALIGN if lp % ROW_ALIGN == 0 else lp


def _rmsnorm_fwd(x, g, *, name):
    lp, d = x.shape
    tr = _row_tile(lp)

    def body(x_ref, g_ref, o_ref):
        xv = x_ref[...]
        r = lax.rsqrt(jnp.mean(xv * xv, axis=-1, keepdims=True) + EPS)
        o_ref[...] = (xv * r * g_ref[...]).astype(BF16)

    return pl.pallas_call(
        body, grid=(lp // tr,),
        in_specs=[_rspec(tr, d), _fspec((1, d))],
        out_specs=_rspec(tr, d),
        out_shape=jax.ShapeDtypeStruct((lp, d), BF16),
        compiler_params=_cp("parallel"), name=name)(x, g)


def _residual_norm(h, t, g, *, name):
    lp, d = h.shape
    tr = _row_tile(lp)

    def body(h_ref, t_ref, g_ref, h1_ref, m_ref):
        xv = h_ref[...] + t_ref[...]
        h1_ref[...] = xv
        r = lax.rsqrt(jnp.mean(xv * xv, axis=-1, keepdims=True) + EPS)
        m_ref[...] = (xv * r * g_ref[...]).astype(BF16)

    return pl.pallas_call(
        body, grid=(lp // tr,),
        in_specs=[_rspec(tr, d), _rspec(tr, d), _fspec((1, d))],
        out_specs=[_rspec(tr, d), _rspec(tr, d)],
        out_shape=[jax.ShapeDtypeStruct((lp, d), F32), jax.ShapeDtypeStruct((lp, d), BF16)],
        compiler_params=_cp("parallel"), name=name)(h, t, g)


def _rmsnorm_bwd_add(x, g, dy, dres, *, name):
    lp, d = x.shape
    tr = _row_tile(lp)

    def body(x_ref, g_ref, dy_ref, dr_ref, dx_ref, dg_ref):
        @pl.when(pl.program_id(0) == 0)
        def _():
            dg_ref[...] = jnp.zeros_like(dg_ref)

        xv = x_ref[...]
        r = lax.rsqrt(jnp.mean(xv * xv, axis=-1, keepdims=True) + EPS)
        xh = xv * r
        dyv = dy_ref[...]
        dg_ref[...] += jnp.sum(dyv * xh, axis=0, keepdims=True)
        dxh = dyv * g_ref[...]
        dx_ref[...] = dr_ref[...] + r * (dxh - xh * jnp.mean(dxh * xh, axis=-1, keepdims=True))

    return pl.pallas_call(
        body, grid=(lp // tr,),
        in_specs=[_rspec(tr, d), _fspec((1, d)), _rspec(tr, d), _rspec(tr, d)],
        out_specs=[_rspec(tr, d), _fspec((1, d))],
        out_shape=[jax.ShapeDtypeStruct((lp, d), F32), jax.ShapeDtypeStruct((1, d), F32)],
        compiler_params=_cp("arbitrary"), name=name)(x, g, dy, dres)


def _final_loss(h1, t2, g, tgt, *, n_tok, name):
    lp, d = h1.shape
    tr = _row_tile(lp)

    def body(h_ref, t_ref, g_ref, tg_ref, dh_ref, loss_ref, dg_ref):
        i = pl.program_id(0)

        @pl.when(i == 0)
        def _():
            loss_ref[...] = jnp.zeros_like(loss_ref)
            dg_ref[...] = jnp.zeros_like(dg_ref)

        xv = h_ref[...] + t_ref[...]
        r = lax.rsqrt(jnp.mean(xv * xv, axis=-1, keepdims=True) + EPS)
        xh = xv * r
        gv = g_ref[...]
        row = i * tr + lax.broadcasted_iota(jnp.int32, (tr, 1), 0)
        valid = (row >= N_META) & (row < N_META + n_tok)
        err = jnp.where(valid, xh * gv - tg_ref[...], 0.0)
        loss_ref[...] += jnp.sum(0.5 * err * err) / d
        dy = err / d
        dg_ref[...] += jnp.sum(dy * xh, axis=0, keepdims=True)
        dxh = dy * gv
        dh_ref[...] = r * (dxh - xh * jnp.mean(dxh * xh, axis=-1, keepdims=True))

    return pl.pallas_call(
        body, grid=(lp // tr,),
        in_specs=[_rspec(tr, d), _rspec(tr, d), _fspec((1, d)), _rspec(tr, d)],
        out_specs=[_rspec(tr, d), _fspec((1, LANES)), _fspec((1, d))],
        out_shape=[jax.ShapeDtypeStruct((lp, d), F32), jax.ShapeDtypeStruct((1, LANES), F32),
                   jax.ShapeDtypeStruct((1, d), F32)],
        compiler_params=_cp("arbitrary"), name=name)(h1, t2, g, tgt)


def _hg_out(o_f, o_b, proj, gain, *, col_g, name):
    lp, w = o_f.shape
    tr = _row_tile(lp)
    hh = w // HG_DK

    def body(of_ref, ob_ref, g_ref, gain_ref, y_ref):
        gv = g_ref[...]
        sg = gv * _sigmoid(gv)
        for h in range(hh):
            sl = slice(h * HG_DK, (h + 1) * HG_DK)
            o = of_ref[:, sl] + ob_ref[:, sl]
            r = lax.rsqrt(jnp.mean(o * o, axis=-1, keepdims=True) + EPS)
            y_ref[:, sl] = (o * r * gain_ref[:, sl] * sg[:, sl]).astype(BF16)

    return pl.pallas_call(
        body, grid=(lp // tr,),
        in_specs=[_rspec(tr, w), _rspec(tr, w), _rspec(tr, w, col_g // w), _fspec((1, w))],
        out_specs=_rspec(tr, w),
        out_shape=jax.ShapeDtypeStruct((lp, w), BF16),
        compiler_params=_cp("parallel"), name=name)(o_f, o_b, proj, gain)


def _hg_out_bwd(o_f, o_b, proj, gain, dy, *, col_g, name):
    lp, w = o_f.shape
    tr = _row_tile(lp)
    hh = w // HG_DK

    def body(of_ref, ob_ref, g_ref, gain_ref, dy_ref, do_ref, dg_ref, dgain_ref):
        @pl.when(pl.program_id(0) == 0)
        def _():
            dgain_ref[...] = jnp.zeros_like(dgain_ref)

        for h in range(hh):
            sl = slice(h * HG_DK, (h + 1) * HG_DK)
            gv = g_ref[:, sl]
            s = _sigmoid(gv)
            sg = gv * s
            dsg = s + gv * s * (1.0 - s)
            o = of_ref[:, sl] + ob_ref[:, sl]
            r = lax.rsqrt(jnp.mean(o * o, axis=-1, keepdims=True) + EPS)
            on = o * r
            dyv = dy_ref[:, sl]
            gn = gain_ref[:, sl]
            dgain_ref[:, sl] += jnp.sum(dyv * on * sg, axis=0, keepdims=True)
            dg_ref[:, sl] = (dyv * on * gn * dsg).astype(BF16)
            don = dyv * gn * sg
            do_ref[:, sl] = r * (don - on * jnp.mean(don * on, axis=-1, keepdims=True))

    return pl.pallas_call(
        body, grid=(lp // tr,),
        in_specs=[_rspec(tr, w), _rspec(tr, w), _rspec(tr, w, col_g // w), _fspec((1, w)),
                  _rspec(tr, w)],
        out_specs=[_rspec(tr, w), _rspec(tr, w), _fspec((1, w))],
        out_shape=[jax.ShapeDtypeStruct((lp, w), F32), jax.ShapeDtypeStruct((lp, w), BF16),
                   jax.ShapeDtypeStruct((1, w), F32)],
        compiler_params=_cp("arbitrary"), name=name)(o_f, o_b, proj, gain, dy)


HG_ROWS = 128
HG_HALVES = (1, 2, 4, 8, 16, 32, 64)


def _hg_gates(zq, z, lbv):
    qh = zq * _sigmoid(zq)
    s = _sigmoid(z)
    f = lbv + (1.0 - lbv) * s
    kk = (1.0 - lbv) * _sigmoid(-z)
    return qh, s, f, jnp.log(f), kk


def _block_cumsum(g, pos, suffix):
    x = g
    for k in HG_HALVES:
        if suffix:
            x = x + jnp.where(pos < HG_ROWS - k, pltpu.roll(x, HG_ROWS - k, 0), 0.0)
        else:
            x = x + jnp.where(pos >= k, pltpu.roll(x, k, 0), 0.0)
    return x


def _pair_levels(b, pos, reverse):
    out = []
    first = b
    for m in HG_HALVES:
        if m > 1:
            first = jnp.where((pos & (m - 1)) >= m // 2, pltpu.roll(first, m // 2, 0), first)
        nxt = pltpu.roll(first, HG_ROWS - m, 0)
        upper = (pos & (2 * m - 1)) >= m
        if reverse:
            eq = jnp.where(upper, 0.0, jnp.exp(jnp.minimum(b - nxt, 0.0)))
            ek = jnp.where(upper, jnp.exp(jnp.minimum(first - b, 0.0)), 0.0)
        else:
            eq = jnp.where(upper, jnp.exp(jnp.minimum(b - first, 0.0)), 0.0)
            ek = jnp.where(upper, 0.0, jnp.exp(jnp.minimum(nxt - b, 0.0)))
        out.append((m.bit_length(), eq, ek))
    return out


def _hg_scan_fwd(proj, lb, *, reverse, col_q, col_z, col_i, hh, name):
    lp = proj.shape[0]
    n_blocks = lp // HG_ROWS
    last = 0 if reverse else HG_ROWS - 1

    def body(q_ref, z_ref, i_ref, lb_ref, o_ref, st_ref):
        lbv = lb_ref[...]
        pos = lax.broadcasted_iota(jnp.int32, (HG_ROWS, 1), 0)
        ri = lax.broadcasted_iota(jnp.int32, (HG_ROWS, HG_ROWS), 0)
        ci = lax.broadcasted_iota(jnp.int32, (HG_ROWS, HG_ROWS), 1)

        def block(bi, st):
            bb = (n_blocks - 1 - bi) if reverse else bi
            r0 = pl.multiple_of(bb * HG_ROWS, HG_ROWS)
            v16 = i_ref[pl.ds(r0, HG_ROWS), :].astype(BF16)
            qh, _, _, g, kk = _hg_gates(q_ref[pl.ds(r0, HG_ROWS), :], z_ref[pl.ds(r0, HG_ROWS), :],
                                        lbv)
            b = _block_cumsum(g, pos, reverse)
            bl = b[last:last + 1, :]
            qe = (qh * jnp.exp(b)).astype(BF16)
            kd = (kk * jnp.exp(bl - b)).astype(BF16)
            a = jnp.where(ri == ci, jnp.sum(qh * kk, axis=1, keepdims=True), 0.0)
            for sh, eq, ek in _pair_levels(b, pos, reverse):
                a = a + jnp.where((ri >> sh) == (ci >> sh),
                                  _nt((qh * eq).astype(BF16), (kk * ek).astype(BF16)), 0.0)
            st_ref[bb] = st
            o_ref[pl.ds(r0, HG_ROWS), :] = _nn(a.astype(BF16), v16) + _nt(qe, st.astype(BF16))
            return jnp.exp(bl) * st + _tn(v16, kd)

        lax.fori_loop(0, n_blocks, block, jnp.zeros((HG_DK, HG_DK), F32))

    cspec = lambda col: pl.BlockSpec((lp, HG_DK), lambda h: (0, col // HG_DK + h))
    return pl.pallas_call(
        body, grid=(hh,),
        in_specs=[cspec(col_q), cspec(col_z), cspec(col_i),
                  pl.BlockSpec((None, 1, HG_DK), lambda h: (h, 0, 0))],
        out_specs=[pl.BlockSpec((lp, HG_DK), lambda h: (0, h)),
                   pl.BlockSpec((None, n_blocks, HG_DK, HG_DK), lambda h: (h, 0, 0, 0))],
        out_shape=[jax.ShapeDtypeStruct((lp, hh * HG_DK), F32),
                   jax.ShapeDtypeStruct((hh, n_blocks, HG_DK, HG_DK), F32)],
        compiler_params=_cp("parallel"), name=name)(proj, proj, proj, lb)


def _hg_scan_bwd(proj, lb, states, do, *, reverse, col_q, col_z, col_i, hh, name):
    lp = proj.shape[0]
    n_blocks = lp // HG_ROWS
    last = 0 if reverse else HG_ROWS - 1

    def body(q_ref, z_ref, i_ref, lb_ref, st_ref, do_ref, dq_ref, dz_ref, dv_ref, dlb_ref):
        lbv = lb_ref[...]
        pos = lax.broadcasted_iota(jnp.int32, (HG_ROWS, 1), 0)
        ri = lax.broadcasted_iota(jnp.int32, (HG_ROWS, HG_ROWS), 0)
        ci = lax.broadcasted_iota(jnp.int32, (HG_ROWS, HG_ROWS), 1)

        def block(bi, carry):
            dst, dlb = carry
            bb = bi if reverse else (n_blocks - 1 - bi)
            r0 = pl.multiple_of(bb * HG_ROWS, HG_ROWS)
            zq = q_ref[pl.ds(r0, HG_ROWS), :]
            v16 = i_ref[pl.ds(r0, HG_ROWS), :].astype(BF16)
            do16 = do_ref[pl.ds(r0, HG_ROWS), :].astype(BF16)
            qh, s, f, g, kk = _hg_gates(zq, z_ref[pl.ds(r0, HG_ROWS), :], lbv)
            b = _block_cumsum(g, pos, reverse)
            bl = b[last:last + 1, :]
            eb = jnp.exp(b)
            ebl = jnp.exp(bl - b)
            decay = jnp.exp(bl)
            qe16 = (qh * eb).astype(BF16)
            kd16 = (kk * ebl).astype(BF16)
            st = st_ref[bb]
            st16, dst16 = st.astype(BF16), dst.astype(BF16)
            same_row = ri == ci
            da = _nt(do16, v16)
            da_diag = jnp.sum(jnp.where(same_row, da, 0.0), axis=1, keepdims=True)
            dq_state = eb * _nn(do16, st16)
            dk_state = ebl * _nn(v16, dst16)
            dq = dq_state + da_diag * kk
            dk = dk_state + da_diag * qh
            dbl = (decay * jnp.sum(st * dst, axis=0, keepdims=True)
                   + jnp.sum(kk * dk_state, axis=0, keepdims=True))
            db = qh * dq_state - kk * dk_state + jnp.where(pos == last, dbl, 0.0)
            a = jnp.where(same_row, jnp.sum(qh * kk, axis=1, keepdims=True), 0.0)
            for sh, eq, ek in _pair_levels(b, pos, reverse):
                same = (ri >> sh) == (ci >> sh)
                q16, k16 = (qh * eq).astype(BF16), (kk * ek).astype(BF16)
                a = a + jnp.where(same, _nt(q16, k16), 0.0)
                da16 = jnp.where(same, da, 0.0).astype(BF16)
                gq, gk = _nn(da16, k16), _tn(da16, q16)
                dq = dq + eq * gq
                dk = dk + ek * gk
                db = db + (q16.astype(F32) * gq - k16.astype(F32) * gk)
            dg = _block_cumsum(db, pos, not reverse)
            df = dg / f - dk
            sq = _sigmoid(zq)
            dq_ref[pl.ds(r0, HG_ROWS), :] = dq * (sq + zq * sq * (1.0 - sq))
            dz_ref[pl.ds(r0, HG_ROWS), :] = df * (1.0 - lbv) * s * (1.0 - s)
            dv_ref[pl.ds(r0, HG_ROWS), :] = _nt(kd16, dst16) + _tn(a.astype(BF16), do16)
            return (decay * dst + _tn(do16, qe16),
                    dlb + jnp.sum(df * (1.0 - s), axis=0, keepdims=True))

        _, dlb = lax.fori_loop(0, n_blocks, block,
                               (jnp.zeros((HG_DK, HG_DK), F32), jnp.zeros((1, HG_DK), F32)))
        dlb_ref[...] = dlb

    cspec = lambda col: pl.BlockSpec((lp, HG_DK), lambda h: (0, col // HG_DK + h))
    ospec = pl.BlockSpec((lp, HG_DK), lambda h: (0, h))
    sds = jax.ShapeDtypeStruct((lp, hh * HG_DK), F32)
    return pl.pallas_call(
        body, grid=(hh,),
        in_specs=[cspec(col_q), cspec(col_z), cspec(col_i),
                  pl.BlockSpec((None, 1, HG_DK), lambda h: (h, 0, 0)),
                  pl.BlockSpec((None, n_blocks, HG_DK, HG_DK), lambda h: (h, 0, 0, 0)),
                  ospec],
        out_specs=[ospec, ospec, ospec, pl.BlockSpec((None, 1, HG_DK), lambda h: (h, 0, 0))],
        out_shape=[sds, sds, sds, jax.ShapeDtypeStruct((hh, 1, HG_DK), F32)],
        compiler_params=_cp("parallel"), name=name)(proj, proj, proj, lb, states, do)


def _na_rows(r, rows):
    rs = jnp.clip(r - NA_WIN_H // 2, 0, rows - NA_WIN_H)
    i0 = rs - r + (NA_WIN_H - 1)
    q0 = pl.multiple_of(N_META + GRID_W * r, 16)
    k0 = pl.multiple_of(N_META + GRID_W * rs, 16)
    return i0, q0, k0


def _na_scores(q16, k16, km16, tb_ref, i0, scale):
    s = _nt(q16, k16) * scale
    bias = jnp.concatenate([tb_ref[i0 + j] for j in range(NA_WIN_H)], axis=1)
    return s + bias, _nt(q16, km16) * scale


NA_HB = LANES // NA_HEAD_DIM


def _na_head_lanes():
    lane = lax.broadcasted_iota(jnp.int32, (1, LANES), 1)
    return [lane // NA_HEAD_DIM == h for h in range(NA_HB)]


def _na_only(mask, x):
    return jnp.where(mask, x, jnp.zeros_like(x))


def _na_fwd(proj, tb, *, n_tok, nh, name):
    lp = proj.shape[0]
    dh, hb = NA_HEAD_DIM, NA_HB
    naw = nh * dh
    rows = n_tok // GRID_W
    scale = dh ** -0.5
    kw = NA_WIN_H * GRID_W

    def body(q_ref, k_ref, v_ref, tb_ref, o_ref, lse_ref, q16_ref, k16_ref, v16_ref):
        o_ref[...] = jnp.zeros_like(o_ref)
        lse_ref[...] = jnp.zeros_like(lse_ref)
        q16_ref[...] = q_ref[...].astype(BF16)
        k16_ref[...] = k_ref[...].astype(BF16)
        v16_ref[...] = v_ref[...].astype(BF16)
        heads = _na_head_lanes()
        km = k16_ref[0:N_META, :]
        vm = v16_ref[0:N_META, :]
        qm = q16_ref[0:N_META, :]
        o_m = None
        for h in range(hb):
            s = _nt(_na_only(heads[h], qm), km) * scale
            m = jnp.max(s, axis=1, keepdims=True)
            p = jnp.exp(s - m)
            l = jnp.sum(p, axis=1, keepdims=True)
            o_h = _nn(p.astype(BF16), vm) / l
            o_m = o_h if o_m is None else jnp.where(heads[h], o_h, o_m)
            lse_ref[h, 0:N_META, :] = m + jnp.log(l)
        o_ref[0:N_META, :] = o_m

        def step(r, carry):
            i0, q0, k0 = _na_rows(r, rows)
            q16 = q16_ref[pl.ds(q0, GRID_W), :]
            k16 = k16_ref[pl.ds(k0, kw), :]
            v16 = v16_ref[pl.ds(k0, kw), :]
            o = None
            for h in range(hb):
                s, sm = _na_scores(_na_only(heads[h], q16), k16, km, tb_ref.at[h], i0, scale)
                m = jnp.maximum(jnp.max(s, axis=1, keepdims=True),
                                jnp.max(sm, axis=1, keepdims=True))
                p = jnp.exp(s - m)
                pm = jnp.exp(sm - m)
                l = jnp.sum(p, axis=1, keepdims=True) + jnp.sum(pm, axis=1, keepdims=True)
                o_h = (_nn(p.astype(BF16), v16) + _nn(pm.astype(BF16), vm)) / l
                o = o_h if o is None else jnp.where(heads[h], o_h, o)
                lse_ref[h, pl.ds(q0, GRID_W), :] = m + jnp.log(l)
            o_ref[pl.ds(q0, GRID_W), :] = o
            return carry

        lax.fori_loop(0, rows, step, 0, unroll=2)

    cblk = lambda col: pl.BlockSpec((lp, LANES), lambda g: (0, col // LANES + g))
    return pl.pallas_call(
        body, grid=(nh // hb,),
        in_specs=[cblk(0), cblk(naw), cblk(2 * naw),
                  pl.BlockSpec((hb, 2 * NA_WIN_H - 1, GRID_W, GRID_W), lambda g: (g, 0, 0, 0))],
        out_specs=[cblk(0), pl.BlockSpec((hb, lp, 1), lambda g: (g, 0, 0))],
        out_shape=[jax.ShapeDtypeStruct((lp, naw), F32), jax.ShapeDtypeStruct((nh, lp, 1), F32)],
        scratch_shapes=[pltpu.VMEM((lp, LANES), BF16)] * 3,
        compiler_params=_cp("parallel"), name=name)(proj, proj, proj, tb)


def _na_bwd(proj, tb, o, lse, do, *, n_tok, nh, name):
    lp = proj.shape[0]
    dh, hb = NA_HEAD_DIM, NA_HB
    naw = nh * dh
    rows = n_tok // GRID_W
    scale = dh ** -0.5
    kw = NA_WIN_H * GRID_W

    def body(q_ref, k_ref, v_ref, tb_ref, o_ref, lse_ref, do_ref, dq_ref, dk_ref, dv_ref, dtb_ref,
             q16_ref, k16_ref, v16_ref):
        dq_ref[...] = jnp.zeros_like(dq_ref)
        dk_ref[...] = jnp.zeros_like(dk_ref)
        dv_ref[...] = jnp.zeros_like(dv_ref)
        dtb_ref[...] = jnp.zeros_like(dtb_ref)
        q16_ref[...] = q_ref[...].astype(BF16)
        k16_ref[...] = k_ref[...].astype(BF16)
        v16_ref[...] = v_ref[...].astype(BF16)
        heads = _na_head_lanes()
        km = k16_ref[0:N_META, :]
        vm = v16_ref[0:N_META, :]
        qm = q16_ref[0:N_META, :]
        dom = do_ref[0:N_META, :]
        prod = dom * o_ref[0:N_META, :]
        dq_m = None
        dkm0 = jnp.zeros((N_META, LANES), F32)
        dvm0 = jnp.zeros((N_META, LANES), F32)
        for h in range(hb):
            q_h = _na_only(heads[h], qm)
            do_h = _na_only(heads[h], dom).astype(BF16)
            p = jnp.exp(_nt(q_h, km) * scale - lse_ref[h, 0:N_META, :])
            delta = jnp.sum(_na_only(heads[h], prod), axis=1, keepdims=True)
            ds = (p * (_nt(do_h, vm) - delta)).astype(BF16)
            dq_h = _nn(ds, km) * scale
            dq_m = dq_h if dq_m is None else jnp.where(heads[h], dq_h, dq_m)
            dkm0 = dkm0 + _tn(ds, q_h) * scale
            dvm0 = dvm0 + _tn(p.astype(BF16), do_h)
        dq_ref[0:N_META, :] = dq_m

        def step(r, carry):
            dkm, dvm = carry
            i0, q0, k0 = _na_rows(r, rows)
            q16 = q16_ref[pl.ds(q0, GRID_W), :]
            k16 = k16_ref[pl.ds(k0, kw), :]
            v16 = v16_ref[pl.ds(k0, kw), :]
            dov = do_ref[pl.ds(q0, GRID_W), :]
            prod = dov * o_ref[pl.ds(q0, GRID_W), :]
            dq = None
            dk = jnp.zeros((kw, LANES), F32)
            dv = jnp.zeros((kw, LANES), F32)
            for h in range(hb):
                q_h = _na_only(heads[h], q16)
                do_h = _na_only(heads[h], dov).astype(BF16)
                s, sm = _na_scores(q_h, k16, km, tb_ref.at[h], i0, scale)
                lse = lse_ref[h, pl.ds(q0, GRID_W), :]
                p = jnp.exp(s - lse)
                pm = jnp.exp(sm - lse)
                delta = jnp.sum(_na_only(heads[h], prod), axis=1, keepdims=True)
                ds = p * (_nt(do_h, v16) - delta)
                dsm = (pm * (_nt(do_h, vm) - delta)).astype(BF16)
                ds16 = ds.astype(BF16)
                dq_h = (_nn(ds16, k16) + _nn(dsm, km)) * scale
                dq = dq_h if dq is None else jnp.where(heads[h], dq_h, dq)
                dk = dk + _tn(ds16, q_h) * scale
                dv = dv + _tn(p.astype(BF16), do_h)
                dkm = dkm + _tn(dsm, q_h) * scale
                dvm = dvm + _tn(pm.astype(BF16), do_h)
                for j in range(NA_WIN_H):
                    dtb_ref[h, i0 + j] += ds[:, j * GRID_W:(j + 1) * GRID_W]
            dq_ref[pl.ds(q0, GRID_W), :] = dq
            dk_ref[pl.ds(k0, kw), :] += dk
            dv_ref[pl.ds(k0, kw), :] += dv
            return dkm, dvm

        dkm, dvm = lax.fori_loop(0, rows, step, (dkm0, dvm0))
        dk_ref[0:N_META, :] += dkm
        dv_ref[0:N_META, :] += dvm

    cblk = lambda col: pl.BlockSpec((lp, LANES), lambda g: (0, col // LANES + g))
    tbs = pl.BlockSpec((hb, 2 * NA_WIN_H - 1, GRID_W, GRID_W), lambda g: (g, 0, 0, 0))
    sds = jax.ShapeDtypeStruct((lp, naw), F32)
    return pl.pallas_call(
        body, grid=(nh // hb,),
        in_specs=[cblk(0), cblk(naw), cblk(2 * naw), tbs, cblk(0),
                  pl.BlockSpec((hb, lp, 1), lambda g: (g, 0, 0)), cblk(0)],
        out_specs=[cblk(0), cblk(0), cblk(0), tbs],
        out_shape=[sds, sds, sds, jax.ShapeDtypeStruct(tb.shape, F32)],
        scratch_shapes=[pltpu.VMEM((lp, LANES), BF16)] * 3,
        compiler_params=_cp("parallel"), name=name)(proj, proj, proj, tb, o, lse, do)


def _rpb_onehot():
    c = np.arange(GRID_W)[:, None]
    w = np.arange(GRID_W)[None, :]
    cs = np.clip(c - NA_WIN_W // 2, 0, GRID_W - NA_WIN_W)
    in_win = (w >= cs) & (w < cs + NA_WIN_W)
    dc = np.clip(w - c, -(NA_WIN_W - 1), NA_WIN_W - 1) + NA_WIN_W - 1
    oh = np.zeros((LANES, GRID_W * GRID_W), np.float32)
    flat = np.arange(GRID_W * GRID_W).reshape(GRID_W, GRID_W)
    oh[dc[in_win], flat[in_win]] = 1.0
    neg = np.where(in_win, 0.0, -1e30).astype(np.float32).reshape(1, -1)
    return oh, neg


def _assemble_dproj(dq_na, dk_na, dv_na, dq_f, dq_b, dz_f, dz_b, dv_f, dv_b, dg, dgn, dgh, *, name):
    lp, naw = dq_na.shape
    hgw = dq_f.shape[1]
    d = dgn.shape[1]
    cols = 3 * naw + 5 * hgw + 2 * d
    tr = _row_tile(lp)

    def body(nq_ref, nk_ref, nv_ref, qf_ref, qb_ref, zf_ref, zb_ref, vf_ref, vb_ref, g_ref, gn_ref,
             gh_ref, o_ref):
        o_ref[:, 0:naw] = nq_ref[...].astype(BF16)
        o_ref[:, naw:2 * naw] = nk_ref[...].astype(BF16)
        o_ref[:, 2 * naw:3 * naw] = nv_ref[...].astype(BF16)
        c0 = 3 * naw
        o_ref[:, c0:c0 + hgw] = (qf_ref[...] + qb_ref[...]).astype(BF16)
        o_ref[:, c0 + hgw:c0 + 2 * hgw] = zf_ref[...].astype(BF16)
        o_ref[:, c0 + 2 * hgw:c0 + 3 * hgw] = zb_ref[...].astype(BF16)
        o_ref[:, c0 + 3 * hgw:c0 + 4 * hgw] = (vf_ref[...] + vb_ref[...]).astype(BF16)
        o_ref[:, c0 + 4 * hgw:c0 + 5 * hgw] = g_ref[...]
        o_ref[:, c0 + 5 * hgw:c0 + 5 * hgw + d] = gn_ref[...]
        o_ref[:, c0 + 5 * hgw + d:] = gh_ref[...]

    hg, na = _rspec(tr, hgw), _rspec(tr, naw)
    return pl.pallas_call(
        body, grid=(lp // tr,),
        in_specs=[na, na, na, hg, hg, hg, hg, hg, hg, hg, _rspec(tr, d), _rspec(tr, d)],
        out_specs=_rspec(tr, cols),
        out_shape=jax.ShapeDtypeStruct((lp, cols), BF16),
        compiler_params=_cp("parallel"), name=name)(dq_na, dk_na, dv_na, dq_f, dq_b, dz_f, dz_b,
                                                    dv_f, dv_b, dg, dgn, dgh)


def _adamw(w, g, m, v, *, name):
    rows, cols = w.shape
    tr = 256 if rows % 256 == 0 else rows

    def body(w_ref, g_ref, m_ref, v_ref, d_ref, mo_ref, vo_ref):
        gv = g_ref[...]
        mn = ADAM_B1 * m_ref[...] + (1.0 - ADAM_B1) * gv
        vn = ADAM_B2 * v_ref[...] + (1.0 - ADAM_B2) * (gv * gv)
        m_hat = mn / (1.0 - ADAM_B1 ** ADAM_STEP)
        v_hat = vn / (1.0 - ADAM_B2 ** ADAM_STEP)
        d_ref[...] = -ADAM_LR * (m_hat / (jnp.sqrt(v_hat) + ADAM_EPS) + ADAM_WD * w_ref[...])
        mo_ref[...] = mn
        vo_ref[...] = vn

    spec = _rspec(tr, cols)
    sds = jax.ShapeDtypeStruct((rows, cols), F32)
    return pl.pallas_call(
        body, grid=(rows // tr,), in_specs=[spec] * 4, out_specs=[spec] * 3, out_shape=[sds] * 3,
        compiler_params=_cp("parallel"), name=name)(w, g, m, v)


def _local_step(x, tgt, meta, first_weight, rest_weights, g_mix, g_mlp, g_fin, hg_gain, rpb, lb,
                early_grads=None, mid_grads=None, late_grad=None):
    n_tok, d = x.shape
    hgw = hg_gain.shape[1]
    nh, hh = rpb.shape[0], hgw // HG_DK
    naw = nh * NA_HEAD_DIM
    l_real = N_META + n_tok
    lp = -(-l_real // ROW_ALIGN) * ROW_ALIGN
    n_chunks = l_real // HG_CHUNK
    pad = lp - l_real
    col_qhg = 3 * naw
    col_zf, col_zb, col_i, col_g = (col_qhg + hgw, col_qhg + 2 * hgw, col_qhg + 3 * hgw,
                                    col_qhg + 4 * hgw)
    col_gate = col_qhg + 5 * hgw

    zpad = jnp.zeros((pad, d), F32)
    h0 = jnp.concatenate([meta, x, zpad], axis=0)
    tgt_p = jnp.concatenate([jnp.zeros((N_META, d), F32), tgt, zpad], axis=0)

    oh_np, neg_np = _rpb_onehot()
    oh = jnp.asarray(oh_np)
    rpb_p = jnp.pad(rpb.reshape(nh * (2 * NA_WIN_H - 1), 2 * NA_WIN_W - 1),
                    ((0, 0), (0, LANES - (2 * NA_WIN_W - 1))))
    tb = _matmul(rpb_p, oh, tm=rpb_p.shape[0], tn=512, tk=LANES, precision=HIGHEST,
                 name="rpb_expand")
    tb = (tb + jnp.asarray(neg_np)).reshape(nh, 2 * NA_WIN_H - 1, GRID_W, GRID_W)

    a = _rmsnorm_fwd(h0, g_mix, name="norm_mix")
    w_in = first_weight(a)
    proj = _matmul(a, w_in, name="mm_in")
    o_na, lse = _na_fwd(proj, tb, n_tok=n_tok, nh=nh, name="na_fwd")
    lb_f = lb[0].reshape(hh, 1, HG_DK)
    lb_b = lb[1].reshape(hh, 1, HG_DK)
    scan_kw = dict(col_q=col_qhg, col_i=col_i, hh=hh)
    o_f, st_f = _hg_scan_fwd(proj, lb_f, reverse=False, col_z=col_zf, name="hg_scan_f", **scan_kw)
    o_b, st_b = _hg_scan_fwd(proj, lb_b, reverse=True, col_z=col_zb, name="hg_scan_b", **scan_kw)
    o_hg = _hg_out(o_f, o_b, proj, hg_gain, col_g=col_g, name="hg_out")
    w_na, w_hg, w_o, w_up, w_down = rest_weights(o_hg)
    y_na = _matmul(o_na, w_na, name="mm_na_out")
    gates = ((proj, col_gate), (proj, col_gate + d))

    def mix_gates(acc, gn, gh, yn):
        return acc, _sigmoid(gn) * yn + _sigmoid(gh) * acc

    def mix_gates_bwd(dmix, gn, gh, yn, yh):
        sn, sh = _sigmoid(gn), _sigmoid(gh)
        return dmix * sn, dmix * sh, dmix * yn * sn * (1.0 - sn), dmix * yh * sh * (1.0 - sh)

    y_hg, mix = _matmul(o_hg, w_hg, name="mm_hg_out", epilogue=mix_gates,
                        tiles=(*gates, (y_na, 0)), out_dtypes=(F32, BF16))
    t1 = _matmul(mix, w_o, name="mm_o")
    h1, mlp_in = _residual_norm(h0, t1, g_mlp, name="resid_norm_mlp")
    u, act = _matmul(mlp_in, w_up, name="mm_up", out_dtypes=(F32, BF16),
                     epilogue=lambda acc: (acc, jnp.square(jnp.maximum(acc, 0.0))))
    t2 = _matmul(act, w_down, name="mm_down")
    dh2, loss, dg_fin = _final_loss(h1, t2, g_fin, tgt_p, n_tok=n_tok, name="final_loss")

    (du,) = _matmul(dh2, w_down, tb=True, name="mm_down_dx", tiles=((u, 0),), out_dtypes=(BF16,),
                    epilogue=lambda acc, uv: (acc * 2.0 * jnp.maximum(uv, 0.0),))
    dw_down = _matmul(act, dh2, ta=True, name="mm_down_dw")
    dm = _matmul(du, w_up, tb=True, name="mm_up_dx")
    dw_up = _matmul(mlp_in, du, ta=True, name="mm_up_dw")
    dh1, dg_mlp = _rmsnorm_bwd_add(h1, g_mlp, dm, dh2, name="norm_mlp_bwd")
    dy_na, dy_hg, dgn, dgh = _matmul(dh1, w_o, tb=True, name="mm_o_dx", epilogue=mix_gates_bwd,
                                     tiles=(*gates, (y_na, 0), (y_hg, 0)), out_dtypes=(BF16,) * 4)
    dw_o = _matmul(mix, dh1, ta=True, name="mm_o_dw")
    do_na = _matmul(dy_na, w_na, tb=True, name="mm_na_out_dx")
    dw_na = _matmul(o_na, dy_na, ta=True, name="mm_na_out_dw")
    do_hg = _matmul(dy_hg, w_hg, tb=True, name="mm_hg_out_dx")
    dw_hg = _matmul(o_hg, dy_hg, ta=True, name="mm_hg_out_dw")
    token = early_grads([dw_na, dw_hg, dw_o, dw_up, dw_down]) if early_grads else None
    if token is not None:
        hg_gain = hg_gain + token[0:1, 0:1]
    d_o, dg_hg, d_gain = _hg_out_bwd(o_f, o_b, proj, hg_gain, do_hg, col_g=col_g, name="hg_out_bwd")
    dq_f, dz_f, dv_f, dlb_f = _hg_scan_bwd(proj, lb_f, st_f, d_o, reverse=False, col_z=col_zf,
                                           name="hg_scan_f_bwd", **scan_kw)
    token = mid_grads(dq_f) if mid_grads else None
    lb_b_late = lb_b if token is None else lb_b + token[0:1, 0:1]
    dq_b, dz_b, dv_b, dlb_b = _hg_scan_bwd(proj, lb_b_late, st_b, d_o, reverse=True, col_z=col_zb,
                                           name="hg_scan_b_bwd", **scan_kw)
    dq_na, dk_na, dv_na, dtb = _na_bwd(proj, tb, o_na, lse, do_na, n_tok=n_tok, nh=nh, name="na_bwd")
    dproj = _assemble_dproj(dq_na, dk_na, dv_na, dq_f, dq_b, dz_f, dz_b, dv_f, dv_b, dg_hg, dgn,
                            dgh, name="assemble_dproj")
    dw_in = _matmul(a, dproj, ta=True, name="mm_in_dw")
    token = late_grad(dw_in) if late_grad else None
    da = _matmul(dproj, w_in, tb=True, name="mm_in_dx", after=token)
    dh0, dg_mix = _rmsnorm_bwd_add(h0, g_mix, da, dh1, name="norm_mix_bwd")
    d_rpb = _matmul(dtb.reshape(nh * (2 * NA_WIN_H - 1), GRID_W * GRID_W), oh, tb=True,
                    tm=nh * (2 * NA_WIN_H - 1), tn=LANES, tk=1024, precision=HIGHEST,
                    name="rpb_reduce")
    d_lb = jnp.concatenate([dlb_f.reshape(1, hgw), dlb_b.reshape(1, hgw)], axis=0)
    return (loss, dh0[N_META:l_real], dh0[:N_META], dw_in, dw_na, dw_hg, dw_o, dw_up, dw_down,
            dg_mix, dg_mlp, dg_fin, d_gain, d_rpb, d_lb)


N_CHIPS = 4
N_DEV = 8
ANY = pl.BlockSpec(memory_space=pl.ANY)


def _place():
    x, y, c = lax.axis_index("x"), lax.axis_index("y"), lax.axis_index("c")
    others = []
    for j in (1, 2, 3):
        tx = (1 - x) if (j >> 1) else x
        ty = (1 - y) if (j & 1) else y
        others.append((tx, ty))
    return x, y, c, others


def _piece(ref, axis, k, half, rh, cs):
    if axis == 1:
        return ref.at[pl.ds(pl.multiple_of(half * rh, 16), rh), pl.ds(pl.multiple_of(k * cs, LANES), cs)]
    return ref.at[pl.ds(pl.multiple_of(k * 2 * rh + half * rh, 16), rh), :]


def _cast_into_full(shard, axis, place, *, name):
    r, cs = shard.shape
    full = (r, cs * N_CHIPS) if axis == 1 else (r * N_CHIPS, cs)
    tr = next(t for t in (256, 128, 64, 32, 16) if r % t == 0)
    nt = r // tr

    def body(p_ref, s_ref, o_ref):
        o_ref[...] = s_ref[...].astype(BF16)

    if axis == 1:
        omap = lambda i, p_ref: (i, p_ref[0])
    else:
        omap = lambda i, p_ref: (p_ref[0] * nt + i, 0)
    return pl.pallas_call(
        body,
        grid_spec=pltpu.PrefetchScalarGridSpec(
            num_scalar_prefetch=1, grid=(nt,),
            in_specs=[pl.BlockSpec((tr, cs), lambda i, p_ref: (i, 0))],
            out_specs=pl.BlockSpec((tr, cs), omap)),
        out_shape=jax.ShapeDtypeStruct(full, BF16),
        compiler_params=_cp("parallel"), name=name)(place, shard)


HBM_SPEC = pl.BlockSpec(memory_space=pltpu.HBM)
SEM_SPEC = pl.BlockSpec(memory_space=pltpu.SEMAPHORE)
SPLIT_COPY = pltpu.CompilerParams(has_side_effects=pltpu.SideEffectType.DATAFLOW_SIDE_EFFECTING)
TOKEN = jax.ShapeDtypeStruct((8, LANES), F32)


def _geo(fulls, axes):
    out = []
    for f, ax in zip(fulls, axes):
        r, cs = (f.shape[0], f.shape[1] // N_CHIPS) if ax == 1 else (f.shape[0] // N_CHIPS, f.shape[1])
        out.append((ax, r // 2, cs))
    return out


def _gather_copies(refs, geo, send_sems, recv_sems):
    x, y, c, others = _place()
    chip = 2 * x + y
    cps = []
    for i, (ax, rh, cs) in enumerate(geo):
        mine = _piece(refs[i], ax, chip, c, rh, cs)
        for j, (tx, ty) in enumerate(others):
            cps.append(pltpu.make_async_remote_copy(
                src_ref=mine, dst_ref=mine, send_sem=send_sems.at[3 * i + j],
                recv_sem=recv_sems.at[3 * i + j], device_id=(tx, ty, c), device_id_type=MESH))
    return cps


def _allgather_start(fulls, axes, after, *, name):
    n = len(fulls)
    geo = _geo(fulls, axes)

    def body(*refs):
        w_refs = refs[:n]
        send_sems, recv_sems = refs[n + 1], refs[n + 2]
        token = refs[2 * n + 3]
        for cp in _gather_copies(w_refs, geo, send_sems, recv_sems):
            cp.start()
        token[...] = jnp.zeros_like(token)

    out = pl.pallas_call(
        body, name=name,
        out_shape=(pltpu.SemaphoreType.DMA((3 * n,)), pltpu.SemaphoreType.DMA((3 * n,)),
                   *[pltpu.HBM(f.shape, f.dtype) for f in fulls], TOKEN),
        in_specs=[HBM_SPEC] * n + [ANY],
        out_specs=(SEM_SPEC, SEM_SPEC, *[HBM_SPEC] * n, pl.BlockSpec(memory_space=pltpu.VMEM)),
        input_output_aliases={i: 2 + i for i in range(n)},
        compiler_params=SPLIT_COPY,
    )(*[pltpu.with_memory_space_constraint(f, pltpu.HBM) for f in fulls], after)
    return out[0], out[1], list(out[2:2 + n]), out[2 + n]


def _allgather_wait(send_sems, recv_sems, fulls, axes, after, *, name):
    n = len(fulls)
    geo = _geo(fulls, axes)

    def body(*refs):
        w_refs = refs[:n]
        for cp in _gather_copies(w_refs, geo, refs[n], refs[n + 1]):
            cp.wait_send()
            cp.wait_recv()

    return list(pl.pallas_call(
        body, name=name,
        out_shape=[pltpu.HBM(f.shape, f.dtype) for f in fulls],
        in_specs=[HBM_SPEC] * n + [SEM_SPEC, SEM_SPEC, ANY],
        out_specs=[HBM_SPEC] * n,
        input_output_aliases={i: i for i in range(n)},
        compiler_params=SPLIT_COPY,
    )(*fulls, send_sems, recv_sems, after))


def _allgather_forward(fulls, axes, *, name):
    n = len(fulls)
    geo = _geo(fulls, axes)

    def body(*refs):
        o_refs = refs[n:2 * n]
        send_sems, recv_sems = refs[2 * n:]
        x, y, c, others = _place()

        def rcopy(i, j, half, to):
            ax, rh, cs = geo[i]
            ref = _piece(o_refs[i], ax, 2 * others[j][0] + others[j][1], half, rh, cs)
            return pltpu.make_async_remote_copy(
                src_ref=ref, dst_ref=ref, send_sem=send_sems.at[3 * i + j],
                recv_sem=recv_sems.at[3 * i + j], device_id=to, device_id_type=MESH)

        cps = [rcopy(i, j, c, (x, y, 1 - c)) for i in range(n) for j in range(3)]
        for cp in cps:
            cp.start()
        for i in range(n):
            for j in range(3):
                rcopy(i, j, 1 - c, (x, y, c)).wait_recv()
        for cp in cps:
            cp.wait_send()

    return list(pl.pallas_call(
        body, in_specs=[ANY] * n, out_specs=[ANY] * n,
        out_shape=[jax.ShapeDtypeStruct(f.shape, f.dtype) for f in fulls],
        input_output_aliases={i: i for i in range(n)},
        scratch_shapes=[pltpu.SemaphoreType.DMA((3 * n,)), pltpu.SemaphoreType.DMA((3 * n,))],
        name=name)(*fulls))


def _scatter_geo(parts, axes):
    out = []
    for p, ax in zip(parts, axes):
        _, rh, cols = p.shape
        out.append((ax, rh, cols // N_CHIPS if ax == 1 else cols))
    return out


def _scatter_copies(p_refs, q_refs, geo, send_sems, recv_sems):
    x, y, c, others = _place()
    chip = 2 * x + y
    cps = []
    for i, (ax, rh, cw) in enumerate(geo):
        for j, (tx, ty) in enumerate(others):
            k = 2 * tx + ty
            src = (p_refs[i].at[0, :, pl.ds(pl.multiple_of(k * cw, LANES), cw)] if ax == 1
                   else p_refs[i].at[k])
            cps.append(pltpu.make_async_remote_copy(
                src_ref=src, dst_ref=q_refs[i].at[chip], send_sem=send_sems.at[3 * i + j],
                recv_sem=recv_sems.at[3 * i + j], device_id=(tx, ty, c), device_id_type=MESH))
    return cps


def _scatter_start(parts, axes, *, name):
    n = len(parts)
    geo = _scatter_geo(parts, axes)
    slots = [pltpu.HBM((N_CHIPS, rh, cw), p.dtype) for p, (_, rh, cw) in zip(parts, geo)]

    def body(*refs):
        p_refs, q_refs = refs[:n], refs[n:2 * n]
        send_sems, recv_sems = refs[2 * n], refs[2 * n + 1]
        token = refs[4 * n + 2]
        for cp in _scatter_copies(p_refs, q_refs, geo, send_sems, recv_sems):
            cp.start()
        token[...] = jnp.zeros_like(token)

    land = [pltpu.with_memory_space_constraint(lax.empty(s.inner_aval.shape, s.inner_aval.dtype), pltpu.HBM)
            for s in slots]
    out = pl.pallas_call(
        body, name=name,
        out_shape=(pltpu.SemaphoreType.DMA((3 * n,)), pltpu.SemaphoreType.DMA((3 * n,)),
                   *[pltpu.HBM(p.shape, p.dtype) for p in parts], *slots, TOKEN),
        in_specs=[HBM_SPEC] * (2 * n),
        out_specs=(SEM_SPEC, SEM_SPEC, *[HBM_SPEC] * (2 * n), pl.BlockSpec(memory_space=pltpu.VMEM)),
        input_output_aliases={i: 2 + i for i in range(2 * n)},
        compiler_params=SPLIT_COPY,
    )(*[pltpu.with_memory_space_constraint(p, pltpu.HBM) for p in parts], *land)
    return out[0], out[1], list(out[2:2 + n]), list(out[2 + n:2 + 2 * n]), out[2 + 2 * n]


def _scatter_wait(send_sems, recv_sems, parts, slots, axes, after, *, name):
    n = len(parts)
    geo = _scatter_geo(parts, axes)

    def body(*refs):
        p_refs, q_refs = refs[:n], refs[n:2 * n]
        for cp in _scatter_copies(p_refs, q_refs, geo, refs[2 * n], refs[2 * n + 1]):
            cp.wait_send()
            cp.wait_recv()

    out = pl.pallas_call(
        body, name=name,
        out_shape=[pltpu.HBM(a.shape, a.dtype) for a in (*parts, *slots)],
        in_specs=[HBM_SPEC] * (2 * n) + [SEM_SPEC, SEM_SPEC, ANY],
        out_specs=[HBM_SPEC] * (2 * n),
        input_output_aliases={i: i for i in range(2 * n)},
        compiler_params=SPLIT_COPY,
    )(*parts, *slots, send_sems, recv_sems, after)
    return list(out[:n]), list(out[n:])


def _sibling_swap(grads, *, name):
    n = len(grads)
    out_shape = [jax.ShapeDtypeStruct((g.shape[0], g.shape[1] // 2, g.shape[2]), g.dtype)
                 for g in grads]

    def body(*refs):
        g_refs, o_refs = refs[:n], refs[n:2 * n]
        send_sems, recv_sems = refs[2 * n:]
        x, y, c, _ = _place()
        cps = []
        for i in range(n):
            rh = grads[i].shape[1] // 2
            src = g_refs[i].at[:, pl.ds(pl.multiple_of((1 - c) * rh, 16), rh), :]
            cp = pltpu.make_async_remote_copy(
                src_ref=src, dst_ref=o_refs[i], send_sem=send_sems.at[i], recv_sem=recv_sems.at[i],
                device_id=(x, y, 1 - c), device_id_type=MESH)
            cp.start()
            cps.append(cp)
        for cp in cps:
            cp.wait()

    return pl.pallas_call(
        body, in_specs=[ANY] * n, out_specs=[ANY] * n, out_shape=out_shape,
        scratch_shapes=[pltpu.SemaphoreType.DMA((n,)), pltpu.SemaphoreType.DMA((n,))],
        name=name)(*grads)


def _swap_copies(g_refs, r_refs, shapes, send_sems, recv_sems):
    x, y, c, _ = _place()
    cps = []
    for i, shape in enumerate(shapes):
        rh = shape[1] // 2
        src = g_refs[i].at[:, pl.ds(pl.multiple_of((1 - c) * rh, 16), rh), :]
        cps.append(pltpu.make_async_remote_copy(
            src_ref=src, dst_ref=r_refs[i], send_sem=send_sems.at[i], recv_sem=recv_sems.at[i],
            device_id=(x, y, 1 - c), device_id_type=MESH))
    return cps


def _sibling_swap_start(grads, *, name):
    n = len(grads)
    shapes = [g.shape for g in grads]
    lands = [pltpu.HBM((s[0], s[1] // 2, s[2]), g.dtype) for s, g in zip(shapes, grads)]

    def body(*refs):
        g_refs, r_refs = refs[:n], refs[n:2 * n]
        token = refs[4 * n + 2]
        for cp in _swap_copies(g_refs, r_refs, shapes, refs[2 * n], refs[2 * n + 1]):
            cp.start()
        token[...] = jnp.zeros_like(token)

    land = [pltpu.with_memory_space_constraint(lax.empty(s.inner_aval.shape, s.inner_aval.dtype), pltpu.HBM)
            for s in lands]
    out = pl.pallas_call(
        body, name=name,
        out_shape=(pltpu.SemaphoreType.DMA((n,)), pltpu.SemaphoreType.DMA((n,)),
                   *[pltpu.HBM(g.shape, g.dtype) for g in grads], *lands, TOKEN),
        in_specs=[HBM_SPEC] * (2 * n),
        out_specs=(SEM_SPEC, SEM_SPEC, *[HBM_SPEC] * (2 * n), pl.BlockSpec(memory_space=pltpu.VMEM)),
        input_output_aliases={i: 2 + i for i in range(2 * n)},
        compiler_params=SPLIT_COPY,
    )(*[pltpu.with_memory_space_constraint(g, pltpu.HBM) for g in grads], *land)
    return out[0], out[1], list(out[2:2 + n]), list(out[2 + n:2 + 2 * n]), out[2 + 2 * n]


def _sibling_swap_wait(send_sems, recv_sems, grads, lands, after, *, name):
    n = len(grads)
    shapes = [g.shape for g in grads]

    def body(*refs):
        g_refs, r_refs = refs[:n], refs[n:2 * n]
        for cp in _swap_copies(g_refs, r_refs, shapes, refs[2 * n], refs[2 * n + 1]):
            cp.wait_send()
            cp.wait_recv()

    out = pl.pallas_call(
        body, name=name,
        out_shape=[pltpu.HBM(a.shape, a.dtype) for a in (*grads, *lands)],
        in_specs=[HBM_SPEC] * (2 * n) + [SEM_SPEC, SEM_SPEC, ANY],
        out_specs=[HBM_SPEC] * (2 * n),
        input_output_aliases={i: i for i in range(2 * n)},
        compiler_params=SPLIT_COPY,
    )(*grads, *lands, send_sems, recv_sems, after)
    return list(out[:n]), list(out[n:])


def _pair_add(g3, rx, c_arr, *, out_dtype, name):
    nb, rows, cols = g3.shape
    rh = rows // 2
    tr = next(t for t in (128, 64, 32, 16) if rh % t == 0)
    nt = rh // tr

    def body(c_ref, g_ref, r_ref, o_ref):
        o_ref[...] = (g_ref[...] + r_ref[...]).astype(out_dtype)

    return pl.pallas_call(
        body,
        grid_spec=pltpu.PrefetchScalarGridSpec(
            num_scalar_prefetch=1, grid=(nb, nt),
            in_specs=[pl.BlockSpec((None, tr, cols), lambda b, i, c_ref: (b, c_ref[0] * nt + i, 0)),
                      pl.BlockSpec((None, tr, cols), lambda b, i, c_ref: (b, i, 0))],
            out_specs=pl.BlockSpec((None, tr, cols), lambda b, i, c_ref: (b, i, 0))),
        out_shape=jax.ShapeDtypeStruct((nb, rh, cols), out_dtype),
        compiler_params=_cp("parallel", "parallel"), name=name)(c_arr, g3, rx)


def _sum_slots(q, *, name):
    ns, rows, cols = q.shape
    tr = next(t for t in (128, 64, 32, 16, 8) if rows % t == 0)

    def body(q_ref, o_ref):
        acc = q_ref[0].astype(F32)
        for k in range(1, ns):
            acc = acc + q_ref[k].astype(F32)
        o_ref[...] = acc

    return pl.pallas_call(
        body, grid=(rows // tr,),
        in_specs=[pl.BlockSpec((ns, tr, cols), lambda i: (0, i, 0))],
        out_specs=_rspec(tr, cols),
        out_shape=jax.ShapeDtypeStruct((rows, cols), F32),
        compiler_params=_cp("parallel"), name=name)(q)


def _sum_chips(q, p, place, axis, *, name):
    _, rh, cw = q.shape
    tr = next(t for t in (128, 64, 32, 16) if rh % t == 0)
    nt = rh // tr

    def body(p_ref, *refs):
        q_refs, own_ref, o_ref = refs[:N_CHIPS], refs[N_CHIPS], refs[N_CHIPS + 1]
        chip = p_ref[0]
        acc = jnp.where(chip == 0, own_ref[...], q_refs[0][...]).astype(F32)
        for k in range(1, N_CHIPS):
            acc = acc + jnp.where(chip == k, own_ref[...], q_refs[k][...]).astype(F32)
        o_ref[...] = acc

    def slot_spec(k):
        return pl.BlockSpec((None, tr, cw),
                            lambda i, p_ref: (jnp.where(p_ref[0] == k, (k + 1) % N_CHIPS, k), i, 0))

    if axis == 1:
        own_spec = pl.BlockSpec((None, tr, cw), lambda i, p_ref: (0, i, p_ref[0]))
    else:
        own_spec = pl.BlockSpec((None, tr, cw), lambda i, p_ref: (p_ref[0], i, 0))
    return pl.pallas_call(
        body,
        grid_spec=pltpu.PrefetchScalarGridSpec(
            num_scalar_prefetch=1, grid=(nt,),
            in_specs=[slot_spec(k) for k in range(N_CHIPS)] + [own_spec],
            out_specs=pl.BlockSpec((tr, cw), lambda i, p_ref: (p_ref[1] * nt + i, 0))),
        out_shape=jax.ShapeDtypeStruct((2 * rh, cw), F32),
        compiler_params=_cp("parallel"), name=name)(place, *([q] * N_CHIPS), p)


def _sibling_share(shards, *, name):
    n = len(shards)

    def body(*refs):
        o_refs = refs[n:2 * n]
        send_sems, recv_sems = refs[2 * n:]
        x, y, c, _ = _place()
        cps = []
        for i in range(n):
            rh = shards[i].shape[0] // 2
            mine = o_refs[i].at[pl.ds(pl.multiple_of(c * rh, 8), rh), :]
            cp = pltpu.make_async_remote_copy(
                src_ref=mine, dst_ref=mine, send_sem=send_sems.at[i], recv_sem=recv_sems.at[i],
                device_id=(x, y, 1 - c), device_id_type=MESH)
            cp.start()
            cps.append(cp)
        for i in range(n):
            rh = shards[i].shape[0] // 2
            theirs = o_refs[i].at[pl.ds(pl.multiple_of((1 - c) * rh, 8), rh), :]
            pltpu.make_async_remote_copy(
                src_ref=theirs, dst_ref=theirs, send_sem=send_sems.at[i], recv_sem=recv_sems.at[i],
                device_id=(x, y, c), device_id_type=MESH).wait_recv()
        for cp in cps:
            cp.wait_send()

    return pl.pallas_call(
        body, in_specs=[ANY] * n, out_specs=[ANY] * n,
        out_shape=[jax.ShapeDtypeStruct(h.shape, h.dtype) for h in shards],
        input_output_aliases={i: i for i in range(n)},
        scratch_shapes=[pltpu.SemaphoreType.DMA((n,)), pltpu.SemaphoreType.DMA((n,))],
        name=name)(*shards)


def _gather_all(blk, *, name, after=None):
    rows, cols = blk.shape
    extra = [] if after is None else [after]

    def body(x_ref, *refs):
        out_ref, send_sems, recv_sems, local_sem = refs[len(extra):]
        x, y, c = lax.axis_index("x"), lax.axis_index("y"), lax.axis_index("c")
        me = 4 * x + 2 * y + c
        mine = pltpu.make_async_copy(x_ref, out_ref.at[me], local_sem)
        mine.start()
        cps = []
        for k in range(1, N_DEV):
            tx = (1 - x) if (k >> 2) & 1 else x
            ty = (1 - y) if (k >> 1) & 1 else y
            tc = (1 - c) if k & 1 else c
            cp = pltpu.make_async_remote_copy(
                src_ref=x_ref, dst_ref=out_ref.at[me], send_sem=send_sems.at[k - 1],
                recv_sem=recv_sems.at[k - 1], device_id=(tx, ty, tc), device_id_type=MESH)
            cp.start()
            cps.append(cp)
        for k in range(1, N_DEV):
            tx = (1 - x) if (k >> 2) & 1 else x
            ty = (1 - y) if (k >> 1) & 1 else y
            tc = (1 - c) if k & 1 else c
            got = out_ref.at[4 * tx + 2 * ty + tc]
            pltpu.make_async_remote_copy(
                src_ref=got, dst_ref=got, send_sem=send_sems.at[k - 1], recv_sem=recv_sems.at[k - 1],
                device_id=(x, y, c), device_id_type=MESH).wait_recv()
        for cp in cps:
            cp.wait_send()
        mine.wait()

    vm = pl.BlockSpec(memory_space=pltpu.VMEM)
    return pl.pallas_call(
        body, in_specs=[vm] + [ANY] * len(extra), out_specs=vm,
        out_shape=jax.ShapeDtypeStruct((N_DEV, rows, cols), blk.dtype),
        scratch_shapes=[pltpu.SemaphoreType.DMA((N_DEV - 1,)), pltpu.SemaphoreType.DMA((N_DEV - 1,)),
                        pltpu.SemaphoreType.DMA],
        name=name)(blk, *extra)


def _as_rows(a):
    flat = a.reshape(-1)
    n = flat.shape[0]
    rows = -(-n // (8 * LANES)) * 8
    return jnp.pad(flat, (0, rows * LANES - n)).reshape(rows, LANES)


def _from_rows(p, shape):
    n = int(np.prod(shape))
    return p.reshape(-1)[:n].reshape(shape)


WEIGHT_AXES = (1, 1, 1, 0, 1, 0)
WIRE = BF16


def kernel(x, meta_tokens, w_in, w_na_out, w_hg_out, w_o, w_up, w_down, norm_mix, norm_mlp, norm_final, hg_norm, na_rpb, hg_lb_logits, loss_target, m_meta_tokens, m_w_in, m_w_na_out, m_w_hg_out, m_w_o, m_w_up, m_w_down, m_norm_mix, m_norm_mlp, m_norm_final, m_hg_norm, m_na_rpb, m_hg_lb_logits, v_meta_tokens, v_w_in, v_w_na_out, v_w_hg_out, v_w_o, v_w_up, v_w_down, v_norm_mix, v_norm_mlp, v_norm_final, v_hg_norm, v_na_rpb, v_hg_lb_logits):
    xi, yi, ci = lax.axis_index("x"), lax.axis_index("y"), lax.axis_index("c")
    chip = 2 * xi + yi
    d = x.shape[-1]
    dshard = meta_tokens.shape[1]
    hgw = hg_norm.shape[1]
    lbs = hg_lb_logits.shape[2]
    big = [w_in[0], w_na_out[0], w_hg_out[0], w_o[0], w_up[0], w_down[0]]
    big_m = [m_w_in[0], m_w_na_out[0], m_w_hg_out[0], m_w_o[0], m_w_up[0], m_w_down[0]]
    big_v = [v_w_in[0], v_w_na_out[0], v_w_hg_out[0], v_w_o[0], v_w_up[0], v_w_down[0]]

    place = jnp.stack([chip, ci]).astype(jnp.int32)
    own_w = [_cast_into_full(w, ax, place, name=f"cast_shard_{i}")
             for i, (w, ax) in enumerate(zip(big, WEIGHT_AXES))]
    in_axes, rest_axes = WEIGHT_AXES[:1], WEIGHT_AXES[1:]
    small_in = jnp.concatenate([_as_rows(meta_tokens), _as_rows(hg_lb_logits)], axis=0)
    small_all = _gather_all(small_in, name="gather_small_params")[0::2]
    in_send, in_recv, in_bufs, in_token = _allgather_start(own_w[:1], in_axes, small_all,
                                                           name="weight_allgather_in_start")
    ag_send, ag_recv, ag_bufs, ag_token = _allgather_start(own_w[1:], rest_axes, in_token,
                                                           name="weight_allgather_rest_start")

    def first_weight(after):
        got = _allgather_wait(in_send, in_recv, in_bufs, in_axes, after,
                              name="weight_allgather_in_wait")
        return _allgather_forward(got, in_axes, name="weight_allgather_in_forward")[0]

    def rest_weights(after):
        got = _allgather_wait(ag_send, ag_recv, ag_bufs, rest_axes, after,
                              name="weight_allgather_rest_wait")
        return _allgather_forward(got, rest_axes, name="weight_allgather_rest_forward")

    n_meta_rows = N_META * dshard // LANES
    meta_full = (small_all[:, :n_meta_rows].reshape(N_CHIPS, N_META, dshard)
                 .transpose(1, 0, 2).reshape(N_META, d))
    lbl_full = (small_all[:, n_meta_rows:].reshape(N_CHIPS, -1)[:, :4 * lbs]
                .reshape(N_CHIPS, 2, 2, lbs).transpose(1, 2, 0, 3).reshape(2, 2, N_CHIPS * lbs))
    lb = jax.nn.softmax(lbl_full, axis=1)[:, 0]

    c_arr = ci.reshape(1).astype(jnp.int32)

    def by_chip(dws, axes):
        return [g.reshape(1, *g.shape) if ax == 1
                else g.reshape(N_CHIPS, g.shape[0] // N_CHIPS, g.shape[1]) for g, ax in zip(dws, axes)]

    flying = {}

    def scatter(tag, axes, g3, rx):
        parts = [_pair_add(g, r, c_arr, out_dtype=WIRE, name=f"grad_pair_add_{tag}_{i}")
                 for i, (g, r) in enumerate(zip(g3, rx))]
        send, recv, parts, slots, token = _scatter_start(parts, axes,
                                                         name=f"grad_scatter_{tag}_start")
        flying[tag] = (send, recv, parts, slots)
        return token

    def swap_rest(dws):
        send, recv, g3, lands, token = _sibling_swap_start(by_chip(dws, rest_axes),
                                                           name="grad_sibling_swap_rest_start")
        flying["swap"] = (send, recv, g3, lands)
        return token

    def scatter_rest(after):
        g3, rx = _sibling_swap_wait(*flying["swap"], after, name="grad_sibling_swap_rest_wait")
        return scatter("rest", rest_axes, g3, rx)

    def scatter_in(dw_in):
        g3 = by_chip([dw_in], in_axes)
        return scatter("in", in_axes, g3, _sibling_swap(g3, name="grad_sibling_swap_in"))

    def landed(tag, axes, after):
        parts, slots = _scatter_wait(*flying[tag], axes, after, name=f"grad_scatter_{tag}_wait")
        return [_sum_chips(q, p, place, ax, name=f"grad_sum_chips_{tag}_{i}")
                for i, (q, p, ax) in enumerate(zip(slots, parts, axes))]

    (loss, dx, dmeta, *_, dg_mix, dg_mlp, dg_fin, d_gain, d_rpb, d_lb) = _local_step(
        x[0], loss_target[0], meta_full, first_weight, rest_weights,
        norm_mix + ag_token[0:1, 0:1], norm_mlp, norm_final.reshape(1, d), hg_norm, na_rpb[0], lb,
        swap_rest, scatter_rest, scatter_in)

    halves_rest = landed("rest", rest_axes, dx)
    halves_in = landed("in", in_axes, halves_rest[-1])
    g_big = _sibling_share(halves_in + halves_rest, name="grad_sibling_share")

    d_rpb_c = d_rpb[:, :2 * NA_WIN_W - 1]
    small_g = [dmeta, dg_mix, dg_mlp, dg_fin, d_gain, d_rpb_c, d_lb, loss]
    packed = jnp.concatenate([_as_rows(a) for a in small_g], axis=0)
    total = _sum_slots(_gather_all(packed, after=halves_in[0], name="gather_small_grads"),
                       name="sum_small_grads")
    offs = np.cumsum([0] + [_as_rows(a).shape[0] for a in small_g])
    take = lambda i, shape: _from_rows(total[offs[i]:offs[i + 1]], shape)
    g_meta_full = take(0, (N_META, d))
    g_norm_mix, g_norm_mlp = take(1, (1, d)), take(2, (1, d))
    g_norm_final = take(3, (d,))
    g_hg_norm = take(4, (1, hgw))
    g_rpb = take(5, na_rpb.shape)
    g_lb = take(6, (2, hgw))
    loss_total = take(7, (1, LANES))[0, 0]
    g_meta = lax.dynamic_slice_in_dim(g_meta_full, chip * dshard, dshard, axis=1)
    dl0 = lb * (1.0 - lb) * g_lb
    g_lbl_full = jnp.stack([dl0, -dl0], axis=1)
    g_lbl = lax.dynamic_slice_in_dim(g_lbl_full, chip * lbs, lbs, axis=2)

    big_out = [_adamw(w, g, m, v, name=f"adamw_{i}")
               for i, (w, g, m, v) in enumerate(zip(big, g_big, big_m, big_v))]
    small_w = [meta_tokens, norm_mix, norm_mlp, norm_final, hg_norm, na_rpb, hg_lb_logits]
    small_gr = [g_meta, g_norm_mix, g_norm_mlp, g_norm_final, g_hg_norm, g_rpb, g_lbl]
    small_m = [m_meta_tokens, m_norm_mix, m_norm_mlp, m_norm_final, m_hg_norm, m_na_rpb, m_hg_lb_logits]
    small_v = [v_meta_tokens, v_norm_mix, v_norm_mlp, v_norm_final, v_hg_norm, v_na_rpb, v_hg_lb_logits]
    pk = lambda lst: jnp.concatenate([_as_rows(a) for a in lst], axis=0)
    sd, sm, sv = _adamw(pk(small_w), pk(small_gr), pk(small_m), pk(small_v), name="adamw_small")
    soffs = np.cumsum([0] + [_as_rows(a).shape[0] for a in small_w])
    unpk = lambda p: [_from_rows(p[soffs[i]:soffs[i + 1]], small_w[i].shape) for i in range(len(small_w))]
    sd, sm, sv = unpk(sd), unpk(sm), unpk(sv)

    def order(bigs, smalls):
        return [smalls[0]] + [b.reshape(1, *b.shape) for b in bigs] + smalls[1:]

    grads = order(g_big, small_gr)
    deltas = order([o[0] for o in big_out], sd)
    new_m = order([o[1] for o in big_out], sm)
    new_v = order([o[2] for o in big_out], sv)
    return (loss_total, dx.reshape(1, *dx.shape), *grads, *deltas, *new_m, *new_v)
```

```python
import functools

import numpy as np
import jax
import jax.numpy as jnp
from jax import lax
from jax.experimental import pallas as pl
from jax.experimental.pallas import tpu as pltpu

F32 = jnp.float32
BF16 = jnp.bfloat16
HIGHEST = lax.Precision.HIGHEST

GRID_W = 64
N_META = 16
EPS = 1e-6
NA_HEAD_DIM = 64
NA_WIN_H = 8
NA_WIN_W = 16
HG_DK = 128
HG_CHUNK = 16
LANES = 128
ROW_ALIGN = 128
VMEM_LIMIT = 48 * 1024 * 1024

ADAM_LR = 0.001
ADAM_B1 = 0.9
ADAM_B2 = 0.999
ADAM_EPS = 1e-08
ADAM_WD = 0.01
ADAM_STEP = 10

MESH = pl.DeviceIdType.MESH


def _cp(*sem):
    return pltpu.CompilerParams(dimension_semantics=sem, vmem_limit_bytes=VMEM_LIMIT)


def _sigmoid(x):
    return 1.0 / (1.0 + jnp.exp(-x))


def _dot(a, b, dims, precision=None):
    return lax.dot_general(a, b, (dims, ((), ())), preferred_element_type=F32, precision=precision)


def _nn(a, b, **kw):
    return _dot(a, b, ((1,), (0,)), **kw)


def _nt(a, b, **kw):
    return _dot(a, b, ((1,), (1,)), **kw)


def _tn(a, b, **kw):
    return _dot(a, b, ((0,), (0,)), **kw)


def _matmul(a, b, *, ta=False, tb=False, tm=None, tn=None, tk=None, out_dtype=F32, name,
            precision=None, after=None, epilogue=None, tiles=(), out_dtypes=None):
    extra = [] if after is None else [after]
    single = out_dtypes is None
    if single:
        out_dtypes = (out_dtype,)
    n_t, n_o = len(tiles), len(out_dtypes)
    if ta:
        kdim, m = a.shape
    else:
        m, kdim = a.shape
    if tb:
        n, k2 = b.shape
    else:
        k2, n = b.shape
    assert kdim == k2, (a.shape, b.shape, ta, tb)
    if tm is None:
        if ta:
            tm = next(t for t in (1024, 512, 256, 128, m) if m % t == 0)
        else:
            tm = m // 2 if (m // 2) % 16 == 0 and m > 512 else m
    if tn is None:
        wide = (1024,) if not ta and len(tiles) <= 1 else ()
        tn = next(t for t in (*wide, 512, 256, 128, n) if n % t == 0)
    if tk is None:
        tk = kdim if ta else next(t for t in (1024, 512, 256, 128, kdim) if kdim % t == 0)
    assert m % tm == 0 and n % tn == 0 and kdim % tk == 0, (m, n, kdim, tm, tn, tk)
    nk = kdim // tk
    op_dtype = F32 if precision is not None else BF16

    def body(a_ref, b_ref, *refs):
        t_refs = refs[:n_t]
        o_refs = refs[n_t + len(extra):n_t + len(extra) + n_o]
        av = a_ref[...].astype(op_dtype)
        bv = b_ref[...].astype(op_dtype)
        dims = ((0 if ta else 1,), (1 if tb else 0,))
        part = _dot(av, bv, dims, precision=precision)

        def finish(acc):
            outs = (acc,) if epilogue is None else epilogue(acc, *[t[...] for t in t_refs])
            for o_ref, val in zip(o_refs, outs):
                o_ref[...] = val.astype(o_ref.dtype)

        if nk == 1:
            finish(part)
            return
        acc_ref = refs[-1]
        kk = pl.program_id(2)

        @pl.when(kk == 0)
        def _():
            acc_ref[...] = part

        @pl.when((kk > 0) & (kk < nk - 1))
        def _():
            acc_ref[...] += part

        @pl.when(kk == nk - 1)
        def _():
            finish(acc_ref[...] + part)

    a_spec = (pl.BlockSpec((tk, tm), lambda i, j, k: (k, i)) if ta
              else pl.BlockSpec((tm, tk), lambda i, j, k: (i, k)))
    b_spec = (pl.BlockSpec((tn, tk), lambda i, j, k: (j, k)) if tb
              else pl.BlockSpec((tk, tn), lambda i, j, k: (k, j)))
    for _, off in tiles:
        assert off % tn == 0, (off, tn)
    t_specs = [pl.BlockSpec((tm, tn), functools.partial(lambda i, j, k, o: (i, o + j), o=off // tn))
               for _, off in tiles]
    o_spec = pl.BlockSpec((tm, tn), lambda i, j, k: (i, j))
    outs = pl.pallas_call(
        body,
        grid=(m // tm, n // tn, nk),
        in_specs=[a_spec, b_spec] + t_specs + [pl.BlockSpec(memory_space=pl.ANY)] * len(extra),
        out_specs=[o_spec] * n_o,
        out_shape=[jax.ShapeDtypeStruct((m, n), dt) for dt in out_dtypes],
        scratch_shapes=[pltpu.VMEM((tm, tn), F32)] if nk > 1 else [],
        compiler_params=_cp("parallel", "parallel", "arbitrary"),
        name=name,
    )(a, b, *[t for t, _ in tiles], *extra)
    return outs[0] if single else outs


def _rspec(tr, w, cb=0):
    return pl.BlockSpec((tr, w), lambda i: (i, cb))


def _fspec(shape):
    nd = len(shape)
    return pl.BlockSpec(shape, lambda i: (0,) * nd)


def _row_tile(lp):
    return ROW_ALIGN if lp % ROW_ALIGN == 0 else lp


def _rmsnorm_fwd(x, g, *, name):
    lp, d = x.shape
    tr = _row_tile(lp)

    def body(x_ref, g_ref, o_ref):
        xv = x_ref[...]
        r = lax.rsqrt(jnp.mean(xv * xv, axis=-1, keepdims=True) + EPS)
        o_ref[...] = (xv * r * g_ref[...]).astype(BF16)

    return pl.pallas_call(
        body, grid=(lp // tr,),
        in_specs=[_rspec(tr, d), _fspec((1, d))],
        out_specs=_rspec(tr, d),
        out_shape=jax.ShapeDtypeStruct((lp, d), BF16),
        compiler_params=_cp("parallel"), name=name)(x, g)


def _residual_norm(h, t, g, *, name):
    lp, d = h.shape
    tr = _row_tile(lp)

    def body(h_ref, t_ref, g_ref, h1_ref, m_ref):
        xv = h_ref[...] + t_ref[...]
        h1_ref[...] = xv
        r = lax.rsqrt(jnp.mean(xv * xv, axis=-1, keepdims=True) + EPS)
        m_ref[...] = (xv * r * g_ref[...]).astype(BF16)

    return pl.pallas_call(
        body, grid=(lp // tr,),
        in_specs=[_rspec(tr, d), _rspec(tr, d), _fspec((1, d))],
        out_specs=[_rspec(tr, d), _rspec(tr, d)],
        out_shape=[jax.ShapeDtypeStruct((lp, d), F32), jax.ShapeDtypeStruct((lp, d), BF16)],
        compiler_params=_cp("parallel"), name=name)(h, t, g)


def _rmsnorm_bwd_add(x, g, dy, dres, *, name):
    lp, d = x.shape
    tr = _row_tile(lp)

    def body(x_ref, g_ref, dy_ref, dr_ref, dx_ref, dg_ref):
        @pl.when(pl.program_id(0) == 0)
        def _():
            dg_ref[...] = jnp.zeros_like(dg_ref)

        xv = x_ref[...]
        r = lax.rsqrt(jnp.mean(xv * xv, axis=-1, keepdims=True) + EPS)
        xh = xv * r
        dyv = dy_ref[...]
        dg_ref[...] += jnp.sum(dyv * xh, axis=0, keepdims=True)
        dxh = dyv * g_ref[...]
        dx_ref[...] = dr_ref[...] + r * (dxh - xh * jnp.mean(dxh * xh, axis=-1, keepdims=True))

    return pl.pallas_call(
        body, grid=(lp // tr,),
        in_specs=[_rspec(tr, d), _fspec((1, d)), _rspec(tr, d), _rspec(tr, d)],
        out_specs=[_rspec(tr, d), _fspec((1, d))],
        out_shape=[jax.ShapeDtypeStruct((lp, d), F32), jax.ShapeDtypeStruct((1, d), F32)],
        compiler_params=_cp("arbitrary"), name=name)(x, g, dy, dres)


def _final_loss(h1, t2, g, tgt, *, n_tok, name):
    lp, d = h1.shape
    tr = _row_tile(lp)

    def body(h_ref, t_ref, g_ref, tg_ref, dh_ref, loss_ref, dg_ref):
        i = pl.program_id(0)

        @pl.when(i == 0)
        def _():
            loss_ref[...] = jnp.zeros_like(loss_ref)
            dg_ref[...] = jnp.zeros_like(dg_ref)

        xv = h_ref[...] + t_ref[...]
        r = lax.rsqrt(jnp.mean(xv * xv, axis=-1, keepdims=True) + EPS)
        xh = xv * r
        gv = g_ref[...]
        row = i * tr + lax.broadcasted_iota(jnp.int32, (tr, 1), 0)
        valid = (row >= N_META) & (row < N_META + n_tok)
        err = jnp.where(valid, xh * gv - tg_ref[...], 0.0)
        loss_ref[...] += jnp.sum(0.5 * err * err) / d
        dy = err / d
        dg_ref[...] += jnp.sum(dy * xh, axis=0, keepdims=True)
        dxh = dy * gv
        dh_ref[...] = r * (dxh - xh * jnp.mean(dxh * xh, axis=-1, keepdims=True))

    return pl.pallas_call(
        body, grid=(lp // tr,),
        in_specs=[_rspec(tr, d), _rspec(tr, d), _fspec((1, d)), _rspec(tr, d)],
        out_specs=[_rspec(tr, d), _fspec((1, LANES)), _fspec((1, d))],
        out_shape=[jax.ShapeDtypeStruct((lp, d), F32), jax.ShapeDtypeStruct((1, LANES), F32),
                   jax.ShapeDtypeStruct((1, d), F32)],
        compiler_params=_cp("arbitrary"), name=name)(h1, t2, g, tgt)


def _hg_out(o_f, o_b, proj, gain, *, col_g, name):
    lp, w = o_f.shape
    tr = _row_tile(lp)
    hh = w // HG_DK

    def body(of_ref, ob_ref, g_ref, gain_ref, y_ref):
        gv = g_ref[...]
        sg = gv * _sigmoid(gv)
        for h in range(hh):
            sl = slice(h * HG_DK, (h + 1) * HG_DK)
            o = of_ref[:, sl] + ob_ref[:, sl]
            r = lax.rsqrt(jnp.mean(o * o, axis=-1, keepdims=True) + EPS)
            y_ref[:, sl] = (o * r * gain_ref[:, sl] * sg[:, sl]).astype(BF16)

    return pl.pallas_call(
        body, grid=(lp // tr,),
        in_specs=[_rspec(tr, w), _rspec(tr, w), _rspec(tr, w, col_g // w), _fspec((1, w))],
        out_specs=_rspec(tr, w),
        out_shape=jax.ShapeDtypeStruct((lp, w), BF16),
        compiler_params=_cp("parallel"), name=name)(o_f, o_b, proj, gain)


def _hg_out_bwd(o_f, o_b, proj, gain, dy, *, col_g, name):
    lp, w = o_f.shape
    tr = _row_tile(lp)
    hh = w // HG_DK

    def body(of_ref, ob_ref, g_ref, gain_ref, dy_ref, do_ref, dg_ref, dgain_ref):
        @pl.when(pl.program_id(0) == 0)
        def _():
            dgain_ref[...] = jnp.zeros_like(dgain_ref)

        for h in range(hh):
            sl = slice(h * HG_DK, (h + 1) * HG_DK)
            gv = g_ref[:, sl]
            s = _sigmoid(gv)
            sg = gv * s
            dsg = s + gv * s * (1.0 - s)
            o = of_ref[:, sl] + ob_ref[:, sl]
            r = lax.rsqrt(jnp.mean(o * o, axis=-1, keepdims=True) + EPS)
            on = o * r
            dyv = dy_ref[:, sl]
            gn = gain_ref[:, sl]
            dgain_ref[:, sl] += jnp.sum(dyv * on * sg, axis=0, keepdims=True)
            dg_ref[:, sl] = (dyv * on * gn * dsg).astype(BF16)
            don = dyv * gn * sg
            do_ref[:, sl] = r * (don - on * jnp.mean(don * on, axis=-1, keepdims=True))

    return pl.pallas_call(
        body, grid=(lp // tr,),
        in_specs=[_rspec(tr, w), _rspec(tr, w), _rspec(tr, w, col_g // w), _fspec((1, w)),
                  _rspec(tr, w)],
        out_specs=[_rspec(tr, w), _rspec(tr, w), _fspec((1, w))],
        out_shape=[jax.ShapeDtypeStruct((lp, w), F32), jax.ShapeDtypeStruct((lp, w), BF16),
                   jax.ShapeDtypeStruct((1, w), F32)],
        compiler_params=_cp("arbitrary"), name=name)(o_f, o_b, proj, gain, dy)


HG_ROWS = 128
HG_HALVES = (1, 2, 4, 8, 16, 32, 64)


def _hg_gates(zq, z, lbv):
    qh = zq * _sigmoid(zq)
    s = _sigmoid(z)
    f = lbv + (1.0 - lbv) * s
    kk = (1.0 - lbv) * _sigmoid(-z)
    return qh, s, f, jnp.log(f), kk


def _block_cumsum(g, pos, suffix):
    x = g
    for k in HG_HALVES:
        if suffix:
            x = x + jnp.where(pos < HG_ROWS - k, pltpu.roll(x, HG_ROWS - k, 0), 0.0)
        else:
            x = x + jnp.where(pos >= k, pltpu.roll(x, k, 0), 0.0)
    return x


def _pair_levels(b, pos, reverse):
    out = []
    first = b
    for m in HG_HALVES:
        if m > 1:
            first = jnp.where((pos & (m - 1)) >= m // 2, pltpu.roll(first, m // 2, 0), first)
        nxt = pltpu.roll(first, HG_ROWS - m, 0)
        upper = (pos & (2 * m - 1)) >= m
        if reverse:
            eq = jnp.where(upper, 0.0, jnp.exp(b - nxt))
            ek = jnp.where(upper, jnp.exp(first - b), 0.0)
        else:
            eq = jnp.where(upper, jnp.exp(b - first), 0.0)
            ek = jnp.where(upper, 0.0, jnp.exp(nxt - b))
        out.append((eq, ek))
    return out


def _pair_masks(mask_ref):
    ri = lax.broadcasted_iota(jnp.int32, (HG_ROWS, HG_ROWS), 0)
    ci = lax.broadcasted_iota(jnp.int32, (HG_ROWS, HG_ROWS), 1)
    for i, m in enumerate(HG_HALVES):
        sh = m.bit_length()
        mask_ref[i] = jnp.where((ri >> sh) == (ci >> sh), 1.0, 0.0)


def _hg_scan_fwd(proj, lb, *, reverse, col_q, col_z, col_i, hh, name):
    lp = proj.shape[0]
    n_blocks = lp // HG_ROWS
    last = 0 if reverse else HG_ROWS - 1

    def body(q_ref, z_ref, i_ref, lb_ref, o_ref, st_ref, mask_ref):
        lbv = lb_ref[...]
        pos = lax.broadcasted_iota(jnp.int32, (HG_ROWS, 1), 0)
        ri = lax.broadcasted_iota(jnp.int32, (HG_ROWS, HG_ROWS), 0)
        ci = lax.broadcasted_iota(jnp.int32, (HG_ROWS, HG_ROWS), 1)
        _pair_masks(mask_ref)

        def block(bi, st):
            bb = (n_blocks - 1 - bi) if reverse else bi
            r0 = pl.multiple_of(bb * HG_ROWS, HG_ROWS)
            v16 = i_ref[pl.ds(r0, HG_ROWS), :].astype(BF16)
            qh, _, _, g, kk = _hg_gates(q_ref[pl.ds(r0, HG_ROWS), :], z_ref[pl.ds(r0, HG_ROWS), :],
                                        lbv)
            b = _block_cumsum(g, pos, reverse)
            bl = b[last:last + 1, :]
            qe = (qh * jnp.exp(b)).astype(BF16)
            kd = (kk * jnp.exp(bl - b)).astype(BF16)
            a = jnp.where(ri == ci, jnp.sum(qh * kk, axis=1, keepdims=True), 0.0)
            for i, (eq, ek) in enumerate(_pair_levels(b, pos, reverse)):
                a = a + mask_ref[i] * _nt((qh * eq).astype(BF16), (kk * ek).astype(BF16))
            st_ref[bb] = st
            o_ref[pl.ds(r0, HG_ROWS), :] = _nn(a.astype(BF16), v16) + _nt(qe, st.astype(BF16))
            return jnp.exp(bl) * st + _tn(v16, kd)

        lax.fori_loop(0, n_blocks, block, jnp.zeros((HG_DK, HG_DK), F32))

    cspec = lambda col: pl.BlockSpec((lp, HG_DK), lambda h: (0, col // HG_DK + h))
    return pl.pallas_call(
        body, grid=(hh,),
        in_specs=[cspec(col_q), cspec(col_z), cspec(col_i),
                  pl.BlockSpec((None, 1, HG_DK), lambda h: (h, 0, 0))],
        out_specs=[pl.BlockSpec((lp, HG_DK), lambda h: (0, h)),
                   pl.BlockSpec((None, n_blocks, HG_DK, HG_DK), lambda h: (h, 0, 0, 0))],
        out_shape=[jax.ShapeDtypeStruct((lp, hh * HG_DK), F32),
                   jax.ShapeDtypeStruct((hh, n_blocks, HG_DK, HG_DK), F32)],
        scratch_shapes=[pltpu.VMEM((len(HG_HALVES), HG_ROWS, HG_ROWS), F32)],
        compiler_params=_cp("parallel"), name=name)(proj, proj, proj, lb)


def _hg_scan_bwd(proj, lb, states, do, *, reverse, col_q, col_z, col_i, hh, name):
    lp = proj.shape[0]
    n_blocks = lp // HG_ROWS
    last = 0 if reverse else HG_ROWS - 1

    def body(q_ref, z_ref, i_ref, lb_ref, st_ref, do_ref, dq_ref, dz_ref, dv_ref, dlb_ref, mask_ref):
        lbv = lb_ref[...]
        pos = lax.broadcasted_iota(jnp.int32, (HG_ROWS, 1), 0)
        ri = lax.broadcasted_iota(jnp.int32, (HG_ROWS, HG_ROWS), 0)
        ci = lax.broadcasted_iota(jnp.int32, (HG_ROWS, HG_ROWS), 1)
        _pair_masks(mask_ref)

        def block(bi, carry):
            dst, dlb = carry
            bb = bi if reverse else (n_blocks - 1 - bi)
            r0 = pl.multiple_of(bb * HG_ROWS, HG_ROWS)
            zq = q_ref[pl.ds(r0, HG_ROWS), :]
            v16 = i_ref[pl.ds(r0, HG_ROWS), :].astype(BF16)
            do16 = do_ref[pl.ds(r0, HG_ROWS), :].astype(BF16)
            qh, s, f, g, kk = _hg_gates(zq, z_ref[pl.ds(r0, HG_ROWS), :], lbv)
            b = _block_cumsum(g, pos, reverse)
            bl = b[last:last + 1, :]
            eb = jnp.exp(b)
            ebl = jnp.exp(bl - b)
            decay = jnp.exp(bl)
            qe16 = (qh * eb).astype(BF16)
            kd16 = (kk * ebl).astype(BF16)
            st = st_ref[bb]
            st16, dst16 = st.astype(BF16), dst.astype(BF16)
            same_row = ri == ci
            da = _nt(do16, v16)
            da_diag = jnp.sum(jnp.where(same_row, da, 0.0), axis=1, keepdims=True)
            dq_state = eb * _nn(do16, st16)
            dk_state = ebl * _nn(v16, dst16)
            dq = dq_state + da_diag * kk
            dk = dk_state + da_diag * qh
            dbl = (decay * jnp.sum(st * dst, axis=0, keepdims=True)
                   + jnp.sum(kk * dk_state, axis=0, keepdims=True))
            db = qh * dq_state - kk * dk_state + jnp.where(pos == last, dbl, 0.0)
            a = jnp.where(same_row, jnp.sum(qh * kk, axis=1, keepdims=True), 0.0)
            for i, (eq, ek) in enumerate(_pair_levels(b, pos, reverse)):
                same = mask_ref[i]
                q16, k16 = (qh * eq).astype(BF16), (kk * ek).astype(BF16)
                a = a + same * _nt(q16, k16)
                da16 = (same * da).astype(BF16)
                gq, gk = _nn(da16, k16), _tn(da16, q16)
                dq = dq + eq * gq
                dk = dk + ek * gk
                db = db + (q16.astype(F32) * gq - k16.astype(F32) * gk)
            dg = _block_cumsum(db, pos, not reverse)
            df = dg / f - dk
            sq = _sigmoid(zq)
            dq_ref[pl.ds(r0, HG_ROWS), :] = dq * (sq + zq * sq * (1.0 - sq))
            dz_ref[pl.ds(r0, HG_ROWS), :] = df * (1.0 - lbv) * s * (1.0 - s)
            dv_ref[pl.ds(r0, HG_ROWS), :] = _nt(kd16, dst16) + _tn(a.astype(BF16), do16)
            return (decay * dst + _tn(do16, qe16),
                    dlb + jnp.sum(df * (1.0 - s), axis=0, keepdims=True))

        _, dlb = lax.fori_loop(0, n_blocks, block,
                               (jnp.zeros((HG_DK, HG_DK), F32), jnp.zeros((1, HG_DK), F32)))
        dlb_ref[...] = dlb

    cspec = lambda col: pl.BlockSpec((lp, HG_DK), lambda h: (0, col // HG_DK + h))
    ospec = pl.BlockSpec((lp, HG_DK), lambda h: (0, h))
    sds = jax.ShapeDtypeStruct((lp, hh * HG_DK), F32)
    return pl.pallas_call(
        body, grid=(hh,),
        in_specs=[cspec(col_q), cspec(col_z), cspec(col_i),
                  pl.BlockSpec((None, 1, HG_DK), lambda h: (h, 0, 0)),
                  pl.BlockSpec((None, n_blocks, HG_DK, HG_DK), lambda h: (h, 0, 0, 0)),
                  ospec],
        out_specs=[ospec, ospec, ospec, pl.BlockSpec((None, 1, HG_DK), lambda h: (h, 0, 0))],
        out_shape=[sds, sds, sds, jax.ShapeDtypeStruct((hh, 1, HG_DK), F32)],
        scratch_shapes=[pltpu.VMEM((len(HG_HALVES), HG_ROWS, HG_ROWS), F32)],
        compiler_params=_cp("parallel"), name=name)(proj, proj, proj, lb, states, do)


def _na_rows(r, rows):
    rs = jnp.clip(r - NA_WIN_H // 2, 0, rows - NA_WIN_H)
    i0 = rs - r + (NA_WIN_H - 1)
    q0 = pl.multiple_of(N_META + GRID_W * r, 16)
    k0 = pl.multiple_of(N_META + GRID_W * rs, 16)
    return i0, q0, k0


def _na_scores(q16, k16, km16, tb_ref, i0, scale):
    s = _nt(q16, k16) * scale
    bias = jnp.concatenate([tb_ref[i0 + j] for j in range(NA_WIN_H)], axis=1)
    return s + bias, _nt(q16, km16) * scale


NA_HB = LANES // NA_HEAD_DIM


def _na_head_lanes():
    lane = lax.broadcasted_iota(jnp.int32, (1, LANES), 1)
    return [lane // NA_HEAD_DIM == h for h in range(NA_HB)]


def _na_only(mask, x):
    return jnp.where(mask, x, jnp.zeros_like(x))


def _na_fwd(proj, tb, *, n_tok, nh, name):
    lp = proj.shape[0]
    dh, hb = NA_HEAD_DIM, NA_HB
    naw = nh * dh
    rows = n_tok // GRID_W
    scale = dh ** -0.5
    kw = NA_WIN_H * GRID_W

    def body(q_ref, k_ref, v_ref, tb_ref, o_ref, lse_ref, q16_ref, k16_ref, v16_ref):
        o_ref[...] = jnp.zeros_like(o_ref)
        lse_ref[...] = jnp.zeros_like(lse_ref)
        q16_ref[...] = q_ref[...].astype(BF16)
        k16_ref[...] = k_ref[...].astype(BF16)
        v16_ref[...] = v_ref[...].astype(BF16)
        heads = _na_head_lanes()
        km = k16_ref[0:N_META, :]
        vm = v16_ref[0:N_META, :]
        qm = q16_ref[0:N_META, :]
        o_m = None
        for h in range(hb):
            s = _nt(_na_only(heads[h], qm), km) * scale
            m = jnp.max(s, axis=1, keepdims=True)
            p = jnp.exp(s - m)
            l = jnp.sum(p, axis=1, keepdims=True)
            o_h = _nn(p.astype(BF16), vm) / l
            o_m = o_h if o_m is None else jnp.where(heads[h], o_h, o_m)
            lse_ref[h, 0:N_META, :] = m + jnp.log(l)
        o_ref[0:N_META, :] = o_m

        def step(r, carry):
            i0, q0, k0 = _na_rows(r, rows)
            q16 = q16_ref[pl.ds(q0, GRID_W), :]
            k16 = k16_ref[pl.ds(k0, kw), :]
            v16 = v16_ref[pl.ds(k0, kw), :]
            o = None
            for h in range(hb):
                s, sm = _na_scores(_na_only(heads[h], q16), k16, km, tb_ref.at[h], i0, scale)
                m = jnp.maximum(jnp.max(s, axis=1, keepdims=True),
                                jnp.max(sm, axis=1, keepdims=True))
                p = jnp.exp(s - m)
                pm = jnp.exp(sm - m)
                l = jnp.sum(p, axis=1, keepdims=True) + jnp.sum(pm, axis=1, keepdims=True)
                o_h = (_nn(p.astype(BF16), v16) + _nn(pm.astype(BF16), vm)) / l
                o = o_h if o is None else jnp.where(heads[h], o_h, o)
                lse_ref[h, pl.ds(q0, GRID_W), :] = m + jnp.log(l)
            o_ref[pl.ds(q0, GRID_W), :] = o
            return carry

        lax.fori_loop(0, rows, step, 0, unroll=2)

    cblk = lambda col: pl.BlockSpec((lp, LANES), lambda g: (0, col // LANES + g))
    return pl.pallas_call(
        body, grid=(nh // hb,),
        in_specs=[cblk(0), cblk(naw), cblk(2 * naw),
                  pl.BlockSpec((hb, 2 * NA_WIN_H - 1, GRID_W, GRID_W), lambda g: (g, 0, 0, 0))],
        out_specs=[cblk(0), pl.BlockSpec((hb, lp, 1), lambda g: (g, 0, 0))],
        out_shape=[jax.ShapeDtypeStruct((lp, naw), F32), jax.ShapeDtypeStruct((nh, lp, 1), F32)],
        scratch_shapes=[pltpu.VMEM((lp, LANES), BF16)] * 3,
        compiler_params=_cp("parallel"), name=name)(proj, proj, proj, tb)


def _na_bwd(proj, tb, o, lse, do, *, n_tok, nh, name):
    lp = proj.shape[0]
    dh, hb = NA_HEAD_DIM, NA_HB
    naw = nh * dh
    rows = n_tok // GRID_W
    scale = dh ** -0.5
    kw = NA_WIN_H * GRID_W

    def body(q_ref, k_ref, v_ref, tb_ref, o_ref, lse_ref, do_ref, dq_ref, dk_ref, dv_ref, dtb_ref,
             q16_ref, k16_ref, v16_ref):
        dq_ref[...] = jnp.zeros_like(dq_ref)
        dk_ref[...] = jnp.zeros_like(dk_ref)
        dv_ref[...] = jnp.zeros_like(dv_ref)
        dtb_ref[...] = jnp.zeros_like(dtb_ref)
        q16_ref[...] = q_ref[...].astype(BF16)
        k16_ref[...] = k_ref[...].astype(BF16)
        v16_ref[...] = v_ref[...].astype(BF16)
        heads = _na_head_lanes()
        km = k16_ref[0:N_META, :]
        vm = v16_ref[0:N_META, :]
        qm = q16_ref[0:N_META, :]
        dom = do_ref[0:N_META, :]
        prod = dom * o_ref[0:N_META, :]
        dq_m = None
        dkm0 = jnp.zeros((N_META, LANES), F32)
        dvm0 = jnp.zeros((N_META, LANES), F32)
        for h in range(hb):
            q_h = _na_only(heads[h], qm)
            do_h = _na_only(heads[h], dom).astype(BF16)
            p = jnp.exp(_nt(q_h, km) * scale - lse_ref[h, 0:N_META, :])
            delta = jnp.sum(_na_only(heads[h], prod), axis=1, keepdims=True)
            ds = (p * (_nt(do_h, vm) - delta)).astype(BF16)
            dq_h = _nn(ds, km) * scale
            dq_m = dq_h if dq_m is None else jnp.where(heads[h], dq_h, dq_m)
            dkm0 = dkm0 + _tn(ds, q_h) * scale
            dvm0 = dvm0 + _tn(p.astype(BF16), do_h)
        dq_ref[0:N_META, :] = dq_m

        def step(r, carry):
            dkm, dvm = carry
            i0, q0, k0 = _na_rows(r, rows)
            q16 = q16_ref[pl.ds(q0, GRID_W), :]
            k16 = k16_ref[pl.ds(k0, kw), :]
            v16 = v16_ref[pl.ds(k0, kw), :]
            dov = do_ref[pl.ds(q0, GRID_W), :]
            prod = dov * o_ref[pl.ds(q0, GRID_W), :]
            dq = None
            dk = jnp.zeros((kw, LANES), F32)
            dv = jnp.zeros((kw, LANES), F32)
            for h in range(hb):
                q_h = _na_only(heads[h], q16)
                do_h = _na_only(heads[h], dov).astype(BF16)
                s, sm = _na_scores(q_h, k16, km, tb_ref.at[h], i0, scale)
                lse = lse_ref[h, pl.ds(q0, GRID_W), :]
                p = jnp.exp(s - lse)
                pm = jnp.exp(sm - lse)
                delta = jnp.sum(_na_only(heads[h], prod), axis=1, keepdims=True)
                ds = p * (_nt(do_h, v16) - delta)
                dsm = (pm * (_nt(do_h, vm) - delta)).astype(BF16)
                ds16 = ds.astype(BF16)
                dq_h = (_nn(ds16, k16) + _nn(dsm, km)) * scale
                dq = dq_h if dq is None else jnp.where(heads[h], dq_h, dq)
                dk = dk + _tn(ds16, q_h) * scale
                dv = dv + _tn(p.astype(BF16), do_h)
                dkm = dkm + _tn(dsm, q_h) * scale
                dvm = dvm + _tn(pm.astype(BF16), do_h)
                for j in range(NA_WIN_H):
                    dtb_ref[h, i0 + j] += ds[:, j * GRID_W:(j + 1) * GRID_W]
            dq_ref[pl.ds(q0, GRID_W), :] = dq
            dk_ref[pl.ds(k0, kw), :] += dk
            dv_ref[pl.ds(k0, kw), :] += dv
            return dkm, dvm

        dkm, dvm = lax.fori_loop(0, rows, step, (dkm0, dvm0))
        dk_ref[0:N_META, :] += dkm
        dv_ref[0:N_META, :] += dvm

    cblk = lambda col: pl.BlockSpec((lp, LANES), lambda g: (0, col // LANES + g))
    tbs = pl.BlockSpec((hb, 2 * NA_WIN_H - 1, GRID_W, GRID_W), lambda g: (g, 0, 0, 0))
    sds = jax.ShapeDtypeStruct((lp, naw), F32)
    return pl.pallas_call(
        body, grid=(nh // hb,),
        in_specs=[cblk(0), cblk(naw), cblk(2 * naw), tbs, cblk(0),
                  pl.BlockSpec((hb, lp, 1), lambda g: (g, 0, 0)), cblk(0)],
        out_specs=[cblk(0), cblk(0), cblk(0), tbs],
        out_shape=[sds, sds, sds, jax.ShapeDtypeStruct(tb.shape, F32)],
        scratch_shapes=[pltpu.VMEM((lp, LANES), BF16)] * 3,
        compiler_params=_cp("parallel"), name=name)(proj, proj, proj, tb, o, lse, do)


def _rpb_onehot():
    c = np.arange(GRID_W)[:, None]
    w = np.arange(GRID_W)[None, :]
    cs = np.clip(c - NA_WIN_W // 2, 0, GRID_W - NA_WIN_W)
    in_win = (w >= cs) & (w < cs + NA_WIN_W)
    dc = np.clip(w - c, -(NA_WIN_W - 1), NA_WIN_W - 1) + NA_WIN_W - 1
    oh = np.zeros((LANES, GRID_W * GRID_W), np.float32)
    flat = np.arange(GRID_W * GRID_W).reshape(GRID_W, GRID_W)
    oh[dc[in_win], flat[in_win]] = 1.0
    neg = np.where(in_win, 0.0, -1e30).astype(np.float32).reshape(1, -1)
    return oh, neg


def _assemble_dproj(dq_na, dk_na, dv_na, dq_f, dq_b, dz_f, dz_b, dv_f, dv_b, dg, dgn, dgh, *, name):
    lp, naw = dq_na.shape
    hgw = dq_f.shape[1]
    d = dgn.shape[1]
    cols = 3 * naw + 5 * hgw + 2 * d
    tr = _row_tile(lp)

    def body(nq_ref, nk_ref, nv_ref, qf_ref, qb_ref, zf_ref, zb_ref, vf_ref, vb_ref, g_ref, gn_ref,
             gh_ref, o_ref):
        o_ref[:, 0:naw] = nq_ref[...].astype(BF16)
        o_ref[:, naw:2 * naw] = nk_ref[...].astype(BF16)
        o_ref[:, 2 * naw:3 * naw] = nv_ref[...].astype(BF16)
        c0 = 3 * naw
        o_ref[:, c0:c0 + hgw] = (qf_ref[...] + qb_ref[...]).astype(BF16)
        o_ref[:, c0 + hgw:c0 + 2 * hgw] = zf_ref[...].astype(BF16)
        o_ref[:, c0 + 2 * hgw:c0 + 3 * hgw] = zb_ref[...].astype(BF16)
        o_ref[:, c0 + 3 * hgw:c0 + 4 * hgw] = (vf_ref[...] + vb_ref[...]).astype(BF16)
        o_ref[:, c0 + 4 * hgw:c0 + 5 * hgw] = g_ref[...]
        o_ref[:, c0 + 5 * hgw:c0 + 5 * hgw + d] = gn_ref[...]
        o_ref[:, c0 + 5 * hgw + d:] = gh_ref[...]

    hg, na = _rspec(tr, hgw), _rspec(tr, naw)
    return pl.pallas_call(
        body, grid=(lp // tr,),
        in_specs=[na, na, na, hg, hg, hg, hg, hg, hg, hg, _rspec(tr, d), _rspec(tr, d)],
        out_specs=_rspec(tr, cols),
        out_shape=jax.ShapeDtypeStruct((lp, cols), BF16),
        compiler_params=_cp("parallel"), name=name)(dq_na, dk_na, dv_na, dq_f, dq_b, dz_f, dz_b,
                                                    dv_f, dv_b, dg, dgn, dgh)


def _adamw(w, g, m, v, *, name):
    rows, cols = w.shape
    tr = 256 if rows % 256 == 0 else rows

    def body(w_ref, g_ref, m_ref, v_ref, d_ref, mo_ref, vo_ref):
        gv = g_ref[...]
        mn = ADAM_B1 * m_ref[...] + (1.0 - ADAM_B1) * gv
        vn = ADAM_B2 * v_ref[...] + (1.0 - ADAM_B2) * (gv * gv)
        m_hat = mn / (1.0 - ADAM_B1 ** ADAM_STEP)
        v_hat = vn / (1.0 - ADAM_B2 ** ADAM_STEP)
        d_ref[...] = -ADAM_LR * (m_hat / (jnp.sqrt(v_hat) + ADAM_EPS) + ADAM_WD * w_ref[...])
        mo_ref[...] = mn
        vo_ref[...] = vn

    spec = _rspec(tr, cols)
    sds = jax.ShapeDtypeStruct((rows, cols), F32)
    return pl.pallas_call(
        body, grid=(rows // tr,), in_specs=[spec] * 4, out_specs=[spec] * 3, out_shape=[sds] * 3,
        compiler_params=_cp("parallel"), name=name)(w, g, m, v)


def _local_step(x, tgt, meta, first_weight, rest_weights, g_mix, g_mlp, g_fin, hg_gain, rpb, lb,
                early_grads=None, mid_grads=None, late_grad=None):
    n_tok, d = x.shape
    hgw = hg_gain.shape[1]
    nh, hh = rpb.shape[0], hgw // HG_DK
    naw = nh * NA_HEAD_DIM
    l_real = N_META + n_tok
    lp = -(-l_real // ROW_ALIGN) * ROW_ALIGN
    n_chunks = l_real // HG_CHUNK
    pad = lp - l_real
    col_qhg = 3 * naw
    col_zf, col_zb, col_i, col_g = (col_qhg + hgw, col_qhg + 2 * hgw, col_qhg + 3 * hgw,
                                    col_qhg + 4 * hgw)
    col_gate = col_qhg + 5 * hgw

    zpad = jnp.zeros((pad, d), F32)
    h0 = jnp.concatenate([meta, x, zpad], axis=0)
    tgt_p = jnp.concatenate([jnp.zeros((N_META, d), F32), tgt, zpad], axis=0)

    oh_np, neg_np = _rpb_onehot()
    oh = jnp.asarray(oh_np)
    rpb_p = jnp.pad(rpb.reshape(nh * (2 * NA_WIN_H - 1), 2 * NA_WIN_W - 1),
                    ((0, 0), (0, LANES - (2 * NA_WIN_W - 1))))
    tb = _matmul(rpb_p, oh, tm=rpb_p.shape[0], tn=512, tk=LANES, precision=HIGHEST,
                 name="rpb_expand")
    tb = (tb + jnp.asarray(neg_np)).reshape(nh, 2 * NA_WIN_H - 1, GRID_W, GRID_W)

    a = _rmsnorm_fwd(h0, g_mix, name="norm_mix")
    w_in = first_weight(a)
    proj = _matmul(a, w_in, name="mm_in")
    o_na, lse = _na_fwd(proj, tb, n_tok=n_tok, nh=nh, name="na_fwd")
    lb_f = lb[0].reshape(hh, 1, HG_DK)
    lb_b = lb[1].reshape(hh, 1, HG_DK)
    scan_kw = dict(col_q=col_qhg, col_i=col_i, hh=hh)
    o_f, st_f = _hg_scan_fwd(proj, lb_f, reverse=False, col_z=col_zf, name="hg_scan_f", **scan_kw)
    o_b, st_b = _hg_scan_fwd(proj, lb_b, reverse=True, col_z=col_zb, name="hg_scan_b", **scan_kw)
    o_hg = _hg_out(o_f, o_b, proj, hg_gain, col_g=col_g, name="hg_out")
    w_na, w_hg, w_o, w_up, w_down = rest_weights(o_hg)
    y_na = _matmul(o_na, w_na, name="mm_na_out")
    gates = ((proj, col_gate), (proj, col_gate + d))

    def mix_gates(acc, gn, gh, yn):
        return acc, _sigmoid(gn) * yn + _sigmoid(gh) * acc

    def mix_gates_bwd(dmix, gn, gh, yn, yh):
        sn, sh = _sigmoid(gn), _sigmoid(gh)
        return dmix * sn, dmix * sh, dmix * yn * sn * (1.0 - sn), dmix * yh * sh * (1.0 - sh)

    y_hg, mix = _matmul(o_hg, w_hg, name="mm_hg_out", epilogue=mix_gates,
                        tiles=(*gates, (y_na, 0)), out_dtypes=(F32, BF16))
    t1 = _matmul(mix, w_o, name="mm_o")
    h1, mlp_in = _residual_norm(h0, t1, g_mlp, name="resid_norm_mlp")
    u, act = _matmul(mlp_in, w_up, name="mm_up", out_dtypes=(BF16, BF16),
                     epilogue=lambda acc: (acc, jnp.square(jnp.maximum(acc, 0.0))))
    t2 = _matmul(act, w_down, name="mm_down")
    dh2, loss, dg_fin = _final_loss(h1, t2, g_fin, tgt_p, n_tok=n_tok, name="final_loss")

    (du,) = _matmul(dh2, w_down, tb=True, name="mm_down_dx", tiles=((u, 0),), out_dtypes=(BF16,),
                    epilogue=lambda acc, uv: (acc * 2.0 * jnp.maximum(uv, 0.0),))
    dw_down = _matmul(act, dh2, ta=True, name="mm_down_dw")
    dm = _matmul(du, w_up, tb=True, name="mm_up_dx")
    dw_up = _matmul(mlp_in, du, ta=True, name="mm_up_dw")
    dh1, dg_mlp = _rmsnorm_bwd_add(h1, g_mlp, dm, dh2, name="norm_mlp_bwd")
    dy_na, dy_hg, dgn, dgh = _matmul(dh1, w_o, tb=True, name="mm_o_dx", epilogue=mix_gates_bwd,
                                     tiles=(*gates, (y_na, 0), (y_hg, 0)), out_dtypes=(BF16,) * 4)
    dw_o = _matmul(mix, dh1, ta=True, name="mm_o_dw")
    do_na = _matmul(dy_na, w_na, tb=True, name="mm_na_out_dx")
    dw_na = _matmul(o_na, dy_na, ta=True, name="mm_na_out_dw")
    do_hg = _matmul(dy_hg, w_hg, tb=True, name="mm_hg_out_dx")
    dw_hg = _matmul(o_hg, dy_hg, ta=True, name="mm_hg_out_dw")
    token = early_grads([dw_na, dw_hg, dw_o, dw_up, dw_down]) if early_grads else None
    if token is not None:
        hg_gain = hg_gain + token[0:1, 0:1]
    d_o, dg_hg, d_gain = _hg_out_bwd(o_f, o_b, proj, hg_gain, do_hg, col_g=col_g, name="hg_out_bwd")
    dq_f, dz_f, dv_f, dlb_f = _hg_scan_bwd(proj, lb_f, st_f, d_o, reverse=False, col_z=col_zf,
                                           name="hg_scan_f_bwd", **scan_kw)
    token = mid_grads(dq_f) if mid_grads else None
    lb_b_late = lb_b if token is None else lb_b + token[0:1, 0:1]
    dq_b, dz_b, dv_b, dlb_b = _hg_scan_bwd(proj, lb_b_late, st_b, d_o, reverse=True, col_z=col_zb,
                                           name="hg_scan_b_bwd", **scan_kw)
    dq_na, dk_na, dv_na, dtb = _na_bwd(proj, tb, o_na, lse, do_na, n_tok=n_tok, nh=nh, name="na_bwd")
    dproj = _assemble_dproj(dq_na, dk_na, dv_na, dq_f, dq_b, dz_f, dz_b, dv_f, dv_b, dg_hg, dgn,
                            dgh, name="assemble_dproj")
    dw_in = _matmul(a, dproj, ta=True, name="mm_in_dw")
    token = late_grad(dw_in) if late_grad else None
    da = _matmul(dproj, w_in, tb=True, name="mm_in_dx", after=token)
    dh0, dg_mix = _rmsnorm_bwd_add(h0, g_mix, da, dh1, name="norm_mix_bwd")
    d_rpb = _matmul(dtb.reshape(nh * (2 * NA_WIN_H - 1), GRID_W * GRID_W), oh, tb=True,
                    tm=nh * (2 * NA_WIN_H - 1), tn=LANES, tk=1024, precision=HIGHEST,
                    name="rpb_reduce")
    d_lb = jnp.concatenate([dlb_f.reshape(1, hgw), dlb_b.reshape(1, hgw)], axis=0)
    return (loss, dh0[N_META:l_real], dh0[:N_META], dw_in, dw_na, dw_hg, dw_o, dw_up, dw_down,
            dg_mix, dg_mlp, dg_fin, d_gain, d_rpb, d_lb)


N_CHIPS = 4
N_DEV = 8
ANY = pl.BlockSpec(memory_space=pl.ANY)


def _place():
    x, y, c = lax.axis_index("x"), lax.axis_index("y"), lax.axis_index("c")
    others = []
    for j in (1, 2, 3):
        tx = (1 - x) if (j >> 1) else x
        ty = (1 - y) if (j & 1) else y
        others.append((tx, ty))
    return x, y, c, others


def _piece(ref, axis, k, half, rh, cs):
    if axis == 1:
        return ref.at[pl.ds(pl.multiple_of(half * rh, 16), rh), pl.ds(pl.multiple_of(k * cs, LANES), cs)]
    return ref.at[pl.ds(pl.multiple_of(k * 2 * rh + half * rh, 16), rh), :]


def _cast_into_full(shard, axis, place, *, name):
    r, cs = shard.shape
    full = (r, cs * N_CHIPS) if axis == 1 else (r * N_CHIPS, cs)
    tr = next(t for t in (256, 128, 64, 32, 16) if r % t == 0)
    nt = r // tr

    def body(p_ref, s_ref, o_ref):
        o_ref[...] = s_ref[...].astype(BF16)

    if axis == 1:
        omap = lambda i, p_ref: (i, p_ref[0])
    else:
        omap = lambda i, p_ref: (p_ref[0] * nt + i, 0)
    return pl.pallas_call(
        body,
        grid_spec=pltpu.PrefetchScalarGridSpec(
            num_scalar_prefetch=1, grid=(nt,),
            in_specs=[pl.BlockSpec((tr, cs), lambda i, p_ref: (i, 0))],
            out_specs=pl.BlockSpec((tr, cs), omap)),
        out_shape=jax.ShapeDtypeStruct(full, BF16),
        compiler_params=_cp("parallel"), name=name)(place, shard)


HBM_SPEC = pl.BlockSpec(memory_space=pltpu.HBM)
SEM_SPEC = pl.BlockSpec(memory_space=pltpu.SEMAPHORE)
SPLIT_COPY = pltpu.CompilerParams(has_side_effects=pltpu.SideEffectType.DATAFLOW_SIDE_EFFECTING)
TOKEN = jax.ShapeDtypeStruct((8, LANES), F32)


def _geo(fulls, axes):
    out = []
    for f, ax in zip(fulls, axes):
        r, cs = (f.shape[0], f.shape[1] // N_CHIPS) if ax == 1 else (f.shape[0] // N_CHIPS, f.shape[1])
        out.append((ax, r // 2, cs))
    return out


def _gather_copies(refs, geo, send_sems, recv_sems):
    x, y, c, others = _place()
    chip = 2 * x + y
    cps = []
    for i, (ax, rh, cs) in enumerate(geo):
        mine = _piece(refs[i], ax, chip, c, rh, cs)
        for j, (tx, ty) in enumerate(others):
            cps.append(pltpu.make_async_remote_copy(
                src_ref=mine, dst_ref=mine, send_sem=send_sems.at[3 * i + j],
                recv_sem=recv_sems.at[3 * i + j], device_id=(tx, ty, c), device_id_type=MESH))
    return cps


def _allgather_start(fulls, axes, after, *, name):
    n = len(fulls)
    geo = _geo(fulls, axes)

    def body(*refs):
        w_refs = refs[:n]
        send_sems, recv_sems = refs[n + 1], refs[n + 2]
        token = refs[2 * n + 3]
        for cp in _gather_copies(w_refs, geo, send_sems, recv_sems):
            cp.start()
        token[...] = jnp.zeros_like(token)

    out = pl.pallas_call(
        body, name=name,
        out_shape=(pltpu.SemaphoreType.DMA((3 * n,)), pltpu.SemaphoreType.DMA((3 * n,)),
                   *[pltpu.HBM(f.shape, f.dtype) for f in fulls], TOKEN),
        in_specs=[HBM_SPEC] * n + [ANY],
        out_specs=(SEM_SPEC, SEM_SPEC, *[HBM_SPEC] * n, pl.BlockSpec(memory_space=pltpu.VMEM)),
        input_output_aliases={i: 2 + i for i in range(n)},
        compiler_params=SPLIT_COPY,
    )(*[pltpu.with_memory_space_constraint(f, pltpu.HBM) for f in fulls], after)
    return out[0], out[1], list(out[2:2 + n]), out[2 + n]


def _allgather_wait(send_sems, recv_sems, fulls, axes, after, *, name):
    n = len(fulls)
    geo = _geo(fulls, axes)

    def body(*refs):
        w_refs = refs[:n]
        for cp in _gather_copies(w_refs, geo, refs[n], refs[n + 1]):
            cp.wait_send()
            cp.wait_recv()

    return list(pl.pallas_call(
        body, name=name,
        out_shape=[pltpu.HBM(f.shape, f.dtype) for f in fulls],
        in_specs=[HBM_SPEC] * n + [SEM_SPEC, SEM_SPEC, ANY],
        out_specs=[HBM_SPEC] * n,
        input_output_aliases={i: i for i in range(n)},
        compiler_params=SPLIT_COPY,
    )(*fulls, send_sems, recv_sems, after))


def _allgather_forward(fulls, axes, *, name):
    n = len(fulls)
    geo = _geo(fulls, axes)

    def body(*refs):
        o_refs = refs[n:2 * n]
        send_sems, recv_sems = refs[2 * n:]
        x, y, c, others = _place()

        def rcopy(i, j, half, to):
            ax, rh, cs = geo[i]
            ref = _piece(o_refs[i], ax, 2 * others[j][0] + others[j][1], half, rh, cs)
            return pltpu.make_async_remote_copy(
                src_ref=ref, dst_ref=ref, send_sem=send_sems.at[3 * i + j],
                recv_sem=recv_sems.at[3 * i + j], device_id=to, device_id_type=MESH)

        cps = [rcopy(i, j, c, (x, y, 1 - c)) for i in range(n) for j in range(3)]
        for cp in cps:
            cp.start()
        for i in range(n):
            for j in range(3):
                rcopy(i, j, 1 - c, (x, y, c)).wait_recv()
        for cp in cps:
            cp.wait_send()

    return list(pl.pallas_call(
        body, in_specs=[ANY] * n, out_specs=[ANY] * n,
        out_shape=[jax.ShapeDtypeStruct(f.shape, f.dtype) for f in fulls],
        input_output_aliases={i: i for i in range(n)},
        scratch_shapes=[pltpu.SemaphoreType.DMA((3 * n,)), pltpu.SemaphoreType.DMA((3 * n,))],
        name=name)(*fulls))


def _scatter_geo(parts, axes):
    out = []
    for p, ax in zip(parts, axes):
        _, rh, cols = p.shape
        out.append((ax, rh, cols // N_CHIPS if ax == 1 else cols))
    return out


def _scatter_copies(p_refs, q_refs, geo, send_sems, recv_sems):
    x, y, c, others = _place()
    chip = 2 * x + y
    cps = []
    for i, (ax, rh, cw) in enumerate(geo):
        for j, (tx, ty) in enumerate(others):
            k = 2 * tx + ty
            src = (p_refs[i].at[0, :, pl.ds(pl.multiple_of(k * cw, LANES), cw)] if ax == 1
                   else p_refs[i].at[k])
            cps.append(pltpu.make_async_remote_copy(
                src_ref=src, dst_ref=q_refs[i].at[chip], send_sem=send_sems.at[3 * i + j],
                recv_sem=recv_sems.at[3 * i + j], device_id=(tx, ty, c), device_id_type=MESH))
    return cps


def _scatter_start(parts, axes, *, name):
    n = len(parts)
    geo = _scatter_geo(parts, axes)
    slots = [pltpu.HBM((N_CHIPS, rh, cw), p.dtype) for p, (_, rh, cw) in zip(parts, geo)]

    def body(*refs):
        p_refs, q_refs = refs[:n], refs[n:2 * n]
        send_sems, recv_sems = refs[2 * n], refs[2 * n + 1]
        token = refs[4 * n + 2]
        for cp in _scatter_copies(p_refs, q_refs, geo, send_sems, recv_sems):
            cp.start()
        token[...] = jnp.zeros_like(token)

    land = [pltpu.with_memory_space_constraint(lax.empty(s.inner_aval.shape, s.inner_aval.dtype), pltpu.HBM)
            for s in slots]
    out = pl.pallas_call(
        body, name=name,
        out_shape=(pltpu.SemaphoreType.DMA((3 * n,)), pltpu.SemaphoreType.DMA((3 * n,)),
                   *[pltpu.HBM(p.shape, p.dtype) for p in parts], *slots, TOKEN),
        in_specs=[HBM_SPEC] * (2 * n),
        out_specs=(SEM_SPEC, SEM_SPEC, *[HBM_SPEC] * (2 * n), pl.BlockSpec(memory_space=pltpu.VMEM)),
        input_output_aliases={i: 2 + i for i in range(2 * n)},
        compiler_params=SPLIT_COPY,
    )(*[pltpu.with_memory_space_constraint(p, pltpu.HBM) for p in parts], *land)
    return out[0], out[1], list(out[2:2 + n]), list(out[2 + n:2 + 2 * n]), out[2 + 2 * n]


def _scatter_wait(send_sems, recv_sems, parts, slots, axes, after, *, name):
    n = len(parts)
    geo = _scatter_geo(parts, axes)

    def body(*refs):
        p_refs, q_refs = refs[:n], refs[n:2 * n]
        for cp in _scatter_copies(p_refs, q_refs, geo, refs[2 * n], refs[2 * n + 1]):
            cp.wait_send()
            cp.wait_recv()

    out = pl.pallas_call(
        body, name=name,
        out_shape=[pltpu.HBM(a.shape, a.dtype) for a in (*parts, *slots)],
        in_specs=[HBM_SPEC] * (2 * n) + [SEM_SPEC, SEM_SPEC, ANY],
        out_specs=[HBM_SPEC] * (2 * n),
        input_output_aliases={i: i for i in range(2 * n)},
        compiler_params=SPLIT_COPY,
    )(*parts, *slots, send_sems, recv_sems, after)
    return list(out[:n]), list(out[n:])


def _sibling_swap(grads, *, name):
    n = len(grads)
    out_shape = [jax.ShapeDtypeStruct((g.shape[0], g.shape[1] // 2, g.shape[2]), g.dtype)
                 for g in grads]

    def body(*refs):
        g_refs, o_refs = refs[:n], refs[n:2 * n]
        send_sems, recv_sems = refs[2 * n:]
        x, y, c, _ = _place()
        cps = []
        for i in range(n):
            rh = grads[i].shape[1] // 2
            src = g_refs[i].at[:, pl.ds(pl.multiple_of((1 - c) * rh, 16), rh), :]
            cp = pltpu.make_async_remote_copy(
                src_ref=src, dst_ref=o_refs[i], send_sem=send_sems.at[i], recv_sem=recv_sems.at[i],
                device_id=(x, y, 1 - c), device_id_type=MESH)
            cp.start()
            cps.append(cp)
        for cp in cps:
            cp.wait()

    return pl.pallas_call(
        body, in_specs=[ANY] * n, out_specs=[ANY] * n, out_shape=out_shape,
        scratch_shapes=[pltpu.SemaphoreType.DMA((n,)), pltpu.SemaphoreType.DMA((n,))],
        name=name)(*grads)


def _swap_copies(g_refs, r_refs, shapes, send_sems, recv_sems):
    x, y, c, _ = _place()
    cps = []
    for i, shape in enumerate(shapes):
        rh = shape[1] // 2
        src = g_refs[i].at[:, pl.ds(pl.multiple_of((1 - c) * rh, 16), rh), :]
        cps.append(pltpu.make_async_remote_copy(
            src_ref=src, dst_ref=r_refs[i], send_sem=send_sems.at[i], recv_sem=recv_sems.at[i],
            device_id=(x, y, 1 - c), device_id_type=MESH))
    return cps


def _sibling_swap_start(grads, *, name):
    n = len(grads)
    shapes = [g.shape for g in grads]
    lands = [pltpu.HBM((s[0], s[1] // 2, s[2]), g.dtype) for s, g in zip(shapes, grads)]

    def body(*refs):
        g_refs, r_refs = refs[:n], refs[n:2 * n]
        token = refs[4 * n + 2]
        for cp in _swap_copies(g_refs, r_refs, shapes, refs[2 * n], refs[2 * n + 1]):
            cp.start()
        token[...] = jnp.zeros_like(token)

    land = [pltpu.with_memory_space_constraint(lax.empty(s.inner_aval.shape, s.inner_aval.dtype), pltpu.HBM)
            for s in lands]
    out = pl.pallas_call(
        body, name=name,
        out_shape=(pltpu.SemaphoreType.DMA((n,)), pltpu.SemaphoreType.DMA((n,)),
                   *[pltpu.HBM(g.shape, g.dtype) for g in grads], *lands, TOKEN),
        in_specs=[HBM_SPEC] * (2 * n),
        out_specs=(SEM_SPEC, SEM_SPEC, *[HBM_SPEC] * (2 * n), pl.BlockSpec(memory_space=pltpu.VMEM)),
        input_output_aliases={i: 2 + i for i in range(2 * n)},
        compiler_params=SPLIT_COPY,
    )(*[pltpu.with_memory_space_constraint(g, pltpu.HBM) for g in grads], *land)
    return out[0], out[1], list(out[2:2 + n]), list(out[2 + n:2 + 2 * n]), out[2 + 2 * n]


def _sibling_swap_wait(send_sems, recv_sems, grads, lands, after, *, name):
    n = len(grads)
    shapes = [g.shape for g in grads]

    def body(*refs):
        g_refs, r_refs = refs[:n], refs[n:2 * n]
        for cp in _swap_copies(g_refs, r_refs, shapes, refs[2 * n], refs[2 * n + 1]):
            cp.wait_send()
            cp.wait_recv()

    out = pl.pallas_call(
        body, name=name,
        out_shape=[pltpu.HBM(a.shape, a.dtype) for a in (*grads, *lands)],
        in_specs=[HBM_SPEC] * (2 * n) + [SEM_SPEC, SEM_SPEC, ANY],
        out_specs=[HBM_SPEC] * (2 * n),
        input_output_aliases={i: i for i in range(2 * n)},
        compiler_params=SPLIT_COPY,
    )(*grads, *lands, send_sems, recv_sems, after)
    return list(out[:n]), list(out[n:])


def _pair_add(g3, rx, c_arr, *, out_dtype, name):
    nb, rows, cols = g3.shape
    rh = rows // 2
    tr = next(t for t in (128, 64, 32, 16) if rh % t == 0)
    nt = rh // tr

    def body(c_ref, g_ref, r_ref, o_ref):
        o_ref[...] = (g_ref[...] + r_ref[...]).astype(out_dtype)

    return pl.pallas_call(
        body,
        grid_spec=pltpu.PrefetchScalarGridSpec(
            num_scalar_prefetch=1, grid=(nb, nt),
            in_specs=[pl.BlockSpec((None, tr, cols), lambda b, i, c_ref: (b, c_ref[0] * nt + i, 0)),
                      pl.BlockSpec((None, tr, cols), lambda b, i, c_ref: (b, i, 0))],
            out_specs=pl.BlockSpec((None, tr, cols), lambda b, i, c_ref: (b, i, 0))),
        out_shape=jax.ShapeDtypeStruct((nb, rh, cols), out_dtype),
        compiler_params=_cp("parallel", "parallel"), name=name)(c_arr, g3, rx)


def _sum_slots(q, *, name):
    ns, rows, cols = q.shape
    tr = next(t for t in (128, 64, 32, 16, 8) if rows % t == 0)

    def body(q_ref, o_ref):
        acc = q_ref[0].astype(F32)
        for k in range(1, ns):
            acc = acc + q_ref[k].astype(F32)
        o_ref[...] = acc

    return pl.pallas_call(
        body, grid=(rows // tr,),
        in_specs=[pl.BlockSpec((ns, tr, cols), lambda i: (0, i, 0))],
        out_specs=_rspec(tr, cols),
        out_shape=jax.ShapeDtypeStruct((rows, cols), F32),
        compiler_params=_cp("parallel"), name=name)(q)


def _sum_chips(q, p, place, axis, *, name):
    _, rh, cw = q.shape
    tr = next(t for t in (128, 64, 32, 16) if rh % t == 0)
    nt = rh // tr

    def body(p_ref, *refs):
        q_refs, own_ref, o_ref = refs[:N_CHIPS], refs[N_CHIPS], refs[N_CHIPS + 1]
        chip = p_ref[0]
        acc = jnp.where(chip == 0, own_ref[...], q_refs[0][...]).astype(F32)
        for k in range(1, N_CHIPS):
            acc = acc + jnp.where(chip == k, own_ref[...], q_refs[k][...]).astype(F32)
        o_ref[...] = acc

    def slot_spec(k):
        return pl.BlockSpec((None, tr, cw),
                            lambda i, p_ref: (jnp.where(p_ref[0] == k, (k + 1) % N_CHIPS, k), i, 0))

    if axis == 1:
        own_spec = pl.BlockSpec((None, tr, cw), lambda i, p_ref: (0, i, p_ref[0]))
    else:
        own_spec = pl.BlockSpec((None, tr, cw), lambda i, p_ref: (p_ref[0], i, 0))
    return pl.pallas_call(
        body,
        grid_spec=pltpu.PrefetchScalarGridSpec(
            num_scalar_prefetch=1, grid=(nt,),
            in_specs=[slot_spec(k) for k in range(N_CHIPS)] + [own_spec],
            out_specs=pl.BlockSpec((tr, cw), lambda i, p_ref: (p_ref[1] * nt + i, 0))),
        out_shape=jax.ShapeDtypeStruct((2 * rh, cw), F32),
        compiler_params=_cp("parallel"), name=name)(place, *([q] * N_CHIPS), p)


def _sibling_share(shards, *, name):
    n = len(shards)

    def body(*refs):
        o_refs = refs[n:2 * n]
        send_sems, recv_sems = refs[2 * n:]
        x, y, c, _ = _place()
        cps = []
        for i in range(n):
            rh = shards[i].shape[0] // 2
            mine = o_refs[i].at[pl.ds(pl.multiple_of(c * rh, 8), rh), :]
            cp = pltpu.make_async_remote_copy(
                src_ref=mine, dst_ref=mine, send_sem=send_sems.at[i], recv_sem=recv_sems.at[i],
                device_id=(x, y, 1 - c), device_id_type=MESH)
            cp.start()
            cps.append(cp)
        for i in range(n):
            rh = shards[i].shape[0] // 2
            theirs = o_refs[i].at[pl.ds(pl.multiple_of((1 - c) * rh, 8), rh), :]
            pltpu.make_async_remote_copy(
                src_ref=theirs, dst_ref=theirs, send_sem=send_sems.at[i], recv_sem=recv_sems.at[i],
                device_id=(x, y, c), device_id_type=MESH).wait_recv()
        for cp in cps:
            cp.wait_send()

    return pl.pallas_call(
        body, in_specs=[ANY] * n, out_specs=[ANY] * n,
        out_shape=[jax.ShapeDtypeStruct(h.shape, h.dtype) for h in shards],
        input_output_aliases={i: i for i in range(n)},
        scratch_shapes=[pltpu.SemaphoreType.DMA((n,)), pltpu.SemaphoreType.DMA((n,))],
        name=name)(*shards)


def _gather_all(blk, *, name, after=None):
    rows, cols = blk.shape
    extra = [] if after is None else [after]

    def body(x_ref, *refs):
        out_ref, send_sems, recv_sems, local_sem = refs[len(extra):]
        x, y, c = lax.axis_index("x"), lax.axis_index("y"), lax.axis_index("c")
        me = 4 * x + 2 * y + c
        mine = pltpu.make_async_copy(x_ref, out_ref.at[me], local_sem)
        mine.start()
        cps = []
        for k in range(1, N_DEV):
            tx = (1 - x) if (k >> 2) & 1 else x
            ty = (1 - y) if (k >> 1) & 1 else y
            tc = (1 - c) if k & 1 else c
            cp = pltpu.make_async_remote_copy(
                src_ref=x_ref, dst_ref=out_ref.at[me], send_sem=send_sems.at[k - 1],
                recv_sem=recv_sems.at[k - 1], device_id=(tx, ty, tc), device_id_type=MESH)
            cp.start()
            cps.append(cp)
        for k in range(1, N_DEV):
            tx = (1 - x) if (k >> 2) & 1 else x
            ty = (1 - y) if (k >> 1) & 1 else y
            tc = (1 - c) if k & 1 else c
            got = out_ref.at[4 * tx + 2 * ty + tc]
            pltpu.make_async_remote_copy(
                src_ref=got, dst_ref=got, send_sem=send_sems.at[k - 1], recv_sem=recv_sems.at[k - 1],
                device_id=(x, y, c), device_id_type=MESH).wait_recv()
        for cp in cps:
            cp.wait_send()
        mine.wait()

    vm = pl.BlockSpec(memory_space=pltpu.VMEM)
    return pl.pallas_call(
        body, in_specs=[vm] + [ANY] * len(extra), out_specs=vm,
        out_shape=jax.ShapeDtypeStruct((N_DEV, rows, cols), blk.dtype),
        scratch_shapes=[pltpu.SemaphoreType.DMA((N_DEV - 1,)), pltpu.SemaphoreType.DMA((N_DEV - 1,)),
                        pltpu.SemaphoreType.DMA],
        name=name)(blk, *extra)


def _as_rows(a):
    flat = a.reshape(-1)
    n = flat.shape[0]
    rows = -(-n // (8 * LANES)) * 8
    return jnp.pad(flat, (0, rows * LANES - n)).reshape(rows, LANES)


def _from_rows(p, shape):
    n = int(np.prod(shape))
    return p.reshape(-1)[:n].reshape(shape)


WEIGHT_AXES = (1, 1, 1, 0, 1, 0)
WIRE = BF16


def kernel(x, meta_tokens, w_in, w_na_out, w_hg_out, w_o, w_up, w_down, norm_mix, norm_mlp, norm_final, hg_norm, na_rpb, hg_lb_logits, loss_target, m_meta_tokens, m_w_in, m_w_na_out, m_w_hg_out, m_w_o, m_w_up, m_w_down, m_norm_mix, m_norm_mlp, m_norm_final, m_hg_norm, m_na_rpb, m_hg_lb_logits, v_meta_tokens, v_w_in, v_w_na_out, v_w_hg_out, v_w_o, v_w_up, v_w_down, v_norm_mix, v_norm_mlp, v_norm_final, v_hg_norm, v_na_rpb, v_hg_lb_logits):
    xi, yi, ci = lax.axis_index("x"), lax.axis_index("y"), lax.axis_index("c")
    chip = 2 * xi + yi
    d = x.shape[-1]
    dshard = meta_tokens.shape[1]
    hgw = hg_norm.shape[1]
    lbs = hg_lb_logits.shape[2]
    big = [w_in[0], w_na_out[0], w_hg_out[0], w_o[0], w_up[0], w_down[0]]
    big_m = [m_w_in[0], m_w_na_out[0], m_w_hg_out[0], m_w_o[0], m_w_up[0], m_w_down[0]]
    big_v = [v_w_in[0], v_w_na_out[0], v_w_hg_out[0], v_w_o[0], v_w_up[0], v_w_down[0]]

    place = jnp.stack([chip, ci]).astype(jnp.int32)
    own_w = [_cast_into_full(w, ax, place, name=f"cast_shard_{i}")
             for i, (w, ax) in enumerate(zip(big, WEIGHT_AXES))]
    in_axes, rest_axes = WEIGHT_AXES[:1], WEIGHT_AXES[1:]
    small_in = jnp.concatenate([_as_rows(meta_tokens), _as_rows(hg_lb_logits)], axis=0)
    small_all = _gather_all(small_in, name="gather_small_params")[0::2]
    in_send, in_recv, in_bufs, in_token = _allgather_start(own_w[:1], in_axes, small_all,
                                                           name="weight_allgather_in_start")
    ag_send, ag_recv, ag_bufs, ag_token = _allgather_start(own_w[1:], rest_axes, in_token,
                                                           name="weight_allgather_rest_start")

    def first_weight(after):
        got = _allgather_wait(in_send, in_recv, in_bufs, in_axes, after,
                              name="weight_allgather_in_wait")
        return _allgather_forward(got, in_axes, name="weight_allgather_in_forward")[0]

    def rest_weights(after):
        got = _allgather_wait(ag_send, ag_recv, ag_bufs, rest_axes, after,
                              name="weight_allgather_rest_wait")
        return _allgather_forward(got, rest_axes, name="weight_allgather_rest_forward")

    n_meta_rows = N_META * dshard // LANES
    meta_full = (small_all[:, :n_meta_rows].reshape(N_CHIPS, N_META, dshard)
                 .transpose(1, 0, 2).reshape(N_META, d))
    lbl_full = (small_all[:, n_meta_rows:].reshape(N_CHIPS, -1)[:, :4 * lbs]
                .reshape(N_CHIPS, 2, 2, lbs).transpose(1, 2, 0, 3).reshape(2, 2, N_CHIPS * lbs))
    lb = jax.nn.softmax(lbl_full, axis=1)[:, 0]

    c_arr = ci.reshape(1).astype(jnp.int32)

    def by_chip(dws, axes):
        return [g.reshape(1, *g.shape) if ax == 1
                else g.reshape(N_CHIPS, g.shape[0] // N_CHIPS, g.shape[1]) for g, ax in zip(dws, axes)]

    flying = {}

    def scatter(tag, axes, g3, rx):
        parts = [_pair_add(g, r, c_arr, out_dtype=WIRE, name=f"grad_pair_add_{tag}_{i}")
                 for i, (g, r) in enumerate(zip(g3, rx))]
        send, recv, parts, slots, token = _scatter_start(parts, axes,
                                                         name=f"grad_scatter_{tag}_start")
        flying[tag] = (send, recv, parts, slots)
        return token

    def swap_rest(dws):
        send, recv, g3, lands, token = _sibling_swap_start(by_chip(dws, rest_axes),
                                                           name="grad_sibling_swap_rest_start")
        flying["swap"] = (send, recv, g3, lands)
        return token

    def scatter_rest(after):
        g3, rx = _sibling_swap_wait(*flying["swap"], after, name="grad_sibling_swap_rest_wait")
        return scatter("rest", rest_axes, g3, rx)

    def scatter_in(dw_in):
        g3 = by_chip([dw_in], in_axes)
        return scatter("in", in_axes, g3, _sibling_swap(g3, name="grad_sibling_swap_in"))

    def landed(tag, axes, after):
        parts, slots = _scatter_wait(*flying[tag], axes, after, name=f"grad_scatter_{tag}_wait")
        return [_sum_chips(q, p, place, ax, name=f"grad_sum_chips_{tag}_{i}")
                for i, (q, p, ax) in enumerate(zip(slots, parts, axes))]

    (loss, dx, dmeta, *_, dg_mix, dg_mlp, dg_fin, d_gain, d_rpb, d_lb) = _local_step(
        x[0], loss_target[0], meta_full, first_weight, rest_weights,
        norm_mix + ag_token[0:1, 0:1], norm_mlp, norm_final.reshape(1, d), hg_norm, na_rpb[0], lb,
        swap_rest, scatter_rest, scatter_in)

    halves_rest = landed("rest", rest_axes, dx)
    halves_in = landed("in", in_axes, halves_rest[-1])
    g_big = _sibling_share(halves_in + halves_rest, name="grad_sibling_share")

    d_rpb_c = d_rpb[:, :2 * NA_WIN_W - 1]
    small_g = [dmeta, dg_mix, dg_mlp, dg_fin, d_gain, d_rpb_c, d_lb, loss]
    packed = jnp.concatenate([_as_rows(a) for a in small_g], axis=0)
    total = _sum_slots(_gather_all(packed, after=halves_in[0], name="gather_small_grads"),
                       name="sum_small_grads")
    offs = np.cumsum([0] + [_as_rows(a).shape[0] for a in small_g])
    take = lambda i, shape: _from_rows(total[offs[i]:offs[i + 1]], shape)
    g_meta_full = take(0, (N_META, d))
    g_norm_mix, g_norm_mlp = take(1, (1, d)), take(2, (1, d))
    g_norm_final = take(3, (d,))
    g_hg_norm = take(4, (1, hgw))
    g_rpb = take(5, na_rpb.shape)
    g_lb = take(6, (2, hgw))
    loss_total = take(7, (1, LANES))[0, 0]
    g_meta = lax.dynamic_slice_in_dim(g_meta_full, chip * dshard, dshard, axis=1)
    dl0 = lb * (1.0 - lb) * g_lb
    g_lbl_full = jnp.stack([dl0, -dl0], axis=1)
    g_lbl = lax.dynamic_slice_in_dim(g_lbl_full, chip * lbs, lbs, axis=2)

    big_out = [_adamw(w, g, m, v, name=f"adamw_{i}")
               for i, (w, g, m, v) in enumerate(zip(big, g_big, big_m, big_v))]
    small_w = [meta_tokens, norm_mix, norm_mlp, norm_final, hg_norm, na_rpb, hg_lb_logits]
    small_gr = [g_meta, g_norm_mix, g_norm_mlp, g_norm_final, g_hg_norm, g_rpb, g_lbl]
    small_m = [m_meta_tokens, m_norm_mix, m_norm_mlp, m_norm_final, m_hg_norm, m_na_rpb, m_hg_lb_logits]
    small_v = [v_meta_tokens, v_norm_mix, v_norm_mlp, v_norm_final, v_hg_norm, v_na_rpb, v_hg_lb_logits]
    pk = lambda lst: jnp.concatenate([_as_rows(a) for a in lst], axis=0)
    sd, sm, sv = _adamw(pk(small_w), pk(small_gr), pk(small_m), pk(small_v), name="adamw_small")
    soffs = np.cumsum([0] + [_as_rows(a).shape[0] for a in small_w])
    unpk = lambda p: [_from_rows(p[soffs[i]:soffs[i + 1]], small_w[i].shape) for i in range(len(small_w))]
    sd, sm, sv = unpk(sd), unpk(sm), unpk(sv)

    def order(bigs, smalls):
        return [smalls[0]] + [b.reshape(1, *b.shape) for b in bigs] + smalls[1:]

    grads = order(g_big, small_gr)
    deltas = order([o[0] for o in big_out], sd)
    new_m = order([o[1] for o in big_out], sm)
    new_v = order([o[2] for o in big_out], sv)
    return (loss_total, dx.reshape(1, *dx.shape), *grads, *deltas, *new_m, *new_v)
```

```python
import functools

import numpy as np
import jax
import jax.numpy as jnp
from jax import lax
from jax.experimental import pallas as pl
from jax.experimental.pallas import tpu as pltpu

F32 = jnp.float32
BF16 = jnp.bfloat16
HIGHEST = lax.Precision.HIGHEST

GRID_W = 64
N_META = 16
EPS = 1e-6
NA_HEAD_DIM = 64
NA_WIN_H = 8
NA_WIN_W = 16
HG_DK = 128
HG_CHUNK = 16
LANES = 128
ROW_ALIGN = 128
VMEM_LIMIT = 48 * 1024 * 1024

ADAM_LR = 0.001
ADAM_B1 = 0.9
ADAM_B2 = 0.999
ADAM_EPS = 1e-08
ADAM_WD = 0.01
ADAM_STEP = 10

MESH = pl.DeviceIdType.MESH


def _cp(*sem):
    return pltpu.CompilerParams(dimension_semantics=sem, vmem_limit_bytes=VMEM_LIMIT)


def _sigmoid(x):
    return 1.0 / (1.0 + jnp.exp(-x))


def _dot(a, b, dims, precision=None):
    return lax.dot_general(a, b, (dims, ((), ())), preferred_element_type=F32, precision=precision)


def _nn(a, b, **kw):
    return _dot(a, b, ((1,), (0,)), **kw)


def _nt(a, b, **kw):
    return _dot(a, b, ((1,), (1,)), **kw)


def _tn(a, b, **kw):
    return _dot(a, b, ((0,), (0,)), **kw)


def _matmul(a, b, *, ta=False, tb=False, tm=None, tn=None, tk=None, out_dtype=F32, name,
            precision=None, after=None, epilogue=None, tiles=(), out_dtypes=None):
    extra = [] if after is None else [after]
    single = out_dtypes is None
    if single:
        out_dtypes = (out_dtype,)
    n_t, n_o = len(tiles), len(out_dtypes)
    if ta:
        kdim, m = a.shape
    else:
        m, kdim = a.shape
    if tb:
        n, k2 = b.shape
    else:
        k2, n = b.shape
    assert kdim == k2, (a.shape, b.shape, ta, tb)
    if tm is None:
        if ta:
            tm = next(t for t in (1024, 512, 256, 128, m) if m % t == 0)
        else:
            tm = m // 2 if (m // 2) % 16 == 0 and m > 512 else m
    if tn is None:
        wide = (1024,) if not ta and len(tiles) <= 1 else ()
        tn = next(t for t in (*wide, 512, 256, 128, n) if n % t == 0)
    if tk is None:
        tk = kdim if ta else next(t for t in (1024, 512, 256, 128, kdim) if kdim % t == 0)
    assert m % tm == 0 and n % tn == 0 and kdim % tk == 0, (m, n, kdim, tm, tn, tk)
    nk = kdim // tk
    op_dtype = F32 if precision is not None else BF16

    def body(a_ref, b_ref, *refs):
        t_refs = refs[:n_t]
        o_refs = refs[n_t + len(extra):n_t + len(extra) + n_o]
        av = a_ref[...].astype(op_dtype)
        bv = b_ref[...].astype(op_dtype)
        dims = ((0 if ta else 1,), (1 if tb else 0,))
        part = _dot(av, bv, dims, precision=precision)

        def finish(acc):
            outs = (acc,) if epilogue is None else epilogue(acc, *[t[...] for t in t_refs])
            for o_ref, val in zip(o_refs, outs):
                o_ref[...] = val.astype(o_ref.dtype)

        if nk == 1:
            finish(part)
            return
        acc_ref = refs[-1]
        kk = pl.program_id(2)

        @pl.when(kk == 0)
        def _():
            acc_ref[...] = part

        @pl.when((kk > 0) & (kk < nk - 1))
        def _():
            acc_ref[...] += part

        @pl.when(kk == nk - 1)
        def _():
            finish(acc_ref[...] + part)

    a_spec = (pl.BlockSpec((tk, tm), lambda i, j, k: (k, i)) if ta
              else pl.BlockSpec((tm, tk), lambda i, j, k: (i, k)))
    b_spec = (pl.BlockSpec((tn, tk), lambda i, j, k: (j, k)) if tb
              else pl.BlockSpec((tk, tn), lambda i, j, k: (k, j)))
    for _, off in tiles:
        assert off % tn == 0, (off, tn)
    t_specs = [pl.BlockSpec((tm, tn), functools.partial(lambda i, j, k, o: (i, o + j), o=off // tn))
               for _, off in tiles]
    o_spec = pl.BlockSpec((tm, tn), lambda i, j, k: (i, j))
    outs = pl.pallas_call(
        body,
        grid=(m // tm, n // tn, nk),
        in_specs=[a_spec, b_spec] + t_specs + [pl.BlockSpec(memory_space=pl.ANY)] * len(extra),
        out_specs=[o_spec] * n_o,
        out_shape=[jax.ShapeDtypeStruct((m, n), dt) for dt in out_dtypes],
        scratch_shapes=[pltpu.VMEM((tm, tn), F32)] if nk > 1 else [],
        compiler_params=_cp("parallel", "parallel", "arbitrary"),
        name=name,
    )(a, b, *[t for t, _ in tiles], *extra)
    return outs[0] if single else outs


def _rspec(tr, w, cb=0):
    return pl.BlockSpec((tr, w), lambda i: (i, cb))


def _fspec(shape):
    nd = len(shape)
    return pl.BlockSpec(shape, lambda i: (0,) * nd)


def _row_tile(lp):
    return ROW_ALIGN if lp % ROW_ALIGN == 0 else lp


def _rmsnorm_fwd(x, g, *, name):
    lp, d = x.shape
    tr = _row_tile(lp)

    def body(x_ref, g_ref, o_ref):
        xv = x_ref[...]
        r = lax.rsqrt(jnp.mean(xv * xv, axis=-1, keepdims=True) + EPS)
        o_ref[...] = (xv * r * g_ref[...]).astype(BF16)

    return pl.pallas_call(
        body, grid=(lp // tr,),
        in_specs=[_rspec(tr, d), _fspec((1, d))],
        out_specs=_rspec(tr, d),
        out_shape=jax.ShapeDtypeStruct((lp, d), BF16),
        compiler_params=_cp("parallel"), name=name)(x, g)


def _residual_norm(h, t, g, *, name):
    lp, d = h.shape
    tr = _row_tile(lp)

    def body(h_ref, t_ref, g_ref, h1_ref, m_ref):
        xv = h_ref[...] + t_ref[...]
        h1_ref[...] = xv
        r = lax.rsqrt(jnp.mean(xv * xv, axis=-1, keepdims=True) + EPS)
        m_ref[...] = (xv * r * g_ref[...]).astype(BF16)

    return pl.pallas_call(
        body, grid=(lp // tr,),
        in_specs=[_rspec(tr, d), _rspec(tr, d), _fspec((1, d))],
        out_specs=[_rspec(tr, d), _rspec(tr, d)],
        out_shape=[jax.ShapeDtypeStruct((lp, d), F32), jax.ShapeDtypeStruct((lp, d), BF16)],
        compiler_params=_cp("parallel"), name=name)(h, t, g)


def _rmsnorm_bwd_add(x, g, dy, dres, *, name):
    lp, d = x.shape
    tr = _row_tile(lp)

    def body(x_ref, g_ref, dy_ref, dr_ref, dx_ref, dg_ref):
        @pl.when(pl.program_id(0) == 0)
        def _():
            dg_ref[...] = jnp.zeros_like(dg_ref)

        xv = x_ref[...]
        r = lax.rsqrt(jnp.mean(xv * xv, axis=-1, keepdims=True) + EPS)
        xh = xv * r
        dyv = dy_ref[...]
        dg_ref[...] += jnp.sum(dyv * xh, axis=0, keepdims=True)
        dxh = dyv * g_ref[...]
        dx_ref[...] = dr_ref[...] + r * (dxh - xh * jnp.mean(dxh * xh, axis=-1, keepdims=True))

    return pl.pallas_call(
        body, grid=(lp // tr,),
        in_specs=[_rspec(tr, d), _fspec((1, d)), _rspec(tr, d), _rspec(tr, d)],
        out_specs=[_rspec(tr, d), _fspec((1, d))],
        out_shape=[jax.ShapeDtypeStruct((lp, d), F32), jax.ShapeDtypeStruct((1, d), F32)],
        compiler_params=_cp("arbitrary"), name=name)(x, g, dy, dres)


def _final_loss(h1, t2, g, tgt, *, n_tok, name):
    lp, d = h1.shape
    tr = _row_tile(lp)

    def body(h_ref, t_ref, g_ref, tg_ref, dh_ref, loss_ref, dg_ref):
        i = pl.program_id(0)

        @pl.when(i == 0)
        def _():
            loss_ref[...] = jnp.zeros_like(loss_ref)
            dg_ref[...] = jnp.zeros_like(dg_ref)

        xv = h_ref[...] + t_ref[...]
        r = lax.rsqrt(jnp.mean(xv * xv, axis=-1, keepdims=True) + EPS)
        xh = xv * r
        gv = g_ref[...]
        row = i * tr + lax.broadcasted_iota(jnp.int32, (tr, 1), 0)
        valid = (row >= N_META) & (row < N_META + n_tok)
        err = jnp.where(valid, xh * gv - tg_ref[...], 0.0)
        loss_ref[...] += jnp.sum(0.5 * err * err) / d
        dy = err / d
        dg_ref[...] += jnp.sum(dy * xh, axis=0, keepdims=True)
        dxh = dy * gv
        dh_ref[...] = r * (dxh - xh * jnp.mean(dxh * xh, axis=-1, keepdims=True))

    return pl.pallas_call(
        body, grid=(lp // tr,),
        in_specs=[_rspec(tr, d), _rspec(tr, d), _fspec((1, d)), _rspec(tr, d)],
        out_specs=[_rspec(tr, d), _fspec((1, LANES)), _fspec((1, d))],
        out_shape=[jax.ShapeDtypeStruct((lp, d), F32), jax.ShapeDtypeStruct((1, LANES), F32),
                   jax.ShapeDtypeStruct((1, d), F32)],
        compiler_params=_cp("arbitrary"), name=name)(h1, t2, g, tgt)


def _hg_out(o_f, o_b, proj, gain, *, col_g, name):
    lp, w = o_f.shape
    tr = _row_tile(lp)
    hh = w // HG_DK

    def body(of_ref, ob_ref, g_ref, gain_ref, y_ref):
        gv = g_ref[...]
        sg = gv * _sigmoid(gv)
        for h in range(hh):
            sl = slice(h * HG_DK, (h + 1) * HG_DK)
            o = of_ref[:, sl] + ob_ref[:, sl]
            r = lax.rsqrt(jnp.mean(o * o, axis=-1, keepdims=True) + EPS)
            y_ref[:, sl] = (o * r * gain_ref[:, sl] * sg[:, sl]).astype(BF16)

    return pl.pallas_call(
        body, grid=(lp // tr,),
        in_specs=[_rspec(tr, w), _rspec(tr, w), _rspec(tr, w, col_g // w), _fspec((1, w))],
        out_specs=_rspec(tr, w),
        out_shape=jax.ShapeDtypeStruct((lp, w), BF16),
        compiler_params=_cp("parallel"), name=name)(o_f, o_b, proj, gain)


def _hg_out_bwd(o_f, o_b, proj, gain, dy, *, col_g, name):
    lp, w = o_f.shape
    tr = _row_tile(lp)
    hh = w // HG_DK

    def body(of_ref, ob_ref, g_ref, gain_ref, dy_ref, do_ref, dg_ref, dgain_ref):
        @pl.when(pl.program_id(0) == 0)
        def _():
            dgain_ref[...] = jnp.zeros_like(dgain_ref)

        for h in range(hh):
            sl = slice(h * HG_DK, (h + 1) * HG_DK)
            gv = g_ref[:, sl]
            s = _sigmoid(gv)
            sg = gv * s
            dsg = s + gv * s * (1.0 - s)
            o = of_ref[:, sl] + ob_ref[:, sl]
            r = lax.rsqrt(jnp.mean(o * o, axis=-1, keepdims=True) + EPS)
            on = o * r
            dyv = dy_ref[:, sl]
            gn = gain_ref[:, sl]
            dgain_ref[:, sl] += jnp.sum(dyv * on * sg, axis=0, keepdims=True)
            dg_ref[:, sl] = (dyv * on * gn * dsg).astype(BF16)
            don = dyv * gn * sg
            do_ref[:, sl] = r * (don - on * jnp.mean(don * on, axis=-1, keepdims=True))

    return pl.pallas_call(
        body, grid=(lp // tr,),
        in_specs=[_rspec(tr, w), _rspec(tr, w), _rspec(tr, w, col_g // w), _fspec((1, w)),
                  _rspec(tr, w)],
        out_specs=[_rspec(tr, w), _rspec(tr, w), _fspec((1, w))],
        out_shape=[jax.ShapeDtypeStruct((lp, w), F32), jax.ShapeDtypeStruct((lp, w), BF16),
                   jax.ShapeDtypeStruct((1, w), F32)],
        compiler_params=_cp("arbitrary"), name=name)(o_f, o_b, proj, gain, dy)


HG_ROWS = 128
HG_HALVES = (1, 2, 4, 8, 16, 32, 64)


def _hg_gates(zq, z, lbv):
    qh = zq * _sigmoid(zq)
    s = _sigmoid(z)
    f = lbv + (1.0 - lbv) * s
    kk = (1.0 - lbv) * _sigmoid(-z)
    return qh, s, f, jnp.log(f), kk


def _block_cumsum(g, pos, suffix):
    x = g
    for k in HG_HALVES:
        if suffix:
            x = x + jnp.where(pos < HG_ROWS - k, pltpu.roll(x, HG_ROWS - k, 0), 0.0)
        else:
            x = x + jnp.where(pos >= k, pltpu.roll(x, k, 0), 0.0)
    return x


def _pair_levels(b, pos, reverse):
    out = []
    first = b
    for m in HG_HALVES:
        if m > 1:
            first = jnp.where((pos & (m - 1)) >= m // 2, pltpu.roll(first, m // 2, 0), first)
        nxt = pltpu.roll(first, HG_ROWS - m, 0)
        upper = (pos & (2 * m - 1)) >= m
        if reverse:
            eq = jnp.where(upper, 0.0, jnp.exp(b - nxt))
            ek = jnp.where(upper, jnp.exp(first - b), 0.0)
        else:
            eq = jnp.where(upper, jnp.exp(b - first), 0.0)
            ek = jnp.where(upper, 0.0, jnp.exp(nxt - b))
        out.append((eq, ek))
    return out


def _pair_masks(mask_ref):
    ri = lax.broadcasted_iota(jnp.int32, (HG_ROWS, HG_ROWS), 0)
    ci = lax.broadcasted_iota(jnp.int32, (HG_ROWS, HG_ROWS), 1)
    for i, m in enumerate(HG_HALVES):
        sh = m.bit_length()
        mask_ref[i] = jnp.where((ri >> sh) == (ci >> sh), 1.0, 0.0)


def _hg_scan_fwd(proj, lb, *, reverse, col_q, col_z, col_i, hh, name):
    lp = proj.shape[0]
    n_blocks = lp // HG_ROWS
    last = 0 if reverse else HG_ROWS - 1

    def body(q_ref, z_ref, i_ref, lb_ref, o_ref, st_ref, mask_ref):
        lbv = lb_ref[...]
        pos = lax.broadcasted_iota(jnp.int32, (HG_ROWS, 1), 0)
        ri = lax.broadcasted_iota(jnp.int32, (HG_ROWS, HG_ROWS), 0)
        ci = lax.broadcasted_iota(jnp.int32, (HG_ROWS, HG_ROWS), 1)
        _pair_masks(mask_ref)

        def block(bi, st):
            bb = (n_blocks - 1 - bi) if reverse else bi
            r0 = pl.multiple_of(bb * HG_ROWS, HG_ROWS)
            v16 = i_ref[pl.ds(r0, HG_ROWS), :].astype(BF16)
            qh, _, _, g, kk = _hg_gates(q_ref[pl.ds(r0, HG_ROWS), :], z_ref[pl.ds(r0, HG_ROWS), :],
                                        lbv)
            b = _block_cumsum(g, pos, reverse)
            bl = b[last:last + 1, :]
            qe = (qh * jnp.exp(b)).astype(BF16)
            kd = (kk * jnp.exp(bl - b)).astype(BF16)
            a = jnp.where(ri == ci, jnp.sum(qh * kk, axis=1, keepdims=True), 0.0)
            for i, (eq, ek) in enumerate(_pair_levels(b, pos, reverse)):
                a = a + mask_ref[i] * _nt((qh * eq).astype(BF16), (kk * ek).astype(BF16))
            st_ref[bb] = st
            o_ref[pl.ds(r0, HG_ROWS), :] = _nn(a.astype(BF16), v16) + _nt(qe, st.astype(BF16))
            return jnp.exp(bl) * st + _tn(v16, kd)

        lax.fori_loop(0, n_blocks, block, jnp.zeros((HG_DK, HG_DK), F32))

    cspec = lambda col: pl.BlockSpec((lp, HG_DK), lambda h: (0, col // HG_DK + h))
    return pl.pallas_call(
        body, grid=(hh,),
        in_specs=[cspec(col_q), cspec(col_z), cspec(col_i),
                  pl.BlockSpec((None, 1, HG_DK), lambda h: (h, 0, 0))],
        out_specs=[pl.BlockSpec((lp, HG_DK), lambda h: (0, h)),
                   pl.BlockSpec((None, n_blocks, HG_DK, HG_DK), lambda h: (h, 0, 0, 0))],
        out_shape=[jax.ShapeDtypeStruct((lp, hh * HG_DK), F32),
                   jax.ShapeDtypeStruct((hh, n_blocks, HG_DK, HG_DK), F32)],
        scratch_shapes=[pltpu.VMEM((len(HG_HALVES), HG_ROWS, HG_ROWS), F32)],
        compiler_params=_cp("parallel"), name=name)(proj, proj, proj, lb)


def _hg_scan_bwd(proj, lb, states, do, *, reverse, col_q, col_z, col_i, hh, name):
    lp = proj.shape[0]
    n_blocks = lp // HG_ROWS
    last = 0 if reverse else HG_ROWS - 1

    def body(q_ref, z_ref, i_ref, lb_ref, st_ref, do_ref, dq_ref, dz_ref, dv_ref, dlb_ref, mask_ref):
        lbv = lb_ref[...]
        pos = lax.broadcasted_iota(jnp.int32, (HG_ROWS, 1), 0)
        ri = lax.broadcasted_iota(jnp.int32, (HG_ROWS, HG_ROWS), 0)
        ci = lax.broadcasted_iota(jnp.int32, (HG_ROWS, HG_ROWS), 1)
        _pair_masks(mask_ref)

        def block(bi, carry):
            dst, dlb = carry
            bb = bi if reverse else (n_blocks - 1 - bi)
            r0 = pl.multiple_of(bb * HG_ROWS, HG_ROWS)
            zq = q_ref[pl.ds(r0, HG_ROWS), :]
            v16 = i_ref[pl.ds(r0, HG_ROWS), :].astype(BF16)
            do16 = do_ref[pl.ds(r0, HG_ROWS), :].astype(BF16)
            qh, s, f, g, kk = _hg_gates(zq, z_ref[pl.ds(r0, HG_ROWS), :], lbv)
            b = _block_cumsum(g, pos, reverse)
            bl = b[last:last + 1, :]
            eb = jnp.exp(b)
            ebl = jnp.exp(bl - b)
            decay = jnp.exp(bl)
            qe16 = (qh * eb).astype(BF16)
            kd16 = (kk * ebl).astype(BF16)
            st = st_ref[bb]
            st16, dst16 = st.astype(BF16), dst.astype(BF16)
            same_row = ri == ci
            da = _nt(do16, v16)
            da_diag = jnp.sum(jnp.where(same_row, da, 0.0), axis=1, keepdims=True)
            dq_state = eb * _nn(do16, st16)
            dk_state = ebl * _nn(v16, dst16)
            dq = dq_state + da_diag * kk
            dk = dk_state + da_diag * qh
            dbl = (decay * jnp.sum(st * dst, axis=0, keepdims=True)
                   + jnp.sum(kk * dk_state, axis=0, keepdims=True))
            db = qh * dq_state - kk * dk_state + jnp.where(pos == last, dbl, 0.0)
            a = jnp.where(same_row, jnp.sum(qh * kk, axis=1, keepdims=True), 0.0)
            for i, (eq, ek) in enumerate(_pair_levels(b, pos, reverse)):
                same = mask_ref[i]
                q16, k16 = (qh * eq).astype(BF16), (kk * ek).astype(BF16)
                a = a + same * _nt(q16, k16)
                da16 = (same * da).astype(BF16)
                gq, gk = _nn(da16, k16), _tn(da16, q16)
                dq = dq + eq * gq
                dk = dk + ek * gk
                db = db + (q16.astype(F32) * gq - k16.astype(F32) * gk)
            dg = _block_cumsum(db, pos, not reverse)
            df = dg / f - dk
            sq = _sigmoid(zq)
            dq_ref[pl.ds(r0, HG_ROWS), :] = dq * (sq + zq * sq * (1.0 - sq))
            dz_ref[pl.ds(r0, HG_ROWS), :] = df * (1.0 - lbv) * s * (1.0 - s)
            dv_ref[pl.ds(r0, HG_ROWS), :] = _nt(kd16, dst16) + _tn(a.astype(BF16), do16)
            return (decay * dst + _tn(do16, qe16),
                    dlb + jnp.sum(df * (1.0 - s), axis=0, keepdims=True))

        _, dlb = lax.fori_loop(0, n_blocks, block,
                               (jnp.zeros((HG_DK, HG_DK), F32), jnp.zeros((1, HG_DK), F32)))
        dlb_ref[...] = dlb

    cspec = lambda col: pl.BlockSpec((lp, HG_DK), lambda h: (0, col // HG_DK + h))
    ospec = pl.BlockSpec((lp, HG_DK), lambda h: (0, h))
    sds = jax.ShapeDtypeStruct((lp, hh * HG_DK), F32)
    return pl.pallas_call(
        body, grid=(hh,),
        in_specs=[cspec(col_q), cspec(col_z), cspec(col_i),
                  pl.BlockSpec((None, 1, HG_DK), lambda h: (h, 0, 0)),
                  pl.BlockSpec((None, n_blocks, HG_DK, HG_DK), lambda h: (h, 0, 0, 0)),
                  ospec],
        out_specs=[ospec, ospec, ospec, pl.BlockSpec((None, 1, HG_DK), lambda h: (h, 0, 0))],
        out_shape=[sds, sds, sds, jax.ShapeDtypeStruct((hh, 1, HG_DK), F32)],
        scratch_shapes=[pltpu.VMEM((len(HG_HALVES), HG_ROWS, HG_ROWS), F32)],
        compiler_params=_cp("parallel"), name=name)(proj, proj, proj, lb, states, do)


NA_HB = LANES // NA_HEAD_DIM
NA_G = 4
NA_U = NA_G + NA_WIN_H
NA_QN = NA_G * GRID_W
NA_KN = NA_U * GRID_W


def _na_table_index(pattern, a, j):
    if pattern == 0:
        return j - a + NA_WIN_H - 1 if j < NA_WIN_H else None
    if pattern == 2:
        return j - a - 1 if j >= NA_U - NA_WIN_H else None
    return j - a + NA_WIN_H // 2 - 1 if a <= j < a + NA_WIN_H else None


def _na_step_rows(pattern, t, rows):
    if pattern == 0:
        r0, us = 0, 0
    elif pattern == 2:
        r0, us = rows - NA_G, rows - NA_U
    else:
        r0 = NA_G * t
        us = r0 - NA_WIN_H // 2
    q0, k0 = N_META + GRID_W * r0, N_META + GRID_W * us
    if pattern == 1:
        q0, k0 = pl.multiple_of(q0, 16), pl.multiple_of(k0, 16)
    return q0, k0


def _na_fill_bias(tb_ref, bias_ref):
    neg = jnp.full((GRID_W, GRID_W), -1e30, F32)
    for h in range(NA_HB):
        for pattern in range(3):
            for a in range(NA_G):
                for j in range(NA_U):
                    idx = _na_table_index(pattern, a, j)
                    bias_ref[h, pattern, a * GRID_W:(a + 1) * GRID_W, j * GRID_W:(j + 1) * GRID_W] = (
                        neg if idx is None else tb_ref[h, idx])


def _na_steps(rows, step, carry):
    n_steps = rows // NA_G
    carry = step(0, 0, carry)
    carry = lax.fori_loop(1, n_steps - 1, functools.partial(step, 1), carry)
    return step(2, n_steps - 1, carry)


def _na_head_lanes():
    lane = lax.broadcasted_iota(jnp.int32, (1, LANES), 1)
    return [lane // NA_HEAD_DIM == h for h in range(NA_HB)]


def _na_only(mask, x):
    return jnp.where(mask, x, jnp.zeros_like(x))


def _na_fwd(proj, tb, *, n_tok, nh, name):
    lp = proj.shape[0]
    dh, hb = NA_HEAD_DIM, NA_HB
    naw = nh * dh
    rows = n_tok // GRID_W
    scale = dh ** -0.5

    def body(q_ref, k_ref, v_ref, tb_ref, o_ref, lse_ref, q16_ref, k16_ref, v16_ref, bias_ref):
        o_ref[...] = jnp.zeros_like(o_ref)
        lse_ref[...] = jnp.zeros_like(lse_ref)
        q16_ref[...] = q_ref[...].astype(BF16)
        k16_ref[...] = k_ref[...].astype(BF16)
        v16_ref[...] = v_ref[...].astype(BF16)
        _na_fill_bias(tb_ref, bias_ref)
        heads = _na_head_lanes()
        km = k16_ref[0:N_META, :]
        vm = v16_ref[0:N_META, :]
        qm = q16_ref[0:N_META, :]
        o_m = None
        for h in range(hb):
            s = _nt(_na_only(heads[h], qm), km) * scale
            m = jnp.max(s, axis=1, keepdims=True)
            p = jnp.exp(s - m)
            l = jnp.sum(p, axis=1, keepdims=True)
            o_h = _nn(p.astype(BF16), vm) / l
            o_m = o_h if o_m is None else jnp.where(heads[h], o_h, o_m)
            lse_ref[h, 0:N_META, :] = m + jnp.log(l)
        o_ref[0:N_META, :] = o_m

        def step(pattern, t, carry):
            q0, k0 = _na_step_rows(pattern, t, rows)
            q16 = q16_ref[pl.ds(q0, NA_QN), :]
            k16 = k16_ref[pl.ds(k0, NA_KN), :]
            v16 = v16_ref[pl.ds(k0, NA_KN), :]
            o = None
            for h in range(hb):
                q_h = _na_only(heads[h], q16)
                s = _nt(q_h, k16) * scale + bias_ref[h, pattern]
                sm = _nt(q_h, km) * scale
                m = jnp.maximum(jnp.max(s, axis=1, keepdims=True),
                                jnp.max(sm, axis=1, keepdims=True))
                p = jnp.exp(s - m)
                pm = jnp.exp(sm - m)
                l = jnp.sum(p, axis=1, keepdims=True) + jnp.sum(pm, axis=1, keepdims=True)
                o_h = (_nn(p.astype(BF16), v16) + _nn(pm.astype(BF16), vm)) / l
                o = o_h if o is None else jnp.where(heads[h], o_h, o)
                lse_ref[h, pl.ds(q0, NA_QN), :] = m + jnp.log(l)
            o_ref[pl.ds(q0, NA_QN), :] = o
            return carry

        _na_steps(rows, step, 0)

    cblk = lambda col: pl.BlockSpec((lp, LANES), lambda g: (0, col // LANES + g))
    return pl.pallas_call(
        body, grid=(nh // hb,),
        in_specs=[cblk(0), cblk(naw), cblk(2 * naw),
                  pl.BlockSpec((hb, 2 * NA_WIN_H - 1, GRID_W, GRID_W), lambda g: (g, 0, 0, 0))],
        out_specs=[cblk(0), pl.BlockSpec((hb, lp, 1), lambda g: (g, 0, 0))],
        out_shape=[jax.ShapeDtypeStruct((lp, naw), F32), jax.ShapeDtypeStruct((nh, lp, 1), F32)],
        scratch_shapes=[pltpu.VMEM((lp, LANES), BF16)] * 3 + [pltpu.VMEM((hb, 3, NA_QN, NA_KN), F32)],
        compiler_params=_cp("parallel"), name=name)(proj, proj, proj, tb)


def _na_bwd(proj, tb, o, lse, do, *, n_tok, nh, name):
    lp = proj.shape[0]
    dh, hb = NA_HEAD_DIM, NA_HB
    naw = nh * dh
    rows = n_tok // GRID_W
    scale = dh ** -0.5

    def body(q_ref, k_ref, v_ref, tb_ref, o_ref, lse_ref, do_ref, dq_ref, dk_ref, dv_ref, dtb_ref,
             q16_ref, k16_ref, v16_ref, bias_ref):
        dq_ref[...] = jnp.zeros_like(dq_ref)
        dk_ref[...] = jnp.zeros_like(dk_ref)
        dv_ref[...] = jnp.zeros_like(dv_ref)
        dtb_ref[...] = jnp.zeros_like(dtb_ref)
        q16_ref[...] = q_ref[...].astype(BF16)
        k16_ref[...] = k_ref[...].astype(BF16)
        v16_ref[...] = v_ref[...].astype(BF16)
        _na_fill_bias(tb_ref, bias_ref)
        heads = _na_head_lanes()
        km = k16_ref[0:N_META, :]
        vm = v16_ref[0:N_META, :]
        qm = q16_ref[0:N_META, :]
        dom = do_ref[0:N_META, :]
        prod = dom * o_ref[0:N_META, :]
        dq_m = None
        dkm0 = jnp.zeros((N_META, LANES), F32)
        dvm0 = jnp.zeros((N_META, LANES), F32)
        for h in range(hb):
            q_h = _na_only(heads[h], qm)
            do_h = _na_only(heads[h], dom).astype(BF16)
            p = jnp.exp(_nt(q_h, km) * scale - lse_ref[h, 0:N_META, :])
            delta = jnp.sum(_na_only(heads[h], prod), axis=1, keepdims=True)
            ds = (p * (_nt(do_h, vm) - delta)).astype(BF16)
            dq_h = _nn(ds, km) * scale
            dq_m = dq_h if dq_m is None else jnp.where(heads[h], dq_h, dq_m)
            dkm0 = dkm0 + _tn(ds, q_h) * scale
            dvm0 = dvm0 + _tn(p.astype(BF16), do_h)
        dq_ref[0:N_META, :] = dq_m

        def step(pattern, t, carry):
            dkm, dvm = carry
            q0, k0 = _na_step_rows(pattern, t, rows)
            q16 = q16_ref[pl.ds(q0, NA_QN), :]
            k16 = k16_ref[pl.ds(k0, NA_KN), :]
            v16 = v16_ref[pl.ds(k0, NA_KN), :]
            dov = do_ref[pl.ds(q0, NA_QN), :]
            prod = dov * o_ref[pl.ds(q0, NA_QN), :]
            dq = None
            dk = jnp.zeros((NA_KN, LANES), F32)
            dv = jnp.zeros((NA_KN, LANES), F32)
            for h in range(hb):
                q_h = _na_only(heads[h], q16)
                do_h = _na_only(heads[h], dov).astype(BF16)
                lse = lse_ref[h, pl.ds(q0, NA_QN), :]
                p = jnp.exp(_nt(q_h, k16) * scale + bias_ref[h, pattern] - lse)
                pm = jnp.exp(_nt(q_h, km) * scale - lse)
                delta = jnp.sum(_na_only(heads[h], prod), axis=1, keepdims=True)
                ds = p * (_nt(do_h, v16) - delta)
                dsm = (pm * (_nt(do_h, vm) - delta)).astype(BF16)
                ds16 = ds.astype(BF16)
                dq_h = (_nn(ds16, k16) + _nn(dsm, km)) * scale
                dq = dq_h if dq is None else jnp.where(heads[h], dq_h, dq)
                dk = dk + _tn(ds16, q_h) * scale
                dv = dv + _tn(p.astype(BF16), do_h)
                dkm = dkm + _tn(dsm, q_h) * scale
                dvm = dvm + _tn(pm.astype(BF16), do_h)
                for a in range(NA_G):
                    for j in range(NA_U):
                        idx = _na_table_index(pattern, a, j)
                        if idx is not None:
                            dtb_ref[h, idx] += ds[a * GRID_W:(a + 1) * GRID_W,
                                                  j * GRID_W:(j + 1) * GRID_W]
            dq_ref[pl.ds(q0, NA_QN), :] = dq
            dk_ref[pl.ds(k0, NA_KN), :] += dk
            dv_ref[pl.ds(k0, NA_KN), :] += dv
            return dkm, dvm

        dkm, dvm = _na_steps(rows, step, (dkm0, dvm0))
        dk_ref[0:N_META, :] += dkm
        dv_ref[0:N_META, :] += dvm

    cblk = lambda col: pl.BlockSpec((lp, LANES), lambda g: (0, col // LANES + g))
    tbs = pl.BlockSpec((hb, 2 * NA_WIN_H - 1, GRID_W, GRID_W), lambda g: (g, 0, 0, 0))
    sds = jax.ShapeDtypeStruct((lp, naw), F32)
    return pl.pallas_call(
        body, grid=(nh // hb,),
        in_specs=[cblk(0), cblk(naw), cblk(2 * naw), tbs, cblk(0),
                  pl.BlockSpec((hb, lp, 1), lambda g: (g, 0, 0)), cblk(0)],
        out_specs=[cblk(0), cblk(0), cblk(0), tbs],
        out_shape=[sds, sds, sds, jax.ShapeDtypeStruct(tb.shape, F32)],
        scratch_shapes=[pltpu.VMEM((lp, LANES), BF16)] * 3 + [pltpu.VMEM((hb, 3, NA_QN, NA_KN), F32)],
        compiler_params=_cp("parallel"), name=name)(proj, proj, proj, tb, o, lse, do)


def _rpb_onehot():
    c = np.arange(GRID_W)[:, None]
    w = np.arange(GRID_W)[None, :]
    cs = np.clip(c - NA_WIN_W // 2, 0, GRID_W - NA_WIN_W)
    in_win = (w >= cs) & (w < cs + NA_WIN_W)
    dc = np.clip(w - c, -(NA_WIN_W - 1), NA_WIN_W - 1) + NA_WIN_W - 1
    oh = np.zeros((LANES, GRID_W * GRID_W), np.float32)
    flat = np.arange(GRID_W * GRID_W).reshape(GRID_W, GRID_W)
    oh[dc[in_win], flat[in_win]] = 1.0
    neg = np.where(in_win, 0.0, -1e30).astype(np.float32).reshape(1, -1)
    return oh, neg


def _assemble_dproj(dq_na, dk_na, dv_na, dq_f, dq_b, dz_f, dz_b, dv_f, dv_b, dg, dgn, dgh, *, name):
    lp, naw = dq_na.shape
    hgw = dq_f.shape[1]
    d = dgn.shape[1]
    cols = 3 * naw + 5 * hgw + 2 * d
    tr = _row_tile(lp)

    def body(nq_ref, nk_ref, nv_ref, qf_ref, qb_ref, zf_ref, zb_ref, vf_ref, vb_ref, g_ref, gn_ref,
             gh_ref, o_ref):
        o_ref[:, 0:naw] = nq_ref[...].astype(BF16)
        o_ref[:, naw:2 * naw] = nk_ref[...].astype(BF16)
        o_ref[:, 2 * naw:3 * naw] = nv_ref[...].astype(BF16)
        c0 = 3 * naw
        o_ref[:, c0:c0 + hgw] = (qf_ref[...] + qb_ref[...]).astype(BF16)
        o_ref[:, c0 + hgw:c0 + 2 * hgw] = zf_ref[...].astype(BF16)
        o_ref[:, c0 + 2 * hgw:c0 + 3 * hgw] = zb_ref[...].astype(BF16)
        o_ref[:, c0 + 3 * hgw:c0 + 4 * hgw] = (vf_ref[...] + vb_ref[...]).astype(BF16)
        o_ref[:, c0 + 4 * hgw:c0 + 5 * hgw] = g_ref[...]
        o_ref[:, c0 + 5 * hgw:c0 + 5 * hgw + d] = gn_ref[...]
        o_ref[:, c0 + 5 * hgw + d:] = gh_ref[...]

    hg, na = _rspec(tr, hgw), _rspec(tr, naw)
    return pl.pallas_call(
        body, grid=(lp // tr,),
        in_specs=[na, na, na, hg, hg, hg, hg, hg, hg, hg, _rspec(tr, d), _rspec(tr, d)],
        out_specs=_rspec(tr, cols),
        out_shape=jax.ShapeDtypeStruct((lp, cols), BF16),
        compiler_params=_cp("parallel"), name=name)(dq_na, dk_na, dv_na, dq_f, dq_b, dz_f, dz_b,
                                                    dv_f, dv_b, dg, dgn, dgh)


def _adamw(w, g, m, v, *, name):
    rows, cols = w.shape
    tr = 256 if rows % 256 == 0 else rows

    def body(w_ref, g_ref, m_ref, v_ref, d_ref, mo_ref, vo_ref):
        gv = g_ref[...]
        mn = ADAM_B1 * m_ref[...] + (1.0 - ADAM_B1) * gv
        vn = ADAM_B2 * v_ref[...] + (1.0 - ADAM_B2) * (gv * gv)
        m_hat = mn / (1.0 - ADAM_B1 ** ADAM_STEP)
        v_hat = vn / (1.0 - ADAM_B2 ** ADAM_STEP)
        d_ref[...] = -ADAM_LR * (m_hat / (jnp.sqrt(v_hat) + ADAM_EPS) + ADAM_WD * w_ref[...])
        mo_ref[...] = mn
        vo_ref[...] = vn

    spec = _rspec(tr, cols)
    sds = jax.ShapeDtypeStruct((rows, cols), F32)
    return pl.pallas_call(
        body, grid=(rows // tr,), in_specs=[spec] * 4, out_specs=[spec] * 3, out_shape=[sds] * 3,
        compiler_params=_cp("parallel"), name=name)(w, g, m, v)


def _local_step(x, tgt, meta, first_weight, rest_weights, g_mix, g_mlp, g_fin, hg_gain, rpb, lb,
                early_grads=None, mid_grads=None, late_grad=None):
    n_tok, d = x.shape
    hgw = hg_gain.shape[1]
    nh, hh = rpb.shape[0], hgw // HG_DK
    naw = nh * NA_HEAD_DIM
    l_real = N_META + n_tok
    lp = -(-l_real // ROW_ALIGN) * ROW_ALIGN
    n_chunks = l_real // HG_CHUNK
    pad = lp - l_real
    col_qhg = 3 * naw
    col_zf, col_zb, col_i, col_g = (col_qhg + hgw, col_qhg + 2 * hgw, col_qhg + 3 * hgw,
                                    col_qhg + 4 * hgw)
    col_gate = col_qhg + 5 * hgw

    zpad = jnp.zeros((pad, d), F32)
    h0 = jnp.concatenate([meta, x, zpad], axis=0)
    tgt_p = jnp.concatenate([jnp.zeros((N_META, d), F32), tgt, zpad], axis=0)

    oh_np, neg_np = _rpb_onehot()
    oh = jnp.asarray(oh_np)
    rpb_p = jnp.pad(rpb.reshape(nh * (2 * NA_WIN_H - 1), 2 * NA_WIN_W - 1),
                    ((0, 0), (0, LANES - (2 * NA_WIN_W - 1))))
    tb = _matmul(rpb_p, oh, tm=rpb_p.shape[0], tn=512, tk=LANES, precision=HIGHEST,
                 name="rpb_expand")
    tb = (tb + jnp.asarray(neg_np)).reshape(nh, 2 * NA_WIN_H - 1, GRID_W, GRID_W)

    a = _rmsnorm_fwd(h0, g_mix, name="norm_mix")
    w_in = first_weight(a)
    proj = _matmul(a, w_in, name="mm_in")
    o_na, lse = _na_fwd(proj, tb, n_tok=n_tok, nh=nh, name="na_fwd")
    lb_f = lb[0].reshape(hh, 1, HG_DK)
    lb_b = lb[1].reshape(hh, 1, HG_DK)
    scan_kw = dict(col_q=col_qhg, col_i=col_i, hh=hh)
    o_f, st_f = _hg_scan_fwd(proj, lb_f, reverse=False, col_z=col_zf, name="hg_scan_f", **scan_kw)
    o_b, st_b = _hg_scan_fwd(proj, lb_b, reverse=True, col_z=col_zb, name="hg_scan_b", **scan_kw)
    o_hg = _hg_out(o_f, o_b, proj, hg_gain, col_g=col_g, name="hg_out")
    w_na, w_hg, w_o, w_up, w_down = rest_weights(o_hg)
    y_na = _matmul(o_na, w_na, name="mm_na_out")
    gates = ((proj, col_gate), (proj, col_gate + d))

    def mix_gates(acc, gn, gh, yn):
        return acc, _sigmoid(gn) * yn + _sigmoid(gh) * acc

    def mix_gates_bwd(dmix, gn, gh, yn, yh):
        sn, sh = _sigmoid(gn), _sigmoid(gh)
        return dmix * sn, dmix * sh, dmix * yn * sn * (1.0 - sn), dmix * yh * sh * (1.0 - sh)

    y_hg, mix = _matmul(o_hg, w_hg, name="mm_hg_out", epilogue=mix_gates,
                        tiles=(*gates, (y_na, 0)), out_dtypes=(F32, BF16))
    t1 = _matmul(mix, w_o, name="mm_o")
    h1, mlp_in = _residual_norm(h0, t1, g_mlp, name="resid_norm_mlp")
    u, act = _matmul(mlp_in, w_up, name="mm_up", out_dtypes=(BF16, BF16),
                     epilogue=lambda acc: (acc, jnp.square(jnp.maximum(acc, 0.0))))
    t2 = _matmul(act, w_down, name="mm_down")
    dh2, loss, dg_fin = _final_loss(h1, t2, g_fin, tgt_p, n_tok=n_tok, name="final_loss")

    (du,) = _matmul(dh2, w_down, tb=True, name="mm_down_dx", tiles=((u, 0),), out_dtypes=(BF16,),
                    epilogue=lambda acc, uv: (acc * 2.0 * jnp.maximum(uv, 0.0),))
    dw_down = _matmul(act, dh2, ta=True, name="mm_down_dw")
    dm = _matmul(du, w_up, tb=True, name="mm_up_dx")
    dw_up = _matmul(mlp_in, du, ta=True, name="mm_up_dw")
    dh1, dg_mlp = _rmsnorm_bwd_add(h1, g_mlp, dm, dh2, name="norm_mlp_bwd")
    dy_na, dy_hg, dgn, dgh = _matmul(dh1, w_o, tb=True, name="mm_o_dx", epilogue=mix_gates_bwd,
                                     tiles=(*gates, (y_na, 0), (y_hg, 0)), out_dtypes=(BF16,) * 4)
    dw_o = _matmul(mix, dh1, ta=True, name="mm_o_dw")
    do_na = _matmul(dy_na, w_na, tb=True, name="mm_na_out_dx")
    dw_na = _matmul(o_na, dy_na, ta=True, name="mm_na_out_dw")
    do_hg = _matmul(dy_hg, w_hg, tb=True, name="mm_hg_out_dx")
    dw_hg = _matmul(o_hg, dy_hg, ta=True, name="mm_hg_out_dw")
    token = early_grads([dw_na, dw_hg, dw_o, dw_up, dw_down]) if early_grads else None
    if token is not None:
        hg_gain = hg_gain + token[0:1, 0:1]
    d_o, dg_hg, d_gain = _hg_out_bwd(o_f, o_b, proj, hg_gain, do_hg, col_g=col_g, name="hg_out_bwd")
    dq_f, dz_f, dv_f, dlb_f = _hg_scan_bwd(proj, lb_f, st_f, d_o, reverse=False, col_z=col_zf,
                                           name="hg_scan_f_bwd", **scan_kw)
    token = mid_grads(dq_f) if mid_grads else None
    lb_b_late = lb_b if token is None else lb_b + token[0:1, 0:1]
    dq_b, dz_b, dv_b, dlb_b = _hg_scan_bwd(proj, lb_b_late, st_b, d_o, reverse=True, col_z=col_zb,
                                           name="hg_scan_b_bwd", **scan_kw)
    dq_na, dk_na, dv_na, dtb = _na_bwd(proj, tb, o_na, lse, do_na, n_tok=n_tok, nh=nh, name="na_bwd")
    dproj = _assemble_dproj(dq_na, dk_na, dv_na, dq_f, dq_b, dz_f, dz_b, dv_f, dv_b, dg_hg, dgn,
                            dgh, name="assemble_dproj")
    dw_in = _matmul(a, dproj, ta=True, name="mm_in_dw")
    token = late_grad(dw_in) if late_grad else None
    da = _matmul(dproj, w_in, tb=True, name="mm_in_dx", after=token)
    dh0, dg_mix = _rmsnorm_bwd_add(h0, g_mix, da, dh1, name="norm_mix_bwd")
    d_rpb = _matmul(dtb.reshape(nh * (2 * NA_WIN_H - 1), GRID_W * GRID_W), oh, tb=True,
                    tm=nh * (2 * NA_WIN_H - 1), tn=LANES, tk=1024, precision=HIGHEST,
                    name="rpb_reduce")
    d_lb = jnp.concatenate([dlb_f.reshape(1, hgw), dlb_b.reshape(1, hgw)], axis=0)
    return (loss, dh0[N_META:l_real], dh0[:N_META], dw_in, dw_na, dw_hg, dw_o, dw_up, dw_down,
            dg_mix, dg_mlp, dg_fin, d_gain, d_rpb, d_lb)


N_CHIPS = 4
N_DEV = 8
ANY = pl.BlockSpec(memory_space=pl.ANY)


def _place():
    x, y, c = lax.axis_index("x"), lax.axis_index("y"), lax.axis_index("c")
    others = []
    for j in (1, 2, 3):
        tx = (1 - x) if (j >> 1) else x
        ty = (1 - y) if (j & 1) else y
        others.append((tx, ty))
    return x, y, c, others


def _piece(ref, axis, k, half, rh, cs):
    if axis == 1:
        return ref.at[pl.ds(pl.multiple_of(half * rh, 16), rh), pl.ds(pl.multiple_of(k * cs, LANES), cs)]
    return ref.at[pl.ds(pl.multiple_of(k * 2 * rh + half * rh, 16), rh), :]


def _cast_into_full(shard, axis, place, *, name):
    r, cs = shard.shape
    full = (r, cs * N_CHIPS) if axis == 1 else (r * N_CHIPS, cs)
    tr = next(t for t in (256, 128, 64, 32, 16) if r % t == 0)
    nt = r // tr

    def body(p_ref, s_ref, o_ref):
        o_ref[...] = s_ref[...].astype(BF16)

    if axis == 1:
        omap = lambda i, p_ref: (i, p_ref[0])
    else:
        omap = lambda i, p_ref: (p_ref[0] * nt + i, 0)
    return pl.pallas_call(
        body,
        grid_spec=pltpu.PrefetchScalarGridSpec(
            num_scalar_prefetch=1, grid=(nt,),
            in_specs=[pl.BlockSpec((tr, cs), lambda i, p_ref: (i, 0))],
            out_specs=pl.BlockSpec((tr, cs), omap)),
        out_shape=jax.ShapeDtypeStruct(full, BF16),
        compiler_params=_cp("parallel"), name=name)(place, shard)


HBM_SPEC = pl.BlockSpec(memory_space=pltpu.HBM)
SEM_SPEC = pl.BlockSpec(memory_space=pltpu.SEMAPHORE)
SPLIT_COPY = pltpu.CompilerParams(has_side_effects=pltpu.SideEffectType.DATAFLOW_SIDE_EFFECTING)
TOKEN = jax.ShapeDtypeStruct((8, LANES), F32)


def _geo(fulls, axes):
    out = []
    for f, ax in zip(fulls, axes):
        r, cs = (f.shape[0], f.shape[1] // N_CHIPS) if ax == 1 else (f.shape[0] // N_CHIPS, f.shape[1])
        out.append((ax, r // 2, cs))
    return out


def _gather_copies(refs, geo, send_sems, recv_sems):
    x, y, c, others = _place()
    chip = 2 * x + y
    cps = []
    for i, (ax, rh, cs) in enumerate(geo):
        mine = _piece(refs[i], ax, chip, c, rh, cs)
        for j, (tx, ty) in enumerate(others):
            cps.append(pltpu.make_async_remote_copy(
                src_ref=mine, dst_ref=mine, send_sem=send_sems.at[3 * i + j],
                recv_sem=recv_sems.at[3 * i + j], device_id=(tx, ty, c), device_id_type=MESH))
    return cps


def _allgather_start(fulls, axes, after, *, name):
    n = len(fulls)
    geo = _geo(fulls, axes)

    def body(*refs):
        w_refs = refs[:n]
        send_sems, recv_sems = refs[n + 1], refs[n + 2]
        token = refs[2 * n + 3]
        for cp in _gather_copies(w_refs, geo, send_sems, recv_sems):
            cp.start()
        token[...] = jnp.zeros_like(token)

    out = pl.pallas_call(
        body, name=name,
        out_shape=(pltpu.SemaphoreType.DMA((3 * n,)), pltpu.SemaphoreType.DMA((3 * n,)),
                   *[pltpu.HBM(f.shape, f.dtype) for f in fulls], TOKEN),
        in_specs=[HBM_SPEC] * n + [ANY],
        out_specs=(SEM_SPEC, SEM_SPEC, *[HBM_SPEC] * n, pl.BlockSpec(memory_space=pltpu.VMEM)),
        input_output_aliases={i: 2 + i for i in range(n)},
        compiler_params=SPLIT_COPY,
    )(*[pltpu.with_memory_space_constraint(f, pltpu.HBM) for f in fulls], after)
    return out[0], out[1], list(out[2:2 + n]), out[2 + n]


def _allgather_wait(send_sems, recv_sems, fulls, axes, after, *, name):
    n = len(fulls)
    geo = _geo(fulls, axes)

    def body(*refs):
        w_refs = refs[:n]
        for cp in _gather_copies(w_refs, geo, refs[n], refs[n + 1]):
            cp.wait_send()
            cp.wait_recv()

    return list(pl.pallas_call(
        body, name=name,
        out_shape=[pltpu.HBM(f.shape, f.dtype) for f in fulls],
        in_specs=[HBM_SPEC] * n + [SEM_SPEC, SEM_SPEC, ANY],
        out_specs=[HBM_SPEC] * n,
        input_output_aliases={i: i for i in range(n)},
        compiler_params=SPLIT_COPY,
    )(*fulls, send_sems, recv_sems, after))


def _allgather_forward(fulls, axes, *, name):
    n = len(fulls)
    geo = _geo(fulls, axes)

    def body(*refs):
        o_refs = refs[n:2 * n]
        send_sems, recv_sems = refs[2 * n:]
        x, y, c, others = _place()

        def rcopy(i, j, half, to):
            ax, rh, cs = geo[i]
            ref = _piece(o_refs[i], ax, 2 * others[j][0] + others[j][1], half, rh, cs)
            return pltpu.make_async_remote_copy(
                src_ref=ref, dst_ref=ref, send_sem=send_sems.at[3 * i + j],
                recv_sem=recv_sems.at[3 * i + j], device_id=to, device_id_type=MESH)

        cps = [rcopy(i, j, c, (x, y, 1 - c)) for i in range(n) for j in range(3)]
        for cp in cps:
            cp.start()
        for i in range(n):
            for j in range(3):
                rcopy(i, j, 1 - c, (x, y, c)).wait_recv()
        for cp in cps:
            cp.wait_send()

    return list(pl.pallas_call(
        body, in_specs=[ANY] * n, out_specs=[ANY] * n,
        out_shape=[jax.ShapeDtypeStruct(f.shape, f.dtype) for f in fulls],
        input_output_aliases={i: i for i in range(n)},
        scratch_shapes=[pltpu.SemaphoreType.DMA((3 * n,)), pltpu.SemaphoreType.DMA((3 * n,))],
        name=name)(*fulls))


def _scatter_geo(parts, axes):
    out = []
    for p, ax in zip(parts, axes):
        _, rh, cols = p.shape
        out.append((ax, rh, cols // N_CHIPS if ax == 1 else cols))
    return out


def _scatter_copies(p_refs, q_refs, geo, send_sems, recv_sems):
    x, y, c, others = _place()
    chip = 2 * x + y
    cps = []
    for i, (ax, rh, cw) in enumerate(geo):
        for j, (tx, ty) in enumerate(others):
            k = 2 * tx + ty
            src = (p_refs[i].at[0, :, pl.ds(pl.multiple_of(k * cw, LANES), cw)] if ax == 1
                   else p_refs[i].at[k])
            cps.append(pltpu.make_async_remote_copy(
                src_ref=src, dst_ref=q_refs[i].at[chip], send_sem=send_sems.at[3 * i + j],
                recv_sem=recv_sems.at[3 * i + j], device_id=(tx, ty, c), device_id_type=MESH))
    return cps


def _scatter_start(parts, axes, *, name):
    n = len(parts)
    geo = _scatter_geo(parts, axes)
    slots = [pltpu.HBM((N_CHIPS, rh, cw), p.dtype) for p, (_, rh, cw) in zip(parts, geo)]

    def body(*refs):
        p_refs, q_refs = refs[:n], refs[n:2 * n]
        send_sems, recv_sems = refs[2 * n], refs[2 * n + 1]
        token = refs[4 * n + 2]
        for cp in _scatter_copies(p_refs, q_refs, geo, send_sems, recv_sems):
            cp.start()
        token[...] = jnp.zeros_like(token)

    land = [pltpu.with_memory_space_constraint(lax.empty(s.inner_aval.shape, s.inner_aval.dtype), pltpu.HBM)
            for s in slots]
    out = pl.pallas_call(
        body, name=name,
        out_shape=(pltpu.SemaphoreType.DMA((3 * n,)), pltpu.SemaphoreType.DMA((3 * n,)),
                   *[pltpu.HBM(p.shape, p.dtype) for p in parts], *slots, TOKEN),
        in_specs=[HBM_SPEC] * (2 * n),
        out_specs=(SEM_SPEC, SEM_SPEC, *[HBM_SPEC] * (2 * n), pl.BlockSpec(memory_space=pltpu.VMEM)),
        input_output_aliases={i: 2 + i for i in range(2 * n)},
        compiler_params=SPLIT_COPY,
    )(*[pltpu.with_memory_space_constraint(p, pltpu.HBM) for p in parts], *land)
    return out[0], out[1], list(out[2:2 + n]), list(out[2 + n:2 + 2 * n]), out[2 + 2 * n]


def _scatter_wait(send_sems, recv_sems, parts, slots, axes, after, *, name):
    n = len(parts)
    geo = _scatter_geo(parts, axes)

    def body(*refs):
        p_refs, q_refs = refs[:n], refs[n:2 * n]
        for cp in _scatter_copies(p_refs, q_refs, geo, refs[2 * n], refs[2 * n + 1]):
            cp.wait_send()
            cp.wait_recv()

    out = pl.pallas_call(
        body, name=name,
        out_shape=[pltpu.HBM(a.shape, a.dtype) for a in (*parts, *slots)],
        in_specs=[HBM_SPEC] * (2 * n) + [SEM_SPEC, SEM_SPEC, ANY],
        out_specs=[HBM_SPEC] * (2 * n),
        input_output_aliases={i: i for i in range(2 * n)},
        compiler_params=SPLIT_COPY,
    )(*parts, *slots, send_sems, recv_sems, after)
    return list(out[:n]), list(out[n:])


def _sibling_swap(grads, *, name):
    n = len(grads)
    out_shape = [jax.ShapeDtypeStruct((g.shape[0], g.shape[1] // 2, g.shape[2]), g.dtype)
                 for g in grads]

    def body(*refs):
        g_refs, o_refs = refs[:n], refs[n:2 * n]
        send_sems, recv_sems = refs[2 * n:]
        x, y, c, _ = _place()
        cps = []
        for i in range(n):
            rh = grads[i].shape[1] // 2
            src = g_refs[i].at[:, pl.ds(pl.multiple_of((1 - c) * rh, 16), rh), :]
            cp = pltpu.make_async_remote_copy(
                src_ref=src, dst_ref=o_refs[i], send_sem=send_sems.at[i], recv_sem=recv_sems.at[i],
                device_id=(x, y, 1 - c), device_id_type=MESH)
            cp.start()
            cps.append(cp)
        for cp in cps:
            cp.wait()

    return pl.pallas_call(
        body, in_specs=[ANY] * n, out_specs=[ANY] * n, out_shape=out_shape,
        scratch_shapes=[pltpu.SemaphoreType.DMA((n,)), pltpu.SemaphoreType.DMA((n,))],
        name=name)(*grads)


def _swap_copies(g_refs, r_refs, shapes, send_sems, recv_sems):
    x, y, c, _ = _place()
    cps = []
    for i, shape in enumerate(shapes):
        rh = shape[1] // 2
        src = g_refs[i].at[:, pl.ds(pl.multiple_of((1 - c) * rh, 16), rh), :]
        cps.append(pltpu.make_async_remote_copy(
            src_ref=src, dst_ref=r_refs[i], send_sem=send_sems.at[i], recv_sem=recv_sems.at[i],
            device_id=(x, y, 1 - c), device_id_type=MESH))
    return cps


def _sibling_swap_start(grads, *, name):
    n = len(grads)
    shapes = [g.shape for g in grads]
    lands = [pltpu.HBM((s[0], s[1] // 2, s[2]), g.dtype) for s, g in zip(shapes, grads)]

    def body(*refs):
        g_refs, r_refs = refs[:n], refs[n:2 * n]
        token = refs[4 * n + 2]
        for cp in _swap_copies(g_refs, r_refs, shapes, refs[2 * n], refs[2 * n + 1]):
            cp.start()
        token[...] = jnp.zeros_like(token)

    land = [pltpu.with_memory_space_constraint(lax.empty(s.inner_aval.shape, s.inner_aval.dtype), pltpu.HBM)
            for s in lands]
    out = pl.pallas_call(
        body, name=name,
        out_shape=(pltpu.SemaphoreType.DMA((n,)), pltpu.SemaphoreType.DMA((n,)),
                   *[pltpu.HBM(g.shape, g.dtype) for g in grads], *lands, TOKEN),
        in_specs=[HBM_SPEC] * (2 * n),
        out_specs=(SEM_SPEC, SEM_SPEC, *[HBM_SPEC] * (2 * n), pl.BlockSpec(memory_space=pltpu.VMEM)),
        input_output_aliases={i: 2 + i for i in range(2 * n)},
        compiler_params=SPLIT_COPY,
    )(*[pltpu.with_memory_space_constraint(g, pltpu.HBM) for g in grads], *land)
    return out[0], out[1], list(out[2:2 + n]), list(out[2 + n:2 + 2 * n]), out[2 + 2 * n]


def _sibling_swap_wait(send_sems, recv_sems, grads, lands, after, *, name):
    n = len(grads)
    shapes = [g.shape for g in grads]

    def body(*refs):
        g_refs, r_refs = refs[:n], refs[n:2 * n]
        for cp in _swap_copies(g_refs, r_refs, shapes, refs[2 * n], refs[2 * n + 1]):
            cp.wait_send()
            cp.wait_recv()

    out = pl.pallas_call(
        body, name=name,
        out_shape=[pltpu.HBM(a.shape, a.dtype) for a in (*grads, *lands)],
        in_specs=[HBM_SPEC] * (2 * n) + [SEM_SPEC, SEM_SPEC, ANY],
        out_specs=[HBM_SPEC] * (2 * n),
        input_output_aliases={i: i for i in range(2 * n)},
        compiler_params=SPLIT_COPY,
    )(*grads, *lands, send_sems, recv_sems, after)
    return list(out[:n]), list(out[n:])


def _pair_add(g3, rx, c_arr, *, out_dtype, name):
    nb, rows, cols = g3.shape
    rh = rows // 2
    tr = next(t for t in (128, 64, 32, 16) if rh % t == 0)
    nt = rh // tr

    def body(c_ref, g_ref, r_ref, o_ref):
        o_ref[...] = (g_ref[...] + r_ref[...]).astype(out_dtype)

    return pl.pallas_call(
        body,
        grid_spec=pltpu.PrefetchScalarGridSpec(
            num_scalar_prefetch=1, grid=(nb, nt),
            in_specs=[pl.BlockSpec((None, tr, cols), lambda b, i, c_ref: (b, c_ref[0] * nt + i, 0)),
                      pl.BlockSpec((None, tr, cols), lambda b, i, c_ref: (b, i, 0))],
            out_specs=pl.BlockSpec((None, tr, cols), lambda b, i, c_ref: (b, i, 0))),
        out_shape=jax.ShapeDtypeStruct((nb, rh, cols), out_dtype),
        compiler_params=_cp("parallel", "parallel"), name=name)(c_arr, g3, rx)


def _sum_slots(q, *, name):
    ns, rows, cols = q.shape
    tr = next(t for t in (128, 64, 32, 16, 8) if rows % t == 0)

    def body(q_ref, o_ref):
        acc = q_ref[0].astype(F32)
        for k in range(1, ns):
            acc = acc + q_ref[k].astype(F32)
        o_ref[...] = acc

    return pl.pallas_call(
        body, grid=(rows // tr,),
        in_specs=[pl.BlockSpec((ns, tr, cols), lambda i: (0, i, 0))],
        out_specs=_rspec(tr, cols),
        out_shape=jax.ShapeDtypeStruct((rows, cols), F32),
        compiler_params=_cp("parallel"), name=name)(q)


def _sum_chips(q, p, place, axis, *, name):
    _, rh, cw = q.shape
    tr = next(t for t in (128, 64, 32, 16) if rh % t == 0)
    nt = rh // tr

    def body(p_ref, *refs):
        q_refs, own_ref, o_ref = refs[:N_CHIPS], refs[N_CHIPS], refs[N_CHIPS + 1]
        chip = p_ref[0]
        acc = jnp.where(chip == 0, own_ref[...], q_refs[0][...]).astype(F32)
        for k in range(1, N_CHIPS):
            acc = acc + jnp.where(chip == k, own_ref[...], q_refs[k][...]).astype(F32)
        o_ref[...] = acc

    def slot_spec(k):
        return pl.BlockSpec((None, tr, cw),
                            lambda i, p_ref: (jnp.where(p_ref[0] == k, (k + 1) % N_CHIPS, k), i, 0))

    if axis == 1:
        own_spec = pl.BlockSpec((None, tr, cw), lambda i, p_ref: (0, i, p_ref[0]))
    else:
        own_spec = pl.BlockSpec((None, tr, cw), lambda i, p_ref: (p_ref[0], i, 0))
    return pl.pallas_call(
        body,
        grid_spec=pltpu.PrefetchScalarGridSpec(
            num_scalar_prefetch=1, grid=(nt,),
            in_specs=[slot_spec(k) for k in range(N_CHIPS)] + [own_spec],
            out_specs=pl.BlockSpec((tr, cw), lambda i, p_ref: (p_ref[1] * nt + i, 0))),
        out_shape=jax.ShapeDtypeStruct((2 * rh, cw), F32),
        compiler_params=_cp("parallel"), name=name)(place, *([q] * N_CHIPS), p)


def _sibling_share(shards, *, name):
    n = len(shards)

    def body(*refs):
        o_refs = refs[n:2 * n]
        send_sems, recv_sems = refs[2 * n:]
        x, y, c, _ = _place()
        cps = []
        for i in range(n):
            rh = shards[i].shape[0] // 2
            mine = o_refs[i].at[pl.ds(pl.multiple_of(c * rh, 8), rh), :]
            cp = pltpu.make_async_remote_copy(
                src_ref=mine, dst_ref=mine, send_sem=send_sems.at[i], recv_sem=recv_sems.at[i],
                device_id=(x, y, 1 - c), device_id_type=MESH)
            cp.start()
            cps.append(cp)
        for i in range(n):
            rh = shards[i].shape[0] // 2
            theirs = o_refs[i].at[pl.ds(pl.multiple_of((1 - c) * rh, 8), rh), :]
            pltpu.make_async_remote_copy(
                src_ref=theirs, dst_ref=theirs, send_sem=send_sems.at[i], recv_sem=recv_sems.at[i],
                device_id=(x, y, c), device_id_type=MESH).wait_recv()
        for cp in cps:
            cp.wait_send()

    return pl.pallas_call(
        body, in_specs=[ANY] * n, out_specs=[ANY] * n,
        out_shape=[jax.ShapeDtypeStruct(h.shape, h.dtype) for h in shards],
        input_output_aliases={i: i for i in range(n)},
        scratch_shapes=[pltpu.SemaphoreType.DMA((n,)), pltpu.SemaphoreType.DMA((n,))],
        name=name)(*shards)


def _gather_all(blk, *, name, after=None):
    rows, cols = blk.shape
    extra = [] if after is None else [after]

    def body(x_ref, *refs):
        out_ref, send_sems, recv_sems, local_sem = refs[len(extra):]
        x, y, c = lax.axis_index("x"), lax.axis_index("y"), lax.axis_index("c")
        me = 4 * x + 2 * y + c
        mine = pltpu.make_async_copy(x_ref, out_ref.at[me], local_sem)
        mine.start()
        cps = []
        for k in range(1, N_DEV):
            tx = (1 - x) if (k >> 2) & 1 else x
            ty = (1 - y) if (k >> 1) & 1 else y
            tc = (1 - c) if k & 1 else c
            cp = pltpu.make_async_remote_copy(
                src_ref=x_ref, dst_ref=out_ref.at[me], send_sem=send_sems.at[k - 1],
                recv_sem=recv_sems.at[k - 1], device_id=(tx, ty, tc), device_id_type=MESH)
            cp.start()
            cps.append(cp)
        for k in range(1, N_DEV):
            tx = (1 - x) if (k >> 2) & 1 else x
            ty = (1 - y) if (k >> 1) & 1 else y
            tc = (1 - c) if k & 1 else c
            got = out_ref.at[4 * tx + 2 * ty + tc]
            pltpu.make_async_remote_copy(
                src_ref=got, dst_ref=got, send_sem=send_sems.at[k - 1], recv_sem=recv_sems.at[k - 1],
                device_id=(x, y, c), device_id_type=MESH).wait_recv()
        for cp in cps:
            cp.wait_send()
        mine.wait()

    vm = pl.BlockSpec(memory_space=pltpu.VMEM)
    return pl.pallas_call(
        body, in_specs=[vm] + [ANY] * len(extra), out_specs=vm,
        out_shape=jax.ShapeDtypeStruct((N_DEV, rows, cols), blk.dtype),
        scratch_shapes=[pltpu.SemaphoreType.DMA((N_DEV - 1,)), pltpu.SemaphoreType.DMA((N_DEV - 1,)),
                        pltpu.SemaphoreType.DMA],
        name=name)(blk, *extra)


def _as_rows(a):
    flat = a.reshape(-1)
    n = flat.shape[0]
    rows = -(-n // (8 * LANES)) * 8
    return jnp.pad(flat, (0, rows * LANES - n)).reshape(rows, LANES)


def _from_rows(p, shape):
    n = int(np.prod(shape))
    return p.reshape(-1)[:n].reshape(shape)


WEIGHT_AXES = (1, 1, 1, 0, 1, 0)
WIRE = BF16


def kernel(x, meta_tokens, w_in, w_na_out, w_hg_out, w_o, w_up, w_down, norm_mix, norm_mlp, norm_final, hg_norm, na_rpb, hg_lb_logits, loss_target, m_meta_tokens, m_w_in, m_w_na_out, m_w_hg_out, m_w_o, m_w_up, m_w_down, m_norm_mix, m_norm_mlp, m_norm_final, m_hg_norm, m_na_rpb, m_hg_lb_logits, v_meta_tokens, v_w_in, v_w_na_out, v_w_hg_out, v_w_o, v_w_up, v_w_down, v_norm_mix, v_norm_mlp, v_norm_final, v_hg_norm, v_na_rpb, v_hg_lb_logits):
    xi, yi, ci = lax.axis_index("x"), lax.axis_index("y"), lax.axis_index("c")
    chip = 2 * xi + yi
    d = x.shape[-1]
    dshard = meta_tokens.shape[1]
    hgw = hg_norm.shape[1]
    lbs = hg_lb_logits.shape[2]
    big = [w_in[0], w_na_out[0], w_hg_out[0], w_o[0], w_up[0], w_down[0]]
    big_m = [m_w_in[0], m_w_na_out[0], m_w_hg_out[0], m_w_o[0], m_w_up[0], m_w_down[0]]
    big_v = [v_w_in[0], v_w_na_out[0], v_w_hg_out[0], v_w_o[0], v_w_up[0], v_w_down[0]]

    place = jnp.stack([chip, ci]).astype(jnp.int32)
    own_w = [_cast_into_full(w, ax, place, name=f"cast_shard_{i}")
             for i, (w, ax) in enumerate(zip(big, WEIGHT_AXES))]
    in_axes, rest_axes = WEIGHT_AXES[:1], WEIGHT_AXES[1:]
    small_in = jnp.concatenate([_as_rows(meta_tokens), _as_rows(hg_lb_logits)], axis=0)
    small_all = _gather_all(small_in, name="gather_small_params")[0::2]
    in_send, in_recv, in_bufs, in_token = _allgather_start(own_w[:1], in_axes, small_all,
                                                           name="weight_allgather_in_start")
    ag_send, ag_recv, ag_bufs, ag_token = _allgather_start(own_w[1:], rest_axes, in_token,
                                                           name="weight_allgather_rest_start")

    def first_weight(after):
        got = _allgather_wait(in_send, in_recv, in_bufs, in_axes, after,
                              name="weight_allgather_in_wait")
        return _allgather_forward(got, in_axes, name="weight_allgather_in_forward")[0]

    def rest_weights(after):
        got = _allgather_wait(ag_send, ag_recv, ag_bufs, rest_axes, after,
                              name="weight_allgather_rest_wait")
        return _allgather_forward(got, rest_axes, name="weight_allgather_rest_forward")

    n_meta_rows = N_META * dshard // LANES
    meta_full = (small_all[:, :n_meta_rows].reshape(N_CHIPS, N_META, dshard)
                 .transpose(1, 0, 2).reshape(N_META, d))
    lbl_full = (small_all[:, n_meta_rows:].reshape(N_CHIPS, -1)[:, :4 * lbs]
                .reshape(N_CHIPS, 2, 2, lbs).transpose(1, 2, 0, 3).reshape(2, 2, N_CHIPS * lbs))
    lb = jax.nn.softmax(lbl_full, axis=1)[:, 0]

    c_arr = ci.reshape(1).astype(jnp.int32)

    def by_chip(dws, axes):
        return [g.reshape(1, *g.shape) if ax == 1
                else g.reshape(N_CHIPS, g.shape[0] // N_CHIPS, g.shape[1]) for g, ax in zip(dws, axes)]

    flying = {}

    def scatter(tag, axes, g3, rx):
        parts = [_pair_add(g, r, c_arr, out_dtype=WIRE, name=f"grad_pair_add_{tag}_{i}")
                 for i, (g, r) in enumerate(zip(g3, rx))]
        send, recv, parts, slots, token = _scatter_start(parts, axes,
                                                         name=f"grad_scatter_{tag}_start")
        flying[tag] = (send, recv, parts, slots)
        return token

    def swap_rest(dws):
        send, recv, g3, lands, token = _sibling_swap_start(by_chip(dws, rest_axes),
                                                           name="grad_sibling_swap_rest_start")
        flying["swap"] = (send, recv, g3, lands)
        return token

    def scatter_rest(after):
        g3, rx = _sibling_swap_wait(*flying["swap"], after, name="grad_sibling_swap_rest_wait")
        return scatter("rest", rest_axes, g3, rx)

    def scatter_in(dw_in):
        g3 = by_chip([dw_in], in_axes)
        return scatter("in", in_axes, g3, _sibling_swap(g3, name="grad_sibling_swap_in"))

    def landed(tag, axes, after):
        parts, slots = _scatter_wait(*flying[tag], axes, after, name=f"grad_scatter_{tag}_wait")
        return [_sum_chips(q, p, place, ax, name=f"grad_sum_chips_{tag}_{i}")
                for i, (q, p, ax) in enumerate(zip(slots, parts, axes))]

    (loss, dx, dmeta, *_, dg_mix, dg_mlp, dg_fin, d_gain, d_rpb, d_lb) = _local_step(
        x[0], loss_target[0], meta_full, first_weight, rest_weights,
        norm_mix + ag_token[0:1, 0:1], norm_mlp, norm_final.reshape(1, d), hg_norm, na_rpb[0], lb,
        swap_rest, scatter_rest, scatter_in)

    halves_rest = landed("rest", rest_axes, dx)
    halves_in = landed("in", in_axes, halves_rest[-1])
    g_big = _sibling_share(halves_in + halves_rest, name="grad_sibling_share")

    d_rpb_c = d_rpb[:, :2 * NA_WIN_W - 1]
    small_g = [dmeta, dg_mix, dg_mlp, dg_fin, d_gain, d_rpb_c, d_lb, loss]
    packed = jnp.concatenate([_as_rows(a) for a in small_g], axis=0)
    total = _sum_slots(_gather_all(packed, after=halves_in[0], name="gather_small_grads"),
                       name="sum_small_grads")
    offs = np.cumsum([0] + [_as_rows(a).shape[0] for a in small_g])
    take = lambda i, shape: _from_rows(total[offs[i]:offs[i + 1]], shape)
    g_meta_full = take(0, (N_META, d))
    g_norm_mix, g_norm_mlp = take(1, (1, d)), take(2, (1, d))
    g_norm_final = take(3, (d,))
    g_hg_norm = take(4, (1, hgw))
    g_rpb = take(5, na_rpb.shape)
    g_lb = take(6, (2, hgw))
    loss_total = take(7, (1, LANES))[0, 0]
    g_meta = lax.dynamic_slice_in_dim(g_meta_full, chip * dshard, dshard, axis=1)
    dl0 = lb * (1.0 - lb) * g_lb
    g_lbl_full = jnp.stack([dl0, -dl0], axis=1)
    g_lbl = lax.dynamic_slice_in_dim(g_lbl_full, chip * lbs, lbs, axis=2)

    big_out = [_adamw(w, g, m, v, name=f"adamw_{i}")
               for i, (w, g, m, v) in enumerate(zip(big, g_big, big_m, big_v))]
    small_w = [meta_tokens, norm_mix, norm_mlp, norm_final, hg_norm, na_rpb, hg_lb_logits]
    small_gr = [g_meta, g_norm_mix, g_norm_mlp, g_norm_final, g_hg_norm, g_rpb, g_lbl]
    small_m = [m_meta_tokens, m_norm_mix, m_norm_mlp, m_norm_final, m_hg_norm, m_na_rpb, m_hg_lb_logits]
    small_v = [v_meta_tokens, v_norm_mix, v_norm_mlp, v_norm_final, v_hg_norm, v_na_rpb, v_hg_lb_logits]
    pk = lambda lst: jnp.concatenate([_as_rows(a) for a in lst], axis=0)
    sd, sm, sv = _adamw(pk(small_w), pk(small_gr), pk(small_m), pk(small_v), name="adamw_small")
    soffs = np.cumsum([0] + [_as_rows(a).shape[0] for a in small_w])
    unpk = lambda p: [_from_rows(p[soffs[i]:soffs[i + 1]], small_w[i].shape) for i in range(len(small_w))]
    sd, sm, sv = unpk(sd), unpk(sm), unpk(sv)

    def order(bigs, smalls):
        return [smalls[0]] + [b.reshape(1, *b.shape) for b in bigs] + smalls[1:]

    grads = order(g_big, small_gr)
    deltas = order([o[0] for o in big_out], sd)
    new_m = order([o[1] for o in big_out], sm)
    new_v = order([o[2] for o in big_out], sv)
    return (loss_total, dx.reshape(1, *dx.shape), *grads, *deltas, *new_m, *new_v)
```

```python
import functools

import numpy as np
import jax
import jax.numpy as jnp
from jax import lax
from jax.experimental import pallas as pl
from jax.experimental.pallas import tpu as pltpu

F32 = jnp.float32
BF16 = jnp.bfloat16
HIGHEST = lax.Precision.HIGHEST

GRID_W = 64
N_META = 16
EPS = 1e-6
NA_HEAD_DIM = 64
NA_WIN_H = 8
NA_WIN_W = 16
HG_DK = 128
HG_CHUNK = 16
LANES = 128
ROW_ALIGN = 128
VMEM_LIMIT = 48 * 1024 * 1024

ADAM_LR = 0.001
ADAM_B1 = 0.9
ADAM_B2 = 0.999
ADAM_EPS = 1e-08
ADAM_WD = 0.01
ADAM_STEP = 10

MESH = pl.DeviceIdType.MESH


def _cp(*sem):
    return pltpu.CompilerParams(dimension_semantics=sem, vmem_limit_bytes=VMEM_LIMIT)


def _sigmoid(x):
    return 1.0 / (1.0 + jnp.exp(-x))


def _dot(a, b, dims, precision=None):
    return lax.dot_general(a, b, (dims, ((), ())), preferred_element_type=F32, precision=precision)


def _nn(a, b, **kw):
    return _dot(a, b, ((1,), (0,)), **kw)


def _nt(a, b, **kw):
    return _dot(a, b, ((1,), (1,)), **kw)


def _tn(a, b, **kw):
    return _dot(a, b, ((0,), (0,)), **kw)


def _matmul(a, b, *, ta=False, tb=False, tm=None, tn=None, tk=None, out_dtype=F32, name,
            precision=None, after=None, epilogue=None, tiles=(), out_dtypes=None):
    extra = [] if after is None else [after]
    single = out_dtypes is None
    if single:
        out_dtypes = (out_dtype,)
    n_t, n_o = len(tiles), len(out_dtypes)
    if ta:
        kdim, m = a.shape
    else:
        m, kdim = a.shape
    if tb:
        n, k2 = b.shape
    else:
        k2, n = b.shape
    assert kdim == k2, (a.shape, b.shape, ta, tb)
    if tm is None:
        if ta:
            tm = next(t for t in (1024, 512, 256, 128, m) if m % t == 0)
        else:
            tm = m // 2 if (m // 2) % 16 == 0 and m > 512 else m
    if tn is None:
        wide = (1024,) if not ta and len(tiles) <= 1 else ()
        tn = next(t for t in (*wide, 512, 256, 128, n) if n % t == 0)
    if tk is None:
        tk = kdim if ta else next(t for t in (1024, 512, 256, 128, kdim) if kdim % t == 0)
    assert m % tm == 0 and n % tn == 0 and kdim % tk == 0, (m, n, kdim, tm, tn, tk)
    nk = kdim // tk
    op_dtype = F32 if precision is not None else BF16

    def body(a_ref, b_ref, *refs):
        t_refs = refs[:n_t]
        o_refs = refs[n_t + len(extra):n_t + len(extra) + n_o]
        av = a_ref[...].astype(op_dtype)
        bv = b_ref[...].astype(op_dtype)
        dims = ((0 if ta else 1,), (1 if tb else 0,))
        part = _dot(av, bv, dims, precision=precision)

        def finish(acc):
            outs = (acc,) if epilogue is None else epilogue(acc, *[t[...] for t in t_refs])
            for o_ref, val in zip(o_refs, outs):
                o_ref[...] = val.astype(o_ref.dtype)

        if nk == 1:
            finish(part)
            return
        acc_ref = refs[-1]
        kk = pl.program_id(2)

        @pl.when(kk == 0)
        def _():
            acc_ref[...] = part

        @pl.when((kk > 0) & (kk < nk - 1))
        def _():
            acc_ref[...] += part

        @pl.when(kk == nk - 1)
        def _():
            finish(acc_ref[...] + part)

    a_spec = (pl.BlockSpec((tk, tm), lambda i, j, k: (k, i)) if ta
              else pl.BlockSpec((tm, tk), lambda i, j, k: (i, k)))
    b_spec = (pl.BlockSpec((tn, tk), lambda i, j, k: (j, k)) if tb
              else pl.BlockSpec((tk, tn), lambda i, j, k: (k, j)))
    for _, off in tiles:
        assert off % tn == 0, (off, tn)
    t_specs = [pl.BlockSpec((tm, tn), functools.partial(lambda i, j, k, o: (i, o + j), o=off // tn))
               for _, off in tiles]
    o_spec = pl.BlockSpec((tm, tn), lambda i, j, k: (i, j))
    outs = pl.pallas_call(
        body,
        grid=(m // tm, n // tn, nk),
        in_specs=[a_spec, b_spec] + t_specs + [pl.BlockSpec(memory_space=pl.ANY)] * len(extra),
        out_specs=[o_spec] * n_o,
        out_shape=[jax.ShapeDtypeStruct((m, n), dt) for dt in out_dtypes],
        scratch_shapes=[pltpu.VMEM((tm, tn), F32)] if nk > 1 else [],
        compiler_params=_cp("parallel", "parallel", "arbitrary"),
        name=name,
    )(a, b, *[t for t, _ in tiles], *extra)
    return outs[0] if single else outs


def _rspec(tr, w, cb=0):
    return pl.BlockSpec((tr, w), lambda i: (i, cb))


def _fspec(shape):
    nd = len(shape)
    return pl.BlockSpec(shape, lambda i: (0,) * nd)


ROW_VMEM_BUDGET = 20 * 1024 * 1024
ROW_MIN_STEPS = 4


def _row_tile(lp, row_bytes):
    for k in range(ROW_MIN_STEPS, lp // 16 + 1):
        tr = lp // k
        if lp % k == 0 and tr % 16 == 0 and 2 * tr * row_bytes <= ROW_VMEM_BUDGET:
            return tr
    return lp


def _rmsnorm_fwd(x, g, *, name):
    lp, d = x.shape
    tr = _row_tile(lp, d * (4 + 2))

    def body(x_ref, g_ref, o_ref):
        xv = x_ref[...]
        r = lax.rsqrt(jnp.mean(xv * xv, axis=-1, keepdims=True) + EPS)
        o_ref[...] = (xv * r * g_ref[...]).astype(BF16)

    return pl.pallas_call(
        body, grid=(lp // tr,),
        in_specs=[_rspec(tr, d), _fspec((1, d))],
        out_specs=_rspec(tr, d),
        out_shape=jax.ShapeDtypeStruct((lp, d), BF16),
        compiler_params=_cp("parallel"), name=name)(x, g)


def _residual_norm(h, t, g, *, name):
    lp, d = h.shape
    tr = _row_tile(lp, d * (4 + 4 + 4 + 2))

    def body(h_ref, t_ref, g_ref, h1_ref, m_ref):
        xv = h_ref[...] + t_ref[...]
        h1_ref[...] = xv
        r = lax.rsqrt(jnp.mean(xv * xv, axis=-1, keepdims=True) + EPS)
        m_ref[...] = (xv * r * g_ref[...]).astype(BF16)

    return pl.pallas_call(
        body, grid=(lp // tr,),
        in_specs=[_rspec(tr, d), _rspec(tr, d), _fspec((1, d))],
        out_specs=[_rspec(tr, d), _rspec(tr, d)],
        out_shape=[jax.ShapeDtypeStruct((lp, d), F32), jax.ShapeDtypeStruct((lp, d), BF16)],
        compiler_params=_cp("parallel"), name=name)(h, t, g)


def _rmsnorm_bwd_add(x, g, dy, dres, *, name):
    lp, d = x.shape
    tr = _row_tile(lp, d * 4 * 4)

    def body(x_ref, g_ref, dy_ref, dr_ref, dx_ref, dg_ref):
        @pl.when(pl.program_id(0) == 0)
        def _():
            dg_ref[...] = jnp.zeros_like(dg_ref)

        xv = x_ref[...]
        r = lax.rsqrt(jnp.mean(xv * xv, axis=-1, keepdims=True) + EPS)
        xh = xv * r
        dyv = dy_ref[...]
        dg_ref[...] += jnp.sum(dyv * xh, axis=0, keepdims=True)
        dxh = dyv * g_ref[...]
        dx_ref[...] = dr_ref[...] + r * (dxh - xh * jnp.mean(dxh * xh, axis=-1, keepdims=True))

    return pl.pallas_call(
        body, grid=(lp // tr,),
        in_specs=[_rspec(tr, d), _fspec((1, d)), _rspec(tr, d), _rspec(tr, d)],
        out_specs=[_rspec(tr, d), _fspec((1, d))],
        out_shape=[jax.ShapeDtypeStruct((lp, d), F32), jax.ShapeDtypeStruct((1, d), F32)],
        compiler_params=_cp("arbitrary"), name=name)(x, g, dy, dres)


def _final_loss(h1, t2, g, tgt, *, n_tok, name):
    lp, d = h1.shape
    tr = _row_tile(lp, d * 4 * 4)

    def body(h_ref, t_ref, g_ref, tg_ref, dh_ref, loss_ref, dg_ref):
        i = pl.program_id(0)

        @pl.when(i == 0)
        def _():
            loss_ref[...] = jnp.zeros_like(loss_ref)
            dg_ref[...] = jnp.zeros_like(dg_ref)

        xv = h_ref[...] + t_ref[...]
        r = lax.rsqrt(jnp.mean(xv * xv, axis=-1, keepdims=True) + EPS)
        xh = xv * r
        gv = g_ref[...]
        row = i * tr + lax.broadcasted_iota(jnp.int32, (tr, 1), 0)
        valid = (row >= N_META) & (row < N_META + n_tok)
        err = jnp.where(valid, xh * gv - tg_ref[...], 0.0)
        loss_ref[...] += jnp.sum(0.5 * err * err) / d
        dy = err / d
        dg_ref[...] += jnp.sum(dy * xh, axis=0, keepdims=True)
        dxh = dy * gv
        dh_ref[...] = r * (dxh - xh * jnp.mean(dxh * xh, axis=-1, keepdims=True))

    return pl.pallas_call(
        body, grid=(lp // tr,),
        in_specs=[_rspec(tr, d), _rspec(tr, d), _fspec((1, d)), _rspec(tr, d)],
        out_specs=[_rspec(tr, d), _fspec((1, LANES)), _fspec((1, d))],
        out_shape=[jax.ShapeDtypeStruct((lp, d), F32), jax.ShapeDtypeStruct((1, LANES), F32),
                   jax.ShapeDtypeStruct((1, d), F32)],
        compiler_params=_cp("arbitrary"), name=name)(h1, t2, g, tgt)


def _hg_out(o_f, o_b, proj, gain, *, col_g, name):
    lp, w = o_f.shape
    tr = _row_tile(lp, w * (3 * 4 + 2))
    hh = w // HG_DK

    def body(of_ref, ob_ref, g_ref, gain_ref, y_ref):
        gv = g_ref[...]
        sg = gv * _sigmoid(gv)
        for h in range(hh):
            sl = slice(h * HG_DK, (h + 1) * HG_DK)
            o = of_ref[:, sl] + ob_ref[:, sl]
            r = lax.rsqrt(jnp.mean(o * o, axis=-1, keepdims=True) + EPS)
            y_ref[:, sl] = (o * r * gain_ref[:, sl] * sg[:, sl]).astype(BF16)

    return pl.pallas_call(
        body, grid=(lp // tr,),
        in_specs=[_rspec(tr, w), _rspec(tr, w), _rspec(tr, w, col_g // w), _fspec((1, w))],
        out_specs=_rspec(tr, w),
        out_shape=jax.ShapeDtypeStruct((lp, w), BF16),
        compiler_params=_cp("parallel"), name=name)(o_f, o_b, proj, gain)


def _hg_out_bwd(o_f, o_b, proj, gain, dy, *, col_g, name):
    lp, w = o_f.shape
    tr = _row_tile(lp, w * (5 * 4 + 2))
    hh = w // HG_DK

    def body(of_ref, ob_ref, g_ref, gain_ref, dy_ref, do_ref, dg_ref, dgain_ref):
        @pl.when(pl.program_id(0) == 0)
        def _():
            dgain_ref[...] = jnp.zeros_like(dgain_ref)

        for h in range(hh):
            sl = slice(h * HG_DK, (h + 1) * HG_DK)
            gv = g_ref[:, sl]
            s = _sigmoid(gv)
            sg = gv * s
            dsg = s + gv * s * (1.0 - s)
            o = of_ref[:, sl] + ob_ref[:, sl]
            r = lax.rsqrt(jnp.mean(o * o, axis=-1, keepdims=True) + EPS)
            on = o * r
            dyv = dy_ref[:, sl]
            gn = gain_ref[:, sl]
            dgain_ref[:, sl] += jnp.sum(dyv * on * sg, axis=0, keepdims=True)
            dg_ref[:, sl] = (dyv * on * gn * dsg).astype(BF16)
            don = dyv * gn * sg
            do_ref[:, sl] = r * (don - on * jnp.mean(don * on, axis=-1, keepdims=True))

    return pl.pallas_call(
        body, grid=(lp // tr,),
        in_specs=[_rspec(tr, w), _rspec(tr, w), _rspec(tr, w, col_g // w), _fspec((1, w)),
                  _rspec(tr, w)],
        out_specs=[_rspec(tr, w), _rspec(tr, w), _fspec((1, w))],
        out_shape=[jax.ShapeDtypeStruct((lp, w), F32), jax.ShapeDtypeStruct((lp, w), BF16),
                   jax.ShapeDtypeStruct((1, w), F32)],
        compiler_params=_cp("arbitrary"), name=name)(o_f, o_b, proj, gain, dy)


HG_ROWS = 128
HG_HALVES = (1, 2, 4, 8, 16, 32, 64)


def _hg_gates(zq, z, lbv):
    qh = zq * _sigmoid(zq)
    s = _sigmoid(z)
    f = lbv + (1.0 - lbv) * s
    kk = (1.0 - lbv) * _sigmoid(-z)
    return qh, s, f, jnp.log(f), kk


def _block_cumsum(g, pos, suffix):
    x = g
    for k in HG_HALVES:
        if suffix:
            x = x + jnp.where(pos < HG_ROWS - k, pltpu.roll(x, HG_ROWS - k, 0), 0.0)
        else:
            x = x + jnp.where(pos >= k, pltpu.roll(x, k, 0), 0.0)
    return x


def _pair_levels(b, pos, reverse):
    out = []
    first = b
    for m in HG_HALVES:
        if m > 1:
            first = jnp.where((pos & (m - 1)) >= m // 2, pltpu.roll(first, m // 2, 0), first)
        nxt = pltpu.roll(first, HG_ROWS - m, 0)
        upper = (pos & (2 * m - 1)) >= m
        if reverse:
            eq = jnp.where(upper, 0.0, jnp.exp(b - nxt))
            ek = jnp.where(upper, jnp.exp(first - b), 0.0)
        else:
            eq = jnp.where(upper, jnp.exp(b - first), 0.0)
            ek = jnp.where(upper, 0.0, jnp.exp(nxt - b))
        out.append((eq, ek))
    return out


def _pair_masks(mask_ref):
    ri = lax.broadcasted_iota(jnp.int32, (HG_ROWS, HG_ROWS), 0)
    ci = lax.broadcasted_iota(jnp.int32, (HG_ROWS, HG_ROWS), 1)
    for i, m in enumerate(HG_HALVES):
        sh = m.bit_length()
        mask_ref[i] = jnp.where((ri >> sh) == (ci >> sh), 1.0, 0.0)


def _hg_scan_fwd(proj, lb, *, reverse, col_q, col_z, col_i, hh, name):
    lp = proj.shape[0]
    n_blocks = lp // HG_ROWS
    last = 0 if reverse else HG_ROWS - 1

    def body(q_ref, z_ref, i_ref, lb_ref, o_ref, st_ref, mask_ref):
        lbv = lb_ref[...]
        pos = lax.broadcasted_iota(jnp.int32, (HG_ROWS, 1), 0)
        ri = lax.broadcasted_iota(jnp.int32, (HG_ROWS, HG_ROWS), 0)
        ci = lax.broadcasted_iota(jnp.int32, (HG_ROWS, HG_ROWS), 1)
        _pair_masks(mask_ref)

        def block(bi, st):
            bb = (n_blocks - 1 - bi) if reverse else bi
            r0 = pl.multiple_of(bb * HG_ROWS, HG_ROWS)
            v16 = i_ref[pl.ds(r0, HG_ROWS), :].astype(BF16)
            qh, _, _, g, kk = _hg_gates(q_ref[pl.ds(r0, HG_ROWS), :], z_ref[pl.ds(r0, HG_ROWS), :],
                                        lbv)
            b = _block_cumsum(g, pos, reverse)
            bl = b[last:last + 1, :]
            qe = (qh * jnp.exp(b)).astype(BF16)
            kd = (kk * jnp.exp(bl - b)).astype(BF16)
            a = jnp.where(ri == ci, jnp.sum(qh * kk, axis=1, keepdims=True), 0.0)
            for i, (eq, ek) in enumerate(_pair_levels(b, pos, reverse)):
                a = a + mask_ref[i] * _nt((qh * eq).astype(BF16), (kk * ek).astype(BF16))
            st_ref[bb] = st
            o_ref[pl.ds(r0, HG_ROWS), :] = _nn(a.astype(BF16), v16) + _nt(qe, st.astype(BF16))
            return jnp.exp(bl) * st + _tn(v16, kd)

        lax.fori_loop(0, n_blocks, block, jnp.zeros((HG_DK, HG_DK), F32))

    cspec = lambda col: pl.BlockSpec((lp, HG_DK), lambda h: (0, col // HG_DK + h))
    return pl.pallas_call(
        body, grid=(hh,),
        in_specs=[cspec(col_q), cspec(col_z), cspec(col_i),
                  pl.BlockSpec((None, 1, HG_DK), lambda h: (h, 0, 0))],
        out_specs=[pl.BlockSpec((lp, HG_DK), lambda h: (0, h)),
                   pl.BlockSpec((None, n_blocks, HG_DK, HG_DK), lambda h: (h, 0, 0, 0))],
        out_shape=[jax.ShapeDtypeStruct((lp, hh * HG_DK), F32),
                   jax.ShapeDtypeStruct((hh, n_blocks, HG_DK, HG_DK), F32)],
        scratch_shapes=[pltpu.VMEM((len(HG_HALVES), HG_ROWS, HG_ROWS), F32)],
        compiler_params=_cp("parallel"), name=name)(proj, proj, proj, lb)


def _hg_scan_bwd(proj, lb, states, do, *, reverse, col_q, col_z, col_i, hh, name):
    lp = proj.shape[0]
    n_blocks = lp // HG_ROWS
    last = 0 if reverse else HG_ROWS - 1

    def body(q_ref, z_ref, i_ref, lb_ref, st_ref, do_ref, dq_ref, dz_ref, dv_ref, dlb_ref, mask_ref):
        lbv = lb_ref[...]
        pos = lax.broadcasted_iota(jnp.int32, (HG_ROWS, 1), 0)
        ri = lax.broadcasted_iota(jnp.int32, (HG_ROWS, HG_ROWS), 0)
        ci = lax.broadcasted_iota(jnp.int32, (HG_ROWS, HG_ROWS), 1)
        _pair_masks(mask_ref)

        def block(bi, carry):
            dst, dlb = carry
            bb = bi if reverse else (n_blocks - 1 - bi)
            r0 = pl.multiple_of(bb * HG_ROWS, HG_ROWS)
            zq = q_ref[pl.ds(r0, HG_ROWS), :]
            v16 = i_ref[pl.ds(r0, HG_ROWS), :].astype(BF16)
            do16 = do_ref[pl.ds(r0, HG_ROWS), :].astype(BF16)
            qh, s, f, g, kk = _hg_gates(zq, z_ref[pl.ds(r0, HG_ROWS), :], lbv)
            b = _block_cumsum(g, pos, reverse)
            bl = b[last:last + 1, :]
            eb = jnp.exp(b)
            ebl = jnp.exp(bl - b)
            decay = jnp.exp(bl)
            qe16 = (qh * eb).astype(BF16)
            kd16 = (kk * ebl).astype(BF16)
            st = st_ref[bb]
            st16, dst16 = st.astype(BF16), dst.astype(BF16)
            same_row = ri == ci
            da = _nt(do16, v16)
            da_diag = jnp.sum(jnp.where(same_row, da, 0.0), axis=1, keepdims=True)
            dq_state = eb * _nn(do16, st16)
            dk_state = ebl * _nn(v16, dst16)
            dq = dq_state + da_diag * kk
            dk = dk_state + da_diag * qh
            dbl = (decay * jnp.sum(st * dst, axis=0, keepdims=True)
                   + jnp.sum(kk * dk_state, axis=0, keepdims=True))
            db = qh * dq_state - kk * dk_state + jnp.where(pos == last, dbl, 0.0)
            a = jnp.where(same_row, jnp.sum(qh * kk, axis=1, keepdims=True), 0.0)
            for i, (eq, ek) in enumerate(_pair_levels(b, pos, reverse)):
                same = mask_ref[i]
                q16, k16 = (qh * eq).astype(BF16), (kk * ek).astype(BF16)
                a = a + same * _nt(q16, k16)
                da16 = (same * da).astype(BF16)
                gq, gk = _nn(da16, k16), _tn(da16, q16)
                dq = dq + eq * gq
                dk = dk + ek * gk
                db = db + (q16.astype(F32) * gq - k16.astype(F32) * gk)
            dg = _block_cumsum(db, pos, not reverse)
            df = dg / f - dk
            sq = _sigmoid(zq)
            dq_ref[pl.ds(r0, HG_ROWS), :] = dq * (sq + zq * sq * (1.0 - sq))
            dz_ref[pl.ds(r0, HG_ROWS), :] = df * (1.0 - lbv) * s * (1.0 - s)
            dv_ref[pl.ds(r0, HG_ROWS), :] = _nt(kd16, dst16) + _tn(a.astype(BF16), do16)
            return (decay * dst + _tn(do16, qe16),
                    dlb + jnp.sum(df * (1.0 - s), axis=0, keepdims=True))

        _, dlb = lax.fori_loop(0, n_blocks, block,
                               (jnp.zeros((HG_DK, HG_DK), F32), jnp.zeros((1, HG_DK), F32)))
        dlb_ref[...] = dlb

    cspec = lambda col: pl.BlockSpec((lp, HG_DK), lambda h: (0, col // HG_DK + h))
    ospec = pl.BlockSpec((lp, HG_DK), lambda h: (0, h))
    sds = jax.ShapeDtypeStruct((lp, hh * HG_DK), F32)
    return pl.pallas_call(
        body, grid=(hh,),
        in_specs=[cspec(col_q), cspec(col_z), cspec(col_i),
                  pl.BlockSpec((None, 1, HG_DK), lambda h: (h, 0, 0)),
                  pl.BlockSpec((None, n_blocks, HG_DK, HG_DK), lambda h: (h, 0, 0, 0)),
                  ospec],
        out_specs=[ospec, ospec, ospec, pl.BlockSpec((None, 1, HG_DK), lambda h: (h, 0, 0))],
        out_shape=[sds, sds, sds, jax.ShapeDtypeStruct((hh, 1, HG_DK), F32)],
        scratch_shapes=[pltpu.VMEM((len(HG_HALVES), HG_ROWS, HG_ROWS), F32)],
        compiler_params=_cp("parallel"), name=name)(proj, proj, proj, lb, states, do)


NA_HB = LANES // NA_HEAD_DIM
NA_G = 4
NA_U = NA_G + NA_WIN_H
NA_QN = NA_G * GRID_W
NA_KN = NA_U * GRID_W


def _na_table_index(pattern, a, j):
    if pattern == 0:
        return j - a + NA_WIN_H - 1 if j < NA_WIN_H else None
    if pattern == 2:
        return j - a - 1 if j >= NA_U - NA_WIN_H else None
    return j - a + NA_WIN_H // 2 - 1 if a <= j < a + NA_WIN_H else None


def _na_step_rows(pattern, t, rows):
    if pattern == 0:
        r0, us = 0, 0
    elif pattern == 2:
        r0, us = rows - NA_G, rows - NA_U
    else:
        r0 = NA_G * t
        us = r0 - NA_WIN_H // 2
    q0, k0 = N_META + GRID_W * r0, N_META + GRID_W * us
    if pattern == 1:
        q0, k0 = pl.multiple_of(q0, 16), pl.multiple_of(k0, 16)
    return q0, k0


def _na_fill_bias(tb_ref, bias_ref):
    neg = jnp.full((GRID_W, GRID_W), -1e30, F32)
    for h in range(NA_HB):
        for pattern in range(3):
            for a in range(NA_G):
                for j in range(NA_U):
                    idx = _na_table_index(pattern, a, j)
                    bias_ref[h, pattern, a * GRID_W:(a + 1) * GRID_W, j * GRID_W:(j + 1) * GRID_W] = (
                        neg if idx is None else tb_ref[h, idx])


def _na_steps(rows, step, carry):
    n_steps = rows // NA_G
    carry = step(0, 0, carry)
    carry = lax.fori_loop(1, n_steps - 1, functools.partial(step, 1), carry)
    return step(2, n_steps - 1, carry)


def _na_head_lanes():
    lane = lax.broadcasted_iota(jnp.int32, (1, LANES), 1)
    return [lane // NA_HEAD_DIM == h for h in range(NA_HB)]


def _na_only(mask, x):
    return jnp.where(mask, x, jnp.zeros_like(x))


def _na_fwd(proj, tb, *, n_tok, nh, name):
    lp = proj.shape[0]
    dh, hb = NA_HEAD_DIM, NA_HB
    naw = nh * dh
    rows = n_tok // GRID_W
    scale = dh ** -0.5

    def body(q_ref, k_ref, v_ref, tb_ref, o_ref, lse_ref, q16_ref, k16_ref, v16_ref, bias_ref):
        o_ref[...] = jnp.zeros_like(o_ref)
        lse_ref[...] = jnp.zeros_like(lse_ref)
        q16_ref[...] = q_ref[...].astype(BF16)
        k16_ref[...] = k_ref[...].astype(BF16)
        v16_ref[...] = v_ref[...].astype(BF16)
        _na_fill_bias(tb_ref, bias_ref)
        heads = _na_head_lanes()
        km = k16_ref[0:N_META, :]
        vm = v16_ref[0:N_META, :]
        qm = q16_ref[0:N_META, :]
        o_m = None
        for h in range(hb):
            s = _nt(_na_only(heads[h], qm), km) * scale
            m = jnp.max(s, axis=1, keepdims=True)
            p = jnp.exp(s - m)
            l = jnp.sum(p, axis=1, keepdims=True)
            o_h = _nn(p.astype(BF16), vm) / l
            o_m = o_h if o_m is None else jnp.where(heads[h], o_h, o_m)
            lse_ref[h, 0:N_META, :] = m + jnp.log(l)
        o_ref[0:N_META, :] = o_m

        def step(pattern, t, carry):
            q0, k0 = _na_step_rows(pattern, t, rows)
            q16 = q16_ref[pl.ds(q0, NA_QN), :]
            k16 = k16_ref[pl.ds(k0, NA_KN), :]
            v16 = v16_ref[pl.ds(k0, NA_KN), :]
            o = None
            for h in range(hb):
                q_h = _na_only(heads[h], q16)
                s = _nt(q_h, k16) * scale + bias_ref[h, pattern]
                sm = _nt(q_h, km) * scale
                m = jnp.maximum(jnp.max(s, axis=1, keepdims=True),
                                jnp.max(sm, axis=1, keepdims=True))
                p = jnp.exp(s - m)
                pm = jnp.exp(sm - m)
                l = jnp.sum(p, axis=1, keepdims=True) + jnp.sum(pm, axis=1, keepdims=True)
                o_h = (_nn(p.astype(BF16), v16) + _nn(pm.astype(BF16), vm)) / l
                o = o_h if o is None else jnp.where(heads[h], o_h, o)
                lse_ref[h, pl.ds(q0, NA_QN), :] = m + jnp.log(l)
            o_ref[pl.ds(q0, NA_QN), :] = o
            return carry

        _na_steps(rows, step, 0)

    cblk = lambda col: pl.BlockSpec((lp, LANES), lambda g: (0, col // LANES + g))
    return pl.pallas_call(
        body, grid=(nh // hb,),
        in_specs=[cblk(0), cblk(naw), cblk(2 * naw),
                  pl.BlockSpec((hb, 2 * NA_WIN_H - 1, GRID_W, GRID_W), lambda g: (g, 0, 0, 0))],
        out_specs=[cblk(0), pl.BlockSpec((hb, lp, 1), lambda g: (g, 0, 0))],
        out_shape=[jax.ShapeDtypeStruct((lp, naw), F32), jax.ShapeDtypeStruct((nh, lp, 1), F32)],
        scratch_shapes=[pltpu.VMEM((lp, LANES), BF16)] * 3 + [pltpu.VMEM((hb, 3, NA_QN, NA_KN), F32)],
        compiler_params=_cp("parallel"), name=name)(proj, proj, proj, tb)


def _na_bwd(proj, tb, o, lse, do, *, n_tok, nh, name):
    lp = proj.shape[0]
    dh, hb = NA_HEAD_DIM, NA_HB
    naw = nh * dh
    rows = n_tok // GRID_W
    scale = dh ** -0.5

    def body(q_ref, k_ref, v_ref, tb_ref, o_ref, lse_ref, do_ref, dq_ref, dk_ref, dv_ref, dtb_ref,
             q16_ref, k16_ref, v16_ref, bias_ref):
        dq_ref[...] = jnp.zeros_like(dq_ref)
        dk_ref[...] = jnp.zeros_like(dk_ref)
        dv_ref[...] = jnp.zeros_like(dv_ref)
        dtb_ref[...] = jnp.zeros_like(dtb_ref)
        q16_ref[...] = q_ref[...].astype(BF16)
        k16_ref[...] = k_ref[...].astype(BF16)
        v16_ref[...] = v_ref[...].astype(BF16)
        _na_fill_bias(tb_ref, bias_ref)
        heads = _na_head_lanes()
        km = k16_ref[0:N_META, :]
        vm = v16_ref[0:N_META, :]
        qm = q16_ref[0:N_META, :]
        dom = do_ref[0:N_META, :]
        prod = dom * o_ref[0:N_META, :]
        dq_m = None
        dkm0 = jnp.zeros((N_META, LANES), F32)
        dvm0 = jnp.zeros((N_META, LANES), F32)
        for h in range(hb):
            q_h = _na_only(heads[h], qm)
            do_h = _na_only(heads[h], dom).astype(BF16)
            p = jnp.exp(_nt(q_h, km) * scale - lse_ref[h, 0:N_META, :])
            delta = jnp.sum(_na_only(heads[h], prod), axis=1, keepdims=True)
            ds = (p * (_nt(do_h, vm) - delta)).astype(BF16)
            dq_h = _nn(ds, km) * scale
            dq_m = dq_h if dq_m is None else jnp.where(heads[h], dq_h, dq_m)
            dkm0 = dkm0 + _tn(ds, q_h) * scale
            dvm0 = dvm0 + _tn(p.astype(BF16), do_h)
        dq_ref[0:N_META, :] = dq_m

        def step(pattern, t, carry):
            dkm, dvm = carry
            q0, k0 = _na_step_rows(pattern, t, rows)
            q16 = q16_ref[pl.ds(q0, NA_QN), :]
            k16 = k16_ref[pl.ds(k0, NA_KN), :]
            v16 = v16_ref[pl.ds(k0, NA_KN), :]
            dov = do_ref[pl.ds(q0, NA_QN), :]
            prod = dov * o_ref[pl.ds(q0, NA_QN), :]
            dq = None
            dk = jnp.zeros((NA_KN, LANES), F32)
            dv = jnp.zeros((NA_KN, LANES), F32)
            for h in range(hb):
                q_h = _na_only(heads[h], q16)
                do_h = _na_only(heads[h], dov).astype(BF16)
                lse = lse_ref[h, pl.ds(q0, NA_QN), :]
                p = jnp.exp(_nt(q_h, k16) * scale + bias_ref[h, pattern] - lse)
                pm = jnp.exp(_nt(q_h, km) * scale - lse)
                delta = jnp.sum(_na_only(heads[h], prod), axis=1, keepdims=True)
                ds = p * (_nt(do_h, v16) - delta)
                dsm = (pm * (_nt(do_h, vm) - delta)).astype(BF16)
                ds16 = ds.astype(BF16)
                dq_h = (_nn(ds16, k16) + _nn(dsm, km)) * scale
                dq = dq_h if dq is None else jnp.where(heads[h], dq_h, dq)
                dk = dk + _tn(ds16, q_h) * scale
                dv = dv + _tn(p.astype(BF16), do_h)
                dkm = dkm + _tn(dsm, q_h) * scale
                dvm = dvm + _tn(pm.astype(BF16), do_h)
                for a in range(NA_G):
                    for j in range(NA_U):
                        idx = _na_table_index(pattern, a, j)
                        if idx is not None:
                            dtb_ref[h, idx] += ds[a * GRID_W:(a + 1) * GRID_W,
                                                  j * GRID_W:(j + 1) * GRID_W]
            dq_ref[pl.ds(q0, NA_QN), :] = dq
            dk_ref[pl.ds(k0, NA_KN), :] += dk
            dv_ref[pl.ds(k0, NA_KN), :] += dv
            return dkm, dvm

        dkm, dvm = _na_steps(rows, step, (dkm0, dvm0))
        dk_ref[0:N_META, :] += dkm
        dv_ref[0:N_META, :] += dvm

    cblk = lambda col: pl.BlockSpec((lp, LANES), lambda g: (0, col // LANES + g))
    tbs = pl.BlockSpec((hb, 2 * NA_WIN_H - 1, GRID_W, GRID_W), lambda g: (g, 0, 0, 0))
    sds = jax.ShapeDtypeStruct((lp, naw), F32)
    return pl.pallas_call(
        body, grid=(nh // hb,),
        in_specs=[cblk(0), cblk(naw), cblk(2 * naw), tbs, cblk(0),
                  pl.BlockSpec((hb, lp, 1), lambda g: (g, 0, 0)), cblk(0)],
        out_specs=[cblk(0), cblk(0), cblk(0), tbs],
        out_shape=[sds, sds, sds, jax.ShapeDtypeStruct(tb.shape, F32)],
        scratch_shapes=[pltpu.VMEM((lp, LANES), BF16)] * 3 + [pltpu.VMEM((hb, 3, NA_QN, NA_KN), F32)],
        compiler_params=_cp("parallel"), name=name)(proj, proj, proj, tb, o, lse, do)


def _rpb_onehot():
    c = np.arange(GRID_W)[:, None]
    w = np.arange(GRID_W)[None, :]
    cs = np.clip(c - NA_WIN_W // 2, 0, GRID_W - NA_WIN_W)
    in_win = (w >= cs) & (w < cs + NA_WIN_W)
    dc = np.clip(w - c, -(NA_WIN_W - 1), NA_WIN_W - 1) + NA_WIN_W - 1
    oh = np.zeros((LANES, GRID_W * GRID_W), np.float32)
    flat = np.arange(GRID_W * GRID_W).reshape(GRID_W, GRID_W)
    oh[dc[in_win], flat[in_win]] = 1.0
    neg = np.where(in_win, 0.0, -1e30).astype(np.float32).reshape(1, -1)
    return oh, neg


def _assemble_dproj(dq_na, dk_na, dv_na, dq_f, dq_b, dz_f, dz_b, dv_f, dv_b, dg, dgn, dgh, *, name):
    lp, naw = dq_na.shape
    hgw = dq_f.shape[1]
    d = dgn.shape[1]
    cols = 3 * naw + 5 * hgw + 2 * d
    tr = _row_tile(lp, 3 * naw * 4 + 6 * hgw * 4 + hgw * 2 + 2 * d * 2 + cols * 2)

    def body(nq_ref, nk_ref, nv_ref, qf_ref, qb_ref, zf_ref, zb_ref, vf_ref, vb_ref, g_ref, gn_ref,
             gh_ref, o_ref):
        o_ref[:, 0:naw] = nq_ref[...].astype(BF16)
        o_ref[:, naw:2 * naw] = nk_ref[...].astype(BF16)
        o_ref[:, 2 * naw:3 * naw] = nv_ref[...].astype(BF16)
        c0 = 3 * naw
        o_ref[:, c0:c0 + hgw] = (qf_ref[...] + qb_ref[...]).astype(BF16)
        o_ref[:, c0 + hgw:c0 + 2 * hgw] = zf_ref[...].astype(BF16)
        o_ref[:, c0 + 2 * hgw:c0 + 3 * hgw] = zb_ref[...].astype(BF16)
        o_ref[:, c0 + 3 * hgw:c0 + 4 * hgw] = (vf_ref[...] + vb_ref[...]).astype(BF16)
        o_ref[:, c0 + 4 * hgw:c0 + 5 * hgw] = g_ref[...]
        o_ref[:, c0 + 5 * hgw:c0 + 5 * hgw + d] = gn_ref[...]
        o_ref[:, c0 + 5 * hgw + d:] = gh_ref[...]

    hg, na = _rspec(tr, hgw), _rspec(tr, naw)
    return pl.pallas_call(
        body, grid=(lp // tr,),
        in_specs=[na, na, na, hg, hg, hg, hg, hg, hg, hg, _rspec(tr, d), _rspec(tr, d)],
        out_specs=_rspec(tr, cols),
        out_shape=jax.ShapeDtypeStruct((lp, cols), BF16),
        compiler_params=_cp("parallel"), name=name)(dq_na, dk_na, dv_na, dq_f, dq_b, dz_f, dz_b,
                                                    dv_f, dv_b, dg, dgn, dgh)


def _adamw(w, g, m, v, *, name):
    rows, cols = w.shape
    tr = 256 if rows % 256 == 0 else rows

    def body(w_ref, g_ref, m_ref, v_ref, d_ref, mo_ref, vo_ref):
        gv = g_ref[...]
        mn = ADAM_B1 * m_ref[...] + (1.0 - ADAM_B1) * gv
        vn = ADAM_B2 * v_ref[...] + (1.0 - ADAM_B2) * (gv * gv)
        m_hat = mn / (1.0 - ADAM_B1 ** ADAM_STEP)
        v_hat = vn / (1.0 - ADAM_B2 ** ADAM_STEP)
        d_ref[...] = -ADAM_LR * (m_hat / (jnp.sqrt(v_hat) + ADAM_EPS) + ADAM_WD * w_ref[...])
        mo_ref[...] = mn
        vo_ref[...] = vn

    spec = _rspec(tr, cols)
    sds = jax.ShapeDtypeStruct((rows, cols), F32)
    return pl.pallas_call(
        body, grid=(rows // tr,), in_specs=[spec] * 4, out_specs=[spec] * 3, out_shape=[sds] * 3,
        compiler_params=_cp("parallel"), name=name)(w, g, m, v)


def _local_step(x, tgt, meta, first_weight, rest_weights, g_mix, g_mlp, g_fin, hg_gain, rpb, lb,
                early_grads=None, mid_grads=None, late_grad=None):
    n_tok, d = x.shape
    hgw = hg_gain.shape[1]
    nh, hh = rpb.shape[0], hgw // HG_DK
    naw = nh * NA_HEAD_DIM
    l_real = N_META + n_tok
    lp = -(-l_real // ROW_ALIGN) * ROW_ALIGN
    n_chunks = l_real // HG_CHUNK
    pad = lp - l_real
    col_qhg = 3 * naw
    col_zf, col_zb, col_i, col_g = (col_qhg + hgw, col_qhg + 2 * hgw, col_qhg + 3 * hgw,
                                    col_qhg + 4 * hgw)
    col_gate = col_qhg + 5 * hgw

    zpad = jnp.zeros((pad, d), F32)
    h0 = jnp.concatenate([meta, x, zpad], axis=0)
    tgt_p = jnp.concatenate([jnp.zeros((N_META, d), F32), tgt, zpad], axis=0)

    oh_np, neg_np = _rpb_onehot()
    oh = jnp.asarray(oh_np)
    rpb_p = jnp.pad(rpb.reshape(nh * (2 * NA_WIN_H - 1), 2 * NA_WIN_W - 1),
                    ((0, 0), (0, LANES - (2 * NA_WIN_W - 1))))
    tb = _matmul(rpb_p, oh, tm=rpb_p.shape[0], tn=512, tk=LANES, precision=HIGHEST,
                 name="rpb_expand")
    tb = (tb + jnp.asarray(neg_np)).reshape(nh, 2 * NA_WIN_H - 1, GRID_W, GRID_W)

    a = _rmsnorm_fwd(h0, g_mix, name="norm_mix")
    w_in = first_weight(a)
    proj = _matmul(a, w_in, name="mm_in")
    o_na, lse = _na_fwd(proj, tb, n_tok=n_tok, nh=nh, name="na_fwd")
    lb_f = lb[0].reshape(hh, 1, HG_DK)
    lb_b = lb[1].reshape(hh, 1, HG_DK)
    scan_kw = dict(col_q=col_qhg, col_i=col_i, hh=hh)
    o_f, st_f = _hg_scan_fwd(proj, lb_f, reverse=False, col_z=col_zf, name="hg_scan_f", **scan_kw)
    o_b, st_b = _hg_scan_fwd(proj, lb_b, reverse=True, col_z=col_zb, name="hg_scan_b", **scan_kw)
    o_hg = _hg_out(o_f, o_b, proj, hg_gain, col_g=col_g, name="hg_out")
    w_na, w_hg, w_o, w_up, w_down = rest_weights(o_hg)
    y_na = _matmul(o_na, w_na, name="mm_na_out")
    gates = ((proj, col_gate), (proj, col_gate + d))

    def mix_gates(acc, gn, gh, yn):
        return acc, _sigmoid(gn) * yn + _sigmoid(gh) * acc

    def mix_gates_bwd(dmix, gn, gh, yn, yh):
        sn, sh = _sigmoid(gn), _sigmoid(gh)
        return dmix * sn, dmix * sh, dmix * yn * sn * (1.0 - sn), dmix * yh * sh * (1.0 - sh)

    y_hg, mix = _matmul(o_hg, w_hg, name="mm_hg_out", epilogue=mix_gates,
                        tiles=(*gates, (y_na, 0)), out_dtypes=(F32, BF16))
    t1 = _matmul(mix, w_o, name="mm_o")
    h1, mlp_in = _residual_norm(h0, t1, g_mlp, name="resid_norm_mlp")
    u, act = _matmul(mlp_in, w_up, name="mm_up", out_dtypes=(BF16, BF16),
                     epilogue=lambda acc: (acc, jnp.square(jnp.maximum(acc, 0.0))))
    t2 = _matmul(act, w_down, name="mm_down")
    dh2, loss, dg_fin = _final_loss(h1, t2, g_fin, tgt_p, n_tok=n_tok, name="final_loss")

    (du,) = _matmul(dh2, w_down, tb=True, name="mm_down_dx", tiles=((u, 0),), out_dtypes=(BF16,),
                    epilogue=lambda acc, uv: (acc * 2.0 * jnp.maximum(uv, 0.0),))
    dw_down = _matmul(act, dh2, ta=True, name="mm_down_dw")
    dm = _matmul(du, w_up, tb=True, name="mm_up_dx")
    dw_up = _matmul(mlp_in, du, ta=True, name="mm_up_dw")
    dh1, dg_mlp = _rmsnorm_bwd_add(h1, g_mlp, dm, dh2, name="norm_mlp_bwd")
    dy_na, dy_hg, dgn, dgh = _matmul(dh1, w_o, tb=True, name="mm_o_dx", epilogue=mix_gates_bwd,
                                     tiles=(*gates, (y_na, 0), (y_hg, 0)), out_dtypes=(BF16,) * 4)
    dw_o = _matmul(mix, dh1, ta=True, name="mm_o_dw")
    do_na = _matmul(dy_na, w_na, tb=True, name="mm_na_out_dx")
    dw_na = _matmul(o_na, dy_na, ta=True, name="mm_na_out_dw")
    do_hg = _matmul(dy_hg, w_hg, tb=True, name="mm_hg_out_dx")
    dw_hg = _matmul(o_hg, dy_hg, ta=True, name="mm_hg_out_dw")
    token = early_grads([dw_na, dw_hg, dw_o, dw_up, dw_down]) if early_grads else None
    if token is not None:
        hg_gain = hg_gain + token[0:1, 0:1]
    d_o, dg_hg, d_gain = _hg_out_bwd(o_f, o_b, proj, hg_gain, do_hg, col_g=col_g, name="hg_out_bwd")
    dq_f, dz_f, dv_f, dlb_f = _hg_scan_bwd(proj, lb_f, st_f, d_o, reverse=False, col_z=col_zf,
                                           name="hg_scan_f_bwd", **scan_kw)
    token = mid_grads(dq_f) if mid_grads else None
    lb_b_late = lb_b if token is None else lb_b + token[0:1, 0:1]
    dq_b, dz_b, dv_b, dlb_b = _hg_scan_bwd(proj, lb_b_late, st_b, d_o, reverse=True, col_z=col_zb,
                                           name="hg_scan_b_bwd", **scan_kw)
    dq_na, dk_na, dv_na, dtb = _na_bwd(proj, tb, o_na, lse, do_na, n_tok=n_tok, nh=nh, name="na_bwd")
    dproj = _assemble_dproj(dq_na, dk_na, dv_na, dq_f, dq_b, dz_f, dz_b, dv_f, dv_b, dg_hg, dgn,
                            dgh, name="assemble_dproj")
    dw_in = _matmul(a, dproj, ta=True, name="mm_in_dw")
    token = late_grad(dw_in) if late_grad else None
    da = _matmul(dproj, w_in, tb=True, name="mm_in_dx", after=token)
    dh0, dg_mix = _rmsnorm_bwd_add(h0, g_mix, da, dh1, name="norm_mix_bwd")
    d_rpb = _matmul(dtb.reshape(nh * (2 * NA_WIN_H - 1), GRID_W * GRID_W), oh, tb=True,
                    tm=nh * (2 * NA_WIN_H - 1), tn=LANES, tk=1024, precision=HIGHEST,
                    name="rpb_reduce")
    d_lb = jnp.concatenate([dlb_f.reshape(1, hgw), dlb_b.reshape(1, hgw)], axis=0)
    return (loss, dh0[N_META:l_real], dh0[:N_META], dw_in, dw_na, dw_hg, dw_o, dw_up, dw_down,
            dg_mix, dg_mlp, dg_fin, d_gain, d_rpb, d_lb)


N_CHIPS = 4
N_DEV = 8
ANY = pl.BlockSpec(memory_space=pl.ANY)


def _place():
    x, y, c = lax.axis_index("x"), lax.axis_index("y"), lax.axis_index("c")
    others = []
    for j in (1, 2, 3):
        tx = (1 - x) if (j >> 1) else x
        ty = (1 - y) if (j & 1) else y
        others.append((tx, ty))
    return x, y, c, others


def _piece(ref, axis, k, half, rh, cs):
    if axis == 1:
        return ref.at[pl.ds(pl.multiple_of(half * rh, 16), rh), pl.ds(pl.multiple_of(k * cs, LANES), cs)]
    return ref.at[pl.ds(pl.multiple_of(k * 2 * rh + half * rh, 16), rh), :]


def _cast_into_full(shard, axis, place, *, name):
    r, cs = shard.shape
    full = (r, cs * N_CHIPS) if axis == 1 else (r * N_CHIPS, cs)
    tr = next(t for t in (256, 128, 64, 32, 16) if r % t == 0)
    nt = r // tr

    def body(p_ref, s_ref, o_ref):
        o_ref[...] = s_ref[...].astype(BF16)

    if axis == 1:
        omap = lambda i, p_ref: (i, p_ref[0])
    else:
        omap = lambda i, p_ref: (p_ref[0] * nt + i, 0)
    return pl.pallas_call(
        body,
        grid_spec=pltpu.PrefetchScalarGridSpec(
            num_scalar_prefetch=1, grid=(nt,),
            in_specs=[pl.BlockSpec((tr, cs), lambda i, p_ref: (i, 0))],
            out_specs=pl.BlockSpec((tr, cs), omap)),
        out_shape=jax.ShapeDtypeStruct(full, BF16),
        compiler_params=_cp("parallel"), name=name)(place, shard)


HBM_SPEC = pl.BlockSpec(memory_space=pltpu.HBM)
SEM_SPEC = pl.BlockSpec(memory_space=pltpu.SEMAPHORE)
SPLIT_COPY = pltpu.CompilerParams(has_side_effects=pltpu.SideEffectType.DATAFLOW_SIDE_EFFECTING)
TOKEN = jax.ShapeDtypeStruct((8, LANES), F32)


def _geo(fulls, axes):
    out = []
    for f, ax in zip(fulls, axes):
        r, cs = (f.shape[0], f.shape[1] // N_CHIPS) if ax == 1 else (f.shape[0] // N_CHIPS, f.shape[1])
        out.append((ax, r // 2, cs))
    return out


def _gather_copies(refs, geo, send_sems, recv_sems):
    x, y, c, others = _place()
    chip = 2 * x + y
    cps = []
    for i, (ax, rh, cs) in enumerate(geo):
        mine = _piece(refs[i], ax, chip, c, rh, cs)
        for j, (tx, ty) in enumerate(others):
            cps.append(pltpu.make_async_remote_copy(
                src_ref=mine, dst_ref=mine, send_sem=send_sems.at[3 * i + j],
                recv_sem=recv_sems.at[3 * i + j], device_id=(tx, ty, c), device_id_type=MESH))
    return cps


def _allgather_start(fulls, axes, after, *, name):
    n = len(fulls)
    geo = _geo(fulls, axes)

    def body(*refs):
        w_refs = refs[:n]
        send_sems, recv_sems = refs[n + 1], refs[n + 2]
        token = refs[2 * n + 3]
        for cp in _gather_copies(w_refs, geo, send_sems, recv_sems):
            cp.start()
        token[...] = jnp.zeros_like(token)

    out = pl.pallas_call(
        body, name=name,
        out_shape=(pltpu.SemaphoreType.DMA((3 * n,)), pltpu.SemaphoreType.DMA((3 * n,)),
                   *[pltpu.HBM(f.shape, f.dtype) for f in fulls], TOKEN),
        in_specs=[HBM_SPEC] * n + [ANY],
        out_specs=(SEM_SPEC, SEM_SPEC, *[HBM_SPEC] * n, pl.BlockSpec(memory_space=pltpu.VMEM)),
        input_output_aliases={i: 2 + i for i in range(n)},
        compiler_params=SPLIT_COPY,
    )(*[pltpu.with_memory_space_constraint(f, pltpu.HBM) for f in fulls], after)
    return out[0], out[1], list(out[2:2 + n]), out[2 + n]


def _allgather_wait(send_sems, recv_sems, fulls, axes, after, *, name):
    n = len(fulls)
    geo = _geo(fulls, axes)

    def body(*refs):
        w_refs = refs[:n]
        for cp in _gather_copies(w_refs, geo, refs[n], refs[n + 1]):
            cp.wait_send()
            cp.wait_recv()

    return list(pl.pallas_call(
        body, name=name,
        out_shape=[pltpu.HBM(f.shape, f.dtype) for f in fulls],
        in_specs=[HBM_SPEC] * n + [SEM_SPEC, SEM_SPEC, ANY],
        out_specs=[HBM_SPEC] * n,
        input_output_aliases={i: i for i in range(n)},
        compiler_params=SPLIT_COPY,
    )(*fulls, send_sems, recv_sems, after))


def _allgather_forward(fulls, axes, *, name):
    n = len(fulls)
    geo = _geo(fulls, axes)

    def body(*refs):
        o_refs = refs[n:2 * n]
        send_sems, recv_sems = refs[2 * n:]
        x, y, c, others = _place()

        def rcopy(i, j, half, to):
            ax, rh, cs = geo[i]
            ref = _piece(o_refs[i], ax, 2 * others[j][0] + others[j][1], half, rh, cs)
            return pltpu.make_async_remote_copy(
                src_ref=ref, dst_ref=ref, send_sem=send_sems.at[3 * i + j],
                recv_sem=recv_sems.at[3 * i + j], device_id=to, device_id_type=MESH)

        cps = [rcopy(i, j, c, (x, y, 1 - c)) for i in range(n) for j in range(3)]
        for cp in cps:
            cp.start()
        for i in range(n):
            for j in range(3):
                rcopy(i, j, 1 - c, (x, y, c)).wait_recv()
        for cp in cps:
            cp.wait_send()

    return list(pl.pallas_call(
        body, in_specs=[ANY] * n, out_specs=[ANY] * n,
        out_shape=[jax.ShapeDtypeStruct(f.shape, f.dtype) for f in fulls],
        input_output_aliases={i: i for i in range(n)},
        scratch_shapes=[pltpu.SemaphoreType.DMA((3 * n,)), pltpu.SemaphoreType.DMA((3 * n,))],
        name=name)(*fulls))


def _scatter_geo(parts, axes):
    out = []
    for p, ax in zip(parts, axes):
        _, rh, cols = p.shape
        out.append((ax, rh, cols // N_CHIPS if ax == 1 else cols))
    return out


def _scatter_copies(p_refs, q_refs, geo, send_sems, recv_sems):
    x, y, c, others = _place()
    chip = 2 * x + y
    cps = []
    for i, (ax, rh, cw) in enumerate(geo):
        for j, (tx, ty) in enumerate(others):
            k = 2 * tx + ty
            src = (p_refs[i].at[0, :, pl.ds(pl.multiple_of(k * cw, LANES), cw)] if ax == 1
                   else p_refs[i].at[k])
            cps.append(pltpu.make_async_remote_copy(
                src_ref=src, dst_ref=q_refs[i].at[chip], send_sem=send_sems.at[3 * i + j],
                recv_sem=recv_sems.at[3 * i + j], device_id=(tx, ty, c), device_id_type=MESH))
    return cps


def _scatter_start(parts, axes, *, name):
    n = len(parts)
    geo = _scatter_geo(parts, axes)
    slots = [pltpu.HBM((N_CHIPS, rh, cw), p.dtype) for p, (_, rh, cw) in zip(parts, geo)]

    def body(*refs):
        p_refs, q_refs = refs[:n], refs[n:2 * n]
        send_sems, recv_sems = refs[2 * n], refs[2 * n + 1]
        token = refs[4 * n + 2]
        for cp in _scatter_copies(p_refs, q_refs, geo, send_sems, recv_sems):
            cp.start()
        token[...] = jnp.zeros_like(token)

    land = [pltpu.with_memory_space_constraint(lax.empty(s.inner_aval.shape, s.inner_aval.dtype), pltpu.HBM)
            for s in slots]
    out = pl.pallas_call(
        body, name=name,
        out_shape=(pltpu.SemaphoreType.DMA((3 * n,)), pltpu.SemaphoreType.DMA((3 * n,)),
                   *[pltpu.HBM(p.shape, p.dtype) for p in parts], *slots, TOKEN),
        in_specs=[HBM_SPEC] * (2 * n),
        out_specs=(SEM_SPEC, SEM_SPEC, *[HBM_SPEC] * (2 * n), pl.BlockSpec(memory_space=pltpu.VMEM)),
        input_output_aliases={i: 2 + i for i in range(2 * n)},
        compiler_params=SPLIT_COPY,
    )(*[pltpu.with_memory_space_constraint(p, pltpu.HBM) for p in parts], *land)
    return out[0], out[1], list(out[2:2 + n]), list(out[2 + n:2 + 2 * n]), out[2 + 2 * n]


def _scatter_wait(send_sems, recv_sems, parts, slots, axes, after, *, name):
    n = len(parts)
    geo = _scatter_geo(parts, axes)

    def body(*refs):
        p_refs, q_refs = refs[:n], refs[n:2 * n]
        for cp in _scatter_copies(p_refs, q_refs, geo, refs[2 * n], refs[2 * n + 1]):
            cp.wait_send()
            cp.wait_recv()

    out = pl.pallas_call(
        body, name=name,
        out_shape=[pltpu.HBM(a.shape, a.dtype) for a in (*parts, *slots)],
        in_specs=[HBM_SPEC] * (2 * n) + [SEM_SPEC, SEM_SPEC, ANY],
        out_specs=[HBM_SPEC] * (2 * n),
        input_output_aliases={i: i for i in range(2 * n)},
        compiler_params=SPLIT_COPY,
    )(*parts, *slots, send_sems, recv_sems, after)
    return list(out[:n]), list(out[n:])


def _sibling_swap(grads, *, name):
    n = len(grads)
    out_shape = [jax.ShapeDtypeStruct((g.shape[0], g.shape[1] // 2, g.shape[2]), g.dtype)
                 for g in grads]

    def body(*refs):
        g_refs, o_refs = refs[:n], refs[n:2 * n]
        send_sems, recv_sems = refs[2 * n:]
        x, y, c, _ = _place()
        cps = []
        for i in range(n):
            rh = grads[i].shape[1] // 2
            src = g_refs[i].at[:, pl.ds(pl.multiple_of((1 - c) * rh, 16), rh), :]
            cp = pltpu.make_async_remote_copy(
                src_ref=src, dst_ref=o_refs[i], send_sem=send_sems.at[i], recv_sem=recv_sems.at[i],
                device_id=(x, y, 1 - c), device_id_type=MESH)
            cp.start()
            cps.append(cp)
        for cp in cps:
            cp.wait()

    return pl.pallas_call(
        body, in_specs=[ANY] * n, out_specs=[ANY] * n, out_shape=out_shape,
        scratch_shapes=[pltpu.SemaphoreType.DMA((n,)), pltpu.SemaphoreType.DMA((n,))],
        name=name)(*grads)


def _swap_copies(g_refs, r_refs, shapes, send_sems, recv_sems):
    x, y, c, _ = _place()
    cps = []
    for i, shape in enumerate(shapes):
        rh = shape[1] // 2
        src = g_refs[i].at[:, pl.ds(pl.multiple_of((1 - c) * rh, 16), rh), :]
        cps.append(pltpu.make_async_remote_copy(
            src_ref=src, dst_ref=r_refs[i], send_sem=send_sems.at[i], recv_sem=recv_sems.at[i],
            device_id=(x, y, 1 - c), device_id_type=MESH))
    return cps


def _sibling_swap_start(grads, *, name):
    n = len(grads)
    shapes = [g.shape for g in grads]
    lands = [pltpu.HBM((s[0], s[1] // 2, s[2]), g.dtype) for s, g in zip(shapes, grads)]

    def body(*refs):
        g_refs, r_refs = refs[:n], refs[n:2 * n]
        token = refs[4 * n + 2]
        for cp in _swap_copies(g_refs, r_refs, shapes, refs[2 * n], refs[2 * n + 1]):
            cp.start()
        token[...] = jnp.zeros_like(token)

    land = [pltpu.with_memory_space_constraint(lax.empty(s.inner_aval.shape, s.inner_aval.dtype), pltpu.HBM)
            for s in lands]
    out = pl.pallas_call(
        body, name=name,
        out_shape=(pltpu.SemaphoreType.DMA((n,)), pltpu.SemaphoreType.DMA((n,)),
                   *[pltpu.HBM(g.shape, g.dtype) for g in grads], *lands, TOKEN),
        in_specs=[HBM_SPEC] * (2 * n),
        out_specs=(SEM_SPEC, SEM_SPEC, *[HBM_SPEC] * (2 * n), pl.BlockSpec(memory_space=pltpu.VMEM)),
        input_output_aliases={i: 2 + i for i in range(2 * n)},
        compiler_params=SPLIT_COPY,
    )(*[pltpu.with_memory_space_constraint(g, pltpu.HBM) for g in grads], *land)
    return out[0], out[1], list(out[2:2 + n]), list(out[2 + n:2 + 2 * n]), out[2 + 2 * n]


def _sibling_swap_wait(send_sems, recv_sems, grads, lands, after, *, name):
    n = len(grads)
    shapes = [g.shape for g in grads]

    def body(*refs):
        g_refs, r_refs = refs[:n], refs[n:2 * n]
        for cp in _swap_copies(g_refs, r_refs, shapes, refs[2 * n], refs[2 * n + 1]):
            cp.wait_send()
            cp.wait_recv()

    out = pl.pallas_call(
        body, name=name,
        out_shape=[pltpu.HBM(a.shape, a.dtype) for a in (*grads, *lands)],
        in_specs=[HBM_SPEC] * (2 * n) + [SEM_SPEC, SEM_SPEC, ANY],
        out_specs=[HBM_SPEC] * (2 * n),
        input_output_aliases={i: i for i in range(2 * n)},
        compiler_params=SPLIT_COPY,
    )(*grads, *lands, send_sems, recv_sems, after)
    return list(out[:n]), list(out[n:])


def _pair_add(g3, rx, c_arr, *, out_dtype, name):
    nb, rows, cols = g3.shape
    rh = rows // 2
    tr = next(t for t in (128, 64, 32, 16) if rh % t == 0)
    nt = rh // tr

    def body(c_ref, g_ref, r_ref, o_ref):
        o_ref[...] = (g_ref[...] + r_ref[...]).astype(out_dtype)

    return pl.pallas_call(
        body,
        grid_spec=pltpu.PrefetchScalarGridSpec(
            num_scalar_prefetch=1, grid=(nb, nt),
            in_specs=[pl.BlockSpec((None, tr, cols), lambda b, i, c_ref: (b, c_ref[0] * nt + i, 0)),
                      pl.BlockSpec((None, tr, cols), lambda b, i, c_ref: (b, i, 0))],
            out_specs=pl.BlockSpec((None, tr, cols), lambda b, i, c_ref: (b, i, 0))),
        out_shape=jax.ShapeDtypeStruct((nb, rh, cols), out_dtype),
        compiler_params=_cp("parallel", "parallel"), name=name)(c_arr, g3, rx)


def _sum_slots(q, *, name):
    ns, rows, cols = q.shape
    tr = next(t for t in (128, 64, 32, 16, 8) if rows % t == 0)

    def body(q_ref, o_ref):
        acc = q_ref[0].astype(F32)
        for k in range(1, ns):
            acc = acc + q_ref[k].astype(F32)
        o_ref[...] = acc

    return pl.pallas_call(
        body, grid=(rows // tr,),
        in_specs=[pl.BlockSpec((ns, tr, cols), lambda i: (0, i, 0))],
        out_specs=_rspec(tr, cols),
        out_shape=jax.ShapeDtypeStruct((rows, cols), F32),
        compiler_params=_cp("parallel"), name=name)(q)


def _sum_chips(q, p, place, axis, *, name):
    _, rh, cw = q.shape
    tr = next(t for t in (128, 64, 32, 16) if rh % t == 0)
    nt = rh // tr

    def body(p_ref, *refs):
        q_refs, own_ref, o_ref = refs[:N_CHIPS], refs[N_CHIPS], refs[N_CHIPS + 1]
        chip = p_ref[0]
        acc = jnp.where(chip == 0, own_ref[...], q_refs[0][...]).astype(F32)
        for k in range(1, N_CHIPS):
            acc = acc + jnp.where(chip == k, own_ref[...], q_refs[k][...]).astype(F32)
        o_ref[...] = acc

    def slot_spec(k):
        return pl.BlockSpec((None, tr, cw),
                            lambda i, p_ref: (jnp.where(p_ref[0] == k, (k + 1) % N_CHIPS, k), i, 0))

    if axis == 1:
        own_spec = pl.BlockSpec((None, tr, cw), lambda i, p_ref: (0, i, p_ref[0]))
    else:
        own_spec = pl.BlockSpec((None, tr, cw), lambda i, p_ref: (p_ref[0], i, 0))
    return pl.pallas_call(
        body,
        grid_spec=pltpu.PrefetchScalarGridSpec(
            num_scalar_prefetch=1, grid=(nt,),
            in_specs=[slot_spec(k) for k in range(N_CHIPS)] + [own_spec],
            out_specs=pl.BlockSpec((tr, cw), lambda i, p_ref: (p_ref[1] * nt + i, 0))),
        out_shape=jax.ShapeDtypeStruct((2 * rh, cw), F32),
        compiler_params=_cp("parallel"), name=name)(place, *([q] * N_CHIPS), p)


def _sibling_share(shards, *, name):
    n = len(shards)

    def body(*refs):
        o_refs = refs[n:2 * n]
        send_sems, recv_sems = refs[2 * n:]
        x, y, c, _ = _place()
        cps = []
        for i in range(n):
            rh = shards[i].shape[0] // 2
            mine = o_refs[i].at[pl.ds(pl.multiple_of(c * rh, 8), rh), :]
            cp = pltpu.make_async_remote_copy(
                src_ref=mine, dst_ref=mine, send_sem=send_sems.at[i], recv_sem=recv_sems.at[i],
                device_id=(x, y, 1 - c), device_id_type=MESH)
            cp.start()
            cps.append(cp)
        for i in range(n):
            rh = shards[i].shape[0] // 2
            theirs = o_refs[i].at[pl.ds(pl.multiple_of((1 - c) * rh, 8), rh), :]
            pltpu.make_async_remote_copy(
                src_ref=theirs, dst_ref=theirs, send_sem=send_sems.at[i], recv_sem=recv_sems.at[i],
                device_id=(x, y, c), device_id_type=MESH).wait_recv()
        for cp in cps:
            cp.wait_send()

    return pl.pallas_call(
        body, in_specs=[ANY] * n, out_specs=[ANY] * n,
        out_shape=[jax.ShapeDtypeStruct(h.shape, h.dtype) for h in shards],
        input_output_aliases={i: i for i in range(n)},
        scratch_shapes=[pltpu.SemaphoreType.DMA((n,)), pltpu.SemaphoreType.DMA((n,))],
        name=name)(*shards)


def _gather_all(blk, *, name, after=None):
    rows, cols = blk.shape
    extra = [] if after is None else [after]

    def body(x_ref, *refs):
        out_ref, send_sems, recv_sems, local_sem = refs[len(extra):]
        x, y, c = lax.axis_index("x"), lax.axis_index("y"), lax.axis_index("c")
        me = 4 * x + 2 * y + c
        mine = pltpu.make_async_copy(x_ref, out_ref.at[me], local_sem)
        mine.start()
        cps = []
        for k in range(1, N_DEV):
            tx = (1 - x) if (k >> 2) & 1 else x
            ty = (1 - y) if (k >> 1) & 1 else y
            tc = (1 - c) if k & 1 else c
            cp = pltpu.make_async_remote_copy(
                src_ref=x_ref, dst_ref=out_ref.at[me], send_sem=send_sems.at[k - 1],
                recv_sem=recv_sems.at[k - 1], device_id=(tx, ty, tc), device_id_type=MESH)
            cp.start()
            cps.append(cp)
        for k in range(1, N_DEV):
            tx = (1 - x) if (k >> 2) & 1 else x
            ty = (1 - y) if (k >> 1) & 1 else y
            tc = (1 - c) if k & 1 else c
            got = out_ref.at[4 * tx + 2 * ty + tc]
            pltpu.make_async_remote_copy(
                src_ref=got, dst_ref=got, send_sem=send_sems.at[k - 1], recv_sem=recv_sems.at[k - 1],
                device_id=(x, y, c), device_id_type=MESH).wait_recv()
        for cp in cps:
            cp.wait_send()
        mine.wait()

    vm = pl.BlockSpec(memory_space=pltpu.VMEM)
    return pl.pallas_call(
        body, in_specs=[vm] + [ANY] * len(extra), out_specs=vm,
        out_shape=jax.ShapeDtypeStruct((N_DEV, rows, cols), blk.dtype),
        scratch_shapes=[pltpu.SemaphoreType.DMA((N_DEV - 1,)), pltpu.SemaphoreType.DMA((N_DEV - 1,)),
                        pltpu.SemaphoreType.DMA],
        name=name)(blk, *extra)


def _as_rows(a):
    flat = a.reshape(-1)
    n = flat.shape[0]
    rows = -(-n // (8 * LANES)) * 8
    return jnp.pad(flat, (0, rows * LANES - n)).reshape(rows, LANES)


def _from_rows(p, shape):
    n = int(np.prod(shape))
    return p.reshape(-1)[:n].reshape(shape)


WEIGHT_AXES = (1, 1, 1, 0, 1, 0)
WIRE = BF16


def kernel(x, meta_tokens, w_in, w_na_out, w_hg_out, w_o, w_up, w_down, norm_mix, norm_mlp, norm_final, hg_norm, na_rpb, hg_lb_logits, loss_target, m_meta_tokens, m_w_in, m_w_na_out, m_w_hg_out, m_w_o, m_w_up, m_w_down, m_norm_mix, m_norm_mlp, m_norm_final, m_hg_norm, m_na_rpb, m_hg_lb_logits, v_meta_tokens, v_w_in, v_w_na_out, v_w_hg_out, v_w_o, v_w_up, v_w_down, v_norm_mix, v_norm_mlp, v_norm_final, v_hg_norm, v_na_rpb, v_hg_lb_logits):
    xi, yi, ci = lax.axis_index("x"), lax.axis_index("y"), lax.axis_index("c")
    chip = 2 * xi + yi
    d = x.shape[-1]
    dshard = meta_tokens.shape[1]
    hgw = hg_norm.shape[1]
    lbs = hg_lb_logits.shape[2]
    big = [w_in[0], w_na_out[0], w_hg_out[0], w_o[0], w_up[0], w_down[0]]
    big_m = [m_w_in[0], m_w_na_out[0], m_w_hg_out[0], m_w_o[0], m_w_up[0], m_w_down[0]]
    big_v = [v_w_in[0], v_w_na_out[0], v_w_hg_out[0], v_w_o[0], v_w_up[0], v_w_down[0]]

    place = jnp.stack([chip, ci]).astype(jnp.int32)
    own_w = [_cast_into_full(w, ax, place, name=f"cast_shard_{i}")
             for i, (w, ax) in enumerate(zip(big, WEIGHT_AXES))]
    in_axes, rest_axes = WEIGHT_AXES[:1], WEIGHT_AXES[1:]
    small_in = jnp.concatenate([_as_rows(meta_tokens), _as_rows(hg_lb_logits)], axis=0)
    small_all = _gather_all(small_in, name="gather_small_params")[0::2]
    in_send, in_recv, in_bufs, in_token = _allgather_start(own_w[:1], in_axes, small_all,
                                                           name="weight_allgather_in_start")
    ag_send, ag_recv, ag_bufs, ag_token = _allgather_start(own_w[1:], rest_axes, in_token,
                                                           name="weight_allgather_rest_start")

    def first_weight(after):
        got = _allgather_wait(in_send, in_recv, in_bufs, in_axes, after,
                              name="weight_allgather_in_wait")
        return _allgather_forward(got, in_axes, name="weight_allgather_in_forward")[0]

    def rest_weights(after):
        got = _allgather_wait(ag_send, ag_recv, ag_bufs, rest_axes, after,
                              name="weight_allgather_rest_wait")
        return _allgather_forward(got, rest_axes, name="weight_allgather_rest_forward")

    n_meta_rows = N_META * dshard // LANES
    meta_full = (small_all[:, :n_meta_rows].reshape(N_CHIPS, N_META, dshard)
                 .transpose(1, 0, 2).reshape(N_META, d))
    lbl_full = (small_all[:, n_meta_rows:].reshape(N_CHIPS, -1)[:, :4 * lbs]
                .reshape(N_CHIPS, 2, 2, lbs).transpose(1, 2, 0, 3).reshape(2, 2, N_CHIPS * lbs))
    lb = jax.nn.softmax(lbl_full, axis=1)[:, 0]

    c_arr = ci.reshape(1).astype(jnp.int32)

    def by_chip(dws, axes):
        return [g.reshape(1, *g.shape) if ax == 1
                else g.reshape(N_CHIPS, g.shape[0] // N_CHIPS, g.shape[1]) for g, ax in zip(dws, axes)]

    flying = {}

    def scatter(tag, axes, g3, rx):
        parts = [_pair_add(g, r, c_arr, out_dtype=WIRE, name=f"grad_pair_add_{tag}_{i}")
                 for i, (g, r) in enumerate(zip(g3, rx))]
        send, recv, parts, slots, token = _scatter_start(parts, axes,
                                                         name=f"grad_scatter_{tag}_start")
        flying[tag] = (send, recv, parts, slots)
        return token

    def swap_rest(dws):
        send, recv, g3, lands, token = _sibling_swap_start(by_chip(dws, rest_axes),
                                                           name="grad_sibling_swap_rest_start")
        flying["swap"] = (send, recv, g3, lands)
        return token

    def scatter_rest(after):
        g3, rx = _sibling_swap_wait(*flying["swap"], after, name="grad_sibling_swap_rest_wait")
        return scatter("rest", rest_axes, g3, rx)

    def scatter_in(dw_in):
        g3 = by_chip([dw_in], in_axes)
        return scatter("in", in_axes, g3, _sibling_swap(g3, name="grad_sibling_swap_in"))

    def landed(tag, axes, after):
        parts, slots = _scatter_wait(*flying[tag], axes, after, name=f"grad_scatter_{tag}_wait")
        return [_sum_chips(q, p, place, ax, name=f"grad_sum_chips_{tag}_{i}")
                for i, (q, p, ax) in enumerate(zip(slots, parts, axes))]

    (loss, dx, dmeta, *_, dg_mix, dg_mlp, dg_fin, d_gain, d_rpb, d_lb) = _local_step(
        x[0], loss_target[0], meta_full, first_weight, rest_weights,
        norm_mix + ag_token[0:1, 0:1], norm_mlp, norm_final.reshape(1, d), hg_norm, na_rpb[0], lb,
        swap_rest, scatter_rest, scatter_in)

    halves_rest = landed("rest", rest_axes, dx)
    halves_in = landed("in", in_axes, halves_rest[-1])
    g_big = _sibling_share(halves_in + halves_rest, name="grad_sibling_share")

    d_rpb_c = d_rpb[:, :2 * NA_WIN_W - 1]
    small_g = [dmeta, dg_mix, dg_mlp, dg_fin, d_gain, d_rpb_c, d_lb, loss]
    packed = jnp.concatenate([_as_rows(a) for a in small_g], axis=0)
    total = _sum_slots(_gather_all(packed, after=halves_in[0], name="gather_small_grads"),
                       name="sum_small_grads")
    offs = np.cumsum([0] + [_as_rows(a).shape[0] for a in small_g])
    take = lambda i, shape: _from_rows(total[offs[i]:offs[i + 1]], shape)
    g_meta_full = take(0, (N_META, d))
    g_norm_mix, g_norm_mlp = take(1, (1, d)), take(2, (1, d))
    g_norm_final = take(3, (d,))
    g_hg_norm = take(4, (1, hgw))
    g_rpb = take(5, na_rpb.shape)
    g_lb = take(6, (2, hgw))
    loss_total = take(7, (1, LANES))[0, 0]
    g_meta = lax.dynamic_slice_in_dim(g_meta_full, chip * dshard, dshard, axis=1)
    dl0 = lb * (1.0 - lb) * g_lb
    g_lbl_full = jnp.stack([dl0, -dl0], axis=1)
    g_lbl = lax.dynamic_slice_in_dim(g_lbl_full, chip * lbs, lbs, axis=2)

    big_out = [_adamw(w, g, m, v, name=f"adamw_{i}")
               for i, (w, g, m, v) in enumerate(zip(big, g_big, big_m, big_v))]
    small_w = [meta_tokens, norm_mix, norm_mlp, norm_final, hg_norm, na_rpb, hg_lb_logits]
    small_gr = [g_meta, g_norm_mix, g_norm_mlp, g_norm_final, g_hg_norm, g_rpb, g_lbl]
    small_m = [m_meta_tokens, m_norm_mix, m_norm_mlp, m_norm_final, m_hg_norm, m_na_rpb, m_hg_lb_logits]
    small_v = [v_meta_tokens, v_norm_mix, v_norm_mlp, v_norm_final, v_hg_norm, v_na_rpb, v_hg_lb_logits]
    pk = lambda lst: jnp.concatenate([_as_rows(a) for a in lst], axis=0)
    sd, sm, sv = _adamw(pk(small_w), pk(small_gr), pk(small_m), pk(small_v), name="adamw_small")
    soffs = np.cumsum([0] + [_as_rows(a).shape[0] for a in small_w])
    unpk = lambda p: [_from_rows(p[soffs[i]:soffs[i + 1]], small_w[i].shape) for i in range(len(small_w))]
    sd, sm, sv = unpk(sd), unpk(sm), unpk(sv)

    def order(bigs, smalls):
        return [smalls[0]] + [b.reshape(1, *b.shape) for b in bigs] + smalls[1:]

    grads = order(g_big, small_gr)
    deltas = order([o[0] for o in big_out], sd)
    new_m = order([o[1] for o in big_out], sm)
    new_v = order([o[2] for o in big_out], sv)
    return (loss_total, dx.reshape(1, *dx.shape), *grads, *deltas, *new_m, *new_v)
```

```python
import functools

import numpy as np
import jax
import jax.numpy as jnp
from jax import lax
from jax.experimental import pallas as pl
from jax.experimental.pallas import tpu as pltpu

F32 = jnp.float32
BF16 = jnp.bfloat16
HIGHEST = lax.Precision.HIGHEST

GRID_W = 64
N_META = 16
EPS = 1e-6
NA_HEAD_DIM = 64
NA_WIN_H = 8
NA_WIN_W = 16
HG_DK = 128
HG_CHUNK = 16
LANES = 128
ROW_ALIGN = 128
VMEM_LIMIT = 48 * 1024 * 1024

ADAM_LR = 0.001
ADAM_B1 = 0.9
ADAM_B2 = 0.999
ADAM_EPS = 1e-08
ADAM_WD = 0.01
ADAM_STEP = 10

MESH = pl.DeviceIdType.MESH


def _cp(*sem):
    return pltpu.CompilerParams(dimension_semantics=sem, vmem_limit_bytes=VMEM_LIMIT)


def _sigmoid(x):
    return 1.0 / (1.0 + jnp.exp(-x))


def _dot(a, b, dims, precision=None):
    return lax.dot_general(a, b, (dims, ((), ())), preferred_element_type=F32, precision=precision)


def _nn(a, b, **kw):
    return _dot(a, b, ((1,), (0,)), **kw)


def _nt(a, b, **kw):
    return _dot(a, b, ((1,), (1,)), **kw)


def _tn(a, b, **kw):
    return _dot(a, b, ((0,), (0,)), **kw)


def _matmul(a, b, *, ta=False, tb=False, tm=None, tn=None, tk=None, out_dtype=F32, name,
            precision=None, after=None, epilogue=None, tiles=(), out_dtypes=None):
    extra = [] if after is None else [after]
    single = out_dtypes is None
    if single:
        out_dtypes = (out_dtype,)
    n_t, n_o = len(tiles), len(out_dtypes)
    if ta:
        kdim, m = a.shape
    else:
        m, kdim = a.shape
    if tb:
        n, k2 = b.shape
    else:
        k2, n = b.shape
    assert kdim == k2, (a.shape, b.shape, ta, tb)
    if tm is None:
        if ta:
            tm = next(t for t in (1024, 512, 256, 128, m) if m % t == 0)
        else:
            tm = m // 2 if (m // 2) % 16 == 0 and m > 512 else m
    if tn is None:
        wide = (1024,) if not ta and len(tiles) <= 1 else ()
        tn = next(t for t in (*wide, 512, 256, 128, n) if n % t == 0)
    if tk is None:
        tk = kdim if ta else next(t for t in (1024, 512, 256, 128, kdim) if kdim % t == 0)
    assert m % tm == 0 and n % tn == 0 and kdim % tk == 0, (m, n, kdim, tm, tn, tk)
    nk = kdim // tk
    op_dtype = F32 if precision is not None else BF16

    def body(a_ref, b_ref, *refs):
        t_refs = refs[:n_t]
        o_refs = refs[n_t + len(extra):n_t + len(extra) + n_o]
        av = a_ref[...].astype(op_dtype)
        bv = b_ref[...].astype(op_dtype)
        dims = ((0 if ta else 1,), (1 if tb else 0,))
        part = _dot(av, bv, dims, precision=precision)

        def finish(acc):
            outs = (acc,) if epilogue is None else epilogue(acc, *[t[...] for t in t_refs])
            for o_ref, val in zip(o_refs, outs):
                o_ref[...] = val.astype(o_ref.dtype)

        if nk == 1:
            finish(part)
            return
        acc_ref = refs[-1]
        kk = pl.program_id(2)

        @pl.when(kk == 0)
        def _():
            acc_ref[...] = part

        @pl.when((kk > 0) & (kk < nk - 1))
        def _():
            acc_ref[...] += part

        @pl.when(kk == nk - 1)
        def _():
            finish(acc_ref[...] + part)

    a_spec = (pl.BlockSpec((tk, tm), lambda i, j, k: (k, i)) if ta
              else pl.BlockSpec((tm, tk), lambda i, j, k: (i, k)))
    b_spec = (pl.BlockSpec((tn, tk), lambda i, j, k: (j, k)) if tb
              else pl.BlockSpec((tk, tn), lambda i, j, k: (k, j)))
    for _, off in tiles:
        assert off % tn == 0, (off, tn)
    t_specs = [pl.BlockSpec((tm, tn), functools.partial(lambda i, j, k, o: (i, o + j), o=off // tn))
               for _, off in tiles]
    o_spec = pl.BlockSpec((tm, tn), lambda i, j, k: (i, j))
    outs = pl.pallas_call(
        body,
        grid=(m // tm, n // tn, nk),
        in_specs=[a_spec, b_spec] + t_specs + [pl.BlockSpec(memory_space=pl.ANY)] * len(extra),
        out_specs=[o_spec] * n_o,
        out_shape=[jax.ShapeDtypeStruct((m, n), dt) for dt in out_dtypes],
        scratch_shapes=[pltpu.VMEM((tm, tn), F32)] if nk > 1 else [],
        compiler_params=_cp("parallel", "parallel", "arbitrary"),
        name=name,
    )(a, b, *[t for t, _ in tiles], *extra)
    return outs[0] if single else outs


def _rspec(tr, w, cb=0):
    return pl.BlockSpec((tr, w), lambda i: (i, cb))


def _fspec(shape):
    nd = len(shape)
    return pl.BlockSpec(shape, lambda i: (0,) * nd)


ROW_VMEM_BUDGET = 20 * 1024 * 1024
ROW_MIN_STEPS = 4


def _row_tile(lp, row_bytes):
    for k in range(ROW_MIN_STEPS, lp // 16 + 1):
        tr = lp // k
        if lp % k == 0 and tr % 16 == 0 and 2 * tr * row_bytes <= ROW_VMEM_BUDGET:
            return tr
    return lp


def _rmsnorm_fwd(x, g, *, name):
    lp, d = x.shape
    tr = _row_tile(lp, d * (4 + 2))

    def body(x_ref, g_ref, o_ref):
        xv = x_ref[...]
        r = lax.rsqrt(jnp.mean(xv * xv, axis=-1, keepdims=True) + EPS)
        o_ref[...] = (xv * r * g_ref[...]).astype(BF16)

    return pl.pallas_call(
        body, grid=(lp // tr,),
        in_specs=[_rspec(tr, d), _fspec((1, d))],
        out_specs=_rspec(tr, d),
        out_shape=jax.ShapeDtypeStruct((lp, d), BF16),
        compiler_params=_cp("parallel"), name=name)(x, g)


def _residual_norm(h, t, g, *, name):
    lp, d = h.shape
    tr = _row_tile(lp, d * (4 + 4 + 4 + 2))

    def body(h_ref, t_ref, g_ref, h1_ref, m_ref):
        xv = h_ref[...] + t_ref[...]
        h1_ref[...] = xv
        r = lax.rsqrt(jnp.mean(xv * xv, axis=-1, keepdims=True) + EPS)
        m_ref[...] = (xv * r * g_ref[...]).astype(BF16)

    return pl.pallas_call(
        body, grid=(lp // tr,),
        in_specs=[_rspec(tr, d), _rspec(tr, d), _fspec((1, d))],
        out_specs=[_rspec(tr, d), _rspec(tr, d)],
        out_shape=[jax.ShapeDtypeStruct((lp, d), F32), jax.ShapeDtypeStruct((lp, d), BF16)],
        compiler_params=_cp("parallel"), name=name)(h, t, g)


def _rmsnorm_bwd_add(x, g, dy, dres, *, name):
    lp, d = x.shape
    tr = _row_tile(lp, d * 4 * 4)

    def body(x_ref, g_ref, dy_ref, dr_ref, dx_ref, dg_ref):
        @pl.when(pl.program_id(0) == 0)
        def _():
            dg_ref[...] = jnp.zeros_like(dg_ref)

        xv = x_ref[...]
        r = lax.rsqrt(jnp.mean(xv * xv, axis=-1, keepdims=True) + EPS)
        xh = xv * r
        dyv = dy_ref[...]
        dg_ref[...] += jnp.sum(dyv * xh, axis=0, keepdims=True)
        dxh = dyv * g_ref[...]
        dx_ref[...] = dr_ref[...] + r * (dxh - xh * jnp.mean(dxh * xh, axis=-1, keepdims=True))

    return pl.pallas_call(
        body, grid=(lp // tr,),
        in_specs=[_rspec(tr, d), _fspec((1, d)), _rspec(tr, d), _rspec(tr, d)],
        out_specs=[_rspec(tr, d), _fspec((1, d))],
        out_shape=[jax.ShapeDtypeStruct((lp, d), F32), jax.ShapeDtypeStruct((1, d), F32)],
        compiler_params=_cp("arbitrary"), name=name)(x, g, dy, dres)


def _final_loss(h1, t2, g, tgt, *, n_tok, name):
    lp, d = h1.shape
    tr = _row_tile(lp, d * 4 * 4)

    def body(h_ref, t_ref, g_ref, tg_ref, dh_ref, loss_ref, dg_ref):
        i = pl.program_id(0)

        @pl.when(i == 0)
        def _():
            loss_ref[...] = jnp.zeros_like(loss_ref)
            dg_ref[...] = jnp.zeros_like(dg_ref)

        xv = h_ref[...] + t_ref[...]
        r = lax.rsqrt(jnp.mean(xv * xv, axis=-1, keepdims=True) + EPS)
        xh = xv * r
        gv = g_ref[...]
        row = i * tr + lax.broadcasted_iota(jnp.int32, (tr, 1), 0)
        valid = (row >= N_META) & (row < N_META + n_tok)
        err = jnp.where(valid, xh * gv - tg_ref[...], 0.0)
        loss_ref[...] += jnp.sum(0.5 * err * err) / d
        dy = err / d
        dg_ref[...] += jnp.sum(dy * xh, axis=0, keepdims=True)
        dxh = dy * gv
        dh_ref[...] = r * (dxh - xh * jnp.mean(dxh * xh, axis=-1, keepdims=True))

    return pl.pallas_call(
        body, grid=(lp // tr,),
        in_specs=[_rspec(tr, d), _rspec(tr, d), _fspec((1, d)), _rspec(tr, d)],
        out_specs=[_rspec(tr, d), _fspec((1, LANES)), _fspec((1, d))],
        out_shape=[jax.ShapeDtypeStruct((lp, d), F32), jax.ShapeDtypeStruct((1, LANES), F32),
                   jax.ShapeDtypeStruct((1, d), F32)],
        compiler_params=_cp("arbitrary"), name=name)(h1, t2, g, tgt)


def _hg_out(o_f, o_b, proj, gain, *, col_g, name):
    lp, w = o_f.shape
    tr = _row_tile(lp, w * (3 * 4 + 2))
    hh = w // HG_DK

    def body(of_ref, ob_ref, g_ref, gain_ref, y_ref):
        gv = g_ref[...]
        sg = gv * _sigmoid(gv)
        for h in range(hh):
            sl = slice(h * HG_DK, (h + 1) * HG_DK)
            o = of_ref[:, sl] + ob_ref[:, sl]
            r = lax.rsqrt(jnp.mean(o * o, axis=-1, keepdims=True) + EPS)
            y_ref[:, sl] = (o * r * gain_ref[:, sl] * sg[:, sl]).astype(BF16)

    return pl.pallas_call(
        body, grid=(lp // tr,),
        in_specs=[_rspec(tr, w), _rspec(tr, w), _rspec(tr, w, col_g // w), _fspec((1, w))],
        out_specs=_rspec(tr, w),
        out_shape=jax.ShapeDtypeStruct((lp, w), BF16),
        compiler_params=_cp("parallel"), name=name)(o_f, o_b, proj, gain)


def _hg_out_bwd(o_f, o_b, proj, gain, dy, *, col_g, name):
    lp, w = o_f.shape
    tr = _row_tile(lp, w * (5 * 4 + 2))
    hh = w // HG_DK

    def body(of_ref, ob_ref, g_ref, gain_ref, dy_ref, do_ref, dg_ref, dgain_ref):
        @pl.when(pl.program_id(0) == 0)
        def _():
            dgain_ref[...] = jnp.zeros_like(dgain_ref)

        for h in range(hh):
            sl = slice(h * HG_DK, (h + 1) * HG_DK)
            gv = g_ref[:, sl]
            s = _sigmoid(gv)
            sg = gv * s
            dsg = s + gv * s * (1.0 - s)
            o = of_ref[:, sl] + ob_ref[:, sl]
            r = lax.rsqrt(jnp.mean(o * o, axis=-1, keepdims=True) + EPS)
            on = o * r
            dyv = dy_ref[:, sl]
            gn = gain_ref[:, sl]
            dgain_ref[:, sl] += jnp.sum(dyv * on * sg, axis=0, keepdims=True)
            dg_ref[:, sl] = (dyv * on * gn * dsg).astype(BF16)
            don = dyv * gn * sg
            do_ref[:, sl] = r * (don - on * jnp.mean(don * on, axis=-1, keepdims=True))

    return pl.pallas_call(
        body, grid=(lp // tr,),
        in_specs=[_rspec(tr, w), _rspec(tr, w), _rspec(tr, w, col_g // w), _fspec((1, w)),
                  _rspec(tr, w)],
        out_specs=[_rspec(tr, w), _rspec(tr, w), _fspec((1, w))],
        out_shape=[jax.ShapeDtypeStruct((lp, w), F32), jax.ShapeDtypeStruct((lp, w), BF16),
                   jax.ShapeDtypeStruct((1, w), F32)],
        compiler_params=_cp("arbitrary"), name=name)(o_f, o_b, proj, gain, dy)


HG_ROWS = 128
HG_HALVES = (1, 2, 4, 8, 16, 32, 64)


def _hg_gates(zq, z, lbv):
    qh = zq * _sigmoid(zq)
    s = _sigmoid(z)
    f = lbv + (1.0 - lbv) * s
    kk = (1.0 - lbv) * _sigmoid(-z)
    return qh, s, f, jnp.log(f), kk


def _block_cumsum(g, pos, suffix):
    x = g
    for k in HG_HALVES:
        if suffix:
            x = x + jnp.where(pos < HG_ROWS - k, pltpu.roll(x, HG_ROWS - k, 0), 0.0)
        else:
            x = x + jnp.where(pos >= k, pltpu.roll(x, k, 0), 0.0)
    return x


def _pair_levels(b, pos, reverse):
    out = []
    first = b
    for m in HG_HALVES:
        if m > 1:
            first = jnp.where((pos & (m - 1)) >= m // 2, pltpu.roll(first, m // 2, 0), first)
        nxt = pltpu.roll(first, HG_ROWS - m, 0)
        upper = (pos & (2 * m - 1)) >= m
        if reverse:
            eq = jnp.where(upper, 0.0, jnp.exp(b - nxt))
            ek = jnp.where(upper, jnp.exp(first - b), 0.0)
        else:
            eq = jnp.where(upper, jnp.exp(b - first), 0.0)
            ek = jnp.where(upper, 0.0, jnp.exp(nxt - b))
        out.append((eq, ek))
    return out


def _pair_masks(mask_ref):
    ri = lax.broadcasted_iota(jnp.int32, (HG_ROWS, HG_ROWS), 0)
    ci = lax.broadcasted_iota(jnp.int32, (HG_ROWS, HG_ROWS), 1)
    for i, m in enumerate(HG_HALVES):
        sh = m.bit_length()
        mask_ref[i] = jnp.where((ri >> sh) == (ci >> sh), 1.0, 0.0)


def _hg_scan_fwd(proj, lb, *, reverse, col_q, col_z, col_i, hh, name):
    lp = proj.shape[0]
    n_blocks = lp // HG_ROWS
    last = 0 if reverse else HG_ROWS - 1

    def body(q_ref, z_ref, i_ref, lb_ref, o_ref, st_ref, mask_ref):
        lbv = lb_ref[...]
        pos = lax.broadcasted_iota(jnp.int32, (HG_ROWS, 1), 0)
        ri = lax.broadcasted_iota(jnp.int32, (HG_ROWS, HG_ROWS), 0)
        ci = lax.broadcasted_iota(jnp.int32, (HG_ROWS, HG_ROWS), 1)
        _pair_masks(mask_ref)

        def block(bi, st):
            bb = (n_blocks - 1 - bi) if reverse else bi
            r0 = pl.multiple_of(bb * HG_ROWS, HG_ROWS)
            v16 = i_ref[pl.ds(r0, HG_ROWS), :].astype(BF16)
            qh, _, _, g, kk = _hg_gates(q_ref[pl.ds(r0, HG_ROWS), :], z_ref[pl.ds(r0, HG_ROWS), :],
                                        lbv)
            b = _block_cumsum(g, pos, reverse)
            bl = b[last:last + 1, :]
            qe = (qh * jnp.exp(b)).astype(BF16)
            kd = (kk * jnp.exp(bl - b)).astype(BF16)
            a = jnp.where(ri == ci, jnp.sum(qh * kk, axis=1, keepdims=True), 0.0)
            for i, (eq, ek) in enumerate(_pair_levels(b, pos, reverse)):
                a = a + mask_ref[i] * _nt((qh * eq).astype(BF16), (kk * ek).astype(BF16))
            st_ref[bb] = st
            o_ref[pl.ds(r0, HG_ROWS), :] = _nn(a.astype(BF16), v16) + _nt(qe, st.astype(BF16))
            return jnp.exp(bl) * st + _tn(v16, kd)

        lax.fori_loop(0, n_blocks, block, jnp.zeros((HG_DK, HG_DK), F32))

    cspec = lambda col: pl.BlockSpec((lp, HG_DK), lambda h: (0, col // HG_DK + h))
    return pl.pallas_call(
        body, grid=(hh,),
        in_specs=[cspec(col_q), cspec(col_z), cspec(col_i),
                  pl.BlockSpec((None, 1, HG_DK), lambda h: (h, 0, 0))],
        out_specs=[pl.BlockSpec((lp, HG_DK), lambda h: (0, h)),
                   pl.BlockSpec((None, n_blocks, HG_DK, HG_DK), lambda h: (h, 0, 0, 0))],
        out_shape=[jax.ShapeDtypeStruct((lp, hh * HG_DK), F32),
                   jax.ShapeDtypeStruct((hh, n_blocks, HG_DK, HG_DK), F32)],
        scratch_shapes=[pltpu.VMEM((len(HG_HALVES), HG_ROWS, HG_ROWS), F32)],
        compiler_params=_cp("parallel"), name=name)(proj, proj, proj, lb)


def _hg_scan_bwd(proj, lb, states, do, *, reverse, col_q, col_z, col_i, hh, name):
    lp = proj.shape[0]
    n_blocks = lp // HG_ROWS
    last = 0 if reverse else HG_ROWS - 1

    def body(q_ref, z_ref, i_ref, lb_ref, st_ref, do_ref, dq_ref, dz_ref, dv_ref, dlb_ref, mask_ref):
        lbv = lb_ref[...]
        pos = lax.broadcasted_iota(jnp.int32, (HG_ROWS, 1), 0)
        ri = lax.broadcasted_iota(jnp.int32, (HG_ROWS, HG_ROWS), 0)
        ci = lax.broadcasted_iota(jnp.int32, (HG_ROWS, HG_ROWS), 1)
        _pair_masks(mask_ref)

        def block(bi, carry):
            dst, dlb = carry
            bb = bi if reverse else (n_blocks - 1 - bi)
            r0 = pl.multiple_of(bb * HG_ROWS, HG_ROWS)
            zq = q_ref[pl.ds(r0, HG_ROWS), :]
            v16 = i_ref[pl.ds(r0, HG_ROWS), :].astype(BF16)
            do16 = do_ref[pl.ds(r0, HG_ROWS), :].astype(BF16)
            qh, s, f, g, kk = _hg_gates(zq, z_ref[pl.ds(r0, HG_ROWS), :], lbv)
            b = _block_cumsum(g, pos, reverse)
            bl = b[last:last + 1, :]
            eb = jnp.exp(b)
            ebl = jnp.exp(bl - b)
            decay = jnp.exp(bl)
            qe16 = (qh * eb).astype(BF16)
            kd16 = (kk * ebl).astype(BF16)
            st = st_ref[bb]
            st16, dst16 = st.astype(BF16), dst.astype(BF16)
            same_row = ri == ci
            da = _nt(do16, v16)
            da_diag = jnp.sum(jnp.where(same_row, da, 0.0), axis=1, keepdims=True)
            dq_state = eb * _nn(do16, st16)
            dk_state = ebl * _nn(v16, dst16)
            dq = dq_state + da_diag * kk
            dk = dk_state + da_diag * qh
            dbl = (decay * jnp.sum(st * dst, axis=0, keepdims=True)
                   + jnp.sum(kk * dk_state, axis=0, keepdims=True))
            db = qh * dq_state - kk * dk_state + jnp.where(pos == last, dbl, 0.0)
            a = jnp.where(same_row, jnp.sum(qh * kk, axis=1, keepdims=True), 0.0)
            for i, (eq, ek) in enumerate(_pair_levels(b, pos, reverse)):
                same = mask_ref[i]
                q16, k16 = (qh * eq).astype(BF16), (kk * ek).astype(BF16)
                a = a + same * _nt(q16, k16)
                da16 = (same * da).astype(BF16)
                gq, gk = _nn(da16, k16), _tn(da16, q16)
                dq = dq + eq * gq
                dk = dk + ek * gk
                db = db + (q16.astype(F32) * gq - k16.astype(F32) * gk)
            dg = _block_cumsum(db, pos, not reverse)
            df = dg / f - dk
            sq = _sigmoid(zq)
            dq_ref[pl.ds(r0, HG_ROWS), :] = dq * (sq + zq * sq * (1.0 - sq))
            dz_ref[pl.ds(r0, HG_ROWS), :] = df * (1.0 - lbv) * s * (1.0 - s)
            dv_ref[pl.ds(r0, HG_ROWS), :] = _nt(kd16, dst16) + _tn(a.astype(BF16), do16)
            return (decay * dst + _tn(do16, qe16),
                    dlb + jnp.sum(df * (1.0 - s), axis=0, keepdims=True))

        _, dlb = lax.fori_loop(0, n_blocks, block,
                               (jnp.zeros((HG_DK, HG_DK), F32), jnp.zeros((1, HG_DK), F32)))
        dlb_ref[...] = dlb

    cspec = lambda col: pl.BlockSpec((lp, HG_DK), lambda h: (0, col // HG_DK + h))
    ospec = pl.BlockSpec((lp, HG_DK), lambda h: (0, h))
    sds = jax.ShapeDtypeStruct((lp, hh * HG_DK), F32)
    return pl.pallas_call(
        body, grid=(hh,),
        in_specs=[cspec(col_q), cspec(col_z), cspec(col_i),
                  pl.BlockSpec((None, 1, HG_DK), lambda h: (h, 0, 0)),
                  pl.BlockSpec((None, n_blocks, HG_DK, HG_DK), lambda h: (h, 0, 0, 0)),
                  ospec],
        out_specs=[ospec, ospec, ospec, pl.BlockSpec((None, 1, HG_DK), lambda h: (h, 0, 0))],
        out_shape=[sds, sds, sds, jax.ShapeDtypeStruct((hh, 1, HG_DK), F32)],
        scratch_shapes=[pltpu.VMEM((len(HG_HALVES), HG_ROWS, HG_ROWS), F32)],
        compiler_params=_cp("parallel"), name=name)(proj, proj, proj, lb, states, do)


NA_HB = LANES // NA_HEAD_DIM
NA_G = 4
NA_U = NA_G + NA_WIN_H
NA_QN = NA_G * GRID_W
NA_KN = NA_U * GRID_W


def _na_table_index(pattern, a, j):
    if pattern == 0:
        return j - a + NA_WIN_H - 1 if j < NA_WIN_H else None
    if pattern == 2:
        return j - a - 1 if j >= NA_U - NA_WIN_H else None
    return j - a + NA_WIN_H // 2 - 1 if a <= j < a + NA_WIN_H else None


def _na_step_rows(pattern, t, rows):
    if pattern == 0:
        r0, us = 0, 0
    elif pattern == 2:
        r0, us = rows - NA_G, rows - NA_U
    else:
        r0 = NA_G * t
        us = r0 - NA_WIN_H // 2
    q0, k0 = N_META + GRID_W * r0, N_META + GRID_W * us
    if pattern == 1:
        q0, k0 = pl.multiple_of(q0, 16), pl.multiple_of(k0, 16)
    return q0, k0


def _na_fill_bias(tb_ref, bias_ref):
    neg = jnp.full((GRID_W, GRID_W), -1e30, F32)
    for h in range(NA_HB):
        for pattern in range(3):
            for a in range(NA_G):
                for j in range(NA_U):
                    idx = _na_table_index(pattern, a, j)
                    bias_ref[h, pattern, a * GRID_W:(a + 1) * GRID_W, j * GRID_W:(j + 1) * GRID_W] = (
                        neg if idx is None else tb_ref[h, idx])


def _na_steps(rows, step, carry):
    n_steps = rows // NA_G
    carry = step(0, 0, carry)
    carry = lax.fori_loop(1, n_steps - 1, functools.partial(step, 1), carry)
    return step(2, n_steps - 1, carry)


def _na_head_lanes():
    lane = lax.broadcasted_iota(jnp.int32, (1, LANES), 1)
    return [lane // NA_HEAD_DIM == h for h in range(NA_HB)]


def _na_only(mask, x):
    return jnp.where(mask, x, jnp.zeros_like(x))


def _na_fwd(proj, tb, *, n_tok, nh, name):
    lp = proj.shape[0]
    dh, hb = NA_HEAD_DIM, NA_HB
    naw = nh * dh
    rows = n_tok // GRID_W
    scale = dh ** -0.5

    def body(q_ref, k_ref, v_ref, tb_ref, o_ref, lse_ref, q16_ref, k16_ref, v16_ref, bias_ref):
        o_ref[...] = jnp.zeros_like(o_ref)
        lse_ref[...] = jnp.zeros_like(lse_ref)
        q16_ref[...] = q_ref[...].astype(BF16)
        k16_ref[...] = k_ref[...].astype(BF16)
        v16_ref[...] = v_ref[...].astype(BF16)
        _na_fill_bias(tb_ref, bias_ref)
        heads = _na_head_lanes()
        km = k16_ref[0:N_META, :]
        vm = v16_ref[0:N_META, :]
        qm = q16_ref[0:N_META, :]
        o_m = None
        for h in range(hb):
            s = _nt(_na_only(heads[h], qm), km) * scale
            m = jnp.max(s, axis=1, keepdims=True)
            p = jnp.exp(s - m)
            l = jnp.sum(p, axis=1, keepdims=True)
            o_h = _nn(p.astype(BF16), vm) / l
            o_m = o_h if o_m is None else jnp.where(heads[h], o_h, o_m)
            lse_ref[h, 0:N_META, :] = m + jnp.log(l)
        o_ref[0:N_META, :] = o_m

        def step(pattern, t, carry):
            q0, k0 = _na_step_rows(pattern, t, rows)
            q16 = q16_ref[pl.ds(q0, NA_QN), :]
            k16 = k16_ref[pl.ds(k0, NA_KN), :]
            v16 = v16_ref[pl.ds(k0, NA_KN), :]
            o = None
            for h in range(hb):
                q_h = _na_only(heads[h], q16)
                s = _nt(q_h, k16) * scale + bias_ref[h, pattern]
                sm = _nt(q_h, km) * scale
                m = jnp.maximum(jnp.max(s, axis=1, keepdims=True),
                                jnp.max(sm, axis=1, keepdims=True))
                p = jnp.exp(s - m)
                pm = jnp.exp(sm - m)
                l = jnp.sum(p, axis=1, keepdims=True) + jnp.sum(pm, axis=1, keepdims=True)
                o_h = (_nn(p.astype(BF16), v16) + _nn(pm.astype(BF16), vm)) / l
                o = o_h if o is None else jnp.where(heads[h], o_h, o)
                lse_ref[h, pl.ds(q0, NA_QN), :] = m + jnp.log(l)
            o_ref[pl.ds(q0, NA_QN), :] = o
            return carry

        _na_steps(rows, step, 0)

    cblk = lambda col: pl.BlockSpec((lp, LANES), lambda g: (0, col // LANES + g))
    return pl.pallas_call(
        body, grid=(nh // hb,),
        in_specs=[cblk(0), cblk(naw), cblk(2 * naw),
                  pl.BlockSpec((hb, 2 * NA_WIN_H - 1, GRID_W, GRID_W), lambda g: (g, 0, 0, 0))],
        out_specs=[cblk(0), pl.BlockSpec((hb, lp, 1), lambda g: (g, 0, 0))],
        out_shape=[jax.ShapeDtypeStruct((lp, naw), F32), jax.ShapeDtypeStruct((nh, lp, 1), F32)],
        scratch_shapes=[pltpu.VMEM((lp, LANES), BF16)] * 3 + [pltpu.VMEM((hb, 3, NA_QN, NA_KN), F32)],
        compiler_params=_cp("parallel"), name=name)(proj, proj, proj, tb)


def _na_bwd(proj, tb, o, lse, do, *, n_tok, nh, name):
    lp = proj.shape[0]
    dh, hb = NA_HEAD_DIM, NA_HB
    naw = nh * dh
    rows = n_tok // GRID_W
    scale = dh ** -0.5

    def body(q_ref, k_ref, v_ref, tb_ref, o_ref, lse_ref, do_ref, dq_ref, dk_ref, dv_ref, dtb_ref,
             q16_ref, k16_ref, v16_ref, bias_ref):
        dq_ref[...] = jnp.zeros_like(dq_ref)
        dk_ref[...] = jnp.zeros_like(dk_ref)
        dv_ref[...] = jnp.zeros_like(dv_ref)
        dtb_ref[...] = jnp.zeros_like(dtb_ref)
        q16_ref[...] = q_ref[...].astype(BF16)
        k16_ref[...] = k_ref[...].astype(BF16)
        v16_ref[...] = v_ref[...].astype(BF16)
        _na_fill_bias(tb_ref, bias_ref)
        heads = _na_head_lanes()
        km = k16_ref[0:N_META, :]
        vm = v16_ref[0:N_META, :]
        qm = q16_ref[0:N_META, :]
        dom = do_ref[0:N_META, :]
        prod = dom * o_ref[0:N_META, :]
        dq_m = None
        dkm0 = jnp.zeros((N_META, LANES), F32)
        dvm0 = jnp.zeros((N_META, LANES), F32)
        for h in range(hb):
            q_h = _na_only(heads[h], qm)
            do_h = _na_only(heads[h], dom).astype(BF16)
            p = jnp.exp(_nt(q_h, km) * scale - lse_ref[h, 0:N_META, :])
            delta = jnp.sum(_na_only(heads[h], prod), axis=1, keepdims=True)
            ds = (p * (_nt(do_h, vm) - delta)).astype(BF16)
            dq_h = _nn(ds, km) * scale
            dq_m = dq_h if dq_m is None else jnp.where(heads[h], dq_h, dq_m)
            dkm0 = dkm0 + _tn(ds, q_h) * scale
            dvm0 = dvm0 + _tn(p.astype(BF16), do_h)
        dq_ref[0:N_META, :] = dq_m

        def step(pattern, t, carry):
            dkm, dvm = carry
            q0, k0 = _na_step_rows(pattern, t, rows)
            q16 = q16_ref[pl.ds(q0, NA_QN), :]
            k16 = k16_ref[pl.ds(k0, NA_KN), :]
            v16 = v16_ref[pl.ds(k0, NA_KN), :]
            dov = do_ref[pl.ds(q0, NA_QN), :]
            prod = dov * o_ref[pl.ds(q0, NA_QN), :]
            dq = None
            dk = jnp.zeros((NA_KN, LANES), F32)
            dv = jnp.zeros((NA_KN, LANES), F32)
            for h in range(hb):
                q_h = _na_only(heads[h], q16)
                do_h = _na_only(heads[h], dov).astype(BF16)
                lse = lse_ref[h, pl.ds(q0, NA_QN), :]
                p = jnp.exp(_nt(q_h, k16) * scale + bias_ref[h, pattern] - lse)
                pm = jnp.exp(_nt(q_h, km) * scale - lse)
                delta = jnp.sum(_na_only(heads[h], prod), axis=1, keepdims=True)
                ds = p * (_nt(do_h, v16) - delta)
                dsm = (pm * (_nt(do_h, vm) - delta)).astype(BF16)
                ds16 = ds.astype(BF16)
                dq_h = (_nn(ds16, k16) + _nn(dsm, km)) * scale
                dq = dq_h if dq is None else jnp.where(heads[h], dq_h, dq)
                dk = dk + _tn(ds16, q_h) * scale
                dv = dv + _tn(p.astype(BF16), do_h)
                dkm = dkm + _tn(dsm, q_h) * scale
                dvm = dvm + _tn(pm.astype(BF16), do_h)
                for a in range(NA_G):
                    for j in range(NA_U):
                        idx = _na_table_index(pattern, a, j)
                        if idx is not None:
                            dtb_ref[h, idx] += ds[a * GRID_W:(a + 1) * GRID_W,
                                                  j * GRID_W:(j + 1) * GRID_W]
            dq_ref[pl.ds(q0, NA_QN), :] = dq
            dk_ref[pl.ds(k0, NA_KN), :] += dk
            dv_ref[pl.ds(k0, NA_KN), :] += dv
            return dkm, dvm

        dkm, dvm = _na_steps(rows, step, (dkm0, dvm0))
        dk_ref[0:N_META, :] += dkm
        dv_ref[0:N_META, :] += dvm

    cblk = lambda col: pl.BlockSpec((lp, LANES), lambda g: (0, col // LANES + g))
    tbs = pl.BlockSpec((hb, 2 * NA_WIN_H - 1, GRID_W, GRID_W), lambda g: (g, 0, 0, 0))
    sds = jax.ShapeDtypeStruct((lp, naw), F32)
    return pl.pallas_call(
        body, grid=(nh // hb,),
        in_specs=[cblk(0), cblk(naw), cblk(2 * naw), tbs, cblk(0),
                  pl.BlockSpec((hb, lp, 1), lambda g: (g, 0, 0)), cblk(0)],
        out_specs=[cblk(0), cblk(0), cblk(0), tbs],
        out_shape=[sds, sds, sds, jax.ShapeDtypeStruct(tb.shape, F32)],
        scratch_shapes=[pltpu.VMEM((lp, LANES), BF16)] * 3 + [pltpu.VMEM((hb, 3, NA_QN, NA_KN), F32)],
        compiler_params=_cp("parallel"), name=name)(proj, proj, proj, tb, o, lse, do)


def _rpb_onehot():
    c = np.arange(GRID_W)[:, None]
    w = np.arange(GRID_W)[None, :]
    cs = np.clip(c - NA_WIN_W // 2, 0, GRID_W - NA_WIN_W)
    in_win = (w >= cs) & (w < cs + NA_WIN_W)
    dc = np.clip(w - c, -(NA_WIN_W - 1), NA_WIN_W - 1) + NA_WIN_W - 1
    oh = np.zeros((LANES, GRID_W * GRID_W), np.float32)
    flat = np.arange(GRID_W * GRID_W).reshape(GRID_W, GRID_W)
    oh[dc[in_win], flat[in_win]] = 1.0
    neg = np.where(in_win, 0.0, -1e30).astype(np.float32).reshape(1, -1)
    return oh, neg


def _assemble_dproj(dq_na, dk_na, dv_na, dq_f, dq_b, dz_f, dz_b, dv_f, dv_b, dg, dgn, dgh, *, name):
    lp, naw = dq_na.shape
    hgw = dq_f.shape[1]
    d = dgn.shape[1]
    cols = 3 * naw + 5 * hgw + 2 * d
    tr = _row_tile(lp, 3 * naw * 4 + 6 * hgw * 4 + hgw * 2 + 2 * d * 2 + cols * 2)

    def body(nq_ref, nk_ref, nv_ref, qf_ref, qb_ref, zf_ref, zb_ref, vf_ref, vb_ref, g_ref, gn_ref,
             gh_ref, o_ref):
        o_ref[:, 0:naw] = nq_ref[...].astype(BF16)
        o_ref[:, naw:2 * naw] = nk_ref[...].astype(BF16)
        o_ref[:, 2 * naw:3 * naw] = nv_ref[...].astype(BF16)
        c0 = 3 * naw
        o_ref[:, c0:c0 + hgw] = (qf_ref[...] + qb_ref[...]).astype(BF16)
        o_ref[:, c0 + hgw:c0 + 2 * hgw] = zf_ref[...].astype(BF16)
        o_ref[:, c0 + 2 * hgw:c0 + 3 * hgw] = zb_ref[...].astype(BF16)
        o_ref[:, c0 + 3 * hgw:c0 + 4 * hgw] = (vf_ref[...] + vb_ref[...]).astype(BF16)
        o_ref[:, c0 + 4 * hgw:c0 + 5 * hgw] = g_ref[...]
        o_ref[:, c0 + 5 * hgw:c0 + 5 * hgw + d] = gn_ref[...]
        o_ref[:, c0 + 5 * hgw + d:] = gh_ref[...]

    hg, na = _rspec(tr, hgw), _rspec(tr, naw)
    return pl.pallas_call(
        body, grid=(lp // tr,),
        in_specs=[na, na, na, hg, hg, hg, hg, hg, hg, hg, _rspec(tr, d), _rspec(tr, d)],
        out_specs=_rspec(tr, cols),
        out_shape=jax.ShapeDtypeStruct((lp, cols), BF16),
        compiler_params=_cp("parallel"), name=name)(dq_na, dk_na, dv_na, dq_f, dq_b, dz_f, dz_b,
                                                    dv_f, dv_b, dg, dgn, dgh)


def _adamw(w, g, m, v, *, name):
    rows, cols = w.shape
    tr = 256 if rows % 256 == 0 else rows

    def body(w_ref, g_ref, m_ref, v_ref, d_ref, mo_ref, vo_ref):
        gv = g_ref[...]
        mn = ADAM_B1 * m_ref[...] + (1.0 - ADAM_B1) * gv
        vn = ADAM_B2 * v_ref[...] + (1.0 - ADAM_B2) * (gv * gv)
        m_hat = mn / (1.0 - ADAM_B1 ** ADAM_STEP)
        v_hat = vn / (1.0 - ADAM_B2 ** ADAM_STEP)
        d_ref[...] = -ADAM_LR * (m_hat / (jnp.sqrt(v_hat) + ADAM_EPS) + ADAM_WD * w_ref[...])
        mo_ref[...] = mn
        vo_ref[...] = vn

    spec = _rspec(tr, cols)
    sds = jax.ShapeDtypeStruct((rows, cols), F32)
    return pl.pallas_call(
        body, grid=(rows // tr,), in_specs=[spec] * 4, out_specs=[spec] * 3, out_shape=[sds] * 3,
        compiler_params=_cp("parallel"), name=name)(w, g, m, v)


def _local_step(x, tgt, meta, first_weight, rest_weights, g_mix, g_mlp, g_fin, hg_gain, rpb, lb,
                early_grads=None, mid_grads=None, late_grad=None, rest_landed=None):
    n_tok, d = x.shape
    hgw = hg_gain.shape[1]
    nh, hh = rpb.shape[0], hgw // HG_DK
    naw = nh * NA_HEAD_DIM
    l_real = N_META + n_tok
    lp = -(-l_real // ROW_ALIGN) * ROW_ALIGN
    n_chunks = l_real // HG_CHUNK
    pad = lp - l_real
    col_qhg = 3 * naw
    col_zf, col_zb, col_i, col_g = (col_qhg + hgw, col_qhg + 2 * hgw, col_qhg + 3 * hgw,
                                    col_qhg + 4 * hgw)
    col_gate = col_qhg + 5 * hgw

    zpad = jnp.zeros((pad, d), F32)
    h0 = jnp.concatenate([meta, x, zpad], axis=0)
    tgt_p = jnp.concatenate([jnp.zeros((N_META, d), F32), tgt, zpad], axis=0)

    oh_np, neg_np = _rpb_onehot()
    oh = jnp.asarray(oh_np)
    rpb_p = jnp.pad(rpb.reshape(nh * (2 * NA_WIN_H - 1), 2 * NA_WIN_W - 1),
                    ((0, 0), (0, LANES - (2 * NA_WIN_W - 1))))
    tb = _matmul(rpb_p, oh, tm=rpb_p.shape[0], tn=512, tk=LANES, precision=HIGHEST,
                 name="rpb_expand")
    tb = (tb + jnp.asarray(neg_np)).reshape(nh, 2 * NA_WIN_H - 1, GRID_W, GRID_W)

    a = _rmsnorm_fwd(h0, g_mix, name="norm_mix")
    w_in = first_weight(a)
    proj = _matmul(a, w_in, name="mm_in")
    o_na, lse = _na_fwd(proj, tb, n_tok=n_tok, nh=nh, name="na_fwd")
    lb_f = lb[0].reshape(hh, 1, HG_DK)
    lb_b = lb[1].reshape(hh, 1, HG_DK)
    scan_kw = dict(col_q=col_qhg, col_i=col_i, hh=hh)
    o_f, st_f = _hg_scan_fwd(proj, lb_f, reverse=False, col_z=col_zf, name="hg_scan_f", **scan_kw)
    token = rest_landed(o_f) if rest_landed else None
    lb_b_late = lb_b if token is None else lb_b + token[0:1, 0:1]
    o_b, st_b = _hg_scan_fwd(proj, lb_b_late, reverse=True, col_z=col_zb, name="hg_scan_b",
                             **scan_kw)
    o_hg = _hg_out(o_f, o_b, proj, hg_gain, col_g=col_g, name="hg_out")
    w_na, w_hg, w_o, w_up, w_down = rest_weights(o_hg)
    y_na = _matmul(o_na, w_na, name="mm_na_out")
    gates = ((proj, col_gate), (proj, col_gate + d))

    def mix_gates(acc, gn, gh, yn):
        return acc, _sigmoid(gn) * yn + _sigmoid(gh) * acc

    def mix_gates_bwd(dmix, gn, gh, yn, yh):
        sn, sh = _sigmoid(gn), _sigmoid(gh)
        return dmix * sn, dmix * sh, dmix * yn * sn * (1.0 - sn), dmix * yh * sh * (1.0 - sh)

    y_hg, mix = _matmul(o_hg, w_hg, name="mm_hg_out", epilogue=mix_gates,
                        tiles=(*gates, (y_na, 0)), out_dtypes=(F32, BF16))
    t1 = _matmul(mix, w_o, name="mm_o")
    h1, mlp_in = _residual_norm(h0, t1, g_mlp, name="resid_norm_mlp")
    u, act = _matmul(mlp_in, w_up, name="mm_up", out_dtypes=(BF16, BF16),
                     epilogue=lambda acc: (acc, jnp.square(jnp.maximum(acc, 0.0))))
    t2 = _matmul(act, w_down, name="mm_down")
    dh2, loss, dg_fin = _final_loss(h1, t2, g_fin, tgt_p, n_tok=n_tok, name="final_loss")

    (du,) = _matmul(dh2, w_down, tb=True, name="mm_down_dx", tiles=((u, 0),), out_dtypes=(BF16,),
                    epilogue=lambda acc, uv: (acc * 2.0 * jnp.maximum(uv, 0.0),))
    dw_down = _matmul(act, dh2, ta=True, name="mm_down_dw")
    dm = _matmul(du, w_up, tb=True, name="mm_up_dx")
    dw_up = _matmul(mlp_in, du, ta=True, name="mm_up_dw")
    dh1, dg_mlp = _rmsnorm_bwd_add(h1, g_mlp, dm, dh2, name="norm_mlp_bwd")
    dy_na, dy_hg, dgn, dgh = _matmul(dh1, w_o, tb=True, name="mm_o_dx", epilogue=mix_gates_bwd,
                                     tiles=(*gates, (y_na, 0), (y_hg, 0)), out_dtypes=(BF16,) * 4)
    dw_o = _matmul(mix, dh1, ta=True, name="mm_o_dw")
    do_na = _matmul(dy_na, w_na, tb=True, name="mm_na_out_dx")
    dw_na = _matmul(o_na, dy_na, ta=True, name="mm_na_out_dw")
    do_hg = _matmul(dy_hg, w_hg, tb=True, name="mm_hg_out_dx")
    dw_hg = _matmul(o_hg, dy_hg, ta=True, name="mm_hg_out_dw")
    token = early_grads([dw_na, dw_hg, dw_o, dw_up, dw_down]) if early_grads else None
    if token is not None:
        hg_gain = hg_gain + token[0:1, 0:1]
    d_o, dg_hg, d_gain = _hg_out_bwd(o_f, o_b, proj, hg_gain, do_hg, col_g=col_g, name="hg_out_bwd")
    dq_f, dz_f, dv_f, dlb_f = _hg_scan_bwd(proj, lb_f, st_f, d_o, reverse=False, col_z=col_zf,
                                           name="hg_scan_f_bwd", **scan_kw)
    token = mid_grads(dq_f) if mid_grads else None
    lb_b_late = lb_b if token is None else lb_b + token[0:1, 0:1]
    dq_b, dz_b, dv_b, dlb_b = _hg_scan_bwd(proj, lb_b_late, st_b, d_o, reverse=True, col_z=col_zb,
                                           name="hg_scan_b_bwd", **scan_kw)
    dq_na, dk_na, dv_na, dtb = _na_bwd(proj, tb, o_na, lse, do_na, n_tok=n_tok, nh=nh, name="na_bwd")
    dproj = _assemble_dproj(dq_na, dk_na, dv_na, dq_f, dq_b, dz_f, dz_b, dv_f, dv_b, dg_hg, dgn,
                            dgh, name="assemble_dproj")
    dw_in = _matmul(a, dproj, ta=True, name="mm_in_dw")
    token = late_grad(dw_in) if late_grad else None
    da = _matmul(dproj, w_in, tb=True, name="mm_in_dx", after=token)
    dh0, dg_mix = _rmsnorm_bwd_add(h0, g_mix, da, dh1, name="norm_mix_bwd")
    d_rpb = _matmul(dtb.reshape(nh * (2 * NA_WIN_H - 1), GRID_W * GRID_W), oh, tb=True,
                    tm=nh * (2 * NA_WIN_H - 1), tn=LANES, tk=1024, precision=HIGHEST,
                    name="rpb_reduce")
    d_lb = jnp.concatenate([dlb_f.reshape(1, hgw), dlb_b.reshape(1, hgw)], axis=0)
    return (loss, dh0[N_META:l_real], dh0[:N_META], dw_in, dw_na, dw_hg, dw_o, dw_up, dw_down,
            dg_mix, dg_mlp, dg_fin, d_gain, d_rpb, d_lb)


N_CHIPS = 4
N_DEV = 8
ANY = pl.BlockSpec(memory_space=pl.ANY)


def _place():
    x, y, c = lax.axis_index("x"), lax.axis_index("y"), lax.axis_index("c")
    others = []
    for j in (1, 2, 3):
        tx = (1 - x) if (j >> 1) else x
        ty = (1 - y) if (j & 1) else y
        others.append((tx, ty))
    return x, y, c, others


def _piece(ref, axis, k, half, rh, cs):
    if axis == 1:
        return ref.at[pl.ds(pl.multiple_of(half * rh, 16), rh), pl.ds(pl.multiple_of(k * cs, LANES), cs)]
    return ref.at[pl.ds(pl.multiple_of(k * 2 * rh + half * rh, 16), rh), :]


def _cast_into_full(shard, axis, place, *, name):
    r, cs = shard.shape
    full = (r, cs * N_CHIPS) if axis == 1 else (r * N_CHIPS, cs)
    tr = next(t for t in (256, 128, 64, 32, 16) if r % t == 0)
    nt = r // tr

    def body(p_ref, s_ref, o_ref):
        o_ref[...] = s_ref[...].astype(BF16)

    if axis == 1:
        omap = lambda i, p_ref: (i, p_ref[0])
    else:
        omap = lambda i, p_ref: (p_ref[0] * nt + i, 0)
    return pl.pallas_call(
        body,
        grid_spec=pltpu.PrefetchScalarGridSpec(
            num_scalar_prefetch=1, grid=(nt,),
            in_specs=[pl.BlockSpec((tr, cs), lambda i, p_ref: (i, 0))],
            out_specs=pl.BlockSpec((tr, cs), omap)),
        out_shape=jax.ShapeDtypeStruct(full, BF16),
        compiler_params=_cp("parallel"), name=name)(place, shard)


HBM_SPEC = pl.BlockSpec(memory_space=pltpu.HBM)
SEM_SPEC = pl.BlockSpec(memory_space=pltpu.SEMAPHORE)
SPLIT_COPY = pltpu.CompilerParams(has_side_effects=pltpu.SideEffectType.DATAFLOW_SIDE_EFFECTING)
TOKEN = jax.ShapeDtypeStruct((8, LANES), F32)


def _geo(fulls, axes):
    out = []
    for f, ax in zip(fulls, axes):
        r, cs = (f.shape[0], f.shape[1] // N_CHIPS) if ax == 1 else (f.shape[0] // N_CHIPS, f.shape[1])
        out.append((ax, r // 2, cs))
    return out


def _gather_copies(refs, geo, send_sems, recv_sems):
    x, y, c, others = _place()
    chip = 2 * x + y
    cps = []
    for i, (ax, rh, cs) in enumerate(geo):
        mine = _piece(refs[i], ax, chip, c, rh, cs)
        for j, (tx, ty) in enumerate(others):
            cps.append(pltpu.make_async_remote_copy(
                src_ref=mine, dst_ref=mine, send_sem=send_sems.at[3 * i + j],
                recv_sem=recv_sems.at[3 * i + j], device_id=(tx, ty, c), device_id_type=MESH))
    return cps


def _allgather_start(fulls, axes, after, *, name):
    n = len(fulls)
    geo = _geo(fulls, axes)

    def body(*refs):
        w_refs = refs[:n]
        send_sems, recv_sems = refs[n + 1], refs[n + 2]
        token = refs[2 * n + 3]
        for cp in _gather_copies(w_refs, geo, send_sems, recv_sems):
            cp.start()
        token[...] = jnp.zeros_like(token)

    out = pl.pallas_call(
        body, name=name,
        out_shape=(pltpu.SemaphoreType.DMA((3 * n,)), pltpu.SemaphoreType.DMA((3 * n,)),
                   *[pltpu.HBM(f.shape, f.dtype) for f in fulls], TOKEN),
        in_specs=[HBM_SPEC] * n + [ANY],
        out_specs=(SEM_SPEC, SEM_SPEC, *[HBM_SPEC] * n, pl.BlockSpec(memory_space=pltpu.VMEM)),
        input_output_aliases={i: 2 + i for i in range(n)},
        compiler_params=SPLIT_COPY,
    )(*[pltpu.with_memory_space_constraint(f, pltpu.HBM) for f in fulls], after)
    return out[0], out[1], list(out[2:2 + n]), out[2 + n]


def _allgather_wait(send_sems, recv_sems, fulls, axes, after, *, name):
    n = len(fulls)
    geo = _geo(fulls, axes)

    def body(*refs):
        w_refs = refs[:n]
        for cp in _gather_copies(w_refs, geo, refs[n], refs[n + 1]):
            cp.wait_send()
            cp.wait_recv()

    return list(pl.pallas_call(
        body, name=name,
        out_shape=[pltpu.HBM(f.shape, f.dtype) for f in fulls],
        in_specs=[HBM_SPEC] * n + [SEM_SPEC, SEM_SPEC, ANY],
        out_specs=[HBM_SPEC] * n,
        input_output_aliases={i: i for i in range(n)},
        compiler_params=SPLIT_COPY,
    )(*fulls, send_sems, recv_sems, after))


def _allgather_forward(fulls, axes, *, name):
    n = len(fulls)
    geo = _geo(fulls, axes)

    def body(*refs):
        o_refs = refs[n:2 * n]
        send_sems, recv_sems = refs[2 * n:]
        x, y, c, others = _place()

        def rcopy(i, j, half, to):
            ax, rh, cs = geo[i]
            ref = _piece(o_refs[i], ax, 2 * others[j][0] + others[j][1], half, rh, cs)
            return pltpu.make_async_remote_copy(
                src_ref=ref, dst_ref=ref, send_sem=send_sems.at[3 * i + j],
                recv_sem=recv_sems.at[3 * i + j], device_id=to, device_id_type=MESH)

        cps = [rcopy(i, j, c, (x, y, 1 - c)) for i in range(n) for j in range(3)]
        for cp in cps:
            cp.start()
        for i in range(n):
            for j in range(3):
                rcopy(i, j, 1 - c, (x, y, c)).wait_recv()
        for cp in cps:
            cp.wait_send()

    return list(pl.pallas_call(
        body, in_specs=[ANY] * n, out_specs=[ANY] * n,
        out_shape=[jax.ShapeDtypeStruct(f.shape, f.dtype) for f in fulls],
        input_output_aliases={i: i for i in range(n)},
        scratch_shapes=[pltpu.SemaphoreType.DMA((3 * n,)), pltpu.SemaphoreType.DMA((3 * n,))],
        name=name)(*fulls))


def _forward_copies(refs, geo, send_sems, recv_sems):
    x, y, c, others = _place()
    cps = []
    for i, (ax, rh, cs) in enumerate(geo):
        for j, (tx, ty) in enumerate(others):
            ref = _piece(refs[i], ax, 2 * tx + ty, c, rh, cs)
            cps.append(pltpu.make_async_remote_copy(
                src_ref=ref, dst_ref=ref, send_sem=send_sems.at[3 * i + j],
                recv_sem=recv_sems.at[3 * i + j], device_id=(x, y, 1 - c), device_id_type=MESH))
    return cps


def _allgather_forward_start(fulls, axes, *, name):
    n = len(fulls)
    geo = _geo(fulls, axes)

    def body(*refs):
        token = refs[2 * n + 2]
        for cp in _forward_copies(refs[:n], geo, refs[n], refs[n + 1]):
            cp.start()
        token[...] = jnp.zeros_like(token)

    out = pl.pallas_call(
        body, name=name,
        out_shape=(pltpu.SemaphoreType.DMA((3 * n,)), pltpu.SemaphoreType.DMA((3 * n,)),
                   *[pltpu.HBM(f.shape, f.dtype) for f in fulls], TOKEN),
        in_specs=[HBM_SPEC] * n,
        out_specs=(SEM_SPEC, SEM_SPEC, *[HBM_SPEC] * n, pl.BlockSpec(memory_space=pltpu.VMEM)),
        input_output_aliases={i: 2 + i for i in range(n)},
        compiler_params=SPLIT_COPY,
    )(*fulls)
    return out[0], out[1], list(out[2:2 + n]), out[2 + n]


def _allgather_forward_wait(send_sems, recv_sems, fulls, axes, after, *, name):
    n = len(fulls)
    geo = _geo(fulls, axes)

    def body(*refs):
        for cp in _forward_copies(refs[:n], geo, refs[n], refs[n + 1]):
            cp.wait_send()
            cp.wait_recv()

    return list(pl.pallas_call(
        body, name=name,
        out_shape=[pltpu.HBM(f.shape, f.dtype) for f in fulls],
        in_specs=[HBM_SPEC] * n + [SEM_SPEC, SEM_SPEC, ANY],
        out_specs=[HBM_SPEC] * n,
        input_output_aliases={i: i for i in range(n)},
        compiler_params=SPLIT_COPY,
    )(*fulls, send_sems, recv_sems, after))


def _chip_copies(blk_ref, land_ref, send_sems, recv_sems):
    x, y, c, others = _place()
    return [pltpu.make_async_remote_copy(
        src_ref=blk_ref, dst_ref=land_ref.at[2 * x + y], send_sem=send_sems.at[j],
        recv_sem=recv_sems.at[j], device_id=(tx, ty, c), device_id_type=MESH)
        for j, (tx, ty) in enumerate(others)]


def _chip_exchange_start(blk, *, name):
    land = pltpu.with_memory_space_constraint(lax.empty((N_CHIPS, *blk.shape), blk.dtype), pltpu.HBM)

    def body(blk_ref, land_ref, send_sems, recv_sems, blk_out, land_out, token):
        for cp in _chip_copies(blk_ref, land_ref, send_sems, recv_sems):
            cp.start()
        token[...] = jnp.zeros_like(token)

    return pl.pallas_call(
        body, name=name,
        out_shape=(pltpu.SemaphoreType.DMA((3,)), pltpu.SemaphoreType.DMA((3,)),
                   pltpu.HBM(blk.shape, blk.dtype), pltpu.HBM(land.shape, land.dtype), TOKEN),
        in_specs=[HBM_SPEC] * 2,
        out_specs=(SEM_SPEC, SEM_SPEC, HBM_SPEC, HBM_SPEC, pl.BlockSpec(memory_space=pltpu.VMEM)),
        input_output_aliases={0: 2, 1: 3},
        compiler_params=SPLIT_COPY,
    )(pltpu.with_memory_space_constraint(blk, pltpu.HBM), land)


def _chip_exchange_wait(send_sems, recv_sems, blk, land, after, *, name):
    def body(blk_ref, land_ref, send_sems, recv_sems, after_ref, blk_out, land_out):
        for cp in _chip_copies(blk_ref, land_ref, send_sems, recv_sems):
            cp.wait_send()
            cp.wait_recv()

    return pl.pallas_call(
        body, name=name,
        out_shape=[pltpu.HBM(blk.shape, blk.dtype), pltpu.HBM(land.shape, land.dtype)],
        in_specs=[HBM_SPEC] * 2 + [SEM_SPEC, SEM_SPEC, ANY],
        out_specs=[HBM_SPEC] * 2,
        input_output_aliases={0: 0, 1: 1},
        compiler_params=SPLIT_COPY,
    )(blk, land, send_sems, recv_sems, after)[1]


def _scatter_geo(parts, axes):
    out = []
    for p, ax in zip(parts, axes):
        _, rh, cols = p.shape
        out.append((ax, rh, cols // N_CHIPS if ax == 1 else cols))
    return out


def _scatter_copies(p_refs, q_refs, geo, send_sems, recv_sems):
    x, y, c, others = _place()
    chip = 2 * x + y
    cps = []
    for i, (ax, rh, cw) in enumerate(geo):
        for j, (tx, ty) in enumerate(others):
            k = 2 * tx + ty
            src = (p_refs[i].at[0, :, pl.ds(pl.multiple_of(k * cw, LANES), cw)] if ax == 1
                   else p_refs[i].at[k])
            cps.append(pltpu.make_async_remote_copy(
                src_ref=src, dst_ref=q_refs[i].at[chip], send_sem=send_sems.at[3 * i + j],
                recv_sem=recv_sems.at[3 * i + j], device_id=(tx, ty, c), device_id_type=MESH))
    return cps


def _scatter_start(parts, axes, *, name):
    n = len(parts)
    geo = _scatter_geo(parts, axes)
    slots = [pltpu.HBM((N_CHIPS, rh, cw), p.dtype) for p, (_, rh, cw) in zip(parts, geo)]

    def body(*refs):
        p_refs, q_refs = refs[:n], refs[n:2 * n]
        send_sems, recv_sems = refs[2 * n], refs[2 * n + 1]
        token = refs[4 * n + 2]
        for cp in _scatter_copies(p_refs, q_refs, geo, send_sems, recv_sems):
            cp.start()
        token[...] = jnp.zeros_like(token)

    land = [pltpu.with_memory_space_constraint(lax.empty(s.inner_aval.shape, s.inner_aval.dtype), pltpu.HBM)
            for s in slots]
    out = pl.pallas_call(
        body, name=name,
        out_shape=(pltpu.SemaphoreType.DMA((3 * n,)), pltpu.SemaphoreType.DMA((3 * n,)),
                   *[pltpu.HBM(p.shape, p.dtype) for p in parts], *slots, TOKEN),
        in_specs=[HBM_SPEC] * (2 * n),
        out_specs=(SEM_SPEC, SEM_SPEC, *[HBM_SPEC] * (2 * n), pl.BlockSpec(memory_space=pltpu.VMEM)),
        input_output_aliases={i: 2 + i for i in range(2 * n)},
        compiler_params=SPLIT_COPY,
    )(*[pltpu.with_memory_space_constraint(p, pltpu.HBM) for p in parts], *land)
    return out[0], out[1], list(out[2:2 + n]), list(out[2 + n:2 + 2 * n]), out[2 + 2 * n]


def _scatter_wait(send_sems, recv_sems, parts, slots, axes, after, *, name):
    n = len(parts)
    geo = _scatter_geo(parts, axes)

    def body(*refs):
        p_refs, q_refs = refs[:n], refs[n:2 * n]
        for cp in _scatter_copies(p_refs, q_refs, geo, refs[2 * n], refs[2 * n + 1]):
            cp.wait_send()
            cp.wait_recv()

    out = pl.pallas_call(
        body, name=name,
        out_shape=[pltpu.HBM(a.shape, a.dtype) for a in (*parts, *slots)],
        in_specs=[HBM_SPEC] * (2 * n) + [SEM_SPEC, SEM_SPEC, ANY],
        out_specs=[HBM_SPEC] * (2 * n),
        input_output_aliases={i: i for i in range(2 * n)},
        compiler_params=SPLIT_COPY,
    )(*parts, *slots, send_sems, recv_sems, after)
    return list(out[:n]), list(out[n:])


def _sibling_swap(grads, *, name):
    n = len(grads)
    out_shape = [jax.ShapeDtypeStruct((g.shape[0], g.shape[1] // 2, g.shape[2]), g.dtype)
                 for g in grads]

    def body(*refs):
        g_refs, o_refs = refs[:n], refs[n:2 * n]
        send_sems, recv_sems = refs[2 * n:]
        x, y, c, _ = _place()
        cps = []
        for i in range(n):
            rh = grads[i].shape[1] // 2
            src = g_refs[i].at[:, pl.ds(pl.multiple_of((1 - c) * rh, 16), rh), :]
            cp = pltpu.make_async_remote_copy(
                src_ref=src, dst_ref=o_refs[i], send_sem=send_sems.at[i], recv_sem=recv_sems.at[i],
                device_id=(x, y, 1 - c), device_id_type=MESH)
            cp.start()
            cps.append(cp)
        for cp in cps:
            cp.wait()

    return pl.pallas_call(
        body, in_specs=[ANY] * n, out_specs=[ANY] * n, out_shape=out_shape,
        scratch_shapes=[pltpu.SemaphoreType.DMA((n,)), pltpu.SemaphoreType.DMA((n,))],
        name=name)(*grads)


def _swap_copies(g_refs, r_refs, shapes, send_sems, recv_sems):
    x, y, c, _ = _place()
    cps = []
    for i, shape in enumerate(shapes):
        rh = shape[1] // 2
        src = g_refs[i].at[:, pl.ds(pl.multiple_of((1 - c) * rh, 16), rh), :]
        cps.append(pltpu.make_async_remote_copy(
            src_ref=src, dst_ref=r_refs[i], send_sem=send_sems.at[i], recv_sem=recv_sems.at[i],
            device_id=(x, y, 1 - c), device_id_type=MESH))
    return cps


def _sibling_swap_start(grads, *, name):
    n = len(grads)
    shapes = [g.shape for g in grads]
    lands = [pltpu.HBM((s[0], s[1] // 2, s[2]), g.dtype) for s, g in zip(shapes, grads)]

    def body(*refs):
        g_refs, r_refs = refs[:n], refs[n:2 * n]
        token = refs[4 * n + 2]
        for cp in _swap_copies(g_refs, r_refs, shapes, refs[2 * n], refs[2 * n + 1]):
            cp.start()
        token[...] = jnp.zeros_like(token)

    land = [pltpu.with_memory_space_constraint(lax.empty(s.inner_aval.shape, s.inner_aval.dtype), pltpu.HBM)
            for s in lands]
    out = pl.pallas_call(
        body, name=name,
        out_shape=(pltpu.SemaphoreType.DMA((n,)), pltpu.SemaphoreType.DMA((n,)),
                   *[pltpu.HBM(g.shape, g.dtype) for g in grads], *lands, TOKEN),
        in_specs=[HBM_SPEC] * (2 * n),
        out_specs=(SEM_SPEC, SEM_SPEC, *[HBM_SPEC] * (2 * n), pl.BlockSpec(memory_space=pltpu.VMEM)),
        input_output_aliases={i: 2 + i for i in range(2 * n)},
        compiler_params=SPLIT_COPY,
    )(*[pltpu.with_memory_space_constraint(g, pltpu.HBM) for g in grads], *land)
    return out[0], out[1], list(out[2:2 + n]), list(out[2 + n:2 + 2 * n]), out[2 + 2 * n]


def _sibling_swap_wait(send_sems, recv_sems, grads, lands, after, *, name):
    n = len(grads)
    shapes = [g.shape for g in grads]

    def body(*refs):
        g_refs, r_refs = refs[:n], refs[n:2 * n]
        for cp in _swap_copies(g_refs, r_refs, shapes, refs[2 * n], refs[2 * n + 1]):
            cp.wait_send()
            cp.wait_recv()

    out = pl.pallas_call(
        body, name=name,
        out_shape=[pltpu.HBM(a.shape, a.dtype) for a in (*grads, *lands)],
        in_specs=[HBM_SPEC] * (2 * n) + [SEM_SPEC, SEM_SPEC, ANY],
        out_specs=[HBM_SPEC] * (2 * n),
        input_output_aliases={i: i for i in range(2 * n)},
        compiler_params=SPLIT_COPY,
    )(*grads, *lands, send_sems, recv_sems, after)
    return list(out[:n]), list(out[n:])


def _pair_add(g3, rx, c_arr, *, out_dtype, name):
    nb, rows, cols = g3.shape
    rh = rows // 2
    tr = next(t for t in (128, 64, 32, 16) if rh % t == 0)
    nt = rh // tr

    def body(c_ref, g_ref, r_ref, o_ref):
        o_ref[...] = (g_ref[...] + r_ref[...]).astype(out_dtype)

    return pl.pallas_call(
        body,
        grid_spec=pltpu.PrefetchScalarGridSpec(
            num_scalar_prefetch=1, grid=(nb, nt),
            in_specs=[pl.BlockSpec((None, tr, cols), lambda b, i, c_ref: (b, c_ref[0] * nt + i, 0)),
                      pl.BlockSpec((None, tr, cols), lambda b, i, c_ref: (b, i, 0))],
            out_specs=pl.BlockSpec((None, tr, cols), lambda b, i, c_ref: (b, i, 0))),
        out_shape=jax.ShapeDtypeStruct((nb, rh, cols), out_dtype),
        compiler_params=_cp("parallel", "parallel"), name=name)(c_arr, g3, rx)


def _sum_slots(q, *, name):
    ns, rows, cols = q.shape
    tr = next(t for t in (128, 64, 32, 16, 8) if rows % t == 0)

    def body(q_ref, o_ref):
        acc = q_ref[0].astype(F32)
        for k in range(1, ns):
            acc = acc + q_ref[k].astype(F32)
        o_ref[...] = acc

    return pl.pallas_call(
        body, grid=(rows // tr,),
        in_specs=[pl.BlockSpec((ns, tr, cols), lambda i: (0, i, 0))],
        out_specs=_rspec(tr, cols),
        out_shape=jax.ShapeDtypeStruct((rows, cols), F32),
        compiler_params=_cp("parallel"), name=name)(q)


def _sum_chips(q, p, place, axis, *, name):
    _, rh, cw = q.shape
    tr = next(t for t in (128, 64, 32, 16) if rh % t == 0)
    nt = rh // tr

    def body(p_ref, *refs):
        q_refs, own_ref, o_ref = refs[:N_CHIPS], refs[N_CHIPS], refs[N_CHIPS + 1]
        chip = p_ref[0]
        acc = jnp.where(chip == 0, own_ref[...], q_refs[0][...]).astype(F32)
        for k in range(1, N_CHIPS):
            acc = acc + jnp.where(chip == k, own_ref[...], q_refs[k][...]).astype(F32)
        o_ref[...] = acc

    def slot_spec(k):
        return pl.BlockSpec((None, tr, cw),
                            lambda i, p_ref: (jnp.where(p_ref[0] == k, (k + 1) % N_CHIPS, k), i, 0))

    if axis == 1:
        own_spec = pl.BlockSpec((None, tr, cw), lambda i, p_ref: (0, i, p_ref[0]))
    else:
        own_spec = pl.BlockSpec((None, tr, cw), lambda i, p_ref: (p_ref[0], i, 0))
    return pl.pallas_call(
        body,
        grid_spec=pltpu.PrefetchScalarGridSpec(
            num_scalar_prefetch=1, grid=(nt,),
            in_specs=[slot_spec(k) for k in range(N_CHIPS)] + [own_spec],
            out_specs=pl.BlockSpec((tr, cw), lambda i, p_ref: (p_ref[1] * nt + i, 0))),
        out_shape=jax.ShapeDtypeStruct((2 * rh, cw), F32),
        compiler_params=_cp("parallel"), name=name)(place, *([q] * N_CHIPS), p)


def _sibling_share(shards, *, name):
    n = len(shards)

    def body(*refs):
        o_refs = refs[n:2 * n]
        send_sems, recv_sems = refs[2 * n:]
        x, y, c, _ = _place()
        cps = []
        for i in range(n):
            rh = shards[i].shape[0] // 2
            mine = o_refs[i].at[pl.ds(pl.multiple_of(c * rh, 8), rh), :]
            cp = pltpu.make_async_remote_copy(
                src_ref=mine, dst_ref=mine, send_sem=send_sems.at[i], recv_sem=recv_sems.at[i],
                device_id=(x, y, 1 - c), device_id_type=MESH)
            cp.start()
            cps.append(cp)
        for i in range(n):
            rh = shards[i].shape[0] // 2
            theirs = o_refs[i].at[pl.ds(pl.multiple_of((1 - c) * rh, 8), rh), :]
            pltpu.make_async_remote_copy(
                src_ref=theirs, dst_ref=theirs, send_sem=send_sems.at[i], recv_sem=recv_sems.at[i],
                device_id=(x, y, c), device_id_type=MESH).wait_recv()
        for cp in cps:
            cp.wait_send()

    return pl.pallas_call(
        body, in_specs=[ANY] * n, out_specs=[ANY] * n,
        out_shape=[jax.ShapeDtypeStruct(h.shape, h.dtype) for h in shards],
        input_output_aliases={i: i for i in range(n)},
        scratch_shapes=[pltpu.SemaphoreType.DMA((n,)), pltpu.SemaphoreType.DMA((n,))],
        name=name)(*shards)


def _gather_all(blk, *, name, after=None):
    rows, cols = blk.shape
    extra = [] if after is None else [after]

    def body(x_ref, *refs):
        out_ref, send_sems, recv_sems, local_sem = refs[len(extra):]
        x, y, c = lax.axis_index("x"), lax.axis_index("y"), lax.axis_index("c")
        me = 4 * x + 2 * y + c
        mine = pltpu.make_async_copy(x_ref, out_ref.at[me], local_sem)
        mine.start()
        cps = []
        for k in range(1, N_DEV):
            tx = (1 - x) if (k >> 2) & 1 else x
            ty = (1 - y) if (k >> 1) & 1 else y
            tc = (1 - c) if k & 1 else c
            cp = pltpu.make_async_remote_copy(
                src_ref=x_ref, dst_ref=out_ref.at[me], send_sem=send_sems.at[k - 1],
                recv_sem=recv_sems.at[k - 1], device_id=(tx, ty, tc), device_id_type=MESH)
            cp.start()
            cps.append(cp)
        for k in range(1, N_DEV):
            tx = (1 - x) if (k >> 2) & 1 else x
            ty = (1 - y) if (k >> 1) & 1 else y
            tc = (1 - c) if k & 1 else c
            got = out_ref.at[4 * tx + 2 * ty + tc]
            pltpu.make_async_remote_copy(
                src_ref=got, dst_ref=got, send_sem=send_sems.at[k - 1], recv_sem=recv_sems.at[k - 1],
                device_id=(x, y, c), device_id_type=MESH).wait_recv()
        for cp in cps:
            cp.wait_send()
        mine.wait()

    vm = pl.BlockSpec(memory_space=pltpu.VMEM)
    return pl.pallas_call(
        body, in_specs=[vm] + [ANY] * len(extra), out_specs=vm,
        out_shape=jax.ShapeDtypeStruct((N_DEV, rows, cols), blk.dtype),
        scratch_shapes=[pltpu.SemaphoreType.DMA((N_DEV - 1,)), pltpu.SemaphoreType.DMA((N_DEV - 1,)),
                        pltpu.SemaphoreType.DMA],
        name=name)(blk, *extra)


def _as_rows(a):
    flat = a.reshape(-1)
    n = flat.shape[0]
    rows = -(-n // (8 * LANES)) * 8
    return jnp.pad(flat, (0, rows * LANES - n)).reshape(rows, LANES)


def _from_rows(p, shape):
    n = int(np.prod(shape))
    return p.reshape(-1)[:n].reshape(shape)


WEIGHT_AXES = (1, 1, 1, 0, 1, 0)
WIRE = BF16


def kernel(x, meta_tokens, w_in, w_na_out, w_hg_out, w_o, w_up, w_down, norm_mix, norm_mlp, norm_final, hg_norm, na_rpb, hg_lb_logits, loss_target, m_meta_tokens, m_w_in, m_w_na_out, m_w_hg_out, m_w_o, m_w_up, m_w_down, m_norm_mix, m_norm_mlp, m_norm_final, m_hg_norm, m_na_rpb, m_hg_lb_logits, v_meta_tokens, v_w_in, v_w_na_out, v_w_hg_out, v_w_o, v_w_up, v_w_down, v_norm_mix, v_norm_mlp, v_norm_final, v_hg_norm, v_na_rpb, v_hg_lb_logits):
    xi, yi, ci = lax.axis_index("x"), lax.axis_index("y"), lax.axis_index("c")
    chip = 2 * xi + yi
    d = x.shape[-1]
    dshard = meta_tokens.shape[1]
    hgw = hg_norm.shape[1]
    lbs = hg_lb_logits.shape[2]
    big = [w_in[0], w_na_out[0], w_hg_out[0], w_o[0], w_up[0], w_down[0]]
    big_m = [m_w_in[0], m_w_na_out[0], m_w_hg_out[0], m_w_o[0], m_w_up[0], m_w_down[0]]
    big_v = [v_w_in[0], v_w_na_out[0], v_w_hg_out[0], v_w_o[0], v_w_up[0], v_w_down[0]]

    place = jnp.stack([chip, ci]).astype(jnp.int32)
    own_w = [_cast_into_full(w, ax, place, name=f"cast_shard_{i}")
             for i, (w, ax) in enumerate(zip(big, WEIGHT_AXES))]
    in_axes, rest_axes = WEIGHT_AXES[:1], WEIGHT_AXES[1:]
    small_in = jnp.concatenate([_as_rows(meta_tokens), _as_rows(hg_lb_logits)], axis=0)
    sm_send, sm_recv, sm_blk, sm_land, sm_token = _chip_exchange_start(small_in,
                                                                       name="small_params_start")
    in_send, in_recv, in_bufs, in_token = _allgather_start(own_w[:1], in_axes, sm_token,
                                                           name="weight_allgather_in_start")
    ag_send, ag_recv, ag_bufs, ag_token = _allgather_start(own_w[1:], rest_axes, in_token,
                                                           name="weight_allgather_rest_start")
    sm_land = _chip_exchange_wait(sm_send, sm_recv, sm_blk, sm_land, ag_token,
                                  name="small_params_wait")
    small_all = lax.dynamic_update_slice(sm_land, small_in[None], (chip, 0, 0))
    forward = {}

    def first_weight(after):
        got = _allgather_wait(in_send, in_recv, in_bufs, in_axes, after,
                              name="weight_allgather_in_wait")
        return _allgather_forward(got, in_axes, name="weight_allgather_in_forward")[0]

    def rest_landed(after):
        got = _allgather_wait(ag_send, ag_recv, ag_bufs, rest_axes, after,
                              name="weight_allgather_rest_wait")
        send, recv, bufs, token = _allgather_forward_start(
            got, rest_axes, name="weight_allgather_rest_forward_start")
        forward["rest"] = (send, recv, bufs)
        return token

    def rest_weights(after):
        return _allgather_forward_wait(*forward["rest"], rest_axes, after,
                                       name="weight_allgather_rest_forward_wait")

    n_meta_rows = N_META * dshard // LANES
    meta_full = (small_all[:, :n_meta_rows].reshape(N_CHIPS, N_META, dshard)
                 .transpose(1, 0, 2).reshape(N_META, d))
    lbl_full = (small_all[:, n_meta_rows:].reshape(N_CHIPS, -1)[:, :4 * lbs]
                .reshape(N_CHIPS, 2, 2, lbs).transpose(1, 2, 0, 3).reshape(2, 2, N_CHIPS * lbs))
    lb = jax.nn.softmax(lbl_full, axis=1)[:, 0]

    c_arr = ci.reshape(1).astype(jnp.int32)

    def by_chip(dws, axes):
        return [g.reshape(1, *g.shape) if ax == 1
                else g.reshape(N_CHIPS, g.shape[0] // N_CHIPS, g.shape[1]) for g, ax in zip(dws, axes)]

    flying = {}

    def scatter(tag, axes, g3, rx):
        parts = [_pair_add(g, r, c_arr, out_dtype=WIRE, name=f"grad_pair_add_{tag}_{i}")
                 for i, (g, r) in enumerate(zip(g3, rx))]
        send, recv, parts, slots, token = _scatter_start(parts, axes,
                                                         name=f"grad_scatter_{tag}_start")
        flying[tag] = (send, recv, parts, slots)
        return token

    def swap_rest(dws):
        send, recv, g3, lands, token = _sibling_swap_start(by_chip(dws, rest_axes),
                                                           name="grad_sibling_swap_rest_start")
        flying["swap"] = (send, recv, g3, lands)
        return token

    def scatter_rest(after):
        g3, rx = _sibling_swap_wait(*flying["swap"], after, name="grad_sibling_swap_rest_wait")
        return scatter("rest", rest_axes, g3, rx)

    def scatter_in(dw_in):
        g3 = by_chip([dw_in], in_axes)
        return scatter("in", in_axes, g3, _sibling_swap(g3, name="grad_sibling_swap_in"))

    def landed(tag, axes, after):
        parts, slots = _scatter_wait(*flying[tag], axes, after, name=f"grad_scatter_{tag}_wait")
        return [_sum_chips(q, p, place, ax, name=f"grad_sum_chips_{tag}_{i}")
                for i, (q, p, ax) in enumerate(zip(slots, parts, axes))]

    (loss, dx, dmeta, *_, dg_mix, dg_mlp, dg_fin, d_gain, d_rpb, d_lb) = _local_step(
        x[0], loss_target[0], meta_full, first_weight, rest_weights, norm_mix, norm_mlp,
        norm_final.reshape(1, d), hg_norm, na_rpb[0], lb, swap_rest, scatter_rest, scatter_in,
        rest_landed)

    halves_rest = landed("rest", rest_axes, dx)
    halves_in = landed("in", in_axes, halves_rest[-1])
    g_big = _sibling_share(halves_in + halves_rest, name="grad_sibling_share")

    d_rpb_c = d_rpb[:, :2 * NA_WIN_W - 1]
    small_g = [dmeta, dg_mix, dg_mlp, dg_fin, d_gain, d_rpb_c, d_lb, loss]
    packed = jnp.concatenate([_as_rows(a) for a in small_g], axis=0)
    total = _sum_slots(_gather_all(packed, after=halves_in[0], name="gather_small_grads"),
                       name="sum_small_grads")
    offs = np.cumsum([0] + [_as_rows(a).shape[0] for a in small_g])
    take = lambda i, shape: _from_rows(total[offs[i]:offs[i + 1]], shape)
    g_meta_full = take(0, (N_META, d))
    g_norm_mix, g_norm_mlp = take(1, (1, d)), take(2, (1, d))
    g_norm_final = take(3, (d,))
    g_hg_norm = take(4, (1, hgw))
    g_rpb = take(5, na_rpb.shape)
    g_lb = take(6, (2, hgw))
    loss_total = take(7, (1, LANES))[0, 0]
    g_meta = lax.dynamic_slice_in_dim(g_meta_full, chip * dshard, dshard, axis=1)
    dl0 = lb * (1.0 - lb) * g_lb
    g_lbl_full = jnp.stack([dl0, -dl0], axis=1)
    g_lbl = lax.dynamic_slice_in_dim(g_lbl_full, chip * lbs, lbs, axis=2)

    big_out = [_adamw(w, g, m, v, name=f"adamw_{i}")
               for i, (w, g, m, v) in enumerate(zip(big, g_big, big_m, big_v))]
    small_w = [meta_tokens, norm_mix, norm_mlp, norm_final, hg_norm, na_rpb, hg_lb_logits]
    small_gr = [g_meta, g_norm_mix, g_norm_mlp, g_norm_final, g_hg_norm, g_rpb, g_lbl]
    small_m = [m_meta_tokens, m_norm_mix, m_norm_mlp, m_norm_final, m_hg_norm, m_na_rpb, m_hg_lb_logits]
    small_v = [v_meta_tokens, v_norm_mix, v_norm_mlp, v_norm_final, v_hg_norm, v_na_rpb, v_hg_lb_logits]
    pk = lambda lst: jnp.concatenate([_as_rows(a) for a in lst], axis=0)
    sd, sm, sv = _adamw(pk(small_w), pk(small_gr), pk(small_m), pk(small_v), name="adamw_small")
    soffs = np.cumsum([0] + [_as_rows(a).shape[0] for a in small_w])
    unpk = lambda p: [_from_rows(p[soffs[i]:soffs[i + 1]], small_w[i].shape) for i in range(len(small_w))]
    sd, sm, sv = unpk(sd), unpk(sm), unpk(sv)

    def order(bigs, smalls):
        return [smalls[0]] + [b.reshape(1, *b.shape) for b in bigs] + smalls[1:]

    grads = order(g_big, small_gr)
    deltas = order([o[0] for o in big_out], sd)
    new_m = order([o[1] for o in big_out], sm)
    new_v = order([o[2] for o in big_out], sv)
    return (loss_total, dx.reshape(1, *dx.shape), *grads, *deltas, *new_m, *new_v)
```

```python
import functools

import numpy as np
import jax
import jax.numpy as jnp
from jax import lax
from jax.experimental import pallas as pl
from jax.experimental.pallas import tpu as pltpu

F32 = jnp.float32
BF16 = jnp.bfloat16
HIGHEST = lax.Precision.HIGHEST

GRID_W = 64
N_META = 16
EPS = 1e-6
NA_HEAD_DIM = 64
NA_WIN_H = 8
NA_WIN_W = 16
HG_DK = 128
HG_CHUNK = 16
LANES = 128
ROW_ALIGN = 128
VMEM_LIMIT = 48 * 1024 * 1024

ADAM_LR = 0.001
ADAM_B1 = 0.9
ADAM_B2 = 0.999
ADAM_EPS = 1e-08
ADAM_WD = 0.01
ADAM_STEP = 10

MESH = pl.DeviceIdType.MESH


def _cp(*sem):
    return pltpu.CompilerParams(dimension_semantics=sem, vmem_limit_bytes=VMEM_LIMIT)


def _sigmoid(x):
    return 1.0 / (1.0 + jnp.exp(-x))


def _dot(a, b, dims, precision=None):
    return lax.dot_general(a, b, (dims, ((), ())), preferred_element_type=F32, precision=precision)


def _nn(a, b, **kw):
    return _dot(a, b, ((1,), (0,)), **kw)


def _nt(a, b, **kw):
    return _dot(a, b, ((1,), (1,)), **kw)


def _tn(a, b, **kw):
    return _dot(a, b, ((0,), (0,)), **kw)


def _matmul(a, b, *, ta=False, tb=False, tm=None, tn=None, tk=None, out_dtype=F32, name,
            precision=None, after=None, epilogue=None, tiles=(), out_dtypes=None):
    extra = [] if after is None else [after]
    single = out_dtypes is None
    if single:
        out_dtypes = (out_dtype,)
    n_t, n_o = len(tiles), len(out_dtypes)
    if ta:
        kdim, m = a.shape
    else:
        m, kdim = a.shape
    if tb:
        n, k2 = b.shape
    else:
        k2, n = b.shape
    assert kdim == k2, (a.shape, b.shape, ta, tb)
    if tm is None:
        if ta:
            tm = next(t for t in (1024, 512, 256, 128, m) if m % t == 0)
        else:
            tm = m // 2 if (m // 2) % 16 == 0 and m > 512 else m
    if tn is None:
        wide = (1024,) if not ta and len(tiles) <= 1 else ()
        tn = next(t for t in (*wide, 512, 256, 128, n) if n % t == 0)
    if tk is None:
        tk = kdim if ta else next(t for t in (1024, 512, 256, 128, kdim) if kdim % t == 0)
    assert m % tm == 0 and n % tn == 0 and kdim % tk == 0, (m, n, kdim, tm, tn, tk)
    nk = kdim // tk
    op_dtype = F32 if precision is not None else BF16

    def body(a_ref, b_ref, *refs):
        t_refs = refs[:n_t]
        o_refs = refs[n_t + len(extra):n_t + len(extra) + n_o]
        av = a_ref[...].astype(op_dtype)
        bv = b_ref[...].astype(op_dtype)
        dims = ((0 if ta else 1,), (1 if tb else 0,))
        part = _dot(av, bv, dims, precision=precision)

        def finish(acc):
            outs = (acc,) if epilogue is None else epilogue(acc, *[t[...] for t in t_refs])
            for o_ref, val in zip(o_refs, outs):
                o_ref[...] = val.astype(o_ref.dtype)

        if nk == 1:
            finish(part)
            return
        acc_ref = refs[-1]
        kk = pl.program_id(2)

        @pl.when(kk == 0)
        def _():
            acc_ref[...] = part

        @pl.when((kk > 0) & (kk < nk - 1))
        def _():
            acc_ref[...] += part

        @pl.when(kk == nk - 1)
        def _():
            finish(acc_ref[...] + part)

    a_spec = (pl.BlockSpec((tk, tm), lambda i, j, k: (k, i)) if ta
              else pl.BlockSpec((tm, tk), lambda i, j, k: (i, k)))
    b_spec = (pl.BlockSpec((tn, tk), lambda i, j, k: (j, k)) if tb
              else pl.BlockSpec((tk, tn), lambda i, j, k: (k, j)))
    for _, off in tiles:
        assert off % tn == 0, (off, tn)
    t_specs = [pl.BlockSpec((tm, tn), functools.partial(lambda i, j, k, o: (i, o + j), o=off // tn))
               for _, off in tiles]
    o_spec = pl.BlockSpec((tm, tn), lambda i, j, k: (i, j))
    outs = pl.pallas_call(
        body,
        grid=(m // tm, n // tn, nk),
        in_specs=[a_spec, b_spec] + t_specs + [pl.BlockSpec(memory_space=pl.ANY)] * len(extra),
        out_specs=[o_spec] * n_o,
        out_shape=[jax.ShapeDtypeStruct((m, n), dt) for dt in out_dtypes],
        scratch_shapes=[pltpu.VMEM((tm, tn), F32)] if nk > 1 else [],
        compiler_params=_cp("parallel", "parallel", "arbitrary"),
        name=name,
    )(a, b, *[t for t, _ in tiles], *extra)
    return outs[0] if single else outs


def _rspec(tr, w, cb=0):
    return pl.BlockSpec((tr, w), lambda i: (i, cb))


def _fspec(shape):
    nd = len(shape)
    return pl.BlockSpec(shape, lambda i: (0,) * nd)


ROW_VMEM_BUDGET = 20 * 1024 * 1024
ROW_MIN_STEPS = 4


def _row_tile(lp, row_bytes):
    for k in range(ROW_MIN_STEPS, lp // 16 + 1):
        tr = lp // k
        if lp % k == 0 and tr % 16 == 0 and 2 * tr * row_bytes <= ROW_VMEM_BUDGET:
            return tr
    return lp


def _rmsnorm_fwd(x, g, *, name):
    lp, d = x.shape
    tr = _row_tile(lp, d * (4 + 2))

    def body(x_ref, g_ref, o_ref):
        xv = x_ref[...]
        r = lax.rsqrt(jnp.mean(xv * xv, axis=-1, keepdims=True) + EPS)
        o_ref[...] = (xv * r * g_ref[...]).astype(BF16)

    return pl.pallas_call(
        body, grid=(lp // tr,),
        in_specs=[_rspec(tr, d), _fspec((1, d))],
        out_specs=_rspec(tr, d),
        out_shape=jax.ShapeDtypeStruct((lp, d), BF16),
        compiler_params=_cp("parallel"), name=name)(x, g)


def _residual_norm(h, t, g, *, name):
    lp, d = h.shape
    tr = _row_tile(lp, d * (4 + 4 + 4 + 2))

    def body(h_ref, t_ref, g_ref, h1_ref, m_ref):
        xv = h_ref[...] + t_ref[...]
        h1_ref[...] = xv
        r = lax.rsqrt(jnp.mean(xv * xv, axis=-1, keepdims=True) + EPS)
        m_ref[...] = (xv * r * g_ref[...]).astype(BF16)

    return pl.pallas_call(
        body, grid=(lp // tr,),
        in_specs=[_rspec(tr, d), _rspec(tr, d), _fspec((1, d))],
        out_specs=[_rspec(tr, d), _rspec(tr, d)],
        out_shape=[jax.ShapeDtypeStruct((lp, d), F32), jax.ShapeDtypeStruct((lp, d), BF16)],
        compiler_params=_cp("parallel"), name=name)(h, t, g)


def _rmsnorm_bwd_add(x, g, dy, dres, *, name):
    lp, d = x.shape
    tr = _row_tile(lp, d * 4 * 4)

    def body(x_ref, g_ref, dy_ref, dr_ref, dx_ref, dg_ref):
        @pl.when(pl.program_id(0) == 0)
        def _():
            dg_ref[...] = jnp.zeros_like(dg_ref)

        xv = x_ref[...]
        r = lax.rsqrt(jnp.mean(xv * xv, axis=-1, keepdims=True) + EPS)
        xh = xv * r
        dyv = dy_ref[...]
        dg_ref[...] += jnp.sum(dyv * xh, axis=0, keepdims=True)
        dxh = dyv * g_ref[...]
        dx_ref[...] = dr_ref[...] + r * (dxh - xh * jnp.mean(dxh * xh, axis=-1, keepdims=True))

    return pl.pallas_call(
        body, grid=(lp // tr,),
        in_specs=[_rspec(tr, d), _fspec((1, d)), _rspec(tr, d), _rspec(tr, d)],
        out_specs=[_rspec(tr, d), _fspec((1, d))],
        out_shape=[jax.ShapeDtypeStruct((lp, d), F32), jax.ShapeDtypeStruct((1, d), F32)],
        compiler_params=_cp("arbitrary"), name=name)(x, g, dy, dres)


def _final_loss(h1, t2, g, tgt, *, n_tok, name):
    lp, d = h1.shape
    tr = _row_tile(lp, d * 4 * 4)

    def body(h_ref, t_ref, g_ref, tg_ref, dh_ref, loss_ref, dg_ref):
        i = pl.program_id(0)

        @pl.when(i == 0)
        def _():
            loss_ref[...] = jnp.zeros_like(loss_ref)
            dg_ref[...] = jnp.zeros_like(dg_ref)

        xv = h_ref[...] + t_ref[...]
        r = lax.rsqrt(jnp.mean(xv * xv, axis=-1, keepdims=True) + EPS)
        xh = xv * r
        gv = g_ref[...]
        row = i * tr + lax.broadcasted_iota(jnp.int32, (tr, 1), 0)
        valid = (row >= N_META) & (row < N_META + n_tok)
        err = jnp.where(valid, xh * gv - tg_ref[...], 0.0)
        loss_ref[...] += jnp.sum(0.5 * err * err) / d
        dy = err / d
        dg_ref[...] += jnp.sum(dy * xh, axis=0, keepdims=True)
        dxh = dy * gv
        dh_ref[...] = r * (dxh - xh * jnp.mean(dxh * xh, axis=-1, keepdims=True))

    return pl.pallas_call(
        body, grid=(lp // tr,),
        in_specs=[_rspec(tr, d), _rspec(tr, d), _fspec((1, d)), _rspec(tr, d)],
        out_specs=[_rspec(tr, d), _fspec((1, LANES)), _fspec((1, d))],
        out_shape=[jax.ShapeDtypeStruct((lp, d), F32), jax.ShapeDtypeStruct((1, LANES), F32),
                   jax.ShapeDtypeStruct((1, d), F32)],
        compiler_params=_cp("arbitrary"), name=name)(h1, t2, g, tgt)


def _hg_out(o_f, o_b, proj, gain, *, col_g, name):
    lp, w = o_f.shape
    tr = _row_tile(lp, w * (3 * 4 + 2))
    hh = w // HG_DK

    def body(of_ref, ob_ref, g_ref, gain_ref, y_ref):
        gv = g_ref[...]
        sg = gv * _sigmoid(gv)
        for h in range(hh):
            sl = slice(h * HG_DK, (h + 1) * HG_DK)
            o = of_ref[:, sl] + ob_ref[:, sl]
            r = lax.rsqrt(jnp.mean(o * o, axis=-1, keepdims=True) + EPS)
            y_ref[:, sl] = (o * r * gain_ref[:, sl] * sg[:, sl]).astype(BF16)

    return pl.pallas_call(
        body, grid=(lp // tr,),
        in_specs=[_rspec(tr, w), _rspec(tr, w), _rspec(tr, w, col_g // w), _fspec((1, w))],
        out_specs=_rspec(tr, w),
        out_shape=jax.ShapeDtypeStruct((lp, w), BF16),
        compiler_params=_cp("parallel"), name=name)(o_f, o_b, proj, gain)


def _hg_out_bwd(o_f, o_b, proj, gain, dy, *, col_g, name):
    lp, w = o_f.shape
    tr = _row_tile(lp, w * (5 * 4 + 2))
    hh = w // HG_DK

    def body(of_ref, ob_ref, g_ref, gain_ref, dy_ref, do_ref, dg_ref, dgain_ref):
        @pl.when(pl.program_id(0) == 0)
        def _():
            dgain_ref[...] = jnp.zeros_like(dgain_ref)

        for h in range(hh):
            sl = slice(h * HG_DK, (h + 1) * HG_DK)
            gv = g_ref[:, sl]
            s = _sigmoid(gv)
            sg = gv * s
            dsg = s + gv * s * (1.0 - s)
            o = of_ref[:, sl] + ob_ref[:, sl]
            r = lax.rsqrt(jnp.mean(o * o, axis=-1, keepdims=True) + EPS)
            on = o * r
            dyv = dy_ref[:, sl]
            gn = gain_ref[:, sl]
            dgain_ref[:, sl] += jnp.sum(dyv * on * sg, axis=0, keepdims=True)
            dg_ref[:, sl] = (dyv * on * gn * dsg).astype(BF16)
            don = dyv * gn * sg
            do_ref[:, sl] = r * (don - on * jnp.mean(don * on, axis=-1, keepdims=True))

    return pl.pallas_call(
        body, grid=(lp // tr,),
        in_specs=[_rspec(tr, w), _rspec(tr, w), _rspec(tr, w, col_g // w), _fspec((1, w)),
                  _rspec(tr, w)],
        out_specs=[_rspec(tr, w), _rspec(tr, w), _fspec((1, w))],
        out_shape=[jax.ShapeDtypeStruct((lp, w), F32), jax.ShapeDtypeStruct((lp, w), BF16),
                   jax.ShapeDtypeStruct((1, w), F32)],
        compiler_params=_cp("arbitrary"), name=name)(o_f, o_b, proj, gain, dy)


HG_ROWS = 128
HG_HALVES = (1, 2, 4, 8, 16, 32, 64)


def _hg_gates(zq, z, lbv):
    qh = zq * _sigmoid(zq)
    s = _sigmoid(z)
    f = lbv + (1.0 - lbv) * s
    kk = (1.0 - lbv) * _sigmoid(-z)
    return qh, s, f, jnp.log(f), kk


def _block_cumsum(g, pos, suffix):
    x = g
    for k in HG_HALVES:
        if suffix:
            x = x + jnp.where(pos < HG_ROWS - k, pltpu.roll(x, HG_ROWS - k, 0), 0.0)
        else:
            x = x + jnp.where(pos >= k, pltpu.roll(x, k, 0), 0.0)
    return x


def _pair_levels(b, pos, reverse):
    out = []
    first = b
    for m in HG_HALVES:
        if m > 1:
            first = jnp.where((pos & (m - 1)) >= m // 2, pltpu.roll(first, m // 2, 0), first)
        nxt = pltpu.roll(first, HG_ROWS - m, 0)
        upper = (pos & (2 * m - 1)) >= m
        if reverse:
            eq = jnp.where(upper, 0.0, jnp.exp(b - nxt))
            ek = jnp.where(upper, jnp.exp(first - b), 0.0)
        else:
            eq = jnp.where(upper, jnp.exp(b - first), 0.0)
            ek = jnp.where(upper, 0.0, jnp.exp(nxt - b))
        out.append((eq, ek))
    return out


def _pair_masks(mask_ref):
    ri = lax.broadcasted_iota(jnp.int32, (HG_ROWS, HG_ROWS), 0)
    ci = lax.broadcasted_iota(jnp.int32, (HG_ROWS, HG_ROWS), 1)
    for i, m in enumerate(HG_HALVES):
        sh = m.bit_length()
        mask_ref[i] = jnp.where((ri >> sh) == (ci >> sh), 1.0, 0.0)


def _hg_scan_fwd(proj, lb, *, reverse, col_q, col_z, col_i, hh, name):
    lp = proj.shape[0]
    n_blocks = lp // HG_ROWS
    last = 0 if reverse else HG_ROWS - 1

    def body(q_ref, z_ref, i_ref, lb_ref, o_ref, st_ref, mask_ref):
        lbv = lb_ref[...]
        pos = lax.broadcasted_iota(jnp.int32, (HG_ROWS, 1), 0)
        ri = lax.broadcasted_iota(jnp.int32, (HG_ROWS, HG_ROWS), 0)
        ci = lax.broadcasted_iota(jnp.int32, (HG_ROWS, HG_ROWS), 1)
        _pair_masks(mask_ref)

        def block(bi, st):
            bb = (n_blocks - 1 - bi) if reverse else bi
            r0 = pl.multiple_of(bb * HG_ROWS, HG_ROWS)
            v16 = i_ref[pl.ds(r0, HG_ROWS), :].astype(BF16)
            qh, _, _, g, kk = _hg_gates(q_ref[pl.ds(r0, HG_ROWS), :], z_ref[pl.ds(r0, HG_ROWS), :],
                                        lbv)
            b = _block_cumsum(g, pos, reverse)
            bl = b[last:last + 1, :]
            qe = (qh * jnp.exp(b)).astype(BF16)
            kd = (kk * jnp.exp(bl - b)).astype(BF16)
            a = jnp.where(ri == ci, jnp.sum(qh * kk, axis=1, keepdims=True), 0.0)
            for i, (eq, ek) in enumerate(_pair_levels(b, pos, reverse)):
                a = a + mask_ref[i] * _nt((qh * eq).astype(BF16), (kk * ek).astype(BF16))
            st_ref[bb] = st
            o_ref[pl.ds(r0, HG_ROWS), :] = _nn(a.astype(BF16), v16) + _nt(qe, st.astype(BF16))
            return jnp.exp(bl) * st + _tn(v16, kd)

        lax.fori_loop(0, n_blocks, block, jnp.zeros((HG_DK, HG_DK), F32))

    cspec = lambda col: pl.BlockSpec((lp, HG_DK), lambda h: (0, col // HG_DK + h))
    return pl.pallas_call(
        body, grid=(hh,),
        in_specs=[cspec(col_q), cspec(col_z), cspec(col_i),
                  pl.BlockSpec((None, 1, HG_DK), lambda h: (h, 0, 0))],
        out_specs=[pl.BlockSpec((lp, HG_DK), lambda h: (0, h)),
                   pl.BlockSpec((None, n_blocks, HG_DK, HG_DK), lambda h: (h, 0, 0, 0))],
        out_shape=[jax.ShapeDtypeStruct((lp, hh * HG_DK), F32),
                   jax.ShapeDtypeStruct((hh, n_blocks, HG_DK, HG_DK), F32)],
        scratch_shapes=[pltpu.VMEM((len(HG_HALVES), HG_ROWS, HG_ROWS), F32)],
        compiler_params=_cp("parallel"), name=name)(proj, proj, proj, lb)


def _hg_scan_bwd(proj, lb, states, do, *, reverse, col_q, col_z, col_i, hh, name):
    lp = proj.shape[0]
    n_blocks = lp // HG_ROWS
    last = 0 if reverse else HG_ROWS - 1

    def body(q_ref, z_ref, i_ref, lb_ref, st_ref, do_ref, dq_ref, dz_ref, dv_ref, dlb_ref, mask_ref):
        lbv = lb_ref[...]
        pos = lax.broadcasted_iota(jnp.int32, (HG_ROWS, 1), 0)
        ri = lax.broadcasted_iota(jnp.int32, (HG_ROWS, HG_ROWS), 0)
        ci = lax.broadcasted_iota(jnp.int32, (HG_ROWS, HG_ROWS), 1)
        _pair_masks(mask_ref)

        def block(bi, carry):
            dst, dlb = carry
            bb = bi if reverse else (n_blocks - 1 - bi)
            r0 = pl.multiple_of(bb * HG_ROWS, HG_ROWS)
            zq = q_ref[pl.ds(r0, HG_ROWS), :]
            v16 = i_ref[pl.ds(r0, HG_ROWS), :].astype(BF16)
            do16 = do_ref[pl.ds(r0, HG_ROWS), :].astype(BF16)
            qh, s, f, g, kk = _hg_gates(zq, z_ref[pl.ds(r0, HG_ROWS), :], lbv)
            b = _block_cumsum(g, pos, reverse)
            bl = b[last:last + 1, :]
            eb = jnp.exp(b)
            ebl = jnp.exp(bl - b)
            decay = jnp.exp(bl)
            qe16 = (qh * eb).astype(BF16)
            kd16 = (kk * ebl).astype(BF16)
            st = st_ref[bb]
            st16, dst16 = st.astype(BF16), dst.astype(BF16)
            same_row = ri == ci
            da = _nt(do16, v16)
            da_diag = jnp.sum(jnp.where(same_row, da, 0.0), axis=1, keepdims=True)
            dq_state = eb * _nn(do16, st16)
            dk_state = ebl * _nn(v16, dst16)
            dq = dq_state + da_diag * kk
            dk = dk_state + da_diag * qh
            dbl = (decay * jnp.sum(st * dst, axis=0, keepdims=True)
                   + jnp.sum(kk * dk_state, axis=0, keepdims=True))
            db = qh * dq_state - kk * dk_state + jnp.where(pos == last, dbl, 0.0)
            a = jnp.where(same_row, jnp.sum(qh * kk, axis=1, keepdims=True), 0.0)
            for i, (eq, ek) in enumerate(_pair_levels(b, pos, reverse)):
                same = mask_ref[i]
                q16, k16 = (qh * eq).astype(BF16), (kk * ek).astype(BF16)
                a = a + same * _nt(q16, k16)
                da16 = (same * da).astype(BF16)
                gq, gk = _nn(da16, k16), _tn(da16, q16)
                dq = dq + eq * gq
                dk = dk + ek * gk
                db = db + (q16.astype(F32) * gq - k16.astype(F32) * gk)
            dg = _block_cumsum(db, pos, not reverse)
            df = dg / f - dk
            sq = _sigmoid(zq)
            dq_ref[pl.ds(r0, HG_ROWS), :] = dq * (sq + zq * sq * (1.0 - sq))
            dz_ref[pl.ds(r0, HG_ROWS), :] = df * (1.0 - lbv) * s * (1.0 - s)
            dv_ref[pl.ds(r0, HG_ROWS), :] = _nt(kd16, dst16) + _tn(a.astype(BF16), do16)
            return (decay * dst + _tn(do16, qe16),
                    dlb + jnp.sum(df * (1.0 - s), axis=0, keepdims=True))

        _, dlb = lax.fori_loop(0, n_blocks, block,
                               (jnp.zeros((HG_DK, HG_DK), F32), jnp.zeros((1, HG_DK), F32)))
        dlb_ref[...] = dlb

    cspec = lambda col: pl.BlockSpec((lp, HG_DK), lambda h: (0, col // HG_DK + h))
    ospec = pl.BlockSpec((lp, HG_DK), lambda h: (0, h))
    sds = jax.ShapeDtypeStruct((lp, hh * HG_DK), F32)
    return pl.pallas_call(
        body, grid=(hh,),
        in_specs=[cspec(col_q), cspec(col_z), cspec(col_i),
                  pl.BlockSpec((None, 1, HG_DK), lambda h: (h, 0, 0)),
                  pl.BlockSpec((None, n_blocks, HG_DK, HG_DK), lambda h: (h, 0, 0, 0)),
                  ospec],
        out_specs=[ospec, ospec, ospec, pl.BlockSpec((None, 1, HG_DK), lambda h: (h, 0, 0))],
        out_shape=[sds, sds, sds, jax.ShapeDtypeStruct((hh, 1, HG_DK), F32)],
        scratch_shapes=[pltpu.VMEM((len(HG_HALVES), HG_ROWS, HG_ROWS), F32)],
        compiler_params=_cp("parallel"), name=name)(proj, proj, proj, lb, states, do)


NA_HB = LANES // NA_HEAD_DIM
NA_G = 4
NA_U = NA_G + NA_WIN_H
NA_QN = NA_G * GRID_W
NA_KN = NA_U * GRID_W


def _na_table_index(pattern, a, j):
    if pattern == 0:
        return j - a + NA_WIN_H - 1 if j < NA_WIN_H else None
    if pattern == 2:
        return j - a - 1 if j >= NA_U - NA_WIN_H else None
    return j - a + NA_WIN_H // 2 - 1 if a <= j < a + NA_WIN_H else None


def _na_step_rows(pattern, t, rows):
    if pattern == 0:
        r0, us = 0, 0
    elif pattern == 2:
        r0, us = rows - NA_G, rows - NA_U
    else:
        r0 = NA_G * t
        us = r0 - NA_WIN_H // 2
    q0, k0 = N_META + GRID_W * r0, N_META + GRID_W * us
    if pattern == 1:
        q0, k0 = pl.multiple_of(q0, 16), pl.multiple_of(k0, 16)
    return q0, k0


def _na_fill_bias(tb_ref, bias_ref):
    neg = jnp.full((GRID_W, GRID_W), -1e30, F32)
    for h in range(NA_HB):
        for pattern in range(3):
            for a in range(NA_G):
                for j in range(NA_U):
                    idx = _na_table_index(pattern, a, j)
                    bias_ref[h, pattern, a * GRID_W:(a + 1) * GRID_W, j * GRID_W:(j + 1) * GRID_W] = (
                        neg if idx is None else tb_ref[h, idx])


def _na_steps(rows, step, carry):
    n_steps = rows // NA_G
    carry = step(0, 0, carry)
    carry = lax.fori_loop(1, n_steps - 1, functools.partial(step, 1), carry)
    return step(2, n_steps - 1, carry)


def _na_head_lanes():
    lane = lax.broadcasted_iota(jnp.int32, (1, LANES), 1)
    return [lane // NA_HEAD_DIM == h for h in range(NA_HB)]


def _na_only(mask, x):
    return jnp.where(mask, x, jnp.zeros_like(x))


def _na_fwd(proj, tb, *, n_tok, nh, name):
    lp = proj.shape[0]
    dh, hb = NA_HEAD_DIM, NA_HB
    naw = nh * dh
    rows = n_tok // GRID_W
    scale = dh ** -0.5

    def body(q_ref, k_ref, v_ref, tb_ref, o_ref, lse_ref, q16_ref, k16_ref, v16_ref, bias_ref):
        o_ref[...] = jnp.zeros_like(o_ref)
        lse_ref[...] = jnp.zeros_like(lse_ref)
        q16_ref[...] = q_ref[...].astype(BF16)
        k16_ref[...] = k_ref[...].astype(BF16)
        v16_ref[...] = v_ref[...].astype(BF16)
        _na_fill_bias(tb_ref, bias_ref)
        heads = _na_head_lanes()
        km = k16_ref[0:N_META, :]
        vm = v16_ref[0:N_META, :]
        qm = q16_ref[0:N_META, :]
        o_m = None
        for h in range(hb):
            s = _nt(_na_only(heads[h], qm), km) * scale
            m = jnp.max(s, axis=1, keepdims=True)
            p = jnp.exp(s - m)
            l = jnp.sum(p, axis=1, keepdims=True)
            o_h = _nn(p.astype(BF16), vm) / l
            o_m = o_h if o_m is None else jnp.where(heads[h], o_h, o_m)
            lse_ref[h, 0:N_META, :] = m + jnp.log(l)
        o_ref[0:N_META, :] = o_m

        def step(pattern, t, carry):
            q0, k0 = _na_step_rows(pattern, t, rows)
            q16 = q16_ref[pl.ds(q0, NA_QN), :]
            k16 = k16_ref[pl.ds(k0, NA_KN), :]
            v16 = v16_ref[pl.ds(k0, NA_KN), :]
            o = None
            for h in range(hb):
                q_h = _na_only(heads[h], q16)
                s = _nt(q_h, k16) * scale + bias_ref[h, pattern]
                sm = _nt(q_h, km) * scale
                m = jnp.maximum(jnp.max(s, axis=1, keepdims=True),
                                jnp.max(sm, axis=1, keepdims=True))
                p = jnp.exp(s - m)
                pm = jnp.exp(sm - m)
                l = jnp.sum(p, axis=1, keepdims=True) + jnp.sum(pm, axis=1, keepdims=True)
                o_h = (_nn(p.astype(BF16), v16) + _nn(pm.astype(BF16), vm)) / l
                o = o_h if o is None else jnp.where(heads[h], o_h, o)
                lse_ref[h, pl.ds(q0, NA_QN), :] = m + jnp.log(l)
            o_ref[pl.ds(q0, NA_QN), :] = o
            return carry

        _na_steps(rows, step, 0)

    cblk = lambda col: pl.BlockSpec((lp, LANES), lambda g: (0, col // LANES + g))
    return pl.pallas_call(
        body, grid=(nh // hb,),
        in_specs=[cblk(0), cblk(naw), cblk(2 * naw),
                  pl.BlockSpec((hb, 2 * NA_WIN_H - 1, GRID_W, GRID_W), lambda g: (g, 0, 0, 0))],
        out_specs=[cblk(0), pl.BlockSpec((hb, lp, 1), lambda g: (g, 0, 0))],
        out_shape=[jax.ShapeDtypeStruct((lp, naw), F32), jax.ShapeDtypeStruct((nh, lp, 1), F32)],
        scratch_shapes=[pltpu.VMEM((lp, LANES), BF16)] * 3 + [pltpu.VMEM((hb, 3, NA_QN, NA_KN), F32)],
        compiler_params=_cp("parallel"), name=name)(proj, proj, proj, tb)


def _na_bwd(proj, tb, o, lse, do, *, n_tok, nh, name):
    lp = proj.shape[0]
    dh, hb = NA_HEAD_DIM, NA_HB
    naw = nh * dh
    rows = n_tok // GRID_W
    scale = dh ** -0.5

    def body(q_ref, k_ref, v_ref, tb_ref, o_ref, lse_ref, do_ref, dq_ref, dk_ref, dv_ref, dtb_ref,
             q16_ref, k16_ref, v16_ref, bias_ref):
        dq_ref[...] = jnp.zeros_like(dq_ref)
        dk_ref[...] = jnp.zeros_like(dk_ref)
        dv_ref[...] = jnp.zeros_like(dv_ref)
        dtb_ref[...] = jnp.zeros_like(dtb_ref)
        q16_ref[...] = q_ref[...].astype(BF16)
        k16_ref[...] = k_ref[...].astype(BF16)
        v16_ref[...] = v_ref[...].astype(BF16)
        _na_fill_bias(tb_ref, bias_ref)
        heads = _na_head_lanes()
        km = k16_ref[0:N_META, :]
        vm = v16_ref[0:N_META, :]
        qm = q16_ref[0:N_META, :]
        dom = do_ref[0:N_META, :]
        prod = dom * o_ref[0:N_META, :]
        dq_m = None
        dkm0 = jnp.zeros((N_META, LANES), F32)
        dvm0 = jnp.zeros((N_META, LANES), F32)
        for h in range(hb):
            q_h = _na_only(heads[h], qm)
            do_h = _na_only(heads[h], dom).astype(BF16)
            p = jnp.exp(_nt(q_h, km) * scale - lse_ref[h, 0:N_META, :])
            delta = jnp.sum(_na_only(heads[h], prod), axis=1, keepdims=True)
            ds = (p * (_nt(do_h, vm) - delta)).astype(BF16)
            dq_h = _nn(ds, km) * scale
            dq_m = dq_h if dq_m is None else jnp.where(heads[h], dq_h, dq_m)
            dkm0 = dkm0 + _tn(ds, q_h) * scale
            dvm0 = dvm0 + _tn(p.astype(BF16), do_h)
        dq_ref[0:N_META, :] = dq_m

        def step(pattern, t, carry):
            dkm, dvm = carry
            q0, k0 = _na_step_rows(pattern, t, rows)
            q16 = q16_ref[pl.ds(q0, NA_QN), :]
            k16 = k16_ref[pl.ds(k0, NA_KN), :]
            v16 = v16_ref[pl.ds(k0, NA_KN), :]
            dov = do_ref[pl.ds(q0, NA_QN), :]
            prod = dov * o_ref[pl.ds(q0, NA_QN), :]
            dq = None
            dk = jnp.zeros((NA_KN, LANES), F32)
            dv = jnp.zeros((NA_KN, LANES), F32)
            for h in range(hb):
                q_h = _na_only(heads[h], q16)
                do_h = _na_only(heads[h], dov).astype(BF16)
                lse = lse_ref[h, pl.ds(q0, NA_QN), :]
                p = jnp.exp(_nt(q_h, k16) * scale + bias_ref[h, pattern] - lse)
                pm = jnp.exp(_nt(q_h, km) * scale - lse)
                delta = jnp.sum(_na_only(heads[h], prod), axis=1, keepdims=True)
                ds = p * (_nt(do_h, v16) - delta)
                dsm = (pm * (_nt(do_h, vm) - delta)).astype(BF16)
                ds16 = ds.astype(BF16)
                dq_h = (_nn(ds16, k16) + _nn(dsm, km)) * scale
                dq = dq_h if dq is None else jnp.where(heads[h], dq_h, dq)
                dk = dk + _tn(ds16, q_h) * scale
                dv = dv + _tn(p.astype(BF16), do_h)
                dkm = dkm + _tn(dsm, q_h) * scale
                dvm = dvm + _tn(pm.astype(BF16), do_h)
                for a in range(NA_G):
                    for j in range(NA_U):
                        idx = _na_table_index(pattern, a, j)
                        if idx is not None:
                            dtb_ref[h, idx] += ds[a * GRID_W:(a + 1) * GRID_W,
                                                  j * GRID_W:(j + 1) * GRID_W]
            dq_ref[pl.ds(q0, NA_QN), :] = dq
            dk_ref[pl.ds(k0, NA_KN), :] += dk
            dv_ref[pl.ds(k0, NA_KN), :] += dv
            return dkm, dvm

        dkm, dvm = _na_steps(rows, step, (dkm0, dvm0))
        dk_ref[0:N_META, :] += dkm
        dv_ref[0:N_META, :] += dvm

    cblk = lambda col: pl.BlockSpec((lp, LANES), lambda g: (0, col // LANES + g))
    tbs = pl.BlockSpec((hb, 2 * NA_WIN_H - 1, GRID_W, GRID_W), lambda g: (g, 0, 0, 0))
    sds = jax.ShapeDtypeStruct((lp, naw), F32)
    return pl.pallas_call(
        body, grid=(nh // hb,),
        in_specs=[cblk(0), cblk(naw), cblk(2 * naw), tbs, cblk(0),
                  pl.BlockSpec((hb, lp, 1), lambda g: (g, 0, 0)), cblk(0)],
        out_specs=[cblk(0), cblk(0), cblk(0), tbs],
        out_shape=[sds, sds, sds, jax.ShapeDtypeStruct(tb.shape, F32)],
        scratch_shapes=[pltpu.VMEM((lp, LANES), BF16)] * 3 + [pltpu.VMEM((hb, 3, NA_QN, NA_KN), F32)],
        compiler_params=_cp("parallel"), name=name)(proj, proj, proj, tb, o, lse, do)


def _rpb_onehot():
    c = np.arange(GRID_W)[:, None]
    w = np.arange(GRID_W)[None, :]
    cs = np.clip(c - NA_WIN_W // 2, 0, GRID_W - NA_WIN_W)
    in_win = (w >= cs) & (w < cs + NA_WIN_W)
    dc = np.clip(w - c, -(NA_WIN_W - 1), NA_WIN_W - 1) + NA_WIN_W - 1
    oh = np.zeros((LANES, GRID_W * GRID_W), np.float32)
    flat = np.arange(GRID_W * GRID_W).reshape(GRID_W, GRID_W)
    oh[dc[in_win], flat[in_win]] = 1.0
    neg = np.where(in_win, 0.0, -1e30).astype(np.float32).reshape(1, -1)
    return oh, neg


def _assemble_dproj(dq_na, dk_na, dv_na, dq_f, dq_b, dz_f, dz_b, dv_f, dv_b, dg, dgn, dgh, *, name):
    lp, naw = dq_na.shape
    hgw = dq_f.shape[1]
    d = dgn.shape[1]
    cols = 3 * naw + 5 * hgw + 2 * d
    tr = _row_tile(lp, 3 * naw * 4 + 6 * hgw * 4 + hgw * 2 + 2 * d * 2 + cols * 2)

    def body(nq_ref, nk_ref, nv_ref, qf_ref, qb_ref, zf_ref, zb_ref, vf_ref, vb_ref, g_ref, gn_ref,
             gh_ref, o_ref):
        o_ref[:, 0:naw] = nq_ref[...].astype(BF16)
        o_ref[:, naw:2 * naw] = nk_ref[...].astype(BF16)
        o_ref[:, 2 * naw:3 * naw] = nv_ref[...].astype(BF16)
        c0 = 3 * naw
        o_ref[:, c0:c0 + hgw] = (qf_ref[...] + qb_ref[...]).astype(BF16)
        o_ref[:, c0 + hgw:c0 + 2 * hgw] = zf_ref[...].astype(BF16)
        o_ref[:, c0 + 2 * hgw:c0 + 3 * hgw] = zb_ref[...].astype(BF16)
        o_ref[:, c0 + 3 * hgw:c0 + 4 * hgw] = (vf_ref[...] + vb_ref[...]).astype(BF16)
        o_ref[:, c0 + 4 * hgw:c0 + 5 * hgw] = g_ref[...]
        o_ref[:, c0 + 5 * hgw:c0 + 5 * hgw + d] = gn_ref[...]
        o_ref[:, c0 + 5 * hgw + d:] = gh_ref[...]

    hg, na = _rspec(tr, hgw), _rspec(tr, naw)
    return pl.pallas_call(
        body, grid=(lp // tr,),
        in_specs=[na, na, na, hg, hg, hg, hg, hg, hg, hg, _rspec(tr, d), _rspec(tr, d)],
        out_specs=_rspec(tr, cols),
        out_shape=jax.ShapeDtypeStruct((lp, cols), BF16),
        compiler_params=_cp("parallel"), name=name)(dq_na, dk_na, dv_na, dq_f, dq_b, dz_f, dz_b,
                                                    dv_f, dv_b, dg, dgn, dgh)


GROUP_STEPS = 8


def _group_tiles(rows, align):
    steps = GROUP_STEPS if all(r % (GROUP_STEPS * align) == 0 for r in rows) else 1
    return steps, [r // steps for r in rows]


def _adamw(ws, gs, ms, vs, *, name):
    n = len(ws)
    steps, trs = _group_tiles([w.shape[0] for w in ws], 8)

    def body(*refs):
        for i in range(n):
            w_ref, g_ref, m_ref, v_ref = refs[4 * i:4 * i + 4]
            d_ref, mo_ref, vo_ref = refs[4 * n + 3 * i:4 * n + 3 * i + 3]
            gv = g_ref[...]
            mn = ADAM_B1 * m_ref[...] + (1.0 - ADAM_B1) * gv
            vn = ADAM_B2 * v_ref[...] + (1.0 - ADAM_B2) * (gv * gv)
            m_hat = mn / (1.0 - ADAM_B1 ** ADAM_STEP)
            v_hat = vn / (1.0 - ADAM_B2 ** ADAM_STEP)
            d_ref[...] = -ADAM_LR * (m_hat / (jnp.sqrt(v_hat) + ADAM_EPS) + ADAM_WD * w_ref[...])
            mo_ref[...] = mn
            vo_ref[...] = vn

    specs = [_rspec(tr, w.shape[1]) for tr, w in zip(trs, ws)]
    out = pl.pallas_call(
        body, grid=(steps,),
        in_specs=[s for s in specs for _ in range(4)],
        out_specs=[s for s in specs for _ in range(3)],
        out_shape=[jax.ShapeDtypeStruct(w.shape, F32) for w in ws for _ in range(3)],
        compiler_params=_cp("parallel"), name=name)(*[a for q in zip(ws, gs, ms, vs) for a in q])
    return [tuple(out[3 * i:3 * i + 3]) for i in range(n)]


def _local_step(x, tgt, meta, first_weight, rest_weights, g_mix, g_mlp, g_fin, hg_gain, rpb, lb,
                early_grads=None, mid_grads=None, late_grad=None, rest_landed=None):
    n_tok, d = x.shape
    hgw = hg_gain.shape[1]
    nh, hh = rpb.shape[0], hgw // HG_DK
    naw = nh * NA_HEAD_DIM
    l_real = N_META + n_tok
    lp = -(-l_real // ROW_ALIGN) * ROW_ALIGN
    n_chunks = l_real // HG_CHUNK
    pad = lp - l_real
    col_qhg = 3 * naw
    col_zf, col_zb, col_i, col_g = (col_qhg + hgw, col_qhg + 2 * hgw, col_qhg + 3 * hgw,
                                    col_qhg + 4 * hgw)
    col_gate = col_qhg + 5 * hgw

    zpad = jnp.zeros((pad, d), F32)
    h0 = jnp.concatenate([meta, x, zpad], axis=0)
    tgt_p = jnp.concatenate([jnp.zeros((N_META, d), F32), tgt, zpad], axis=0)

    oh_np, neg_np = _rpb_onehot()
    oh = jnp.asarray(oh_np)
    rpb_p = jnp.pad(rpb.reshape(nh * (2 * NA_WIN_H - 1), 2 * NA_WIN_W - 1),
                    ((0, 0), (0, LANES - (2 * NA_WIN_W - 1))))
    tb = _matmul(rpb_p, oh, tm=rpb_p.shape[0], tn=512, tk=LANES, precision=HIGHEST,
                 name="rpb_expand")
    tb = (tb + jnp.asarray(neg_np)).reshape(nh, 2 * NA_WIN_H - 1, GRID_W, GRID_W)

    a = _rmsnorm_fwd(h0, g_mix, name="norm_mix")
    w_in = first_weight(a)
    proj = _matmul(a, w_in, name="mm_in")
    o_na, lse = _na_fwd(proj, tb, n_tok=n_tok, nh=nh, name="na_fwd")
    lb_f = lb[0].reshape(hh, 1, HG_DK)
    lb_b = lb[1].reshape(hh, 1, HG_DK)
    scan_kw = dict(col_q=col_qhg, col_i=col_i, hh=hh)
    o_f, st_f = _hg_scan_fwd(proj, lb_f, reverse=False, col_z=col_zf, name="hg_scan_f", **scan_kw)
    token = rest_landed(o_f) if rest_landed else None
    lb_b_late = lb_b if token is None else lb_b + token[0:1, 0:1]
    o_b, st_b = _hg_scan_fwd(proj, lb_b_late, reverse=True, col_z=col_zb, name="hg_scan_b",
                             **scan_kw)
    o_hg = _hg_out(o_f, o_b, proj, hg_gain, col_g=col_g, name="hg_out")
    w_na, w_hg, w_o, w_up, w_down = rest_weights(o_hg)
    y_na = _matmul(o_na, w_na, name="mm_na_out")
    gates = ((proj, col_gate), (proj, col_gate + d))

    def mix_gates(acc, gn, gh, yn):
        return acc, _sigmoid(gn) * yn + _sigmoid(gh) * acc

    def mix_gates_bwd(dmix, gn, gh, yn, yh):
        sn, sh = _sigmoid(gn), _sigmoid(gh)
        return dmix * sn, dmix * sh, dmix * yn * sn * (1.0 - sn), dmix * yh * sh * (1.0 - sh)

    y_hg, mix = _matmul(o_hg, w_hg, name="mm_hg_out", epilogue=mix_gates,
                        tiles=(*gates, (y_na, 0)), out_dtypes=(F32, BF16))
    t1 = _matmul(mix, w_o, name="mm_o")
    h1, mlp_in = _residual_norm(h0, t1, g_mlp, name="resid_norm_mlp")
    u, act = _matmul(mlp_in, w_up, name="mm_up", out_dtypes=(BF16, BF16),
                     epilogue=lambda acc: (acc, jnp.square(jnp.maximum(acc, 0.0))))
    t2 = _matmul(act, w_down, name="mm_down")
    dh2, loss, dg_fin = _final_loss(h1, t2, g_fin, tgt_p, n_tok=n_tok, name="final_loss")

    (du,) = _matmul(dh2, w_down, tb=True, name="mm_down_dx", tiles=((u, 0),), out_dtypes=(BF16,),
                    epilogue=lambda acc, uv: (acc * 2.0 * jnp.maximum(uv, 0.0),))
    dw_down = _matmul(act, dh2, ta=True, name="mm_down_dw")
    dm = _matmul(du, w_up, tb=True, name="mm_up_dx")
    dw_up = _matmul(mlp_in, du, ta=True, name="mm_up_dw")
    dh1, dg_mlp = _rmsnorm_bwd_add(h1, g_mlp, dm, dh2, name="norm_mlp_bwd")
    dy_na, dy_hg, dgn, dgh = _matmul(dh1, w_o, tb=True, name="mm_o_dx", epilogue=mix_gates_bwd,
                                     tiles=(*gates, (y_na, 0), (y_hg, 0)), out_dtypes=(BF16,) * 4)
    dw_o = _matmul(mix, dh1, ta=True, name="mm_o_dw")
    do_na = _matmul(dy_na, w_na, tb=True, name="mm_na_out_dx")
    dw_na = _matmul(o_na, dy_na, ta=True, name="mm_na_out_dw")
    do_hg = _matmul(dy_hg, w_hg, tb=True, name="mm_hg_out_dx")
    dw_hg = _matmul(o_hg, dy_hg, ta=True, name="mm_hg_out_dw")
    token = early_grads([dw_na, dw_hg, dw_o, dw_up, dw_down]) if early_grads else None
    if token is not None:
        hg_gain = hg_gain + token[0:1, 0:1]
    d_o, dg_hg, d_gain = _hg_out_bwd(o_f, o_b, proj, hg_gain, do_hg, col_g=col_g, name="hg_out_bwd")
    dq_f, dz_f, dv_f, dlb_f = _hg_scan_bwd(proj, lb_f, st_f, d_o, reverse=False, col_z=col_zf,
                                           name="hg_scan_f_bwd", **scan_kw)
    token = mid_grads(dq_f) if mid_grads else None
    lb_b_late = lb_b if token is None else lb_b + token[0:1, 0:1]
    dq_b, dz_b, dv_b, dlb_b = _hg_scan_bwd(proj, lb_b_late, st_b, d_o, reverse=True, col_z=col_zb,
                                           name="hg_scan_b_bwd", **scan_kw)
    dq_na, dk_na, dv_na, dtb = _na_bwd(proj, tb, o_na, lse, do_na, n_tok=n_tok, nh=nh, name="na_bwd")
    dproj = _assemble_dproj(dq_na, dk_na, dv_na, dq_f, dq_b, dz_f, dz_b, dv_f, dv_b, dg_hg, dgn,
                            dgh, name="assemble_dproj")
    dw_in = _matmul(a, dproj, ta=True, name="mm_in_dw")
    token = late_grad(dw_in) if late_grad else None
    da = _matmul(dproj, w_in, tb=True, name="mm_in_dx", after=token)
    dh0, dg_mix = _rmsnorm_bwd_add(h0, g_mix, da, dh1, name="norm_mix_bwd")
    d_rpb = _matmul(dtb.reshape(nh * (2 * NA_WIN_H - 1), GRID_W * GRID_W), oh, tb=True,
                    tm=nh * (2 * NA_WIN_H - 1), tn=LANES, tk=1024, precision=HIGHEST,
                    name="rpb_reduce")
    d_lb = jnp.concatenate([dlb_f.reshape(1, hgw), dlb_b.reshape(1, hgw)], axis=0)
    return (loss, dh0[N_META:l_real], dh0[:N_META], dw_in, dw_na, dw_hg, dw_o, dw_up, dw_down,
            dg_mix, dg_mlp, dg_fin, d_gain, d_rpb, d_lb)


N_CHIPS = 4
N_DEV = 8
ANY = pl.BlockSpec(memory_space=pl.ANY)


def _place():
    x, y, c = lax.axis_index("x"), lax.axis_index("y"), lax.axis_index("c")
    others = []
    for j in (1, 2, 3):
        tx = (1 - x) if (j >> 1) else x
        ty = (1 - y) if (j & 1) else y
        others.append((tx, ty))
    return x, y, c, others


def _piece(ref, axis, k, half, rh, cs):
    if axis == 1:
        return ref.at[pl.ds(pl.multiple_of(half * rh, 16), rh), pl.ds(pl.multiple_of(k * cs, LANES), cs)]
    return ref.at[pl.ds(pl.multiple_of(k * 2 * rh + half * rh, 16), rh), :]


def _cast_into_full(shards, axes, place, *, name):
    n = len(shards)
    steps, trs = _group_tiles([s.shape[0] for s in shards], 16)

    def body(p_ref, *refs):
        for i in range(n):
            refs[n + i][...] = refs[i][...].astype(BF16)

    def out_spec(tr, cs, axis):
        if axis == 1:
            return pl.BlockSpec((tr, cs), lambda i, p_ref: (i, p_ref[0]))
        return pl.BlockSpec((tr, cs), lambda i, p_ref: (p_ref[0] * steps + i, 0))

    return pl.pallas_call(
        body,
        grid_spec=pltpu.PrefetchScalarGridSpec(
            num_scalar_prefetch=1, grid=(steps,),
            in_specs=[pl.BlockSpec((tr, s.shape[1]), lambda i, p_ref: (i, 0))
                      for tr, s in zip(trs, shards)],
            out_specs=[out_spec(tr, s.shape[1], ax) for tr, s, ax in zip(trs, shards, axes)]),
        out_shape=[jax.ShapeDtypeStruct((s.shape[0], s.shape[1] * N_CHIPS) if ax == 1
                                        else (s.shape[0] * N_CHIPS, s.shape[1]), BF16)
                   for s, ax in zip(shards, axes)],
        compiler_params=_cp("parallel"), name=name)(place, *shards)


HBM_SPEC = pl.BlockSpec(memory_space=pltpu.HBM)
SEM_SPEC = pl.BlockSpec(memory_space=pltpu.SEMAPHORE)
SPLIT_COPY = pltpu.CompilerParams(has_side_effects=pltpu.SideEffectType.DATAFLOW_SIDE_EFFECTING)
TOKEN = jax.ShapeDtypeStruct((8, LANES), F32)


def _geo(fulls, axes):
    out = []
    for f, ax in zip(fulls, axes):
        r, cs = (f.shape[0], f.shape[1] // N_CHIPS) if ax == 1 else (f.shape[0] // N_CHIPS, f.shape[1])
        out.append((ax, r // 2, cs))
    return out


def _gather_copies(refs, geo, send_sems, recv_sems):
    x, y, c, others = _place()
    chip = 2 * x + y
    cps = []
    for i, (ax, rh, cs) in enumerate(geo):
        mine = _piece(refs[i], ax, chip, c, rh, cs)
        for j, (tx, ty) in enumerate(others):
            cps.append(pltpu.make_async_remote_copy(
                src_ref=mine, dst_ref=mine, send_sem=send_sems.at[3 * i + j],
                recv_sem=recv_sems.at[3 * i + j], device_id=(tx, ty, c), device_id_type=MESH))
    return cps


def _allgather_start(fulls, axes, after, *, name):
    n = len(fulls)
    geo = _geo(fulls, axes)

    def body(*refs):
        w_refs = refs[:n]
        send_sems, recv_sems = refs[n + 1], refs[n + 2]
        token = refs[2 * n + 3]
        for cp in _gather_copies(w_refs, geo, send_sems, recv_sems):
            cp.start()
        token[...] = jnp.zeros_like(token)

    out = pl.pallas_call(
        body, name=name,
        out_shape=(pltpu.SemaphoreType.DMA((3 * n,)), pltpu.SemaphoreType.DMA((3 * n,)),
                   *[pltpu.HBM(f.shape, f.dtype) for f in fulls], TOKEN),
        in_specs=[HBM_SPEC] * n + [ANY],
        out_specs=(SEM_SPEC, SEM_SPEC, *[HBM_SPEC] * n, pl.BlockSpec(memory_space=pltpu.VMEM)),
        input_output_aliases={i: 2 + i for i in range(n)},
        compiler_params=SPLIT_COPY,
    )(*[pltpu.with_memory_space_constraint(f, pltpu.HBM) for f in fulls], after)
    return out[0], out[1], list(out[2:2 + n]), out[2 + n]


def _allgather_wait(send_sems, recv_sems, fulls, axes, after, *, name):
    n = len(fulls)
    geo = _geo(fulls, axes)

    def body(*refs):
        w_refs = refs[:n]
        for cp in _gather_copies(w_refs, geo, refs[n], refs[n + 1]):
            cp.wait_send()
            cp.wait_recv()

    return list(pl.pallas_call(
        body, name=name,
        out_shape=[pltpu.HBM(f.shape, f.dtype) for f in fulls],
        in_specs=[HBM_SPEC] * n + [SEM_SPEC, SEM_SPEC, ANY],
        out_specs=[HBM_SPEC] * n,
        input_output_aliases={i: i for i in range(n)},
        compiler_params=SPLIT_COPY,
    )(*fulls, send_sems, recv_sems, after))


def _allgather_forward(fulls, axes, *, name):
    n = len(fulls)
    geo = _geo(fulls, axes)

    def body(*refs):
        o_refs = refs[n:2 * n]
        send_sems, recv_sems = refs[2 * n:]
        x, y, c, others = _place()

        def rcopy(i, j, half, to):
            ax, rh, cs = geo[i]
            ref = _piece(o_refs[i], ax, 2 * others[j][0] + others[j][1], half, rh, cs)
            return pltpu.make_async_remote_copy(
                src_ref=ref, dst_ref=ref, send_sem=send_sems.at[3 * i + j],
                recv_sem=recv_sems.at[3 * i + j], device_id=to, device_id_type=MESH)

        cps = [rcopy(i, j, c, (x, y, 1 - c)) for i in range(n) for j in range(3)]
        for cp in cps:
            cp.start()
        for i in range(n):
            for j in range(3):
                rcopy(i, j, 1 - c, (x, y, c)).wait_recv()
        for cp in cps:
            cp.wait_send()

    return list(pl.pallas_call(
        body, in_specs=[ANY] * n, out_specs=[ANY] * n,
        out_shape=[jax.ShapeDtypeStruct(f.shape, f.dtype) for f in fulls],
        input_output_aliases={i: i for i in range(n)},
        scratch_shapes=[pltpu.SemaphoreType.DMA((3 * n,)), pltpu.SemaphoreType.DMA((3 * n,))],
        name=name)(*fulls))


def _forward_copies(refs, geo, send_sems, recv_sems):
    x, y, c, others = _place()
    cps = []
    for i, (ax, rh, cs) in enumerate(geo):
        for j, (tx, ty) in enumerate(others):
            ref = _piece(refs[i], ax, 2 * tx + ty, c, rh, cs)
            cps.append(pltpu.make_async_remote_copy(
                src_ref=ref, dst_ref=ref, send_sem=send_sems.at[3 * i + j],
                recv_sem=recv_sems.at[3 * i + j], device_id=(x, y, 1 - c), device_id_type=MESH))
    return cps


def _allgather_forward_start(fulls, axes, *, name):
    n = len(fulls)
    geo = _geo(fulls, axes)

    def body(*refs):
        token = refs[2 * n + 2]
        for cp in _forward_copies(refs[:n], geo, refs[n], refs[n + 1]):
            cp.start()
        token[...] = jnp.zeros_like(token)

    out = pl.pallas_call(
        body, name=name,
        out_shape=(pltpu.SemaphoreType.DMA((3 * n,)), pltpu.SemaphoreType.DMA((3 * n,)),
                   *[pltpu.HBM(f.shape, f.dtype) for f in fulls], TOKEN),
        in_specs=[HBM_SPEC] * n,
        out_specs=(SEM_SPEC, SEM_SPEC, *[HBM_SPEC] * n, pl.BlockSpec(memory_space=pltpu.VMEM)),
        input_output_aliases={i: 2 + i for i in range(n)},
        compiler_params=SPLIT_COPY,
    )(*fulls)
    return out[0], out[1], list(out[2:2 + n]), out[2 + n]


def _allgather_forward_wait(send_sems, recv_sems, fulls, axes, after, *, name):
    n = len(fulls)
    geo = _geo(fulls, axes)

    def body(*refs):
        for cp in _forward_copies(refs[:n], geo, refs[n], refs[n + 1]):
            cp.wait_send()
            cp.wait_recv()

    return list(pl.pallas_call(
        body, name=name,
        out_shape=[pltpu.HBM(f.shape, f.dtype) for f in fulls],
        in_specs=[HBM_SPEC] * n + [SEM_SPEC, SEM_SPEC, ANY],
        out_specs=[HBM_SPEC] * n,
        input_output_aliases={i: i for i in range(n)},
        compiler_params=SPLIT_COPY,
    )(*fulls, send_sems, recv_sems, after))


def _chip_copies(blk_ref, land_ref, send_sems, recv_sems):
    x, y, c, others = _place()
    return [pltpu.make_async_remote_copy(
        src_ref=blk_ref, dst_ref=land_ref.at[2 * x + y], send_sem=send_sems.at[j],
        recv_sem=recv_sems.at[j], device_id=(tx, ty, c), device_id_type=MESH)
        for j, (tx, ty) in enumerate(others)]


def _chip_exchange_start(blk, *, name):
    land = pltpu.with_memory_space_constraint(lax.empty((N_CHIPS, *blk.shape), blk.dtype), pltpu.HBM)

    def body(blk_ref, land_ref, send_sems, recv_sems, blk_out, land_out, token):
        for cp in _chip_copies(blk_ref, land_ref, send_sems, recv_sems):
            cp.start()
        token[...] = jnp.zeros_like(token)

    return pl.pallas_call(
        body, name=name,
        out_shape=(pltpu.SemaphoreType.DMA((3,)), pltpu.SemaphoreType.DMA((3,)),
                   pltpu.HBM(blk.shape, blk.dtype), pltpu.HBM(land.shape, land.dtype), TOKEN),
        in_specs=[HBM_SPEC] * 2,
        out_specs=(SEM_SPEC, SEM_SPEC, HBM_SPEC, HBM_SPEC, pl.BlockSpec(memory_space=pltpu.VMEM)),
        input_output_aliases={0: 2, 1: 3},
        compiler_params=SPLIT_COPY,
    )(pltpu.with_memory_space_constraint(blk, pltpu.HBM), land)


def _chip_exchange_wait(send_sems, recv_sems, blk, land, after, *, name):
    def body(blk_ref, land_ref, send_sems, recv_sems, after_ref, blk_out, land_out):
        for cp in _chip_copies(blk_ref, land_ref, send_sems, recv_sems):
            cp.wait_send()
            cp.wait_recv()

    return pl.pallas_call(
        body, name=name,
        out_shape=[pltpu.HBM(blk.shape, blk.dtype), pltpu.HBM(land.shape, land.dtype)],
        in_specs=[HBM_SPEC] * 2 + [SEM_SPEC, SEM_SPEC, ANY],
        out_specs=[HBM_SPEC] * 2,
        input_output_aliases={0: 0, 1: 1},
        compiler_params=SPLIT_COPY,
    )(blk, land, send_sems, recv_sems, after)[1]


def _scatter_geo(parts, axes):
    out = []
    for p, ax in zip(parts, axes):
        _, rh, cols = p.shape
        out.append((ax, rh, cols // N_CHIPS if ax == 1 else cols))
    return out


def _scatter_copies(p_refs, q_refs, geo, send_sems, recv_sems):
    x, y, c, others = _place()
    chip = 2 * x + y
    cps = []
    for i, (ax, rh, cw) in enumerate(geo):
        for j, (tx, ty) in enumerate(others):
            k = 2 * tx + ty
            src = (p_refs[i].at[0, :, pl.ds(pl.multiple_of(k * cw, LANES), cw)] if ax == 1
                   else p_refs[i].at[k])
            cps.append(pltpu.make_async_remote_copy(
                src_ref=src, dst_ref=q_refs[i].at[chip], send_sem=send_sems.at[3 * i + j],
                recv_sem=recv_sems.at[3 * i + j], device_id=(tx, ty, c), device_id_type=MESH))
    return cps


def _scatter_start(parts, axes, *, name):
    n = len(parts)
    geo = _scatter_geo(parts, axes)
    slots = [pltpu.HBM((N_CHIPS, rh, cw), p.dtype) for p, (_, rh, cw) in zip(parts, geo)]

    def body(*refs):
        p_refs, q_refs = refs[:n], refs[n:2 * n]
        send_sems, recv_sems = refs[2 * n], refs[2 * n + 1]
        token = refs[4 * n + 2]
        for cp in _scatter_copies(p_refs, q_refs, geo, send_sems, recv_sems):
            cp.start()
        token[...] = jnp.zeros_like(token)

    land = [pltpu.with_memory_space_constraint(lax.empty(s.inner_aval.shape, s.inner_aval.dtype), pltpu.HBM)
            for s in slots]
    out = pl.pallas_call(
        body, name=name,
        out_shape=(pltpu.SemaphoreType.DMA((3 * n,)), pltpu.SemaphoreType.DMA((3 * n,)),
                   *[pltpu.HBM(p.shape, p.dtype) for p in parts], *slots, TOKEN),
        in_specs=[HBM_SPEC] * (2 * n),
        out_specs=(SEM_SPEC, SEM_SPEC, *[HBM_SPEC] * (2 * n), pl.BlockSpec(memory_space=pltpu.VMEM)),
        input_output_aliases={i: 2 + i for i in range(2 * n)},
        compiler_params=SPLIT_COPY,
    )(*[pltpu.with_memory_space_constraint(p, pltpu.HBM) for p in parts], *land)
    return out[0], out[1], list(out[2:2 + n]), list(out[2 + n:2 + 2 * n]), out[2 + 2 * n]


def _scatter_wait(send_sems, recv_sems, parts, slots, axes, after, *, name):
    n = len(parts)
    geo = _scatter_geo(parts, axes)

    def body(*refs):
        p_refs, q_refs = refs[:n], refs[n:2 * n]
        for cp in _scatter_copies(p_refs, q_refs, geo, refs[2 * n], refs[2 * n + 1]):
            cp.wait_send()
            cp.wait_recv()

    out = pl.pallas_call(
        body, name=name,
        out_shape=[pltpu.HBM(a.shape, a.dtype) for a in (*parts, *slots)],
        in_specs=[HBM_SPEC] * (2 * n) + [SEM_SPEC, SEM_SPEC, ANY],
        out_specs=[HBM_SPEC] * (2 * n),
        input_output_aliases={i: i for i in range(2 * n)},
        compiler_params=SPLIT_COPY,
    )(*parts, *slots, send_sems, recv_sems, after)
    return list(out[:n]), list(out[n:])


def _sibling_swap(grads, *, name):
    n = len(grads)
    out_shape = [jax.ShapeDtypeStruct((g.shape[0], g.shape[1] // 2, g.shape[2]), g.dtype)
                 for g in grads]

    def body(*refs):
        g_refs, o_refs = refs[:n], refs[n:2 * n]
        send_sems, recv_sems = refs[2 * n:]
        x, y, c, _ = _place()
        cps = []
        for i in range(n):
            rh = grads[i].shape[1] // 2
            src = g_refs[i].at[:, pl.ds(pl.multiple_of((1 - c) * rh, 16), rh), :]
            cp = pltpu.make_async_remote_copy(
                src_ref=src, dst_ref=o_refs[i], send_sem=send_sems.at[i], recv_sem=recv_sems.at[i],
                device_id=(x, y, 1 - c), device_id_type=MESH)
            cp.start()
            cps.append(cp)
        for cp in cps:
            cp.wait()

    return pl.pallas_call(
        body, in_specs=[ANY] * n, out_specs=[ANY] * n, out_shape=out_shape,
        scratch_shapes=[pltpu.SemaphoreType.DMA((n,)), pltpu.SemaphoreType.DMA((n,))],
        name=name)(*grads)


def _swap_copies(g_refs, r_refs, shapes, send_sems, recv_sems):
    x, y, c, _ = _place()
    cps = []
    for i, shape in enumerate(shapes):
        rh = shape[1] // 2
        src = g_refs[i].at[:, pl.ds(pl.multiple_of((1 - c) * rh, 16), rh), :]
        cps.append(pltpu.make_async_remote_copy(
            src_ref=src, dst_ref=r_refs[i], send_sem=send_sems.at[i], recv_sem=recv_sems.at[i],
            device_id=(x, y, 1 - c), device_id_type=MESH))
    return cps


def _sibling_swap_start(grads, *, name):
    n = len(grads)
    shapes = [g.shape for g in grads]
    lands = [pltpu.HBM((s[0], s[1] // 2, s[2]), g.dtype) for s, g in zip(shapes, grads)]

    def body(*refs):
        g_refs, r_refs = refs[:n], refs[n:2 * n]
        token = refs[4 * n + 2]
        for cp in _swap_copies(g_refs, r_refs, shapes, refs[2 * n], refs[2 * n + 1]):
            cp.start()
        token[...] = jnp.zeros_like(token)

    land = [pltpu.with_memory_space_constraint(lax.empty(s.inner_aval.shape, s.inner_aval.dtype), pltpu.HBM)
            for s in lands]
    out = pl.pallas_call(
        body, name=name,
        out_shape=(pltpu.SemaphoreType.DMA((n,)), pltpu.SemaphoreType.DMA((n,)),
                   *[pltpu.HBM(g.shape, g.dtype) for g in grads], *lands, TOKEN),
        in_specs=[HBM_SPEC] * (2 * n),
        out_specs=(SEM_SPEC, SEM_SPEC, *[HBM_SPEC] * (2 * n), pl.BlockSpec(memory_space=pltpu.VMEM)),
        input_output_aliases={i: 2 + i for i in range(2 * n)},
        compiler_params=SPLIT_COPY,
    )(*[pltpu.with_memory_space_constraint(g, pltpu.HBM) for g in grads], *land)
    return out[0], out[1], list(out[2:2 + n]), list(out[2 + n:2 + 2 * n]), out[2 + 2 * n]


def _sibling_swap_wait(send_sems, recv_sems, grads, lands, after, *, name):
    n = len(grads)
    shapes = [g.shape for g in grads]

    def body(*refs):
        g_refs, r_refs = refs[:n], refs[n:2 * n]
        for cp in _swap_copies(g_refs, r_refs, shapes, refs[2 * n], refs[2 * n + 1]):
            cp.wait_send()
            cp.wait_recv()

    out = pl.pallas_call(
        body, name=name,
        out_shape=[pltpu.HBM(a.shape, a.dtype) for a in (*grads, *lands)],
        in_specs=[HBM_SPEC] * (2 * n) + [SEM_SPEC, SEM_SPEC, ANY],
        out_specs=[HBM_SPEC] * (2 * n),
        input_output_aliases={i: i for i in range(2 * n)},
        compiler_params=SPLIT_COPY,
    )(*grads, *lands, send_sems, recv_sems, after)
    return list(out[:n]), list(out[n:])


def _pair_add(g3s, rxs, place, *, out_dtype, name):
    n = len(g3s)
    steps, trs = _group_tiles([g.shape[1] // 2 for g in g3s], 16)

    def body(p_ref, *refs):
        for i in range(n):
            refs[2 * n + i][...] = (refs[2 * i][...] + refs[2 * i + 1][...]).astype(out_dtype)

    in_specs, out_specs = [], []
    for g, tr in zip(g3s, trs):
        blk = (g.shape[0], tr, g.shape[2])
        in_specs += [pl.BlockSpec(blk, lambda i, p_ref: (0, p_ref[1] * steps + i, 0)),
                     pl.BlockSpec(blk, lambda i, p_ref: (0, i, 0))]
        out_specs.append(pl.BlockSpec(blk, lambda i, p_ref: (0, i, 0)))
    return pl.pallas_call(
        body,
        grid_spec=pltpu.PrefetchScalarGridSpec(
            num_scalar_prefetch=1, grid=(steps,), in_specs=in_specs, out_specs=out_specs),
        out_shape=[jax.ShapeDtypeStruct((g.shape[0], g.shape[1] // 2, g.shape[2]), out_dtype)
                   for g in g3s],
        compiler_params=_cp("parallel"), name=name)(place, *[a for q in zip(g3s, rxs) for a in q])


def _sum_slots(q, *, name):
    ns, rows, cols = q.shape
    tr = next(t for t in (128, 64, 32, 16, 8) if rows % t == 0)

    def body(q_ref, o_ref):
        acc = q_ref[0].astype(F32)
        for k in range(1, ns):
            acc = acc + q_ref[k].astype(F32)
        o_ref[...] = acc

    return pl.pallas_call(
        body, grid=(rows // tr,),
        in_specs=[pl.BlockSpec((ns, tr, cols), lambda i: (0, i, 0))],
        out_specs=_rspec(tr, cols),
        out_shape=jax.ShapeDtypeStruct((rows, cols), F32),
        compiler_params=_cp("parallel"), name=name)(q)


def _sum_chips(qs, ps, place, axes, *, name):
    n = len(qs)
    per = N_CHIPS + 1
    steps, trs = _group_tiles([q.shape[1] for q in qs], 16)

    def body(p_ref, *refs):
        chip = p_ref[0]
        for i in range(n):
            q_refs, own_ref = refs[per * i:per * i + N_CHIPS], refs[per * i + N_CHIPS]
            acc = jnp.where(chip == 0, own_ref[...], q_refs[0][...]).astype(F32)
            for k in range(1, N_CHIPS):
                acc = acc + jnp.where(chip == k, own_ref[...], q_refs[k][...]).astype(F32)
            refs[per * n + i][...] = acc

    def slot_spec(k, tr, cw):
        return pl.BlockSpec((None, tr, cw),
                            lambda i, p_ref: (jnp.where(p_ref[0] == k, (k + 1) % N_CHIPS, k), i, 0))

    in_specs, out_specs, operands = [], [], []
    for q, p, ax, tr in zip(qs, ps, axes, trs):
        cw = q.shape[2]
        in_specs += [slot_spec(k, tr, cw) for k in range(N_CHIPS)]
        in_specs.append(pl.BlockSpec((None, tr, cw), (lambda i, p_ref: (0, i, p_ref[0])) if ax == 1
                                     else (lambda i, p_ref: (p_ref[0], i, 0))))
        out_specs.append(pl.BlockSpec((tr, cw), lambda i, p_ref: (p_ref[1] * steps + i, 0)))
        operands += [q] * N_CHIPS + [p]
    return pl.pallas_call(
        body,
        grid_spec=pltpu.PrefetchScalarGridSpec(
            num_scalar_prefetch=1, grid=(steps,), in_specs=in_specs, out_specs=out_specs),
        out_shape=[jax.ShapeDtypeStruct((2 * q.shape[1], q.shape[2]), F32) for q in qs],
        compiler_params=_cp("parallel"), name=name)(place, *operands)


def _sibling_share(shards, *, name):
    n = len(shards)

    def body(*refs):
        o_refs = refs[n:2 * n]
        send_sems, recv_sems = refs[2 * n:]
        x, y, c, _ = _place()
        cps = []
        for i in range(n):
            rh = shards[i].shape[0] // 2
            mine = o_refs[i].at[pl.ds(pl.multiple_of(c * rh, 8), rh), :]
            cp = pltpu.make_async_remote_copy(
                src_ref=mine, dst_ref=mine, send_sem=send_sems.at[i], recv_sem=recv_sems.at[i],
                device_id=(x, y, 1 - c), device_id_type=MESH)
            cp.start()
            cps.append(cp)
        for i in range(n):
            rh = shards[i].shape[0] // 2
            theirs = o_refs[i].at[pl.ds(pl.multiple_of((1 - c) * rh, 8), rh), :]
            pltpu.make_async_remote_copy(
                src_ref=theirs, dst_ref=theirs, send_sem=send_sems.at[i], recv_sem=recv_sems.at[i],
                device_id=(x, y, c), device_id_type=MESH).wait_recv()
        for cp in cps:
            cp.wait_send()

    return pl.pallas_call(
        body, in_specs=[ANY] * n, out_specs=[ANY] * n,
        out_shape=[jax.ShapeDtypeStruct(h.shape, h.dtype) for h in shards],
        input_output_aliases={i: i for i in range(n)},
        scratch_shapes=[pltpu.SemaphoreType.DMA((n,)), pltpu.SemaphoreType.DMA((n,))],
        name=name)(*shards)


def _gather_all(blk, *, name, after=None):
    rows, cols = blk.shape
    extra = [] if after is None else [after]

    def body(x_ref, *refs):
        out_ref, send_sems, recv_sems, local_sem = refs[len(extra):]
        x, y, c = lax.axis_index("x"), lax.axis_index("y"), lax.axis_index("c")
        me = 4 * x + 2 * y + c
        mine = pltpu.make_async_copy(x_ref, out_ref.at[me], local_sem)
        mine.start()
        cps = []
        for k in range(1, N_DEV):
            tx = (1 - x) if (k >> 2) & 1 else x
            ty = (1 - y) if (k >> 1) & 1 else y
            tc = (1 - c) if k & 1 else c
            cp = pltpu.make_async_remote_copy(
                src_ref=x_ref, dst_ref=out_ref.at[me], send_sem=send_sems.at[k - 1],
                recv_sem=recv_sems.at[k - 1], device_id=(tx, ty, tc), device_id_type=MESH)
            cp.start()
            cps.append(cp)
        for k in range(1, N_DEV):
            tx = (1 - x) if (k >> 2) & 1 else x
            ty = (1 - y) if (k >> 1) & 1 else y
            tc = (1 - c) if k & 1 else c
            got = out_ref.at[4 * tx + 2 * ty + tc]
            pltpu.make_async_remote_copy(
                src_ref=got, dst_ref=got, send_sem=send_sems.at[k - 1], recv_sem=recv_sems.at[k - 1],
                device_id=(x, y, c), device_id_type=MESH).wait_recv()
        for cp in cps:
            cp.wait_send()
        mine.wait()

    vm = pl.BlockSpec(memory_space=pltpu.VMEM)
    return pl.pallas_call(
        body, in_specs=[vm] + [ANY] * len(extra), out_specs=vm,
        out_shape=jax.ShapeDtypeStruct((N_DEV, rows, cols), blk.dtype),
        scratch_shapes=[pltpu.SemaphoreType.DMA((N_DEV - 1,)), pltpu.SemaphoreType.DMA((N_DEV - 1,)),
                        pltpu.SemaphoreType.DMA],
        name=name)(blk, *extra)


def _as_rows(a):
    flat = a.reshape(-1)
    n = flat.shape[0]
    rows = -(-n // (8 * LANES)) * 8
    return jnp.pad(flat, (0, rows * LANES - n)).reshape(rows, LANES)


def _from_rows(p, shape):
    n = int(np.prod(shape))
    return p.reshape(-1)[:n].reshape(shape)


WEIGHT_AXES = (1, 1, 1, 0, 1, 0)
WIRE = BF16


def kernel(x, meta_tokens, w_in, w_na_out, w_hg_out, w_o, w_up, w_down, norm_mix, norm_mlp, norm_final, hg_norm, na_rpb, hg_lb_logits, loss_target, m_meta_tokens, m_w_in, m_w_na_out, m_w_hg_out, m_w_o, m_w_up, m_w_down, m_norm_mix, m_norm_mlp, m_norm_final, m_hg_norm, m_na_rpb, m_hg_lb_logits, v_meta_tokens, v_w_in, v_w_na_out, v_w_hg_out, v_w_o, v_w_up, v_w_down, v_norm_mix, v_norm_mlp, v_norm_final, v_hg_norm, v_na_rpb, v_hg_lb_logits):
    xi, yi, ci = lax.axis_index("x"), lax.axis_index("y"), lax.axis_index("c")
    chip = 2 * xi + yi
    d = x.shape[-1]
    dshard = meta_tokens.shape[1]
    hgw = hg_norm.shape[1]
    lbs = hg_lb_logits.shape[2]
    big = [w_in[0], w_na_out[0], w_hg_out[0], w_o[0], w_up[0], w_down[0]]
    big_m = [m_w_in[0], m_w_na_out[0], m_w_hg_out[0], m_w_o[0], m_w_up[0], m_w_down[0]]
    big_v = [v_w_in[0], v_w_na_out[0], v_w_hg_out[0], v_w_o[0], v_w_up[0], v_w_down[0]]

    place = jnp.stack([chip, ci]).astype(jnp.int32)
    own_w = _cast_into_full(big, WEIGHT_AXES, place, name="cast_shards")
    in_axes, rest_axes = WEIGHT_AXES[:1], WEIGHT_AXES[1:]
    small_in = jnp.concatenate([_as_rows(meta_tokens), _as_rows(hg_lb_logits)], axis=0)
    sm_send, sm_recv, sm_blk, sm_land, sm_token = _chip_exchange_start(small_in,
                                                                       name="small_params_start")
    in_send, in_recv, in_bufs, in_token = _allgather_start(own_w[:1], in_axes, sm_token,
                                                           name="weight_allgather_in_start")
    ag_send, ag_recv, ag_bufs, ag_token = _allgather_start(own_w[1:], rest_axes, in_token,
                                                           name="weight_allgather_rest_start")
    sm_land = _chip_exchange_wait(sm_send, sm_recv, sm_blk, sm_land, ag_token,
                                  name="small_params_wait")
    small_all = lax.dynamic_update_slice(sm_land, small_in[None], (chip, 0, 0))
    forward = {}

    def first_weight(after):
        got = _allgather_wait(in_send, in_recv, in_bufs, in_axes, after,
                              name="weight_allgather_in_wait")
        return _allgather_forward(got, in_axes, name="weight_allgather_in_forward")[0]

    def rest_landed(after):
        got = _allgather_wait(ag_send, ag_recv, ag_bufs, rest_axes, after,
                              name="weight_allgather_rest_wait")
        send, recv, bufs, token = _allgather_forward_start(
            got, rest_axes, name="weight_allgather_rest_forward_start")
        forward["rest"] = (send, recv, bufs)
        return token

    def rest_weights(after):
        return _allgather_forward_wait(*forward["rest"], rest_axes, after,
                                       name="weight_allgather_rest_forward_wait")

    n_meta_rows = N_META * dshard // LANES
    meta_full = (small_all[:, :n_meta_rows].reshape(N_CHIPS, N_META, dshard)
                 .transpose(1, 0, 2).reshape(N_META, d))
    lbl_full = (small_all[:, n_meta_rows:].reshape(N_CHIPS, -1)[:, :4 * lbs]
                .reshape(N_CHIPS, 2, 2, lbs).transpose(1, 2, 0, 3).reshape(2, 2, N_CHIPS * lbs))
    lb = jax.nn.softmax(lbl_full, axis=1)[:, 0]

    def by_chip(dws, axes):
        return [g.reshape(1, *g.shape) if ax == 1
                else g.reshape(N_CHIPS, g.shape[0] // N_CHIPS, g.shape[1]) for g, ax in zip(dws, axes)]

    flying = {}

    def scatter(tag, axes, g3, rx):
        parts = _pair_add(g3, rx, place, out_dtype=WIRE, name=f"grad_pair_add_{tag}")
        send, recv, parts, slots, token = _scatter_start(parts, axes,
                                                         name=f"grad_scatter_{tag}_start")
        flying[tag] = (send, recv, parts, slots)
        return token

    def swap_rest(dws):
        send, recv, g3, lands, token = _sibling_swap_start(by_chip(dws, rest_axes),
                                                           name="grad_sibling_swap_rest_start")
        flying["swap"] = (send, recv, g3, lands)
        return token

    def scatter_rest(after):
        g3, rx = _sibling_swap_wait(*flying["swap"], after, name="grad_sibling_swap_rest_wait")
        return scatter("rest", rest_axes, g3, rx)

    def scatter_in(dw_in):
        g3 = by_chip([dw_in], in_axes)
        return scatter("in", in_axes, g3, _sibling_swap(g3, name="grad_sibling_swap_in"))

    def landed(tag, axes, after):
        return _scatter_wait(*flying[tag], axes, after, name=f"grad_scatter_{tag}_wait")

    (loss, dx, dmeta, *_, dg_mix, dg_mlp, dg_fin, d_gain, d_rpb, d_lb) = _local_step(
        x[0], loss_target[0], meta_full, first_weight, rest_weights, norm_mix, norm_mlp,
        norm_final.reshape(1, d), hg_norm, na_rpb[0], lb, swap_rest, scatter_rest, scatter_in,
        rest_landed)

    parts_rest, slots_rest = landed("rest", rest_axes, dx)
    parts_in, slots_in = landed("in", in_axes, slots_rest[-1])
    halves = _sum_chips(slots_in + slots_rest, parts_in + parts_rest, place, WEIGHT_AXES,
                        name="grad_sum_chips")
    g_big = _sibling_share(halves, name="grad_sibling_share")

    d_rpb_c = d_rpb[:, :2 * NA_WIN_W - 1]
    small_g = [dmeta, dg_mix, dg_mlp, dg_fin, d_gain, d_rpb_c, d_lb, loss]
    packed = jnp.concatenate([_as_rows(a) for a in small_g], axis=0)
    total = _sum_slots(_gather_all(packed, after=halves[0], name="gather_small_grads"),
                       name="sum_small_grads")
    offs = np.cumsum([0] + [_as_rows(a).shape[0] for a in small_g])
    take = lambda i, shape: _from_rows(total[offs[i]:offs[i + 1]], shape)
    g_meta_full = take(0, (N_META, d))
    g_norm_mix, g_norm_mlp = take(1, (1, d)), take(2, (1, d))
    g_norm_final = take(3, (d,))
    g_hg_norm = take(4, (1, hgw))
    g_rpb = take(5, na_rpb.shape)
    g_lb = take(6, (2, hgw))
    loss_total = take(7, (1, LANES))[0, 0]
    g_meta = lax.dynamic_slice_in_dim(g_meta_full, chip * dshard, dshard, axis=1)
    dl0 = lb * (1.0 - lb) * g_lb
    g_lbl_full = jnp.stack([dl0, -dl0], axis=1)
    g_lbl = lax.dynamic_slice_in_dim(g_lbl_full, chip * lbs, lbs, axis=2)

    big_out = _adamw(big, g_big, big_m, big_v, name="adamw_big")
    small_w = [meta_tokens, norm_mix, norm_mlp, norm_final, hg_norm, na_rpb, hg_lb_logits]
    small_gr = [g_meta, g_norm_mix, g_norm_mlp, g_norm_final, g_hg_norm, g_rpb, g_lbl]
    small_m = [m_meta_tokens, m_norm_mix, m_norm_mlp, m_norm_final, m_hg_norm, m_na_rpb, m_hg_lb_logits]
    small_v = [v_meta_tokens, v_norm_mix, v_norm_mlp, v_norm_final, v_hg_norm, v_na_rpb, v_hg_lb_logits]
    pk = lambda lst: jnp.concatenate([_as_rows(a) for a in lst], axis=0)
    ((sd, sm, sv),) = _adamw([pk(small_w)], [pk(small_gr)], [pk(small_m)], [pk(small_v)],
                             name="adamw_small")
    soffs = np.cumsum([0] + [_as_rows(a).shape[0] for a in small_w])
    unpk = lambda p: [_from_rows(p[soffs[i]:soffs[i + 1]], small_w[i].shape) for i in range(len(small_w))]
    sd, sm, sv = unpk(sd), unpk(sm), unpk(sv)

    def order(bigs, smalls):
        return [smalls[0]] + [b.reshape(1, *b.shape) for b in bigs] + smalls[1:]

    grads = order(g_big, small_gr)
    deltas = order([o[0] for o in big_out], sd)
    new_m = order([o[1] for o in big_out], sm)
    new_v = order([o[2] for o in big_out], sv)
    return (loss_total, dx.reshape(1, *dx.shape), *grads, *deltas, *new_m, *new_v)
```

```python
import functools

import numpy as np
import jax
import jax.numpy as jnp
from jax import lax
from jax.experimental import pallas as pl
from jax.experimental.pallas import tpu as pltpu

F32 = jnp.float32
BF16 = jnp.bfloat16
HIGHEST = lax.Precision.HIGHEST

GRID_W = 64
N_META = 16
EPS = 1e-6
NA_HEAD_DIM = 64
NA_WIN_H = 8
NA_WIN_W = 16
HG_DK = 128
HG_CHUNK = 16
LANES = 128
ROW_ALIGN = 128
VMEM_LIMIT = 48 * 1024 * 1024

ADAM_LR = 0.001
ADAM_B1 = 0.9
ADAM_B2 = 0.999
ADAM_EPS = 1e-08
ADAM_WD = 0.01
ADAM_STEP = 10

MESH = pl.DeviceIdType.MESH


def _cp(*sem):
    return pltpu.CompilerParams(dimension_semantics=sem, vmem_limit_bytes=VMEM_LIMIT)


def _sigmoid(x):
    return 0.5 * jnp.tanh(0.5 * x) + 0.5


def _dot(a, b, dims, precision=None):
    return lax.dot_general(a, b, (dims, ((), ())), preferred_element_type=F32, precision=precision)


def _nn(a, b, **kw):
    return _dot(a, b, ((1,), (0,)), **kw)


def _nt(a, b, **kw):
    return _dot(a, b, ((1,), (1,)), **kw)


def _tn(a, b, **kw):
    return _dot(a, b, ((0,), (0,)), **kw)


def _matmul(a, b, *, ta=False, tb=False, tm=None, tn=None, tk=None, out_dtype=F32, name,
            precision=None, after=None, epilogue=None, tiles=(), out_dtypes=None):
    extra = [] if after is None else [after]
    single = out_dtypes is None
    if single:
        out_dtypes = (out_dtype,)
    n_t, n_o = len(tiles), len(out_dtypes)
    if ta:
        kdim, m = a.shape
    else:
        m, kdim = a.shape
    if tb:
        n, k2 = b.shape
    else:
        k2, n = b.shape
    assert kdim == k2, (a.shape, b.shape, ta, tb)
    if tm is None:
        if ta:
            tm = next(t for t in (1024, 512, 256, 128, m) if m % t == 0)
        else:
            tm = m // 2 if (m // 2) % 16 == 0 and m > 512 else m
    if tn is None:
        wide = (1024,) if not ta and len(tiles) <= 1 else ()
        tn = next(t for t in (*wide, 512, 256, 128, n) if n % t == 0)
    if tk is None:
        tk = kdim if ta else next(t for t in (1024, 512, 256, 128, kdim) if kdim % t == 0)
    assert m % tm == 0 and n % tn == 0 and kdim % tk == 0, (m, n, kdim, tm, tn, tk)
    nk = kdim // tk
    op_dtype = F32 if precision is not None else BF16

    def body(a_ref, b_ref, *refs):
        t_refs = refs[:n_t]
        o_refs = refs[n_t + len(extra):n_t + len(extra) + n_o]
        av = a_ref[...].astype(op_dtype)
        bv = b_ref[...].astype(op_dtype)
        dims = ((0 if ta else 1,), (1 if tb else 0,))
        part = _dot(av, bv, dims, precision=precision)

        def finish(acc):
            outs = (acc,) if epilogue is None else epilogue(acc, *[t[...] for t in t_refs])
            for o_ref, val in zip(o_refs, outs):
                o_ref[...] = val.astype(o_ref.dtype)

        if nk == 1:
            finish(part)
            return
        acc_ref = refs[-1]
        kk = pl.program_id(2)

        @pl.when(kk == 0)
        def _():
            acc_ref[...] = part

        @pl.when((kk > 0) & (kk < nk - 1))
        def _():
            acc_ref[...] += part

        @pl.when(kk == nk - 1)
        def _():
            finish(acc_ref[...] + part)

    a_spec = (pl.BlockSpec((tk, tm), lambda i, j, k: (k, i)) if ta
              else pl.BlockSpec((tm, tk), lambda i, j, k: (i, k)))
    b_spec = (pl.BlockSpec((tn, tk), lambda i, j, k: (j, k)) if tb
              else pl.BlockSpec((tk, tn), lambda i, j, k: (k, j)))
    for _, off in tiles:
        assert off % tn == 0, (off, tn)
    t_specs = [pl.BlockSpec((tm, tn), functools.partial(lambda i, j, k, o: (i, o + j), o=off // tn))
               for _, off in tiles]
    o_spec = pl.BlockSpec((tm, tn), lambda i, j, k: (i, j))
    outs = pl.pallas_call(
        body,
        grid=(m // tm, n // tn, nk),
        in_specs=[a_spec, b_spec] + t_specs + [pl.BlockSpec(memory_space=pl.ANY)] * len(extra),
        out_specs=[o_spec] * n_o,
        out_shape=[jax.ShapeDtypeStruct((m, n), dt) for dt in out_dtypes],
        scratch_shapes=[pltpu.VMEM((tm, tn), F32)] if nk > 1 else [],
        compiler_params=_cp("parallel", "parallel", "arbitrary"),
        name=name,
    )(a, b, *[t for t, _ in tiles], *extra)
    return outs[0] if single else outs


def _rspec(tr, w, cb=0):
    return pl.BlockSpec((tr, w), lambda i: (i, cb))


def _fspec(shape):
    nd = len(shape)
    return pl.BlockSpec(shape, lambda i: (0,) * nd)


ROW_VMEM_BUDGET = 20 * 1024 * 1024
ROW_MIN_STEPS = 4


def _row_tile(lp, row_bytes):
    for k in range(ROW_MIN_STEPS, lp // 16 + 1):
        tr = lp // k
        if lp % k == 0 and tr % 16 == 0 and 2 * tr * row_bytes <= ROW_VMEM_BUDGET:
            return tr
    return lp


def _rmsnorm_fwd(x, g, *, name):
    lp, d = x.shape
    tr = _row_tile(lp, d * (4 + 2))

    def body(x_ref, g_ref, o_ref):
        xv = x_ref[...]
        r = lax.rsqrt(jnp.mean(xv * xv, axis=-1, keepdims=True) + EPS)
        o_ref[...] = (xv * r * g_ref[...]).astype(BF16)

    return pl.pallas_call(
        body, grid=(lp // tr,),
        in_specs=[_rspec(tr, d), _fspec((1, d))],
        out_specs=_rspec(tr, d),
        out_shape=jax.ShapeDtypeStruct((lp, d), BF16),
        compiler_params=_cp("parallel"), name=name)(x, g)


def _residual_norm(h, t, g, *, name):
    lp, d = h.shape
    tr = _row_tile(lp, d * (4 + 4 + 4 + 2))

    def body(h_ref, t_ref, g_ref, h1_ref, m_ref):
        xv = h_ref[...] + t_ref[...]
        h1_ref[...] = xv
        r = lax.rsqrt(jnp.mean(xv * xv, axis=-1, keepdims=True) + EPS)
        m_ref[...] = (xv * r * g_ref[...]).astype(BF16)

    return pl.pallas_call(
        body, grid=(lp // tr,),
        in_specs=[_rspec(tr, d), _rspec(tr, d), _fspec((1, d))],
        out_specs=[_rspec(tr, d), _rspec(tr, d)],
        out_shape=[jax.ShapeDtypeStruct((lp, d), F32), jax.ShapeDtypeStruct((lp, d), BF16)],
        compiler_params=_cp("parallel"), name=name)(h, t, g)


def _rmsnorm_bwd_add(x, g, dy, dres, *, name):
    lp, d = x.shape
    tr = _row_tile(lp, d * 4 * 4)

    def body(x_ref, g_ref, dy_ref, dr_ref, dx_ref, dg_ref):
        @pl.when(pl.program_id(0) == 0)
        def _():
            dg_ref[...] = jnp.zeros_like(dg_ref)

        xv = x_ref[...]
        r = lax.rsqrt(jnp.mean(xv * xv, axis=-1, keepdims=True) + EPS)
        xh = xv * r
        dyv = dy_ref[...]
        dg_ref[...] += jnp.sum(dyv * xh, axis=0, keepdims=True)
        dxh = dyv * g_ref[...]
        dx_ref[...] = dr_ref[...] + r * (dxh - xh * jnp.mean(dxh * xh, axis=-1, keepdims=True))

    return pl.pallas_call(
        body, grid=(lp // tr,),
        in_specs=[_rspec(tr, d), _fspec((1, d)), _rspec(tr, d), _rspec(tr, d)],
        out_specs=[_rspec(tr, d), _fspec((1, d))],
        out_shape=[jax.ShapeDtypeStruct((lp, d), F32), jax.ShapeDtypeStruct((1, d), F32)],
        compiler_params=_cp("arbitrary"), name=name)(x, g, dy, dres)


def _final_loss(h1, t2, g, tgt, *, n_tok, name):
    lp, d = h1.shape
    tr = _row_tile(lp, d * 4 * 4)

    def body(h_ref, t_ref, g_ref, tg_ref, dh_ref, loss_ref, dg_ref):
        i = pl.program_id(0)

        @pl.when(i == 0)
        def _():
            loss_ref[...] = jnp.zeros_like(loss_ref)
            dg_ref[...] = jnp.zeros_like(dg_ref)

        xv = h_ref[...] + t_ref[...]
        r = lax.rsqrt(jnp.mean(xv * xv, axis=-1, keepdims=True) + EPS)
        xh = xv * r
        gv = g_ref[...]
        row = i * tr + lax.broadcasted_iota(jnp.int32, (tr, 1), 0)
        valid = (row >= N_META) & (row < N_META + n_tok)
        err = jnp.where(valid, xh * gv - tg_ref[...], 0.0)
        loss_ref[...] += jnp.sum(0.5 * err * err) / d
        dy = err / d
        dg_ref[...] += jnp.sum(dy * xh, axis=0, keepdims=True)
        dxh = dy * gv
        dh_ref[...] = r * (dxh - xh * jnp.mean(dxh * xh, axis=-1, keepdims=True))

    return pl.pallas_call(
        body, grid=(lp // tr,),
        in_specs=[_rspec(tr, d), _rspec(tr, d), _fspec((1, d)), _rspec(tr, d)],
        out_specs=[_rspec(tr, d), _fspec((1, LANES)), _fspec((1, d))],
        out_shape=[jax.ShapeDtypeStruct((lp, d), F32), jax.ShapeDtypeStruct((1, LANES), F32),
                   jax.ShapeDtypeStruct((1, d), F32)],
        compiler_params=_cp("arbitrary"), name=name)(h1, t2, g, tgt)


def _hg_out(o_f, o_b, proj, gain, *, col_g, name):
    lp, w = o_f.shape
    tr = _row_tile(lp, w * (3 * 4 + 2))
    hh = w // HG_DK

    def body(of_ref, ob_ref, g_ref, gain_ref, y_ref):
        gv = g_ref[...]
        sg = gv * _sigmoid(gv)
        for h in range(hh):
            sl = slice(h * HG_DK, (h + 1) * HG_DK)
            o = of_ref[:, sl] + ob_ref[:, sl]
            r = lax.rsqrt(jnp.mean(o * o, axis=-1, keepdims=True) + EPS)
            y_ref[:, sl] = (o * r * gain_ref[:, sl] * sg[:, sl]).astype(BF16)

    return pl.pallas_call(
        body, grid=(lp // tr,),
        in_specs=[_rspec(tr, w), _rspec(tr, w), _rspec(tr, w, col_g // w), _fspec((1, w))],
        out_specs=_rspec(tr, w),
        out_shape=jax.ShapeDtypeStruct((lp, w), BF16),
        compiler_params=_cp("parallel"), name=name)(o_f, o_b, proj, gain)


def _hg_out_bwd(o_f, o_b, proj, gain, dy, *, col_g, name):
    lp, w = o_f.shape
    tr = _row_tile(lp, w * (5 * 4 + 2))
    hh = w // HG_DK

    def body(of_ref, ob_ref, g_ref, gain_ref, dy_ref, do_ref, dg_ref, dgain_ref):
        @pl.when(pl.program_id(0) == 0)
        def _():
            dgain_ref[...] = jnp.zeros_like(dgain_ref)

        for h in range(hh):
            sl = slice(h * HG_DK, (h + 1) * HG_DK)
            gv = g_ref[:, sl]
            s = _sigmoid(gv)
            sg = gv * s
            dsg = s + gv * s * (1.0 - s)
            o = of_ref[:, sl] + ob_ref[:, sl]
            r = lax.rsqrt(jnp.mean(o * o, axis=-1, keepdims=True) + EPS)
            on = o * r
            dyv = dy_ref[:, sl]
            gn = gain_ref[:, sl]
            dgain_ref[:, sl] += jnp.sum(dyv * on * sg, axis=0, keepdims=True)
            dg_ref[:, sl] = (dyv * on * gn * dsg).astype(BF16)
            don = dyv * gn * sg
            do_ref[:, sl] = r * (don - on * jnp.mean(don * on, axis=-1, keepdims=True))

    return pl.pallas_call(
        body, grid=(lp // tr,),
        in_specs=[_rspec(tr, w), _rspec(tr, w), _rspec(tr, w, col_g // w), _fspec((1, w)),
                  _rspec(tr, w)],
        out_specs=[_rspec(tr, w), _rspec(tr, w), _fspec((1, w))],
        out_shape=[jax.ShapeDtypeStruct((lp, w), F32), jax.ShapeDtypeStruct((lp, w), BF16),
                   jax.ShapeDtypeStruct((1, w), F32)],
        compiler_params=_cp("arbitrary"), name=name)(o_f, o_b, proj, gain, dy)


HG_ROWS = 128
HG_HALVES = (1, 2, 4, 8, 16, 32, 64)


def _hg_gates(zq, z, lbv):
    sq = _sigmoid(zq)
    s = _sigmoid(z)
    f = lbv + (1.0 - lbv) * s
    kk = (1.0 - lbv) * (1.0 - s)
    return zq * sq, sq, s, f, jnp.log(f), kk


def _block_cumsum(g, pos, suffix):
    x = g
    for k in HG_HALVES:
        if suffix:
            x = x + jnp.where(pos < HG_ROWS - k, pltpu.roll(x, HG_ROWS - k, 0), 0.0)
        else:
            x = x + jnp.where(pos >= k, pltpu.roll(x, k, 0), 0.0)
    return x


def _pair_levels(b, pos, reverse):
    out = []
    first = b
    for m in HG_HALVES:
        if m > 1:
            first = jnp.where((pos & (m - 1)) >= m // 2, pltpu.roll(first, m // 2, 0), first)
        nxt = pltpu.roll(first, HG_ROWS - m, 0)
        upper = (pos & (2 * m - 1)) >= m
        if reverse:
            eq = jnp.where(upper, 0.0, jnp.exp(b - nxt))
            ek = jnp.where(upper, jnp.exp(first - b), 0.0)
        else:
            eq = jnp.where(upper, jnp.exp(b - first), 0.0)
            ek = jnp.where(upper, 0.0, jnp.exp(nxt - b))
        out.append((eq, ek))
    return out


def _pair_masks(mask_ref):
    ri = lax.broadcasted_iota(jnp.int32, (HG_ROWS, HG_ROWS), 0)
    ci = lax.broadcasted_iota(jnp.int32, (HG_ROWS, HG_ROWS), 1)
    for i, m in enumerate(HG_HALVES):
        sh = m.bit_length()
        mask_ref[i] = jnp.where((ri >> sh) == (ci >> sh), 1.0, 0.0)


def _hg_scan_fwd(proj, lb, *, reverse, col_q, col_z, col_i, hh, name):
    lp = proj.shape[0]
    n_blocks = lp // HG_ROWS
    last = 0 if reverse else HG_ROWS - 1

    def body(q_ref, z_ref, i_ref, lb_ref, o_ref, st_ref, mask_ref):
        lbv = lb_ref[...]
        pos = lax.broadcasted_iota(jnp.int32, (HG_ROWS, 1), 0)
        ri = lax.broadcasted_iota(jnp.int32, (HG_ROWS, HG_ROWS), 0)
        ci = lax.broadcasted_iota(jnp.int32, (HG_ROWS, HG_ROWS), 1)
        _pair_masks(mask_ref)

        def block(bi, st):
            bb = (n_blocks - 1 - bi) if reverse else bi
            r0 = pl.multiple_of(bb * HG_ROWS, HG_ROWS)
            v16 = i_ref[pl.ds(r0, HG_ROWS), :].astype(BF16)
            qh, _, _, _, g, kk = _hg_gates(q_ref[pl.ds(r0, HG_ROWS), :],
                                           z_ref[pl.ds(r0, HG_ROWS), :], lbv)
            b = _block_cumsum(g, pos, reverse)
            bl = b[last:last + 1, :]
            qe = (qh * jnp.exp(b)).astype(BF16)
            kd = (kk * jnp.exp(bl - b)).astype(BF16)
            a = jnp.where(ri == ci, jnp.sum(qh * kk, axis=1, keepdims=True), 0.0)
            for i, (eq, ek) in enumerate(_pair_levels(b, pos, reverse)):
                a = a + mask_ref[i] * _nt((qh * eq).astype(BF16), (kk * ek).astype(BF16))
            st_ref[bb] = st
            o_ref[pl.ds(r0, HG_ROWS), :] = _nn(a.astype(BF16), v16) + _nt(qe, st.astype(BF16))
            return jnp.exp(bl) * st + _tn(v16, kd)

        lax.fori_loop(0, n_blocks, block, jnp.zeros((HG_DK, HG_DK), F32))

    cspec = lambda col: pl.BlockSpec((lp, HG_DK), lambda h: (0, col // HG_DK + h))
    return pl.pallas_call(
        body, grid=(hh,),
        in_specs=[cspec(col_q), cspec(col_z), cspec(col_i),
                  pl.BlockSpec((None, 1, HG_DK), lambda h: (h, 0, 0))],
        out_specs=[pl.BlockSpec((lp, HG_DK), lambda h: (0, h)),
                   pl.BlockSpec((None, n_blocks, HG_DK, HG_DK), lambda h: (h, 0, 0, 0))],
        out_shape=[jax.ShapeDtypeStruct((lp, hh * HG_DK), F32),
                   jax.ShapeDtypeStruct((hh, n_blocks, HG_DK, HG_DK), F32)],
        scratch_shapes=[pltpu.VMEM((len(HG_HALVES), HG_ROWS, HG_ROWS), F32)],
        compiler_params=_cp("parallel"), name=name)(proj, proj, proj, lb)


def _hg_scan_bwd(proj, lb, states, do, *, reverse, col_q, col_z, col_i, hh, name):
    lp = proj.shape[0]
    n_blocks = lp // HG_ROWS
    last = 0 if reverse else HG_ROWS - 1

    def body(q_ref, z_ref, i_ref, lb_ref, st_ref, do_ref, dq_ref, dz_ref, dv_ref, dlb_ref, mask_ref):
        lbv = lb_ref[...]
        pos = lax.broadcasted_iota(jnp.int32, (HG_ROWS, 1), 0)
        ri = lax.broadcasted_iota(jnp.int32, (HG_ROWS, HG_ROWS), 0)
        ci = lax.broadcasted_iota(jnp.int32, (HG_ROWS, HG_ROWS), 1)
        _pair_masks(mask_ref)

        def block(bi, carry):
            dst, dlb = carry
            bb = bi if reverse else (n_blocks - 1 - bi)
            r0 = pl.multiple_of(bb * HG_ROWS, HG_ROWS)
            zq = q_ref[pl.ds(r0, HG_ROWS), :]
            v16 = i_ref[pl.ds(r0, HG_ROWS), :].astype(BF16)
            do16 = do_ref[pl.ds(r0, HG_ROWS), :].astype(BF16)
            qh, sq, s, f, g, kk = _hg_gates(zq, z_ref[pl.ds(r0, HG_ROWS), :], lbv)
            b = _block_cumsum(g, pos, reverse)
            bl = b[last:last + 1, :]
            eb = jnp.exp(b)
            ebl = jnp.exp(bl - b)
            decay = jnp.exp(bl)
            qe16 = (qh * eb).astype(BF16)
            kd16 = (kk * ebl).astype(BF16)
            st = st_ref[bb]
            st16, dst16 = st.astype(BF16), dst.astype(BF16)
            same_row = ri == ci
            da = _nt(do16, v16)
            da_diag = jnp.sum(jnp.where(same_row, da, 0.0), axis=1, keepdims=True)
            dq_state = eb * _nn(do16, st16)
            dk_state = ebl * _nn(v16, dst16)
            dq = dq_state + da_diag * kk
            dk = dk_state + da_diag * qh
            dbl = (decay * jnp.sum(st * dst, axis=0, keepdims=True)
                   + jnp.sum(kk * dk_state, axis=0, keepdims=True))
            db = qh * dq_state - kk * dk_state + jnp.where(pos == last, dbl, 0.0)
            a = jnp.where(same_row, jnp.sum(qh * kk, axis=1, keepdims=True), 0.0)
            for i, (eq, ek) in enumerate(_pair_levels(b, pos, reverse)):
                same = mask_ref[i]
                q16, k16 = (qh * eq).astype(BF16), (kk * ek).astype(BF16)
                a = a + same * _nt(q16, k16)
                da16 = (same * da).astype(BF16)
                gq, gk = _nn(da16, k16), _tn(da16, q16)
                dq = dq + eq * gq
                dk = dk + ek * gk
                db = db + (q16.astype(F32) * gq - k16.astype(F32) * gk)
            dg = _block_cumsum(db, pos, not reverse)
            df = dg / f - dk
            dq_ref[pl.ds(r0, HG_ROWS), :] = dq * (sq + zq * sq * (1.0 - sq))
            dz_ref[pl.ds(r0, HG_ROWS), :] = df * (1.0 - lbv) * s * (1.0 - s)
            dv_ref[pl.ds(r0, HG_ROWS), :] = _nt(kd16, dst16) + _tn(a.astype(BF16), do16)
            return (decay * dst + _tn(do16, qe16),
                    dlb + jnp.sum(df * (1.0 - s), axis=0, keepdims=True))

        _, dlb = lax.fori_loop(0, n_blocks, block,
                               (jnp.zeros((HG_DK, HG_DK), F32), jnp.zeros((1, HG_DK), F32)))
        dlb_ref[...] = dlb

    cspec = lambda col: pl.BlockSpec((lp, HG_DK), lambda h: (0, col // HG_DK + h))
    ospec = pl.BlockSpec((lp, HG_DK), lambda h: (0, h))
    sds = jax.ShapeDtypeStruct((lp, hh * HG_DK), F32)
    return pl.pallas_call(
        body, grid=(hh,),
        in_specs=[cspec(col_q), cspec(col_z), cspec(col_i),
                  pl.BlockSpec((None, 1, HG_DK), lambda h: (h, 0, 0)),
                  pl.BlockSpec((None, n_blocks, HG_DK, HG_DK), lambda h: (h, 0, 0, 0)),
                  ospec],
        out_specs=[ospec, ospec, ospec, pl.BlockSpec((None, 1, HG_DK), lambda h: (h, 0, 0))],
        out_shape=[sds, sds, sds, jax.ShapeDtypeStruct((hh, 1, HG_DK), F32)],
        scratch_shapes=[pltpu.VMEM((len(HG_HALVES), HG_ROWS, HG_ROWS), F32)],
        compiler_params=_cp("parallel"), name=name)(proj, proj, proj, lb, states, do)


NA_HB = LANES // NA_HEAD_DIM
NA_G = 4
NA_U = NA_G + NA_WIN_H
NA_QN = NA_G * GRID_W
NA_KN = NA_U * GRID_W


def _na_table_index(pattern, a, j):
    if pattern == 0:
        return j - a + NA_WIN_H - 1 if j < NA_WIN_H else None
    if pattern == 2:
        return j - a - 1 if j >= NA_U - NA_WIN_H else None
    return j - a + NA_WIN_H // 2 - 1 if a <= j < a + NA_WIN_H else None


def _na_step_rows(pattern, t, rows):
    if pattern == 0:
        r0, us = 0, 0
    elif pattern == 2:
        r0, us = rows - NA_G, rows - NA_U
    else:
        r0 = NA_G * t
        us = r0 - NA_WIN_H // 2
    q0, k0 = N_META + GRID_W * r0, N_META + GRID_W * us
    if pattern == 1:
        q0, k0 = pl.multiple_of(q0, 16), pl.multiple_of(k0, 16)
    return q0, k0


def _na_fill_bias(tb_ref, bias_ref):
    neg = jnp.full((GRID_W, GRID_W), -1e30, F32)
    for h in range(NA_HB):
        for pattern in range(3):
            for a in range(NA_G):
                for j in range(NA_U):
                    idx = _na_table_index(pattern, a, j)
                    bias_ref[h, pattern, a * GRID_W:(a + 1) * GRID_W, j * GRID_W:(j + 1) * GRID_W] = (
                        neg if idx is None else tb_ref[h, idx])


def _na_steps(rows, step, carry):
    n_steps = rows // NA_G
    carry = step(0, 0, carry)
    carry = lax.fori_loop(1, n_steps - 1, functools.partial(step, 1), carry)
    return step(2, n_steps - 1, carry)


def _na_head_lanes():
    lane = lax.broadcasted_iota(jnp.int32, (1, LANES), 1)
    return [lane // NA_HEAD_DIM == h for h in range(NA_HB)]


def _na_only(mask, x):
    return jnp.where(mask, x, jnp.zeros_like(x))


def _na_fwd(proj, tb, *, n_tok, nh, name):
    lp = proj.shape[0]
    dh, hb = NA_HEAD_DIM, NA_HB
    naw = nh * dh
    rows = n_tok // GRID_W
    scale = dh ** -0.5

    def body(q_ref, k_ref, v_ref, tb_ref, o_ref, lse_ref, q16_ref, k16_ref, v16_ref, bias_ref):
        o_ref[...] = jnp.zeros_like(o_ref)
        lse_ref[...] = jnp.zeros_like(lse_ref)
        q16_ref[...] = q_ref[...].astype(BF16)
        k16_ref[...] = k_ref[...].astype(BF16)
        v16_ref[...] = v_ref[...].astype(BF16)
        _na_fill_bias(tb_ref, bias_ref)
        heads = _na_head_lanes()
        km = k16_ref[0:N_META, :]
        vm = v16_ref[0:N_META, :]
        qm = q16_ref[0:N_META, :]
        o_m = None
        for h in range(hb):
            s = _nt(_na_only(heads[h], qm), km) * scale
            m = jnp.max(s, axis=1, keepdims=True)
            p = jnp.exp(s - m)
            l = jnp.sum(p, axis=1, keepdims=True)
            o_h = _nn(p.astype(BF16), vm) / l
            o_m = o_h if o_m is None else jnp.where(heads[h], o_h, o_m)
            lse_ref[h, 0:N_META, :] = m + jnp.log(l)
        o_ref[0:N_META, :] = o_m

        def step(pattern, t, carry):
            q0, k0 = _na_step_rows(pattern, t, rows)
            q16 = q16_ref[pl.ds(q0, NA_QN), :]
            k16 = k16_ref[pl.ds(k0, NA_KN), :]
            v16 = v16_ref[pl.ds(k0, NA_KN), :]
            o = None
            for h in range(hb):
                q_h = _na_only(heads[h], q16)
                s = _nt(q_h, k16) * scale + bias_ref[h, pattern]
                sm = _nt(q_h, km) * scale
                m = jnp.maximum(jnp.max(s, axis=1, keepdims=True),
                                jnp.max(sm, axis=1, keepdims=True))
                p = jnp.exp(s - m)
                pm = jnp.exp(sm - m)
                l = jnp.sum(p, axis=1, keepdims=True) + jnp.sum(pm, axis=1, keepdims=True)
                o_h = (_nn(p.astype(BF16), v16) + _nn(pm.astype(BF16), vm)) / l
                o = o_h if o is None else jnp.where(heads[h], o_h, o)
                lse_ref[h, pl.ds(q0, NA_QN), :] = m + jnp.log(l)
            o_ref[pl.ds(q0, NA_QN), :] = o
            return carry

        _na_steps(rows, step, 0)

    cblk = lambda col: pl.BlockSpec((lp, LANES), lambda g: (0, col // LANES + g))
    return pl.pallas_call(
        body, grid=(nh // hb,),
        in_specs=[cblk(0), cblk(naw), cblk(2 * naw),
                  pl.BlockSpec((hb, 2 * NA_WIN_H - 1, GRID_W, GRID_W), lambda g: (g, 0, 0, 0))],
        out_specs=[cblk(0), pl.BlockSpec((hb, lp, 1), lambda g: (g, 0, 0))],
        out_shape=[jax.ShapeDtypeStruct((lp, naw), F32), jax.ShapeDtypeStruct((nh, lp, 1), F32)],
        scratch_shapes=[pltpu.VMEM((lp, LANES), BF16)] * 3 + [pltpu.VMEM((hb, 3, NA_QN, NA_KN), F32)],
        compiler_params=_cp("parallel"), name=name)(proj, proj, proj, tb)


def _na_bwd(proj, tb, o, lse, do, *, n_tok, nh, name):
    lp = proj.shape[0]
    dh, hb = NA_HEAD_DIM, NA_HB
    naw = nh * dh
    rows = n_tok // GRID_W
    scale = dh ** -0.5

    def body(q_ref, k_ref, v_ref, tb_ref, o_ref, lse_ref, do_ref, dq_ref, dk_ref, dv_ref, dtb_ref,
             q16_ref, k16_ref, v16_ref, bias_ref):
        dq_ref[...] = jnp.zeros_like(dq_ref)
        dk_ref[...] = jnp.zeros_like(dk_ref)
        dv_ref[...] = jnp.zeros_like(dv_ref)
        dtb_ref[...] = jnp.zeros_like(dtb_ref)
        q16_ref[...] = q_ref[...].astype(BF16)
        k16_ref[...] = k_ref[...].astype(BF16)
        v16_ref[...] = v_ref[...].astype(BF16)
        _na_fill_bias(tb_ref, bias_ref)
        heads = _na_head_lanes()
        km = k16_ref[0:N_META, :]
        vm = v16_ref[0:N_META, :]
        qm = q16_ref[0:N_META, :]
        dom = do_ref[0:N_META, :]
        prod = dom * o_ref[0:N_META, :]
        dq_m = None
        dkm0 = jnp.zeros((N_META, LANES), F32)
        dvm0 = jnp.zeros((N_META, LANES), F32)
        for h in range(hb):
            q_h = _na_only(heads[h], qm)
            do_h = _na_only(heads[h], dom).astype(BF16)
            p = jnp.exp(_nt(q_h, km) * scale - lse_ref[h, 0:N_META, :])
            delta = jnp.sum(_na_only(heads[h], prod), axis=1, keepdims=True)
            ds = (p * (_nt(do_h, vm) - delta)).astype(BF16)
            dq_h = _nn(ds, km) * scale
            dq_m = dq_h if dq_m is None else jnp.where(heads[h], dq_h, dq_m)
            dkm0 = dkm0 + _tn(ds, q_h) * scale
            dvm0 = dvm0 + _tn(p.astype(BF16), do_h)
        dq_ref[0:N_META, :] = dq_m

        def step(pattern, t, carry):
            dkm, dvm = carry
            q0, k0 = _na_step_rows(pattern, t, rows)
            q16 = q16_ref[pl.ds(q0, NA_QN), :]
            k16 = k16_ref[pl.ds(k0, NA_KN), :]
            v16 = v16_ref[pl.ds(k0, NA_KN), :]
            dov = do_ref[pl.ds(q0, NA_QN), :]
            prod = dov * o_ref[pl.ds(q0, NA_QN), :]
            dq = None
            dk = jnp.zeros((NA_KN, LANES), F32)
            dv = jnp.zeros((NA_KN, LANES), F32)
            for h in range(hb):
                q_h = _na_only(heads[h], q16)
                do_h = _na_only(heads[h], dov).astype(BF16)
                lse = lse_ref[h, pl.ds(q0, NA_QN), :]
                p = jnp.exp(_nt(q_h, k16) * scale + bias_ref[h, pattern] - lse)
                pm = jnp.exp(_nt(q_h, km) * scale - lse)
                delta = jnp.sum(_na_only(heads[h], prod), axis=1, keepdims=True)
                ds = p * (_nt(do_h, v16) - delta)
                dsm = (pm * (_nt(do_h, vm) - delta)).astype(BF16)
                ds16 = ds.astype(BF16)
                dq_h = (_nn(ds16, k16) + _nn(dsm, km)) * scale
                dq = dq_h if dq is None else jnp.where(heads[h], dq_h, dq)
                dk = dk + _tn(ds16, q_h) * scale
                dv = dv + _tn(p.astype(BF16), do_h)
                dkm = dkm + _tn(dsm, q_h) * scale
                dvm = dvm + _tn(pm.astype(BF16), do_h)
                for a in range(NA_G):
                    for j in range(NA_U):
                        idx = _na_table_index(pattern, a, j)
                        if idx is not None:
                            dtb_ref[h, idx] += ds[a * GRID_W:(a + 1) * GRID_W,
                                                  j * GRID_W:(j + 1) * GRID_W]
            dq_ref[pl.ds(q0, NA_QN), :] = dq
            dk_ref[pl.ds(k0, NA_KN), :] += dk
            dv_ref[pl.ds(k0, NA_KN), :] += dv
            return dkm, dvm

        dkm, dvm = _na_steps(rows, step, (dkm0, dvm0))
        dk_ref[0:N_META, :] += dkm
        dv_ref[0:N_META, :] += dvm

    cblk = lambda col: pl.BlockSpec((lp, LANES), lambda g: (0, col // LANES + g))
    tbs = pl.BlockSpec((hb, 2 * NA_WIN_H - 1, GRID_W, GRID_W), lambda g: (g, 0, 0, 0))
    sds = jax.ShapeDtypeStruct((lp, naw), F32)
    return pl.pallas_call(
        body, grid=(nh // hb,),
        in_specs=[cblk(0), cblk(naw), cblk(2 * naw), tbs, cblk(0),
                  pl.BlockSpec((hb, lp, 1), lambda g: (g, 0, 0)), cblk(0)],
        out_specs=[cblk(0), cblk(0), cblk(0), tbs],
        out_shape=[sds, sds, sds, jax.ShapeDtypeStruct(tb.shape, F32)],
        scratch_shapes=[pltpu.VMEM((lp, LANES), BF16)] * 3 + [pltpu.VMEM((hb, 3, NA_QN, NA_KN), F32)],
        compiler_params=_cp("parallel"), name=name)(proj, proj, proj, tb, o, lse, do)


def _rpb_onehot():
    c = np.arange(GRID_W)[:, None]
    w = np.arange(GRID_W)[None, :]
    cs = np.clip(c - NA_WIN_W // 2, 0, GRID_W - NA_WIN_W)
    in_win = (w >= cs) & (w < cs + NA_WIN_W)
    dc = np.clip(w - c, -(NA_WIN_W - 1), NA_WIN_W - 1) + NA_WIN_W - 1
    oh = np.zeros((LANES, GRID_W * GRID_W), np.float32)
    flat = np.arange(GRID_W * GRID_W).reshape(GRID_W, GRID_W)
    oh[dc[in_win], flat[in_win]] = 1.0
    neg = np.where(in_win, 0.0, -1e30).astype(np.float32).reshape(1, -1)
    return oh, neg


def _assemble_dproj(dq_na, dk_na, dv_na, dq_f, dq_b, dz_f, dz_b, dv_f, dv_b, dg, dgn, dgh, *, name):
    lp, naw = dq_na.shape
    hgw = dq_f.shape[1]
    d = dgn.shape[1]
    cols = 3 * naw + 5 * hgw + 2 * d
    tr = _row_tile(lp, 3 * naw * 4 + 6 * hgw * 4 + hgw * 2 + 2 * d * 2 + cols * 2)

    def body(nq_ref, nk_ref, nv_ref, qf_ref, qb_ref, zf_ref, zb_ref, vf_ref, vb_ref, g_ref, gn_ref,
             gh_ref, o_ref):
        o_ref[:, 0:naw] = nq_ref[...].astype(BF16)
        o_ref[:, naw:2 * naw] = nk_ref[...].astype(BF16)
        o_ref[:, 2 * naw:3 * naw] = nv_ref[...].astype(BF16)
        c0 = 3 * naw
        o_ref[:, c0:c0 + hgw] = (qf_ref[...] + qb_ref[...]).astype(BF16)
        o_ref[:, c0 + hgw:c0 + 2 * hgw] = zf_ref[...].astype(BF16)
        o_ref[:, c0 + 2 * hgw:c0 + 3 * hgw] = zb_ref[...].astype(BF16)
        o_ref[:, c0 + 3 * hgw:c0 + 4 * hgw] = (vf_ref[...] + vb_ref[...]).astype(BF16)
        o_ref[:, c0 + 4 * hgw:c0 + 5 * hgw] = g_ref[...]
        o_ref[:, c0 + 5 * hgw:c0 + 5 * hgw + d] = gn_ref[...]
        o_ref[:, c0 + 5 * hgw + d:] = gh_ref[...]

    hg, na = _rspec(tr, hgw), _rspec(tr, naw)
    return pl.pallas_call(
        body, grid=(lp // tr,),
        in_specs=[na, na, na, hg, hg, hg, hg, hg, hg, hg, _rspec(tr, d), _rspec(tr, d)],
        out_specs=_rspec(tr, cols),
        out_shape=jax.ShapeDtypeStruct((lp, cols), BF16),
        compiler_params=_cp("parallel"), name=name)(dq_na, dk_na, dv_na, dq_f, dq_b, dz_f, dz_b,
                                                    dv_f, dv_b, dg, dgn, dgh)


GROUP_STEPS = 8


def _group_tiles(rows, align):
    steps = GROUP_STEPS if all(r % (GROUP_STEPS * align) == 0 for r in rows) else 1
    return steps, [r // steps for r in rows]


def _adamw(ws, gs, ms, vs, *, name):
    n = len(ws)
    steps, trs = _group_tiles([w.shape[0] for w in ws], 8)

    def body(*refs):
        for i in range(n):
            w_ref, g_ref, m_ref, v_ref = refs[4 * i:4 * i + 4]
            d_ref, mo_ref, vo_ref = refs[4 * n + 3 * i:4 * n + 3 * i + 3]
            gv = g_ref[...]
            mn = ADAM_B1 * m_ref[...] + (1.0 - ADAM_B1) * gv
            vn = ADAM_B2 * v_ref[...] + (1.0 - ADAM_B2) * (gv * gv)
            m_hat = mn / (1.0 - ADAM_B1 ** ADAM_STEP)
            v_hat = vn / (1.0 - ADAM_B2 ** ADAM_STEP)
            d_ref[...] = -ADAM_LR * (m_hat / (jnp.sqrt(v_hat) + ADAM_EPS) + ADAM_WD * w_ref[...])
            mo_ref[...] = mn
            vo_ref[...] = vn

    specs = [_rspec(tr, w.shape[1]) for tr, w in zip(trs, ws)]
    out = pl.pallas_call(
        body, grid=(steps,),
        in_specs=[s for s in specs for _ in range(4)],
        out_specs=[s for s in specs for _ in range(3)],
        out_shape=[jax.ShapeDtypeStruct(w.shape, F32) for w in ws for _ in range(3)],
        compiler_params=_cp("parallel"), name=name)(*[a for q in zip(ws, gs, ms, vs) for a in q])
    return [tuple(out[3 * i:3 * i + 3]) for i in range(n)]


def _local_step(x, tgt, meta, first_weight, rest_weights, g_mix, g_mlp, g_fin, hg_gain, rpb, lb,
                early_grads=None, mid_grads=None, late_grad=None, rest_landed=None):
    n_tok, d = x.shape
    hgw = hg_gain.shape[1]
    nh, hh = rpb.shape[0], hgw // HG_DK
    naw = nh * NA_HEAD_DIM
    l_real = N_META + n_tok
    lp = -(-l_real // ROW_ALIGN) * ROW_ALIGN
    n_chunks = l_real // HG_CHUNK
    pad = lp - l_real
    col_qhg = 3 * naw
    col_zf, col_zb, col_i, col_g = (col_qhg + hgw, col_qhg + 2 * hgw, col_qhg + 3 * hgw,
                                    col_qhg + 4 * hgw)
    col_gate = col_qhg + 5 * hgw

    zpad = jnp.zeros((pad, d), F32)
    h0 = jnp.concatenate([meta, x, zpad], axis=0)
    tgt_p = jnp.concatenate([jnp.zeros((N_META, d), F32), tgt, zpad], axis=0)

    oh_np, neg_np = _rpb_onehot()
    oh = jnp.asarray(oh_np)
    rpb_p = jnp.pad(rpb.reshape(nh * (2 * NA_WIN_H - 1), 2 * NA_WIN_W - 1),
                    ((0, 0), (0, LANES - (2 * NA_WIN_W - 1))))
    tb = _matmul(rpb_p, oh, tm=rpb_p.shape[0], tn=512, tk=LANES, precision=HIGHEST,
                 name="rpb_expand")
    tb = (tb + jnp.asarray(neg_np)).reshape(nh, 2 * NA_WIN_H - 1, GRID_W, GRID_W)

    a = _rmsnorm_fwd(h0, g_mix, name="norm_mix")
    w_in = first_weight(a)
    proj = _matmul(a, w_in, name="mm_in")
    o_na, lse = _na_fwd(proj, tb, n_tok=n_tok, nh=nh, name="na_fwd")
    lb_f = lb[0].reshape(hh, 1, HG_DK)
    lb_b = lb[1].reshape(hh, 1, HG_DK)
    scan_kw = dict(col_q=col_qhg, col_i=col_i, hh=hh)
    o_f, st_f = _hg_scan_fwd(proj, lb_f, reverse=False, col_z=col_zf, name="hg_scan_f", **scan_kw)
    token = rest_landed(o_f) if rest_landed else None
    lb_b_late = lb_b if token is None else lb_b + token[0:1, 0:1]
    o_b, st_b = _hg_scan_fwd(proj, lb_b_late, reverse=True, col_z=col_zb, name="hg_scan_b",
                             **scan_kw)
    o_hg = _hg_out(o_f, o_b, proj, hg_gain, col_g=col_g, name="hg_out")
    w_na, w_hg, w_o, w_up, w_down = rest_weights(o_hg)
    y_na = _matmul(o_na, w_na, name="mm_na_out")
    gates = ((proj, col_gate), (proj, col_gate + d))

    def mix_gates(acc, gn, gh, yn):
        return acc, _sigmoid(gn) * yn + _sigmoid(gh) * acc

    def mix_gates_bwd(dmix, gn, gh, yn, yh):
        sn, sh = _sigmoid(gn), _sigmoid(gh)
        return dmix * sn, dmix * sh, dmix * yn * sn * (1.0 - sn), dmix * yh * sh * (1.0 - sh)

    y_hg, mix = _matmul(o_hg, w_hg, name="mm_hg_out", epilogue=mix_gates,
                        tiles=(*gates, (y_na, 0)), out_dtypes=(F32, BF16))
    t1 = _matmul(mix, w_o, name="mm_o")
    h1, mlp_in = _residual_norm(h0, t1, g_mlp, name="resid_norm_mlp")
    u, act = _matmul(mlp_in, w_up, name="mm_up", out_dtypes=(BF16, BF16),
                     epilogue=lambda acc: (acc, jnp.square(jnp.maximum(acc, 0.0))))
    t2 = _matmul(act, w_down, name="mm_down")
    dh2, loss, dg_fin = _final_loss(h1, t2, g_fin, tgt_p, n_tok=n_tok, name="final_loss")

    (du,) = _matmul(dh2, w_down, tb=True, name="mm_down_dx", tiles=((u, 0),), out_dtypes=(BF16,),
                    epilogue=lambda acc, uv: (acc * 2.0 * jnp.maximum(uv, 0.0),))
    dw_down = _matmul(act, dh2, ta=True, name="mm_down_dw")
    dm = _matmul(du, w_up, tb=True, name="mm_up_dx")
    dw_up = _matmul(mlp_in, du, ta=True, name="mm_up_dw")
    dh1, dg_mlp = _rmsnorm_bwd_add(h1, g_mlp, dm, dh2, name="norm_mlp_bwd")
    dy_na, dy_hg, dgn, dgh = _matmul(dh1, w_o, tb=True, name="mm_o_dx", epilogue=mix_gates_bwd,
                                     tiles=(*gates, (y_na, 0), (y_hg, 0)), out_dtypes=(BF16,) * 4)
    dw_o = _matmul(mix, dh1, ta=True, name="mm_o_dw")
    do_na = _matmul(dy_na, w_na, tb=True, name="mm_na_out_dx")
    dw_na = _matmul(o_na, dy_na, ta=True, name="mm_na_out_dw")
    do_hg = _matmul(dy_hg, w_hg, tb=True, name="mm_hg_out_dx")
    dw_hg = _matmul(o_hg, dy_hg, ta=True, name="mm_hg_out_dw")
    token = early_grads([dw_na, dw_hg, dw_o, dw_up, dw_down]) if early_grads else None
    if token is not None:
        hg_gain = hg_gain + token[0:1, 0:1]
    d_o, dg_hg, d_gain = _hg_out_bwd(o_f, o_b, proj, hg_gain, do_hg, col_g=col_g, name="hg_out_bwd")
    dq_f, dz_f, dv_f, dlb_f = _hg_scan_bwd(proj, lb_f, st_f, d_o, reverse=False, col_z=col_zf,
                                           name="hg_scan_f_bwd", **scan_kw)
    token = mid_grads(dq_f) if mid_grads else None
    lb_b_late = lb_b if token is None else lb_b + token[0:1, 0:1]
    dq_b, dz_b, dv_b, dlb_b = _hg_scan_bwd(proj, lb_b_late, st_b, d_o, reverse=True, col_z=col_zb,
                                           name="hg_scan_b_bwd", **scan_kw)
    dq_na, dk_na, dv_na, dtb = _na_bwd(proj, tb, o_na, lse, do_na, n_tok=n_tok, nh=nh, name="na_bwd")
    dproj = _assemble_dproj(dq_na, dk_na, dv_na, dq_f, dq_b, dz_f, dz_b, dv_f, dv_b, dg_hg, dgn,
                            dgh, name="assemble_dproj")
    dw_in = _matmul(a, dproj, ta=True, name="mm_in_dw")
    token = late_grad(dw_in) if late_grad else None
    da = _matmul(dproj, w_in, tb=True, name="mm_in_dx", after=token)
    dh0, dg_mix = _rmsnorm_bwd_add(h0, g_mix, da, dh1, name="norm_mix_bwd")
    d_rpb = _matmul(dtb.reshape(nh * (2 * NA_WIN_H - 1), GRID_W * GRID_W), oh, tb=True,
                    tm=nh * (2 * NA_WIN_H - 1), tn=LANES, tk=1024, precision=HIGHEST,
                    name="rpb_reduce")
    d_lb = jnp.concatenate([dlb_f.reshape(1, hgw), dlb_b.reshape(1, hgw)], axis=0)
    return (loss, dh0[N_META:l_real], dh0[:N_META], dw_in, dw_na, dw_hg, dw_o, dw_up, dw_down,
            dg_mix, dg_mlp, dg_fin, d_gain, d_rpb, d_lb)


N_CHIPS = 4
N_DEV = 8
ANY = pl.BlockSpec(memory_space=pl.ANY)


def _place():
    x, y, c = lax.axis_index("x"), lax.axis_index("y"), lax.axis_index("c")
    others = []
    for j in (1, 2, 3):
        tx = (1 - x) if (j >> 1) else x
        ty = (1 - y) if (j & 1) else y
        others.append((tx, ty))
    return x, y, c, others


def _piece(ref, axis, k, half, rh, cs):
    if axis == 1:
        return ref.at[pl.ds(pl.multiple_of(half * rh, 16), rh), pl.ds(pl.multiple_of(k * cs, LANES), cs)]
    return ref.at[pl.ds(pl.multiple_of(k * 2 * rh + half * rh, 16), rh), :]


def _cast_into_full(shards, axes, place, *, name):
    n = len(shards)
    steps, trs = _group_tiles([s.shape[0] for s in shards], 16)

    def body(p_ref, *refs):
        for i in range(n):
            refs[n + i][...] = refs[i][...].astype(BF16)

    def out_spec(tr, cs, axis):
        if axis == 1:
            return pl.BlockSpec((tr, cs), lambda i, p_ref: (i, p_ref[0]))
        return pl.BlockSpec((tr, cs), lambda i, p_ref: (p_ref[0] * steps + i, 0))

    return pl.pallas_call(
        body,
        grid_spec=pltpu.PrefetchScalarGridSpec(
            num_scalar_prefetch=1, grid=(steps,),
            in_specs=[pl.BlockSpec((tr, s.shape[1]), lambda i, p_ref: (i, 0))
                      for tr, s in zip(trs, shards)],
            out_specs=[out_spec(tr, s.shape[1], ax) for tr, s, ax in zip(trs, shards, axes)]),
        out_shape=[jax.ShapeDtypeStruct((s.shape[0], s.shape[1] * N_CHIPS) if ax == 1
                                        else (s.shape[0] * N_CHIPS, s.shape[1]), BF16)
                   for s, ax in zip(shards, axes)],
        compiler_params=_cp("parallel"), name=name)(place, *shards)


HBM_SPEC = pl.BlockSpec(memory_space=pltpu.HBM)
SEM_SPEC = pl.BlockSpec(memory_space=pltpu.SEMAPHORE)
SPLIT_COPY = pltpu.CompilerParams(has_side_effects=pltpu.SideEffectType.DATAFLOW_SIDE_EFFECTING)
TOKEN = jax.ShapeDtypeStruct((8, LANES), F32)


def _geo(fulls, axes):
    out = []
    for f, ax in zip(fulls, axes):
        r, cs = (f.shape[0], f.shape[1] // N_CHIPS) if ax == 1 else (f.shape[0] // N_CHIPS, f.shape[1])
        out.append((ax, r // 2, cs))
    return out


def _gather_copies(refs, geo, send_sems, recv_sems):
    x, y, c, others = _place()
    chip = 2 * x + y
    cps = []
    for i, (ax, rh, cs) in enumerate(geo):
        mine = _piece(refs[i], ax, chip, c, rh, cs)
        for j, (tx, ty) in enumerate(others):
            cps.append(pltpu.make_async_remote_copy(
                src_ref=mine, dst_ref=mine, send_sem=send_sems.at[3 * i + j],
                recv_sem=recv_sems.at[3 * i + j], device_id=(tx, ty, c), device_id_type=MESH))
    return cps


def _allgather_start(fulls, axes, after, *, name):
    n = len(fulls)
    geo = _geo(fulls, axes)

    def body(*refs):
        w_refs = refs[:n]
        send_sems, recv_sems = refs[n + 1], refs[n + 2]
        token = refs[2 * n + 3]
        for cp in _gather_copies(w_refs, geo, send_sems, recv_sems):
            cp.start()
        token[...] = jnp.zeros_like(token)

    out = pl.pallas_call(
        body, name=name,
        out_shape=(pltpu.SemaphoreType.DMA((3 * n,)), pltpu.SemaphoreType.DMA((3 * n,)),
                   *[pltpu.HBM(f.shape, f.dtype) for f in fulls], TOKEN),
        in_specs=[HBM_SPEC] * n + [ANY],
        out_specs=(SEM_SPEC, SEM_SPEC, *[HBM_SPEC] * n, pl.BlockSpec(memory_space=pltpu.VMEM)),
        input_output_aliases={i: 2 + i for i in range(n)},
        compiler_params=SPLIT_COPY,
    )(*[pltpu.with_memory_space_constraint(f, pltpu.HBM) for f in fulls], after)
    return out[0], out[1], list(out[2:2 + n]), out[2 + n]


def _allgather_wait(send_sems, recv_sems, fulls, axes, after, *, name):
    n = len(fulls)
    geo = _geo(fulls, axes)

    def body(*refs):
        w_refs = refs[:n]
        for cp in _gather_copies(w_refs, geo, refs[n], refs[n + 1]):
            cp.wait_send()
            cp.wait_recv()

    return list(pl.pallas_call(
        body, name=name,
        out_shape=[pltpu.HBM(f.shape, f.dtype) for f in fulls],
        in_specs=[HBM_SPEC] * n + [SEM_SPEC, SEM_SPEC, ANY],
        out_specs=[HBM_SPEC] * n,
        input_output_aliases={i: i for i in range(n)},
        compiler_params=SPLIT_COPY,
    )(*fulls, send_sems, recv_sems, after))


def _allgather_forward(fulls, axes, *, name):
    n = len(fulls)
    geo = _geo(fulls, axes)

    def body(*refs):
        o_refs = refs[n:2 * n]
        send_sems, recv_sems = refs[2 * n:]
        x, y, c, others = _place()

        def rcopy(i, j, half, to):
            ax, rh, cs = geo[i]
            ref = _piece(o_refs[i], ax, 2 * others[j][0] + others[j][1], half, rh, cs)
            return pltpu.make_async_remote_copy(
                src_ref=ref, dst_ref=ref, send_sem=send_sems.at[3 * i + j],
                recv_sem=recv_sems.at[3 * i + j], device_id=to, device_id_type=MESH)

        cps = [rcopy(i, j, c, (x, y, 1 - c)) for i in range(n) for j in range(3)]
        for cp in cps:
            cp.start()
        for i in range(n):
            for j in range(3):
                rcopy(i, j, 1 - c, (x, y, c)).wait_recv()
        for cp in cps:
            cp.wait_send()

    return list(pl.pallas_call(
        body, in_specs=[ANY] * n, out_specs=[ANY] * n,
        out_shape=[jax.ShapeDtypeStruct(f.shape, f.dtype) for f in fulls],
        input_output_aliases={i: i for i in range(n)},
        scratch_shapes=[pltpu.SemaphoreType.DMA((3 * n,)), pltpu.SemaphoreType.DMA((3 * n,))],
        name=name)(*fulls))


def _forward_copies(refs, geo, send_sems, recv_sems):
    x, y, c, others = _place()
    cps = []
    for i, (ax, rh, cs) in enumerate(geo):
        for j, (tx, ty) in enumerate(others):
            ref = _piece(refs[i], ax, 2 * tx + ty, c, rh, cs)
            cps.append(pltpu.make_async_remote_copy(
                src_ref=ref, dst_ref=ref, send_sem=send_sems.at[3 * i + j],
                recv_sem=recv_sems.at[3 * i + j], device_id=(x, y, 1 - c), device_id_type=MESH))
    return cps


def _allgather_forward_start(fulls, axes, *, name):
    n = len(fulls)
    geo = _geo(fulls, axes)

    def body(*refs):
        token = refs[2 * n + 2]
        for cp in _forward_copies(refs[:n], geo, refs[n], refs[n + 1]):
            cp.start()
        token[...] = jnp.zeros_like(token)

    out = pl.pallas_call(
        body, name=name,
        out_shape=(pltpu.SemaphoreType.DMA((3 * n,)), pltpu.SemaphoreType.DMA((3 * n,)),
                   *[pltpu.HBM(f.shape, f.dtype) for f in fulls], TOKEN),
        in_specs=[HBM_SPEC] * n,
        out_specs=(SEM_SPEC, SEM_SPEC, *[HBM_SPEC] * n, pl.BlockSpec(memory_space=pltpu.VMEM)),
        input_output_aliases={i: 2 + i for i in range(n)},
        compiler_params=SPLIT_COPY,
    )(*fulls)
    return out[0], out[1], list(out[2:2 + n]), out[2 + n]


def _allgather_forward_wait(send_sems, recv_sems, fulls, axes, after, *, name):
    n = len(fulls)
    geo = _geo(fulls, axes)

    def body(*refs):
        for cp in _forward_copies(refs[:n], geo, refs[n], refs[n + 1]):
            cp.wait_send()
            cp.wait_recv()

    return list(pl.pallas_call(
        body, name=name,
        out_shape=[pltpu.HBM(f.shape, f.dtype) for f in fulls],
        in_specs=[HBM_SPEC] * n + [SEM_SPEC, SEM_SPEC, ANY],
        out_specs=[HBM_SPEC] * n,
        input_output_aliases={i: i for i in range(n)},
        compiler_params=SPLIT_COPY,
    )(*fulls, send_sems, recv_sems, after))


def _chip_copies(blk_ref, land_ref, send_sems, recv_sems):
    x, y, c, others = _place()
    return [pltpu.make_async_remote_copy(
        src_ref=blk_ref, dst_ref=land_ref.at[2 * x + y], send_sem=send_sems.at[j],
        recv_sem=recv_sems.at[j], device_id=(tx, ty, c), device_id_type=MESH)
        for j, (tx, ty) in enumerate(others)]


def _chip_exchange_start(blk, *, name):
    land = pltpu.with_memory_space_constraint(lax.empty((N_CHIPS, *blk.shape), blk.dtype), pltpu.HBM)

    def body(blk_ref, land_ref, send_sems, recv_sems, blk_out, land_out, token):
        for cp in _chip_copies(blk_ref, land_ref, send_sems, recv_sems):
            cp.start()
        token[...] = jnp.zeros_like(token)

    return pl.pallas_call(
        body, name=name,
        out_shape=(pltpu.SemaphoreType.DMA((3,)), pltpu.SemaphoreType.DMA((3,)),
                   pltpu.HBM(blk.shape, blk.dtype), pltpu.HBM(land.shape, land.dtype), TOKEN),
        in_specs=[HBM_SPEC] * 2,
        out_specs=(SEM_SPEC, SEM_SPEC, HBM_SPEC, HBM_SPEC, pl.BlockSpec(memory_space=pltpu.VMEM)),
        input_output_aliases={0: 2, 1: 3},
        compiler_params=SPLIT_COPY,
    )(pltpu.with_memory_space_constraint(blk, pltpu.HBM), land)


def _chip_exchange_wait(send_sems, recv_sems, blk, land, after, *, name):
    def body(blk_ref, land_ref, send_sems, recv_sems, after_ref, blk_out, land_out):
        for cp in _chip_copies(blk_ref, land_ref, send_sems, recv_sems):
            cp.wait_send()
            cp.wait_recv()

    return pl.pallas_call(
        body, name=name,
        out_shape=[pltpu.HBM(blk.shape, blk.dtype), pltpu.HBM(land.shape, land.dtype)],
        in_specs=[HBM_SPEC] * 2 + [SEM_SPEC, SEM_SPEC, ANY],
        out_specs=[HBM_SPEC] * 2,
        input_output_aliases={0: 0, 1: 1},
        compiler_params=SPLIT_COPY,
    )(blk, land, send_sems, recv_sems, after)[1]


def _scatter_geo(parts, axes):
    out = []
    for p, ax in zip(parts, axes):
        _, rh, cols = p.shape
        out.append((ax, rh, cols // N_CHIPS if ax == 1 else cols))
    return out


def _scatter_copies(p_refs, q_refs, geo, send_sems, recv_sems):
    x, y, c, others = _place()
    chip = 2 * x + y
    cps = []
    for i, (ax, rh, cw) in enumerate(geo):
        for j, (tx, ty) in enumerate(others):
            k = 2 * tx + ty
            src = (p_refs[i].at[0, :, pl.ds(pl.multiple_of(k * cw, LANES), cw)] if ax == 1
                   else p_refs[i].at[k])
            cps.append(pltpu.make_async_remote_copy(
                src_ref=src, dst_ref=q_refs[i].at[chip], send_sem=send_sems.at[3 * i + j],
                recv_sem=recv_sems.at[3 * i + j], device_id=(tx, ty, c), device_id_type=MESH))
    return cps


def _scatter_start(parts, axes, *, name):
    n = len(parts)
    geo = _scatter_geo(parts, axes)
    slots = [pltpu.HBM((N_CHIPS, rh, cw), p.dtype) for p, (_, rh, cw) in zip(parts, geo)]

    def body(*refs):
        p_refs, q_refs = refs[:n], refs[n:2 * n]
        send_sems, recv_sems = refs[2 * n], refs[2 * n + 1]
        token = refs[4 * n + 2]
        for cp in _scatter_copies(p_refs, q_refs, geo, send_sems, recv_sems):
            cp.start()
        token[...] = jnp.zeros_like(token)

    land = [pltpu.with_memory_space_constraint(lax.empty(s.inner_aval.shape, s.inner_aval.dtype), pltpu.HBM)
            for s in slots]
    out = pl.pallas_call(
        body, name=name,
        out_shape=(pltpu.SemaphoreType.DMA((3 * n,)), pltpu.SemaphoreType.DMA((3 * n,)),
                   *[pltpu.HBM(p.shape, p.dtype) for p in parts], *slots, TOKEN),
        in_specs=[HBM_SPEC] * (2 * n),
        out_specs=(SEM_SPEC, SEM_SPEC, *[HBM_SPEC] * (2 * n), pl.BlockSpec(memory_space=pltpu.VMEM)),
        input_output_aliases={i: 2 + i for i in range(2 * n)},
        compiler_params=SPLIT_COPY,
    )(*[pltpu.with_memory_space_constraint(p, pltpu.HBM) for p in parts], *land)
    return out[0], out[1], list(out[2:2 + n]), list(out[2 + n:2 + 2 * n]), out[2 + 2 * n]


def _scatter_wait(send_sems, recv_sems, parts, slots, axes, after, *, name):
    n = len(parts)
    geo = _scatter_geo(parts, axes)

    def body(*refs):
        p_refs, q_refs = refs[:n], refs[n:2 * n]
        for cp in _scatter_copies(p_refs, q_refs, geo, refs[2 * n], refs[2 * n + 1]):
            cp.wait_send()
            cp.wait_recv()

    out = pl.pallas_call(
        body, name=name,
        out_shape=[pltpu.HBM(a.shape, a.dtype) for a in (*parts, *slots)],
        in_specs=[HBM_SPEC] * (2 * n) + [SEM_SPEC, SEM_SPEC, ANY],
        out_specs=[HBM_SPEC] * (2 * n),
        input_output_aliases={i: i for i in range(2 * n)},
        compiler_params=SPLIT_COPY,
    )(*parts, *slots, send_sems, recv_sems, after)
    return list(out[:n]), list(out[n:])


def _sibling_swap(grads, *, name):
    n = len(grads)
    out_shape = [jax.ShapeDtypeStruct((g.shape[0], g.shape[1] // 2, g.shape[2]), g.dtype)
                 for g in grads]

    def body(*refs):
        g_refs, o_refs = refs[:n], refs[n:2 * n]
        send_sems, recv_sems = refs[2 * n:]
        x, y, c, _ = _place()
        cps = []
        for i in range(n):
            rh = grads[i].shape[1] // 2
            src = g_refs[i].at[:, pl.ds(pl.multiple_of((1 - c) * rh, 16), rh), :]
            cp = pltpu.make_async_remote_copy(
                src_ref=src, dst_ref=o_refs[i], send_sem=send_sems.at[i], recv_sem=recv_sems.at[i],
                device_id=(x, y, 1 - c), device_id_type=MESH)
            cp.start()
            cps.append(cp)
        for cp in cps:
            cp.wait()

    return pl.pallas_call(
        body, in_specs=[ANY] * n, out_specs=[ANY] * n, out_shape=out_shape,
        scratch_shapes=[pltpu.SemaphoreType.DMA((n,)), pltpu.SemaphoreType.DMA((n,))],
        name=name)(*grads)


def _swap_copies(g_refs, r_refs, shapes, send_sems, recv_sems):
    x, y, c, _ = _place()
    cps = []
    for i, shape in enumerate(shapes):
        rh = shape[1] // 2
        src = g_refs[i].at[:, pl.ds(pl.multiple_of((1 - c) * rh, 16), rh), :]
        cps.append(pltpu.make_async_remote_copy(
            src_ref=src, dst_ref=r_refs[i], send_sem=send_sems.at[i], recv_sem=recv_sems.at[i],
            device_id=(x, y, 1 - c), device_id_type=MESH))
    return cps


def _sibling_swap_start(grads, *, name):
    n = len(grads)
    shapes = [g.shape for g in grads]
    lands = [pltpu.HBM((s[0], s[1] // 2, s[2]), g.dtype) for s, g in zip(shapes, grads)]

    def body(*refs):
        g_refs, r_refs = refs[:n], refs[n:2 * n]
        token = refs[4 * n + 2]
        for cp in _swap_copies(g_refs, r_refs, shapes, refs[2 * n], refs[2 * n + 1]):
            cp.start()
        token[...] = jnp.zeros_like(token)

    land = [pltpu.with_memory_space_constraint(lax.empty(s.inner_aval.shape, s.inner_aval.dtype), pltpu.HBM)
            for s in lands]
    out = pl.pallas_call(
        body, name=name,
        out_shape=(pltpu.SemaphoreType.DMA((n,)), pltpu.SemaphoreType.DMA((n,)),
                   *[pltpu.HBM(g.shape, g.dtype) for g in grads], *lands, TOKEN),
        in_specs=[HBM_SPEC] * (2 * n),
        out_specs=(SEM_SPEC, SEM_SPEC, *[HBM_SPEC] * (2 * n), pl.BlockSpec(memory_space=pltpu.VMEM)),
        input_output_aliases={i: 2 + i for i in range(2 * n)},
        compiler_params=SPLIT_COPY,
    )(*[pltpu.with_memory_space_constraint(g, pltpu.HBM) for g in grads], *land)
    return out[0], out[1], list(out[2:2 + n]), list(out[2 + n:2 + 2 * n]), out[2 + 2 * n]


def _sibling_swap_wait(send_sems, recv_sems, grads, lands, after, *, name):
    n = len(grads)
    shapes = [g.shape for g in grads]

    def body(*refs):
        g_refs, r_refs = refs[:n], refs[n:2 * n]
        for cp in _swap_copies(g_refs, r_refs, shapes, refs[2 * n], refs[2 * n + 1]):
            cp.wait_send()
            cp.wait_recv()

    out = pl.pallas_call(
        body, name=name,
        out_shape=[pltpu.HBM(a.shape, a.dtype) for a in (*grads, *lands)],
        in_specs=[HBM_SPEC] * (2 * n) + [SEM_SPEC, SEM_SPEC, ANY],
        out_specs=[HBM_SPEC] * (2 * n),
        input_output_aliases={i: i for i in range(2 * n)},
        compiler_params=SPLIT_COPY,
    )(*grads, *lands, send_sems, recv_sems, after)
    return list(out[:n]), list(out[n:])


def _pair_add(g3s, rxs, place, *, out_dtype, name):
    n = len(g3s)
    steps, trs = _group_tiles([g.shape[1] // 2 for g in g3s], 16)

    def body(p_ref, *refs):
        for i in range(n):
            refs[2 * n + i][...] = (refs[2 * i][...] + refs[2 * i + 1][...]).astype(out_dtype)

    in_specs, out_specs = [], []
    for g, tr in zip(g3s, trs):
        blk = (g.shape[0], tr, g.shape[2])
        in_specs += [pl.BlockSpec(blk, lambda i, p_ref: (0, p_ref[1] * steps + i, 0)),
                     pl.BlockSpec(blk, lambda i, p_ref: (0, i, 0))]
        out_specs.append(pl.BlockSpec(blk, lambda i, p_ref: (0, i, 0)))
    return pl.pallas_call(
        body,
        grid_spec=pltpu.PrefetchScalarGridSpec(
            num_scalar_prefetch=1, grid=(steps,), in_specs=in_specs, out_specs=out_specs),
        out_shape=[jax.ShapeDtypeStruct((g.shape[0], g.shape[1] // 2, g.shape[2]), out_dtype)
                   for g in g3s],
        compiler_params=_cp("parallel"), name=name)(place, *[a for q in zip(g3s, rxs) for a in q])


def _sum_slots(q, *, name):
    ns, rows, cols = q.shape
    tr = next(t for t in (128, 64, 32, 16, 8) if rows % t == 0)

    def body(q_ref, o_ref):
        acc = q_ref[0].astype(F32)
        for k in range(1, ns):
            acc = acc + q_ref[k].astype(F32)
        o_ref[...] = acc

    return pl.pallas_call(
        body, grid=(rows // tr,),
        in_specs=[pl.BlockSpec((ns, tr, cols), lambda i: (0, i, 0))],
        out_specs=_rspec(tr, cols),
        out_shape=jax.ShapeDtypeStruct((rows, cols), F32),
        compiler_params=_cp("parallel"), name=name)(q)


def _sum_chips(qs, ps, place, axes, *, name):
    n = len(qs)
    per = N_CHIPS + 1
    steps, trs = _group_tiles([q.shape[1] for q in qs], 16)

    def body(p_ref, *refs):
        chip = p_ref[0]
        for i in range(n):
            q_refs, own_ref = refs[per * i:per * i + N_CHIPS], refs[per * i + N_CHIPS]
            acc = jnp.where(chip == 0, own_ref[...], q_refs[0][...]).astype(F32)
            for k in range(1, N_CHIPS):
                acc = acc + jnp.where(chip == k, own_ref[...], q_refs[k][...]).astype(F32)
            refs[per * n + i][...] = acc

    def slot_spec(k, tr, cw):
        return pl.BlockSpec((None, tr, cw),
                            lambda i, p_ref: (jnp.where(p_ref[0] == k, (k + 1) % N_CHIPS, k), i, 0))

    in_specs, out_specs, operands = [], [], []
    for q, p, ax, tr in zip(qs, ps, axes, trs):
        cw = q.shape[2]
        in_specs += [slot_spec(k, tr, cw) for k in range(N_CHIPS)]
        in_specs.append(pl.BlockSpec((None, tr, cw), (lambda i, p_ref: (0, i, p_ref[0])) if ax == 1
                                     else (lambda i, p_ref: (p_ref[0], i, 0))))
        out_specs.append(pl.BlockSpec((tr, cw), lambda i, p_ref: (p_ref[1] * steps + i, 0)))
        operands += [q] * N_CHIPS + [p]
    return pl.pallas_call(
        body,
        grid_spec=pltpu.PrefetchScalarGridSpec(
            num_scalar_prefetch=1, grid=(steps,), in_specs=in_specs, out_specs=out_specs),
        out_shape=[jax.ShapeDtypeStruct((2 * q.shape[1], q.shape[2]), F32) for q in qs],
        compiler_params=_cp("parallel"), name=name)(place, *operands)


def _sibling_share(shards, *, name):
    n = len(shards)

    def body(*refs):
        o_refs = refs[n:2 * n]
        send_sems, recv_sems = refs[2 * n:]
        x, y, c, _ = _place()
        cps = []
        for i in range(n):
            rh = shards[i].shape[0] // 2
            mine = o_refs[i].at[pl.ds(pl.multiple_of(c * rh, 8), rh), :]
            cp = pltpu.make_async_remote_copy(
                src_ref=mine, dst_ref=mine, send_sem=send_sems.at[i], recv_sem=recv_sems.at[i],
                device_id=(x, y, 1 - c), device_id_type=MESH)
            cp.start()
            cps.append(cp)
        for i in range(n):
            rh = shards[i].shape[0] // 2
            theirs = o_refs[i].at[pl.ds(pl.multiple_of((1 - c) * rh, 8), rh), :]
            pltpu.make_async_remote_copy(
                src_ref=theirs, dst_ref=theirs, send_sem=send_sems.at[i], recv_sem=recv_sems.at[i],
                device_id=(x, y, c), device_id_type=MESH).wait_recv()
        for cp in cps:
            cp.wait_send()

    return pl.pallas_call(
        body, in_specs=[ANY] * n, out_specs=[ANY] * n,
        out_shape=[jax.ShapeDtypeStruct(h.shape, h.dtype) for h in shards],
        input_output_aliases={i: i for i in range(n)},
        scratch_shapes=[pltpu.SemaphoreType.DMA((n,)), pltpu.SemaphoreType.DMA((n,))],
        name=name)(*shards)


def _gather_all(blk, *, name, after=None):
    rows, cols = blk.shape
    extra = [] if after is None else [after]

    def body(x_ref, *refs):
        out_ref, send_sems, recv_sems, local_sem = refs[len(extra):]
        x, y, c = lax.axis_index("x"), lax.axis_index("y"), lax.axis_index("c")
        me = 4 * x + 2 * y + c
        mine = pltpu.make_async_copy(x_ref, out_ref.at[me], local_sem)
        mine.start()
        cps = []
        for k in range(1, N_DEV):
            tx = (1 - x) if (k >> 2) & 1 else x
            ty = (1 - y) if (k >> 1) & 1 else y
            tc = (1 - c) if k & 1 else c
            cp = pltpu.make_async_remote_copy(
                src_ref=x_ref, dst_ref=out_ref.at[me], send_sem=send_sems.at[k - 1],
                recv_sem=recv_sems.at[k - 1], device_id=(tx, ty, tc), device_id_type=MESH)
            cp.start()
            cps.append(cp)
        for k in range(1, N_DEV):
            tx = (1 - x) if (k >> 2) & 1 else x
            ty = (1 - y) if (k >> 1) & 1 else y
            tc = (1 - c) if k & 1 else c
            got = out_ref.at[4 * tx + 2 * ty + tc]
            pltpu.make_async_remote_copy(
                src_ref=got, dst_ref=got, send_sem=send_sems.at[k - 1], recv_sem=recv_sems.at[k - 1],
                device_id=(x, y, c), device_id_type=MESH).wait_recv()
        for cp in cps:
            cp.wait_send()
        mine.wait()

    vm = pl.BlockSpec(memory_space=pltpu.VMEM)
    return pl.pallas_call(
        body, in_specs=[vm] + [ANY] * len(extra), out_specs=vm,
        out_shape=jax.ShapeDtypeStruct((N_DEV, rows, cols), blk.dtype),
        scratch_shapes=[pltpu.SemaphoreType.DMA((N_DEV - 1,)), pltpu.SemaphoreType.DMA((N_DEV - 1,)),
                        pltpu.SemaphoreType.DMA],
        name=name)(blk, *extra)


def _as_rows(a):
    flat = a.reshape(-1)
    n = flat.shape[0]
    rows = -(-n // (8 * LANES)) * 8
    return jnp.pad(flat, (0, rows * LANES - n)).reshape(rows, LANES)


def _from_rows(p, shape):
    n = int(np.prod(shape))
    return p.reshape(-1)[:n].reshape(shape)


WEIGHT_AXES = (1, 1, 1, 0, 1, 0)
WIRE = BF16


def kernel(x, meta_tokens, w_in, w_na_out, w_hg_out, w_o, w_up, w_down, norm_mix, norm_mlp, norm_final, hg_norm, na_rpb, hg_lb_logits, loss_target, m_meta_tokens, m_w_in, m_w_na_out, m_w_hg_out, m_w_o, m_w_up, m_w_down, m_norm_mix, m_norm_mlp, m_norm_final, m_hg_norm, m_na_rpb, m_hg_lb_logits, v_meta_tokens, v_w_in, v_w_na_out, v_w_hg_out, v_w_o, v_w_up, v_w_down, v_norm_mix, v_norm_mlp, v_norm_final, v_hg_norm, v_na_rpb, v_hg_lb_logits):
    xi, yi, ci = lax.axis_index("x"), lax.axis_index("y"), lax.axis_index("c")
    chip = 2 * xi + yi
    d = x.shape[-1]
    dshard = meta_tokens.shape[1]
    hgw = hg_norm.shape[1]
    lbs = hg_lb_logits.shape[2]
    big = [w_in[0], w_na_out[0], w_hg_out[0], w_o[0], w_up[0], w_down[0]]
    big_m = [m_w_in[0], m_w_na_out[0], m_w_hg_out[0], m_w_o[0], m_w_up[0], m_w_down[0]]
    big_v = [v_w_in[0], v_w_na_out[0], v_w_hg_out[0], v_w_o[0], v_w_up[0], v_w_down[0]]

    place = jnp.stack([chip, ci]).astype(jnp.int32)
    own_w = _cast_into_full(big, WEIGHT_AXES, place, name="cast_shards")
    in_axes, rest_axes = WEIGHT_AXES[:1], WEIGHT_AXES[1:]
    small_in = jnp.concatenate([_as_rows(meta_tokens), _as_rows(hg_lb_logits)], axis=0)
    sm_send, sm_recv, sm_blk, sm_land, sm_token = _chip_exchange_start(small_in,
                                                                       name="small_params_start")
    in_send, in_recv, in_bufs, in_token = _allgather_start(own_w[:1], in_axes, sm_token,
                                                           name="weight_allgather_in_start")
    ag_send, ag_recv, ag_bufs, ag_token = _allgather_start(own_w[1:], rest_axes, in_token,
                                                           name="weight_allgather_rest_start")
    sm_land = _chip_exchange_wait(sm_send, sm_recv, sm_blk, sm_land, ag_token,
                                  name="small_params_wait")
    small_all = lax.dynamic_update_slice(sm_land, small_in[None], (chip, 0, 0))
    forward = {}

    def first_weight(after):
        got = _allgather_wait(in_send, in_recv, in_bufs, in_axes, after,
                              name="weight_allgather_in_wait")
        return _allgather_forward(got, in_axes, name="weight_allgather_in_forward")[0]

    def rest_landed(after):
        got = _allgather_wait(ag_send, ag_recv, ag_bufs, rest_axes, after,
                              name="weight_allgather_rest_wait")
        send, recv, bufs, token = _allgather_forward_start(
            got, rest_axes, name="weight_allgather_rest_forward_start")
        forward["rest"] = (send, recv, bufs)
        return token

    def rest_weights(after):
        return _allgather_forward_wait(*forward["rest"], rest_axes, after,
                                       name="weight_allgather_rest_forward_wait")

    n_meta_rows = N_META * dshard // LANES
    meta_full = (small_all[:, :n_meta_rows].reshape(N_CHIPS, N_META, dshard)
                 .transpose(1, 0, 2).reshape(N_META, d))
    lbl_full = (small_all[:, n_meta_rows:].reshape(N_CHIPS, -1)[:, :4 * lbs]
                .reshape(N_CHIPS, 2, 2, lbs).transpose(1, 2, 0, 3).reshape(2, 2, N_CHIPS * lbs))
    lb = jax.nn.softmax(lbl_full, axis=1)[:, 0]

    def by_chip(dws, axes):
        return [g.reshape(1, *g.shape) if ax == 1
                else g.reshape(N_CHIPS, g.shape[0] // N_CHIPS, g.shape[1]) for g, ax in zip(dws, axes)]

    flying = {}

    def scatter(tag, axes, g3, rx):
        parts = _pair_add(g3, rx, place, out_dtype=WIRE, name=f"grad_pair_add_{tag}")
        send, recv, parts, slots, token = _scatter_start(parts, axes,
                                                         name=f"grad_scatter_{tag}_start")
        flying[tag] = (send, recv, parts, slots)
        return token

    def swap_rest(dws):
        send, recv, g3, lands, token = _sibling_swap_start(by_chip(dws, rest_axes),
                                                           name="grad_sibling_swap_rest_start")
        flying["swap"] = (send, recv, g3, lands)
        return token

    def scatter_rest(after):
        g3, rx = _sibling_swap_wait(*flying["swap"], after, name="grad_sibling_swap_rest_wait")
        return scatter("rest", rest_axes, g3, rx)

    def scatter_in(dw_in):
        g3 = by_chip([dw_in], in_axes)
        return scatter("in", in_axes, g3, _sibling_swap(g3, name="grad_sibling_swap_in"))

    def landed(tag, axes, after):
        return _scatter_wait(*flying[tag], axes, after, name=f"grad_scatter_{tag}_wait")

    (loss, dx, dmeta, *_, dg_mix, dg_mlp, dg_fin, d_gain, d_rpb, d_lb) = _local_step(
        x[0], loss_target[0], meta_full, first_weight, rest_weights, norm_mix, norm_mlp,
        norm_final.reshape(1, d), hg_norm, na_rpb[0], lb, swap_rest, scatter_rest, scatter_in,
        rest_landed)

    parts_rest, slots_rest = landed("rest", rest_axes, dx)
    parts_in, slots_in = landed("in", in_axes, slots_rest[-1])
    halves = _sum_chips(slots_in + slots_rest, parts_in + parts_rest, place, WEIGHT_AXES,
                        name="grad_sum_chips")
    g_big = _sibling_share(halves, name="grad_sibling_share")

    d_rpb_c = d_rpb[:, :2 * NA_WIN_W - 1]
    small_g = [dmeta, dg_mix, dg_mlp, dg_fin, d_gain, d_rpb_c, d_lb, loss]
    packed = jnp.concatenate([_as_rows(a) for a in small_g], axis=0)
    total = _sum_slots(_gather_all(packed, after=halves[0], name="gather_small_grads"),
                       name="sum_small_grads")
    offs = np.cumsum([0] + [_as_rows(a).shape[0] for a in small_g])
    take = lambda i, shape: _from_rows(total[offs[i]:offs[i + 1]], shape)
    g_meta_full = take(0, (N_META, d))
    g_norm_mix, g_norm_mlp = take(1, (1, d)), take(2, (1, d))
    g_norm_final = take(3, (d,))
    g_hg_norm = take(4, (1, hgw))
    g_rpb = take(5, na_rpb.shape)
    g_lb = take(6, (2, hgw))
    loss_total = take(7, (1, LANES))[0, 0]
    g_meta = lax.dynamic_slice_in_dim(g_meta_full, chip * dshard, dshard, axis=1)
    dl0 = lb * (1.0 - lb) * g_lb
    g_lbl_full = jnp.stack([dl0, -dl0], axis=1)
    g_lbl = lax.dynamic_slice_in_dim(g_lbl_full, chip * lbs, lbs, axis=2)

    big_out = _adamw(big, g_big, big_m, big_v, name="adamw_big")
    small_w = [meta_tokens, norm_mix, norm_mlp, norm_final, hg_norm, na_rpb, hg_lb_logits]
    small_gr = [g_meta, g_norm_mix, g_norm_mlp, g_norm_final, g_hg_norm, g_rpb, g_lbl]
    small_m = [m_meta_tokens, m_norm_mix, m_norm_mlp, m_norm_final, m_hg_norm, m_na_rpb, m_hg_lb_logits]
    small_v = [v_meta_tokens, v_norm_mix, v_norm_mlp, v_norm_final, v_hg_norm, v_na_rpb, v_hg_lb_logits]
    pk = lambda lst: jnp.concatenate([_as_rows(a) for a in lst], axis=0)
    ((sd, sm, sv),) = _adamw([pk(small_w)], [pk(small_gr)], [pk(small_m)], [pk(small_v)],
                             name="adamw_small")
    soffs = np.cumsum([0] + [_as_rows(a).shape[0] for a in small_w])
    unpk = lambda p: [_from_rows(p[soffs[i]:soffs[i + 1]], small_w[i].shape) for i in range(len(small_w))]
    sd, sm, sv = unpk(sd), unpk(sm), unpk(sv)

    def order(bigs, smalls):
        return [smalls[0]] + [b.reshape(1, *b.shape) for b in bigs] + smalls[1:]

    grads = order(g_big, small_gr)
    deltas = order([o[0] for o in big_out], sd)
    new_m = order([o[1] for o in big_out], sm)
    new_v = order([o[2] for o in big_out], sv)
    return (loss_total, dx.reshape(1, *dx.shape), *grads, *deltas, *new_m, *new_v)
```

```python
import functools

import numpy as np
import jax
import jax.numpy as jnp
from jax import lax
from jax.experimental import pallas as pl
from jax.experimental.pallas import tpu as pltpu

F32 = jnp.float32
BF16 = jnp.bfloat16
HIGHEST = lax.Precision.HIGHEST

GRID_W = 64
N_META = 16
EPS = 1e-6
NA_HEAD_DIM = 64
NA_WIN_H = 8
NA_WIN_W = 16
HG_DK = 128
HG_CHUNK = 16
LANES = 128
ROW_ALIGN = 128
VMEM_LIMIT = 48 * 1024 * 1024

ADAM_LR = 0.001
ADAM_B1 = 0.9
ADAM_B2 = 0.999
ADAM_EPS = 1e-08
ADAM_WD = 0.01
ADAM_STEP = 10

MESH = pl.DeviceIdType.MESH


def _cp(*sem):
    return pltpu.CompilerParams(dimension_semantics=sem, vmem_limit_bytes=VMEM_LIMIT)


def _sigmoid(x):
    return 0.5 * jnp.tanh(0.5 * x) + 0.5


def _dot(a, b, dims, precision=None):
    return lax.dot_general(a, b, (dims, ((), ())), preferred_element_type=F32, precision=precision)


def _nn(a, b, **kw):
    return _dot(a, b, ((1,), (0,)), **kw)


def _nt(a, b, **kw):
    return _dot(a, b, ((1,), (1,)), **kw)


def _tn(a, b, **kw):
    return _dot(a, b, ((0,), (0,)), **kw)


def _matmul(a, b, *, ta=False, tb=False, tm=None, tn=None, tk=None, out_dtype=F32, name,
            precision=None, after=None, epilogue=None, tiles=(), out_dtypes=None):
    extra = [] if after is None else [after]
    single = out_dtypes is None
    if single:
        out_dtypes = (out_dtype,)
    n_t, n_o = len(tiles), len(out_dtypes)
    if ta:
        kdim, m = a.shape
    else:
        m, kdim = a.shape
    if tb:
        n, k2 = b.shape
    else:
        k2, n = b.shape
    assert kdim == k2, (a.shape, b.shape, ta, tb)
    if tm is None:
        if ta:
            tm = next(t for t in (1024, 512, 256, 128, m) if m % t == 0)
        else:
            tm = m // 2 if (m // 2) % 16 == 0 and m > 512 else m
    if tn is None:
        wide = (1024,) if not ta and len(tiles) <= 1 else ()
        tn = next(t for t in (*wide, 512, 256, 128, n) if n % t == 0)
    if tk is None:
        tk = kdim if ta else next(t for t in (1024, 512, 256, 128, kdim) if kdim % t == 0)
    assert m % tm == 0 and n % tn == 0 and kdim % tk == 0, (m, n, kdim, tm, tn, tk)
    nk = kdim // tk
    op_dtype = F32 if precision is not None else BF16

    def body(a_ref, b_ref, *refs):
        t_refs = refs[:n_t]
        o_refs = refs[n_t + len(extra):n_t + len(extra) + n_o]
        av = a_ref[...].astype(op_dtype)
        bv = b_ref[...].astype(op_dtype)
        dims = ((0 if ta else 1,), (1 if tb else 0,))
        part = _dot(av, bv, dims, precision=precision)

        def finish(acc):
            outs = (acc,) if epilogue is None else epilogue(acc, *[t[...] for t in t_refs])
            for o_ref, val in zip(o_refs, outs):
                o_ref[...] = val.astype(o_ref.dtype)

        if nk == 1:
            finish(part)
            return
        acc_ref = refs[-1]
        kk = pl.program_id(2)

        @pl.when(kk == 0)
        def _():
            acc_ref[...] = part

        @pl.when((kk > 0) & (kk < nk - 1))
        def _():
            acc_ref[...] += part

        @pl.when(kk == nk - 1)
        def _():
            finish(acc_ref[...] + part)

    a_spec = (pl.BlockSpec((tk, tm), lambda i, j, k: (k, i)) if ta
              else pl.BlockSpec((tm, tk), lambda i, j, k: (i, k)))
    b_spec = (pl.BlockSpec((tn, tk), lambda i, j, k: (j, k)) if tb
              else pl.BlockSpec((tk, tn), lambda i, j, k: (k, j)))
    for _, off in tiles:
        assert off % tn == 0, (off, tn)
    t_specs = [pl.BlockSpec((tm, tn), functools.partial(lambda i, j, k, o: (i, o + j), o=off // tn))
               for _, off in tiles]
    o_spec = pl.BlockSpec((tm, tn), lambda i, j, k: (i, j))
    outs = pl.pallas_call(
        body,
        grid=(m // tm, n // tn, nk),
        in_specs=[a_spec, b_spec] + t_specs + [pl.BlockSpec(memory_space=pl.ANY)] * len(extra),
        out_specs=[o_spec] * n_o,
        out_shape=[jax.ShapeDtypeStruct((m, n), dt) for dt in out_dtypes],
        scratch_shapes=[pltpu.VMEM((tm, tn), F32)] if nk > 1 else [],
        compiler_params=_cp("parallel", "parallel", "arbitrary"),
        name=name,
    )(a, b, *[t for t, _ in tiles], *extra)
    return outs[0] if single else outs


def _rspec(tr, w, cb=0):
    return pl.BlockSpec((tr, w), lambda i: (i, cb))


def _fspec(shape):
    nd = len(shape)
    return pl.BlockSpec(shape, lambda i: (0,) * nd)


ROW_VMEM_BUDGET = 20 * 1024 * 1024
ROW_MIN_STEPS = 4


def _row_tile(lp, row_bytes):
    for k in range(ROW_MIN_STEPS, lp // 16 + 1):
        tr = lp // k
        if lp % k == 0 and tr % 16 == 0 and 2 * tr * row_bytes <= ROW_VMEM_BUDGET:
            return tr
    return lp


def _rmsnorm_fwd(x, g, *, name):
    lp, d = x.shape
    tr = _row_tile(lp, d * (4 + 2))

    def body(x_ref, g_ref, o_ref):
        xv = x_ref[...]
        r = lax.rsqrt(jnp.mean(xv * xv, axis=-1, keepdims=True) + EPS)
        o_ref[...] = (xv * r * g_ref[...]).astype(BF16)

    return pl.pallas_call(
        body, grid=(lp // tr,),
        in_specs=[_rspec(tr, d), _fspec((1, d))],
        out_specs=_rspec(tr, d),
        out_shape=jax.ShapeDtypeStruct((lp, d), BF16),
        compiler_params=_cp("parallel"), name=name)(x, g)


def _residual_norm(h, t, g, *, name):
    lp, d = h.shape
    tr = _row_tile(lp, d * (4 + 4 + 4 + 2))

    def body(h_ref, t_ref, g_ref, h1_ref, m_ref):
        xv = h_ref[...] + t_ref[...]
        h1_ref[...] = xv
        r = lax.rsqrt(jnp.mean(xv * xv, axis=-1, keepdims=True) + EPS)
        m_ref[...] = (xv * r * g_ref[...]).astype(BF16)

    return pl.pallas_call(
        body, grid=(lp // tr,),
        in_specs=[_rspec(tr, d), _rspec(tr, d), _fspec((1, d))],
        out_specs=[_rspec(tr, d), _rspec(tr, d)],
        out_shape=[jax.ShapeDtypeStruct((lp, d), F32), jax.ShapeDtypeStruct((lp, d), BF16)],
        compiler_params=_cp("parallel"), name=name)(h, t, g)


def _rmsnorm_bwd_add(x, g, dy, dres, *, name):
    lp, d = x.shape
    tr = _row_tile(lp, d * 4 * 4)

    def body(x_ref, g_ref, dy_ref, dr_ref, dx_ref, dg_ref):
        @pl.when(pl.program_id(0) == 0)
        def _():
            dg_ref[...] = jnp.zeros_like(dg_ref)

        xv = x_ref[...]
        r = lax.rsqrt(jnp.mean(xv * xv, axis=-1, keepdims=True) + EPS)
        xh = xv * r
        dyv = dy_ref[...]
        dg_ref[...] += jnp.sum(dyv * xh, axis=0, keepdims=True)
        dxh = dyv * g_ref[...]
        dx_ref[...] = dr_ref[...] + r * (dxh - xh * jnp.mean(dxh * xh, axis=-1, keepdims=True))

    return pl.pallas_call(
        body, grid=(lp // tr,),
        in_specs=[_rspec(tr, d), _fspec((1, d)), _rspec(tr, d), _rspec(tr, d)],
        out_specs=[_rspec(tr, d), _fspec((1, d))],
        out_shape=[jax.ShapeDtypeStruct((lp, d), F32), jax.ShapeDtypeStruct((1, d), F32)],
        compiler_params=_cp("arbitrary"), name=name)(x, g, dy, dres)


def _final_loss(h1, t2, g, tgt, *, n_tok, name):
    lp, d = h1.shape
    tr = _row_tile(lp, d * 4 * 4)

    def body(h_ref, t_ref, g_ref, tg_ref, dh_ref, loss_ref, dg_ref):
        i = pl.program_id(0)

        @pl.when(i == 0)
        def _():
            loss_ref[...] = jnp.zeros_like(loss_ref)
            dg_ref[...] = jnp.zeros_like(dg_ref)

        xv = h_ref[...] + t_ref[...]
        r = lax.rsqrt(jnp.mean(xv * xv, axis=-1, keepdims=True) + EPS)
        xh = xv * r
        gv = g_ref[...]
        row = i * tr + lax.broadcasted_iota(jnp.int32, (tr, 1), 0)
        valid = (row >= N_META) & (row < N_META + n_tok)
        err = jnp.where(valid, xh * gv - tg_ref[...], 0.0)
        loss_ref[...] += jnp.sum(0.5 * err * err) / d
        dy = err / d
        dg_ref[...] += jnp.sum(dy * xh, axis=0, keepdims=True)
        dxh = dy * gv
        dh_ref[...] = r * (dxh - xh * jnp.mean(dxh * xh, axis=-1, keepdims=True))

    return pl.pallas_call(
        body, grid=(lp // tr,),
        in_specs=[_rspec(tr, d), _rspec(tr, d), _fspec((1, d)), _rspec(tr, d)],
        out_specs=[_rspec(tr, d), _fspec((1, LANES)), _fspec((1, d))],
        out_shape=[jax.ShapeDtypeStruct((lp, d), F32), jax.ShapeDtypeStruct((1, LANES), F32),
                   jax.ShapeDtypeStruct((1, d), F32)],
        compiler_params=_cp("arbitrary"), name=name)(h1, t2, g, tgt)


def _hg_out(o_f, o_b, proj, gain, *, col_g, name):
    lp, w = o_f.shape
    tr = _row_tile(lp, w * (3 * 4 + 2))
    hh = w // HG_DK

    def body(of_ref, ob_ref, g_ref, gain_ref, y_ref):
        gv = g_ref[...]
        sg = gv * _sigmoid(gv)
        for h in range(hh):
            sl = slice(h * HG_DK, (h + 1) * HG_DK)
            o = of_ref[:, sl] + ob_ref[:, sl]
            r = lax.rsqrt(jnp.mean(o * o, axis=-1, keepdims=True) + EPS)
            y_ref[:, sl] = (o * r * gain_ref[:, sl] * sg[:, sl]).astype(BF16)

    return pl.pallas_call(
        body, grid=(lp // tr,),
        in_specs=[_rspec(tr, w), _rspec(tr, w), _rspec(tr, w, col_g // w), _fspec((1, w))],
        out_specs=_rspec(tr, w),
        out_shape=jax.ShapeDtypeStruct((lp, w), BF16),
        compiler_params=_cp("parallel"), name=name)(o_f, o_b, proj, gain)


def _hg_out_bwd(o_f, o_b, proj, gain, dy, *, col_g, name):
    lp, w = o_f.shape
    tr = _row_tile(lp, w * (5 * 4 + 2))
    hh = w // HG_DK

    def body(of_ref, ob_ref, g_ref, gain_ref, dy_ref, do_ref, dg_ref, dgain_ref):
        @pl.when(pl.program_id(0) == 0)
        def _():
            dgain_ref[...] = jnp.zeros_like(dgain_ref)

        for h in range(hh):
            sl = slice(h * HG_DK, (h + 1) * HG_DK)
            gv = g_ref[:, sl]
            s = _sigmoid(gv)
            sg = gv * s
            dsg = s + gv * s * (1.0 - s)
            o = of_ref[:, sl] + ob_ref[:, sl]
            r = lax.rsqrt(jnp.mean(o * o, axis=-1, keepdims=True) + EPS)
            on = o * r
            dyv = dy_ref[:, sl]
            gn = gain_ref[:, sl]
            dgain_ref[:, sl] += jnp.sum(dyv * on * sg, axis=0, keepdims=True)
            dg_ref[:, sl] = (dyv * on * gn * dsg).astype(BF16)
            don = dyv * gn * sg
            do_ref[:, sl] = r * (don - on * jnp.mean(don * on, axis=-1, keepdims=True))

    return pl.pallas_call(
        body, grid=(lp // tr,),
        in_specs=[_rspec(tr, w), _rspec(tr, w), _rspec(tr, w, col_g // w), _fspec((1, w)),
                  _rspec(tr, w)],
        out_specs=[_rspec(tr, w), _rspec(tr, w), _fspec((1, w))],
        out_shape=[jax.ShapeDtypeStruct((lp, w), F32), jax.ShapeDtypeStruct((lp, w), BF16),
                   jax.ShapeDtypeStruct((1, w), F32)],
        compiler_params=_cp("arbitrary"), name=name)(o_f, o_b, proj, gain, dy)


HG_ROWS = 128
HG_HALVES = (1, 2, 4, 8, 16, 32, 64)


def _hg_gates(zq, z, lbv):
    sq = _sigmoid(zq)
    s = _sigmoid(z)
    f = lbv + (1.0 - lbv) * s
    kk = (1.0 - lbv) * (1.0 - s)
    return zq * sq, sq, s, f, jnp.log(f), kk


def _block_cumsum(g, pos, suffix):
    x = g
    for k in HG_HALVES:
        if suffix:
            x = x + jnp.where(pos < HG_ROWS - k, pltpu.roll(x, HG_ROWS - k, 0), 0.0)
        else:
            x = x + jnp.where(pos >= k, pltpu.roll(x, k, 0), 0.0)
    return x


def _pair_levels(b, pos, reverse):
    out = []
    first = b
    for m in HG_HALVES:
        if m > 1:
            first = jnp.where((pos & (m - 1)) >= m // 2, pltpu.roll(first, m // 2, 0), first)
        nxt = pltpu.roll(first, HG_ROWS - m, 0)
        upper = (pos & (2 * m - 1)) >= m
        if reverse:
            eq = jnp.where(upper, 0.0, jnp.exp(b - nxt))
            ek = jnp.where(upper, jnp.exp(first - b), 0.0)
        else:
            eq = jnp.where(upper, jnp.exp(b - first), 0.0)
            ek = jnp.where(upper, 0.0, jnp.exp(nxt - b))
        out.append((eq, ek))
    return out


def _pair_masks(mask_ref):
    ri = lax.broadcasted_iota(jnp.int32, (HG_ROWS, HG_ROWS), 0)
    ci = lax.broadcasted_iota(jnp.int32, (HG_ROWS, HG_ROWS), 1)
    for i, m in enumerate(HG_HALVES):
        sh = m.bit_length()
        mask_ref[i] = jnp.where((ri >> sh) == (ci >> sh), 1.0, 0.0)


def _hg_scan_fwd(proj, lb, *, reverse, col_q, col_z, col_i, hh, name):
    lp = proj.shape[0]
    n_blocks = lp // HG_ROWS
    last = 0 if reverse else HG_ROWS - 1

    def body(q_ref, z_ref, i_ref, lb_ref, o_ref, st_ref, mask_ref):
        lbv = lb_ref[...]
        pos = lax.broadcasted_iota(jnp.int32, (HG_ROWS, 1), 0)
        ri = lax.broadcasted_iota(jnp.int32, (HG_ROWS, HG_ROWS), 0)
        ci = lax.broadcasted_iota(jnp.int32, (HG_ROWS, HG_ROWS), 1)
        _pair_masks(mask_ref)

        def block(bi, st):
            bb = (n_blocks - 1 - bi) if reverse else bi
            r0 = pl.multiple_of(bb * HG_ROWS, HG_ROWS)
            v16 = i_ref[pl.ds(r0, HG_ROWS), :].astype(BF16)
            qh, _, _, _, g, kk = _hg_gates(q_ref[pl.ds(r0, HG_ROWS), :],
                                           z_ref[pl.ds(r0, HG_ROWS), :], lbv)
            b = _block_cumsum(g, pos, reverse)
            bl = b[last:last + 1, :]
            qe = (qh * jnp.exp(b)).astype(BF16)
            kd = (kk * jnp.exp(bl - b)).astype(BF16)
            a = jnp.where(ri == ci, jnp.sum(qh * kk, axis=1, keepdims=True), 0.0)
            for i, (eq, ek) in enumerate(_pair_levels(b, pos, reverse)):
                a = a + mask_ref[i] * _nt((qh * eq).astype(BF16), (kk * ek).astype(BF16))
            st_ref[bb] = st
            o_ref[pl.ds(r0, HG_ROWS), :] = _nn(a.astype(BF16), v16) + _nt(qe, st.astype(BF16))
            return jnp.exp(bl) * st + _tn(v16, kd)

        lax.fori_loop(0, n_blocks, block, jnp.zeros((HG_DK, HG_DK), F32))

    cspec = lambda col: pl.BlockSpec((lp, HG_DK), lambda h: (0, col // HG_DK + h))
    return pl.pallas_call(
        body, grid=(hh,),
        in_specs=[cspec(col_q), cspec(col_z), cspec(col_i),
                  pl.BlockSpec((None, 1, HG_DK), lambda h: (h, 0, 0))],
        out_specs=[pl.BlockSpec((lp, HG_DK), lambda h: (0, h)),
                   pl.BlockSpec((None, n_blocks, HG_DK, HG_DK), lambda h: (h, 0, 0, 0))],
        out_shape=[jax.ShapeDtypeStruct((lp, hh * HG_DK), F32),
                   jax.ShapeDtypeStruct((hh, n_blocks, HG_DK, HG_DK), F32)],
        scratch_shapes=[pltpu.VMEM((len(HG_HALVES), HG_ROWS, HG_ROWS), F32)],
        compiler_params=_cp("parallel"), name=name)(proj, proj, proj, lb)


def _hg_scan_bwd(proj, lb, states, do, *, reverse, col_q, col_z, col_i, hh, name):
    lp = proj.shape[0]
    n_blocks = lp // HG_ROWS
    last = 0 if reverse else HG_ROWS - 1

    def body(q_ref, z_ref, i_ref, lb_ref, st_ref, do_ref, dq_ref, dz_ref, dv_ref, dlb_ref, mask_ref):
        lbv = lb_ref[...]
        pos = lax.broadcasted_iota(jnp.int32, (HG_ROWS, 1), 0)
        ri = lax.broadcasted_iota(jnp.int32, (HG_ROWS, HG_ROWS), 0)
        ci = lax.broadcasted_iota(jnp.int32, (HG_ROWS, HG_ROWS), 1)
        _pair_masks(mask_ref)

        def block(bi, carry):
            dst, dlb = carry
            bb = bi if reverse else (n_blocks - 1 - bi)
            r0 = pl.multiple_of(bb * HG_ROWS, HG_ROWS)
            zq = q_ref[pl.ds(r0, HG_ROWS), :]
            v16 = i_ref[pl.ds(r0, HG_ROWS), :].astype(BF16)
            do16 = do_ref[pl.ds(r0, HG_ROWS), :].astype(BF16)
            qh, sq, s, f, g, kk = _hg_gates(zq, z_ref[pl.ds(r0, HG_ROWS), :], lbv)
            b = _block_cumsum(g, pos, reverse)
            bl = b[last:last + 1, :]
            eb = jnp.exp(b)
            ebl = jnp.exp(bl - b)
            decay = jnp.exp(bl)
            qe16 = (qh * eb).astype(BF16)
            kd16 = (kk * ebl).astype(BF16)
            st = st_ref[bb]
            st16, dst16 = st.astype(BF16), dst.astype(BF16)
            same_row = ri == ci
            da = _nt(do16, v16)
            da_diag = jnp.sum(jnp.where(same_row, da, 0.0), axis=1, keepdims=True)
            dq_state = eb * _nn(do16, st16)
            dk_state = ebl * _nn(v16, dst16)
            dq = dq_state + da_diag * kk
            dk = dk_state + da_diag * qh
            dbl = (decay * jnp.sum(st * dst, axis=0, keepdims=True)
                   + jnp.sum(kk * dk_state, axis=0, keepdims=True))
            db = qh * dq_state - kk * dk_state + jnp.where(pos == last, dbl, 0.0)
            a = jnp.where(same_row, jnp.sum(qh * kk, axis=1, keepdims=True), 0.0)
            for i, (eq, ek) in enumerate(_pair_levels(b, pos, reverse)):
                same = mask_ref[i]
                q16, k16 = (qh * eq).astype(BF16), (kk * ek).astype(BF16)
                a = a + same * _nt(q16, k16)
                da16 = (same * da).astype(BF16)
                gq, gk = _nn(da16, k16), _tn(da16, q16)
                dq = dq + eq * gq
                dk = dk + ek * gk
                db = db + (q16.astype(F32) * gq - k16.astype(F32) * gk)
            dg = _block_cumsum(db, pos, not reverse)
            df = dg / f - dk
            dq_ref[pl.ds(r0, HG_ROWS), :] = dq * (sq + zq * sq * (1.0 - sq))
            dz_ref[pl.ds(r0, HG_ROWS), :] = df * (1.0 - lbv) * s * (1.0 - s)
            dv_ref[pl.ds(r0, HG_ROWS), :] = _nt(kd16, dst16) + _tn(a.astype(BF16), do16)
            return (decay * dst + _tn(do16, qe16),
                    dlb + jnp.sum(df * (1.0 - s), axis=0, keepdims=True))

        _, dlb = lax.fori_loop(0, n_blocks, block,
                               (jnp.zeros((HG_DK, HG_DK), F32), jnp.zeros((1, HG_DK), F32)))
        dlb_ref[...] = dlb

    cspec = lambda col: pl.BlockSpec((lp, HG_DK), lambda h: (0, col // HG_DK + h))
    ospec = pl.BlockSpec((lp, HG_DK), lambda h: (0, h))
    sds = jax.ShapeDtypeStruct((lp, hh * HG_DK), F32)
    return pl.pallas_call(
        body, grid=(hh,),
        in_specs=[cspec(col_q), cspec(col_z), cspec(col_i),
                  pl.BlockSpec((None, 1, HG_DK), lambda h: (h, 0, 0)),
                  pl.BlockSpec((None, n_blocks, HG_DK, HG_DK), lambda h: (h, 0, 0, 0)),
                  ospec],
        out_specs=[ospec, ospec, ospec, pl.BlockSpec((None, 1, HG_DK), lambda h: (h, 0, 0))],
        out_shape=[sds, sds, sds, jax.ShapeDtypeStruct((hh, 1, HG_DK), F32)],
        scratch_shapes=[pltpu.VMEM((len(HG_HALVES), HG_ROWS, HG_ROWS), F32)],
        compiler_params=_cp("parallel"), name=name)(proj, proj, proj, lb, states, do)


NA_HB = LANES // NA_HEAD_DIM
NA_G = 4
NA_U = NA_G + NA_WIN_H
NA_QN = NA_G * GRID_W
NA_KN = NA_U * GRID_W


def _na_table_index(pattern, a, j):
    if pattern == 0:
        return j - a + NA_WIN_H - 1 if j < NA_WIN_H else None
    if pattern == 2:
        return j - a - 1 if j >= NA_U - NA_WIN_H else None
    return j - a + NA_WIN_H // 2 - 1 if a <= j < a + NA_WIN_H else None


def _na_step_rows(pattern, t, rows):
    if pattern == 0:
        r0, us = 0, 0
    elif pattern == 2:
        r0, us = rows - NA_G, rows - NA_U
    else:
        r0 = NA_G * t
        us = r0 - NA_WIN_H // 2
    q0, k0 = N_META + GRID_W * r0, N_META + GRID_W * us
    if pattern == 1:
        q0, k0 = pl.multiple_of(q0, 16), pl.multiple_of(k0, 16)
    return q0, k0


def _na_fill_bias(tb_ref, bias_ref):
    neg = jnp.full((GRID_W, GRID_W), -1e30, F32)
    for h in range(NA_HB):
        for pattern in range(3):
            for a in range(NA_G):
                for j in range(NA_U):
                    idx = _na_table_index(pattern, a, j)
                    bias_ref[h, pattern, a * GRID_W:(a + 1) * GRID_W, j * GRID_W:(j + 1) * GRID_W] = (
                        neg if idx is None else tb_ref[h, idx])


def _na_steps(rows, step, carry):
    n_steps = rows // NA_G
    carry = step(0, 0, carry)
    carry = lax.fori_loop(1, n_steps - 1, functools.partial(step, 1), carry)
    return step(2, n_steps - 1, carry)


def _na_head_lanes():
    lane = lax.broadcasted_iota(jnp.int32, (1, LANES), 1)
    return [lane // NA_HEAD_DIM == h for h in range(NA_HB)]


def _na_only(mask, x):
    return jnp.where(mask, x, jnp.zeros_like(x))


def _na_fwd(proj, tb, *, n_tok, nh, name):
    lp = proj.shape[0]
    dh, hb = NA_HEAD_DIM, NA_HB
    naw = nh * dh
    rows = n_tok // GRID_W
    scale = dh ** -0.5

    def body(q_ref, k_ref, v_ref, tb_ref, o_ref, lse_ref, q16_ref, k16_ref, v16_ref, bias_ref):
        o_ref[...] = jnp.zeros_like(o_ref)
        lse_ref[...] = jnp.zeros_like(lse_ref)
        q16_ref[...] = q_ref[...].astype(BF16)
        k16_ref[...] = k_ref[...].astype(BF16)
        v16_ref[...] = v_ref[...].astype(BF16)
        _na_fill_bias(tb_ref, bias_ref)
        heads = _na_head_lanes()
        km = k16_ref[0:N_META, :]
        vm = v16_ref[0:N_META, :]
        qm = q16_ref[0:N_META, :]
        o_m = None
        for h in range(hb):
            s = _nt(_na_only(heads[h], qm), km) * scale
            m = jnp.max(s, axis=1, keepdims=True)
            p = jnp.exp(s - m)
            l = jnp.sum(p, axis=1, keepdims=True)
            o_h = _nn(p.astype(BF16), vm) / l
            o_m = o_h if o_m is None else jnp.where(heads[h], o_h, o_m)
            lse_ref[h, 0:N_META, :] = m + jnp.log(l)
        o_ref[0:N_META, :] = o_m

        def step(pattern, t, carry):
            q0, k0 = _na_step_rows(pattern, t, rows)
            q16 = q16_ref[pl.ds(q0, NA_QN), :]
            k16 = k16_ref[pl.ds(k0, NA_KN), :]
            v16 = v16_ref[pl.ds(k0, NA_KN), :]
            o = None
            for h in range(hb):
                q_h = _na_only(heads[h], q16)
                s = _nt(q_h, k16) * scale + bias_ref[h, pattern]
                sm = _nt(q_h, km) * scale
                m = jnp.maximum(jnp.max(s, axis=1, keepdims=True),
                                jnp.max(sm, axis=1, keepdims=True))
                p = jnp.exp(s - m)
                pm = jnp.exp(sm - m)
                l = jnp.sum(p, axis=1, keepdims=True) + jnp.sum(pm, axis=1, keepdims=True)
                o_h = (_nn(p.astype(BF16), v16) + _nn(pm.astype(BF16), vm)) / l
                o = o_h if o is None else jnp.where(heads[h], o_h, o)
                lse_ref[h, pl.ds(q0, NA_QN), :] = m + jnp.log(l)
            o_ref[pl.ds(q0, NA_QN), :] = o
            return carry

        _na_steps(rows, step, 0)

    cblk = lambda col: pl.BlockSpec((lp, LANES), lambda g: (0, col // LANES + g))
    return pl.pallas_call(
        body, grid=(nh // hb,),
        in_specs=[cblk(0), cblk(naw), cblk(2 * naw),
                  pl.BlockSpec((hb, 2 * NA_WIN_H - 1, GRID_W, GRID_W), lambda g: (g, 0, 0, 0))],
        out_specs=[cblk(0), pl.BlockSpec((hb, lp, 1), lambda g: (g, 0, 0))],
        out_shape=[jax.ShapeDtypeStruct((lp, naw), F32), jax.ShapeDtypeStruct((nh, lp, 1), F32)],
        scratch_shapes=[pltpu.VMEM((lp, LANES), BF16)] * 3 + [pltpu.VMEM((hb, 3, NA_QN, NA_KN), F32)],
        compiler_params=_cp("parallel"), name=name)(proj, proj, proj, tb)


def _na_bwd(proj, tb, o, lse, do, *, n_tok, nh, name):
    lp = proj.shape[0]
    dh, hb = NA_HEAD_DIM, NA_HB
    naw = nh * dh
    rows = n_tok // GRID_W
    scale = dh ** -0.5

    def body(q_ref, k_ref, v_ref, tb_ref, o_ref, lse_ref, do_ref, dq_ref, dk_ref, dv_ref, dtb_ref,
             q16_ref, k16_ref, v16_ref, bias_ref):
        dq_ref[...] = jnp.zeros_like(dq_ref)
        dk_ref[...] = jnp.zeros_like(dk_ref)
        dv_ref[...] = jnp.zeros_like(dv_ref)
        dtb_ref[...] = jnp.zeros_like(dtb_ref)
        q16_ref[...] = q_ref[...].astype(BF16)
        k16_ref[...] = k_ref[...].astype(BF16)
        v16_ref[...] = v_ref[...].astype(BF16)
        _na_fill_bias(tb_ref, bias_ref)
        heads = _na_head_lanes()
        km = k16_ref[0:N_META, :]
        vm = v16_ref[0:N_META, :]
        qm = q16_ref[0:N_META, :]
        dom = do_ref[0:N_META, :]
        prod = dom * o_ref[0:N_META, :]
        dq_m = None
        dkm0 = jnp.zeros((N_META, LANES), F32)
        dvm0 = jnp.zeros((N_META, LANES), F32)
        for h in range(hb):
            q_h = _na_only(heads[h], qm)
            do_h = _na_only(heads[h], dom).astype(BF16)
            p = jnp.exp(_nt(q_h, km) * scale - lse_ref[h, 0:N_META, :])
            delta = jnp.sum(_na_only(heads[h], prod), axis=1, keepdims=True)
            ds = (p * (_nt(do_h, vm) - delta)).astype(BF16)
            dq_h = _nn(ds, km) * scale
            dq_m = dq_h if dq_m is None else jnp.where(heads[h], dq_h, dq_m)
            dkm0 = dkm0 + _tn(ds, q_h) * scale
            dvm0 = dvm0 + _tn(p.astype(BF16), do_h)
        dq_ref[0:N_META, :] = dq_m

        def step(pattern, t, carry):
            dkm, dvm = carry
            q0, k0 = _na_step_rows(pattern, t, rows)
            q16 = q16_ref[pl.ds(q0, NA_QN), :]
            k16 = k16_ref[pl.ds(k0, NA_KN), :]
            v16 = v16_ref[pl.ds(k0, NA_KN), :]
            dov = do_ref[pl.ds(q0, NA_QN), :]
            prod = dov * o_ref[pl.ds(q0, NA_QN), :]
            dq = None
            dk = jnp.zeros((NA_KN, LANES), F32)
            dv = jnp.zeros((NA_KN, LANES), F32)
            for h in range(hb):
                q_h = _na_only(heads[h], q16)
                do_h = _na_only(heads[h], dov).astype(BF16)
                lse = lse_ref[h, pl.ds(q0, NA_QN), :]
                p = jnp.exp(_nt(q_h, k16) * scale + bias_ref[h, pattern] - lse)
                pm = jnp.exp(_nt(q_h, km) * scale - lse)
                delta = jnp.sum(_na_only(heads[h], prod), axis=1, keepdims=True)
                ds = p * (_nt(do_h, v16) - delta)
                dsm = (pm * (_nt(do_h, vm) - delta)).astype(BF16)
                ds16 = ds.astype(BF16)
                dq_h = (_nn(ds16, k16) + _nn(dsm, km)) * scale
                dq = dq_h if dq is None else jnp.where(heads[h], dq_h, dq)
                dk = dk + _tn(ds16, q_h) * scale
                dv = dv + _tn(p.astype(BF16), do_h)
                dkm = dkm + _tn(dsm, q_h) * scale
                dvm = dvm + _tn(pm.astype(BF16), do_h)
                for a in range(NA_G):
                    for j in range(NA_U):
                        idx = _na_table_index(pattern, a, j)
                        if idx is not None:
                            dtb_ref[h, idx] += ds[a * GRID_W:(a + 1) * GRID_W,
                                                  j * GRID_W:(j + 1) * GRID_W]
            dq_ref[pl.ds(q0, NA_QN), :] = dq
            dk_ref[pl.ds(k0, NA_KN), :] += dk
            dv_ref[pl.ds(k0, NA_KN), :] += dv
            return dkm, dvm

        dkm, dvm = _na_steps(rows, step, (dkm0, dvm0))
        dk_ref[0:N_META, :] += dkm
        dv_ref[0:N_META, :] += dvm

    cblk = lambda col: pl.BlockSpec((lp, LANES), lambda g: (0, col // LANES + g))
    tbs = pl.BlockSpec((hb, 2 * NA_WIN_H - 1, GRID_W, GRID_W), lambda g: (g, 0, 0, 0))
    sds = jax.ShapeDtypeStruct((lp, naw), F32)
    return pl.pallas_call(
        body, grid=(nh // hb,),
        in_specs=[cblk(0), cblk(naw), cblk(2 * naw), tbs, cblk(0),
                  pl.BlockSpec((hb, lp, 1), lambda g: (g, 0, 0)), cblk(0)],
        out_specs=[cblk(0), cblk(0), cblk(0), tbs],
        out_shape=[sds, sds, sds, jax.ShapeDtypeStruct(tb.shape, F32)],
        scratch_shapes=[pltpu.VMEM((lp, LANES), BF16)] * 3 + [pltpu.VMEM((hb, 3, NA_QN, NA_KN), F32)],
        compiler_params=_cp("parallel"), name=name)(proj, proj, proj, tb, o, lse, do)


def _rpb_onehot():
    c = np.arange(GRID_W)[:, None]
    w = np.arange(GRID_W)[None, :]
    cs = np.clip(c - NA_WIN_W // 2, 0, GRID_W - NA_WIN_W)
    in_win = (w >= cs) & (w < cs + NA_WIN_W)
    dc = np.clip(w - c, -(NA_WIN_W - 1), NA_WIN_W - 1) + NA_WIN_W - 1
    oh = np.zeros((LANES, GRID_W * GRID_W), np.float32)
    flat = np.arange(GRID_W * GRID_W).reshape(GRID_W, GRID_W)
    oh[dc[in_win], flat[in_win]] = 1.0
    neg = np.where(in_win, 0.0, -1e30).astype(np.float32).reshape(1, -1)
    return oh, neg


def _assemble_dproj(dq_na, dk_na, dv_na, dq_f, dq_b, dz_f, dz_b, dv_f, dv_b, dg, dgn, dgh, *, name):
    lp, naw = dq_na.shape
    hgw = dq_f.shape[1]
    d = dgn.shape[1]
    cols = 3 * naw + 5 * hgw + 2 * d
    tr = _row_tile(lp, 3 * naw * 4 + 6 * hgw * 4 + hgw * 2 + 2 * d * 2 + cols * 2)

    def body(nq_ref, nk_ref, nv_ref, qf_ref, qb_ref, zf_ref, zb_ref, vf_ref, vb_ref, g_ref, gn_ref,
             gh_ref, o_ref):
        o_ref[:, 0:naw] = nq_ref[...].astype(BF16)
        o_ref[:, naw:2 * naw] = nk_ref[...].astype(BF16)
        o_ref[:, 2 * naw:3 * naw] = nv_ref[...].astype(BF16)
        c0 = 3 * naw
        o_ref[:, c0:c0 + hgw] = (qf_ref[...] + qb_ref[...]).astype(BF16)
        o_ref[:, c0 + hgw:c0 + 2 * hgw] = zf_ref[...].astype(BF16)
        o_ref[:, c0 + 2 * hgw:c0 + 3 * hgw] = zb_ref[...].astype(BF16)
        o_ref[:, c0 + 3 * hgw:c0 + 4 * hgw] = (vf_ref[...] + vb_ref[...]).astype(BF16)
        o_ref[:, c0 + 4 * hgw:c0 + 5 * hgw] = g_ref[...]
        o_ref[:, c0 + 5 * hgw:c0 + 5 * hgw + d] = gn_ref[...]
        o_ref[:, c0 + 5 * hgw + d:] = gh_ref[...]

    hg, na = _rspec(tr, hgw), _rspec(tr, naw)
    return pl.pallas_call(
        body, grid=(lp // tr,),
        in_specs=[na, na, na, hg, hg, hg, hg, hg, hg, hg, _rspec(tr, d), _rspec(tr, d)],
        out_specs=_rspec(tr, cols),
        out_shape=jax.ShapeDtypeStruct((lp, cols), BF16),
        compiler_params=_cp("parallel"), name=name)(dq_na, dk_na, dv_na, dq_f, dq_b, dz_f, dz_b,
                                                    dv_f, dv_b, dg, dgn, dgh)


GROUP_STEPS = 8


def _group_tiles(rows, align):
    steps = GROUP_STEPS if all(r % (GROUP_STEPS * align) == 0 for r in rows) else 1
    return steps, [r // steps for r in rows]


def _adamw(ws, gs, ms, vs, *, name):
    n = len(ws)
    steps, trs = _group_tiles([w.shape[0] for w in ws], 8)

    def body(*refs):
        for i in range(n):
            w_ref, g_ref, m_ref, v_ref = refs[4 * i:4 * i + 4]
            d_ref, mo_ref, vo_ref = refs[4 * n + 3 * i:4 * n + 3 * i + 3]
            gv = g_ref[...]
            mn = ADAM_B1 * m_ref[...] + (1.0 - ADAM_B1) * gv
            vn = ADAM_B2 * v_ref[...] + (1.0 - ADAM_B2) * (gv * gv)
            m_hat = mn / (1.0 - ADAM_B1 ** ADAM_STEP)
            v_hat = vn / (1.0 - ADAM_B2 ** ADAM_STEP)
            d_ref[...] = -ADAM_LR * (m_hat / (jnp.sqrt(v_hat) + ADAM_EPS) + ADAM_WD * w_ref[...])
            mo_ref[...] = mn
            vo_ref[...] = vn

    specs = [_rspec(tr, w.shape[1]) for tr, w in zip(trs, ws)]
    out = pl.pallas_call(
        body, grid=(steps,),
        in_specs=[s for s in specs for _ in range(4)],
        out_specs=[s for s in specs for _ in range(3)],
        out_shape=[jax.ShapeDtypeStruct(w.shape, F32) for w in ws for _ in range(3)],
        compiler_params=_cp("parallel"), name=name)(*[a for q in zip(ws, gs, ms, vs) for a in q])
    return [tuple(out[3 * i:3 * i + 3]) for i in range(n)]


def _local_step(x, tgt, meta, first_weight, rest_weights, g_mix, g_mlp, g_fin, hg_gain, rpb, lb,
                early_grads=None, mid_grads=None, late_grad=None, rest_landed=None,
                last_grads=None):
    n_tok, d = x.shape
    hgw = hg_gain.shape[1]
    nh, hh = rpb.shape[0], hgw // HG_DK
    naw = nh * NA_HEAD_DIM
    l_real = N_META + n_tok
    lp = -(-l_real // ROW_ALIGN) * ROW_ALIGN
    n_chunks = l_real // HG_CHUNK
    pad = lp - l_real
    col_qhg = 3 * naw
    col_zf, col_zb, col_i, col_g = (col_qhg + hgw, col_qhg + 2 * hgw, col_qhg + 3 * hgw,
                                    col_qhg + 4 * hgw)
    col_gate = col_qhg + 5 * hgw

    zpad = jnp.zeros((pad, d), F32)
    h0 = jnp.concatenate([meta, x, zpad], axis=0)
    tgt_p = jnp.concatenate([jnp.zeros((N_META, d), F32), tgt, zpad], axis=0)

    oh_np, neg_np = _rpb_onehot()
    oh = jnp.asarray(oh_np)
    rpb_p = jnp.pad(rpb.reshape(nh * (2 * NA_WIN_H - 1), 2 * NA_WIN_W - 1),
                    ((0, 0), (0, LANES - (2 * NA_WIN_W - 1))))
    tb = _matmul(rpb_p, oh, tm=rpb_p.shape[0], tn=512, tk=LANES, precision=HIGHEST,
                 name="rpb_expand")
    tb = (tb + jnp.asarray(neg_np)).reshape(nh, 2 * NA_WIN_H - 1, GRID_W, GRID_W)

    a = _rmsnorm_fwd(h0, g_mix, name="norm_mix")
    w_in = first_weight(a)
    proj = _matmul(a, w_in, name="mm_in")
    o_na, lse = _na_fwd(proj, tb, n_tok=n_tok, nh=nh, name="na_fwd")
    lb_f = lb[0].reshape(hh, 1, HG_DK)
    lb_b = lb[1].reshape(hh, 1, HG_DK)
    scan_kw = dict(col_q=col_qhg, col_i=col_i, hh=hh)
    o_f, st_f = _hg_scan_fwd(proj, lb_f, reverse=False, col_z=col_zf, name="hg_scan_f", **scan_kw)
    token = rest_landed(o_f) if rest_landed else None
    lb_b_late = lb_b if token is None else lb_b + token[0:1, 0:1]
    o_b, st_b = _hg_scan_fwd(proj, lb_b_late, reverse=True, col_z=col_zb, name="hg_scan_b",
                             **scan_kw)
    o_hg = _hg_out(o_f, o_b, proj, hg_gain, col_g=col_g, name="hg_out")
    w_na, w_hg, w_o, w_up, w_down = rest_weights(o_hg)
    y_na = _matmul(o_na, w_na, name="mm_na_out")
    gates = ((proj, col_gate), (proj, col_gate + d))

    def mix_gates(acc, gn, gh, yn):
        return acc, _sigmoid(gn) * yn + _sigmoid(gh) * acc

    def mix_gates_bwd(dmix, gn, gh, yn, yh):
        sn, sh = _sigmoid(gn), _sigmoid(gh)
        return dmix * sn, dmix * sh, dmix * yn * sn * (1.0 - sn), dmix * yh * sh * (1.0 - sh)

    y_hg, mix = _matmul(o_hg, w_hg, name="mm_hg_out", epilogue=mix_gates,
                        tiles=(*gates, (y_na, 0)), out_dtypes=(F32, BF16))
    t1 = _matmul(mix, w_o, name="mm_o")
    h1, mlp_in = _residual_norm(h0, t1, g_mlp, name="resid_norm_mlp")
    u, act = _matmul(mlp_in, w_up, name="mm_up", out_dtypes=(BF16, BF16),
                     epilogue=lambda acc: (acc, jnp.square(jnp.maximum(acc, 0.0))))
    t2 = _matmul(act, w_down, name="mm_down")
    dh2, loss, dg_fin = _final_loss(h1, t2, g_fin, tgt_p, n_tok=n_tok, name="final_loss")

    (du,) = _matmul(dh2, w_down, tb=True, name="mm_down_dx", tiles=((u, 0),), out_dtypes=(BF16,),
                    epilogue=lambda acc, uv: (acc * 2.0 * jnp.maximum(uv, 0.0),))
    dw_down = _matmul(act, dh2, ta=True, name="mm_down_dw")
    dm = _matmul(du, w_up, tb=True, name="mm_up_dx")
    dw_up = _matmul(mlp_in, du, ta=True, name="mm_up_dw")
    dh1, dg_mlp = _rmsnorm_bwd_add(h1, g_mlp, dm, dh2, name="norm_mlp_bwd")
    dy_na, dy_hg, dgn, dgh = _matmul(dh1, w_o, tb=True, name="mm_o_dx", epilogue=mix_gates_bwd,
                                     tiles=(*gates, (y_na, 0), (y_hg, 0)), out_dtypes=(BF16,) * 4)
    dw_o = _matmul(mix, dh1, ta=True, name="mm_o_dw")
    do_na = _matmul(dy_na, w_na, tb=True, name="mm_na_out_dx")
    dw_na = _matmul(o_na, dy_na, ta=True, name="mm_na_out_dw")
    do_hg = _matmul(dy_hg, w_hg, tb=True, name="mm_hg_out_dx")
    dw_hg = _matmul(o_hg, dy_hg, ta=True, name="mm_hg_out_dw")
    token = early_grads([dw_na, dw_hg, dw_o, dw_up, dw_down]) if early_grads else None
    if token is not None:
        hg_gain = hg_gain + token[0:1, 0:1]
    d_o, dg_hg, d_gain = _hg_out_bwd(o_f, o_b, proj, hg_gain, do_hg, col_g=col_g, name="hg_out_bwd")
    dq_f, dz_f, dv_f, dlb_f = _hg_scan_bwd(proj, lb_f, st_f, d_o, reverse=False, col_z=col_zf,
                                           name="hg_scan_f_bwd", **scan_kw)
    token = mid_grads(dq_f) if mid_grads else None
    lb_b_late = lb_b if token is None else lb_b + token[0:1, 0:1]
    dq_b, dz_b, dv_b, dlb_b = _hg_scan_bwd(proj, lb_b_late, st_b, d_o, reverse=True, col_z=col_zb,
                                           name="hg_scan_b_bwd", **scan_kw)
    dq_na, dk_na, dv_na, dtb = _na_bwd(proj, tb, o_na, lse, do_na, n_tok=n_tok, nh=nh, name="na_bwd")
    dproj = _assemble_dproj(dq_na, dk_na, dv_na, dq_f, dq_b, dz_f, dz_b, dv_f, dv_b, dg_hg, dgn,
                            dgh, name="assemble_dproj")
    dw_in = _matmul(a, dproj, ta=True, name="mm_in_dw")
    token = late_grad(dw_in) if late_grad else None
    da = _matmul(dproj, w_in, tb=True, name="mm_in_dx", after=token)
    token = last_grads(da) if last_grads else None
    g_mix_late = g_mix if token is None else g_mix + token[0:1, 0:1]
    dh0, dg_mix = _rmsnorm_bwd_add(h0, g_mix_late, da, dh1, name="norm_mix_bwd")
    d_rpb = _matmul(dtb.reshape(nh * (2 * NA_WIN_H - 1), GRID_W * GRID_W), oh, tb=True,
                    tm=nh * (2 * NA_WIN_H - 1), tn=LANES, tk=1024, precision=HIGHEST,
                    name="rpb_reduce")
    d_lb = jnp.concatenate([dlb_f.reshape(1, hgw), dlb_b.reshape(1, hgw)], axis=0)
    return (loss, dh0[N_META:l_real], dh0[:N_META], dw_in, dw_na, dw_hg, dw_o, dw_up, dw_down,
            dg_mix, dg_mlp, dg_fin, d_gain, d_rpb, d_lb)


N_CHIPS = 4
N_DEV = 8
ANY = pl.BlockSpec(memory_space=pl.ANY)


def _place():
    x, y, c = lax.axis_index("x"), lax.axis_index("y"), lax.axis_index("c")
    others = []
    for j in (1, 2, 3):
        tx = (1 - x) if (j >> 1) else x
        ty = (1 - y) if (j & 1) else y
        others.append((tx, ty))
    return x, y, c, others


def _piece(ref, axis, k, half, rh, cs):
    if axis == 1:
        return ref.at[pl.ds(pl.multiple_of(half * rh, 16), rh), pl.ds(pl.multiple_of(k * cs, LANES), cs)]
    return ref.at[pl.ds(pl.multiple_of(k * 2 * rh + half * rh, 16), rh), :]


def _cast_into_full(shards, axes, place, *, name):
    n = len(shards)
    steps, trs = _group_tiles([s.shape[0] for s in shards], 16)

    def body(p_ref, *refs):
        for i in range(n):
            refs[n + i][...] = refs[i][...].astype(BF16)

    def out_spec(tr, cs, axis):
        if axis == 1:
            return pl.BlockSpec((tr, cs), lambda i, p_ref: (i, p_ref[0]))
        return pl.BlockSpec((tr, cs), lambda i, p_ref: (p_ref[0] * steps + i, 0))

    return pl.pallas_call(
        body,
        grid_spec=pltpu.PrefetchScalarGridSpec(
            num_scalar_prefetch=1, grid=(steps,),
            in_specs=[pl.BlockSpec((tr, s.shape[1]), lambda i, p_ref: (i, 0))
                      for tr, s in zip(trs, shards)],
            out_specs=[out_spec(tr, s.shape[1], ax) for tr, s, ax in zip(trs, shards, axes)]),
        out_shape=[jax.ShapeDtypeStruct((s.shape[0], s.shape[1] * N_CHIPS) if ax == 1
                                        else (s.shape[0] * N_CHIPS, s.shape[1]), BF16)
                   for s, ax in zip(shards, axes)],
        compiler_params=_cp("parallel"), name=name)(place, *shards)


HBM_SPEC = pl.BlockSpec(memory_space=pltpu.HBM)
SEM_SPEC = pl.BlockSpec(memory_space=pltpu.SEMAPHORE)
SPLIT_COPY = pltpu.CompilerParams(has_side_effects=pltpu.SideEffectType.DATAFLOW_SIDE_EFFECTING)
TOKEN = jax.ShapeDtypeStruct((8, LANES), F32)


def _geo(fulls, axes):
    out = []
    for f, ax in zip(fulls, axes):
        r, cs = (f.shape[0], f.shape[1] // N_CHIPS) if ax == 1 else (f.shape[0] // N_CHIPS, f.shape[1])
        out.append((ax, r // 2, cs))
    return out


def _gather_copies(refs, geo, send_sems, recv_sems):
    x, y, c, others = _place()
    chip = 2 * x + y
    cps = []
    for i, (ax, rh, cs) in enumerate(geo):
        mine = _piece(refs[i], ax, chip, c, rh, cs)
        for j, (tx, ty) in enumerate(others):
            cps.append(pltpu.make_async_remote_copy(
                src_ref=mine, dst_ref=mine, send_sem=send_sems.at[3 * i + j],
                recv_sem=recv_sems.at[3 * i + j], device_id=(tx, ty, c), device_id_type=MESH))
    return cps


def _allgather_start(fulls, axes, after, *, name):
    n = len(fulls)
    geo = _geo(fulls, axes)

    def body(*refs):
        w_refs = refs[:n]
        send_sems, recv_sems = refs[n + 1], refs[n + 2]
        token = refs[2 * n + 3]
        for cp in _gather_copies(w_refs, geo, send_sems, recv_sems):
            cp.start()
        token[...] = jnp.zeros_like(token)

    out = pl.pallas_call(
        body, name=name,
        out_shape=(pltpu.SemaphoreType.DMA((3 * n,)), pltpu.SemaphoreType.DMA((3 * n,)),
                   *[pltpu.HBM(f.shape, f.dtype) for f in fulls], TOKEN),
        in_specs=[HBM_SPEC] * n + [ANY],
        out_specs=(SEM_SPEC, SEM_SPEC, *[HBM_SPEC] * n, pl.BlockSpec(memory_space=pltpu.VMEM)),
        input_output_aliases={i: 2 + i for i in range(n)},
        compiler_params=SPLIT_COPY,
    )(*[pltpu.with_memory_space_constraint(f, pltpu.HBM) for f in fulls], after)
    return out[0], out[1], list(out[2:2 + n]), out[2 + n]


def _allgather_wait(send_sems, recv_sems, fulls, axes, after, *, name):
    n = len(fulls)
    geo = _geo(fulls, axes)

    def body(*refs):
        w_refs = refs[:n]
        for cp in _gather_copies(w_refs, geo, refs[n], refs[n + 1]):
            cp.wait_send()
            cp.wait_recv()

    return list(pl.pallas_call(
        body, name=name,
        out_shape=[pltpu.HBM(f.shape, f.dtype) for f in fulls],
        in_specs=[HBM_SPEC] * n + [SEM_SPEC, SEM_SPEC, ANY],
        out_specs=[HBM_SPEC] * n,
        input_output_aliases={i: i for i in range(n)},
        compiler_params=SPLIT_COPY,
    )(*fulls, send_sems, recv_sems, after))


def _allgather_forward(fulls, axes, *, name):
    n = len(fulls)
    geo = _geo(fulls, axes)

    def body(*refs):
        o_refs = refs[n:2 * n]
        send_sems, recv_sems = refs[2 * n:]
        x, y, c, others = _place()

        def rcopy(i, j, half, to):
            ax, rh, cs = geo[i]
            ref = _piece(o_refs[i], ax, 2 * others[j][0] + others[j][1], half, rh, cs)
            return pltpu.make_async_remote_copy(
                src_ref=ref, dst_ref=ref, send_sem=send_sems.at[3 * i + j],
                recv_sem=recv_sems.at[3 * i + j], device_id=to, device_id_type=MESH)

        cps = [rcopy(i, j, c, (x, y, 1 - c)) for i in range(n) for j in range(3)]
        for cp in cps:
            cp.start()
        for i in range(n):
            for j in range(3):
                rcopy(i, j, 1 - c, (x, y, c)).wait_recv()
        for cp in cps:
            cp.wait_send()

    return list(pl.pallas_call(
        body, in_specs=[ANY] * n, out_specs=[ANY] * n,
        out_shape=[jax.ShapeDtypeStruct(f.shape, f.dtype) for f in fulls],
        input_output_aliases={i: i for i in range(n)},
        scratch_shapes=[pltpu.SemaphoreType.DMA((3 * n,)), pltpu.SemaphoreType.DMA((3 * n,))],
        name=name)(*fulls))


def _forward_copies(refs, geo, send_sems, recv_sems):
    x, y, c, others = _place()
    cps = []
    for i, (ax, rh, cs) in enumerate(geo):
        for j, (tx, ty) in enumerate(others):
            ref = _piece(refs[i], ax, 2 * tx + ty, c, rh, cs)
            cps.append(pltpu.make_async_remote_copy(
                src_ref=ref, dst_ref=ref, send_sem=send_sems.at[3 * i + j],
                recv_sem=recv_sems.at[3 * i + j], device_id=(x, y, 1 - c), device_id_type=MESH))
    return cps


def _allgather_forward_start(fulls, axes, *, name):
    n = len(fulls)
    geo = _geo(fulls, axes)

    def body(*refs):
        token = refs[2 * n + 2]
        for cp in _forward_copies(refs[:n], geo, refs[n], refs[n + 1]):
            cp.start()
        token[...] = jnp.zeros_like(token)

    out = pl.pallas_call(
        body, name=name,
        out_shape=(pltpu.SemaphoreType.DMA((3 * n,)), pltpu.SemaphoreType.DMA((3 * n,)),
                   *[pltpu.HBM(f.shape, f.dtype) for f in fulls], TOKEN),
        in_specs=[HBM_SPEC] * n,
        out_specs=(SEM_SPEC, SEM_SPEC, *[HBM_SPEC] * n, pl.BlockSpec(memory_space=pltpu.VMEM)),
        input_output_aliases={i: 2 + i for i in range(n)},
        compiler_params=SPLIT_COPY,
    )(*fulls)
    return out[0], out[1], list(out[2:2 + n]), out[2 + n]


def _allgather_forward_wait(send_sems, recv_sems, fulls, axes, after, *, name):
    n = len(fulls)
    geo = _geo(fulls, axes)

    def body(*refs):
        for cp in _forward_copies(refs[:n], geo, refs[n], refs[n + 1]):
            cp.wait_send()
            cp.wait_recv()

    return list(pl.pallas_call(
        body, name=name,
        out_shape=[pltpu.HBM(f.shape, f.dtype) for f in fulls],
        in_specs=[HBM_SPEC] * n + [SEM_SPEC, SEM_SPEC, ANY],
        out_specs=[HBM_SPEC] * n,
        input_output_aliases={i: i for i in range(n)},
        compiler_params=SPLIT_COPY,
    )(*fulls, send_sems, recv_sems, after))


def _chip_copies(blk_ref, land_ref, send_sems, recv_sems):
    x, y, c, others = _place()
    return [pltpu.make_async_remote_copy(
        src_ref=blk_ref, dst_ref=land_ref.at[2 * x + y], send_sem=send_sems.at[j],
        recv_sem=recv_sems.at[j], device_id=(tx, ty, c), device_id_type=MESH)
        for j, (tx, ty) in enumerate(others)]


def _chip_exchange_start(blk, *, name):
    land = pltpu.with_memory_space_constraint(lax.empty((N_CHIPS, *blk.shape), blk.dtype), pltpu.HBM)

    def body(blk_ref, land_ref, send_sems, recv_sems, blk_out, land_out, token):
        for cp in _chip_copies(blk_ref, land_ref, send_sems, recv_sems):
            cp.start()
        token[...] = jnp.zeros_like(token)

    return pl.pallas_call(
        body, name=name,
        out_shape=(pltpu.SemaphoreType.DMA((3,)), pltpu.SemaphoreType.DMA((3,)),
                   pltpu.HBM(blk.shape, blk.dtype), pltpu.HBM(land.shape, land.dtype), TOKEN),
        in_specs=[HBM_SPEC] * 2,
        out_specs=(SEM_SPEC, SEM_SPEC, HBM_SPEC, HBM_SPEC, pl.BlockSpec(memory_space=pltpu.VMEM)),
        input_output_aliases={0: 2, 1: 3},
        compiler_params=SPLIT_COPY,
    )(pltpu.with_memory_space_constraint(blk, pltpu.HBM), land)


def _chip_exchange_wait(send_sems, recv_sems, blk, land, after, *, name):
    def body(blk_ref, land_ref, send_sems, recv_sems, after_ref, blk_out, land_out):
        for cp in _chip_copies(blk_ref, land_ref, send_sems, recv_sems):
            cp.wait_send()
            cp.wait_recv()

    return pl.pallas_call(
        body, name=name,
        out_shape=[pltpu.HBM(blk.shape, blk.dtype), pltpu.HBM(land.shape, land.dtype)],
        in_specs=[HBM_SPEC] * 2 + [SEM_SPEC, SEM_SPEC, ANY],
        out_specs=[HBM_SPEC] * 2,
        input_output_aliases={0: 0, 1: 1},
        compiler_params=SPLIT_COPY,
    )(blk, land, send_sems, recv_sems, after)[1]


def _scatter_geo(parts, axes):
    out = []
    for p, ax in zip(parts, axes):
        _, rh, cols = p.shape
        out.append((ax, rh, cols // N_CHIPS if ax == 1 else cols))
    return out


def _scatter_copies(p_refs, q_refs, geo, send_sems, recv_sems):
    x, y, c, others = _place()
    chip = 2 * x + y
    cps = []
    for i, (ax, rh, cw) in enumerate(geo):
        for j, (tx, ty) in enumerate(others):
            k = 2 * tx + ty
            src = (p_refs[i].at[0, :, pl.ds(pl.multiple_of(k * cw, LANES), cw)] if ax == 1
                   else p_refs[i].at[k])
            cps.append(pltpu.make_async_remote_copy(
                src_ref=src, dst_ref=q_refs[i].at[chip], send_sem=send_sems.at[3 * i + j],
                recv_sem=recv_sems.at[3 * i + j], device_id=(tx, ty, c), device_id_type=MESH))
    return cps


def _scatter_start(parts, axes, *, name):
    n = len(parts)
    geo = _scatter_geo(parts, axes)
    slots = [pltpu.HBM((N_CHIPS, rh, cw), p.dtype) for p, (_, rh, cw) in zip(parts, geo)]

    def body(*refs):
        p_refs, q_refs = refs[:n], refs[n:2 * n]
        send_sems, recv_sems = refs[2 * n], refs[2 * n + 1]
        token = refs[4 * n + 2]
        for cp in _scatter_copies(p_refs, q_refs, geo, send_sems, recv_sems):
            cp.start()
        token[...] = jnp.zeros_like(token)

    land = [pltpu.with_memory_space_constraint(lax.empty(s.inner_aval.shape, s.inner_aval.dtype), pltpu.HBM)
            for s in slots]
    out = pl.pallas_call(
        body, name=name,
        out_shape=(pltpu.SemaphoreType.DMA((3 * n,)), pltpu.SemaphoreType.DMA((3 * n,)),
                   *[pltpu.HBM(p.shape, p.dtype) for p in parts], *slots, TOKEN),
        in_specs=[HBM_SPEC] * (2 * n),
        out_specs=(SEM_SPEC, SEM_SPEC, *[HBM_SPEC] * (2 * n), pl.BlockSpec(memory_space=pltpu.VMEM)),
        input_output_aliases={i: 2 + i for i in range(2 * n)},
        compiler_params=SPLIT_COPY,
    )(*[pltpu.with_memory_space_constraint(p, pltpu.HBM) for p in parts], *land)
    return out[0], out[1], list(out[2:2 + n]), list(out[2 + n:2 + 2 * n]), out[2 + 2 * n]


def _scatter_wait(send_sems, recv_sems, parts, slots, axes, after, *, name):
    n = len(parts)
    geo = _scatter_geo(parts, axes)

    def body(*refs):
        p_refs, q_refs = refs[:n], refs[n:2 * n]
        for cp in _scatter_copies(p_refs, q_refs, geo, refs[2 * n], refs[2 * n + 1]):
            cp.wait_send()
            cp.wait_recv()

    out = pl.pallas_call(
        body, name=name,
        out_shape=[pltpu.HBM(a.shape, a.dtype) for a in (*parts, *slots)],
        in_specs=[HBM_SPEC] * (2 * n) + [SEM_SPEC, SEM_SPEC, ANY],
        out_specs=[HBM_SPEC] * (2 * n),
        input_output_aliases={i: i for i in range(2 * n)},
        compiler_params=SPLIT_COPY,
    )(*parts, *slots, send_sems, recv_sems, after)
    return list(out[:n]), list(out[n:])


def _sibling_swap(grads, *, name):
    n = len(grads)
    out_shape = [jax.ShapeDtypeStruct((g.shape[0], g.shape[1] // 2, g.shape[2]), g.dtype)
                 for g in grads]

    def body(*refs):
        g_refs, o_refs = refs[:n], refs[n:2 * n]
        send_sems, recv_sems = refs[2 * n:]
        x, y, c, _ = _place()
        cps = []
        for i in range(n):
            rh = grads[i].shape[1] // 2
            src = g_refs[i].at[:, pl.ds(pl.multiple_of((1 - c) * rh, 16), rh), :]
            cp = pltpu.make_async_remote_copy(
                src_ref=src, dst_ref=o_refs[i], send_sem=send_sems.at[i], recv_sem=recv_sems.at[i],
                device_id=(x, y, 1 - c), device_id_type=MESH)
            cp.start()
            cps.append(cp)
        for cp in cps:
            cp.wait()

    return pl.pallas_call(
        body, in_specs=[ANY] * n, out_specs=[ANY] * n, out_shape=out_shape,
        scratch_shapes=[pltpu.SemaphoreType.DMA((n,)), pltpu.SemaphoreType.DMA((n,))],
        name=name)(*grads)


def _swap_copies(g_refs, r_refs, shapes, send_sems, recv_sems):
    x, y, c, _ = _place()
    cps = []
    for i, shape in enumerate(shapes):
        rh = shape[1] // 2
        src = g_refs[i].at[:, pl.ds(pl.multiple_of((1 - c) * rh, 16), rh), :]
        cps.append(pltpu.make_async_remote_copy(
            src_ref=src, dst_ref=r_refs[i], send_sem=send_sems.at[i], recv_sem=recv_sems.at[i],
            device_id=(x, y, 1 - c), device_id_type=MESH))
    return cps


def _sibling_swap_start(grads, *, name):
    n = len(grads)
    shapes = [g.shape for g in grads]
    lands = [pltpu.HBM((s[0], s[1] // 2, s[2]), g.dtype) for s, g in zip(shapes, grads)]

    def body(*refs):
        g_refs, r_refs = refs[:n], refs[n:2 * n]
        token = refs[4 * n + 2]
        for cp in _swap_copies(g_refs, r_refs, shapes, refs[2 * n], refs[2 * n + 1]):
            cp.start()
        token[...] = jnp.zeros_like(token)

    land = [pltpu.with_memory_space_constraint(lax.empty(s.inner_aval.shape, s.inner_aval.dtype), pltpu.HBM)
            for s in lands]
    out = pl.pallas_call(
        body, name=name,
        out_shape=(pltpu.SemaphoreType.DMA((n,)), pltpu.SemaphoreType.DMA((n,)),
                   *[pltpu.HBM(g.shape, g.dtype) for g in grads], *lands, TOKEN),
        in_specs=[HBM_SPEC] * (2 * n),
        out_specs=(SEM_SPEC, SEM_SPEC, *[HBM_SPEC] * (2 * n), pl.BlockSpec(memory_space=pltpu.VMEM)),
        input_output_aliases={i: 2 + i for i in range(2 * n)},
        compiler_params=SPLIT_COPY,
    )(*[pltpu.with_memory_space_constraint(g, pltpu.HBM) for g in grads], *land)
    return out[0], out[1], list(out[2:2 + n]), list(out[2 + n:2 + 2 * n]), out[2 + 2 * n]


def _sibling_swap_wait(send_sems, recv_sems, grads, lands, after, *, name):
    n = len(grads)
    shapes = [g.shape for g in grads]

    def body(*refs):
        g_refs, r_refs = refs[:n], refs[n:2 * n]
        for cp in _swap_copies(g_refs, r_refs, shapes, refs[2 * n], refs[2 * n + 1]):
            cp.wait_send()
            cp.wait_recv()

    out = pl.pallas_call(
        body, name=name,
        out_shape=[pltpu.HBM(a.shape, a.dtype) for a in (*grads, *lands)],
        in_specs=[HBM_SPEC] * (2 * n) + [SEM_SPEC, SEM_SPEC, ANY],
        out_specs=[HBM_SPEC] * (2 * n),
        input_output_aliases={i: i for i in range(2 * n)},
        compiler_params=SPLIT_COPY,
    )(*grads, *lands, send_sems, recv_sems, after)
    return list(out[:n]), list(out[n:])


def _pair_add(g3s, rxs, place, *, out_dtype, name):
    n = len(g3s)
    steps, trs = _group_tiles([g.shape[1] // 2 for g in g3s], 16)

    def body(p_ref, *refs):
        for i in range(n):
            refs[2 * n + i][...] = (refs[2 * i][...] + refs[2 * i + 1][...]).astype(out_dtype)

    in_specs, out_specs = [], []
    for g, tr in zip(g3s, trs):
        blk = (g.shape[0], tr, g.shape[2])
        in_specs += [pl.BlockSpec(blk, lambda i, p_ref: (0, p_ref[1] * steps + i, 0)),
                     pl.BlockSpec(blk, lambda i, p_ref: (0, i, 0))]
        out_specs.append(pl.BlockSpec(blk, lambda i, p_ref: (0, i, 0)))
    return pl.pallas_call(
        body,
        grid_spec=pltpu.PrefetchScalarGridSpec(
            num_scalar_prefetch=1, grid=(steps,), in_specs=in_specs, out_specs=out_specs),
        out_shape=[jax.ShapeDtypeStruct((g.shape[0], g.shape[1] // 2, g.shape[2]), out_dtype)
                   for g in g3s],
        compiler_params=_cp("parallel"), name=name)(place, *[a for q in zip(g3s, rxs) for a in q])


def _sum_slots(q, *, name):
    ns, rows, cols = q.shape
    tr = next(t for t in (128, 64, 32, 16, 8) if rows % t == 0)

    def body(q_ref, o_ref):
        acc = q_ref[0].astype(F32)
        for k in range(1, ns):
            acc = acc + q_ref[k].astype(F32)
        o_ref[...] = acc

    return pl.pallas_call(
        body, grid=(rows // tr,),
        in_specs=[pl.BlockSpec((ns, tr, cols), lambda i: (0, i, 0))],
        out_specs=_rspec(tr, cols),
        out_shape=jax.ShapeDtypeStruct((rows, cols), F32),
        compiler_params=_cp("parallel"), name=name)(q)


def _sum_chips(qs, ps, place, axes, *, name):
    n = len(qs)
    per = N_CHIPS + 1
    steps, trs = _group_tiles([q.shape[1] for q in qs], 16)

    def body(p_ref, *refs):
        chip = p_ref[0]
        for i in range(n):
            q_refs, own_ref = refs[per * i:per * i + N_CHIPS], refs[per * i + N_CHIPS]
            acc = jnp.where(chip == 0, own_ref[...], q_refs[0][...]).astype(F32)
            for k in range(1, N_CHIPS):
                acc = acc + jnp.where(chip == k, own_ref[...], q_refs[k][...]).astype(F32)
            refs[per * n + i][...] = acc

    def slot_spec(k, tr, cw):
        return pl.BlockSpec((None, tr, cw),
                            lambda i, p_ref: (jnp.where(p_ref[0] == k, (k + 1) % N_CHIPS, k), i, 0))

    in_specs, out_specs, operands = [], [], []
    for q, p, ax, tr in zip(qs, ps, axes, trs):
        cw = q.shape[2]
        in_specs += [slot_spec(k, tr, cw) for k in range(N_CHIPS)]
        in_specs.append(pl.BlockSpec((None, tr, cw), (lambda i, p_ref: (0, i, p_ref[0])) if ax == 1
                                     else (lambda i, p_ref: (p_ref[0], i, 0))))
        out_specs.append(pl.BlockSpec((tr, cw), lambda i, p_ref: (p_ref[1] * steps + i, 0)))
        operands += [q] * N_CHIPS + [p]
    return pl.pallas_call(
        body,
        grid_spec=pltpu.PrefetchScalarGridSpec(
            num_scalar_prefetch=1, grid=(steps,), in_specs=in_specs, out_specs=out_specs),
        out_shape=[jax.ShapeDtypeStruct((2 * q.shape[1], q.shape[2]), F32) for q in qs],
        compiler_params=_cp("parallel"), name=name)(place, *operands)


def _sibling_share(shards, *, name):
    n = len(shards)

    def body(*refs):
        o_refs = refs[n:2 * n]
        send_sems, recv_sems = refs[2 * n:]
        x, y, c, _ = _place()
        cps = []
        for i in range(n):
            rh = shards[i].shape[0] // 2
            mine = o_refs[i].at[pl.ds(pl.multiple_of(c * rh, 8), rh), :]
            cp = pltpu.make_async_remote_copy(
                src_ref=mine, dst_ref=mine, send_sem=send_sems.at[i], recv_sem=recv_sems.at[i],
                device_id=(x, y, 1 - c), device_id_type=MESH)
            cp.start()
            cps.append(cp)
        for i in range(n):
            rh = shards[i].shape[0] // 2
            theirs = o_refs[i].at[pl.ds(pl.multiple_of((1 - c) * rh, 8), rh), :]
            pltpu.make_async_remote_copy(
                src_ref=theirs, dst_ref=theirs, send_sem=send_sems.at[i], recv_sem=recv_sems.at[i],
                device_id=(x, y, c), device_id_type=MESH).wait_recv()
        for cp in cps:
            cp.wait_send()

    return pl.pallas_call(
        body, in_specs=[ANY] * n, out_specs=[ANY] * n,
        out_shape=[jax.ShapeDtypeStruct(h.shape, h.dtype) for h in shards],
        input_output_aliases={i: i for i in range(n)},
        scratch_shapes=[pltpu.SemaphoreType.DMA((n,)), pltpu.SemaphoreType.DMA((n,))],
        name=name)(*shards)


def _gather_all(blk, *, name, after=None, shares=()):
    rows, cols = blk.shape
    extra = [] if after is None else [after]
    n_s = len(shares)

    def body(x_ref, *refs):
        s_refs = refs[len(extra) + n_s + 1:len(extra) + 2 * n_s + 1]
        out_ref = refs[len(extra) + n_s]
        send_sems, recv_sems, local_sem, s_send, s_recv = refs[len(extra) + 2 * n_s + 1:]
        x, y, c = lax.axis_index("x"), lax.axis_index("y"), lax.axis_index("c")
        me = 4 * x + 2 * y + c
        mine = pltpu.make_async_copy(x_ref, out_ref.at[me], local_sem)
        mine.start()
        cps = []
        for i in range(n_s):
            rh = shares[i].shape[0] // 2
            half = s_refs[i].at[pl.ds(pl.multiple_of(c * rh, 8), rh), :]
            cp = pltpu.make_async_remote_copy(
                src_ref=half, dst_ref=half, send_sem=s_send.at[i], recv_sem=s_recv.at[i],
                device_id=(x, y, 1 - c), device_id_type=MESH)
            cp.start()
            cps.append(cp)
        for k in range(1, N_DEV):
            tx = (1 - x) if (k >> 2) & 1 else x
            ty = (1 - y) if (k >> 1) & 1 else y
            tc = (1 - c) if k & 1 else c
            cp = pltpu.make_async_remote_copy(
                src_ref=x_ref, dst_ref=out_ref.at[me], send_sem=send_sems.at[k - 1],
                recv_sem=recv_sems.at[k - 1], device_id=(tx, ty, tc), device_id_type=MESH)
            cp.start()
            cps.append(cp)
        for k in range(1, N_DEV):
            tx = (1 - x) if (k >> 2) & 1 else x
            ty = (1 - y) if (k >> 1) & 1 else y
            tc = (1 - c) if k & 1 else c
            got = out_ref.at[4 * tx + 2 * ty + tc]
            pltpu.make_async_remote_copy(
                src_ref=got, dst_ref=got, send_sem=send_sems.at[k - 1], recv_sem=recv_sems.at[k - 1],
                device_id=(x, y, c), device_id_type=MESH).wait_recv()
        for i in range(n_s):
            rh = shares[i].shape[0] // 2
            theirs = s_refs[i].at[pl.ds(pl.multiple_of((1 - c) * rh, 8), rh), :]
            pltpu.make_async_remote_copy(
                src_ref=theirs, dst_ref=theirs, send_sem=s_send.at[i], recv_sem=s_recv.at[i],
                device_id=(x, y, c), device_id_type=MESH).wait_recv()
        for cp in cps:
            cp.wait_send()
        mine.wait()

    vm = pl.BlockSpec(memory_space=pltpu.VMEM)
    out = pl.pallas_call(
        body, in_specs=[vm] + [ANY] * (len(extra) + n_s), out_specs=[vm] + [ANY] * n_s,
        out_shape=[jax.ShapeDtypeStruct((N_DEV, rows, cols), blk.dtype)]
        + [jax.ShapeDtypeStruct(s.shape, s.dtype) for s in shares],
        input_output_aliases={1 + len(extra) + i: 1 + i for i in range(n_s)},
        scratch_shapes=[pltpu.SemaphoreType.DMA((N_DEV - 1,)), pltpu.SemaphoreType.DMA((N_DEV - 1,)),
                        pltpu.SemaphoreType.DMA, pltpu.SemaphoreType.DMA((max(n_s, 1),)),
                        pltpu.SemaphoreType.DMA((max(n_s, 1),))],
        name=name)(blk, *extra, *shares)
    return (out[0], *out[1:]) if n_s else out[0]


def _as_rows(a):
    flat = a.reshape(-1)
    n = flat.shape[0]
    rows = -(-n // (8 * LANES)) * 8
    return jnp.pad(flat, (0, rows * LANES - n)).reshape(rows, LANES)


def _from_rows(p, shape):
    n = int(np.prod(shape))
    return p.reshape(-1)[:n].reshape(shape)


WEIGHT_AXES = (1, 1, 1, 0, 1, 0)
WIRE = BF16


def kernel(x, meta_tokens, w_in, w_na_out, w_hg_out, w_o, w_up, w_down, norm_mix, norm_mlp, norm_final, hg_norm, na_rpb, hg_lb_logits, loss_target, m_meta_tokens, m_w_in, m_w_na_out, m_w_hg_out, m_w_o, m_w_up, m_w_down, m_norm_mix, m_norm_mlp, m_norm_final, m_hg_norm, m_na_rpb, m_hg_lb_logits, v_meta_tokens, v_w_in, v_w_na_out, v_w_hg_out, v_w_o, v_w_up, v_w_down, v_norm_mix, v_norm_mlp, v_norm_final, v_hg_norm, v_na_rpb, v_hg_lb_logits):
    xi, yi, ci = lax.axis_index("x"), lax.axis_index("y"), lax.axis_index("c")
    chip = 2 * xi + yi
    d = x.shape[-1]
    dshard = meta_tokens.shape[1]
    hgw = hg_norm.shape[1]
    lbs = hg_lb_logits.shape[2]
    big = [w_in[0], w_na_out[0], w_hg_out[0], w_o[0], w_up[0], w_down[0]]
    big_m = [m_w_in[0], m_w_na_out[0], m_w_hg_out[0], m_w_o[0], m_w_up[0], m_w_down[0]]
    big_v = [v_w_in[0], v_w_na_out[0], v_w_hg_out[0], v_w_o[0], v_w_up[0], v_w_down[0]]

    place = jnp.stack([chip, ci]).astype(jnp.int32)
    own_w = _cast_into_full(big, WEIGHT_AXES, place, name="cast_shards")
    in_axes, rest_axes = WEIGHT_AXES[:1], WEIGHT_AXES[1:]
    small_in = jnp.concatenate([_as_rows(meta_tokens), _as_rows(hg_lb_logits)], axis=0)
    sm_send, sm_recv, sm_blk, sm_land, sm_token = _chip_exchange_start(small_in,
                                                                       name="small_params_start")
    in_send, in_recv, in_bufs, in_token = _allgather_start(own_w[:1], in_axes, sm_token,
                                                           name="weight_allgather_in_start")
    ag_send, ag_recv, ag_bufs, ag_token = _allgather_start(own_w[1:], rest_axes, in_token,
                                                           name="weight_allgather_rest_start")
    sm_land = _chip_exchange_wait(sm_send, sm_recv, sm_blk, sm_land, ag_token,
                                  name="small_params_wait")
    small_all = lax.dynamic_update_slice(sm_land, small_in[None], (chip, 0, 0))
    forward = {}

    def first_weight(after):
        got = _allgather_wait(in_send, in_recv, in_bufs, in_axes, after,
                              name="weight_allgather_in_wait")
        return _allgather_forward(got, in_axes, name="weight_allgather_in_forward")[0]

    def rest_landed(after):
        got = _allgather_wait(ag_send, ag_recv, ag_bufs, rest_axes, after,
                              name="weight_allgather_rest_wait")
        send, recv, bufs, token = _allgather_forward_start(
            got, rest_axes, name="weight_allgather_rest_forward_start")
        forward["rest"] = (send, recv, bufs)
        return token

    def rest_weights(after):
        return _allgather_forward_wait(*forward["rest"], rest_axes, after,
                                       name="weight_allgather_rest_forward_wait")

    n_meta_rows = N_META * dshard // LANES
    meta_full = (small_all[:, :n_meta_rows].reshape(N_CHIPS, N_META, dshard)
                 .transpose(1, 0, 2).reshape(N_META, d))
    lbl_full = (small_all[:, n_meta_rows:].reshape(N_CHIPS, -1)[:, :4 * lbs]
                .reshape(N_CHIPS, 2, 2, lbs).transpose(1, 2, 0, 3).reshape(2, 2, N_CHIPS * lbs))
    lb = jax.nn.softmax(lbl_full, axis=1)[:, 0]

    def by_chip(dws, axes):
        return [g.reshape(1, *g.shape) if ax == 1
                else g.reshape(N_CHIPS, g.shape[0] // N_CHIPS, g.shape[1]) for g, ax in zip(dws, axes)]

    flying = {}

    def scatter(tag, axes, g3, rx):
        parts = _pair_add(g3, rx, place, out_dtype=WIRE, name=f"grad_pair_add_{tag}")
        send, recv, parts, slots, token = _scatter_start(parts, axes,
                                                         name=f"grad_scatter_{tag}_start")
        flying[tag] = (send, recv, parts, slots)
        return token

    def swap(tag, axes):
        def start(dws):
            send, recv, g3, lands, token = _sibling_swap_start(
                by_chip(dws, axes), name=f"grad_sibling_swap_{tag}_start")
            flying["swap_" + tag] = (send, recv, g3, lands)
            return token

        def finish(after):
            g3, rx = _sibling_swap_wait(*flying["swap_" + tag], after,
                                        name=f"grad_sibling_swap_{tag}_wait")
            return scatter(tag, axes, g3, rx)
        return start, finish

    swap_rest, scatter_rest = swap("rest", rest_axes)
    swap_in, scatter_in = swap("in", in_axes)

    def landed(tag, axes, after):
        return _scatter_wait(*flying[tag], axes, after, name=f"grad_scatter_{tag}_wait")

    (loss, dx, dmeta, *_, dg_mix, dg_mlp, dg_fin, d_gain, d_rpb, d_lb) = _local_step(
        x[0], loss_target[0], meta_full, first_weight, rest_weights, norm_mix, norm_mlp,
        norm_final.reshape(1, d), hg_norm, na_rpb[0], lb, swap_rest, scatter_rest,
        lambda dw_in: swap_in([dw_in]), rest_landed, scatter_in)

    parts_rest, slots_rest = landed("rest", rest_axes, dx)
    g_rest = _sibling_share(_sum_chips(slots_rest, parts_rest, place, rest_axes,
                                       name="grad_sum_chips_rest"),
                            name="grad_sibling_share_rest")
    out_rest = _adamw(big[1:], g_rest, big_m[1:], big_v[1:], name="adamw_rest")
    parts_in, slots_in = landed("in", in_axes, out_rest[-1][0])
    half_in = _sum_chips(slots_in, parts_in, place, in_axes, name="grad_sum_chips_in")

    d_rpb_c = d_rpb[:, :2 * NA_WIN_W - 1]
    small_g = [dmeta, dg_mix, dg_mlp, dg_fin, d_gain, d_rpb_c, d_lb, loss]
    packed = jnp.concatenate([_as_rows(a) for a in small_g], axis=0)
    gathered, g_in = _gather_all(packed, shares=half_in, name="gather_small_grads")
    g_big = [g_in] + list(g_rest)
    total = _sum_slots(gathered, name="sum_small_grads")
    offs = np.cumsum([0] + [_as_rows(a).shape[0] for a in small_g])
    take = lambda i, shape: _from_rows(total[offs[i]:offs[i + 1]], shape)
    g_meta_full = take(0, (N_META, d))
    g_norm_mix, g_norm_mlp = take(1, (1, d)), take(2, (1, d))
    g_norm_final = take(3, (d,))
    g_hg_norm = take(4, (1, hgw))
    g_rpb = take(5, na_rpb.shape)
    g_lb = take(6, (2, hgw))
    loss_total = take(7, (1, LANES))[0, 0]
    g_meta = lax.dynamic_slice_in_dim(g_meta_full, chip * dshard, dshard, axis=1)
    dl0 = lb * (1.0 - lb) * g_lb
    g_lbl_full = jnp.stack([dl0, -dl0], axis=1)
    g_lbl = lax.dynamic_slice_in_dim(g_lbl_full, chip * lbs, lbs, axis=2)

    big_out = _adamw(big[:1], [g_in], big_m[:1], big_v[:1], name="adamw_in") + out_rest
    small_w = [meta_tokens, norm_mix, norm_mlp, norm_final, hg_norm, na_rpb, hg_lb_logits]
    small_gr = [g_meta, g_norm_mix, g_norm_mlp, g_norm_final, g_hg_norm, g_rpb, g_lbl]
    small_m = [m_meta_tokens, m_norm_mix, m_norm_mlp, m_norm_final, m_hg_norm, m_na_rpb, m_hg_lb_logits]
    small_v = [v_meta_tokens, v_norm_mix, v_norm_mlp, v_norm_final, v_hg_norm, v_na_rpb, v_hg_lb_logits]
    pk = lambda lst: jnp.concatenate([_as_rows(a) for a in lst], axis=0)
    ((sd, sm, sv),) = _adamw([pk(small_w)], [pk(small_gr)], [pk(small_m)], [pk(small_v)],
                             name="adamw_small")
    soffs = np.cumsum([0] + [_as_rows(a).shape[0] for a in small_w])
    unpk = lambda p: [_from_rows(p[soffs[i]:soffs[i + 1]], small_w[i].shape) for i in range(len(small_w))]
    sd, sm, sv = unpk(sd), unpk(sm), unpk(sv)

    def order(bigs, smalls):
        return [smalls[0]] + [b.reshape(1, *b.shape) for b in bigs] + smalls[1:]

    grads = order(g_big, small_gr)
    deltas = order([o[0] for o in big_out], sd)
    new_m = order([o[1] for o in big_out], sm)
    new_v = order([o[2] for o in big_out], sv)
    return (loss_total, dx.reshape(1, *dx.shape), *grads, *deltas, *new_m, *new_v)
```

```python
import functools

import numpy as np
import jax
import jax.numpy as jnp
from jax import lax
from jax.experimental import pallas as pl
from jax.experimental.pallas import tpu as pltpu

F32 = jnp.float32
BF16 = jnp.bfloat16
HIGHEST = lax.Precision.HIGHEST

GRID_W = 64
N_META = 16
EPS = 1e-6
NA_HEAD_DIM = 64
NA_WIN_H = 8
NA_WIN_W = 16
HG_DK = 128
HG_CHUNK = 16
LANES = 128
ROW_ALIGN = 128
VMEM_LIMIT = 48 * 1024 * 1024

ADAM_LR = 0.001
ADAM_B1 = 0.9
ADAM_B2 = 0.999
ADAM_EPS = 1e-08
ADAM_WD = 0.01
ADAM_STEP = 10

MESH = pl.DeviceIdType.MESH


def _cp(*sem):
    return pltpu.CompilerParams(dimension_semantics=sem, vmem_limit_bytes=VMEM_LIMIT)


def _sigmoid(x):
    return 0.5 * jnp.tanh(0.5 * x) + 0.5


def _dot(a, b, dims, precision=None):
    return lax.dot_general(a, b, (dims, ((), ())), preferred_element_type=F32, precision=precision)


def _nn(a, b, **kw):
    return _dot(a, b, ((1,), (0,)), **kw)


def _nt(a, b, **kw):
    return _dot(a, b, ((1,), (1,)), **kw)


def _tn(a, b, **kw):
    return _dot(a, b, ((0,), (0,)), **kw)


def _matmul(a, b, *, ta=False, tb=False, tm=None, tn=None, tk=None, out_dtype=F32, name,
            precision=None, after=None, epilogue=None, tiles=(), out_dtypes=None):
    extra = [] if after is None else [after]
    single = out_dtypes is None
    if single:
        out_dtypes = (out_dtype,)
    n_t, n_o = len(tiles), len(out_dtypes)
    if ta:
        kdim, m = a.shape
    else:
        m, kdim = a.shape
    if tb:
        n, k2 = b.shape
    else:
        k2, n = b.shape
    assert kdim == k2, (a.shape, b.shape, ta, tb)
    if tm is None:
        if ta:
            tm = next(t for t in (1024, 512, 256, 128, m) if m % t == 0)
        else:
            tm = m // 2 if (m // 2) % 16 == 0 and m > 512 else m
    if tn is None:
        wide = (1024,) if not ta and len(tiles) <= 1 else ()
        tn = next(t for t in (*wide, 512, 256, 128, n) if n % t == 0)
    if tk is None:
        tk = kdim if ta else next(t for t in (1024, 512, 256, 128, kdim) if kdim % t == 0)
    assert m % tm == 0 and n % tn == 0 and kdim % tk == 0, (m, n, kdim, tm, tn, tk)
    nk = kdim // tk
    op_dtype = F32 if precision is not None else BF16

    def body(a_ref, b_ref, *refs):
        t_refs = refs[:n_t]
        o_refs = refs[n_t + len(extra):n_t + len(extra) + n_o]
        av = a_ref[...].astype(op_dtype)
        bv = b_ref[...].astype(op_dtype)
        dims = ((0 if ta else 1,), (1 if tb else 0,))
        part = _dot(av, bv, dims, precision=precision)

        def finish(acc):
            outs = (acc,) if epilogue is None else epilogue(acc, *[t[...] for t in t_refs])
            for o_ref, val in zip(o_refs, outs):
                o_ref[...] = val.astype(o_ref.dtype)

        if nk == 1:
            finish(part)
            return
        acc_ref = refs[-1]
        kk = pl.program_id(2)

        @pl.when(kk == 0)
        def _():
            acc_ref[...] = part

        @pl.when((kk > 0) & (kk < nk - 1))
        def _():
            acc_ref[...] += part

        @pl.when(kk == nk - 1)
        def _():
            finish(acc_ref[...] + part)

    a_spec = (pl.BlockSpec((tk, tm), lambda i, j, k: (k, i)) if ta
              else pl.BlockSpec((tm, tk), lambda i, j, k: (i, k)))
    b_spec = (pl.BlockSpec((tn, tk), lambda i, j, k: (j, k)) if tb
              else pl.BlockSpec((tk, tn), lambda i, j, k: (k, j)))
    for _, off in tiles:
        assert off % tn == 0, (off, tn)
    t_specs = [pl.BlockSpec((tm, tn), functools.partial(lambda i, j, k, o: (i, o + j), o=off // tn))
               for _, off in tiles]
    o_spec = pl.BlockSpec((tm, tn), lambda i, j, k: (i, j))
    outs = pl.pallas_call(
        body,
        grid=(m // tm, n // tn, nk),
        in_specs=[a_spec, b_spec] + t_specs + [pl.BlockSpec(memory_space=pl.ANY)] * len(extra),
        out_specs=[o_spec] * n_o,
        out_shape=[jax.ShapeDtypeStruct((m, n), dt) for dt in out_dtypes],
        scratch_shapes=[pltpu.VMEM((tm, tn), F32)] if nk > 1 else [],
        compiler_params=_cp("parallel", "parallel", "arbitrary"),
        name=name,
    )(a, b, *[t for t, _ in tiles], *extra)
    return outs[0] if single else outs


def _rspec(tr, w, cb=0):
    return pl.BlockSpec((tr, w), lambda i: (i, cb))


def _fspec(shape):
    nd = len(shape)
    return pl.BlockSpec(shape, lambda i: (0,) * nd)


ROW_VMEM_BUDGET = 20 * 1024 * 1024
ROW_MIN_STEPS = 4


def _row_tile(lp, row_bytes):
    for k in range(ROW_MIN_STEPS, lp // 16 + 1):
        tr = lp // k
        if lp % k == 0 and tr % 16 == 0 and 2 * tr * row_bytes <= ROW_VMEM_BUDGET:
            return tr
    return lp


def _token_rows_copy(i, n_tiles, tr, n_tok, tok_ref, buf_ref, sem, *, to_tokens):
    assert n_tiles >= 2 and 0 < n_tok + N_META - (n_tiles - 1) * tr <= tr

    def run(tok_row, buf_row, count):
        tok = tok_ref.at[pl.ds(tok_row, count), :]
        buf = buf_ref.at[pl.ds(buf_row, count), :]
        cp = pltpu.make_async_copy(buf, tok, sem) if to_tokens else pltpu.make_async_copy(tok, buf, sem)
        cp.start()
        cp.wait()

    @pl.when(i == 0)
    def _():
        run(0, N_META, tr - N_META)

    if n_tiles > 2:
        @pl.when((i > 0) & (i < n_tiles - 1))
        def _():
            run(pl.multiple_of(i * tr - N_META, 8), 0, tr)

    @pl.when(i == n_tiles - 1)
    def _():
        run((n_tiles - 1) * tr - N_META, 0, n_tok + N_META - (n_tiles - 1) * tr)


def _embed_norm(x, meta, g, *, lp, name):
    n_tok, d = x.shape
    tr = _row_tile(lp, d * (4 + 2))
    n_tiles = lp // tr

    def body(x_ref, meta_ref, g_ref, h_ref, o_ref, buf_ref, sem):
        i = pl.program_id(0)
        buf_ref[...] = jnp.zeros_like(buf_ref)

        @pl.when(i == 0)
        def _():
            buf_ref[0:N_META, :] = meta_ref[...]

        _token_rows_copy(i, n_tiles, tr, n_tok, x_ref, buf_ref, sem, to_tokens=False)
        xv = buf_ref[...]
        h_ref[...] = xv
        r = lax.rsqrt(jnp.mean(xv * xv, axis=-1, keepdims=True) + EPS)
        o_ref[...] = (xv * r * g_ref[...]).astype(BF16)

    return pl.pallas_call(
        body, grid=(n_tiles,),
        in_specs=[ANY, _fspec((N_META, d)), _fspec((1, d))],
        out_specs=[_rspec(tr, d), _rspec(tr, d)],
        out_shape=[jax.ShapeDtypeStruct((lp, d), F32), jax.ShapeDtypeStruct((lp, d), BF16)],
        scratch_shapes=[pltpu.VMEM((tr, d), F32), pltpu.SemaphoreType.DMA],
        compiler_params=_cp("parallel"), name=name)(x, meta, g)


def _residual_norm(h, t, g, *, name):
    lp, d = h.shape
    tr = _row_tile(lp, d * (4 + 4 + 4 + 2))

    def body(h_ref, t_ref, g_ref, h1_ref, m_ref):
        xv = h_ref[...] + t_ref[...]
        h1_ref[...] = xv
        r = lax.rsqrt(jnp.mean(xv * xv, axis=-1, keepdims=True) + EPS)
        m_ref[...] = (xv * r * g_ref[...]).astype(BF16)

    return pl.pallas_call(
        body, grid=(lp // tr,),
        in_specs=[_rspec(tr, d), _rspec(tr, d), _fspec((1, d))],
        out_specs=[_rspec(tr, d), _rspec(tr, d)],
        out_shape=[jax.ShapeDtypeStruct((lp, d), F32), jax.ShapeDtypeStruct((lp, d), BF16)],
        compiler_params=_cp("parallel"), name=name)(h, t, g)


def _rmsnorm_bwd_add(x, g, dy, dres, *, name):
    lp, d = x.shape
    tr = _row_tile(lp, d * 4 * 4)

    def body(x_ref, g_ref, dy_ref, dr_ref, dx_ref, dg_ref):
        @pl.when(pl.program_id(0) == 0)
        def _():
            dg_ref[...] = jnp.zeros_like(dg_ref)

        xv = x_ref[...]
        r = lax.rsqrt(jnp.mean(xv * xv, axis=-1, keepdims=True) + EPS)
        xh = xv * r
        dyv = dy_ref[...]
        dg_ref[...] += jnp.sum(dyv * xh, axis=0, keepdims=True)
        dxh = dyv * g_ref[...]
        dx_ref[...] = dr_ref[...] + r * (dxh - xh * jnp.mean(dxh * xh, axis=-1, keepdims=True))

    return pl.pallas_call(
        body, grid=(lp // tr,),
        in_specs=[_rspec(tr, d), _fspec((1, d)), _rspec(tr, d), _rspec(tr, d)],
        out_specs=[_rspec(tr, d), _fspec((1, d))],
        out_shape=[jax.ShapeDtypeStruct((lp, d), F32), jax.ShapeDtypeStruct((1, d), F32)],
        compiler_params=_cp("arbitrary"), name=name)(x, g, dy, dres)


def _rmsnorm_bwd_tokens(x, g, dy, dres, *, n_tok, name):
    lp, d = x.shape
    tr = _row_tile(lp, d * 4 * 4)
    n_tiles = lp // tr

    def body(x_ref, g_ref, dy_ref, dr_ref, dtok_ref, dmeta_ref, dg_ref, buf_ref, sem):
        i = pl.program_id(0)

        @pl.when(i == 0)
        def _():
            dg_ref[...] = jnp.zeros_like(dg_ref)

        xv = x_ref[...]
        r = lax.rsqrt(jnp.mean(xv * xv, axis=-1, keepdims=True) + EPS)
        xh = xv * r
        dyv = dy_ref[...]
        dg_ref[...] += jnp.sum(dyv * xh, axis=0, keepdims=True)
        dxh = dyv * g_ref[...]
        buf_ref[...] = dr_ref[...] + r * (dxh - xh * jnp.mean(dxh * xh, axis=-1, keepdims=True))

        @pl.when(i == 0)
        def _():
            dmeta_ref[...] = buf_ref[0:N_META, :]

        _token_rows_copy(i, n_tiles, tr, n_tok, dtok_ref, buf_ref, sem, to_tokens=True)

    return pl.pallas_call(
        body, grid=(n_tiles,),
        in_specs=[_rspec(tr, d), _fspec((1, d)), _rspec(tr, d), _rspec(tr, d)],
        out_specs=[ANY, _fspec((N_META, d)), _fspec((1, d))],
        out_shape=[jax.ShapeDtypeStruct((n_tok, d), F32), jax.ShapeDtypeStruct((N_META, d), F32),
                   jax.ShapeDtypeStruct((1, d), F32)],
        scratch_shapes=[pltpu.VMEM((tr, d), F32), pltpu.SemaphoreType.DMA],
        compiler_params=_cp("arbitrary"), name=name)(x, g, dy, dres)


def _final_loss(h1, t2, g, tgt, *, name):
    lp, d = h1.shape
    n_tok = tgt.shape[0]
    tr = _row_tile(lp, d * 4 * 4)
    n_tiles = lp // tr

    def body(h_ref, t_ref, g_ref, tgt_ref, dh_ref, loss_ref, dg_ref, tg_ref, sem):
        i = pl.program_id(0)

        @pl.when(i == 0)
        def _():
            loss_ref[...] = jnp.zeros_like(loss_ref)
            dg_ref[...] = jnp.zeros_like(dg_ref)
            tg_ref[...] = jnp.zeros_like(tg_ref)

        _token_rows_copy(i, n_tiles, tr, n_tok, tgt_ref, tg_ref, sem, to_tokens=False)

        xv = h_ref[...] + t_ref[...]
        r = lax.rsqrt(jnp.mean(xv * xv, axis=-1, keepdims=True) + EPS)
        xh = xv * r
        gv = g_ref[...]
        row = i * tr + lax.broadcasted_iota(jnp.int32, (tr, 1), 0)
        valid = (row >= N_META) & (row < N_META + n_tok)
        err = jnp.where(valid, xh * gv - tg_ref[...], 0.0)
        loss_ref[...] += jnp.sum(0.5 * err * err) / d
        dy = err / d
        dg_ref[...] += jnp.sum(dy * xh, axis=0, keepdims=True)
        dxh = dy * gv
        dh_ref[...] = r * (dxh - xh * jnp.mean(dxh * xh, axis=-1, keepdims=True))

    return pl.pallas_call(
        body, grid=(n_tiles,),
        in_specs=[_rspec(tr, d), _rspec(tr, d), _fspec((1, d)), ANY],
        out_specs=[_rspec(tr, d), _fspec((1, LANES)), _fspec((1, d))],
        out_shape=[jax.ShapeDtypeStruct((lp, d), F32), jax.ShapeDtypeStruct((1, LANES), F32),
                   jax.ShapeDtypeStruct((1, d), F32)],
        scratch_shapes=[pltpu.VMEM((tr, d), F32), pltpu.SemaphoreType.DMA],
        compiler_params=_cp("arbitrary"), name=name)(h1, t2, g, tgt)


def _hg_out(o_f, o_b, proj, gain, *, col_g, name):
    lp, w = o_f.shape
    tr = _row_tile(lp, w * (3 * 4 + 2))
    hh = w // HG_DK

    def body(of_ref, ob_ref, g_ref, gain_ref, y_ref):
        gv = g_ref[...]
        sg = gv * _sigmoid(gv)
        for h in range(hh):
            sl = slice(h * HG_DK, (h + 1) * HG_DK)
            o = of_ref[:, sl] + ob_ref[:, sl]
            r = lax.rsqrt(jnp.mean(o * o, axis=-1, keepdims=True) + EPS)
            y_ref[:, sl] = (o * r * gain_ref[:, sl] * sg[:, sl]).astype(BF16)

    return pl.pallas_call(
        body, grid=(lp // tr,),
        in_specs=[_rspec(tr, w), _rspec(tr, w), _rspec(tr, w, col_g // w), _fspec((1, w))],
        out_specs=_rspec(tr, w),
        out_shape=jax.ShapeDtypeStruct((lp, w), BF16),
        compiler_params=_cp("parallel"), name=name)(o_f, o_b, proj, gain)


def _hg_out_bwd(o_f, o_b, proj, gain, dy, *, col_g, name):
    lp, w = o_f.shape
    tr = _row_tile(lp, w * (5 * 4 + 2))
    hh = w // HG_DK

    def body(of_ref, ob_ref, g_ref, gain_ref, dy_ref, do_ref, dg_ref, dgain_ref):
        @pl.when(pl.program_id(0) == 0)
        def _():
            dgain_ref[...] = jnp.zeros_like(dgain_ref)

        for h in range(hh):
            sl = slice(h * HG_DK, (h + 1) * HG_DK)
            gv = g_ref[:, sl]
            s = _sigmoid(gv)
            sg = gv * s
            dsg = s + gv * s * (1.0 - s)
            o = of_ref[:, sl] + ob_ref[:, sl]
            r = lax.rsqrt(jnp.mean(o * o, axis=-1, keepdims=True) + EPS)
            on = o * r
            dyv = dy_ref[:, sl]
            gn = gain_ref[:, sl]
            dgain_ref[:, sl] += jnp.sum(dyv * on * sg, axis=0, keepdims=True)
            dg_ref[:, sl] = (dyv * on * gn * dsg).astype(BF16)
            don = dyv * gn * sg
            do_ref[:, sl] = r * (don - on * jnp.mean(don * on, axis=-1, keepdims=True))

    return pl.pallas_call(
        body, grid=(lp // tr,),
        in_specs=[_rspec(tr, w), _rspec(tr, w), _rspec(tr, w, col_g // w), _fspec((1, w)),
                  _rspec(tr, w)],
        out_specs=[_rspec(tr, w), _rspec(tr, w), _fspec((1, w))],
        out_shape=[jax.ShapeDtypeStruct((lp, w), F32), jax.ShapeDtypeStruct((lp, w), BF16),
                   jax.ShapeDtypeStruct((1, w), F32)],
        compiler_params=_cp("arbitrary"), name=name)(o_f, o_b, proj, gain, dy)


HG_ROWS = 128
HG_HALVES = (1, 2, 4, 8, 16, 32, 64)


def _hg_gates(zq, z, lbv):
    sq = _sigmoid(zq)
    s = _sigmoid(z)
    f = lbv + (1.0 - lbv) * s
    kk = (1.0 - lbv) * (1.0 - s)
    return zq * sq, sq, s, f, jnp.log(f), kk


def _block_cumsum(g, pos, suffix):
    x = g
    for k in HG_HALVES:
        if suffix:
            x = x + jnp.where(pos < HG_ROWS - k, pltpu.roll(x, HG_ROWS - k, 0), 0.0)
        else:
            x = x + jnp.where(pos >= k, pltpu.roll(x, k, 0), 0.0)
    return x


def _pair_levels(b, pos, reverse):
    out = []
    first = b
    for m in HG_HALVES:
        if m > 1:
            first = jnp.where((pos & (m - 1)) >= m // 2, pltpu.roll(first, m // 2, 0), first)
        nxt = pltpu.roll(first, HG_ROWS - m, 0)
        upper = (pos & (2 * m - 1)) >= m
        if reverse:
            eq = jnp.where(upper, 0.0, jnp.exp(b - nxt))
            ek = jnp.where(upper, jnp.exp(first - b), 0.0)
        else:
            eq = jnp.where(upper, jnp.exp(b - first), 0.0)
            ek = jnp.where(upper, 0.0, jnp.exp(nxt - b))
        out.append((eq, ek))
    return out


def _pair_masks(mask_ref):
    ri = lax.broadcasted_iota(jnp.int32, (HG_ROWS, HG_ROWS), 0)
    ci = lax.broadcasted_iota(jnp.int32, (HG_ROWS, HG_ROWS), 1)
    for i, m in enumerate(HG_HALVES):
        sh = m.bit_length()
        mask_ref[i] = jnp.where((ri >> sh) == (ci >> sh), 1.0, 0.0)


def _hg_scan_fwd(proj, lb, *, reverse, col_q, col_z, col_i, hh, name):
    lp = proj.shape[0]
    n_blocks = lp // HG_ROWS
    last = 0 if reverse else HG_ROWS - 1

    def body(q_ref, z_ref, i_ref, lb_ref, o_ref, st_ref, mask_ref):
        lbv = lb_ref[...]
        pos = lax.broadcasted_iota(jnp.int32, (HG_ROWS, 1), 0)
        ri = lax.broadcasted_iota(jnp.int32, (HG_ROWS, HG_ROWS), 0)
        ci = lax.broadcasted_iota(jnp.int32, (HG_ROWS, HG_ROWS), 1)
        _pair_masks(mask_ref)

        def block(bi, st):
            bb = (n_blocks - 1 - bi) if reverse else bi
            r0 = pl.multiple_of(bb * HG_ROWS, HG_ROWS)
            v16 = i_ref[pl.ds(r0, HG_ROWS), :].astype(BF16)
            qh, _, _, _, g, kk = _hg_gates(q_ref[pl.ds(r0, HG_ROWS), :],
                                           z_ref[pl.ds(r0, HG_ROWS), :], lbv)
            b = _block_cumsum(g, pos, reverse)
            bl = b[last:last + 1, :]
            qe = (qh * jnp.exp(b)).astype(BF16)
            kd = (kk * jnp.exp(bl - b)).astype(BF16)
            a = jnp.where(ri == ci, jnp.sum(qh * kk, axis=1, keepdims=True), 0.0)
            for i, (eq, ek) in enumerate(_pair_levels(b, pos, reverse)):
                a = a + mask_ref[i] * _nt((qh * eq).astype(BF16), (kk * ek).astype(BF16))
            st_ref[bb] = st
            o_ref[pl.ds(r0, HG_ROWS), :] = _nn(a.astype(BF16), v16) + _nt(qe, st.astype(BF16))
            return jnp.exp(bl) * st + _tn(v16, kd)

        lax.fori_loop(0, n_blocks, block, jnp.zeros((HG_DK, HG_DK), F32))

    cspec = lambda col: pl.BlockSpec((lp, HG_DK), lambda h: (0, col // HG_DK + h))
    return pl.pallas_call(
        body, grid=(hh,),
        in_specs=[cspec(col_q), cspec(col_z), cspec(col_i),
                  pl.BlockSpec((None, 1, HG_DK), lambda h: (h, 0, 0))],
        out_specs=[pl.BlockSpec((lp, HG_DK), lambda h: (0, h)),
                   pl.BlockSpec((None, n_blocks, HG_DK, HG_DK), lambda h: (h, 0, 0, 0))],
        out_shape=[jax.ShapeDtypeStruct((lp, hh * HG_DK), F32),
                   jax.ShapeDtypeStruct((hh, n_blocks, HG_DK, HG_DK), F32)],
        scratch_shapes=[pltpu.VMEM((len(HG_HALVES), HG_ROWS, HG_ROWS), F32)],
        compiler_params=_cp("parallel"), name=name)(proj, proj, proj, lb)


def _hg_scan_bwd(proj, lb, states, do, *, reverse, col_q, col_z, col_i, hh, name):
    lp = proj.shape[0]
    n_blocks = lp // HG_ROWS
    last = 0 if reverse else HG_ROWS - 1

    def body(q_ref, z_ref, i_ref, lb_ref, st_ref, do_ref, dq_ref, dz_ref, dv_ref, dlb_ref, mask_ref):
        lbv = lb_ref[...]
        pos = lax.broadcasted_iota(jnp.int32, (HG_ROWS, 1), 0)
        ri = lax.broadcasted_iota(jnp.int32, (HG_ROWS, HG_ROWS), 0)
        ci = lax.broadcasted_iota(jnp.int32, (HG_ROWS, HG_ROWS), 1)
        _pair_masks(mask_ref)

        def block(bi, carry):
            dst, dlb = carry
            bb = bi if reverse else (n_blocks - 1 - bi)
            r0 = pl.multiple_of(bb * HG_ROWS, HG_ROWS)
            zq = q_ref[pl.ds(r0, HG_ROWS), :]
            v16 = i_ref[pl.ds(r0, HG_ROWS), :].astype(BF16)
            do16 = do_ref[pl.ds(r0, HG_ROWS), :].astype(BF16)
            qh, sq, s, f, g, kk = _hg_gates(zq, z_ref[pl.ds(r0, HG_ROWS), :], lbv)
            b = _block_cumsum(g, pos, reverse)
            bl = b[last:last + 1, :]
            eb = jnp.exp(b)
            ebl = jnp.exp(bl - b)
            decay = jnp.exp(bl)
            qe16 = (qh * eb).astype(BF16)
            kd16 = (kk * ebl).astype(BF16)
            st = st_ref[bb]
            st16, dst16 = st.astype(BF16), dst.astype(BF16)
            same_row = ri == ci
            da = _nt(do16, v16)
            da_diag = jnp.sum(jnp.where(same_row, da, 0.0), axis=1, keepdims=True)
            dq_state = eb * _nn(do16, st16)
            dk_state = ebl * _nn(v16, dst16)
            dq = dq_state + da_diag * kk
            dk = dk_state + da_diag * qh
            dbl = (decay * jnp.sum(st * dst, axis=0, keepdims=True)
                   + jnp.sum(kk * dk_state, axis=0, keepdims=True))
            db = qh * dq_state - kk * dk_state + jnp.where(pos == last, dbl, 0.0)
            a = jnp.where(same_row, jnp.sum(qh * kk, axis=1, keepdims=True), 0.0)
            for i, (eq, ek) in enumerate(_pair_levels(b, pos, reverse)):
                same = mask_ref[i]
                q16, k16 = (qh * eq).astype(BF16), (kk * ek).astype(BF16)
                a = a + same * _nt(q16, k16)
                da16 = (same * da).astype(BF16)
                gq, gk = _nn(da16, k16), _tn(da16, q16)
                dq = dq + eq * gq
                dk = dk + ek * gk
                db = db + (q16.astype(F32) * gq - k16.astype(F32) * gk)
            dg = _block_cumsum(db, pos, not reverse)
            df = dg / f - dk
            dq_ref[pl.ds(r0, HG_ROWS), :] = dq * (sq + zq * sq * (1.0 - sq))
            dz_ref[pl.ds(r0, HG_ROWS), :] = df * (1.0 - lbv) * s * (1.0 - s)
            dv_ref[pl.ds(r0, HG_ROWS), :] = _nt(kd16, dst16) + _tn(a.astype(BF16), do16)
            return (decay * dst + _tn(do16, qe16),
                    dlb + jnp.sum(df * (1.0 - s), axis=0, keepdims=True))

        _, dlb = lax.fori_loop(0, n_blocks, block,
                               (jnp.zeros((HG_DK, HG_DK), F32), jnp.zeros((1, HG_DK), F32)))
        dlb_ref[...] = dlb

    cspec = lambda col: pl.BlockSpec((lp, HG_DK), lambda h: (0, col // HG_DK + h))
    ospec = pl.BlockSpec((lp, HG_DK), lambda h: (0, h))
    sds = jax.ShapeDtypeStruct((lp, hh * HG_DK), F32)
    return pl.pallas_call(
        body, grid=(hh,),
        in_specs=[cspec(col_q), cspec(col_z), cspec(col_i),
                  pl.BlockSpec((None, 1, HG_DK), lambda h: (h, 0, 0)),
                  pl.BlockSpec((None, n_blocks, HG_DK, HG_DK), lambda h: (h, 0, 0, 0)),
                  ospec],
        out_specs=[ospec, ospec, ospec, pl.BlockSpec((None, 1, HG_DK), lambda h: (h, 0, 0))],
        out_shape=[sds, sds, sds, jax.ShapeDtypeStruct((hh, 1, HG_DK), F32)],
        scratch_shapes=[pltpu.VMEM((len(HG_HALVES), HG_ROWS, HG_ROWS), F32)],
        compiler_params=_cp("parallel"), name=name)(proj, proj, proj, lb, states, do)


NA_HB = LANES // NA_HEAD_DIM
NA_G = 4
NA_U = NA_G + NA_WIN_H
NA_QN = NA_G * GRID_W
NA_KN = NA_U * GRID_W


def _na_table_index(pattern, a, j):
    if pattern == 0:
        return j - a + NA_WIN_H - 1 if j < NA_WIN_H else None
    if pattern == 2:
        return j - a - 1 if j >= NA_U - NA_WIN_H else None
    return j - a + NA_WIN_H // 2 - 1 if a <= j < a + NA_WIN_H else None


def _na_step_rows(pattern, t, rows):
    if pattern == 0:
        r0, us = 0, 0
    elif pattern == 2:
        r0, us = rows - NA_G, rows - NA_U
    else:
        r0 = NA_G * t
        us = r0 - NA_WIN_H // 2
    q0, k0 = N_META + GRID_W * r0, N_META + GRID_W * us
    if pattern == 1:
        q0, k0 = pl.multiple_of(q0, 16), pl.multiple_of(k0, 16)
    return q0, k0


def _na_fill_bias(tb_ref, bias_ref):
    neg = jnp.full((GRID_W, GRID_W), -1e30, F32)
    for h in range(NA_HB):
        for pattern in range(3):
            for a in range(NA_G):
                for j in range(NA_U):
                    idx = _na_table_index(pattern, a, j)
                    bias_ref[h, pattern, a * GRID_W:(a + 1) * GRID_W, j * GRID_W:(j + 1) * GRID_W] = (
                        neg if idx is None else tb_ref[h, idx])


def _na_steps(rows, step, carry):
    n_steps = rows // NA_G
    carry = step(0, 0, carry)
    carry = lax.fori_loop(1, n_steps - 1, functools.partial(step, 1), carry)
    return step(2, n_steps - 1, carry)


def _na_head_lanes():
    lane = lax.broadcasted_iota(jnp.int32, (1, LANES), 1)
    return [lane // NA_HEAD_DIM == h for h in range(NA_HB)]


def _na_only(mask, x):
    return jnp.where(mask, x, jnp.zeros_like(x))


def _na_fwd(proj, tb, *, n_tok, nh, name):
    lp = proj.shape[0]
    dh, hb = NA_HEAD_DIM, NA_HB
    naw = nh * dh
    rows = n_tok // GRID_W
    scale = dh ** -0.5

    def body(q_ref, k_ref, v_ref, tb_ref, o_ref, lse_ref, q16_ref, k16_ref, v16_ref, bias_ref):
        o_ref[...] = jnp.zeros_like(o_ref)
        lse_ref[...] = jnp.zeros_like(lse_ref)
        q16_ref[...] = q_ref[...].astype(BF16)
        k16_ref[...] = k_ref[...].astype(BF16)
        v16_ref[...] = v_ref[...].astype(BF16)
        _na_fill_bias(tb_ref, bias_ref)
        heads = _na_head_lanes()
        km = k16_ref[0:N_META, :]
        vm = v16_ref[0:N_META, :]
        qm = q16_ref[0:N_META, :]
        o_m = None
        for h in range(hb):
            s = _nt(_na_only(heads[h], qm), km) * scale
            m = jnp.max(s, axis=1, keepdims=True)
            p = jnp.exp(s - m)
            l = jnp.sum(p, axis=1, keepdims=True)
            o_h = _nn(p.astype(BF16), vm) / l
            o_m = o_h if o_m is None else jnp.where(heads[h], o_h, o_m)
            lse_ref[h, 0:N_META, :] = m + jnp.log(l)
        o_ref[0:N_META, :] = o_m

        def step(pattern, t, carry):
            q0, k0 = _na_step_rows(pattern, t, rows)
            q16 = q16_ref[pl.ds(q0, NA_QN), :]
            k16 = k16_ref[pl.ds(k0, NA_KN), :]
            v16 = v16_ref[pl.ds(k0, NA_KN), :]
            o = None
            for h in range(hb):
                q_h = _na_only(heads[h], q16)
                s = _nt(q_h, k16) * scale + bias_ref[h, pattern]
                sm = _nt(q_h, km) * scale
                m = jnp.maximum(jnp.max(s, axis=1, keepdims=True),
                                jnp.max(sm, axis=1, keepdims=True))
                p = jnp.exp(s - m)
                pm = jnp.exp(sm - m)
                l = jnp.sum(p, axis=1, keepdims=True) + jnp.sum(pm, axis=1, keepdims=True)
                o_h = (_nn(p.astype(BF16), v16) + _nn(pm.astype(BF16), vm)) / l
                o = o_h if o is None else jnp.where(heads[h], o_h, o)
                lse_ref[h, pl.ds(q0, NA_QN), :] = m + jnp.log(l)
            o_ref[pl.ds(q0, NA_QN), :] = o
            return carry

        _na_steps(rows, step, 0)

    cblk = lambda col: pl.BlockSpec((lp, LANES), lambda g: (0, col // LANES + g))
    return pl.pallas_call(
        body, grid=(nh // hb,),
        in_specs=[cblk(0), cblk(naw), cblk(2 * naw),
                  pl.BlockSpec((hb, 2 * NA_WIN_H - 1, GRID_W, GRID_W), lambda g: (g, 0, 0, 0))],
        out_specs=[cblk(0), pl.BlockSpec((hb, lp, 1), lambda g: (g, 0, 0))],
        out_shape=[jax.ShapeDtypeStruct((lp, naw), F32), jax.ShapeDtypeStruct((nh, lp, 1), F32)],
        scratch_shapes=[pltpu.VMEM((lp, LANES), BF16)] * 3 + [pltpu.VMEM((hb, 3, NA_QN, NA_KN), F32)],
        compiler_params=_cp("parallel"), name=name)(proj, proj, proj, tb)


def _na_bwd(proj, tb, o, lse, do, *, n_tok, nh, name):
    lp = proj.shape[0]
    dh, hb = NA_HEAD_DIM, NA_HB
    naw = nh * dh
    rows = n_tok // GRID_W
    scale = dh ** -0.5

    def body(q_ref, k_ref, v_ref, tb_ref, o_ref, lse_ref, do_ref, dq_ref, dk_ref, dv_ref, dtb_ref,
             q16_ref, k16_ref, v16_ref, bias_ref):
        dq_ref[...] = jnp.zeros_like(dq_ref)
        dk_ref[...] = jnp.zeros_like(dk_ref)
        dv_ref[...] = jnp.zeros_like(dv_ref)
        dtb_ref[...] = jnp.zeros_like(dtb_ref)
        q16_ref[...] = q_ref[...].astype(BF16)
        k16_ref[...] = k_ref[...].astype(BF16)
        v16_ref[...] = v_ref[...].astype(BF16)
        _na_fill_bias(tb_ref, bias_ref)
        heads = _na_head_lanes()
        km = k16_ref[0:N_META, :]
        vm = v16_ref[0:N_META, :]
        qm = q16_ref[0:N_META, :]
        dom = do_ref[0:N_META, :]
        prod = dom * o_ref[0:N_META, :]
        dq_m = None
        dkm0 = jnp.zeros((N_META, LANES), F32)
        dvm0 = jnp.zeros((N_META, LANES), F32)
        for h in range(hb):
            q_h = _na_only(heads[h], qm)
            do_h = _na_only(heads[h], dom).astype(BF16)
            p = jnp.exp(_nt(q_h, km) * scale - lse_ref[h, 0:N_META, :])
            delta = jnp.sum(_na_only(heads[h], prod), axis=1, keepdims=True)
            ds = (p * (_nt(do_h, vm) - delta)).astype(BF16)
            dq_h = _nn(ds, km) * scale
            dq_m = dq_h if dq_m is None else jnp.where(heads[h], dq_h, dq_m)
            dkm0 = dkm0 + _tn(ds, q_h) * scale
            dvm0 = dvm0 + _tn(p.astype(BF16), do_h)
        dq_ref[0:N_META, :] = dq_m

        def step(pattern, t, carry):
            dkm, dvm = carry
            q0, k0 = _na_step_rows(pattern, t, rows)
            q16 = q16_ref[pl.ds(q0, NA_QN), :]
            k16 = k16_ref[pl.ds(k0, NA_KN), :]
            v16 = v16_ref[pl.ds(k0, NA_KN), :]
            dov = do_ref[pl.ds(q0, NA_QN), :]
            prod = dov * o_ref[pl.ds(q0, NA_QN), :]
            dq = None
            dk = jnp.zeros((NA_KN, LANES), F32)
            dv = jnp.zeros((NA_KN, LANES), F32)
            for h in range(hb):
                q_h = _na_only(heads[h], q16)
                do_h = _na_only(heads[h], dov).astype(BF16)
                lse = lse_ref[h, pl.ds(q0, NA_QN), :]
                p = jnp.exp(_nt(q_h, k16) * scale + bias_ref[h, pattern] - lse)
                pm = jnp.exp(_nt(q_h, km) * scale - lse)
                delta = jnp.sum(_na_only(heads[h], prod), axis=1, keepdims=True)
                ds = p * (_nt(do_h, v16) - delta)
                dsm = (pm * (_nt(do_h, vm) - delta)).astype(BF16)
                ds16 = ds.astype(BF16)
                dq_h = (_nn(ds16, k16) + _nn(dsm, km)) * scale
                dq = dq_h if dq is None else jnp.where(heads[h], dq_h, dq)
                dk = dk + _tn(ds16, q_h) * scale
                dv = dv + _tn(p.astype(BF16), do_h)
                dkm = dkm + _tn(dsm, q_h) * scale
                dvm = dvm + _tn(pm.astype(BF16), do_h)
                for a in range(NA_G):
                    for j in range(NA_U):
                        idx = _na_table_index(pattern, a, j)
                        if idx is not None:
                            dtb_ref[h, idx] += ds[a * GRID_W:(a + 1) * GRID_W,
                                                  j * GRID_W:(j + 1) * GRID_W]
            dq_ref[pl.ds(q0, NA_QN), :] = dq
            dk_ref[pl.ds(k0, NA_KN), :] += dk
            dv_ref[pl.ds(k0, NA_KN), :] += dv
            return dkm, dvm

        dkm, dvm = _na_steps(rows, step, (dkm0, dvm0))
        dk_ref[0:N_META, :] += dkm
        dv_ref[0:N_META, :] += dvm

    cblk = lambda col: pl.BlockSpec((lp, LANES), lambda g: (0, col // LANES + g))
    tbs = pl.BlockSpec((hb, 2 * NA_WIN_H - 1, GRID_W, GRID_W), lambda g: (g, 0, 0, 0))
    sds = jax.ShapeDtypeStruct((lp, naw), F32)
    return pl.pallas_call(
        body, grid=(nh // hb,),
        in_specs=[cblk(0), cblk(naw), cblk(2 * naw), tbs, cblk(0),
                  pl.BlockSpec((hb, lp, 1), lambda g: (g, 0, 0)), cblk(0)],
        out_specs=[cblk(0), cblk(0), cblk(0), tbs],
        out_shape=[sds, sds, sds, jax.ShapeDtypeStruct(tb.shape, F32)],
        scratch_shapes=[pltpu.VMEM((lp, LANES), BF16)] * 3 + [pltpu.VMEM((hb, 3, NA_QN, NA_KN), F32)],
        compiler_params=_cp("parallel"), name=name)(proj, proj, proj, tb, o, lse, do)


def _rpb_onehot():
    c = np.arange(GRID_W)[:, None]
    w = np.arange(GRID_W)[None, :]
    cs = np.clip(c - NA_WIN_W // 2, 0, GRID_W - NA_WIN_W)
    in_win = (w >= cs) & (w < cs + NA_WIN_W)
    dc = np.clip(w - c, -(NA_WIN_W - 1), NA_WIN_W - 1) + NA_WIN_W - 1
    oh = np.zeros((LANES, GRID_W * GRID_W), np.float32)
    flat = np.arange(GRID_W * GRID_W).reshape(GRID_W, GRID_W)
    oh[dc[in_win], flat[in_win]] = 1.0
    neg = np.where(in_win, 0.0, -1e30).astype(np.float32).reshape(1, -1)
    return oh, neg


def _assemble_dproj(dq_na, dk_na, dv_na, dq_f, dq_b, dz_f, dz_b, dv_f, dv_b, dg, dgn, dgh, *, name):
    lp, naw = dq_na.shape
    hgw = dq_f.shape[1]
    d = dgn.shape[1]
    cols = 3 * naw + 5 * hgw + 2 * d
    tr = _row_tile(lp, 3 * naw * 4 + 6 * hgw * 4 + hgw * 2 + 2 * d * 2 + cols * 2)

    def body(nq_ref, nk_ref, nv_ref, qf_ref, qb_ref, zf_ref, zb_ref, vf_ref, vb_ref, g_ref, gn_ref,
             gh_ref, o_ref):
        o_ref[:, 0:naw] = nq_ref[...].astype(BF16)
        o_ref[:, naw:2 * naw] = nk_ref[...].astype(BF16)
        o_ref[:, 2 * naw:3 * naw] = nv_ref[...].astype(BF16)
        c0 = 3 * naw
        o_ref[:, c0:c0 + hgw] = (qf_ref[...] + qb_ref[...]).astype(BF16)
        o_ref[:, c0 + hgw:c0 + 2 * hgw] = zf_ref[...].astype(BF16)
        o_ref[:, c0 + 2 * hgw:c0 + 3 * hgw] = zb_ref[...].astype(BF16)
        o_ref[:, c0 + 3 * hgw:c0 + 4 * hgw] = (vf_ref[...] + vb_ref[...]).astype(BF16)
        o_ref[:, c0 + 4 * hgw:c0 + 5 * hgw] = g_ref[...]
        o_ref[:, c0 + 5 * hgw:c0 + 5 * hgw + d] = gn_ref[...]
        o_ref[:, c0 + 5 * hgw + d:] = gh_ref[...]

    hg, na = _rspec(tr, hgw), _rspec(tr, naw)
    return pl.pallas_call(
        body, grid=(lp // tr,),
        in_specs=[na, na, na, hg, hg, hg, hg, hg, hg, hg, _rspec(tr, d), _rspec(tr, d)],
        out_specs=_rspec(tr, cols),
        out_shape=jax.ShapeDtypeStruct((lp, cols), BF16),
        compiler_params=_cp("parallel"), name=name)(dq_na, dk_na, dv_na, dq_f, dq_b, dz_f, dz_b,
                                                    dv_f, dv_b, dg, dgn, dgh)


GROUP_STEPS = 8


def _group_tiles(rows, align):
    steps = GROUP_STEPS if all(r % (GROUP_STEPS * align) == 0 for r in rows) else 1
    return steps, [r // steps for r in rows]


def _adamw(ws, gs, ms, vs, *, name):
    n = len(ws)
    steps, trs = _group_tiles([w.shape[0] for w in ws], 8)

    def body(*refs):
        for i in range(n):
            w_ref, g_ref, m_ref, v_ref = refs[4 * i:4 * i + 4]
            d_ref, mo_ref, vo_ref = refs[4 * n + 3 * i:4 * n + 3 * i + 3]
            gv = g_ref[...]
            mn = ADAM_B1 * m_ref[...] + (1.0 - ADAM_B1) * gv
            vn = ADAM_B2 * v_ref[...] + (1.0 - ADAM_B2) * (gv * gv)
            m_hat = mn / (1.0 - ADAM_B1 ** ADAM_STEP)
            v_hat = vn / (1.0 - ADAM_B2 ** ADAM_STEP)
            d_ref[...] = -ADAM_LR * (m_hat / (jnp.sqrt(v_hat) + ADAM_EPS) + ADAM_WD * w_ref[...])
            mo_ref[...] = mn
            vo_ref[...] = vn

    specs = [_rspec(tr, w.shape[1]) for tr, w in zip(trs, ws)]
    out = pl.pallas_call(
        body, grid=(steps,),
        in_specs=[s for s in specs for _ in range(4)],
        out_specs=[s for s in specs for _ in range(3)],
        out_shape=[jax.ShapeDtypeStruct(w.shape, F32) for w in ws for _ in range(3)],
        compiler_params=_cp("parallel"), name=name)(*[a for q in zip(ws, gs, ms, vs) for a in q])
    return [tuple(out[3 * i:3 * i + 3]) for i in range(n)]


def _local_step(x, tgt, meta, first_weight, rest_weights, g_mix, g_mlp, g_fin, hg_gain, rpb, lb,
                early_grads=None, mid_grads=None, late_grad=None, rest_landed=None,
                last_grads=None):
    n_tok, d = x.shape
    hgw = hg_gain.shape[1]
    nh, hh = rpb.shape[0], hgw // HG_DK
    naw = nh * NA_HEAD_DIM
    l_real = N_META + n_tok
    lp = -(-l_real // ROW_ALIGN) * ROW_ALIGN
    col_qhg = 3 * naw
    col_zf, col_zb, col_i, col_g = (col_qhg + hgw, col_qhg + 2 * hgw, col_qhg + 3 * hgw,
                                    col_qhg + 4 * hgw)
    col_gate = col_qhg + 5 * hgw

    oh_np, neg_np = _rpb_onehot()
    oh = jnp.asarray(oh_np)
    rpb_p = jnp.pad(rpb.reshape(nh * (2 * NA_WIN_H - 1), 2 * NA_WIN_W - 1),
                    ((0, 0), (0, LANES - (2 * NA_WIN_W - 1))))
    tb = _matmul(rpb_p, oh, tm=rpb_p.shape[0], tn=512, tk=LANES, precision=HIGHEST,
                 name="rpb_expand")
    tb = (tb + jnp.asarray(neg_np)).reshape(nh, 2 * NA_WIN_H - 1, GRID_W, GRID_W)

    h0, a = _embed_norm(x, meta, g_mix, lp=lp, name="norm_mix")
    w_in = first_weight(a)
    proj = _matmul(a, w_in, name="mm_in")
    o_na, lse = _na_fwd(proj, tb, n_tok=n_tok, nh=nh, name="na_fwd")
    lb_f = lb[0].reshape(hh, 1, HG_DK)
    lb_b = lb[1].reshape(hh, 1, HG_DK)
    scan_kw = dict(col_q=col_qhg, col_i=col_i, hh=hh)
    o_f, st_f = _hg_scan_fwd(proj, lb_f, reverse=False, col_z=col_zf, name="hg_scan_f", **scan_kw)
    token = rest_landed(o_f) if rest_landed else None
    lb_b_late = lb_b if token is None else lb_b + token[0:1, 0:1]
    o_b, st_b = _hg_scan_fwd(proj, lb_b_late, reverse=True, col_z=col_zb, name="hg_scan_b",
                             **scan_kw)
    o_hg = _hg_out(o_f, o_b, proj, hg_gain, col_g=col_g, name="hg_out")
    w_na, w_hg, w_o, w_up, w_down = rest_weights(o_hg)
    y_na = _matmul(o_na, w_na, name="mm_na_out")
    gates = ((proj, col_gate), (proj, col_gate + d))

    def mix_gates(acc, gn, gh, yn):
        return acc, _sigmoid(gn) * yn + _sigmoid(gh) * acc

    def mix_gates_bwd(dmix, gn, gh, yn, yh):
        sn, sh = _sigmoid(gn), _sigmoid(gh)
        return dmix * sn, dmix * sh, dmix * yn * sn * (1.0 - sn), dmix * yh * sh * (1.0 - sh)

    y_hg, mix = _matmul(o_hg, w_hg, name="mm_hg_out", epilogue=mix_gates,
                        tiles=(*gates, (y_na, 0)), out_dtypes=(F32, BF16))
    t1 = _matmul(mix, w_o, name="mm_o")
    h1, mlp_in = _residual_norm(h0, t1, g_mlp, name="resid_norm_mlp")
    u, act = _matmul(mlp_in, w_up, name="mm_up", out_dtypes=(BF16, BF16),
                     epilogue=lambda acc: (acc, jnp.square(jnp.maximum(acc, 0.0))))
    t2 = _matmul(act, w_down, name="mm_down")
    dh2, loss, dg_fin = _final_loss(h1, t2, g_fin, tgt, name="final_loss")

    (du,) = _matmul(dh2, w_down, tb=True, name="mm_down_dx", tiles=((u, 0),), out_dtypes=(BF16,),
                    epilogue=lambda acc, uv: (acc * 2.0 * jnp.maximum(uv, 0.0),))
    dw_down = _matmul(act, dh2, ta=True, name="mm_down_dw")
    dm = _matmul(du, w_up, tb=True, name="mm_up_dx")
    dw_up = _matmul(mlp_in, du, ta=True, name="mm_up_dw")
    dh1, dg_mlp = _rmsnorm_bwd_add(h1, g_mlp, dm, dh2, name="norm_mlp_bwd")
    dy_na, dy_hg, dgn, dgh = _matmul(dh1, w_o, tb=True, name="mm_o_dx", epilogue=mix_gates_bwd,
                                     tiles=(*gates, (y_na, 0), (y_hg, 0)), out_dtypes=(BF16,) * 4)
    dw_o = _matmul(mix, dh1, ta=True, name="mm_o_dw")
    do_na = _matmul(dy_na, w_na, tb=True, name="mm_na_out_dx")
    dw_na = _matmul(o_na, dy_na, ta=True, name="mm_na_out_dw")
    do_hg = _matmul(dy_hg, w_hg, tb=True, name="mm_hg_out_dx")
    dw_hg = _matmul(o_hg, dy_hg, ta=True, name="mm_hg_out_dw")
    token = early_grads([dw_na, dw_hg, dw_o, dw_up, dw_down]) if early_grads else None
    if token is not None:
        hg_gain = hg_gain + token[0:1, 0:1]
    d_o, dg_hg, d_gain = _hg_out_bwd(o_f, o_b, proj, hg_gain, do_hg, col_g=col_g, name="hg_out_bwd")
    dq_f, dz_f, dv_f, dlb_f = _hg_scan_bwd(proj, lb_f, st_f, d_o, reverse=False, col_z=col_zf,
                                           name="hg_scan_f_bwd", **scan_kw)
    token = mid_grads(dq_f) if mid_grads else None
    lb_b_late = lb_b if token is None else lb_b + token[0:1, 0:1]
    dq_b, dz_b, dv_b, dlb_b = _hg_scan_bwd(proj, lb_b_late, st_b, d_o, reverse=True, col_z=col_zb,
                                           name="hg_scan_b_bwd", **scan_kw)
    dq_na, dk_na, dv_na, dtb = _na_bwd(proj, tb, o_na, lse, do_na, n_tok=n_tok, nh=nh, name="na_bwd")
    dproj = _assemble_dproj(dq_na, dk_na, dv_na, dq_f, dq_b, dz_f, dz_b, dv_f, dv_b, dg_hg, dgn,
                            dgh, name="assemble_dproj")
    dw_in = _matmul(a, dproj, ta=True, name="mm_in_dw")
    token = late_grad(dw_in) if late_grad else None
    da = _matmul(dproj, w_in, tb=True, name="mm_in_dx", after=token)
    token = last_grads(da) if last_grads else None
    g_mix_late = g_mix if token is None else g_mix + token[0:1, 0:1]
    dx, dmeta, dg_mix = _rmsnorm_bwd_tokens(h0, g_mix_late, da, dh1, n_tok=n_tok,
                                            name="norm_mix_bwd")
    d_rpb = _matmul(dtb.reshape(nh * (2 * NA_WIN_H - 1), GRID_W * GRID_W), oh, tb=True,
                    tm=nh * (2 * NA_WIN_H - 1), tn=LANES, tk=1024, precision=HIGHEST,
                    name="rpb_reduce")
    d_lb = jnp.concatenate([dlb_f.reshape(1, hgw), dlb_b.reshape(1, hgw)], axis=0)
    return (loss, dx, dmeta, dw_in, dw_na, dw_hg, dw_o, dw_up, dw_down,
            dg_mix, dg_mlp, dg_fin, d_gain, d_rpb, d_lb)


N_CHIPS = 4
N_DEV = 8
ANY = pl.BlockSpec(memory_space=pl.ANY)


def _place():
    x, y, c = lax.axis_index("x"), lax.axis_index("y"), lax.axis_index("c")
    others = []
    for j in (1, 2, 3):
        tx = (1 - x) if (j >> 1) else x
        ty = (1 - y) if (j & 1) else y
        others.append((tx, ty))
    return x, y, c, others


def _piece(ref, axis, k, half, rh, cs):
    if axis == 1:
        return ref.at[pl.ds(pl.multiple_of(half * rh, 16), rh), pl.ds(pl.multiple_of(k * cs, LANES), cs)]
    return ref.at[pl.ds(pl.multiple_of(k * 2 * rh + half * rh, 16), rh), :]


def _cast_into_full(shards, axes, place, *, name):
    n = len(shards)
    steps, trs = _group_tiles([s.shape[0] for s in shards], 16)

    def body(p_ref, *refs):
        for i in range(n):
            refs[n + i][...] = refs[i][...].astype(BF16)

    def out_spec(tr, cs, axis):
        if axis == 1:
            return pl.BlockSpec((tr, cs), lambda i, p_ref: (i, p_ref[0]))
        return pl.BlockSpec((tr, cs), lambda i, p_ref: (p_ref[0] * steps + i, 0))

    return pl.pallas_call(
        body,
        grid_spec=pltpu.PrefetchScalarGridSpec(
            num_scalar_prefetch=1, grid=(steps,),
            in_specs=[pl.BlockSpec((tr, s.shape[1]), lambda i, p_ref: (i, 0))
                      for tr, s in zip(trs, shards)],
            out_specs=[out_spec(tr, s.shape[1], ax) for tr, s, ax in zip(trs, shards, axes)]),
        out_shape=[jax.ShapeDtypeStruct((s.shape[0], s.shape[1] * N_CHIPS) if ax == 1
                                        else (s.shape[0] * N_CHIPS, s.shape[1]), BF16)
                   for s, ax in zip(shards, axes)],
        compiler_params=_cp("parallel"), name=name)(place, *shards)


HBM_SPEC = pl.BlockSpec(memory_space=pltpu.HBM)
SEM_SPEC = pl.BlockSpec(memory_space=pltpu.SEMAPHORE)
SPLIT_COPY = pltpu.CompilerParams(has_side_effects=pltpu.SideEffectType.DATAFLOW_SIDE_EFFECTING)
TOKEN = jax.ShapeDtypeStruct((8, LANES), F32)


def _geo(fulls, axes):
    out = []
    for f, ax in zip(fulls, axes):
        r, cs = (f.shape[0], f.shape[1] // N_CHIPS) if ax == 1 else (f.shape[0] // N_CHIPS, f.shape[1])
        out.append((ax, r // 2, cs))
    return out


def _gather_copies(refs, geo, send_sems, recv_sems):
    x, y, c, others = _place()
    chip = 2 * x + y
    cps = []
    for i, (ax, rh, cs) in enumerate(geo):
        mine = _piece(refs[i], ax, chip, c, rh, cs)
        for j, (tx, ty) in enumerate(others):
            cps.append(pltpu.make_async_remote_copy(
                src_ref=mine, dst_ref=mine, send_sem=send_sems.at[3 * i + j],
                recv_sem=recv_sems.at[3 * i + j], device_id=(tx, ty, c), device_id_type=MESH))
    return cps


def _allgather_start(fulls, axes, after, *, name):
    n = len(fulls)
    geo = _geo(fulls, axes)

    def body(*refs):
        w_refs = refs[:n]
        send_sems, recv_sems = refs[n + 1], refs[n + 2]
        token = refs[2 * n + 3]
        for cp in _gather_copies(w_refs, geo, send_sems, recv_sems):
            cp.start()
        token[...] = jnp.zeros_like(token)

    out = pl.pallas_call(
        body, name=name,
        out_shape=(pltpu.SemaphoreType.DMA((3 * n,)), pltpu.SemaphoreType.DMA((3 * n,)),
                   *[pltpu.HBM(f.shape, f.dtype) for f in fulls], TOKEN),
        in_specs=[HBM_SPEC] * n + [ANY],
        out_specs=(SEM_SPEC, SEM_SPEC, *[HBM_SPEC] * n, pl.BlockSpec(memory_space=pltpu.VMEM)),
        input_output_aliases={i: 2 + i for i in range(n)},
        compiler_params=SPLIT_COPY,
    )(*[pltpu.with_memory_space_constraint(f, pltpu.HBM) for f in fulls], after)
    return out[0], out[1], list(out[2:2 + n]), out[2 + n]


def _allgather_wait(send_sems, recv_sems, fulls, axes, after, *, name):
    n = len(fulls)
    geo = _geo(fulls, axes)

    def body(*refs):
        w_refs = refs[:n]
        for cp in _gather_copies(w_refs, geo, refs[n], refs[n + 1]):
            cp.wait_send()
            cp.wait_recv()

    return list(pl.pallas_call(
        body, name=name,
        out_shape=[pltpu.HBM(f.shape, f.dtype) for f in fulls],
        in_specs=[HBM_SPEC] * n + [SEM_SPEC, SEM_SPEC, ANY],
        out_specs=[HBM_SPEC] * n,
        input_output_aliases={i: i for i in range(n)},
        compiler_params=SPLIT_COPY,
    )(*fulls, send_sems, recv_sems, after))


def _allgather_forward(fulls, axes, *, name):
    n = len(fulls)
    geo = _geo(fulls, axes)

    def body(*refs):
        o_refs = refs[n:2 * n]
        send_sems, recv_sems = refs[2 * n:]
        x, y, c, others = _place()

        def rcopy(i, j, half, to):
            ax, rh, cs = geo[i]
            ref = _piece(o_refs[i], ax, 2 * others[j][0] + others[j][1], half, rh, cs)
            return pltpu.make_async_remote_copy(
                src_ref=ref, dst_ref=ref, send_sem=send_sems.at[3 * i + j],
                recv_sem=recv_sems.at[3 * i + j], device_id=to, device_id_type=MESH)

        cps = [rcopy(i, j, c, (x, y, 1 - c)) for i in range(n) for j in range(3)]
        for cp in cps:
            cp.start()
        for i in range(n):
            for j in range(3):
                rcopy(i, j, 1 - c, (x, y, c)).wait_recv()
        for cp in cps:
            cp.wait_send()

    return list(pl.pallas_call(
        body, in_specs=[ANY] * n, out_specs=[ANY] * n,
        out_shape=[jax.ShapeDtypeStruct(f.shape, f.dtype) for f in fulls],
        input_output_aliases={i: i for i in range(n)},
        scratch_shapes=[pltpu.SemaphoreType.DMA((3 * n,)), pltpu.SemaphoreType.DMA((3 * n,))],
        name=name)(*fulls))


def _forward_copies(refs, geo, send_sems, recv_sems):
    x, y, c, others = _place()
    cps = []
    for i, (ax, rh, cs) in enumerate(geo):
        for j, (tx, ty) in enumerate(others):
            ref = _piece(refs[i], ax, 2 * tx + ty, c, rh, cs)
            cps.append(pltpu.make_async_remote_copy(
                src_ref=ref, dst_ref=ref, send_sem=send_sems.at[3 * i + j],
                recv_sem=recv_sems.at[3 * i + j], device_id=(x, y, 1 - c), device_id_type=MESH))
    return cps


def _allgather_forward_start(fulls, axes, *, name):
    n = len(fulls)
    geo = _geo(fulls, axes)

    def body(*refs):
        token = refs[2 * n + 2]
        for cp in _forward_copies(refs[:n], geo, refs[n], refs[n + 1]):
            cp.start()
        token[...] = jnp.zeros_like(token)

    out = pl.pallas_call(
        body, name=name,
        out_shape=(pltpu.SemaphoreType.DMA((3 * n,)), pltpu.SemaphoreType.DMA((3 * n,)),
                   *[pltpu.HBM(f.shape, f.dtype) for f in fulls], TOKEN),
        in_specs=[HBM_SPEC] * n,
        out_specs=(SEM_SPEC, SEM_SPEC, *[HBM_SPEC] * n, pl.BlockSpec(memory_space=pltpu.VMEM)),
        input_output_aliases={i: 2 + i for i in range(n)},
        compiler_params=SPLIT_COPY,
    )(*fulls)
    return out[0], out[1], list(out[2:2 + n]), out[2 + n]


def _allgather_forward_wait(send_sems, recv_sems, fulls, axes, after, *, name):
    n = len(fulls)
    geo = _geo(fulls, axes)

    def body(*refs):
        for cp in _forward_copies(refs[:n], geo, refs[n], refs[n + 1]):
            cp.wait_send()
            cp.wait_recv()

    return list(pl.pallas_call(
        body, name=name,
        out_shape=[pltpu.HBM(f.shape, f.dtype) for f in fulls],
        in_specs=[HBM_SPEC] * n + [SEM_SPEC, SEM_SPEC, ANY],
        out_specs=[HBM_SPEC] * n,
        input_output_aliases={i: i for i in range(n)},
        compiler_params=SPLIT_COPY,
    )(*fulls, send_sems, recv_sems, after))


def _chip_copies(blk_ref, land_ref, send_sems, recv_sems):
    x, y, c, others = _place()
    return [pltpu.make_async_remote_copy(
        src_ref=blk_ref, dst_ref=land_ref.at[2 * x + y], send_sem=send_sems.at[j],
        recv_sem=recv_sems.at[j], device_id=(tx, ty, c), device_id_type=MESH)
        for j, (tx, ty) in enumerate(others)]


def _chip_exchange_start(blk, *, name):
    land = pltpu.with_memory_space_constraint(lax.empty((N_CHIPS, *blk.shape), blk.dtype), pltpu.HBM)

    def body(blk_ref, land_ref, send_sems, recv_sems, blk_out, land_out, token):
        for cp in _chip_copies(blk_ref, land_ref, send_sems, recv_sems):
            cp.start()
        token[...] = jnp.zeros_like(token)

    return pl.pallas_call(
        body, name=name,
        out_shape=(pltpu.SemaphoreType.DMA((3,)), pltpu.SemaphoreType.DMA((3,)),
                   pltpu.HBM(blk.shape, blk.dtype), pltpu.HBM(land.shape, land.dtype), TOKEN),
        in_specs=[HBM_SPEC] * 2,
        out_specs=(SEM_SPEC, SEM_SPEC, HBM_SPEC, HBM_SPEC, pl.BlockSpec(memory_space=pltpu.VMEM)),
        input_output_aliases={0: 2, 1: 3},
        compiler_params=SPLIT_COPY,
    )(pltpu.with_memory_space_constraint(blk, pltpu.HBM), land)


def _chip_exchange_wait(send_sems, recv_sems, blk, land, after, *, name):
    def body(blk_ref, land_ref, send_sems, recv_sems, after_ref, blk_out, land_out):
        for cp in _chip_copies(blk_ref, land_ref, send_sems, recv_sems):
            cp.wait_send()
            cp.wait_recv()

    return pl.pallas_call(
        body, name=name,
        out_shape=[pltpu.HBM(blk.shape, blk.dtype), pltpu.HBM(land.shape, land.dtype)],
        in_specs=[HBM_SPEC] * 2 + [SEM_SPEC, SEM_SPEC, ANY],
        out_specs=[HBM_SPEC] * 2,
        input_output_aliases={0: 0, 1: 1},
        compiler_params=SPLIT_COPY,
    )(blk, land, send_sems, recv_sems, after)[1]


def _scatter_geo(parts, axes):
    out = []
    for p, ax in zip(parts, axes):
        _, rh, cols = p.shape
        out.append((ax, rh, cols // N_CHIPS if ax == 1 else cols))
    return out


def _scatter_copies(p_refs, q_refs, geo, send_sems, recv_sems):
    x, y, c, others = _place()
    chip = 2 * x + y
    cps = []
    for i, (ax, rh, cw) in enumerate(geo):
        for j, (tx, ty) in enumerate(others):
            k = 2 * tx + ty
            src = (p_refs[i].at[0, :, pl.ds(pl.multiple_of(k * cw, LANES), cw)] if ax == 1
                   else p_refs[i].at[k])
            cps.append(pltpu.make_async_remote_copy(
                src_ref=src, dst_ref=q_refs[i].at[chip], send_sem=send_sems.at[3 * i + j],
                recv_sem=recv_sems.at[3 * i + j], device_id=(tx, ty, c), device_id_type=MESH))
    return cps


def _scatter_start(parts, axes, *, name):
    n = len(parts)
    geo = _scatter_geo(parts, axes)
    slots = [pltpu.HBM((N_CHIPS, rh, cw), p.dtype) for p, (_, rh, cw) in zip(parts, geo)]

    def body(*refs):
        p_refs, q_refs = refs[:n], refs[n:2 * n]
        send_sems, recv_sems = refs[2 * n], refs[2 * n + 1]
        token = refs[4 * n + 2]
        for cp in _scatter_copies(p_refs, q_refs, geo, send_sems, recv_sems):
            cp.start()
        token[...] = jnp.zeros_like(token)

    land = [pltpu.with_memory_space_constraint(lax.empty(s.inner_aval.shape, s.inner_aval.dtype), pltpu.HBM)
            for s in slots]
    out = pl.pallas_call(
        body, name=name,
        out_shape=(pltpu.SemaphoreType.DMA((3 * n,)), pltpu.SemaphoreType.DMA((3 * n,)),
                   *[pltpu.HBM(p.shape, p.dtype) for p in parts], *slots, TOKEN),
        in_specs=[HBM_SPEC] * (2 * n),
        out_specs=(SEM_SPEC, SEM_SPEC, *[HBM_SPEC] * (2 * n), pl.BlockSpec(memory_space=pltpu.VMEM)),
        input_output_aliases={i: 2 + i for i in range(2 * n)},
        compiler_params=SPLIT_COPY,
    )(*[pltpu.with_memory_space_constraint(p, pltpu.HBM) for p in parts], *land)
    return out[0], out[1], list(out[2:2 + n]), list(out[2 + n:2 + 2 * n]), out[2 + 2 * n]


def _scatter_wait(send_sems, recv_sems, parts, slots, axes, after, *, name):
    n = len(parts)
    geo = _scatter_geo(parts, axes)

    def body(*refs):
        p_refs, q_refs = refs[:n], refs[n:2 * n]
        for cp in _scatter_copies(p_refs, q_refs, geo, refs[2 * n], refs[2 * n + 1]):
            cp.wait_send()
            cp.wait_recv()

    out = pl.pallas_call(
        body, name=name,
        out_shape=[pltpu.HBM(a.shape, a.dtype) for a in (*parts, *slots)],
        in_specs=[HBM_SPEC] * (2 * n) + [SEM_SPEC, SEM_SPEC, ANY],
        out_specs=[HBM_SPEC] * (2 * n),
        input_output_aliases={i: i for i in range(2 * n)},
        compiler_params=SPLIT_COPY,
    )(*parts, *slots, send_sems, recv_sems, after)
    return list(out[:n]), list(out[n:])


def _sibling_swap(grads, *, name):
    n = len(grads)
    out_shape = [jax.ShapeDtypeStruct((g.shape[0], g.shape[1] // 2, g.shape[2]), g.dtype)
                 for g in grads]

    def body(*refs):
        g_refs, o_refs = refs[:n], refs[n:2 * n]
        send_sems, recv_sems = refs[2 * n:]
        x, y, c, _ = _place()
        cps = []
        for i in range(n):
            rh = grads[i].shape[1] // 2
            src = g_refs[i].at[:, pl.ds(pl.multiple_of((1 - c) * rh, 16), rh), :]
            cp = pltpu.make_async_remote_copy(
                src_ref=src, dst_ref=o_refs[i], send_sem=send_sems.at[i], recv_sem=recv_sems.at[i],
                device_id=(x, y, 1 - c), device_id_type=MESH)
            cp.start()
            cps.append(cp)
        for cp in cps:
            cp.wait()

    return pl.pallas_call(
        body, in_specs=[ANY] * n, out_specs=[ANY] * n, out_shape=out_shape,
        scratch_shapes=[pltpu.SemaphoreType.DMA((n,)), pltpu.SemaphoreType.DMA((n,))],
        name=name)(*grads)


def _swap_copies(g_refs, r_refs, shapes, send_sems, recv_sems):
    x, y, c, _ = _place()
    cps = []
    for i, shape in enumerate(shapes):
        rh = shape[1] // 2
        src = g_refs[i].at[:, pl.ds(pl.multiple_of((1 - c) * rh, 16), rh), :]
        cps.append(pltpu.make_async_remote_copy(
            src_ref=src, dst_ref=r_refs[i], send_sem=send_sems.at[i], recv_sem=recv_sems.at[i],
            device_id=(x, y, 1 - c), device_id_type=MESH))
    return cps


def _sibling_swap_start(grads, *, name):
    n = len(grads)
    shapes = [g.shape for g in grads]
    lands = [pltpu.HBM((s[0], s[1] // 2, s[2]), g.dtype) for s, g in zip(shapes, grads)]

    def body(*refs):
        g_refs, r_refs = refs[:n], refs[n:2 * n]
        token = refs[4 * n + 2]
        for cp in _swap_copies(g_refs, r_refs, shapes, refs[2 * n], refs[2 * n + 1]):
            cp.start()
        token[...] = jnp.zeros_like(token)

    land = [pltpu.with_memory_space_constraint(lax.empty(s.inner_aval.shape, s.inner_aval.dtype), pltpu.HBM)
            for s in lands]
    out = pl.pallas_call(
        body, name=name,
        out_shape=(pltpu.SemaphoreType.DMA((n,)), pltpu.SemaphoreType.DMA((n,)),
                   *[pltpu.HBM(g.shape, g.dtype) for g in grads], *lands, TOKEN),
        in_specs=[HBM_SPEC] * (2 * n),
        out_specs=(SEM_SPEC, SEM_SPEC, *[HBM_SPEC] * (2 * n), pl.BlockSpec(memory_space=pltpu.VMEM)),
        input_output_aliases={i: 2 + i for i in range(2 * n)},
        compiler_params=SPLIT_COPY,
    )(*[pltpu.with_memory_space_constraint(g, pltpu.HBM) for g in grads], *land)
    return out[0], out[1], list(out[2:2 + n]), list(out[2 + n:2 + 2 * n]), out[2 + 2 * n]


def _sibling_swap_wait(send_sems, recv_sems, grads, lands, after, *, name):
    n = len(grads)
    shapes = [g.shape for g in grads]

    def body(*refs):
        g_refs, r_refs = refs[:n], refs[n:2 * n]
        for cp in _swap_copies(g_refs, r_refs, shapes, refs[2 * n], refs[2 * n + 1]):
            cp.wait_send()
            cp.wait_recv()

    out = pl.pallas_call(
        body, name=name,
        out_shape=[pltpu.HBM(a.shape, a.dtype) for a in (*grads, *lands)],
        in_specs=[HBM_SPEC] * (2 * n) + [SEM_SPEC, SEM_SPEC, ANY],
        out_specs=[HBM_SPEC] * (2 * n),
        input_output_aliases={i: i for i in range(2 * n)},
        compiler_params=SPLIT_COPY,
    )(*grads, *lands, send_sems, recv_sems, after)
    return list(out[:n]), list(out[n:])


def _pair_add(g3s, rxs, place, *, out_dtype, name):
    n = len(g3s)
    steps, trs = _group_tiles([g.shape[1] // 2 for g in g3s], 16)

    def body(p_ref, *refs):
        for i in range(n):
            refs[2 * n + i][...] = (refs[2 * i][...] + refs[2 * i + 1][...]).astype(out_dtype)

    in_specs, out_specs = [], []
    for g, tr in zip(g3s, trs):
        blk = (g.shape[0], tr, g.shape[2])
        in_specs += [pl.BlockSpec(blk, lambda i, p_ref: (0, p_ref[1] * steps + i, 0)),
                     pl.BlockSpec(blk, lambda i, p_ref: (0, i, 0))]
        out_specs.append(pl.BlockSpec(blk, lambda i, p_ref: (0, i, 0)))
    return pl.pallas_call(
        body,
        grid_spec=pltpu.PrefetchScalarGridSpec(
            num_scalar_prefetch=1, grid=(steps,), in_specs=in_specs, out_specs=out_specs),
        out_shape=[jax.ShapeDtypeStruct((g.shape[0], g.shape[1] // 2, g.shape[2]), out_dtype)
                   for g in g3s],
        compiler_params=_cp("parallel"), name=name)(place, *[a for q in zip(g3s, rxs) for a in q])


def _sum_slots(q, *, name):
    ns, rows, cols = q.shape
    tr = next(t for t in (128, 64, 32, 16, 8) if rows % t == 0)

    def body(q_ref, o_ref):
        acc = q_ref[0].astype(F32)
        for k in range(1, ns):
            acc = acc + q_ref[k].astype(F32)
        o_ref[...] = acc

    return pl.pallas_call(
        body, grid=(rows // tr,),
        in_specs=[pl.BlockSpec((ns, tr, cols), lambda i: (0, i, 0))],
        out_specs=_rspec(tr, cols),
        out_shape=jax.ShapeDtypeStruct((rows, cols), F32),
        compiler_params=_cp("parallel"), name=name)(q)


def _sum_chips(qs, ps, place, axes, *, name):
    n = len(qs)
    per = N_CHIPS + 1
    steps, trs = _group_tiles([q.shape[1] for q in qs], 16)

    def body(p_ref, *refs):
        chip = p_ref[0]
        for i in range(n):
            q_refs, own_ref = refs[per * i:per * i + N_CHIPS], refs[per * i + N_CHIPS]
            acc = jnp.where(chip == 0, own_ref[...], q_refs[0][...]).astype(F32)
            for k in range(1, N_CHIPS):
                acc = acc + jnp.where(chip == k, own_ref[...], q_refs[k][...]).astype(F32)
            refs[per * n + i][...] = acc

    def slot_spec(k, tr, cw):
        return pl.BlockSpec((None, tr, cw),
                            lambda i, p_ref: (jnp.where(p_ref[0] == k, (k + 1) % N_CHIPS, k), i, 0))

    in_specs, out_specs, operands = [], [], []
    for q, p, ax, tr in zip(qs, ps, axes, trs):
        cw = q.shape[2]
        in_specs += [slot_spec(k, tr, cw) for k in range(N_CHIPS)]
        in_specs.append(pl.BlockSpec((None, tr, cw), (lambda i, p_ref: (0, i, p_ref[0])) if ax == 1
                                     else (lambda i, p_ref: (p_ref[0], i, 0))))
        out_specs.append(pl.BlockSpec((tr, cw), lambda i, p_ref: (p_ref[1] * steps + i, 0)))
        operands += [q] * N_CHIPS + [p]
    return pl.pallas_call(
        body,
        grid_spec=pltpu.PrefetchScalarGridSpec(
            num_scalar_prefetch=1, grid=(steps,), in_specs=in_specs, out_specs=out_specs),
        out_shape=[jax.ShapeDtypeStruct((2 * q.shape[1], q.shape[2]), F32) for q in qs],
        compiler_params=_cp("parallel"), name=name)(place, *operands)


def _sibling_share(shards, *, name):
    n = len(shards)

    def body(*refs):
        o_refs = refs[n:2 * n]
        send_sems, recv_sems = refs[2 * n:]
        x, y, c, _ = _place()
        cps = []
        for i in range(n):
            rh = shards[i].shape[0] // 2
            mine = o_refs[i].at[pl.ds(pl.multiple_of(c * rh, 8), rh), :]
            cp = pltpu.make_async_remote_copy(
                src_ref=mine, dst_ref=mine, send_sem=send_sems.at[i], recv_sem=recv_sems.at[i],
                device_id=(x, y, 1 - c), device_id_type=MESH)
            cp.start()
            cps.append(cp)
        for i in range(n):
            rh = shards[i].shape[0] // 2
            theirs = o_refs[i].at[pl.ds(pl.multiple_of((1 - c) * rh, 8), rh), :]
            pltpu.make_async_remote_copy(
                src_ref=theirs, dst_ref=theirs, send_sem=send_sems.at[i], recv_sem=recv_sems.at[i],
                device_id=(x, y, c), device_id_type=MESH).wait_recv()
        for cp in cps:
            cp.wait_send()

    return pl.pallas_call(
        body, in_specs=[ANY] * n, out_specs=[ANY] * n,
        out_shape=[jax.ShapeDtypeStruct(h.shape, h.dtype) for h in shards],
        input_output_aliases={i: i for i in range(n)},
        scratch_shapes=[pltpu.SemaphoreType.DMA((n,)), pltpu.SemaphoreType.DMA((n,))],
        name=name)(*shards)


def _gather_all(blk, *, name, after=None, shares=()):
    rows, cols = blk.shape
    extra = [] if after is None else [after]
    n_s = len(shares)

    def body(x_ref, *refs):
        s_refs = refs[len(extra) + n_s + 1:len(extra) + 2 * n_s + 1]
        out_ref = refs[len(extra) + n_s]
        send_sems, recv_sems, local_sem, s_send, s_recv = refs[len(extra) + 2 * n_s + 1:]
        x, y, c = lax.axis_index("x"), lax.axis_index("y"), lax.axis_index("c")
        me = 4 * x + 2 * y + c
        mine = pltpu.make_async_copy(x_ref, out_ref.at[me], local_sem)
        mine.start()
        cps = []
        for i in range(n_s):
            rh = shares[i].shape[0] // 2
            half = s_refs[i].at[pl.ds(pl.multiple_of(c * rh, 8), rh), :]
            cp = pltpu.make_async_remote_copy(
                src_ref=half, dst_ref=half, send_sem=s_send.at[i], recv_sem=s_recv.at[i],
                device_id=(x, y, 1 - c), device_id_type=MESH)
            cp.start()
            cps.append(cp)
        for k in range(1, N_DEV):
            tx = (1 - x) if (k >> 2) & 1 else x
            ty = (1 - y) if (k >> 1) & 1 else y
            tc = (1 - c) if k & 1 else c
            cp = pltpu.make_async_remote_copy(
                src_ref=x_ref, dst_ref=out_ref.at[me], send_sem=send_sems.at[k - 1],
                recv_sem=recv_sems.at[k - 1], device_id=(tx, ty, tc), device_id_type=MESH)
            cp.start()
            cps.append(cp)
        for k in range(1, N_DEV):
            tx = (1 - x) if (k >> 2) & 1 else x
            ty = (1 - y) if (k >> 1) & 1 else y
            tc = (1 - c) if k & 1 else c
            got = out_ref.at[4 * tx + 2 * ty + tc]
            pltpu.make_async_remote_copy(
                src_ref=got, dst_ref=got, send_sem=send_sems.at[k - 1], recv_sem=recv_sems.at[k - 1],
                device_id=(x, y, c), device_id_type=MESH).wait_recv()
        for i in range(n_s):
            rh = shares[i].shape[0] // 2
            theirs = s_refs[i].at[pl.ds(pl.multiple_of((1 - c) * rh, 8), rh), :]
            pltpu.make_async_remote_copy(
                src_ref=theirs, dst_ref=theirs, send_sem=s_send.at[i], recv_sem=s_recv.at[i],
                device_id=(x, y, c), device_id_type=MESH).wait_recv()
        for cp in cps:
            cp.wait_send()
        mine.wait()

    vm = pl.BlockSpec(memory_space=pltpu.VMEM)
    out = pl.pallas_call(
        body, in_specs=[vm] + [ANY] * (len(extra) + n_s), out_specs=[vm] + [ANY] * n_s,
        out_shape=[jax.ShapeDtypeStruct((N_DEV, rows, cols), blk.dtype)]
        + [jax.ShapeDtypeStruct(s.shape, s.dtype) for s in shares],
        input_output_aliases={1 + len(extra) + i: 1 + i for i in range(n_s)},
        scratch_shapes=[pltpu.SemaphoreType.DMA((N_DEV - 1,)), pltpu.SemaphoreType.DMA((N_DEV - 1,)),
                        pltpu.SemaphoreType.DMA, pltpu.SemaphoreType.DMA((max(n_s, 1),)),
                        pltpu.SemaphoreType.DMA((max(n_s, 1),))],
        name=name)(blk, *extra, *shares)
    return (out[0], *out[1:]) if n_s else out[0]


def _as_rows(a):
    flat = a.reshape(-1)
    n = flat.shape[0]
    rows = -(-n // (8 * LANES)) * 8
    return jnp.pad(flat, (0, rows * LANES - n)).reshape(rows, LANES)


def _from_rows(p, shape):
    n = int(np.prod(shape))
    return p.reshape(-1)[:n].reshape(shape)


WEIGHT_AXES = (1, 1, 1, 0, 1, 0)
WIRE = BF16


def kernel(x, meta_tokens, w_in, w_na_out, w_hg_out, w_o, w_up, w_down, norm_mix, norm_mlp, norm_final, hg_norm, na_rpb, hg_lb_logits, loss_target, m_meta_tokens, m_w_in, m_w_na_out, m_w_hg_out, m_w_o, m_w_up, m_w_down, m_norm_mix, m_norm_mlp, m_norm_final, m_hg_norm, m_na_rpb, m_hg_lb_logits, v_meta_tokens, v_w_in, v_w_na_out, v_w_hg_out, v_w_o, v_w_up, v_w_down, v_norm_mix, v_norm_mlp, v_norm_final, v_hg_norm, v_na_rpb, v_hg_lb_logits):
    xi, yi, ci = lax.axis_index("x"), lax.axis_index("y"), lax.axis_index("c")
    chip = 2 * xi + yi
    d = x.shape[-1]
    dshard = meta_tokens.shape[1]
    hgw = hg_norm.shape[1]
    lbs = hg_lb_logits.shape[2]
    big = [w_in[0], w_na_out[0], w_hg_out[0], w_o[0], w_up[0], w_down[0]]
    big_m = [m_w_in[0], m_w_na_out[0], m_w_hg_out[0], m_w_o[0], m_w_up[0], m_w_down[0]]
    big_v = [v_w_in[0], v_w_na_out[0], v_w_hg_out[0], v_w_o[0], v_w_up[0], v_w_down[0]]

    place = jnp.stack([chip, ci]).astype(jnp.int32)
    own_w = _cast_into_full(big, WEIGHT_AXES, place, name="cast_shards")
    in_axes, rest_axes = WEIGHT_AXES[:1], WEIGHT_AXES[1:]
    small_in = jnp.concatenate([_as_rows(meta_tokens), _as_rows(hg_lb_logits)], axis=0)
    sm_send, sm_recv, sm_blk, sm_land, sm_token = _chip_exchange_start(small_in,
                                                                       name="small_params_start")
    in_send, in_recv, in_bufs, in_token = _allgather_start(own_w[:1], in_axes, sm_token,
                                                           name="weight_allgather_in_start")
    ag_send, ag_recv, ag_bufs, ag_token = _allgather_start(own_w[1:], rest_axes, in_token,
                                                           name="weight_allgather_rest_start")
    sm_land = _chip_exchange_wait(sm_send, sm_recv, sm_blk, sm_land, ag_token,
                                  name="small_params_wait")
    small_all = lax.dynamic_update_slice(sm_land, small_in[None], (chip, 0, 0))
    forward = {}

    def first_weight(after):
        got = _allgather_wait(in_send, in_recv, in_bufs, in_axes, after,
                              name="weight_allgather_in_wait")
        return _allgather_forward(got, in_axes, name="weight_allgather_in_forward")[0]

    def rest_landed(after):
        got = _allgather_wait(ag_send, ag_recv, ag_bufs, rest_axes, after,
                              name="weight_allgather_rest_wait")
        send, recv, bufs, token = _allgather_forward_start(
            got, rest_axes, name="weight_allgather_rest_forward_start")
        forward["rest"] = (send, recv, bufs)
        return token

    def rest_weights(after):
        return _allgather_forward_wait(*forward["rest"], rest_axes, after,
                                       name="weight_allgather_rest_forward_wait")

    n_meta_rows = N_META * dshard // LANES
    meta_full = (small_all[:, :n_meta_rows].reshape(N_CHIPS, N_META, dshard)
                 .transpose(1, 0, 2).reshape(N_META, d))
    lbl_full = (small_all[:, n_meta_rows:].reshape(N_CHIPS, -1)[:, :4 * lbs]
                .reshape(N_CHIPS, 2, 2, lbs).transpose(1, 2, 0, 3).reshape(2, 2, N_CHIPS * lbs))
    lb = jax.nn.softmax(lbl_full, axis=1)[:, 0]

    def by_chip(dws, axes):
        return [g.reshape(1, *g.shape) if ax == 1
                else g.reshape(N_CHIPS, g.shape[0] // N_CHIPS, g.shape[1]) for g, ax in zip(dws, axes)]

    flying = {}

    def scatter(tag, axes, g3, rx):
        parts = _pair_add(g3, rx, place, out_dtype=WIRE, name=f"grad_pair_add_{tag}")
        send, recv, parts, slots, token = _scatter_start(parts, axes,
                                                         name=f"grad_scatter_{tag}_start")
        flying[tag] = (send, recv, parts, slots)
        return token

    def swap(tag, axes):
        def start(dws):
            send, recv, g3, lands, token = _sibling_swap_start(
                by_chip(dws, axes), name=f"grad_sibling_swap_{tag}_start")
            flying["swap_" + tag] = (send, recv, g3, lands)
            return token

        def finish(after):
            g3, rx = _sibling_swap_wait(*flying["swap_" + tag], after,
                                        name=f"grad_sibling_swap_{tag}_wait")
            return scatter(tag, axes, g3, rx)
        return start, finish

    swap_rest, scatter_rest = swap("rest", rest_axes)
    swap_in, scatter_in = swap("in", in_axes)

    def landed(tag, axes, after):
        return _scatter_wait(*flying[tag], axes, after, name=f"grad_scatter_{tag}_wait")

    (loss, dx, dmeta, *_, dg_mix, dg_mlp, dg_fin, d_gain, d_rpb, d_lb) = _local_step(
        x[0], loss_target[0], meta_full, first_weight, rest_weights, norm_mix, norm_mlp,
        norm_final.reshape(1, d), hg_norm, na_rpb[0], lb, swap_rest, scatter_rest,
        lambda dw_in: swap_in([dw_in]), rest_landed, scatter_in)

    parts_rest, slots_rest = landed("rest", rest_axes, dx)
    g_rest = _sibling_share(_sum_chips(slots_rest, parts_rest, place, rest_axes,
                                       name="grad_sum_chips_rest"),
                            name="grad_sibling_share_rest")
    out_rest = _adamw(big[1:], g_rest, big_m[1:], big_v[1:], name="adamw_rest")
    parts_in, slots_in = landed("in", in_axes, out_rest[-1][0])
    half_in = _sum_chips(slots_in, parts_in, place, in_axes, name="grad_sum_chips_in")

    d_rpb_c = d_rpb[:, :2 * NA_WIN_W - 1]
    small_g = [dmeta, dg_mix, dg_mlp, dg_fin, d_gain, d_rpb_c, d_lb, loss]
    packed = jnp.concatenate([_as_rows(a) for a in small_g], axis=0)
    gathered, g_in = _gather_all(packed, shares=half_in, name="gather_small_grads")
    g_big = [g_in] + list(g_rest)
    total = _sum_slots(gathered, name="sum_small_grads")
    offs = np.cumsum([0] + [_as_rows(a).shape[0] for a in small_g])
    take = lambda i, shape: _from_rows(total[offs[i]:offs[i + 1]], shape)
    g_meta_full = take(0, (N_META, d))
    g_norm_mix, g_norm_mlp = take(1, (1, d)), take(2, (1, d))
    g_norm_final = take(3, (d,))
    g_hg_norm = take(4, (1, hgw))
    g_rpb = take(5, na_rpb.shape)
    g_lb = take(6, (2, hgw))
    loss_total = take(7, (1, LANES))[0, 0]
    g_meta = lax.dynamic_slice_in_dim(g_meta_full, chip * dshard, dshard, axis=1)
    dl0 = lb * (1.0 - lb) * g_lb
    g_lbl_full = jnp.stack([dl0, -dl0], axis=1)
    g_lbl = lax.dynamic_slice_in_dim(g_lbl_full, chip * lbs, lbs, axis=2)

    big_out = _adamw(big[:1], [g_in], big_m[:1], big_v[:1], name="adamw_in") + out_rest
    small_w = [meta_tokens, norm_mix, norm_mlp, norm_final, hg_norm, na_rpb, hg_lb_logits]
    small_gr = [g_meta, g_norm_mix, g_norm_mlp, g_norm_final, g_hg_norm, g_rpb, g_lbl]
    small_m = [m_meta_tokens, m_norm_mix, m_norm_mlp, m_norm_final, m_hg_norm, m_na_rpb, m_hg_lb_logits]
    small_v = [v_meta_tokens, v_norm_mix, v_norm_mlp, v_norm_final, v_hg_norm, v_na_rpb, v_hg_lb_logits]
    pk = lambda lst: jnp.concatenate([_as_rows(a) for a in lst], axis=0)
    ((sd, sm, sv),) = _adamw([pk(small_w)], [pk(small_gr)], [pk(small_m)], [pk(small_v)],
                             name="adamw_small")
    soffs = np.cumsum([0] + [_as_rows(a).shape[0] for a in small_w])
    unpk = lambda p: [_from_rows(p[soffs[i]:soffs[i + 1]], small_w[i].shape) for i in range(len(small_w))]
    sd, sm, sv = unpk(sd), unpk(sm), unpk(sv)

    def order(bigs, smalls):
        return [smalls[0]] + [b.reshape(1, *b.shape) for b in bigs] + smalls[1:]

    grads = order(g_big, small_gr)
    deltas = order([o[0] for o in big_out], sd)
    new_m = order([o[1] for o in big_out], sm)
    new_v = order([o[2] for o in big_out], sv)
    return (loss_total, dx.reshape(1, *dx.shape), *grads, *deltas, *new_m, *new_v)
```

```python
import functools

import numpy as np
import jax
import jax.numpy as jnp
from jax import lax
from jax.experimental import pallas as pl
from jax.experimental.pallas import tpu as pltpu

F32 = jnp.float32
BF16 = jnp.bfloat16
HIGHEST = lax.Precision.HIGHEST

GRID_W = 64
N_META = 16
EPS = 1e-6
NA_HEAD_DIM = 64
NA_WIN_H = 8
NA_WIN_W = 16
HG_DK = 128
HG_CHUNK = 16
LANES = 128
ROW_ALIGN = 128
VMEM_LIMIT = 48 * 1024 * 1024

ADAM_LR = 0.001
ADAM_B1 = 0.9
ADAM_B2 = 0.999
ADAM_EPS = 1e-08
ADAM_WD = 0.01
ADAM_STEP = 10

MESH = pl.DeviceIdType.MESH


def _cp(*sem):
    return pltpu.CompilerParams(dimension_semantics=sem, vmem_limit_bytes=VMEM_LIMIT)


def _sigmoid(x):
    return 0.5 * jnp.tanh(0.5 * x) + 0.5


def _dot(a, b, dims, precision=None):
    return lax.dot_general(a, b, (dims, ((), ())), preferred_element_type=F32, precision=precision)


def _nn(a, b, **kw):
    return _dot(a, b, ((1,), (0,)), **kw)


def _nt(a, b, **kw):
    return _dot(a, b, ((1,), (1,)), **kw)


def _tn(a, b, **kw):
    return _dot(a, b, ((0,), (0,)), **kw)


def _matmul(a, b, *, ta=False, tb=False, tm=None, tn=None, tk=None, out_dtype=F32, name,
            precision=None, after=None, epilogue=None, tiles=(), out_dtypes=None):
    extra = [] if after is None else [after]
    single = out_dtypes is None
    if single:
        out_dtypes = (out_dtype,)
    n_t, n_o = len(tiles), len(out_dtypes)
    if ta:
        kdim, m = a.shape
    else:
        m, kdim = a.shape
    if tb:
        n, k2 = b.shape
    else:
        k2, n = b.shape
    assert kdim == k2, (a.shape, b.shape, ta, tb)
    if tm is None:
        if ta:
            tm = next(t for t in (1024, 512, 256, 128, m) if m % t == 0)
        else:
            tm = m // 2 if (m // 2) % 16 == 0 and m > 512 else m
    if tn is None:
        wide = (1024,) if not ta and len(tiles) <= 1 else ()
        tn = next(t for t in (*wide, 512, 256, 128, n) if n % t == 0)
    if tk is None:
        tk = kdim if ta else next(t for t in (1024, 512, 256, 128, kdim) if kdim % t == 0)
    assert m % tm == 0 and n % tn == 0 and kdim % tk == 0, (m, n, kdim, tm, tn, tk)
    nk = kdim // tk
    op_dtype = F32 if precision is not None else BF16

    def body(a_ref, b_ref, *refs):
        t_refs = refs[:n_t]
        o_refs = refs[n_t + len(extra):n_t + len(extra) + n_o]
        av = a_ref[...].astype(op_dtype)
        bv = b_ref[...].astype(op_dtype)
        dims = ((0 if ta else 1,), (1 if tb else 0,))
        part = _dot(av, bv, dims, precision=precision)

        def finish(acc):
            outs = (acc,) if epilogue is None else epilogue(acc, *[t[...] for t in t_refs])
            for o_ref, val in zip(o_refs, outs):
                o_ref[...] = val.astype(o_ref.dtype)

        if nk == 1:
            finish(part)
            return
        acc_ref = refs[-1]
        kk = pl.program_id(2)

        @pl.when(kk == 0)
        def _():
            acc_ref[...] = part

        @pl.when((kk > 0) & (kk < nk - 1))
        def _():
            acc_ref[...] += part

        @pl.when(kk == nk - 1)
        def _():
            finish(acc_ref[...] + part)

    a_spec = (pl.BlockSpec((tk, tm), lambda i, j, k: (k, i)) if ta
              else pl.BlockSpec((tm, tk), lambda i, j, k: (i, k)))
    b_spec = (pl.BlockSpec((tn, tk), lambda i, j, k: (j, k)) if tb
              else pl.BlockSpec((tk, tn), lambda i, j, k: (k, j)))
    for _, off in tiles:
        assert off % tn == 0, (off, tn)
    t_specs = [pl.BlockSpec((tm, tn), functools.partial(lambda i, j, k, o: (i, o + j), o=off // tn))
               for _, off in tiles]
    o_spec = pl.BlockSpec((tm, tn), lambda i, j, k: (i, j))
    outs = pl.pallas_call(
        body,
        grid=(m // tm, n // tn, nk),
        in_specs=[a_spec, b_spec] + t_specs + [pl.BlockSpec(memory_space=pl.ANY)] * len(extra),
        out_specs=[o_spec] * n_o,
        out_shape=[jax.ShapeDtypeStruct((m, n), dt) for dt in out_dtypes],
        scratch_shapes=[pltpu.VMEM((tm, tn), F32)] if nk > 1 else [],
        compiler_params=_cp("parallel", "parallel", "arbitrary"),
        name=name,
    )(a, b, *[t for t, _ in tiles], *extra)
    return outs[0] if single else outs


def _rspec(tr, w, cb=0):
    return pl.BlockSpec((tr, w), lambda i: (i, cb))


def _fspec(shape):
    nd = len(shape)
    return pl.BlockSpec(shape, lambda i: (0,) * nd)


ROW_VMEM_BUDGET = 20 * 1024 * 1024
ROW_MIN_STEPS = 4


def _row_tile(lp, row_bytes):
    for k in range(ROW_MIN_STEPS, lp // 16 + 1):
        tr = lp // k
        if lp % k == 0 and tr % 16 == 0 and 2 * tr * row_bytes <= ROW_VMEM_BUDGET:
            return tr
    return lp


def _token_rows_copy(i, n_tiles, tr, n_tok, tok_ref, buf_ref, sem, *, to_tokens, start=True,
                     wait=True):
    assert n_tiles >= 2 and 0 < n_tok + N_META - (n_tiles - 1) * tr <= tr

    def run(tok_row, buf_row, count):
        tok = tok_ref.at[pl.ds(tok_row, count), :]
        buf = buf_ref.at[pl.ds(buf_row, count), :]
        cp = pltpu.make_async_copy(buf, tok, sem) if to_tokens else pltpu.make_async_copy(tok, buf, sem)
        if start:
            cp.start()
        if wait:
            cp.wait()

    @pl.when(i == 0)
    def _():
        run(0, N_META, tr - N_META)

    if n_tiles > 2:
        @pl.when((i > 0) & (i < n_tiles - 1))
        def _():
            run(pl.multiple_of(i * tr - N_META, 8), 0, tr)

    @pl.when(i == n_tiles - 1)
    def _():
        run((n_tiles - 1) * tr - N_META, 0, n_tok + N_META - (n_tiles - 1) * tr)


def _embed_norm(x, meta, g, *, lp, name):
    n_tok, d = x.shape
    tr = _row_tile(lp, d * (4 + 2))
    n_tiles = lp // tr

    def body(x_ref, meta_ref, g_ref, h_ref, o_ref, buf_ref, sem):
        i = pl.program_id(0)
        buf_ref[...] = jnp.zeros_like(buf_ref)

        @pl.when(i == 0)
        def _():
            buf_ref[0:N_META, :] = meta_ref[...]

        _token_rows_copy(i, n_tiles, tr, n_tok, x_ref, buf_ref, sem, to_tokens=False)
        xv = buf_ref[...]
        h_ref[...] = xv
        r = lax.rsqrt(jnp.mean(xv * xv, axis=-1, keepdims=True) + EPS)
        o_ref[...] = (xv * r * g_ref[...]).astype(BF16)

    return pl.pallas_call(
        body, grid=(n_tiles,),
        in_specs=[ANY, _fspec((N_META, d)), _fspec((1, d))],
        out_specs=[_rspec(tr, d), _rspec(tr, d)],
        out_shape=[jax.ShapeDtypeStruct((lp, d), F32), jax.ShapeDtypeStruct((lp, d), BF16)],
        scratch_shapes=[pltpu.VMEM((tr, d), F32), pltpu.SemaphoreType.DMA],
        compiler_params=_cp("parallel"), name=name)(x, meta, g)


def _residual_norm(h, t, g, *, name):
    lp, d = h.shape
    tr = _row_tile(lp, d * (4 + 4 + 4 + 2))

    def body(h_ref, t_ref, g_ref, h1_ref, m_ref):
        xv = h_ref[...] + t_ref[...]
        h1_ref[...] = xv
        r = lax.rsqrt(jnp.mean(xv * xv, axis=-1, keepdims=True) + EPS)
        m_ref[...] = (xv * r * g_ref[...]).astype(BF16)

    return pl.pallas_call(
        body, grid=(lp // tr,),
        in_specs=[_rspec(tr, d), _rspec(tr, d), _fspec((1, d))],
        out_specs=[_rspec(tr, d), _rspec(tr, d)],
        out_shape=[jax.ShapeDtypeStruct((lp, d), F32), jax.ShapeDtypeStruct((lp, d), BF16)],
        compiler_params=_cp("parallel"), name=name)(h, t, g)


def _rmsnorm_bwd_add(x, g, dy, dres, *, name):
    lp, d = x.shape
    tr = _row_tile(lp, d * 4 * 4)

    def body(x_ref, g_ref, dy_ref, dr_ref, dx_ref, dg_ref):
        @pl.when(pl.program_id(0) == 0)
        def _():
            dg_ref[...] = jnp.zeros_like(dg_ref)

        xv = x_ref[...]
        r = lax.rsqrt(jnp.mean(xv * xv, axis=-1, keepdims=True) + EPS)
        xh = xv * r
        dyv = dy_ref[...]
        dg_ref[...] += jnp.sum(dyv * xh, axis=0, keepdims=True)
        dxh = dyv * g_ref[...]
        dx_ref[...] = dr_ref[...] + r * (dxh - xh * jnp.mean(dxh * xh, axis=-1, keepdims=True))

    return pl.pallas_call(
        body, grid=(lp // tr,),
        in_specs=[_rspec(tr, d), _fspec((1, d)), _rspec(tr, d), _rspec(tr, d)],
        out_specs=[_rspec(tr, d), _fspec((1, d))],
        out_shape=[jax.ShapeDtypeStruct((lp, d), F32), jax.ShapeDtypeStruct((1, d), F32)],
        compiler_params=_cp("arbitrary"), name=name)(x, g, dy, dres)


def _rmsnorm_bwd_tokens(x, g, dy, dres, *, n_tok, name):
    lp, d = x.shape
    tr = _row_tile(lp, d * 4 * 4)
    n_tiles = lp // tr

    def body(x_ref, g_ref, dy_ref, dr_ref, dtok_ref, dmeta_ref, dg_ref, buf_ref, sem):
        i = pl.program_id(0)

        @pl.when(i == 0)
        def _():
            dg_ref[...] = jnp.zeros_like(dg_ref)

        xv = x_ref[...]
        r = lax.rsqrt(jnp.mean(xv * xv, axis=-1, keepdims=True) + EPS)
        xh = xv * r
        dyv = dy_ref[...]
        dg_ref[...] += jnp.sum(dyv * xh, axis=0, keepdims=True)
        dxh = dyv * g_ref[...]
        buf_ref[...] = dr_ref[...] + r * (dxh - xh * jnp.mean(dxh * xh, axis=-1, keepdims=True))

        @pl.when(i == 0)
        def _():
            dmeta_ref[...] = buf_ref[0:N_META, :]

        _token_rows_copy(i, n_tiles, tr, n_tok, dtok_ref, buf_ref, sem, to_tokens=True)

    return pl.pallas_call(
        body, grid=(n_tiles,),
        in_specs=[_rspec(tr, d), _fspec((1, d)), _rspec(tr, d), _rspec(tr, d)],
        out_specs=[ANY, _fspec((N_META, d)), _fspec((1, d))],
        out_shape=[jax.ShapeDtypeStruct((n_tok, d), F32), jax.ShapeDtypeStruct((N_META, d), F32),
                   jax.ShapeDtypeStruct((1, d), F32)],
        scratch_shapes=[pltpu.VMEM((tr, d), F32), pltpu.SemaphoreType.DMA],
        compiler_params=_cp("arbitrary"), name=name)(x, g, dy, dres)


def _final_loss(h1, t2, g, tgt, *, name):
    lp, d = h1.shape
    n_tok = tgt.shape[0]
    tr = _row_tile(lp, d * 4 * 4)
    n_tiles = lp // tr

    def body(h_ref, t_ref, g_ref, tgt_ref, dh_ref, loss_ref, dg_ref, tg_ref, sem):
        i = pl.program_id(0)

        @pl.when(i == 0)
        def _():
            loss_ref[...] = jnp.zeros_like(loss_ref)
            dg_ref[...] = jnp.zeros_like(dg_ref)
            tg_ref[...] = jnp.zeros_like(tg_ref)

        _token_rows_copy(i, n_tiles, tr, n_tok, tgt_ref, tg_ref, sem, to_tokens=False, wait=False)
        xv = h_ref[...] + t_ref[...]
        r = lax.rsqrt(jnp.mean(xv * xv, axis=-1, keepdims=True) + EPS)
        xh = xv * r
        gv = g_ref[...]
        _token_rows_copy(i, n_tiles, tr, n_tok, tgt_ref, tg_ref, sem, to_tokens=False, start=False)
        row = i * tr + lax.broadcasted_iota(jnp.int32, (tr, 1), 0)
        valid = (row >= N_META) & (row < N_META + n_tok)
        err = jnp.where(valid, xh * gv - tg_ref[...], 0.0)
        loss_ref[...] += jnp.sum(0.5 * err * err) / d
        dy = err / d
        dg_ref[...] += jnp.sum(dy * xh, axis=0, keepdims=True)
        dxh = dy * gv
        dh_ref[...] = r * (dxh - xh * jnp.mean(dxh * xh, axis=-1, keepdims=True))

    return pl.pallas_call(
        body, grid=(n_tiles,),
        in_specs=[_rspec(tr, d), _rspec(tr, d), _fspec((1, d)), ANY],
        out_specs=[_rspec(tr, d), _fspec((1, LANES)), _fspec((1, d))],
        out_shape=[jax.ShapeDtypeStruct((lp, d), F32), jax.ShapeDtypeStruct((1, LANES), F32),
                   jax.ShapeDtypeStruct((1, d), F32)],
        scratch_shapes=[pltpu.VMEM((tr, d), F32), pltpu.SemaphoreType.DMA],
        compiler_params=_cp("arbitrary"), name=name)(h1, t2, g, tgt)


def _hg_out(o_f, o_b, proj, gain, *, col_g, name):
    lp, w = o_f.shape
    tr = _row_tile(lp, w * (3 * 4 + 2))
    hh = w // HG_DK

    def body(of_ref, ob_ref, g_ref, gain_ref, y_ref):
        gv = g_ref[...]
        sg = gv * _sigmoid(gv)
        for h in range(hh):
            sl = slice(h * HG_DK, (h + 1) * HG_DK)
            o = of_ref[:, sl] + ob_ref[:, sl]
            r = lax.rsqrt(jnp.mean(o * o, axis=-1, keepdims=True) + EPS)
            y_ref[:, sl] = (o * r * gain_ref[:, sl] * sg[:, sl]).astype(BF16)

    return pl.pallas_call(
        body, grid=(lp // tr,),
        in_specs=[_rspec(tr, w), _rspec(tr, w), _rspec(tr, w, col_g // w), _fspec((1, w))],
        out_specs=_rspec(tr, w),
        out_shape=jax.ShapeDtypeStruct((lp, w), BF16),
        compiler_params=_cp("parallel"), name=name)(o_f, o_b, proj, gain)


def _hg_out_bwd(o_f, o_b, proj, gain, dy, *, col_g, name):
    lp, w = o_f.shape
    tr = _row_tile(lp, w * (5 * 4 + 2))
    hh = w // HG_DK

    def body(of_ref, ob_ref, g_ref, gain_ref, dy_ref, do_ref, dg_ref, dgain_ref):
        @pl.when(pl.program_id(0) == 0)
        def _():
            dgain_ref[...] = jnp.zeros_like(dgain_ref)

        for h in range(hh):
            sl = slice(h * HG_DK, (h + 1) * HG_DK)
            gv = g_ref[:, sl]
            s = _sigmoid(gv)
            sg = gv * s
            dsg = s + gv * s * (1.0 - s)
            o = of_ref[:, sl] + ob_ref[:, sl]
            r = lax.rsqrt(jnp.mean(o * o, axis=-1, keepdims=True) + EPS)
            on = o * r
            dyv = dy_ref[:, sl]
            gn = gain_ref[:, sl]
            dgain_ref[:, sl] += jnp.sum(dyv * on * sg, axis=0, keepdims=True)
            dg_ref[:, sl] = (dyv * on * gn * dsg).astype(BF16)
            don = dyv * gn * sg
            do_ref[:, sl] = r * (don - on * jnp.mean(don * on, axis=-1, keepdims=True))

    return pl.pallas_call(
        body, grid=(lp // tr,),
        in_specs=[_rspec(tr, w), _rspec(tr, w), _rspec(tr, w, col_g // w), _fspec((1, w)),
                  _rspec(tr, w)],
        out_specs=[_rspec(tr, w), _rspec(tr, w), _fspec((1, w))],
        out_shape=[jax.ShapeDtypeStruct((lp, w), F32), jax.ShapeDtypeStruct((lp, w), BF16),
                   jax.ShapeDtypeStruct((1, w), F32)],
        compiler_params=_cp("arbitrary"), name=name)(o_f, o_b, proj, gain, dy)


HG_ROWS = 128
HG_HALVES = (1, 2, 4, 8, 16, 32, 64)


def _hg_gates(zq, z, lbv):
    sq = _sigmoid(zq)
    s = _sigmoid(z)
    f = lbv + (1.0 - lbv) * s
    kk = (1.0 - lbv) * (1.0 - s)
    return zq * sq, sq, s, f, jnp.log(f), kk


def _block_cumsum(g, pos, suffix):
    x = g
    for k in HG_HALVES:
        if suffix:
            x = x + jnp.where(pos < HG_ROWS - k, pltpu.roll(x, HG_ROWS - k, 0), 0.0)
        else:
            x = x + jnp.where(pos >= k, pltpu.roll(x, k, 0), 0.0)
    return x


def _pair_levels(b, pos, reverse):
    out = []
    first = b
    for m in HG_HALVES:
        if m > 1:
            first = jnp.where((pos & (m - 1)) >= m // 2, pltpu.roll(first, m // 2, 0), first)
        nxt = pltpu.roll(first, HG_ROWS - m, 0)
        upper = (pos & (2 * m - 1)) >= m
        if reverse:
            eq = jnp.where(upper, 0.0, jnp.exp(b - nxt))
            ek = jnp.where(upper, jnp.exp(first - b), 0.0)
        else:
            eq = jnp.where(upper, jnp.exp(b - first), 0.0)
            ek = jnp.where(upper, 0.0, jnp.exp(nxt - b))
        out.append((eq, ek))
    return out


def _pair_masks(mask_ref):
    ri = lax.broadcasted_iota(jnp.int32, (HG_ROWS, HG_ROWS), 0)
    ci = lax.broadcasted_iota(jnp.int32, (HG_ROWS, HG_ROWS), 1)
    for i, m in enumerate(HG_HALVES):
        sh = m.bit_length()
        mask_ref[i] = jnp.where((ri >> sh) == (ci >> sh), 1.0, 0.0)


def _hg_scan_fwd(proj, lb, *, reverse, col_q, col_z, col_i, hh, name):
    lp = proj.shape[0]
    n_blocks = lp // HG_ROWS
    last = 0 if reverse else HG_ROWS - 1

    def body(q_ref, z_ref, i_ref, lb_ref, o_ref, st_ref, mask_ref):
        lbv = lb_ref[...]
        pos = lax.broadcasted_iota(jnp.int32, (HG_ROWS, 1), 0)
        ri = lax.broadcasted_iota(jnp.int32, (HG_ROWS, HG_ROWS), 0)
        ci = lax.broadcasted_iota(jnp.int32, (HG_ROWS, HG_ROWS), 1)
        _pair_masks(mask_ref)

        def block(bi, st):
            bb = (n_blocks - 1 - bi) if reverse else bi
            r0 = pl.multiple_of(bb * HG_ROWS, HG_ROWS)
            v16 = i_ref[pl.ds(r0, HG_ROWS), :].astype(BF16)
            qh, _, _, _, g, kk = _hg_gates(q_ref[pl.ds(r0, HG_ROWS), :],
                                           z_ref[pl.ds(r0, HG_ROWS), :], lbv)
            b = _block_cumsum(g, pos, reverse)
            bl = b[last:last + 1, :]
            qe = (qh * jnp.exp(b)).astype(BF16)
            kd = (kk * jnp.exp(bl - b)).astype(BF16)
            a = jnp.where(ri == ci, jnp.sum(qh * kk, axis=1, keepdims=True), 0.0)
            for i, (eq, ek) in enumerate(_pair_levels(b, pos, reverse)):
                a = a + mask_ref[i] * _nt((qh * eq).astype(BF16), (kk * ek).astype(BF16))
            st_ref[bb] = st
            o_ref[pl.ds(r0, HG_ROWS), :] = _nn(a.astype(BF16), v16) + _nt(qe, st.astype(BF16))
            return jnp.exp(bl) * st + _tn(v16, kd)

        lax.fori_loop(0, n_blocks, block, jnp.zeros((HG_DK, HG_DK), F32))

    cspec = lambda col: pl.BlockSpec((lp, HG_DK), lambda h: (0, col // HG_DK + h))
    return pl.pallas_call(
        body, grid=(hh,),
        in_specs=[cspec(col_q), cspec(col_z), cspec(col_i),
                  pl.BlockSpec((None, 1, HG_DK), lambda h: (h, 0, 0))],
        out_specs=[pl.BlockSpec((lp, HG_DK), lambda h: (0, h)),
                   pl.BlockSpec((None, n_blocks, HG_DK, HG_DK), lambda h: (h, 0, 0, 0))],
        out_shape=[jax.ShapeDtypeStruct((lp, hh * HG_DK), F32),
                   jax.ShapeDtypeStruct((hh, n_blocks, HG_DK, HG_DK), F32)],
        scratch_shapes=[pltpu.VMEM((len(HG_HALVES), HG_ROWS, HG_ROWS), F32)],
        compiler_params=_cp("parallel"), name=name)(proj, proj, proj, lb)


def _hg_scan_bwd(proj, lb, states, do, *, reverse, col_q, col_z, col_i, hh, name):
    lp = proj.shape[0]
    n_blocks = lp // HG_ROWS
    last = 0 if reverse else HG_ROWS - 1

    def body(q_ref, z_ref, i_ref, lb_ref, st_ref, do_ref, dq_ref, dz_ref, dv_ref, dlb_ref, mask_ref):
        lbv = lb_ref[...]
        pos = lax.broadcasted_iota(jnp.int32, (HG_ROWS, 1), 0)
        ri = lax.broadcasted_iota(jnp.int32, (HG_ROWS, HG_ROWS), 0)
        ci = lax.broadcasted_iota(jnp.int32, (HG_ROWS, HG_ROWS), 1)
        _pair_masks(mask_ref)

        def block(bi, carry):
            dst, dlb = carry
            bb = bi if reverse else (n_blocks - 1 - bi)
            r0 = pl.multiple_of(bb * HG_ROWS, HG_ROWS)
            zq = q_ref[pl.ds(r0, HG_ROWS), :]
            v16 = i_ref[pl.ds(r0, HG_ROWS), :].astype(BF16)
            do16 = do_ref[pl.ds(r0, HG_ROWS), :].astype(BF16)
            qh, sq, s, f, g, kk = _hg_gates(zq, z_ref[pl.ds(r0, HG_ROWS), :], lbv)
            b = _block_cumsum(g, pos, reverse)
            bl = b[last:last + 1, :]
            eb = jnp.exp(b)
            ebl = jnp.exp(bl - b)
            decay = jnp.exp(bl)
            qe16 = (qh * eb).astype(BF16)
            kd16 = (kk * ebl).astype(BF16)
            st = st_ref[bb]
            st16, dst16 = st.astype(BF16), dst.astype(BF16)
            same_row = ri == ci
            da = _nt(do16, v16)
            da_diag = jnp.sum(jnp.where(same_row, da, 0.0), axis=1, keepdims=True)
            dq_state = eb * _nn(do16, st16)
            dk_state = ebl * _nn(v16, dst16)
            dq = dq_state + da_diag * kk
            dk = dk_state + da_diag * qh
            dbl = (decay * jnp.sum(st * dst, axis=0, keepdims=True)
                   + jnp.sum(kk * dk_state, axis=0, keepdims=True))
            db = qh * dq_state - kk * dk_state + jnp.where(pos == last, dbl, 0.0)
            a = jnp.where(same_row, jnp.sum(qh * kk, axis=1, keepdims=True), 0.0)
            for i, (eq, ek) in enumerate(_pair_levels(b, pos, reverse)):
                same = mask_ref[i]
                q16, k16 = (qh * eq).astype(BF16), (kk * ek).astype(BF16)
                a = a + same * _nt(q16, k16)
                da16 = (same * da).astype(BF16)
                gq, gk = _nn(da16, k16), _tn(da16, q16)
                dq = dq + eq * gq
                dk = dk + ek * gk
                db = db + (q16.astype(F32) * gq - k16.astype(F32) * gk)
            dg = _block_cumsum(db, pos, not reverse)
            df = dg / f - dk
            dq_ref[pl.ds(r0, HG_ROWS), :] = dq * (sq + zq * sq * (1.0 - sq))
            dz_ref[pl.ds(r0, HG_ROWS), :] = df * (1.0 - lbv) * s * (1.0 - s)
            dv_ref[pl.ds(r0, HG_ROWS), :] = _nt(kd16, dst16) + _tn(a.astype(BF16), do16)
            return (decay * dst + _tn(do16, qe16),
                    dlb + jnp.sum(df * (1.0 - s), axis=0, keepdims=True))

        _, dlb = lax.fori_loop(0, n_blocks, block,
                               (jnp.zeros((HG_DK, HG_DK), F32), jnp.zeros((1, HG_DK), F32)))
        dlb_ref[...] = dlb

    cspec = lambda col: pl.BlockSpec((lp, HG_DK), lambda h: (0, col // HG_DK + h))
    ospec = pl.BlockSpec((lp, HG_DK), lambda h: (0, h))
    sds = jax.ShapeDtypeStruct((lp, hh * HG_DK), F32)
    return pl.pallas_call(
        body, grid=(hh,),
        in_specs=[cspec(col_q), cspec(col_z), cspec(col_i),
                  pl.BlockSpec((None, 1, HG_DK), lambda h: (h, 0, 0)),
                  pl.BlockSpec((None, n_blocks, HG_DK, HG_DK), lambda h: (h, 0, 0, 0)),
                  ospec],
        out_specs=[ospec, ospec, ospec, pl.BlockSpec((None, 1, HG_DK), lambda h: (h, 0, 0))],
        out_shape=[sds, sds, sds, jax.ShapeDtypeStruct((hh, 1, HG_DK), F32)],
        scratch_shapes=[pltpu.VMEM((len(HG_HALVES), HG_ROWS, HG_ROWS), F32)],
        compiler_params=_cp("parallel"), name=name)(proj, proj, proj, lb, states, do)


NA_HB = LANES // NA_HEAD_DIM
NA_G = 4
NA_U = NA_G + NA_WIN_H
NA_QN = NA_G * GRID_W
NA_KN = NA_U * GRID_W


def _na_table_index(pattern, a, j):
    if pattern == 0:
        return j - a + NA_WIN_H - 1 if j < NA_WIN_H else None
    if pattern == 2:
        return j - a - 1 if j >= NA_U - NA_WIN_H else None
    return j - a + NA_WIN_H // 2 - 1 if a <= j < a + NA_WIN_H else None


def _na_step_rows(pattern, t, rows):
    if pattern == 0:
        r0, us = 0, 0
    elif pattern == 2:
        r0, us = rows - NA_G, rows - NA_U
    else:
        r0 = NA_G * t
        us = r0 - NA_WIN_H // 2
    q0, k0 = N_META + GRID_W * r0, N_META + GRID_W * us
    if pattern == 1:
        q0, k0 = pl.multiple_of(q0, 16), pl.multiple_of(k0, 16)
    return q0, k0


def _na_fill_bias(tb_ref, bias_ref):
    neg = jnp.full((GRID_W, GRID_W), -1e30, F32)
    for h in range(NA_HB):
        for pattern in range(3):
            for a in range(NA_G):
                for j in range(NA_U):
                    idx = _na_table_index(pattern, a, j)
                    bias_ref[h, pattern, a * GRID_W:(a + 1) * GRID_W, j * GRID_W:(j + 1) * GRID_W] = (
                        neg if idx is None else tb_ref[h, idx])


def _na_steps(rows, step, carry):
    n_steps = rows // NA_G
    carry = step(0, 0, carry)
    carry = lax.fori_loop(1, n_steps - 1, functools.partial(step, 1), carry)
    return step(2, n_steps - 1, carry)


def _na_head_lanes():
    lane = lax.broadcasted_iota(jnp.int32, (1, LANES), 1)
    return [lane // NA_HEAD_DIM == h for h in range(NA_HB)]


def _na_only(mask, x):
    return jnp.where(mask, x, jnp.zeros_like(x))


def _na_fwd(proj, tb, *, n_tok, nh, name):
    lp = proj.shape[0]
    dh, hb = NA_HEAD_DIM, NA_HB
    naw = nh * dh
    rows = n_tok // GRID_W
    scale = dh ** -0.5

    def body(q_ref, k_ref, v_ref, tb_ref, o_ref, lse_ref, q16_ref, k16_ref, v16_ref, bias_ref):
        o_ref[...] = jnp.zeros_like(o_ref)
        lse_ref[...] = jnp.zeros_like(lse_ref)
        q16_ref[...] = q_ref[...].astype(BF16)
        k16_ref[...] = k_ref[...].astype(BF16)
        v16_ref[...] = v_ref[...].astype(BF16)
        _na_fill_bias(tb_ref, bias_ref)
        heads = _na_head_lanes()
        km = k16_ref[0:N_META, :]
        vm = v16_ref[0:N_META, :]
        qm = q16_ref[0:N_META, :]
        o_m = None
        for h in range(hb):
            s = _nt(_na_only(heads[h], qm), km) * scale
            m = jnp.max(s, axis=1, keepdims=True)
            p = jnp.exp(s - m)
            l = jnp.sum(p, axis=1, keepdims=True)
            o_h = _nn(p.astype(BF16), vm) / l
            o_m = o_h if o_m is None else jnp.where(heads[h], o_h, o_m)
            lse_ref[h, 0:N_META, :] = m + jnp.log(l)
        o_ref[0:N_META, :] = o_m

        def step(pattern, t, carry):
            q0, k0 = _na_step_rows(pattern, t, rows)
            q16 = q16_ref[pl.ds(q0, NA_QN), :]
            k16 = k16_ref[pl.ds(k0, NA_KN), :]
            v16 = v16_ref[pl.ds(k0, NA_KN), :]
            o = None
            for h in range(hb):
                q_h = _na_only(heads[h], q16)
                s = _nt(q_h, k16) * scale + bias_ref[h, pattern]
                sm = _nt(q_h, km) * scale
                m = jnp.maximum(jnp.max(s, axis=1, keepdims=True),
                                jnp.max(sm, axis=1, keepdims=True))
                p = jnp.exp(s - m)
                pm = jnp.exp(sm - m)
                l = jnp.sum(p, axis=1, keepdims=True) + jnp.sum(pm, axis=1, keepdims=True)
                o_h = (_nn(p.astype(BF16), v16) + _nn(pm.astype(BF16), vm)) / l
                o = o_h if o is None else jnp.where(heads[h], o_h, o)
                lse_ref[h, pl.ds(q0, NA_QN), :] = m + jnp.log(l)
            o_ref[pl.ds(q0, NA_QN), :] = o
            return carry

        _na_steps(rows, step, 0)

    cblk = lambda col: pl.BlockSpec((lp, LANES), lambda g: (0, col // LANES + g))
    return pl.pallas_call(
        body, grid=(nh // hb,),
        in_specs=[cblk(0), cblk(naw), cblk(2 * naw),
                  pl.BlockSpec((hb, 2 * NA_WIN_H - 1, GRID_W, GRID_W), lambda g: (g, 0, 0, 0))],
        out_specs=[cblk(0), pl.BlockSpec((hb, lp, 1), lambda g: (g, 0, 0))],
        out_shape=[jax.ShapeDtypeStruct((lp, naw), F32), jax.ShapeDtypeStruct((nh, lp, 1), F32)],
        scratch_shapes=[pltpu.VMEM((lp, LANES), BF16)] * 3 + [pltpu.VMEM((hb, 3, NA_QN, NA_KN), F32)],
        compiler_params=_cp("parallel"), name=name)(proj, proj, proj, tb)


def _na_bwd(proj, tb, o, lse, do, *, n_tok, nh, name):
    lp = proj.shape[0]
    dh, hb = NA_HEAD_DIM, NA_HB
    naw = nh * dh
    rows = n_tok // GRID_W
    scale = dh ** -0.5

    def body(q_ref, k_ref, v_ref, tb_ref, o_ref, lse_ref, do_ref, dq_ref, dk_ref, dv_ref, dtb_ref,
             q16_ref, k16_ref, v16_ref, bias_ref):
        dq_ref[...] = jnp.zeros_like(dq_ref)
        dk_ref[...] = jnp.zeros_like(dk_ref)
        dv_ref[...] = jnp.zeros_like(dv_ref)
        dtb_ref[...] = jnp.zeros_like(dtb_ref)
        q16_ref[...] = q_ref[...].astype(BF16)
        k16_ref[...] = k_ref[...].astype(BF16)
        v16_ref[...] = v_ref[...].astype(BF16)
        _na_fill_bias(tb_ref, bias_ref)
        heads = _na_head_lanes()
        km = k16_ref[0:N_META, :]
        vm = v16_ref[0:N_META, :]
        qm = q16_ref[0:N_META, :]
        dom = do_ref[0:N_META, :]
        prod = dom * o_ref[0:N_META, :]
        dq_m = None
        dkm0 = jnp.zeros((N_META, LANES), F32)
        dvm0 = jnp.zeros((N_META, LANES), F32)
        for h in range(hb):
            q_h = _na_only(heads[h], qm)
            do_h = _na_only(heads[h], dom).astype(BF16)
            p = jnp.exp(_nt(q_h, km) * scale - lse_ref[h, 0:N_META, :])
            delta = jnp.sum(_na_only(heads[h], prod), axis=1, keepdims=True)
            ds = (p * (_nt(do_h, vm) - delta)).astype(BF16)
            dq_h = _nn(ds, km) * scale
            dq_m = dq_h if dq_m is None else jnp.where(heads[h], dq_h, dq_m)
            dkm0 = dkm0 + _tn(ds, q_h) * scale
            dvm0 = dvm0 + _tn(p.astype(BF16), do_h)
        dq_ref[0:N_META, :] = dq_m

        def step(pattern, t, carry):
            dkm, dvm = carry
            q0, k0 = _na_step_rows(pattern, t, rows)
            q16 = q16_ref[pl.ds(q0, NA_QN), :]
            k16 = k16_ref[pl.ds(k0, NA_KN), :]
            v16 = v16_ref[pl.ds(k0, NA_KN), :]
            dov = do_ref[pl.ds(q0, NA_QN), :]
            prod = dov * o_ref[pl.ds(q0, NA_QN), :]
            dq = None
            dk = jnp.zeros((NA_KN, LANES), F32)
            dv = jnp.zeros((NA_KN, LANES), F32)
            for h in range(hb):
                q_h = _na_only(heads[h], q16)
                do_h = _na_only(heads[h], dov).astype(BF16)
                lse = lse_ref[h, pl.ds(q0, NA_QN), :]
                p = jnp.exp(_nt(q_h, k16) * scale + bias_ref[h, pattern] - lse)
                pm = jnp.exp(_nt(q_h, km) * scale - lse)
                delta = jnp.sum(_na_only(heads[h], prod), axis=1, keepdims=True)
                ds = p * (_nt(do_h, v16) - delta)
                dsm = (pm * (_nt(do_h, vm) - delta)).astype(BF16)
                ds16 = ds.astype(BF16)
                dq_h = (_nn(ds16, k16) + _nn(dsm, km)) * scale
                dq = dq_h if dq is None else jnp.where(heads[h], dq_h, dq)
                dk = dk + _tn(ds16, q_h) * scale
                dv = dv + _tn(p.astype(BF16), do_h)
                dkm = dkm + _tn(dsm, q_h) * scale
                dvm = dvm + _tn(pm.astype(BF16), do_h)
                for a in range(NA_G):
                    for j in range(NA_U):
                        idx = _na_table_index(pattern, a, j)
                        if idx is not None:
                            dtb_ref[h, idx] += ds[a * GRID_W:(a + 1) * GRID_W,
                                                  j * GRID_W:(j + 1) * GRID_W]
            dq_ref[pl.ds(q0, NA_QN), :] = dq
            dk_ref[pl.ds(k0, NA_KN), :] += dk
            dv_ref[pl.ds(k0, NA_KN), :] += dv
            return dkm, dvm

        dkm, dvm = _na_steps(rows, step, (dkm0, dvm0))
        dk_ref[0:N_META, :] += dkm
        dv_ref[0:N_META, :] += dvm

    cblk = lambda col: pl.BlockSpec((lp, LANES), lambda g: (0, col // LANES + g))
    tbs = pl.BlockSpec((hb, 2 * NA_WIN_H - 1, GRID_W, GRID_W), lambda g: (g, 0, 0, 0))
    sds = jax.ShapeDtypeStruct((lp, naw), F32)
    return pl.pallas_call(
        body, grid=(nh // hb,),
        in_specs=[cblk(0), cblk(naw), cblk(2 * naw), tbs, cblk(0),
                  pl.BlockSpec((hb, lp, 1), lambda g: (g, 0, 0)), cblk(0)],
        out_specs=[cblk(0), cblk(0), cblk(0), tbs],
        out_shape=[sds, sds, sds, jax.ShapeDtypeStruct(tb.shape, F32)],
        scratch_shapes=[pltpu.VMEM((lp, LANES), BF16)] * 3 + [pltpu.VMEM((hb, 3, NA_QN, NA_KN), F32)],
        compiler_params=_cp("parallel"), name=name)(proj, proj, proj, tb, o, lse, do)


def _rpb_onehot():
    c = np.arange(GRID_W)[:, None]
    w = np.arange(GRID_W)[None, :]
    cs = np.clip(c - NA_WIN_W // 2, 0, GRID_W - NA_WIN_W)
    in_win = (w >= cs) & (w < cs + NA_WIN_W)
    dc = np.clip(w - c, -(NA_WIN_W - 1), NA_WIN_W - 1) + NA_WIN_W - 1
    oh = np.zeros((LANES, GRID_W * GRID_W), np.float32)
    flat = np.arange(GRID_W * GRID_W).reshape(GRID_W, GRID_W)
    oh[dc[in_win], flat[in_win]] = 1.0
    neg = np.where(in_win, 0.0, -1e30).astype(np.float32).reshape(1, -1)
    return oh, neg


def _assemble_dproj(dq_na, dk_na, dv_na, dq_f, dq_b, dz_f, dz_b, dv_f, dv_b, dg, dgn, dgh, *, name):
    lp, naw = dq_na.shape
    hgw = dq_f.shape[1]
    d = dgn.shape[1]
    cols = 3 * naw + 5 * hgw + 2 * d
    tr = _row_tile(lp, 3 * naw * 4 + 6 * hgw * 4 + hgw * 2 + 2 * d * 2 + cols * 2)

    def body(nq_ref, nk_ref, nv_ref, qf_ref, qb_ref, zf_ref, zb_ref, vf_ref, vb_ref, g_ref, gn_ref,
             gh_ref, o_ref):
        o_ref[:, 0:naw] = nq_ref[...].astype(BF16)
        o_ref[:, naw:2 * naw] = nk_ref[...].astype(BF16)
        o_ref[:, 2 * naw:3 * naw] = nv_ref[...].astype(BF16)
        c0 = 3 * naw
        o_ref[:, c0:c0 + hgw] = (qf_ref[...] + qb_ref[...]).astype(BF16)
        o_ref[:, c0 + hgw:c0 + 2 * hgw] = zf_ref[...].astype(BF16)
        o_ref[:, c0 + 2 * hgw:c0 + 3 * hgw] = zb_ref[...].astype(BF16)
        o_ref[:, c0 + 3 * hgw:c0 + 4 * hgw] = (vf_ref[...] + vb_ref[...]).astype(BF16)
        o_ref[:, c0 + 4 * hgw:c0 + 5 * hgw] = g_ref[...]
        o_ref[:, c0 + 5 * hgw:c0 + 5 * hgw + d] = gn_ref[...]
        o_ref[:, c0 + 5 * hgw + d:] = gh_ref[...]

    hg, na = _rspec(tr, hgw), _rspec(tr, naw)
    return pl.pallas_call(
        body, grid=(lp // tr,),
        in_specs=[na, na, na, hg, hg, hg, hg, hg, hg, hg, _rspec(tr, d), _rspec(tr, d)],
        out_specs=_rspec(tr, cols),
        out_shape=jax.ShapeDtypeStruct((lp, cols), BF16),
        compiler_params=_cp("parallel"), name=name)(dq_na, dk_na, dv_na, dq_f, dq_b, dz_f, dz_b,
                                                    dv_f, dv_b, dg, dgn, dgh)


GROUP_STEPS = 8


def _group_tiles(rows, align):
    steps = GROUP_STEPS if all(r % (GROUP_STEPS * align) == 0 for r in rows) else 1
    return steps, [r // steps for r in rows]


def _adamw(ws, gs, ms, vs, *, name):
    n = len(ws)
    steps, trs = _group_tiles([w.shape[0] for w in ws], 8)

    def body(*refs):
        for i in range(n):
            w_ref, g_ref, m_ref, v_ref = refs[4 * i:4 * i + 4]
            d_ref, mo_ref, vo_ref = refs[4 * n + 3 * i:4 * n + 3 * i + 3]
            gv = g_ref[...]
            mn = ADAM_B1 * m_ref[...] + (1.0 - ADAM_B1) * gv
            vn = ADAM_B2 * v_ref[...] + (1.0 - ADAM_B2) * (gv * gv)
            m_hat = mn / (1.0 - ADAM_B1 ** ADAM_STEP)
            v_hat = vn / (1.0 - ADAM_B2 ** ADAM_STEP)
            d_ref[...] = -ADAM_LR * (m_hat / (jnp.sqrt(v_hat) + ADAM_EPS) + ADAM_WD * w_ref[...])
            mo_ref[...] = mn
            vo_ref[...] = vn

    specs = [_rspec(tr, w.shape[1]) for tr, w in zip(trs, ws)]
    out = pl.pallas_call(
        body, grid=(steps,),
        in_specs=[s for s in specs for _ in range(4)],
        out_specs=[s for s in specs for _ in range(3)],
        out_shape=[jax.ShapeDtypeStruct(w.shape, F32) for w in ws for _ in range(3)],
        compiler_params=_cp("parallel"), name=name)(*[a for q in zip(ws, gs, ms, vs) for a in q])
    return [tuple(out[3 * i:3 * i + 3]) for i in range(n)]


def _local_step(x, tgt, meta, first_weight, rest_weights, g_mix, g_mlp, g_fin, hg_gain, rpb, lb,
                early_grads=None, mid_grads=None, late_grad=None, rest_landed=None,
                last_grads=None):
    n_tok, d = x.shape
    hgw = hg_gain.shape[1]
    nh, hh = rpb.shape[0], hgw // HG_DK
    naw = nh * NA_HEAD_DIM
    l_real = N_META + n_tok
    lp = -(-l_real // ROW_ALIGN) * ROW_ALIGN
    col_qhg = 3 * naw
    col_zf, col_zb, col_i, col_g = (col_qhg + hgw, col_qhg + 2 * hgw, col_qhg + 3 * hgw,
                                    col_qhg + 4 * hgw)
    col_gate = col_qhg + 5 * hgw

    oh_np, neg_np = _rpb_onehot()
    oh = jnp.asarray(oh_np)
    rpb_p = jnp.pad(rpb.reshape(nh * (2 * NA_WIN_H - 1), 2 * NA_WIN_W - 1),
                    ((0, 0), (0, LANES - (2 * NA_WIN_W - 1))))
    tb = _matmul(rpb_p, oh, tm=rpb_p.shape[0], tn=512, tk=LANES, precision=HIGHEST,
                 name="rpb_expand")
    tb = (tb + jnp.asarray(neg_np)).reshape(nh, 2 * NA_WIN_H - 1, GRID_W, GRID_W)

    h0, a = _embed_norm(x, meta, g_mix, lp=lp, name="norm_mix")
    w_in = first_weight(a)
    proj = _matmul(a, w_in, name="mm_in")
    o_na, lse = _na_fwd(proj, tb, n_tok=n_tok, nh=nh, name="na_fwd")
    lb_f = lb[0].reshape(hh, 1, HG_DK)
    lb_b = lb[1].reshape(hh, 1, HG_DK)
    scan_kw = dict(col_q=col_qhg, col_i=col_i, hh=hh)
    o_f, st_f = _hg_scan_fwd(proj, lb_f, reverse=False, col_z=col_zf, name="hg_scan_f", **scan_kw)
    token = rest_landed(o_f) if rest_landed else None
    lb_b_late = lb_b if token is None else lb_b + token[0:1, 0:1]
    o_b, st_b = _hg_scan_fwd(proj, lb_b_late, reverse=True, col_z=col_zb, name="hg_scan_b",
                             **scan_kw)
    o_hg = _hg_out(o_f, o_b, proj, hg_gain, col_g=col_g, name="hg_out")
    w_na, w_hg, w_o, w_up, w_down = rest_weights(o_hg)
    y_na = _matmul(o_na, w_na, name="mm_na_out")
    gates = ((proj, col_gate), (proj, col_gate + d))

    def mix_gates(acc, gn, gh, yn):
        return acc, _sigmoid(gn) * yn + _sigmoid(gh) * acc

    def mix_gates_bwd(dmix, gn, gh, yn, yh):
        sn, sh = _sigmoid(gn), _sigmoid(gh)
        return dmix * sn, dmix * sh, dmix * yn * sn * (1.0 - sn), dmix * yh * sh * (1.0 - sh)

    y_hg, mix = _matmul(o_hg, w_hg, name="mm_hg_out", epilogue=mix_gates,
                        tiles=(*gates, (y_na, 0)), out_dtypes=(F32, BF16))
    t1 = _matmul(mix, w_o, name="mm_o")
    h1, mlp_in = _residual_norm(h0, t1, g_mlp, name="resid_norm_mlp")
    u, act = _matmul(mlp_in, w_up, name="mm_up", out_dtypes=(BF16, BF16),
                     epilogue=lambda acc: (acc, jnp.square(jnp.maximum(acc, 0.0))))
    t2 = _matmul(act, w_down, name="mm_down")
    dh2, loss, dg_fin = _final_loss(h1, t2, g_fin, tgt, name="final_loss")

    (du,) = _matmul(dh2, w_down, tb=True, name="mm_down_dx", tiles=((u, 0),), out_dtypes=(BF16,),
                    epilogue=lambda acc, uv: (acc * 2.0 * jnp.maximum(uv, 0.0),))
    dw_down = _matmul(act, dh2, ta=True, name="mm_down_dw")
    dm = _matmul(du, w_up, tb=True, name="mm_up_dx")
    dw_up = _matmul(mlp_in, du, ta=True, name="mm_up_dw")
    dh1, dg_mlp = _rmsnorm_bwd_add(h1, g_mlp, dm, dh2, name="norm_mlp_bwd")
    dy_na, dy_hg, dgn, dgh = _matmul(dh1, w_o, tb=True, name="mm_o_dx", epilogue=mix_gates_bwd,
                                     tiles=(*gates, (y_na, 0), (y_hg, 0)), out_dtypes=(BF16,) * 4)
    dw_o = _matmul(mix, dh1, ta=True, name="mm_o_dw")
    do_na = _matmul(dy_na, w_na, tb=True, name="mm_na_out_dx")
    dw_na = _matmul(o_na, dy_na, ta=True, name="mm_na_out_dw")
    do_hg = _matmul(dy_hg, w_hg, tb=True, name="mm_hg_out_dx")
    dw_hg = _matmul(o_hg, dy_hg, ta=True, name="mm_hg_out_dw")
    token = early_grads([dw_na, dw_hg, dw_o, dw_up, dw_down]) if early_grads else None
    if token is not None:
        hg_gain = hg_gain + token[0:1, 0:1]
    d_o, dg_hg, d_gain = _hg_out_bwd(o_f, o_b, proj, hg_gain, do_hg, col_g=col_g, name="hg_out_bwd")
    dq_f, dz_f, dv_f, dlb_f = _hg_scan_bwd(proj, lb_f, st_f, d_o, reverse=False, col_z=col_zf,
                                           name="hg_scan_f_bwd", **scan_kw)
    token = mid_grads(dq_f) if mid_grads else None
    lb_b_late = lb_b if token is None else lb_b + token[0:1, 0:1]
    dq_b, dz_b, dv_b, dlb_b = _hg_scan_bwd(proj, lb_b_late, st_b, d_o, reverse=True, col_z=col_zb,
                                           name="hg_scan_b_bwd", **scan_kw)
    dq_na, dk_na, dv_na, dtb = _na_bwd(proj, tb, o_na, lse, do_na, n_tok=n_tok, nh=nh, name="na_bwd")
    dproj = _assemble_dproj(dq_na, dk_na, dv_na, dq_f, dq_b, dz_f, dz_b, dv_f, dv_b, dg_hg, dgn,
                            dgh, name="assemble_dproj")
    dw_in = _matmul(a, dproj, ta=True, name="mm_in_dw")
    token = late_grad(dw_in) if late_grad else None
    da = _matmul(dproj, w_in, tb=True, name="mm_in_dx", after=token)
    token = last_grads(da) if last_grads else None
    g_mix_late = g_mix if token is None else g_mix + token[0:1, 0:1]
    dx, dmeta, dg_mix = _rmsnorm_bwd_tokens(h0, g_mix_late, da, dh1, n_tok=n_tok,
                                            name="norm_mix_bwd")
    d_rpb = _matmul(dtb.reshape(nh * (2 * NA_WIN_H - 1), GRID_W * GRID_W), oh, tb=True,
                    tm=nh * (2 * NA_WIN_H - 1), tn=LANES, tk=1024, precision=HIGHEST,
                    name="rpb_reduce")
    d_lb = jnp.concatenate([dlb_f.reshape(1, hgw), dlb_b.reshape(1, hgw)], axis=0)
    return (loss, dx, dmeta, dw_in, dw_na, dw_hg, dw_o, dw_up, dw_down,
            dg_mix, dg_mlp, dg_fin, d_gain, d_rpb, d_lb)


N_CHIPS = 4
N_DEV = 8
ANY = pl.BlockSpec(memory_space=pl.ANY)


def _place():
    x, y, c = lax.axis_index("x"), lax.axis_index("y"), lax.axis_index("c")
    others = []
    for j in (1, 2, 3):
        tx = (1 - x) if (j >> 1) else x
        ty = (1 - y) if (j & 1) else y
        others.append((tx, ty))
    return x, y, c, others


def _piece(ref, axis, k, half, rh, cs):
    if axis == 1:
        return ref.at[pl.ds(pl.multiple_of(half * rh, 16), rh), pl.ds(pl.multiple_of(k * cs, LANES), cs)]
    return ref.at[pl.ds(pl.multiple_of(k * 2 * rh + half * rh, 16), rh), :]


def _cast_into_full(shards, axes, place, *, name):
    n = len(shards)
    steps, trs = _group_tiles([s.shape[0] for s in shards], 16)

    def body(p_ref, *refs):
        for i in range(n):
            refs[n + i][...] = refs[i][...].astype(BF16)

    def out_spec(tr, cs, axis):
        if axis == 1:
            return pl.BlockSpec((tr, cs), lambda i, p_ref: (i, p_ref[0]))
        return pl.BlockSpec((tr, cs), lambda i, p_ref: (p_ref[0] * steps + i, 0))

    return pl.pallas_call(
        body,
        grid_spec=pltpu.PrefetchScalarGridSpec(
            num_scalar_prefetch=1, grid=(steps,),
            in_specs=[pl.BlockSpec((tr, s.shape[1]), lambda i, p_ref: (i, 0))
                      for tr, s in zip(trs, shards)],
            out_specs=[out_spec(tr, s.shape[1], ax) for tr, s, ax in zip(trs, shards, axes)]),
        out_shape=[jax.ShapeDtypeStruct((s.shape[0], s.shape[1] * N_CHIPS) if ax == 1
                                        else (s.shape[0] * N_CHIPS, s.shape[1]), BF16)
                   for s, ax in zip(shards, axes)],
        compiler_params=_cp("parallel"), name=name)(place, *shards)


HBM_SPEC = pl.BlockSpec(memory_space=pltpu.HBM)
SEM_SPEC = pl.BlockSpec(memory_space=pltpu.SEMAPHORE)
SPLIT_COPY = pltpu.CompilerParams(has_side_effects=pltpu.SideEffectType.DATAFLOW_SIDE_EFFECTING)
TOKEN = jax.ShapeDtypeStruct((8, LANES), F32)


def _geo(fulls, axes):
    out = []
    for f, ax in zip(fulls, axes):
        r, cs = (f.shape[0], f.shape[1] // N_CHIPS) if ax == 1 else (f.shape[0] // N_CHIPS, f.shape[1])
        out.append((ax, r // 2, cs))
    return out


def _gather_copies(refs, geo, send_sems, recv_sems):
    x, y, c, others = _place()
    chip = 2 * x + y
    cps = []
    for i, (ax, rh, cs) in enumerate(geo):
        mine = _piece(refs[i], ax, chip, c, rh, cs)
        for j, (tx, ty) in enumerate(others):
            cps.append(pltpu.make_async_remote_copy(
                src_ref=mine, dst_ref=mine, send_sem=send_sems.at[3 * i + j],
                recv_sem=recv_sems.at[3 * i + j], device_id=(tx, ty, c), device_id_type=MESH))
    return cps


def _allgather_start(fulls, axes, after, *, name):
    n = len(fulls)
    geo = _geo(fulls, axes)

    def body(*refs):
        w_refs = refs[:n]
        send_sems, recv_sems = refs[n + 1], refs[n + 2]
        token = refs[2 * n + 3]
        for cp in _gather_copies(w_refs, geo, send_sems, recv_sems):
            cp.start()
        token[...] = jnp.zeros_like(token)

    out = pl.pallas_call(
        body, name=name,
        out_shape=(pltpu.SemaphoreType.DMA((3 * n,)), pltpu.SemaphoreType.DMA((3 * n,)),
                   *[pltpu.HBM(f.shape, f.dtype) for f in fulls], TOKEN),
        in_specs=[HBM_SPEC] * n + [ANY],
        out_specs=(SEM_SPEC, SEM_SPEC, *[HBM_SPEC] * n, pl.BlockSpec(memory_space=pltpu.VMEM)),
        input_output_aliases={i: 2 + i for i in range(n)},
        compiler_params=SPLIT_COPY,
    )(*[pltpu.with_memory_space_constraint(f, pltpu.HBM) for f in fulls], after)
    return out[0], out[1], list(out[2:2 + n]), out[2 + n]


def _allgather_wait(send_sems, recv_sems, fulls, axes, after, *, name):
    n = len(fulls)
    geo = _geo(fulls, axes)

    def body(*refs):
        w_refs = refs[:n]
        for cp in _gather_copies(w_refs, geo, refs[n], refs[n + 1]):
            cp.wait_send()
            cp.wait_recv()

    return list(pl.pallas_call(
        body, name=name,
        out_shape=[pltpu.HBM(f.shape, f.dtype) for f in fulls],
        in_specs=[HBM_SPEC] * n + [SEM_SPEC, SEM_SPEC, ANY],
        out_specs=[HBM_SPEC] * n,
        input_output_aliases={i: i for i in range(n)},
        compiler_params=SPLIT_COPY,
    )(*fulls, send_sems, recv_sems, after))


def _allgather_forward(fulls, axes, *, name):
    n = len(fulls)
    geo = _geo(fulls, axes)

    def body(*refs):
        o_refs = refs[n:2 * n]
        send_sems, recv_sems = refs[2 * n:]
        x, y, c, others = _place()

        def rcopy(i, j, half, to):
            ax, rh, cs = geo[i]
            ref = _piece(o_refs[i], ax, 2 * others[j][0] + others[j][1], half, rh, cs)
            return pltpu.make_async_remote_copy(
                src_ref=ref, dst_ref=ref, send_sem=send_sems.at[3 * i + j],
                recv_sem=recv_sems.at[3 * i + j], device_id=to, device_id_type=MESH)

        cps = [rcopy(i, j, c, (x, y, 1 - c)) for i in range(n) for j in range(3)]
        for cp in cps:
            cp.start()
        for i in range(n):
            for j in range(3):
                rcopy(i, j, 1 - c, (x, y, c)).wait_recv()
        for cp in cps:
            cp.wait_send()

    return list(pl.pallas_call(
        body, in_specs=[ANY] * n, out_specs=[ANY] * n,
        out_shape=[jax.ShapeDtypeStruct(f.shape, f.dtype) for f in fulls],
        input_output_aliases={i: i for i in range(n)},
        scratch_shapes=[pltpu.SemaphoreType.DMA((3 * n,)), pltpu.SemaphoreType.DMA((3 * n,))],
        name=name)(*fulls))


def _forward_copies(refs, geo, send_sems, recv_sems):
    x, y, c, others = _place()
    cps = []
    for i, (ax, rh, cs) in enumerate(geo):
        for j, (tx, ty) in enumerate(others):
            ref = _piece(refs[i], ax, 2 * tx + ty, c, rh, cs)
            cps.append(pltpu.make_async_remote_copy(
                src_ref=ref, dst_ref=ref, send_sem=send_sems.at[3 * i + j],
                recv_sem=recv_sems.at[3 * i + j], device_id=(x, y, 1 - c), device_id_type=MESH))
    return cps


def _allgather_forward_start(fulls, axes, *, name):
    n = len(fulls)
    geo = _geo(fulls, axes)

    def body(*refs):
        token = refs[2 * n + 2]
        for cp in _forward_copies(refs[:n], geo, refs[n], refs[n + 1]):
            cp.start()
        token[...] = jnp.zeros_like(token)

    out = pl.pallas_call(
        body, name=name,
        out_shape=(pltpu.SemaphoreType.DMA((3 * n,)), pltpu.SemaphoreType.DMA((3 * n,)),
                   *[pltpu.HBM(f.shape, f.dtype) for f in fulls], TOKEN),
        in_specs=[HBM_SPEC] * n,
        out_specs=(SEM_SPEC, SEM_SPEC, *[HBM_SPEC] * n, pl.BlockSpec(memory_space=pltpu.VMEM)),
        input_output_aliases={i: 2 + i for i in range(n)},
        compiler_params=SPLIT_COPY,
    )(*fulls)
    return out[0], out[1], list(out[2:2 + n]), out[2 + n]


def _allgather_forward_wait(send_sems, recv_sems, fulls, axes, after, *, name):
    n = len(fulls)
    geo = _geo(fulls, axes)

    def body(*refs):
        for cp in _forward_copies(refs[:n], geo, refs[n], refs[n + 1]):
            cp.wait_send()
            cp.wait_recv()

    return list(pl.pallas_call(
        body, name=name,
        out_shape=[pltpu.HBM(f.shape, f.dtype) for f in fulls],
        in_specs=[HBM_SPEC] * n + [SEM_SPEC, SEM_SPEC, ANY],
        out_specs=[HBM_SPEC] * n,
        input_output_aliases={i: i for i in range(n)},
        compiler_params=SPLIT_COPY,
    )(*fulls, send_sems, recv_sems, after))


def _chip_copies(blk_ref, land_ref, send_sems, recv_sems):
    x, y, c, others = _place()
    return [pltpu.make_async_remote_copy(
        src_ref=blk_ref, dst_ref=land_ref.at[2 * x + y], send_sem=send_sems.at[j],
        recv_sem=recv_sems.at[j], device_id=(tx, ty, c), device_id_type=MESH)
        for j, (tx, ty) in enumerate(others)]


def _chip_exchange_start(blk, *, name):
    land = pltpu.with_memory_space_constraint(lax.empty((N_CHIPS, *blk.shape), blk.dtype), pltpu.HBM)

    def body(blk_ref, land_ref, send_sems, recv_sems, blk_out, land_out, token):
        for cp in _chip_copies(blk_ref, land_ref, send_sems, recv_sems):
            cp.start()
        token[...] = jnp.zeros_like(token)

    return pl.pallas_call(
        body, name=name,
        out_shape=(pltpu.SemaphoreType.DMA((3,)), pltpu.SemaphoreType.DMA((3,)),
                   pltpu.HBM(blk.shape, blk.dtype), pltpu.HBM(land.shape, land.dtype), TOKEN),
        in_specs=[HBM_SPEC] * 2,
        out_specs=(SEM_SPEC, SEM_SPEC, HBM_SPEC, HBM_SPEC, pl.BlockSpec(memory_space=pltpu.VMEM)),
        input_output_aliases={0: 2, 1: 3},
        compiler_params=SPLIT_COPY,
    )(pltpu.with_memory_space_constraint(blk, pltpu.HBM), land)


def _chip_exchange_wait(send_sems, recv_sems, blk, land, after, *, name):
    def body(blk_ref, land_ref, send_sems, recv_sems, after_ref, blk_out, land_out):
        for cp in _chip_copies(blk_ref, land_ref, send_sems, recv_sems):
            cp.wait_send()
            cp.wait_recv()

    return pl.pallas_call(
        body, name=name,
        out_shape=[pltpu.HBM(blk.shape, blk.dtype), pltpu.HBM(land.shape, land.dtype)],
        in_specs=[HBM_SPEC] * 2 + [SEM_SPEC, SEM_SPEC, ANY],
        out_specs=[HBM_SPEC] * 2,
        input_output_aliases={0: 0, 1: 1},
        compiler_params=SPLIT_COPY,
    )(blk, land, send_sems, recv_sems, after)[1]


def _scatter_geo(parts, axes):
    out = []
    for p, ax in zip(parts, axes):
        _, rh, cols = p.shape
        out.append((ax, rh, cols // N_CHIPS if ax == 1 else cols))
    return out


def _scatter_copies(p_refs, q_refs, geo, send_sems, recv_sems):
    x, y, c, others = _place()
    chip = 2 * x + y
    cps = []
    for i, (ax, rh, cw) in enumerate(geo):
        for j, (tx, ty) in enumerate(others):
            k = 2 * tx + ty
            src = (p_refs[i].at[0, :, pl.ds(pl.multiple_of(k * cw, LANES), cw)] if ax == 1
                   else p_refs[i].at[k])
            cps.append(pltpu.make_async_remote_copy(
                src_ref=src, dst_ref=q_refs[i].at[chip], send_sem=send_sems.at[3 * i + j],
                recv_sem=recv_sems.at[3 * i + j], device_id=(tx, ty, c), device_id_type=MESH))
    return cps


def _scatter_start(parts, axes, *, name):
    n = len(parts)
    geo = _scatter_geo(parts, axes)
    slots = [pltpu.HBM((N_CHIPS, rh, cw), p.dtype) for p, (_, rh, cw) in zip(parts, geo)]

    def body(*refs):
        p_refs, q_refs = refs[:n], refs[n:2 * n]
        send_sems, recv_sems = refs[2 * n], refs[2 * n + 1]
        token = refs[4 * n + 2]
        for cp in _scatter_copies(p_refs, q_refs, geo, send_sems, recv_sems):
            cp.start()
        token[...] = jnp.zeros_like(token)

    land = [pltpu.with_memory_space_constraint(lax.empty(s.inner_aval.shape, s.inner_aval.dtype), pltpu.HBM)
            for s in slots]
    out = pl.pallas_call(
        body, name=name,
        out_shape=(pltpu.SemaphoreType.DMA((3 * n,)), pltpu.SemaphoreType.DMA((3 * n,)),
                   *[pltpu.HBM(p.shape, p.dtype) for p in parts], *slots, TOKEN),
        in_specs=[HBM_SPEC] * (2 * n),
        out_specs=(SEM_SPEC, SEM_SPEC, *[HBM_SPEC] * (2 * n), pl.BlockSpec(memory_space=pltpu.VMEM)),
        input_output_aliases={i: 2 + i for i in range(2 * n)},
        compiler_params=SPLIT_COPY,
    )(*[pltpu.with_memory_space_constraint(p, pltpu.HBM) for p in parts], *land)
    return out[0], out[1], list(out[2:2 + n]), list(out[2 + n:2 + 2 * n]), out[2 + 2 * n]


def _scatter_wait(send_sems, recv_sems, parts, slots, axes, after, *, name):
    n = len(parts)
    geo = _scatter_geo(parts, axes)

    def body(*refs):
        p_refs, q_refs = refs[:n], refs[n:2 * n]
        for cp in _scatter_copies(p_refs, q_refs, geo, refs[2 * n], refs[2 * n + 1]):
            cp.wait_send()
            cp.wait_recv()

    out = pl.pallas_call(
        body, name=name,
        out_shape=[pltpu.HBM(a.shape, a.dtype) for a in (*parts, *slots)],
        in_specs=[HBM_SPEC] * (2 * n) + [SEM_SPEC, SEM_SPEC, ANY],
        out_specs=[HBM_SPEC] * (2 * n),
        input_output_aliases={i: i for i in range(2 * n)},
        compiler_params=SPLIT_COPY,
    )(*parts, *slots, send_sems, recv_sems, after)
    return list(out[:n]), list(out[n:])


def _sibling_swap(grads, *, name):
    n = len(grads)
    out_shape = [jax.ShapeDtypeStruct((g.shape[0], g.shape[1] // 2, g.shape[2]), g.dtype)
                 for g in grads]

    def body(*refs):
        g_refs, o_refs = refs[:n], refs[n:2 * n]
        send_sems, recv_sems = refs[2 * n:]
        x, y, c, _ = _place()
        cps = []
        for i in range(n):
            rh = grads[i].shape[1] // 2
            src = g_refs[i].at[:, pl.ds(pl.multiple_of((1 - c) * rh, 16), rh), :]
            cp = pltpu.make_async_remote_copy(
                src_ref=src, dst_ref=o_refs[i], send_sem=send_sems.at[i], recv_sem=recv_sems.at[i],
                device_id=(x, y, 1 - c), device_id_type=MESH)
            cp.start()
            cps.append(cp)
        for cp in cps:
            cp.wait()

    return pl.pallas_call(
        body, in_specs=[ANY] * n, out_specs=[ANY] * n, out_shape=out_shape,
        scratch_shapes=[pltpu.SemaphoreType.DMA((n,)), pltpu.SemaphoreType.DMA((n,))],
        name=name)(*grads)


def _swap_copies(g_refs, r_refs, shapes, send_sems, recv_sems):
    x, y, c, _ = _place()
    cps = []
    for i, shape in enumerate(shapes):
        rh = shape[1] // 2
        src = g_refs[i].at[:, pl.ds(pl.multiple_of((1 - c) * rh, 16), rh), :]
        cps.append(pltpu.make_async_remote_copy(
            src_ref=src, dst_ref=r_refs[i], send_sem=send_sems.at[i], recv_sem=recv_sems.at[i],
            device_id=(x, y, 1 - c), device_id_type=MESH))
    return cps


def _sibling_swap_start(grads, *, name):
    n = len(grads)
    shapes = [g.shape for g in grads]
    lands = [pltpu.HBM((s[0], s[1] // 2, s[2]), g.dtype) for s, g in zip(shapes, grads)]

    def body(*refs):
        g_refs, r_refs = refs[:n], refs[n:2 * n]
        token = refs[4 * n + 2]
        for cp in _swap_copies(g_refs, r_refs, shapes, refs[2 * n], refs[2 * n + 1]):
            cp.start()
        token[...] = jnp.zeros_like(token)

    land = [pltpu.with_memory_space_constraint(lax.empty(s.inner_aval.shape, s.inner_aval.dtype), pltpu.HBM)
            for s in lands]
    out = pl.pallas_call(
        body, name=name,
        out_shape=(pltpu.SemaphoreType.DMA((n,)), pltpu.SemaphoreType.DMA((n,)),
                   *[pltpu.HBM(g.shape, g.dtype) for g in grads], *lands, TOKEN),
        in_specs=[HBM_SPEC] * (2 * n),
        out_specs=(SEM_SPEC, SEM_SPEC, *[HBM_SPEC] * (2 * n), pl.BlockSpec(memory_space=pltpu.VMEM)),
        input_output_aliases={i: 2 + i for i in range(2 * n)},
        compiler_params=SPLIT_COPY,
    )(*[pltpu.with_memory_space_constraint(g, pltpu.HBM) for g in grads], *land)
    return out[0], out[1], list(out[2:2 + n]), list(out[2 + n:2 + 2 * n]), out[2 + 2 * n]


def _sibling_swap_wait(send_sems, recv_sems, grads, lands, after, *, name):
    n = len(grads)
    shapes = [g.shape for g in grads]

    def body(*refs):
        g_refs, r_refs = refs[:n], refs[n:2 * n]
        for cp in _swap_copies(g_refs, r_refs, shapes, refs[2 * n], refs[2 * n + 1]):
            cp.wait_send()
            cp.wait_recv()

    out = pl.pallas_call(
        body, name=name,
        out_shape=[pltpu.HBM(a.shape, a.dtype) for a in (*grads, *lands)],
        in_specs=[HBM_SPEC] * (2 * n) + [SEM_SPEC, SEM_SPEC, ANY],
        out_specs=[HBM_SPEC] * (2 * n),
        input_output_aliases={i: i for i in range(2 * n)},
        compiler_params=SPLIT_COPY,
    )(*grads, *lands, send_sems, recv_sems, after)
    return list(out[:n]), list(out[n:])


def _pair_add(g3s, rxs, place, *, out_dtype, name):
    n = len(g3s)
    steps, trs = _group_tiles([g.shape[1] // 2 for g in g3s], 16)

    def body(p_ref, *refs):
        for i in range(n):
            refs[2 * n + i][...] = (refs[2 * i][...] + refs[2 * i + 1][...]).astype(out_dtype)

    in_specs, out_specs = [], []
    for g, tr in zip(g3s, trs):
        blk = (g.shape[0], tr, g.shape[2])
        in_specs += [pl.BlockSpec(blk, lambda i, p_ref: (0, p_ref[1] * steps + i, 0)),
                     pl.BlockSpec(blk, lambda i, p_ref: (0, i, 0))]
        out_specs.append(pl.BlockSpec(blk, lambda i, p_ref: (0, i, 0)))
    return pl.pallas_call(
        body,
        grid_spec=pltpu.PrefetchScalarGridSpec(
            num_scalar_prefetch=1, grid=(steps,), in_specs=in_specs, out_specs=out_specs),
        out_shape=[jax.ShapeDtypeStruct((g.shape[0], g.shape[1] // 2, g.shape[2]), out_dtype)
                   for g in g3s],
        compiler_params=_cp("parallel"), name=name)(place, *[a for q in zip(g3s, rxs) for a in q])


def _sum_slots(q, *, name):
    ns, rows, cols = q.shape
    tr = next(t for t in (128, 64, 32, 16, 8) if rows % t == 0)

    def body(q_ref, o_ref):
        acc = q_ref[0].astype(F32)
        for k in range(1, ns):
            acc = acc + q_ref[k].astype(F32)
        o_ref[...] = acc

    return pl.pallas_call(
        body, grid=(rows // tr,),
        in_specs=[pl.BlockSpec((ns, tr, cols), lambda i: (0, i, 0))],
        out_specs=_rspec(tr, cols),
        out_shape=jax.ShapeDtypeStruct((rows, cols), F32),
        compiler_params=_cp("parallel"), name=name)(q)


def _sum_chips(qs, ps, place, axes, *, name):
    n = len(qs)
    per = N_CHIPS + 1
    steps, trs = _group_tiles([q.shape[1] for q in qs], 16)

    def body(p_ref, *refs):
        chip = p_ref[0]
        for i in range(n):
            q_refs, own_ref = refs[per * i:per * i + N_CHIPS], refs[per * i + N_CHIPS]
            acc = jnp.where(chip == 0, own_ref[...], q_refs[0][...]).astype(F32)
            for k in range(1, N_CHIPS):
                acc = acc + jnp.where(chip == k, own_ref[...], q_refs[k][...]).astype(F32)
            refs[per * n + i][...] = acc

    def slot_spec(k, tr, cw):
        return pl.BlockSpec((None, tr, cw),
                            lambda i, p_ref: (jnp.where(p_ref[0] == k, (k + 1) % N_CHIPS, k), i, 0))

    in_specs, out_specs, operands = [], [], []
    for q, p, ax, tr in zip(qs, ps, axes, trs):
        cw = q.shape[2]
        in_specs += [slot_spec(k, tr, cw) for k in range(N_CHIPS)]
        in_specs.append(pl.BlockSpec((None, tr, cw), (lambda i, p_ref: (0, i, p_ref[0])) if ax == 1
                                     else (lambda i, p_ref: (p_ref[0], i, 0))))
        out_specs.append(pl.BlockSpec((tr, cw), lambda i, p_ref: (p_ref[1] * steps + i, 0)))
        operands += [q] * N_CHIPS + [p]
    return pl.pallas_call(
        body,
        grid_spec=pltpu.PrefetchScalarGridSpec(
            num_scalar_prefetch=1, grid=(steps,), in_specs=in_specs, out_specs=out_specs),
        out_shape=[jax.ShapeDtypeStruct((2 * q.shape[1], q.shape[2]), F32) for q in qs],
        compiler_params=_cp("parallel"), name=name)(place, *operands)


def _sibling_share(shards, *, name):
    n = len(shards)

    def body(*refs):
        o_refs = refs[n:2 * n]
        send_sems, recv_sems = refs[2 * n:]
        x, y, c, _ = _place()
        cps = []
        for i in range(n):
            rh = shards[i].shape[0] // 2
            mine = o_refs[i].at[pl.ds(pl.multiple_of(c * rh, 8), rh), :]
            cp = pltpu.make_async_remote_copy(
                src_ref=mine, dst_ref=mine, send_sem=send_sems.at[i], recv_sem=recv_sems.at[i],
                device_id=(x, y, 1 - c), device_id_type=MESH)
            cp.start()
            cps.append(cp)
        for i in range(n):
            rh = shards[i].shape[0] // 2
            theirs = o_refs[i].at[pl.ds(pl.multiple_of((1 - c) * rh, 8), rh), :]
            pltpu.make_async_remote_copy(
                src_ref=theirs, dst_ref=theirs, send_sem=send_sems.at[i], recv_sem=recv_sems.at[i],
                device_id=(x, y, c), device_id_type=MESH).wait_recv()
        for cp in cps:
            cp.wait_send()

    return pl.pallas_call(
        body, in_specs=[ANY] * n, out_specs=[ANY] * n,
        out_shape=[jax.ShapeDtypeStruct(h.shape, h.dtype) for h in shards],
        input_output_aliases={i: i for i in range(n)},
        scratch_shapes=[pltpu.SemaphoreType.DMA((n,)), pltpu.SemaphoreType.DMA((n,))],
        name=name)(*shards)


def _gather_all(blk, *, name, after=None, shares=()):
    rows, cols = blk.shape
    extra = [] if after is None else [after]
    n_s = len(shares)

    def body(x_ref, *refs):
        s_refs = refs[len(extra) + n_s + 1:len(extra) + 2 * n_s + 1]
        out_ref = refs[len(extra) + n_s]
        send_sems, recv_sems, local_sem, s_send, s_recv = refs[len(extra) + 2 * n_s + 1:]
        x, y, c = lax.axis_index("x"), lax.axis_index("y"), lax.axis_index("c")
        me = 4 * x + 2 * y + c
        mine = pltpu.make_async_copy(x_ref, out_ref.at[me], local_sem)
        mine.start()
        cps = []
        for i in range(n_s):
            rh = shares[i].shape[0] // 2
            half = s_refs[i].at[pl.ds(pl.multiple_of(c * rh, 8), rh), :]
            cp = pltpu.make_async_remote_copy(
                src_ref=half, dst_ref=half, send_sem=s_send.at[i], recv_sem=s_recv.at[i],
                device_id=(x, y, 1 - c), device_id_type=MESH)
            cp.start()
            cps.append(cp)
        for k in range(1, N_DEV):
            tx = (1 - x) if (k >> 2) & 1 else x
            ty = (1 - y) if (k >> 1) & 1 else y
            tc = (1 - c) if k & 1 else c
            cp = pltpu.make_async_remote_copy(
                src_ref=x_ref, dst_ref=out_ref.at[me], send_sem=send_sems.at[k - 1],
                recv_sem=recv_sems.at[k - 1], device_id=(tx, ty, tc), device_id_type=MESH)
            cp.start()
            cps.append(cp)
        for k in range(1, N_DEV):
            tx = (1 - x) if (k >> 2) & 1 else x
            ty = (1 - y) if (k >> 1) & 1 else y
            tc = (1 - c) if k & 1 else c
            got = out_ref.at[4 * tx + 2 * ty + tc]
            pltpu.make_async_remote_copy(
                src_ref=got, dst_ref=got, send_sem=send_sems.at[k - 1], recv_sem=recv_sems.at[k - 1],
                device_id=(x, y, c), device_id_type=MESH).wait_recv()
        for i in range(n_s):
            rh = shares[i].shape[0] // 2
            theirs = s_refs[i].at[pl.ds(pl.multiple_of((1 - c) * rh, 8), rh), :]
            pltpu.make_async_remote_copy(
                src_ref=theirs, dst_ref=theirs, send_sem=s_send.at[i], recv_sem=s_recv.at[i],
                device_id=(x, y, c), device_id_type=MESH).wait_recv()
        for cp in cps:
            cp.wait_send()
        mine.wait()

    vm = pl.BlockSpec(memory_space=pltpu.VMEM)
    out = pl.pallas_call(
        body, in_specs=[vm] + [ANY] * (len(extra) + n_s), out_specs=[vm] + [ANY] * n_s,
        out_shape=[jax.ShapeDtypeStruct((N_DEV, rows, cols), blk.dtype)]
        + [jax.ShapeDtypeStruct(s.shape, s.dtype) for s in shares],
        input_output_aliases={1 + len(extra) + i: 1 + i for i in range(n_s)},
        scratch_shapes=[pltpu.SemaphoreType.DMA((N_DEV - 1,)), pltpu.SemaphoreType.DMA((N_DEV - 1,)),
                        pltpu.SemaphoreType.DMA, pltpu.SemaphoreType.DMA((max(n_s, 1),)),
                        pltpu.SemaphoreType.DMA((max(n_s, 1),))],
        name=name)(blk, *extra, *shares)
    return (out[0], *out[1:]) if n_s else out[0]


def _as_rows(a):
    flat = a.reshape(-1)
    n = flat.shape[0]
    rows = -(-n // (8 * LANES)) * 8
    return jnp.pad(flat, (0, rows * LANES - n)).reshape(rows, LANES)


def _from_rows(p, shape):
    n = int(np.prod(shape))
    return p.reshape(-1)[:n].reshape(shape)


WEIGHT_AXES = (1, 1, 1, 0, 1, 0)
WIRE = BF16


def kernel(x, meta_tokens, w_in, w_na_out, w_hg_out, w_o, w_up, w_down, norm_mix, norm_mlp, norm_final, hg_norm, na_rpb, hg_lb_logits, loss_target, m_meta_tokens, m_w_in, m_w_na_out, m_w_hg_out, m_w_o, m_w_up, m_w_down, m_norm_mix, m_norm_mlp, m_norm_final, m_hg_norm, m_na_rpb, m_hg_lb_logits, v_meta_tokens, v_w_in, v_w_na_out, v_w_hg_out, v_w_o, v_w_up, v_w_down, v_norm_mix, v_norm_mlp, v_norm_final, v_hg_norm, v_na_rpb, v_hg_lb_logits):
    xi, yi, ci = lax.axis_index("x"), lax.axis_index("y"), lax.axis_index("c")
    chip = 2 * xi + yi
    d = x.shape[-1]
    dshard = meta_tokens.shape[1]
    hgw = hg_norm.shape[1]
    lbs = hg_lb_logits.shape[2]
    big = [w_in[0], w_na_out[0], w_hg_out[0], w_o[0], w_up[0], w_down[0]]
    big_m = [m_w_in[0], m_w_na_out[0], m_w_hg_out[0], m_w_o[0], m_w_up[0], m_w_down[0]]
    big_v = [v_w_in[0], v_w_na_out[0], v_w_hg_out[0], v_w_o[0], v_w_up[0], v_w_down[0]]

    place = jnp.stack([chip, ci]).astype(jnp.int32)
    own_w = _cast_into_full(big, WEIGHT_AXES, place, name="cast_shards")
    in_axes, rest_axes = WEIGHT_AXES[:1], WEIGHT_AXES[1:]
    small_in = jnp.concatenate([_as_rows(meta_tokens), _as_rows(hg_lb_logits)], axis=0)
    sm_send, sm_recv, sm_blk, sm_land, sm_token = _chip_exchange_start(small_in,
                                                                       name="small_params_start")
    in_send, in_recv, in_bufs, in_token = _allgather_start(own_w[:1], in_axes, sm_token,
                                                           name="weight_allgather_in_start")
    ag_send, ag_recv, ag_bufs, ag_token = _allgather_start(own_w[1:], rest_axes, in_token,
                                                           name="weight_allgather_rest_start")
    sm_land = _chip_exchange_wait(sm_send, sm_recv, sm_blk, sm_land, ag_token,
                                  name="small_params_wait")
    small_all = lax.dynamic_update_slice(sm_land, small_in[None], (chip, 0, 0))
    forward = {}

    def first_weight(after):
        got = _allgather_wait(in_send, in_recv, in_bufs, in_axes, after,
                              name="weight_allgather_in_wait")
        return _allgather_forward(got, in_axes, name="weight_allgather_in_forward")[0]

    def rest_landed(after):
        got = _allgather_wait(ag_send, ag_recv, ag_bufs, rest_axes, after,
                              name="weight_allgather_rest_wait")
        send, recv, bufs, token = _allgather_forward_start(
            got, rest_axes, name="weight_allgather_rest_forward_start")
        forward["rest"] = (send, recv, bufs)
        return token

    def rest_weights(after):
        return _allgather_forward_wait(*forward["rest"], rest_axes, after,
                                       name="weight_allgather_rest_forward_wait")

    n_meta_rows = N_META * dshard // LANES
    meta_full = (small_all[:, :n_meta_rows].reshape(N_CHIPS, N_META, dshard)
                 .transpose(1, 0, 2).reshape(N_META, d))
    lbl_full = (small_all[:, n_meta_rows:].reshape(N_CHIPS, -1)[:, :4 * lbs]
                .reshape(N_CHIPS, 2, 2, lbs).transpose(1, 2, 0, 3).reshape(2, 2, N_CHIPS * lbs))
    lb = jax.nn.softmax(lbl_full, axis=1)[:, 0]

    def by_chip(dws, axes):
        return [g.reshape(1, *g.shape) if ax == 1
                else g.reshape(N_CHIPS, g.shape[0] // N_CHIPS, g.shape[1]) for g, ax in zip(dws, axes)]

    flying = {}

    def scatter(tag, axes, g3, rx):
        parts = _pair_add(g3, rx, place, out_dtype=WIRE, name=f"grad_pair_add_{tag}")
        send, recv, parts, slots, token = _scatter_start(parts, axes,
                                                         name=f"grad_scatter_{tag}_start")
        flying[tag] = (send, recv, parts, slots)
        return token

    def swap(tag, axes):
        def start(dws):
            send, recv, g3, lands, token = _sibling_swap_start(
                by_chip(dws, axes), name=f"grad_sibling_swap_{tag}_start")
            flying["swap_" + tag] = (send, recv, g3, lands)
            return token

        def finish(after):
            g3, rx = _sibling_swap_wait(*flying["swap_" + tag], after,
                                        name=f"grad_sibling_swap_{tag}_wait")
            return scatter(tag, axes, g3, rx)
        return start, finish

    swap_rest, scatter_rest = swap("rest", rest_axes)
    swap_in, scatter_in = swap("in", in_axes)

    def landed(tag, axes, after):
        return _scatter_wait(*flying[tag], axes, after, name=f"grad_scatter_{tag}_wait")

    (loss, dx, dmeta, *_, dg_mix, dg_mlp, dg_fin, d_gain, d_rpb, d_lb) = _local_step(
        x[0], loss_target[0], meta_full, first_weight, rest_weights, norm_mix, norm_mlp,
        norm_final.reshape(1, d), hg_norm, na_rpb[0], lb, swap_rest, scatter_rest,
        lambda dw_in: swap_in([dw_in]), rest_landed, scatter_in)

    parts_rest, slots_rest = landed("rest", rest_axes, dx)
    g_rest = _sibling_share(_sum_chips(slots_rest, parts_rest, place, rest_axes,
                                       name="grad_sum_chips_rest"),
                            name="grad_sibling_share_rest")
    out_rest = _adamw(big[1:], g_rest, big_m[1:], big_v[1:], name="adamw_rest")
    parts_in, slots_in = landed("in", in_axes, out_rest[-1][0])
    half_in = _sum_chips(slots_in, parts_in, place, in_axes, name="grad_sum_chips_in")

    d_rpb_c = d_rpb[:, :2 * NA_WIN_W - 1]
    small_g = [dmeta, dg_mix, dg_mlp, dg_fin, d_gain, d_rpb_c, d_lb, loss]
    packed = jnp.concatenate([_as_rows(a) for a in small_g], axis=0)
    gathered, g_in = _gather_all(packed, shares=half_in, name="gather_small_grads")
    g_big = [g_in] + list(g_rest)
    total = _sum_slots(gathered, name="sum_small_grads")
    offs = np.cumsum([0] + [_as_rows(a).shape[0] for a in small_g])
    take = lambda i, shape: _from_rows(total[offs[i]:offs[i + 1]], shape)
    g_meta_full = take(0, (N_META, d))
    g_norm_mix, g_norm_mlp = take(1, (1, d)), take(2, (1, d))
    g_norm_final = take(3, (d,))
    g_hg_norm = take(4, (1, hgw))
    g_rpb = take(5, na_rpb.shape)
    g_lb = take(6, (2, hgw))
    loss_total = take(7, (1, LANES))[0, 0]
    g_meta = lax.dynamic_slice_in_dim(g_meta_full, chip * dshard, dshard, axis=1)
    dl0 = lb * (1.0 - lb) * g_lb
    g_lbl_full = jnp.stack([dl0, -dl0], axis=1)
    g_lbl = lax.dynamic_slice_in_dim(g_lbl_full, chip * lbs, lbs, axis=2)

    big_out = _adamw(big[:1], [g_in], big_m[:1], big_v[:1], name="adamw_in") + out_rest
    small_w = [meta_tokens, norm_mix, norm_mlp, norm_final, hg_norm, na_rpb, hg_lb_logits]
    small_gr = [g_meta, g_norm_mix, g_norm_mlp, g_norm_final, g_hg_norm, g_rpb, g_lbl]
    small_m = [m_meta_tokens, m_norm_mix, m_norm_mlp, m_norm_final, m_hg_norm, m_na_rpb, m_hg_lb_logits]
    small_v = [v_meta_tokens, v_norm_mix, v_norm_mlp, v_norm_final, v_hg_norm, v_na_rpb, v_hg_lb_logits]
    pk = lambda lst: jnp.concatenate([_as_rows(a) for a in lst], axis=0)
    ((sd, sm, sv),) = _adamw([pk(small_w)], [pk(small_gr)], [pk(small_m)], [pk(small_v)],
                             name="adamw_small")
    soffs = np.cumsum([0] + [_as_rows(a).shape[0] for a in small_w])
    unpk = lambda p: [_from_rows(p[soffs[i]:soffs[i + 1]], small_w[i].shape) for i in range(len(small_w))]
    sd, sm, sv = unpk(sd), unpk(sm), unpk(sv)

    def order(bigs, smalls):
        return [smalls[0]] + [b.reshape(1, *b.shape) for b in bigs] + smalls[1:]

    grads = order(g_big, small_gr)
    deltas = order([o[0] for o in big_out], sd)
    new_m = order([o[1] for o in big_out], sm)
    new_v = order([o[2] for o in big_out], sv)
    return (loss_total, dx.reshape(1, *dx.shape), *grads, *deltas, *new_m, *new_v)
```

```python
import functools

import numpy as np
import jax
import jax.numpy as jnp
from jax import lax
from jax.experimental import pallas as pl
from jax.experimental.pallas import tpu as pltpu

F32 = jnp.float32
BF16 = jnp.bfloat16
HIGHEST = lax.Precision.HIGHEST

GRID_W = 64
N_META = 16
EPS = 1e-6
NA_HEAD_DIM = 64
NA_WIN_H = 8
NA_WIN_W = 16
HG_DK = 128
HG_CHUNK = 16
LANES = 128
ROW_ALIGN = 128
VMEM_LIMIT = 48 * 1024 * 1024

ADAM_LR = 0.001
ADAM_B1 = 0.9
ADAM_B2 = 0.999
ADAM_EPS = 1e-08
ADAM_WD = 0.01
ADAM_STEP = 10

MESH = pl.DeviceIdType.MESH


def _cp(*sem):
    return pltpu.CompilerParams(dimension_semantics=sem, vmem_limit_bytes=VMEM_LIMIT)


def _sigmoid(x):
    return 0.5 * jnp.tanh(0.5 * x) + 0.5


def _dot(a, b, dims, precision=None):
    return lax.dot_general(a, b, (dims, ((), ())), preferred_element_type=F32, precision=precision)


def _nn(a, b, **kw):
    return _dot(a, b, ((1,), (0,)), **kw)


def _nt(a, b, **kw):
    return _dot(a, b, ((1,), (1,)), **kw)


def _tn(a, b, **kw):
    return _dot(a, b, ((0,), (0,)), **kw)


def _matmul(a, b, *, ta=False, tb=False, tm=None, tn=None, tk=None, out_dtype=F32, name,
            precision=None, after=None, epilogue=None, tiles=(), out_dtypes=None):
    extra = [] if after is None else [after]
    single = out_dtypes is None
    if single:
        out_dtypes = (out_dtype,)
    n_t, n_o = len(tiles), len(out_dtypes)
    if ta:
        kdim, m = a.shape
    else:
        m, kdim = a.shape
    if tb:
        n, k2 = b.shape
    else:
        k2, n = b.shape
    assert kdim == k2, (a.shape, b.shape, ta, tb)
    if tm is None:
        if ta:
            tm = next(t for t in (1024, 512, 256, 128, m) if m % t == 0)
        else:
            tm = m // 2 if (m // 2) % 16 == 0 and m > 512 else m
    if tn is None:
        wide = (1024,) if not ta and len(tiles) <= 1 else ()
        tn = next(t for t in (*wide, 512, 256, 128, n) if n % t == 0)
    if tk is None:
        tk = kdim if ta else next(t for t in (1024, 512, 256, 128, kdim) if kdim % t == 0)
    assert m % tm == 0 and n % tn == 0 and kdim % tk == 0, (m, n, kdim, tm, tn, tk)
    nk = kdim // tk
    op_dtype = F32 if precision is not None else BF16

    def body(a_ref, b_ref, *refs):
        t_refs = refs[:n_t]
        o_refs = refs[n_t + len(extra):n_t + len(extra) + n_o]
        av = a_ref[...].astype(op_dtype)
        bv = b_ref[...].astype(op_dtype)
        dims = ((0 if ta else 1,), (1 if tb else 0,))
        part = _dot(av, bv, dims, precision=precision)

        def finish(acc):
            outs = (acc,) if epilogue is None else epilogue(acc, *[t[...] for t in t_refs])
            for o_ref, val in zip(o_refs, outs):
                o_ref[...] = val.astype(o_ref.dtype)

        if nk == 1:
            finish(part)
            return
        acc_ref = refs[-1]
        kk = pl.program_id(2)

        @pl.when(kk == 0)
        def _():
            acc_ref[...] = part

        @pl.when((kk > 0) & (kk < nk - 1))
        def _():
            acc_ref[...] += part

        @pl.when(kk == nk - 1)
        def _():
            finish(acc_ref[...] + part)

    a_spec = (pl.BlockSpec((tk, tm), lambda i, j, k: (k, i)) if ta
              else pl.BlockSpec((tm, tk), lambda i, j, k: (i, k)))
    b_spec = (pl.BlockSpec((tn, tk), lambda i, j, k: (j, k)) if tb
              else pl.BlockSpec((tk, tn), lambda i, j, k: (k, j)))
    for _, off in tiles:
        assert off % tn == 0, (off, tn)
    t_specs = [pl.BlockSpec((tm, tn), functools.partial(lambda i, j, k, o: (i, o + j), o=off // tn))
               for _, off in tiles]
    o_spec = pl.BlockSpec((tm, tn), lambda i, j, k: (i, j))
    outs = pl.pallas_call(
        body,
        grid=(m // tm, n // tn, nk),
        in_specs=[a_spec, b_spec] + t_specs + [pl.BlockSpec(memory_space=pl.ANY)] * len(extra),
        out_specs=[o_spec] * n_o,
        out_shape=[jax.ShapeDtypeStruct((m, n), dt) for dt in out_dtypes],
        scratch_shapes=[pltpu.VMEM((tm, tn), F32)] if nk > 1 else [],
        compiler_params=_cp("parallel", "parallel", "arbitrary"),
        name=name,
    )(a, b, *[t for t, _ in tiles], *extra)
    return outs[0] if single else outs


def _rspec(tr, w, cb=0):
    return pl.BlockSpec((tr, w), lambda i: (i, cb))


def _fspec(shape):
    nd = len(shape)
    return pl.BlockSpec(shape, lambda i: (0,) * nd)


ROW_VMEM_BUDGET = 20 * 1024 * 1024
ROW_MIN_STEPS = 4


def _row_tile(lp, row_bytes):
    for k in range(ROW_MIN_STEPS, lp // 16 + 1):
        tr = lp // k
        if lp % k == 0 and tr % 16 == 0 and 2 * tr * row_bytes <= ROW_VMEM_BUDGET:
            return tr
    return lp


def _token_rows_copy(i, n_tiles, tr, n_tok, tok_ref, buf_ref, sem, *, to_tokens, start=True,
                     wait=True):
    assert n_tiles >= 2 and 0 < n_tok + N_META - (n_tiles - 1) * tr <= tr

    def run(tok_row, buf_row, count):
        tok = tok_ref.at[pl.ds(tok_row, count), :]
        buf = buf_ref.at[pl.ds(buf_row, count), :]
        cp = pltpu.make_async_copy(buf, tok, sem) if to_tokens else pltpu.make_async_copy(tok, buf, sem)
        if start:
            cp.start()
        if wait:
            cp.wait()

    @pl.when(i == 0)
    def _():
        run(0, N_META, tr - N_META)

    if n_tiles > 2:
        @pl.when((i > 0) & (i < n_tiles - 1))
        def _():
            run(pl.multiple_of(i * tr - N_META, 8), 0, tr)

    @pl.when(i == n_tiles - 1)
    def _():
        run((n_tiles - 1) * tr - N_META, 0, n_tok + N_META - (n_tiles - 1) * tr)


def _embed_norm(x, meta, g, *, lp, name):
    n_tok, d = x.shape
    tr = _row_tile(lp, d * (4 + 2))
    n_tiles = lp // tr

    def body(x_ref, meta_ref, g_ref, h_ref, o_ref, buf_ref, sem):
        i = pl.program_id(0)
        buf_ref[...] = jnp.zeros_like(buf_ref)

        @pl.when(i == 0)
        def _():
            buf_ref[0:N_META, :] = meta_ref[...]

        _token_rows_copy(i, n_tiles, tr, n_tok, x_ref, buf_ref, sem, to_tokens=False)
        xv = buf_ref[...]
        h_ref[...] = xv
        r = lax.rsqrt(jnp.mean(xv * xv, axis=-1, keepdims=True) + EPS)
        o_ref[...] = (xv * r * g_ref[...]).astype(BF16)

    return pl.pallas_call(
        body, grid=(n_tiles,),
        in_specs=[ANY, _fspec((N_META, d)), _fspec((1, d))],
        out_specs=[_rspec(tr, d), _rspec(tr, d)],
        out_shape=[jax.ShapeDtypeStruct((lp, d), F32), jax.ShapeDtypeStruct((lp, d), BF16)],
        scratch_shapes=[pltpu.VMEM((tr, d), F32), pltpu.SemaphoreType.DMA],
        compiler_params=_cp("parallel"), name=name)(x, meta, g)


def _residual_norm(h, t, g, *, name):
    lp, d = h.shape
    tr = _row_tile(lp, d * (4 + 4 + 4 + 2))

    def body(h_ref, t_ref, g_ref, h1_ref, m_ref):
        xv = h_ref[...] + t_ref[...]
        h1_ref[...] = xv
        r = lax.rsqrt(jnp.mean(xv * xv, axis=-1, keepdims=True) + EPS)
        m_ref[...] = (xv * r * g_ref[...]).astype(BF16)

    return pl.pallas_call(
        body, grid=(lp // tr,),
        in_specs=[_rspec(tr, d), _rspec(tr, d), _fspec((1, d))],
        out_specs=[_rspec(tr, d), _rspec(tr, d)],
        out_shape=[jax.ShapeDtypeStruct((lp, d), F32), jax.ShapeDtypeStruct((lp, d), BF16)],
        compiler_params=_cp("parallel"), name=name)(h, t, g)


def _rmsnorm_bwd_add(x, g, dy, dres, *, name):
    lp, d = x.shape
    tr = _row_tile(lp, d * (4 * 4 + 2))

    def body(x_ref, g_ref, dy_ref, dr_ref, dx_ref, dx16_ref, dg_ref):
        @pl.when(pl.program_id(0) == 0)
        def _():
            dg_ref[...] = jnp.zeros_like(dg_ref)

        xv = x_ref[...]
        r = lax.rsqrt(jnp.mean(xv * xv, axis=-1, keepdims=True) + EPS)
        xh = xv * r
        dyv = dy_ref[...]
        dg_ref[...] += jnp.sum(dyv * xh, axis=0, keepdims=True)
        dxh = dyv * g_ref[...]
        dx = dr_ref[...] + r * (dxh - xh * jnp.mean(dxh * xh, axis=-1, keepdims=True))
        dx_ref[...] = dx
        dx16_ref[...] = dx.astype(BF16)

    return pl.pallas_call(
        body, grid=(lp // tr,),
        in_specs=[_rspec(tr, d), _fspec((1, d)), _rspec(tr, d), _rspec(tr, d)],
        out_specs=[_rspec(tr, d), _rspec(tr, d), _fspec((1, d))],
        out_shape=[jax.ShapeDtypeStruct((lp, d), F32), jax.ShapeDtypeStruct((lp, d), BF16),
                   jax.ShapeDtypeStruct((1, d), F32)],
        compiler_params=_cp("arbitrary"), name=name)(x, g, dy, dres)


def _rmsnorm_bwd_tokens(x, g, dy, dres, *, n_tok, name):
    lp, d = x.shape
    tr = _row_tile(lp, d * 4 * 4)
    n_tiles = lp // tr

    def body(x_ref, g_ref, dy_ref, dr_ref, dtok_ref, dmeta_ref, dg_ref, buf_ref, sem):
        i = pl.program_id(0)

        @pl.when(i == 0)
        def _():
            dg_ref[...] = jnp.zeros_like(dg_ref)

        xv = x_ref[...]
        r = lax.rsqrt(jnp.mean(xv * xv, axis=-1, keepdims=True) + EPS)
        xh = xv * r
        dyv = dy_ref[...]
        dg_ref[...] += jnp.sum(dyv * xh, axis=0, keepdims=True)
        dxh = dyv * g_ref[...]
        buf_ref[...] = dr_ref[...] + r * (dxh - xh * jnp.mean(dxh * xh, axis=-1, keepdims=True))

        @pl.when(i == 0)
        def _():
            dmeta_ref[...] = buf_ref[0:N_META, :]

        _token_rows_copy(i, n_tiles, tr, n_tok, dtok_ref, buf_ref, sem, to_tokens=True)

    return pl.pallas_call(
        body, grid=(n_tiles,),
        in_specs=[_rspec(tr, d), _fspec((1, d)), _rspec(tr, d), _rspec(tr, d)],
        out_specs=[ANY, _fspec((N_META, d)), _fspec((1, d))],
        out_shape=[jax.ShapeDtypeStruct((n_tok, d), F32), jax.ShapeDtypeStruct((N_META, d), F32),
                   jax.ShapeDtypeStruct((1, d), F32)],
        scratch_shapes=[pltpu.VMEM((tr, d), F32), pltpu.SemaphoreType.DMA],
        compiler_params=_cp("arbitrary"), name=name)(x, g, dy, dres)


def _final_loss(h1, t2, g, tgt, *, name):
    lp, d = h1.shape
    n_tok = tgt.shape[0]
    tr = _row_tile(lp, d * 4 * 4)
    n_tiles = lp // tr

    def body(h_ref, t_ref, g_ref, tgt_ref, dh_ref, dh16_ref, loss_ref, dg_ref, tg_ref, sem):
        i = pl.program_id(0)

        @pl.when(i == 0)
        def _():
            loss_ref[...] = jnp.zeros_like(loss_ref)
            dg_ref[...] = jnp.zeros_like(dg_ref)
            tg_ref[...] = jnp.zeros_like(tg_ref)

        _token_rows_copy(i, n_tiles, tr, n_tok, tgt_ref, tg_ref, sem, to_tokens=False, wait=False)
        xv = h_ref[...] + t_ref[...]
        r = lax.rsqrt(jnp.mean(xv * xv, axis=-1, keepdims=True) + EPS)
        xh = xv * r
        gv = g_ref[...]
        _token_rows_copy(i, n_tiles, tr, n_tok, tgt_ref, tg_ref, sem, to_tokens=False, start=False)
        row = i * tr + lax.broadcasted_iota(jnp.int32, (tr, 1), 0)
        valid = (row >= N_META) & (row < N_META + n_tok)
        err = jnp.where(valid, xh * gv - tg_ref[...], 0.0)
        loss_ref[...] += jnp.sum(0.5 * err * err) / d
        dy = err / d
        dg_ref[...] += jnp.sum(dy * xh, axis=0, keepdims=True)
        dxh = dy * gv
        dh = r * (dxh - xh * jnp.mean(dxh * xh, axis=-1, keepdims=True))
        dh_ref[...] = dh
        dh16_ref[...] = dh.astype(BF16)

    return pl.pallas_call(
        body, grid=(n_tiles,),
        in_specs=[_rspec(tr, d), _rspec(tr, d), _fspec((1, d)), ANY],
        out_specs=[_rspec(tr, d), _rspec(tr, d), _fspec((1, LANES)), _fspec((1, d))],
        out_shape=[jax.ShapeDtypeStruct((lp, d), F32), jax.ShapeDtypeStruct((lp, d), BF16),
                   jax.ShapeDtypeStruct((1, LANES), F32), jax.ShapeDtypeStruct((1, d), F32)],
        scratch_shapes=[pltpu.VMEM((tr, d), F32), pltpu.SemaphoreType.DMA],
        compiler_params=_cp("arbitrary"), name=name)(h1, t2, g, tgt)


def _hg_out(o_f, o_b, proj, gain, *, col_g, name):
    lp, w = o_f.shape
    tr = _row_tile(lp, w * (3 * 4 + 2))
    hh = w // HG_DK

    def body(of_ref, ob_ref, g_ref, gain_ref, y_ref):
        gv = g_ref[...]
        sg = gv * _sigmoid(gv)
        for h in range(hh):
            sl = slice(h * HG_DK, (h + 1) * HG_DK)
            o = of_ref[:, sl] + ob_ref[:, sl]
            r = lax.rsqrt(jnp.mean(o * o, axis=-1, keepdims=True) + EPS)
            y_ref[:, sl] = (o * r * gain_ref[:, sl] * sg[:, sl]).astype(BF16)

    return pl.pallas_call(
        body, grid=(lp // tr,),
        in_specs=[_rspec(tr, w), _rspec(tr, w), _rspec(tr, w, col_g // w), _fspec((1, w))],
        out_specs=_rspec(tr, w),
        out_shape=jax.ShapeDtypeStruct((lp, w), BF16),
        compiler_params=_cp("parallel"), name=name)(o_f, o_b, proj, gain)


def _hg_out_bwd(o_f, o_b, proj, gain, dy, *, col_g, name):
    lp, w = o_f.shape
    tr = _row_tile(lp, w * (5 * 4 + 2))
    hh = w // HG_DK

    def body(of_ref, ob_ref, g_ref, gain_ref, dy_ref, do_ref, dg_ref, dgain_ref):
        @pl.when(pl.program_id(0) == 0)
        def _():
            dgain_ref[...] = jnp.zeros_like(dgain_ref)

        for h in range(hh):
            sl = slice(h * HG_DK, (h + 1) * HG_DK)
            gv = g_ref[:, sl]
            s = _sigmoid(gv)
            sg = gv * s
            dsg = s + gv * s * (1.0 - s)
            o = of_ref[:, sl] + ob_ref[:, sl]
            r = lax.rsqrt(jnp.mean(o * o, axis=-1, keepdims=True) + EPS)
            on = o * r
            dyv = dy_ref[:, sl]
            gn = gain_ref[:, sl]
            dgain_ref[:, sl] += jnp.sum(dyv * on * sg, axis=0, keepdims=True)
            dg_ref[:, sl] = (dyv * on * gn * dsg).astype(BF16)
            don = dyv * gn * sg
            do_ref[:, sl] = r * (don - on * jnp.mean(don * on, axis=-1, keepdims=True))

    return pl.pallas_call(
        body, grid=(lp // tr,),
        in_specs=[_rspec(tr, w), _rspec(tr, w), _rspec(tr, w, col_g // w), _fspec((1, w)),
                  _rspec(tr, w)],
        out_specs=[_rspec(tr, w), _rspec(tr, w), _fspec((1, w))],
        out_shape=[jax.ShapeDtypeStruct((lp, w), F32), jax.ShapeDtypeStruct((lp, w), BF16),
                   jax.ShapeDtypeStruct((1, w), F32)],
        compiler_params=_cp("arbitrary"), name=name)(o_f, o_b, proj, gain, dy)


HG_ROWS = 128
HG_HALVES = (1, 2, 4, 8, 16, 32, 64)


def _hg_gates(zq, z, lbv):
    sq = _sigmoid(zq)
    s = _sigmoid(z)
    f = lbv + (1.0 - lbv) * s
    kk = (1.0 - lbv) * (1.0 - s)
    return zq * sq, sq, s, f, jnp.log(f), kk


def _block_cumsum(g, pos, suffix):
    x = g
    for k in HG_HALVES:
        if suffix:
            x = x + jnp.where(pos < HG_ROWS - k, pltpu.roll(x, HG_ROWS - k, 0), 0.0)
        else:
            x = x + jnp.where(pos >= k, pltpu.roll(x, k, 0), 0.0)
    return x


def _pair_levels(b, pos, reverse):
    out = []
    first = b
    for m in HG_HALVES:
        if m > 1:
            first = jnp.where((pos & (m - 1)) >= m // 2, pltpu.roll(first, m // 2, 0), first)
        nxt = pltpu.roll(first, HG_ROWS - m, 0)
        upper = (pos & (2 * m - 1)) >= m
        if reverse:
            eq = jnp.where(upper, 0.0, jnp.exp(b - nxt))
            ek = jnp.where(upper, jnp.exp(first - b), 0.0)
        else:
            eq = jnp.where(upper, jnp.exp(b - first), 0.0)
            ek = jnp.where(upper, 0.0, jnp.exp(nxt - b))
        out.append((eq, ek))
    return out


def _pair_masks(mask_ref):
    ri = lax.broadcasted_iota(jnp.int32, (HG_ROWS, HG_ROWS), 0)
    ci = lax.broadcasted_iota(jnp.int32, (HG_ROWS, HG_ROWS), 1)
    for i, m in enumerate(HG_HALVES):
        sh = m.bit_length()
        mask_ref[i] = jnp.where((ri >> sh) == (ci >> sh), 1.0, 0.0)


def _hg_scan_fwd(proj, lb, *, reverse, col_q, col_z, col_i, hh, name):
    lp = proj.shape[0]
    n_blocks = lp // HG_ROWS
    last = 0 if reverse else HG_ROWS - 1

    def body(q_ref, z_ref, i_ref, lb_ref, o_ref, st_ref, mask_ref):
        lbv = lb_ref[...]
        pos = lax.broadcasted_iota(jnp.int32, (HG_ROWS, 1), 0)
        ri = lax.broadcasted_iota(jnp.int32, (HG_ROWS, HG_ROWS), 0)
        ci = lax.broadcasted_iota(jnp.int32, (HG_ROWS, HG_ROWS), 1)
        _pair_masks(mask_ref)

        def block(bi, st):
            bb = (n_blocks - 1 - bi) if reverse else bi
            r0 = pl.multiple_of(bb * HG_ROWS, HG_ROWS)
            v16 = i_ref[pl.ds(r0, HG_ROWS), :].astype(BF16)
            qh, _, _, _, g, kk = _hg_gates(q_ref[pl.ds(r0, HG_ROWS), :],
                                           z_ref[pl.ds(r0, HG_ROWS), :], lbv)
            b = _block_cumsum(g, pos, reverse)
            bl = b[last:last + 1, :]
            qe = (qh * jnp.exp(b)).astype(BF16)
            kd = (kk * jnp.exp(bl - b)).astype(BF16)
            a = jnp.where(ri == ci, jnp.sum(qh * kk, axis=1, keepdims=True), 0.0)
            for i, (eq, ek) in enumerate(_pair_levels(b, pos, reverse)):
                a = a + mask_ref[i] * _nt((qh * eq).astype(BF16), (kk * ek).astype(BF16))
            st_ref[bb] = st
            o_ref[pl.ds(r0, HG_ROWS), :] = _nn(a.astype(BF16), v16) + _nt(qe, st.astype(BF16))
            return jnp.exp(bl) * st + _tn(v16, kd)

        lax.fori_loop(0, n_blocks, block, jnp.zeros((HG_DK, HG_DK), F32))

    cspec = lambda col: pl.BlockSpec((lp, HG_DK), lambda h: (0, col // HG_DK + h))
    return pl.pallas_call(
        body, grid=(hh,),
        in_specs=[cspec(col_q), cspec(col_z), cspec(col_i),
                  pl.BlockSpec((None, 1, HG_DK), lambda h: (h, 0, 0))],
        out_specs=[pl.BlockSpec((lp, HG_DK), lambda h: (0, h)),
                   pl.BlockSpec((None, n_blocks, HG_DK, HG_DK), lambda h: (h, 0, 0, 0))],
        out_shape=[jax.ShapeDtypeStruct((lp, hh * HG_DK), F32),
                   jax.ShapeDtypeStruct((hh, n_blocks, HG_DK, HG_DK), F32)],
        scratch_shapes=[pltpu.VMEM((len(HG_HALVES), HG_ROWS, HG_ROWS), F32)],
        compiler_params=_cp("parallel"), name=name)(proj, proj, proj, lb)


def _hg_scan_bwd(proj, lb, states, do, *, reverse, col_q, col_z, col_i, hh, name):
    lp = proj.shape[0]
    n_blocks = lp // HG_ROWS
    last = 0 if reverse else HG_ROWS - 1

    def body(q_ref, z_ref, i_ref, lb_ref, st_ref, do_ref, dq_ref, dz_ref, dv_ref, dlb_ref, mask_ref):
        lbv = lb_ref[...]
        pos = lax.broadcasted_iota(jnp.int32, (HG_ROWS, 1), 0)
        ri = lax.broadcasted_iota(jnp.int32, (HG_ROWS, HG_ROWS), 0)
        ci = lax.broadcasted_iota(jnp.int32, (HG_ROWS, HG_ROWS), 1)
        _pair_masks(mask_ref)

        def block(bi, carry):
            dst, dlb = carry
            bb = bi if reverse else (n_blocks - 1 - bi)
            r0 = pl.multiple_of(bb * HG_ROWS, HG_ROWS)
            zq = q_ref[pl.ds(r0, HG_ROWS), :]
            v16 = i_ref[pl.ds(r0, HG_ROWS), :].astype(BF16)
            do16 = do_ref[pl.ds(r0, HG_ROWS), :].astype(BF16)
            qh, sq, s, f, g, kk = _hg_gates(zq, z_ref[pl.ds(r0, HG_ROWS), :], lbv)
            b = _block_cumsum(g, pos, reverse)
            bl = b[last:last + 1, :]
            eb = jnp.exp(b)
            ebl = jnp.exp(bl - b)
            decay = jnp.exp(bl)
            qe16 = (qh * eb).astype(BF16)
            kd16 = (kk * ebl).astype(BF16)
            st = st_ref[bb]
            st16, dst16 = st.astype(BF16), dst.astype(BF16)
            same_row = ri == ci
            da = _nt(do16, v16)
            da_diag = jnp.sum(jnp.where(same_row, da, 0.0), axis=1, keepdims=True)
            dq_state = eb * _nn(do16, st16)
            dk_state = ebl * _nn(v16, dst16)
            dq = dq_state + da_diag * kk
            dk = dk_state + da_diag * qh
            dbl = (decay * jnp.sum(st * dst, axis=0, keepdims=True)
                   + jnp.sum(kk * dk_state, axis=0, keepdims=True))
            db = qh * dq_state - kk * dk_state + jnp.where(pos == last, dbl, 0.0)
            a = jnp.where(same_row, jnp.sum(qh * kk, axis=1, keepdims=True), 0.0)
            for i, (eq, ek) in enumerate(_pair_levels(b, pos, reverse)):
                same = mask_ref[i]
                q16, k16 = (qh * eq).astype(BF16), (kk * ek).astype(BF16)
                a = a + same * _nt(q16, k16)
                da16 = (same * da).astype(BF16)
                gq, gk = _nn(da16, k16), _tn(da16, q16)
                dq = dq + eq * gq
                dk = dk + ek * gk
                db = db + (q16.astype(F32) * gq - k16.astype(F32) * gk)
            dg = _block_cumsum(db, pos, not reverse)
            df = dg / f - dk
            dq_ref[pl.ds(r0, HG_ROWS), :] = dq * (sq + zq * sq * (1.0 - sq))
            dz_ref[pl.ds(r0, HG_ROWS), :] = df * (1.0 - lbv) * s * (1.0 - s)
            dv_ref[pl.ds(r0, HG_ROWS), :] = _nt(kd16, dst16) + _tn(a.astype(BF16), do16)
            return (decay * dst + _tn(do16, qe16),
                    dlb + jnp.sum(df * (1.0 - s), axis=0, keepdims=True))

        _, dlb = lax.fori_loop(0, n_blocks, block,
                               (jnp.zeros((HG_DK, HG_DK), F32), jnp.zeros((1, HG_DK), F32)))
        dlb_ref[...] = dlb

    cspec = lambda col: pl.BlockSpec((lp, HG_DK), lambda h: (0, col // HG_DK + h))
    ospec = pl.BlockSpec((lp, HG_DK), lambda h: (0, h))
    sds = jax.ShapeDtypeStruct((lp, hh * HG_DK), F32)
    return pl.pallas_call(
        body, grid=(hh,),
        in_specs=[cspec(col_q), cspec(col_z), cspec(col_i),
                  pl.BlockSpec((None, 1, HG_DK), lambda h: (h, 0, 0)),
                  pl.BlockSpec((None, n_blocks, HG_DK, HG_DK), lambda h: (h, 0, 0, 0)),
                  ospec],
        out_specs=[ospec, ospec, ospec, pl.BlockSpec((None, 1, HG_DK), lambda h: (h, 0, 0))],
        out_shape=[sds, sds, sds, jax.ShapeDtypeStruct((hh, 1, HG_DK), F32)],
        scratch_shapes=[pltpu.VMEM((len(HG_HALVES), HG_ROWS, HG_ROWS), F32)],
        compiler_params=_cp("parallel"), name=name)(proj, proj, proj, lb, states, do)


NA_HB = LANES // NA_HEAD_DIM
NA_G = 4
NA_U = NA_G + NA_WIN_H
NA_QN = NA_G * GRID_W
NA_KN = NA_U * GRID_W


def _na_table_index(pattern, a, j):
    if pattern == 0:
        return j - a + NA_WIN_H - 1 if j < NA_WIN_H else None
    if pattern == 2:
        return j - a - 1 if j >= NA_U - NA_WIN_H else None
    return j - a + NA_WIN_H // 2 - 1 if a <= j < a + NA_WIN_H else None


def _na_step_rows(pattern, t, rows):
    if pattern == 0:
        r0, us = 0, 0
    elif pattern == 2:
        r0, us = rows - NA_G, rows - NA_U
    else:
        r0 = NA_G * t
        us = r0 - NA_WIN_H // 2
    q0, k0 = N_META + GRID_W * r0, N_META + GRID_W * us
    if pattern == 1:
        q0, k0 = pl.multiple_of(q0, 16), pl.multiple_of(k0, 16)
    return q0, k0


def _na_fill_bias(tb_ref, bias_ref):
    neg = jnp.full((GRID_W, GRID_W), -1e30, F32)
    for h in range(NA_HB):
        for pattern in range(3):
            for a in range(NA_G):
                for j in range(NA_U):
                    idx = _na_table_index(pattern, a, j)
                    bias_ref[h, pattern, a * GRID_W:(a + 1) * GRID_W, j * GRID_W:(j + 1) * GRID_W] = (
                        neg if idx is None else tb_ref[h, idx])


def _na_steps(rows, step, carry):
    n_steps = rows // NA_G
    carry = step(0, 0, carry)
    carry = lax.fori_loop(1, n_steps - 1, functools.partial(step, 1), carry)
    return step(2, n_steps - 1, carry)


def _na_head_lanes():
    lane = lax.broadcasted_iota(jnp.int32, (1, LANES), 1)
    return [lane // NA_HEAD_DIM == h for h in range(NA_HB)]


def _na_only(mask, x):
    return jnp.where(mask, x, jnp.zeros_like(x))


def _na_fwd(proj, tb, *, n_tok, nh, name):
    lp = proj.shape[0]
    dh, hb = NA_HEAD_DIM, NA_HB
    naw = nh * dh
    rows = n_tok // GRID_W
    scale = dh ** -0.5

    def body(q_ref, k_ref, v_ref, tb_ref, o_ref, lse_ref, q16_ref, k16_ref, v16_ref, bias_ref):
        o_ref[...] = jnp.zeros_like(o_ref)
        lse_ref[...] = jnp.zeros_like(lse_ref)
        q16_ref[...] = q_ref[...].astype(BF16)
        k16_ref[...] = k_ref[...].astype(BF16)
        v16_ref[...] = v_ref[...].astype(BF16)
        _na_fill_bias(tb_ref, bias_ref)
        heads = _na_head_lanes()
        km = k16_ref[0:N_META, :]
        vm = v16_ref[0:N_META, :]
        qm = q16_ref[0:N_META, :]
        o_m = None
        for h in range(hb):
            s = _nt(_na_only(heads[h], qm), km) * scale
            m = jnp.max(s, axis=1, keepdims=True)
            p = jnp.exp(s - m)
            l = jnp.sum(p, axis=1, keepdims=True)
            o_h = _nn(p.astype(BF16), vm) / l
            o_m = o_h if o_m is None else jnp.where(heads[h], o_h, o_m)
            lse_ref[h, 0:N_META, :] = m + jnp.log(l)
        o_ref[0:N_META, :] = o_m

        def step(pattern, t, carry):
            q0, k0 = _na_step_rows(pattern, t, rows)
            q16 = q16_ref[pl.ds(q0, NA_QN), :]
            k16 = k16_ref[pl.ds(k0, NA_KN), :]
            v16 = v16_ref[pl.ds(k0, NA_KN), :]
            o = None
            for h in range(hb):
                q_h = _na_only(heads[h], q16)
                s = _nt(q_h, k16) * scale + bias_ref[h, pattern]
                sm = _nt(q_h, km) * scale
                m = jnp.maximum(jnp.max(s, axis=1, keepdims=True),
                                jnp.max(sm, axis=1, keepdims=True))
                p = jnp.exp(s - m)
                pm = jnp.exp(sm - m)
                l = jnp.sum(p, axis=1, keepdims=True) + jnp.sum(pm, axis=1, keepdims=True)
                o_h = (_nn(p.astype(BF16), v16) + _nn(pm.astype(BF16), vm)) / l
                o = o_h if o is None else jnp.where(heads[h], o_h, o)
                lse_ref[h, pl.ds(q0, NA_QN), :] = m + jnp.log(l)
            o_ref[pl.ds(q0, NA_QN), :] = o
            return carry

        _na_steps(rows, step, 0)

    cblk = lambda col: pl.BlockSpec((lp, LANES), lambda g: (0, col // LANES + g))
    return pl.pallas_call(
        body, grid=(nh // hb,),
        in_specs=[cblk(0), cblk(naw), cblk(2 * naw),
                  pl.BlockSpec((hb, 2 * NA_WIN_H - 1, GRID_W, GRID_W), lambda g: (g, 0, 0, 0))],
        out_specs=[cblk(0), pl.BlockSpec((hb, lp, 1), lambda g: (g, 0, 0))],
        out_shape=[jax.ShapeDtypeStruct((lp, naw), F32), jax.ShapeDtypeStruct((nh, lp, 1), F32)],
        scratch_shapes=[pltpu.VMEM((lp, LANES), BF16)] * 3 + [pltpu.VMEM((hb, 3, NA_QN, NA_KN), F32)],
        compiler_params=_cp("parallel"), name=name)(proj, proj, proj, tb)


def _na_bwd(proj, tb, o, lse, do, *, n_tok, nh, name):
    lp = proj.shape[0]
    dh, hb = NA_HEAD_DIM, NA_HB
    naw = nh * dh
    rows = n_tok // GRID_W
    scale = dh ** -0.5

    def body(q_ref, k_ref, v_ref, tb_ref, o_ref, lse_ref, do_ref, dq_ref, dk_ref, dv_ref, dtb_ref,
             q16_ref, k16_ref, v16_ref, bias_ref):
        dq_ref[...] = jnp.zeros_like(dq_ref)
        dk_ref[...] = jnp.zeros_like(dk_ref)
        dv_ref[...] = jnp.zeros_like(dv_ref)
        dtb_ref[...] = jnp.zeros_like(dtb_ref)
        q16_ref[...] = q_ref[...].astype(BF16)
        k16_ref[...] = k_ref[...].astype(BF16)
        v16_ref[...] = v_ref[...].astype(BF16)
        _na_fill_bias(tb_ref, bias_ref)
        heads = _na_head_lanes()
        km = k16_ref[0:N_META, :]
        vm = v16_ref[0:N_META, :]
        qm = q16_ref[0:N_META, :]
        dom = do_ref[0:N_META, :]
        prod = dom * o_ref[0:N_META, :]
        dq_m = None
        dkm0 = jnp.zeros((N_META, LANES), F32)
        dvm0 = jnp.zeros((N_META, LANES), F32)
        for h in range(hb):
            q_h = _na_only(heads[h], qm)
            do_h = _na_only(heads[h], dom).astype(BF16)
            p = jnp.exp(_nt(q_h, km) * scale - lse_ref[h, 0:N_META, :])
            delta = jnp.sum(_na_only(heads[h], prod), axis=1, keepdims=True)
            ds = (p * (_nt(do_h, vm) - delta)).astype(BF16)
            dq_h = _nn(ds, km) * scale
            dq_m = dq_h if dq_m is None else jnp.where(heads[h], dq_h, dq_m)
            dkm0 = dkm0 + _tn(ds, q_h) * scale
            dvm0 = dvm0 + _tn(p.astype(BF16), do_h)
        dq_ref[0:N_META, :] = dq_m

        def step(pattern, t, carry):
            dkm, dvm = carry
            q0, k0 = _na_step_rows(pattern, t, rows)
            q16 = q16_ref[pl.ds(q0, NA_QN), :]
            k16 = k16_ref[pl.ds(k0, NA_KN), :]
            v16 = v16_ref[pl.ds(k0, NA_KN), :]
            dov = do_ref[pl.ds(q0, NA_QN), :]
            prod = dov * o_ref[pl.ds(q0, NA_QN), :]
            dq = None
            dk = jnp.zeros((NA_KN, LANES), F32)
            dv = jnp.zeros((NA_KN, LANES), F32)
            for h in range(hb):
                q_h = _na_only(heads[h], q16)
                do_h = _na_only(heads[h], dov).astype(BF16)
                lse = lse_ref[h, pl.ds(q0, NA_QN), :]
                p = jnp.exp(_nt(q_h, k16) * scale + bias_ref[h, pattern] - lse)
                pm = jnp.exp(_nt(q_h, km) * scale - lse)
                delta = jnp.sum(_na_only(heads[h], prod), axis=1, keepdims=True)
                ds = p * (_nt(do_h, v16) - delta)
                dsm = (pm * (_nt(do_h, vm) - delta)).astype(BF16)
                ds16 = ds.astype(BF16)
                dq_h = (_nn(ds16, k16) + _nn(dsm, km)) * scale
                dq = dq_h if dq is None else jnp.where(heads[h], dq_h, dq)
                dk = dk + _tn(ds16, q_h) * scale
                dv = dv + _tn(p.astype(BF16), do_h)
                dkm = dkm + _tn(dsm, q_h) * scale
                dvm = dvm + _tn(pm.astype(BF16), do_h)
                for a in range(NA_G):
                    for j in range(NA_U):
                        idx = _na_table_index(pattern, a, j)
                        if idx is not None:
                            dtb_ref[h, idx] += ds[a * GRID_W:(a + 1) * GRID_W,
                                                  j * GRID_W:(j + 1) * GRID_W]
            dq_ref[pl.ds(q0, NA_QN), :] = dq
            dk_ref[pl.ds(k0, NA_KN), :] += dk
            dv_ref[pl.ds(k0, NA_KN), :] += dv
            return dkm, dvm

        dkm, dvm = _na_steps(rows, step, (dkm0, dvm0))
        dk_ref[0:N_META, :] += dkm
        dv_ref[0:N_META, :] += dvm

    cblk = lambda col: pl.BlockSpec((lp, LANES), lambda g: (0, col // LANES + g))
    tbs = pl.BlockSpec((hb, 2 * NA_WIN_H - 1, GRID_W, GRID_W), lambda g: (g, 0, 0, 0))
    sds = jax.ShapeDtypeStruct((lp, naw), F32)
    return pl.pallas_call(
        body, grid=(nh // hb,),
        in_specs=[cblk(0), cblk(naw), cblk(2 * naw), tbs, cblk(0),
                  pl.BlockSpec((hb, lp, 1), lambda g: (g, 0, 0)), cblk(0)],
        out_specs=[cblk(0), cblk(0), cblk(0), tbs],
        out_shape=[sds, sds, sds, jax.ShapeDtypeStruct(tb.shape, F32)],
        scratch_shapes=[pltpu.VMEM((lp, LANES), BF16)] * 3 + [pltpu.VMEM((hb, 3, NA_QN, NA_KN), F32)],
        compiler_params=_cp("parallel"), name=name)(proj, proj, proj, tb, o, lse, do)


def _rpb_onehot():
    c = np.arange(GRID_W)[:, None]
    w = np.arange(GRID_W)[None, :]
    cs = np.clip(c - NA_WIN_W // 2, 0, GRID_W - NA_WIN_W)
    in_win = (w >= cs) & (w < cs + NA_WIN_W)
    dc = np.clip(w - c, -(NA_WIN_W - 1), NA_WIN_W - 1) + NA_WIN_W - 1
    oh = np.zeros((LANES, GRID_W * GRID_W), np.float32)
    flat = np.arange(GRID_W * GRID_W).reshape(GRID_W, GRID_W)
    oh[dc[in_win], flat[in_win]] = 1.0
    neg = np.where(in_win, 0.0, -1e30).astype(np.float32).reshape(1, -1)
    return oh, neg


def _assemble_dproj(dq_na, dk_na, dv_na, dq_f, dq_b, dz_f, dz_b, dv_f, dv_b, dg, dgn, dgh, *, name):
    lp, naw = dq_na.shape
    hgw = dq_f.shape[1]
    d = dgn.shape[1]
    cols = 3 * naw + 5 * hgw + 2 * d
    tr = _row_tile(lp, 3 * naw * 4 + 6 * hgw * 4 + hgw * 2 + 2 * d * 2 + cols * 2)

    def body(nq_ref, nk_ref, nv_ref, qf_ref, qb_ref, zf_ref, zb_ref, vf_ref, vb_ref, g_ref, gn_ref,
             gh_ref, o_ref):
        o_ref[:, 0:naw] = nq_ref[...].astype(BF16)
        o_ref[:, naw:2 * naw] = nk_ref[...].astype(BF16)
        o_ref[:, 2 * naw:3 * naw] = nv_ref[...].astype(BF16)
        c0 = 3 * naw
        o_ref[:, c0:c0 + hgw] = (qf_ref[...] + qb_ref[...]).astype(BF16)
        o_ref[:, c0 + hgw:c0 + 2 * hgw] = zf_ref[...].astype(BF16)
        o_ref[:, c0 + 2 * hgw:c0 + 3 * hgw] = zb_ref[...].astype(BF16)
        o_ref[:, c0 + 3 * hgw:c0 + 4 * hgw] = (vf_ref[...] + vb_ref[...]).astype(BF16)
        o_ref[:, c0 + 4 * hgw:c0 + 5 * hgw] = g_ref[...]
        o_ref[:, c0 + 5 * hgw:c0 + 5 * hgw + d] = gn_ref[...]
        o_ref[:, c0 + 5 * hgw + d:] = gh_ref[...]

    hg, na = _rspec(tr, hgw), _rspec(tr, naw)
    return pl.pallas_call(
        body, grid=(lp // tr,),
        in_specs=[na, na, na, hg, hg, hg, hg, hg, hg, hg, _rspec(tr, d), _rspec(tr, d)],
        out_specs=_rspec(tr, cols),
        out_shape=jax.ShapeDtypeStruct((lp, cols), BF16),
        compiler_params=_cp("parallel"), name=name)(dq_na, dk_na, dv_na, dq_f, dq_b, dz_f, dz_b,
                                                    dv_f, dv_b, dg, dgn, dgh)


GROUP_STEPS = 8


def _group_tiles(rows, align):
    steps = GROUP_STEPS if all(r % (GROUP_STEPS * align) == 0 for r in rows) else 1
    return steps, [r // steps for r in rows]


def _adamw(ws, gs, ms, vs, *, name):
    n = len(ws)
    steps, trs = _group_tiles([w.shape[0] for w in ws], 8)

    def body(*refs):
        for i in range(n):
            w_ref, g_ref, m_ref, v_ref = refs[4 * i:4 * i + 4]
            d_ref, mo_ref, vo_ref = refs[4 * n + 3 * i:4 * n + 3 * i + 3]
            gv = g_ref[...]
            mn = ADAM_B1 * m_ref[...] + (1.0 - ADAM_B1) * gv
            vn = ADAM_B2 * v_ref[...] + (1.0 - ADAM_B2) * (gv * gv)
            m_hat = mn / (1.0 - ADAM_B1 ** ADAM_STEP)
            v_hat = vn / (1.0 - ADAM_B2 ** ADAM_STEP)
            d_ref[...] = -ADAM_LR * (m_hat / (jnp.sqrt(v_hat) + ADAM_EPS) + ADAM_WD * w_ref[...])
            mo_ref[...] = mn
            vo_ref[...] = vn

    specs = [_rspec(tr, w.shape[1]) for tr, w in zip(trs, ws)]
    out = pl.pallas_call(
        body, grid=(steps,),
        in_specs=[s for s in specs for _ in range(4)],
        out_specs=[s for s in specs for _ in range(3)],
        out_shape=[jax.ShapeDtypeStruct(w.shape, F32) for w in ws for _ in range(3)],
        compiler_params=_cp("parallel"), name=name)(*[a for q in zip(ws, gs, ms, vs) for a in q])
    return [tuple(out[3 * i:3 * i + 3]) for i in range(n)]


def _local_step(x, tgt, meta, first_weight, rest_weights, g_mix, g_mlp, g_fin, hg_gain, rpb, lb,
                early_grads=None, mid_grads=None, late_grad=None, rest_landed=None,
                last_grads=None):
    n_tok, d = x.shape
    hgw = hg_gain.shape[1]
    nh, hh = rpb.shape[0], hgw // HG_DK
    naw = nh * NA_HEAD_DIM
    l_real = N_META + n_tok
    lp = -(-l_real // ROW_ALIGN) * ROW_ALIGN
    col_qhg = 3 * naw
    col_zf, col_zb, col_i, col_g = (col_qhg + hgw, col_qhg + 2 * hgw, col_qhg + 3 * hgw,
                                    col_qhg + 4 * hgw)
    col_gate = col_qhg + 5 * hgw

    oh_np, neg_np = _rpb_onehot()
    oh = jnp.asarray(oh_np)
    rpb_p = jnp.pad(rpb.reshape(nh * (2 * NA_WIN_H - 1), 2 * NA_WIN_W - 1),
                    ((0, 0), (0, LANES - (2 * NA_WIN_W - 1))))
    tb = _matmul(rpb_p, oh, tm=rpb_p.shape[0], tn=512, tk=LANES, precision=HIGHEST,
                 name="rpb_expand")
    tb = (tb + jnp.asarray(neg_np)).reshape(nh, 2 * NA_WIN_H - 1, GRID_W, GRID_W)

    h0, a = _embed_norm(x, meta, g_mix, lp=lp, name="norm_mix")
    w_in = first_weight(a)
    proj = _matmul(a, w_in, name="mm_in")
    o_na, lse = _na_fwd(proj, tb, n_tok=n_tok, nh=nh, name="na_fwd")
    lb_f = lb[0].reshape(hh, 1, HG_DK)
    lb_b = lb[1].reshape(hh, 1, HG_DK)
    scan_kw = dict(col_q=col_qhg, col_i=col_i, hh=hh)
    o_f, st_f = _hg_scan_fwd(proj, lb_f, reverse=False, col_z=col_zf, name="hg_scan_f", **scan_kw)
    token = rest_landed(o_f) if rest_landed else None
    lb_b_late = lb_b if token is None else lb_b + token[0:1, 0:1]
    o_b, st_b = _hg_scan_fwd(proj, lb_b_late, reverse=True, col_z=col_zb, name="hg_scan_b",
                             **scan_kw)
    o_hg = _hg_out(o_f, o_b, proj, hg_gain, col_g=col_g, name="hg_out")
    w_na, w_hg, w_o, w_up, w_down = rest_weights(o_hg)
    y_na = _matmul(o_na, w_na, name="mm_na_out", out_dtype=BF16)
    gates = ((proj, col_gate), (proj, col_gate + d))

    def mix_gates(acc, gn, gh, yn):
        return acc, _sigmoid(gn) * yn + _sigmoid(gh) * acc

    def mix_gates_bwd(dmix, gn, gh, yn, yh):
        sn, sh = _sigmoid(gn), _sigmoid(gh)
        return dmix * sn, dmix * sh, dmix * yn * sn * (1.0 - sn), dmix * yh * sh * (1.0 - sh)

    y_hg, mix = _matmul(o_hg, w_hg, name="mm_hg_out", epilogue=mix_gates,
                        tiles=(*gates, (y_na, 0)), out_dtypes=(BF16, BF16))
    t1 = _matmul(mix, w_o, name="mm_o")
    h1, mlp_in = _residual_norm(h0, t1, g_mlp, name="resid_norm_mlp")
    u, act = _matmul(mlp_in, w_up, name="mm_up", out_dtypes=(BF16, BF16),
                     epilogue=lambda acc: (acc, jnp.square(jnp.maximum(acc, 0.0))))
    t2 = _matmul(act, w_down, name="mm_down")
    dh2, dh2_16, loss, dg_fin = _final_loss(h1, t2, g_fin, tgt, name="final_loss")

    (du,) = _matmul(dh2_16, w_down, tb=True, name="mm_down_dx", tiles=((u, 0),),
                    out_dtypes=(BF16,),
                    epilogue=lambda acc, uv: (acc * 2.0 * jnp.maximum(uv, 0.0),))
    dw_down = _matmul(act, dh2_16, ta=True, name="mm_down_dw")
    dm = _matmul(du, w_up, tb=True, name="mm_up_dx")
    dw_up = _matmul(mlp_in, du, ta=True, name="mm_up_dw")
    dh1, dh1_16, dg_mlp = _rmsnorm_bwd_add(h1, g_mlp, dm, dh2, name="norm_mlp_bwd")
    dy_na, dy_hg, dgn, dgh = _matmul(dh1_16, w_o, tb=True, name="mm_o_dx", epilogue=mix_gates_bwd,
                                     tiles=(*gates, (y_na, 0), (y_hg, 0)), out_dtypes=(BF16,) * 4)
    dw_o = _matmul(mix, dh1_16, ta=True, name="mm_o_dw")
    do_na = _matmul(dy_na, w_na, tb=True, name="mm_na_out_dx")
    dw_na = _matmul(o_na, dy_na, ta=True, name="mm_na_out_dw")
    do_hg = _matmul(dy_hg, w_hg, tb=True, name="mm_hg_out_dx")
    dw_hg = _matmul(o_hg, dy_hg, ta=True, name="mm_hg_out_dw")
    token = early_grads([dw_na, dw_hg, dw_o, dw_up, dw_down]) if early_grads else None
    if token is not None:
        hg_gain = hg_gain + token[0:1, 0:1]
    d_o, dg_hg, d_gain = _hg_out_bwd(o_f, o_b, proj, hg_gain, do_hg, col_g=col_g, name="hg_out_bwd")
    dq_f, dz_f, dv_f, dlb_f = _hg_scan_bwd(proj, lb_f, st_f, d_o, reverse=False, col_z=col_zf,
                                           name="hg_scan_f_bwd", **scan_kw)
    token = mid_grads(dq_f) if mid_grads else None
    lb_b_late = lb_b if token is None else lb_b + token[0:1, 0:1]
    dq_b, dz_b, dv_b, dlb_b = _hg_scan_bwd(proj, lb_b_late, st_b, d_o, reverse=True, col_z=col_zb,
                                           name="hg_scan_b_bwd", **scan_kw)
    dq_na, dk_na, dv_na, dtb = _na_bwd(proj, tb, o_na, lse, do_na, n_tok=n_tok, nh=nh, name="na_bwd")
    dproj = _assemble_dproj(dq_na, dk_na, dv_na, dq_f, dq_b, dz_f, dz_b, dv_f, dv_b, dg_hg, dgn,
                            dgh, name="assemble_dproj")
    dw_in = _matmul(a, dproj, ta=True, name="mm_in_dw")
    token = late_grad(dw_in) if late_grad else None
    da = _matmul(dproj, w_in, tb=True, name="mm_in_dx", after=token)
    token = last_grads(da) if last_grads else None
    g_mix_late = g_mix if token is None else g_mix + token[0:1, 0:1]
    dx, dmeta, dg_mix = _rmsnorm_bwd_tokens(h0, g_mix_late, da, dh1, n_tok=n_tok,
                                            name="norm_mix_bwd")
    d_rpb = _matmul(dtb.reshape(nh * (2 * NA_WIN_H - 1), GRID_W * GRID_W), oh, tb=True,
                    tm=nh * (2 * NA_WIN_H - 1), tn=LANES, tk=1024, precision=HIGHEST,
                    name="rpb_reduce")
    d_lb = jnp.concatenate([dlb_f.reshape(1, hgw), dlb_b.reshape(1, hgw)], axis=0)
    return (loss, dx, dmeta, dw_in, dw_na, dw_hg, dw_o, dw_up, dw_down,
            dg_mix, dg_mlp, dg_fin, d_gain, d_rpb, d_lb)


N_CHIPS = 4
N_DEV = 8
ANY = pl.BlockSpec(memory_space=pl.ANY)


def _place():
    x, y, c = lax.axis_index("x"), lax.axis_index("y"), lax.axis_index("c")
    others = []
    for j in (1, 2, 3):
        tx = (1 - x) if (j >> 1) else x
        ty = (1 - y) if (j & 1) else y
        others.append((tx, ty))
    return x, y, c, others


def _piece(ref, axis, k, half, rh, cs):
    if axis == 1:
        return ref.at[pl.ds(pl.multiple_of(half * rh, 16), rh), pl.ds(pl.multiple_of(k * cs, LANES), cs)]
    return ref.at[pl.ds(pl.multiple_of(k * 2 * rh + half * rh, 16), rh), :]


def _cast_into_full(shards, axes, place, *, name):
    n = len(shards)
    steps, trs = _group_tiles([s.shape[0] for s in shards], 16)

    def body(p_ref, *refs):
        for i in range(n):
            refs[n + i][...] = refs[i][...].astype(BF16)

    def out_spec(tr, cs, axis):
        if axis == 1:
            return pl.BlockSpec((tr, cs), lambda i, p_ref: (i, p_ref[0]))
        return pl.BlockSpec((tr, cs), lambda i, p_ref: (p_ref[0] * steps + i, 0))

    return pl.pallas_call(
        body,
        grid_spec=pltpu.PrefetchScalarGridSpec(
            num_scalar_prefetch=1, grid=(steps,),
            in_specs=[pl.BlockSpec((tr, s.shape[1]), lambda i, p_ref: (i, 0))
                      for tr, s in zip(trs, shards)],
            out_specs=[out_spec(tr, s.shape[1], ax) for tr, s, ax in zip(trs, shards, axes)]),
        out_shape=[jax.ShapeDtypeStruct((s.shape[0], s.shape[1] * N_CHIPS) if ax == 1
                                        else (s.shape[0] * N_CHIPS, s.shape[1]), BF16)
                   for s, ax in zip(shards, axes)],
        compiler_params=_cp("parallel"), name=name)(place, *shards)


HBM_SPEC = pl.BlockSpec(memory_space=pltpu.HBM)
SEM_SPEC = pl.BlockSpec(memory_space=pltpu.SEMAPHORE)
SPLIT_COPY = pltpu.CompilerParams(has_side_effects=pltpu.SideEffectType.DATAFLOW_SIDE_EFFECTING)
TOKEN = jax.ShapeDtypeStruct((8, LANES), F32)


def _geo(fulls, axes):
    out = []
    for f, ax in zip(fulls, axes):
        r, cs = (f.shape[0], f.shape[1] // N_CHIPS) if ax == 1 else (f.shape[0] // N_CHIPS, f.shape[1])
        out.append((ax, r // 2, cs))
    return out


def _gather_copies(refs, geo, send_sems, recv_sems):
    x, y, c, others = _place()
    chip = 2 * x + y
    cps = []
    for i, (ax, rh, cs) in enumerate(geo):
        mine = _piece(refs[i], ax, chip, c, rh, cs)
        for j, (tx, ty) in enumerate(others):
            cps.append(pltpu.make_async_remote_copy(
                src_ref=mine, dst_ref=mine, send_sem=send_sems.at[3 * i + j],
                recv_sem=recv_sems.at[3 * i + j], device_id=(tx, ty, c), device_id_type=MESH))
    return cps


def _allgather_start(fulls, axes, after, *, name):
    n = len(fulls)
    geo = _geo(fulls, axes)

    def body(*refs):
        w_refs = refs[:n]
        send_sems, recv_sems = refs[n + 1], refs[n + 2]
        token = refs[2 * n + 3]
        for cp in _gather_copies(w_refs, geo, send_sems, recv_sems):
            cp.start()
        token[...] = jnp.zeros_like(token)

    out = pl.pallas_call(
        body, name=name,
        out_shape=(pltpu.SemaphoreType.DMA((3 * n,)), pltpu.SemaphoreType.DMA((3 * n,)),
                   *[pltpu.HBM(f.shape, f.dtype) for f in fulls], TOKEN),
        in_specs=[HBM_SPEC] * n + [ANY],
        out_specs=(SEM_SPEC, SEM_SPEC, *[HBM_SPEC] * n, pl.BlockSpec(memory_space=pltpu.VMEM)),
        input_output_aliases={i: 2 + i for i in range(n)},
        compiler_params=SPLIT_COPY,
    )(*[pltpu.with_memory_space_constraint(f, pltpu.HBM) for f in fulls], after)
    return out[0], out[1], list(out[2:2 + n]), out[2 + n]


def _allgather_wait(send_sems, recv_sems, fulls, axes, after, *, name):
    n = len(fulls)
    geo = _geo(fulls, axes)

    def body(*refs):
        w_refs = refs[:n]
        for cp in _gather_copies(w_refs, geo, refs[n], refs[n + 1]):
            cp.wait_send()
            cp.wait_recv()

    return list(pl.pallas_call(
        body, name=name,
        out_shape=[pltpu.HBM(f.shape, f.dtype) for f in fulls],
        in_specs=[HBM_SPEC] * n + [SEM_SPEC, SEM_SPEC, ANY],
        out_specs=[HBM_SPEC] * n,
        input_output_aliases={i: i for i in range(n)},
        compiler_params=SPLIT_COPY,
    )(*fulls, send_sems, recv_sems, after))


def _allgather_forward(fulls, axes, *, name):
    n = len(fulls)
    geo = _geo(fulls, axes)

    def body(*refs):
        o_refs = refs[n:2 * n]
        send_sems, recv_sems = refs[2 * n:]
        x, y, c, others = _place()

        def rcopy(i, j, half, to):
            ax, rh, cs = geo[i]
            ref = _piece(o_refs[i], ax, 2 * others[j][0] + others[j][1], half, rh, cs)
            return pltpu.make_async_remote_copy(
                src_ref=ref, dst_ref=ref, send_sem=send_sems.at[3 * i + j],
                recv_sem=recv_sems.at[3 * i + j], device_id=to, device_id_type=MESH)

        cps = [rcopy(i, j, c, (x, y, 1 - c)) for i in range(n) for j in range(3)]
        for cp in cps:
            cp.start()
        for i in range(n):
            for j in range(3):
                rcopy(i, j, 1 - c, (x, y, c)).wait_recv()
        for cp in cps:
            cp.wait_send()

    return list(pl.pallas_call(
        body, in_specs=[ANY] * n, out_specs=[ANY] * n,
        out_shape=[jax.ShapeDtypeStruct(f.shape, f.dtype) for f in fulls],
        input_output_aliases={i: i for i in range(n)},
        scratch_shapes=[pltpu.SemaphoreType.DMA((3 * n,)), pltpu.SemaphoreType.DMA((3 * n,))],
        name=name)(*fulls))


def _forward_copies(refs, geo, send_sems, recv_sems):
    x, y, c, others = _place()
    cps = []
    for i, (ax, rh, cs) in enumerate(geo):
        for j, (tx, ty) in enumerate(others):
            ref = _piece(refs[i], ax, 2 * tx + ty, c, rh, cs)
            cps.append(pltpu.make_async_remote_copy(
                src_ref=ref, dst_ref=ref, send_sem=send_sems.at[3 * i + j],
                recv_sem=recv_sems.at[3 * i + j], device_id=(x, y, 1 - c), device_id_type=MESH))
    return cps


def _allgather_forward_start(fulls, axes, *, name):
    n = len(fulls)
    geo = _geo(fulls, axes)

    def body(*refs):
        token = refs[2 * n + 2]
        for cp in _forward_copies(refs[:n], geo, refs[n], refs[n + 1]):
            cp.start()
        token[...] = jnp.zeros_like(token)

    out = pl.pallas_call(
        body, name=name,
        out_shape=(pltpu.SemaphoreType.DMA((3 * n,)), pltpu.SemaphoreType.DMA((3 * n,)),
                   *[pltpu.HBM(f.shape, f.dtype) for f in fulls], TOKEN),
        in_specs=[HBM_SPEC] * n,
        out_specs=(SEM_SPEC, SEM_SPEC, *[HBM_SPEC] * n, pl.BlockSpec(memory_space=pltpu.VMEM)),
        input_output_aliases={i: 2 + i for i in range(n)},
        compiler_params=SPLIT_COPY,
    )(*fulls)
    return out[0], out[1], list(out[2:2 + n]), out[2 + n]


def _allgather_forward_wait(send_sems, recv_sems, fulls, axes, after, *, name):
    n = len(fulls)
    geo = _geo(fulls, axes)

    def body(*refs):
        for cp in _forward_copies(refs[:n], geo, refs[n], refs[n + 1]):
            cp.wait_send()
            cp.wait_recv()

    return list(pl.pallas_call(
        body, name=name,
        out_shape=[pltpu.HBM(f.shape, f.dtype) for f in fulls],
        in_specs=[HBM_SPEC] * n + [SEM_SPEC, SEM_SPEC, ANY],
        out_specs=[HBM_SPEC] * n,
        input_output_aliases={i: i for i in range(n)},
        compiler_params=SPLIT_COPY,
    )(*fulls, send_sems, recv_sems, after))


def _chip_copies(blk_ref, land_ref, send_sems, recv_sems):
    x, y, c, others = _place()
    return [pltpu.make_async_remote_copy(
        src_ref=blk_ref, dst_ref=land_ref.at[2 * x + y], send_sem=send_sems.at[j],
        recv_sem=recv_sems.at[j], device_id=(tx, ty, c), device_id_type=MESH)
        for j, (tx, ty) in enumerate(others)]


def _chip_exchange_start(blk, *, name):
    land = pltpu.with_memory_space_constraint(lax.empty((N_CHIPS, *blk.shape), blk.dtype), pltpu.HBM)

    def body(blk_ref, land_ref, send_sems, recv_sems, blk_out, land_out, token):
        for cp in _chip_copies(blk_ref, land_ref, send_sems, recv_sems):
            cp.start()
        token[...] = jnp.zeros_like(token)

    return pl.pallas_call(
        body, name=name,
        out_shape=(pltpu.SemaphoreType.DMA((3,)), pltpu.SemaphoreType.DMA((3,)),
                   pltpu.HBM(blk.shape, blk.dtype), pltpu.HBM(land.shape, land.dtype), TOKEN),
        in_specs=[HBM_SPEC] * 2,
        out_specs=(SEM_SPEC, SEM_SPEC, HBM_SPEC, HBM_SPEC, pl.BlockSpec(memory_space=pltpu.VMEM)),
        input_output_aliases={0: 2, 1: 3},
        compiler_params=SPLIT_COPY,
    )(pltpu.with_memory_space_constraint(blk, pltpu.HBM), land)


def _chip_exchange_wait(send_sems, recv_sems, blk, land, after, *, name):
    def body(blk_ref, land_ref, send_sems, recv_sems, after_ref, blk_out, land_out):
        for cp in _chip_copies(blk_ref, land_ref, send_sems, recv_sems):
            cp.wait_send()
            cp.wait_recv()

    return pl.pallas_call(
        body, name=name,
        out_shape=[pltpu.HBM(blk.shape, blk.dtype), pltpu.HBM(land.shape, land.dtype)],
        in_specs=[HBM_SPEC] * 2 + [SEM_SPEC, SEM_SPEC, ANY],
        out_specs=[HBM_SPEC] * 2,
        input_output_aliases={0: 0, 1: 1},
        compiler_params=SPLIT_COPY,
    )(blk, land, send_sems, recv_sems, after)[1]


def _scatter_geo(parts, axes):
    out = []
    for p, ax in zip(parts, axes):
        _, rh, cols = p.shape
        out.append((ax, rh, cols // N_CHIPS if ax == 1 else cols))
    return out


def _scatter_copies(p_refs, q_refs, geo, send_sems, recv_sems):
    x, y, c, others = _place()
    chip = 2 * x + y
    cps = []
    for i, (ax, rh, cw) in enumerate(geo):
        for j, (tx, ty) in enumerate(others):
            k = 2 * tx + ty
            src = (p_refs[i].at[0, :, pl.ds(pl.multiple_of(k * cw, LANES), cw)] if ax == 1
                   else p_refs[i].at[k])
            cps.append(pltpu.make_async_remote_copy(
                src_ref=src, dst_ref=q_refs[i].at[chip], send_sem=send_sems.at[3 * i + j],
                recv_sem=recv_sems.at[3 * i + j], device_id=(tx, ty, c), device_id_type=MESH))
    return cps


def _scatter_start(parts, axes, *, name):
    n = len(parts)
    geo = _scatter_geo(parts, axes)
    slots = [pltpu.HBM((N_CHIPS, rh, cw), p.dtype) for p, (_, rh, cw) in zip(parts, geo)]

    def body(*refs):
        p_refs, q_refs = refs[:n], refs[n:2 * n]
        send_sems, recv_sems = refs[2 * n], refs[2 * n + 1]
        token = refs[4 * n + 2]
        for cp in _scatter_copies(p_refs, q_refs, geo, send_sems, recv_sems):
            cp.start()
        token[...] = jnp.zeros_like(token)

    land = [pltpu.with_memory_space_constraint(lax.empty(s.inner_aval.shape, s.inner_aval.dtype), pltpu.HBM)
            for s in slots]
    out = pl.pallas_call(
        body, name=name,
        out_shape=(pltpu.SemaphoreType.DMA((3 * n,)), pltpu.SemaphoreType.DMA((3 * n,)),
                   *[pltpu.HBM(p.shape, p.dtype) for p in parts], *slots, TOKEN),
        in_specs=[HBM_SPEC] * (2 * n),
        out_specs=(SEM_SPEC, SEM_SPEC, *[HBM_SPEC] * (2 * n), pl.BlockSpec(memory_space=pltpu.VMEM)),
        input_output_aliases={i: 2 + i for i in range(2 * n)},
        compiler_params=SPLIT_COPY,
    )(*[pltpu.with_memory_space_constraint(p, pltpu.HBM) for p in parts], *land)
    return out[0], out[1], list(out[2:2 + n]), list(out[2 + n:2 + 2 * n]), out[2 + 2 * n]


def _scatter_wait(send_sems, recv_sems, parts, slots, axes, after, *, name):
    n = len(parts)
    geo = _scatter_geo(parts, axes)

    def body(*refs):
        p_refs, q_refs = refs[:n], refs[n:2 * n]
        for cp in _scatter_copies(p_refs, q_refs, geo, refs[2 * n], refs[2 * n + 1]):
            cp.wait_send()
            cp.wait_recv()

    out = pl.pallas_call(
        body, name=name,
        out_shape=[pltpu.HBM(a.shape, a.dtype) for a in (*parts, *slots)],
        in_specs=[HBM_SPEC] * (2 * n) + [SEM_SPEC, SEM_SPEC, ANY],
        out_specs=[HBM_SPEC] * (2 * n),
        input_output_aliases={i: i for i in range(2 * n)},
        compiler_params=SPLIT_COPY,
    )(*parts, *slots, send_sems, recv_sems, after)
    return list(out[:n]), list(out[n:])


def _sibling_swap(grads, *, name):
    n = len(grads)
    out_shape = [jax.ShapeDtypeStruct((g.shape[0], g.shape[1] // 2, g.shape[2]), g.dtype)
                 for g in grads]

    def body(*refs):
        g_refs, o_refs = refs[:n], refs[n:2 * n]
        send_sems, recv_sems = refs[2 * n:]
        x, y, c, _ = _place()
        cps = []
        for i in range(n):
            rh = grads[i].shape[1] // 2
            src = g_refs[i].at[:, pl.ds(pl.multiple_of((1 - c) * rh, 16), rh), :]
            cp = pltpu.make_async_remote_copy(
                src_ref=src, dst_ref=o_refs[i], send_sem=send_sems.at[i], recv_sem=recv_sems.at[i],
                device_id=(x, y, 1 - c), device_id_type=MESH)
            cp.start()
            cps.append(cp)
        for cp in cps:
            cp.wait()

    return pl.pallas_call(
        body, in_specs=[ANY] * n, out_specs=[ANY] * n, out_shape=out_shape,
        scratch_shapes=[pltpu.SemaphoreType.DMA((n,)), pltpu.SemaphoreType.DMA((n,))],
        name=name)(*grads)


def _swap_copies(g_refs, r_refs, shapes, send_sems, recv_sems):
    x, y, c, _ = _place()
    cps = []
    for i, shape in enumerate(shapes):
        rh = shape[1] // 2
        src = g_refs[i].at[:, pl.ds(pl.multiple_of((1 - c) * rh, 16), rh), :]
        cps.append(pltpu.make_async_remote_copy(
            src_ref=src, dst_ref=r_refs[i], send_sem=send_sems.at[i], recv_sem=recv_sems.at[i],
            device_id=(x, y, 1 - c), device_id_type=MESH))
    return cps


def _sibling_swap_start(grads, *, name):
    n = len(grads)
    shapes = [g.shape for g in grads]
    lands = [pltpu.HBM((s[0], s[1] // 2, s[2]), g.dtype) for s, g in zip(shapes, grads)]

    def body(*refs):
        g_refs, r_refs = refs[:n], refs[n:2 * n]
        token = refs[4 * n + 2]
        for cp in _swap_copies(g_refs, r_refs, shapes, refs[2 * n], refs[2 * n + 1]):
            cp.start()
        token[...] = jnp.zeros_like(token)

    land = [pltpu.with_memory_space_constraint(lax.empty(s.inner_aval.shape, s.inner_aval.dtype), pltpu.HBM)
            for s in lands]
    out = pl.pallas_call(
        body, name=name,
        out_shape=(pltpu.SemaphoreType.DMA((n,)), pltpu.SemaphoreType.DMA((n,)),
                   *[pltpu.HBM(g.shape, g.dtype) for g in grads], *lands, TOKEN),
        in_specs=[HBM_SPEC] * (2 * n),
        out_specs=(SEM_SPEC, SEM_SPEC, *[HBM_SPEC] * (2 * n), pl.BlockSpec(memory_space=pltpu.VMEM)),
        input_output_aliases={i: 2 + i for i in range(2 * n)},
        compiler_params=SPLIT_COPY,
    )(*[pltpu.with_memory_space_constraint(g, pltpu.HBM) for g in grads], *land)
    return out[0], out[1], list(out[2:2 + n]), list(out[2 + n:2 + 2 * n]), out[2 + 2 * n]


def _sibling_swap_wait(send_sems, recv_sems, grads, lands, after, *, name):
    n = len(grads)
    shapes = [g.shape for g in grads]

    def body(*refs):
        g_refs, r_refs = refs[:n], refs[n:2 * n]
        for cp in _swap_copies(g_refs, r_refs, shapes, refs[2 * n], refs[2 * n + 1]):
            cp.wait_send()
            cp.wait_recv()

    out = pl.pallas_call(
        body, name=name,
        out_shape=[pltpu.HBM(a.shape, a.dtype) for a in (*grads, *lands)],
        in_specs=[HBM_SPEC] * (2 * n) + [SEM_SPEC, SEM_SPEC, ANY],
        out_specs=[HBM_SPEC] * (2 * n),
        input_output_aliases={i: i for i in range(2 * n)},
        compiler_params=SPLIT_COPY,
    )(*grads, *lands, send_sems, recv_sems, after)
    return list(out[:n]), list(out[n:])


def _pair_add(g3s, rxs, place, *, out_dtype, name):
    n = len(g3s)
    steps, trs = _group_tiles([g.shape[1] // 2 for g in g3s], 16)

    def body(p_ref, *refs):
        for i in range(n):
            refs[2 * n + i][...] = (refs[2 * i][...] + refs[2 * i + 1][...]).astype(out_dtype)

    in_specs, out_specs = [], []
    for g, tr in zip(g3s, trs):
        blk = (g.shape[0], tr, g.shape[2])
        in_specs += [pl.BlockSpec(blk, lambda i, p_ref: (0, p_ref[1] * steps + i, 0)),
                     pl.BlockSpec(blk, lambda i, p_ref: (0, i, 0))]
        out_specs.append(pl.BlockSpec(blk, lambda i, p_ref: (0, i, 0)))
    return pl.pallas_call(
        body,
        grid_spec=pltpu.PrefetchScalarGridSpec(
            num_scalar_prefetch=1, grid=(steps,), in_specs=in_specs, out_specs=out_specs),
        out_shape=[jax.ShapeDtypeStruct((g.shape[0], g.shape[1] // 2, g.shape[2]), out_dtype)
                   for g in g3s],
        compiler_params=_cp("parallel"), name=name)(place, *[a for q in zip(g3s, rxs) for a in q])


def _sum_slots(q, *, name):
    ns, rows, cols = q.shape
    tr = next(t for t in (128, 64, 32, 16, 8) if rows % t == 0)

    def body(q_ref, o_ref):
        acc = q_ref[0].astype(F32)
        for k in range(1, ns):
            acc = acc + q_ref[k].astype(F32)
        o_ref[...] = acc

    return pl.pallas_call(
        body, grid=(rows // tr,),
        in_specs=[pl.BlockSpec((ns, tr, cols), lambda i: (0, i, 0))],
        out_specs=_rspec(tr, cols),
        out_shape=jax.ShapeDtypeStruct((rows, cols), F32),
        compiler_params=_cp("parallel"), name=name)(q)


def _sum_chips(qs, ps, place, axes, *, name):
    n = len(qs)
    per = N_CHIPS + 1
    steps, trs = _group_tiles([q.shape[1] for q in qs], 16)

    def body(p_ref, *refs):
        chip = p_ref[0]
        for i in range(n):
            q_refs, own_ref = refs[per * i:per * i + N_CHIPS], refs[per * i + N_CHIPS]
            acc = jnp.where(chip == 0, own_ref[...], q_refs[0][...]).astype(F32)
            for k in range(1, N_CHIPS):
                acc = acc + jnp.where(chip == k, own_ref[...], q_refs[k][...]).astype(F32)
            refs[per * n + i][...] = acc

    def slot_spec(k, tr, cw):
        return pl.BlockSpec((None, tr, cw),
                            lambda i, p_ref: (jnp.where(p_ref[0] == k, (k + 1) % N_CHIPS, k), i, 0))

    in_specs, out_specs, operands = [], [], []
    for q, p, ax, tr in zip(qs, ps, axes, trs):
        cw = q.shape[2]
        in_specs += [slot_spec(k, tr, cw) for k in range(N_CHIPS)]
        in_specs.append(pl.BlockSpec((None, tr, cw), (lambda i, p_ref: (0, i, p_ref[0])) if ax == 1
                                     else (lambda i, p_ref: (p_ref[0], i, 0))))
        out_specs.append(pl.BlockSpec((tr, cw), lambda i, p_ref: (p_ref[1] * steps + i, 0)))
        operands += [q] * N_CHIPS + [p]
    return pl.pallas_call(
        body,
        grid_spec=pltpu.PrefetchScalarGridSpec(
            num_scalar_prefetch=1, grid=(steps,), in_specs=in_specs, out_specs=out_specs),
        out_shape=[jax.ShapeDtypeStruct((2 * q.shape[1], q.shape[2]), F32) for q in qs],
        compiler_params=_cp("parallel"), name=name)(place, *operands)


def _sibling_share(shards, *, name):
    n = len(shards)

    def body(*refs):
        o_refs = refs[n:2 * n]
        send_sems, recv_sems = refs[2 * n:]
        x, y, c, _ = _place()
        cps = []
        for i in range(n):
            rh = shards[i].shape[0] // 2
            mine = o_refs[i].at[pl.ds(pl.multiple_of(c * rh, 8), rh), :]
            cp = pltpu.make_async_remote_copy(
                src_ref=mine, dst_ref=mine, send_sem=send_sems.at[i], recv_sem=recv_sems.at[i],
                device_id=(x, y, 1 - c), device_id_type=MESH)
            cp.start()
            cps.append(cp)
        for i in range(n):
            rh = shards[i].shape[0] // 2
            theirs = o_refs[i].at[pl.ds(pl.multiple_of((1 - c) * rh, 8), rh), :]
            pltpu.make_async_remote_copy(
                src_ref=theirs, dst_ref=theirs, send_sem=send_sems.at[i], recv_sem=recv_sems.at[i],
                device_id=(x, y, c), device_id_type=MESH).wait_recv()
        for cp in cps:
            cp.wait_send()

    return pl.pallas_call(
        body, in_specs=[ANY] * n, out_specs=[ANY] * n,
        out_shape=[jax.ShapeDtypeStruct(h.shape, h.dtype) for h in shards],
        input_output_aliases={i: i for i in range(n)},
        scratch_shapes=[pltpu.SemaphoreType.DMA((n,)), pltpu.SemaphoreType.DMA((n,))],
        name=name)(*shards)


def _gather_all(blk, *, name, after=None, shares=()):
    rows, cols = blk.shape
    extra = [] if after is None else [after]
    n_s = len(shares)

    def body(x_ref, *refs):
        s_refs = refs[len(extra) + n_s + 1:len(extra) + 2 * n_s + 1]
        out_ref = refs[len(extra) + n_s]
        send_sems, recv_sems, local_sem, s_send, s_recv = refs[len(extra) + 2 * n_s + 1:]
        x, y, c = lax.axis_index("x"), lax.axis_index("y"), lax.axis_index("c")
        me = 4 * x + 2 * y + c
        mine = pltpu.make_async_copy(x_ref, out_ref.at[me], local_sem)
        mine.start()
        cps = []
        for i in range(n_s):
            rh = shares[i].shape[0] // 2
            half = s_refs[i].at[pl.ds(pl.multiple_of(c * rh, 8), rh), :]
            cp = pltpu.make_async_remote_copy(
                src_ref=half, dst_ref=half, send_sem=s_send.at[i], recv_sem=s_recv.at[i],
                device_id=(x, y, 1 - c), device_id_type=MESH)
            cp.start()
            cps.append(cp)
        for k in range(1, N_DEV):
            tx = (1 - x) if (k >> 2) & 1 else x
            ty = (1 - y) if (k >> 1) & 1 else y
            tc = (1 - c) if k & 1 else c
            cp = pltpu.make_async_remote_copy(
                src_ref=x_ref, dst_ref=out_ref.at[me], send_sem=send_sems.at[k - 1],
                recv_sem=recv_sems.at[k - 1], device_id=(tx, ty, tc), device_id_type=MESH)
            cp.start()
            cps.append(cp)
        for k in range(1, N_DEV):
            tx = (1 - x) if (k >> 2) & 1 else x
            ty = (1 - y) if (k >> 1) & 1 else y
            tc = (1 - c) if k & 1 else c
            got = out_ref.at[4 * tx + 2 * ty + tc]
            pltpu.make_async_remote_copy(
                src_ref=got, dst_ref=got, send_sem=send_sems.at[k - 1], recv_sem=recv_sems.at[k - 1],
                device_id=(x, y, c), device_id_type=MESH).wait_recv()
        for i in range(n_s):
            rh = shares[i].shape[0] // 2
            theirs = s_refs[i].at[pl.ds(pl.multiple_of((1 - c) * rh, 8), rh), :]
            pltpu.make_async_remote_copy(
                src_ref=theirs, dst_ref=theirs, send_sem=s_send.at[i], recv_sem=s_recv.at[i],
                device_id=(x, y, c), device_id_type=MESH).wait_recv()
        for cp in cps:
            cp.wait_send()
        mine.wait()

    vm = pl.BlockSpec(memory_space=pltpu.VMEM)
    out = pl.pallas_call(
        body, in_specs=[vm] + [ANY] * (len(extra) + n_s), out_specs=[vm] + [ANY] * n_s,
        out_shape=[jax.ShapeDtypeStruct((N_DEV, rows, cols), blk.dtype)]
        + [jax.ShapeDtypeStruct(s.shape, s.dtype) for s in shares],
        input_output_aliases={1 + len(extra) + i: 1 + i for i in range(n_s)},
        scratch_shapes=[pltpu.SemaphoreType.DMA((N_DEV - 1,)), pltpu.SemaphoreType.DMA((N_DEV - 1,)),
                        pltpu.SemaphoreType.DMA, pltpu.SemaphoreType.DMA((max(n_s, 1),)),
                        pltpu.SemaphoreType.DMA((max(n_s, 1),))],
        name=name)(blk, *extra, *shares)
    return (out[0], *out[1:]) if n_s else out[0]


def _as_rows(a):
    flat = a.reshape(-1)
    n = flat.shape[0]
    rows = -(-n // (8 * LANES)) * 8
    return jnp.pad(flat, (0, rows * LANES - n)).reshape(rows, LANES)


def _from_rows(p, shape):
    n = int(np.prod(shape))
    return p.reshape(-1)[:n].reshape(shape)


WEIGHT_AXES = (1, 1, 1, 0, 1, 0)
WIRE = BF16


def kernel(x, meta_tokens, w_in, w_na_out, w_hg_out, w_o, w_up, w_down, norm_mix, norm_mlp, norm_final, hg_norm, na_rpb, hg_lb_logits, loss_target, m_meta_tokens, m_w_in, m_w_na_out, m_w_hg_out, m_w_o, m_w_up, m_w_down, m_norm_mix, m_norm_mlp, m_norm_final, m_hg_norm, m_na_rpb, m_hg_lb_logits, v_meta_tokens, v_w_in, v_w_na_out, v_w_hg_out, v_w_o, v_w_up, v_w_down, v_norm_mix, v_norm_mlp, v_norm_final, v_hg_norm, v_na_rpb, v_hg_lb_logits):
    xi, yi, ci = lax.axis_index("x"), lax.axis_index("y"), lax.axis_index("c")
    chip = 2 * xi + yi
    d = x.shape[-1]
    dshard = meta_tokens.shape[1]
    hgw = hg_norm.shape[1]
    lbs = hg_lb_logits.shape[2]
    big = [w_in[0], w_na_out[0], w_hg_out[0], w_o[0], w_up[0], w_down[0]]
    big_m = [m_w_in[0], m_w_na_out[0], m_w_hg_out[0], m_w_o[0], m_w_up[0], m_w_down[0]]
    big_v = [v_w_in[0], v_w_na_out[0], v_w_hg_out[0], v_w_o[0], v_w_up[0], v_w_down[0]]

    place = jnp.stack([chip, ci]).astype(jnp.int32)
    own_w = _cast_into_full(big, WEIGHT_AXES, place, name="cast_shards")
    in_axes, rest_axes = WEIGHT_AXES[:1], WEIGHT_AXES[1:]
    small_in = jnp.concatenate([_as_rows(meta_tokens), _as_rows(hg_lb_logits)], axis=0)
    sm_send, sm_recv, sm_blk, sm_land, sm_token = _chip_exchange_start(small_in,
                                                                       name="small_params_start")
    in_send, in_recv, in_bufs, in_token = _allgather_start(own_w[:1], in_axes, sm_token,
                                                           name="weight_allgather_in_start")
    ag_send, ag_recv, ag_bufs, ag_token = _allgather_start(own_w[1:], rest_axes, in_token,
                                                           name="weight_allgather_rest_start")
    sm_land = _chip_exchange_wait(sm_send, sm_recv, sm_blk, sm_land, ag_token,
                                  name="small_params_wait")
    small_all = lax.dynamic_update_slice(sm_land, small_in[None], (chip, 0, 0))
    forward = {}

    def first_weight(after):
        got = _allgather_wait(in_send, in_recv, in_bufs, in_axes, after,
                              name="weight_allgather_in_wait")
        return _allgather_forward(got, in_axes, name="weight_allgather_in_forward")[0]

    def rest_landed(after):
        got = _allgather_wait(ag_send, ag_recv, ag_bufs, rest_axes, after,
                              name="weight_allgather_rest_wait")
        send, recv, bufs, token = _allgather_forward_start(
            got, rest_axes, name="weight_allgather_rest_forward_start")
        forward["rest"] = (send, recv, bufs)
        return token

    def rest_weights(after):
        return _allgather_forward_wait(*forward["rest"], rest_axes, after,
                                       name="weight_allgather_rest_forward_wait")

    n_meta_rows = N_META * dshard // LANES
    meta_full = (small_all[:, :n_meta_rows].reshape(N_CHIPS, N_META, dshard)
                 .transpose(1, 0, 2).reshape(N_META, d))
    lbl_full = (small_all[:, n_meta_rows:].reshape(N_CHIPS, -1)[:, :4 * lbs]
                .reshape(N_CHIPS, 2, 2, lbs).transpose(1, 2, 0, 3).reshape(2, 2, N_CHIPS * lbs))
    lb = jax.nn.softmax(lbl_full, axis=1)[:, 0]

    def by_chip(dws, axes):
        return [g.reshape(1, *g.shape) if ax == 1
                else g.reshape(N_CHIPS, g.shape[0] // N_CHIPS, g.shape[1]) for g, ax in zip(dws, axes)]

    flying = {}

    def scatter(tag, axes, g3, rx):
        parts = _pair_add(g3, rx, place, out_dtype=WIRE, name=f"grad_pair_add_{tag}")
        send, recv, parts, slots, token = _scatter_start(parts, axes,
                                                         name=f"grad_scatter_{tag}_start")
        flying[tag] = (send, recv, parts, slots)
        return token

    def swap(tag, axes):
        def start(dws):
            send, recv, g3, lands, token = _sibling_swap_start(
                by_chip(dws, axes), name=f"grad_sibling_swap_{tag}_start")
            flying["swap_" + tag] = (send, recv, g3, lands)
            return token

        def finish(after):
            g3, rx = _sibling_swap_wait(*flying["swap_" + tag], after,
                                        name=f"grad_sibling_swap_{tag}_wait")
            return scatter(tag, axes, g3, rx)
        return start, finish

    swap_rest, scatter_rest = swap("rest", rest_axes)
    swap_in, scatter_in = swap("in", in_axes)

    def landed(tag, axes, after):
        return _scatter_wait(*flying[tag], axes, after, name=f"grad_scatter_{tag}_wait")

    (loss, dx, dmeta, *_, dg_mix, dg_mlp, dg_fin, d_gain, d_rpb, d_lb) = _local_step(
        x[0], loss_target[0], meta_full, first_weight, rest_weights, norm_mix, norm_mlp,
        norm_final.reshape(1, d), hg_norm, na_rpb[0], lb, swap_rest, scatter_rest,
        lambda dw_in: swap_in([dw_in]), rest_landed, scatter_in)

    parts_rest, slots_rest = landed("rest", rest_axes, dx)
    g_rest = _sibling_share(_sum_chips(slots_rest, parts_rest, place, rest_axes,
                                       name="grad_sum_chips_rest"),
                            name="grad_sibling_share_rest")
    out_rest = _adamw(big[1:], g_rest, big_m[1:], big_v[1:], name="adamw_rest")
    parts_in, slots_in = landed("in", in_axes, out_rest[-1][0])
    half_in = _sum_chips(slots_in, parts_in, place, in_axes, name="grad_sum_chips_in")

    d_rpb_c = d_rpb[:, :2 * NA_WIN_W - 1]
    small_g = [dmeta, dg_mix, dg_mlp, dg_fin, d_gain, d_rpb_c, d_lb, loss]
    packed = jnp.concatenate([_as_rows(a) for a in small_g], axis=0)
    gathered, g_in = _gather_all(packed, shares=half_in, name="gather_small_grads")
    g_big = [g_in] + list(g_rest)
    total = _sum_slots(gathered, name="sum_small_grads")
    offs = np.cumsum([0] + [_as_rows(a).shape[0] for a in small_g])
    take = lambda i, shape: _from_rows(total[offs[i]:offs[i + 1]], shape)
    g_meta_full = take(0, (N_META, d))
    g_norm_mix, g_norm_mlp = take(1, (1, d)), take(2, (1, d))
    g_norm_final = take(3, (d,))
    g_hg_norm = take(4, (1, hgw))
    g_rpb = take(5, na_rpb.shape)
    g_lb = take(6, (2, hgw))
    loss_total = take(7, (1, LANES))[0, 0]
    g_meta = lax.dynamic_slice_in_dim(g_meta_full, chip * dshard, dshard, axis=1)
    dl0 = lb * (1.0 - lb) * g_lb
    g_lbl_full = jnp.stack([dl0, -dl0], axis=1)
    g_lbl = lax.dynamic_slice_in_dim(g_lbl_full, chip * lbs, lbs, axis=2)

    big_out = _adamw(big[:1], [g_in], big_m[:1], big_v[:1], name="adamw_in") + out_rest
    small_w = [meta_tokens, norm_mix, norm_mlp, norm_final, hg_norm, na_rpb, hg_lb_logits]
    small_gr = [g_meta, g_norm_mix, g_norm_mlp, g_norm_final, g_hg_norm, g_rpb, g_lbl]
    small_m = [m_meta_tokens, m_norm_mix, m_norm_mlp, m_norm_final, m_hg_norm, m_na_rpb, m_hg_lb_logits]
    small_v = [v_meta_tokens, v_norm_mix, v_norm_mlp, v_norm_final, v_hg_norm, v_na_rpb, v_hg_lb_logits]
    pk = lambda lst: jnp.concatenate([_as_rows(a) for a in lst], axis=0)
    ((sd, sm, sv),) = _adamw([pk(small_w)], [pk(small_gr)], [pk(small_m)], [pk(small_v)],
                             name="adamw_small")
    soffs = np.cumsum([0] + [_as_rows(a).shape[0] for a in small_w])
    unpk = lambda p: [_from_rows(p[soffs[i]:soffs[i + 1]], small_w[i].shape) for i in range(len(small_w))]
    sd, sm, sv = unpk(sd), unpk(sm), unpk(sv)

    def order(bigs, smalls):
        return [smalls[0]] + [b.reshape(1, *b.shape) for b in bigs] + smalls[1:]

    grads = order(g_big, small_gr)
    deltas = order([o[0] for o in big_out], sd)
    new_m = order([o[1] for o in big_out], sm)
    new_v = order([o[2] for o in big_out], sv)
    return (loss_total, dx.reshape(1, *dx.shape), *grads, *deltas, *new_m, *new_v)
```

```python
import functools

import numpy as np
import jax
import jax.numpy as jnp
from jax import lax
from jax.experimental import pallas as pl
from jax.experimental.pallas import tpu as pltpu

F32 = jnp.float32
BF16 = jnp.bfloat16
HIGHEST = lax.Precision.HIGHEST

GRID_W = 64
N_META = 16
EPS = 1e-6
NA_HEAD_DIM = 64
NA_WIN_H = 8
NA_WIN_W = 16
HG_DK = 128
HG_CHUNK = 16
LANES = 128
ROW_ALIGN = 128
VMEM_LIMIT = 48 * 1024 * 1024

ADAM_LR = 0.001
ADAM_B1 = 0.9
ADAM_B2 = 0.999
ADAM_EPS = 1e-08
ADAM_WD = 0.01
ADAM_STEP = 10

MESH = pl.DeviceIdType.MESH


def _cp(*sem):
    return pltpu.CompilerParams(dimension_semantics=sem, vmem_limit_bytes=VMEM_LIMIT)


def _sigmoid(x):
    return 0.5 * jnp.tanh(0.5 * x) + 0.5


def _dot(a, b, dims, precision=None):
    return lax.dot_general(a, b, (dims, ((), ())), preferred_element_type=F32, precision=precision)


def _nn(a, b, **kw):
    return _dot(a, b, ((1,), (0,)), **kw)


def _nt(a, b, **kw):
    return _dot(a, b, ((1,), (1,)), **kw)


def _tn(a, b, **kw):
    return _dot(a, b, ((0,), (0,)), **kw)


def _matmul(a, b, *, ta=False, tb=False, tm=None, tn=None, tk=None, out_dtype=F32, name,
            precision=None, after=None, epilogue=None, tiles=(), out_dtypes=None):
    extra = [] if after is None else [after]
    single = out_dtypes is None
    if single:
        out_dtypes = (out_dtype,)
    n_t, n_o = len(tiles), len(out_dtypes)
    if ta:
        kdim, m = a.shape
    else:
        m, kdim = a.shape
    if tb:
        n, k2 = b.shape
    else:
        k2, n = b.shape
    assert kdim == k2, (a.shape, b.shape, ta, tb)
    if tm is None:
        if ta:
            tm = next(t for t in (1024, 512, 256, 128, m) if m % t == 0)
        else:
            tm = m // 2 if (m // 2) % 16 == 0 and m > 512 else m
    if tn is None:
        wide = (1024,) if not ta and len(tiles) <= 1 else ()
        tn = next(t for t in (*wide, 512, 256, 128, n) if n % t == 0)
    if tk is None:
        tk = kdim if ta else next(t for t in (2048, 1024, 512, 256, 128, kdim) if kdim % t == 0)
    assert m % tm == 0 and n % tn == 0 and kdim % tk == 0, (m, n, kdim, tm, tn, tk)
    nk = kdim // tk
    op_dtype = F32 if precision is not None else BF16

    def body(a_ref, b_ref, *refs):
        t_refs = refs[:n_t]
        o_refs = refs[n_t + len(extra):n_t + len(extra) + n_o]
        av = a_ref[...].astype(op_dtype)
        bv = b_ref[...].astype(op_dtype)
        dims = ((0 if ta else 1,), (1 if tb else 0,))
        part = _dot(av, bv, dims, precision=precision)

        def finish(acc):
            outs = (acc,) if epilogue is None else epilogue(acc, *[t[...] for t in t_refs])
            for o_ref, val in zip(o_refs, outs):
                o_ref[...] = val.astype(o_ref.dtype)

        if nk == 1:
            finish(part)
            return
        acc_ref = refs[-1]
        kk = pl.program_id(2)

        @pl.when(kk == 0)
        def _():
            acc_ref[...] = part

        @pl.when((kk > 0) & (kk < nk - 1))
        def _():
            acc_ref[...] += part

        @pl.when(kk == nk - 1)
        def _():
            finish(acc_ref[...] + part)

    a_spec = (pl.BlockSpec((tk, tm), lambda i, j, k: (k, i)) if ta
              else pl.BlockSpec((tm, tk), lambda i, j, k: (i, k)))
    b_spec = (pl.BlockSpec((tn, tk), lambda i, j, k: (j, k)) if tb
              else pl.BlockSpec((tk, tn), lambda i, j, k: (k, j)))
    for _, off in tiles:
        assert off % tn == 0, (off, tn)
    t_specs = [pl.BlockSpec((tm, tn), functools.partial(lambda i, j, k, o: (i, o + j), o=off // tn))
               for _, off in tiles]
    o_spec = pl.BlockSpec((tm, tn), lambda i, j, k: (i, j))
    outs = pl.pallas_call(
        body,
        grid=(m // tm, n // tn, nk),
        in_specs=[a_spec, b_spec] + t_specs + [pl.BlockSpec(memory_space=pl.ANY)] * len(extra),
        out_specs=[o_spec] * n_o,
        out_shape=[jax.ShapeDtypeStruct((m, n), dt) for dt in out_dtypes],
        scratch_shapes=[pltpu.VMEM((tm, tn), F32)] if nk > 1 else [],
        compiler_params=_cp("parallel", "parallel", "arbitrary"),
        name=name,
    )(a, b, *[t for t, _ in tiles], *extra)
    return outs[0] if single else outs


def _rspec(tr, w, cb=0):
    return pl.BlockSpec((tr, w), lambda i: (i, cb))


def _fspec(shape):
    nd = len(shape)
    return pl.BlockSpec(shape, lambda i: (0,) * nd)


ROW_VMEM_BUDGET = 20 * 1024 * 1024
ROW_MIN_STEPS = 4


def _row_tile(lp, row_bytes):
    for k in range(ROW_MIN_STEPS, lp // 16 + 1):
        tr = lp // k
        if lp % k == 0 and tr % 16 == 0 and 2 * tr * row_bytes <= ROW_VMEM_BUDGET:
            return tr
    return lp


def _token_rows_copy(i, n_tiles, tr, n_tok, tok_ref, buf_ref, sem, *, to_tokens, start=True,
                     wait=True):
    assert n_tiles >= 2 and 0 < n_tok + N_META - (n_tiles - 1) * tr <= tr

    def run(tok_row, buf_row, count):
        tok = tok_ref.at[pl.ds(tok_row, count), :]
        buf = buf_ref.at[pl.ds(buf_row, count), :]
        cp = pltpu.make_async_copy(buf, tok, sem) if to_tokens else pltpu.make_async_copy(tok, buf, sem)
        if start:
            cp.start()
        if wait:
            cp.wait()

    @pl.when(i == 0)
    def _():
        run(0, N_META, tr - N_META)

    if n_tiles > 2:
        @pl.when((i > 0) & (i < n_tiles - 1))
        def _():
            run(pl.multiple_of(i * tr - N_META, 8), 0, tr)

    @pl.when(i == n_tiles - 1)
    def _():
        run((n_tiles - 1) * tr - N_META, 0, n_tok + N_META - (n_tiles - 1) * tr)


def _embed_norm(x, meta, g, *, lp, name):
    n_tok, d = x.shape
    tr = _row_tile(lp, d * (4 + 2))
    n_tiles = lp // tr

    def body(x_ref, meta_ref, g_ref, h_ref, o_ref, buf_ref, sem):
        i = pl.program_id(0)
        buf_ref[...] = jnp.zeros_like(buf_ref)

        @pl.when(i == 0)
        def _():
            buf_ref[0:N_META, :] = meta_ref[...]

        _token_rows_copy(i, n_tiles, tr, n_tok, x_ref, buf_ref, sem, to_tokens=False)
        xv = buf_ref[...]
        h_ref[...] = xv
        r = lax.rsqrt(jnp.mean(xv * xv, axis=-1, keepdims=True) + EPS)
        o_ref[...] = (xv * r * g_ref[...]).astype(BF16)

    return pl.pallas_call(
        body, grid=(n_tiles,),
        in_specs=[ANY, _fspec((N_META, d)), _fspec((1, d))],
        out_specs=[_rspec(tr, d), _rspec(tr, d)],
        out_shape=[jax.ShapeDtypeStruct((lp, d), F32), jax.ShapeDtypeStruct((lp, d), BF16)],
        scratch_shapes=[pltpu.VMEM((tr, d), F32), pltpu.SemaphoreType.DMA],
        compiler_params=_cp("parallel"), name=name)(x, meta, g)


def _residual_norm(h, t, g, *, name):
    lp, d = h.shape
    tr = _row_tile(lp, d * (4 + 4 + 4 + 2))

    def body(h_ref, t_ref, g_ref, h1_ref, m_ref):
        xv = h_ref[...] + t_ref[...]
        h1_ref[...] = xv
        r = lax.rsqrt(jnp.mean(xv * xv, axis=-1, keepdims=True) + EPS)
        m_ref[...] = (xv * r * g_ref[...]).astype(BF16)

    return pl.pallas_call(
        body, grid=(lp // tr,),
        in_specs=[_rspec(tr, d), _rspec(tr, d), _fspec((1, d))],
        out_specs=[_rspec(tr, d), _rspec(tr, d)],
        out_shape=[jax.ShapeDtypeStruct((lp, d), F32), jax.ShapeDtypeStruct((lp, d), BF16)],
        compiler_params=_cp("parallel"), name=name)(h, t, g)


def _rmsnorm_bwd_add(x, g, dy, dres, *, name):
    lp, d = x.shape
    tr = _row_tile(lp, d * (4 * 4 + 2))

    def body(x_ref, g_ref, dy_ref, dr_ref, dx_ref, dx16_ref, dg_ref):
        @pl.when(pl.program_id(0) == 0)
        def _():
            dg_ref[...] = jnp.zeros_like(dg_ref)

        xv = x_ref[...]
        r = lax.rsqrt(jnp.mean(xv * xv, axis=-1, keepdims=True) + EPS)
        xh = xv * r
        dyv = dy_ref[...]
        dg_ref[...] += jnp.sum(dyv * xh, axis=0, keepdims=True)
        dxh = dyv * g_ref[...]
        dx = dr_ref[...] + r * (dxh - xh * jnp.mean(dxh * xh, axis=-1, keepdims=True))
        dx_ref[...] = dx
        dx16_ref[...] = dx.astype(BF16)

    return pl.pallas_call(
        body, grid=(lp // tr,),
        in_specs=[_rspec(tr, d), _fspec((1, d)), _rspec(tr, d), _rspec(tr, d)],
        out_specs=[_rspec(tr, d), _rspec(tr, d), _fspec((1, d))],
        out_shape=[jax.ShapeDtypeStruct((lp, d), F32), jax.ShapeDtypeStruct((lp, d), BF16),
                   jax.ShapeDtypeStruct((1, d), F32)],
        compiler_params=_cp("arbitrary"), name=name)(x, g, dy, dres)


def _rmsnorm_bwd_tokens(x, g, dy, dres, *, n_tok, name):
    lp, d = x.shape
    tr = _row_tile(lp, d * 4 * 4)
    n_tiles = lp // tr

    def body(x_ref, g_ref, dy_ref, dr_ref, dtok_ref, dmeta_ref, dg_ref, buf_ref, sem):
        i = pl.program_id(0)

        @pl.when(i == 0)
        def _():
            dg_ref[...] = jnp.zeros_like(dg_ref)

        xv = x_ref[...]
        r = lax.rsqrt(jnp.mean(xv * xv, axis=-1, keepdims=True) + EPS)
        xh = xv * r
        dyv = dy_ref[...]
        dg_ref[...] += jnp.sum(dyv * xh, axis=0, keepdims=True)
        dxh = dyv * g_ref[...]
        buf_ref[...] = dr_ref[...] + r * (dxh - xh * jnp.mean(dxh * xh, axis=-1, keepdims=True))

        @pl.when(i == 0)
        def _():
            dmeta_ref[...] = buf_ref[0:N_META, :]

        _token_rows_copy(i, n_tiles, tr, n_tok, dtok_ref, buf_ref, sem, to_tokens=True)

    return pl.pallas_call(
        body, grid=(n_tiles,),
        in_specs=[_rspec(tr, d), _fspec((1, d)), _rspec(tr, d), _rspec(tr, d)],
        out_specs=[ANY, _fspec((N_META, d)), _fspec((1, d))],
        out_shape=[jax.ShapeDtypeStruct((n_tok, d), F32), jax.ShapeDtypeStruct((N_META, d), F32),
                   jax.ShapeDtypeStruct((1, d), F32)],
        scratch_shapes=[pltpu.VMEM((tr, d), F32), pltpu.SemaphoreType.DMA],
        compiler_params=_cp("arbitrary"), name=name)(x, g, dy, dres)


def _final_loss(h1, t2, g, tgt, *, name):
    lp, d = h1.shape
    n_tok = tgt.shape[0]
    tr = _row_tile(lp, d * 4 * 4)
    n_tiles = lp // tr

    def body(h_ref, t_ref, g_ref, tgt_ref, dh_ref, dh16_ref, loss_ref, dg_ref, tg_ref, sem):
        i = pl.program_id(0)

        @pl.when(i == 0)
        def _():
            loss_ref[...] = jnp.zeros_like(loss_ref)
            dg_ref[...] = jnp.zeros_like(dg_ref)
            tg_ref[...] = jnp.zeros_like(tg_ref)

        _token_rows_copy(i, n_tiles, tr, n_tok, tgt_ref, tg_ref, sem, to_tokens=False, wait=False)
        xv = h_ref[...] + t_ref[...]
        r = lax.rsqrt(jnp.mean(xv * xv, axis=-1, keepdims=True) + EPS)
        xh = xv * r
        gv = g_ref[...]
        _token_rows_copy(i, n_tiles, tr, n_tok, tgt_ref, tg_ref, sem, to_tokens=False, start=False)
        row = i * tr + lax.broadcasted_iota(jnp.int32, (tr, 1), 0)
        valid = (row >= N_META) & (row < N_META + n_tok)
        err = jnp.where(valid, xh * gv - tg_ref[...], 0.0)
        loss_ref[...] += jnp.sum(0.5 * err * err) / d
        dy = err / d
        dg_ref[...] += jnp.sum(dy * xh, axis=0, keepdims=True)
        dxh = dy * gv
        dh = r * (dxh - xh * jnp.mean(dxh * xh, axis=-1, keepdims=True))
        dh_ref[...] = dh
        dh16_ref[...] = dh.astype(BF16)

    return pl.pallas_call(
        body, grid=(n_tiles,),
        in_specs=[_rspec(tr, d), _rspec(tr, d), _fspec((1, d)), ANY],
        out_specs=[_rspec(tr, d), _rspec(tr, d), _fspec((1, LANES)), _fspec((1, d))],
        out_shape=[jax.ShapeDtypeStruct((lp, d), F32), jax.ShapeDtypeStruct((lp, d), BF16),
                   jax.ShapeDtypeStruct((1, LANES), F32), jax.ShapeDtypeStruct((1, d), F32)],
        scratch_shapes=[pltpu.VMEM((tr, d), F32), pltpu.SemaphoreType.DMA],
        compiler_params=_cp("arbitrary"), name=name)(h1, t2, g, tgt)


def _hg_out(o_f, o_b, proj, gain, *, col_g, name):
    lp, w = o_f.shape
    tr = _row_tile(lp, w * (3 * 4 + 2))
    hh = w // HG_DK

    def body(of_ref, ob_ref, g_ref, gain_ref, y_ref):
        gv = g_ref[...]
        sg = gv * _sigmoid(gv)
        for h in range(hh):
            sl = slice(h * HG_DK, (h + 1) * HG_DK)
            o = of_ref[:, sl] + ob_ref[:, sl]
            r = lax.rsqrt(jnp.mean(o * o, axis=-1, keepdims=True) + EPS)
            y_ref[:, sl] = (o * r * gain_ref[:, sl] * sg[:, sl]).astype(BF16)

    return pl.pallas_call(
        body, grid=(lp // tr,),
        in_specs=[_rspec(tr, w), _rspec(tr, w), _rspec(tr, w, col_g // w), _fspec((1, w))],
        out_specs=_rspec(tr, w),
        out_shape=jax.ShapeDtypeStruct((lp, w), BF16),
        compiler_params=_cp("parallel"), name=name)(o_f, o_b, proj, gain)


def _hg_out_bwd(o_f, o_b, proj, gain, dy, *, col_g, name):
    lp, w = o_f.shape
    tr = _row_tile(lp, w * (5 * 4 + 2))
    hh = w // HG_DK

    def body(of_ref, ob_ref, g_ref, gain_ref, dy_ref, do_ref, dg_ref, dgain_ref):
        @pl.when(pl.program_id(0) == 0)
        def _():
            dgain_ref[...] = jnp.zeros_like(dgain_ref)

        for h in range(hh):
            sl = slice(h * HG_DK, (h + 1) * HG_DK)
            gv = g_ref[:, sl]
            s = _sigmoid(gv)
            sg = gv * s
            dsg = s + gv * s * (1.0 - s)
            o = of_ref[:, sl] + ob_ref[:, sl]
            r = lax.rsqrt(jnp.mean(o * o, axis=-1, keepdims=True) + EPS)
            on = o * r
            dyv = dy_ref[:, sl]
            gn = gain_ref[:, sl]
            dgain_ref[:, sl] += jnp.sum(dyv * on * sg, axis=0, keepdims=True)
            dg_ref[:, sl] = (dyv * on * gn * dsg).astype(BF16)
            don = dyv * gn * sg
            do_ref[:, sl] = r * (don - on * jnp.mean(don * on, axis=-1, keepdims=True))

    return pl.pallas_call(
        body, grid=(lp // tr,),
        in_specs=[_rspec(tr, w), _rspec(tr, w), _rspec(tr, w, col_g // w), _fspec((1, w)),
                  _rspec(tr, w)],
        out_specs=[_rspec(tr, w), _rspec(tr, w), _fspec((1, w))],
        out_shape=[jax.ShapeDtypeStruct((lp, w), F32), jax.ShapeDtypeStruct((lp, w), BF16),
                   jax.ShapeDtypeStruct((1, w), F32)],
        compiler_params=_cp("arbitrary"), name=name)(o_f, o_b, proj, gain, dy)


HG_ROWS = 128
HG_HALVES = (1, 2, 4, 8, 16, 32, 64)


def _hg_gates(zq, z, lbv):
    sq = _sigmoid(zq)
    s = _sigmoid(z)
    f = lbv + (1.0 - lbv) * s
    kk = (1.0 - lbv) * (1.0 - s)
    return zq * sq, sq, s, f, jnp.log(f), kk


def _block_cumsum(g, pos, suffix):
    x = g
    for k in HG_HALVES:
        if suffix:
            x = x + jnp.where(pos < HG_ROWS - k, pltpu.roll(x, HG_ROWS - k, 0), 0.0)
        else:
            x = x + jnp.where(pos >= k, pltpu.roll(x, k, 0), 0.0)
    return x


def _pair_levels(b, pos, reverse):
    out = []
    first = b
    for m in HG_HALVES:
        if m > 1:
            first = jnp.where((pos & (m - 1)) >= m // 2, pltpu.roll(first, m // 2, 0), first)
        nxt = pltpu.roll(first, HG_ROWS - m, 0)
        upper = (pos & (2 * m - 1)) >= m
        if reverse:
            eq = jnp.where(upper, 0.0, jnp.exp(b - nxt))
            ek = jnp.where(upper, jnp.exp(first - b), 0.0)
        else:
            eq = jnp.where(upper, jnp.exp(b - first), 0.0)
            ek = jnp.where(upper, 0.0, jnp.exp(nxt - b))
        out.append((eq, ek))
    return out


def _pair_masks(mask_ref):
    ri = lax.broadcasted_iota(jnp.int32, (HG_ROWS, HG_ROWS), 0)
    ci = lax.broadcasted_iota(jnp.int32, (HG_ROWS, HG_ROWS), 1)
    for i, m in enumerate(HG_HALVES):
        sh = m.bit_length()
        mask_ref[i] = jnp.where((ri >> sh) == (ci >> sh), 1.0, 0.0)


def _hg_scan_fwd(proj, lb, *, reverse, col_q, col_z, col_i, hh, name):
    lp = proj.shape[0]
    n_blocks = lp // HG_ROWS
    last = 0 if reverse else HG_ROWS - 1

    def body(q_ref, z_ref, i_ref, lb_ref, o_ref, st_ref, mask_ref):
        lbv = lb_ref[...]
        pos = lax.broadcasted_iota(jnp.int32, (HG_ROWS, 1), 0)
        ri = lax.broadcasted_iota(jnp.int32, (HG_ROWS, HG_ROWS), 0)
        ci = lax.broadcasted_iota(jnp.int32, (HG_ROWS, HG_ROWS), 1)
        _pair_masks(mask_ref)

        def block(bi, st):
            bb = (n_blocks - 1 - bi) if reverse else bi
            r0 = pl.multiple_of(bb * HG_ROWS, HG_ROWS)
            v16 = i_ref[pl.ds(r0, HG_ROWS), :].astype(BF16)
            qh, _, _, _, g, kk = _hg_gates(q_ref[pl.ds(r0, HG_ROWS), :],
                                           z_ref[pl.ds(r0, HG_ROWS), :], lbv)
            b = _block_cumsum(g, pos, reverse)
            bl = b[last:last + 1, :]
            qe = (qh * jnp.exp(b)).astype(BF16)
            kd = (kk * jnp.exp(bl - b)).astype(BF16)
            a = jnp.where(ri == ci, jnp.sum(qh * kk, axis=1, keepdims=True), 0.0)
            for i, (eq, ek) in enumerate(_pair_levels(b, pos, reverse)):
                a = a + mask_ref[i] * _nt((qh * eq).astype(BF16), (kk * ek).astype(BF16))
            st_ref[bb] = st
            o_ref[pl.ds(r0, HG_ROWS), :] = _nn(a.astype(BF16), v16) + _nt(qe, st.astype(BF16))
            return jnp.exp(bl) * st + _tn(v16, kd)

        lax.fori_loop(0, n_blocks, block, jnp.zeros((HG_DK, HG_DK), F32))

    cspec = lambda col: pl.BlockSpec((lp, HG_DK), lambda h: (0, col // HG_DK + h))
    return pl.pallas_call(
        body, grid=(hh,),
        in_specs=[cspec(col_q), cspec(col_z), cspec(col_i),
                  pl.BlockSpec((None, 1, HG_DK), lambda h: (h, 0, 0))],
        out_specs=[pl.BlockSpec((lp, HG_DK), lambda h: (0, h)),
                   pl.BlockSpec((None, n_blocks, HG_DK, HG_DK), lambda h: (h, 0, 0, 0))],
        out_shape=[jax.ShapeDtypeStruct((lp, hh * HG_DK), F32),
                   jax.ShapeDtypeStruct((hh, n_blocks, HG_DK, HG_DK), F32)],
        scratch_shapes=[pltpu.VMEM((len(HG_HALVES), HG_ROWS, HG_ROWS), F32)],
        compiler_params=_cp("parallel"), name=name)(proj, proj, proj, lb)


def _hg_scan_bwd(proj, lb, states, do, *, reverse, col_q, col_z, col_i, hh, name):
    lp = proj.shape[0]
    n_blocks = lp // HG_ROWS
    last = 0 if reverse else HG_ROWS - 1

    def body(q_ref, z_ref, i_ref, lb_ref, st_ref, do_ref, dq_ref, dz_ref, dv_ref, dlb_ref, mask_ref):
        lbv = lb_ref[...]
        pos = lax.broadcasted_iota(jnp.int32, (HG_ROWS, 1), 0)
        ri = lax.broadcasted_iota(jnp.int32, (HG_ROWS, HG_ROWS), 0)
        ci = lax.broadcasted_iota(jnp.int32, (HG_ROWS, HG_ROWS), 1)
        _pair_masks(mask_ref)

        def block(bi, carry):
            dst, dlb = carry
            bb = bi if reverse else (n_blocks - 1 - bi)
            r0 = pl.multiple_of(bb * HG_ROWS, HG_ROWS)
            zq = q_ref[pl.ds(r0, HG_ROWS), :]
            v16 = i_ref[pl.ds(r0, HG_ROWS), :].astype(BF16)
            do16 = do_ref[pl.ds(r0, HG_ROWS), :].astype(BF16)
            qh, sq, s, f, g, kk = _hg_gates(zq, z_ref[pl.ds(r0, HG_ROWS), :], lbv)
            b = _block_cumsum(g, pos, reverse)
            bl = b[last:last + 1, :]
            eb = jnp.exp(b)
            ebl = jnp.exp(bl - b)
            decay = jnp.exp(bl)
            qe16 = (qh * eb).astype(BF16)
            kd16 = (kk * ebl).astype(BF16)
            st = st_ref[bb]
            st16, dst16 = st.astype(BF16), dst.astype(BF16)
            same_row = ri == ci
            da = _nt(do16, v16)
            da_diag = jnp.sum(jnp.where(same_row, da, 0.0), axis=1, keepdims=True)
            dq_state = eb * _nn(do16, st16)
            dk_state = ebl * _nn(v16, dst16)
            dq = dq_state + da_diag * kk
            dk = dk_state + da_diag * qh
            dbl = (decay * jnp.sum(st * dst, axis=0, keepdims=True)
                   + jnp.sum(kk * dk_state, axis=0, keepdims=True))
            db = qh * dq_state - kk * dk_state + jnp.where(pos == last, dbl, 0.0)
            a = jnp.where(same_row, jnp.sum(qh * kk, axis=1, keepdims=True), 0.0)
            for i, (eq, ek) in enumerate(_pair_levels(b, pos, reverse)):
                same = mask_ref[i]
                q16, k16 = (qh * eq).astype(BF16), (kk * ek).astype(BF16)
                a = a + same * _nt(q16, k16)
                da16 = (same * da).astype(BF16)
                gq, gk = _nn(da16, k16), _tn(da16, q16)
                dq = dq + eq * gq
                dk = dk + ek * gk
                db = db + (q16.astype(F32) * gq - k16.astype(F32) * gk)
            dg = _block_cumsum(db, pos, not reverse)
            df = dg / f - dk
            dq_ref[pl.ds(r0, HG_ROWS), :] = dq * (sq + zq * sq * (1.0 - sq))
            dz_ref[pl.ds(r0, HG_ROWS), :] = df * (1.0 - lbv) * s * (1.0 - s)
            dv_ref[pl.ds(r0, HG_ROWS), :] = _nt(kd16, dst16) + _tn(a.astype(BF16), do16)
            return (decay * dst + _tn(do16, qe16),
                    dlb + jnp.sum(df * (1.0 - s), axis=0, keepdims=True))

        _, dlb = lax.fori_loop(0, n_blocks, block,
                               (jnp.zeros((HG_DK, HG_DK), F32), jnp.zeros((1, HG_DK), F32)))
        dlb_ref[...] = dlb

    cspec = lambda col: pl.BlockSpec((lp, HG_DK), lambda h: (0, col // HG_DK + h))
    ospec = pl.BlockSpec((lp, HG_DK), lambda h: (0, h))
    sds = jax.ShapeDtypeStruct((lp, hh * HG_DK), F32)
    return pl.pallas_call(
        body, grid=(hh,),
        in_specs=[cspec(col_q), cspec(col_z), cspec(col_i),
                  pl.BlockSpec((None, 1, HG_DK), lambda h: (h, 0, 0)),
                  pl.BlockSpec((None, n_blocks, HG_DK, HG_DK), lambda h: (h, 0, 0, 0)),
                  ospec],
        out_specs=[ospec, ospec, ospec, pl.BlockSpec((None, 1, HG_DK), lambda h: (h, 0, 0))],
        out_shape=[sds, sds, sds, jax.ShapeDtypeStruct((hh, 1, HG_DK), F32)],
        scratch_shapes=[pltpu.VMEM((len(HG_HALVES), HG_ROWS, HG_ROWS), F32)],
        compiler_params=_cp("parallel"), name=name)(proj, proj, proj, lb, states, do)


NA_HB = LANES // NA_HEAD_DIM
NA_G = 4
NA_U = NA_G + NA_WIN_H
NA_QN = NA_G * GRID_W
NA_KN = NA_U * GRID_W


def _na_table_index(pattern, a, j):
    if pattern == 0:
        return j - a + NA_WIN_H - 1 if j < NA_WIN_H else None
    if pattern == 2:
        return j - a - 1 if j >= NA_U - NA_WIN_H else None
    return j - a + NA_WIN_H // 2 - 1 if a <= j < a + NA_WIN_H else None


def _na_step_rows(pattern, t, rows):
    if pattern == 0:
        r0, us = 0, 0
    elif pattern == 2:
        r0, us = rows - NA_G, rows - NA_U
    else:
        r0 = NA_G * t
        us = r0 - NA_WIN_H // 2
    q0, k0 = N_META + GRID_W * r0, N_META + GRID_W * us
    if pattern == 1:
        q0, k0 = pl.multiple_of(q0, 16), pl.multiple_of(k0, 16)
    return q0, k0


def _na_fill_bias(tb_ref, bias_ref):
    neg = jnp.full((GRID_W, GRID_W), -1e30, F32)
    for h in range(NA_HB):
        for pattern in range(3):
            for a in range(NA_G):
                for j in range(NA_U):
                    idx = _na_table_index(pattern, a, j)
                    bias_ref[h, pattern, a * GRID_W:(a + 1) * GRID_W, j * GRID_W:(j + 1) * GRID_W] = (
                        neg if idx is None else tb_ref[h, idx])


def _na_steps(rows, step, carry):
    n_steps = rows // NA_G
    carry = step(0, 0, carry)
    carry = lax.fori_loop(1, n_steps - 1, functools.partial(step, 1), carry)
    return step(2, n_steps - 1, carry)


def _na_head_lanes():
    lane = lax.broadcasted_iota(jnp.int32, (1, LANES), 1)
    return [lane // NA_HEAD_DIM == h for h in range(NA_HB)]


def _na_only(mask, x):
    return jnp.where(mask, x, jnp.zeros_like(x))


def _na_fwd(proj, tb, *, n_tok, nh, name):
    lp = proj.shape[0]
    dh, hb = NA_HEAD_DIM, NA_HB
    naw = nh * dh
    rows = n_tok // GRID_W
    scale = dh ** -0.5

    def body(q_ref, k_ref, v_ref, tb_ref, o_ref, lse_ref, q16_ref, k16_ref, v16_ref, bias_ref):
        o_ref[...] = jnp.zeros_like(o_ref)
        lse_ref[...] = jnp.zeros_like(lse_ref)
        q16_ref[...] = q_ref[...].astype(BF16)
        k16_ref[...] = k_ref[...].astype(BF16)
        v16_ref[...] = v_ref[...].astype(BF16)
        _na_fill_bias(tb_ref, bias_ref)
        heads = _na_head_lanes()
        km = k16_ref[0:N_META, :]
        vm = v16_ref[0:N_META, :]
        qm = q16_ref[0:N_META, :]
        o_m = None
        for h in range(hb):
            s = _nt(_na_only(heads[h], qm), km) * scale
            m = jnp.max(s, axis=1, keepdims=True)
            p = jnp.exp(s - m)
            l = jnp.sum(p, axis=1, keepdims=True)
            o_h = _nn(p.astype(BF16), vm) / l
            o_m = o_h if o_m is None else jnp.where(heads[h], o_h, o_m)
            lse_ref[h, 0:N_META, :] = m + jnp.log(l)
        o_ref[0:N_META, :] = o_m

        def step(pattern, t, carry):
            q0, k0 = _na_step_rows(pattern, t, rows)
            q16 = q16_ref[pl.ds(q0, NA_QN), :]
            k16 = k16_ref[pl.ds(k0, NA_KN), :]
            v16 = v16_ref[pl.ds(k0, NA_KN), :]
            o = None
            for h in range(hb):
                q_h = _na_only(heads[h], q16)
                s = _nt(q_h, k16) * scale + bias_ref[h, pattern]
                sm = _nt(q_h, km) * scale
                m = jnp.maximum(jnp.max(s, axis=1, keepdims=True),
                                jnp.max(sm, axis=1, keepdims=True))
                p = jnp.exp(s - m)
                pm = jnp.exp(sm - m)
                l = jnp.sum(p, axis=1, keepdims=True) + jnp.sum(pm, axis=1, keepdims=True)
                o_h = (_nn(p.astype(BF16), v16) + _nn(pm.astype(BF16), vm)) / l
                o = o_h if o is None else jnp.where(heads[h], o_h, o)
                lse_ref[h, pl.ds(q0, NA_QN), :] = m + jnp.log(l)
            o_ref[pl.ds(q0, NA_QN), :] = o
            return carry

        _na_steps(rows, step, 0)

    cblk = lambda col: pl.BlockSpec((lp, LANES), lambda g: (0, col // LANES + g))
    return pl.pallas_call(
        body, grid=(nh // hb,),
        in_specs=[cblk(0), cblk(naw), cblk(2 * naw),
                  pl.BlockSpec((hb, 2 * NA_WIN_H - 1, GRID_W, GRID_W), lambda g: (g, 0, 0, 0))],
        out_specs=[cblk(0), pl.BlockSpec((hb, lp, 1), lambda g: (g, 0, 0))],
        out_shape=[jax.ShapeDtypeStruct((lp, naw), F32), jax.ShapeDtypeStruct((nh, lp, 1), F32)],
        scratch_shapes=[pltpu.VMEM((lp, LANES), BF16)] * 3 + [pltpu.VMEM((hb, 3, NA_QN, NA_KN), F32)],
        compiler_params=_cp("parallel"), name=name)(proj, proj, proj, tb)


def _na_bwd(proj, tb, o, lse, do, *, n_tok, nh, name):
    lp = proj.shape[0]
    dh, hb = NA_HEAD_DIM, NA_HB
    naw = nh * dh
    rows = n_tok // GRID_W
    scale = dh ** -0.5

    def body(q_ref, k_ref, v_ref, tb_ref, o_ref, lse_ref, do_ref, dq_ref, dk_ref, dv_ref, dtb_ref,
             q16_ref, k16_ref, v16_ref, bias_ref):
        dq_ref[...] = jnp.zeros_like(dq_ref)
        dk_ref[...] = jnp.zeros_like(dk_ref)
        dv_ref[...] = jnp.zeros_like(dv_ref)
        dtb_ref[...] = jnp.zeros_like(dtb_ref)
        q16_ref[...] = q_ref[...].astype(BF16)
        k16_ref[...] = k_ref[...].astype(BF16)
        v16_ref[...] = v_ref[...].astype(BF16)
        _na_fill_bias(tb_ref, bias_ref)
        heads = _na_head_lanes()
        km = k16_ref[0:N_META, :]
        vm = v16_ref[0:N_META, :]
        qm = q16_ref[0:N_META, :]
        dom = do_ref[0:N_META, :]
        prod = dom * o_ref[0:N_META, :]
        dq_m = None
        dkm0 = jnp.zeros((N_META, LANES), F32)
        dvm0 = jnp.zeros((N_META, LANES), F32)
        for h in range(hb):
            q_h = _na_only(heads[h], qm)
            do_h = _na_only(heads[h], dom).astype(BF16)
            p = jnp.exp(_nt(q_h, km) * scale - lse_ref[h, 0:N_META, :])
            delta = jnp.sum(_na_only(heads[h], prod), axis=1, keepdims=True)
            ds = (p * (_nt(do_h, vm) - delta)).astype(BF16)
            dq_h = _nn(ds, km) * scale
            dq_m = dq_h if dq_m is None else jnp.where(heads[h], dq_h, dq_m)
            dkm0 = dkm0 + _tn(ds, q_h) * scale
            dvm0 = dvm0 + _tn(p.astype(BF16), do_h)
        dq_ref[0:N_META, :] = dq_m

        def step(pattern, t, carry):
            dkm, dvm = carry
            q0, k0 = _na_step_rows(pattern, t, rows)
            q16 = q16_ref[pl.ds(q0, NA_QN), :]
            k16 = k16_ref[pl.ds(k0, NA_KN), :]
            v16 = v16_ref[pl.ds(k0, NA_KN), :]
            dov = do_ref[pl.ds(q0, NA_QN), :]
            prod = dov * o_ref[pl.ds(q0, NA_QN), :]
            dq = None
            dk = jnp.zeros((NA_KN, LANES), F32)
            dv = jnp.zeros((NA_KN, LANES), F32)
            for h in range(hb):
                q_h = _na_only(heads[h], q16)
                do_h = _na_only(heads[h], dov).astype(BF16)
                lse = lse_ref[h, pl.ds(q0, NA_QN), :]
                p = jnp.exp(_nt(q_h, k16) * scale + bias_ref[h, pattern] - lse)
                pm = jnp.exp(_nt(q_h, km) * scale - lse)
                delta = jnp.sum(_na_only(heads[h], prod), axis=1, keepdims=True)
                ds = p * (_nt(do_h, v16) - delta)
                dsm = (pm * (_nt(do_h, vm) - delta)).astype(BF16)
                ds16 = ds.astype(BF16)
                dq_h = (_nn(ds16, k16) + _nn(dsm, km)) * scale
                dq = dq_h if dq is None else jnp.where(heads[h], dq_h, dq)
                dk = dk + _tn(ds16, q_h) * scale
                dv = dv + _tn(p.astype(BF16), do_h)
                dkm = dkm + _tn(dsm, q_h) * scale
                dvm = dvm + _tn(pm.astype(BF16), do_h)
                for a in range(NA_G):
                    for j in range(NA_U):
                        idx = _na_table_index(pattern, a, j)
                        if idx is not None:
                            dtb_ref[h, idx] += ds[a * GRID_W:(a + 1) * GRID_W,
                                                  j * GRID_W:(j + 1) * GRID_W]
            dq_ref[pl.ds(q0, NA_QN), :] = dq
            dk_ref[pl.ds(k0, NA_KN), :] += dk
            dv_ref[pl.ds(k0, NA_KN), :] += dv
            return dkm, dvm

        dkm, dvm = _na_steps(rows, step, (dkm0, dvm0))
        dk_ref[0:N_META, :] += dkm
        dv_ref[0:N_META, :] += dvm

    cblk = lambda col: pl.BlockSpec((lp, LANES), lambda g: (0, col // LANES + g))
    tbs = pl.BlockSpec((hb, 2 * NA_WIN_H - 1, GRID_W, GRID_W), lambda g: (g, 0, 0, 0))
    sds = jax.ShapeDtypeStruct((lp, naw), F32)
    return pl.pallas_call(
        body, grid=(nh // hb,),
        in_specs=[cblk(0), cblk(naw), cblk(2 * naw), tbs, cblk(0),
                  pl.BlockSpec((hb, lp, 1), lambda g: (g, 0, 0)), cblk(0)],
        out_specs=[cblk(0), cblk(0), cblk(0), tbs],
        out_shape=[sds, sds, sds, jax.ShapeDtypeStruct(tb.shape, F32)],
        scratch_shapes=[pltpu.VMEM((lp, LANES), BF16)] * 3 + [pltpu.VMEM((hb, 3, NA_QN, NA_KN), F32)],
        compiler_params=_cp("parallel"), name=name)(proj, proj, proj, tb, o, lse, do)


def _rpb_onehot():
    c = np.arange(GRID_W)[:, None]
    w = np.arange(GRID_W)[None, :]
    cs = np.clip(c - NA_WIN_W // 2, 0, GRID_W - NA_WIN_W)
    in_win = (w >= cs) & (w < cs + NA_WIN_W)
    dc = np.clip(w - c, -(NA_WIN_W - 1), NA_WIN_W - 1) + NA_WIN_W - 1
    oh = np.zeros((LANES, GRID_W * GRID_W), np.float32)
    flat = np.arange(GRID_W * GRID_W).reshape(GRID_W, GRID_W)
    oh[dc[in_win], flat[in_win]] = 1.0
    neg = np.where(in_win, 0.0, -1e30).astype(np.float32).reshape(1, -1)
    return oh, neg


def _assemble_dproj(dq_na, dk_na, dv_na, dq_f, dq_b, dz_f, dz_b, dv_f, dv_b, dg, dgn, dgh, *, name):
    lp, naw = dq_na.shape
    hgw = dq_f.shape[1]
    d = dgn.shape[1]
    cols = 3 * naw + 5 * hgw + 2 * d
    tr = _row_tile(lp, 3 * naw * 4 + 6 * hgw * 4 + hgw * 2 + 2 * d * 2 + cols * 2)

    def body(nq_ref, nk_ref, nv_ref, qf_ref, qb_ref, zf_ref, zb_ref, vf_ref, vb_ref, g_ref, gn_ref,
             gh_ref, o_ref):
        o_ref[:, 0:naw] = nq_ref[...].astype(BF16)
        o_ref[:, naw:2 * naw] = nk_ref[...].astype(BF16)
        o_ref[:, 2 * naw:3 * naw] = nv_ref[...].astype(BF16)
        c0 = 3 * naw
        o_ref[:, c0:c0 + hgw] = (qf_ref[...] + qb_ref[...]).astype(BF16)
        o_ref[:, c0 + hgw:c0 + 2 * hgw] = zf_ref[...].astype(BF16)
        o_ref[:, c0 + 2 * hgw:c0 + 3 * hgw] = zb_ref[...].astype(BF16)
        o_ref[:, c0 + 3 * hgw:c0 + 4 * hgw] = (vf_ref[...] + vb_ref[...]).astype(BF16)
        o_ref[:, c0 + 4 * hgw:c0 + 5 * hgw] = g_ref[...]
        o_ref[:, c0 + 5 * hgw:c0 + 5 * hgw + d] = gn_ref[...]
        o_ref[:, c0 + 5 * hgw + d:] = gh_ref[...]

    hg, na = _rspec(tr, hgw), _rspec(tr, naw)
    return pl.pallas_call(
        body, grid=(lp // tr,),
        in_specs=[na, na, na, hg, hg, hg, hg, hg, hg, hg, _rspec(tr, d), _rspec(tr, d)],
        out_specs=_rspec(tr, cols),
        out_shape=jax.ShapeDtypeStruct((lp, cols), BF16),
        compiler_params=_cp("parallel"), name=name)(dq_na, dk_na, dv_na, dq_f, dq_b, dz_f, dz_b,
                                                    dv_f, dv_b, dg, dgn, dgh)


GROUP_STEPS = 8


def _group_tiles(rows, align):
    steps = GROUP_STEPS if all(r % (GROUP_STEPS * align) == 0 for r in rows) else 1
    return steps, [r // steps for r in rows]


def _adamw(ws, gs, ms, vs, *, name):
    n = len(ws)
    steps, trs = _group_tiles([w.shape[0] for w in ws], 8)

    def body(*refs):
        for i in range(n):
            w_ref, g_ref, m_ref, v_ref = refs[4 * i:4 * i + 4]
            d_ref, mo_ref, vo_ref = refs[4 * n + 3 * i:4 * n + 3 * i + 3]
            gv = g_ref[...]
            mn = ADAM_B1 * m_ref[...] + (1.0 - ADAM_B1) * gv
            vn = ADAM_B2 * v_ref[...] + (1.0 - ADAM_B2) * (gv * gv)
            m_hat = mn / (1.0 - ADAM_B1 ** ADAM_STEP)
            v_hat = vn / (1.0 - ADAM_B2 ** ADAM_STEP)
            d_ref[...] = -ADAM_LR * (m_hat / (jnp.sqrt(v_hat) + ADAM_EPS) + ADAM_WD * w_ref[...])
            mo_ref[...] = mn
            vo_ref[...] = vn

    specs = [_rspec(tr, w.shape[1]) for tr, w in zip(trs, ws)]
    out = pl.pallas_call(
        body, grid=(steps,),
        in_specs=[s for s in specs for _ in range(4)],
        out_specs=[s for s in specs for _ in range(3)],
        out_shape=[jax.ShapeDtypeStruct(w.shape, F32) for w in ws for _ in range(3)],
        compiler_params=_cp("parallel"), name=name)(*[a for q in zip(ws, gs, ms, vs) for a in q])
    return [tuple(out[3 * i:3 * i + 3]) for i in range(n)]


def _local_step(x, tgt, meta, first_weight, rest_weights, g_mix, g_mlp, g_fin, hg_gain, rpb, lb,
                early_grads=None, mid_grads=None, late_grad=None, rest_landed=None,
                last_grads=None):
    n_tok, d = x.shape
    hgw = hg_gain.shape[1]
    nh, hh = rpb.shape[0], hgw // HG_DK
    naw = nh * NA_HEAD_DIM
    l_real = N_META + n_tok
    lp = -(-l_real // ROW_ALIGN) * ROW_ALIGN
    col_qhg = 3 * naw
    col_zf, col_zb, col_i, col_g = (col_qhg + hgw, col_qhg + 2 * hgw, col_qhg + 3 * hgw,
                                    col_qhg + 4 * hgw)
    col_gate = col_qhg + 5 * hgw

    oh_np, neg_np = _rpb_onehot()
    oh = jnp.asarray(oh_np)
    rpb_p = jnp.pad(rpb.reshape(nh * (2 * NA_WIN_H - 1), 2 * NA_WIN_W - 1),
                    ((0, 0), (0, LANES - (2 * NA_WIN_W - 1))))
    tb = _matmul(rpb_p, oh, tm=rpb_p.shape[0], tn=512, tk=LANES, precision=HIGHEST,
                 name="rpb_expand")
    tb = (tb + jnp.asarray(neg_np)).reshape(nh, 2 * NA_WIN_H - 1, GRID_W, GRID_W)

    h0, a = _embed_norm(x, meta, g_mix, lp=lp, name="norm_mix")
    w_in = first_weight(a)
    proj = _matmul(a, w_in, name="mm_in")
    o_na, lse = _na_fwd(proj, tb, n_tok=n_tok, nh=nh, name="na_fwd")
    lb_f = lb[0].reshape(hh, 1, HG_DK)
    lb_b = lb[1].reshape(hh, 1, HG_DK)
    scan_kw = dict(col_q=col_qhg, col_i=col_i, hh=hh)
    o_f, st_f = _hg_scan_fwd(proj, lb_f, reverse=False, col_z=col_zf, name="hg_scan_f", **scan_kw)
    token = rest_landed(o_f) if rest_landed else None
    lb_b_late = lb_b if token is None else lb_b + token[0:1, 0:1]
    o_b, st_b = _hg_scan_fwd(proj, lb_b_late, reverse=True, col_z=col_zb, name="hg_scan_b",
                             **scan_kw)
    o_hg = _hg_out(o_f, o_b, proj, hg_gain, col_g=col_g, name="hg_out")
    w_na, w_hg, w_o, w_up, w_down = rest_weights(o_hg)
    y_na = _matmul(o_na, w_na, name="mm_na_out", out_dtype=BF16)
    gates = ((proj, col_gate), (proj, col_gate + d))

    def mix_gates(acc, gn, gh, yn):
        return acc, _sigmoid(gn) * yn + _sigmoid(gh) * acc

    def mix_gates_bwd(dmix, gn, gh, yn, yh):
        sn, sh = _sigmoid(gn), _sigmoid(gh)
        return dmix * sn, dmix * sh, dmix * yn * sn * (1.0 - sn), dmix * yh * sh * (1.0 - sh)

    y_hg, mix = _matmul(o_hg, w_hg, name="mm_hg_out", epilogue=mix_gates,
                        tiles=(*gates, (y_na, 0)), out_dtypes=(BF16, BF16))
    t1 = _matmul(mix, w_o, name="mm_o")
    h1, mlp_in = _residual_norm(h0, t1, g_mlp, name="resid_norm_mlp")
    u, act = _matmul(mlp_in, w_up, name="mm_up", out_dtypes=(BF16, BF16),
                     epilogue=lambda acc: (acc, jnp.square(jnp.maximum(acc, 0.0))))
    t2 = _matmul(act, w_down, name="mm_down")
    dh2, dh2_16, loss, dg_fin = _final_loss(h1, t2, g_fin, tgt, name="final_loss")

    (du,) = _matmul(dh2_16, w_down, tb=True, name="mm_down_dx", tiles=((u, 0),),
                    out_dtypes=(BF16,),
                    epilogue=lambda acc, uv: (acc * 2.0 * jnp.maximum(uv, 0.0),))
    dw_down = _matmul(act, dh2_16, ta=True, name="mm_down_dw")
    dm = _matmul(du, w_up, tb=True, name="mm_up_dx")
    dw_up = _matmul(mlp_in, du, ta=True, name="mm_up_dw")
    dh1, dh1_16, dg_mlp = _rmsnorm_bwd_add(h1, g_mlp, dm, dh2, name="norm_mlp_bwd")
    dy_na, dy_hg, dgn, dgh = _matmul(dh1_16, w_o, tb=True, name="mm_o_dx", epilogue=mix_gates_bwd,
                                     tiles=(*gates, (y_na, 0), (y_hg, 0)), out_dtypes=(BF16,) * 4)
    dw_o = _matmul(mix, dh1_16, ta=True, name="mm_o_dw")
    do_na = _matmul(dy_na, w_na, tb=True, name="mm_na_out_dx")
    dw_na = _matmul(o_na, dy_na, ta=True, name="mm_na_out_dw")
    do_hg = _matmul(dy_hg, w_hg, tb=True, name="mm_hg_out_dx")
    dw_hg = _matmul(o_hg, dy_hg, ta=True, name="mm_hg_out_dw")
    token = early_grads([dw_na, dw_hg, dw_o, dw_up, dw_down]) if early_grads else None
    if token is not None:
        hg_gain = hg_gain + token[0:1, 0:1]
    d_o, dg_hg, d_gain = _hg_out_bwd(o_f, o_b, proj, hg_gain, do_hg, col_g=col_g, name="hg_out_bwd")
    dq_f, dz_f, dv_f, dlb_f = _hg_scan_bwd(proj, lb_f, st_f, d_o, reverse=False, col_z=col_zf,
                                           name="hg_scan_f_bwd", **scan_kw)
    token = mid_grads(dq_f) if mid_grads else None
    lb_b_late = lb_b if token is None else lb_b + token[0:1, 0:1]
    dq_b, dz_b, dv_b, dlb_b = _hg_scan_bwd(proj, lb_b_late, st_b, d_o, reverse=True, col_z=col_zb,
                                           name="hg_scan_b_bwd", **scan_kw)
    dq_na, dk_na, dv_na, dtb = _na_bwd(proj, tb, o_na, lse, do_na, n_tok=n_tok, nh=nh, name="na_bwd")
    dproj = _assemble_dproj(dq_na, dk_na, dv_na, dq_f, dq_b, dz_f, dz_b, dv_f, dv_b, dg_hg, dgn,
                            dgh, name="assemble_dproj")
    dw_in = _matmul(a, dproj, ta=True, name="mm_in_dw")
    token = late_grad(dw_in) if late_grad else None
    da = _matmul(dproj, w_in, tb=True, name="mm_in_dx", after=token)
    token = last_grads(da) if last_grads else None
    g_mix_late = g_mix if token is None else g_mix + token[0:1, 0:1]
    dx, dmeta, dg_mix = _rmsnorm_bwd_tokens(h0, g_mix_late, da, dh1, n_tok=n_tok,
                                            name="norm_mix_bwd")
    d_rpb = _matmul(dtb.reshape(nh * (2 * NA_WIN_H - 1), GRID_W * GRID_W), oh, tb=True,
                    tm=nh * (2 * NA_WIN_H - 1), tn=LANES, tk=1024, precision=HIGHEST,
                    name="rpb_reduce")
    d_lb = jnp.concatenate([dlb_f.reshape(1, hgw), dlb_b.reshape(1, hgw)], axis=0)
    return (loss, dx, dmeta, dw_in, dw_na, dw_hg, dw_o, dw_up, dw_down,
            dg_mix, dg_mlp, dg_fin, d_gain, d_rpb, d_lb)


N_CHIPS = 4
N_DEV = 8
ANY = pl.BlockSpec(memory_space=pl.ANY)


def _place():
    x, y, c = lax.axis_index("x"), lax.axis_index("y"), lax.axis_index("c")
    others = []
    for j in (1, 2, 3):
        tx = (1 - x) if (j >> 1) else x
        ty = (1 - y) if (j & 1) else y
        others.append((tx, ty))
    return x, y, c, others


def _piece(ref, axis, k, half, rh, cs):
    if axis == 1:
        return ref.at[pl.ds(pl.multiple_of(half * rh, 16), rh), pl.ds(pl.multiple_of(k * cs, LANES), cs)]
    return ref.at[pl.ds(pl.multiple_of(k * 2 * rh + half * rh, 16), rh), :]


def _cast_into_full(shards, axes, place, *, name):
    n = len(shards)
    steps, trs = _group_tiles([s.shape[0] for s in shards], 16)

    def body(p_ref, *refs):
        for i in range(n):
            refs[n + i][...] = refs[i][...].astype(BF16)

    def out_spec(tr, cs, axis):
        if axis == 1:
            return pl.BlockSpec((tr, cs), lambda i, p_ref: (i, p_ref[0]))
        return pl.BlockSpec((tr, cs), lambda i, p_ref: (p_ref[0] * steps + i, 0))

    return pl.pallas_call(
        body,
        grid_spec=pltpu.PrefetchScalarGridSpec(
            num_scalar_prefetch=1, grid=(steps,),
            in_specs=[pl.BlockSpec((tr, s.shape[1]), lambda i, p_ref: (i, 0))
                      for tr, s in zip(trs, shards)],
            out_specs=[out_spec(tr, s.shape[1], ax) for tr, s, ax in zip(trs, shards, axes)]),
        out_shape=[jax.ShapeDtypeStruct((s.shape[0], s.shape[1] * N_CHIPS) if ax == 1
                                        else (s.shape[0] * N_CHIPS, s.shape[1]), BF16)
                   for s, ax in zip(shards, axes)],
        compiler_params=_cp("parallel"), name=name)(place, *shards)


HBM_SPEC = pl.BlockSpec(memory_space=pltpu.HBM)
SEM_SPEC = pl.BlockSpec(memory_space=pltpu.SEMAPHORE)
SPLIT_COPY = pltpu.CompilerParams(has_side_effects=pltpu.SideEffectType.DATAFLOW_SIDE_EFFECTING)
TOKEN = jax.ShapeDtypeStruct((8, LANES), F32)


def _geo(fulls, axes):
    out = []
    for f, ax in zip(fulls, axes):
        r, cs = (f.shape[0], f.shape[1] // N_CHIPS) if ax == 1 else (f.shape[0] // N_CHIPS, f.shape[1])
        out.append((ax, r // 2, cs))
    return out


def _gather_copies(refs, geo, send_sems, recv_sems):
    x, y, c, others = _place()
    chip = 2 * x + y
    cps = []
    for i, (ax, rh, cs) in enumerate(geo):
        mine = _piece(refs[i], ax, chip, c, rh, cs)
        for j, (tx, ty) in enumerate(others):
            cps.append(pltpu.make_async_remote_copy(
                src_ref=mine, dst_ref=mine, send_sem=send_sems.at[3 * i + j],
                recv_sem=recv_sems.at[3 * i + j], device_id=(tx, ty, c), device_id_type=MESH))
    return cps


def _allgather_start(fulls, axes, after, *, name):
    n = len(fulls)
    geo = _geo(fulls, axes)

    def body(*refs):
        w_refs = refs[:n]
        send_sems, recv_sems = refs[n + 1], refs[n + 2]
        token = refs[2 * n + 3]
        for cp in _gather_copies(w_refs, geo, send_sems, recv_sems):
            cp.start()
        token[...] = jnp.zeros_like(token)

    out = pl.pallas_call(
        body, name=name,
        out_shape=(pltpu.SemaphoreType.DMA((3 * n,)), pltpu.SemaphoreType.DMA((3 * n,)),
                   *[pltpu.HBM(f.shape, f.dtype) for f in fulls], TOKEN),
        in_specs=[HBM_SPEC] * n + [ANY],
        out_specs=(SEM_SPEC, SEM_SPEC, *[HBM_SPEC] * n, pl.BlockSpec(memory_space=pltpu.VMEM)),
        input_output_aliases={i: 2 + i for i in range(n)},
        compiler_params=SPLIT_COPY,
    )(*[pltpu.with_memory_space_constraint(f, pltpu.HBM) for f in fulls], after)
    return out[0], out[1], list(out[2:2 + n]), out[2 + n]


def _allgather_wait(send_sems, recv_sems, fulls, axes, after, *, name):
    n = len(fulls)
    geo = _geo(fulls, axes)

    def body(*refs):
        w_refs = refs[:n]
        for cp in _gather_copies(w_refs, geo, refs[n], refs[n + 1]):
            cp.wait_send()
            cp.wait_recv()

    return list(pl.pallas_call(
        body, name=name,
        out_shape=[pltpu.HBM(f.shape, f.dtype) for f in fulls],
        in_specs=[HBM_SPEC] * n + [SEM_SPEC, SEM_SPEC, ANY],
        out_specs=[HBM_SPEC] * n,
        input_output_aliases={i: i for i in range(n)},
        compiler_params=SPLIT_COPY,
    )(*fulls, send_sems, recv_sems, after))


def _allgather_forward(fulls, axes, *, name):
    n = len(fulls)
    geo = _geo(fulls, axes)

    def body(*refs):
        o_refs = refs[n:2 * n]
        send_sems, recv_sems = refs[2 * n:]
        x, y, c, others = _place()

        def rcopy(i, j, half, to):
            ax, rh, cs = geo[i]
            ref = _piece(o_refs[i], ax, 2 * others[j][0] + others[j][1], half, rh, cs)
            return pltpu.make_async_remote_copy(
                src_ref=ref, dst_ref=ref, send_sem=send_sems.at[3 * i + j],
                recv_sem=recv_sems.at[3 * i + j], device_id=to, device_id_type=MESH)

        cps = [rcopy(i, j, c, (x, y, 1 - c)) for i in range(n) for j in range(3)]
        for cp in cps:
            cp.start()
        for i in range(n):
            for j in range(3):
                rcopy(i, j, 1 - c, (x, y, c)).wait_recv()
        for cp in cps:
            cp.wait_send()

    return list(pl.pallas_call(
        body, in_specs=[ANY] * n, out_specs=[ANY] * n,
        out_shape=[jax.ShapeDtypeStruct(f.shape, f.dtype) for f in fulls],
        input_output_aliases={i: i for i in range(n)},
        scratch_shapes=[pltpu.SemaphoreType.DMA((3 * n,)), pltpu.SemaphoreType.DMA((3 * n,))],
        name=name)(*fulls))


def _forward_copies(refs, geo, send_sems, recv_sems):
    x, y, c, others = _place()
    cps = []
    for i, (ax, rh, cs) in enumerate(geo):
        for j, (tx, ty) in enumerate(others):
            ref = _piece(refs[i], ax, 2 * tx + ty, c, rh, cs)
            cps.append(pltpu.make_async_remote_copy(
                src_ref=ref, dst_ref=ref, send_sem=send_sems.at[3 * i + j],
                recv_sem=recv_sems.at[3 * i + j], device_id=(x, y, 1 - c), device_id_type=MESH))
    return cps


def _allgather_forward_start(fulls, axes, *, name):
    n = len(fulls)
    geo = _geo(fulls, axes)

    def body(*refs):
        token = refs[2 * n + 2]
        for cp in _forward_copies(refs[:n], geo, refs[n], refs[n + 1]):
            cp.start()
        token[...] = jnp.zeros_like(token)

    out = pl.pallas_call(
        body, name=name,
        out_shape=(pltpu.SemaphoreType.DMA((3 * n,)), pltpu.SemaphoreType.DMA((3 * n,)),
                   *[pltpu.HBM(f.shape, f.dtype) for f in fulls], TOKEN),
        in_specs=[HBM_SPEC] * n,
        out_specs=(SEM_SPEC, SEM_SPEC, *[HBM_SPEC] * n, pl.BlockSpec(memory_space=pltpu.VMEM)),
        input_output_aliases={i: 2 + i for i in range(n)},
        compiler_params=SPLIT_COPY,
    )(*fulls)
    return out[0], out[1], list(out[2:2 + n]), out[2 + n]


def _allgather_forward_wait(send_sems, recv_sems, fulls, axes, after, *, name):
    n = len(fulls)
    geo = _geo(fulls, axes)

    def body(*refs):
        for cp in _forward_copies(refs[:n], geo, refs[n], refs[n + 1]):
            cp.wait_send()
            cp.wait_recv()

    return list(pl.pallas_call(
        body, name=name,
        out_shape=[pltpu.HBM(f.shape, f.dtype) for f in fulls],
        in_specs=[HBM_SPEC] * n + [SEM_SPEC, SEM_SPEC, ANY],
        out_specs=[HBM_SPEC] * n,
        input_output_aliases={i: i for i in range(n)},
        compiler_params=SPLIT_COPY,
    )(*fulls, send_sems, recv_sems, after))


def _chip_copies(blk_ref, land_ref, send_sems, recv_sems):
    x, y, c, others = _place()
    return [pltpu.make_async_remote_copy(
        src_ref=blk_ref, dst_ref=land_ref.at[2 * x + y], send_sem=send_sems.at[j],
        recv_sem=recv_sems.at[j], device_id=(tx, ty, c), device_id_type=MESH)
        for j, (tx, ty) in enumerate(others)]


def _chip_exchange_start(blk, *, name):
    land = pltpu.with_memory_space_constraint(lax.empty((N_CHIPS, *blk.shape), blk.dtype), pltpu.HBM)

    def body(blk_ref, land_ref, send_sems, recv_sems, blk_out, land_out, token):
        for cp in _chip_copies(blk_ref, land_ref, send_sems, recv_sems):
            cp.start()
        token[...] = jnp.zeros_like(token)

    return pl.pallas_call(
        body, name=name,
        out_shape=(pltpu.SemaphoreType.DMA((3,)), pltpu.SemaphoreType.DMA((3,)),
                   pltpu.HBM(blk.shape, blk.dtype), pltpu.HBM(land.shape, land.dtype), TOKEN),
        in_specs=[HBM_SPEC] * 2,
        out_specs=(SEM_SPEC, SEM_SPEC, HBM_SPEC, HBM_SPEC, pl.BlockSpec(memory_space=pltpu.VMEM)),
        input_output_aliases={0: 2, 1: 3},
        compiler_params=SPLIT_COPY,
    )(pltpu.with_memory_space_constraint(blk, pltpu.HBM), land)


def _chip_exchange_wait(send_sems, recv_sems, blk, land, after, *, name):
    def body(blk_ref, land_ref, send_sems, recv_sems, after_ref, blk_out, land_out):
        for cp in _chip_copies(blk_ref, land_ref, send_sems, recv_sems):
            cp.wait_send()
            cp.wait_recv()

    return pl.pallas_call(
        body, name=name,
        out_shape=[pltpu.HBM(blk.shape, blk.dtype), pltpu.HBM(land.shape, land.dtype)],
        in_specs=[HBM_SPEC] * 2 + [SEM_SPEC, SEM_SPEC, ANY],
        out_specs=[HBM_SPEC] * 2,
        input_output_aliases={0: 0, 1: 1},
        compiler_params=SPLIT_COPY,
    )(blk, land, send_sems, recv_sems, after)[1]


def _scatter_geo(parts, axes):
    out = []
    for p, ax in zip(parts, axes):
        _, rh, cols = p.shape
        out.append((ax, rh, cols // N_CHIPS if ax == 1 else cols))
    return out


def _scatter_copies(p_refs, q_refs, geo, send_sems, recv_sems):
    x, y, c, others = _place()
    chip = 2 * x + y
    cps = []
    for i, (ax, rh, cw) in enumerate(geo):
        for j, (tx, ty) in enumerate(others):
            k = 2 * tx + ty
            src = (p_refs[i].at[0, :, pl.ds(pl.multiple_of(k * cw, LANES), cw)] if ax == 1
                   else p_refs[i].at[k])
            cps.append(pltpu.make_async_remote_copy(
                src_ref=src, dst_ref=q_refs[i].at[chip], send_sem=send_sems.at[3 * i + j],
                recv_sem=recv_sems.at[3 * i + j], device_id=(tx, ty, c), device_id_type=MESH))
    return cps


def _scatter_start(parts, axes, *, name):
    n = len(parts)
    geo = _scatter_geo(parts, axes)
    slots = [pltpu.HBM((N_CHIPS, rh, cw), p.dtype) for p, (_, rh, cw) in zip(parts, geo)]

    def body(*refs):
        p_refs, q_refs = refs[:n], refs[n:2 * n]
        send_sems, recv_sems = refs[2 * n], refs[2 * n + 1]
        token = refs[4 * n + 2]
        for cp in _scatter_copies(p_refs, q_refs, geo, send_sems, recv_sems):
            cp.start()
        token[...] = jnp.zeros_like(token)

    land = [pltpu.with_memory_space_constraint(lax.empty(s.inner_aval.shape, s.inner_aval.dtype), pltpu.HBM)
            for s in slots]
    out = pl.pallas_call(
        body, name=name,
        out_shape=(pltpu.SemaphoreType.DMA((3 * n,)), pltpu.SemaphoreType.DMA((3 * n,)),
                   *[pltpu.HBM(p.shape, p.dtype) for p in parts], *slots, TOKEN),
        in_specs=[HBM_SPEC] * (2 * n),
        out_specs=(SEM_SPEC, SEM_SPEC, *[HBM_SPEC] * (2 * n), pl.BlockSpec(memory_space=pltpu.VMEM)),
        input_output_aliases={i: 2 + i for i in range(2 * n)},
        compiler_params=SPLIT_COPY,
    )(*[pltpu.with_memory_space_constraint(p, pltpu.HBM) for p in parts], *land)
    return out[0], out[1], list(out[2:2 + n]), list(out[2 + n:2 + 2 * n]), out[2 + 2 * n]


def _scatter_wait(send_sems, recv_sems, parts, slots, axes, after, *, name):
    n = len(parts)
    geo = _scatter_geo(parts, axes)

    def body(*refs):
        p_refs, q_refs = refs[:n], refs[n:2 * n]
        for cp in _scatter_copies(p_refs, q_refs, geo, refs[2 * n], refs[2 * n + 1]):
            cp.wait_send()
            cp.wait_recv()

    out = pl.pallas_call(
        body, name=name,
        out_shape=[pltpu.HBM(a.shape, a.dtype) for a in (*parts, *slots)],
        in_specs=[HBM_SPEC] * (2 * n) + [SEM_SPEC, SEM_SPEC, ANY],
        out_specs=[HBM_SPEC] * (2 * n),
        input_output_aliases={i: i for i in range(2 * n)},
        compiler_params=SPLIT_COPY,
    )(*parts, *slots, send_sems, recv_sems, after)
    return list(out[:n]), list(out[n:])


def _sibling_swap(grads, *, name):
    n = len(grads)
    out_shape = [jax.ShapeDtypeStruct((g.shape[0], g.shape[1] // 2, g.shape[2]), g.dtype)
                 for g in grads]

    def body(*refs):
        g_refs, o_refs = refs[:n], refs[n:2 * n]
        send_sems, recv_sems = refs[2 * n:]
        x, y, c, _ = _place()
        cps = []
        for i in range(n):
            rh = grads[i].shape[1] // 2
            src = g_refs[i].at[:, pl.ds(pl.multiple_of((1 - c) * rh, 16), rh), :]
            cp = pltpu.make_async_remote_copy(
                src_ref=src, dst_ref=o_refs[i], send_sem=send_sems.at[i], recv_sem=recv_sems.at[i],
                device_id=(x, y, 1 - c), device_id_type=MESH)
            cp.start()
            cps.append(cp)
        for cp in cps:
            cp.wait()

    return pl.pallas_call(
        body, in_specs=[ANY] * n, out_specs=[ANY] * n, out_shape=out_shape,
        scratch_shapes=[pltpu.SemaphoreType.DMA((n,)), pltpu.SemaphoreType.DMA((n,))],
        name=name)(*grads)


def _swap_copies(g_refs, r_refs, shapes, send_sems, recv_sems):
    x, y, c, _ = _place()
    cps = []
    for i, shape in enumerate(shapes):
        rh = shape[1] // 2
        src = g_refs[i].at[:, pl.ds(pl.multiple_of((1 - c) * rh, 16), rh), :]
        cps.append(pltpu.make_async_remote_copy(
            src_ref=src, dst_ref=r_refs[i], send_sem=send_sems.at[i], recv_sem=recv_sems.at[i],
            device_id=(x, y, 1 - c), device_id_type=MESH))
    return cps


def _sibling_swap_start(grads, *, name):
    n = len(grads)
    shapes = [g.shape for g in grads]
    lands = [pltpu.HBM((s[0], s[1] // 2, s[2]), g.dtype) for s, g in zip(shapes, grads)]

    def body(*refs):
        g_refs, r_refs = refs[:n], refs[n:2 * n]
        token = refs[4 * n + 2]
        for cp in _swap_copies(g_refs, r_refs, shapes, refs[2 * n], refs[2 * n + 1]):
            cp.start()
        token[...] = jnp.zeros_like(token)

    land = [pltpu.with_memory_space_constraint(lax.empty(s.inner_aval.shape, s.inner_aval.dtype), pltpu.HBM)
            for s in lands]
    out = pl.pallas_call(
        body, name=name,
        out_shape=(pltpu.SemaphoreType.DMA((n,)), pltpu.SemaphoreType.DMA((n,)),
                   *[pltpu.HBM(g.shape, g.dtype) for g in grads], *lands, TOKEN),
        in_specs=[HBM_SPEC] * (2 * n),
        out_specs=(SEM_SPEC, SEM_SPEC, *[HBM_SPEC] * (2 * n), pl.BlockSpec(memory_space=pltpu.VMEM)),
        input_output_aliases={i: 2 + i for i in range(2 * n)},
        compiler_params=SPLIT_COPY,
    )(*[pltpu.with_memory_space_constraint(g, pltpu.HBM) for g in grads], *land)
    return out[0], out[1], list(out[2:2 + n]), list(out[2 + n:2 + 2 * n]), out[2 + 2 * n]


def _sibling_swap_wait(send_sems, recv_sems, grads, lands, after, *, name):
    n = len(grads)
    shapes = [g.shape for g in grads]

    def body(*refs):
        g_refs, r_refs = refs[:n], refs[n:2 * n]
        for cp in _swap_copies(g_refs, r_refs, shapes, refs[2 * n], refs[2 * n + 1]):
            cp.wait_send()
            cp.wait_recv()

    out = pl.pallas_call(
        body, name=name,
        out_shape=[pltpu.HBM(a.shape, a.dtype) for a in (*grads, *lands)],
        in_specs=[HBM_SPEC] * (2 * n) + [SEM_SPEC, SEM_SPEC, ANY],
        out_specs=[HBM_SPEC] * (2 * n),
        input_output_aliases={i: i for i in range(2 * n)},
        compiler_params=SPLIT_COPY,
    )(*grads, *lands, send_sems, recv_sems, after)
    return list(out[:n]), list(out[n:])


def _pair_add(g3s, rxs, place, *, out_dtype, name):
    n = len(g3s)
    steps, trs = _group_tiles([g.shape[1] // 2 for g in g3s], 16)

    def body(p_ref, *refs):
        for i in range(n):
            refs[2 * n + i][...] = (refs[2 * i][...] + refs[2 * i + 1][...]).astype(out_dtype)

    in_specs, out_specs = [], []
    for g, tr in zip(g3s, trs):
        blk = (g.shape[0], tr, g.shape[2])
        in_specs += [pl.BlockSpec(blk, lambda i, p_ref: (0, p_ref[1] * steps + i, 0)),
                     pl.BlockSpec(blk, lambda i, p_ref: (0, i, 0))]
        out_specs.append(pl.BlockSpec(blk, lambda i, p_ref: (0, i, 0)))
    return pl.pallas_call(
        body,
        grid_spec=pltpu.PrefetchScalarGridSpec(
            num_scalar_prefetch=1, grid=(steps,), in_specs=in_specs, out_specs=out_specs),
        out_shape=[jax.ShapeDtypeStruct((g.shape[0], g.shape[1] // 2, g.shape[2]), out_dtype)
                   for g in g3s],
        compiler_params=_cp("parallel"), name=name)(place, *[a for q in zip(g3s, rxs) for a in q])


def _sum_slots(q, *, name):
    ns, rows, cols = q.shape
    tr = next(t for t in (128, 64, 32, 16, 8) if rows % t == 0)

    def body(q_ref, o_ref):
        acc = q_ref[0].astype(F32)
        for k in range(1, ns):
            acc = acc + q_ref[k].astype(F32)
        o_ref[...] = acc

    return pl.pallas_call(
        body, grid=(rows // tr,),
        in_specs=[pl.BlockSpec((ns, tr, cols), lambda i: (0, i, 0))],
        out_specs=_rspec(tr, cols),
        out_shape=jax.ShapeDtypeStruct((rows, cols), F32),
        compiler_params=_cp("parallel"), name=name)(q)


def _sum_chips(qs, ps, place, axes, *, name):
    n = len(qs)
    per = N_CHIPS + 1
    steps, trs = _group_tiles([q.shape[1] for q in qs], 16)

    def body(p_ref, *refs):
        chip = p_ref[0]
        for i in range(n):
            q_refs, own_ref = refs[per * i:per * i + N_CHIPS], refs[per * i + N_CHIPS]
            acc = jnp.where(chip == 0, own_ref[...], q_refs[0][...]).astype(F32)
            for k in range(1, N_CHIPS):
                acc = acc + jnp.where(chip == k, own_ref[...], q_refs[k][...]).astype(F32)
            refs[per * n + i][...] = acc

    def slot_spec(k, tr, cw):
        return pl.BlockSpec((None, tr, cw),
                            lambda i, p_ref: (jnp.where(p_ref[0] == k, (k + 1) % N_CHIPS, k), i, 0))

    in_specs, out_specs, operands = [], [], []
    for q, p, ax, tr in zip(qs, ps, axes, trs):
        cw = q.shape[2]
        in_specs += [slot_spec(k, tr, cw) for k in range(N_CHIPS)]
        in_specs.append(pl.BlockSpec((None, tr, cw), (lambda i, p_ref: (0, i, p_ref[0])) if ax == 1
                                     else (lambda i, p_ref: (p_ref[0], i, 0))))
        out_specs.append(pl.BlockSpec((tr, cw), lambda i, p_ref: (p_ref[1] * steps + i, 0)))
        operands += [q] * N_CHIPS + [p]
    return pl.pallas_call(
        body,
        grid_spec=pltpu.PrefetchScalarGridSpec(
            num_scalar_prefetch=1, grid=(steps,), in_specs=in_specs, out_specs=out_specs),
        out_shape=[jax.ShapeDtypeStruct((2 * q.shape[1], q.shape[2]), F32) for q in qs],
        compiler_params=_cp("parallel"), name=name)(place, *operands)


def _sibling_share(shards, *, name):
    n = len(shards)

    def body(*refs):
        o_refs = refs[n:2 * n]
        send_sems, recv_sems = refs[2 * n:]
        x, y, c, _ = _place()
        cps = []
        for i in range(n):
            rh = shards[i].shape[0] // 2
            mine = o_refs[i].at[pl.ds(pl.multiple_of(c * rh, 8), rh), :]
            cp = pltpu.make_async_remote_copy(
                src_ref=mine, dst_ref=mine, send_sem=send_sems.at[i], recv_sem=recv_sems.at[i],
                device_id=(x, y, 1 - c), device_id_type=MESH)
            cp.start()
            cps.append(cp)
        for i in range(n):
            rh = shards[i].shape[0] // 2
            theirs = o_refs[i].at[pl.ds(pl.multiple_of((1 - c) * rh, 8), rh), :]
            pltpu.make_async_remote_copy(
                src_ref=theirs, dst_ref=theirs, send_sem=send_sems.at[i], recv_sem=recv_sems.at[i],
                device_id=(x, y, c), device_id_type=MESH).wait_recv()
        for cp in cps:
            cp.wait_send()

    return pl.pallas_call(
        body, in_specs=[ANY] * n, out_specs=[ANY] * n,
        out_shape=[jax.ShapeDtypeStruct(h.shape, h.dtype) for h in shards],
        input_output_aliases={i: i for i in range(n)},
        scratch_shapes=[pltpu.SemaphoreType.DMA((n,)), pltpu.SemaphoreType.DMA((n,))],
        name=name)(*shards)


def _gather_all(blk, *, name, after=None, shares=()):
    rows, cols = blk.shape
    extra = [] if after is None else [after]
    n_s = len(shares)

    def body(x_ref, *refs):
        s_refs = refs[len(extra) + n_s + 1:len(extra) + 2 * n_s + 1]
        out_ref = refs[len(extra) + n_s]
        send_sems, recv_sems, local_sem, s_send, s_recv = refs[len(extra) + 2 * n_s + 1:]
        x, y, c = lax.axis_index("x"), lax.axis_index("y"), lax.axis_index("c")
        me = 4 * x + 2 * y + c
        mine = pltpu.make_async_copy(x_ref, out_ref.at[me], local_sem)
        mine.start()
        cps = []
        for i in range(n_s):
            rh = shares[i].shape[0] // 2
            half = s_refs[i].at[pl.ds(pl.multiple_of(c * rh, 8), rh), :]
            cp = pltpu.make_async_remote_copy(
                src_ref=half, dst_ref=half, send_sem=s_send.at[i], recv_sem=s_recv.at[i],
                device_id=(x, y, 1 - c), device_id_type=MESH)
            cp.start()
            cps.append(cp)
        for k in range(1, N_DEV):
            tx = (1 - x) if (k >> 2) & 1 else x
            ty = (1 - y) if (k >> 1) & 1 else y
            tc = (1 - c) if k & 1 else c
            cp = pltpu.make_async_remote_copy(
                src_ref=x_ref, dst_ref=out_ref.at[me], send_sem=send_sems.at[k - 1],
                recv_sem=recv_sems.at[k - 1], device_id=(tx, ty, tc), device_id_type=MESH)
            cp.start()
            cps.append(cp)
        for k in range(1, N_DEV):
            tx = (1 - x) if (k >> 2) & 1 else x
            ty = (1 - y) if (k >> 1) & 1 else y
            tc = (1 - c) if k & 1 else c
            got = out_ref.at[4 * tx + 2 * ty + tc]
            pltpu.make_async_remote_copy(
                src_ref=got, dst_ref=got, send_sem=send_sems.at[k - 1], recv_sem=recv_sems.at[k - 1],
                device_id=(x, y, c), device_id_type=MESH).wait_recv()
        for i in range(n_s):
            rh = shares[i].shape[0] // 2
            theirs = s_refs[i].at[pl.ds(pl.multiple_of((1 - c) * rh, 8), rh), :]
            pltpu.make_async_remote_copy(
                src_ref=theirs, dst_ref=theirs, send_sem=s_send.at[i], recv_sem=s_recv.at[i],
                device_id=(x, y, c), device_id_type=MESH).wait_recv()
        for cp in cps:
            cp.wait_send()
        mine.wait()

    vm = pl.BlockSpec(memory_space=pltpu.VMEM)
    out = pl.pallas_call(
        body, in_specs=[vm] + [ANY] * (len(extra) + n_s), out_specs=[vm] + [ANY] * n_s,
        out_shape=[jax.ShapeDtypeStruct((N_DEV, rows, cols), blk.dtype)]
        + [jax.ShapeDtypeStruct(s.shape, s.dtype) for s in shares],
        input_output_aliases={1 + len(extra) + i: 1 + i for i in range(n_s)},
        scratch_shapes=[pltpu.SemaphoreType.DMA((N_DEV - 1,)), pltpu.SemaphoreType.DMA((N_DEV - 1,)),
                        pltpu.SemaphoreType.DMA, pltpu.SemaphoreType.DMA((max(n_s, 1),)),
                        pltpu.SemaphoreType.DMA((max(n_s, 1),))],
        name=name)(blk, *extra, *shares)
    return (out[0], *out[1:]) if n_s else out[0]


def _as_rows(a):
    flat = a.reshape(-1)
    n = flat.shape[0]
    rows = -(-n // (8 * LANES)) * 8
    return jnp.pad(flat, (0, rows * LANES - n)).reshape(rows, LANES)


def _from_rows(p, shape):
    n = int(np.prod(shape))
    return p.reshape(-1)[:n].reshape(shape)


WEIGHT_AXES = (1, 1, 1, 0, 1, 0)
WIRE = BF16


def kernel(x, meta_tokens, w_in, w_na_out, w_hg_out, w_o, w_up, w_down, norm_mix, norm_mlp, norm_final, hg_norm, na_rpb, hg_lb_logits, loss_target, m_meta_tokens, m_w_in, m_w_na_out, m_w_hg_out, m_w_o, m_w_up, m_w_down, m_norm_mix, m_norm_mlp, m_norm_final, m_hg_norm, m_na_rpb, m_hg_lb_logits, v_meta_tokens, v_w_in, v_w_na_out, v_w_hg_out, v_w_o, v_w_up, v_w_down, v_norm_mix, v_norm_mlp, v_norm_final, v_hg_norm, v_na_rpb, v_hg_lb_logits):
    xi, yi, ci = lax.axis_index("x"), lax.axis_index("y"), lax.axis_index("c")
    chip = 2 * xi + yi
    d = x.shape[-1]
    dshard = meta_tokens.shape[1]
    hgw = hg_norm.shape[1]
    lbs = hg_lb_logits.shape[2]
    big = [w_in[0], w_na_out[0], w_hg_out[0], w_o[0], w_up[0], w_down[0]]
    big_m = [m_w_in[0], m_w_na_out[0], m_w_hg_out[0], m_w_o[0], m_w_up[0], m_w_down[0]]
    big_v = [v_w_in[0], v_w_na_out[0], v_w_hg_out[0], v_w_o[0], v_w_up[0], v_w_down[0]]

    place = jnp.stack([chip, ci]).astype(jnp.int32)
    own_w = _cast_into_full(big, WEIGHT_AXES, place, name="cast_shards")
    in_axes, rest_axes = WEIGHT_AXES[:1], WEIGHT_AXES[1:]
    small_in = jnp.concatenate([_as_rows(meta_tokens), _as_rows(hg_lb_logits)], axis=0)
    sm_send, sm_recv, sm_blk, sm_land, sm_token = _chip_exchange_start(small_in,
                                                                       name="small_params_start")
    in_send, in_recv, in_bufs, in_token = _allgather_start(own_w[:1], in_axes, sm_token,
                                                           name="weight_allgather_in_start")
    ag_send, ag_recv, ag_bufs, ag_token = _allgather_start(own_w[1:], rest_axes, in_token,
                                                           name="weight_allgather_rest_start")
    sm_land = _chip_exchange_wait(sm_send, sm_recv, sm_blk, sm_land, ag_token,
                                  name="small_params_wait")
    small_all = lax.dynamic_update_slice(sm_land, small_in[None], (chip, 0, 0))
    forward = {}

    def first_weight(after):
        got = _allgather_wait(in_send, in_recv, in_bufs, in_axes, after,
                              name="weight_allgather_in_wait")
        return _allgather_forward(got, in_axes, name="weight_allgather_in_forward")[0]

    def rest_landed(after):
        got = _allgather_wait(ag_send, ag_recv, ag_bufs, rest_axes, after,
                              name="weight_allgather_rest_wait")
        send, recv, bufs, token = _allgather_forward_start(
            got, rest_axes, name="weight_allgather_rest_forward_start")
        forward["rest"] = (send, recv, bufs)
        return token

    def rest_weights(after):
        return _allgather_forward_wait(*forward["rest"], rest_axes, after,
                                       name="weight_allgather_rest_forward_wait")

    n_meta_rows = N_META * dshard // LANES
    meta_full = (small_all[:, :n_meta_rows].reshape(N_CHIPS, N_META, dshard)
                 .transpose(1, 0, 2).reshape(N_META, d))
    lbl_full = (small_all[:, n_meta_rows:].reshape(N_CHIPS, -1)[:, :4 * lbs]
                .reshape(N_CHIPS, 2, 2, lbs).transpose(1, 2, 0, 3).reshape(2, 2, N_CHIPS * lbs))
    lb = jax.nn.softmax(lbl_full, axis=1)[:, 0]

    def by_chip(dws, axes):
        return [g.reshape(1, *g.shape) if ax == 1
                else g.reshape(N_CHIPS, g.shape[0] // N_CHIPS, g.shape[1]) for g, ax in zip(dws, axes)]

    flying = {}

    def scatter(tag, axes, g3, rx):
        parts = _pair_add(g3, rx, place, out_dtype=WIRE, name=f"grad_pair_add_{tag}")
        send, recv, parts, slots, token = _scatter_start(parts, axes,
                                                         name=f"grad_scatter_{tag}_start")
        flying[tag] = (send, recv, parts, slots)
        return token

    def swap(tag, axes):
        def start(dws):
            send, recv, g3, lands, token = _sibling_swap_start(
                by_chip(dws, axes), name=f"grad_sibling_swap_{tag}_start")
            flying["swap_" + tag] = (send, recv, g3, lands)
            return token

        def finish(after):
            g3, rx = _sibling_swap_wait(*flying["swap_" + tag], after,
                                        name=f"grad_sibling_swap_{tag}_wait")
            return scatter(tag, axes, g3, rx)
        return start, finish

    swap_rest, scatter_rest = swap("rest", rest_axes)
    swap_in, scatter_in = swap("in", in_axes)

    def landed(tag, axes, after):
        return _scatter_wait(*flying[tag], axes, after, name=f"grad_scatter_{tag}_wait")

    (loss, dx, dmeta, *_, dg_mix, dg_mlp, dg_fin, d_gain, d_rpb, d_lb) = _local_step(
        x[0], loss_target[0], meta_full, first_weight, rest_weights, norm_mix, norm_mlp,
        norm_final.reshape(1, d), hg_norm, na_rpb[0], lb, swap_rest, scatter_rest,
        lambda dw_in: swap_in([dw_in]), rest_landed, scatter_in)

    parts_rest, slots_rest = landed("rest", rest_axes, dx)
    g_rest = _sibling_share(_sum_chips(slots_rest, parts_rest, place, rest_axes,
                                       name="grad_sum_chips_rest"),
                            name="grad_sibling_share_rest")
    out_rest = _adamw(big[1:], g_rest, big_m[1:], big_v[1:], name="adamw_rest")
    parts_in, slots_in = landed("in", in_axes, out_rest[-1][0])
    half_in = _sum_chips(slots_in, parts_in, place, in_axes, name="grad_sum_chips_in")

    d_rpb_c = d_rpb[:, :2 * NA_WIN_W - 1]
    small_g = [dmeta, dg_mix, dg_mlp, dg_fin, d_gain, d_rpb_c, d_lb, loss]
    packed = jnp.concatenate([_as_rows(a) for a in small_g], axis=0)
    gathered, g_in = _gather_all(packed, shares=half_in, name="gather_small_grads")
    g_big = [g_in] + list(g_rest)
    total = _sum_slots(gathered, name="sum_small_grads")
    offs = np.cumsum([0] + [_as_rows(a).shape[0] for a in small_g])
    take = lambda i, shape: _from_rows(total[offs[i]:offs[i + 1]], shape)
    g_meta_full = take(0, (N_META, d))
    g_norm_mix, g_norm_mlp = take(1, (1, d)), take(2, (1, d))
    g_norm_final = take(3, (d,))
    g_hg_norm = take(4, (1, hgw))
    g_rpb = take(5, na_rpb.shape)
    g_lb = take(6, (2, hgw))
    loss_total = take(7, (1, LANES))[0, 0]
    g_meta = lax.dynamic_slice_in_dim(g_meta_full, chip * dshard, dshard, axis=1)
    dl0 = lb * (1.0 - lb) * g_lb
    g_lbl_full = jnp.stack([dl0, -dl0], axis=1)
    g_lbl = lax.dynamic_slice_in_dim(g_lbl_full, chip * lbs, lbs, axis=2)

    big_out = _adamw(big[:1], [g_in], big_m[:1], big_v[:1], name="adamw_in") + out_rest
    small_w = [meta_tokens, norm_mix, norm_mlp, norm_final, hg_norm, na_rpb, hg_lb_logits]
    small_gr = [g_meta, g_norm_mix, g_norm_mlp, g_norm_final, g_hg_norm, g_rpb, g_lbl]
    small_m = [m_meta_tokens, m_norm_mix, m_norm_mlp, m_norm_final, m_hg_norm, m_na_rpb, m_hg_lb_logits]
    small_v = [v_meta_tokens, v_norm_mix, v_norm_mlp, v_norm_final, v_hg_norm, v_na_rpb, v_hg_lb_logits]
    pk = lambda lst: jnp.concatenate([_as_rows(a) for a in lst], axis=0)
    ((sd, sm, sv),) = _adamw([pk(small_w)], [pk(small_gr)], [pk(small_m)], [pk(small_v)],
                             name="adamw_small")
    soffs = np.cumsum([0] + [_as_rows(a).shape[0] for a in small_w])
    unpk = lambda p: [_from_rows(p[soffs[i]:soffs[i + 1]], small_w[i].shape) for i in range(len(small_w))]
    sd, sm, sv = unpk(sd), unpk(sm), unpk(sv)

    def order(bigs, smalls):
        return [smalls[0]] + [b.reshape(1, *b.shape) for b in bigs] + smalls[1:]

    grads = order(g_big, small_gr)
    deltas = order([o[0] for o in big_out], sd)
    new_m = order([o[1] for o in big_out], sm)
    new_v = order([o[2] for o in big_out], sv)
    return (loss_total, dx.reshape(1, *dx.shape), *grads, *deltas, *new_m, *new_v)
```

```python
import functools

import numpy as np
import jax
import jax.numpy as jnp
from jax import lax
from jax.experimental import pallas as pl
from jax.experimental.pallas import tpu as pltpu

F32 = jnp.float32
BF16 = jnp.bfloat16
HIGHEST = lax.Precision.HIGHEST

GRID_W = 64
N_META = 16
EPS = 1e-6
NA_HEAD_DIM = 64
NA_WIN_H = 8
NA_WIN_W = 16
HG_DK = 128
HG_CHUNK = 16
LANES = 128
ROW_ALIGN = 128
VMEM_LIMIT = 48 * 1024 * 1024

ADAM_LR = 0.001
ADAM_B1 = 0.9
ADAM_B2 = 0.999
ADAM_EPS = 1e-08
ADAM_WD = 0.01
ADAM_STEP = 10

MESH = pl.DeviceIdType.MESH


def _cp(*sem):
    return pltpu.CompilerParams(dimension_semantics=sem, vmem_limit_bytes=VMEM_LIMIT)


def _sigmoid(x):
    return 0.5 * jnp.tanh(0.5 * x) + 0.5


def _dot(a, b, dims, precision=None):
    return lax.dot_general(a, b, (dims, ((), ())), preferred_element_type=F32, precision=precision)


def _nn(a, b, **kw):
    return _dot(a, b, ((1,), (0,)), **kw)


def _nt(a, b, **kw):
    return _dot(a, b, ((1,), (1,)), **kw)


def _tn(a, b, **kw):
    return _dot(a, b, ((0,), (0,)), **kw)


def _matmul(a, b, *, ta=False, tb=False, tm=None, tn=None, tk=None, out_dtype=F32, name,
            precision=None, after=None, epilogue=None, tiles=(), out_dtypes=None):
    extra = [] if after is None else [after]
    single = out_dtypes is None
    if single:
        out_dtypes = (out_dtype,)
    n_t, n_o = len(tiles), len(out_dtypes)
    if ta:
        kdim, m = a.shape
    else:
        m, kdim = a.shape
    if tb:
        n, k2 = b.shape
    else:
        k2, n = b.shape
    assert kdim == k2, (a.shape, b.shape, ta, tb)
    if tm is None:
        if ta:
            tm = next(t for t in (1024, 512, 256, 128, m) if m % t == 0)
        else:
            tm = m // 2 if (m // 2) % 16 == 0 and m > 512 else m
    if tn is None:
        wide = (1024,) if not ta and len(tiles) <= 1 else ()
        tn = next(t for t in (*wide, 512, 256, 128, n) if n % t == 0)
    if tk is None:
        tk = kdim if ta else next(t for t in (2048, 1024, 512, 256, 128, kdim) if kdim % t == 0)
    assert m % tm == 0 and n % tn == 0 and kdim % tk == 0, (m, n, kdim, tm, tn, tk)
    nk = kdim // tk
    op_dtype = F32 if precision is not None else BF16

    def body(a_ref, b_ref, *refs):
        t_refs = refs[:n_t]
        o_refs = refs[n_t + len(extra):n_t + len(extra) + n_o]
        av = a_ref[...].astype(op_dtype)
        bv = b_ref[...].astype(op_dtype)
        dims = ((0 if ta else 1,), (1 if tb else 0,))
        part = _dot(av, bv, dims, precision=precision)

        def finish(acc):
            outs = (acc,) if epilogue is None else epilogue(acc, *[t[...] for t in t_refs])
            for o_ref, val in zip(o_refs, outs):
                o_ref[...] = val.astype(o_ref.dtype)

        if nk == 1:
            finish(part)
            return
        acc_ref = refs[-1]
        kk = pl.program_id(2)

        @pl.when(kk == 0)
        def _():
            acc_ref[...] = part

        @pl.when((kk > 0) & (kk < nk - 1))
        def _():
            acc_ref[...] += part

        @pl.when(kk == nk - 1)
        def _():
            finish(acc_ref[...] + part)

    a_spec = (pl.BlockSpec((tk, tm), lambda i, j, k: (k, i)) if ta
              else pl.BlockSpec((tm, tk), lambda i, j, k: (i, k)))
    b_spec = (pl.BlockSpec((tn, tk), lambda i, j, k: (j, k)) if tb
              else pl.BlockSpec((tk, tn), lambda i, j, k: (k, j)))
    for _, off in tiles:
        assert off % tn == 0, (off, tn)
    t_specs = [pl.BlockSpec((tm, tn), functools.partial(lambda i, j, k, o: (i, o + j), o=off // tn))
               for _, off in tiles]
    o_spec = pl.BlockSpec((tm, tn), lambda i, j, k: (i, j))
    outs = pl.pallas_call(
        body,
        grid=(m // tm, n // tn, nk),
        in_specs=[a_spec, b_spec] + t_specs + [pl.BlockSpec(memory_space=pl.ANY)] * len(extra),
        out_specs=[o_spec] * n_o,
        out_shape=[jax.ShapeDtypeStruct((m, n), dt) for dt in out_dtypes],
        scratch_shapes=[pltpu.VMEM((tm, tn), F32)] if nk > 1 else [],
        compiler_params=_cp("parallel", "parallel", "arbitrary"),
        name=name,
    )(a, b, *[t for t, _ in tiles], *extra)
    return outs[0] if single else outs


def _rspec(tr, w, cb=0):
    return pl.BlockSpec((tr, w), lambda i: (i, cb))


def _fspec(shape):
    nd = len(shape)
    return pl.BlockSpec(shape, lambda i: (0,) * nd)


ROW_VMEM_BUDGET = 20 * 1024 * 1024
ROW_MIN_STEPS = 4


def _row_tile(lp, row_bytes):
    for k in range(ROW_MIN_STEPS, lp // 16 + 1):
        tr = lp // k
        if lp % k == 0 and tr % 16 == 0 and 2 * tr * row_bytes <= ROW_VMEM_BUDGET:
            return tr
    return lp


def _token_rows_copy(i, n_tiles, tr, n_tok, tok_ref, buf_ref, sem, *, to_tokens, start=True,
                     wait=True):
    assert n_tiles >= 2 and 0 < n_tok + N_META - (n_tiles - 1) * tr <= tr

    def run(tok_row, buf_row, count):
        tok = tok_ref.at[pl.ds(tok_row, count), :]
        buf = buf_ref.at[pl.ds(buf_row, count), :]
        cp = pltpu.make_async_copy(buf, tok, sem) if to_tokens else pltpu.make_async_copy(tok, buf, sem)
        if start:
            cp.start()
        if wait:
            cp.wait()

    @pl.when(i == 0)
    def _():
        run(0, N_META, tr - N_META)

    if n_tiles > 2:
        @pl.when((i > 0) & (i < n_tiles - 1))
        def _():
            run(pl.multiple_of(i * tr - N_META, 8), 0, tr)

    @pl.when(i == n_tiles - 1)
    def _():
        run((n_tiles - 1) * tr - N_META, 0, n_tok + N_META - (n_tiles - 1) * tr)


def _embed_norm(x, meta, g, *, lp, name):
    n_tok, d = x.shape
    tr = _row_tile(lp, d * (4 + 2))
    n_tiles = lp // tr

    def body(x_ref, meta_ref, g_ref, h_ref, o_ref, buf_ref, sem):
        i = pl.program_id(0)
        buf_ref[...] = jnp.zeros_like(buf_ref)

        @pl.when(i == 0)
        def _():
            buf_ref[0:N_META, :] = meta_ref[...]

        _token_rows_copy(i, n_tiles, tr, n_tok, x_ref, buf_ref, sem, to_tokens=False)
        xv = buf_ref[...]
        h_ref[...] = xv
        r = lax.rsqrt(jnp.mean(xv * xv, axis=-1, keepdims=True) + EPS)
        o_ref[...] = (xv * r * g_ref[...]).astype(BF16)

    return pl.pallas_call(
        body, grid=(n_tiles,),
        in_specs=[ANY, _fspec((N_META, d)), _fspec((1, d))],
        out_specs=[_rspec(tr, d), _rspec(tr, d)],
        out_shape=[jax.ShapeDtypeStruct((lp, d), F32), jax.ShapeDtypeStruct((lp, d), BF16)],
        scratch_shapes=[pltpu.VMEM((tr, d), F32), pltpu.SemaphoreType.DMA],
        compiler_params=_cp("parallel"), name=name)(x, meta, g)


def _residual_norm(h, t, g, *, name):
    lp, d = h.shape
    tr = _row_tile(lp, d * (4 + 4 + 4 + 2))

    def body(h_ref, t_ref, g_ref, h1_ref, m_ref):
        xv = h_ref[...] + t_ref[...]
        h1_ref[...] = xv
        r = lax.rsqrt(jnp.mean(xv * xv, axis=-1, keepdims=True) + EPS)
        m_ref[...] = (xv * r * g_ref[...]).astype(BF16)

    return pl.pallas_call(
        body, grid=(lp // tr,),
        in_specs=[_rspec(tr, d), _rspec(tr, d), _fspec((1, d))],
        out_specs=[_rspec(tr, d), _rspec(tr, d)],
        out_shape=[jax.ShapeDtypeStruct((lp, d), F32), jax.ShapeDtypeStruct((lp, d), BF16)],
        compiler_params=_cp("parallel"), name=name)(h, t, g)


def _rmsnorm_bwd_add(x, g, dy, dres, *, name):
    lp, d = x.shape
    tr = _row_tile(lp, d * (4 * 4 + 2))

    def body(x_ref, g_ref, dy_ref, dr_ref, dx_ref, dx16_ref, dg_ref):
        @pl.when(pl.program_id(0) == 0)
        def _():
            dg_ref[...] = jnp.zeros_like(dg_ref)

        xv = x_ref[...]
        r = lax.rsqrt(jnp.mean(xv * xv, axis=-1, keepdims=True) + EPS)
        xh = xv * r
        dyv = dy_ref[...]
        dg_ref[...] += jnp.sum(dyv * xh, axis=0, keepdims=True)
        dxh = dyv * g_ref[...]
        dx = dr_ref[...] + r * (dxh - xh * jnp.mean(dxh * xh, axis=-1, keepdims=True))
        dx_ref[...] = dx
        dx16_ref[...] = dx.astype(BF16)

    return pl.pallas_call(
        body, grid=(lp // tr,),
        in_specs=[_rspec(tr, d), _fspec((1, d)), _rspec(tr, d), _rspec(tr, d)],
        out_specs=[_rspec(tr, d), _rspec(tr, d), _fspec((1, d))],
        out_shape=[jax.ShapeDtypeStruct((lp, d), F32), jax.ShapeDtypeStruct((lp, d), BF16),
                   jax.ShapeDtypeStruct((1, d), F32)],
        compiler_params=_cp("arbitrary"), name=name)(x, g, dy, dres)


def _rmsnorm_bwd_tokens(x, g, dy, dres, *, n_tok, name):
    lp, d = x.shape
    tr = _row_tile(lp, d * 4 * 4)
    n_tiles = lp // tr

    def body(x_ref, g_ref, dy_ref, dr_ref, dtok_ref, dmeta_ref, dg_ref, buf_ref, sem):
        i = pl.program_id(0)

        @pl.when(i == 0)
        def _():
            dg_ref[...] = jnp.zeros_like(dg_ref)

        xv = x_ref[...]
        r = lax.rsqrt(jnp.mean(xv * xv, axis=-1, keepdims=True) + EPS)
        xh = xv * r
        dyv = dy_ref[...]
        dg_ref[...] += jnp.sum(dyv * xh, axis=0, keepdims=True)
        dxh = dyv * g_ref[...]
        buf_ref[...] = dr_ref[...] + r * (dxh - xh * jnp.mean(dxh * xh, axis=-1, keepdims=True))

        @pl.when(i == 0)
        def _():
            dmeta_ref[...] = buf_ref[0:N_META, :]

        _token_rows_copy(i, n_tiles, tr, n_tok, dtok_ref, buf_ref, sem, to_tokens=True)

    return pl.pallas_call(
        body, grid=(n_tiles,),
        in_specs=[_rspec(tr, d), _fspec((1, d)), _rspec(tr, d), _rspec(tr, d)],
        out_specs=[ANY, _fspec((N_META, d)), _fspec((1, d))],
        out_shape=[jax.ShapeDtypeStruct((n_tok, d), F32), jax.ShapeDtypeStruct((N_META, d), F32),
                   jax.ShapeDtypeStruct((1, d), F32)],
        scratch_shapes=[pltpu.VMEM((tr, d), F32), pltpu.SemaphoreType.DMA],
        compiler_params=_cp("arbitrary"), name=name)(x, g, dy, dres)


def _final_loss(h1, t2, g, tgt, *, name):
    lp, d = h1.shape
    n_tok = tgt.shape[0]
    tr = _row_tile(lp, d * 4 * 4)
    n_tiles = lp // tr

    def body(h_ref, t_ref, g_ref, tgt_ref, dh_ref, dh16_ref, loss_ref, dg_ref, tg_ref, sem):
        i = pl.program_id(0)

        @pl.when(i == 0)
        def _():
            loss_ref[...] = jnp.zeros_like(loss_ref)
            dg_ref[...] = jnp.zeros_like(dg_ref)
            tg_ref[...] = jnp.zeros_like(tg_ref)

        _token_rows_copy(i, n_tiles, tr, n_tok, tgt_ref, tg_ref, sem, to_tokens=False, wait=False)
        xv = h_ref[...] + t_ref[...]
        r = lax.rsqrt(jnp.mean(xv * xv, axis=-1, keepdims=True) + EPS)
        xh = xv * r
        gv = g_ref[...]
        _token_rows_copy(i, n_tiles, tr, n_tok, tgt_ref, tg_ref, sem, to_tokens=False, start=False)
        row = i * tr + lax.broadcasted_iota(jnp.int32, (tr, 1), 0)
        valid = (row >= N_META) & (row < N_META + n_tok)
        err = jnp.where(valid, xh * gv - tg_ref[...], 0.0)
        loss_ref[...] += jnp.sum(0.5 * err * err) / d
        dy = err / d
        dg_ref[...] += jnp.sum(dy * xh, axis=0, keepdims=True)
        dxh = dy * gv
        dh = r * (dxh - xh * jnp.mean(dxh * xh, axis=-1, keepdims=True))
        dh_ref[...] = dh
        dh16_ref[...] = dh.astype(BF16)

    return pl.pallas_call(
        body, grid=(n_tiles,),
        in_specs=[_rspec(tr, d), _rspec(tr, d), _fspec((1, d)), ANY],
        out_specs=[_rspec(tr, d), _rspec(tr, d), _fspec((1, LANES)), _fspec((1, d))],
        out_shape=[jax.ShapeDtypeStruct((lp, d), F32), jax.ShapeDtypeStruct((lp, d), BF16),
                   jax.ShapeDtypeStruct((1, LANES), F32), jax.ShapeDtypeStruct((1, d), F32)],
        scratch_shapes=[pltpu.VMEM((tr, d), F32), pltpu.SemaphoreType.DMA],
        compiler_params=_cp("arbitrary"), name=name)(h1, t2, g, tgt)


def _hg_out(o_f, o_b, proj, gain, *, col_g, name):
    lp, w = o_f.shape
    tr = _row_tile(lp, w * (3 * 4 + 2))
    hh = w // HG_DK

    def body(of_ref, ob_ref, g_ref, gain_ref, y_ref):
        gv = g_ref[...]
        sg = gv * _sigmoid(gv)
        for h in range(hh):
            sl = slice(h * HG_DK, (h + 1) * HG_DK)
            o = of_ref[:, sl] + ob_ref[:, sl]
            r = lax.rsqrt(jnp.mean(o * o, axis=-1, keepdims=True) + EPS)
            y_ref[:, sl] = (o * r * gain_ref[:, sl] * sg[:, sl]).astype(BF16)

    return pl.pallas_call(
        body, grid=(lp // tr,),
        in_specs=[_rspec(tr, w), _rspec(tr, w), _rspec(tr, w, col_g // w), _fspec((1, w))],
        out_specs=_rspec(tr, w),
        out_shape=jax.ShapeDtypeStruct((lp, w), BF16),
        compiler_params=_cp("parallel"), name=name)(o_f, o_b, proj, gain)


def _hg_out_bwd(o_f, o_b, proj, gain, dy, *, col_g, name):
    lp, w = o_f.shape
    tr = _row_tile(lp, w * (5 * 4 + 2))
    hh = w // HG_DK

    def body(of_ref, ob_ref, g_ref, gain_ref, dy_ref, do_ref, dg_ref, dgain_ref):
        @pl.when(pl.program_id(0) == 0)
        def _():
            dgain_ref[...] = jnp.zeros_like(dgain_ref)

        for h in range(hh):
            sl = slice(h * HG_DK, (h + 1) * HG_DK)
            gv = g_ref[:, sl]
            s = _sigmoid(gv)
            sg = gv * s
            dsg = s + gv * s * (1.0 - s)
            o = of_ref[:, sl] + ob_ref[:, sl]
            r = lax.rsqrt(jnp.mean(o * o, axis=-1, keepdims=True) + EPS)
            on = o * r
            dyv = dy_ref[:, sl]
            gn = gain_ref[:, sl]
            dgain_ref[:, sl] += jnp.sum(dyv * on * sg, axis=0, keepdims=True)
            dg_ref[:, sl] = (dyv * on * gn * dsg).astype(BF16)
            don = dyv * gn * sg
            do_ref[:, sl] = r * (don - on * jnp.mean(don * on, axis=-1, keepdims=True))

    return pl.pallas_call(
        body, grid=(lp // tr,),
        in_specs=[_rspec(tr, w), _rspec(tr, w), _rspec(tr, w, col_g // w), _fspec((1, w)),
                  _rspec(tr, w)],
        out_specs=[_rspec(tr, w), _rspec(tr, w), _fspec((1, w))],
        out_shape=[jax.ShapeDtypeStruct((lp, w), F32), jax.ShapeDtypeStruct((lp, w), BF16),
                   jax.ShapeDtypeStruct((1, w), F32)],
        compiler_params=_cp("arbitrary"), name=name)(o_f, o_b, proj, gain, dy)


HG_ROWS = 128
HG_HALVES = (1, 2, 4, 8, 16, 32, 64)


def _hg_gates(zq, z, lbv):
    sq = _sigmoid(zq)
    s = _sigmoid(z)
    f = lbv + (1.0 - lbv) * s
    kk = (1.0 - lbv) * (1.0 - s)
    return zq * sq, sq, s, f, jnp.log(f), kk


def _block_cumsum(g, pos, suffix):
    x = g
    for k in HG_HALVES:
        if suffix:
            x = x + jnp.where(pos < HG_ROWS - k, pltpu.roll(x, HG_ROWS - k, 0), 0.0)
        else:
            x = x + jnp.where(pos >= k, pltpu.roll(x, k, 0), 0.0)
    return x


def _pair_levels(b, pos, reverse):
    out = []
    first = b
    for m in HG_HALVES:
        if m > 1:
            first = jnp.where((pos & (m - 1)) >= m // 2, pltpu.roll(first, m // 2, 0), first)
        nxt = pltpu.roll(first, HG_ROWS - m, 0)
        upper = (pos & (2 * m - 1)) >= m
        if reverse:
            eq = jnp.where(upper, 0.0, jnp.exp(b - nxt))
            ek = jnp.where(upper, jnp.exp(first - b), 0.0)
        else:
            eq = jnp.where(upper, jnp.exp(b - first), 0.0)
            ek = jnp.where(upper, 0.0, jnp.exp(nxt - b))
        out.append((eq, ek))
    return out


def _pair_masks(mask_ref):
    ri = lax.broadcasted_iota(jnp.int32, (HG_ROWS, HG_ROWS), 0)
    ci = lax.broadcasted_iota(jnp.int32, (HG_ROWS, HG_ROWS), 1)
    for i, m in enumerate(HG_HALVES):
        sh = m.bit_length()
        mask_ref[i] = jnp.where((ri >> sh) == (ci >> sh), 1.0, 0.0)


def _hg_scan_fwd(proj, lb, *, reverse, col_q, col_z, col_i, hh, name):
    lp = proj.shape[0]
    n_blocks = lp // HG_ROWS
    last = 0 if reverse else HG_ROWS - 1

    def body(q_ref, z_ref, i_ref, lb_ref, o_ref, st_ref, mask_ref):
        lbv = lb_ref[...]
        pos = lax.broadcasted_iota(jnp.int32, (HG_ROWS, 1), 0)
        ri = lax.broadcasted_iota(jnp.int32, (HG_ROWS, HG_ROWS), 0)
        ci = lax.broadcasted_iota(jnp.int32, (HG_ROWS, HG_ROWS), 1)
        _pair_masks(mask_ref)

        def block(bi, st):
            bb = (n_blocks - 1 - bi) if reverse else bi
            r0 = pl.multiple_of(bb * HG_ROWS, HG_ROWS)
            v16 = i_ref[pl.ds(r0, HG_ROWS), :].astype(BF16)
            qh, _, _, _, g, kk = _hg_gates(q_ref[pl.ds(r0, HG_ROWS), :],
                                           z_ref[pl.ds(r0, HG_ROWS), :], lbv)
            b = _block_cumsum(g, pos, reverse)
            bl = b[last:last + 1, :]
            qe = (qh * jnp.exp(b)).astype(BF16)
            kd = (kk * jnp.exp(bl - b)).astype(BF16)
            a = jnp.where(ri == ci, jnp.sum(qh * kk, axis=1, keepdims=True), 0.0)
            for i, (eq, ek) in enumerate(_pair_levels(b, pos, reverse)):
                a = a + mask_ref[i] * _nt((qh * eq).astype(BF16), (kk * ek).astype(BF16))
            st_ref[bb] = st
            o_ref[pl.ds(r0, HG_ROWS), :] = _nn(a.astype(BF16), v16) + _nt(qe, st.astype(BF16))
            return jnp.exp(bl) * st + _tn(v16, kd)

        lax.fori_loop(0, n_blocks, block, jnp.zeros((HG_DK, HG_DK), F32))

    cspec = lambda col: pl.BlockSpec((lp, HG_DK), lambda h: (0, col // HG_DK + h))
    return pl.pallas_call(
        body, grid=(hh,),
        in_specs=[cspec(col_q), cspec(col_z), cspec(col_i),
                  pl.BlockSpec((None, 1, HG_DK), lambda h: (h, 0, 0))],
        out_specs=[pl.BlockSpec((lp, HG_DK), lambda h: (0, h)),
                   pl.BlockSpec((None, n_blocks, HG_DK, HG_DK), lambda h: (h, 0, 0, 0))],
        out_shape=[jax.ShapeDtypeStruct((lp, hh * HG_DK), F32),
                   jax.ShapeDtypeStruct((hh, n_blocks, HG_DK, HG_DK), F32)],
        scratch_shapes=[pltpu.VMEM((len(HG_HALVES), HG_ROWS, HG_ROWS), F32)],
        compiler_params=_cp("parallel"), name=name)(proj, proj, proj, lb)


def _hg_scan_bwd(proj, lb, states, do, *, reverse, col_q, col_z, col_i, hh, name):
    lp = proj.shape[0]
    n_blocks = lp // HG_ROWS
    last = 0 if reverse else HG_ROWS - 1

    def body(q_ref, z_ref, i_ref, lb_ref, st_ref, do_ref, dq_ref, dz_ref, dv_ref, dlb_ref, mask_ref):
        lbv = lb_ref[...]
        pos = lax.broadcasted_iota(jnp.int32, (HG_ROWS, 1), 0)
        ri = lax.broadcasted_iota(jnp.int32, (HG_ROWS, HG_ROWS), 0)
        ci = lax.broadcasted_iota(jnp.int32, (HG_ROWS, HG_ROWS), 1)
        _pair_masks(mask_ref)

        def block(bi, carry):
            dst, dlb = carry
            bb = bi if reverse else (n_blocks - 1 - bi)
            r0 = pl.multiple_of(bb * HG_ROWS, HG_ROWS)
            zq = q_ref[pl.ds(r0, HG_ROWS), :]
            v16 = i_ref[pl.ds(r0, HG_ROWS), :].astype(BF16)
            do16 = do_ref[pl.ds(r0, HG_ROWS), :].astype(BF16)
            qh, sq, s, f, g, kk = _hg_gates(zq, z_ref[pl.ds(r0, HG_ROWS), :], lbv)
            b = _block_cumsum(g, pos, reverse)
            bl = b[last:last + 1, :]
            eb = jnp.exp(b)
            ebl = jnp.exp(bl - b)
            decay = jnp.exp(bl)
            qe16 = (qh * eb).astype(BF16)
            kd16 = (kk * ebl).astype(BF16)
            st = st_ref[bb]
            st16, dst16 = st.astype(BF16), dst.astype(BF16)
            same_row = ri == ci
            da = _nt(do16, v16)
            da_diag = jnp.sum(jnp.where(same_row, da, 0.0), axis=1, keepdims=True)
            dq_state = eb * _nn(do16, st16)
            dk_state = ebl * _nn(v16, dst16)
            dq = dq_state + da_diag * kk
            dk = dk_state + da_diag * qh
            dbl = (decay * jnp.sum(st * dst, axis=0, keepdims=True)
                   + jnp.sum(kk * dk_state, axis=0, keepdims=True))
            db = qh * dq_state - kk * dk_state + jnp.where(pos == last, dbl, 0.0)
            a = jnp.where(same_row, jnp.sum(qh * kk, axis=1, keepdims=True), 0.0)
            for i, (eq, ek) in enumerate(_pair_levels(b, pos, reverse)):
                same = mask_ref[i]
                q16, k16 = (qh * eq).astype(BF16), (kk * ek).astype(BF16)
                a = a + same * _nt(q16, k16)
                da16 = (same * da).astype(BF16)
                gq, gk = _nn(da16, k16), _tn(da16, q16)
                dq = dq + eq * gq
                dk = dk + ek * gk
                db = db + (q16.astype(F32) * gq - k16.astype(F32) * gk)
            dg = _block_cumsum(db, pos, not reverse)
            df = dg / f - dk
            dq_ref[pl.ds(r0, HG_ROWS), :] = dq * (sq + zq * sq * (1.0 - sq))
            dz_ref[pl.ds(r0, HG_ROWS), :] = df * (1.0 - lbv) * s * (1.0 - s)
            dv_ref[pl.ds(r0, HG_ROWS), :] = _nt(kd16, dst16) + _tn(a.astype(BF16), do16)
            return (decay * dst + _tn(do16, qe16),
                    dlb + jnp.sum(df * (1.0 - s), axis=0, keepdims=True))

        _, dlb = lax.fori_loop(0, n_blocks, block,
                               (jnp.zeros((HG_DK, HG_DK), F32), jnp.zeros((1, HG_DK), F32)))
        dlb_ref[...] = dlb

    cspec = lambda col: pl.BlockSpec((lp, HG_DK), lambda h: (0, col // HG_DK + h))
    ospec = pl.BlockSpec((lp, HG_DK), lambda h: (0, h))
    sds = jax.ShapeDtypeStruct((lp, hh * HG_DK), F32)
    return pl.pallas_call(
        body, grid=(hh,),
        in_specs=[cspec(col_q), cspec(col_z), cspec(col_i),
                  pl.BlockSpec((None, 1, HG_DK), lambda h: (h, 0, 0)),
                  pl.BlockSpec((None, n_blocks, HG_DK, HG_DK), lambda h: (h, 0, 0, 0)),
                  ospec],
        out_specs=[ospec, ospec, ospec, pl.BlockSpec((None, 1, HG_DK), lambda h: (h, 0, 0))],
        out_shape=[sds, sds, sds, jax.ShapeDtypeStruct((hh, 1, HG_DK), F32)],
        scratch_shapes=[pltpu.VMEM((len(HG_HALVES), HG_ROWS, HG_ROWS), F32)],
        compiler_params=_cp("parallel"), name=name)(proj, proj, proj, lb, states, do)


NA_HB = LANES // NA_HEAD_DIM
NA_G = 4
NA_U = NA_G + NA_WIN_H
NA_QN = NA_G * GRID_W
NA_KN = NA_U * GRID_W


def _na_table_index(pattern, a, j):
    if pattern == 0:
        return j - a + NA_WIN_H - 1 if j < NA_WIN_H else None
    if pattern == 2:
        return j - a - 1 if j >= NA_U - NA_WIN_H else None
    return j - a + NA_WIN_H // 2 - 1 if a <= j < a + NA_WIN_H else None


def _na_step_rows(pattern, t, rows):
    if pattern == 0:
        r0, us = 0, 0
    elif pattern == 2:
        r0, us = rows - NA_G, rows - NA_U
    else:
        r0 = NA_G * t
        us = r0 - NA_WIN_H // 2
    q0, k0 = N_META + GRID_W * r0, N_META + GRID_W * us
    if pattern == 1:
        q0, k0 = pl.multiple_of(q0, 16), pl.multiple_of(k0, 16)
    return q0, k0


def _na_fill_bias(tb_ref, bias_ref):
    neg = jnp.full((GRID_W, GRID_W), -1e30, F32)
    for h in range(NA_HB):
        for pattern in range(3):
            for a in range(NA_G):
                for j in range(NA_U):
                    idx = _na_table_index(pattern, a, j)
                    bias_ref[h, pattern, a * GRID_W:(a + 1) * GRID_W, j * GRID_W:(j + 1) * GRID_W] = (
                        neg if idx is None else tb_ref[h, idx])


def _na_steps(rows, step, carry):
    n_steps = rows // NA_G
    carry = step(0, 0, carry)
    carry = lax.fori_loop(1, n_steps - 1, functools.partial(step, 1), carry)
    return step(2, n_steps - 1, carry)


def _na_head_lanes():
    lane = lax.broadcasted_iota(jnp.int32, (1, LANES), 1)
    return [lane // NA_HEAD_DIM == h for h in range(NA_HB)]


def _na_only(mask, x):
    return jnp.where(mask, x, jnp.zeros_like(x))


def _na_fwd(proj, tb, *, n_tok, nh, name):
    lp = proj.shape[0]
    dh, hb = NA_HEAD_DIM, NA_HB
    naw = nh * dh
    rows = n_tok // GRID_W
    scale = dh ** -0.5

    def body(q_ref, k_ref, v_ref, tb_ref, o_ref, lse_ref, q16_ref, k16_ref, v16_ref, bias_ref):
        o_ref[...] = jnp.zeros_like(o_ref)
        lse_ref[...] = jnp.zeros_like(lse_ref)
        q16_ref[...] = q_ref[...].astype(BF16)
        k16_ref[...] = k_ref[...].astype(BF16)
        v16_ref[...] = v_ref[...].astype(BF16)
        _na_fill_bias(tb_ref, bias_ref)
        heads = _na_head_lanes()
        km = k16_ref[0:N_META, :]
        vm = v16_ref[0:N_META, :]
        qm = q16_ref[0:N_META, :]
        o_m = None
        for h in range(hb):
            s = _nt(_na_only(heads[h], qm), km) * scale
            m = jnp.max(s, axis=1, keepdims=True)
            p = jnp.exp(s - m)
            l = jnp.sum(p, axis=1, keepdims=True)
            o_h = _nn(p.astype(BF16), vm) / l
            o_m = o_h if o_m is None else jnp.where(heads[h], o_h, o_m)
            lse_ref[h, 0:N_META, :] = m + jnp.log(l)
        o_ref[0:N_META, :] = o_m

        def step(pattern, t, carry):
            q0, k0 = _na_step_rows(pattern, t, rows)
            q16 = q16_ref[pl.ds(q0, NA_QN), :]
            k16 = k16_ref[pl.ds(k0, NA_KN), :]
            v16 = v16_ref[pl.ds(k0, NA_KN), :]
            o = None
            for h in range(hb):
                q_h = _na_only(heads[h], q16)
                s = _nt(q_h, k16) * scale + bias_ref[h, pattern]
                sm = _nt(q_h, km) * scale
                m = jnp.maximum(jnp.max(s, axis=1, keepdims=True),
                                jnp.max(sm, axis=1, keepdims=True))
                p = jnp.exp(s - m)
                pm = jnp.exp(sm - m)
                l = jnp.sum(p, axis=1, keepdims=True) + jnp.sum(pm, axis=1, keepdims=True)
                o_h = (_nn(p.astype(BF16), v16) + _nn(pm.astype(BF16), vm)) / l
                o = o_h if o is None else jnp.where(heads[h], o_h, o)
                lse_ref[h, pl.ds(q0, NA_QN), :] = m + jnp.log(l)
            o_ref[pl.ds(q0, NA_QN), :] = o
            return carry

        _na_steps(rows, step, 0)

    cblk = lambda col: pl.BlockSpec((lp, LANES), lambda g: (0, col // LANES + g))
    return pl.pallas_call(
        body, grid=(nh // hb,),
        in_specs=[cblk(0), cblk(naw), cblk(2 * naw),
                  pl.BlockSpec((hb, 2 * NA_WIN_H - 1, GRID_W, GRID_W), lambda g: (g, 0, 0, 0))],
        out_specs=[cblk(0), pl.BlockSpec((hb, lp, 1), lambda g: (g, 0, 0))],
        out_shape=[jax.ShapeDtypeStruct((lp, naw), F32), jax.ShapeDtypeStruct((nh, lp, 1), F32)],
        scratch_shapes=[pltpu.VMEM((lp, LANES), BF16)] * 3 + [pltpu.VMEM((hb, 3, NA_QN, NA_KN), F32)],
        compiler_params=_cp("parallel"), name=name)(proj, proj, proj, tb)


def _na_bwd(proj, tb, o, lse, do, *, n_tok, nh, name):
    lp = proj.shape[0]
    dh, hb = NA_HEAD_DIM, NA_HB
    naw = nh * dh
    rows = n_tok // GRID_W
    scale = dh ** -0.5

    def body(q_ref, k_ref, v_ref, tb_ref, o_ref, lse_ref, do_ref, dq_ref, dk_ref, dv_ref, dtb_ref,
             q16_ref, k16_ref, v16_ref, bias_ref):
        dq_ref[...] = jnp.zeros_like(dq_ref)
        dk_ref[...] = jnp.zeros_like(dk_ref)
        dv_ref[...] = jnp.zeros_like(dv_ref)
        dtb_ref[...] = jnp.zeros_like(dtb_ref)
        q16_ref[...] = q_ref[...].astype(BF16)
        k16_ref[...] = k_ref[...].astype(BF16)
        v16_ref[...] = v_ref[...].astype(BF16)
        _na_fill_bias(tb_ref, bias_ref)
        heads = _na_head_lanes()
        km = k16_ref[0:N_META, :]
        vm = v16_ref[0:N_META, :]
        qm = q16_ref[0:N_META, :]
        dom = do_ref[0:N_META, :]
        prod = dom * o_ref[0:N_META, :]
        dq_m = None
        dkm0 = jnp.zeros((N_META, LANES), F32)
        dvm0 = jnp.zeros((N_META, LANES), F32)
        for h in range(hb):
            q_h = _na_only(heads[h], qm)
            do_h = _na_only(heads[h], dom).astype(BF16)
            p = jnp.exp(_nt(q_h, km) * scale - lse_ref[h, 0:N_META, :])
            delta = jnp.sum(_na_only(heads[h], prod), axis=1, keepdims=True)
            ds = (p * (_nt(do_h, vm) - delta)).astype(BF16)
            dq_h = _nn(ds, km) * scale
            dq_m = dq_h if dq_m is None else jnp.where(heads[h], dq_h, dq_m)
            dkm0 = dkm0 + _tn(ds, q_h) * scale
            dvm0 = dvm0 + _tn(p.astype(BF16), do_h)
        dq_ref[0:N_META, :] = dq_m

        def step(pattern, t, carry):
            dkm, dvm = carry
            q0, k0 = _na_step_rows(pattern, t, rows)
            q16 = q16_ref[pl.ds(q0, NA_QN), :]
            k16 = k16_ref[pl.ds(k0, NA_KN), :]
            v16 = v16_ref[pl.ds(k0, NA_KN), :]
            dov = do_ref[pl.ds(q0, NA_QN), :]
            prod = dov * o_ref[pl.ds(q0, NA_QN), :]
            dq = None
            dk = jnp.zeros((NA_KN, LANES), F32)
            dv = jnp.zeros((NA_KN, LANES), F32)
            for h in range(hb):
                q_h = _na_only(heads[h], q16)
                do_h = _na_only(heads[h], dov).astype(BF16)
                lse = lse_ref[h, pl.ds(q0, NA_QN), :]
                p = jnp.exp(_nt(q_h, k16) * scale + bias_ref[h, pattern] - lse)
                pm = jnp.exp(_nt(q_h, km) * scale - lse)
                delta = jnp.sum(_na_only(heads[h], prod), axis=1, keepdims=True)
                ds = p * (_nt(do_h, v16) - delta)
                dsm = (pm * (_nt(do_h, vm) - delta)).astype(BF16)
                ds16 = ds.astype(BF16)
                dq_h = (_nn(ds16, k16) + _nn(dsm, km)) * scale
                dq = dq_h if dq is None else jnp.where(heads[h], dq_h, dq)
                dk = dk + _tn(ds16, q_h) * scale
                dv = dv + _tn(p.astype(BF16), do_h)
                dkm = dkm + _tn(dsm, q_h) * scale
                dvm = dvm + _tn(pm.astype(BF16), do_h)
                for a in range(NA_G):
                    for j in range(NA_U):
                        idx = _na_table_index(pattern, a, j)
                        if idx is not None:
                            dtb_ref[h, idx] += ds[a * GRID_W:(a + 1) * GRID_W,
                                                  j * GRID_W:(j + 1) * GRID_W]
            dq_ref[pl.ds(q0, NA_QN), :] = dq
            dk_ref[pl.ds(k0, NA_KN), :] += dk
            dv_ref[pl.ds(k0, NA_KN), :] += dv
            return dkm, dvm

        dkm, dvm = _na_steps(rows, step, (dkm0, dvm0))
        dk_ref[0:N_META, :] += dkm
        dv_ref[0:N_META, :] += dvm

    cblk = lambda col: pl.BlockSpec((lp, LANES), lambda g: (0, col // LANES + g))
    tbs = pl.BlockSpec((hb, 2 * NA_WIN_H - 1, GRID_W, GRID_W), lambda g: (g, 0, 0, 0))
    sds = jax.ShapeDtypeStruct((lp, naw), F32)
    return pl.pallas_call(
        body, grid=(nh // hb,),
        in_specs=[cblk(0), cblk(naw), cblk(2 * naw), tbs, cblk(0),
                  pl.BlockSpec((hb, lp, 1), lambda g: (g, 0, 0)), cblk(0)],
        out_specs=[cblk(0), cblk(0), cblk(0), tbs],
        out_shape=[sds, sds, sds, jax.ShapeDtypeStruct(tb.shape, F32)],
        scratch_shapes=[pltpu.VMEM((lp, LANES), BF16)] * 3 + [pltpu.VMEM((hb, 3, NA_QN, NA_KN), F32)],
        compiler_params=_cp("parallel"), name=name)(proj, proj, proj, tb, o, lse, do)


def _rpb_onehot():
    c = np.arange(GRID_W)[:, None]
    w = np.arange(GRID_W)[None, :]
    cs = np.clip(c - NA_WIN_W // 2, 0, GRID_W - NA_WIN_W)
    in_win = (w >= cs) & (w < cs + NA_WIN_W)
    dc = np.clip(w - c, -(NA_WIN_W - 1), NA_WIN_W - 1) + NA_WIN_W - 1
    oh = np.zeros((LANES, GRID_W * GRID_W), np.float32)
    flat = np.arange(GRID_W * GRID_W).reshape(GRID_W, GRID_W)
    oh[dc[in_win], flat[in_win]] = 1.0
    neg = np.where(in_win, 0.0, -1e30).astype(np.float32).reshape(1, -1)
    return oh, neg


def _assemble_dproj(dq_na, dk_na, dv_na, dq_f, dq_b, dz_f, dz_b, dv_f, dv_b, dg, dgn, dgh, *, name):
    lp, naw = dq_na.shape
    hgw = dq_f.shape[1]
    d = dgn.shape[1]
    cols = 3 * naw + 5 * hgw + 2 * d
    tr = _row_tile(lp, 3 * naw * 4 + 6 * hgw * 4 + hgw * 2 + 2 * d * 2 + cols * 2)

    def body(nq_ref, nk_ref, nv_ref, qf_ref, qb_ref, zf_ref, zb_ref, vf_ref, vb_ref, g_ref, gn_ref,
             gh_ref, o_ref):
        o_ref[:, 0:naw] = nq_ref[...].astype(BF16)
        o_ref[:, naw:2 * naw] = nk_ref[...].astype(BF16)
        o_ref[:, 2 * naw:3 * naw] = nv_ref[...].astype(BF16)
        c0 = 3 * naw
        o_ref[:, c0:c0 + hgw] = (qf_ref[...] + qb_ref[...]).astype(BF16)
        o_ref[:, c0 + hgw:c0 + 2 * hgw] = zf_ref[...].astype(BF16)
        o_ref[:, c0 + 2 * hgw:c0 + 3 * hgw] = zb_ref[...].astype(BF16)
        o_ref[:, c0 + 3 * hgw:c0 + 4 * hgw] = (vf_ref[...] + vb_ref[...]).astype(BF16)
        o_ref[:, c0 + 4 * hgw:c0 + 5 * hgw] = g_ref[...]
        o_ref[:, c0 + 5 * hgw:c0 + 5 * hgw + d] = gn_ref[...]
        o_ref[:, c0 + 5 * hgw + d:] = gh_ref[...]

    hg, na = _rspec(tr, hgw), _rspec(tr, naw)
    return pl.pallas_call(
        body, grid=(lp // tr,),
        in_specs=[na, na, na, hg, hg, hg, hg, hg, hg, hg, _rspec(tr, d), _rspec(tr, d)],
        out_specs=_rspec(tr, cols),
        out_shape=jax.ShapeDtypeStruct((lp, cols), BF16),
        compiler_params=_cp("parallel"), name=name)(dq_na, dk_na, dv_na, dq_f, dq_b, dz_f, dz_b,
                                                    dv_f, dv_b, dg, dgn, dgh)


GROUP_STEPS = 8


def _group_tiles(rows, align):
    steps = GROUP_STEPS if all(r % (GROUP_STEPS * align) == 0 for r in rows) else 1
    return steps, [r // steps for r in rows]


def _adamw(ws, gs, ms, vs, *, name):
    n = len(ws)
    steps, trs = _group_tiles([w.shape[0] for w in ws], 8)

    def body(*refs):
        for i in range(n):
            w_ref, g_ref, m_ref, v_ref = refs[4 * i:4 * i + 4]
            d_ref, mo_ref, vo_ref = refs[4 * n + 3 * i:4 * n + 3 * i + 3]
            gv = g_ref[...]
            mn = ADAM_B1 * m_ref[...] + (1.0 - ADAM_B1) * gv
            vn = ADAM_B2 * v_ref[...] + (1.0 - ADAM_B2) * (gv * gv)
            m_hat = mn / (1.0 - ADAM_B1 ** ADAM_STEP)
            v_hat = vn / (1.0 - ADAM_B2 ** ADAM_STEP)
            d_ref[...] = -ADAM_LR * (m_hat / (jnp.sqrt(v_hat) + ADAM_EPS) + ADAM_WD * w_ref[...])
            mo_ref[...] = mn
            vo_ref[...] = vn

    specs = [_rspec(tr, w.shape[1]) for tr, w in zip(trs, ws)]
    out = pl.pallas_call(
        body, grid=(steps,),
        in_specs=[s for s in specs for _ in range(4)],
        out_specs=[s for s in specs for _ in range(3)],
        out_shape=[jax.ShapeDtypeStruct(w.shape, F32) for w in ws for _ in range(3)],
        compiler_params=_cp("parallel"), name=name)(*[a for q in zip(ws, gs, ms, vs) for a in q])
    return [tuple(out[3 * i:3 * i + 3]) for i in range(n)]


def _local_step(x, tgt, meta, first_weight, rest_weights, g_mix, g_mlp, g_fin, hg_gain, rpb, lb,
                early_grads=None, mid_grads=None, late_grad=None, rest_landed=None,
                last_grads=None):
    n_tok, d = x.shape
    hgw = hg_gain.shape[1]
    nh, hh = rpb.shape[0], hgw // HG_DK
    naw = nh * NA_HEAD_DIM
    l_real = N_META + n_tok
    lp = -(-l_real // ROW_ALIGN) * ROW_ALIGN
    col_qhg = 3 * naw
    col_zf, col_zb, col_i, col_g = (col_qhg + hgw, col_qhg + 2 * hgw, col_qhg + 3 * hgw,
                                    col_qhg + 4 * hgw)
    col_gate = col_qhg + 5 * hgw

    oh_np, neg_np = _rpb_onehot()
    oh = jnp.asarray(oh_np)
    rpb_p = jnp.pad(rpb.reshape(nh * (2 * NA_WIN_H - 1), 2 * NA_WIN_W - 1),
                    ((0, 0), (0, LANES - (2 * NA_WIN_W - 1))))
    tb = _matmul(rpb_p, oh, tm=rpb_p.shape[0], tn=512, tk=LANES, precision=HIGHEST,
                 name="rpb_expand")
    tb = (tb + jnp.asarray(neg_np)).reshape(nh, 2 * NA_WIN_H - 1, GRID_W, GRID_W)

    h0, a = _embed_norm(x, meta, g_mix, lp=lp, name="norm_mix")
    w_in = first_weight(a)
    proj = _matmul(a, w_in, name="mm_in")
    o_na, lse = _na_fwd(proj, tb, n_tok=n_tok, nh=nh, name="na_fwd")
    lb_f = lb[0].reshape(hh, 1, HG_DK)
    lb_b = lb[1].reshape(hh, 1, HG_DK)
    scan_kw = dict(col_q=col_qhg, col_i=col_i, hh=hh)
    o_f, st_f = _hg_scan_fwd(proj, lb_f, reverse=False, col_z=col_zf, name="hg_scan_f", **scan_kw)
    token = rest_landed(o_f) if rest_landed else None
    lb_b_late = lb_b if token is None else lb_b + token[0:1, 0:1]
    o_b, st_b = _hg_scan_fwd(proj, lb_b_late, reverse=True, col_z=col_zb, name="hg_scan_b",
                             **scan_kw)
    o_hg = _hg_out(o_f, o_b, proj, hg_gain, col_g=col_g, name="hg_out")
    w_na, w_hg, w_o, w_up, w_down = rest_weights(o_hg)
    y_na = _matmul(o_na, w_na, name="mm_na_out", out_dtype=BF16)
    gates = ((proj, col_gate), (proj, col_gate + d))

    def mix_gates(acc, gn, gh, yn):
        return acc, _sigmoid(gn) * yn + _sigmoid(gh) * acc

    def mix_gates_bwd(dmix, gn, gh, yn, yh):
        sn, sh = _sigmoid(gn), _sigmoid(gh)
        return dmix * sn, dmix * sh, dmix * yn * sn * (1.0 - sn), dmix * yh * sh * (1.0 - sh)

    y_hg, mix = _matmul(o_hg, w_hg, name="mm_hg_out", epilogue=mix_gates,
                        tiles=(*gates, (y_na, 0)), out_dtypes=(BF16, BF16))
    t1 = _matmul(mix, w_o, name="mm_o")
    h1, mlp_in = _residual_norm(h0, t1, g_mlp, name="resid_norm_mlp")
    u, act = _matmul(mlp_in, w_up, name="mm_up", out_dtypes=(BF16, BF16),
                     epilogue=lambda acc: (acc, jnp.square(jnp.maximum(acc, 0.0))))
    t2 = _matmul(act, w_down, name="mm_down")
    dh2, dh2_16, loss, dg_fin = _final_loss(h1, t2, g_fin, tgt, name="final_loss")

    (du,) = _matmul(dh2_16, w_down, tb=True, name="mm_down_dx", tiles=((u, 0),),
                    out_dtypes=(BF16,),
                    epilogue=lambda acc, uv: (acc * 2.0 * jnp.maximum(uv, 0.0),))
    dw_down = _matmul(act, dh2_16, ta=True, name="mm_down_dw")
    dm = _matmul(du, w_up, tb=True, name="mm_up_dx")
    dw_up = _matmul(mlp_in, du, ta=True, name="mm_up_dw")
    dh1, dh1_16, dg_mlp = _rmsnorm_bwd_add(h1, g_mlp, dm, dh2, name="norm_mlp_bwd")
    dy_na, dy_hg, dgn, dgh = _matmul(dh1_16, w_o, tb=True, name="mm_o_dx", epilogue=mix_gates_bwd,
                                     tiles=(*gates, (y_na, 0), (y_hg, 0)), out_dtypes=(BF16,) * 4)
    dw_o = _matmul(mix, dh1_16, ta=True, name="mm_o_dw")
    do_na = _matmul(dy_na, w_na, tb=True, name="mm_na_out_dx")
    dw_na = _matmul(o_na, dy_na, ta=True, name="mm_na_out_dw")
    do_hg = _matmul(dy_hg, w_hg, tb=True, name="mm_hg_out_dx")
    dw_hg = _matmul(o_hg, dy_hg, ta=True, name="mm_hg_out_dw")
    token = early_grads([dw_na, dw_hg, dw_o, dw_up, dw_down]) if early_grads else None
    if token is not None:
        hg_gain = hg_gain + token[0:1, 0:1]
    d_o, dg_hg, d_gain = _hg_out_bwd(o_f, o_b, proj, hg_gain, do_hg, col_g=col_g, name="hg_out_bwd")
    dq_f, dz_f, dv_f, dlb_f = _hg_scan_bwd(proj, lb_f, st_f, d_o, reverse=False, col_z=col_zf,
                                           name="hg_scan_f_bwd", **scan_kw)
    token = mid_grads(dq_f) if mid_grads else None
    lb_b_late = lb_b if token is None else lb_b + token[0:1, 0:1]
    dq_b, dz_b, dv_b, dlb_b = _hg_scan_bwd(proj, lb_b_late, st_b, d_o, reverse=True, col_z=col_zb,
                                           name="hg_scan_b_bwd", **scan_kw)
    dq_na, dk_na, dv_na, dtb = _na_bwd(proj, tb, o_na, lse, do_na, n_tok=n_tok, nh=nh, name="na_bwd")
    dproj = _assemble_dproj(dq_na, dk_na, dv_na, dq_f, dq_b, dz_f, dz_b, dv_f, dv_b, dg_hg, dgn,
                            dgh, name="assemble_dproj")
    dw_in = _matmul(a, dproj, ta=True, name="mm_in_dw")
    token = late_grad(dw_in) if late_grad else None
    da = _matmul(dproj, w_in, tb=True, name="mm_in_dx", after=token)
    token = last_grads(da) if last_grads else None
    g_mix_late = g_mix if token is None else g_mix + token[0:1, 0:1]
    dx, dmeta, dg_mix = _rmsnorm_bwd_tokens(h0, g_mix_late, da, dh1, n_tok=n_tok,
                                            name="norm_mix_bwd")
    d_rpb = _matmul(dtb.reshape(nh * (2 * NA_WIN_H - 1), GRID_W * GRID_W), oh, tb=True,
                    tm=nh * (2 * NA_WIN_H - 1), tn=LANES, tk=1024, precision=HIGHEST,
                    name="rpb_reduce")
    d_lb = jnp.concatenate([dlb_f.reshape(1, hgw), dlb_b.reshape(1, hgw)], axis=0)
    return (loss, dx, dmeta, dw_in, dw_na, dw_hg, dw_o, dw_up, dw_down,
            dg_mix, dg_mlp, dg_fin, d_gain, d_rpb, d_lb)


N_CHIPS = 4
N_DEV = 8
ANY = pl.BlockSpec(memory_space=pl.ANY)


def _place():
    x, y, c = lax.axis_index("x"), lax.axis_index("y"), lax.axis_index("c")
    others = []
    for j in (1, 2, 3):
        tx = (1 - x) if (j >> 1) else x
        ty = (1 - y) if (j & 1) else y
        others.append((tx, ty))
    return x, y, c, others


def _piece(ref, axis, k, half, rh, cs):
    if axis == 1:
        return ref.at[pl.ds(pl.multiple_of(half * rh, 16), rh), pl.ds(pl.multiple_of(k * cs, LANES), cs)]
    return ref.at[pl.ds(pl.multiple_of(k * 2 * rh + half * rh, 16), rh), :]


def _cast_into_full(shards, axes, place, *, name):
    n = len(shards)
    steps, trs = _group_tiles([s.shape[0] for s in shards], 16)

    def body(p_ref, *refs):
        for i in range(n):
            refs[n + i][...] = refs[i][...].astype(BF16)

    def out_spec(tr, cs, axis):
        if axis == 1:
            return pl.BlockSpec((tr, cs), lambda i, p_ref: (i, p_ref[0]))
        return pl.BlockSpec((tr, cs), lambda i, p_ref: (p_ref[0] * steps + i, 0))

    return pl.pallas_call(
        body,
        grid_spec=pltpu.PrefetchScalarGridSpec(
            num_scalar_prefetch=1, grid=(steps,),
            in_specs=[pl.BlockSpec((tr, s.shape[1]), lambda i, p_ref: (i, 0))
                      for tr, s in zip(trs, shards)],
            out_specs=[out_spec(tr, s.shape[1], ax) for tr, s, ax in zip(trs, shards, axes)]),
        out_shape=[jax.ShapeDtypeStruct((s.shape[0], s.shape[1] * N_CHIPS) if ax == 1
                                        else (s.shape[0] * N_CHIPS, s.shape[1]), BF16)
                   for s, ax in zip(shards, axes)],
        compiler_params=_cp("parallel"), name=name)(place, *shards)


HBM_SPEC = pl.BlockSpec(memory_space=pltpu.HBM)
SEM_SPEC = pl.BlockSpec(memory_space=pltpu.SEMAPHORE)
SPLIT_COPY = pltpu.CompilerParams(has_side_effects=pltpu.SideEffectType.DATAFLOW_SIDE_EFFECTING)
TOKEN = jax.ShapeDtypeStruct((8, LANES), F32)


def _geo(fulls, axes):
    out = []
    for f, ax in zip(fulls, axes):
        r, cs = (f.shape[0], f.shape[1] // N_CHIPS) if ax == 1 else (f.shape[0] // N_CHIPS, f.shape[1])
        out.append((ax, r // 2, cs))
    return out


def _gather_copies(refs, geo, send_sems, recv_sems):
    x, y, c, others = _place()
    chip = 2 * x + y
    cps = []
    for i, (ax, rh, cs) in enumerate(geo):
        mine = _piece(refs[i], ax, chip, c, rh, cs)
        for j, (tx, ty) in enumerate(others):
            cps.append(pltpu.make_async_remote_copy(
                src_ref=mine, dst_ref=mine, send_sem=send_sems.at[3 * i + j],
                recv_sem=recv_sems.at[3 * i + j], device_id=(tx, ty, c), device_id_type=MESH))
    return cps


def _allgather_start(fulls, axes, after, *, name):
    n = len(fulls)
    geo = _geo(fulls, axes)

    def body(*refs):
        w_refs = refs[:n]
        send_sems, recv_sems = refs[n + 1], refs[n + 2]
        token = refs[2 * n + 3]
        for cp in _gather_copies(w_refs, geo, send_sems, recv_sems):
            cp.start()
        token[...] = jnp.zeros_like(token)

    out = pl.pallas_call(
        body, name=name,
        out_shape=(pltpu.SemaphoreType.DMA((3 * n,)), pltpu.SemaphoreType.DMA((3 * n,)),
                   *[pltpu.HBM(f.shape, f.dtype) for f in fulls], TOKEN),
        in_specs=[HBM_SPEC] * n + [ANY],
        out_specs=(SEM_SPEC, SEM_SPEC, *[HBM_SPEC] * n, pl.BlockSpec(memory_space=pltpu.VMEM)),
        input_output_aliases={i: 2 + i for i in range(n)},
        compiler_params=SPLIT_COPY,
    )(*[pltpu.with_memory_space_constraint(f, pltpu.HBM) for f in fulls], after)
    return out[0], out[1], list(out[2:2 + n]), out[2 + n]


def _allgather_wait(send_sems, recv_sems, fulls, axes, after, *, name):
    n = len(fulls)
    geo = _geo(fulls, axes)

    def body(*refs):
        w_refs = refs[:n]
        for cp in _gather_copies(w_refs, geo, refs[n], refs[n + 1]):
            cp.wait_send()
            cp.wait_recv()

    return list(pl.pallas_call(
        body, name=name,
        out_shape=[pltpu.HBM(f.shape, f.dtype) for f in fulls],
        in_specs=[HBM_SPEC] * n + [SEM_SPEC, SEM_SPEC, ANY],
        out_specs=[HBM_SPEC] * n,
        input_output_aliases={i: i for i in range(n)},
        compiler_params=SPLIT_COPY,
    )(*fulls, send_sems, recv_sems, after))


def _allgather_forward(fulls, axes, *, name):
    n = len(fulls)
    geo = _geo(fulls, axes)

    def body(*refs):
        o_refs = refs[n:2 * n]
        send_sems, recv_sems = refs[2 * n:]
        x, y, c, others = _place()

        def rcopy(i, j, half, to):
            ax, rh, cs = geo[i]
            ref = _piece(o_refs[i], ax, 2 * others[j][0] + others[j][1], half, rh, cs)
            return pltpu.make_async_remote_copy(
                src_ref=ref, dst_ref=ref, send_sem=send_sems.at[3 * i + j],
                recv_sem=recv_sems.at[3 * i + j], device_id=to, device_id_type=MESH)

        cps = [rcopy(i, j, c, (x, y, 1 - c)) for i in range(n) for j in range(3)]
        for cp in cps:
            cp.start()
        for i in range(n):
            for j in range(3):
                rcopy(i, j, 1 - c, (x, y, c)).wait_recv()
        for cp in cps:
            cp.wait_send()

    return list(pl.pallas_call(
        body, in_specs=[ANY] * n, out_specs=[ANY] * n,
        out_shape=[jax.ShapeDtypeStruct(f.shape, f.dtype) for f in fulls],
        input_output_aliases={i: i for i in range(n)},
        scratch_shapes=[pltpu.SemaphoreType.DMA((3 * n,)), pltpu.SemaphoreType.DMA((3 * n,))],
        name=name)(*fulls))


def _forward_copies(refs, geo, send_sems, recv_sems):
    x, y, c, others = _place()
    cps = []
    for i, (ax, rh, cs) in enumerate(geo):
        for j, (tx, ty) in enumerate(others):
            ref = _piece(refs[i], ax, 2 * tx + ty, c, rh, cs)
            cps.append(pltpu.make_async_remote_copy(
                src_ref=ref, dst_ref=ref, send_sem=send_sems.at[3 * i + j],
                recv_sem=recv_sems.at[3 * i + j], device_id=(x, y, 1 - c), device_id_type=MESH))
    return cps


def _allgather_forward_start(fulls, axes, *, name):
    n = len(fulls)
    geo = _geo(fulls, axes)

    def body(*refs):
        token = refs[2 * n + 2]
        for cp in _forward_copies(refs[:n], geo, refs[n], refs[n + 1]):
            cp.start()
        token[...] = jnp.zeros_like(token)

    out = pl.pallas_call(
        body, name=name,
        out_shape=(pltpu.SemaphoreType.DMA((3 * n,)), pltpu.SemaphoreType.DMA((3 * n,)),
                   *[pltpu.HBM(f.shape, f.dtype) for f in fulls], TOKEN),
        in_specs=[HBM_SPEC] * n,
        out_specs=(SEM_SPEC, SEM_SPEC, *[HBM_SPEC] * n, pl.BlockSpec(memory_space=pltpu.VMEM)),
        input_output_aliases={i: 2 + i for i in range(n)},
        compiler_params=SPLIT_COPY,
    )(*fulls)
    return out[0], out[1], list(out[2:2 + n]), out[2 + n]


def _allgather_forward_wait(send_sems, recv_sems, fulls, axes, after, *, name):
    n = len(fulls)
    geo = _geo(fulls, axes)

    def body(*refs):
        for cp in _forward_copies(refs[:n], geo, refs[n], refs[n + 1]):
            cp.wait_send()
            cp.wait_recv()

    return list(pl.pallas_call(
        body, name=name,
        out_shape=[pltpu.HBM(f.shape, f.dtype) for f in fulls],
        in_specs=[HBM_SPEC] * n + [SEM_SPEC, SEM_SPEC, ANY],
        out_specs=[HBM_SPEC] * n,
        input_output_aliases={i: i for i in range(n)},
        compiler_params=SPLIT_COPY,
    )(*fulls, send_sems, recv_sems, after))


def _chip_copies(blk_ref, land_ref, send_sems, recv_sems):
    x, y, c, others = _place()
    return [pltpu.make_async_remote_copy(
        src_ref=blk_ref, dst_ref=land_ref.at[2 * x + y], send_sem=send_sems.at[j],
        recv_sem=recv_sems.at[j], device_id=(tx, ty, c), device_id_type=MESH)
        for j, (tx, ty) in enumerate(others)]


def _chip_exchange_start(blk, *, name):
    land = pltpu.with_memory_space_constraint(lax.empty((N_CHIPS, *blk.shape), blk.dtype), pltpu.HBM)

    def body(blk_ref, land_ref, send_sems, recv_sems, blk_out, land_out, token):
        for cp in _chip_copies(blk_ref, land_ref, send_sems, recv_sems):
            cp.start()
        token[...] = jnp.zeros_like(token)

    return pl.pallas_call(
        body, name=name,
        out_shape=(pltpu.SemaphoreType.DMA((3,)), pltpu.SemaphoreType.DMA((3,)),
                   pltpu.HBM(blk.shape, blk.dtype), pltpu.HBM(land.shape, land.dtype), TOKEN),
        in_specs=[HBM_SPEC] * 2,
        out_specs=(SEM_SPEC, SEM_SPEC, HBM_SPEC, HBM_SPEC, pl.BlockSpec(memory_space=pltpu.VMEM)),
        input_output_aliases={0: 2, 1: 3},
        compiler_params=SPLIT_COPY,
    )(pltpu.with_memory_space_constraint(blk, pltpu.HBM), land)


def _chip_exchange_wait(send_sems, recv_sems, blk, land, after, *, name):
    def body(blk_ref, land_ref, send_sems, recv_sems, after_ref, blk_out, land_out):
        for cp in _chip_copies(blk_ref, land_ref, send_sems, recv_sems):
            cp.wait_send()
            cp.wait_recv()

    return pl.pallas_call(
        body, name=name,
        out_shape=[pltpu.HBM(blk.shape, blk.dtype), pltpu.HBM(land.shape, land.dtype)],
        in_specs=[HBM_SPEC] * 2 + [SEM_SPEC, SEM_SPEC, ANY],
        out_specs=[HBM_SPEC] * 2,
        input_output_aliases={0: 0, 1: 1},
        compiler_params=SPLIT_COPY,
    )(blk, land, send_sems, recv_sems, after)[1]


def _scatter_geo(parts, axes):
    out = []
    for p, ax in zip(parts, axes):
        _, rh, cols = p.shape
        out.append((ax, rh, cols // N_CHIPS if ax == 1 else cols))
    return out


def _scatter_copies(p_refs, q_refs, geo, send_sems, recv_sems):
    x, y, c, others = _place()
    chip = 2 * x + y
    cps = []
    for i, (ax, rh, cw) in enumerate(geo):
        for j, (tx, ty) in enumerate(others):
            k = 2 * tx + ty
            src = (p_refs[i].at[0, :, pl.ds(pl.multiple_of(k * cw, LANES), cw)] if ax == 1
                   else p_refs[i].at[k])
            cps.append(pltpu.make_async_remote_copy(
                src_ref=src, dst_ref=q_refs[i].at[chip], send_sem=send_sems.at[3 * i + j],
                recv_sem=recv_sems.at[3 * i + j], device_id=(tx, ty, c), device_id_type=MESH))
    return cps


def _scatter_start(parts, axes, *, name):
    n = len(parts)
    geo = _scatter_geo(parts, axes)
    slots = [pltpu.HBM((N_CHIPS, rh, cw), p.dtype) for p, (_, rh, cw) in zip(parts, geo)]

    def body(*refs):
        p_refs, q_refs = refs[:n], refs[n:2 * n]
        send_sems, recv_sems = refs[2 * n], refs[2 * n + 1]
        token = refs[4 * n + 2]
        for cp in _scatter_copies(p_refs, q_refs, geo, send_sems, recv_sems):
            cp.start()
        token[...] = jnp.zeros_like(token)

    land = [pltpu.with_memory_space_constraint(lax.empty(s.inner_aval.shape, s.inner_aval.dtype), pltpu.HBM)
            for s in slots]
    out = pl.pallas_call(
        body, name=name,
        out_shape=(pltpu.SemaphoreType.DMA((3 * n,)), pltpu.SemaphoreType.DMA((3 * n,)),
                   *[pltpu.HBM(p.shape, p.dtype) for p in parts], *slots, TOKEN),
        in_specs=[HBM_SPEC] * (2 * n),
        out_specs=(SEM_SPEC, SEM_SPEC, *[HBM_SPEC] * (2 * n), pl.BlockSpec(memory_space=pltpu.VMEM)),
        input_output_aliases={i: 2 + i for i in range(2 * n)},
        compiler_params=SPLIT_COPY,
    )(*[pltpu.with_memory_space_constraint(p, pltpu.HBM) for p in parts], *land)
    return out[0], out[1], list(out[2:2 + n]), list(out[2 + n:2 + 2 * n]), out[2 + 2 * n]


def _scatter_wait(send_sems, recv_sems, parts, slots, axes, after, *, name):
    n = len(parts)
    geo = _scatter_geo(parts, axes)

    def body(*refs):
        p_refs, q_refs = refs[:n], refs[n:2 * n]
        for cp in _scatter_copies(p_refs, q_refs, geo, refs[2 * n], refs[2 * n + 1]):
            cp.wait_send()
            cp.wait_recv()

    out = pl.pallas_call(
        body, name=name,
        out_shape=[pltpu.HBM(a.shape, a.dtype) for a in (*parts, *slots)],
        in_specs=[HBM_SPEC] * (2 * n) + [SEM_SPEC, SEM_SPEC, ANY],
        out_specs=[HBM_SPEC] * (2 * n),
        input_output_aliases={i: i for i in range(2 * n)},
        compiler_params=SPLIT_COPY,
    )(*parts, *slots, send_sems, recv_sems, after)
    return list(out[:n]), list(out[n:])


def _sibling_swap(grads, *, name):
    n = len(grads)
    out_shape = [jax.ShapeDtypeStruct((g.shape[0], g.shape[1] // 2, g.shape[2]), g.dtype)
                 for g in grads]

    def body(*refs):
        g_refs, o_refs = refs[:n], refs[n:2 * n]
        send_sems, recv_sems = refs[2 * n:]
        x, y, c, _ = _place()
        cps = []
        for i in range(n):
            rh = grads[i].shape[1] // 2
            src = g_refs[i].at[:, pl.ds(pl.multiple_of((1 - c) * rh, 16), rh), :]
            cp = pltpu.make_async_remote_copy(
                src_ref=src, dst_ref=o_refs[i], send_sem=send_sems.at[i], recv_sem=recv_sems.at[i],
                device_id=(x, y, 1 - c), device_id_type=MESH)
            cp.start()
            cps.append(cp)
        for cp in cps:
            cp.wait()

    return pl.pallas_call(
        body, in_specs=[ANY] * n, out_specs=[ANY] * n, out_shape=out_shape,
        scratch_shapes=[pltpu.SemaphoreType.DMA((n,)), pltpu.SemaphoreType.DMA((n,))],
        name=name)(*grads)


def _swap_copies(g_refs, r_refs, shapes, send_sems, recv_sems):
    x, y, c, _ = _place()
    cps = []
    for i, shape in enumerate(shapes):
        rh = shape[1] // 2
        src = g_refs[i].at[:, pl.ds(pl.multiple_of((1 - c) * rh, 16), rh), :]
        cps.append(pltpu.make_async_remote_copy(
            src_ref=src, dst_ref=r_refs[i], send_sem=send_sems.at[i], recv_sem=recv_sems.at[i],
            device_id=(x, y, 1 - c), device_id_type=MESH))
    return cps


def _sibling_swap_start(grads, *, name):
    n = len(grads)
    shapes = [g.shape for g in grads]
    lands = [pltpu.HBM((s[0], s[1] // 2, s[2]), g.dtype) for s, g in zip(shapes, grads)]

    def body(*refs):
        g_refs, r_refs = refs[:n], refs[n:2 * n]
        token = refs[4 * n + 2]
        for cp in _swap_copies(g_refs, r_refs, shapes, refs[2 * n], refs[2 * n + 1]):
            cp.start()
        token[...] = jnp.zeros_like(token)

    land = [pltpu.with_memory_space_constraint(lax.empty(s.inner_aval.shape, s.inner_aval.dtype), pltpu.HBM)
            for s in lands]
    out = pl.pallas_call(
        body, name=name,
        out_shape=(pltpu.SemaphoreType.DMA((n,)), pltpu.SemaphoreType.DMA((n,)),
                   *[pltpu.HBM(g.shape, g.dtype) for g in grads], *lands, TOKEN),
        in_specs=[HBM_SPEC] * (2 * n),
        out_specs=(SEM_SPEC, SEM_SPEC, *[HBM_SPEC] * (2 * n), pl.BlockSpec(memory_space=pltpu.VMEM)),
        input_output_aliases={i: 2 + i for i in range(2 * n)},
        compiler_params=SPLIT_COPY,
    )(*[pltpu.with_memory_space_constraint(g, pltpu.HBM) for g in grads], *land)
    return out[0], out[1], list(out[2:2 + n]), list(out[2 + n:2 + 2 * n]), out[2 + 2 * n]


def _sibling_swap_wait(send_sems, recv_sems, grads, lands, after, *, name):
    n = len(grads)
    shapes = [g.shape for g in grads]

    def body(*refs):
        g_refs, r_refs = refs[:n], refs[n:2 * n]
        for cp in _swap_copies(g_refs, r_refs, shapes, refs[2 * n], refs[2 * n + 1]):
            cp.wait_send()
            cp.wait_recv()

    out = pl.pallas_call(
        body, name=name,
        out_shape=[pltpu.HBM(a.shape, a.dtype) for a in (*grads, *lands)],
        in_specs=[HBM_SPEC] * (2 * n) + [SEM_SPEC, SEM_SPEC, ANY],
        out_specs=[HBM_SPEC] * (2 * n),
        input_output_aliases={i: i for i in range(2 * n)},
        compiler_params=SPLIT_COPY,
    )(*grads, *lands, send_sems, recv_sems, after)
    return list(out[:n]), list(out[n:])


def _pair_add(g3s, rxs, place, *, out_dtype, name):
    n = len(g3s)
    steps, trs = _group_tiles([g.shape[1] // 2 for g in g3s], 16)

    def body(p_ref, *refs):
        for i in range(n):
            refs[2 * n + i][...] = (refs[2 * i][...] + refs[2 * i + 1][...]).astype(out_dtype)

    in_specs, out_specs = [], []
    for g, tr in zip(g3s, trs):
        blk = (g.shape[0], tr, g.shape[2])
        in_specs += [pl.BlockSpec(blk, lambda i, p_ref: (0, p_ref[1] * steps + i, 0)),
                     pl.BlockSpec(blk, lambda i, p_ref: (0, i, 0))]
        out_specs.append(pl.BlockSpec(blk, lambda i, p_ref: (0, i, 0)))
    return pl.pallas_call(
        body,
        grid_spec=pltpu.PrefetchScalarGridSpec(
            num_scalar_prefetch=1, grid=(steps,), in_specs=in_specs, out_specs=out_specs),
        out_shape=[jax.ShapeDtypeStruct((g.shape[0], g.shape[1] // 2, g.shape[2]), out_dtype)
                   for g in g3s],
        compiler_params=_cp("parallel"), name=name)(place, *[a for q in zip(g3s, rxs) for a in q])


def _sum_slots(q, *, name):
    ns, rows, cols = q.shape
    tr = next(t for t in (128, 64, 32, 16, 8) if rows % t == 0)

    def body(q_ref, o_ref):
        acc = q_ref[0].astype(F32)
        for k in range(1, ns):
            acc = acc + q_ref[k].astype(F32)
        o_ref[...] = acc

    return pl.pallas_call(
        body, grid=(rows // tr,),
        in_specs=[pl.BlockSpec((ns, tr, cols), lambda i: (0, i, 0))],
        out_specs=_rspec(tr, cols),
        out_shape=jax.ShapeDtypeStruct((rows, cols), F32),
        compiler_params=_cp("parallel"), name=name)(q)


def _sum_chips(qs, ps, place, axes, *, name):
    n = len(qs)
    per = N_CHIPS + 1
    steps, trs = _group_tiles([q.shape[1] for q in qs], 16)

    def body(p_ref, *refs):
        chip = p_ref[0]
        for i in range(n):
            q_refs, own_ref = refs[per * i:per * i + N_CHIPS], refs[per * i + N_CHIPS]
            acc = jnp.where(chip == 0, own_ref[...], q_refs[0][...]).astype(F32)
            for k in range(1, N_CHIPS):
                acc = acc + jnp.where(chip == k, own_ref[...], q_refs[k][...]).astype(F32)
            refs[per * n + i][...] = acc

    def slot_spec(k, tr, cw):
        return pl.BlockSpec((None, tr, cw),
                            lambda i, p_ref: (jnp.where(p_ref[0] == k, (k + 1) % N_CHIPS, k), i, 0))

    in_specs, out_specs, operands = [], [], []
    for q, p, ax, tr in zip(qs, ps, axes, trs):
        cw = q.shape[2]
        in_specs += [slot_spec(k, tr, cw) for k in range(N_CHIPS)]
        in_specs.append(pl.BlockSpec((None, tr, cw), (lambda i, p_ref: (0, i, p_ref[0])) if ax == 1
                                     else (lambda i, p_ref: (p_ref[0], i, 0))))
        out_specs.append(pl.BlockSpec((tr, cw), lambda i, p_ref: (p_ref[1] * steps + i, 0)))
        operands += [q] * N_CHIPS + [p]
    return pl.pallas_call(
        body,
        grid_spec=pltpu.PrefetchScalarGridSpec(
            num_scalar_prefetch=1, grid=(steps,), in_specs=in_specs, out_specs=out_specs),
        out_shape=[jax.ShapeDtypeStruct((2 * q.shape[1], q.shape[2]), F32) for q in qs],
        compiler_params=_cp("parallel"), name=name)(place, *operands)


def _sibling_share(shards, *, name):
    n = len(shards)

    def body(*refs):
        o_refs = refs[n:2 * n]
        send_sems, recv_sems = refs[2 * n:]
        x, y, c, _ = _place()
        cps = []
        for i in range(n):
            rh = shards[i].shape[0] // 2
            mine = o_refs[i].at[pl.ds(pl.multiple_of(c * rh, 8), rh), :]
            cp = pltpu.make_async_remote_copy(
                src_ref=mine, dst_ref=mine, send_sem=send_sems.at[i], recv_sem=recv_sems.at[i],
                device_id=(x, y, 1 - c), device_id_type=MESH)
            cp.start()
            cps.append(cp)
        for i in range(n):
            rh = shards[i].shape[0] // 2
            theirs = o_refs[i].at[pl.ds(pl.multiple_of((1 - c) * rh, 8), rh), :]
            pltpu.make_async_remote_copy(
                src_ref=theirs, dst_ref=theirs, send_sem=send_sems.at[i], recv_sem=recv_sems.at[i],
                device_id=(x, y, c), device_id_type=MESH).wait_recv()
        for cp in cps:
            cp.wait_send()

    return pl.pallas_call(
        body, in_specs=[ANY] * n, out_specs=[ANY] * n,
        out_shape=[jax.ShapeDtypeStruct(h.shape, h.dtype) for h in shards],
        input_output_aliases={i: i for i in range(n)},
        scratch_shapes=[pltpu.SemaphoreType.DMA((n,)), pltpu.SemaphoreType.DMA((n,))],
        name=name)(*shards)


def _gather_all(blk, *, name, after=None, shares=()):
    rows, cols = blk.shape
    extra = [] if after is None else [after]
    n_s = len(shares)

    def body(x_ref, *refs):
        s_refs = refs[len(extra) + n_s + 1:len(extra) + 2 * n_s + 1]
        out_ref = refs[len(extra) + n_s]
        send_sems, recv_sems, local_sem, s_send, s_recv = refs[len(extra) + 2 * n_s + 1:]
        x, y, c = lax.axis_index("x"), lax.axis_index("y"), lax.axis_index("c")
        me = 4 * x + 2 * y + c
        mine = pltpu.make_async_copy(x_ref, out_ref.at[me], local_sem)
        mine.start()
        cps = []
        for i in range(n_s):
            rh = shares[i].shape[0] // 2
            half = s_refs[i].at[pl.ds(pl.multiple_of(c * rh, 8), rh), :]
            cp = pltpu.make_async_remote_copy(
                src_ref=half, dst_ref=half, send_sem=s_send.at[i], recv_sem=s_recv.at[i],
                device_id=(x, y, 1 - c), device_id_type=MESH)
            cp.start()
            cps.append(cp)
        for k in range(1, N_DEV):
            tx = (1 - x) if (k >> 2) & 1 else x
            ty = (1 - y) if (k >> 1) & 1 else y
            tc = (1 - c) if k & 1 else c
            cp = pltpu.make_async_remote_copy(
                src_ref=x_ref, dst_ref=out_ref.at[me], send_sem=send_sems.at[k - 1],
                recv_sem=recv_sems.at[k - 1], device_id=(tx, ty, tc), device_id_type=MESH)
            cp.start()
            cps.append(cp)
        for k in range(1, N_DEV):
            tx = (1 - x) if (k >> 2) & 1 else x
            ty = (1 - y) if (k >> 1) & 1 else y
            tc = (1 - c) if k & 1 else c
            got = out_ref.at[4 * tx + 2 * ty + tc]
            pltpu.make_async_remote_copy(
                src_ref=got, dst_ref=got, send_sem=send_sems.at[k - 1], recv_sem=recv_sems.at[k - 1],
                device_id=(x, y, c), device_id_type=MESH).wait_recv()
        for i in range(n_s):
            rh = shares[i].shape[0] // 2
            theirs = s_refs[i].at[pl.ds(pl.multiple_of((1 - c) * rh, 8), rh), :]
            pltpu.make_async_remote_copy(
                src_ref=theirs, dst_ref=theirs, send_sem=s_send.at[i], recv_sem=s_recv.at[i],
                device_id=(x, y, c), device_id_type=MESH).wait_recv()
        for cp in cps:
            cp.wait_send()
        mine.wait()

    vm = pl.BlockSpec(memory_space=pltpu.VMEM)
    out = pl.pallas_call(
        body, in_specs=[vm] + [ANY] * (len(extra) + n_s), out_specs=[vm] + [ANY] * n_s,
        out_shape=[jax.ShapeDtypeStruct((N_DEV, rows, cols), blk.dtype)]
        + [jax.ShapeDtypeStruct(s.shape, s.dtype) for s in shares],
        input_output_aliases={1 + len(extra) + i: 1 + i for i in range(n_s)},
        scratch_shapes=[pltpu.SemaphoreType.DMA((N_DEV - 1,)), pltpu.SemaphoreType.DMA((N_DEV - 1,)),
                        pltpu.SemaphoreType.DMA, pltpu.SemaphoreType.DMA((max(n_s, 1),)),
                        pltpu.SemaphoreType.DMA((max(n_s, 1),))],
        name=name)(blk, *extra, *shares)
    return (out[0], *out[1:]) if n_s else out[0]


def _as_rows(a):
    flat = a.reshape(-1)
    n = flat.shape[0]
    rows = -(-n // (8 * LANES)) * 8
    return jnp.pad(flat, (0, rows * LANES - n)).reshape(rows, LANES)


def _from_rows(p, shape):
    n = int(np.prod(shape))
    return p.reshape(-1)[:n].reshape(shape)


WEIGHT_AXES = (1, 1, 1, 0, 1, 0)
WIRE = BF16


def kernel(x, meta_tokens, w_in, w_na_out, w_hg_out, w_o, w_up, w_down, norm_mix, norm_mlp, norm_final, hg_norm, na_rpb, hg_lb_logits, loss_target, m_meta_tokens, m_w_in, m_w_na_out, m_w_hg_out, m_w_o, m_w_up, m_w_down, m_norm_mix, m_norm_mlp, m_norm_final, m_hg_norm, m_na_rpb, m_hg_lb_logits, v_meta_tokens, v_w_in, v_w_na_out, v_w_hg_out, v_w_o, v_w_up, v_w_down, v_norm_mix, v_norm_mlp, v_norm_final, v_hg_norm, v_na_rpb, v_hg_lb_logits):
    xi, yi, ci = lax.axis_index("x"), lax.axis_index("y"), lax.axis_index("c")
    chip = 2 * xi + yi
    d = x.shape[-1]
    dshard = meta_tokens.shape[1]
    hgw = hg_norm.shape[1]
    lbs = hg_lb_logits.shape[2]
    big = [w_in[0], w_na_out[0], w_hg_out[0], w_o[0], w_up[0], w_down[0]]
    big_m = [m_w_in[0], m_w_na_out[0], m_w_hg_out[0], m_w_o[0], m_w_up[0], m_w_down[0]]
    big_v = [v_w_in[0], v_w_na_out[0], v_w_hg_out[0], v_w_o[0], v_w_up[0], v_w_down[0]]

    place = jnp.stack([chip, ci]).astype(jnp.int32)
    in_axes, rest_axes = WEIGHT_AXES[:1], WEIGHT_AXES[1:]
    own_in = _cast_into_full(big[:1], in_axes, place, name="cast_shard_in")
    small_in = jnp.concatenate([_as_rows(meta_tokens), _as_rows(hg_lb_logits)], axis=0)
    sm_send, sm_recv, sm_blk, sm_land, sm_token = _chip_exchange_start(small_in,
                                                                       name="small_params_start")
    in_send, in_recv, in_bufs, in_token = _allgather_start(own_in, in_axes, sm_token,
                                                           name="weight_allgather_in_start")
    own_rest = _cast_into_full(big[1:], rest_axes, place, name="cast_shards_rest")
    ag_send, ag_recv, ag_bufs, ag_token = _allgather_start(own_rest, rest_axes, in_token,
                                                           name="weight_allgather_rest_start")
    sm_land = _chip_exchange_wait(sm_send, sm_recv, sm_blk, sm_land, ag_token,
                                  name="small_params_wait")
    small_all = lax.dynamic_update_slice(sm_land, small_in[None], (chip, 0, 0))
    forward = {}

    def first_weight(after):
        got = _allgather_wait(in_send, in_recv, in_bufs, in_axes, after,
                              name="weight_allgather_in_wait")
        return _allgather_forward(got, in_axes, name="weight_allgather_in_forward")[0]

    def rest_landed(after):
        got = _allgather_wait(ag_send, ag_recv, ag_bufs, rest_axes, after,
                              name="weight_allgather_rest_wait")
        send, recv, bufs, token = _allgather_forward_start(
            got, rest_axes, name="weight_allgather_rest_forward_start")
        forward["rest"] = (send, recv, bufs)
        return token

    def rest_weights(after):
        return _allgather_forward_wait(*forward["rest"], rest_axes, after,
                                       name="weight_allgather_rest_forward_wait")

    n_meta_rows = N_META * dshard // LANES
    meta_full = (small_all[:, :n_meta_rows].reshape(N_CHIPS, N_META, dshard)
                 .transpose(1, 0, 2).reshape(N_META, d))
    lbl_full = (small_all[:, n_meta_rows:].reshape(N_CHIPS, -1)[:, :4 * lbs]
                .reshape(N_CHIPS, 2, 2, lbs).transpose(1, 2, 0, 3).reshape(2, 2, N_CHIPS * lbs))
    lb = jax.nn.softmax(lbl_full, axis=1)[:, 0]

    def by_chip(dws, axes):
        return [g.reshape(1, *g.shape) if ax == 1
                else g.reshape(N_CHIPS, g.shape[0] // N_CHIPS, g.shape[1]) for g, ax in zip(dws, axes)]

    flying = {}

    def scatter(tag, axes, g3, rx):
        parts = _pair_add(g3, rx, place, out_dtype=WIRE, name=f"grad_pair_add_{tag}")
        send, recv, parts, slots, token = _scatter_start(parts, axes,
                                                         name=f"grad_scatter_{tag}_start")
        flying[tag] = (send, recv, parts, slots)
        return token

    def swap(tag, axes):
        def start(dws):
            send, recv, g3, lands, token = _sibling_swap_start(
                by_chip(dws, axes), name=f"grad_sibling_swap_{tag}_start")
            flying["swap_" + tag] = (send, recv, g3, lands)
            return token

        def finish(after):
            g3, rx = _sibling_swap_wait(*flying["swap_" + tag], after,
                                        name=f"grad_sibling_swap_{tag}_wait")
            return scatter(tag, axes, g3, rx)
        return start, finish

    swap_rest, scatter_rest = swap("rest", rest_axes)
    swap_in, scatter_in = swap("in", in_axes)

    def landed(tag, axes, after):
        return _scatter_wait(*flying[tag], axes, after, name=f"grad_scatter_{tag}_wait")

    (loss, dx, dmeta, *_, dg_mix, dg_mlp, dg_fin, d_gain, d_rpb, d_lb) = _local_step(
        x[0], loss_target[0], meta_full, first_weight, rest_weights, norm_mix, norm_mlp,
        norm_final.reshape(1, d), hg_norm, na_rpb[0], lb, swap_rest, scatter_rest,
        lambda dw_in: swap_in([dw_in]), rest_landed, scatter_in)

    parts_rest, slots_rest = landed("rest", rest_axes, dx)
    g_rest = _sibling_share(_sum_chips(slots_rest, parts_rest, place, rest_axes,
                                       name="grad_sum_chips_rest"),
                            name="grad_sibling_share_rest")
    out_rest = _adamw(big[1:], g_rest, big_m[1:], big_v[1:], name="adamw_rest")
    parts_in, slots_in = landed("in", in_axes, out_rest[-1][0])
    half_in = _sum_chips(slots_in, parts_in, place, in_axes, name="grad_sum_chips_in")

    d_rpb_c = d_rpb[:, :2 * NA_WIN_W - 1]
    small_g = [dmeta, dg_mix, dg_mlp, dg_fin, d_gain, d_rpb_c, d_lb, loss]
    packed = jnp.concatenate([_as_rows(a) for a in small_g], axis=0)
    gathered, g_in = _gather_all(packed, shares=half_in, name="gather_small_grads")
    g_big = [g_in] + list(g_rest)
    total = _sum_slots(gathered, name="sum_small_grads")
    offs = np.cumsum([0] + [_as_rows(a).shape[0] for a in small_g])
    take = lambda i, shape: _from_rows(total[offs[i]:offs[i + 1]], shape)
    g_meta_full = take(0, (N_META, d))
    g_norm_mix, g_norm_mlp = take(1, (1, d)), take(2, (1, d))
    g_norm_final = take(3, (d,))
    g_hg_norm = take(4, (1, hgw))
    g_rpb = take(5, na_rpb.shape)
    g_lb = take(6, (2, hgw))
    loss_total = take(7, (1, LANES))[0, 0]
    g_meta = lax.dynamic_slice_in_dim(g_meta_full, chip * dshard, dshard, axis=1)
    dl0 = lb * (1.0 - lb) * g_lb
    g_lbl_full = jnp.stack([dl0, -dl0], axis=1)
    g_lbl = lax.dynamic_slice_in_dim(g_lbl_full, chip * lbs, lbs, axis=2)

    big_out = _adamw(big[:1], [g_in], big_m[:1], big_v[:1], name="adamw_in") + out_rest
    small_w = [meta_tokens, norm_mix, norm_mlp, norm_final, hg_norm, na_rpb, hg_lb_logits]
    small_gr = [g_meta, g_norm_mix, g_norm_mlp, g_norm_final, g_hg_norm, g_rpb, g_lbl]
    small_m = [m_meta_tokens, m_norm_mix, m_norm_mlp, m_norm_final, m_hg_norm, m_na_rpb, m_hg_lb_logits]
    small_v = [v_meta_tokens, v_norm_mix, v_norm_mlp, v_norm_final, v_hg_norm, v_na_rpb, v_hg_lb_logits]
    pk = lambda lst: jnp.concatenate([_as_rows(a) for a in lst], axis=0)
    ((sd, sm, sv),) = _adamw([pk(small_w)], [pk(small_gr)], [pk(small_m)], [pk(small_v)],
                             name="adamw_small")
    soffs = np.cumsum([0] + [_as_rows(a).shape[0] for a in small_w])
    unpk = lambda p: [_from_rows(p[soffs[i]:soffs[i + 1]], small_w[i].shape) for i in range(len(small_w))]
    sd, sm, sv = unpk(sd), unpk(sm), unpk(sv)

    def order(bigs, smalls):
        return [smalls[0]] + [b.reshape(1, *b.shape) for b in bigs] + smalls[1:]

    grads = order(g_big, small_gr)
    deltas = order([o[0] for o in big_out], sd)
    new_m = order([o[1] for o in big_out], sm)
    new_v = order([o[2] for o in big_out], sv)
    return (loss_total, dx.reshape(1, *dx.shape), *grads, *deltas, *new_m, *new_v)
```

```python
import functools

import numpy as np
import jax
import jax.numpy as jnp
from jax import lax
from jax.experimental import pallas as pl
from jax.experimental.pallas import tpu as pltpu

F32 = jnp.float32
BF16 = jnp.bfloat16
HIGHEST = lax.Precision.HIGHEST

GRID_W = 64
N_META = 16
EPS = 1e-6
NA_HEAD_DIM = 64
NA_WIN_H = 8
NA_WIN_W = 16
HG_DK = 128
HG_CHUNK = 16
LANES = 128
ROW_ALIGN = 128
VMEM_LIMIT = 48 * 1024 * 1024

ADAM_LR = 0.001
ADAM_B1 = 0.9
ADAM_B2 = 0.999
ADAM_EPS = 1e-08
ADAM_WD = 0.01
ADAM_STEP = 10

MESH = pl.DeviceIdType.MESH


def _cp(*sem):
    return pltpu.CompilerParams(dimension_semantics=sem, vmem_limit_bytes=VMEM_LIMIT)


def _sigmoid(x):
    return 0.5 * jnp.tanh(0.5 * x) + 0.5


def _dot(a, b, dims, precision=None):
    return lax.dot_general(a, b, (dims, ((), ())), preferred_element_type=F32, precision=precision)


def _nn(a, b, **kw):
    return _dot(a, b, ((1,), (0,)), **kw)


def _nt(a, b, **kw):
    return _dot(a, b, ((1,), (1,)), **kw)


def _tn(a, b, **kw):
    return _dot(a, b, ((0,), (0,)), **kw)


def _matmul(a, b, *, ta=False, tb=False, tm=None, tn=None, tk=None, out_dtype=F32, name,
            precision=None, after=None, epilogue=None, tiles=(), out_dtypes=None):
    extra = [] if after is None else [after]
    single = out_dtypes is None
    if single:
        out_dtypes = (out_dtype,)
    n_t, n_o = len(tiles), len(out_dtypes)
    if ta:
        kdim, m = a.shape
    else:
        m, kdim = a.shape
    if tb:
        n, k2 = b.shape
    else:
        k2, n = b.shape
    assert kdim == k2, (a.shape, b.shape, ta, tb)
    if tm is None:
        if ta:
            tm = next(t for t in (1024, 512, 256, 128, m) if m % t == 0)
        else:
            tm = m // 2 if (m // 2) % 16 == 0 and m > 512 else m
    if tn is None:
        wide = (1024,) if not ta and len(tiles) <= 1 else ()
        tn = next(t for t in (*wide, 512, 256, 128, n) if n % t == 0)
    if tk is None:
        tk = kdim if ta else next(t for t in (2048, 1024, 512, 256, 128, kdim) if kdim % t == 0)
    assert m % tm == 0 and n % tn == 0 and kdim % tk == 0, (m, n, kdim, tm, tn, tk)
    nk = kdim // tk
    op_dtype = F32 if precision is not None else BF16

    def body(a_ref, b_ref, *refs):
        t_refs = refs[:n_t]
        o_refs = refs[n_t + len(extra):n_t + len(extra) + n_o]
        av = a_ref[...].astype(op_dtype)
        bv = b_ref[...].astype(op_dtype)
        dims = ((0 if ta else 1,), (1 if tb else 0,))
        part = _dot(av, bv, dims, precision=precision)

        def finish(acc):
            outs = (acc,) if epilogue is None else epilogue(acc, *[t[...] for t in t_refs])
            for o_ref, val in zip(o_refs, outs):
                o_ref[...] = val.astype(o_ref.dtype)

        if nk == 1:
            finish(part)
            return
        acc_ref = refs[-1]
        kk = pl.program_id(2)

        @pl.when(kk == 0)
        def _():
            acc_ref[...] = part

        @pl.when((kk > 0) & (kk < nk - 1))
        def _():
            acc_ref[...] += part

        @pl.when(kk == nk - 1)
        def _():
            finish(acc_ref[...] + part)

    a_spec = (pl.BlockSpec((tk, tm), lambda i, j, k: (k, i)) if ta
              else pl.BlockSpec((tm, tk), lambda i, j, k: (i, k)))
    b_spec = (pl.BlockSpec((tn, tk), lambda i, j, k: (j, k)) if tb
              else pl.BlockSpec((tk, tn), lambda i, j, k: (k, j)))
    for _, off in tiles:
        assert off % tn == 0, (off, tn)
    t_specs = [pl.BlockSpec((tm, tn), functools.partial(lambda i, j, k, o: (i, o + j), o=off // tn))
               for _, off in tiles]
    o_spec = pl.BlockSpec((tm, tn), lambda i, j, k: (i, j))
    outs = pl.pallas_call(
        body,
        grid=(m // tm, n // tn, nk),
        in_specs=[a_spec, b_spec] + t_specs + [pl.BlockSpec(memory_space=pl.ANY)] * len(extra),
        out_specs=[o_spec] * n_o,
        out_shape=[jax.ShapeDtypeStruct((m, n), dt) for dt in out_dtypes],
        scratch_shapes=[pltpu.VMEM((tm, tn), F32)] if nk > 1 else [],
        compiler_params=_cp("parallel", "parallel", "arbitrary"),
        name=name,
    )(a, b, *[t for t, _ in tiles], *extra)
    return outs[0] if single else outs


def _rspec(tr, w, cb=0):
    return pl.BlockSpec((tr, w), lambda i: (i, cb))


def _fspec(shape):
    nd = len(shape)
    return pl.BlockSpec(shape, lambda i: (0,) * nd)


ROW_VMEM_BUDGET = 20 * 1024 * 1024
ROW_MIN_STEPS = 4


def _row_tile(lp, row_bytes):
    for k in range(ROW_MIN_STEPS, lp // 16 + 1):
        tr = lp // k
        if lp % k == 0 and tr % 16 == 0 and 2 * tr * row_bytes <= ROW_VMEM_BUDGET:
            return tr
    return lp


def _token_rows_copy(i, n_tiles, tr, n_tok, tok_ref, buf_ref, sem, *, to_tokens, start=True,
                     wait=True):
    assert n_tiles >= 2 and 0 < n_tok + N_META - (n_tiles - 1) * tr <= tr

    def run(tok_row, buf_row, count):
        tok = tok_ref.at[pl.ds(tok_row, count), :]
        buf = buf_ref.at[pl.ds(buf_row, count), :]
        cp = pltpu.make_async_copy(buf, tok, sem) if to_tokens else pltpu.make_async_copy(tok, buf, sem)
        if start:
            cp.start()
        if wait:
            cp.wait()

    @pl.when(i == 0)
    def _():
        run(0, N_META, tr - N_META)

    if n_tiles > 2:
        @pl.when((i > 0) & (i < n_tiles - 1))
        def _():
            run(pl.multiple_of(i * tr - N_META, 8), 0, tr)

    @pl.when(i == n_tiles - 1)
    def _():
        run((n_tiles - 1) * tr - N_META, 0, n_tok + N_META - (n_tiles - 1) * tr)


def _embed_norm(x, meta, g, *, lp, name):
    n_tok, d = x.shape
    tr = _row_tile(lp, d * (4 + 2))
    n_tiles = lp // tr

    def body(x_ref, meta_ref, g_ref, h_ref, o_ref, buf_ref, sem):
        i = pl.program_id(0)
        buf_ref[...] = jnp.zeros_like(buf_ref)

        @pl.when(i == 0)
        def _():
            buf_ref[0:N_META, :] = meta_ref[...]

        _token_rows_copy(i, n_tiles, tr, n_tok, x_ref, buf_ref, sem, to_tokens=False)
        xv = buf_ref[...]
        h_ref[...] = xv
        r = lax.rsqrt(jnp.mean(xv * xv, axis=-1, keepdims=True) + EPS)
        o_ref[...] = (xv * r * g_ref[...]).astype(BF16)

    return pl.pallas_call(
        body, grid=(n_tiles,),
        in_specs=[ANY, _fspec((N_META, d)), _fspec((1, d))],
        out_specs=[_rspec(tr, d), _rspec(tr, d)],
        out_shape=[jax.ShapeDtypeStruct((lp, d), F32), jax.ShapeDtypeStruct((lp, d), BF16)],
        scratch_shapes=[pltpu.VMEM((tr, d), F32), pltpu.SemaphoreType.DMA],
        compiler_params=_cp("parallel"), name=name)(x, meta, g)


def _residual_norm(h, t, g, *, name):
    lp, d = h.shape
    tr = _row_tile(lp, d * (4 + 4 + 4 + 2))

    def body(h_ref, t_ref, g_ref, h1_ref, m_ref):
        xv = h_ref[...] + t_ref[...]
        h1_ref[...] = xv
        r = lax.rsqrt(jnp.mean(xv * xv, axis=-1, keepdims=True) + EPS)
        m_ref[...] = (xv * r * g_ref[...]).astype(BF16)

    return pl.pallas_call(
        body, grid=(lp // tr,),
        in_specs=[_rspec(tr, d), _rspec(tr, d), _fspec((1, d))],
        out_specs=[_rspec(tr, d), _rspec(tr, d)],
        out_shape=[jax.ShapeDtypeStruct((lp, d), F32), jax.ShapeDtypeStruct((lp, d), BF16)],
        compiler_params=_cp("parallel"), name=name)(h, t, g)


def _rmsnorm_bwd_add(x, g, dy, dres, *, name):
    lp, d = x.shape
    tr = _row_tile(lp, d * (4 * 4 + 2))

    def body(x_ref, g_ref, dy_ref, dr_ref, dx_ref, dx16_ref, dg_ref):
        @pl.when(pl.program_id(0) == 0)
        def _():
            dg_ref[...] = jnp.zeros_like(dg_ref)

        xv = x_ref[...]
        r = lax.rsqrt(jnp.mean(xv * xv, axis=-1, keepdims=True) + EPS)
        xh = xv * r
        dyv = dy_ref[...]
        dg_ref[...] += jnp.sum(dyv * xh, axis=0, keepdims=True)
        dxh = dyv * g_ref[...]
        dx = dr_ref[...] + r * (dxh - xh * jnp.mean(dxh * xh, axis=-1, keepdims=True))
        dx_ref[...] = dx
        dx16_ref[...] = dx.astype(BF16)

    return pl.pallas_call(
        body, grid=(lp // tr,),
        in_specs=[_rspec(tr, d), _fspec((1, d)), _rspec(tr, d), _rspec(tr, d)],
        out_specs=[_rspec(tr, d), _rspec(tr, d), _fspec((1, d))],
        out_shape=[jax.ShapeDtypeStruct((lp, d), F32), jax.ShapeDtypeStruct((lp, d), BF16),
                   jax.ShapeDtypeStruct((1, d), F32)],
        compiler_params=_cp("arbitrary"), name=name)(x, g, dy, dres)


def _rmsnorm_bwd_tokens(x, g, dy, dres, *, n_tok, name):
    lp, d = x.shape
    tr = _row_tile(lp, d * 4 * 4)
    n_tiles = lp // tr

    def body(x_ref, g_ref, dy_ref, dr_ref, dtok_ref, dmeta_ref, dg_ref, buf_ref, sem):
        i = pl.program_id(0)

        @pl.when(i == 0)
        def _():
            dg_ref[...] = jnp.zeros_like(dg_ref)

        xv = x_ref[...]
        r = lax.rsqrt(jnp.mean(xv * xv, axis=-1, keepdims=True) + EPS)
        xh = xv * r
        dyv = dy_ref[...]
        dg_ref[...] += jnp.sum(dyv * xh, axis=0, keepdims=True)
        dxh = dyv * g_ref[...]
        buf_ref[...] = dr_ref[...] + r * (dxh - xh * jnp.mean(dxh * xh, axis=-1, keepdims=True))

        @pl.when(i == 0)
        def _():
            dmeta_ref[...] = buf_ref[0:N_META, :]

        _token_rows_copy(i, n_tiles, tr, n_tok, dtok_ref, buf_ref, sem, to_tokens=True)

    return pl.pallas_call(
        body, grid=(n_tiles,),
        in_specs=[_rspec(tr, d), _fspec((1, d)), _rspec(tr, d), _rspec(tr, d)],
        out_specs=[ANY, _fspec((N_META, d)), _fspec((1, d))],
        out_shape=[jax.ShapeDtypeStruct((n_tok, d), F32), jax.ShapeDtypeStruct((N_META, d), F32),
                   jax.ShapeDtypeStruct((1, d), F32)],
        scratch_shapes=[pltpu.VMEM((tr, d), F32), pltpu.SemaphoreType.DMA],
        compiler_params=_cp("arbitrary"), name=name)(x, g, dy, dres)


def _final_loss(h1, t2, g, tgt, *, name):
    lp, d = h1.shape
    n_tok = tgt.shape[0]
    tr = _row_tile(lp, d * 4 * 4)
    n_tiles = lp // tr

    def body(h_ref, t_ref, g_ref, tgt_ref, dh_ref, dh16_ref, loss_ref, dg_ref, tg_ref, sem):
        i = pl.program_id(0)

        @pl.when(i == 0)
        def _():
            loss_ref[...] = jnp.zeros_like(loss_ref)
            dg_ref[...] = jnp.zeros_like(dg_ref)
            tg_ref[...] = jnp.zeros_like(tg_ref)

        _token_rows_copy(i, n_tiles, tr, n_tok, tgt_ref, tg_ref, sem, to_tokens=False, wait=False)
        xv = h_ref[...] + t_ref[...]
        r = lax.rsqrt(jnp.mean(xv * xv, axis=-1, keepdims=True) + EPS)
        xh = xv * r
        gv = g_ref[...]
        _token_rows_copy(i, n_tiles, tr, n_tok, tgt_ref, tg_ref, sem, to_tokens=False, start=False)
        row = i * tr + lax.broadcasted_iota(jnp.int32, (tr, 1), 0)
        valid = (row >= N_META) & (row < N_META + n_tok)
        err = jnp.where(valid, xh * gv - tg_ref[...], 0.0)
        loss_ref[...] += jnp.sum(0.5 * err * err) / d
        dy = err / d
        dg_ref[...] += jnp.sum(dy * xh, axis=0, keepdims=True)
        dxh = dy * gv
        dh = r * (dxh - xh * jnp.mean(dxh * xh, axis=-1, keepdims=True))
        dh_ref[...] = dh
        dh16_ref[...] = dh.astype(BF16)

    return pl.pallas_call(
        body, grid=(n_tiles,),
        in_specs=[_rspec(tr, d), _rspec(tr, d), _fspec((1, d)), ANY],
        out_specs=[_rspec(tr, d), _rspec(tr, d), _fspec((1, LANES)), _fspec((1, d))],
        out_shape=[jax.ShapeDtypeStruct((lp, d), F32), jax.ShapeDtypeStruct((lp, d), BF16),
                   jax.ShapeDtypeStruct((1, LANES), F32), jax.ShapeDtypeStruct((1, d), F32)],
        scratch_shapes=[pltpu.VMEM((tr, d), F32), pltpu.SemaphoreType.DMA],
        compiler_params=_cp("arbitrary"), name=name)(h1, t2, g, tgt)


def _hg_out(o_f, o_b, proj, gain, *, col_g, name):
    lp, w = o_f.shape
    tr = _row_tile(lp, w * (3 * 4 + 2))
    hh = w // HG_DK

    def body(of_ref, ob_ref, g_ref, gain_ref, y_ref):
        gv = g_ref[...]
        sg = gv * _sigmoid(gv)
        for h in range(hh):
            sl = slice(h * HG_DK, (h + 1) * HG_DK)
            o = of_ref[:, sl] + ob_ref[:, sl]
            r = lax.rsqrt(jnp.mean(o * o, axis=-1, keepdims=True) + EPS)
            y_ref[:, sl] = (o * r * gain_ref[:, sl] * sg[:, sl]).astype(BF16)

    return pl.pallas_call(
        body, grid=(lp // tr,),
        in_specs=[_rspec(tr, w), _rspec(tr, w), _rspec(tr, w, col_g // w), _fspec((1, w))],
        out_specs=_rspec(tr, w),
        out_shape=jax.ShapeDtypeStruct((lp, w), BF16),
        compiler_params=_cp("parallel"), name=name)(o_f, o_b, proj, gain)


def _hg_out_bwd(o_f, o_b, proj, gain, dy, *, col_g, name):
    lp, w = o_f.shape
    tr = _row_tile(lp, w * (5 * 4 + 2))
    hh = w // HG_DK

    def body(of_ref, ob_ref, g_ref, gain_ref, dy_ref, do_ref, dg_ref, dgain_ref):
        @pl.when(pl.program_id(0) == 0)
        def _():
            dgain_ref[...] = jnp.zeros_like(dgain_ref)

        for h in range(hh):
            sl = slice(h * HG_DK, (h + 1) * HG_DK)
            gv = g_ref[:, sl]
            s = _sigmoid(gv)
            sg = gv * s
            dsg = s + gv * s * (1.0 - s)
            o = of_ref[:, sl] + ob_ref[:, sl]
            r = lax.rsqrt(jnp.mean(o * o, axis=-1, keepdims=True) + EPS)
            on = o * r
            dyv = dy_ref[:, sl]
            gn = gain_ref[:, sl]
            dgain_ref[:, sl] += jnp.sum(dyv * on * sg, axis=0, keepdims=True)
            dg_ref[:, sl] = (dyv * on * gn * dsg).astype(BF16)
            don = dyv * gn * sg
            do_ref[:, sl] = r * (don - on * jnp.mean(don * on, axis=-1, keepdims=True))

    return pl.pallas_call(
        body, grid=(lp // tr,),
        in_specs=[_rspec(tr, w), _rspec(tr, w), _rspec(tr, w, col_g // w), _fspec((1, w)),
                  _rspec(tr, w)],
        out_specs=[_rspec(tr, w), _rspec(tr, w), _fspec((1, w))],
        out_shape=[jax.ShapeDtypeStruct((lp, w), F32), jax.ShapeDtypeStruct((lp, w), BF16),
                   jax.ShapeDtypeStruct((1, w), F32)],
        compiler_params=_cp("arbitrary"), name=name)(o_f, o_b, proj, gain, dy)


HG_ROWS = 128
HG_HALVES = (1, 2, 4, 8, 16, 32, 64)


def _hg_gates(zq, z, lbv):
    sq = _sigmoid(zq)
    s = _sigmoid(z)
    f = lbv + (1.0 - lbv) * s
    kk = (1.0 - lbv) * (1.0 - s)
    return zq * sq, sq, s, f, jnp.log(f), kk


def _block_cumsum(g, pos, suffix):
    x = g
    for k in HG_HALVES:
        if suffix:
            x = x + jnp.where(pos < HG_ROWS - k, pltpu.roll(x, HG_ROWS - k, 0), 0.0)
        else:
            x = x + jnp.where(pos >= k, pltpu.roll(x, k, 0), 0.0)
    return x


def _pair_levels(b, pos, reverse):
    out = []
    first = b
    for m in HG_HALVES:
        if m > 1:
            first = jnp.where((pos & (m - 1)) >= m // 2, pltpu.roll(first, m // 2, 0), first)
        nxt = pltpu.roll(first, HG_ROWS - m, 0)
        upper = (pos & (2 * m - 1)) >= m
        if reverse:
            eq = jnp.where(upper, 0.0, jnp.exp(b - nxt))
            ek = jnp.where(upper, jnp.exp(first - b), 0.0)
        else:
            eq = jnp.where(upper, jnp.exp(b - first), 0.0)
            ek = jnp.where(upper, 0.0, jnp.exp(nxt - b))
        out.append((eq, ek))
    return out


def _pair_masks(mask_ref):
    ri = lax.broadcasted_iota(jnp.int32, (HG_ROWS, HG_ROWS), 0)
    ci = lax.broadcasted_iota(jnp.int32, (HG_ROWS, HG_ROWS), 1)
    for i, m in enumerate(HG_HALVES):
        sh = m.bit_length()
        mask_ref[i] = jnp.where((ri >> sh) == (ci >> sh), 1.0, 0.0)


def _hg_scan_fwd(proj, lb, *, reverse, col_q, col_z, col_i, hh, name):
    lp = proj.shape[0]
    n_blocks = lp // HG_ROWS
    last = 0 if reverse else HG_ROWS - 1

    def body(q_ref, z_ref, i_ref, lb_ref, o_ref, st_ref, mask_ref):
        lbv = lb_ref[...]
        pos = lax.broadcasted_iota(jnp.int32, (HG_ROWS, 1), 0)
        ri = lax.broadcasted_iota(jnp.int32, (HG_ROWS, HG_ROWS), 0)
        ci = lax.broadcasted_iota(jnp.int32, (HG_ROWS, HG_ROWS), 1)
        _pair_masks(mask_ref)

        def block(bi, st):
            bb = (n_blocks - 1 - bi) if reverse else bi
            r0 = pl.multiple_of(bb * HG_ROWS, HG_ROWS)
            v16 = i_ref[pl.ds(r0, HG_ROWS), :].astype(BF16)
            qh, _, _, _, g, kk = _hg_gates(q_ref[pl.ds(r0, HG_ROWS), :],
                                           z_ref[pl.ds(r0, HG_ROWS), :], lbv)
            b = _block_cumsum(g, pos, reverse)
            bl = b[last:last + 1, :]
            qe = (qh * jnp.exp(b)).astype(BF16)
            kd = (kk * jnp.exp(bl - b)).astype(BF16)
            a = jnp.where(ri == ci, jnp.sum(qh * kk, axis=1, keepdims=True), 0.0)
            for i, (eq, ek) in enumerate(_pair_levels(b, pos, reverse)):
                a = a + mask_ref[i] * _nt((qh * eq).astype(BF16), (kk * ek).astype(BF16))
            st_ref[bb] = st
            o_ref[pl.ds(r0, HG_ROWS), :] = _nn(a.astype(BF16), v16) + _nt(qe, st.astype(BF16))
            return jnp.exp(bl) * st + _tn(v16, kd)

        lax.fori_loop(0, n_blocks, block, jnp.zeros((HG_DK, HG_DK), F32))

    cspec = lambda col: pl.BlockSpec((lp, HG_DK), lambda h: (0, col // HG_DK + h))
    return pl.pallas_call(
        body, grid=(hh,),
        in_specs=[cspec(col_q), cspec(col_z), cspec(col_i),
                  pl.BlockSpec((None, 1, HG_DK), lambda h: (h, 0, 0))],
        out_specs=[pl.BlockSpec((lp, HG_DK), lambda h: (0, h)),
                   pl.BlockSpec((None, n_blocks, HG_DK, HG_DK), lambda h: (h, 0, 0, 0))],
        out_shape=[jax.ShapeDtypeStruct((lp, hh * HG_DK), F32),
                   jax.ShapeDtypeStruct((hh, n_blocks, HG_DK, HG_DK), F32)],
        scratch_shapes=[pltpu.VMEM((len(HG_HALVES), HG_ROWS, HG_ROWS), F32)],
        compiler_params=_cp("parallel"), name=name)(proj, proj, proj, lb)


def _hg_scan_bwd(proj, lb, states, do, *, reverse, col_q, col_z, col_i, hh, name):
    lp = proj.shape[0]
    n_blocks = lp // HG_ROWS
    last = 0 if reverse else HG_ROWS - 1

    def body(q_ref, z_ref, i_ref, lb_ref, st_ref, do_ref, dq_ref, dz_ref, dv_ref, dlb_ref, mask_ref):
        lbv = lb_ref[...]
        pos = lax.broadcasted_iota(jnp.int32, (HG_ROWS, 1), 0)
        ri = lax.broadcasted_iota(jnp.int32, (HG_ROWS, HG_ROWS), 0)
        ci = lax.broadcasted_iota(jnp.int32, (HG_ROWS, HG_ROWS), 1)
        _pair_masks(mask_ref)

        def block(bi, carry):
            dst, dlb = carry
            bb = bi if reverse else (n_blocks - 1 - bi)
            r0 = pl.multiple_of(bb * HG_ROWS, HG_ROWS)
            zq = q_ref[pl.ds(r0, HG_ROWS), :]
            v16 = i_ref[pl.ds(r0, HG_ROWS), :].astype(BF16)
            do16 = do_ref[pl.ds(r0, HG_ROWS), :].astype(BF16)
            qh, sq, s, f, g, kk = _hg_gates(zq, z_ref[pl.ds(r0, HG_ROWS), :], lbv)
            b = _block_cumsum(g, pos, reverse)
            bl = b[last:last + 1, :]
            eb = jnp.exp(b)
            ebl = jnp.exp(bl - b)
            decay = jnp.exp(bl)
            qe16 = (qh * eb).astype(BF16)
            kd16 = (kk * ebl).astype(BF16)
            st = st_ref[bb]
            st16, dst16 = st.astype(BF16), dst.astype(BF16)
            same_row = ri == ci
            da = _nt(do16, v16)
            da_diag = jnp.sum(jnp.where(same_row, da, 0.0), axis=1, keepdims=True)
            dq_state = eb * _nn(do16, st16)
            dk_state = ebl * _nn(v16, dst16)
            dq = dq_state + da_diag * kk
            dk = dk_state + da_diag * qh
            dbl = (decay * jnp.sum(st * dst, axis=0, keepdims=True)
                   + jnp.sum(kk * dk_state, axis=0, keepdims=True))
            db = qh * dq_state - kk * dk_state + jnp.where(pos == last, dbl, 0.0)
            a = jnp.where(same_row, jnp.sum(qh * kk, axis=1, keepdims=True), 0.0)
            for i, (eq, ek) in enumerate(_pair_levels(b, pos, reverse)):
                same = mask_ref[i]
                q16, k16 = (qh * eq).astype(BF16), (kk * ek).astype(BF16)
                a = a + same * _nt(q16, k16)
                da16 = (same * da).astype(BF16)
                gq, gk = _nn(da16, k16), _tn(da16, q16)
                dq = dq + eq * gq
                dk = dk + ek * gk
                db = db + (q16.astype(F32) * gq - k16.astype(F32) * gk)
            dg = _block_cumsum(db, pos, not reverse)
            df = dg / f - dk
            dq_ref[pl.ds(r0, HG_ROWS), :] = dq * (sq + zq * sq * (1.0 - sq))
            dz_ref[pl.ds(r0, HG_ROWS), :] = df * (1.0 - lbv) * s * (1.0 - s)
            dv_ref[pl.ds(r0, HG_ROWS), :] = _nt(kd16, dst16) + _tn(a.astype(BF16), do16)
            return (decay * dst + _tn(do16, qe16),
                    dlb + jnp.sum(df * (1.0 - s), axis=0, keepdims=True))

        _, dlb = lax.fori_loop(0, n_blocks, block,
                               (jnp.zeros((HG_DK, HG_DK), F32), jnp.zeros((1, HG_DK), F32)))
        dlb_ref[...] = dlb

    cspec = lambda col: pl.BlockSpec((lp, HG_DK), lambda h: (0, col // HG_DK + h))
    ospec = pl.BlockSpec((lp, HG_DK), lambda h: (0, h))
    sds = jax.ShapeDtypeStruct((lp, hh * HG_DK), F32)
    return pl.pallas_call(
        body, grid=(hh,),
        in_specs=[cspec(col_q), cspec(col_z), cspec(col_i),
                  pl.BlockSpec((None, 1, HG_DK), lambda h: (h, 0, 0)),
                  pl.BlockSpec((None, n_blocks, HG_DK, HG_DK), lambda h: (h, 0, 0, 0)),
                  ospec],
        out_specs=[ospec, ospec, ospec, pl.BlockSpec((None, 1, HG_DK), lambda h: (h, 0, 0))],
        out_shape=[sds, sds, sds, jax.ShapeDtypeStruct((hh, 1, HG_DK), F32)],
        scratch_shapes=[pltpu.VMEM((len(HG_HALVES), HG_ROWS, HG_ROWS), F32)],
        compiler_params=_cp("parallel"), name=name)(proj, proj, proj, lb, states, do)


NA_HB = LANES // NA_HEAD_DIM
NA_G = 4
NA_U = NA_G + NA_WIN_H
NA_QN = NA_G * GRID_W
NA_KN = NA_U * GRID_W


def _na_table_index(pattern, a, j):
    if pattern == 0:
        return j - a + NA_WIN_H - 1 if j < NA_WIN_H else None
    if pattern == 2:
        return j - a - 1 if j >= NA_U - NA_WIN_H else None
    return j - a + NA_WIN_H // 2 - 1 if a <= j < a + NA_WIN_H else None


def _na_step_rows(pattern, t, rows):
    if pattern == 0:
        r0, us = 0, 0
    elif pattern == 2:
        r0, us = rows - NA_G, rows - NA_U
    else:
        r0 = NA_G * t
        us = r0 - NA_WIN_H // 2
    q0, k0 = N_META + GRID_W * r0, N_META + GRID_W * us
    if pattern == 1:
        q0, k0 = pl.multiple_of(q0, 16), pl.multiple_of(k0, 16)
    return q0, k0


def _na_fill_bias(tb_ref, bias_ref):
    neg = jnp.full((GRID_W, GRID_W), -1e30, F32)
    for h in range(NA_HB):
        for pattern in range(3):
            for a in range(NA_G):
                for j in range(NA_U):
                    idx = _na_table_index(pattern, a, j)
                    bias_ref[h, pattern, a * GRID_W:(a + 1) * GRID_W, j * GRID_W:(j + 1) * GRID_W] = (
                        neg if idx is None else tb_ref[h, idx])


def _na_steps(rows, step, carry):
    n_steps = rows // NA_G
    carry = step(0, 0, carry)
    carry = lax.fori_loop(1, n_steps - 1, functools.partial(step, 1), carry)
    return step(2, n_steps - 1, carry)


def _na_head_lanes():
    lane = lax.broadcasted_iota(jnp.int32, (1, LANES), 1)
    return [lane // NA_HEAD_DIM == h for h in range(NA_HB)]


def _na_only(mask, x):
    return jnp.where(mask, x, jnp.zeros_like(x))


def _na_fwd(proj, tb, *, n_tok, nh, name):
    lp = proj.shape[0]
    dh, hb = NA_HEAD_DIM, NA_HB
    naw = nh * dh
    rows = n_tok // GRID_W
    scale = dh ** -0.5

    def body(q_ref, k_ref, v_ref, tb_ref, o_ref, lse_ref, q16_ref, k16_ref, v16_ref, bias_ref):
        o_ref[...] = jnp.zeros_like(o_ref)
        lse_ref[...] = jnp.zeros_like(lse_ref)
        q16_ref[...] = q_ref[...].astype(BF16)
        k16_ref[...] = k_ref[...].astype(BF16)
        v16_ref[...] = v_ref[...].astype(BF16)
        _na_fill_bias(tb_ref, bias_ref)
        heads = _na_head_lanes()
        km = k16_ref[0:N_META, :]
        vm = v16_ref[0:N_META, :]
        qm = q16_ref[0:N_META, :]
        o_m = None
        for h in range(hb):
            s = _nt(_na_only(heads[h], qm), km) * scale
            m = jnp.max(s, axis=1, keepdims=True)
            p = jnp.exp(s - m)
            l = jnp.sum(p, axis=1, keepdims=True)
            o_h = _nn(p.astype(BF16), vm) / l
            o_m = o_h if o_m is None else jnp.where(heads[h], o_h, o_m)
            lse_ref[h, 0:N_META, :] = m + jnp.log(l)
        o_ref[0:N_META, :] = o_m

        def step(pattern, t, carry):
            q0, k0 = _na_step_rows(pattern, t, rows)
            q16 = q16_ref[pl.ds(q0, NA_QN), :]
            k16 = k16_ref[pl.ds(k0, NA_KN), :]
            v16 = v16_ref[pl.ds(k0, NA_KN), :]
            o = None
            for h in range(hb):
                q_h = _na_only(heads[h], q16)
                s = _nt(q_h, k16) * scale + bias_ref[h, pattern]
                sm = _nt(q_h, km) * scale
                m = jnp.maximum(jnp.max(s, axis=1, keepdims=True),
                                jnp.max(sm, axis=1, keepdims=True))
                p = jnp.exp(s - m)
                pm = jnp.exp(sm - m)
                l = jnp.sum(p, axis=1, keepdims=True) + jnp.sum(pm, axis=1, keepdims=True)
                o_h = (_nn(p.astype(BF16), v16) + _nn(pm.astype(BF16), vm)) / l
                o = o_h if o is None else jnp.where(heads[h], o_h, o)
                lse_ref[h, pl.ds(q0, NA_QN), :] = m + jnp.log(l)
            o_ref[pl.ds(q0, NA_QN), :] = o
            return carry

        _na_steps(rows, step, 0)

    cblk = lambda col: pl.BlockSpec((lp, LANES), lambda g: (0, col // LANES + g))
    return pl.pallas_call(
        body, grid=(nh // hb,),
        in_specs=[cblk(0), cblk(naw), cblk(2 * naw),
                  pl.BlockSpec((hb, 2 * NA_WIN_H - 1, GRID_W, GRID_W), lambda g: (g, 0, 0, 0))],
        out_specs=[cblk(0), pl.BlockSpec((hb, lp, 1), lambda g: (g, 0, 0))],
        out_shape=[jax.ShapeDtypeStruct((lp, naw), F32), jax.ShapeDtypeStruct((nh, lp, 1), F32)],
        scratch_shapes=[pltpu.VMEM((lp, LANES), BF16)] * 3 + [pltpu.VMEM((hb, 3, NA_QN, NA_KN), F32)],
        compiler_params=_cp("parallel"), name=name)(proj, proj, proj, tb)


def _na_bwd(proj, tb, o, lse, do, *, n_tok, nh, name):
    lp = proj.shape[0]
    dh, hb = NA_HEAD_DIM, NA_HB
    naw = nh * dh
    rows = n_tok // GRID_W
    scale = dh ** -0.5

    def body(q_ref, k_ref, v_ref, tb_ref, o_ref, lse_ref, do_ref, dq_ref, dk_ref, dv_ref, dtb_ref,
             q16_ref, k16_ref, v16_ref, bias_ref):
        dq_ref[...] = jnp.zeros_like(dq_ref)
        dk_ref[...] = jnp.zeros_like(dk_ref)
        dv_ref[...] = jnp.zeros_like(dv_ref)
        dtb_ref[...] = jnp.zeros_like(dtb_ref)
        q16_ref[...] = q_ref[...].astype(BF16)
        k16_ref[...] = k_ref[...].astype(BF16)
        v16_ref[...] = v_ref[...].astype(BF16)
        _na_fill_bias(tb_ref, bias_ref)
        heads = _na_head_lanes()
        km = k16_ref[0:N_META, :]
        vm = v16_ref[0:N_META, :]
        qm = q16_ref[0:N_META, :]
        dom = do_ref[0:N_META, :]
        prod = dom * o_ref[0:N_META, :]
        dq_m = None
        dkm0 = jnp.zeros((N_META, LANES), F32)
        dvm0 = jnp.zeros((N_META, LANES), F32)
        for h in range(hb):
            q_h = _na_only(heads[h], qm)
            do_h = _na_only(heads[h], dom).astype(BF16)
            p = jnp.exp(_nt(q_h, km) * scale - lse_ref[h, 0:N_META, :])
            delta = jnp.sum(_na_only(heads[h], prod), axis=1, keepdims=True)
            ds = (p * (_nt(do_h, vm) - delta)).astype(BF16)
            dq_h = _nn(ds, km) * scale
            dq_m = dq_h if dq_m is None else jnp.where(heads[h], dq_h, dq_m)
            dkm0 = dkm0 + _tn(ds, q_h) * scale
            dvm0 = dvm0 + _tn(p.astype(BF16), do_h)
        dq_ref[0:N_META, :] = dq_m

        def step(pattern, t, carry):
            dkm, dvm = carry
            q0, k0 = _na_step_rows(pattern, t, rows)
            q16 = q16_ref[pl.ds(q0, NA_QN), :]
            k16 = k16_ref[pl.ds(k0, NA_KN), :]
            v16 = v16_ref[pl.ds(k0, NA_KN), :]
            dov = do_ref[pl.ds(q0, NA_QN), :]
            prod = dov * o_ref[pl.ds(q0, NA_QN), :]
            dq = None
            dk = jnp.zeros((NA_KN, LANES), F32)
            dv = jnp.zeros((NA_KN, LANES), F32)
            for h in range(hb):
                q_h = _na_only(heads[h], q16)
                do_h = _na_only(heads[h], dov).astype(BF16)
                lse = lse_ref[h, pl.ds(q0, NA_QN), :]
                p = jnp.exp(_nt(q_h, k16) * scale + bias_ref[h, pattern] - lse)
                pm = jnp.exp(_nt(q_h, km) * scale - lse)
                delta = jnp.sum(_na_only(heads[h], prod), axis=1, keepdims=True)
                ds = p * (_nt(do_h, v16) - delta)
                dsm = (pm * (_nt(do_h, vm) - delta)).astype(BF16)
                ds16 = ds.astype(BF16)
                dq_h = (_nn(ds16, k16) + _nn(dsm, km)) * scale
                dq = dq_h if dq is None else jnp.where(heads[h], dq_h, dq)
                dk = dk + _tn(ds16, q_h) * scale
                dv = dv + _tn(p.astype(BF16), do_h)
                dkm = dkm + _tn(dsm, q_h) * scale
                dvm = dvm + _tn(pm.astype(BF16), do_h)
                for a in range(NA_G):
                    for j in range(NA_U):
                        idx = _na_table_index(pattern, a, j)
                        if idx is not None:
                            dtb_ref[h, idx] += ds[a * GRID_W:(a + 1) * GRID_W,
                                                  j * GRID_W:(j + 1) * GRID_W]
            dq_ref[pl.ds(q0, NA_QN), :] = dq
            dk_ref[pl.ds(k0, NA_KN), :] += dk
            dv_ref[pl.ds(k0, NA_KN), :] += dv
            return dkm, dvm

        dkm, dvm = _na_steps(rows, step, (dkm0, dvm0))
        dk_ref[0:N_META, :] += dkm
        dv_ref[0:N_META, :] += dvm

    cblk = lambda col: pl.BlockSpec((lp, LANES), lambda g: (0, col // LANES + g))
    tbs = pl.BlockSpec((hb, 2 * NA_WIN_H - 1, GRID_W, GRID_W), lambda g: (g, 0, 0, 0))
    sds = jax.ShapeDtypeStruct((lp, naw), F32)
    return pl.pallas_call(
        body, grid=(nh // hb,),
        in_specs=[cblk(0), cblk(naw), cblk(2 * naw), tbs, cblk(0),
                  pl.BlockSpec((hb, lp, 1), lambda g: (g, 0, 0)), cblk(0)],
        out_specs=[cblk(0), cblk(0), cblk(0), tbs],
        out_shape=[sds, sds, sds, jax.ShapeDtypeStruct(tb.shape, F32)],
        scratch_shapes=[pltpu.VMEM((lp, LANES), BF16)] * 3 + [pltpu.VMEM((hb, 3, NA_QN, NA_KN), F32)],
        compiler_params=_cp("parallel"), name=name)(proj, proj, proj, tb, o, lse, do)


def _rpb_onehot():
    c = np.arange(GRID_W)[:, None]
    w = np.arange(GRID_W)[None, :]
    cs = np.clip(c - NA_WIN_W // 2, 0, GRID_W - NA_WIN_W)
    in_win = (w >= cs) & (w < cs + NA_WIN_W)
    dc = np.clip(w - c, -(NA_WIN_W - 1), NA_WIN_W - 1) + NA_WIN_W - 1
    oh = np.zeros((LANES, GRID_W * GRID_W), np.float32)
    flat = np.arange(GRID_W * GRID_W).reshape(GRID_W, GRID_W)
    oh[dc[in_win], flat[in_win]] = 1.0
    neg = np.where(in_win, 0.0, -1e30).astype(np.float32).reshape(1, -1)
    return oh, neg


def _assemble_dproj(dq_na, dk_na, dv_na, dq_f, dq_b, dz_f, dz_b, dv_f, dv_b, dg, dgn, dgh, *, name):
    lp, naw = dq_na.shape
    hgw = dq_f.shape[1]
    d = dgn.shape[1]
    cols = 3 * naw + 5 * hgw + 2 * d
    tr = _row_tile(lp, 3 * naw * 4 + 6 * hgw * 4 + hgw * 2 + 2 * d * 2 + cols * 2)

    def body(nq_ref, nk_ref, nv_ref, qf_ref, qb_ref, zf_ref, zb_ref, vf_ref, vb_ref, g_ref, gn_ref,
             gh_ref, o_ref):
        o_ref[:, 0:naw] = nq_ref[...].astype(BF16)
        o_ref[:, naw:2 * naw] = nk_ref[...].astype(BF16)
        o_ref[:, 2 * naw:3 * naw] = nv_ref[...].astype(BF16)
        c0 = 3 * naw
        o_ref[:, c0:c0 + hgw] = (qf_ref[...] + qb_ref[...]).astype(BF16)
        o_ref[:, c0 + hgw:c0 + 2 * hgw] = zf_ref[...].astype(BF16)
        o_ref[:, c0 + 2 * hgw:c0 + 3 * hgw] = zb_ref[...].astype(BF16)
        o_ref[:, c0 + 3 * hgw:c0 + 4 * hgw] = (vf_ref[...] + vb_ref[...]).astype(BF16)
        o_ref[:, c0 + 4 * hgw:c0 + 5 * hgw] = g_ref[...]
        o_ref[:, c0 + 5 * hgw:c0 + 5 * hgw + d] = gn_ref[...]
        o_ref[:, c0 + 5 * hgw + d:] = gh_ref[...]

    hg, na = _rspec(tr, hgw), _rspec(tr, naw)
    return pl.pallas_call(
        body, grid=(lp // tr,),
        in_specs=[na, na, na, hg, hg, hg, hg, hg, hg, hg, _rspec(tr, d), _rspec(tr, d)],
        out_specs=_rspec(tr, cols),
        out_shape=jax.ShapeDtypeStruct((lp, cols), BF16),
        compiler_params=_cp("parallel"), name=name)(dq_na, dk_na, dv_na, dq_f, dq_b, dz_f, dz_b,
                                                    dv_f, dv_b, dg, dgn, dgh)


GROUP_STEPS = 8


def _group_tiles(rows, align):
    steps = GROUP_STEPS if all(r % (GROUP_STEPS * align) == 0 for r in rows) else 1
    return steps, [r // steps for r in rows]


def _adamw(ws, gs, ms, vs, *, name):
    n = len(ws)
    steps, trs = _group_tiles([w.shape[0] for w in ws], 8)

    def body(*refs):
        for i in range(n):
            w_ref, g_ref, m_ref, v_ref = refs[4 * i:4 * i + 4]
            d_ref, mo_ref, vo_ref = refs[4 * n + 3 * i:4 * n + 3 * i + 3]
            gv = g_ref[...]
            mn = ADAM_B1 * m_ref[...] + (1.0 - ADAM_B1) * gv
            vn = ADAM_B2 * v_ref[...] + (1.0 - ADAM_B2) * (gv * gv)
            m_hat = mn / (1.0 - ADAM_B1 ** ADAM_STEP)
            v_hat = vn / (1.0 - ADAM_B2 ** ADAM_STEP)
            d_ref[...] = -ADAM_LR * (m_hat / (jnp.sqrt(v_hat) + ADAM_EPS) + ADAM_WD * w_ref[...])
            mo_ref[...] = mn
            vo_ref[...] = vn

    specs = [_rspec(tr, w.shape[1]) for tr, w in zip(trs, ws)]
    out = pl.pallas_call(
        body, grid=(steps,),
        in_specs=[s for s in specs for _ in range(4)],
        out_specs=[s for s in specs for _ in range(3)],
        out_shape=[jax.ShapeDtypeStruct(w.shape, F32) for w in ws for _ in range(3)],
        compiler_params=_cp("parallel"), name=name)(*[a for q in zip(ws, gs, ms, vs) for a in q])
    return [tuple(out[3 * i:3 * i + 3]) for i in range(n)]


def _local_step(x, tgt, meta, first_weight, rest_weights, g_mix, g_mlp, g_fin, hg_gain, rpb, lb,
                early_grads=None, mid_grads=None, late_grad=None, rest_landed=None,
                last_grads=None):
    n_tok, d = x.shape
    hgw = hg_gain.shape[1]
    nh, hh = rpb.shape[0], hgw // HG_DK
    naw = nh * NA_HEAD_DIM
    l_real = N_META + n_tok
    lp = -(-l_real // ROW_ALIGN) * ROW_ALIGN
    col_qhg = 3 * naw
    col_zf, col_zb, col_i, col_g = (col_qhg + hgw, col_qhg + 2 * hgw, col_qhg + 3 * hgw,
                                    col_qhg + 4 * hgw)
    col_gate = col_qhg + 5 * hgw

    oh_np, neg_np = _rpb_onehot()
    oh = jnp.asarray(oh_np)
    rpb_p = jnp.pad(rpb.reshape(nh * (2 * NA_WIN_H - 1), 2 * NA_WIN_W - 1),
                    ((0, 0), (0, LANES - (2 * NA_WIN_W - 1))))
    tb = _matmul(rpb_p, oh, tm=rpb_p.shape[0], tn=512, tk=LANES, precision=HIGHEST,
                 name="rpb_expand")
    tb = (tb + jnp.asarray(neg_np)).reshape(nh, 2 * NA_WIN_H - 1, GRID_W, GRID_W)

    h0, a = _embed_norm(x, meta, g_mix, lp=lp, name="norm_mix")
    w_in = first_weight((a, tb))
    proj = _matmul(a, w_in, name="mm_in")
    o_na, lse = _na_fwd(proj, tb, n_tok=n_tok, nh=nh, name="na_fwd")
    lb_f = lb[0].reshape(hh, 1, HG_DK)
    lb_b = lb[1].reshape(hh, 1, HG_DK)
    scan_kw = dict(col_q=col_qhg, col_i=col_i, hh=hh)
    o_f, st_f = _hg_scan_fwd(proj, lb_f, reverse=False, col_z=col_zf, name="hg_scan_f", **scan_kw)
    token = rest_landed(o_f) if rest_landed else None
    lb_b_late = lb_b if token is None else lb_b + token[0:1, 0:1]
    o_b, st_b = _hg_scan_fwd(proj, lb_b_late, reverse=True, col_z=col_zb, name="hg_scan_b",
                             **scan_kw)
    o_hg = _hg_out(o_f, o_b, proj, hg_gain, col_g=col_g, name="hg_out")
    w_na, w_hg, w_o, w_up, w_down = rest_weights(o_hg)
    y_na = _matmul(o_na, w_na, name="mm_na_out", out_dtype=BF16)
    gates = ((proj, col_gate), (proj, col_gate + d))

    def mix_gates(acc, gn, gh, yn):
        return acc, _sigmoid(gn) * yn + _sigmoid(gh) * acc

    def mix_gates_bwd(dmix, gn, gh, yn, yh):
        sn, sh = _sigmoid(gn), _sigmoid(gh)
        return dmix * sn, dmix * sh, dmix * yn * sn * (1.0 - sn), dmix * yh * sh * (1.0 - sh)

    y_hg, mix = _matmul(o_hg, w_hg, name="mm_hg_out", epilogue=mix_gates,
                        tiles=(*gates, (y_na, 0)), out_dtypes=(BF16, BF16))
    t1 = _matmul(mix, w_o, name="mm_o")
    h1, mlp_in = _residual_norm(h0, t1, g_mlp, name="resid_norm_mlp")
    u, act = _matmul(mlp_in, w_up, name="mm_up", out_dtypes=(BF16, BF16),
                     epilogue=lambda acc: (acc, jnp.square(jnp.maximum(acc, 0.0))))
    t2 = _matmul(act, w_down, name="mm_down")
    dh2, dh2_16, loss, dg_fin = _final_loss(h1, t2, g_fin, tgt, name="final_loss")

    (du,) = _matmul(dh2_16, w_down, tb=True, name="mm_down_dx", tiles=((u, 0),),
                    out_dtypes=(BF16,),
                    epilogue=lambda acc, uv: (acc * 2.0 * jnp.maximum(uv, 0.0),))
    dw_down = _matmul(act, dh2_16, ta=True, name="mm_down_dw")
    dm = _matmul(du, w_up, tb=True, name="mm_up_dx")
    dw_up = _matmul(mlp_in, du, ta=True, name="mm_up_dw")
    dh1, dh1_16, dg_mlp = _rmsnorm_bwd_add(h1, g_mlp, dm, dh2, name="norm_mlp_bwd")
    dy_na, dy_hg, dgn, dgh = _matmul(dh1_16, w_o, tb=True, name="mm_o_dx", epilogue=mix_gates_bwd,
                                     tiles=(*gates, (y_na, 0), (y_hg, 0)), out_dtypes=(BF16,) * 4)
    dw_o = _matmul(mix, dh1_16, ta=True, name="mm_o_dw")
    do_na = _matmul(dy_na, w_na, tb=True, name="mm_na_out_dx")
    dw_na = _matmul(o_na, dy_na, ta=True, name="mm_na_out_dw")
    do_hg = _matmul(dy_hg, w_hg, tb=True, name="mm_hg_out_dx")
    dw_hg = _matmul(o_hg, dy_hg, ta=True, name="mm_hg_out_dw")
    token = early_grads([dw_na, dw_hg, dw_o, dw_up, dw_down]) if early_grads else None
    if token is not None:
        hg_gain = hg_gain + token[0:1, 0:1]
    d_o, dg_hg, d_gain = _hg_out_bwd(o_f, o_b, proj, hg_gain, do_hg, col_g=col_g, name="hg_out_bwd")
    dq_f, dz_f, dv_f, dlb_f = _hg_scan_bwd(proj, lb_f, st_f, d_o, reverse=False, col_z=col_zf,
                                           name="hg_scan_f_bwd", **scan_kw)
    token = mid_grads(dq_f) if mid_grads else None
    lb_b_late = lb_b if token is None else lb_b + token[0:1, 0:1]
    dq_b, dz_b, dv_b, dlb_b = _hg_scan_bwd(proj, lb_b_late, st_b, d_o, reverse=True, col_z=col_zb,
                                           name="hg_scan_b_bwd", **scan_kw)
    dq_na, dk_na, dv_na, dtb = _na_bwd(proj, tb, o_na, lse, do_na, n_tok=n_tok, nh=nh, name="na_bwd")
    dproj = _assemble_dproj(dq_na, dk_na, dv_na, dq_f, dq_b, dz_f, dz_b, dv_f, dv_b, dg_hg, dgn,
                            dgh, name="assemble_dproj")
    dw_in = _matmul(a, dproj, ta=True, name="mm_in_dw")
    token = late_grad(dw_in) if late_grad else None
    da = _matmul(dproj, w_in, tb=True, name="mm_in_dx", after=token)
    token = last_grads(da) if last_grads else None
    g_mix_late = g_mix if token is None else g_mix + token[0:1, 0:1]
    dx, dmeta, dg_mix = _rmsnorm_bwd_tokens(h0, g_mix_late, da, dh1, n_tok=n_tok,
                                            name="norm_mix_bwd")
    d_rpb = _matmul(dtb.reshape(nh * (2 * NA_WIN_H - 1), GRID_W * GRID_W), oh, tb=True,
                    tm=nh * (2 * NA_WIN_H - 1), tn=LANES, tk=1024, precision=HIGHEST,
                    name="rpb_reduce")
    d_lb = jnp.concatenate([dlb_f.reshape(1, hgw), dlb_b.reshape(1, hgw)], axis=0)
    return (loss, dx, dmeta, dw_in, dw_na, dw_hg, dw_o, dw_up, dw_down,
            dg_mix, dg_mlp, dg_fin, d_gain, d_rpb, d_lb)


N_CHIPS = 4
N_DEV = 8
ANY = pl.BlockSpec(memory_space=pl.ANY)


def _place():
    x, y, c = lax.axis_index("x"), lax.axis_index("y"), lax.axis_index("c")
    others = []
    for j in (1, 2, 3):
        tx = (1 - x) if (j >> 1) else x
        ty = (1 - y) if (j & 1) else y
        others.append((tx, ty))
    return x, y, c, others


def _piece(ref, axis, k, half, rh, cs):
    if axis == 1:
        return ref.at[pl.ds(pl.multiple_of(half * rh, 16), rh), pl.ds(pl.multiple_of(k * cs, LANES), cs)]
    return ref.at[pl.ds(pl.multiple_of(k * 2 * rh + half * rh, 16), rh), :]


def _cast_into_full(shards, axes, place, *, name):
    n = len(shards)
    steps, trs = _group_tiles([s.shape[0] for s in shards], 16)

    def body(p_ref, *refs):
        for i in range(n):
            refs[n + i][...] = refs[i][...].astype(BF16)

    def out_spec(tr, cs, axis):
        if axis == 1:
            return pl.BlockSpec((tr, cs), lambda i, p_ref: (i, p_ref[0]))
        return pl.BlockSpec((tr, cs), lambda i, p_ref: (p_ref[0] * steps + i, 0))

    return pl.pallas_call(
        body,
        grid_spec=pltpu.PrefetchScalarGridSpec(
            num_scalar_prefetch=1, grid=(steps,),
            in_specs=[pl.BlockSpec((tr, s.shape[1]), lambda i, p_ref: (i, 0))
                      for tr, s in zip(trs, shards)],
            out_specs=[out_spec(tr, s.shape[1], ax) for tr, s, ax in zip(trs, shards, axes)]),
        out_shape=[jax.ShapeDtypeStruct((s.shape[0], s.shape[1] * N_CHIPS) if ax == 1
                                        else (s.shape[0] * N_CHIPS, s.shape[1]), BF16)
                   for s, ax in zip(shards, axes)],
        compiler_params=_cp("parallel"), name=name)(place, *shards)


HBM_SPEC = pl.BlockSpec(memory_space=pltpu.HBM)
SEM_SPEC = pl.BlockSpec(memory_space=pltpu.SEMAPHORE)
SPLIT_COPY = pltpu.CompilerParams(has_side_effects=pltpu.SideEffectType.DATAFLOW_SIDE_EFFECTING)
TOKEN = jax.ShapeDtypeStruct((8, LANES), F32)


def _geo(fulls, axes):
    out = []
    for f, ax in zip(fulls, axes):
        r, cs = (f.shape[0], f.shape[1] // N_CHIPS) if ax == 1 else (f.shape[0] // N_CHIPS, f.shape[1])
        out.append((ax, r // 2, cs))
    return out


def _gather_copies(refs, geo, send_sems, recv_sems):
    x, y, c, others = _place()
    chip = 2 * x + y
    cps = []
    for i, (ax, rh, cs) in enumerate(geo):
        mine = _piece(refs[i], ax, chip, c, rh, cs)
        for j, (tx, ty) in enumerate(others):
            cps.append(pltpu.make_async_remote_copy(
                src_ref=mine, dst_ref=mine, send_sem=send_sems.at[3 * i + j],
                recv_sem=recv_sems.at[3 * i + j], device_id=(tx, ty, c), device_id_type=MESH))
    return cps


def _allgather_start(fulls, axes, after, *, name):
    n = len(fulls)
    geo = _geo(fulls, axes)

    def body(*refs):
        w_refs = refs[:n]
        send_sems, recv_sems = refs[n + 1], refs[n + 2]
        token = refs[2 * n + 3]
        for cp in _gather_copies(w_refs, geo, send_sems, recv_sems):
            cp.start()
        token[...] = jnp.zeros_like(token)

    out = pl.pallas_call(
        body, name=name,
        out_shape=(pltpu.SemaphoreType.DMA((3 * n,)), pltpu.SemaphoreType.DMA((3 * n,)),
                   *[pltpu.HBM(f.shape, f.dtype) for f in fulls], TOKEN),
        in_specs=[HBM_SPEC] * n + [ANY],
        out_specs=(SEM_SPEC, SEM_SPEC, *[HBM_SPEC] * n, pl.BlockSpec(memory_space=pltpu.VMEM)),
        input_output_aliases={i: 2 + i for i in range(n)},
        compiler_params=SPLIT_COPY,
    )(*[pltpu.with_memory_space_constraint(f, pltpu.HBM) for f in fulls], after)
    return out[0], out[1], list(out[2:2 + n]), out[2 + n]


def _allgather_wait(send_sems, recv_sems, fulls, axes, after, *, name):
    n = len(fulls)
    geo = _geo(fulls, axes)
    afters = tuple(after) if isinstance(after, (tuple, list)) else (after,)

    def body(*refs):
        w_refs = refs[:n]
        for cp in _gather_copies(w_refs, geo, refs[n], refs[n + 1]):
            cp.wait_send()
            cp.wait_recv()

    return list(pl.pallas_call(
        body, name=name,
        out_shape=[pltpu.HBM(f.shape, f.dtype) for f in fulls],
        in_specs=[HBM_SPEC] * n + [SEM_SPEC, SEM_SPEC] + [ANY] * len(afters),
        out_specs=[HBM_SPEC] * n,
        input_output_aliases={i: i for i in range(n)},
        compiler_params=SPLIT_COPY,
    )(*fulls, send_sems, recv_sems, *afters))


def _allgather_forward(fulls, axes, *, name):
    n = len(fulls)
    geo = _geo(fulls, axes)

    def body(*refs):
        o_refs = refs[n:2 * n]
        send_sems, recv_sems = refs[2 * n:]
        x, y, c, others = _place()

        def rcopy(i, j, half, to):
            ax, rh, cs = geo[i]
            ref = _piece(o_refs[i], ax, 2 * others[j][0] + others[j][1], half, rh, cs)
            return pltpu.make_async_remote_copy(
                src_ref=ref, dst_ref=ref, send_sem=send_sems.at[3 * i + j],
                recv_sem=recv_sems.at[3 * i + j], device_id=to, device_id_type=MESH)

        cps = [rcopy(i, j, c, (x, y, 1 - c)) for i in range(n) for j in range(3)]
        for cp in cps:
            cp.start()
        for i in range(n):
            for j in range(3):
                rcopy(i, j, 1 - c, (x, y, c)).wait_recv()
        for cp in cps:
            cp.wait_send()

    return list(pl.pallas_call(
        body, in_specs=[ANY] * n, out_specs=[ANY] * n,
        out_shape=[jax.ShapeDtypeStruct(f.shape, f.dtype) for f in fulls],
        input_output_aliases={i: i for i in range(n)},
        scratch_shapes=[pltpu.SemaphoreType.DMA((3 * n,)), pltpu.SemaphoreType.DMA((3 * n,))],
        name=name)(*fulls))


def _forward_copies(refs, geo, send_sems, recv_sems):
    x, y, c, others = _place()
    cps = []
    for i, (ax, rh, cs) in enumerate(geo):
        for j, (tx, ty) in enumerate(others):
            ref = _piece(refs[i], ax, 2 * tx + ty, c, rh, cs)
            cps.append(pltpu.make_async_remote_copy(
                src_ref=ref, dst_ref=ref, send_sem=send_sems.at[3 * i + j],
                recv_sem=recv_sems.at[3 * i + j], device_id=(x, y, 1 - c), device_id_type=MESH))
    return cps


def _allgather_forward_start(fulls, axes, *, name):
    n = len(fulls)
    geo = _geo(fulls, axes)

    def body(*refs):
        token = refs[2 * n + 2]
        for cp in _forward_copies(refs[:n], geo, refs[n], refs[n + 1]):
            cp.start()
        token[...] = jnp.zeros_like(token)

    out = pl.pallas_call(
        body, name=name,
        out_shape=(pltpu.SemaphoreType.DMA((3 * n,)), pltpu.SemaphoreType.DMA((3 * n,)),
                   *[pltpu.HBM(f.shape, f.dtype) for f in fulls], TOKEN),
        in_specs=[HBM_SPEC] * n,
        out_specs=(SEM_SPEC, SEM_SPEC, *[HBM_SPEC] * n, pl.BlockSpec(memory_space=pltpu.VMEM)),
        input_output_aliases={i: 2 + i for i in range(n)},
        compiler_params=SPLIT_COPY,
    )(*fulls)
    return out[0], out[1], list(out[2:2 + n]), out[2 + n]


def _allgather_forward_wait(send_sems, recv_sems, fulls, axes, after, *, name):
    n = len(fulls)
    geo = _geo(fulls, axes)

    def body(*refs):
        for cp in _forward_copies(refs[:n], geo, refs[n], refs[n + 1]):
            cp.wait_send()
            cp.wait_recv()

    return list(pl.pallas_call(
        body, name=name,
        out_shape=[pltpu.HBM(f.shape, f.dtype) for f in fulls],
        in_specs=[HBM_SPEC] * n + [SEM_SPEC, SEM_SPEC, ANY],
        out_specs=[HBM_SPEC] * n,
        input_output_aliases={i: i for i in range(n)},
        compiler_params=SPLIT_COPY,
    )(*fulls, send_sems, recv_sems, after))


def _chip_copies(blk_ref, land_ref, send_sems, recv_sems):
    x, y, c, others = _place()
    return [pltpu.make_async_remote_copy(
        src_ref=blk_ref, dst_ref=land_ref.at[2 * x + y], send_sem=send_sems.at[j],
        recv_sem=recv_sems.at[j], device_id=(tx, ty, c), device_id_type=MESH)
        for j, (tx, ty) in enumerate(others)]


def _chip_exchange_start(blk, *, name):
    land = pltpu.with_memory_space_constraint(lax.empty((N_CHIPS, *blk.shape), blk.dtype), pltpu.HBM)

    def body(blk_ref, land_ref, send_sems, recv_sems, blk_out, land_out, token):
        for cp in _chip_copies(blk_ref, land_ref, send_sems, recv_sems):
            cp.start()
        token[...] = jnp.zeros_like(token)

    return pl.pallas_call(
        body, name=name,
        out_shape=(pltpu.SemaphoreType.DMA((3,)), pltpu.SemaphoreType.DMA((3,)),
                   pltpu.HBM(blk.shape, blk.dtype), pltpu.HBM(land.shape, land.dtype), TOKEN),
        in_specs=[HBM_SPEC] * 2,
        out_specs=(SEM_SPEC, SEM_SPEC, HBM_SPEC, HBM_SPEC, pl.BlockSpec(memory_space=pltpu.VMEM)),
        input_output_aliases={0: 2, 1: 3},
        compiler_params=SPLIT_COPY,
    )(pltpu.with_memory_space_constraint(blk, pltpu.HBM), land)


def _chip_exchange_wait(send_sems, recv_sems, blk, land, after, *, name):
    def body(blk_ref, land_ref, send_sems, recv_sems, after_ref, blk_out, land_out):
        for cp in _chip_copies(blk_ref, land_ref, send_sems, recv_sems):
            cp.wait_send()
            cp.wait_recv()

    return pl.pallas_call(
        body, name=name,
        out_shape=[pltpu.HBM(blk.shape, blk.dtype), pltpu.HBM(land.shape, land.dtype)],
        in_specs=[HBM_SPEC] * 2 + [SEM_SPEC, SEM_SPEC, ANY],
        out_specs=[HBM_SPEC] * 2,
        input_output_aliases={0: 0, 1: 1},
        compiler_params=SPLIT_COPY,
    )(blk, land, send_sems, recv_sems, after)[1]


def _scatter_geo(parts, axes):
    out = []
    for p, ax in zip(parts, axes):
        _, rh, cols = p.shape
        out.append((ax, rh, cols // N_CHIPS if ax == 1 else cols))
    return out


def _scatter_copies(p_refs, q_refs, geo, send_sems, recv_sems):
    x, y, c, others = _place()
    chip = 2 * x + y
    cps = []
    for i, (ax, rh, cw) in enumerate(geo):
        for j, (tx, ty) in enumerate(others):
            k = 2 * tx + ty
            src = (p_refs[i].at[0, :, pl.ds(pl.multiple_of(k * cw, LANES), cw)] if ax == 1
                   else p_refs[i].at[k])
            cps.append(pltpu.make_async_remote_copy(
                src_ref=src, dst_ref=q_refs[i].at[chip], send_sem=send_sems.at[3 * i + j],
                recv_sem=recv_sems.at[3 * i + j], device_id=(tx, ty, c), device_id_type=MESH))
    return cps


def _scatter_start(parts, axes, *, name):
    n = len(parts)
    geo = _scatter_geo(parts, axes)
    slots = [pltpu.HBM((N_CHIPS, rh, cw), p.dtype) for p, (_, rh, cw) in zip(parts, geo)]

    def body(*refs):
        p_refs, q_refs = refs[:n], refs[n:2 * n]
        send_sems, recv_sems = refs[2 * n], refs[2 * n + 1]
        token = refs[4 * n + 2]
        for cp in _scatter_copies(p_refs, q_refs, geo, send_sems, recv_sems):
            cp.start()
        token[...] = jnp.zeros_like(token)

    land = [pltpu.with_memory_space_constraint(lax.empty(s.inner_aval.shape, s.inner_aval.dtype), pltpu.HBM)
            for s in slots]
    out = pl.pallas_call(
        body, name=name,
        out_shape=(pltpu.SemaphoreType.DMA((3 * n,)), pltpu.SemaphoreType.DMA((3 * n,)),
                   *[pltpu.HBM(p.shape, p.dtype) for p in parts], *slots, TOKEN),
        in_specs=[HBM_SPEC] * (2 * n),
        out_specs=(SEM_SPEC, SEM_SPEC, *[HBM_SPEC] * (2 * n), pl.BlockSpec(memory_space=pltpu.VMEM)),
        input_output_aliases={i: 2 + i for i in range(2 * n)},
        compiler_params=SPLIT_COPY,
    )(*[pltpu.with_memory_space_constraint(p, pltpu.HBM) for p in parts], *land)
    return out[0], out[1], list(out[2:2 + n]), list(out[2 + n:2 + 2 * n]), out[2 + 2 * n]


def _scatter_wait(send_sems, recv_sems, parts, slots, axes, after, *, name):
    n = len(parts)
    geo = _scatter_geo(parts, axes)

    def body(*refs):
        p_refs, q_refs = refs[:n], refs[n:2 * n]
        for cp in _scatter_copies(p_refs, q_refs, geo, refs[2 * n], refs[2 * n + 1]):
            cp.wait_send()
            cp.wait_recv()

    out = pl.pallas_call(
        body, name=name,
        out_shape=[pltpu.HBM(a.shape, a.dtype) for a in (*parts, *slots)],
        in_specs=[HBM_SPEC] * (2 * n) + [SEM_SPEC, SEM_SPEC, ANY],
        out_specs=[HBM_SPEC] * (2 * n),
        input_output_aliases={i: i for i in range(2 * n)},
        compiler_params=SPLIT_COPY,
    )(*parts, *slots, send_sems, recv_sems, after)
    return list(out[:n]), list(out[n:])


def _sibling_swap(grads, *, name):
    n = len(grads)
    out_shape = [jax.ShapeDtypeStruct((g.shape[0], g.shape[1] // 2, g.shape[2]), g.dtype)
                 for g in grads]

    def body(*refs):
        g_refs, o_refs = refs[:n], refs[n:2 * n]
        send_sems, recv_sems = refs[2 * n:]
        x, y, c, _ = _place()
        cps = []
        for i in range(n):
            rh = grads[i].shape[1] // 2
            src = g_refs[i].at[:, pl.ds(pl.multiple_of((1 - c) * rh, 16), rh), :]
            cp = pltpu.make_async_remote_copy(
                src_ref=src, dst_ref=o_refs[i], send_sem=send_sems.at[i], recv_sem=recv_sems.at[i],
                device_id=(x, y, 1 - c), device_id_type=MESH)
            cp.start()
            cps.append(cp)
        for cp in cps:
            cp.wait()

    return pl.pallas_call(
        body, in_specs=[ANY] * n, out_specs=[ANY] * n, out_shape=out_shape,
        scratch_shapes=[pltpu.SemaphoreType.DMA((n,)), pltpu.SemaphoreType.DMA((n,))],
        name=name)(*grads)


def _swap_copies(g_refs, r_refs, shapes, send_sems, recv_sems):
    x, y, c, _ = _place()
    cps = []
    for i, shape in enumerate(shapes):
        rh = shape[1] // 2
        src = g_refs[i].at[:, pl.ds(pl.multiple_of((1 - c) * rh, 16), rh), :]
        cps.append(pltpu.make_async_remote_copy(
            src_ref=src, dst_ref=r_refs[i], send_sem=send_sems.at[i], recv_sem=recv_sems.at[i],
            device_id=(x, y, 1 - c), device_id_type=MESH))
    return cps


def _sibling_swap_start(grads, *, name):
    n = len(grads)
    shapes = [g.shape for g in grads]
    lands = [pltpu.HBM((s[0], s[1] // 2, s[2]), g.dtype) for s, g in zip(shapes, grads)]

    def body(*refs):
        g_refs, r_refs = refs[:n], refs[n:2 * n]
        token = refs[4 * n + 2]
        for cp in _swap_copies(g_refs, r_refs, shapes, refs[2 * n], refs[2 * n + 1]):
            cp.start()
        token[...] = jnp.zeros_like(token)

    land = [pltpu.with_memory_space_constraint(lax.empty(s.inner_aval.shape, s.inner_aval.dtype), pltpu.HBM)
            for s in lands]
    out = pl.pallas_call(
        body, name=name,
        out_shape=(pltpu.SemaphoreType.DMA((n,)), pltpu.SemaphoreType.DMA((n,)),
                   *[pltpu.HBM(g.shape, g.dtype) for g in grads], *lands, TOKEN),
        in_specs=[HBM_SPEC] * (2 * n),
        out_specs=(SEM_SPEC, SEM_SPEC, *[HBM_SPEC] * (2 * n), pl.BlockSpec(memory_space=pltpu.VMEM)),
        input_output_aliases={i: 2 + i for i in range(2 * n)},
        compiler_params=SPLIT_COPY,
    )(*[pltpu.with_memory_space_constraint(g, pltpu.HBM) for g in grads], *land)
    return out[0], out[1], list(out[2:2 + n]), list(out[2 + n:2 + 2 * n]), out[2 + 2 * n]


def _sibling_swap_wait(send_sems, recv_sems, grads, lands, after, *, name):
    n = len(grads)
    shapes = [g.shape for g in grads]

    def body(*refs):
        g_refs, r_refs = refs[:n], refs[n:2 * n]
        for cp in _swap_copies(g_refs, r_refs, shapes, refs[2 * n], refs[2 * n + 1]):
            cp.wait_send()
            cp.wait_recv()

    out = pl.pallas_call(
        body, name=name,
        out_shape=[pltpu.HBM(a.shape, a.dtype) for a in (*grads, *lands)],
        in_specs=[HBM_SPEC] * (2 * n) + [SEM_SPEC, SEM_SPEC, ANY],
        out_specs=[HBM_SPEC] * (2 * n),
        input_output_aliases={i: i for i in range(2 * n)},
        compiler_params=SPLIT_COPY,
    )(*grads, *lands, send_sems, recv_sems, after)
    return list(out[:n]), list(out[n:])


def _pair_add(g3s, rxs, place, *, out_dtype, name):
    n = len(g3s)
    steps, trs = _group_tiles([g.shape[1] // 2 for g in g3s], 16)

    def body(p_ref, *refs):
        for i in range(n):
            refs[2 * n + i][...] = (refs[2 * i][...] + refs[2 * i + 1][...]).astype(out_dtype)

    in_specs, out_specs = [], []
    for g, tr in zip(g3s, trs):
        blk = (g.shape[0], tr, g.shape[2])
        in_specs += [pl.BlockSpec(blk, lambda i, p_ref: (0, p_ref[1] * steps + i, 0)),
                     pl.BlockSpec(blk, lambda i, p_ref: (0, i, 0))]
        out_specs.append(pl.BlockSpec(blk, lambda i, p_ref: (0, i, 0)))
    return pl.pallas_call(
        body,
        grid_spec=pltpu.PrefetchScalarGridSpec(
            num_scalar_prefetch=1, grid=(steps,), in_specs=in_specs, out_specs=out_specs),
        out_shape=[jax.ShapeDtypeStruct((g.shape[0], g.shape[1] // 2, g.shape[2]), out_dtype)
                   for g in g3s],
        compiler_params=_cp("parallel"), name=name)(place, *[a for q in zip(g3s, rxs) for a in q])


def _sum_slots(q, *, name):
    ns, rows, cols = q.shape
    tr = next(t for t in (128, 64, 32, 16, 8) if rows % t == 0)

    def body(q_ref, o_ref):
        acc = q_ref[0].astype(F32)
        for k in range(1, ns):
            acc = acc + q_ref[k].astype(F32)
        o_ref[...] = acc

    return pl.pallas_call(
        body, grid=(rows // tr,),
        in_specs=[pl.BlockSpec((ns, tr, cols), lambda i: (0, i, 0))],
        out_specs=_rspec(tr, cols),
        out_shape=jax.ShapeDtypeStruct((rows, cols), F32),
        compiler_params=_cp("parallel"), name=name)(q)


def _sum_chips(qs, ps, place, axes, *, name):
    n = len(qs)
    per = N_CHIPS + 1
    steps, trs = _group_tiles([q.shape[1] for q in qs], 16)

    def body(p_ref, *refs):
        chip = p_ref[0]
        for i in range(n):
            q_refs, own_ref = refs[per * i:per * i + N_CHIPS], refs[per * i + N_CHIPS]
            acc = jnp.where(chip == 0, own_ref[...], q_refs[0][...]).astype(F32)
            for k in range(1, N_CHIPS):
                acc = acc + jnp.where(chip == k, own_ref[...], q_refs[k][...]).astype(F32)
            refs[per * n + i][...] = acc

    def slot_spec(k, tr, cw):
        return pl.BlockSpec((None, tr, cw),
                            lambda i, p_ref: (jnp.where(p_ref[0] == k, (k + 1) % N_CHIPS, k), i, 0))

    in_specs, out_specs, operands = [], [], []
    for q, p, ax, tr in zip(qs, ps, axes, trs):
        cw = q.shape[2]
        in_specs += [slot_spec(k, tr, cw) for k in range(N_CHIPS)]
        in_specs.append(pl.BlockSpec((None, tr, cw), (lambda i, p_ref: (0, i, p_ref[0])) if ax == 1
                                     else (lambda i, p_ref: (p_ref[0], i, 0))))
        out_specs.append(pl.BlockSpec((tr, cw), lambda i, p_ref: (p_ref[1] * steps + i, 0)))
        operands += [q] * N_CHIPS + [p]
    return pl.pallas_call(
        body,
        grid_spec=pltpu.PrefetchScalarGridSpec(
            num_scalar_prefetch=1, grid=(steps,), in_specs=in_specs, out_specs=out_specs),
        out_shape=[jax.ShapeDtypeStruct((2 * q.shape[1], q.shape[2]), F32) for q in qs],
        compiler_params=_cp("parallel"), name=name)(place, *operands)


def _sibling_share(shards, *, name):
    n = len(shards)

    def body(*refs):
        o_refs = refs[n:2 * n]
        send_sems, recv_sems = refs[2 * n:]
        x, y, c, _ = _place()
        cps = []
        for i in range(n):
            rh = shards[i].shape[0] // 2
            mine = o_refs[i].at[pl.ds(pl.multiple_of(c * rh, 8), rh), :]
            cp = pltpu.make_async_remote_copy(
                src_ref=mine, dst_ref=mine, send_sem=send_sems.at[i], recv_sem=recv_sems.at[i],
                device_id=(x, y, 1 - c), device_id_type=MESH)
            cp.start()
            cps.append(cp)
        for i in range(n):
            rh = shards[i].shape[0] // 2
            theirs = o_refs[i].at[pl.ds(pl.multiple_of((1 - c) * rh, 8), rh), :]
            pltpu.make_async_remote_copy(
                src_ref=theirs, dst_ref=theirs, send_sem=send_sems.at[i], recv_sem=recv_sems.at[i],
                device_id=(x, y, c), device_id_type=MESH).wait_recv()
        for cp in cps:
            cp.wait_send()

    return pl.pallas_call(
        body, in_specs=[ANY] * n, out_specs=[ANY] * n,
        out_shape=[jax.ShapeDtypeStruct(h.shape, h.dtype) for h in shards],
        input_output_aliases={i: i for i in range(n)},
        scratch_shapes=[pltpu.SemaphoreType.DMA((n,)), pltpu.SemaphoreType.DMA((n,))],
        name=name)(*shards)


def _gather_all(blk, *, name, after=None, shares=()):
    rows, cols = blk.shape
    extra = [] if after is None else [after]
    n_s = len(shares)

    def body(x_ref, *refs):
        s_refs = refs[len(extra) + n_s + 1:len(extra) + 2 * n_s + 1]
        out_ref = refs[len(extra) + n_s]
        send_sems, recv_sems, local_sem, s_send, s_recv = refs[len(extra) + 2 * n_s + 1:]
        x, y, c = lax.axis_index("x"), lax.axis_index("y"), lax.axis_index("c")
        me = 4 * x + 2 * y + c
        mine = pltpu.make_async_copy(x_ref, out_ref.at[me], local_sem)
        mine.start()
        cps = []
        for i in range(n_s):
            rh = shares[i].shape[0] // 2
            half = s_refs[i].at[pl.ds(pl.multiple_of(c * rh, 8), rh), :]
            cp = pltpu.make_async_remote_copy(
                src_ref=half, dst_ref=half, send_sem=s_send.at[i], recv_sem=s_recv.at[i],
                device_id=(x, y, 1 - c), device_id_type=MESH)
            cp.start()
            cps.append(cp)
        for k in range(1, N_DEV):
            tx = (1 - x) if (k >> 2) & 1 else x
            ty = (1 - y) if (k >> 1) & 1 else y
            tc = (1 - c) if k & 1 else c
            cp = pltpu.make_async_remote_copy(
                src_ref=x_ref, dst_ref=out_ref.at[me], send_sem=send_sems.at[k - 1],
                recv_sem=recv_sems.at[k - 1], device_id=(tx, ty, tc), device_id_type=MESH)
            cp.start()
            cps.append(cp)
        for k in range(1, N_DEV):
            tx = (1 - x) if (k >> 2) & 1 else x
            ty = (1 - y) if (k >> 1) & 1 else y
            tc = (1 - c) if k & 1 else c
            got = out_ref.at[4 * tx + 2 * ty + tc]
            pltpu.make_async_remote_copy(
                src_ref=got, dst_ref=got, send_sem=send_sems.at[k - 1], recv_sem=recv_sems.at[k - 1],
                device_id=(x, y, c), device_id_type=MESH).wait_recv()
        for i in range(n_s):
            rh = shares[i].shape[0] // 2
            theirs = s_refs[i].at[pl.ds(pl.multiple_of((1 - c) * rh, 8), rh), :]
            pltpu.make_async_remote_copy(
                src_ref=theirs, dst_ref=theirs, send_sem=s_send.at[i], recv_sem=s_recv.at[i],
                device_id=(x, y, c), device_id_type=MESH).wait_recv()
        for cp in cps:
            cp.wait_send()
        mine.wait()

    vm = pl.BlockSpec(memory_space=pltpu.VMEM)
    out = pl.pallas_call(
        body, in_specs=[vm] + [ANY] * (len(extra) + n_s), out_specs=[vm] + [ANY] * n_s,
        out_shape=[jax.ShapeDtypeStruct((N_DEV, rows, cols), blk.dtype)]
        + [jax.ShapeDtypeStruct(s.shape, s.dtype) for s in shares],
        input_output_aliases={1 + len(extra) + i: 1 + i for i in range(n_s)},
        scratch_shapes=[pltpu.SemaphoreType.DMA((N_DEV - 1,)), pltpu.SemaphoreType.DMA((N_DEV - 1,)),
                        pltpu.SemaphoreType.DMA, pltpu.SemaphoreType.DMA((max(n_s, 1),)),
                        pltpu.SemaphoreType.DMA((max(n_s, 1),))],
        name=name)(blk, *extra, *shares)
    return (out[0], *out[1:]) if n_s else out[0]


def _as_rows(a):
    flat = a.reshape(-1)
    n = flat.shape[0]
    rows = -(-n // (8 * LANES)) * 8
    return jnp.pad(flat, (0, rows * LANES - n)).reshape(rows, LANES)


def _from_rows(p, shape):
    n = int(np.prod(shape))
    return p.reshape(-1)[:n].reshape(shape)


WEIGHT_AXES = (1, 1, 1, 0, 1, 0)
WIRE = BF16


def kernel(x, meta_tokens, w_in, w_na_out, w_hg_out, w_o, w_up, w_down, norm_mix, norm_mlp, norm_final, hg_norm, na_rpb, hg_lb_logits, loss_target, m_meta_tokens, m_w_in, m_w_na_out, m_w_hg_out, m_w_o, m_w_up, m_w_down, m_norm_mix, m_norm_mlp, m_norm_final, m_hg_norm, m_na_rpb, m_hg_lb_logits, v_meta_tokens, v_w_in, v_w_na_out, v_w_hg_out, v_w_o, v_w_up, v_w_down, v_norm_mix, v_norm_mlp, v_norm_final, v_hg_norm, v_na_rpb, v_hg_lb_logits):
    xi, yi, ci = lax.axis_index("x"), lax.axis_index("y"), lax.axis_index("c")
    chip = 2 * xi + yi
    d = x.shape[-1]
    dshard = meta_tokens.shape[1]
    hgw = hg_norm.shape[1]
    lbs = hg_lb_logits.shape[2]
    big = [w_in[0], w_na_out[0], w_hg_out[0], w_o[0], w_up[0], w_down[0]]
    big_m = [m_w_in[0], m_w_na_out[0], m_w_hg_out[0], m_w_o[0], m_w_up[0], m_w_down[0]]
    big_v = [v_w_in[0], v_w_na_out[0], v_w_hg_out[0], v_w_o[0], v_w_up[0], v_w_down[0]]

    place = jnp.stack([chip, ci]).astype(jnp.int32)
    in_axes, rest_axes = WEIGHT_AXES[:1], WEIGHT_AXES[1:]
    own_w = _cast_into_full(big, WEIGHT_AXES, place, name="cast_shards")
    small_in = jnp.concatenate([_as_rows(meta_tokens), _as_rows(hg_lb_logits)], axis=0)
    sm_send, sm_recv, sm_blk, sm_land, sm_token = _chip_exchange_start(small_in,
                                                                       name="small_params_start")
    in_send, in_recv, in_bufs, in_token = _allgather_start(own_w[:1], in_axes, sm_token,
                                                           name="weight_allgather_in_start")
    ag_send, ag_recv, ag_bufs, ag_token = _allgather_start(own_w[1:], rest_axes, in_token,
                                                           name="weight_allgather_rest_start")
    sm_land = _chip_exchange_wait(sm_send, sm_recv, sm_blk, sm_land, ag_token,
                                  name="small_params_wait")
    small_all = lax.dynamic_update_slice(sm_land, small_in[None], (chip, 0, 0))
    forward = {}

    def first_weight(after):
        got = _allgather_wait(in_send, in_recv, in_bufs, in_axes, after,
                              name="weight_allgather_in_wait")
        return _allgather_forward(got, in_axes, name="weight_allgather_in_forward")[0]

    def rest_landed(after):
        got = _allgather_wait(ag_send, ag_recv, ag_bufs, rest_axes, after,
                              name="weight_allgather_rest_wait")
        send, recv, bufs, token = _allgather_forward_start(
            got, rest_axes, name="weight_allgather_rest_forward_start")
        forward["rest"] = (send, recv, bufs)
        return token

    def rest_weights(after):
        return _allgather_forward_wait(*forward["rest"], rest_axes, after,
                                       name="weight_allgather_rest_forward_wait")

    n_meta_rows = N_META * dshard // LANES
    meta_full = (small_all[:, :n_meta_rows].reshape(N_CHIPS, N_META, dshard)
                 .transpose(1, 0, 2).reshape(N_META, d))
    lbl_full = (small_all[:, n_meta_rows:].reshape(N_CHIPS, -1)[:, :4 * lbs]
                .reshape(N_CHIPS, 2, 2, lbs).transpose(1, 2, 0, 3).reshape(2, 2, N_CHIPS * lbs))
    lb = jax.nn.softmax(lbl_full, axis=1)[:, 0]

    def by_chip(dws, axes):
        return [g.reshape(1, *g.shape) if ax == 1
                else g.reshape(N_CHIPS, g.shape[0] // N_CHIPS, g.shape[1]) for g, ax in zip(dws, axes)]

    flying = {}

    def scatter(tag, axes, g3, rx):
        parts = _pair_add(g3, rx, place, out_dtype=WIRE, name=f"grad_pair_add_{tag}")
        send, recv, parts, slots, token = _scatter_start(parts, axes,
                                                         name=f"grad_scatter_{tag}_start")
        flying[tag] = (send, recv, parts, slots)
        return token

    def swap(tag, axes):
        def start(dws):
            send, recv, g3, lands, token = _sibling_swap_start(
                by_chip(dws, axes), name=f"grad_sibling_swap_{tag}_start")
            flying["swap_" + tag] = (send, recv, g3, lands)
            return token

        def finish(after):
            g3, rx = _sibling_swap_wait(*flying["swap_" + tag], after,
                                        name=f"grad_sibling_swap_{tag}_wait")
            return scatter(tag, axes, g3, rx)
        return start, finish

    swap_rest, scatter_rest = swap("rest", rest_axes)
    swap_in, scatter_in = swap("in", in_axes)

    def landed(tag, axes, after):
        return _scatter_wait(*flying[tag], axes, after, name=f"grad_scatter_{tag}_wait")

    (loss, dx, dmeta, *_, dg_mix, dg_mlp, dg_fin, d_gain, d_rpb, d_lb) = _local_step(
        x[0], loss_target[0], meta_full, first_weight, rest_weights, norm_mix, norm_mlp,
        norm_final.reshape(1, d), hg_norm, na_rpb[0], lb, swap_rest, scatter_rest,
        lambda dw_in: swap_in([dw_in]), rest_landed, scatter_in)

    parts_rest, slots_rest = landed("rest", rest_axes, dx)
    g_rest = _sibling_share(_sum_chips(slots_rest, parts_rest, place, rest_axes,
                                       name="grad_sum_chips_rest"),
                            name="grad_sibling_share_rest")
    out_rest = _adamw(big[1:], g_rest, big_m[1:], big_v[1:], name="adamw_rest")
    parts_in, slots_in = landed("in", in_axes, out_rest[-1][0])
    half_in = _sum_chips(slots_in, parts_in, place, in_axes, name="grad_sum_chips_in")

    d_rpb_c = d_rpb[:, :2 * NA_WIN_W - 1]
    small_g = [dmeta, dg_mix, dg_mlp, dg_fin, d_gain, d_rpb_c, d_lb, loss]
    packed = jnp.concatenate([_as_rows(a) for a in small_g], axis=0)
    gathered, g_in = _gather_all(packed, shares=half_in, name="gather_small_grads")
    g_big = [g_in] + list(g_rest)
    total = _sum_slots(gathered, name="sum_small_grads")
    offs = np.cumsum([0] + [_as_rows(a).shape[0] for a in small_g])
    take = lambda i, shape: _from_rows(total[offs[i]:offs[i + 1]], shape)
    g_meta_full = take(0, (N_META, d))
    g_norm_mix, g_norm_mlp = take(1, (1, d)), take(2, (1, d))
    g_norm_final = take(3, (d,))
    g_hg_norm = take(4, (1, hgw))
    g_rpb = take(5, na_rpb.shape)
    g_lb = take(6, (2, hgw))
    loss_total = take(7, (1, LANES))[0, 0]
    g_meta = lax.dynamic_slice_in_dim(g_meta_full, chip * dshard, dshard, axis=1)
    dl0 = lb * (1.0 - lb) * g_lb
    g_lbl_full = jnp.stack([dl0, -dl0], axis=1)
    g_lbl = lax.dynamic_slice_in_dim(g_lbl_full, chip * lbs, lbs, axis=2)

    big_out = _adamw(big[:1], [g_in], big_m[:1], big_v[:1], name="adamw_in") + out_rest
    small_w = [meta_tokens, norm_mix, norm_mlp, norm_final, hg_norm, na_rpb, hg_lb_logits]
    small_gr = [g_meta, g_norm_mix, g_norm_mlp, g_norm_final, g_hg_norm, g_rpb, g_lbl]
    small_m = [m_meta_tokens, m_norm_mix, m_norm_mlp, m_norm_final, m_hg_norm, m_na_rpb, m_hg_lb_logits]
    small_v = [v_meta_tokens, v_norm_mix, v_norm_mlp, v_norm_final, v_hg_norm, v_na_rpb, v_hg_lb_logits]
    pk = lambda lst: jnp.concatenate([_as_rows(a) for a in lst], axis=0)
    ((sd, sm, sv),) = _adamw([pk(small_w)], [pk(small_gr)], [pk(small_m)], [pk(small_v)],
                             name="adamw_small")
    soffs = np.cumsum([0] + [_as_rows(a).shape[0] for a in small_w])
    unpk = lambda p: [_from_rows(p[soffs[i]:soffs[i + 1]], small_w[i].shape) for i in range(len(small_w))]
    sd, sm, sv = unpk(sd), unpk(sm), unpk(sv)

    def order(bigs, smalls):
        return [smalls[0]] + [b.reshape(1, *b.shape) for b in bigs] + smalls[1:]

    grads = order(g_big, small_gr)
    deltas = order([o[0] for o in big_out], sd)
    new_m = order([o[1] for o in big_out], sm)
    new_v = order([o[2] for o in big_out], sv)
    return (loss_total, dx.reshape(1, *dx.shape), *grads, *deltas, *new_m, *new_v)
```

```python
import functools

import numpy as np
import jax
import jax.numpy as jnp
from jax import lax
from jax.experimental import pallas as pl
from jax.experimental.pallas import tpu as pltpu

F32 = jnp.float32
BF16 = jnp.bfloat16
HIGHEST = lax.Precision.HIGHEST

GRID_W = 64
N_META = 16
EPS = 1e-6
NA_HEAD_DIM = 64
NA_WIN_H = 8
NA_WIN_W = 16
HG_DK = 128
HG_CHUNK = 16
LANES = 128
ROW_ALIGN = 128
VMEM_LIMIT = 48 * 1024 * 1024

ADAM_LR = 0.001
ADAM_B1 = 0.9
ADAM_B2 = 0.999
ADAM_EPS = 1e-08
ADAM_WD = 0.01
ADAM_STEP = 10

MESH = pl.DeviceIdType.MESH


def _cp(*sem):
    return pltpu.CompilerParams(dimension_semantics=sem, vmem_limit_bytes=VMEM_LIMIT)


def _sigmoid(x):
    return 0.5 * jnp.tanh(0.5 * x) + 0.5


def _dot(a, b, dims, precision=None):
    return lax.dot_general(a, b, (dims, ((), ())), preferred_element_type=F32, precision=precision)


def _nn(a, b, **kw):
    return _dot(a, b, ((1,), (0,)), **kw)


def _nt(a, b, **kw):
    return _dot(a, b, ((1,), (1,)), **kw)


def _tn(a, b, **kw):
    return _dot(a, b, ((0,), (0,)), **kw)


def _matmul(a, b, *, ta=False, tb=False, tm=None, tn=None, tk=None, out_dtype=F32, name,
            precision=None, after=None, epilogue=None, tiles=(), out_dtypes=None):
    extra = [] if after is None else [after]
    single = out_dtypes is None
    if single:
        out_dtypes = (out_dtype,)
    n_t, n_o = len(tiles), len(out_dtypes)
    if ta:
        kdim, m = a.shape
    else:
        m, kdim = a.shape
    if tb:
        n, k2 = b.shape
    else:
        k2, n = b.shape
    assert kdim == k2, (a.shape, b.shape, ta, tb)
    if tm is None:
        if ta:
            tm = next(t for t in (1024, 512, 256, 128, m) if m % t == 0)
        else:
            tm = m // 2 if (m // 2) % 16 == 0 and m > 512 else m
    if tn is None:
        wide = (1024,) if not ta and len(tiles) <= 1 else ()
        tn = next(t for t in (*wide, 512, 256, 128, n) if n % t == 0)
    if tk is None:
        tk = kdim if ta else next(t for t in (2048, 1024, 512, 256, 128, kdim) if kdim % t == 0)
    assert m % tm == 0 and n % tn == 0 and kdim % tk == 0, (m, n, kdim, tm, tn, tk)
    nk = kdim // tk
    op_dtype = F32 if precision is not None else BF16

    def body(a_ref, b_ref, *refs):
        t_refs = refs[:n_t]
        o_refs = refs[n_t + len(extra):n_t + len(extra) + n_o]
        av = a_ref[...].astype(op_dtype)
        bv = b_ref[...].astype(op_dtype)
        dims = ((0 if ta else 1,), (1 if tb else 0,))
        part = _dot(av, bv, dims, precision=precision)

        def finish(acc):
            outs = (acc,) if epilogue is None else epilogue(acc, *[t[...] for t in t_refs])
            for o_ref, val in zip(o_refs, outs):
                o_ref[...] = val.astype(o_ref.dtype)

        if nk == 1:
            finish(part)
            return
        acc_ref = refs[-1]
        kk = pl.program_id(2)

        @pl.when(kk == 0)
        def _():
            acc_ref[...] = part

        @pl.when((kk > 0) & (kk < nk - 1))
        def _():
            acc_ref[...] += part

        @pl.when(kk == nk - 1)
        def _():
            finish(acc_ref[...] + part)

    a_spec = (pl.BlockSpec((tk, tm), lambda i, j, k: (k, i)) if ta
              else pl.BlockSpec((tm, tk), lambda i, j, k: (i, k)))
    b_spec = (pl.BlockSpec((tn, tk), lambda i, j, k: (j, k)) if tb
              else pl.BlockSpec((tk, tn), lambda i, j, k: (k, j)))
    for _, off in tiles:
        assert off % tn == 0, (off, tn)
    t_specs = [pl.BlockSpec((tm, tn), functools.partial(lambda i, j, k, o: (i, o + j), o=off // tn))
               for _, off in tiles]
    o_spec = pl.BlockSpec((tm, tn), lambda i, j, k: (i, j))
    outs = pl.pallas_call(
        body,
        grid=(m // tm, n // tn, nk),
        in_specs=[a_spec, b_spec] + t_specs + [pl.BlockSpec(memory_space=pl.ANY)] * len(extra),
        out_specs=[o_spec] * n_o,
        out_shape=[jax.ShapeDtypeStruct((m, n), dt) for dt in out_dtypes],
        scratch_shapes=[pltpu.VMEM((tm, tn), F32)] if nk > 1 else [],
        compiler_params=_cp("parallel", "parallel", "arbitrary"),
        name=name,
    )(a, b, *[t for t, _ in tiles], *extra)
    return outs[0] if single else outs


def _rspec(tr, w, cb=0):
    return pl.BlockSpec((tr, w), lambda i: (i, cb))


def _fspec(shape):
    nd = len(shape)
    return pl.BlockSpec(shape, lambda i: (0,) * nd)


ROW_VMEM_BUDGET = 20 * 1024 * 1024
ROW_MIN_STEPS = 4


def _row_tile(lp, row_bytes):
    for k in range(ROW_MIN_STEPS, lp // 16 + 1):
        tr = lp // k
        if lp % k == 0 and tr % 16 == 0 and 2 * tr * row_bytes <= ROW_VMEM_BUDGET:
            return tr
    return lp


def _token_rows_copy(i, n_tiles, tr, n_tok, tok_ref, buf_ref, sem, *, to_tokens, start=True,
                     wait=True):
    assert n_tiles >= 2 and 0 < n_tok + N_META - (n_tiles - 1) * tr <= tr

    def run(tok_row, buf_row, count):
        tok = tok_ref.at[pl.ds(tok_row, count), :]
        buf = buf_ref.at[pl.ds(buf_row, count), :]
        cp = pltpu.make_async_copy(buf, tok, sem) if to_tokens else pltpu.make_async_copy(tok, buf, sem)
        if start:
            cp.start()
        if wait:
            cp.wait()

    @pl.when(i == 0)
    def _():
        run(0, N_META, tr - N_META)

    if n_tiles > 2:
        @pl.when((i > 0) & (i < n_tiles - 1))
        def _():
            run(pl.multiple_of(i * tr - N_META, 8), 0, tr)

    @pl.when(i == n_tiles - 1)
    def _():
        run((n_tiles - 1) * tr - N_META, 0, n_tok + N_META - (n_tiles - 1) * tr)


def _embed_norm(x, tgt, meta, g, *, lp, name):
    n_tok, d = x.shape
    tr = _row_tile(lp, d * (4 + 2 + 4))
    n_tiles = lp // tr

    def body(x_ref, tgt_ref, meta_ref, g_ref, h_ref, o_ref, tp_ref, buf_ref, tbuf_ref, sems):
        i = pl.program_id(0)
        buf_ref[...] = jnp.zeros_like(buf_ref)
        tbuf_ref[...] = jnp.zeros_like(tbuf_ref)

        @pl.when(i == 0)
        def _():
            buf_ref[0:N_META, :] = meta_ref[...]

        _token_rows_copy(i, n_tiles, tr, n_tok, tgt_ref, tbuf_ref, sems.at[1], to_tokens=False,
                         wait=False)
        _token_rows_copy(i, n_tiles, tr, n_tok, x_ref, buf_ref, sems.at[0], to_tokens=False)
        xv = buf_ref[...]
        h_ref[...] = xv
        r = lax.rsqrt(jnp.mean(xv * xv, axis=-1, keepdims=True) + EPS)
        o_ref[...] = (xv * r * g_ref[...]).astype(BF16)
        _token_rows_copy(i, n_tiles, tr, n_tok, tgt_ref, tbuf_ref, sems.at[1], to_tokens=False,
                         start=False)
        tp_ref[...] = tbuf_ref[...]

    return pl.pallas_call(
        body, grid=(n_tiles,),
        in_specs=[ANY, ANY, _fspec((N_META, d)), _fspec((1, d))],
        out_specs=[_rspec(tr, d), _rspec(tr, d), _rspec(tr, d)],
        out_shape=[jax.ShapeDtypeStruct((lp, d), F32), jax.ShapeDtypeStruct((lp, d), BF16),
                   jax.ShapeDtypeStruct((lp, d), F32)],
        scratch_shapes=[pltpu.VMEM((tr, d), F32), pltpu.VMEM((tr, d), F32),
                        pltpu.SemaphoreType.DMA((2,))],
        compiler_params=_cp("parallel"), name=name)(x, tgt, meta, g)


def _residual_norm(h, t, g, *, name):
    lp, d = h.shape
    tr = _row_tile(lp, d * (4 + 4 + 4 + 2))

    def body(h_ref, t_ref, g_ref, h1_ref, m_ref):
        xv = h_ref[...] + t_ref[...]
        h1_ref[...] = xv
        r = lax.rsqrt(jnp.mean(xv * xv, axis=-1, keepdims=True) + EPS)
        m_ref[...] = (xv * r * g_ref[...]).astype(BF16)

    return pl.pallas_call(
        body, grid=(lp // tr,),
        in_specs=[_rspec(tr, d), _rspec(tr, d), _fspec((1, d))],
        out_specs=[_rspec(tr, d), _rspec(tr, d)],
        out_shape=[jax.ShapeDtypeStruct((lp, d), F32), jax.ShapeDtypeStruct((lp, d), BF16)],
        compiler_params=_cp("parallel"), name=name)(h, t, g)


def _rmsnorm_bwd_add(x, g, dy, dres, *, name):
    lp, d = x.shape
    tr = _row_tile(lp, d * (4 * 4 + 2))

    def body(x_ref, g_ref, dy_ref, dr_ref, dx_ref, dx16_ref, dg_ref):
        @pl.when(pl.program_id(0) == 0)
        def _():
            dg_ref[...] = jnp.zeros_like(dg_ref)

        xv = x_ref[...]
        r = lax.rsqrt(jnp.mean(xv * xv, axis=-1, keepdims=True) + EPS)
        xh = xv * r
        dyv = dy_ref[...]
        dg_ref[...] += jnp.sum(dyv * xh, axis=0, keepdims=True)
        dxh = dyv * g_ref[...]
        dx = dr_ref[...] + r * (dxh - xh * jnp.mean(dxh * xh, axis=-1, keepdims=True))
        dx_ref[...] = dx
        dx16_ref[...] = dx.astype(BF16)

    return pl.pallas_call(
        body, grid=(lp // tr,),
        in_specs=[_rspec(tr, d), _fspec((1, d)), _rspec(tr, d), _rspec(tr, d)],
        out_specs=[_rspec(tr, d), _rspec(tr, d), _fspec((1, d))],
        out_shape=[jax.ShapeDtypeStruct((lp, d), F32), jax.ShapeDtypeStruct((lp, d), BF16),
                   jax.ShapeDtypeStruct((1, d), F32)],
        compiler_params=_cp("arbitrary"), name=name)(x, g, dy, dres)


def _rmsnorm_bwd_tokens(x, g, dy, dres, *, n_tok, name):
    lp, d = x.shape
    tr = _row_tile(lp, d * 4 * 4)
    n_tiles = lp // tr

    def body(x_ref, g_ref, dy_ref, dr_ref, dtok_ref, dmeta_ref, dg_ref, buf_ref, sem):
        i = pl.program_id(0)

        @pl.when(i == 0)
        def _():
            dg_ref[...] = jnp.zeros_like(dg_ref)

        xv = x_ref[...]
        r = lax.rsqrt(jnp.mean(xv * xv, axis=-1, keepdims=True) + EPS)
        xh = xv * r
        dyv = dy_ref[...]
        dg_ref[...] += jnp.sum(dyv * xh, axis=0, keepdims=True)
        dxh = dyv * g_ref[...]
        buf_ref[...] = dr_ref[...] + r * (dxh - xh * jnp.mean(dxh * xh, axis=-1, keepdims=True))

        @pl.when(i == 0)
        def _():
            dmeta_ref[...] = buf_ref[0:N_META, :]

        _token_rows_copy(i, n_tiles, tr, n_tok, dtok_ref, buf_ref, sem, to_tokens=True)

    return pl.pallas_call(
        body, grid=(n_tiles,),
        in_specs=[_rspec(tr, d), _fspec((1, d)), _rspec(tr, d), _rspec(tr, d)],
        out_specs=[ANY, _fspec((N_META, d)), _fspec((1, d))],
        out_shape=[jax.ShapeDtypeStruct((n_tok, d), F32), jax.ShapeDtypeStruct((N_META, d), F32),
                   jax.ShapeDtypeStruct((1, d), F32)],
        scratch_shapes=[pltpu.VMEM((tr, d), F32), pltpu.SemaphoreType.DMA],
        compiler_params=_cp("arbitrary"), name=name)(x, g, dy, dres)


def _final_loss(h1, t2, g, tgt, *, n_tok, name):
    lp, d = h1.shape
    tr = _row_tile(lp, d * (4 * 4 + 2))
    n_tiles = lp // tr

    def body(h_ref, t_ref, g_ref, tg_ref, dh_ref, dh16_ref, loss_ref, dg_ref):
        i = pl.program_id(0)

        @pl.when(i == 0)
        def _():
            loss_ref[...] = jnp.zeros_like(loss_ref)
            dg_ref[...] = jnp.zeros_like(dg_ref)

        xv = h_ref[...] + t_ref[...]
        r = lax.rsqrt(jnp.mean(xv * xv, axis=-1, keepdims=True) + EPS)
        xh = xv * r
        gv = g_ref[...]
        row = i * tr + lax.broadcasted_iota(jnp.int32, (tr, 1), 0)
        valid = (row >= N_META) & (row < N_META + n_tok)
        err = jnp.where(valid, xh * gv - tg_ref[...], 0.0)
        loss_ref[...] += jnp.sum(0.5 * err * err) / d
        dy = err / d
        dg_ref[...] += jnp.sum(dy * xh, axis=0, keepdims=True)
        dxh = dy * gv
        dh = r * (dxh - xh * jnp.mean(dxh * xh, axis=-1, keepdims=True))
        dh_ref[...] = dh
        dh16_ref[...] = dh.astype(BF16)

    return pl.pallas_call(
        body, grid=(n_tiles,),
        in_specs=[_rspec(tr, d), _rspec(tr, d), _fspec((1, d)), _rspec(tr, d)],
        out_specs=[_rspec(tr, d), _rspec(tr, d), _fspec((1, LANES)), _fspec((1, d))],
        out_shape=[jax.ShapeDtypeStruct((lp, d), F32), jax.ShapeDtypeStruct((lp, d), BF16),
                   jax.ShapeDtypeStruct((1, LANES), F32), jax.ShapeDtypeStruct((1, d), F32)],
        compiler_params=_cp("arbitrary"), name=name)(h1, t2, g, tgt)


def _hg_out(o_f, o_b, proj, gain, *, col_g, name):
    lp, w = o_f.shape
    tr = _row_tile(lp, w * (3 * 4 + 2))
    hh = w // HG_DK

    def body(of_ref, ob_ref, g_ref, gain_ref, y_ref):
        gv = g_ref[...]
        sg = gv * _sigmoid(gv)
        for h in range(hh):
            sl = slice(h * HG_DK, (h + 1) * HG_DK)
            o = of_ref[:, sl] + ob_ref[:, sl]
            r = lax.rsqrt(jnp.mean(o * o, axis=-1, keepdims=True) + EPS)
            y_ref[:, sl] = (o * r * gain_ref[:, sl] * sg[:, sl]).astype(BF16)

    return pl.pallas_call(
        body, grid=(lp // tr,),
        in_specs=[_rspec(tr, w), _rspec(tr, w), _rspec(tr, w, col_g // w), _fspec((1, w))],
        out_specs=_rspec(tr, w),
        out_shape=jax.ShapeDtypeStruct((lp, w), BF16),
        compiler_params=_cp("parallel"), name=name)(o_f, o_b, proj, gain)


def _hg_out_bwd(o_f, o_b, proj, gain, dy, *, col_g, name):
    lp, w = o_f.shape
    tr = _row_tile(lp, w * (5 * 4 + 2))
    hh = w // HG_DK

    def body(of_ref, ob_ref, g_ref, gain_ref, dy_ref, do_ref, dg_ref, dgain_ref):
        @pl.when(pl.program_id(0) == 0)
        def _():
            dgain_ref[...] = jnp.zeros_like(dgain_ref)

        for h in range(hh):
            sl = slice(h * HG_DK, (h + 1) * HG_DK)
            gv = g_ref[:, sl]
            s = _sigmoid(gv)
            sg = gv * s
            dsg = s + gv * s * (1.0 - s)
            o = of_ref[:, sl] + ob_ref[:, sl]
            r = lax.rsqrt(jnp.mean(o * o, axis=-1, keepdims=True) + EPS)
            on = o * r
            dyv = dy_ref[:, sl]
            gn = gain_ref[:, sl]
            dgain_ref[:, sl] += jnp.sum(dyv * on * sg, axis=0, keepdims=True)
            dg_ref[:, sl] = (dyv * on * gn * dsg).astype(BF16)
            don = dyv * gn * sg
            do_ref[:, sl] = r * (don - on * jnp.mean(don * on, axis=-1, keepdims=True))

    return pl.pallas_call(
        body, grid=(lp // tr,),
        in_specs=[_rspec(tr, w), _rspec(tr, w), _rspec(tr, w, col_g // w), _fspec((1, w)),
                  _rspec(tr, w)],
        out_specs=[_rspec(tr, w), _rspec(tr, w), _fspec((1, w))],
        out_shape=[jax.ShapeDtypeStruct((lp, w), F32), jax.ShapeDtypeStruct((lp, w), BF16),
                   jax.ShapeDtypeStruct((1, w), F32)],
        compiler_params=_cp("arbitrary"), name=name)(o_f, o_b, proj, gain, dy)


HG_ROWS = 128
HG_HALVES = (1, 2, 4, 8, 16, 32, 64)


def _hg_gates(zq, z, lbv):
    sq = _sigmoid(zq)
    s = _sigmoid(z)
    f = lbv + (1.0 - lbv) * s
    kk = (1.0 - lbv) * (1.0 - s)
    return zq * sq, sq, s, f, jnp.log(f), kk


def _block_cumsum(g, pos, suffix):
    x = g
    for k in HG_HALVES:
        if suffix:
            x = x + jnp.where(pos < HG_ROWS - k, pltpu.roll(x, HG_ROWS - k, 0), 0.0)
        else:
            x = x + jnp.where(pos >= k, pltpu.roll(x, k, 0), 0.0)
    return x


def _pair_levels(b, pos, reverse):
    out = []
    first = b
    for m in HG_HALVES:
        if m > 1:
            first = jnp.where((pos & (m - 1)) >= m // 2, pltpu.roll(first, m // 2, 0), first)
        nxt = pltpu.roll(first, HG_ROWS - m, 0)
        upper = (pos & (2 * m - 1)) >= m
        if reverse:
            eq = jnp.where(upper, 0.0, jnp.exp(b - nxt))
            ek = jnp.where(upper, jnp.exp(first - b), 0.0)
        else:
            eq = jnp.where(upper, jnp.exp(b - first), 0.0)
            ek = jnp.where(upper, 0.0, jnp.exp(nxt - b))
        out.append((eq, ek))
    return out


def _pair_masks(mask_ref):
    ri = lax.broadcasted_iota(jnp.int32, (HG_ROWS, HG_ROWS), 0)
    ci = lax.broadcasted_iota(jnp.int32, (HG_ROWS, HG_ROWS), 1)
    for i, m in enumerate(HG_HALVES):
        sh = m.bit_length()
        mask_ref[i] = jnp.where((ri >> sh) == (ci >> sh), 1.0, 0.0)


def _hg_scan_fwd(proj, lb, *, reverse, col_q, col_z, col_i, hh, name):
    lp = proj.shape[0]
    n_blocks = lp // HG_ROWS
    last = 0 if reverse else HG_ROWS - 1

    def body(q_ref, z_ref, i_ref, lb_ref, o_ref, st_ref, mask_ref):
        lbv = lb_ref[...]
        pos = lax.broadcasted_iota(jnp.int32, (HG_ROWS, 1), 0)
        ri = lax.broadcasted_iota(jnp.int32, (HG_ROWS, HG_ROWS), 0)
        ci = lax.broadcasted_iota(jnp.int32, (HG_ROWS, HG_ROWS), 1)
        _pair_masks(mask_ref)

        def block(bi, st):
            bb = (n_blocks - 1 - bi) if reverse else bi
            r0 = pl.multiple_of(bb * HG_ROWS, HG_ROWS)
            v16 = i_ref[pl.ds(r0, HG_ROWS), :].astype(BF16)
            qh, _, _, _, g, kk = _hg_gates(q_ref[pl.ds(r0, HG_ROWS), :],
                                           z_ref[pl.ds(r0, HG_ROWS), :], lbv)
            b = _block_cumsum(g, pos, reverse)
            bl = b[last:last + 1, :]
            qe = (qh * jnp.exp(b)).astype(BF16)
            kd = (kk * jnp.exp(bl - b)).astype(BF16)
            a = jnp.where(ri == ci, jnp.sum(qh * kk, axis=1, keepdims=True), 0.0)
            for i, (eq, ek) in enumerate(_pair_levels(b, pos, reverse)):
                a = a + mask_ref[i] * _nt((qh * eq).astype(BF16), (kk * ek).astype(BF16))
            st_ref[bb] = st
            o_ref[pl.ds(r0, HG_ROWS), :] = _nn(a.astype(BF16), v16) + _nt(qe, st.astype(BF16))
            return jnp.exp(bl) * st + _tn(v16, kd)

        lax.fori_loop(0, n_blocks, block, jnp.zeros((HG_DK, HG_DK), F32))

    cspec = lambda col: pl.BlockSpec((lp, HG_DK), lambda h: (0, col // HG_DK + h))
    return pl.pallas_call(
        body, grid=(hh,),
        in_specs=[cspec(col_q), cspec(col_z), cspec(col_i),
                  pl.BlockSpec((None, 1, HG_DK), lambda h: (h, 0, 0))],
        out_specs=[pl.BlockSpec((lp, HG_DK), lambda h: (0, h)),
                   pl.BlockSpec((None, n_blocks, HG_DK, HG_DK), lambda h: (h, 0, 0, 0))],
        out_shape=[jax.ShapeDtypeStruct((lp, hh * HG_DK), F32),
                   jax.ShapeDtypeStruct((hh, n_blocks, HG_DK, HG_DK), F32)],
        scratch_shapes=[pltpu.VMEM((len(HG_HALVES), HG_ROWS, HG_ROWS), F32)],
        compiler_params=_cp("parallel"), name=name)(proj, proj, proj, lb)


def _hg_scan_bwd(proj, lb, states, do, *, reverse, col_q, col_z, col_i, hh, name):
    lp = proj.shape[0]
    n_blocks = lp // HG_ROWS
    last = 0 if reverse else HG_ROWS - 1

    def body(q_ref, z_ref, i_ref, lb_ref, st_ref, do_ref, dq_ref, dz_ref, dv_ref, dlb_ref, mask_ref):
        lbv = lb_ref[...]
        pos = lax.broadcasted_iota(jnp.int32, (HG_ROWS, 1), 0)
        ri = lax.broadcasted_iota(jnp.int32, (HG_ROWS, HG_ROWS), 0)
        ci = lax.broadcasted_iota(jnp.int32, (HG_ROWS, HG_ROWS), 1)
        _pair_masks(mask_ref)

        def block(bi, carry):
            dst, dlb = carry
            bb = bi if reverse else (n_blocks - 1 - bi)
            r0 = pl.multiple_of(bb * HG_ROWS, HG_ROWS)
            zq = q_ref[pl.ds(r0, HG_ROWS), :]
            v16 = i_ref[pl.ds(r0, HG_ROWS), :].astype(BF16)
            do16 = do_ref[pl.ds(r0, HG_ROWS), :].astype(BF16)
            qh, sq, s, f, g, kk = _hg_gates(zq, z_ref[pl.ds(r0, HG_ROWS), :], lbv)
            b = _block_cumsum(g, pos, reverse)
            bl = b[last:last + 1, :]
            eb = jnp.exp(b)
            ebl = jnp.exp(bl - b)
            decay = jnp.exp(bl)
            qe16 = (qh * eb).astype(BF16)
            kd16 = (kk * ebl).astype(BF16)
            st = st_ref[bb]
            st16, dst16 = st.astype(BF16), dst.astype(BF16)
            same_row = ri == ci
            da = _nt(do16, v16)
            da_diag = jnp.sum(jnp.where(same_row, da, 0.0), axis=1, keepdims=True)
            dq_state = eb * _nn(do16, st16)
            dk_state = ebl * _nn(v16, dst16)
            dq = dq_state + da_diag * kk
            dk = dk_state + da_diag * qh
            dbl = (decay * jnp.sum(st * dst, axis=0, keepdims=True)
                   + jnp.sum(kk * dk_state, axis=0, keepdims=True))
            db = qh * dq_state - kk * dk_state + jnp.where(pos == last, dbl, 0.0)
            a = jnp.where(same_row, jnp.sum(qh * kk, axis=1, keepdims=True), 0.0)
            for i, (eq, ek) in enumerate(_pair_levels(b, pos, reverse)):
                same = mask_ref[i]
                q16, k16 = (qh * eq).astype(BF16), (kk * ek).astype(BF16)
                a = a + same * _nt(q16, k16)
                da16 = (same * da).astype(BF16)
                gq, gk = _nn(da16, k16), _tn(da16, q16)
                dq = dq + eq * gq
                dk = dk + ek * gk
                db = db + (q16.astype(F32) * gq - k16.astype(F32) * gk)
            dg = _block_cumsum(db, pos, not reverse)
            df = dg / f - dk
            dq_ref[pl.ds(r0, HG_ROWS), :] = dq * (sq + zq * sq * (1.0 - sq))
            dz_ref[pl.ds(r0, HG_ROWS), :] = df * (1.0 - lbv) * s * (1.0 - s)
            dv_ref[pl.ds(r0, HG_ROWS), :] = _nt(kd16, dst16) + _tn(a.astype(BF16), do16)
            return (decay * dst + _tn(do16, qe16),
                    dlb + jnp.sum(df * (1.0 - s), axis=0, keepdims=True))

        _, dlb = lax.fori_loop(0, n_blocks, block,
                               (jnp.zeros((HG_DK, HG_DK), F32), jnp.zeros((1, HG_DK), F32)))
        dlb_ref[...] = dlb

    cspec = lambda col: pl.BlockSpec((lp, HG_DK), lambda h: (0, col // HG_DK + h))
    ospec = pl.BlockSpec((lp, HG_DK), lambda h: (0, h))
    sds = jax.ShapeDtypeStruct((lp, hh * HG_DK), F32)
    return pl.pallas_call(
        body, grid=(hh,),
        in_specs=[cspec(col_q), cspec(col_z), cspec(col_i),
                  pl.BlockSpec((None, 1, HG_DK), lambda h: (h, 0, 0)),
                  pl.BlockSpec((None, n_blocks, HG_DK, HG_DK), lambda h: (h, 0, 0, 0)),
                  ospec],
        out_specs=[ospec, ospec, ospec, pl.BlockSpec((None, 1, HG_DK), lambda h: (h, 0, 0))],
        out_shape=[sds, sds, sds, jax.ShapeDtypeStruct((hh, 1, HG_DK), F32)],
        scratch_shapes=[pltpu.VMEM((len(HG_HALVES), HG_ROWS, HG_ROWS), F32)],
        compiler_params=_cp("parallel"), name=name)(proj, proj, proj, lb, states, do)


NA_HB = LANES // NA_HEAD_DIM
NA_G = 4
NA_U = NA_G + NA_WIN_H
NA_QN = NA_G * GRID_W
NA_KN = NA_U * GRID_W


def _na_table_index(pattern, a, j):
    if pattern == 0:
        return j - a + NA_WIN_H - 1 if j < NA_WIN_H else None
    if pattern == 2:
        return j - a - 1 if j >= NA_U - NA_WIN_H else None
    return j - a + NA_WIN_H // 2 - 1 if a <= j < a + NA_WIN_H else None


def _na_step_rows(pattern, t, rows):
    if pattern == 0:
        r0, us = 0, 0
    elif pattern == 2:
        r0, us = rows - NA_G, rows - NA_U
    else:
        r0 = NA_G * t
        us = r0 - NA_WIN_H // 2
    q0, k0 = N_META + GRID_W * r0, N_META + GRID_W * us
    if pattern == 1:
        q0, k0 = pl.multiple_of(q0, 16), pl.multiple_of(k0, 16)
    return q0, k0


def _na_fill_bias(tb_ref, bias_ref):
    neg = jnp.full((GRID_W, GRID_W), -1e30, F32)
    for h in range(NA_HB):
        for pattern in range(3):
            for a in range(NA_G):
                for j in range(NA_U):
                    idx = _na_table_index(pattern, a, j)
                    bias_ref[h, pattern, a * GRID_W:(a + 1) * GRID_W, j * GRID_W:(j + 1) * GRID_W] = (
                        neg if idx is None else tb_ref[h, idx])


def _na_steps(rows, step, carry):
    n_steps = rows // NA_G
    carry = step(0, 0, carry)
    carry = lax.fori_loop(1, n_steps - 1, functools.partial(step, 1), carry)
    return step(2, n_steps - 1, carry)


def _na_head_lanes():
    lane = lax.broadcasted_iota(jnp.int32, (1, LANES), 1)
    return [lane // NA_HEAD_DIM == h for h in range(NA_HB)]


def _na_only(mask, x):
    return jnp.where(mask, x, jnp.zeros_like(x))


def _na_fwd(proj, tb, *, n_tok, nh, name):
    lp = proj.shape[0]
    dh, hb = NA_HEAD_DIM, NA_HB
    naw = nh * dh
    rows = n_tok // GRID_W
    scale = dh ** -0.5

    def body(q_ref, k_ref, v_ref, tb_ref, o_ref, lse_ref, q16_ref, k16_ref, v16_ref, bias_ref):
        o_ref[...] = jnp.zeros_like(o_ref)
        lse_ref[...] = jnp.zeros_like(lse_ref)
        q16_ref[...] = q_ref[...].astype(BF16)
        k16_ref[...] = k_ref[...].astype(BF16)
        v16_ref[...] = v_ref[...].astype(BF16)
        _na_fill_bias(tb_ref, bias_ref)
        heads = _na_head_lanes()
        km = k16_ref[0:N_META, :]
        vm = v16_ref[0:N_META, :]
        qm = q16_ref[0:N_META, :]
        o_m = None
        for h in range(hb):
            s = _nt(_na_only(heads[h], qm), km) * scale
            m = jnp.max(s, axis=1, keepdims=True)
            p = jnp.exp(s - m)
            l = jnp.sum(p, axis=1, keepdims=True)
            o_h = _nn(p.astype(BF16), vm) / l
            o_m = o_h if o_m is None else jnp.where(heads[h], o_h, o_m)
            lse_ref[h, 0:N_META, :] = m + jnp.log(l)
        o_ref[0:N_META, :] = o_m

        def step(pattern, t, carry):
            q0, k0 = _na_step_rows(pattern, t, rows)
            q16 = q16_ref[pl.ds(q0, NA_QN), :]
            k16 = k16_ref[pl.ds(k0, NA_KN), :]
            v16 = v16_ref[pl.ds(k0, NA_KN), :]
            o = None
            for h in range(hb):
                q_h = _na_only(heads[h], q16)
                s = _nt(q_h, k16) * scale + bias_ref[h, pattern]
                sm = _nt(q_h, km) * scale
                m = jnp.maximum(jnp.max(s, axis=1, keepdims=True),
                                jnp.max(sm, axis=1, keepdims=True))
                p = jnp.exp(s - m)
                pm = jnp.exp(sm - m)
                l = jnp.sum(p, axis=1, keepdims=True) + jnp.sum(pm, axis=1, keepdims=True)
                o_h = (_nn(p.astype(BF16), v16) + _nn(pm.astype(BF16), vm)) / l
                o = o_h if o is None else jnp.where(heads[h], o_h, o)
                lse_ref[h, pl.ds(q0, NA_QN), :] = m + jnp.log(l)
            o_ref[pl.ds(q0, NA_QN), :] = o
            return carry

        _na_steps(rows, step, 0)

    cblk = lambda col: pl.BlockSpec((lp, LANES), lambda g: (0, col // LANES + g))
    return pl.pallas_call(
        body, grid=(nh // hb,),
        in_specs=[cblk(0), cblk(naw), cblk(2 * naw),
                  pl.BlockSpec((hb, 2 * NA_WIN_H - 1, GRID_W, GRID_W), lambda g: (g, 0, 0, 0))],
        out_specs=[cblk(0), pl.BlockSpec((hb, lp, 1), lambda g: (g, 0, 0))],
        out_shape=[jax.ShapeDtypeStruct((lp, naw), F32), jax.ShapeDtypeStruct((nh, lp, 1), F32)],
        scratch_shapes=[pltpu.VMEM((lp, LANES), BF16)] * 3 + [pltpu.VMEM((hb, 3, NA_QN, NA_KN), F32)],
        compiler_params=_cp("parallel"), name=name)(proj, proj, proj, tb)


def _na_bwd(proj, tb, o, lse, do, *, n_tok, nh, name):
    lp = proj.shape[0]
    dh, hb = NA_HEAD_DIM, NA_HB
    naw = nh * dh
    rows = n_tok // GRID_W
    scale = dh ** -0.5

    def body(q_ref, k_ref, v_ref, tb_ref, o_ref, lse_ref, do_ref, dq_ref, dk_ref, dv_ref, dtb_ref,
             q16_ref, k16_ref, v16_ref, bias_ref):
        dq_ref[...] = jnp.zeros_like(dq_ref)
        dk_ref[...] = jnp.zeros_like(dk_ref)
        dv_ref[...] = jnp.zeros_like(dv_ref)
        dtb_ref[...] = jnp.zeros_like(dtb_ref)
        q16_ref[...] = q_ref[...].astype(BF16)
        k16_ref[...] = k_ref[...].astype(BF16)
        v16_ref[...] = v_ref[...].astype(BF16)
        _na_fill_bias(tb_ref, bias_ref)
        heads = _na_head_lanes()
        km = k16_ref[0:N_META, :]
        vm = v16_ref[0:N_META, :]
        qm = q16_ref[0:N_META, :]
        dom = do_ref[0:N_META, :]
        prod = dom * o_ref[0:N_META, :]
        dq_m = None
        dkm0 = jnp.zeros((N_META, LANES), F32)
        dvm0 = jnp.zeros((N_META, LANES), F32)
        for h in range(hb):
            q_h = _na_only(heads[h], qm)
            do_h = _na_only(heads[h], dom).astype(BF16)
            p = jnp.exp(_nt(q_h, km) * scale - lse_ref[h, 0:N_META, :])
            delta = jnp.sum(_na_only(heads[h], prod), axis=1, keepdims=True)
            ds = (p * (_nt(do_h, vm) - delta)).astype(BF16)
            dq_h = _nn(ds, km) * scale
            dq_m = dq_h if dq_m is None else jnp.where(heads[h], dq_h, dq_m)
            dkm0 = dkm0 + _tn(ds, q_h) * scale
            dvm0 = dvm0 + _tn(p.astype(BF16), do_h)
        dq_ref[0:N_META, :] = dq_m

        def step(pattern, t, carry):
            dkm, dvm = carry
            q0, k0 = _na_step_rows(pattern, t, rows)
            q16 = q16_ref[pl.ds(q0, NA_QN), :]
            k16 = k16_ref[pl.ds(k0, NA_KN), :]
            v16 = v16_ref[pl.ds(k0, NA_KN), :]
            dov = do_ref[pl.ds(q0, NA_QN), :]
            prod = dov * o_ref[pl.ds(q0, NA_QN), :]
            dq = None
            dk = jnp.zeros((NA_KN, LANES), F32)
            dv = jnp.zeros((NA_KN, LANES), F32)
            for h in range(hb):
                q_h = _na_only(heads[h], q16)
                do_h = _na_only(heads[h], dov).astype(BF16)
                lse = lse_ref[h, pl.ds(q0, NA_QN), :]
                p = jnp.exp(_nt(q_h, k16) * scale + bias_ref[h, pattern] - lse)
                pm = jnp.exp(_nt(q_h, km) * scale - lse)
                delta = jnp.sum(_na_only(heads[h], prod), axis=1, keepdims=True)
                ds = p * (_nt(do_h, v16) - delta)
                dsm = (pm * (_nt(do_h, vm) - delta)).astype(BF16)
                ds16 = ds.astype(BF16)
                dq_h = (_nn(ds16, k16) + _nn(dsm, km)) * scale
                dq = dq_h if dq is None else jnp.where(heads[h], dq_h, dq)
                dk = dk + _tn(ds16, q_h) * scale
                dv = dv + _tn(p.astype(BF16), do_h)
                dkm = dkm + _tn(dsm, q_h) * scale
                dvm = dvm + _tn(pm.astype(BF16), do_h)
                for a in range(NA_G):
                    for j in range(NA_U):
                        idx = _na_table_index(pattern, a, j)
                        if idx is not None:
                            dtb_ref[h, idx] += ds[a * GRID_W:(a + 1) * GRID_W,
                                                  j * GRID_W:(j + 1) * GRID_W]
            dq_ref[pl.ds(q0, NA_QN), :] = dq
            dk_ref[pl.ds(k0, NA_KN), :] += dk
            dv_ref[pl.ds(k0, NA_KN), :] += dv
            return dkm, dvm

        dkm, dvm = _na_steps(rows, step, (dkm0, dvm0))
        dk_ref[0:N_META, :] += dkm
        dv_ref[0:N_META, :] += dvm

    cblk = lambda col: pl.BlockSpec((lp, LANES), lambda g: (0, col // LANES + g))
    tbs = pl.BlockSpec((hb, 2 * NA_WIN_H - 1, GRID_W, GRID_W), lambda g: (g, 0, 0, 0))
    sds = jax.ShapeDtypeStruct((lp, naw), F32)
    return pl.pallas_call(
        body, grid=(nh // hb,),
        in_specs=[cblk(0), cblk(naw), cblk(2 * naw), tbs, cblk(0),
                  pl.BlockSpec((hb, lp, 1), lambda g: (g, 0, 0)), cblk(0)],
        out_specs=[cblk(0), cblk(0), cblk(0), tbs],
        out_shape=[sds, sds, sds, jax.ShapeDtypeStruct(tb.shape, F32)],
        scratch_shapes=[pltpu.VMEM((lp, LANES), BF16)] * 3 + [pltpu.VMEM((hb, 3, NA_QN, NA_KN), F32)],
        compiler_params=_cp("parallel"), name=name)(proj, proj, proj, tb, o, lse, do)


def _rpb_onehot():
    c = np.arange(GRID_W)[:, None]
    w = np.arange(GRID_W)[None, :]
    cs = np.clip(c - NA_WIN_W // 2, 0, GRID_W - NA_WIN_W)
    in_win = (w >= cs) & (w < cs + NA_WIN_W)
    dc = np.clip(w - c, -(NA_WIN_W - 1), NA_WIN_W - 1) + NA_WIN_W - 1
    oh = np.zeros((LANES, GRID_W * GRID_W), np.float32)
    flat = np.arange(GRID_W * GRID_W).reshape(GRID_W, GRID_W)
    oh[dc[in_win], flat[in_win]] = 1.0
    neg = np.where(in_win, 0.0, -1e30).astype(np.float32).reshape(1, -1)
    return oh, neg


def _assemble_dproj(dq_na, dk_na, dv_na, dq_f, dq_b, dz_f, dz_b, dv_f, dv_b, dg, dgn, dgh, *, name):
    lp, naw = dq_na.shape
    hgw = dq_f.shape[1]
    d = dgn.shape[1]
    cols = 3 * naw + 5 * hgw + 2 * d
    tr = _row_tile(lp, 3 * naw * 4 + 6 * hgw * 4 + hgw * 2 + 2 * d * 2 + cols * 2)

    def body(nq_ref, nk_ref, nv_ref, qf_ref, qb_ref, zf_ref, zb_ref, vf_ref, vb_ref, g_ref, gn_ref,
             gh_ref, o_ref):
        o_ref[:, 0:naw] = nq_ref[...].astype(BF16)
        o_ref[:, naw:2 * naw] = nk_ref[...].astype(BF16)
        o_ref[:, 2 * naw:3 * naw] = nv_ref[...].astype(BF16)
        c0 = 3 * naw
        o_ref[:, c0:c0 + hgw] = (qf_ref[...] + qb_ref[...]).astype(BF16)
        o_ref[:, c0 + hgw:c0 + 2 * hgw] = zf_ref[...].astype(BF16)
        o_ref[:, c0 + 2 * hgw:c0 + 3 * hgw] = zb_ref[...].astype(BF16)
        o_ref[:, c0 + 3 * hgw:c0 + 4 * hgw] = (vf_ref[...] + vb_ref[...]).astype(BF16)
        o_ref[:, c0 + 4 * hgw:c0 + 5 * hgw] = g_ref[...]
        o_ref[:, c0 + 5 * hgw:c0 + 5 * hgw + d] = gn_ref[...]
        o_ref[:, c0 + 5 * hgw + d:] = gh_ref[...]

    hg, na = _rspec(tr, hgw), _rspec(tr, naw)
    return pl.pallas_call(
        body, grid=(lp // tr,),
        in_specs=[na, na, na, hg, hg, hg, hg, hg, hg, hg, _rspec(tr, d), _rspec(tr, d)],
        out_specs=_rspec(tr, cols),
        out_shape=jax.ShapeDtypeStruct((lp, cols), BF16),
        compiler_params=_cp("parallel"), name=name)(dq_na, dk_na, dv_na, dq_f, dq_b, dz_f, dz_b,
                                                    dv_f, dv_b, dg, dgn, dgh)


GROUP_STEPS = 8


def _group_tiles(rows, align):
    steps = GROUP_STEPS if all(r % (GROUP_STEPS * align) == 0 for r in rows) else 1
    return steps, [r // steps for r in rows]


def _adamw(ws, gs, ms, vs, *, name):
    n = len(ws)
    steps, trs = _group_tiles([w.shape[0] for w in ws], 8)

    def body(*refs):
        for i in range(n):
            w_ref, g_ref, m_ref, v_ref = refs[4 * i:4 * i + 4]
            d_ref, mo_ref, vo_ref = refs[4 * n + 3 * i:4 * n + 3 * i + 3]
            gv = g_ref[...]
            mn = ADAM_B1 * m_ref[...] + (1.0 - ADAM_B1) * gv
            vn = ADAM_B2 * v_ref[...] + (1.0 - ADAM_B2) * (gv * gv)
            m_hat = mn / (1.0 - ADAM_B1 ** ADAM_STEP)
            v_hat = vn / (1.0 - ADAM_B2 ** ADAM_STEP)
            d_ref[...] = -ADAM_LR * (m_hat / (jnp.sqrt(v_hat) + ADAM_EPS) + ADAM_WD * w_ref[...])
            mo_ref[...] = mn
            vo_ref[...] = vn

    specs = [_rspec(tr, w.shape[1]) for tr, w in zip(trs, ws)]
    out = pl.pallas_call(
        body, grid=(steps,),
        in_specs=[s for s in specs for _ in range(4)],
        out_specs=[s for s in specs for _ in range(3)],
        out_shape=[jax.ShapeDtypeStruct(w.shape, F32) for w in ws for _ in range(3)],
        compiler_params=_cp("parallel"), name=name)(*[a for q in zip(ws, gs, ms, vs) for a in q])
    return [tuple(out[3 * i:3 * i + 3]) for i in range(n)]


def _local_step(x, tgt, meta, first_weight, rest_weights, g_mix, g_mlp, g_fin, hg_gain, rpb, lb,
                early_grads=None, mid_grads=None, late_grad=None, rest_landed=None,
                last_grads=None):
    n_tok, d = x.shape
    hgw = hg_gain.shape[1]
    nh, hh = rpb.shape[0], hgw // HG_DK
    naw = nh * NA_HEAD_DIM
    l_real = N_META + n_tok
    lp = -(-l_real // ROW_ALIGN) * ROW_ALIGN
    col_qhg = 3 * naw
    col_zf, col_zb, col_i, col_g = (col_qhg + hgw, col_qhg + 2 * hgw, col_qhg + 3 * hgw,
                                    col_qhg + 4 * hgw)
    col_gate = col_qhg + 5 * hgw

    oh_np, neg_np = _rpb_onehot()
    oh = jnp.asarray(oh_np)
    rpb_p = jnp.pad(rpb.reshape(nh * (2 * NA_WIN_H - 1), 2 * NA_WIN_W - 1),
                    ((0, 0), (0, LANES - (2 * NA_WIN_W - 1))))
    tb = _matmul(rpb_p, oh, tm=rpb_p.shape[0], tn=512, tk=LANES, precision=HIGHEST,
                 name="rpb_expand")
    tb = (tb + jnp.asarray(neg_np)).reshape(nh, 2 * NA_WIN_H - 1, GRID_W, GRID_W)

    h0, a, tgt_p = _embed_norm(x, tgt, meta, g_mix, lp=lp, name="norm_mix")
    w_in = first_weight((a, tb))
    proj = _matmul(a, w_in, name="mm_in")
    o_na, lse = _na_fwd(proj, tb, n_tok=n_tok, nh=nh, name="na_fwd")
    lb_f = lb[0].reshape(hh, 1, HG_DK)
    lb_b = lb[1].reshape(hh, 1, HG_DK)
    scan_kw = dict(col_q=col_qhg, col_i=col_i, hh=hh)
    o_f, st_f = _hg_scan_fwd(proj, lb_f, reverse=False, col_z=col_zf, name="hg_scan_f", **scan_kw)
    token = rest_landed(o_f) if rest_landed else None
    lb_b_late = lb_b if token is None else lb_b + token[0:1, 0:1]
    o_b, st_b = _hg_scan_fwd(proj, lb_b_late, reverse=True, col_z=col_zb, name="hg_scan_b",
                             **scan_kw)
    o_hg = _hg_out(o_f, o_b, proj, hg_gain, col_g=col_g, name="hg_out")
    w_na, w_hg, w_o, w_up, w_down = rest_weights(o_hg)
    y_na = _matmul(o_na, w_na, name="mm_na_out", out_dtype=BF16)
    gates = ((proj, col_gate), (proj, col_gate + d))

    def mix_gates(acc, gn, gh, yn):
        return acc, _sigmoid(gn) * yn + _sigmoid(gh) * acc

    def mix_gates_bwd(dmix, gn, gh, yn, yh):
        sn, sh = _sigmoid(gn), _sigmoid(gh)
        return dmix * sn, dmix * sh, dmix * yn * sn * (1.0 - sn), dmix * yh * sh * (1.0 - sh)

    y_hg, mix = _matmul(o_hg, w_hg, name="mm_hg_out", epilogue=mix_gates,
                        tiles=(*gates, (y_na, 0)), out_dtypes=(BF16, BF16))
    t1 = _matmul(mix, w_o, name="mm_o")
    h1, mlp_in = _residual_norm(h0, t1, g_mlp, name="resid_norm_mlp")
    u, act = _matmul(mlp_in, w_up, name="mm_up", out_dtypes=(BF16, BF16),
                     epilogue=lambda acc: (acc, jnp.square(jnp.maximum(acc, 0.0))))
    t2 = _matmul(act, w_down, name="mm_down")
    dh2, dh2_16, loss, dg_fin = _final_loss(h1, t2, g_fin, tgt_p, n_tok=n_tok, name="final_loss")

    (du,) = _matmul(dh2_16, w_down, tb=True, name="mm_down_dx", tiles=((u, 0),),
                    out_dtypes=(BF16,),
                    epilogue=lambda acc, uv: (acc * 2.0 * jnp.maximum(uv, 0.0),))
    dw_down = _matmul(act, dh2_16, ta=True, name="mm_down_dw")
    dm = _matmul(du, w_up, tb=True, name="mm_up_dx")
    dw_up = _matmul(mlp_in, du, ta=True, name="mm_up_dw")
    dh1, dh1_16, dg_mlp = _rmsnorm_bwd_add(h1, g_mlp, dm, dh2, name="norm_mlp_bwd")
    dy_na, dy_hg, dgn, dgh = _matmul(dh1_16, w_o, tb=True, name="mm_o_dx", epilogue=mix_gates_bwd,
                                     tiles=(*gates, (y_na, 0), (y_hg, 0)), out_dtypes=(BF16,) * 4)
    dw_o = _matmul(mix, dh1_16, ta=True, name="mm_o_dw")
    do_na = _matmul(dy_na, w_na, tb=True, name="mm_na_out_dx")
    dw_na = _matmul(o_na, dy_na, ta=True, name="mm_na_out_dw")
    do_hg = _matmul(dy_hg, w_hg, tb=True, name="mm_hg_out_dx")
    dw_hg = _matmul(o_hg, dy_hg, ta=True, name="mm_hg_out_dw")
    token = early_grads([dw_na, dw_hg, dw_o, dw_up, dw_down]) if early_grads else None
    if token is not None:
        hg_gain = hg_gain + token[0:1, 0:1]
    d_o, dg_hg, d_gain = _hg_out_bwd(o_f, o_b, proj, hg_gain, do_hg, col_g=col_g, name="hg_out_bwd")
    dq_f, dz_f, dv_f, dlb_f = _hg_scan_bwd(proj, lb_f, st_f, d_o, reverse=False, col_z=col_zf,
                                           name="hg_scan_f_bwd", **scan_kw)
    token = mid_grads(dq_f) if mid_grads else None
    lb_b_late = lb_b if token is None else lb_b + token[0:1, 0:1]
    dq_b, dz_b, dv_b, dlb_b = _hg_scan_bwd(proj, lb_b_late, st_b, d_o, reverse=True, col_z=col_zb,
                                           name="hg_scan_b_bwd", **scan_kw)
    dq_na, dk_na, dv_na, dtb = _na_bwd(proj, tb, o_na, lse, do_na, n_tok=n_tok, nh=nh, name="na_bwd")
    dproj = _assemble_dproj(dq_na, dk_na, dv_na, dq_f, dq_b, dz_f, dz_b, dv_f, dv_b, dg_hg, dgn,
                            dgh, name="assemble_dproj")
    dw_in = _matmul(a, dproj, ta=True, name="mm_in_dw")
    token = late_grad(dw_in) if late_grad else None
    da = _matmul(dproj, w_in, tb=True, name="mm_in_dx", after=token)
    token = last_grads(da) if last_grads else None
    g_mix_late = g_mix if token is None else g_mix + token[0:1, 0:1]
    dx, dmeta, dg_mix = _rmsnorm_bwd_tokens(h0, g_mix_late, da, dh1, n_tok=n_tok,
                                            name="norm_mix_bwd")
    d_rpb = _matmul(dtb.reshape(nh * (2 * NA_WIN_H - 1), GRID_W * GRID_W), oh, tb=True,
                    tm=nh * (2 * NA_WIN_H - 1), tn=LANES, tk=1024, precision=HIGHEST,
                    name="rpb_reduce")
    d_lb = jnp.concatenate([dlb_f.reshape(1, hgw), dlb_b.reshape(1, hgw)], axis=0)
    return (loss, dx, dmeta, dw_in, dw_na, dw_hg, dw_o, dw_up, dw_down,
            dg_mix, dg_mlp, dg_fin, d_gain, d_rpb, d_lb)


N_CHIPS = 4
N_DEV = 8
ANY = pl.BlockSpec(memory_space=pl.ANY)


def _place():
    x, y, c = lax.axis_index("x"), lax.axis_index("y"), lax.axis_index("c")
    others = []
    for j in (1, 2, 3):
        tx = (1 - x) if (j >> 1) else x
        ty = (1 - y) if (j & 1) else y
        others.append((tx, ty))
    return x, y, c, others


def _piece(ref, axis, k, half, rh, cs):
    if axis == 1:
        return ref.at[pl.ds(pl.multiple_of(half * rh, 16), rh), pl.ds(pl.multiple_of(k * cs, LANES), cs)]
    return ref.at[pl.ds(pl.multiple_of(k * 2 * rh + half * rh, 16), rh), :]


def _cast_into_full(shards, axes, place, *, name):
    n = len(shards)
    steps, trs = _group_tiles([s.shape[0] for s in shards], 16)

    def body(p_ref, *refs):
        for i in range(n):
            refs[n + i][...] = refs[i][...].astype(BF16)

    def out_spec(tr, cs, axis):
        if axis == 1:
            return pl.BlockSpec((tr, cs), lambda i, p_ref: (i, p_ref[0]))
        return pl.BlockSpec((tr, cs), lambda i, p_ref: (p_ref[0] * steps + i, 0))

    return pl.pallas_call(
        body,
        grid_spec=pltpu.PrefetchScalarGridSpec(
            num_scalar_prefetch=1, grid=(steps,),
            in_specs=[pl.BlockSpec((tr, s.shape[1]), lambda i, p_ref: (i, 0))
                      for tr, s in zip(trs, shards)],
            out_specs=[out_spec(tr, s.shape[1], ax) for tr, s, ax in zip(trs, shards, axes)]),
        out_shape=[jax.ShapeDtypeStruct((s.shape[0], s.shape[1] * N_CHIPS) if ax == 1
                                        else (s.shape[0] * N_CHIPS, s.shape[1]), BF16)
                   for s, ax in zip(shards, axes)],
        compiler_params=_cp("parallel"), name=name)(place, *shards)


HBM_SPEC = pl.BlockSpec(memory_space=pltpu.HBM)
SEM_SPEC = pl.BlockSpec(memory_space=pltpu.SEMAPHORE)
SPLIT_COPY = pltpu.CompilerParams(has_side_effects=pltpu.SideEffectType.DATAFLOW_SIDE_EFFECTING)
TOKEN = jax.ShapeDtypeStruct((8, LANES), F32)


def _geo(fulls, axes):
    out = []
    for f, ax in zip(fulls, axes):
        r, cs = (f.shape[0], f.shape[1] // N_CHIPS) if ax == 1 else (f.shape[0] // N_CHIPS, f.shape[1])
        out.append((ax, r // 2, cs))
    return out


def _gather_copies(refs, geo, send_sems, recv_sems):
    x, y, c, others = _place()
    chip = 2 * x + y
    cps = []
    for i, (ax, rh, cs) in enumerate(geo):
        mine = _piece(refs[i], ax, chip, c, rh, cs)
        for j, (tx, ty) in enumerate(others):
            cps.append(pltpu.make_async_remote_copy(
                src_ref=mine, dst_ref=mine, send_sem=send_sems.at[3 * i + j],
                recv_sem=recv_sems.at[3 * i + j], device_id=(tx, ty, c), device_id_type=MESH))
    return cps


def _allgather_start(fulls, axes, after, *, name):
    n = len(fulls)
    geo = _geo(fulls, axes)

    def body(*refs):
        w_refs = refs[:n]
        send_sems, recv_sems = refs[n + 1], refs[n + 2]
        token = refs[2 * n + 3]
        for cp in _gather_copies(w_refs, geo, send_sems, recv_sems):
            cp.start()
        token[...] = jnp.zeros_like(token)

    out = pl.pallas_call(
        body, name=name,
        out_shape=(pltpu.SemaphoreType.DMA((3 * n,)), pltpu.SemaphoreType.DMA((3 * n,)),
                   *[pltpu.HBM(f.shape, f.dtype) for f in fulls], TOKEN),
        in_specs=[HBM_SPEC] * n + [ANY],
        out_specs=(SEM_SPEC, SEM_SPEC, *[HBM_SPEC] * n, pl.BlockSpec(memory_space=pltpu.VMEM)),
        input_output_aliases={i: 2 + i for i in range(n)},
        compiler_params=SPLIT_COPY,
    )(*[pltpu.with_memory_space_constraint(f, pltpu.HBM) for f in fulls], after)
    return out[0], out[1], list(out[2:2 + n]), out[2 + n]


def _allgather_wait(send_sems, recv_sems, fulls, axes, after, *, name):
    n = len(fulls)
    geo = _geo(fulls, axes)
    afters = tuple(after) if isinstance(after, (tuple, list)) else (after,)

    def body(*refs):
        w_refs = refs[:n]
        for cp in _gather_copies(w_refs, geo, refs[n], refs[n + 1]):
            cp.wait_send()
            cp.wait_recv()

    return list(pl.pallas_call(
        body, name=name,
        out_shape=[pltpu.HBM(f.shape, f.dtype) for f in fulls],
        in_specs=[HBM_SPEC] * n + [SEM_SPEC, SEM_SPEC] + [ANY] * len(afters),
        out_specs=[HBM_SPEC] * n,
        input_output_aliases={i: i for i in range(n)},
        compiler_params=SPLIT_COPY,
    )(*fulls, send_sems, recv_sems, *afters))


def _allgather_forward(fulls, axes, *, name):
    n = len(fulls)
    geo = _geo(fulls, axes)

    def body(*refs):
        o_refs = refs[n:2 * n]
        send_sems, recv_sems = refs[2 * n:]
        x, y, c, others = _place()

        def rcopy(i, j, half, to):
            ax, rh, cs = geo[i]
            ref = _piece(o_refs[i], ax, 2 * others[j][0] + others[j][1], half, rh, cs)
            return pltpu.make_async_remote_copy(
                src_ref=ref, dst_ref=ref, send_sem=send_sems.at[3 * i + j],
                recv_sem=recv_sems.at[3 * i + j], device_id=to, device_id_type=MESH)

        cps = [rcopy(i, j, c, (x, y, 1 - c)) for i in range(n) for j in range(3)]
        for cp in cps:
            cp.start()
        for i in range(n):
            for j in range(3):
                rcopy(i, j, 1 - c, (x, y, c)).wait_recv()
        for cp in cps:
            cp.wait_send()

    return list(pl.pallas_call(
        body, in_specs=[ANY] * n, out_specs=[ANY] * n,
        out_shape=[jax.ShapeDtypeStruct(f.shape, f.dtype) for f in fulls],
        input_output_aliases={i: i for i in range(n)},
        scratch_shapes=[pltpu.SemaphoreType.DMA((3 * n,)), pltpu.SemaphoreType.DMA((3 * n,))],
        name=name)(*fulls))


def _forward_copies(refs, geo, send_sems, recv_sems):
    x, y, c, others = _place()
    cps = []
    for i, (ax, rh, cs) in enumerate(geo):
        for j, (tx, ty) in enumerate(others):
            ref = _piece(refs[i], ax, 2 * tx + ty, c, rh, cs)
            cps.append(pltpu.make_async_remote_copy(
                src_ref=ref, dst_ref=ref, send_sem=send_sems.at[3 * i + j],
                recv_sem=recv_sems.at[3 * i + j], device_id=(x, y, 1 - c), device_id_type=MESH))
    return cps


def _allgather_forward_start(fulls, axes, *, name):
    n = len(fulls)
    geo = _geo(fulls, axes)

    def body(*refs):
        token = refs[2 * n + 2]
        for cp in _forward_copies(refs[:n], geo, refs[n], refs[n + 1]):
            cp.start()
        token[...] = jnp.zeros_like(token)

    out = pl.pallas_call(
        body, name=name,
        out_shape=(pltpu.SemaphoreType.DMA((3 * n,)), pltpu.SemaphoreType.DMA((3 * n,)),
                   *[pltpu.HBM(f.shape, f.dtype) for f in fulls], TOKEN),
        in_specs=[HBM_SPEC] * n,
        out_specs=(SEM_SPEC, SEM_SPEC, *[HBM_SPEC] * n, pl.BlockSpec(memory_space=pltpu.VMEM)),
        input_output_aliases={i: 2 + i for i in range(n)},
        compiler_params=SPLIT_COPY,
    )(*fulls)
    return out[0], out[1], list(out[2:2 + n]), out[2 + n]


def _allgather_forward_wait(send_sems, recv_sems, fulls, axes, after, *, name):
    n = len(fulls)
    geo = _geo(fulls, axes)

    def body(*refs):
        for cp in _forward_copies(refs[:n], geo, refs[n], refs[n + 1]):
            cp.wait_send()
            cp.wait_recv()

    return list(pl.pallas_call(
        body, name=name,
        out_shape=[pltpu.HBM(f.shape, f.dtype) for f in fulls],
        in_specs=[HBM_SPEC] * n + [SEM_SPEC, SEM_SPEC, ANY],
        out_specs=[HBM_SPEC] * n,
        input_output_aliases={i: i for i in range(n)},
        compiler_params=SPLIT_COPY,
    )(*fulls, send_sems, recv_sems, after))


def _chip_copies(blk_ref, land_ref, send_sems, recv_sems):
    x, y, c, others = _place()
    return [pltpu.make_async_remote_copy(
        src_ref=blk_ref, dst_ref=land_ref.at[2 * x + y], send_sem=send_sems.at[j],
        recv_sem=recv_sems.at[j], device_id=(tx, ty, c), device_id_type=MESH)
        for j, (tx, ty) in enumerate(others)]


def _chip_exchange_start(blk, *, name):
    land = pltpu.with_memory_space_constraint(lax.empty((N_CHIPS, *blk.shape), blk.dtype), pltpu.HBM)

    def body(blk_ref, land_ref, send_sems, recv_sems, blk_out, land_out, token):
        for cp in _chip_copies(blk_ref, land_ref, send_sems, recv_sems):
            cp.start()
        token[...] = jnp.zeros_like(token)

    return pl.pallas_call(
        body, name=name,
        out_shape=(pltpu.SemaphoreType.DMA((3,)), pltpu.SemaphoreType.DMA((3,)),
                   pltpu.HBM(blk.shape, blk.dtype), pltpu.HBM(land.shape, land.dtype), TOKEN),
        in_specs=[HBM_SPEC] * 2,
        out_specs=(SEM_SPEC, SEM_SPEC, HBM_SPEC, HBM_SPEC, pl.BlockSpec(memory_space=pltpu.VMEM)),
        input_output_aliases={0: 2, 1: 3},
        compiler_params=SPLIT_COPY,
    )(pltpu.with_memory_space_constraint(blk, pltpu.HBM), land)


def _chip_exchange_wait(send_sems, recv_sems, blk, land, after, *, name):
    def body(blk_ref, land_ref, send_sems, recv_sems, after_ref, blk_out, land_out):
        for cp in _chip_copies(blk_ref, land_ref, send_sems, recv_sems):
            cp.wait_send()
            cp.wait_recv()

    return pl.pallas_call(
        body, name=name,
        out_shape=[pltpu.HBM(blk.shape, blk.dtype), pltpu.HBM(land.shape, land.dtype)],
        in_specs=[HBM_SPEC] * 2 + [SEM_SPEC, SEM_SPEC, ANY],
        out_specs=[HBM_SPEC] * 2,
        input_output_aliases={0: 0, 1: 1},
        compiler_params=SPLIT_COPY,
    )(blk, land, send_sems, recv_sems, after)[1]


def _scatter_geo(parts, axes):
    out = []
    for p, ax in zip(parts, axes):
        _, rh, cols = p.shape
        out.append((ax, rh, cols // N_CHIPS if ax == 1 else cols))
    return out


def _scatter_copies(p_refs, q_refs, geo, send_sems, recv_sems):
    x, y, c, others = _place()
    chip = 2 * x + y
    cps = []
    for i, (ax, rh, cw) in enumerate(geo):
        for j, (tx, ty) in enumerate(others):
            k = 2 * tx + ty
            src = (p_refs[i].at[0, :, pl.ds(pl.multiple_of(k * cw, LANES), cw)] if ax == 1
                   else p_refs[i].at[k])
            cps.append(pltpu.make_async_remote_copy(
                src_ref=src, dst_ref=q_refs[i].at[chip], send_sem=send_sems.at[3 * i + j],
                recv_sem=recv_sems.at[3 * i + j], device_id=(tx, ty, c), device_id_type=MESH))
    return cps


def _scatter_start(parts, axes, *, name):
    n = len(parts)
    geo = _scatter_geo(parts, axes)
    slots = [pltpu.HBM((N_CHIPS, rh, cw), p.dtype) for p, (_, rh, cw) in zip(parts, geo)]

    def body(*refs):
        p_refs, q_refs = refs[:n], refs[n:2 * n]
        send_sems, recv_sems = refs[2 * n], refs[2 * n + 1]
        token = refs[4 * n + 2]
        for cp in _scatter_copies(p_refs, q_refs, geo, send_sems, recv_sems):
            cp.start()
        token[...] = jnp.zeros_like(token)

    land = [pltpu.with_memory_space_constraint(lax.empty(s.inner_aval.shape, s.inner_aval.dtype), pltpu.HBM)
            for s in slots]
    out = pl.pallas_call(
        body, name=name,
        out_shape=(pltpu.SemaphoreType.DMA((3 * n,)), pltpu.SemaphoreType.DMA((3 * n,)),
                   *[pltpu.HBM(p.shape, p.dtype) for p in parts], *slots, TOKEN),
        in_specs=[HBM_SPEC] * (2 * n),
        out_specs=(SEM_SPEC, SEM_SPEC, *[HBM_SPEC] * (2 * n), pl.BlockSpec(memory_space=pltpu.VMEM)),
        input_output_aliases={i: 2 + i for i in range(2 * n)},
        compiler_params=SPLIT_COPY,
    )(*[pltpu.with_memory_space_constraint(p, pltpu.HBM) for p in parts], *land)
    return out[0], out[1], list(out[2:2 + n]), list(out[2 + n:2 + 2 * n]), out[2 + 2 * n]


def _scatter_wait(send_sems, recv_sems, parts, slots, axes, after, *, name):
    n = len(parts)
    geo = _scatter_geo(parts, axes)

    def body(*refs):
        p_refs, q_refs = refs[:n], refs[n:2 * n]
        for cp in _scatter_copies(p_refs, q_refs, geo, refs[2 * n], refs[2 * n + 1]):
            cp.wait_send()
            cp.wait_recv()

    out = pl.pallas_call(
        body, name=name,
        out_shape=[pltpu.HBM(a.shape, a.dtype) for a in (*parts, *slots)],
        in_specs=[HBM_SPEC] * (2 * n) + [SEM_SPEC, SEM_SPEC, ANY],
        out_specs=[HBM_SPEC] * (2 * n),
        input_output_aliases={i: i for i in range(2 * n)},
        compiler_params=SPLIT_COPY,
    )(*parts, *slots, send_sems, recv_sems, after)
    return list(out[:n]), list(out[n:])


def _sibling_swap(grads, *, name):
    n = len(grads)
    out_shape = [jax.ShapeDtypeStruct((g.shape[0], g.shape[1] // 2, g.shape[2]), g.dtype)
                 for g in grads]

    def body(*refs):
        g_refs, o_refs = refs[:n], refs[n:2 * n]
        send_sems, recv_sems = refs[2 * n:]
        x, y, c, _ = _place()
        cps = []
        for i in range(n):
            rh = grads[i].shape[1] // 2
            src = g_refs[i].at[:, pl.ds(pl.multiple_of((1 - c) * rh, 16), rh), :]
            cp = pltpu.make_async_remote_copy(
                src_ref=src, dst_ref=o_refs[i], send_sem=send_sems.at[i], recv_sem=recv_sems.at[i],
                device_id=(x, y, 1 - c), device_id_type=MESH)
            cp.start()
            cps.append(cp)
        for cp in cps:
            cp.wait()

    return pl.pallas_call(
        body, in_specs=[ANY] * n, out_specs=[ANY] * n, out_shape=out_shape,
        scratch_shapes=[pltpu.SemaphoreType.DMA((n,)), pltpu.SemaphoreType.DMA((n,))],
        name=name)(*grads)


def _swap_copies(g_refs, r_refs, shapes, send_sems, recv_sems):
    x, y, c, _ = _place()
    cps = []
    for i, shape in enumerate(shapes):
        rh = shape[1] // 2
        src = g_refs[i].at[:, pl.ds(pl.multiple_of((1 - c) * rh, 16), rh), :]
        cps.append(pltpu.make_async_remote_copy(
            src_ref=src, dst_ref=r_refs[i], send_sem=send_sems.at[i], recv_sem=recv_sems.at[i],
            device_id=(x, y, 1 - c), device_id_type=MESH))
    return cps


def _sibling_swap_start(grads, *, name):
    n = len(grads)
    shapes = [g.shape for g in grads]
    lands = [pltpu.HBM((s[0], s[1] // 2, s[2]), g.dtype) for s, g in zip(shapes, grads)]

    def body(*refs):
        g_refs, r_refs = refs[:n], refs[n:2 * n]
        token = refs[4 * n + 2]
        for cp in _swap_copies(g_refs, r_refs, shapes, refs[2 * n], refs[2 * n + 1]):
            cp.start()
        token[...] = jnp.zeros_like(token)

    land = [pltpu.with_memory_space_constraint(lax.empty(s.inner_aval.shape, s.inner_aval.dtype), pltpu.HBM)
            for s in lands]
    out = pl.pallas_call(
        body, name=name,
        out_shape=(pltpu.SemaphoreType.DMA((n,)), pltpu.SemaphoreType.DMA((n,)),
                   *[pltpu.HBM(g.shape, g.dtype) for g in grads], *lands, TOKEN),
        in_specs=[HBM_SPEC] * (2 * n),
        out_specs=(SEM_SPEC, SEM_SPEC, *[HBM_SPEC] * (2 * n), pl.BlockSpec(memory_space=pltpu.VMEM)),
        input_output_aliases={i: 2 + i for i in range(2 * n)},
        compiler_params=SPLIT_COPY,
    )(*[pltpu.with_memory_space_constraint(g, pltpu.HBM) for g in grads], *land)
    return out[0], out[1], list(out[2:2 + n]), list(out[2 + n:2 + 2 * n]), out[2 + 2 * n]


def _sibling_swap_wait(send_sems, recv_sems, grads, lands, after, *, name):
    n = len(grads)
    shapes = [g.shape for g in grads]

    def body(*refs):
        g_refs, r_refs = refs[:n], refs[n:2 * n]
        for cp in _swap_copies(g_refs, r_refs, shapes, refs[2 * n], refs[2 * n + 1]):
            cp.wait_send()
            cp.wait_recv()

    out = pl.pallas_call(
        body, name=name,
        out_shape=[pltpu.HBM(a.shape, a.dtype) for a in (*grads, *lands)],
        in_specs=[HBM_SPEC] * (2 * n) + [SEM_SPEC, SEM_SPEC, ANY],
        out_specs=[HBM_SPEC] * (2 * n),
        input_output_aliases={i: i for i in range(2 * n)},
        compiler_params=SPLIT_COPY,
    )(*grads, *lands, send_sems, recv_sems, after)
    return list(out[:n]), list(out[n:])


def _pair_add(g3s, rxs, place, *, out_dtype, name):
    n = len(g3s)
    steps, trs = _group_tiles([g.shape[1] // 2 for g in g3s], 16)

    def body(p_ref, *refs):
        for i in range(n):
            refs[2 * n + i][...] = (refs[2 * i][...] + refs[2 * i + 1][...]).astype(out_dtype)

    in_specs, out_specs = [], []
    for g, tr in zip(g3s, trs):
        blk = (g.shape[0], tr, g.shape[2])
        in_specs += [pl.BlockSpec(blk, lambda i, p_ref: (0, p_ref[1] * steps + i, 0)),
                     pl.BlockSpec(blk, lambda i, p_ref: (0, i, 0))]
        out_specs.append(pl.BlockSpec(blk, lambda i, p_ref: (0, i, 0)))
    return pl.pallas_call(
        body,
        grid_spec=pltpu.PrefetchScalarGridSpec(
            num_scalar_prefetch=1, grid=(steps,), in_specs=in_specs, out_specs=out_specs),
        out_shape=[jax.ShapeDtypeStruct((g.shape[0], g.shape[1] // 2, g.shape[2]), out_dtype)
                   for g in g3s],
        compiler_params=_cp("parallel"), name=name)(place, *[a for q in zip(g3s, rxs) for a in q])


def _sum_slots(q, *, name):
    ns, rows, cols = q.shape
    tr = next(t for t in (128, 64, 32, 16, 8) if rows % t == 0)

    def body(q_ref, o_ref):
        acc = q_ref[0].astype(F32)
        for k in range(1, ns):
            acc = acc + q_ref[k].astype(F32)
        o_ref[...] = acc

    return pl.pallas_call(
        body, grid=(rows // tr,),
        in_specs=[pl.BlockSpec((ns, tr, cols), lambda i: (0, i, 0))],
        out_specs=_rspec(tr, cols),
        out_shape=jax.ShapeDtypeStruct((rows, cols), F32),
        compiler_params=_cp("parallel"), name=name)(q)


def _sum_chips(qs, ps, place, axes, *, name):
    n = len(qs)
    per = N_CHIPS + 1
    steps, trs = _group_tiles([q.shape[1] for q in qs], 16)

    def body(p_ref, *refs):
        chip = p_ref[0]
        for i in range(n):
            q_refs, own_ref = refs[per * i:per * i + N_CHIPS], refs[per * i + N_CHIPS]
            acc = jnp.where(chip == 0, own_ref[...], q_refs[0][...]).astype(F32)
            for k in range(1, N_CHIPS):
                acc = acc + jnp.where(chip == k, own_ref[...], q_refs[k][...]).astype(F32)
            refs[per * n + i][...] = acc

    def slot_spec(k, tr, cw):
        return pl.BlockSpec((None, tr, cw),
                            lambda i, p_ref: (jnp.where(p_ref[0] == k, (k + 1) % N_CHIPS, k), i, 0))

    in_specs, out_specs, operands = [], [], []
    for q, p, ax, tr in zip(qs, ps, axes, trs):
        cw = q.shape[2]
        in_specs += [slot_spec(k, tr, cw) for k in range(N_CHIPS)]
        in_specs.append(pl.BlockSpec((None, tr, cw), (lambda i, p_ref: (0, i, p_ref[0])) if ax == 1
                                     else (lambda i, p_ref: (p_ref[0], i, 0))))
        out_specs.append(pl.BlockSpec((tr, cw), lambda i, p_ref: (p_ref[1] * steps + i, 0)))
        operands += [q] * N_CHIPS + [p]
    return pl.pallas_call(
        body,
        grid_spec=pltpu.PrefetchScalarGridSpec(
            num_scalar_prefetch=1, grid=(steps,), in_specs=in_specs, out_specs=out_specs),
        out_shape=[jax.ShapeDtypeStruct((2 * q.shape[1], q.shape[2]), F32) for q in qs],
        compiler_params=_cp("parallel"), name=name)(place, *operands)


def _sibling_share(shards, *, name):
    n = len(shards)

    def body(*refs):
        o_refs = refs[n:2 * n]
        send_sems, recv_sems = refs[2 * n:]
        x, y, c, _ = _place()
        cps = []
        for i in range(n):
            rh = shards[i].shape[0] // 2
            mine = o_refs[i].at[pl.ds(pl.multiple_of(c * rh, 8), rh), :]
            cp = pltpu.make_async_remote_copy(
                src_ref=mine, dst_ref=mine, send_sem=send_sems.at[i], recv_sem=recv_sems.at[i],
                device_id=(x, y, 1 - c), device_id_type=MESH)
            cp.start()
            cps.append(cp)
        for i in range(n):
            rh = shards[i].shape[0] // 2
            theirs = o_refs[i].at[pl.ds(pl.multiple_of((1 - c) * rh, 8), rh), :]
            pltpu.make_async_remote_copy(
                src_ref=theirs, dst_ref=theirs, send_sem=send_sems.at[i], recv_sem=recv_sems.at[i],
                device_id=(x, y, c), device_id_type=MESH).wait_recv()
        for cp in cps:
            cp.wait_send()

    return pl.pallas_call(
        body, in_specs=[ANY] * n, out_specs=[ANY] * n,
        out_shape=[jax.ShapeDtypeStruct(h.shape, h.dtype) for h in shards],
        input_output_aliases={i: i for i in range(n)},
        scratch_shapes=[pltpu.SemaphoreType.DMA((n,)), pltpu.SemaphoreType.DMA((n,))],
        name=name)(*shards)


def _gather_all(blk, *, name, after=None, shares=()):
    rows, cols = blk.shape
    extra = [] if after is None else [after]
    n_s = len(shares)

    def body(x_ref, *refs):
        s_refs = refs[len(extra) + n_s + 1:len(extra) + 2 * n_s + 1]
        out_ref = refs[len(extra) + n_s]
        send_sems, recv_sems, local_sem, s_send, s_recv = refs[len(extra) + 2 * n_s + 1:]
        x, y, c = lax.axis_index("x"), lax.axis_index("y"), lax.axis_index("c")
        me = 4 * x + 2 * y + c
        mine = pltpu.make_async_copy(x_ref, out_ref.at[me], local_sem)
        mine.start()
        cps = []
        for i in range(n_s):
            rh = shares[i].shape[0] // 2
            half = s_refs[i].at[pl.ds(pl.multiple_of(c * rh, 8), rh), :]
            cp = pltpu.make_async_remote_copy(
                src_ref=half, dst_ref=half, send_sem=s_send.at[i], recv_sem=s_recv.at[i],
                device_id=(x, y, 1 - c), device_id_type=MESH)
            cp.start()
            cps.append(cp)
        for k in range(1, N_DEV):
            tx = (1 - x) if (k >> 2) & 1 else x
            ty = (1 - y) if (k >> 1) & 1 else y
            tc = (1 - c) if k & 1 else c
            cp = pltpu.make_async_remote_copy(
                src_ref=x_ref, dst_ref=out_ref.at[me], send_sem=send_sems.at[k - 1],
                recv_sem=recv_sems.at[k - 1], device_id=(tx, ty, tc), device_id_type=MESH)
            cp.start()
            cps.append(cp)
        for k in range(1, N_DEV):
            tx = (1 - x) if (k >> 2) & 1 else x
            ty = (1 - y) if (k >> 1) & 1 else y
            tc = (1 - c) if k & 1 else c
            got = out_ref.at[4 * tx + 2 * ty + tc]
            pltpu.make_async_remote_copy(
                src_ref=got, dst_ref=got, send_sem=send_sems.at[k - 1], recv_sem=recv_sems.at[k - 1],
                device_id=(x, y, c), device_id_type=MESH).wait_recv()
        for i in range(n_s):
            rh = shares[i].shape[0] // 2
            theirs = s_refs[i].at[pl.ds(pl.multiple_of((1 - c) * rh, 8), rh), :]
            pltpu.make_async_remote_copy(
                src_ref=theirs, dst_ref=theirs, send_sem=s_send.at[i], recv_sem=s_recv.at[i],
                device_id=(x, y, c), device_id_type=MESH).wait_recv()
        for cp in cps:
            cp.wait_send()
        mine.wait()

    vm = pl.BlockSpec(memory_space=pltpu.VMEM)
    out = pl.pallas_call(
        body, in_specs=[vm] + [ANY] * (len(extra) + n_s), out_specs=[vm] + [ANY] * n_s,
        out_shape=[jax.ShapeDtypeStruct((N_DEV, rows, cols), blk.dtype)]
        + [jax.ShapeDtypeStruct(s.shape, s.dtype) for s in shares],
        input_output_aliases={1 + len(extra) + i: 1 + i for i in range(n_s)},
        scratch_shapes=[pltpu.SemaphoreType.DMA((N_DEV - 1,)), pltpu.SemaphoreType.DMA((N_DEV - 1,)),
                        pltpu.SemaphoreType.DMA, pltpu.SemaphoreType.DMA((max(n_s, 1),)),
                        pltpu.SemaphoreType.DMA((max(n_s, 1),))],
        name=name)(blk, *extra, *shares)
    return (out[0], *out[1:]) if n_s else out[0]


def _as_rows(a):
    flat = a.reshape(-1)
    n = flat.shape[0]
    rows = -(-n // (8 * LANES)) * 8
    return jnp.pad(flat, (0, rows * LANES - n)).reshape(rows, LANES)


def _from_rows(p, shape):
    n = int(np.prod(shape))
    return p.reshape(-1)[:n].reshape(shape)


WEIGHT_AXES = (1, 1, 1, 0, 1, 0)
WIRE = BF16


def kernel(x, meta_tokens, w_in, w_na_out, w_hg_out, w_o, w_up, w_down, norm_mix, norm_mlp, norm_final, hg_norm, na_rpb, hg_lb_logits, loss_target, m_meta_tokens, m_w_in, m_w_na_out, m_w_hg_out, m_w_o, m_w_up, m_w_down, m_norm_mix, m_norm_mlp, m_norm_final, m_hg_norm, m_na_rpb, m_hg_lb_logits, v_meta_tokens, v_w_in, v_w_na_out, v_w_hg_out, v_w_o, v_w_up, v_w_down, v_norm_mix, v_norm_mlp, v_norm_final, v_hg_norm, v_na_rpb, v_hg_lb_logits):
    xi, yi, ci = lax.axis_index("x"), lax.axis_index("y"), lax.axis_index("c")
    chip = 2 * xi + yi
    d = x.shape[-1]
    dshard = meta_tokens.shape[1]
    hgw = hg_norm.shape[1]
    lbs = hg_lb_logits.shape[2]
    big = [w_in[0], w_na_out[0], w_hg_out[0], w_o[0], w_up[0], w_down[0]]
    big_m = [m_w_in[0], m_w_na_out[0], m_w_hg_out[0], m_w_o[0], m_w_up[0], m_w_down[0]]
    big_v = [v_w_in[0], v_w_na_out[0], v_w_hg_out[0], v_w_o[0], v_w_up[0], v_w_down[0]]

    place = jnp.stack([chip, ci]).astype(jnp.int32)
    in_axes, rest_axes = WEIGHT_AXES[:1], WEIGHT_AXES[1:]
    own_w = _cast_into_full(big, WEIGHT_AXES, place, name="cast_shards")
    small_in = jnp.concatenate([_as_rows(meta_tokens), _as_rows(hg_lb_logits)], axis=0)
    sm_send, sm_recv, sm_blk, sm_land, sm_token = _chip_exchange_start(small_in,
                                                                       name="small_params_start")
    in_send, in_recv, in_bufs, in_token = _allgather_start(own_w[:1], in_axes, sm_token,
                                                           name="weight_allgather_in_start")
    ag_send, ag_recv, ag_bufs, ag_token = _allgather_start(own_w[1:], rest_axes, in_token,
                                                           name="weight_allgather_rest_start")
    sm_land = _chip_exchange_wait(sm_send, sm_recv, sm_blk, sm_land, ag_token,
                                  name="small_params_wait")
    small_all = lax.dynamic_update_slice(sm_land, small_in[None], (chip, 0, 0))
    forward = {}

    def first_weight(after):
        got = _allgather_wait(in_send, in_recv, in_bufs, in_axes, after,
                              name="weight_allgather_in_wait")
        return _allgather_forward(got, in_axes, name="weight_allgather_in_forward")[0]

    def rest_landed(after):
        got = _allgather_wait(ag_send, ag_recv, ag_bufs, rest_axes, after,
                              name="weight_allgather_rest_wait")
        send, recv, bufs, token = _allgather_forward_start(
            got, rest_axes, name="weight_allgather_rest_forward_start")
        forward["rest"] = (send, recv, bufs)
        return token

    def rest_weights(after):
        return _allgather_forward_wait(*forward["rest"], rest_axes, after,
                                       name="weight_allgather_rest_forward_wait")

    n_meta_rows = N_META * dshard // LANES
    meta_full = (small_all[:, :n_meta_rows].reshape(N_CHIPS, N_META, dshard)
                 .transpose(1, 0, 2).reshape(N_META, d))
    lbl_full = (small_all[:, n_meta_rows:].reshape(N_CHIPS, -1)[:, :4 * lbs]
                .reshape(N_CHIPS, 2, 2, lbs).transpose(1, 2, 0, 3).reshape(2, 2, N_CHIPS * lbs))
    lb = jax.nn.softmax(lbl_full, axis=1)[:, 0]

    def by_chip(dws, axes):
        return [g.reshape(1, *g.shape) if ax == 1
                else g.reshape(N_CHIPS, g.shape[0] // N_CHIPS, g.shape[1]) for g, ax in zip(dws, axes)]

    flying = {}

    def scatter(tag, axes, g3, rx):
        parts = _pair_add(g3, rx, place, out_dtype=WIRE, name=f"grad_pair_add_{tag}")
        send, recv, parts, slots, token = _scatter_start(parts, axes,
                                                         name=f"grad_scatter_{tag}_start")
        flying[tag] = (send, recv, parts, slots)
        return token

    def swap(tag, axes):
        def start(dws):
            send, recv, g3, lands, token = _sibling_swap_start(
                by_chip(dws, axes), name=f"grad_sibling_swap_{tag}_start")
            flying["swap_" + tag] = (send, recv, g3, lands)
            return token

        def finish(after):
            g3, rx = _sibling_swap_wait(*flying["swap_" + tag], after,
                                        name=f"grad_sibling_swap_{tag}_wait")
            return scatter(tag, axes, g3, rx)
        return start, finish

    swap_rest, scatter_rest = swap("rest", rest_axes)
    swap_in, scatter_in = swap("in", in_axes)

    def landed(tag, axes, after):
        return _scatter_wait(*flying[tag], axes, after, name=f"grad_scatter_{tag}_wait")

    (loss, dx, dmeta, *_, dg_mix, dg_mlp, dg_fin, d_gain, d_rpb, d_lb) = _local_step(
        x[0], loss_target[0], meta_full, first_weight, rest_weights, norm_mix, norm_mlp,
        norm_final.reshape(1, d), hg_norm, na_rpb[0], lb, swap_rest, scatter_rest,
        lambda dw_in: swap_in([dw_in]), rest_landed, scatter_in)

    parts_rest, slots_rest = landed("rest", rest_axes, dx)
    g_rest = _sibling_share(_sum_chips(slots_rest, parts_rest, place, rest_axes,
                                       name="grad_sum_chips_rest"),
                            name="grad_sibling_share_rest")
    out_rest = _adamw(big[1:], g_rest, big_m[1:], big_v[1:], name="adamw_rest")
    parts_in, slots_in = landed("in", in_axes, out_rest[-1][0])
    half_in = _sum_chips(slots_in, parts_in, place, in_axes, name="grad_sum_chips_in")

    d_rpb_c = d_rpb[:, :2 * NA_WIN_W - 1]
    small_g = [dmeta, dg_mix, dg_mlp, dg_fin, d_gain, d_rpb_c, d_lb, loss]
    packed = jnp.concatenate([_as_rows(a) for a in small_g], axis=0)
    gathered, g_in = _gather_all(packed, shares=half_in, name="gather_small_grads")
    g_big = [g_in] + list(g_rest)
    total = _sum_slots(gathered, name="sum_small_grads")
    offs = np.cumsum([0] + [_as_rows(a).shape[0] for a in small_g])
    take = lambda i, shape: _from_rows(total[offs[i]:offs[i + 1]], shape)
    g_meta_full = take(0, (N_META, d))
    g_norm_mix, g_norm_mlp = take(1, (1, d)), take(2, (1, d))
    g_norm_final = take(3, (d,))
    g_hg_norm = take(4, (1, hgw))
    g_rpb = take(5, na_rpb.shape)
    g_lb = take(6, (2, hgw))
    loss_total = take(7, (1, LANES))[0, 0]
    g_meta = lax.dynamic_slice_in_dim(g_meta_full, chip * dshard, dshard, axis=1)
    dl0 = lb * (1.0 - lb) * g_lb
    g_lbl_full = jnp.stack([dl0, -dl0], axis=1)
    g_lbl = lax.dynamic_slice_in_dim(g_lbl_full, chip * lbs, lbs, axis=2)

    big_out = _adamw(big[:1], [g_in], big_m[:1], big_v[:1], name="adamw_in") + out_rest
    small_w = [meta_tokens, norm_mix, norm_mlp, norm_final, hg_norm, na_rpb, hg_lb_logits]
    small_gr = [g_meta, g_norm_mix, g_norm_mlp, g_norm_final, g_hg_norm, g_rpb, g_lbl]
    small_m = [m_meta_tokens, m_norm_mix, m_norm_mlp, m_norm_final, m_hg_norm, m_na_rpb, m_hg_lb_logits]
    small_v = [v_meta_tokens, v_norm_mix, v_norm_mlp, v_norm_final, v_hg_norm, v_na_rpb, v_hg_lb_logits]
    pk = lambda lst: jnp.concatenate([_as_rows(a) for a in lst], axis=0)
    ((sd, sm, sv),) = _adamw([pk(small_w)], [pk(small_gr)], [pk(small_m)], [pk(small_v)],
                             name="adamw_small")
    soffs = np.cumsum([0] + [_as_rows(a).shape[0] for a in small_w])
    unpk = lambda p: [_from_rows(p[soffs[i]:soffs[i + 1]], small_w[i].shape) for i in range(len(small_w))]
    sd, sm, sv = unpk(sd), unpk(sm), unpk(sv)

    def order(bigs, smalls):
        return [smalls[0]] + [b.reshape(1, *b.shape) for b in bigs] + smalls[1:]

    grads = order(g_big, small_gr)
    deltas = order([o[0] for o in big_out], sd)
    new_m = order([o[1] for o in big_out], sm)
    new_v = order([o[2] for o in big_out], sv)
    return (loss_total, dx.reshape(1, *dx.shape), *grads, *deltas, *new_m, *new_v)
```

```python
import functools

import numpy as np
import jax
import jax.numpy as jnp
from jax import lax
from jax.experimental import pallas as pl
from jax.experimental.pallas import tpu as pltpu

F32 = jnp.float32
BF16 = jnp.bfloat16
HIGHEST = lax.Precision.HIGHEST

GRID_W = 64
N_META = 16
EPS = 1e-6
NA_HEAD_DIM = 64
NA_WIN_H = 8
NA_WIN_W = 16
HG_DK = 128
HG_CHUNK = 16
LANES = 128
ROW_ALIGN = 128
VMEM_LIMIT = 48 * 1024 * 1024

ADAM_LR = 0.001
ADAM_B1 = 0.9
ADAM_B2 = 0.999
ADAM_EPS = 1e-08
ADAM_WD = 0.01
ADAM_STEP = 10

MESH = pl.DeviceIdType.MESH


def _cp(*sem):
    return pltpu.CompilerParams(dimension_semantics=sem, vmem_limit_bytes=VMEM_LIMIT)


def _sigmoid(x):
    return 0.5 * jnp.tanh(0.5 * x) + 0.5


def _dot(a, b, dims, precision=None):
    return lax.dot_general(a, b, (dims, ((), ())), preferred_element_type=F32, precision=precision)


def _nn(a, b, **kw):
    return _dot(a, b, ((1,), (0,)), **kw)


def _nt(a, b, **kw):
    return _dot(a, b, ((1,), (1,)), **kw)


def _tn(a, b, **kw):
    return _dot(a, b, ((0,), (0,)), **kw)


def _matmul(a, b, *, ta=False, tb=False, tm=None, tn=None, tk=None, out_dtype=F32, name,
            precision=None, after=None, epilogue=None, tiles=(), out_dtypes=None):
    extra = [] if after is None else [after]
    single = out_dtypes is None
    if single:
        out_dtypes = (out_dtype,)
    n_t, n_o = len(tiles), len(out_dtypes)
    if ta:
        kdim, m = a.shape
    else:
        m, kdim = a.shape
    if tb:
        n, k2 = b.shape
    else:
        k2, n = b.shape
    assert kdim == k2, (a.shape, b.shape, ta, tb)
    if tm is None:
        if ta:
            tm = next(t for t in (1024, 512, 256, 128, m) if m % t == 0)
        else:
            tm = m // 2 if (m // 2) % 16 == 0 and m > 512 else m
    if tn is None:
        wide = (1024,) if not ta and len(tiles) <= 1 else ()
        tn = next(t for t in (*wide, 512, 256, 128, n) if n % t == 0)
    if tk is None:
        tk = kdim if ta else next(t for t in (2048, 1024, 512, 256, 128, kdim) if kdim % t == 0)
    assert m % tm == 0 and n % tn == 0 and kdim % tk == 0, (m, n, kdim, tm, tn, tk)
    nk = kdim // tk
    op_dtype = F32 if precision is not None else BF16

    def body(a_ref, b_ref, *refs):
        t_refs = refs[:n_t]
        o_refs = refs[n_t + len(extra):n_t + len(extra) + n_o]
        av = a_ref[...].astype(op_dtype)
        bv = b_ref[...].astype(op_dtype)
        dims = ((0 if ta else 1,), (1 if tb else 0,))
        part = _dot(av, bv, dims, precision=precision)

        def finish(acc):
            outs = (acc,) if epilogue is None else epilogue(acc, *[t[...] for t in t_refs])
            for o_ref, val in zip(o_refs, outs):
                o_ref[...] = val.astype(o_ref.dtype)

        if nk == 1:
            finish(part)
            return
        acc_ref = refs[-1]
        kk = pl.program_id(2)

        @pl.when(kk == 0)
        def _():
            acc_ref[...] = part

        @pl.when((kk > 0) & (kk < nk - 1))
        def _():
            acc_ref[...] += part

        @pl.when(kk == nk - 1)
        def _():
            finish(acc_ref[...] + part)

    a_spec = (pl.BlockSpec((tk, tm), lambda i, j, k: (k, i)) if ta
              else pl.BlockSpec((tm, tk), lambda i, j, k: (i, k)))
    b_spec = (pl.BlockSpec((tn, tk), lambda i, j, k: (j, k)) if tb
              else pl.BlockSpec((tk, tn), lambda i, j, k: (k, j)))
    for _, off in tiles:
        assert off % tn == 0, (off, tn)
    t_specs = [pl.BlockSpec((tm, tn), functools.partial(lambda i, j, k, o: (i, o + j), o=off // tn))
               for _, off in tiles]
    o_spec = pl.BlockSpec((tm, tn), lambda i, j, k: (i, j))
    outs = pl.pallas_call(
        body,
        grid=(m // tm, n // tn, nk),
        in_specs=[a_spec, b_spec] + t_specs + [pl.BlockSpec(memory_space=pl.ANY)] * len(extra),
        out_specs=[o_spec] * n_o,
        out_shape=[jax.ShapeDtypeStruct((m, n), dt) for dt in out_dtypes],
        scratch_shapes=[pltpu.VMEM((tm, tn), F32)] if nk > 1 else [],
        compiler_params=_cp("parallel", "parallel", "arbitrary"),
        name=name,
    )(a, b, *[t for t, _ in tiles], *extra)
    return outs[0] if single else outs


def _rspec(tr, w, cb=0):
    return pl.BlockSpec((tr, w), lambda i: (i, cb))


def _fspec(shape):
    nd = len(shape)
    return pl.BlockSpec(shape, lambda i: (0,) * nd)


ROW_VMEM_BUDGET = 20 * 1024 * 1024
ROW_MIN_STEPS = 4


def _row_tile(lp, row_bytes):
    for k in range(ROW_MIN_STEPS, lp // 16 + 1):
        tr = lp // k
        if lp % k == 0 and tr % 16 == 0 and 2 * tr * row_bytes <= ROW_VMEM_BUDGET:
            return tr
    return lp


def _token_rows_copy(i, n_tiles, tr, n_tok, tok_ref, buf_ref, sem, *, to_tokens, start=True,
                     wait=True):
    assert n_tiles >= 2 and 0 < n_tok + N_META - (n_tiles - 1) * tr <= tr

    def run(tok_row, buf_row, count):
        tok = tok_ref.at[pl.ds(tok_row, count), :]
        buf = buf_ref.at[pl.ds(buf_row, count), :]
        cp = pltpu.make_async_copy(buf, tok, sem) if to_tokens else pltpu.make_async_copy(tok, buf, sem)
        if start:
            cp.start()
        if wait:
            cp.wait()

    @pl.when(i == 0)
    def _():
        run(0, N_META, tr - N_META)

    if n_tiles > 2:
        @pl.when((i > 0) & (i < n_tiles - 1))
        def _():
            run(pl.multiple_of(i * tr - N_META, 8), 0, tr)

    @pl.when(i == n_tiles - 1)
    def _():
        run((n_tiles - 1) * tr - N_META, 0, n_tok + N_META - (n_tiles - 1) * tr)


def _embed_norm(x, tgt, meta, g, *, lp, name):
    n_tok, d = x.shape
    tr = _row_tile(lp, d * (4 + 2 + 4))
    n_tiles = lp // tr

    def body(x_ref, tgt_ref, meta_ref, g_ref, h_ref, o_ref, tp_ref, buf_ref, tbuf_ref, sems):
        i = pl.program_id(0)
        buf_ref[...] = jnp.zeros_like(buf_ref)
        tbuf_ref[...] = jnp.zeros_like(tbuf_ref)

        @pl.when(i == 0)
        def _():
            buf_ref[0:N_META, :] = meta_ref[...]

        _token_rows_copy(i, n_tiles, tr, n_tok, tgt_ref, tbuf_ref, sems.at[1], to_tokens=False,
                         wait=False)
        _token_rows_copy(i, n_tiles, tr, n_tok, x_ref, buf_ref, sems.at[0], to_tokens=False)
        xv = buf_ref[...]
        h_ref[...] = xv
        r = lax.rsqrt(jnp.mean(xv * xv, axis=-1, keepdims=True) + EPS)
        o_ref[...] = (xv * r * g_ref[...]).astype(BF16)
        _token_rows_copy(i, n_tiles, tr, n_tok, tgt_ref, tbuf_ref, sems.at[1], to_tokens=False,
                         start=False)
        tp_ref[...] = tbuf_ref[...]

    return pl.pallas_call(
        body, grid=(n_tiles,),
        in_specs=[ANY, ANY, _fspec((N_META, d)), _fspec((1, d))],
        out_specs=[_rspec(tr, d), _rspec(tr, d), _rspec(tr, d)],
        out_shape=[jax.ShapeDtypeStruct((lp, d), F32), jax.ShapeDtypeStruct((lp, d), BF16),
                   jax.ShapeDtypeStruct((lp, d), F32)],
        scratch_shapes=[pltpu.VMEM((tr, d), F32), pltpu.VMEM((tr, d), F32),
                        pltpu.SemaphoreType.DMA((2,))],
        compiler_params=_cp("parallel"), name=name)(x, tgt, meta, g)


def _residual_norm(h, t, g, *, name):
    lp, d = h.shape
    tr = _row_tile(lp, d * (4 + 4 + 4 + 2))

    def body(h_ref, t_ref, g_ref, h1_ref, m_ref):
        xv = h_ref[...] + t_ref[...]
        h1_ref[...] = xv
        r = lax.rsqrt(jnp.mean(xv * xv, axis=-1, keepdims=True) + EPS)
        m_ref[...] = (xv * r * g_ref[...]).astype(BF16)

    return pl.pallas_call(
        body, grid=(lp // tr,),
        in_specs=[_rspec(tr, d), _rspec(tr, d), _fspec((1, d))],
        out_specs=[_rspec(tr, d), _rspec(tr, d)],
        out_shape=[jax.ShapeDtypeStruct((lp, d), F32), jax.ShapeDtypeStruct((lp, d), BF16)],
        compiler_params=_cp("parallel"), name=name)(h, t, g)


def _rmsnorm_bwd_add(x, g, dy, dres, *, name):
    lp, d = x.shape
    tr = _row_tile(lp, d * (4 * 4 + 2))

    def body(x_ref, g_ref, dy_ref, dr_ref, dx_ref, dx16_ref, dg_ref):
        @pl.when(pl.program_id(0) == 0)
        def _():
            dg_ref[...] = jnp.zeros_like(dg_ref)

        xv = x_ref[...]
        r = lax.rsqrt(jnp.mean(xv * xv, axis=-1, keepdims=True) + EPS)
        xh = xv * r
        dyv = dy_ref[...]
        dg_ref[...] += jnp.sum(dyv * xh, axis=0, keepdims=True)
        dxh = dyv * g_ref[...]
        dx = dr_ref[...] + r * (dxh - xh * jnp.mean(dxh * xh, axis=-1, keepdims=True))
        dx_ref[...] = dx
        dx16_ref[...] = dx.astype(BF16)

    return pl.pallas_call(
        body, grid=(lp // tr,),
        in_specs=[_rspec(tr, d), _fspec((1, d)), _rspec(tr, d), _rspec(tr, d)],
        out_specs=[_rspec(tr, d), _rspec(tr, d), _fspec((1, d))],
        out_shape=[jax.ShapeDtypeStruct((lp, d), F32), jax.ShapeDtypeStruct((lp, d), BF16),
                   jax.ShapeDtypeStruct((1, d), F32)],
        compiler_params=_cp("arbitrary"), name=name)(x, g, dy, dres)


def _rmsnorm_bwd_tokens(x, g, dy, dres, *, n_tok, name):
    lp, d = x.shape
    tr = _row_tile(lp, d * 4 * 4)
    n_tiles = lp // tr

    def body(x_ref, g_ref, dy_ref, dr_ref, dtok_ref, dmeta_ref, dg_ref, buf_ref, sem):
        i = pl.program_id(0)

        @pl.when(i == 0)
        def _():
            dg_ref[...] = jnp.zeros_like(dg_ref)

        xv = x_ref[...]
        r = lax.rsqrt(jnp.mean(xv * xv, axis=-1, keepdims=True) + EPS)
        xh = xv * r
        dyv = dy_ref[...]
        dg_ref[...] += jnp.sum(dyv * xh, axis=0, keepdims=True)
        dxh = dyv * g_ref[...]
        buf_ref[...] = dr_ref[...] + r * (dxh - xh * jnp.mean(dxh * xh, axis=-1, keepdims=True))

        @pl.when(i == 0)
        def _():
            dmeta_ref[...] = buf_ref[0:N_META, :]

        _token_rows_copy(i, n_tiles, tr, n_tok, dtok_ref, buf_ref, sem, to_tokens=True)

    return pl.pallas_call(
        body, grid=(n_tiles,),
        in_specs=[_rspec(tr, d), _fspec((1, d)), _rspec(tr, d), _rspec(tr, d)],
        out_specs=[ANY, _fspec((N_META, d)), _fspec((1, d))],
        out_shape=[jax.ShapeDtypeStruct((n_tok, d), F32), jax.ShapeDtypeStruct((N_META, d), F32),
                   jax.ShapeDtypeStruct((1, d), F32)],
        scratch_shapes=[pltpu.VMEM((tr, d), F32), pltpu.SemaphoreType.DMA],
        compiler_params=_cp("arbitrary"), name=name)(x, g, dy, dres)


def _final_loss(h1, t2, g, tgt, *, n_tok, name):
    lp, d = h1.shape
    tr = _row_tile(lp, d * (4 * 4 + 2))
    n_tiles = lp // tr

    def body(h_ref, t_ref, g_ref, tg_ref, dh_ref, dh16_ref, loss_ref, dg_ref):
        i = pl.program_id(0)

        @pl.when(i == 0)
        def _():
            loss_ref[...] = jnp.zeros_like(loss_ref)
            dg_ref[...] = jnp.zeros_like(dg_ref)

        xv = h_ref[...] + t_ref[...]
        r = lax.rsqrt(jnp.mean(xv * xv, axis=-1, keepdims=True) + EPS)
        xh = xv * r
        gv = g_ref[...]
        row = i * tr + lax.broadcasted_iota(jnp.int32, (tr, 1), 0)
        valid = (row >= N_META) & (row < N_META + n_tok)
        err = jnp.where(valid, xh * gv - tg_ref[...], 0.0)
        loss_ref[...] += jnp.sum(0.5 * err * err) / d
        dy = err / d
        dg_ref[...] += jnp.sum(dy * xh, axis=0, keepdims=True)
        dxh = dy * gv
        dh = r * (dxh - xh * jnp.mean(dxh * xh, axis=-1, keepdims=True))
        dh_ref[...] = dh
        dh16_ref[...] = dh.astype(BF16)

    return pl.pallas_call(
        body, grid=(n_tiles,),
        in_specs=[_rspec(tr, d), _rspec(tr, d), _fspec((1, d)), _rspec(tr, d)],
        out_specs=[_rspec(tr, d), _rspec(tr, d), _fspec((1, LANES)), _fspec((1, d))],
        out_shape=[jax.ShapeDtypeStruct((lp, d), F32), jax.ShapeDtypeStruct((lp, d), BF16),
                   jax.ShapeDtypeStruct((1, LANES), F32), jax.ShapeDtypeStruct((1, d), F32)],
        compiler_params=_cp("arbitrary"), name=name)(h1, t2, g, tgt)


def _hg_out(o_f, o_b, proj, gain, *, col_g, name):
    lp, w = o_f.shape
    tr = _row_tile(lp, w * (3 * 4 + 2))
    hh = w // HG_DK

    def body(of_ref, ob_ref, g_ref, gain_ref, y_ref):
        gv = g_ref[...]
        sg = gv * _sigmoid(gv)
        for h in range(hh):
            sl = slice(h * HG_DK, (h + 1) * HG_DK)
            o = of_ref[:, sl] + ob_ref[:, sl]
            r = lax.rsqrt(jnp.mean(o * o, axis=-1, keepdims=True) + EPS)
            y_ref[:, sl] = (o * r * gain_ref[:, sl] * sg[:, sl]).astype(BF16)

    return pl.pallas_call(
        body, grid=(lp // tr,),
        in_specs=[_rspec(tr, w), _rspec(tr, w), _rspec(tr, w, col_g // w), _fspec((1, w))],
        out_specs=_rspec(tr, w),
        out_shape=jax.ShapeDtypeStruct((lp, w), BF16),
        compiler_params=_cp("parallel"), name=name)(o_f, o_b, proj, gain)


def _hg_out_bwd(o_f, o_b, proj, gain, dy, *, col_g, name):
    lp, w = o_f.shape
    tr = _row_tile(lp, w * (5 * 4 + 2))
    hh = w // HG_DK

    def body(of_ref, ob_ref, g_ref, gain_ref, dy_ref, do_ref, dg_ref, dgain_ref):
        @pl.when(pl.program_id(0) == 0)
        def _():
            dgain_ref[...] = jnp.zeros_like(dgain_ref)

        for h in range(hh):
            sl = slice(h * HG_DK, (h + 1) * HG_DK)
            gv = g_ref[:, sl]
            s = _sigmoid(gv)
            sg = gv * s
            dsg = s + gv * s * (1.0 - s)
            o = of_ref[:, sl] + ob_ref[:, sl]
            r = lax.rsqrt(jnp.mean(o * o, axis=-1, keepdims=True) + EPS)
            on = o * r
            dyv = dy_ref[:, sl]
            gn = gain_ref[:, sl]
            dgain_ref[:, sl] += jnp.sum(dyv * on * sg, axis=0, keepdims=True)
            dg_ref[:, sl] = (dyv * on * gn * dsg).astype(BF16)
            don = dyv * gn * sg
            do_ref[:, sl] = r * (don - on * jnp.mean(don * on, axis=-1, keepdims=True))

    return pl.pallas_call(
        body, grid=(lp // tr,),
        in_specs=[_rspec(tr, w), _rspec(tr, w), _rspec(tr, w, col_g // w), _fspec((1, w)),
                  _rspec(tr, w)],
        out_specs=[_rspec(tr, w), _rspec(tr, w), _fspec((1, w))],
        out_shape=[jax.ShapeDtypeStruct((lp, w), F32), jax.ShapeDtypeStruct((lp, w), BF16),
                   jax.ShapeDtypeStruct((1, w), F32)],
        compiler_params=_cp("arbitrary"), name=name)(o_f, o_b, proj, gain, dy)


HG_ROWS = 128
HG_HALVES = (1, 2, 4, 8, 16, 32, 64)


def _hg_gates(zq, z, lbv):
    sq = _sigmoid(zq)
    s = _sigmoid(z)
    f = lbv + (1.0 - lbv) * s
    kk = (1.0 - lbv) * (1.0 - s)
    return zq * sq, sq, s, f, jnp.log(f), kk


def _block_cumsum(g, pos, suffix):
    x = g
    for k in HG_HALVES:
        if suffix:
            x = x + jnp.where(pos < HG_ROWS - k, pltpu.roll(x, HG_ROWS - k, 0), 0.0)
        else:
            x = x + jnp.where(pos >= k, pltpu.roll(x, k, 0), 0.0)
    return x


def _pair_levels(b, pos, reverse):
    out = []
    first = b
    for m in HG_HALVES:
        if m > 1:
            first = jnp.where((pos & (m - 1)) >= m // 2, pltpu.roll(first, m // 2, 0), first)
        nxt = pltpu.roll(first, HG_ROWS - m, 0)
        upper = (pos & (2 * m - 1)) >= m
        if reverse:
            eq = jnp.where(upper, 0.0, jnp.exp(b - nxt))
            ek = jnp.where(upper, jnp.exp(first - b), 0.0)
        else:
            eq = jnp.where(upper, jnp.exp(b - first), 0.0)
            ek = jnp.where(upper, 0.0, jnp.exp(nxt - b))
        out.append((eq, ek))
    return out


def _pair_masks(mask_ref):
    ri = lax.broadcasted_iota(jnp.int32, (HG_ROWS, HG_ROWS), 0)
    ci = lax.broadcasted_iota(jnp.int32, (HG_ROWS, HG_ROWS), 1)
    for i, m in enumerate(HG_HALVES):
        sh = m.bit_length()
        mask_ref[i] = jnp.where((ri >> sh) == (ci >> sh), 1.0, 0.0)


def _hg_scan_fwd(proj, lb, *, reverse, col_q, col_z, col_i, hh, name):
    lp = proj.shape[0]
    n_blocks = lp // HG_ROWS
    last = 0 if reverse else HG_ROWS - 1

    def body(q_ref, z_ref, i_ref, lb_ref, o_ref, st_ref, mask_ref):
        lbv = lb_ref[...]
        pos = lax.broadcasted_iota(jnp.int32, (HG_ROWS, 1), 0)
        ri = lax.broadcasted_iota(jnp.int32, (HG_ROWS, HG_ROWS), 0)
        ci = lax.broadcasted_iota(jnp.int32, (HG_ROWS, HG_ROWS), 1)
        _pair_masks(mask_ref)

        def block(bi, st):
            bb = (n_blocks - 1 - bi) if reverse else bi
            r0 = pl.multiple_of(bb * HG_ROWS, HG_ROWS)
            v16 = i_ref[pl.ds(r0, HG_ROWS), :].astype(BF16)
            qh, _, _, _, g, kk = _hg_gates(q_ref[pl.ds(r0, HG_ROWS), :],
                                           z_ref[pl.ds(r0, HG_ROWS), :], lbv)
            b = _block_cumsum(g, pos, reverse)
            bl = b[last:last + 1, :]
            qe = (qh * jnp.exp(b)).astype(BF16)
            kd = (kk * jnp.exp(bl - b)).astype(BF16)
            a = jnp.where(ri == ci, jnp.sum(qh * kk, axis=1, keepdims=True), 0.0)
            for i, (eq, ek) in enumerate(_pair_levels(b, pos, reverse)):
                a = a + mask_ref[i] * _nt((qh * eq).astype(BF16), (kk * ek).astype(BF16))
            st_ref[bb] = st
            o_ref[pl.ds(r0, HG_ROWS), :] = _nn(a.astype(BF16), v16) + _nt(qe, st.astype(BF16))
            return jnp.exp(bl) * st + _tn(v16, kd)

        lax.fori_loop(0, n_blocks, block, jnp.zeros((HG_DK, HG_DK), F32))

    cspec = lambda col: pl.BlockSpec((lp, HG_DK), lambda h: (0, col // HG_DK + h))
    return pl.pallas_call(
        body, grid=(hh,),
        in_specs=[cspec(col_q), cspec(col_z), cspec(col_i),
                  pl.BlockSpec((None, 1, HG_DK), lambda h: (h, 0, 0))],
        out_specs=[pl.BlockSpec((lp, HG_DK), lambda h: (0, h)),
                   pl.BlockSpec((None, n_blocks, HG_DK, HG_DK), lambda h: (h, 0, 0, 0))],
        out_shape=[jax.ShapeDtypeStruct((lp, hh * HG_DK), F32),
                   jax.ShapeDtypeStruct((hh, n_blocks, HG_DK, HG_DK), F32)],
        scratch_shapes=[pltpu.VMEM((len(HG_HALVES), HG_ROWS, HG_ROWS), F32)],
        compiler_params=_cp("parallel"), name=name)(proj, proj, proj, lb)


def _hg_scan_bwd(proj, lb, states, do, *, reverse, col_q, col_z, col_i, hh, name):
    lp = proj.shape[0]
    n_blocks = lp // HG_ROWS
    last = 0 if reverse else HG_ROWS - 1

    def body(q_ref, z_ref, i_ref, lb_ref, st_ref, do_ref, dq_ref, dz_ref, dv_ref, dlb_ref, mask_ref):
        lbv = lb_ref[...]
        pos = lax.broadcasted_iota(jnp.int32, (HG_ROWS, 1), 0)
        ri = lax.broadcasted_iota(jnp.int32, (HG_ROWS, HG_ROWS), 0)
        ci = lax.broadcasted_iota(jnp.int32, (HG_ROWS, HG_ROWS), 1)
        _pair_masks(mask_ref)

        def block(bi, carry):
            dst, dlb = carry
            bb = bi if reverse else (n_blocks - 1 - bi)
            r0 = pl.multiple_of(bb * HG_ROWS, HG_ROWS)
            zq = q_ref[pl.ds(r0, HG_ROWS), :]
            v16 = i_ref[pl.ds(r0, HG_ROWS), :].astype(BF16)
            do16 = do_ref[pl.ds(r0, HG_ROWS), :].astype(BF16)
            qh, sq, s, f, g, kk = _hg_gates(zq, z_ref[pl.ds(r0, HG_ROWS), :], lbv)
            b = _block_cumsum(g, pos, reverse)
            bl = b[last:last + 1, :]
            eb = jnp.exp(b)
            ebl = jnp.exp(bl - b)
            decay = jnp.exp(bl)
            qe16 = (qh * eb).astype(BF16)
            kd16 = (kk * ebl).astype(BF16)
            st = st_ref[bb]
            st16, dst16 = st.astype(BF16), dst.astype(BF16)
            same_row = ri == ci
            da = _nt(do16, v16)
            da_diag = jnp.sum(jnp.where(same_row, da, 0.0), axis=1, keepdims=True)
            dq_state = eb * _nn(do16, st16)
            dk_state = ebl * _nn(v16, dst16)
            dq = dq_state + da_diag * kk
            dk = dk_state + da_diag * qh
            dbl = (decay * jnp.sum(st * dst, axis=0, keepdims=True)
                   + jnp.sum(kk * dk_state, axis=0, keepdims=True))
            db = qh * dq_state - kk * dk_state + jnp.where(pos == last, dbl, 0.0)
            a = jnp.where(same_row, jnp.sum(qh * kk, axis=1, keepdims=True), 0.0)
            for i, (eq, ek) in enumerate(_pair_levels(b, pos, reverse)):
                same = mask_ref[i]
                q16, k16 = (qh * eq).astype(BF16), (kk * ek).astype(BF16)
                a = a + same * _nt(q16, k16)
                da16 = (same * da).astype(BF16)
                gq, gk = _nn(da16, k16), _tn(da16, q16)
                dq = dq + eq * gq
                dk = dk + ek * gk
                db = db + (q16.astype(F32) * gq - k16.astype(F32) * gk)
            dg = _block_cumsum(db, pos, not reverse)
            df = dg / f - dk
            dq_ref[pl.ds(r0, HG_ROWS), :] = dq * (sq + zq * sq * (1.0 - sq))
            dz_ref[pl.ds(r0, HG_ROWS), :] = df * (1.0 - lbv) * s * (1.0 - s)
            dv_ref[pl.ds(r0, HG_ROWS), :] = _nt(kd16, dst16) + _tn(a.astype(BF16), do16)
            return (decay * dst + _tn(do16, qe16),
                    dlb + jnp.sum(df * (1.0 - s), axis=0, keepdims=True))

        _, dlb = lax.fori_loop(0, n_blocks, block,
                               (jnp.zeros((HG_DK, HG_DK), F32), jnp.zeros((1, HG_DK), F32)))
        dlb_ref[...] = dlb

    cspec = lambda col: pl.BlockSpec((lp, HG_DK), lambda h: (0, col // HG_DK + h))
    ospec = pl.BlockSpec((lp, HG_DK), lambda h: (0, h))
    sds = jax.ShapeDtypeStruct((lp, hh * HG_DK), F32)
    return pl.pallas_call(
        body, grid=(hh,),
        in_specs=[cspec(col_q), cspec(col_z), cspec(col_i),
                  pl.BlockSpec((None, 1, HG_DK), lambda h: (h, 0, 0)),
                  pl.BlockSpec((None, n_blocks, HG_DK, HG_DK), lambda h: (h, 0, 0, 0)),
                  ospec],
        out_specs=[ospec, ospec, ospec, pl.BlockSpec((None, 1, HG_DK), lambda h: (h, 0, 0))],
        out_shape=[sds, sds, sds, jax.ShapeDtypeStruct((hh, 1, HG_DK), F32)],
        scratch_shapes=[pltpu.VMEM((len(HG_HALVES), HG_ROWS, HG_ROWS), F32)],
        compiler_params=_cp("parallel"), name=name)(proj, proj, proj, lb, states, do)


NA_HB = LANES // NA_HEAD_DIM
NA_G = 4
NA_U = NA_G + NA_WIN_H
NA_QN = NA_G * GRID_W
NA_KN = NA_U * GRID_W


def _na_table_index(pattern, a, j):
    if pattern == 0:
        return j - a + NA_WIN_H - 1 if j < NA_WIN_H else None
    if pattern == 2:
        return j - a - 1 if j >= NA_U - NA_WIN_H else None
    return j - a + NA_WIN_H // 2 - 1 if a <= j < a + NA_WIN_H else None


def _na_step_rows(pattern, t, rows):
    if pattern == 0:
        r0, us = 0, 0
    elif pattern == 2:
        r0, us = rows - NA_G, rows - NA_U
    else:
        r0 = NA_G * t
        us = r0 - NA_WIN_H // 2
    q0, k0 = N_META + GRID_W * r0, N_META + GRID_W * us
    if pattern == 1:
        q0, k0 = pl.multiple_of(q0, 16), pl.multiple_of(k0, 16)
    return q0, k0


def _na_fill_bias(tb_ref, bias_ref):
    neg = jnp.full((GRID_W, GRID_W), -1e30, F32)
    for h in range(NA_HB):
        for pattern in range(3):
            for a in range(NA_G):
                for j in range(NA_U):
                    idx = _na_table_index(pattern, a, j)
                    bias_ref[h, pattern, a * GRID_W:(a + 1) * GRID_W, j * GRID_W:(j + 1) * GRID_W] = (
                        neg if idx is None else tb_ref[h, idx])


def _na_steps(rows, step, carry):
    n_steps = rows // NA_G
    carry = step(0, 0, carry)
    carry = lax.fori_loop(1, n_steps - 1, functools.partial(step, 1), carry)
    return step(2, n_steps - 1, carry)


def _na_head_lanes():
    lane = lax.broadcasted_iota(jnp.int32, (1, LANES), 1)
    return [lane // NA_HEAD_DIM == h for h in range(NA_HB)]


def _na_only(mask, x):
    return jnp.where(mask, x, jnp.zeros_like(x))


def _na_fwd(proj, tb, *, n_tok, nh, name):
    lp = proj.shape[0]
    dh, hb = NA_HEAD_DIM, NA_HB
    naw = nh * dh
    rows = n_tok // GRID_W
    scale = dh ** -0.5

    def body(q_ref, k_ref, v_ref, tb_ref, o_ref, lse_ref, q16_ref, k16_ref, v16_ref, bias_ref):
        o_ref[...] = jnp.zeros_like(o_ref)
        lse_ref[...] = jnp.zeros_like(lse_ref)
        q16_ref[...] = q_ref[...].astype(BF16)
        k16_ref[...] = k_ref[...].astype(BF16)
        v16_ref[...] = v_ref[...].astype(BF16)
        _na_fill_bias(tb_ref, bias_ref)
        heads = _na_head_lanes()
        km = k16_ref[0:N_META, :]
        vm = v16_ref[0:N_META, :]
        qm = q16_ref[0:N_META, :]
        o_m = None
        for h in range(hb):
            s = _nt(_na_only(heads[h], qm), km) * scale
            m = jnp.max(s, axis=1, keepdims=True)
            p = jnp.exp(s - m)
            l = jnp.sum(p, axis=1, keepdims=True)
            o_h = _nn(p.astype(BF16), vm) / l
            o_m = o_h if o_m is None else jnp.where(heads[h], o_h, o_m)
            lse_ref[h, 0:N_META, :] = m + jnp.log(l)
        o_ref[0:N_META, :] = o_m

        def step(pattern, t, carry):
            q0, k0 = _na_step_rows(pattern, t, rows)
            q16 = q16_ref[pl.ds(q0, NA_QN), :]
            k16 = k16_ref[pl.ds(k0, NA_KN), :]
            v16 = v16_ref[pl.ds(k0, NA_KN), :]
            o = None
            for h in range(hb):
                q_h = _na_only(heads[h], q16)
                s = _nt(q_h, k16) * scale + bias_ref[h, pattern]
                sm = _nt(q_h, km) * scale
                m = jnp.maximum(jnp.max(s, axis=1, keepdims=True),
                                jnp.max(sm, axis=1, keepdims=True))
                p = jnp.exp(s - m)
                pm = jnp.exp(sm - m)
                l = jnp.sum(p, axis=1, keepdims=True) + jnp.sum(pm, axis=1, keepdims=True)
                o_h = (_nn(p.astype(BF16), v16) + _nn(pm.astype(BF16), vm)) / l
                o = o_h if o is None else jnp.where(heads[h], o_h, o)
                lse_ref[h, pl.ds(q0, NA_QN), :] = m + jnp.log(l)
            o_ref[pl.ds(q0, NA_QN), :] = o
            return carry

        _na_steps(rows, step, 0)

    cblk = lambda col: pl.BlockSpec((lp, LANES), lambda g: (0, col // LANES + g))
    return pl.pallas_call(
        body, grid=(nh // hb,),
        in_specs=[cblk(0), cblk(naw), cblk(2 * naw),
                  pl.BlockSpec((hb, 2 * NA_WIN_H - 1, GRID_W, GRID_W), lambda g: (g, 0, 0, 0))],
        out_specs=[cblk(0), pl.BlockSpec((hb, lp, 1), lambda g: (g, 0, 0))],
        out_shape=[jax.ShapeDtypeStruct((lp, naw), F32), jax.ShapeDtypeStruct((nh, lp, 1), F32)],
        scratch_shapes=[pltpu.VMEM((lp, LANES), BF16)] * 3 + [pltpu.VMEM((hb, 3, NA_QN, NA_KN), F32)],
        compiler_params=_cp("parallel"), name=name)(proj, proj, proj, tb)


def _na_bwd(proj, tb, o, lse, do, *, n_tok, nh, name):
    lp = proj.shape[0]
    dh, hb = NA_HEAD_DIM, NA_HB
    naw = nh * dh
    rows = n_tok // GRID_W
    scale = dh ** -0.5

    def body(q_ref, k_ref, v_ref, tb_ref, o_ref, lse_ref, do_ref, dq_ref, dk_ref, dv_ref, dtb_ref,
             q16_ref, k16_ref, v16_ref, bias_ref):
        dq_ref[...] = jnp.zeros_like(dq_ref)
        dk_ref[...] = jnp.zeros_like(dk_ref)
        dv_ref[...] = jnp.zeros_like(dv_ref)
        dtb_ref[...] = jnp.zeros_like(dtb_ref)
        q16_ref[...] = q_ref[...].astype(BF16)
        k16_ref[...] = k_ref[...].astype(BF16)
        v16_ref[...] = v_ref[...].astype(BF16)
        _na_fill_bias(tb_ref, bias_ref)
        heads = _na_head_lanes()
        km = k16_ref[0:N_META, :]
        vm = v16_ref[0:N_META, :]
        qm = q16_ref[0:N_META, :]
        dom = do_ref[0:N_META, :]
        prod = dom * o_ref[0:N_META, :]
        dq_m = None
        dkm0 = jnp.zeros((N_META, LANES), F32)
        dvm0 = jnp.zeros((N_META, LANES), F32)
        for h in range(hb):
            q_h = _na_only(heads[h], qm)
            do_h = _na_only(heads[h], dom).astype(BF16)
            p = jnp.exp(_nt(q_h, km) * scale - lse_ref[h, 0:N_META, :])
            delta = jnp.sum(_na_only(heads[h], prod), axis=1, keepdims=True)
            ds = (p * (_nt(do_h, vm) - delta)).astype(BF16)
            dq_h = _nn(ds, km) * scale
            dq_m = dq_h if dq_m is None else jnp.where(heads[h], dq_h, dq_m)
            dkm0 = dkm0 + _tn(ds, q_h) * scale
            dvm0 = dvm0 + _tn(p.astype(BF16), do_h)
        dq_ref[0:N_META, :] = dq_m

        def step(pattern, t, carry):
            dkm, dvm = carry
            q0, k0 = _na_step_rows(pattern, t, rows)
            q16 = q16_ref[pl.ds(q0, NA_QN), :]
            k16 = k16_ref[pl.ds(k0, NA_KN), :]
            v16 = v16_ref[pl.ds(k0, NA_KN), :]
            dov = do_ref[pl.ds(q0, NA_QN), :]
            prod = dov * o_ref[pl.ds(q0, NA_QN), :]
            dq = None
            dk = jnp.zeros((NA_KN, LANES), F32)
            dv = jnp.zeros((NA_KN, LANES), F32)
            for h in range(hb):
                q_h = _na_only(heads[h], q16)
                do_h = _na_only(heads[h], dov).astype(BF16)
                lse = lse_ref[h, pl.ds(q0, NA_QN), :]
                p = jnp.exp(_nt(q_h, k16) * scale + bias_ref[h, pattern] - lse)
                pm = jnp.exp(_nt(q_h, km) * scale - lse)
                delta = jnp.sum(_na_only(heads[h], prod), axis=1, keepdims=True)
                ds = p * (_nt(do_h, v16) - delta)
                dsm = (pm * (_nt(do_h, vm) - delta)).astype(BF16)
                ds16 = ds.astype(BF16)
                dq_h = (_nn(ds16, k16) + _nn(dsm, km)) * scale
                dq = dq_h if dq is None else jnp.where(heads[h], dq_h, dq)
                dk = dk + _tn(ds16, q_h) * scale
                dv = dv + _tn(p.astype(BF16), do_h)
                dkm = dkm + _tn(dsm, q_h) * scale
                dvm = dvm + _tn(pm.astype(BF16), do_h)
                for a in range(NA_G):
                    for j in range(NA_U):
                        idx = _na_table_index(pattern, a, j)
                        if idx is not None:
                            dtb_ref[h, idx] += ds[a * GRID_W:(a + 1) * GRID_W,
                                                  j * GRID_W:(j + 1) * GRID_W]
            dq_ref[pl.ds(q0, NA_QN), :] = dq
            dk_ref[pl.ds(k0, NA_KN), :] += dk
            dv_ref[pl.ds(k0, NA_KN), :] += dv
            return dkm, dvm

        dkm, dvm = _na_steps(rows, step, (dkm0, dvm0))
        dk_ref[0:N_META, :] += dkm
        dv_ref[0:N_META, :] += dvm

    cblk = lambda col: pl.BlockSpec((lp, LANES), lambda g: (0, col // LANES + g))
    tbs = pl.BlockSpec((hb, 2 * NA_WIN_H - 1, GRID_W, GRID_W), lambda g: (g, 0, 0, 0))
    sds = jax.ShapeDtypeStruct((lp, naw), F32)
    return pl.pallas_call(
        body, grid=(nh // hb,),
        in_specs=[cblk(0), cblk(naw), cblk(2 * naw), tbs, cblk(0),
                  pl.BlockSpec((hb, lp, 1), lambda g: (g, 0, 0)), cblk(0)],
        out_specs=[cblk(0), cblk(0), cblk(0), tbs],
        out_shape=[sds, sds, sds, jax.ShapeDtypeStruct(tb.shape, F32)],
        scratch_shapes=[pltpu.VMEM((lp, LANES), BF16)] * 3 + [pltpu.VMEM((hb, 3, NA_QN, NA_KN), F32)],
        compiler_params=_cp("parallel"), name=name)(proj, proj, proj, tb, o, lse, do)


def _rpb_onehot():
    c = np.arange(GRID_W)[:, None]
    w = np.arange(GRID_W)[None, :]
    cs = np.clip(c - NA_WIN_W // 2, 0, GRID_W - NA_WIN_W)
    in_win = (w >= cs) & (w < cs + NA_WIN_W)
    dc = np.clip(w - c, -(NA_WIN_W - 1), NA_WIN_W - 1) + NA_WIN_W - 1
    oh = np.zeros((LANES, GRID_W * GRID_W), np.float32)
    flat = np.arange(GRID_W * GRID_W).reshape(GRID_W, GRID_W)
    oh[dc[in_win], flat[in_win]] = 1.0
    neg = np.where(in_win, 0.0, -1e30).astype(np.float32).reshape(1, -1)
    return oh, neg


def _assemble_dproj(dq_na, dk_na, dv_na, dq_f, dq_b, dz_f, dz_b, dv_f, dv_b, dg, dgn, dgh, *, name):
    lp, naw = dq_na.shape
    hgw = dq_f.shape[1]
    d = dgn.shape[1]
    cols = 3 * naw + 5 * hgw + 2 * d
    tr = _row_tile(lp, 3 * naw * 4 + 6 * hgw * 4 + hgw * 2 + 2 * d * 2 + cols * 2)

    def body(nq_ref, nk_ref, nv_ref, qf_ref, qb_ref, zf_ref, zb_ref, vf_ref, vb_ref, g_ref, gn_ref,
             gh_ref, o_ref):
        o_ref[:, 0:naw] = nq_ref[...].astype(BF16)
        o_ref[:, naw:2 * naw] = nk_ref[...].astype(BF16)
        o_ref[:, 2 * naw:3 * naw] = nv_ref[...].astype(BF16)
        c0 = 3 * naw
        o_ref[:, c0:c0 + hgw] = (qf_ref[...] + qb_ref[...]).astype(BF16)
        o_ref[:, c0 + hgw:c0 + 2 * hgw] = zf_ref[...].astype(BF16)
        o_ref[:, c0 + 2 * hgw:c0 + 3 * hgw] = zb_ref[...].astype(BF16)
        o_ref[:, c0 + 3 * hgw:c0 + 4 * hgw] = (vf_ref[...] + vb_ref[...]).astype(BF16)
        o_ref[:, c0 + 4 * hgw:c0 + 5 * hgw] = g_ref[...]
        o_ref[:, c0 + 5 * hgw:c0 + 5 * hgw + d] = gn_ref[...]
        o_ref[:, c0 + 5 * hgw + d:] = gh_ref[...]

    hg, na = _rspec(tr, hgw), _rspec(tr, naw)
    return pl.pallas_call(
        body, grid=(lp // tr,),
        in_specs=[na, na, na, hg, hg, hg, hg, hg, hg, hg, _rspec(tr, d), _rspec(tr, d)],
        out_specs=_rspec(tr, cols),
        out_shape=jax.ShapeDtypeStruct((lp, cols), BF16),
        compiler_params=_cp("parallel"), name=name)(dq_na, dk_na, dv_na, dq_f, dq_b, dz_f, dz_b,
                                                    dv_f, dv_b, dg, dgn, dgh)


GROUP_STEPS = 8


def _group_tiles(rows, align):
    steps = GROUP_STEPS if all(r % (GROUP_STEPS * align) == 0 for r in rows) else 1
    return steps, [r // steps for r in rows]


def _adamw(ws, gs, ms, vs, *, name):
    n = len(ws)
    steps, trs = _group_tiles([w.shape[0] for w in ws], 8)

    def body(*refs):
        for i in range(n):
            w_ref, g_ref, m_ref, v_ref = refs[4 * i:4 * i + 4]
            d_ref, mo_ref, vo_ref, go_ref = refs[4 * n + 4 * i:4 * n + 4 * i + 4]
            gv = g_ref[...]
            go_ref[...] = gv
            mn = ADAM_B1 * m_ref[...] + (1.0 - ADAM_B1) * gv
            vn = ADAM_B2 * v_ref[...] + (1.0 - ADAM_B2) * (gv * gv)
            m_hat = mn / (1.0 - ADAM_B1 ** ADAM_STEP)
            v_hat = vn / (1.0 - ADAM_B2 ** ADAM_STEP)
            d_ref[...] = -ADAM_LR * (m_hat / (jnp.sqrt(v_hat) + ADAM_EPS) + ADAM_WD * w_ref[...])
            mo_ref[...] = mn
            vo_ref[...] = vn

    specs = [_rspec(tr, w.shape[1]) for tr, w in zip(trs, ws)]
    out = pl.pallas_call(
        body, grid=(steps,),
        in_specs=[s for s in specs for _ in range(4)],
        out_specs=[s for s in specs for _ in range(4)],
        out_shape=[jax.ShapeDtypeStruct(w.shape, F32) for w in ws for _ in range(4)],
        compiler_params=_cp("parallel"), name=name)(*[a for q in zip(ws, gs, ms, vs) for a in q])
    return [tuple(out[4 * i:4 * i + 4]) for i in range(n)]


def _local_step(x, tgt, meta, first_weight, rest_weights, g_mix, g_mlp, g_fin, hg_gain, rpb, lb,
                early_grads=None, mid_grads=None, late_grad=None, rest_landed=None,
                last_grads=None):
    n_tok, d = x.shape
    hgw = hg_gain.shape[1]
    nh, hh = rpb.shape[0], hgw // HG_DK
    naw = nh * NA_HEAD_DIM
    l_real = N_META + n_tok
    lp = -(-l_real // ROW_ALIGN) * ROW_ALIGN
    col_qhg = 3 * naw
    col_zf, col_zb, col_i, col_g = (col_qhg + hgw, col_qhg + 2 * hgw, col_qhg + 3 * hgw,
                                    col_qhg + 4 * hgw)
    col_gate = col_qhg + 5 * hgw

    oh_np, neg_np = _rpb_onehot()
    oh = jnp.asarray(oh_np)
    rpb_p = jnp.pad(rpb.reshape(nh * (2 * NA_WIN_H - 1), 2 * NA_WIN_W - 1),
                    ((0, 0), (0, LANES - (2 * NA_WIN_W - 1))))
    tb = _matmul(rpb_p, oh, tm=rpb_p.shape[0], tn=512, tk=LANES, precision=HIGHEST,
                 name="rpb_expand")
    tb = (tb + jnp.asarray(neg_np)).reshape(nh, 2 * NA_WIN_H - 1, GRID_W, GRID_W)

    h0, a, tgt_p = _embed_norm(x, tgt, meta, g_mix, lp=lp, name="norm_mix")
    w_in = first_weight((a, tb))
    proj = _matmul(a, w_in, name="mm_in")
    o_na, lse = _na_fwd(proj, tb, n_tok=n_tok, nh=nh, name="na_fwd")
    lb_f = lb[0].reshape(hh, 1, HG_DK)
    lb_b = lb[1].reshape(hh, 1, HG_DK)
    scan_kw = dict(col_q=col_qhg, col_i=col_i, hh=hh)
    o_f, st_f = _hg_scan_fwd(proj, lb_f, reverse=False, col_z=col_zf, name="hg_scan_f", **scan_kw)
    token = rest_landed(o_f) if rest_landed else None
    lb_b_late = lb_b if token is None else lb_b + token[0:1, 0:1]
    o_b, st_b = _hg_scan_fwd(proj, lb_b_late, reverse=True, col_z=col_zb, name="hg_scan_b",
                             **scan_kw)
    o_hg = _hg_out(o_f, o_b, proj, hg_gain, col_g=col_g, name="hg_out")
    w_na, w_hg, w_o, w_up, w_down = rest_weights(o_hg)
    y_na = _matmul(o_na, w_na, name="mm_na_out", out_dtype=BF16)
    gates = ((proj, col_gate), (proj, col_gate + d))

    def mix_gates(acc, gn, gh, yn):
        return acc, _sigmoid(gn) * yn + _sigmoid(gh) * acc

    def mix_gates_bwd(dmix, gn, gh, yn, yh):
        sn, sh = _sigmoid(gn), _sigmoid(gh)
        return dmix * sn, dmix * sh, dmix * yn * sn * (1.0 - sn), dmix * yh * sh * (1.0 - sh)

    y_hg, mix = _matmul(o_hg, w_hg, name="mm_hg_out", epilogue=mix_gates,
                        tiles=(*gates, (y_na, 0)), out_dtypes=(BF16, BF16))
    t1 = _matmul(mix, w_o, name="mm_o")
    h1, mlp_in = _residual_norm(h0, t1, g_mlp, name="resid_norm_mlp")
    u, act = _matmul(mlp_in, w_up, name="mm_up", out_dtypes=(BF16, BF16),
                     epilogue=lambda acc: (acc, jnp.square(jnp.maximum(acc, 0.0))))
    t2 = _matmul(act, w_down, name="mm_down")
    dh2, dh2_16, loss, dg_fin = _final_loss(h1, t2, g_fin, tgt_p, n_tok=n_tok, name="final_loss")

    (du,) = _matmul(dh2_16, w_down, tb=True, name="mm_down_dx", tiles=((u, 0),),
                    out_dtypes=(BF16,),
                    epilogue=lambda acc, uv: (acc * 2.0 * jnp.maximum(uv, 0.0),))
    dw_down = _matmul(act, dh2_16, ta=True, name="mm_down_dw")
    dm = _matmul(du, w_up, tb=True, name="mm_up_dx")
    dw_up = _matmul(mlp_in, du, ta=True, name="mm_up_dw")
    dh1, dh1_16, dg_mlp = _rmsnorm_bwd_add(h1, g_mlp, dm, dh2, name="norm_mlp_bwd")
    dy_na, dy_hg, dgn, dgh = _matmul(dh1_16, w_o, tb=True, name="mm_o_dx", epilogue=mix_gates_bwd,
                                     tiles=(*gates, (y_na, 0), (y_hg, 0)), out_dtypes=(BF16,) * 4)
    dw_o = _matmul(mix, dh1_16, ta=True, name="mm_o_dw")
    do_na = _matmul(dy_na, w_na, tb=True, name="mm_na_out_dx")
    dw_na = _matmul(o_na, dy_na, ta=True, name="mm_na_out_dw")
    do_hg = _matmul(dy_hg, w_hg, tb=True, name="mm_hg_out_dx")
    dw_hg = _matmul(o_hg, dy_hg, ta=True, name="mm_hg_out_dw")
    token = early_grads([dw_na, dw_hg, dw_o, dw_up, dw_down]) if early_grads else None
    if token is not None:
        hg_gain = hg_gain + token[0:1, 0:1]
    d_o, dg_hg, d_gain = _hg_out_bwd(o_f, o_b, proj, hg_gain, do_hg, col_g=col_g, name="hg_out_bwd")
    dq_f, dz_f, dv_f, dlb_f = _hg_scan_bwd(proj, lb_f, st_f, d_o, reverse=False, col_z=col_zf,
                                           name="hg_scan_f_bwd", **scan_kw)
    token = mid_grads(dq_f) if mid_grads else None
    lb_b_late = lb_b if token is None else lb_b + token[0:1, 0:1]
    dq_b, dz_b, dv_b, dlb_b = _hg_scan_bwd(proj, lb_b_late, st_b, d_o, reverse=True, col_z=col_zb,
                                           name="hg_scan_b_bwd", **scan_kw)
    dq_na, dk_na, dv_na, dtb = _na_bwd(proj, tb, o_na, lse, do_na, n_tok=n_tok, nh=nh, name="na_bwd")
    dproj = _assemble_dproj(dq_na, dk_na, dv_na, dq_f, dq_b, dz_f, dz_b, dv_f, dv_b, dg_hg, dgn,
                            dgh, name="assemble_dproj")
    dw_in = _matmul(a, dproj, ta=True, name="mm_in_dw")
    token = late_grad(dw_in) if late_grad else None
    da = _matmul(dproj, w_in, tb=True, name="mm_in_dx", after=token)
    token = last_grads(da) if last_grads else None
    g_mix_late = g_mix if token is None else g_mix + token[0:1, 0:1]
    dx, dmeta, dg_mix = _rmsnorm_bwd_tokens(h0, g_mix_late, da, dh1, n_tok=n_tok,
                                            name="norm_mix_bwd")
    d_rpb = _matmul(dtb.reshape(nh * (2 * NA_WIN_H - 1), GRID_W * GRID_W), oh, tb=True,
                    tm=nh * (2 * NA_WIN_H - 1), tn=LANES, tk=1024, precision=HIGHEST,
                    name="rpb_reduce")
    d_lb = jnp.concatenate([dlb_f.reshape(1, hgw), dlb_b.reshape(1, hgw)], axis=0)
    return (loss, dx, dmeta, dw_in, dw_na, dw_hg, dw_o, dw_up, dw_down,
            dg_mix, dg_mlp, dg_fin, d_gain, d_rpb, d_lb)


N_CHIPS = 4
N_DEV = 8
ANY = pl.BlockSpec(memory_space=pl.ANY)


def _place():
    x, y, c = lax.axis_index("x"), lax.axis_index("y"), lax.axis_index("c")
    others = []
    for j in (1, 2, 3):
        tx = (1 - x) if (j >> 1) else x
        ty = (1 - y) if (j & 1) else y
        others.append((tx, ty))
    return x, y, c, others


def _piece(ref, axis, k, half, rh, cs):
    if axis == 1:
        return ref.at[pl.ds(pl.multiple_of(half * rh, 16), rh), pl.ds(pl.multiple_of(k * cs, LANES), cs)]
    return ref.at[pl.ds(pl.multiple_of(k * 2 * rh + half * rh, 16), rh), :]


def _cast_into_full(shards, axes, place, *, name):
    n = len(shards)
    steps, trs = _group_tiles([s.shape[0] for s in shards], 16)

    def body(p_ref, *refs):
        for i in range(n):
            refs[n + i][...] = refs[i][...].astype(BF16)

    def out_spec(tr, cs, axis):
        if axis == 1:
            return pl.BlockSpec((tr, cs), lambda i, p_ref: (i, p_ref[0]))
        return pl.BlockSpec((tr, cs), lambda i, p_ref: (p_ref[0] * steps + i, 0))

    return pl.pallas_call(
        body,
        grid_spec=pltpu.PrefetchScalarGridSpec(
            num_scalar_prefetch=1, grid=(steps,),
            in_specs=[pl.BlockSpec((tr, s.shape[1]), lambda i, p_ref: (i, 0))
                      for tr, s in zip(trs, shards)],
            out_specs=[out_spec(tr, s.shape[1], ax) for tr, s, ax in zip(trs, shards, axes)]),
        out_shape=[jax.ShapeDtypeStruct((s.shape[0], s.shape[1] * N_CHIPS) if ax == 1
                                        else (s.shape[0] * N_CHIPS, s.shape[1]), BF16)
                   for s, ax in zip(shards, axes)],
        compiler_params=_cp("parallel"), name=name)(place, *shards)


HBM_SPEC = pl.BlockSpec(memory_space=pltpu.HBM)
SEM_SPEC = pl.BlockSpec(memory_space=pltpu.SEMAPHORE)
SPLIT_COPY = pltpu.CompilerParams(has_side_effects=pltpu.SideEffectType.DATAFLOW_SIDE_EFFECTING)
TOKEN = jax.ShapeDtypeStruct((8, LANES), F32)


def _geo(fulls, axes):
    out = []
    for f, ax in zip(fulls, axes):
        r, cs = (f.shape[0], f.shape[1] // N_CHIPS) if ax == 1 else (f.shape[0] // N_CHIPS, f.shape[1])
        out.append((ax, r // 2, cs))
    return out


def _gather_copies(refs, geo, send_sems, recv_sems):
    x, y, c, others = _place()
    chip = 2 * x + y
    cps = []
    for i, (ax, rh, cs) in enumerate(geo):
        mine = _piece(refs[i], ax, chip, c, rh, cs)
        for j, (tx, ty) in enumerate(others):
            cps.append(pltpu.make_async_remote_copy(
                src_ref=mine, dst_ref=mine, send_sem=send_sems.at[3 * i + j],
                recv_sem=recv_sems.at[3 * i + j], device_id=(tx, ty, c), device_id_type=MESH))
    return cps


def _allgather_start(fulls, axes, after, *, name):
    n = len(fulls)
    geo = _geo(fulls, axes)

    def body(*refs):
        w_refs = refs[:n]
        send_sems, recv_sems = refs[n + 1], refs[n + 2]
        token = refs[2 * n + 3]
        for cp in _gather_copies(w_refs, geo, send_sems, recv_sems):
            cp.start()
        token[...] = jnp.zeros_like(token)

    out = pl.pallas_call(
        body, name=name,
        out_shape=(pltpu.SemaphoreType.DMA((3 * n,)), pltpu.SemaphoreType.DMA((3 * n,)),
                   *[pltpu.HBM(f.shape, f.dtype) for f in fulls], TOKEN),
        in_specs=[HBM_SPEC] * n + [ANY],
        out_specs=(SEM_SPEC, SEM_SPEC, *[HBM_SPEC] * n, pl.BlockSpec(memory_space=pltpu.VMEM)),
        input_output_aliases={i: 2 + i for i in range(n)},
        compiler_params=SPLIT_COPY,
    )(*[pltpu.with_memory_space_constraint(f, pltpu.HBM) for f in fulls], after)
    return out[0], out[1], list(out[2:2 + n]), out[2 + n]


def _allgather_wait(send_sems, recv_sems, fulls, axes, after, *, name):
    n = len(fulls)
    geo = _geo(fulls, axes)
    afters = tuple(after) if isinstance(after, (tuple, list)) else (after,)

    def body(*refs):
        w_refs = refs[:n]
        for cp in _gather_copies(w_refs, geo, refs[n], refs[n + 1]):
            cp.wait_send()
            cp.wait_recv()

    return list(pl.pallas_call(
        body, name=name,
        out_shape=[pltpu.HBM(f.shape, f.dtype) for f in fulls],
        in_specs=[HBM_SPEC] * n + [SEM_SPEC, SEM_SPEC] + [ANY] * len(afters),
        out_specs=[HBM_SPEC] * n,
        input_output_aliases={i: i for i in range(n)},
        compiler_params=SPLIT_COPY,
    )(*fulls, send_sems, recv_sems, *afters))


def _allgather_forward(fulls, axes, *, name):
    n = len(fulls)
    geo = _geo(fulls, axes)

    def body(*refs):
        o_refs = refs[n:2 * n]
        send_sems, recv_sems = refs[2 * n:]
        x, y, c, others = _place()

        def rcopy(i, j, half, to):
            ax, rh, cs = geo[i]
            ref = _piece(o_refs[i], ax, 2 * others[j][0] + others[j][1], half, rh, cs)
            return pltpu.make_async_remote_copy(
                src_ref=ref, dst_ref=ref, send_sem=send_sems.at[3 * i + j],
                recv_sem=recv_sems.at[3 * i + j], device_id=to, device_id_type=MESH)

        cps = [rcopy(i, j, c, (x, y, 1 - c)) for i in range(n) for j in range(3)]
        for cp in cps:
            cp.start()
        for i in range(n):
            for j in range(3):
                rcopy(i, j, 1 - c, (x, y, c)).wait_recv()
        for cp in cps:
            cp.wait_send()

    return list(pl.pallas_call(
        body, in_specs=[ANY] * n, out_specs=[ANY] * n,
        out_shape=[jax.ShapeDtypeStruct(f.shape, f.dtype) for f in fulls],
        input_output_aliases={i: i for i in range(n)},
        scratch_shapes=[pltpu.SemaphoreType.DMA((3 * n,)), pltpu.SemaphoreType.DMA((3 * n,))],
        name=name)(*fulls))


def _forward_copies(refs, geo, send_sems, recv_sems):
    x, y, c, others = _place()
    cps = []
    for i, (ax, rh, cs) in enumerate(geo):
        for j, (tx, ty) in enumerate(others):
            ref = _piece(refs[i], ax, 2 * tx + ty, c, rh, cs)
            cps.append(pltpu.make_async_remote_copy(
                src_ref=ref, dst_ref=ref, send_sem=send_sems.at[3 * i + j],
                recv_sem=recv_sems.at[3 * i + j], device_id=(x, y, 1 - c), device_id_type=MESH))
    return cps


def _allgather_forward_start(fulls, axes, *, name):
    n = len(fulls)
    geo = _geo(fulls, axes)

    def body(*refs):
        token = refs[2 * n + 2]
        for cp in _forward_copies(refs[:n], geo, refs[n], refs[n + 1]):
            cp.start()
        token[...] = jnp.zeros_like(token)

    out = pl.pallas_call(
        body, name=name,
        out_shape=(pltpu.SemaphoreType.DMA((3 * n,)), pltpu.SemaphoreType.DMA((3 * n,)),
                   *[pltpu.HBM(f.shape, f.dtype) for f in fulls], TOKEN),
        in_specs=[HBM_SPEC] * n,
        out_specs=(SEM_SPEC, SEM_SPEC, *[HBM_SPEC] * n, pl.BlockSpec(memory_space=pltpu.VMEM)),
        input_output_aliases={i: 2 + i for i in range(n)},
        compiler_params=SPLIT_COPY,
    )(*fulls)
    return out[0], out[1], list(out[2:2 + n]), out[2 + n]


def _allgather_forward_wait(send_sems, recv_sems, fulls, axes, after, *, name):
    n = len(fulls)
    geo = _geo(fulls, axes)

    def body(*refs):
        for cp in _forward_copies(refs[:n], geo, refs[n], refs[n + 1]):
            cp.wait_send()
            cp.wait_recv()

    return list(pl.pallas_call(
        body, name=name,
        out_shape=[pltpu.HBM(f.shape, f.dtype) for f in fulls],
        in_specs=[HBM_SPEC] * n + [SEM_SPEC, SEM_SPEC, ANY],
        out_specs=[HBM_SPEC] * n,
        input_output_aliases={i: i for i in range(n)},
        compiler_params=SPLIT_COPY,
    )(*fulls, send_sems, recv_sems, after))


def _chip_copies(blk_ref, land_ref, send_sems, recv_sems):
    x, y, c, others = _place()
    return [pltpu.make_async_remote_copy(
        src_ref=blk_ref, dst_ref=land_ref.at[2 * x + y], send_sem=send_sems.at[j],
        recv_sem=recv_sems.at[j], device_id=(tx, ty, c), device_id_type=MESH)
        for j, (tx, ty) in enumerate(others)]


def _chip_exchange_start(blk, *, name):
    land = pltpu.with_memory_space_constraint(lax.empty((N_CHIPS, *blk.shape), blk.dtype), pltpu.HBM)

    def body(blk_ref, land_ref, send_sems, recv_sems, blk_out, land_out, token):
        for cp in _chip_copies(blk_ref, land_ref, send_sems, recv_sems):
            cp.start()
        token[...] = jnp.zeros_like(token)

    return pl.pallas_call(
        body, name=name,
        out_shape=(pltpu.SemaphoreType.DMA((3,)), pltpu.SemaphoreType.DMA((3,)),
                   pltpu.HBM(blk.shape, blk.dtype), pltpu.HBM(land.shape, land.dtype), TOKEN),
        in_specs=[HBM_SPEC] * 2,
        out_specs=(SEM_SPEC, SEM_SPEC, HBM_SPEC, HBM_SPEC, pl.BlockSpec(memory_space=pltpu.VMEM)),
        input_output_aliases={0: 2, 1: 3},
        compiler_params=SPLIT_COPY,
    )(pltpu.with_memory_space_constraint(blk, pltpu.HBM), land)


def _chip_exchange_wait(send_sems, recv_sems, blk, land, after, *, name):
    def body(blk_ref, land_ref, send_sems, recv_sems, after_ref, blk_out, land_out):
        for cp in _chip_copies(blk_ref, land_ref, send_sems, recv_sems):
            cp.wait_send()
            cp.wait_recv()

    return pl.pallas_call(
        body, name=name,
        out_shape=[pltpu.HBM(blk.shape, blk.dtype), pltpu.HBM(land.shape, land.dtype)],
        in_specs=[HBM_SPEC] * 2 + [SEM_SPEC, SEM_SPEC, ANY],
        out_specs=[HBM_SPEC] * 2,
        input_output_aliases={0: 0, 1: 1},
        compiler_params=SPLIT_COPY,
    )(blk, land, send_sems, recv_sems, after)[1]


def _scatter_geo(parts, axes):
    out = []
    for p, ax in zip(parts, axes):
        _, rh, cols = p.shape
        out.append((ax, rh, cols // N_CHIPS if ax == 1 else cols))
    return out


def _scatter_copies(p_refs, q_refs, geo, send_sems, recv_sems):
    x, y, c, others = _place()
    chip = 2 * x + y
    cps = []
    for i, (ax, rh, cw) in enumerate(geo):
        for j, (tx, ty) in enumerate(others):
            k = 2 * tx + ty
            src = (p_refs[i].at[0, :, pl.ds(pl.multiple_of(k * cw, LANES), cw)] if ax == 1
                   else p_refs[i].at[k])
            cps.append(pltpu.make_async_remote_copy(
                src_ref=src, dst_ref=q_refs[i].at[chip], send_sem=send_sems.at[3 * i + j],
                recv_sem=recv_sems.at[3 * i + j], device_id=(tx, ty, c), device_id_type=MESH))
    return cps


def _scatter_start(parts, axes, *, name):
    n = len(parts)
    geo = _scatter_geo(parts, axes)
    slots = [pltpu.HBM((N_CHIPS, rh, cw), p.dtype) for p, (_, rh, cw) in zip(parts, geo)]

    def body(*refs):
        p_refs, q_refs = refs[:n], refs[n:2 * n]
        send_sems, recv_sems = refs[2 * n], refs[2 * n + 1]
        token = refs[4 * n + 2]
        for cp in _scatter_copies(p_refs, q_refs, geo, send_sems, recv_sems):
            cp.start()
        token[...] = jnp.zeros_like(token)

    land = [pltpu.with_memory_space_constraint(lax.empty(s.inner_aval.shape, s.inner_aval.dtype), pltpu.HBM)
            for s in slots]
    out = pl.pallas_call(
        body, name=name,
        out_shape=(pltpu.SemaphoreType.DMA((3 * n,)), pltpu.SemaphoreType.DMA((3 * n,)),
                   *[pltpu.HBM(p.shape, p.dtype) for p in parts], *slots, TOKEN),
        in_specs=[HBM_SPEC] * (2 * n),
        out_specs=(SEM_SPEC, SEM_SPEC, *[HBM_SPEC] * (2 * n), pl.BlockSpec(memory_space=pltpu.VMEM)),
        input_output_aliases={i: 2 + i for i in range(2 * n)},
        compiler_params=SPLIT_COPY,
    )(*[pltpu.with_memory_space_constraint(p, pltpu.HBM) for p in parts], *land)
    return out[0], out[1], list(out[2:2 + n]), list(out[2 + n:2 + 2 * n]), out[2 + 2 * n]


def _scatter_wait(send_sems, recv_sems, parts, slots, axes, after, *, name):
    n = len(parts)
    geo = _scatter_geo(parts, axes)

    def body(*refs):
        p_refs, q_refs = refs[:n], refs[n:2 * n]
        for cp in _scatter_copies(p_refs, q_refs, geo, refs[2 * n], refs[2 * n + 1]):
            cp.wait_send()
            cp.wait_recv()

    out = pl.pallas_call(
        body, name=name,
        out_shape=[pltpu.HBM(a.shape, a.dtype) for a in (*parts, *slots)],
        in_specs=[HBM_SPEC] * (2 * n) + [SEM_SPEC, SEM_SPEC, ANY],
        out_specs=[HBM_SPEC] * (2 * n),
        input_output_aliases={i: i for i in range(2 * n)},
        compiler_params=SPLIT_COPY,
    )(*parts, *slots, send_sems, recv_sems, after)
    return list(out[:n]), list(out[n:])


def _sibling_swap(grads, *, name):
    n = len(grads)
    out_shape = [jax.ShapeDtypeStruct((g.shape[0], g.shape[1] // 2, g.shape[2]), g.dtype)
                 for g in grads]

    def body(*refs):
        g_refs, o_refs = refs[:n], refs[n:2 * n]
        send_sems, recv_sems = refs[2 * n:]
        x, y, c, _ = _place()
        cps = []
        for i in range(n):
            rh = grads[i].shape[1] // 2
            src = g_refs[i].at[:, pl.ds(pl.multiple_of((1 - c) * rh, 16), rh), :]
            cp = pltpu.make_async_remote_copy(
                src_ref=src, dst_ref=o_refs[i], send_sem=send_sems.at[i], recv_sem=recv_sems.at[i],
                device_id=(x, y, 1 - c), device_id_type=MESH)
            cp.start()
            cps.append(cp)
        for cp in cps:
            cp.wait()

    return pl.pallas_call(
        body, in_specs=[ANY] * n, out_specs=[ANY] * n, out_shape=out_shape,
        scratch_shapes=[pltpu.SemaphoreType.DMA((n,)), pltpu.SemaphoreType.DMA((n,))],
        name=name)(*grads)


def _swap_copies(g_refs, r_refs, shapes, send_sems, recv_sems):
    x, y, c, _ = _place()
    cps = []
    for i, shape in enumerate(shapes):
        rh = shape[1] // 2
        src = g_refs[i].at[:, pl.ds(pl.multiple_of((1 - c) * rh, 16), rh), :]
        cps.append(pltpu.make_async_remote_copy(
            src_ref=src, dst_ref=r_refs[i], send_sem=send_sems.at[i], recv_sem=recv_sems.at[i],
            device_id=(x, y, 1 - c), device_id_type=MESH))
    return cps


def _sibling_swap_start(grads, *, name):
    n = len(grads)
    shapes = [g.shape for g in grads]
    lands = [pltpu.HBM((s[0], s[1] // 2, s[2]), g.dtype) for s, g in zip(shapes, grads)]

    def body(*refs):
        g_refs, r_refs = refs[:n], refs[n:2 * n]
        token = refs[4 * n + 2]
        for cp in _swap_copies(g_refs, r_refs, shapes, refs[2 * n], refs[2 * n + 1]):
            cp.start()
        token[...] = jnp.zeros_like(token)

    land = [pltpu.with_memory_space_constraint(lax.empty(s.inner_aval.shape, s.inner_aval.dtype), pltpu.HBM)
            for s in lands]
    out = pl.pallas_call(
        body, name=name,
        out_shape=(pltpu.SemaphoreType.DMA((n,)), pltpu.SemaphoreType.DMA((n,)),
                   *[pltpu.HBM(g.shape, g.dtype) for g in grads], *lands, TOKEN),
        in_specs=[HBM_SPEC] * (2 * n),
        out_specs=(SEM_SPEC, SEM_SPEC, *[HBM_SPEC] * (2 * n), pl.BlockSpec(memory_space=pltpu.VMEM)),
        input_output_aliases={i: 2 + i for i in range(2 * n)},
        compiler_params=SPLIT_COPY,
    )(*[pltpu.with_memory_space_constraint(g, pltpu.HBM) for g in grads], *land)
    return out[0], out[1], list(out[2:2 + n]), list(out[2 + n:2 + 2 * n]), out[2 + 2 * n]


def _sibling_swap_wait(send_sems, recv_sems, grads, lands, after, *, name):
    n = len(grads)
    shapes = [g.shape for g in grads]

    def body(*refs):
        g_refs, r_refs = refs[:n], refs[n:2 * n]
        for cp in _swap_copies(g_refs, r_refs, shapes, refs[2 * n], refs[2 * n + 1]):
            cp.wait_send()
            cp.wait_recv()

    out = pl.pallas_call(
        body, name=name,
        out_shape=[pltpu.HBM(a.shape, a.dtype) for a in (*grads, *lands)],
        in_specs=[HBM_SPEC] * (2 * n) + [SEM_SPEC, SEM_SPEC, ANY],
        out_specs=[HBM_SPEC] * (2 * n),
        input_output_aliases={i: i for i in range(2 * n)},
        compiler_params=SPLIT_COPY,
    )(*grads, *lands, send_sems, recv_sems, after)
    return list(out[:n]), list(out[n:])


def _pair_add(g3s, rxs, place, *, out_dtype, name):
    n = len(g3s)
    steps, trs = _group_tiles([g.shape[1] // 2 for g in g3s], 16)

    def body(p_ref, *refs):
        for i in range(n):
            refs[2 * n + i][...] = (refs[2 * i][...] + refs[2 * i + 1][...]).astype(out_dtype)

    in_specs, out_specs = [], []
    for g, tr in zip(g3s, trs):
        blk = (g.shape[0], tr, g.shape[2])
        in_specs += [pl.BlockSpec(blk, lambda i, p_ref: (0, p_ref[1] * steps + i, 0)),
                     pl.BlockSpec(blk, lambda i, p_ref: (0, i, 0))]
        out_specs.append(pl.BlockSpec(blk, lambda i, p_ref: (0, i, 0)))
    return pl.pallas_call(
        body,
        grid_spec=pltpu.PrefetchScalarGridSpec(
            num_scalar_prefetch=1, grid=(steps,), in_specs=in_specs, out_specs=out_specs),
        out_shape=[jax.ShapeDtypeStruct((g.shape[0], g.shape[1] // 2, g.shape[2]), out_dtype)
                   for g in g3s],
        compiler_params=_cp("parallel"), name=name)(place, *[a for q in zip(g3s, rxs) for a in q])


def _sum_slots(q, *, name):
    ns, rows, cols = q.shape
    tr = next(t for t in (128, 64, 32, 16, 8) if rows % t == 0)

    def body(q_ref, o_ref):
        acc = q_ref[0].astype(F32)
        for k in range(1, ns):
            acc = acc + q_ref[k].astype(F32)
        o_ref[...] = acc

    return pl.pallas_call(
        body, grid=(rows // tr,),
        in_specs=[pl.BlockSpec((ns, tr, cols), lambda i: (0, i, 0))],
        out_specs=_rspec(tr, cols),
        out_shape=jax.ShapeDtypeStruct((rows, cols), F32),
        compiler_params=_cp("parallel"), name=name)(q)


def _sum_chips(qs, ps, place, axes, *, name):
    n = len(qs)
    per = N_CHIPS + 1
    steps, trs = _group_tiles([q.shape[1] for q in qs], 16)

    def body(p_ref, *refs):
        chip = p_ref[0]
        for i in range(n):
            q_refs, own_ref = refs[per * i:per * i + N_CHIPS], refs[per * i + N_CHIPS]
            acc = jnp.where(chip == 0, own_ref[...], q_refs[0][...]).astype(F32)
            for k in range(1, N_CHIPS):
                acc = acc + jnp.where(chip == k, own_ref[...], q_refs[k][...]).astype(F32)
            refs[per * n + i][...] = acc

    def slot_spec(k, tr, cw):
        return pl.BlockSpec((None, tr, cw),
                            lambda i, p_ref: (jnp.where(p_ref[0] == k, (k + 1) % N_CHIPS, k), i, 0))

    in_specs, out_specs, operands = [], [], []
    for q, p, ax, tr in zip(qs, ps, axes, trs):
        cw = q.shape[2]
        in_specs += [slot_spec(k, tr, cw) for k in range(N_CHIPS)]
        in_specs.append(pl.BlockSpec((None, tr, cw), (lambda i, p_ref: (0, i, p_ref[0])) if ax == 1
                                     else (lambda i, p_ref: (p_ref[0], i, 0))))
        out_specs.append(pl.BlockSpec((tr, cw), lambda i, p_ref: (p_ref[1] * steps + i, 0)))
        operands += [q] * N_CHIPS + [p]
    return pl.pallas_call(
        body,
        grid_spec=pltpu.PrefetchScalarGridSpec(
            num_scalar_prefetch=1, grid=(steps,), in_specs=in_specs, out_specs=out_specs),
        out_shape=[jax.ShapeDtypeStruct((2 * q.shape[1], q.shape[2]), F32) for q in qs],
        compiler_params=_cp("parallel"), name=name)(place, *operands)


def _sibling_share(shards, *, name):
    n = len(shards)

    def body(*refs):
        o_refs = refs[n:2 * n]
        send_sems, recv_sems = refs[2 * n:]
        x, y, c, _ = _place()
        cps = []
        for i in range(n):
            rh = shards[i].shape[0] // 2
            mine = o_refs[i].at[pl.ds(pl.multiple_of(c * rh, 8), rh), :]
            cp = pltpu.make_async_remote_copy(
                src_ref=mine, dst_ref=mine, send_sem=send_sems.at[i], recv_sem=recv_sems.at[i],
                device_id=(x, y, 1 - c), device_id_type=MESH)
            cp.start()
            cps.append(cp)
        for i in range(n):
            rh = shards[i].shape[0] // 2
            theirs = o_refs[i].at[pl.ds(pl.multiple_of((1 - c) * rh, 8), rh), :]
            pltpu.make_async_remote_copy(
                src_ref=theirs, dst_ref=theirs, send_sem=send_sems.at[i], recv_sem=recv_sems.at[i],
                device_id=(x, y, c), device_id_type=MESH).wait_recv()
        for cp in cps:
            cp.wait_send()

    return pl.pallas_call(
        body, in_specs=[ANY] * n, out_specs=[ANY] * n,
        out_shape=[jax.ShapeDtypeStruct(h.shape, h.dtype) for h in shards],
        input_output_aliases={i: i for i in range(n)},
        scratch_shapes=[pltpu.SemaphoreType.DMA((n,)), pltpu.SemaphoreType.DMA((n,))],
        name=name)(*shards)


def _gather_all(blk, *, name, after=None, shares=()):
    rows, cols = blk.shape
    extra = [] if after is None else [after]
    n_s = len(shares)

    def body(x_ref, *refs):
        s_refs = refs[len(extra) + n_s + 1:len(extra) + 2 * n_s + 1]
        out_ref = refs[len(extra) + n_s]
        send_sems, recv_sems, local_sem, s_send, s_recv = refs[len(extra) + 2 * n_s + 1:]
        x, y, c = lax.axis_index("x"), lax.axis_index("y"), lax.axis_index("c")
        me = 4 * x + 2 * y + c
        mine = pltpu.make_async_copy(x_ref, out_ref.at[me], local_sem)
        mine.start()
        cps = []
        for i in range(n_s):
            rh = shares[i].shape[0] // 2
            half = s_refs[i].at[pl.ds(pl.multiple_of(c * rh, 8), rh), :]
            cp = pltpu.make_async_remote_copy(
                src_ref=half, dst_ref=half, send_sem=s_send.at[i], recv_sem=s_recv.at[i],
                device_id=(x, y, 1 - c), device_id_type=MESH)
            cp.start()
            cps.append(cp)
        for k in range(1, N_DEV):
            tx = (1 - x) if (k >> 2) & 1 else x
            ty = (1 - y) if (k >> 1) & 1 else y
            tc = (1 - c) if k & 1 else c
            cp = pltpu.make_async_remote_copy(
                src_ref=x_ref, dst_ref=out_ref.at[me], send_sem=send_sems.at[k - 1],
                recv_sem=recv_sems.at[k - 1], device_id=(tx, ty, tc), device_id_type=MESH)
            cp.start()
            cps.append(cp)
        for k in range(1, N_DEV):
            tx = (1 - x) if (k >> 2) & 1 else x
            ty = (1 - y) if (k >> 1) & 1 else y
            tc = (1 - c) if k & 1 else c
            got = out_ref.at[4 * tx + 2 * ty + tc]
            pltpu.make_async_remote_copy(
                src_ref=got, dst_ref=got, send_sem=send_sems.at[k - 1], recv_sem=recv_sems.at[k - 1],
                device_id=(x, y, c), device_id_type=MESH).wait_recv()
        for i in range(n_s):
            rh = shares[i].shape[0] // 2
            theirs = s_refs[i].at[pl.ds(pl.multiple_of((1 - c) * rh, 8), rh), :]
            pltpu.make_async_remote_copy(
                src_ref=theirs, dst_ref=theirs, send_sem=s_send.at[i], recv_sem=s_recv.at[i],
                device_id=(x, y, c), device_id_type=MESH).wait_recv()
        for cp in cps:
            cp.wait_send()
        mine.wait()

    vm = pl.BlockSpec(memory_space=pltpu.VMEM)
    out = pl.pallas_call(
        body, in_specs=[vm] + [ANY] * (len(extra) + n_s), out_specs=[vm] + [ANY] * n_s,
        out_shape=[jax.ShapeDtypeStruct((N_DEV, rows, cols), blk.dtype)]
        + [jax.ShapeDtypeStruct(s.shape, s.dtype) for s in shares],
        input_output_aliases={1 + len(extra) + i: 1 + i for i in range(n_s)},
        scratch_shapes=[pltpu.SemaphoreType.DMA((N_DEV - 1,)), pltpu.SemaphoreType.DMA((N_DEV - 1,)),
                        pltpu.SemaphoreType.DMA, pltpu.SemaphoreType.DMA((max(n_s, 1),)),
                        pltpu.SemaphoreType.DMA((max(n_s, 1),))],
        name=name)(blk, *extra, *shares)
    return (out[0], *out[1:]) if n_s else out[0]


def _as_rows(a):
    flat = a.reshape(-1)
    n = flat.shape[0]
    rows = -(-n // (8 * LANES)) * 8
    return jnp.pad(flat, (0, rows * LANES - n)).reshape(rows, LANES)


def _from_rows(p, shape):
    n = int(np.prod(shape))
    return p.reshape(-1)[:n].reshape(shape)


WEIGHT_AXES = (1, 1, 1, 0, 1, 0)
WIRE = BF16


def kernel(x, meta_tokens, w_in, w_na_out, w_hg_out, w_o, w_up, w_down, norm_mix, norm_mlp, norm_final, hg_norm, na_rpb, hg_lb_logits, loss_target, m_meta_tokens, m_w_in, m_w_na_out, m_w_hg_out, m_w_o, m_w_up, m_w_down, m_norm_mix, m_norm_mlp, m_norm_final, m_hg_norm, m_na_rpb, m_hg_lb_logits, v_meta_tokens, v_w_in, v_w_na_out, v_w_hg_out, v_w_o, v_w_up, v_w_down, v_norm_mix, v_norm_mlp, v_norm_final, v_hg_norm, v_na_rpb, v_hg_lb_logits):
    xi, yi, ci = lax.axis_index("x"), lax.axis_index("y"), lax.axis_index("c")
    chip = 2 * xi + yi
    d = x.shape[-1]
    dshard = meta_tokens.shape[1]
    hgw = hg_norm.shape[1]
    lbs = hg_lb_logits.shape[2]
    big = [w_in[0], w_na_out[0], w_hg_out[0], w_o[0], w_up[0], w_down[0]]
    big_m = [m_w_in[0], m_w_na_out[0], m_w_hg_out[0], m_w_o[0], m_w_up[0], m_w_down[0]]
    big_v = [v_w_in[0], v_w_na_out[0], v_w_hg_out[0], v_w_o[0], v_w_up[0], v_w_down[0]]

    place = jnp.stack([chip, ci]).astype(jnp.int32)
    in_axes, rest_axes = WEIGHT_AXES[:1], WEIGHT_AXES[1:]
    own_w = _cast_into_full(big, WEIGHT_AXES, place, name="cast_shards")
    small_in = jnp.concatenate([_as_rows(meta_tokens), _as_rows(hg_lb_logits)], axis=0)
    sm_send, sm_recv, sm_blk, sm_land, sm_token = _chip_exchange_start(small_in,
                                                                       name="small_params_start")
    in_send, in_recv, in_bufs, in_token = _allgather_start(own_w[:1], in_axes, sm_token,
                                                           name="weight_allgather_in_start")
    ag_send, ag_recv, ag_bufs, ag_token = _allgather_start(own_w[1:], rest_axes, in_token,
                                                           name="weight_allgather_rest_start")
    sm_land = _chip_exchange_wait(sm_send, sm_recv, sm_blk, sm_land, ag_token,
                                  name="small_params_wait")
    small_all = lax.dynamic_update_slice(sm_land, small_in[None], (chip, 0, 0))
    forward = {}

    def first_weight(after):
        got = _allgather_wait(in_send, in_recv, in_bufs, in_axes, after,
                              name="weight_allgather_in_wait")
        return _allgather_forward(got, in_axes, name="weight_allgather_in_forward")[0]

    def rest_landed(after):
        got = _allgather_wait(ag_send, ag_recv, ag_bufs, rest_axes, after,
                              name="weight_allgather_rest_wait")
        send, recv, bufs, token = _allgather_forward_start(
            got, rest_axes, name="weight_allgather_rest_forward_start")
        forward["rest"] = (send, recv, bufs)
        return token

    def rest_weights(after):
        return _allgather_forward_wait(*forward["rest"], rest_axes, after,
                                       name="weight_allgather_rest_forward_wait")

    n_meta_rows = N_META * dshard // LANES
    meta_full = (small_all[:, :n_meta_rows].reshape(N_CHIPS, N_META, dshard)
                 .transpose(1, 0, 2).reshape(N_META, d))
    lbl_full = (small_all[:, n_meta_rows:].reshape(N_CHIPS, -1)[:, :4 * lbs]
                .reshape(N_CHIPS, 2, 2, lbs).transpose(1, 2, 0, 3).reshape(2, 2, N_CHIPS * lbs))
    lb = jax.nn.softmax(lbl_full, axis=1)[:, 0]

    def by_chip(dws, axes):
        return [g.reshape(1, *g.shape) if ax == 1
                else g.reshape(N_CHIPS, g.shape[0] // N_CHIPS, g.shape[1]) for g, ax in zip(dws, axes)]

    flying = {}

    def scatter(tag, axes, g3, rx):
        parts = _pair_add(g3, rx, place, out_dtype=WIRE, name=f"grad_pair_add_{tag}")
        send, recv, parts, slots, token = _scatter_start(parts, axes,
                                                         name=f"grad_scatter_{tag}_start")
        flying[tag] = (send, recv, parts, slots)
        return token

    def swap(tag, axes):
        def start(dws):
            send, recv, g3, lands, token = _sibling_swap_start(
                by_chip(dws, axes), name=f"grad_sibling_swap_{tag}_start")
            flying["swap_" + tag] = (send, recv, g3, lands)
            return token

        def finish(after):
            g3, rx = _sibling_swap_wait(*flying["swap_" + tag], after,
                                        name=f"grad_sibling_swap_{tag}_wait")
            return scatter(tag, axes, g3, rx)
        return start, finish

    swap_rest, scatter_rest = swap("rest", rest_axes)
    swap_in, scatter_in = swap("in", in_axes)

    def landed(tag, axes, after):
        return _scatter_wait(*flying[tag], axes, after, name=f"grad_scatter_{tag}_wait")

    (loss, dx, dmeta, *_, dg_mix, dg_mlp, dg_fin, d_gain, d_rpb, d_lb) = _local_step(
        x[0], loss_target[0], meta_full, first_weight, rest_weights, norm_mix, norm_mlp,
        norm_final.reshape(1, d), hg_norm, na_rpb[0], lb, swap_rest, scatter_rest,
        lambda dw_in: swap_in([dw_in]), rest_landed, scatter_in)

    parts_rest, slots_rest = landed("rest", rest_axes, dx)
    g_rest = _sibling_share(_sum_chips(slots_rest, parts_rest, place, rest_axes,
                                       name="grad_sum_chips_rest"),
                            name="grad_sibling_share_rest")
    out_rest = _adamw(big[1:], g_rest, big_m[1:], big_v[1:], name="adamw_rest")
    parts_in, slots_in = landed("in", in_axes, out_rest[-1][0])
    half_in = _sum_chips(slots_in, parts_in, place, in_axes, name="grad_sum_chips_in")

    d_rpb_c = d_rpb[:, :2 * NA_WIN_W - 1]
    small_g = [dmeta, dg_mix, dg_mlp, dg_fin, d_gain, d_rpb_c, d_lb, loss]
    packed = jnp.concatenate([_as_rows(a) for a in small_g], axis=0)
    gathered, g_in = _gather_all(packed, shares=half_in, name="gather_small_grads")
    g_big = [g_in] + list(g_rest)
    total = _sum_slots(gathered, name="sum_small_grads")
    offs = np.cumsum([0] + [_as_rows(a).shape[0] for a in small_g])
    take = lambda i, shape: _from_rows(total[offs[i]:offs[i + 1]], shape)
    g_meta_full = take(0, (N_META, d))
    g_norm_mix, g_norm_mlp = take(1, (1, d)), take(2, (1, d))
    g_norm_final = take(3, (d,))
    g_hg_norm = take(4, (1, hgw))
    g_rpb = take(5, na_rpb.shape)
    g_lb = take(6, (2, hgw))
    loss_total = take(7, (1, LANES))[0, 0]
    g_meta = lax.dynamic_slice_in_dim(g_meta_full, chip * dshard, dshard, axis=1)
    dl0 = lb * (1.0 - lb) * g_lb
    g_lbl_full = jnp.stack([dl0, -dl0], axis=1)
    g_lbl = lax.dynamic_slice_in_dim(g_lbl_full, chip * lbs, lbs, axis=2)

    big_out = _adamw(big[:1], [g_in], big_m[:1], big_v[:1], name="adamw_in") + out_rest
    small_w = [meta_tokens, norm_mix, norm_mlp, norm_final, hg_norm, na_rpb, hg_lb_logits]
    small_gr = [g_meta, g_norm_mix, g_norm_mlp, g_norm_final, g_hg_norm, g_rpb, g_lbl]
    small_m = [m_meta_tokens, m_norm_mix, m_norm_mlp, m_norm_final, m_hg_norm, m_na_rpb, m_hg_lb_logits]
    small_v = [v_meta_tokens, v_norm_mix, v_norm_mlp, v_norm_final, v_hg_norm, v_na_rpb, v_hg_lb_logits]
    pk = lambda lst: jnp.concatenate([_as_rows(a) for a in lst], axis=0)
    ((sd, sm, sv, _),) = _adamw([pk(small_w)], [pk(small_gr)], [pk(small_m)], [pk(small_v)],
                             name="adamw_small")
    soffs = np.cumsum([0] + [_as_rows(a).shape[0] for a in small_w])
    unpk = lambda p: [_from_rows(p[soffs[i]:soffs[i + 1]], small_w[i].shape) for i in range(len(small_w))]
    sd, sm, sv = unpk(sd), unpk(sm), unpk(sv)

    def order(bigs, smalls):
        return [smalls[0]] + [b.reshape(1, *b.shape) for b in bigs] + smalls[1:]

    grads = order([o[3] for o in big_out], small_gr)
    deltas = order([o[0] for o in big_out], sd)
    new_m = order([o[1] for o in big_out], sm)
    new_v = order([o[2] for o in big_out], sv)
    return (loss_total, dx.reshape(1, *dx.shape), *grads, *deltas, *new_m, *new_v)
```

```python
import functools

import numpy as np
import jax
import jax.numpy as jnp
from jax import lax
from jax.experimental import pallas as pl
from jax.experimental.pallas import tpu as pltpu

F32 = jnp.float32
BF16 = jnp.bfloat16
HIGHEST = lax.Precision.HIGHEST

GRID_W = 64
N_META = 16
EPS = 1e-6
NA_HEAD_DIM = 64
NA_WIN_H = 8
NA_WIN_W = 16
HG_DK = 128
LANES = 128
ROW_ALIGN = 128
VMEM_LIMIT = 48 * 1024 * 1024
MATMUL_VMEM_BUDGET = 40 * 1024 * 1024

ADAM_LR = 0.001
ADAM_B1 = 0.9
ADAM_B2 = 0.999
ADAM_EPS = 1e-08
ADAM_WD = 0.01
ADAM_STEP = 10

MESH = pl.DeviceIdType.MESH


def _cp(*sem):
    return pltpu.CompilerParams(dimension_semantics=sem, vmem_limit_bytes=VMEM_LIMIT)


def _sigmoid(x):
    return 0.5 * jnp.tanh(0.5 * x) + 0.5


def _dot(a, b, dims, precision=None):
    return lax.dot_general(a, b, (dims, ((), ())), preferred_element_type=F32, precision=precision)


def _nn(a, b, **kw):
    return _dot(a, b, ((1,), (0,)), **kw)


def _nt(a, b, **kw):
    return _dot(a, b, ((1,), (1,)), **kw)


def _tn(a, b, **kw):
    return _dot(a, b, ((0,), (0,)), **kw)


def _matmul(a, b, *, ta=False, tb=False, tm=None, tn=None, tk=None, out_dtype=F32, name,
            precision=None, after=None, epilogue=None, tiles=(), out_dtypes=None):
    extra = [] if after is None else [after]
    single = out_dtypes is None
    if single:
        out_dtypes = (out_dtype,)
    n_t, n_o = len(tiles), len(out_dtypes)
    if ta:
        kdim, m = a.shape
    else:
        m, kdim = a.shape
    if tb:
        n, k2 = b.shape
    else:
        k2, n = b.shape
    assert kdim == k2, (a.shape, b.shape, ta, tb)
    whole_m = tm is None and not ta
    if tm is None:
        if ta:
            tm = next(t for t in (1024, 512, 256, 128, m) if m % t == 0)
        else:
            tm = m // 2 if (m // 2) % 16 == 0 and m > 512 else m
    if tn is None:
        wide = (1024,) if not ta and len(tiles) <= 1 else ()
        tn = next(t for t in (*wide, 512, 256, 128, n) if n % t == 0)
    if tk is None:
        tk = kdim if ta else next(t for t in (2048, 1024, 512, 256, 128, kdim) if kdim % t == 0)
    if whole_m and kdim == tk:
        blocks = (m * tk * a.dtype.itemsize + tk * tn * b.dtype.itemsize
                  + m * tn * sum(jnp.dtype(dt).itemsize for dt in out_dtypes)
                  + sum(m * tn * t.dtype.itemsize for t, _ in tiles))
        if 2 * blocks + m * tn * 4 <= MATMUL_VMEM_BUDGET:
            tm = m
    assert m % tm == 0 and n % tn == 0 and kdim % tk == 0, (m, n, kdim, tm, tn, tk)
    nk = kdim // tk
    op_dtype = F32 if precision is not None else BF16

    def body(a_ref, b_ref, *refs):
        t_refs = refs[:n_t]
        o_refs = refs[n_t + len(extra):n_t + len(extra) + n_o]
        av = a_ref[...].astype(op_dtype)
        bv = b_ref[...].astype(op_dtype)
        dims = ((0 if ta else 1,), (1 if tb else 0,))
        part = _dot(av, bv, dims, precision=precision)

        def finish(acc):
            outs = (acc,) if epilogue is None else epilogue(acc, *[t[...] for t in t_refs])
            for o_ref, val in zip(o_refs, outs):
                o_ref[...] = val.astype(o_ref.dtype)

        if nk == 1:
            finish(part)
            return
        acc_ref = refs[-1]
        kk = pl.program_id(2)

        @pl.when(kk == 0)
        def _():
            acc_ref[...] = part

        @pl.when((kk > 0) & (kk < nk - 1))
        def _():
            acc_ref[...] += part

        @pl.when(kk == nk - 1)
        def _():
            finish(acc_ref[...] + part)

    a_spec = (pl.BlockSpec((tk, tm), lambda i, j, k: (k, i)) if ta
              else pl.BlockSpec((tm, tk), lambda i, j, k: (i, k)))
    b_spec = (pl.BlockSpec((tn, tk), lambda i, j, k: (j, k)) if tb
              else pl.BlockSpec((tk, tn), lambda i, j, k: (k, j)))
    for _, off in tiles:
        assert off % tn == 0, (off, tn)
    t_specs = [pl.BlockSpec((tm, tn), functools.partial(lambda i, j, k, o: (i, o + j), o=off // tn))
               for _, off in tiles]
    o_spec = pl.BlockSpec((tm, tn), lambda i, j, k: (i, j))
    outs = pl.pallas_call(
        body,
        grid=(m // tm, n // tn, nk),
        in_specs=[a_spec, b_spec] + t_specs + [pl.BlockSpec(memory_space=pl.ANY)] * len(extra),
        out_specs=[o_spec] * n_o,
        out_shape=[jax.ShapeDtypeStruct((m, n), dt) for dt in out_dtypes],
        scratch_shapes=[pltpu.VMEM((tm, tn), F32)] if nk > 1 else [],
        compiler_params=_cp("parallel", "parallel", "arbitrary"),
        name=name,
    )(a, b, *[t for t, _ in tiles], *extra)
    return outs[0] if single else outs


def _rspec(tr, w, cb=0):
    return pl.BlockSpec((tr, w), lambda i: (i, cb))


def _fspec(shape):
    nd = len(shape)
    return pl.BlockSpec(shape, lambda i: (0,) * nd)


ROW_VMEM_BUDGET = 20 * 1024 * 1024
ROW_MIN_STEPS = 4


def _row_tile(lp, row_bytes):
    for k in range(ROW_MIN_STEPS, lp // 16 + 1):
        tr = lp // k
        if lp % k == 0 and tr % 16 == 0 and 2 * tr * row_bytes <= ROW_VMEM_BUDGET:
            return tr
    return lp


def _token_rows_copy(i, n_tiles, tr, n_tok, tok_ref, buf_ref, sem, *, to_tokens, start=True,
                     wait=True):
    assert n_tiles >= 2 and 0 < n_tok + N_META - (n_tiles - 1) * tr <= tr

    def run(tok_row, buf_row, count):
        tok = tok_ref.at[pl.ds(tok_row, count), :]
        buf = buf_ref.at[pl.ds(buf_row, count), :]
        cp = pltpu.make_async_copy(buf, tok, sem) if to_tokens else pltpu.make_async_copy(tok, buf, sem)
        if start:
            cp.start()
        if wait:
            cp.wait()

    @pl.when(i == 0)
    def _():
        run(0, N_META, tr - N_META)

    if n_tiles > 2:
        @pl.when((i > 0) & (i < n_tiles - 1))
        def _():
            run(pl.multiple_of(i * tr - N_META, 8), 0, tr)

    @pl.when(i == n_tiles - 1)
    def _():
        run((n_tiles - 1) * tr - N_META, 0, n_tok + N_META - (n_tiles - 1) * tr)


def _embed_norm(x, tgt, meta, g, *, lp, name):
    n_tok, d = x.shape
    tr = _row_tile(lp, d * (4 + 2 + 4))
    n_tiles = lp // tr

    def body(x_ref, tgt_ref, meta_ref, g_ref, h_ref, o_ref, tp_ref, buf_ref, tbuf_ref, sems):
        i = pl.program_id(0)
        buf_ref[...] = jnp.zeros_like(buf_ref)
        tbuf_ref[...] = jnp.zeros_like(tbuf_ref)

        @pl.when(i == 0)
        def _():
            buf_ref[0:N_META, :] = meta_ref[...]

        _token_rows_copy(i, n_tiles, tr, n_tok, tgt_ref, tbuf_ref, sems.at[1], to_tokens=False,
                         wait=False)
        _token_rows_copy(i, n_tiles, tr, n_tok, x_ref, buf_ref, sems.at[0], to_tokens=False)
        xv = buf_ref[...]
        h_ref[...] = xv
        r = lax.rsqrt(jnp.mean(xv * xv, axis=-1, keepdims=True) + EPS)
        o_ref[...] = (xv * r * g_ref[...]).astype(BF16)
        _token_rows_copy(i, n_tiles, tr, n_tok, tgt_ref, tbuf_ref, sems.at[1], to_tokens=False,
                         start=False)
        tp_ref[...] = tbuf_ref[...]

    return pl.pallas_call(
        body, grid=(n_tiles,),
        in_specs=[ANY, ANY, _fspec((N_META, d)), _fspec((1, d))],
        out_specs=[_rspec(tr, d), _rspec(tr, d), _rspec(tr, d)],
        out_shape=[jax.ShapeDtypeStruct((lp, d), F32), jax.ShapeDtypeStruct((lp, d), BF16),
                   jax.ShapeDtypeStruct((lp, d), F32)],
        scratch_shapes=[pltpu.VMEM((tr, d), F32), pltpu.VMEM((tr, d), F32),
                        pltpu.SemaphoreType.DMA((2,))],
        compiler_params=_cp("parallel"), name=name)(x, tgt, meta, g)


def _residual_norm(h, t, g, *, name):
    lp, d = h.shape
    tr = _row_tile(lp, d * (4 + 4 + 4 + 2))

    def body(h_ref, t_ref, g_ref, h1_ref, m_ref):
        xv = h_ref[...] + t_ref[...]
        h1_ref[...] = xv
        r = lax.rsqrt(jnp.mean(xv * xv, axis=-1, keepdims=True) + EPS)
        m_ref[...] = (xv * r * g_ref[...]).astype(BF16)

    return pl.pallas_call(
        body, grid=(lp // tr,),
        in_specs=[_rspec(tr, d), _rspec(tr, d), _fspec((1, d))],
        out_specs=[_rspec(tr, d), _rspec(tr, d)],
        out_shape=[jax.ShapeDtypeStruct((lp, d), F32), jax.ShapeDtypeStruct((lp, d), BF16)],
        compiler_params=_cp("parallel"), name=name)(h, t, g)


def _rmsnorm_bwd_add(x, g, dy, dres, *, name):
    lp, d = x.shape
    tr = _row_tile(lp, d * (4 * 4 + 2))

    def body(x_ref, g_ref, dy_ref, dr_ref, dx_ref, dx16_ref, dg_ref):
        @pl.when(pl.program_id(0) == 0)
        def _():
            dg_ref[...] = jnp.zeros_like(dg_ref)

        xv = x_ref[...]
        r = lax.rsqrt(jnp.mean(xv * xv, axis=-1, keepdims=True) + EPS)
        xh = xv * r
        dyv = dy_ref[...]
        dg_ref[...] += jnp.sum(dyv * xh, axis=0, keepdims=True)
        dxh = dyv * g_ref[...]
        dx = dr_ref[...] + r * (dxh - xh * jnp.mean(dxh * xh, axis=-1, keepdims=True))
        dx_ref[...] = dx
        dx16_ref[...] = dx.astype(BF16)

    return pl.pallas_call(
        body, grid=(lp // tr,),
        in_specs=[_rspec(tr, d), _fspec((1, d)), _rspec(tr, d), _rspec(tr, d)],
        out_specs=[_rspec(tr, d), _rspec(tr, d), _fspec((1, d))],
        out_shape=[jax.ShapeDtypeStruct((lp, d), F32), jax.ShapeDtypeStruct((lp, d), BF16),
                   jax.ShapeDtypeStruct((1, d), F32)],
        compiler_params=_cp("arbitrary"), name=name)(x, g, dy, dres)


def _rmsnorm_bwd_tokens(x, g, dy, dres, *, n_tok, name):
    lp, d = x.shape
    tr = _row_tile(lp, d * 4 * 4)
    n_tiles = lp // tr

    def body(x_ref, g_ref, dy_ref, dr_ref, dtok_ref, dmeta_ref, dg_ref, buf_ref, sem):
        i = pl.program_id(0)

        @pl.when(i == 0)
        def _():
            dg_ref[...] = jnp.zeros_like(dg_ref)

        xv = x_ref[...]
        r = lax.rsqrt(jnp.mean(xv * xv, axis=-1, keepdims=True) + EPS)
        xh = xv * r
        dyv = dy_ref[...]
        dg_ref[...] += jnp.sum(dyv * xh, axis=0, keepdims=True)
        dxh = dyv * g_ref[...]
        buf_ref[...] = dr_ref[...] + r * (dxh - xh * jnp.mean(dxh * xh, axis=-1, keepdims=True))

        @pl.when(i == 0)
        def _():
            dmeta_ref[...] = buf_ref[0:N_META, :]

        _token_rows_copy(i, n_tiles, tr, n_tok, dtok_ref, buf_ref, sem, to_tokens=True)

    return pl.pallas_call(
        body, grid=(n_tiles,),
        in_specs=[_rspec(tr, d), _fspec((1, d)), _rspec(tr, d), _rspec(tr, d)],
        out_specs=[ANY, _fspec((N_META, d)), _fspec((1, d))],
        out_shape=[jax.ShapeDtypeStruct((n_tok, d), F32), jax.ShapeDtypeStruct((N_META, d), F32),
                   jax.ShapeDtypeStruct((1, d), F32)],
        scratch_shapes=[pltpu.VMEM((tr, d), F32), pltpu.SemaphoreType.DMA],
        compiler_params=_cp("arbitrary"), name=name)(x, g, dy, dres)


def _final_loss(h1, t2, g, tgt, *, n_tok, name):
    lp, d = h1.shape
    tr = _row_tile(lp, d * (4 * 4 + 2))
    n_tiles = lp // tr

    def body(h_ref, t_ref, g_ref, tg_ref, dh_ref, dh16_ref, loss_ref, dg_ref):
        i = pl.program_id(0)

        @pl.when(i == 0)
        def _():
            loss_ref[...] = jnp.zeros_like(loss_ref)
            dg_ref[...] = jnp.zeros_like(dg_ref)

        xv = h_ref[...] + t_ref[...]
        r = lax.rsqrt(jnp.mean(xv * xv, axis=-1, keepdims=True) + EPS)
        xh = xv * r
        gv = g_ref[...]
        row = i * tr + lax.broadcasted_iota(jnp.int32, (tr, 1), 0)
        valid = (row >= N_META) & (row < N_META + n_tok)
        err = jnp.where(valid, xh * gv - tg_ref[...], 0.0)
        loss_ref[...] += jnp.sum(0.5 * err * err) / d
        dy = err / d
        dg_ref[...] += jnp.sum(dy * xh, axis=0, keepdims=True)
        dxh = dy * gv
        dh = r * (dxh - xh * jnp.mean(dxh * xh, axis=-1, keepdims=True))
        dh_ref[...] = dh
        dh16_ref[...] = dh.astype(BF16)

    return pl.pallas_call(
        body, grid=(n_tiles,),
        in_specs=[_rspec(tr, d), _rspec(tr, d), _fspec((1, d)), _rspec(tr, d)],
        out_specs=[_rspec(tr, d), _rspec(tr, d), _fspec((1, LANES)), _fspec((1, d))],
        out_shape=[jax.ShapeDtypeStruct((lp, d), F32), jax.ShapeDtypeStruct((lp, d), BF16),
                   jax.ShapeDtypeStruct((1, LANES), F32), jax.ShapeDtypeStruct((1, d), F32)],
        compiler_params=_cp("arbitrary"), name=name)(h1, t2, g, tgt)


def _hg_out(o_f, o_b, proj, gain, *, col_g, name):
    lp, w = o_f.shape
    tr = _row_tile(lp, w * (3 * 4 + 2))
    hh = w // HG_DK

    def body(of_ref, ob_ref, g_ref, gain_ref, y_ref):
        gv = g_ref[...]
        sg = gv * _sigmoid(gv)
        for h in range(hh):
            sl = slice(h * HG_DK, (h + 1) * HG_DK)
            o = of_ref[:, sl] + ob_ref[:, sl]
            r = lax.rsqrt(jnp.mean(o * o, axis=-1, keepdims=True) + EPS)
            y_ref[:, sl] = (o * r * gain_ref[:, sl] * sg[:, sl]).astype(BF16)

    return pl.pallas_call(
        body, grid=(lp // tr,),
        in_specs=[_rspec(tr, w), _rspec(tr, w), _rspec(tr, w, col_g // w), _fspec((1, w))],
        out_specs=_rspec(tr, w),
        out_shape=jax.ShapeDtypeStruct((lp, w), BF16),
        compiler_params=_cp("parallel"), name=name)(o_f, o_b, proj, gain)


def _hg_out_bwd(o_f, o_b, proj, gain, dy, *, col_g, name):
    lp, w = o_f.shape
    tr = _row_tile(lp, w * (5 * 4 + 2))
    hh = w // HG_DK

    def body(of_ref, ob_ref, g_ref, gain_ref, dy_ref, do_ref, dg_ref, dgain_ref):
        @pl.when(pl.program_id(0) == 0)
        def _():
            dgain_ref[...] = jnp.zeros_like(dgain_ref)

        for h in range(hh):
            sl = slice(h * HG_DK, (h + 1) * HG_DK)
            gv = g_ref[:, sl]
            s = _sigmoid(gv)
            sg = gv * s
            dsg = s + gv * s * (1.0 - s)
            o = of_ref[:, sl] + ob_ref[:, sl]
            r = lax.rsqrt(jnp.mean(o * o, axis=-1, keepdims=True) + EPS)
            on = o * r
            dyv = dy_ref[:, sl]
            gn = gain_ref[:, sl]
            dgain_ref[:, sl] += jnp.sum(dyv * on * sg, axis=0, keepdims=True)
            dg_ref[:, sl] = (dyv * on * gn * dsg).astype(BF16)
            don = dyv * gn * sg
            do_ref[:, sl] = r * (don - on * jnp.mean(don * on, axis=-1, keepdims=True))

    return pl.pallas_call(
        body, grid=(lp // tr,),
        in_specs=[_rspec(tr, w), _rspec(tr, w), _rspec(tr, w, col_g // w), _fspec((1, w)),
                  _rspec(tr, w)],
        out_specs=[_rspec(tr, w), _rspec(tr, w), _fspec((1, w))],
        out_shape=[jax.ShapeDtypeStruct((lp, w), F32), jax.ShapeDtypeStruct((lp, w), BF16),
                   jax.ShapeDtypeStruct((1, w), F32)],
        compiler_params=_cp("arbitrary"), name=name)(o_f, o_b, proj, gain, dy)


HG_ROWS = 128
HG_HALVES = (1, 2, 4, 8, 16, 32, 64)


def _hg_gates(zq, z, lbv):
    sq = _sigmoid(zq)
    s = _sigmoid(z)
    f = lbv + (1.0 - lbv) * s
    kk = (1.0 - lbv) * (1.0 - s)
    return zq * sq, sq, s, f, jnp.log(f), kk


def _block_cumsum(g, pos, suffix):
    x = g
    for k in HG_HALVES:
        if suffix:
            x = x + jnp.where(pos < HG_ROWS - k, pltpu.roll(x, HG_ROWS - k, 0), 0.0)
        else:
            x = x + jnp.where(pos >= k, pltpu.roll(x, k, 0), 0.0)
    return x


def _pair_levels(b, pos, reverse):
    out = []
    first = b
    for m in HG_HALVES:
        if m > 1:
            first = jnp.where((pos & (m - 1)) >= m // 2, pltpu.roll(first, m // 2, 0), first)
        nxt = pltpu.roll(first, HG_ROWS - m, 0)
        upper = (pos & (2 * m - 1)) >= m
        if reverse:
            eq = jnp.where(upper, 0.0, jnp.exp(b - nxt))
            ek = jnp.where(upper, jnp.exp(first - b), 0.0)
        else:
            eq = jnp.where(upper, jnp.exp(b - first), 0.0)
            ek = jnp.where(upper, 0.0, jnp.exp(nxt - b))
        out.append((eq, ek))
    return out


def _pair_masks(mask_ref):
    ri = lax.broadcasted_iota(jnp.int32, (HG_ROWS, HG_ROWS), 0)
    ci = lax.broadcasted_iota(jnp.int32, (HG_ROWS, HG_ROWS), 1)
    for i, m in enumerate(HG_HALVES):
        sh = m.bit_length()
        mask_ref[i] = jnp.where((ri >> sh) == (ci >> sh), 1.0, 0.0)


def _hg_scan_fwd(proj, lb, *, reverse, col_q, col_z, col_i, hh, name):
    lp = proj.shape[0]
    n_blocks = lp // HG_ROWS
    last = 0 if reverse else HG_ROWS - 1

    def body(q_ref, z_ref, i_ref, lb_ref, o_ref, st_ref, mask_ref):
        lbv = lb_ref[...]
        pos = lax.broadcasted_iota(jnp.int32, (HG_ROWS, 1), 0)
        ri = lax.broadcasted_iota(jnp.int32, (HG_ROWS, HG_ROWS), 0)
        ci = lax.broadcasted_iota(jnp.int32, (HG_ROWS, HG_ROWS), 1)
        _pair_masks(mask_ref)

        def block(bi, st):
            bb = (n_blocks - 1 - bi) if reverse else bi
            r0 = pl.multiple_of(bb * HG_ROWS, HG_ROWS)
            v16 = i_ref[pl.ds(r0, HG_ROWS), :].astype(BF16)
            qh, _, _, _, g, kk = _hg_gates(q_ref[pl.ds(r0, HG_ROWS), :],
                                           z_ref[pl.ds(r0, HG_ROWS), :], lbv)
            b = _block_cumsum(g, pos, reverse)
            bl = b[last:last + 1, :]
            qe = (qh * jnp.exp(b)).astype(BF16)
            kd = (kk * jnp.exp(bl - b)).astype(BF16)
            a = jnp.where(ri == ci, jnp.sum(qh * kk, axis=1, keepdims=True), 0.0)
            for i, (eq, ek) in enumerate(_pair_levels(b, pos, reverse)):
                a = a + mask_ref[i] * _nt((qh * eq).astype(BF16), (kk * ek).astype(BF16))
            st_ref[bb] = st
            o_ref[pl.ds(r0, HG_ROWS), :] = _nn(a.astype(BF16), v16) + _nt(qe, st.astype(BF16))
            return jnp.exp(bl) * st + _tn(v16, kd)

        lax.fori_loop(0, n_blocks, block, jnp.zeros((HG_DK, HG_DK), F32))

    cspec = lambda col: pl.BlockSpec((lp, HG_DK), lambda h: (0, col // HG_DK + h))
    return pl.pallas_call(
        body, grid=(hh,),
        in_specs=[cspec(col_q), cspec(col_z), cspec(col_i),
                  pl.BlockSpec((None, 1, HG_DK), lambda h: (h, 0, 0))],
        out_specs=[pl.BlockSpec((lp, HG_DK), lambda h: (0, h)),
                   pl.BlockSpec((None, n_blocks, HG_DK, HG_DK), lambda h: (h, 0, 0, 0))],
        out_shape=[jax.ShapeDtypeStruct((lp, hh * HG_DK), F32),
                   jax.ShapeDtypeStruct((hh, n_blocks, HG_DK, HG_DK), F32)],
        scratch_shapes=[pltpu.VMEM((len(HG_HALVES), HG_ROWS, HG_ROWS), F32)],
        compiler_params=_cp("parallel"), name=name)(proj, proj, proj, lb)


def _hg_scan_bwd(proj, lb, states, do, *, reverse, col_q, col_z, col_i, hh, name):
    lp = proj.shape[0]
    n_blocks = lp // HG_ROWS
    last = 0 if reverse else HG_ROWS - 1

    def body(q_ref, z_ref, i_ref, lb_ref, st_ref, do_ref, dq_ref, dz_ref, dv_ref, dlb_ref, mask_ref):
        lbv = lb_ref[...]
        pos = lax.broadcasted_iota(jnp.int32, (HG_ROWS, 1), 0)
        ri = lax.broadcasted_iota(jnp.int32, (HG_ROWS, HG_ROWS), 0)
        ci = lax.broadcasted_iota(jnp.int32, (HG_ROWS, HG_ROWS), 1)
        _pair_masks(mask_ref)

        def block(bi, carry):
            dst, dlb = carry
            bb = bi if reverse else (n_blocks - 1 - bi)
            r0 = pl.multiple_of(bb * HG_ROWS, HG_ROWS)
            zq = q_ref[pl.ds(r0, HG_ROWS), :]
            v16 = i_ref[pl.ds(r0, HG_ROWS), :].astype(BF16)
            do16 = do_ref[pl.ds(r0, HG_ROWS), :].astype(BF16)
            qh, sq, s, f, g, kk = _hg_gates(zq, z_ref[pl.ds(r0, HG_ROWS), :], lbv)
            b = _block_cumsum(g, pos, reverse)
            bl = b[last:last + 1, :]
            eb = jnp.exp(b)
            ebl = jnp.exp(bl - b)
            decay = jnp.exp(bl)
            qe16 = (qh * eb).astype(BF16)
            kd16 = (kk * ebl).astype(BF16)
            st = st_ref[bb]
            st16, dst16 = st.astype(BF16), dst.astype(BF16)
            same_row = ri == ci
            da = _nt(do16, v16)
            da_diag = jnp.sum(jnp.where(same_row, da, 0.0), axis=1, keepdims=True)
            dq_state = eb * _nn(do16, st16)
            dk_state = ebl * _nn(v16, dst16)
            dq = dq_state + da_diag * kk
            dk = dk_state + da_diag * qh
            dbl = (decay * jnp.sum(st * dst, axis=0, keepdims=True)
                   + jnp.sum(kk * dk_state, axis=0, keepdims=True))
            db = qh * dq_state - kk * dk_state + jnp.where(pos == last, dbl, 0.0)
            a = jnp.where(same_row, jnp.sum(qh * kk, axis=1, keepdims=True), 0.0)
            for i, (eq, ek) in enumerate(_pair_levels(b, pos, reverse)):
                same = mask_ref[i]
                q16, k16 = (qh * eq).astype(BF16), (kk * ek).astype(BF16)
                a = a + same * _nt(q16, k16)
                da16 = (same * da).astype(BF16)
                gq, gk = _nn(da16, k16), _tn(da16, q16)
                dq = dq + eq * gq
                dk = dk + ek * gk
                db = db + (q16.astype(F32) * gq - k16.astype(F32) * gk)
            dg = _block_cumsum(db, pos, not reverse)
            df = dg / f - dk
            dq_ref[pl.ds(r0, HG_ROWS), :] = dq * (sq + zq * sq * (1.0 - sq))
            dz_ref[pl.ds(r0, HG_ROWS), :] = df * (1.0 - lbv) * s * (1.0 - s)
            dv_ref[pl.ds(r0, HG_ROWS), :] = _nt(kd16, dst16) + _tn(a.astype(BF16), do16)
            return (decay * dst + _tn(do16, qe16),
                    dlb + jnp.sum(df * (1.0 - s), axis=0, keepdims=True))

        _, dlb = lax.fori_loop(0, n_blocks, block,
                               (jnp.zeros((HG_DK, HG_DK), F32), jnp.zeros((1, HG_DK), F32)))
        dlb_ref[...] = dlb

    cspec = lambda col: pl.BlockSpec((lp, HG_DK), lambda h: (0, col // HG_DK + h))
    ospec = pl.BlockSpec((lp, HG_DK), lambda h: (0, h))
    sds = jax.ShapeDtypeStruct((lp, hh * HG_DK), F32)
    return pl.pallas_call(
        body, grid=(hh,),
        in_specs=[cspec(col_q), cspec(col_z), cspec(col_i),
                  pl.BlockSpec((None, 1, HG_DK), lambda h: (h, 0, 0)),
                  pl.BlockSpec((None, n_blocks, HG_DK, HG_DK), lambda h: (h, 0, 0, 0)),
                  ospec],
        out_specs=[ospec, ospec, ospec, pl.BlockSpec((None, 1, HG_DK), lambda h: (h, 0, 0))],
        out_shape=[sds, sds, sds, jax.ShapeDtypeStruct((hh, 1, HG_DK), F32)],
        scratch_shapes=[pltpu.VMEM((len(HG_HALVES), HG_ROWS, HG_ROWS), F32)],
        compiler_params=_cp("parallel"), name=name)(proj, proj, proj, lb, states, do)


NA_HB = LANES // NA_HEAD_DIM
NA_G = 4
NA_U = NA_G + NA_WIN_H
NA_QN = NA_G * GRID_W
NA_KN = NA_U * GRID_W


def _na_table_index(pattern, a, j):
    if pattern == 0:
        return j - a + NA_WIN_H - 1 if j < NA_WIN_H else None
    if pattern == 2:
        return j - a - 1 if j >= NA_U - NA_WIN_H else None
    return j - a + NA_WIN_H // 2 - 1 if a <= j < a + NA_WIN_H else None


def _na_step_rows(pattern, t, rows):
    if pattern == 0:
        r0, us = 0, 0
    elif pattern == 2:
        r0, us = rows - NA_G, rows - NA_U
    else:
        r0 = NA_G * t
        us = r0 - NA_WIN_H // 2
    q0, k0 = N_META + GRID_W * r0, N_META + GRID_W * us
    if pattern == 1:
        q0, k0 = pl.multiple_of(q0, 16), pl.multiple_of(k0, 16)
    return q0, k0


def _na_fill_bias(tb_ref, bias_ref):
    neg = jnp.full((GRID_W, GRID_W), -1e30, F32)
    for h in range(NA_HB):
        for pattern in range(3):
            for a in range(NA_G):
                for j in range(NA_U):
                    idx = _na_table_index(pattern, a, j)
                    bias_ref[h, pattern, a * GRID_W:(a + 1) * GRID_W, j * GRID_W:(j + 1) * GRID_W] = (
                        neg if idx is None else tb_ref[h, idx])


def _na_steps(rows, step, carry):
    n_steps = rows // NA_G
    carry = step(0, 0, carry)
    carry = lax.fori_loop(1, n_steps - 1, functools.partial(step, 1), carry)
    return step(2, n_steps - 1, carry)


def _na_head_lanes():
    lane = lax.broadcasted_iota(jnp.int32, (1, LANES), 1)
    return [lane // NA_HEAD_DIM == h for h in range(NA_HB)]


def _na_only(mask, x):
    return jnp.where(mask, x, jnp.zeros_like(x))


def _na_fwd(proj, tb, *, n_tok, nh, name):
    lp = proj.shape[0]
    dh, hb = NA_HEAD_DIM, NA_HB
    naw = nh * dh
    rows = n_tok // GRID_W
    scale = dh ** -0.5

    def body(q_ref, k_ref, v_ref, tb_ref, o_ref, lse_ref, q16_ref, k16_ref, v16_ref, bias_ref):
        o_ref[...] = jnp.zeros_like(o_ref)
        lse_ref[...] = jnp.zeros_like(lse_ref)
        q16_ref[...] = q_ref[...].astype(BF16)
        k16_ref[...] = k_ref[...].astype(BF16)
        v16_ref[...] = v_ref[...].astype(BF16)
        _na_fill_bias(tb_ref, bias_ref)
        heads = _na_head_lanes()
        km = k16_ref[0:N_META, :]
        vm = v16_ref[0:N_META, :]
        qm = q16_ref[0:N_META, :]
        o_m = None
        for h in range(hb):
            s = _nt(_na_only(heads[h], qm), km) * scale
            m = jnp.max(s, axis=1, keepdims=True)
            p = jnp.exp(s - m)
            l = jnp.sum(p, axis=1, keepdims=True)
            o_h = _nn(p.astype(BF16), vm) / l
            o_m = o_h if o_m is None else jnp.where(heads[h], o_h, o_m)
            lse_ref[h, 0:N_META, :] = m + jnp.log(l)
        o_ref[0:N_META, :] = o_m

        def step(pattern, t, carry):
            q0, k0 = _na_step_rows(pattern, t, rows)
            q16 = q16_ref[pl.ds(q0, NA_QN), :]
            k16 = k16_ref[pl.ds(k0, NA_KN), :]
            v16 = v16_ref[pl.ds(k0, NA_KN), :]
            o = None
            for h in range(hb):
                q_h = _na_only(heads[h], q16)
                s = _nt(q_h, k16) * scale + bias_ref[h, pattern]
                sm = _nt(q_h, km) * scale
                m = jnp.maximum(jnp.max(s, axis=1, keepdims=True),
                                jnp.max(sm, axis=1, keepdims=True))
                p = jnp.exp(s - m)
                pm = jnp.exp(sm - m)
                l = jnp.sum(p, axis=1, keepdims=True) + jnp.sum(pm, axis=1, keepdims=True)
                o_h = (_nn(p.astype(BF16), v16) + _nn(pm.astype(BF16), vm)) / l
                o = o_h if o is None else jnp.where(heads[h], o_h, o)
                lse_ref[h, pl.ds(q0, NA_QN), :] = m + jnp.log(l)
            o_ref[pl.ds(q0, NA_QN), :] = o
            return carry

        _na_steps(rows, step, 0)

    cblk = lambda col: pl.BlockSpec((lp, LANES), lambda g: (0, col // LANES + g))
    return pl.pallas_call(
        body, grid=(nh // hb,),
        in_specs=[cblk(0), cblk(naw), cblk(2 * naw),
                  pl.BlockSpec((hb, 2 * NA_WIN_H - 1, GRID_W, GRID_W), lambda g: (g, 0, 0, 0))],
        out_specs=[cblk(0), pl.BlockSpec((hb, lp, 1), lambda g: (g, 0, 0))],
        out_shape=[jax.ShapeDtypeStruct((lp, naw), F32), jax.ShapeDtypeStruct((nh, lp, 1), F32)],
        scratch_shapes=[pltpu.VMEM((lp, LANES), BF16)] * 3 + [pltpu.VMEM((hb, 3, NA_QN, NA_KN), F32)],
        compiler_params=_cp("parallel"), name=name)(proj, proj, proj, tb)


def _na_bwd(proj, tb, o, lse, do, *, n_tok, nh, name):
    lp = proj.shape[0]
    dh, hb = NA_HEAD_DIM, NA_HB
    naw = nh * dh
    rows = n_tok // GRID_W
    scale = dh ** -0.5

    def body(q_ref, k_ref, v_ref, tb_ref, o_ref, lse_ref, do_ref, dq_ref, dk_ref, dv_ref, dtb_ref,
             q16_ref, k16_ref, v16_ref, bias_ref):
        dq_ref[...] = jnp.zeros_like(dq_ref)
        dk_ref[...] = jnp.zeros_like(dk_ref)
        dv_ref[...] = jnp.zeros_like(dv_ref)
        dtb_ref[...] = jnp.zeros_like(dtb_ref)
        q16_ref[...] = q_ref[...].astype(BF16)
        k16_ref[...] = k_ref[...].astype(BF16)
        v16_ref[...] = v_ref[...].astype(BF16)
        _na_fill_bias(tb_ref, bias_ref)
        heads = _na_head_lanes()
        km = k16_ref[0:N_META, :]
        vm = v16_ref[0:N_META, :]
        qm = q16_ref[0:N_META, :]
        dom = do_ref[0:N_META, :]
        prod = dom * o_ref[0:N_META, :]
        dq_m = None
        dkm0 = jnp.zeros((N_META, LANES), F32)
        dvm0 = jnp.zeros((N_META, LANES), F32)
        for h in range(hb):
            q_h = _na_only(heads[h], qm)
            do_h = _na_only(heads[h], dom).astype(BF16)
            p = jnp.exp(_nt(q_h, km) * scale - lse_ref[h, 0:N_META, :])
            delta = jnp.sum(_na_only(heads[h], prod), axis=1, keepdims=True)
            ds = (p * (_nt(do_h, vm) - delta)).astype(BF16)
            dq_h = _nn(ds, km) * scale
            dq_m = dq_h if dq_m is None else jnp.where(heads[h], dq_h, dq_m)
            dkm0 = dkm0 + _tn(ds, q_h) * scale
            dvm0 = dvm0 + _tn(p.astype(BF16), do_h)
        dq_ref[0:N_META, :] = dq_m

        def step(pattern, t, carry):
            dkm, dvm = carry
            q0, k0 = _na_step_rows(pattern, t, rows)
            q16 = q16_ref[pl.ds(q0, NA_QN), :]
            k16 = k16_ref[pl.ds(k0, NA_KN), :]
            v16 = v16_ref[pl.ds(k0, NA_KN), :]
            dov = do_ref[pl.ds(q0, NA_QN), :]
            prod = dov * o_ref[pl.ds(q0, NA_QN), :]
            dq = None
            dk = jnp.zeros((NA_KN, LANES), F32)
            dv = jnp.zeros((NA_KN, LANES), F32)
            for h in range(hb):
                q_h = _na_only(heads[h], q16)
                do_h = _na_only(heads[h], dov).astype(BF16)
                lse = lse_ref[h, pl.ds(q0, NA_QN), :]
                p = jnp.exp(_nt(q_h, k16) * scale + bias_ref[h, pattern] - lse)
                pm = jnp.exp(_nt(q_h, km) * scale - lse)
                delta = jnp.sum(_na_only(heads[h], prod), axis=1, keepdims=True)
                ds = p * (_nt(do_h, v16) - delta)
                dsm = (pm * (_nt(do_h, vm) - delta)).astype(BF16)
                ds16 = ds.astype(BF16)
                dq_h = (_nn(ds16, k16) + _nn(dsm, km)) * scale
                dq = dq_h if dq is None else jnp.where(heads[h], dq_h, dq)
                dk = dk + _tn(ds16, q_h) * scale
                dv = dv + _tn(p.astype(BF16), do_h)
                dkm = dkm + _tn(dsm, q_h) * scale
                dvm = dvm + _tn(pm.astype(BF16), do_h)
                for a in range(NA_G):
                    for j in range(NA_U):
                        idx = _na_table_index(pattern, a, j)
                        if idx is not None:
                            dtb_ref[h, idx] += ds[a * GRID_W:(a + 1) * GRID_W,
                                                  j * GRID_W:(j + 1) * GRID_W]
            dq_ref[pl.ds(q0, NA_QN), :] = dq
            dk_ref[pl.ds(k0, NA_KN), :] += dk
            dv_ref[pl.ds(k0, NA_KN), :] += dv
            return dkm, dvm

        dkm, dvm = _na_steps(rows, step, (dkm0, dvm0))
        dk_ref[0:N_META, :] += dkm
        dv_ref[0:N_META, :] += dvm

    cblk = lambda col: pl.BlockSpec((lp, LANES), lambda g: (0, col // LANES + g))
    tbs = pl.BlockSpec((hb, 2 * NA_WIN_H - 1, GRID_W, GRID_W), lambda g: (g, 0, 0, 0))
    sds = jax.ShapeDtypeStruct((lp, naw), F32)
    return pl.pallas_call(
        body, grid=(nh // hb,),
        in_specs=[cblk(0), cblk(naw), cblk(2 * naw), tbs, cblk(0),
                  pl.BlockSpec((hb, lp, 1), lambda g: (g, 0, 0)), cblk(0)],
        out_specs=[cblk(0), cblk(0), cblk(0), tbs],
        out_shape=[sds, sds, sds, jax.ShapeDtypeStruct(tb.shape, F32)],
        scratch_shapes=[pltpu.VMEM((lp, LANES), BF16)] * 3 + [pltpu.VMEM((hb, 3, NA_QN, NA_KN), F32)],
        compiler_params=_cp("parallel"), name=name)(proj, proj, proj, tb, o, lse, do)


def _rpb_onehot():
    c = np.arange(GRID_W)[:, None]
    w = np.arange(GRID_W)[None, :]
    cs = np.clip(c - NA_WIN_W // 2, 0, GRID_W - NA_WIN_W)
    in_win = (w >= cs) & (w < cs + NA_WIN_W)
    dc = np.clip(w - c, -(NA_WIN_W - 1), NA_WIN_W - 1) + NA_WIN_W - 1
    oh = np.zeros((LANES, GRID_W * GRID_W), np.float32)
    flat = np.arange(GRID_W * GRID_W).reshape(GRID_W, GRID_W)
    oh[dc[in_win], flat[in_win]] = 1.0
    neg = np.where(in_win, 0.0, -1e30).astype(np.float32).reshape(1, -1)
    return oh, neg


def _assemble_dproj(dq_na, dk_na, dv_na, dq_f, dq_b, dz_f, dz_b, dv_f, dv_b, dg, dgn, dgh, *, name):
    lp, naw = dq_na.shape
    hgw = dq_f.shape[1]
    d = dgn.shape[1]
    cols = 3 * naw + 5 * hgw + 2 * d
    tr = _row_tile(lp, 3 * naw * 4 + 6 * hgw * 4 + hgw * 2 + 2 * d * 2 + cols * 2)

    def body(nq_ref, nk_ref, nv_ref, qf_ref, qb_ref, zf_ref, zb_ref, vf_ref, vb_ref, g_ref, gn_ref,
             gh_ref, o_ref):
        o_ref[:, 0:naw] = nq_ref[...].astype(BF16)
        o_ref[:, naw:2 * naw] = nk_ref[...].astype(BF16)
        o_ref[:, 2 * naw:3 * naw] = nv_ref[...].astype(BF16)
        c0 = 3 * naw
        o_ref[:, c0:c0 + hgw] = (qf_ref[...] + qb_ref[...]).astype(BF16)
        o_ref[:, c0 + hgw:c0 + 2 * hgw] = zf_ref[...].astype(BF16)
        o_ref[:, c0 + 2 * hgw:c0 + 3 * hgw] = zb_ref[...].astype(BF16)
        o_ref[:, c0 + 3 * hgw:c0 + 4 * hgw] = (vf_ref[...] + vb_ref[...]).astype(BF16)
        o_ref[:, c0 + 4 * hgw:c0 + 5 * hgw] = g_ref[...]
        o_ref[:, c0 + 5 * hgw:c0 + 5 * hgw + d] = gn_ref[...]
        o_ref[:, c0 + 5 * hgw + d:] = gh_ref[...]

    hg, na = _rspec(tr, hgw), _rspec(tr, naw)
    return pl.pallas_call(
        body, grid=(lp // tr,),
        in_specs=[na, na, na, hg, hg, hg, hg, hg, hg, hg, _rspec(tr, d), _rspec(tr, d)],
        out_specs=_rspec(tr, cols),
        out_shape=jax.ShapeDtypeStruct((lp, cols), BF16),
        compiler_params=_cp("parallel"), name=name)(dq_na, dk_na, dv_na, dq_f, dq_b, dz_f, dz_b,
                                                    dv_f, dv_b, dg, dgn, dgh)


GROUP_STEPS = 8


def _group_tiles(rows, align):
    steps = GROUP_STEPS if all(r % (GROUP_STEPS * align) == 0 for r in rows) else 1
    return steps, [r // steps for r in rows]


def _adamw(ws, gs, ms, vs, *, name):
    n = len(ws)
    steps, trs = _group_tiles([w.shape[0] for w in ws], 8)

    def body(*refs):
        for i in range(n):
            w_ref, g_ref, m_ref, v_ref = refs[4 * i:4 * i + 4]
            d_ref, mo_ref, vo_ref, go_ref = refs[4 * n + 4 * i:4 * n + 4 * i + 4]
            gv = g_ref[...]
            go_ref[...] = gv
            mn = ADAM_B1 * m_ref[...] + (1.0 - ADAM_B1) * gv
            vn = ADAM_B2 * v_ref[...] + (1.0 - ADAM_B2) * (gv * gv)
            m_hat = mn / (1.0 - ADAM_B1 ** ADAM_STEP)
            v_hat = vn / (1.0 - ADAM_B2 ** ADAM_STEP)
            d_ref[...] = -ADAM_LR * (m_hat / (jnp.sqrt(v_hat) + ADAM_EPS) + ADAM_WD * w_ref[...])
            mo_ref[...] = mn
            vo_ref[...] = vn

    specs = [_rspec(tr, w.shape[1]) for tr, w in zip(trs, ws)]
    out = pl.pallas_call(
        body, grid=(steps,),
        in_specs=[s for s in specs for _ in range(4)],
        out_specs=[s for s in specs for _ in range(4)],
        out_shape=[jax.ShapeDtypeStruct(w.shape, F32) for w in ws for _ in range(4)],
        compiler_params=_cp("parallel"), name=name)(*[a for q in zip(ws, gs, ms, vs) for a in q])
    return [tuple(out[4 * i:4 * i + 4]) for i in range(n)]


def _local_step(x, tgt, meta, first_weight, rest_weights, g_mix, g_mlp, g_fin, hg_gain, rpb, lb,
                early_grads=None, mid_grads=None, late_grad=None, rest_landed=None,
                last_grads=None):
    n_tok, d = x.shape
    hgw = hg_gain.shape[1]
    nh, hh = rpb.shape[0], hgw // HG_DK
    naw = nh * NA_HEAD_DIM
    l_real = N_META + n_tok
    lp = -(-l_real // ROW_ALIGN) * ROW_ALIGN
    col_qhg = 3 * naw
    col_zf, col_zb, col_i, col_g = (col_qhg + hgw, col_qhg + 2 * hgw, col_qhg + 3 * hgw,
                                    col_qhg + 4 * hgw)
    col_gate = col_qhg + 5 * hgw

    oh_np, neg_np = _rpb_onehot()
    oh = jnp.asarray(oh_np)
    rpb_p = jnp.pad(rpb.reshape(nh * (2 * NA_WIN_H - 1), 2 * NA_WIN_W - 1),
                    ((0, 0), (0, LANES - (2 * NA_WIN_W - 1))))
    tb = _matmul(rpb_p, oh, tm=rpb_p.shape[0], tn=512, tk=LANES, precision=HIGHEST,
                 name="rpb_expand")
    tb = (tb + jnp.asarray(neg_np)).reshape(nh, 2 * NA_WIN_H - 1, GRID_W, GRID_W)

    h0, a, tgt_p = _embed_norm(x, tgt, meta, g_mix, lp=lp, name="norm_mix")
    w_in = first_weight((a, tb))
    proj = _matmul(a, w_in, name="mm_in")
    o_na, lse = _na_fwd(proj, tb, n_tok=n_tok, nh=nh, name="na_fwd")
    lb_f = lb[0].reshape(hh, 1, HG_DK)
    lb_b = lb[1].reshape(hh, 1, HG_DK)
    scan_kw = dict(col_q=col_qhg, col_i=col_i, hh=hh)
    o_f, st_f = _hg_scan_fwd(proj, lb_f, reverse=False, col_z=col_zf, name="hg_scan_f", **scan_kw)
    token = rest_landed(o_f) if rest_landed else None
    lb_b_late = lb_b if token is None else lb_b + token[0:1, 0:1]
    o_b, st_b = _hg_scan_fwd(proj, lb_b_late, reverse=True, col_z=col_zb, name="hg_scan_b",
                             **scan_kw)
    o_hg = _hg_out(o_f, o_b, proj, hg_gain, col_g=col_g, name="hg_out")
    w_na, w_hg, w_o, w_up, w_down = rest_weights(o_hg)
    y_na = _matmul(o_na, w_na, name="mm_na_out", out_dtype=BF16)
    gates = ((proj, col_gate), (proj, col_gate + d))

    def mix_gates(acc, gn, gh, yn):
        return acc, _sigmoid(gn) * yn + _sigmoid(gh) * acc

    def mix_gates_bwd(dmix, gn, gh, yn, yh):
        sn, sh = _sigmoid(gn), _sigmoid(gh)
        return dmix * sn, dmix * sh, dmix * yn * sn * (1.0 - sn), dmix * yh * sh * (1.0 - sh)

    y_hg, mix = _matmul(o_hg, w_hg, name="mm_hg_out", epilogue=mix_gates,
                        tiles=(*gates, (y_na, 0)), out_dtypes=(BF16, BF16))
    t1 = _matmul(mix, w_o, name="mm_o")
    h1, mlp_in = _residual_norm(h0, t1, g_mlp, name="resid_norm_mlp")
    u, act = _matmul(mlp_in, w_up, name="mm_up", out_dtypes=(BF16, BF16),
                     epilogue=lambda acc: (acc, jnp.square(jnp.maximum(acc, 0.0))))
    t2 = _matmul(act, w_down, name="mm_down")
    dh2, dh2_16, loss, dg_fin = _final_loss(h1, t2, g_fin, tgt_p, n_tok=n_tok, name="final_loss")

    (du,) = _matmul(dh2_16, w_down, tb=True, name="mm_down_dx", tiles=((u, 0),),
                    out_dtypes=(BF16,),
                    epilogue=lambda acc, uv: (acc * 2.0 * jnp.maximum(uv, 0.0),))
    dw_down = _matmul(act, dh2_16, ta=True, name="mm_down_dw")
    dm = _matmul(du, w_up, tb=True, name="mm_up_dx")
    dw_up = _matmul(mlp_in, du, ta=True, name="mm_up_dw")
    dh1, dh1_16, dg_mlp = _rmsnorm_bwd_add(h1, g_mlp, dm, dh2, name="norm_mlp_bwd")
    dy_na, dy_hg, dgn, dgh = _matmul(dh1_16, w_o, tb=True, name="mm_o_dx", epilogue=mix_gates_bwd,
                                     tiles=(*gates, (y_na, 0), (y_hg, 0)), out_dtypes=(BF16,) * 4)
    dw_o = _matmul(mix, dh1_16, ta=True, name="mm_o_dw")
    do_na = _matmul(dy_na, w_na, tb=True, name="mm_na_out_dx")
    dw_na = _matmul(o_na, dy_na, ta=True, name="mm_na_out_dw")
    do_hg = _matmul(dy_hg, w_hg, tb=True, name="mm_hg_out_dx")
    dw_hg = _matmul(o_hg, dy_hg, ta=True, name="mm_hg_out_dw")
    token = early_grads([dw_na, dw_hg, dw_o, dw_up, dw_down]) if early_grads else None
    if token is not None:
        hg_gain = hg_gain + token[0:1, 0:1]
    d_o, dg_hg, d_gain = _hg_out_bwd(o_f, o_b, proj, hg_gain, do_hg, col_g=col_g, name="hg_out_bwd")
    dq_f, dz_f, dv_f, dlb_f = _hg_scan_bwd(proj, lb_f, st_f, d_o, reverse=False, col_z=col_zf,
                                           name="hg_scan_f_bwd", **scan_kw)
    token = mid_grads(dq_f) if mid_grads else None
    lb_b_late = lb_b if token is None else lb_b + token[0:1, 0:1]
    dq_b, dz_b, dv_b, dlb_b = _hg_scan_bwd(proj, lb_b_late, st_b, d_o, reverse=True, col_z=col_zb,
                                           name="hg_scan_b_bwd", **scan_kw)
    dq_na, dk_na, dv_na, dtb = _na_bwd(proj, tb, o_na, lse, do_na, n_tok=n_tok, nh=nh, name="na_bwd")
    dproj = _assemble_dproj(dq_na, dk_na, dv_na, dq_f, dq_b, dz_f, dz_b, dv_f, dv_b, dg_hg, dgn,
                            dgh, name="assemble_dproj")
    dw_in = _matmul(a, dproj, ta=True, name="mm_in_dw")
    token = late_grad(dw_in) if late_grad else None
    da = _matmul(dproj, w_in, tb=True, name="mm_in_dx", after=token)
    token = last_grads(da) if last_grads else None
    g_mix_late = g_mix if token is None else g_mix + token[0:1, 0:1]
    dx, dmeta, dg_mix = _rmsnorm_bwd_tokens(h0, g_mix_late, da, dh1, n_tok=n_tok,
                                            name="norm_mix_bwd")
    d_rpb = _matmul(dtb.reshape(nh * (2 * NA_WIN_H - 1), GRID_W * GRID_W), oh, tb=True,
                    tm=nh * (2 * NA_WIN_H - 1), tn=LANES, tk=1024, precision=HIGHEST,
                    name="rpb_reduce")
    d_lb = jnp.concatenate([dlb_f.reshape(1, hgw), dlb_b.reshape(1, hgw)], axis=0)
    return (loss, dx, dmeta, dw_in, dw_na, dw_hg, dw_o, dw_up, dw_down,
            dg_mix, dg_mlp, dg_fin, d_gain, d_rpb, d_lb)


N_CHIPS = 4
N_DEV = 8
ANY = pl.BlockSpec(memory_space=pl.ANY)


def _place():
    x, y, c = lax.axis_index("x"), lax.axis_index("y"), lax.axis_index("c")
    others = []
    for j in (1, 2, 3):
        tx = (1 - x) if (j >> 1) else x
        ty = (1 - y) if (j & 1) else y
        others.append((tx, ty))
    return x, y, c, others


def _piece(ref, axis, k, half, rh, cs):
    if axis == 1:
        return ref.at[pl.ds(pl.multiple_of(half * rh, 16), rh), pl.ds(pl.multiple_of(k * cs, LANES), cs)]
    return ref.at[pl.ds(pl.multiple_of(k * 2 * rh + half * rh, 16), rh), :]


def _cast_into_full(shards, axes, place, *, name):
    n = len(shards)
    steps, trs = _group_tiles([s.shape[0] for s in shards], 16)

    def body(p_ref, *refs):
        for i in range(n):
            refs[n + i][...] = refs[i][...].astype(BF16)

    def out_spec(tr, cs, axis):
        if axis == 1:
            return pl.BlockSpec((tr, cs), lambda i, p_ref: (i, p_ref[0]))
        return pl.BlockSpec((tr, cs), lambda i, p_ref: (p_ref[0] * steps + i, 0))

    return pl.pallas_call(
        body,
        grid_spec=pltpu.PrefetchScalarGridSpec(
            num_scalar_prefetch=1, grid=(steps,),
            in_specs=[pl.BlockSpec((tr, s.shape[1]), lambda i, p_ref: (i, 0))
                      for tr, s in zip(trs, shards)],
            out_specs=[out_spec(tr, s.shape[1], ax) for tr, s, ax in zip(trs, shards, axes)]),
        out_shape=[jax.ShapeDtypeStruct((s.shape[0], s.shape[1] * N_CHIPS) if ax == 1
                                        else (s.shape[0] * N_CHIPS, s.shape[1]), BF16)
                   for s, ax in zip(shards, axes)],
        compiler_params=_cp("parallel"), name=name)(place, *shards)


HBM_SPEC = pl.BlockSpec(memory_space=pltpu.HBM)
SEM_SPEC = pl.BlockSpec(memory_space=pltpu.SEMAPHORE)
SPLIT_COPY = pltpu.CompilerParams(has_side_effects=pltpu.SideEffectType.DATAFLOW_SIDE_EFFECTING)
TOKEN = jax.ShapeDtypeStruct((8, LANES), F32)


def _geo(fulls, axes):
    out = []
    for f, ax in zip(fulls, axes):
        r, cs = (f.shape[0], f.shape[1] // N_CHIPS) if ax == 1 else (f.shape[0] // N_CHIPS, f.shape[1])
        out.append((ax, r // 2, cs))
    return out


def _gather_copies(refs, geo, send_sems, recv_sems):
    x, y, c, others = _place()
    chip = 2 * x + y
    cps = []
    for i, (ax, rh, cs) in enumerate(geo):
        mine = _piece(refs[i], ax, chip, c, rh, cs)
        for j, (tx, ty) in enumerate(others):
            cps.append(pltpu.make_async_remote_copy(
                src_ref=mine, dst_ref=mine, send_sem=send_sems.at[3 * i + j],
                recv_sem=recv_sems.at[3 * i + j], device_id=(tx, ty, c), device_id_type=MESH))
    return cps


def _allgather_start(fulls, axes, after, *, name):
    n = len(fulls)
    geo = _geo(fulls, axes)

    def body(*refs):
        w_refs = refs[:n]
        send_sems, recv_sems = refs[n + 1], refs[n + 2]
        token = refs[2 * n + 3]
        for cp in _gather_copies(w_refs, geo, send_sems, recv_sems):
            cp.start()
        token[...] = jnp.zeros_like(token)

    out = pl.pallas_call(
        body, name=name,
        out_shape=(pltpu.SemaphoreType.DMA((3 * n,)), pltpu.SemaphoreType.DMA((3 * n,)),
                   *[pltpu.HBM(f.shape, f.dtype) for f in fulls], TOKEN),
        in_specs=[HBM_SPEC] * n + [ANY],
        out_specs=(SEM_SPEC, SEM_SPEC, *[HBM_SPEC] * n, pl.BlockSpec(memory_space=pltpu.VMEM)),
        input_output_aliases={i: 2 + i for i in range(n)},
        compiler_params=SPLIT_COPY,
    )(*[pltpu.with_memory_space_constraint(f, pltpu.HBM) for f in fulls], after)
    return out[0], out[1], list(out[2:2 + n]), out[2 + n]


def _allgather_wait(send_sems, recv_sems, fulls, axes, after, *, name):
    n = len(fulls)
    geo = _geo(fulls, axes)
    afters = tuple(after) if isinstance(after, (tuple, list)) else (after,)

    def body(*refs):
        w_refs = refs[:n]
        for cp in _gather_copies(w_refs, geo, refs[n], refs[n + 1]):
            cp.wait_send()
            cp.wait_recv()

    return list(pl.pallas_call(
        body, name=name,
        out_shape=[pltpu.HBM(f.shape, f.dtype) for f in fulls],
        in_specs=[HBM_SPEC] * n + [SEM_SPEC, SEM_SPEC] + [ANY] * len(afters),
        out_specs=[HBM_SPEC] * n,
        input_output_aliases={i: i for i in range(n)},
        compiler_params=SPLIT_COPY,
    )(*fulls, send_sems, recv_sems, *afters))


def _allgather_forward(fulls, axes, *, name):
    n = len(fulls)
    geo = _geo(fulls, axes)

    def body(*refs):
        o_refs = refs[n:2 * n]
        send_sems, recv_sems = refs[2 * n:]
        x, y, c, others = _place()

        def rcopy(i, j, half, to):
            ax, rh, cs = geo[i]
            ref = _piece(o_refs[i], ax, 2 * others[j][0] + others[j][1], half, rh, cs)
            return pltpu.make_async_remote_copy(
                src_ref=ref, dst_ref=ref, send_sem=send_sems.at[3 * i + j],
                recv_sem=recv_sems.at[3 * i + j], device_id=to, device_id_type=MESH)

        cps = [rcopy(i, j, c, (x, y, 1 - c)) for i in range(n) for j in range(3)]
        for cp in cps:
            cp.start()
        for i in range(n):
            for j in range(3):
                rcopy(i, j, 1 - c, (x, y, c)).wait_recv()
        for cp in cps:
            cp.wait_send()

    return list(pl.pallas_call(
        body, in_specs=[ANY] * n, out_specs=[ANY] * n,
        out_shape=[jax.ShapeDtypeStruct(f.shape, f.dtype) for f in fulls],
        input_output_aliases={i: i for i in range(n)},
        scratch_shapes=[pltpu.SemaphoreType.DMA((3 * n,)), pltpu.SemaphoreType.DMA((3 * n,))],
        name=name)(*fulls))


def _forward_copies(refs, geo, send_sems, recv_sems):
    x, y, c, others = _place()
    cps = []
    for i, (ax, rh, cs) in enumerate(geo):
        for j, (tx, ty) in enumerate(others):
            ref = _piece(refs[i], ax, 2 * tx + ty, c, rh, cs)
            cps.append(pltpu.make_async_remote_copy(
                src_ref=ref, dst_ref=ref, send_sem=send_sems.at[3 * i + j],
                recv_sem=recv_sems.at[3 * i + j], device_id=(x, y, 1 - c), device_id_type=MESH))
    return cps


def _allgather_forward_start(fulls, axes, *, name):
    n = len(fulls)
    geo = _geo(fulls, axes)

    def body(*refs):
        token = refs[2 * n + 2]
        for cp in _forward_copies(refs[:n], geo, refs[n], refs[n + 1]):
            cp.start()
        token[...] = jnp.zeros_like(token)

    out = pl.pallas_call(
        body, name=name,
        out_shape=(pltpu.SemaphoreType.DMA((3 * n,)), pltpu.SemaphoreType.DMA((3 * n,)),
                   *[pltpu.HBM(f.shape, f.dtype) for f in fulls], TOKEN),
        in_specs=[HBM_SPEC] * n,
        out_specs=(SEM_SPEC, SEM_SPEC, *[HBM_SPEC] * n, pl.BlockSpec(memory_space=pltpu.VMEM)),
        input_output_aliases={i: 2 + i for i in range(n)},
        compiler_params=SPLIT_COPY,
    )(*fulls)
    return out[0], out[1], list(out[2:2 + n]), out[2 + n]


def _allgather_forward_wait(send_sems, recv_sems, fulls, axes, after, *, name):
    n = len(fulls)
    geo = _geo(fulls, axes)

    def body(*refs):
        for cp in _forward_copies(refs[:n], geo, refs[n], refs[n + 1]):
            cp.wait_send()
            cp.wait_recv()

    return list(pl.pallas_call(
        body, name=name,
        out_shape=[pltpu.HBM(f.shape, f.dtype) for f in fulls],
        in_specs=[HBM_SPEC] * n + [SEM_SPEC, SEM_SPEC, ANY],
        out_specs=[HBM_SPEC] * n,
        input_output_aliases={i: i for i in range(n)},
        compiler_params=SPLIT_COPY,
    )(*fulls, send_sems, recv_sems, after))


def _chip_copies(blk_ref, land_ref, send_sems, recv_sems):
    x, y, c, others = _place()
    return [pltpu.make_async_remote_copy(
        src_ref=blk_ref, dst_ref=land_ref.at[2 * x + y], send_sem=send_sems.at[j],
        recv_sem=recv_sems.at[j], device_id=(tx, ty, c), device_id_type=MESH)
        for j, (tx, ty) in enumerate(others)]


def _chip_exchange_start(blk, *, name):
    land = pltpu.with_memory_space_constraint(lax.empty((N_CHIPS, *blk.shape), blk.dtype), pltpu.HBM)

    def body(blk_ref, land_ref, send_sems, recv_sems, blk_out, land_out, token):
        for cp in _chip_copies(blk_ref, land_ref, send_sems, recv_sems):
            cp.start()
        token[...] = jnp.zeros_like(token)

    return pl.pallas_call(
        body, name=name,
        out_shape=(pltpu.SemaphoreType.DMA((3,)), pltpu.SemaphoreType.DMA((3,)),
                   pltpu.HBM(blk.shape, blk.dtype), pltpu.HBM(land.shape, land.dtype), TOKEN),
        in_specs=[HBM_SPEC] * 2,
        out_specs=(SEM_SPEC, SEM_SPEC, HBM_SPEC, HBM_SPEC, pl.BlockSpec(memory_space=pltpu.VMEM)),
        input_output_aliases={0: 2, 1: 3},
        compiler_params=SPLIT_COPY,
    )(pltpu.with_memory_space_constraint(blk, pltpu.HBM), land)


def _chip_exchange_wait(send_sems, recv_sems, blk, land, after, *, name):
    def body(blk_ref, land_ref, send_sems, recv_sems, after_ref, blk_out, land_out):
        for cp in _chip_copies(blk_ref, land_ref, send_sems, recv_sems):
            cp.wait_send()
            cp.wait_recv()

    return pl.pallas_call(
        body, name=name,
        out_shape=[pltpu.HBM(blk.shape, blk.dtype), pltpu.HBM(land.shape, land.dtype)],
        in_specs=[HBM_SPEC] * 2 + [SEM_SPEC, SEM_SPEC, ANY],
        out_specs=[HBM_SPEC] * 2,
        input_output_aliases={0: 0, 1: 1},
        compiler_params=SPLIT_COPY,
    )(blk, land, send_sems, recv_sems, after)[1]


def _scatter_geo(parts, axes):
    out = []
    for p, ax in zip(parts, axes):
        _, rh, cols = p.shape
        out.append((ax, rh, cols // N_CHIPS if ax == 1 else cols))
    return out


def _scatter_copies(p_refs, q_refs, geo, send_sems, recv_sems):
    x, y, c, others = _place()
    chip = 2 * x + y
    cps = []
    for i, (ax, rh, cw) in enumerate(geo):
        for j, (tx, ty) in enumerate(others):
            k = 2 * tx + ty
            src = (p_refs[i].at[0, :, pl.ds(pl.multiple_of(k * cw, LANES), cw)] if ax == 1
                   else p_refs[i].at[k])
            cps.append(pltpu.make_async_remote_copy(
                src_ref=src, dst_ref=q_refs[i].at[chip], send_sem=send_sems.at[3 * i + j],
                recv_sem=recv_sems.at[3 * i + j], device_id=(tx, ty, c), device_id_type=MESH))
    return cps


def _scatter_start(parts, axes, *, name):
    n = len(parts)
    geo = _scatter_geo(parts, axes)
    slots = [pltpu.HBM((N_CHIPS, rh, cw), p.dtype) for p, (_, rh, cw) in zip(parts, geo)]

    def body(*refs):
        p_refs, q_refs = refs[:n], refs[n:2 * n]
        send_sems, recv_sems = refs[2 * n], refs[2 * n + 1]
        token = refs[4 * n + 2]
        for cp in _scatter_copies(p_refs, q_refs, geo, send_sems, recv_sems):
            cp.start()
        token[...] = jnp.zeros_like(token)

    land = [pltpu.with_memory_space_constraint(lax.empty(s.inner_aval.shape, s.inner_aval.dtype), pltpu.HBM)
            for s in slots]
    out = pl.pallas_call(
        body, name=name,
        out_shape=(pltpu.SemaphoreType.DMA((3 * n,)), pltpu.SemaphoreType.DMA((3 * n,)),
                   *[pltpu.HBM(p.shape, p.dtype) for p in parts], *slots, TOKEN),
        in_specs=[HBM_SPEC] * (2 * n),
        out_specs=(SEM_SPEC, SEM_SPEC, *[HBM_SPEC] * (2 * n), pl.BlockSpec(memory_space=pltpu.VMEM)),
        input_output_aliases={i: 2 + i for i in range(2 * n)},
        compiler_params=SPLIT_COPY,
    )(*[pltpu.with_memory_space_constraint(p, pltpu.HBM) for p in parts], *land)
    return out[0], out[1], list(out[2:2 + n]), list(out[2 + n:2 + 2 * n]), out[2 + 2 * n]


def _scatter_wait(send_sems, recv_sems, parts, slots, axes, after, *, name):
    n = len(parts)
    geo = _scatter_geo(parts, axes)

    def body(*refs):
        p_refs, q_refs = refs[:n], refs[n:2 * n]
        for cp in _scatter_copies(p_refs, q_refs, geo, refs[2 * n], refs[2 * n + 1]):
            cp.wait_send()
            cp.wait_recv()

    out = pl.pallas_call(
        body, name=name,
        out_shape=[pltpu.HBM(a.shape, a.dtype) for a in (*parts, *slots)],
        in_specs=[HBM_SPEC] * (2 * n) + [SEM_SPEC, SEM_SPEC, ANY],
        out_specs=[HBM_SPEC] * (2 * n),
        input_output_aliases={i: i for i in range(2 * n)},
        compiler_params=SPLIT_COPY,
    )(*parts, *slots, send_sems, recv_sems, after)
    return list(out[:n]), list(out[n:])


def _swap_copies(g_refs, r_refs, shapes, send_sems, recv_sems):
    x, y, c, _ = _place()
    cps = []
    for i, shape in enumerate(shapes):
        rh = shape[1] // 2
        src = g_refs[i].at[:, pl.ds(pl.multiple_of((1 - c) * rh, 16), rh), :]
        cps.append(pltpu.make_async_remote_copy(
            src_ref=src, dst_ref=r_refs[i], send_sem=send_sems.at[i], recv_sem=recv_sems.at[i],
            device_id=(x, y, 1 - c), device_id_type=MESH))
    return cps


def _sibling_swap_start(grads, *, name):
    n = len(grads)
    shapes = [g.shape for g in grads]
    lands = [pltpu.HBM((s[0], s[1] // 2, s[2]), g.dtype) for s, g in zip(shapes, grads)]

    def body(*refs):
        g_refs, r_refs = refs[:n], refs[n:2 * n]
        token = refs[4 * n + 2]
        for cp in _swap_copies(g_refs, r_refs, shapes, refs[2 * n], refs[2 * n + 1]):
            cp.start()
        token[...] = jnp.zeros_like(token)

    land = [pltpu.with_memory_space_constraint(lax.empty(s.inner_aval.shape, s.inner_aval.dtype), pltpu.HBM)
            for s in lands]
    out = pl.pallas_call(
        body, name=name,
        out_shape=(pltpu.SemaphoreType.DMA((n,)), pltpu.SemaphoreType.DMA((n,)),
                   *[pltpu.HBM(g.shape, g.dtype) for g in grads], *lands, TOKEN),
        in_specs=[HBM_SPEC] * (2 * n),
        out_specs=(SEM_SPEC, SEM_SPEC, *[HBM_SPEC] * (2 * n), pl.BlockSpec(memory_space=pltpu.VMEM)),
        input_output_aliases={i: 2 + i for i in range(2 * n)},
        compiler_params=SPLIT_COPY,
    )(*[pltpu.with_memory_space_constraint(g, pltpu.HBM) for g in grads], *land)
    return out[0], out[1], list(out[2:2 + n]), list(out[2 + n:2 + 2 * n]), out[2 + 2 * n]


def _sibling_swap_wait(send_sems, recv_sems, grads, lands, after, *, name):
    n = len(grads)
    shapes = [g.shape for g in grads]

    def body(*refs):
        g_refs, r_refs = refs[:n], refs[n:2 * n]
        for cp in _swap_copies(g_refs, r_refs, shapes, refs[2 * n], refs[2 * n + 1]):
            cp.wait_send()
            cp.wait_recv()

    out = pl.pallas_call(
        body, name=name,
        out_shape=[pltpu.HBM(a.shape, a.dtype) for a in (*grads, *lands)],
        in_specs=[HBM_SPEC] * (2 * n) + [SEM_SPEC, SEM_SPEC, ANY],
        out_specs=[HBM_SPEC] * (2 * n),
        input_output_aliases={i: i for i in range(2 * n)},
        compiler_params=SPLIT_COPY,
    )(*grads, *lands, send_sems, recv_sems, after)
    return list(out[:n]), list(out[n:])


def _pair_add(g3s, rxs, place, *, out_dtype, name):
    n = len(g3s)
    steps, trs = _group_tiles([g.shape[1] // 2 for g in g3s], 16)

    def body(p_ref, *refs):
        for i in range(n):
            refs[2 * n + i][...] = (refs[2 * i][...] + refs[2 * i + 1][...]).astype(out_dtype)

    in_specs, out_specs = [], []
    for g, tr in zip(g3s, trs):
        blk = (g.shape[0], tr, g.shape[2])
        in_specs += [pl.BlockSpec(blk, lambda i, p_ref: (0, p_ref[1] * steps + i, 0)),
                     pl.BlockSpec(blk, lambda i, p_ref: (0, i, 0))]
        out_specs.append(pl.BlockSpec(blk, lambda i, p_ref: (0, i, 0)))
    return pl.pallas_call(
        body,
        grid_spec=pltpu.PrefetchScalarGridSpec(
            num_scalar_prefetch=1, grid=(steps,), in_specs=in_specs, out_specs=out_specs),
        out_shape=[jax.ShapeDtypeStruct((g.shape[0], g.shape[1] // 2, g.shape[2]), out_dtype)
                   for g in g3s],
        compiler_params=_cp("parallel"), name=name)(place, *[a for q in zip(g3s, rxs) for a in q])


def _sum_slots(q, *, name):
    ns, rows, cols = q.shape
    tr = next(t for t in (128, 64, 32, 16, 8) if rows % t == 0)

    def body(q_ref, o_ref):
        acc = q_ref[0].astype(F32)
        for k in range(1, ns):
            acc = acc + q_ref[k].astype(F32)
        o_ref[...] = acc

    return pl.pallas_call(
        body, grid=(rows // tr,),
        in_specs=[pl.BlockSpec((ns, tr, cols), lambda i: (0, i, 0))],
        out_specs=_rspec(tr, cols),
        out_shape=jax.ShapeDtypeStruct((rows, cols), F32),
        compiler_params=_cp("parallel"), name=name)(q)


def _sum_chips(qs, ps, place, axes, *, name):
    n = len(qs)
    per = N_CHIPS + 1
    steps, trs = _group_tiles([q.shape[1] for q in qs], 16)

    def body(p_ref, *refs):
        chip = p_ref[0]
        for i in range(n):
            q_refs, own_ref = refs[per * i:per * i + N_CHIPS], refs[per * i + N_CHIPS]
            acc = jnp.where(chip == 0, own_ref[...], q_refs[0][...]).astype(F32)
            for k in range(1, N_CHIPS):
                acc = acc + jnp.where(chip == k, own_ref[...], q_refs[k][...]).astype(F32)
            refs[per * n + i][...] = acc

    def slot_spec(k, tr, cw):
        return pl.BlockSpec((None, tr, cw),
                            lambda i, p_ref: (jnp.where(p_ref[0] == k, (k + 1) % N_CHIPS, k), i, 0))

    in_specs, out_specs, operands = [], [], []
    for q, p, ax, tr in zip(qs, ps, axes, trs):
        cw = q.shape[2]
        in_specs += [slot_spec(k, tr, cw) for k in range(N_CHIPS)]
        in_specs.append(pl.BlockSpec((None, tr, cw), (lambda i, p_ref: (0, i, p_ref[0])) if ax == 1
                                     else (lambda i, p_ref: (p_ref[0], i, 0))))
        out_specs.append(pl.BlockSpec((tr, cw), lambda i, p_ref: (p_ref[1] * steps + i, 0)))
        operands += [q] * N_CHIPS + [p]
    return pl.pallas_call(
        body,
        grid_spec=pltpu.PrefetchScalarGridSpec(
            num_scalar_prefetch=1, grid=(steps,), in_specs=in_specs, out_specs=out_specs),
        out_shape=[jax.ShapeDtypeStruct((2 * q.shape[1], q.shape[2]), F32) for q in qs],
        compiler_params=_cp("parallel"), name=name)(place, *operands)


def _sibling_share(shards, *, name):
    n = len(shards)

    def body(*refs):
        o_refs = refs[n:2 * n]
        send_sems, recv_sems = refs[2 * n:]
        x, y, c, _ = _place()
        cps = []
        for i in range(n):
            rh = shards[i].shape[0] // 2
            mine = o_refs[i].at[pl.ds(pl.multiple_of(c * rh, 8), rh), :]
            cp = pltpu.make_async_remote_copy(
                src_ref=mine, dst_ref=mine, send_sem=send_sems.at[i], recv_sem=recv_sems.at[i],
                device_id=(x, y, 1 - c), device_id_type=MESH)
            cp.start()
            cps.append(cp)
        for i in range(n):
            rh = shards[i].shape[0] // 2
            theirs = o_refs[i].at[pl.ds(pl.multiple_of((1 - c) * rh, 8), rh), :]
            pltpu.make_async_remote_copy(
                src_ref=theirs, dst_ref=theirs, send_sem=send_sems.at[i], recv_sem=recv_sems.at[i],
                device_id=(x, y, c), device_id_type=MESH).wait_recv()
        for cp in cps:
            cp.wait_send()

    return pl.pallas_call(
        body, in_specs=[ANY] * n, out_specs=[ANY] * n,
        out_shape=[jax.ShapeDtypeStruct(h.shape, h.dtype) for h in shards],
        input_output_aliases={i: i for i in range(n)},
        scratch_shapes=[pltpu.SemaphoreType.DMA((n,)), pltpu.SemaphoreType.DMA((n,))],
        name=name)(*shards)


def _gather_all(blk, *, name, after=None, shares=()):
    rows, cols = blk.shape
    extra = [] if after is None else [after]
    n_s = len(shares)

    def body(x_ref, *refs):
        s_refs = refs[len(extra) + n_s + 1:len(extra) + 2 * n_s + 1]
        out_ref = refs[len(extra) + n_s]
        send_sems, recv_sems, local_sem, s_send, s_recv = refs[len(extra) + 2 * n_s + 1:]
        x, y, c = lax.axis_index("x"), lax.axis_index("y"), lax.axis_index("c")
        me = 4 * x + 2 * y + c
        mine = pltpu.make_async_copy(x_ref, out_ref.at[me], local_sem)
        mine.start()
        cps = []
        for i in range(n_s):
            rh = shares[i].shape[0] // 2
            half = s_refs[i].at[pl.ds(pl.multiple_of(c * rh, 8), rh), :]
            cp = pltpu.make_async_remote_copy(
                src_ref=half, dst_ref=half, send_sem=s_send.at[i], recv_sem=s_recv.at[i],
                device_id=(x, y, 1 - c), device_id_type=MESH)
            cp.start()
            cps.append(cp)
        for k in range(1, N_DEV):
            tx = (1 - x) if (k >> 2) & 1 else x
            ty = (1 - y) if (k >> 1) & 1 else y
            tc = (1 - c) if k & 1 else c
            cp = pltpu.make_async_remote_copy(
                src_ref=x_ref, dst_ref=out_ref.at[me], send_sem=send_sems.at[k - 1],
                recv_sem=recv_sems.at[k - 1], device_id=(tx, ty, tc), device_id_type=MESH)
            cp.start()
            cps.append(cp)
        for k in range(1, N_DEV):
            tx = (1 - x) if (k >> 2) & 1 else x
            ty = (1 - y) if (k >> 1) & 1 else y
            tc = (1 - c) if k & 1 else c
            got = out_ref.at[4 * tx + 2 * ty + tc]
            pltpu.make_async_remote_copy(
                src_ref=got, dst_ref=got, send_sem=send_sems.at[k - 1], recv_sem=recv_sems.at[k - 1],
                device_id=(x, y, c), device_id_type=MESH).wait_recv()
        for i in range(n_s):
            rh = shares[i].shape[0] // 2
            theirs = s_refs[i].at[pl.ds(pl.multiple_of((1 - c) * rh, 8), rh), :]
            pltpu.make_async_remote_copy(
                src_ref=theirs, dst_ref=theirs, send_sem=s_send.at[i], recv_sem=s_recv.at[i],
                device_id=(x, y, c), device_id_type=MESH).wait_recv()
        for cp in cps:
            cp.wait_send()
        mine.wait()

    vm = pl.BlockSpec(memory_space=pltpu.VMEM)
    out = pl.pallas_call(
        body, in_specs=[vm] + [ANY] * (len(extra) + n_s), out_specs=[vm] + [ANY] * n_s,
        out_shape=[jax.ShapeDtypeStruct((N_DEV, rows, cols), blk.dtype)]
        + [jax.ShapeDtypeStruct(s.shape, s.dtype) for s in shares],
        input_output_aliases={1 + len(extra) + i: 1 + i for i in range(n_s)},
        scratch_shapes=[pltpu.SemaphoreType.DMA((N_DEV - 1,)), pltpu.SemaphoreType.DMA((N_DEV - 1,)),
                        pltpu.SemaphoreType.DMA, pltpu.SemaphoreType.DMA((max(n_s, 1),)),
                        pltpu.SemaphoreType.DMA((max(n_s, 1),))],
        name=name)(blk, *extra, *shares)
    return (out[0], *out[1:]) if n_s else out[0]


def _as_rows(a):
    flat = a.reshape(-1)
    n = flat.shape[0]
    rows = -(-n // (8 * LANES)) * 8
    return jnp.pad(flat, (0, rows * LANES - n)).reshape(rows, LANES)


def _from_rows(p, shape):
    n = int(np.prod(shape))
    return p.reshape(-1)[:n].reshape(shape)


WEIGHT_AXES = (1, 1, 1, 0, 1, 0)
WIRE = BF16


def kernel(x, meta_tokens, w_in, w_na_out, w_hg_out, w_o, w_up, w_down, norm_mix, norm_mlp, norm_final, hg_norm, na_rpb, hg_lb_logits, loss_target, m_meta_tokens, m_w_in, m_w_na_out, m_w_hg_out, m_w_o, m_w_up, m_w_down, m_norm_mix, m_norm_mlp, m_norm_final, m_hg_norm, m_na_rpb, m_hg_lb_logits, v_meta_tokens, v_w_in, v_w_na_out, v_w_hg_out, v_w_o, v_w_up, v_w_down, v_norm_mix, v_norm_mlp, v_norm_final, v_hg_norm, v_na_rpb, v_hg_lb_logits):
    xi, yi, ci = lax.axis_index("x"), lax.axis_index("y"), lax.axis_index("c")
    chip = 2 * xi + yi
    d = x.shape[-1]
    dshard = meta_tokens.shape[1]
    hgw = hg_norm.shape[1]
    lbs = hg_lb_logits.shape[2]
    big = [w_in[0], w_na_out[0], w_hg_out[0], w_o[0], w_up[0], w_down[0]]
    big_m = [m_w_in[0], m_w_na_out[0], m_w_hg_out[0], m_w_o[0], m_w_up[0], m_w_down[0]]
    big_v = [v_w_in[0], v_w_na_out[0], v_w_hg_out[0], v_w_o[0], v_w_up[0], v_w_down[0]]

    place = jnp.stack([chip, ci]).astype(jnp.int32)
    in_axes, rest_axes = WEIGHT_AXES[:1], WEIGHT_AXES[1:]
    own_w = _cast_into_full(big, WEIGHT_AXES, place, name="cast_shards")
    small_in = jnp.concatenate([_as_rows(meta_tokens), _as_rows(hg_lb_logits)], axis=0)
    sm_send, sm_recv, sm_blk, sm_land, sm_token = _chip_exchange_start(small_in,
                                                                       name="small_params_start")
    in_send, in_recv, in_bufs, in_token = _allgather_start(own_w[:1], in_axes, sm_token,
                                                           name="weight_allgather_in_start")
    ag_send, ag_recv, ag_bufs, ag_token = _allgather_start(own_w[1:], rest_axes, in_token,
                                                           name="weight_allgather_rest_start")
    sm_land = _chip_exchange_wait(sm_send, sm_recv, sm_blk, sm_land, ag_token,
                                  name="small_params_wait")
    small_all = lax.dynamic_update_slice(sm_land, small_in[None], (chip, 0, 0))
    forward = {}

    def first_weight(after):
        got = _allgather_wait(in_send, in_recv, in_bufs, in_axes, after,
                              name="weight_allgather_in_wait")
        return _allgather_forward(got, in_axes, name="weight_allgather_in_forward")[0]

    def rest_landed(after):
        got = _allgather_wait(ag_send, ag_recv, ag_bufs, rest_axes, after,
                              name="weight_allgather_rest_wait")
        send, recv, bufs, token = _allgather_forward_start(
            got, rest_axes, name="weight_allgather_rest_forward_start")
        forward["rest"] = (send, recv, bufs)
        return token

    def rest_weights(after):
        return _allgather_forward_wait(*forward["rest"], rest_axes, after,
                                       name="weight_allgather_rest_forward_wait")

    n_meta_rows = N_META * dshard // LANES
    meta_full = (small_all[:, :n_meta_rows].reshape(N_CHIPS, N_META, dshard)
                 .transpose(1, 0, 2).reshape(N_META, d))
    lbl_full = (small_all[:, n_meta_rows:].reshape(N_CHIPS, -1)[:, :4 * lbs]
                .reshape(N_CHIPS, 2, 2, lbs).transpose(1, 2, 0, 3).reshape(2, 2, N_CHIPS * lbs))
    lb = jax.nn.softmax(lbl_full, axis=1)[:, 0]

    def by_chip(dws, axes):
        return [g.reshape(1, *g.shape) if ax == 1
                else g.reshape(N_CHIPS, g.shape[0] // N_CHIPS, g.shape[1]) for g, ax in zip(dws, axes)]

    flying = {}

    def scatter(tag, axes, g3, rx):
        parts = _pair_add(g3, rx, place, out_dtype=WIRE, name=f"grad_pair_add_{tag}")
        send, recv, parts, slots, token = _scatter_start(parts, axes,
                                                         name=f"grad_scatter_{tag}_start")
        flying[tag] = (send, recv, parts, slots)
        return token

    def swap(tag, axes):
        def start(dws):
            send, recv, g3, lands, token = _sibling_swap_start(
                by_chip(dws, axes), name=f"grad_sibling_swap_{tag}_start")
            flying["swap_" + tag] = (send, recv, g3, lands)
            return token

        def finish(after):
            g3, rx = _sibling_swap_wait(*flying["swap_" + tag], after,
                                        name=f"grad_sibling_swap_{tag}_wait")
            return scatter(tag, axes, g3, rx)
        return start, finish

    swap_rest, scatter_rest = swap("rest", rest_axes)
    swap_in, scatter_in = swap("in", in_axes)

    def landed(tag, axes, after):
        return _scatter_wait(*flying[tag], axes, after, name=f"grad_scatter_{tag}_wait")

    (loss, dx, dmeta, *_, dg_mix, dg_mlp, dg_fin, d_gain, d_rpb, d_lb) = _local_step(
        x[0], loss_target[0], meta_full, first_weight, rest_weights, norm_mix, norm_mlp,
        norm_final.reshape(1, d), hg_norm, na_rpb[0], lb, swap_rest, scatter_rest,
        lambda dw_in: swap_in([dw_in]), rest_landed, scatter_in)

    parts_rest, slots_rest = landed("rest", rest_axes, dx)
    g_rest = _sibling_share(_sum_chips(slots_rest, parts_rest, place, rest_axes,
                                       name="grad_sum_chips_rest"),
                            name="grad_sibling_share_rest")
    out_rest = _adamw(big[1:], g_rest, big_m[1:], big_v[1:], name="adamw_rest")
    parts_in, slots_in = landed("in", in_axes, out_rest[-1][0])
    half_in = _sum_chips(slots_in, parts_in, place, in_axes, name="grad_sum_chips_in")

    d_rpb_c = d_rpb[:, :2 * NA_WIN_W - 1]
    small_g = [dmeta, dg_mix, dg_mlp, dg_fin, d_gain, d_rpb_c, d_lb, loss]
    packed = jnp.concatenate([_as_rows(a) for a in small_g], axis=0)
    gathered, g_in = _gather_all(packed, shares=half_in, name="gather_small_grads")
    total = _sum_slots(gathered, name="sum_small_grads")
    offs = np.cumsum([0] + [_as_rows(a).shape[0] for a in small_g])
    take = lambda i, shape: _from_rows(total[offs[i]:offs[i + 1]], shape)
    g_meta_full = take(0, (N_META, d))
    g_norm_mix, g_norm_mlp = take(1, (1, d)), take(2, (1, d))
    g_norm_final = take(3, (d,))
    g_hg_norm = take(4, (1, hgw))
    g_rpb = take(5, na_rpb.shape)
    g_lb = take(6, (2, hgw))
    loss_total = take(7, (1, LANES))[0, 0]
    g_meta = lax.dynamic_slice_in_dim(g_meta_full, chip * dshard, dshard, axis=1)
    dl0 = lb * (1.0 - lb) * g_lb
    g_lbl_full = jnp.stack([dl0, -dl0], axis=1)
    g_lbl = lax.dynamic_slice_in_dim(g_lbl_full, chip * lbs, lbs, axis=2)

    big_out = _adamw(big[:1], [g_in], big_m[:1], big_v[:1], name="adamw_in") + out_rest
    small_w = [meta_tokens, norm_mix, norm_mlp, norm_final, hg_norm, na_rpb, hg_lb_logits]
    small_gr = [g_meta, g_norm_mix, g_norm_mlp, g_norm_final, g_hg_norm, g_rpb, g_lbl]
    small_m = [m_meta_tokens, m_norm_mix, m_norm_mlp, m_norm_final, m_hg_norm, m_na_rpb, m_hg_lb_logits]
    small_v = [v_meta_tokens, v_norm_mix, v_norm_mlp, v_norm_final, v_hg_norm, v_na_rpb, v_hg_lb_logits]
    pk = lambda lst: jnp.concatenate([_as_rows(a) for a in lst], axis=0)
    ((sd, sm, sv, _),) = _adamw([pk(small_w)], [pk(small_gr)], [pk(small_m)], [pk(small_v)],
                             name="adamw_small")
    soffs = np.cumsum([0] + [_as_rows(a).shape[0] for a in small_w])
    unpk = lambda p: [_from_rows(p[soffs[i]:soffs[i + 1]], small_w[i].shape) for i in range(len(small_w))]
    sd, sm, sv = unpk(sd), unpk(sm), unpk(sv)

    def order(bigs, smalls):
        return [smalls[0]] + [b.reshape(1, *b.shape) for b in bigs] + smalls[1:]

    grads = order([o[3] for o in big_out], small_gr)
    deltas = order([o[0] for o in big_out], sd)
    new_m = order([o[1] for o in big_out], sm)
    new_v = order([o[2] for o in big_out], sv)
    return (loss_total, dx.reshape(1, *dx.shape), *grads, *deltas, *new_m, *new_v)
```

```python
import functools

import numpy as np
import jax
import jax.numpy as jnp
from jax import lax
from jax.experimental import pallas as pl
from jax.experimental.pallas import tpu as pltpu

F32 = jnp.float32
BF16 = jnp.bfloat16
HIGHEST = lax.Precision.HIGHEST

GRID_W = 64
N_META = 16
EPS = 1e-6
NA_HEAD_DIM = 64
NA_WIN_H = 8
NA_WIN_W = 16
HG_DK = 128
LANES = 128
ROW_ALIGN = 128
VMEM_LIMIT = 48 * 1024 * 1024
ADAM_LR = 0.001
ADAM_B1 = 0.9
ADAM_B2 = 0.999
ADAM_EPS = 1e-08
ADAM_WD = 0.01
ADAM_STEP = 10

MESH = pl.DeviceIdType.MESH


def _cp(*sem):
    return pltpu.CompilerParams(dimension_semantics=sem, vmem_limit_bytes=VMEM_LIMIT)


def _sigmoid(x):
    return 0.5 * jnp.tanh(0.5 * x) + 0.5


def _dot(a, b, dims, precision=None):
    return lax.dot_general(a, b, (dims, ((), ())), preferred_element_type=F32, precision=precision)


def _nn(a, b, **kw):
    return _dot(a, b, ((1,), (0,)), **kw)


def _nt(a, b, **kw):
    return _dot(a, b, ((1,), (1,)), **kw)


def _tn(a, b, **kw):
    return _dot(a, b, ((0,), (0,)), **kw)


def _matmul(a, b, *, ta=False, tb=False, tm=None, tn=None, tk=None, out_dtype=F32, name,
            precision=None, after=None, epilogue=None, tiles=(), out_dtypes=None):
    extra = [] if after is None else [after]
    single = out_dtypes is None
    if single:
        out_dtypes = (out_dtype,)
    n_t, n_o = len(tiles), len(out_dtypes)
    if ta:
        kdim, m = a.shape
    else:
        m, kdim = a.shape
    if tb:
        n, k2 = b.shape
    else:
        k2, n = b.shape
    assert kdim == k2, (a.shape, b.shape, ta, tb)
    if tm is None:
        if ta:
            tm = next(t for t in (1024, 512, 256, 128, m) if m % t == 0)
        else:
            tm = m // 2 if (m // 2) % 16 == 0 and m > 512 else m
    if tn is None:
        wide = (1024,) if not ta and len(tiles) <= 1 else ()
        tn = next(t for t in (*wide, 512, 256, 128, n) if n % t == 0)
    if tk is None:
        tk = kdim if ta else next(t for t in (2048, 1024, 512, 256, 128, kdim) if kdim % t == 0)
    assert m % tm == 0 and n % tn == 0 and kdim % tk == 0, (m, n, kdim, tm, tn, tk)
    nk = kdim // tk
    op_dtype = F32 if precision is not None else BF16

    def body(a_ref, b_ref, *refs):
        t_refs = refs[:n_t]
        o_refs = refs[n_t + len(extra):n_t + len(extra) + n_o]
        av = a_ref[...].astype(op_dtype)
        bv = b_ref[...].astype(op_dtype)
        dims = ((0 if ta else 1,), (1 if tb else 0,))
        part = _dot(av, bv, dims, precision=precision)

        def finish(acc):
            outs = (acc,) if epilogue is None else epilogue(acc, *[t[...] for t in t_refs])
            for o_ref, val in zip(o_refs, outs):
                o_ref[...] = val.astype(o_ref.dtype)

        if nk == 1:
            finish(part)
            return
        acc_ref = refs[-1]
        kk = pl.program_id(2)

        @pl.when(kk == 0)
        def _():
            acc_ref[...] = part

        @pl.when((kk > 0) & (kk < nk - 1))
        def _():
            acc_ref[...] += part

        @pl.when(kk == nk - 1)
        def _():
            finish(acc_ref[...] + part)

    a_spec = (pl.BlockSpec((tk, tm), lambda i, j, k: (k, i)) if ta
              else pl.BlockSpec((tm, tk), lambda i, j, k: (i, k)))
    b_spec = (pl.BlockSpec((tn, tk), lambda i, j, k: (j, k)) if tb
              else pl.BlockSpec((tk, tn), lambda i, j, k: (k, j)))
    for _, off in tiles:
        assert off % tn == 0, (off, tn)
    t_specs = [pl.BlockSpec((tm, tn), functools.partial(lambda i, j, k, o: (i, o + j), o=off // tn))
               for _, off in tiles]
    o_spec = pl.BlockSpec((tm, tn), lambda i, j, k: (i, j))
    outs = pl.pallas_call(
        body,
        grid=(m // tm, n // tn, nk),
        in_specs=[a_spec, b_spec] + t_specs + [pl.BlockSpec(memory_space=pl.ANY)] * len(extra),
        out_specs=[o_spec] * n_o,
        out_shape=[jax.ShapeDtypeStruct((m, n), dt) for dt in out_dtypes],
        scratch_shapes=[pltpu.VMEM((tm, tn), F32)] if nk > 1 else [],
        compiler_params=_cp("parallel", "parallel", "arbitrary"),
        name=name,
    )(a, b, *[t for t, _ in tiles], *extra)
    return outs[0] if single else outs


def _rspec(tr, w, cb=0):
    return pl.BlockSpec((tr, w), lambda i: (i, cb))


def _fspec(shape):
    nd = len(shape)
    return pl.BlockSpec(shape, lambda i: (0,) * nd)


ROW_VMEM_BUDGET = 20 * 1024 * 1024
ROW_MIN_STEPS = 4


def _row_tile(lp, row_bytes):
    for k in range(ROW_MIN_STEPS, lp // 16 + 1):
        tr = lp // k
        if lp % k == 0 and tr % 16 == 0 and 2 * tr * row_bytes <= ROW_VMEM_BUDGET:
            return tr
    return lp


def _token_rows_copy(i, n_tiles, tr, n_tok, tok_ref, buf_ref, sem, *, to_tokens, start=True,
                     wait=True):
    assert n_tiles >= 2 and 0 < n_tok + N_META - (n_tiles - 1) * tr <= tr

    def run(tok_row, buf_row, count):
        tok = tok_ref.at[pl.ds(tok_row, count), :]
        buf = buf_ref.at[pl.ds(buf_row, count), :]
        cp = pltpu.make_async_copy(buf, tok, sem) if to_tokens else pltpu.make_async_copy(tok, buf, sem)
        if start:
            cp.start()
        if wait:
            cp.wait()

    @pl.when(i == 0)
    def _():
        run(0, N_META, tr - N_META)

    if n_tiles > 2:
        @pl.when((i > 0) & (i < n_tiles - 1))
        def _():
            run(pl.multiple_of(i * tr - N_META, 8), 0, tr)

    @pl.when(i == n_tiles - 1)
    def _():
        run((n_tiles - 1) * tr - N_META, 0, n_tok + N_META - (n_tiles - 1) * tr)


def _embed_norm(x, tgt, meta, g, *, lp, name):
    n_tok, d = x.shape
    tr = _row_tile(lp, d * (4 + 2 + 4))
    n_tiles = lp // tr

    def body(x_ref, tgt_ref, meta_ref, g_ref, h_ref, o_ref, tp_ref, buf_ref, tbuf_ref, sems):
        i = pl.program_id(0)
        buf_ref[...] = jnp.zeros_like(buf_ref)
        tbuf_ref[...] = jnp.zeros_like(tbuf_ref)

        @pl.when(i == 0)
        def _():
            buf_ref[0:N_META, :] = meta_ref[...]

        _token_rows_copy(i, n_tiles, tr, n_tok, tgt_ref, tbuf_ref, sems.at[1], to_tokens=False,
                         wait=False)
        _token_rows_copy(i, n_tiles, tr, n_tok, x_ref, buf_ref, sems.at[0], to_tokens=False)
        xv = buf_ref[...]
        h_ref[...] = xv
        r = lax.rsqrt(jnp.mean(xv * xv, axis=-1, keepdims=True) + EPS)
        o_ref[...] = (xv * r * g_ref[...]).astype(BF16)
        _token_rows_copy(i, n_tiles, tr, n_tok, tgt_ref, tbuf_ref, sems.at[1], to_tokens=False,
                         start=False)
        tp_ref[...] = tbuf_ref[...]

    return pl.pallas_call(
        body, grid=(n_tiles,),
        in_specs=[ANY, ANY, _fspec((N_META, d)), _fspec((1, d))],
        out_specs=[_rspec(tr, d), _rspec(tr, d), _rspec(tr, d)],
        out_shape=[jax.ShapeDtypeStruct((lp, d), F32), jax.ShapeDtypeStruct((lp, d), BF16),
                   jax.ShapeDtypeStruct((lp, d), F32)],
        scratch_shapes=[pltpu.VMEM((tr, d), F32), pltpu.VMEM((tr, d), F32),
                        pltpu.SemaphoreType.DMA((2,))],
        compiler_params=_cp("parallel"), name=name)(x, tgt, meta, g)


def _residual_norm(h, t, g, *, name):
    lp, d = h.shape
    tr = _row_tile(lp, d * (4 + 4 + 4 + 2))

    def body(h_ref, t_ref, g_ref, h1_ref, m_ref):
        xv = h_ref[...] + t_ref[...]
        h1_ref[...] = xv
        r = lax.rsqrt(jnp.mean(xv * xv, axis=-1, keepdims=True) + EPS)
        m_ref[...] = (xv * r * g_ref[...]).astype(BF16)

    return pl.pallas_call(
        body, grid=(lp // tr,),
        in_specs=[_rspec(tr, d), _rspec(tr, d), _fspec((1, d))],
        out_specs=[_rspec(tr, d), _rspec(tr, d)],
        out_shape=[jax.ShapeDtypeStruct((lp, d), F32), jax.ShapeDtypeStruct((lp, d), BF16)],
        compiler_params=_cp("parallel"), name=name)(h, t, g)


def _rmsnorm_bwd_add(x, g, dy, dres, *, name):
    lp, d = x.shape
    tr = _row_tile(lp, d * (4 * 4 + 2))

    def body(x_ref, g_ref, dy_ref, dr_ref, dx_ref, dx16_ref, dg_ref):
        @pl.when(pl.program_id(0) == 0)
        def _():
            dg_ref[...] = jnp.zeros_like(dg_ref)

        xv = x_ref[...]
        r = lax.rsqrt(jnp.mean(xv * xv, axis=-1, keepdims=True) + EPS)
        xh = xv * r
        dyv = dy_ref[...]
        dg_ref[...] += jnp.sum(dyv * xh, axis=0, keepdims=True)
        dxh = dyv * g_ref[...]
        dx = dr_ref[...] + r * (dxh - xh * jnp.mean(dxh * xh, axis=-1, keepdims=True))
        dx_ref[...] = dx
        dx16_ref[...] = dx.astype(BF16)

    return pl.pallas_call(
        body, grid=(lp // tr,),
        in_specs=[_rspec(tr, d), _fspec((1, d)), _rspec(tr, d), _rspec(tr, d)],
        out_specs=[_rspec(tr, d), _rspec(tr, d), _fspec((1, d))],
        out_shape=[jax.ShapeDtypeStruct((lp, d), F32), jax.ShapeDtypeStruct((lp, d), BF16),
                   jax.ShapeDtypeStruct((1, d), F32)],
        compiler_params=_cp("arbitrary"), name=name)(x, g, dy, dres)


def _rmsnorm_bwd_tokens(x, g, dy, dres, *, n_tok, name):
    lp, d = x.shape
    tr = _row_tile(lp, d * 4 * 4)
    n_tiles = lp // tr

    def body(x_ref, g_ref, dy_ref, dr_ref, dtok_ref, dmeta_ref, dg_ref, buf_ref, sem):
        i = pl.program_id(0)

        @pl.when(i == 0)
        def _():
            dg_ref[...] = jnp.zeros_like(dg_ref)

        xv = x_ref[...]
        r = lax.rsqrt(jnp.mean(xv * xv, axis=-1, keepdims=True) + EPS)
        xh = xv * r
        dyv = dy_ref[...]
        dg_ref[...] += jnp.sum(dyv * xh, axis=0, keepdims=True)
        dxh = dyv * g_ref[...]
        buf_ref[...] = dr_ref[...] + r * (dxh - xh * jnp.mean(dxh * xh, axis=-1, keepdims=True))

        @pl.when(i == 0)
        def _():
            dmeta_ref[...] = buf_ref[0:N_META, :]

        _token_rows_copy(i, n_tiles, tr, n_tok, dtok_ref, buf_ref, sem, to_tokens=True)

    return pl.pallas_call(
        body, grid=(n_tiles,),
        in_specs=[_rspec(tr, d), _fspec((1, d)), _rspec(tr, d), _rspec(tr, d)],
        out_specs=[ANY, _fspec((N_META, d)), _fspec((1, d))],
        out_shape=[jax.ShapeDtypeStruct((n_tok, d), F32), jax.ShapeDtypeStruct((N_META, d), F32),
                   jax.ShapeDtypeStruct((1, d), F32)],
        scratch_shapes=[pltpu.VMEM((tr, d), F32), pltpu.SemaphoreType.DMA],
        compiler_params=_cp("arbitrary"), name=name)(x, g, dy, dres)


def _final_loss(h1, t2, g, tgt, *, n_tok, name):
    lp, d = h1.shape
    tr = _row_tile(lp, d * (4 * 4 + 2))
    n_tiles = lp // tr

    def body(h_ref, t_ref, g_ref, tg_ref, dh_ref, dh16_ref, loss_ref, dg_ref):
        i = pl.program_id(0)

        @pl.when(i == 0)
        def _():
            loss_ref[...] = jnp.zeros_like(loss_ref)
            dg_ref[...] = jnp.zeros_like(dg_ref)

        xv = h_ref[...] + t_ref[...]
        r = lax.rsqrt(jnp.mean(xv * xv, axis=-1, keepdims=True) + EPS)
        xh = xv * r
        gv = g_ref[...]
        row = i * tr + lax.broadcasted_iota(jnp.int32, (tr, 1), 0)
        valid = (row >= N_META) & (row < N_META + n_tok)
        err = jnp.where(valid, xh * gv - tg_ref[...], 0.0)
        loss_ref[...] += jnp.sum(0.5 * err * err) / d
        dy = err / d
        dg_ref[...] += jnp.sum(dy * xh, axis=0, keepdims=True)
        dxh = dy * gv
        dh = r * (dxh - xh * jnp.mean(dxh * xh, axis=-1, keepdims=True))
        dh_ref[...] = dh
        dh16_ref[...] = dh.astype(BF16)

    return pl.pallas_call(
        body, grid=(n_tiles,),
        in_specs=[_rspec(tr, d), _rspec(tr, d), _fspec((1, d)), _rspec(tr, d)],
        out_specs=[_rspec(tr, d), _rspec(tr, d), _fspec((1, LANES)), _fspec((1, d))],
        out_shape=[jax.ShapeDtypeStruct((lp, d), F32), jax.ShapeDtypeStruct((lp, d), BF16),
                   jax.ShapeDtypeStruct((1, LANES), F32), jax.ShapeDtypeStruct((1, d), F32)],
        compiler_params=_cp("arbitrary"), name=name)(h1, t2, g, tgt)


def _hg_out(o_f, o_b, proj, gain, *, col_g, name):
    lp, w = o_f.shape
    tr = _row_tile(lp, w * (3 * 4 + 2))
    hh = w // HG_DK

    def body(of_ref, ob_ref, g_ref, gain_ref, y_ref):
        gv = g_ref[...]
        sg = gv * _sigmoid(gv)
        for h in range(hh):
            sl = slice(h * HG_DK, (h + 1) * HG_DK)
            o = of_ref[:, sl] + ob_ref[:, sl]
            r = lax.rsqrt(jnp.mean(o * o, axis=-1, keepdims=True) + EPS)
            y_ref[:, sl] = (o * r * gain_ref[:, sl] * sg[:, sl]).astype(BF16)

    return pl.pallas_call(
        body, grid=(lp // tr,),
        in_specs=[_rspec(tr, w), _rspec(tr, w), _rspec(tr, w, col_g // w), _fspec((1, w))],
        out_specs=_rspec(tr, w),
        out_shape=jax.ShapeDtypeStruct((lp, w), BF16),
        compiler_params=_cp("parallel"), name=name)(o_f, o_b, proj, gain)


def _hg_out_bwd(o_f, o_b, proj, gain, dy, *, col_g, name):
    lp, w = o_f.shape
    tr = _row_tile(lp, w * (5 * 4 + 2))
    hh = w // HG_DK

    def body(of_ref, ob_ref, g_ref, gain_ref, dy_ref, do_ref, dg_ref, dgain_ref):
        @pl.when(pl.program_id(0) == 0)
        def _():
            dgain_ref[...] = jnp.zeros_like(dgain_ref)

        for h in range(hh):
            sl = slice(h * HG_DK, (h + 1) * HG_DK)
            gv = g_ref[:, sl]
            s = _sigmoid(gv)
            sg = gv * s
            dsg = s + gv * s * (1.0 - s)
            o = of_ref[:, sl] + ob_ref[:, sl]
            r = lax.rsqrt(jnp.mean(o * o, axis=-1, keepdims=True) + EPS)
            on = o * r
            dyv = dy_ref[:, sl]
            gn = gain_ref[:, sl]
            dgain_ref[:, sl] += jnp.sum(dyv * on * sg, axis=0, keepdims=True)
            dg_ref[:, sl] = (dyv * on * gn * dsg).astype(BF16)
            don = dyv * gn * sg
            do_ref[:, sl] = r * (don - on * jnp.mean(don * on, axis=-1, keepdims=True))

    return pl.pallas_call(
        body, grid=(lp // tr,),
        in_specs=[_rspec(tr, w), _rspec(tr, w), _rspec(tr, w, col_g // w), _fspec((1, w)),
                  _rspec(tr, w)],
        out_specs=[_rspec(tr, w), _rspec(tr, w), _fspec((1, w))],
        out_shape=[jax.ShapeDtypeStruct((lp, w), F32), jax.ShapeDtypeStruct((lp, w), BF16),
                   jax.ShapeDtypeStruct((1, w), F32)],
        compiler_params=_cp("arbitrary"), name=name)(o_f, o_b, proj, gain, dy)


HG_ROWS = 128
HG_HALVES = (1, 2, 4, 8, 16, 32, 64)


def _hg_gates(zq, z, lbv):
    sq = _sigmoid(zq)
    s = _sigmoid(z)
    f = lbv + (1.0 - lbv) * s
    kk = (1.0 - lbv) * (1.0 - s)
    return zq * sq, sq, s, f, jnp.log(f), kk


def _block_cumsum(g, pos, suffix):
    x = g
    for k in HG_HALVES:
        if suffix:
            x = x + jnp.where(pos < HG_ROWS - k, pltpu.roll(x, HG_ROWS - k, 0), 0.0)
        else:
            x = x + jnp.where(pos >= k, pltpu.roll(x, k, 0), 0.0)
    return x


def _pair_levels(b, pos, reverse):
    out = []
    first = b
    for m in HG_HALVES:
        if m > 1:
            first = jnp.where((pos & (m - 1)) >= m // 2, pltpu.roll(first, m // 2, 0), first)
        nxt = pltpu.roll(first, HG_ROWS - m, 0)
        upper = (pos & (2 * m - 1)) >= m
        if reverse:
            eq = jnp.where(upper, 0.0, jnp.exp(b - nxt))
            ek = jnp.where(upper, jnp.exp(first - b), 0.0)
        else:
            eq = jnp.where(upper, jnp.exp(b - first), 0.0)
            ek = jnp.where(upper, 0.0, jnp.exp(nxt - b))
        out.append((eq, ek))
    return out


def _pair_masks(mask_ref):
    ri = lax.broadcasted_iota(jnp.int32, (HG_ROWS, HG_ROWS), 0)
    ci = lax.broadcasted_iota(jnp.int32, (HG_ROWS, HG_ROWS), 1)
    for i, m in enumerate(HG_HALVES):
        sh = m.bit_length()
        mask_ref[i] = jnp.where((ri >> sh) == (ci >> sh), 1.0, 0.0)


def _hg_scan_fwd(proj, lb, *, reverse, col_q, col_z, col_i, hh, name):
    lp = proj.shape[0]
    n_blocks = lp // HG_ROWS
    last = 0 if reverse else HG_ROWS - 1

    def body(q_ref, z_ref, i_ref, lb_ref, o_ref, st_ref, mask_ref):
        lbv = lb_ref[...]
        pos = lax.broadcasted_iota(jnp.int32, (HG_ROWS, 1), 0)
        ri = lax.broadcasted_iota(jnp.int32, (HG_ROWS, HG_ROWS), 0)
        ci = lax.broadcasted_iota(jnp.int32, (HG_ROWS, HG_ROWS), 1)
        _pair_masks(mask_ref)

        def block(bi, st):
            bb = (n_blocks - 1 - bi) if reverse else bi
            r0 = pl.multiple_of(bb * HG_ROWS, HG_ROWS)
            v16 = i_ref[pl.ds(r0, HG_ROWS), :].astype(BF16)
            qh, _, _, _, g, kk = _hg_gates(q_ref[pl.ds(r0, HG_ROWS), :],
                                           z_ref[pl.ds(r0, HG_ROWS), :], lbv)
            b = _block_cumsum(g, pos, reverse)
            bl = b[last:last + 1, :]
            qe = (qh * jnp.exp(b)).astype(BF16)
            kd = (kk * jnp.exp(bl - b)).astype(BF16)
            a = jnp.where(ri == ci, jnp.sum(qh * kk, axis=1, keepdims=True), 0.0)
            for i, (eq, ek) in enumerate(_pair_levels(b, pos, reverse)):
                a = a + mask_ref[i] * _nt((qh * eq).astype(BF16), (kk * ek).astype(BF16))
            st_ref[bb] = st
            o_ref[pl.ds(r0, HG_ROWS), :] = _nn(a.astype(BF16), v16) + _nt(qe, st.astype(BF16))
            return jnp.exp(bl) * st + _tn(v16, kd)

        lax.fori_loop(0, n_blocks, block, jnp.zeros((HG_DK, HG_DK), F32))

    cspec = lambda col: pl.BlockSpec((lp, HG_DK), lambda h: (0, col // HG_DK + h))
    return pl.pallas_call(
        body, grid=(hh,),
        in_specs=[cspec(col_q), cspec(col_z), cspec(col_i),
                  pl.BlockSpec((None, 1, HG_DK), lambda h: (h, 0, 0))],
        out_specs=[pl.BlockSpec((lp, HG_DK), lambda h: (0, h)),
                   pl.BlockSpec((None, n_blocks, HG_DK, HG_DK), lambda h: (h, 0, 0, 0))],
        out_shape=[jax.ShapeDtypeStruct((lp, hh * HG_DK), F32),
                   jax.ShapeDtypeStruct((hh, n_blocks, HG_DK, HG_DK), F32)],
        scratch_shapes=[pltpu.VMEM((len(HG_HALVES), HG_ROWS, HG_ROWS), F32)],
        compiler_params=_cp("parallel"), name=name)(proj, proj, proj, lb)


def _hg_scan_bwd(proj, lb, states, do, *, reverse, col_q, col_z, col_i, hh, name):
    lp = proj.shape[0]
    n_blocks = lp // HG_ROWS
    last = 0 if reverse else HG_ROWS - 1

    def body(q_ref, z_ref, i_ref, lb_ref, st_ref, do_ref, dq_ref, dz_ref, dv_ref, dlb_ref, mask_ref):
        lbv = lb_ref[...]
        pos = lax.broadcasted_iota(jnp.int32, (HG_ROWS, 1), 0)
        ri = lax.broadcasted_iota(jnp.int32, (HG_ROWS, HG_ROWS), 0)
        ci = lax.broadcasted_iota(jnp.int32, (HG_ROWS, HG_ROWS), 1)
        _pair_masks(mask_ref)

        def block(bi, carry):
            dst, dlb = carry
            bb = bi if reverse else (n_blocks - 1 - bi)
            r0 = pl.multiple_of(bb * HG_ROWS, HG_ROWS)
            zq = q_ref[pl.ds(r0, HG_ROWS), :]
            v16 = i_ref[pl.ds(r0, HG_ROWS), :].astype(BF16)
            do16 = do_ref[pl.ds(r0, HG_ROWS), :].astype(BF16)
            qh, sq, s, f, g, kk = _hg_gates(zq, z_ref[pl.ds(r0, HG_ROWS), :], lbv)
            b = _block_cumsum(g, pos, reverse)
            bl = b[last:last + 1, :]
            eb = jnp.exp(b)
            ebl = jnp.exp(bl - b)
            decay = jnp.exp(bl)
            qe16 = (qh * eb).astype(BF16)
            kd16 = (kk * ebl).astype(BF16)
            st = st_ref[bb]
            st16, dst16 = st.astype(BF16), dst.astype(BF16)
            same_row = ri == ci
            da = _nt(do16, v16)
            da_diag = jnp.sum(jnp.where(same_row, da, 0.0), axis=1, keepdims=True)
            dq_state = eb * _nn(do16, st16)
            dk_state = ebl * _nn(v16, dst16)
            dq = dq_state + da_diag * kk
            dk = dk_state + da_diag * qh
            dbl = (decay * jnp.sum(st * dst, axis=0, keepdims=True)
                   + jnp.sum(kk * dk_state, axis=0, keepdims=True))
            db = qh * dq_state - kk * dk_state + jnp.where(pos == last, dbl, 0.0)
            a = jnp.where(same_row, jnp.sum(qh * kk, axis=1, keepdims=True), 0.0)
            for i, (eq, ek) in enumerate(_pair_levels(b, pos, reverse)):
                same = mask_ref[i]
                q16, k16 = (qh * eq).astype(BF16), (kk * ek).astype(BF16)
                a = a + same * _nt(q16, k16)
                da16 = (same * da).astype(BF16)
                gq, gk = _nn(da16, k16), _tn(da16, q16)
                dq = dq + eq * gq
                dk = dk + ek * gk
                db = db + (q16.astype(F32) * gq - k16.astype(F32) * gk)
            dg = _block_cumsum(db, pos, not reverse)
            df = dg / f - dk
            dq_ref[pl.ds(r0, HG_ROWS), :] = dq * (sq + zq * sq * (1.0 - sq))
            dz_ref[pl.ds(r0, HG_ROWS), :] = df * (1.0 - lbv) * s * (1.0 - s)
            dv_ref[pl.ds(r0, HG_ROWS), :] = _nt(kd16, dst16) + _tn(a.astype(BF16), do16)
            return (decay * dst + _tn(do16, qe16),
                    dlb + jnp.sum(df * (1.0 - s), axis=0, keepdims=True))

        _, dlb = lax.fori_loop(0, n_blocks, block,
                               (jnp.zeros((HG_DK, HG_DK), F32), jnp.zeros((1, HG_DK), F32)))
        dlb_ref[...] = dlb

    cspec = lambda col: pl.BlockSpec((lp, HG_DK), lambda h: (0, col // HG_DK + h))
    ospec = pl.BlockSpec((lp, HG_DK), lambda h: (0, h))
    sds = jax.ShapeDtypeStruct((lp, hh * HG_DK), F32)
    return pl.pallas_call(
        body, grid=(hh,),
        in_specs=[cspec(col_q), cspec(col_z), cspec(col_i),
                  pl.BlockSpec((None, 1, HG_DK), lambda h: (h, 0, 0)),
                  pl.BlockSpec((None, n_blocks, HG_DK, HG_DK), lambda h: (h, 0, 0, 0)),
                  ospec],
        out_specs=[ospec, ospec, ospec, pl.BlockSpec((None, 1, HG_DK), lambda h: (h, 0, 0))],
        out_shape=[sds, sds, sds, jax.ShapeDtypeStruct((hh, 1, HG_DK), F32)],
        scratch_shapes=[pltpu.VMEM((len(HG_HALVES), HG_ROWS, HG_ROWS), F32)],
        compiler_params=_cp("parallel"), name=name)(proj, proj, proj, lb, states, do)


NA_HB = LANES // NA_HEAD_DIM
NA_G = 4
NA_U = NA_G + NA_WIN_H
NA_QN = NA_G * GRID_W
NA_KN = NA_U * GRID_W


def _na_table_index(pattern, a, j):
    if pattern == 0:
        return j - a + NA_WIN_H - 1 if j < NA_WIN_H else None
    if pattern == 2:
        return j - a - 1 if j >= NA_U - NA_WIN_H else None
    return j - a + NA_WIN_H // 2 - 1 if a <= j < a + NA_WIN_H else None


def _na_step_rows(pattern, t, rows):
    if pattern == 0:
        r0, us = 0, 0
    elif pattern == 2:
        r0, us = rows - NA_G, rows - NA_U
    else:
        r0 = NA_G * t
        us = r0 - NA_WIN_H // 2
    q0, k0 = N_META + GRID_W * r0, N_META + GRID_W * us
    if pattern == 1:
        q0, k0 = pl.multiple_of(q0, 16), pl.multiple_of(k0, 16)
    return q0, k0


def _na_fill_bias(tb_ref, bias_ref):
    neg = jnp.full((GRID_W, GRID_W), -1e30, F32)
    for h in range(NA_HB):
        for pattern in range(3):
            for a in range(NA_G):
                for j in range(NA_U):
                    idx = _na_table_index(pattern, a, j)
                    bias_ref[h, pattern, a * GRID_W:(a + 1) * GRID_W, j * GRID_W:(j + 1) * GRID_W] = (
                        neg if idx is None else tb_ref[h, idx])


def _na_steps(rows, step, carry):
    n_steps = rows // NA_G
    carry = step(0, 0, carry)
    carry = lax.fori_loop(1, n_steps - 1, functools.partial(step, 1), carry)
    return step(2, n_steps - 1, carry)


def _na_head_lanes():
    lane = lax.broadcasted_iota(jnp.int32, (1, LANES), 1)
    return [lane // NA_HEAD_DIM == h for h in range(NA_HB)]


def _na_only(mask, x):
    return jnp.where(mask, x, jnp.zeros_like(x))


def _na_stack(heads, x):
    return jnp.concatenate([_na_only(mask, x) for mask in heads], axis=0)


def _na_unstack(heads, y):
    rows = y.shape[0] // NA_HB
    out = y[0:rows]
    for h in range(1, NA_HB):
        out = jnp.where(heads[h], y[h * rows:(h + 1) * rows], out)
    return out


def _na_fwd(proj, tb, *, n_tok, nh, name):
    lp = proj.shape[0]
    dh, hb = NA_HEAD_DIM, NA_HB
    naw = nh * dh
    rows = n_tok // GRID_W
    scale = dh ** -0.5

    def body(q_ref, k_ref, v_ref, tb_ref, o_ref, lse_ref, q16_ref, k16_ref, v16_ref, bias_ref):
        o_ref[...] = jnp.zeros_like(o_ref)
        lse_ref[...] = jnp.zeros_like(lse_ref)
        q16_ref[...] = q_ref[...].astype(BF16)
        k16_ref[...] = k_ref[...].astype(BF16)
        v16_ref[...] = v_ref[...].astype(BF16)
        _na_fill_bias(tb_ref, bias_ref)
        heads = _na_head_lanes()
        km = k16_ref[0:N_META, :]
        vm = v16_ref[0:N_META, :]
        qm = q16_ref[0:N_META, :]
        o_m = None
        for h in range(hb):
            s = _nt(_na_only(heads[h], qm), km) * scale
            m = jnp.max(s, axis=1, keepdims=True)
            p = jnp.exp(s - m)
            l = jnp.sum(p, axis=1, keepdims=True)
            o_h = _nn(p.astype(BF16), vm) / l
            o_m = o_h if o_m is None else jnp.where(heads[h], o_h, o_m)
            lse_ref[h, 0:N_META, :] = m + jnp.log(l)
        o_ref[0:N_META, :] = o_m

        def step(pattern, t, carry):
            q0, k0 = _na_step_rows(pattern, t, rows)
            k16 = k16_ref[pl.ds(k0, NA_KN), :]
            v16 = v16_ref[pl.ds(k0, NA_KN), :]
            q2 = _na_stack(heads, q16_ref[pl.ds(q0, NA_QN), :])
            s = _nt(q2, k16) * scale + bias_ref[:, pattern].reshape(hb * NA_QN, NA_KN)
            sm = _nt(q2, km) * scale
            m = jnp.maximum(jnp.max(s, axis=1, keepdims=True), jnp.max(sm, axis=1, keepdims=True))
            p = jnp.exp(s - m)
            pm = jnp.exp(sm - m)
            l = jnp.sum(p, axis=1, keepdims=True) + jnp.sum(pm, axis=1, keepdims=True)
            o2 = (_nn(p.astype(BF16), v16) + _nn(pm.astype(BF16), vm)) / l
            o_ref[pl.ds(q0, NA_QN), :] = _na_unstack(heads, o2)
            lse2 = m + jnp.log(l)
            for h in range(hb):
                lse_ref[h, pl.ds(q0, NA_QN), :] = lse2[h * NA_QN:(h + 1) * NA_QN]
            return carry

        _na_steps(rows, step, 0)

    cblk = lambda col: pl.BlockSpec((lp, LANES), lambda g: (0, col // LANES + g))
    return pl.pallas_call(
        body, grid=(nh // hb,),
        in_specs=[cblk(0), cblk(naw), cblk(2 * naw),
                  pl.BlockSpec((hb, 2 * NA_WIN_H - 1, GRID_W, GRID_W), lambda g: (g, 0, 0, 0))],
        out_specs=[cblk(0), pl.BlockSpec((hb, lp, 1), lambda g: (g, 0, 0))],
        out_shape=[jax.ShapeDtypeStruct((lp, naw), F32), jax.ShapeDtypeStruct((nh, lp, 1), F32)],
        scratch_shapes=[pltpu.VMEM((lp, LANES), BF16)] * 3 + [pltpu.VMEM((hb, 3, NA_QN, NA_KN), F32)],
        compiler_params=_cp("parallel"), name=name)(proj, proj, proj, tb)


def _na_bwd(proj, tb, o, lse, do, *, n_tok, nh, name):
    lp = proj.shape[0]
    dh, hb = NA_HEAD_DIM, NA_HB
    naw = nh * dh
    rows = n_tok // GRID_W
    scale = dh ** -0.5

    def body(q_ref, k_ref, v_ref, tb_ref, o_ref, lse_ref, do_ref, dq_ref, dk_ref, dv_ref, dtb_ref,
             q16_ref, k16_ref, v16_ref, bias_ref):
        dq_ref[...] = jnp.zeros_like(dq_ref)
        dk_ref[...] = jnp.zeros_like(dk_ref)
        dv_ref[...] = jnp.zeros_like(dv_ref)
        dtb_ref[...] = jnp.zeros_like(dtb_ref)
        q16_ref[...] = q_ref[...].astype(BF16)
        k16_ref[...] = k_ref[...].astype(BF16)
        v16_ref[...] = v_ref[...].astype(BF16)
        _na_fill_bias(tb_ref, bias_ref)
        heads = _na_head_lanes()
        km = k16_ref[0:N_META, :]
        vm = v16_ref[0:N_META, :]
        qm = q16_ref[0:N_META, :]
        dom = do_ref[0:N_META, :]
        prod = dom * o_ref[0:N_META, :]
        dq_m = None
        dkm0 = jnp.zeros((N_META, LANES), F32)
        dvm0 = jnp.zeros((N_META, LANES), F32)
        for h in range(hb):
            q_h = _na_only(heads[h], qm)
            do_h = _na_only(heads[h], dom).astype(BF16)
            p = jnp.exp(_nt(q_h, km) * scale - lse_ref[h, 0:N_META, :])
            delta = jnp.sum(_na_only(heads[h], prod), axis=1, keepdims=True)
            ds = (p * (_nt(do_h, vm) - delta)).astype(BF16)
            dq_h = _nn(ds, km) * scale
            dq_m = dq_h if dq_m is None else jnp.where(heads[h], dq_h, dq_m)
            dkm0 = dkm0 + _tn(ds, q_h) * scale
            dvm0 = dvm0 + _tn(p.astype(BF16), do_h)
        dq_ref[0:N_META, :] = dq_m

        def step(pattern, t, carry):
            dkm, dvm = carry
            q0, k0 = _na_step_rows(pattern, t, rows)
            k16 = k16_ref[pl.ds(k0, NA_KN), :]
            v16 = v16_ref[pl.ds(k0, NA_KN), :]
            q2 = _na_stack(heads, q16_ref[pl.ds(q0, NA_QN), :])
            do2 = _na_stack(heads, do_ref[pl.ds(q0, NA_QN), :])
            do16 = do2.astype(BF16)
            ov = o_ref[pl.ds(q0, NA_QN), :]
            delta = jnp.sum(do2 * jnp.concatenate([ov] * hb, axis=0), axis=1, keepdims=True)
            lse = jnp.concatenate([lse_ref[h, pl.ds(q0, NA_QN), :] for h in range(hb)], axis=0)
            p = jnp.exp(_nt(q2, k16) * scale + bias_ref[:, pattern].reshape(hb * NA_QN, NA_KN)
                        - lse)
            pm = jnp.exp(_nt(q2, km) * scale - lse)
            ds = p * (_nt(do16, v16) - delta)
            dsm = (pm * (_nt(do16, vm) - delta)).astype(BF16)
            ds16 = ds.astype(BF16)
            dq2 = (_nn(ds16, k16) + _nn(dsm, km)) * scale
            dq_ref[pl.ds(q0, NA_QN), :] = _na_unstack(heads, dq2)
            dk_ref[pl.ds(k0, NA_KN), :] += _tn(ds16, q2) * scale
            dv_ref[pl.ds(k0, NA_KN), :] += _tn(p.astype(BF16), do16)
            for h in range(hb):
                for a in range(NA_G):
                    for j in range(NA_U):
                        idx = _na_table_index(pattern, a, j)
                        if idx is not None:
                            r = h * NA_QN + a * GRID_W
                            dtb_ref[h, idx] += ds[r:r + GRID_W, j * GRID_W:(j + 1) * GRID_W]
            return dkm + _tn(dsm, q2) * scale, dvm + _tn(pm.astype(BF16), do16)

        dkm, dvm = _na_steps(rows, step, (dkm0, dvm0))
        dk_ref[0:N_META, :] += dkm
        dv_ref[0:N_META, :] += dvm

    cblk = lambda col: pl.BlockSpec((lp, LANES), lambda g: (0, col // LANES + g))
    tbs = pl.BlockSpec((hb, 2 * NA_WIN_H - 1, GRID_W, GRID_W), lambda g: (g, 0, 0, 0))
    sds = jax.ShapeDtypeStruct((lp, naw), F32)
    return pl.pallas_call(
        body, grid=(nh // hb,),
        in_specs=[cblk(0), cblk(naw), cblk(2 * naw), tbs, cblk(0),
                  pl.BlockSpec((hb, lp, 1), lambda g: (g, 0, 0)), cblk(0)],
        out_specs=[cblk(0), cblk(0), cblk(0), tbs],
        out_shape=[sds, sds, sds, jax.ShapeDtypeStruct(tb.shape, F32)],
        scratch_shapes=[pltpu.VMEM((lp, LANES), BF16)] * 3 + [pltpu.VMEM((hb, 3, NA_QN, NA_KN), F32)],
        compiler_params=_cp("parallel"), name=name)(proj, proj, proj, tb, o, lse, do)


def _rpb_onehot():
    c = np.arange(GRID_W)[:, None]
    w = np.arange(GRID_W)[None, :]
    cs = np.clip(c - NA_WIN_W // 2, 0, GRID_W - NA_WIN_W)
    in_win = (w >= cs) & (w < cs + NA_WIN_W)
    dc = np.clip(w - c, -(NA_WIN_W - 1), NA_WIN_W - 1) + NA_WIN_W - 1
    oh = np.zeros((LANES, GRID_W * GRID_W), np.float32)
    flat = np.arange(GRID_W * GRID_W).reshape(GRID_W, GRID_W)
    oh[dc[in_win], flat[in_win]] = 1.0
    neg = np.where(in_win, 0.0, -1e30).astype(np.float32).reshape(1, -1)
    return oh, neg


def _assemble_dproj(dq_na, dk_na, dv_na, dq_f, dq_b, dz_f, dz_b, dv_f, dv_b, dg, dgn, dgh, *, name):
    lp, naw = dq_na.shape
    hgw = dq_f.shape[1]
    d = dgn.shape[1]
    cols = 3 * naw + 5 * hgw + 2 * d
    tr = _row_tile(lp, 3 * naw * 4 + 6 * hgw * 4 + hgw * 2 + 2 * d * 2 + cols * 2)

    def body(nq_ref, nk_ref, nv_ref, qf_ref, qb_ref, zf_ref, zb_ref, vf_ref, vb_ref, g_ref, gn_ref,
             gh_ref, o_ref):
        o_ref[:, 0:naw] = nq_ref[...].astype(BF16)
        o_ref[:, naw:2 * naw] = nk_ref[...].astype(BF16)
        o_ref[:, 2 * naw:3 * naw] = nv_ref[...].astype(BF16)
        c0 = 3 * naw
        o_ref[:, c0:c0 + hgw] = (qf_ref[...] + qb_ref[...]).astype(BF16)
        o_ref[:, c0 + hgw:c0 + 2 * hgw] = zf_ref[...].astype(BF16)
        o_ref[:, c0 + 2 * hgw:c0 + 3 * hgw] = zb_ref[...].astype(BF16)
        o_ref[:, c0 + 3 * hgw:c0 + 4 * hgw] = (vf_ref[...] + vb_ref[...]).astype(BF16)
        o_ref[:, c0 + 4 * hgw:c0 + 5 * hgw] = g_ref[...]
        o_ref[:, c0 + 5 * hgw:c0 + 5 * hgw + d] = gn_ref[...]
        o_ref[:, c0 + 5 * hgw + d:] = gh_ref[...]

    hg, na = _rspec(tr, hgw), _rspec(tr, naw)
    return pl.pallas_call(
        body, grid=(lp // tr,),
        in_specs=[na, na, na, hg, hg, hg, hg, hg, hg, hg, _rspec(tr, d), _rspec(tr, d)],
        out_specs=_rspec(tr, cols),
        out_shape=jax.ShapeDtypeStruct((lp, cols), BF16),
        compiler_params=_cp("parallel"), name=name)(dq_na, dk_na, dv_na, dq_f, dq_b, dz_f, dz_b,
                                                    dv_f, dv_b, dg, dgn, dgh)


GROUP_STEPS = 8


def _group_tiles(rows, align):
    steps = GROUP_STEPS if all(r % (GROUP_STEPS * align) == 0 for r in rows) else 1
    return steps, [r // steps for r in rows]


def _adamw(ws, gs, ms, vs, *, name):
    n = len(ws)
    steps, trs = _group_tiles([w.shape[0] for w in ws], 8)

    def body(*refs):
        for i in range(n):
            w_ref, g_ref, m_ref, v_ref = refs[4 * i:4 * i + 4]
            d_ref, mo_ref, vo_ref, go_ref = refs[4 * n + 4 * i:4 * n + 4 * i + 4]
            gv = g_ref[...]
            go_ref[...] = gv
            mn = ADAM_B1 * m_ref[...] + (1.0 - ADAM_B1) * gv
            vn = ADAM_B2 * v_ref[...] + (1.0 - ADAM_B2) * (gv * gv)
            m_hat = mn / (1.0 - ADAM_B1 ** ADAM_STEP)
            v_hat = vn / (1.0 - ADAM_B2 ** ADAM_STEP)
            d_ref[...] = -ADAM_LR * (m_hat / (jnp.sqrt(v_hat) + ADAM_EPS) + ADAM_WD * w_ref[...])
            mo_ref[...] = mn
            vo_ref[...] = vn

    specs = [_rspec(tr, w.shape[1]) for tr, w in zip(trs, ws)]
    out = pl.pallas_call(
        body, grid=(steps,),
        in_specs=[s for s in specs for _ in range(4)],
        out_specs=[s for s in specs for _ in range(4)],
        out_shape=[jax.ShapeDtypeStruct(w.shape, F32) for w in ws for _ in range(4)],
        compiler_params=_cp("parallel"), name=name)(*[a for q in zip(ws, gs, ms, vs) for a in q])
    return [tuple(out[4 * i:4 * i + 4]) for i in range(n)]


def _local_step(x, tgt, meta, first_weight, rest_weights, g_mix, g_mlp, g_fin, hg_gain, rpb, lb,
                early_grads=None, mid_grads=None, late_grad=None, rest_landed=None,
                last_grads=None):
    n_tok, d = x.shape
    hgw = hg_gain.shape[1]
    nh, hh = rpb.shape[0], hgw // HG_DK
    naw = nh * NA_HEAD_DIM
    l_real = N_META + n_tok
    lp = -(-l_real // ROW_ALIGN) * ROW_ALIGN
    col_qhg = 3 * naw
    col_zf, col_zb, col_i, col_g = (col_qhg + hgw, col_qhg + 2 * hgw, col_qhg + 3 * hgw,
                                    col_qhg + 4 * hgw)
    col_gate = col_qhg + 5 * hgw

    oh_np, neg_np = _rpb_onehot()
    oh = jnp.asarray(oh_np)
    rpb_p = jnp.pad(rpb.reshape(nh * (2 * NA_WIN_H - 1), 2 * NA_WIN_W - 1),
                    ((0, 0), (0, LANES - (2 * NA_WIN_W - 1))))
    tb = _matmul(rpb_p, oh, tm=rpb_p.shape[0], tn=512, tk=LANES, precision=HIGHEST,
                 name="rpb_expand")
    tb = (tb + jnp.asarray(neg_np)).reshape(nh, 2 * NA_WIN_H - 1, GRID_W, GRID_W)

    h0, a, tgt_p = _embed_norm(x, tgt, meta, g_mix, lp=lp, name="norm_mix")
    w_in = first_weight((a, tb))
    proj = _matmul(a, w_in, name="mm_in")
    o_na, lse = _na_fwd(proj, tb, n_tok=n_tok, nh=nh, name="na_fwd")
    lb_f = lb[0].reshape(hh, 1, HG_DK)
    lb_b = lb[1].reshape(hh, 1, HG_DK)
    scan_kw = dict(col_q=col_qhg, col_i=col_i, hh=hh)
    o_f, st_f = _hg_scan_fwd(proj, lb_f, reverse=False, col_z=col_zf, name="hg_scan_f", **scan_kw)
    token = rest_landed(o_f) if rest_landed else None
    lb_b_late = lb_b if token is None else lb_b + token[0:1, 0:1]
    o_b, st_b = _hg_scan_fwd(proj, lb_b_late, reverse=True, col_z=col_zb, name="hg_scan_b",
                             **scan_kw)
    o_hg = _hg_out(o_f, o_b, proj, hg_gain, col_g=col_g, name="hg_out")
    w_na, w_hg, w_o, w_up, w_down = rest_weights(o_hg)
    y_na = _matmul(o_na, w_na, name="mm_na_out", out_dtype=BF16)
    gates = ((proj, col_gate), (proj, col_gate + d))

    def mix_gates(acc, gn, gh, yn):
        return acc, _sigmoid(gn) * yn + _sigmoid(gh) * acc

    def mix_gates_bwd(dmix, gn, gh, yn, yh):
        sn, sh = _sigmoid(gn), _sigmoid(gh)
        return dmix * sn, dmix * sh, dmix * yn * sn * (1.0 - sn), dmix * yh * sh * (1.0 - sh)

    y_hg, mix = _matmul(o_hg, w_hg, name="mm_hg_out", epilogue=mix_gates,
                        tiles=(*gates, (y_na, 0)), out_dtypes=(BF16, BF16))
    t1 = _matmul(mix, w_o, name="mm_o")
    h1, mlp_in = _residual_norm(h0, t1, g_mlp, name="resid_norm_mlp")
    u, act = _matmul(mlp_in, w_up, name="mm_up", out_dtypes=(BF16, BF16),
                     epilogue=lambda acc: (acc, jnp.square(jnp.maximum(acc, 0.0))))
    t2 = _matmul(act, w_down, name="mm_down")
    dh2, dh2_16, loss, dg_fin = _final_loss(h1, t2, g_fin, tgt_p, n_tok=n_tok, name="final_loss")

    (du,) = _matmul(dh2_16, w_down, tb=True, name="mm_down_dx", tiles=((u, 0),),
                    out_dtypes=(BF16,),
                    epilogue=lambda acc, uv: (acc * 2.0 * jnp.maximum(uv, 0.0),))
    dw_down = _matmul(act, dh2_16, ta=True, name="mm_down_dw")
    dm = _matmul(du, w_up, tb=True, name="mm_up_dx")
    dw_up = _matmul(mlp_in, du, ta=True, name="mm_up_dw")
    dh1, dh1_16, dg_mlp = _rmsnorm_bwd_add(h1, g_mlp, dm, dh2, name="norm_mlp_bwd")
    dy_na, dy_hg, dgn, dgh = _matmul(dh1_16, w_o, tb=True, name="mm_o_dx", epilogue=mix_gates_bwd,
                                     tiles=(*gates, (y_na, 0), (y_hg, 0)), out_dtypes=(BF16,) * 4)
    dw_o = _matmul(mix, dh1_16, ta=True, name="mm_o_dw")
    do_na = _matmul(dy_na, w_na, tb=True, name="mm_na_out_dx")
    dw_na = _matmul(o_na, dy_na, ta=True, name="mm_na_out_dw")
    do_hg = _matmul(dy_hg, w_hg, tb=True, name="mm_hg_out_dx")
    dw_hg = _matmul(o_hg, dy_hg, ta=True, name="mm_hg_out_dw")
    token = early_grads([dw_na, dw_hg, dw_o, dw_up, dw_down]) if early_grads else None
    if token is not None:
        hg_gain = hg_gain + token[0:1, 0:1]
    d_o, dg_hg, d_gain = _hg_out_bwd(o_f, o_b, proj, hg_gain, do_hg, col_g=col_g, name="hg_out_bwd")
    dq_f, dz_f, dv_f, dlb_f = _hg_scan_bwd(proj, lb_f, st_f, d_o, reverse=False, col_z=col_zf,
                                           name="hg_scan_f_bwd", **scan_kw)
    token = mid_grads(dq_f) if mid_grads else None
    lb_b_late = lb_b if token is None else lb_b + token[0:1, 0:1]
    dq_b, dz_b, dv_b, dlb_b = _hg_scan_bwd(proj, lb_b_late, st_b, d_o, reverse=True, col_z=col_zb,
                                           name="hg_scan_b_bwd", **scan_kw)
    dq_na, dk_na, dv_na, dtb = _na_bwd(proj, tb, o_na, lse, do_na, n_tok=n_tok, nh=nh, name="na_bwd")
    dproj = _assemble_dproj(dq_na, dk_na, dv_na, dq_f, dq_b, dz_f, dz_b, dv_f, dv_b, dg_hg, dgn,
                            dgh, name="assemble_dproj")
    dw_in = _matmul(a, dproj, ta=True, name="mm_in_dw")
    token = late_grad(dw_in) if late_grad else None
    da = _matmul(dproj, w_in, tb=True, name="mm_in_dx", after=token)
    token = last_grads(da) if last_grads else None
    g_mix_late = g_mix if token is None else g_mix + token[0:1, 0:1]
    dx, dmeta, dg_mix = _rmsnorm_bwd_tokens(h0, g_mix_late, da, dh1, n_tok=n_tok,
                                            name="norm_mix_bwd")
    d_rpb = _matmul(dtb.reshape(nh * (2 * NA_WIN_H - 1), GRID_W * GRID_W), oh, tb=True,
                    tm=nh * (2 * NA_WIN_H - 1), tn=LANES, tk=1024, precision=HIGHEST,
                    name="rpb_reduce")
    d_lb = jnp.concatenate([dlb_f.reshape(1, hgw), dlb_b.reshape(1, hgw)], axis=0)
    return (loss, dx, dmeta, dw_in, dw_na, dw_hg, dw_o, dw_up, dw_down,
            dg_mix, dg_mlp, dg_fin, d_gain, d_rpb, d_lb)


N_CHIPS = 4
N_DEV = 8
ANY = pl.BlockSpec(memory_space=pl.ANY)


def _place():
    x, y, c = lax.axis_index("x"), lax.axis_index("y"), lax.axis_index("c")
    others = []
    for j in (1, 2, 3):
        tx = (1 - x) if (j >> 1) else x
        ty = (1 - y) if (j & 1) else y
        others.append((tx, ty))
    return x, y, c, others


def _piece(ref, axis, k, half, rh, cs):
    if axis == 1:
        return ref.at[pl.ds(pl.multiple_of(half * rh, 16), rh), pl.ds(pl.multiple_of(k * cs, LANES), cs)]
    return ref.at[pl.ds(pl.multiple_of(k * 2 * rh + half * rh, 16), rh), :]


def _cast_into_full(shards, axes, place, *, name):
    n = len(shards)
    steps, trs = _group_tiles([s.shape[0] for s in shards], 16)

    def body(p_ref, *refs):
        for i in range(n):
            refs[n + i][...] = refs[i][...].astype(BF16)

    def out_spec(tr, cs, axis):
        if axis == 1:
            return pl.BlockSpec((tr, cs), lambda i, p_ref: (i, p_ref[0]))
        return pl.BlockSpec((tr, cs), lambda i, p_ref: (p_ref[0] * steps + i, 0))

    return pl.pallas_call(
        body,
        grid_spec=pltpu.PrefetchScalarGridSpec(
            num_scalar_prefetch=1, grid=(steps,),
            in_specs=[pl.BlockSpec((tr, s.shape[1]), lambda i, p_ref: (i, 0))
                      for tr, s in zip(trs, shards)],
            out_specs=[out_spec(tr, s.shape[1], ax) for tr, s, ax in zip(trs, shards, axes)]),
        out_shape=[jax.ShapeDtypeStruct((s.shape[0], s.shape[1] * N_CHIPS) if ax == 1
                                        else (s.shape[0] * N_CHIPS, s.shape[1]), BF16)
                   for s, ax in zip(shards, axes)],
        compiler_params=_cp("parallel"), name=name)(place, *shards)


HBM_SPEC = pl.BlockSpec(memory_space=pltpu.HBM)
SEM_SPEC = pl.BlockSpec(memory_space=pltpu.SEMAPHORE)
SPLIT_COPY = pltpu.CompilerParams(has_side_effects=pltpu.SideEffectType.DATAFLOW_SIDE_EFFECTING)
TOKEN = jax.ShapeDtypeStruct((8, LANES), F32)


def _geo(fulls, axes):
    out = []
    for f, ax in zip(fulls, axes):
        r, cs = (f.shape[0], f.shape[1] // N_CHIPS) if ax == 1 else (f.shape[0] // N_CHIPS, f.shape[1])
        out.append((ax, r // 2, cs))
    return out


def _gather_copies(refs, geo, send_sems, recv_sems):
    x, y, c, others = _place()
    chip = 2 * x + y
    cps = []
    for i, (ax, rh, cs) in enumerate(geo):
        mine = _piece(refs[i], ax, chip, c, rh, cs)
        for j, (tx, ty) in enumerate(others):
            cps.append(pltpu.make_async_remote_copy(
                src_ref=mine, dst_ref=mine, send_sem=send_sems.at[3 * i + j],
                recv_sem=recv_sems.at[3 * i + j], device_id=(tx, ty, c), device_id_type=MESH))
    return cps


def _allgather_start(fulls, axes, after, *, name):
    n = len(fulls)
    geo = _geo(fulls, axes)

    def body(*refs):
        w_refs = refs[:n]
        send_sems, recv_sems = refs[n + 1], refs[n + 2]
        token = refs[2 * n + 3]
        for cp in _gather_copies(w_refs, geo, send_sems, recv_sems):
            cp.start()
        token[...] = jnp.zeros_like(token)

    out = pl.pallas_call(
        body, name=name,
        out_shape=(pltpu.SemaphoreType.DMA((3 * n,)), pltpu.SemaphoreType.DMA((3 * n,)),
                   *[pltpu.HBM(f.shape, f.dtype) for f in fulls], TOKEN),
        in_specs=[HBM_SPEC] * n + [ANY],
        out_specs=(SEM_SPEC, SEM_SPEC, *[HBM_SPEC] * n, pl.BlockSpec(memory_space=pltpu.VMEM)),
        input_output_aliases={i: 2 + i for i in range(n)},
        compiler_params=SPLIT_COPY,
    )(*[pltpu.with_memory_space_constraint(f, pltpu.HBM) for f in fulls], after)
    return out[0], out[1], list(out[2:2 + n]), out[2 + n]


def _allgather_wait(send_sems, recv_sems, fulls, axes, after, *, name):
    n = len(fulls)
    geo = _geo(fulls, axes)
    afters = tuple(after) if isinstance(after, (tuple, list)) else (after,)

    def body(*refs):
        w_refs = refs[:n]
        for cp in _gather_copies(w_refs, geo, refs[n], refs[n + 1]):
            cp.wait_send()
            cp.wait_recv()

    return list(pl.pallas_call(
        body, name=name,
        out_shape=[pltpu.HBM(f.shape, f.dtype) for f in fulls],
        in_specs=[HBM_SPEC] * n + [SEM_SPEC, SEM_SPEC] + [ANY] * len(afters),
        out_specs=[HBM_SPEC] * n,
        input_output_aliases={i: i for i in range(n)},
        compiler_params=SPLIT_COPY,
    )(*fulls, send_sems, recv_sems, *afters))


def _allgather_forward(fulls, axes, *, name):
    n = len(fulls)
    geo = _geo(fulls, axes)

    def body(*refs):
        o_refs = refs[n:2 * n]
        send_sems, recv_sems = refs[2 * n:]
        x, y, c, others = _place()

        def rcopy(i, j, half, to):
            ax, rh, cs = geo[i]
            ref = _piece(o_refs[i], ax, 2 * others[j][0] + others[j][1], half, rh, cs)
            return pltpu.make_async_remote_copy(
                src_ref=ref, dst_ref=ref, send_sem=send_sems.at[3 * i + j],
                recv_sem=recv_sems.at[3 * i + j], device_id=to, device_id_type=MESH)

        cps = [rcopy(i, j, c, (x, y, 1 - c)) for i in range(n) for j in range(3)]
        for cp in cps:
            cp.start()
        for i in range(n):
            for j in range(3):
                rcopy(i, j, 1 - c, (x, y, c)).wait_recv()
        for cp in cps:
            cp.wait_send()

    return list(pl.pallas_call(
        body, in_specs=[ANY] * n, out_specs=[ANY] * n,
        out_shape=[jax.ShapeDtypeStruct(f.shape, f.dtype) for f in fulls],
        input_output_aliases={i: i for i in range(n)},
        scratch_shapes=[pltpu.SemaphoreType.DMA((3 * n,)), pltpu.SemaphoreType.DMA((3 * n,))],
        name=name)(*fulls))


def _forward_copies(refs, geo, send_sems, recv_sems):
    x, y, c, others = _place()
    cps = []
    for i, (ax, rh, cs) in enumerate(geo):
        for j, (tx, ty) in enumerate(others):
            ref = _piece(refs[i], ax, 2 * tx + ty, c, rh, cs)
            cps.append(pltpu.make_async_remote_copy(
                src_ref=ref, dst_ref=ref, send_sem=send_sems.at[3 * i + j],
                recv_sem=recv_sems.at[3 * i + j], device_id=(x, y, 1 - c), device_id_type=MESH))
    return cps


def _allgather_forward_start(fulls, axes, *, name):
    n = len(fulls)
    geo = _geo(fulls, axes)

    def body(*refs):
        token = refs[2 * n + 2]
        for cp in _forward_copies(refs[:n], geo, refs[n], refs[n + 1]):
            cp.start()
        token[...] = jnp.zeros_like(token)

    out = pl.pallas_call(
        body, name=name,
        out_shape=(pltpu.SemaphoreType.DMA((3 * n,)), pltpu.SemaphoreType.DMA((3 * n,)),
                   *[pltpu.HBM(f.shape, f.dtype) for f in fulls], TOKEN),
        in_specs=[HBM_SPEC] * n,
        out_specs=(SEM_SPEC, SEM_SPEC, *[HBM_SPEC] * n, pl.BlockSpec(memory_space=pltpu.VMEM)),
        input_output_aliases={i: 2 + i for i in range(n)},
        compiler_params=SPLIT_COPY,
    )(*fulls)
    return out[0], out[1], list(out[2:2 + n]), out[2 + n]


def _allgather_forward_wait(send_sems, recv_sems, fulls, axes, after, *, name):
    n = len(fulls)
    geo = _geo(fulls, axes)

    def body(*refs):
        for cp in _forward_copies(refs[:n], geo, refs[n], refs[n + 1]):
            cp.wait_send()
            cp.wait_recv()

    return list(pl.pallas_call(
        body, name=name,
        out_shape=[pltpu.HBM(f.shape, f.dtype) for f in fulls],
        in_specs=[HBM_SPEC] * n + [SEM_SPEC, SEM_SPEC, ANY],
        out_specs=[HBM_SPEC] * n,
        input_output_aliases={i: i for i in range(n)},
        compiler_params=SPLIT_COPY,
    )(*fulls, send_sems, recv_sems, after))


def _chip_copies(blk_ref, land_ref, send_sems, recv_sems):
    x, y, c, others = _place()
    return [pltpu.make_async_remote_copy(
        src_ref=blk_ref, dst_ref=land_ref.at[2 * x + y], send_sem=send_sems.at[j],
        recv_sem=recv_sems.at[j], device_id=(tx, ty, c), device_id_type=MESH)
        for j, (tx, ty) in enumerate(others)]


def _chip_exchange_start(blk, *, name):
    land = pltpu.with_memory_space_constraint(lax.empty((N_CHIPS, *blk.shape), blk.dtype), pltpu.HBM)

    def body(blk_ref, land_ref, send_sems, recv_sems, blk_out, land_out, token):
        for cp in _chip_copies(blk_ref, land_ref, send_sems, recv_sems):
            cp.start()
        token[...] = jnp.zeros_like(token)

    return pl.pallas_call(
        body, name=name,
        out_shape=(pltpu.SemaphoreType.DMA((3,)), pltpu.SemaphoreType.DMA((3,)),
                   pltpu.HBM(blk.shape, blk.dtype), pltpu.HBM(land.shape, land.dtype), TOKEN),
        in_specs=[HBM_SPEC] * 2,
        out_specs=(SEM_SPEC, SEM_SPEC, HBM_SPEC, HBM_SPEC, pl.BlockSpec(memory_space=pltpu.VMEM)),
        input_output_aliases={0: 2, 1: 3},
        compiler_params=SPLIT_COPY,
    )(pltpu.with_memory_space_constraint(blk, pltpu.HBM), land)


def _chip_exchange_wait(send_sems, recv_sems, blk, land, after, *, name):
    def body(blk_ref, land_ref, send_sems, recv_sems, after_ref, blk_out, land_out):
        for cp in _chip_copies(blk_ref, land_ref, send_sems, recv_sems):
            cp.wait_send()
            cp.wait_recv()

    return pl.pallas_call(
        body, name=name,
        out_shape=[pltpu.HBM(blk.shape, blk.dtype), pltpu.HBM(land.shape, land.dtype)],
        in_specs=[HBM_SPEC] * 2 + [SEM_SPEC, SEM_SPEC, ANY],
        out_specs=[HBM_SPEC] * 2,
        input_output_aliases={0: 0, 1: 1},
        compiler_params=SPLIT_COPY,
    )(blk, land, send_sems, recv_sems, after)[1]


def _scatter_geo(parts, axes):
    out = []
    for p, ax in zip(parts, axes):
        _, rh, cols = p.shape
        out.append((ax, rh, cols // N_CHIPS if ax == 1 else cols))
    return out


def _scatter_copies(p_refs, q_refs, geo, send_sems, recv_sems):
    x, y, c, others = _place()
    chip = 2 * x + y
    cps = []
    for i, (ax, rh, cw) in enumerate(geo):
        for j, (tx, ty) in enumerate(others):
            k = 2 * tx + ty
            src = (p_refs[i].at[0, :, pl.ds(pl.multiple_of(k * cw, LANES), cw)] if ax == 1
                   else p_refs[i].at[k])
            cps.append(pltpu.make_async_remote_copy(
                src_ref=src, dst_ref=q_refs[i].at[chip], send_sem=send_sems.at[3 * i + j],
                recv_sem=recv_sems.at[3 * i + j], device_id=(tx, ty, c), device_id_type=MESH))
    return cps


def _scatter_start(parts, axes, *, name):
    n = len(parts)
    geo = _scatter_geo(parts, axes)
    slots = [pltpu.HBM((N_CHIPS, rh, cw), p.dtype) for p, (_, rh, cw) in zip(parts, geo)]

    def body(*refs):
        p_refs, q_refs = refs[:n], refs[n:2 * n]
        send_sems, recv_sems = refs[2 * n], refs[2 * n + 1]
        token = refs[4 * n + 2]
        for cp in _scatter_copies(p_refs, q_refs, geo, send_sems, recv_sems):
            cp.start()
        token[...] = jnp.zeros_like(token)

    land = [pltpu.with_memory_space_constraint(lax.empty(s.inner_aval.shape, s.inner_aval.dtype), pltpu.HBM)
            for s in slots]
    out = pl.pallas_call(
        body, name=name,
        out_shape=(pltpu.SemaphoreType.DMA((3 * n,)), pltpu.SemaphoreType.DMA((3 * n,)),
                   *[pltpu.HBM(p.shape, p.dtype) for p in parts], *slots, TOKEN),
        in_specs=[HBM_SPEC] * (2 * n),
        out_specs=(SEM_SPEC, SEM_SPEC, *[HBM_SPEC] * (2 * n), pl.BlockSpec(memory_space=pltpu.VMEM)),
        input_output_aliases={i: 2 + i for i in range(2 * n)},
        compiler_params=SPLIT_COPY,
    )(*[pltpu.with_memory_space_constraint(p, pltpu.HBM) for p in parts], *land)
    return out[0], out[1], list(out[2:2 + n]), list(out[2 + n:2 + 2 * n]), out[2 + 2 * n]


def _scatter_wait(send_sems, recv_sems, parts, slots, axes, after, *, name):
    n = len(parts)
    geo = _scatter_geo(parts, axes)

    def body(*refs):
        p_refs, q_refs = refs[:n], refs[n:2 * n]
        for cp in _scatter_copies(p_refs, q_refs, geo, refs[2 * n], refs[2 * n + 1]):
            cp.wait_send()
            cp.wait_recv()

    out = pl.pallas_call(
        body, name=name,
        out_shape=[pltpu.HBM(a.shape, a.dtype) for a in (*parts, *slots)],
        in_specs=[HBM_SPEC] * (2 * n) + [SEM_SPEC, SEM_SPEC, ANY],
        out_specs=[HBM_SPEC] * (2 * n),
        input_output_aliases={i: i for i in range(2 * n)},
        compiler_params=SPLIT_COPY,
    )(*parts, *slots, send_sems, recv_sems, after)
    return list(out[:n]), list(out[n:])


def _swap_copies(g_refs, r_refs, shapes, send_sems, recv_sems):
    x, y, c, _ = _place()
    cps = []
    for i, shape in enumerate(shapes):
        rh = shape[1] // 2
        src = g_refs[i].at[:, pl.ds(pl.multiple_of((1 - c) * rh, 16), rh), :]
        cps.append(pltpu.make_async_remote_copy(
            src_ref=src, dst_ref=r_refs[i], send_sem=send_sems.at[i], recv_sem=recv_sems.at[i],
            device_id=(x, y, 1 - c), device_id_type=MESH))
    return cps


def _sibling_swap_start(grads, *, name):
    n = len(grads)
    shapes = [g.shape for g in grads]
    lands = [pltpu.HBM((s[0], s[1] // 2, s[2]), g.dtype) for s, g in zip(shapes, grads)]

    def body(*refs):
        g_refs, r_refs = refs[:n], refs[n:2 * n]
        token = refs[4 * n + 2]
        for cp in _swap_copies(g_refs, r_refs, shapes, refs[2 * n], refs[2 * n + 1]):
            cp.start()
        token[...] = jnp.zeros_like(token)

    land = [pltpu.with_memory_space_constraint(lax.empty(s.inner_aval.shape, s.inner_aval.dtype), pltpu.HBM)
            for s in lands]
    out = pl.pallas_call(
        body, name=name,
        out_shape=(pltpu.SemaphoreType.DMA((n,)), pltpu.SemaphoreType.DMA((n,)),
                   *[pltpu.HBM(g.shape, g.dtype) for g in grads], *lands, TOKEN),
        in_specs=[HBM_SPEC] * (2 * n),
        out_specs=(SEM_SPEC, SEM_SPEC, *[HBM_SPEC] * (2 * n), pl.BlockSpec(memory_space=pltpu.VMEM)),
        input_output_aliases={i: 2 + i for i in range(2 * n)},
        compiler_params=SPLIT_COPY,
    )(*[pltpu.with_memory_space_constraint(g, pltpu.HBM) for g in grads], *land)
    return out[0], out[1], list(out[2:2 + n]), list(out[2 + n:2 + 2 * n]), out[2 + 2 * n]


def _sibling_swap_wait(send_sems, recv_sems, grads, lands, after, *, name):
    n = len(grads)
    shapes = [g.shape for g in grads]

    def body(*refs):
        g_refs, r_refs = refs[:n], refs[n:2 * n]
        for cp in _swap_copies(g_refs, r_refs, shapes, refs[2 * n], refs[2 * n + 1]):
            cp.wait_send()
            cp.wait_recv()

    out = pl.pallas_call(
        body, name=name,
        out_shape=[pltpu.HBM(a.shape, a.dtype) for a in (*grads, *lands)],
        in_specs=[HBM_SPEC] * (2 * n) + [SEM_SPEC, SEM_SPEC, ANY],
        out_specs=[HBM_SPEC] * (2 * n),
        input_output_aliases={i: i for i in range(2 * n)},
        compiler_params=SPLIT_COPY,
    )(*grads, *lands, send_sems, recv_sems, after)
    return list(out[:n]), list(out[n:])


def _pair_add(g3s, rxs, place, *, out_dtype, name):
    n = len(g3s)
    steps, trs = _group_tiles([g.shape[1] // 2 for g in g3s], 16)

    def body(p_ref, *refs):
        for i in range(n):
            refs[2 * n + i][...] = (refs[2 * i][...] + refs[2 * i + 1][...]).astype(out_dtype)

    in_specs, out_specs = [], []
    for g, tr in zip(g3s, trs):
        blk = (g.shape[0], tr, g.shape[2])
        in_specs += [pl.BlockSpec(blk, lambda i, p_ref: (0, p_ref[1] * steps + i, 0)),
                     pl.BlockSpec(blk, lambda i, p_ref: (0, i, 0))]
        out_specs.append(pl.BlockSpec(blk, lambda i, p_ref: (0, i, 0)))
    return pl.pallas_call(
        body,
        grid_spec=pltpu.PrefetchScalarGridSpec(
            num_scalar_prefetch=1, grid=(steps,), in_specs=in_specs, out_specs=out_specs),
        out_shape=[jax.ShapeDtypeStruct((g.shape[0], g.shape[1] // 2, g.shape[2]), out_dtype)
                   for g in g3s],
        compiler_params=_cp("parallel"), name=name)(place, *[a for q in zip(g3s, rxs) for a in q])


def _sum_slots(q, *, name):
    ns, rows, cols = q.shape
    tr = next(t for t in (128, 64, 32, 16, 8) if rows % t == 0)

    def body(q_ref, o_ref):
        acc = q_ref[0].astype(F32)
        for k in range(1, ns):
            acc = acc + q_ref[k].astype(F32)
        o_ref[...] = acc

    return pl.pallas_call(
        body, grid=(rows // tr,),
        in_specs=[pl.BlockSpec((ns, tr, cols), lambda i: (0, i, 0))],
        out_specs=_rspec(tr, cols),
        out_shape=jax.ShapeDtypeStruct((rows, cols), F32),
        compiler_params=_cp("parallel"), name=name)(q)


def _sum_chips(qs, ps, place, axes, *, name):
    n = len(qs)
    per = N_CHIPS + 1
    steps, trs = _group_tiles([q.shape[1] for q in qs], 16)

    def body(p_ref, *refs):
        chip = p_ref[0]
        for i in range(n):
            q_refs, own_ref = refs[per * i:per * i + N_CHIPS], refs[per * i + N_CHIPS]
            acc = jnp.where(chip == 0, own_ref[...], q_refs[0][...]).astype(F32)
            for k in range(1, N_CHIPS):
                acc = acc + jnp.where(chip == k, own_ref[...], q_refs[k][...]).astype(F32)
            refs[per * n + i][...] = acc

    def slot_spec(k, tr, cw):
        return pl.BlockSpec((None, tr, cw),
                            lambda i, p_ref: (jnp.where(p_ref[0] == k, (k + 1) % N_CHIPS, k), i, 0))

    in_specs, out_specs, operands = [], [], []
    for q, p, ax, tr in zip(qs, ps, axes, trs):
        cw = q.shape[2]
        in_specs += [slot_spec(k, tr, cw) for k in range(N_CHIPS)]
        in_specs.append(pl.BlockSpec((None, tr, cw), (lambda i, p_ref: (0, i, p_ref[0])) if ax == 1
                                     else (lambda i, p_ref: (p_ref[0], i, 0))))
        out_specs.append(pl.BlockSpec((tr, cw), lambda i, p_ref: (p_ref[1] * steps + i, 0)))
        operands += [q] * N_CHIPS + [p]
    return pl.pallas_call(
        body,
        grid_spec=pltpu.PrefetchScalarGridSpec(
            num_scalar_prefetch=1, grid=(steps,), in_specs=in_specs, out_specs=out_specs),
        out_shape=[jax.ShapeDtypeStruct((2 * q.shape[1], q.shape[2]), F32) for q in qs],
        compiler_params=_cp("parallel"), name=name)(place, *operands)


def _sibling_share(shards, *, name):
    n = len(shards)

    def body(*refs):
        o_refs = refs[n:2 * n]
        send_sems, recv_sems = refs[2 * n:]
        x, y, c, _ = _place()
        cps = []
        for i in range(n):
            rh = shards[i].shape[0] // 2
            mine = o_refs[i].at[pl.ds(pl.multiple_of(c * rh, 8), rh), :]
            cp = pltpu.make_async_remote_copy(
                src_ref=mine, dst_ref=mine, send_sem=send_sems.at[i], recv_sem=recv_sems.at[i],
                device_id=(x, y, 1 - c), device_id_type=MESH)
            cp.start()
            cps.append(cp)
        for i in range(n):
            rh = shards[i].shape[0] // 2
            theirs = o_refs[i].at[pl.ds(pl.multiple_of((1 - c) * rh, 8), rh), :]
            pltpu.make_async_remote_copy(
                src_ref=theirs, dst_ref=theirs, send_sem=send_sems.at[i], recv_sem=recv_sems.at[i],
                device_id=(x, y, c), device_id_type=MESH).wait_recv()
        for cp in cps:
            cp.wait_send()

    return pl.pallas_call(
        body, in_specs=[ANY] * n, out_specs=[ANY] * n,
        out_shape=[jax.ShapeDtypeStruct(h.shape, h.dtype) for h in shards],
        input_output_aliases={i: i for i in range(n)},
        scratch_shapes=[pltpu.SemaphoreType.DMA((n,)), pltpu.SemaphoreType.DMA((n,))],
        name=name)(*shards)


def _gather_all(blk, *, name, after=None, shares=()):
    rows, cols = blk.shape
    extra = [] if after is None else [after]
    n_s = len(shares)

    def body(x_ref, *refs):
        s_refs = refs[len(extra) + n_s + 1:len(extra) + 2 * n_s + 1]
        out_ref = refs[len(extra) + n_s]
        send_sems, recv_sems, local_sem, s_send, s_recv = refs[len(extra) + 2 * n_s + 1:]
        x, y, c = lax.axis_index("x"), lax.axis_index("y"), lax.axis_index("c")
        me = 4 * x + 2 * y + c
        mine = pltpu.make_async_copy(x_ref, out_ref.at[me], local_sem)
        mine.start()
        cps = []
        for i in range(n_s):
            rh = shares[i].shape[0] // 2
            half = s_refs[i].at[pl.ds(pl.multiple_of(c * rh, 8), rh), :]
            cp = pltpu.make_async_remote_copy(
                src_ref=half, dst_ref=half, send_sem=s_send.at[i], recv_sem=s_recv.at[i],
                device_id=(x, y, 1 - c), device_id_type=MESH)
            cp.start()
            cps.append(cp)
        for k in range(1, N_DEV):
            tx = (1 - x) if (k >> 2) & 1 else x
            ty = (1 - y) if (k >> 1) & 1 else y
            tc = (1 - c) if k & 1 else c
            cp = pltpu.make_async_remote_copy(
                src_ref=x_ref, dst_ref=out_ref.at[me], send_sem=send_sems.at[k - 1],
                recv_sem=recv_sems.at[k - 1], device_id=(tx, ty, tc), device_id_type=MESH)
            cp.start()
            cps.append(cp)
        for k in range(1, N_DEV):
            tx = (1 - x) if (k >> 2) & 1 else x
            ty = (1 - y) if (k >> 1) & 1 else y
            tc = (1 - c) if k & 1 else c
            got = out_ref.at[4 * tx + 2 * ty + tc]
            pltpu.make_async_remote_copy(
                src_ref=got, dst_ref=got, send_sem=send_sems.at[k - 1], recv_sem=recv_sems.at[k - 1],
                device_id=(x, y, c), device_id_type=MESH).wait_recv()
        for i in range(n_s):
            rh = shares[i].shape[0] // 2
            theirs = s_refs[i].at[pl.ds(pl.multiple_of((1 - c) * rh, 8), rh), :]
            pltpu.make_async_remote_copy(
                src_ref=theirs, dst_ref=theirs, send_sem=s_send.at[i], recv_sem=s_recv.at[i],
                device_id=(x, y, c), device_id_type=MESH).wait_recv()
        for cp in cps:
            cp.wait_send()
        mine.wait()

    vm = pl.BlockSpec(memory_space=pltpu.VMEM)
    out = pl.pallas_call(
        body, in_specs=[vm] + [ANY] * (len(extra) + n_s), out_specs=[vm] + [ANY] * n_s,
        out_shape=[jax.ShapeDtypeStruct((N_DEV, rows, cols), blk.dtype)]
        + [jax.ShapeDtypeStruct(s.shape, s.dtype) for s in shares],
        input_output_aliases={1 + len(extra) + i: 1 + i for i in range(n_s)},
        scratch_shapes=[pltpu.SemaphoreType.DMA((N_DEV - 1,)), pltpu.SemaphoreType.DMA((N_DEV - 1,)),
                        pltpu.SemaphoreType.DMA, pltpu.SemaphoreType.DMA((max(n_s, 1),)),
                        pltpu.SemaphoreType.DMA((max(n_s, 1),))],
        name=name)(blk, *extra, *shares)
    return (out[0], *out[1:]) if n_s else out[0]


def _as_rows(a):
    flat = a.reshape(-1)
    n = flat.shape[0]
    rows = -(-n // (8 * LANES)) * 8
    return jnp.pad(flat, (0, rows * LANES - n)).reshape(rows, LANES)


def _from_rows(p, shape):
    n = int(np.prod(shape))
    return p.reshape(-1)[:n].reshape(shape)


WEIGHT_AXES = (1, 1, 1, 0, 1, 0)
WIRE = BF16


def kernel(x, meta_tokens, w_in, w_na_out, w_hg_out, w_o, w_up, w_down, norm_mix, norm_mlp, norm_final, hg_norm, na_rpb, hg_lb_logits, loss_target, m_meta_tokens, m_w_in, m_w_na_out, m_w_hg_out, m_w_o, m_w_up, m_w_down, m_norm_mix, m_norm_mlp, m_norm_final, m_hg_norm, m_na_rpb, m_hg_lb_logits, v_meta_tokens, v_w_in, v_w_na_out, v_w_hg_out, v_w_o, v_w_up, v_w_down, v_norm_mix, v_norm_mlp, v_norm_final, v_hg_norm, v_na_rpb, v_hg_lb_logits):
    xi, yi, ci = lax.axis_index("x"), lax.axis_index("y"), lax.axis_index("c")
    chip = 2 * xi + yi
    d = x.shape[-1]
    dshard = meta_tokens.shape[1]
    hgw = hg_norm.shape[1]
    lbs = hg_lb_logits.shape[2]
    big = [w_in[0], w_na_out[0], w_hg_out[0], w_o[0], w_up[0], w_down[0]]
    big_m = [m_w_in[0], m_w_na_out[0], m_w_hg_out[0], m_w_o[0], m_w_up[0], m_w_down[0]]
    big_v = [v_w_in[0], v_w_na_out[0], v_w_hg_out[0], v_w_o[0], v_w_up[0], v_w_down[0]]

    place = jnp.stack([chip, ci]).astype(jnp.int32)
    in_axes, rest_axes = WEIGHT_AXES[:1], WEIGHT_AXES[1:]
    own_w = _cast_into_full(big, WEIGHT_AXES, place, name="cast_shards")
    small_in = jnp.concatenate([_as_rows(meta_tokens), _as_rows(hg_lb_logits)], axis=0)
    sm_send, sm_recv, sm_blk, sm_land, sm_token = _chip_exchange_start(small_in,
                                                                       name="small_params_start")
    in_send, in_recv, in_bufs, in_token = _allgather_start(own_w[:1], in_axes, sm_token,
                                                           name="weight_allgather_in_start")
    ag_send, ag_recv, ag_bufs, ag_token = _allgather_start(own_w[1:], rest_axes, in_token,
                                                           name="weight_allgather_rest_start")
    sm_land = _chip_exchange_wait(sm_send, sm_recv, sm_blk, sm_land, ag_token,
                                  name="small_params_wait")
    small_all = lax.dynamic_update_slice(sm_land, small_in[None], (chip, 0, 0))
    forward = {}

    def first_weight(after):
        got = _allgather_wait(in_send, in_recv, in_bufs, in_axes, after,
                              name="weight_allgather_in_wait")
        return _allgather_forward(got, in_axes, name="weight_allgather_in_forward")[0]

    def rest_landed(after):
        got = _allgather_wait(ag_send, ag_recv, ag_bufs, rest_axes, after,
                              name="weight_allgather_rest_wait")
        send, recv, bufs, token = _allgather_forward_start(
            got, rest_axes, name="weight_allgather_rest_forward_start")
        forward["rest"] = (send, recv, bufs)
        return token

    def rest_weights(after):
        return _allgather_forward_wait(*forward["rest"], rest_axes, after,
                                       name="weight_allgather_rest_forward_wait")

    n_meta_rows = N_META * dshard // LANES
    meta_full = (small_all[:, :n_meta_rows].reshape(N_CHIPS, N_META, dshard)
                 .transpose(1, 0, 2).reshape(N_META, d))
    lbl_full = (small_all[:, n_meta_rows:].reshape(N_CHIPS, -1)[:, :4 * lbs]
                .reshape(N_CHIPS, 2, 2, lbs).transpose(1, 2, 0, 3).reshape(2, 2, N_CHIPS * lbs))
    lb = jax.nn.softmax(lbl_full, axis=1)[:, 0]

    def by_chip(dws, axes):
        return [g.reshape(1, *g.shape) if ax == 1
                else g.reshape(N_CHIPS, g.shape[0] // N_CHIPS, g.shape[1]) for g, ax in zip(dws, axes)]

    flying = {}

    def scatter(tag, axes, g3, rx):
        parts = _pair_add(g3, rx, place, out_dtype=WIRE, name=f"grad_pair_add_{tag}")
        send, recv, parts, slots, token = _scatter_start(parts, axes,
                                                         name=f"grad_scatter_{tag}_start")
        flying[tag] = (send, recv, parts, slots)
        return token

    def swap(tag, axes):
        def start(dws):
            send, recv, g3, lands, token = _sibling_swap_start(
                by_chip(dws, axes), name=f"grad_sibling_swap_{tag}_start")
            flying["swap_" + tag] = (send, recv, g3, lands)
            return token

        def finish(after):
            g3, rx = _sibling_swap_wait(*flying["swap_" + tag], after,
                                        name=f"grad_sibling_swap_{tag}_wait")
            return scatter(tag, axes, g3, rx)
        return start, finish

    swap_rest, scatter_rest = swap("rest", rest_axes)
    swap_in, scatter_in = swap("in", in_axes)

    def landed(tag, axes, after):
        return _scatter_wait(*flying[tag], axes, after, name=f"grad_scatter_{tag}_wait")

    (loss, dx, dmeta, *_, dg_mix, dg_mlp, dg_fin, d_gain, d_rpb, d_lb) = _local_step(
        x[0], loss_target[0], meta_full, first_weight, rest_weights, norm_mix, norm_mlp,
        norm_final.reshape(1, d), hg_norm, na_rpb[0], lb, swap_rest, scatter_rest,
        lambda dw_in: swap_in([dw_in]), rest_landed, scatter_in)

    parts_rest, slots_rest = landed("rest", rest_axes, dx)
    g_rest = _sibling_share(_sum_chips(slots_rest, parts_rest, place, rest_axes,
                                       name="grad_sum_chips_rest"),
                            name="grad_sibling_share_rest")
    out_rest = _adamw(big[1:], g_rest, big_m[1:], big_v[1:], name="adamw_rest")
    parts_in, slots_in = landed("in", in_axes, out_rest[-1][0])
    half_in = _sum_chips(slots_in, parts_in, place, in_axes, name="grad_sum_chips_in")

    d_rpb_c = d_rpb[:, :2 * NA_WIN_W - 1]
    small_g = [dmeta, dg_mix, dg_mlp, dg_fin, d_gain, d_rpb_c, d_lb, loss]
    packed = jnp.concatenate([_as_rows(a) for a in small_g], axis=0)
    gathered, g_in = _gather_all(packed, shares=half_in, name="gather_small_grads")
    total = _sum_slots(gathered, name="sum_small_grads")
    offs = np.cumsum([0] + [_as_rows(a).shape[0] for a in small_g])
    take = lambda i, shape: _from_rows(total[offs[i]:offs[i + 1]], shape)
    g_meta_full = take(0, (N_META, d))
    g_norm_mix, g_norm_mlp = take(1, (1, d)), take(2, (1, d))
    g_norm_final = take(3, (d,))
    g_hg_norm = take(4, (1, hgw))
    g_rpb = take(5, na_rpb.shape)
    g_lb = take(6, (2, hgw))
    loss_total = take(7, (1, LANES))[0, 0]
    g_meta = lax.dynamic_slice_in_dim(g_meta_full, chip * dshard, dshard, axis=1)
    dl0 = lb * (1.0 - lb) * g_lb
    g_lbl_full = jnp.stack([dl0, -dl0], axis=1)
    g_lbl = lax.dynamic_slice_in_dim(g_lbl_full, chip * lbs, lbs, axis=2)

    big_out = _adamw(big[:1], [g_in], big_m[:1], big_v[:1], name="adamw_in") + out_rest
    small_w = [meta_tokens, norm_mix, norm_mlp, norm_final, hg_norm, na_rpb, hg_lb_logits]
    small_gr = [g_meta, g_norm_mix, g_norm_mlp, g_norm_final, g_hg_norm, g_rpb, g_lbl]
    small_m = [m_meta_tokens, m_norm_mix, m_norm_mlp, m_norm_final, m_hg_norm, m_na_rpb, m_hg_lb_logits]
    small_v = [v_meta_tokens, v_norm_mix, v_norm_mlp, v_norm_final, v_hg_norm, v_na_rpb, v_hg_lb_logits]
    pk = lambda lst: jnp.concatenate([_as_rows(a) for a in lst], axis=0)
    ((sd, sm, sv, _),) = _adamw([pk(small_w)], [pk(small_gr)], [pk(small_m)], [pk(small_v)],
                             name="adamw_small")
    soffs = np.cumsum([0] + [_as_rows(a).shape[0] for a in small_w])
    unpk = lambda p: [_from_rows(p[soffs[i]:soffs[i + 1]], small_w[i].shape) for i in range(len(small_w))]
    sd, sm, sv = unpk(sd), unpk(sm), unpk(sv)

    def order(bigs, smalls):
        return [smalls[0]] + [b.reshape(1, *b.shape) for b in bigs] + smalls[1:]

    grads = order([o[3] for o in big_out], small_gr)
    deltas = order([o[0] for o in big_out], sd)
    new_m = order([o[1] for o in big_out], sm)
    new_v = order([o[2] for o in big_out], sv)
    return (loss_total, dx.reshape(1, *dx.shape), *grads, *deltas, *new_m, *new_v)
```

```python
import functools

import numpy as np
import jax
import jax.numpy as jnp
from jax import lax
from jax.experimental import pallas as pl
from jax.experimental.pallas import tpu as pltpu

F32 = jnp.float32
BF16 = jnp.bfloat16
HIGHEST = lax.Precision.HIGHEST

GRID_W = 64
N_META = 16
EPS = 1e-6
NA_HEAD_DIM = 64
NA_WIN_H = 8
NA_WIN_W = 16
HG_DK = 128
LANES = 128
ROW_ALIGN = 128
VMEM_LIMIT = 48 * 1024 * 1024
ADAM_LR = 0.001
ADAM_B1 = 0.9
ADAM_B2 = 0.999
ADAM_EPS = 1e-08
ADAM_WD = 0.01
ADAM_STEP = 10

MESH = pl.DeviceIdType.MESH


def _cp(*sem):
    return pltpu.CompilerParams(dimension_semantics=sem, vmem_limit_bytes=VMEM_LIMIT)


def _sigmoid(x):
    return 0.5 * jnp.tanh(0.5 * x) + 0.5


def _dot(a, b, dims, precision=None):
    return lax.dot_general(a, b, (dims, ((), ())), preferred_element_type=F32, precision=precision)


def _nn(a, b, **kw):
    return _dot(a, b, ((1,), (0,)), **kw)


def _nt(a, b, **kw):
    return _dot(a, b, ((1,), (1,)), **kw)


def _tn(a, b, **kw):
    return _dot(a, b, ((0,), (0,)), **kw)


def _matmul(a, b, *, ta=False, tb=False, tm=None, tn=None, tk=None, out_dtype=F32, name,
            precision=None, after=None, epilogue=None, tiles=(), out_dtypes=None):
    extra = [] if after is None else [after]
    single = out_dtypes is None
    if single:
        out_dtypes = (out_dtype,)
    n_t, n_o = len(tiles), len(out_dtypes)
    if ta:
        kdim, m = a.shape
    else:
        m, kdim = a.shape
    if tb:
        n, k2 = b.shape
    else:
        k2, n = b.shape
    assert kdim == k2, (a.shape, b.shape, ta, tb)
    if tm is None:
        if ta:
            tm = next(t for t in (1024, 512, 256, 128, m) if m % t == 0)
        else:
            tm = m // 2 if (m // 2) % 16 == 0 and m > 512 else m
    if tn is None:
        wide = (1024,) if not ta and len(tiles) <= 1 else ()
        tn = next(t for t in (*wide, 512, 256, 128, n) if n % t == 0)
    if tk is None:
        tk = kdim if ta else next(t for t in (2048, 1024, 512, 256, 128, kdim) if kdim % t == 0)
    assert m % tm == 0 and n % tn == 0 and kdim % tk == 0, (m, n, kdim, tm, tn, tk)
    nk = kdim // tk
    op_dtype = F32 if precision is not None else BF16

    def body(a_ref, b_ref, *refs):
        t_refs = refs[:n_t]
        o_refs = refs[n_t + len(extra):n_t + len(extra) + n_o]
        av = a_ref[...].astype(op_dtype)
        bv = b_ref[...].astype(op_dtype)
        dims = ((0 if ta else 1,), (1 if tb else 0,))
        part = _dot(av, bv, dims, precision=precision)

        def finish(acc):
            outs = (acc,) if epilogue is None else epilogue(acc, *[t[...] for t in t_refs])
            for o_ref, val in zip(o_refs, outs):
                o_ref[...] = val.astype(o_ref.dtype)

        if nk == 1:
            finish(part)
            return
        acc_ref = refs[-1]
        kk = pl.program_id(2)

        @pl.when(kk == 0)
        def _():
            acc_ref[...] = part

        @pl.when((kk > 0) & (kk < nk - 1))
        def _():
            acc_ref[...] += part

        @pl.when(kk == nk - 1)
        def _():
            finish(acc_ref[...] + part)

    a_spec = (pl.BlockSpec((tk, tm), lambda i, j, k: (k, i)) if ta
              else pl.BlockSpec((tm, tk), lambda i, j, k: (i, k)))
    b_spec = (pl.BlockSpec((tn, tk), lambda i, j, k: (j, k)) if tb
              else pl.BlockSpec((tk, tn), lambda i, j, k: (k, j)))
    for _, off in tiles:
        assert off % tn == 0, (off, tn)
    t_specs = [pl.BlockSpec((tm, tn), functools.partial(lambda i, j, k, o: (i, o + j), o=off // tn))
               for _, off in tiles]
    o_spec = pl.BlockSpec((tm, tn), lambda i, j, k: (i, j))
    outs = pl.pallas_call(
        body,
        grid=(m // tm, n // tn, nk),
        in_specs=[a_spec, b_spec] + t_specs + [pl.BlockSpec(memory_space=pl.ANY)] * len(extra),
        out_specs=[o_spec] * n_o,
        out_shape=[jax.ShapeDtypeStruct((m, n), dt) for dt in out_dtypes],
        scratch_shapes=[pltpu.VMEM((tm, tn), F32)] if nk > 1 else [],
        compiler_params=_cp("parallel", "parallel", "arbitrary"),
        name=name,
    )(a, b, *[t for t, _ in tiles], *extra)
    return outs[0] if single else outs


def _rspec(tr, w, cb=0):
    return pl.BlockSpec((tr, w), lambda i: (i, cb))


def _fspec(shape):
    nd = len(shape)
    return pl.BlockSpec(shape, lambda i: (0,) * nd)


ROW_VMEM_BUDGET = 20 * 1024 * 1024
ROW_MIN_STEPS = 4


def _row_tile(lp, row_bytes):
    for k in range(ROW_MIN_STEPS, lp // 16 + 1):
        tr = lp // k
        if lp % k == 0 and tr % 16 == 0 and 2 * tr * row_bytes <= ROW_VMEM_BUDGET:
            return tr
    return lp


def _token_rows_copy(i, n_tiles, tr, n_tok, tok_ref, buf_ref, sem, *, to_tokens, start=True,
                     wait=True):
    assert n_tiles >= 2 and 0 < n_tok + N_META - (n_tiles - 1) * tr <= tr

    def run(tok_row, buf_row, count):
        tok = tok_ref.at[pl.ds(tok_row, count), :]
        buf = buf_ref.at[pl.ds(buf_row, count), :]
        cp = pltpu.make_async_copy(buf, tok, sem) if to_tokens else pltpu.make_async_copy(tok, buf, sem)
        if start:
            cp.start()
        if wait:
            cp.wait()

    @pl.when(i == 0)
    def _():
        run(0, N_META, tr - N_META)

    if n_tiles > 2:
        @pl.when((i > 0) & (i < n_tiles - 1))
        def _():
            run(pl.multiple_of(i * tr - N_META, 8), 0, tr)

    @pl.when(i == n_tiles - 1)
    def _():
        run((n_tiles - 1) * tr - N_META, 0, n_tok + N_META - (n_tiles - 1) * tr)


def _embed_norm(x, tgt, meta, g, *, lp, name):
    n_tok, d = x.shape
    tr = _row_tile(lp, d * (4 + 2 + 4))
    n_tiles = lp // tr

    def body(x_ref, tgt_ref, meta_ref, g_ref, h_ref, o_ref, tp_ref, buf_ref, tbuf_ref, sems):
        i = pl.program_id(0)
        buf_ref[...] = jnp.zeros_like(buf_ref)
        tbuf_ref[...] = jnp.zeros_like(tbuf_ref)

        @pl.when(i == 0)
        def _():
            buf_ref[0:N_META, :] = meta_ref[...]

        _token_rows_copy(i, n_tiles, tr, n_tok, tgt_ref, tbuf_ref, sems.at[1], to_tokens=False,
                         wait=False)
        _token_rows_copy(i, n_tiles, tr, n_tok, x_ref, buf_ref, sems.at[0], to_tokens=False)
        xv = buf_ref[...]
        h_ref[...] = xv
        r = lax.rsqrt(jnp.mean(xv * xv, axis=-1, keepdims=True) + EPS)
        o_ref[...] = (xv * r * g_ref[...]).astype(BF16)
        _token_rows_copy(i, n_tiles, tr, n_tok, tgt_ref, tbuf_ref, sems.at[1], to_tokens=False,
                         start=False)
        tp_ref[...] = tbuf_ref[...]

    return pl.pallas_call(
        body, grid=(n_tiles,),
        in_specs=[ANY, ANY, _fspec((N_META, d)), _fspec((1, d))],
        out_specs=[_rspec(tr, d), _rspec(tr, d), _rspec(tr, d)],
        out_shape=[jax.ShapeDtypeStruct((lp, d), F32), jax.ShapeDtypeStruct((lp, d), BF16),
                   jax.ShapeDtypeStruct((lp, d), F32)],
        scratch_shapes=[pltpu.VMEM((tr, d), F32), pltpu.VMEM((tr, d), F32),
                        pltpu.SemaphoreType.DMA((2,))],
        compiler_params=_cp("parallel"), name=name)(x, tgt, meta, g)


def _residual_norm(h, t, g, *, name):
    lp, d = h.shape
    tr = _row_tile(lp, d * (4 + 4 + 4 + 2))

    def body(h_ref, t_ref, g_ref, h1_ref, m_ref):
        xv = h_ref[...] + t_ref[...]
        h1_ref[...] = xv
        r = lax.rsqrt(jnp.mean(xv * xv, axis=-1, keepdims=True) + EPS)
        m_ref[...] = (xv * r * g_ref[...]).astype(BF16)

    return pl.pallas_call(
        body, grid=(lp // tr,),
        in_specs=[_rspec(tr, d), _rspec(tr, d), _fspec((1, d))],
        out_specs=[_rspec(tr, d), _rspec(tr, d)],
        out_shape=[jax.ShapeDtypeStruct((lp, d), F32), jax.ShapeDtypeStruct((lp, d), BF16)],
        compiler_params=_cp("parallel"), name=name)(h, t, g)


def _rmsnorm_bwd_add(x, g, dy, dres, *, name):
    lp, d = x.shape
    tr = _row_tile(lp, d * (4 * 4 + 2))

    def body(x_ref, g_ref, dy_ref, dr_ref, dx_ref, dx16_ref, dg_ref):
        @pl.when(pl.program_id(0) == 0)
        def _():
            dg_ref[...] = jnp.zeros_like(dg_ref)

        xv = x_ref[...]
        r = lax.rsqrt(jnp.mean(xv * xv, axis=-1, keepdims=True) + EPS)
        xh = xv * r
        dyv = dy_ref[...]
        dg_ref[...] += jnp.sum(dyv * xh, axis=0, keepdims=True)
        dxh = dyv * g_ref[...]
        dx = dr_ref[...] + r * (dxh - xh * jnp.mean(dxh * xh, axis=-1, keepdims=True))
        dx_ref[...] = dx
        dx16_ref[...] = dx.astype(BF16)

    return pl.pallas_call(
        body, grid=(lp // tr,),
        in_specs=[_rspec(tr, d), _fspec((1, d)), _rspec(tr, d), _rspec(tr, d)],
        out_specs=[_rspec(tr, d), _rspec(tr, d), _fspec((1, d))],
        out_shape=[jax.ShapeDtypeStruct((lp, d), F32), jax.ShapeDtypeStruct((lp, d), BF16),
                   jax.ShapeDtypeStruct((1, d), F32)],
        compiler_params=_cp("arbitrary"), name=name)(x, g, dy, dres)


def _rmsnorm_bwd_tokens(x, g, dy, dres, *, n_tok, name):
    lp, d = x.shape
    tr = _row_tile(lp, d * 4 * 4)
    n_tiles = lp // tr

    def body(x_ref, g_ref, dy_ref, dr_ref, dtok_ref, dmeta_ref, dg_ref, buf_ref, sem):
        i = pl.program_id(0)

        @pl.when(i == 0)
        def _():
            dg_ref[...] = jnp.zeros_like(dg_ref)

        xv = x_ref[...]
        r = lax.rsqrt(jnp.mean(xv * xv, axis=-1, keepdims=True) + EPS)
        xh = xv * r
        dyv = dy_ref[...]
        dg_ref[...] += jnp.sum(dyv * xh, axis=0, keepdims=True)
        dxh = dyv * g_ref[...]
        buf_ref[...] = dr_ref[...] + r * (dxh - xh * jnp.mean(dxh * xh, axis=-1, keepdims=True))

        @pl.when(i == 0)
        def _():
            dmeta_ref[...] = buf_ref[0:N_META, :]

        _token_rows_copy(i, n_tiles, tr, n_tok, dtok_ref, buf_ref, sem, to_tokens=True)

    return pl.pallas_call(
        body, grid=(n_tiles,),
        in_specs=[_rspec(tr, d), _fspec((1, d)), _rspec(tr, d), _rspec(tr, d)],
        out_specs=[ANY, _fspec((N_META, d)), _fspec((1, d))],
        out_shape=[jax.ShapeDtypeStruct((n_tok, d), F32), jax.ShapeDtypeStruct((N_META, d), F32),
                   jax.ShapeDtypeStruct((1, d), F32)],
        scratch_shapes=[pltpu.VMEM((tr, d), F32), pltpu.SemaphoreType.DMA],
        compiler_params=_cp("arbitrary"), name=name)(x, g, dy, dres)


def _final_loss(h1, t2, g, tgt, *, n_tok, name):
    lp, d = h1.shape
    tr = _row_tile(lp, d * (4 * 4 + 2))
    n_tiles = lp // tr

    def body(h_ref, t_ref, g_ref, tg_ref, dh_ref, dh16_ref, loss_ref, dg_ref):
        i = pl.program_id(0)

        @pl.when(i == 0)
        def _():
            loss_ref[...] = jnp.zeros_like(loss_ref)
            dg_ref[...] = jnp.zeros_like(dg_ref)

        xv = h_ref[...] + t_ref[...]
        r = lax.rsqrt(jnp.mean(xv * xv, axis=-1, keepdims=True) + EPS)
        xh = xv * r
        gv = g_ref[...]
        row = i * tr + lax.broadcasted_iota(jnp.int32, (tr, 1), 0)
        valid = (row >= N_META) & (row < N_META + n_tok)
        err = jnp.where(valid, xh * gv - tg_ref[...], 0.0)
        loss_ref[...] += jnp.sum(0.5 * err * err) / d
        dy = err / d
        dg_ref[...] += jnp.sum(dy * xh, axis=0, keepdims=True)
        dxh = dy * gv
        dh = r * (dxh - xh * jnp.mean(dxh * xh, axis=-1, keepdims=True))
        dh_ref[...] = dh
        dh16_ref[...] = dh.astype(BF16)

    return pl.pallas_call(
        body, grid=(n_tiles,),
        in_specs=[_rspec(tr, d), _rspec(tr, d), _fspec((1, d)), _rspec(tr, d)],
        out_specs=[_rspec(tr, d), _rspec(tr, d), _fspec((1, LANES)), _fspec((1, d))],
        out_shape=[jax.ShapeDtypeStruct((lp, d), F32), jax.ShapeDtypeStruct((lp, d), BF16),
                   jax.ShapeDtypeStruct((1, LANES), F32), jax.ShapeDtypeStruct((1, d), F32)],
        compiler_params=_cp("arbitrary"), name=name)(h1, t2, g, tgt)


def _hg_out(o_f, o_b, proj, gain, *, col_g, name):
    lp, w = o_f.shape
    tr = _row_tile(lp, w * (3 * 4 + 2))
    hh = w // HG_DK

    def body(of_ref, ob_ref, g_ref, gain_ref, y_ref):
        gv = g_ref[...]
        sg = gv * _sigmoid(gv)
        for h in range(hh):
            sl = slice(h * HG_DK, (h + 1) * HG_DK)
            o = of_ref[:, sl] + ob_ref[:, sl]
            r = lax.rsqrt(jnp.mean(o * o, axis=-1, keepdims=True) + EPS)
            y_ref[:, sl] = (o * r * gain_ref[:, sl] * sg[:, sl]).astype(BF16)

    return pl.pallas_call(
        body, grid=(lp // tr,),
        in_specs=[_rspec(tr, w), _rspec(tr, w), _rspec(tr, w, col_g // w), _fspec((1, w))],
        out_specs=_rspec(tr, w),
        out_shape=jax.ShapeDtypeStruct((lp, w), BF16),
        compiler_params=_cp("parallel"), name=name)(o_f, o_b, proj, gain)


def _hg_out_bwd(o_f, o_b, proj, gain, dy, *, col_g, name):
    lp, w = o_f.shape
    tr = _row_tile(lp, w * (5 * 4 + 2))
    hh = w // HG_DK

    def body(of_ref, ob_ref, g_ref, gain_ref, dy_ref, do_ref, dg_ref, dgain_ref):
        @pl.when(pl.program_id(0) == 0)
        def _():
            dgain_ref[...] = jnp.zeros_like(dgain_ref)

        for h in range(hh):
            sl = slice(h * HG_DK, (h + 1) * HG_DK)
            gv = g_ref[:, sl]
            s = _sigmoid(gv)
            sg = gv * s
            dsg = s + gv * s * (1.0 - s)
            o = of_ref[:, sl] + ob_ref[:, sl]
            r = lax.rsqrt(jnp.mean(o * o, axis=-1, keepdims=True) + EPS)
            on = o * r
            dyv = dy_ref[:, sl]
            gn = gain_ref[:, sl]
            dgain_ref[:, sl] += jnp.sum(dyv * on * sg, axis=0, keepdims=True)
            dg_ref[:, sl] = (dyv * on * gn * dsg).astype(BF16)
            don = dyv * gn * sg
            do_ref[:, sl] = r * (don - on * jnp.mean(don * on, axis=-1, keepdims=True))

    return pl.pallas_call(
        body, grid=(lp // tr,),
        in_specs=[_rspec(tr, w), _rspec(tr, w), _rspec(tr, w, col_g // w), _fspec((1, w)),
                  _rspec(tr, w)],
        out_specs=[_rspec(tr, w), _rspec(tr, w), _fspec((1, w))],
        out_shape=[jax.ShapeDtypeStruct((lp, w), F32), jax.ShapeDtypeStruct((lp, w), BF16),
                   jax.ShapeDtypeStruct((1, w), F32)],
        compiler_params=_cp("arbitrary"), name=name)(o_f, o_b, proj, gain, dy)


HG_ROWS = 128
HG_HALVES = (1, 2, 4, 8, 16, 32, 64)


def _hg_gates(zq, z, lbv):
    sq = _sigmoid(zq)
    s = _sigmoid(z)
    f = lbv + (1.0 - lbv) * s
    kk = (1.0 - lbv) * (1.0 - s)
    return zq * sq, sq, s, f, jnp.log(f), kk


def _block_cumsum(g, pos, suffix):
    x = g
    for k in HG_HALVES:
        if suffix:
            x = x + jnp.where(pos < HG_ROWS - k, pltpu.roll(x, HG_ROWS - k, 0), 0.0)
        else:
            x = x + jnp.where(pos >= k, pltpu.roll(x, k, 0), 0.0)
    return x


def _pair_levels(b, pos, reverse):
    out = []
    first = b
    for m in HG_HALVES:
        if m > 1:
            first = jnp.where((pos & (m - 1)) >= m // 2, pltpu.roll(first, m // 2, 0), first)
        nxt = pltpu.roll(first, HG_ROWS - m, 0)
        upper = (pos & (2 * m - 1)) >= m
        if reverse:
            eq = jnp.where(upper, 0.0, jnp.exp(b - nxt))
            ek = jnp.where(upper, jnp.exp(first - b), 0.0)
        else:
            eq = jnp.where(upper, jnp.exp(b - first), 0.0)
            ek = jnp.where(upper, 0.0, jnp.exp(nxt - b))
        out.append((eq, ek))
    return out


def _pair_masks(mask_ref):
    ri = lax.broadcasted_iota(jnp.int32, (HG_ROWS, HG_ROWS), 0)
    ci = lax.broadcasted_iota(jnp.int32, (HG_ROWS, HG_ROWS), 1)
    for i, m in enumerate(HG_HALVES):
        sh = m.bit_length()
        mask_ref[i] = jnp.where((ri >> sh) == (ci >> sh), 1.0, 0.0)


def _hg_scan_fwd(proj, lb, *, reverse, col_q, col_z, col_i, hh, name):
    lp = proj.shape[0]
    n_blocks = lp // HG_ROWS
    last = 0 if reverse else HG_ROWS - 1

    def body(q_ref, z_ref, i_ref, lb_ref, o_ref, st_ref, mask_ref):
        lbv = lb_ref[...]
        pos = lax.broadcasted_iota(jnp.int32, (HG_ROWS, 1), 0)
        ri = lax.broadcasted_iota(jnp.int32, (HG_ROWS, HG_ROWS), 0)
        ci = lax.broadcasted_iota(jnp.int32, (HG_ROWS, HG_ROWS), 1)
        _pair_masks(mask_ref)

        def block(bi, st):
            bb = (n_blocks - 1 - bi) if reverse else bi
            r0 = pl.multiple_of(bb * HG_ROWS, HG_ROWS)
            v16 = i_ref[pl.ds(r0, HG_ROWS), :].astype(BF16)
            qh, _, _, _, g, kk = _hg_gates(q_ref[pl.ds(r0, HG_ROWS), :],
                                           z_ref[pl.ds(r0, HG_ROWS), :], lbv)
            b = _block_cumsum(g, pos, reverse)
            bl = b[last:last + 1, :]
            qe = (qh * jnp.exp(b)).astype(BF16)
            kd = (kk * jnp.exp(bl - b)).astype(BF16)
            a = jnp.where(ri == ci, jnp.sum(qh * kk, axis=1, keepdims=True), 0.0)
            for i, (eq, ek) in enumerate(_pair_levels(b, pos, reverse)):
                a = a + mask_ref[i] * _nt((qh * eq).astype(BF16), (kk * ek).astype(BF16))
            st_ref[bb] = st
            o_ref[pl.ds(r0, HG_ROWS), :] = _nn(a.astype(BF16), v16) + _nt(qe, st.astype(BF16))
            return jnp.exp(bl) * st + _tn(v16, kd)

        lax.fori_loop(0, n_blocks, block, jnp.zeros((HG_DK, HG_DK), F32))

    cspec = lambda col: pl.BlockSpec((lp, HG_DK), lambda h: (0, col // HG_DK + h))
    return pl.pallas_call(
        body, grid=(hh,),
        in_specs=[cspec(col_q), cspec(col_z), cspec(col_i),
                  pl.BlockSpec((None, 1, HG_DK), lambda h: (h, 0, 0))],
        out_specs=[pl.BlockSpec((lp, HG_DK), lambda h: (0, h)),
                   pl.BlockSpec((None, n_blocks, HG_DK, HG_DK), lambda h: (h, 0, 0, 0))],
        out_shape=[jax.ShapeDtypeStruct((lp, hh * HG_DK), F32),
                   jax.ShapeDtypeStruct((hh, n_blocks, HG_DK, HG_DK), F32)],
        scratch_shapes=[pltpu.VMEM((len(HG_HALVES), HG_ROWS, HG_ROWS), F32)],
        compiler_params=_cp("parallel"), name=name)(proj, proj, proj, lb)


def _hg_scan_bwd(proj, lb, states, do, *, reverse, col_q, col_z, col_i, hh, name):
    lp = proj.shape[0]
    n_blocks = lp // HG_ROWS
    last = 0 if reverse else HG_ROWS - 1

    def body(q_ref, z_ref, i_ref, lb_ref, st_ref, do_ref, dq_ref, dz_ref, dv_ref, dlb_ref, mask_ref):
        lbv = lb_ref[...]
        pos = lax.broadcasted_iota(jnp.int32, (HG_ROWS, 1), 0)
        ri = lax.broadcasted_iota(jnp.int32, (HG_ROWS, HG_ROWS), 0)
        ci = lax.broadcasted_iota(jnp.int32, (HG_ROWS, HG_ROWS), 1)
        _pair_masks(mask_ref)

        def block(bi, carry):
            dst, dlb = carry
            bb = bi if reverse else (n_blocks - 1 - bi)
            r0 = pl.multiple_of(bb * HG_ROWS, HG_ROWS)
            zq = q_ref[pl.ds(r0, HG_ROWS), :]
            v16 = i_ref[pl.ds(r0, HG_ROWS), :].astype(BF16)
            do16 = do_ref[pl.ds(r0, HG_ROWS), :].astype(BF16)
            qh, sq, s, f, g, kk = _hg_gates(zq, z_ref[pl.ds(r0, HG_ROWS), :], lbv)
            b = _block_cumsum(g, pos, reverse)
            bl = b[last:last + 1, :]
            eb = jnp.exp(b)
            ebl = jnp.exp(bl - b)
            decay = jnp.exp(bl)
            qe16 = (qh * eb).astype(BF16)
            kd16 = (kk * ebl).astype(BF16)
            st = st_ref[bb]
            st16, dst16 = st.astype(BF16), dst.astype(BF16)
            same_row = ri == ci
            da = _nt(do16, v16)
            da_diag = jnp.sum(jnp.where(same_row, da, 0.0), axis=1, keepdims=True)
            dq_state = eb * _nn(do16, st16)
            dk_state = ebl * _nn(v16, dst16)
            dq = dq_state + da_diag * kk
            dk = dk_state + da_diag * qh
            dbl = (decay * jnp.sum(st * dst, axis=0, keepdims=True)
                   + jnp.sum(kk * dk_state, axis=0, keepdims=True))
            db = qh * dq_state - kk * dk_state + jnp.where(pos == last, dbl, 0.0)
            a = jnp.where(same_row, jnp.sum(qh * kk, axis=1, keepdims=True), 0.0)
            for i, (eq, ek) in enumerate(_pair_levels(b, pos, reverse)):
                same = mask_ref[i]
                q16, k16 = (qh * eq).astype(BF16), (kk * ek).astype(BF16)
                a = a + same * _nt(q16, k16)
                da16 = (same * da).astype(BF16)
                gq, gk = _nn(da16, k16), _tn(da16, q16)
                dq = dq + eq * gq
                dk = dk + ek * gk
                db = db + (q16.astype(F32) * gq - k16.astype(F32) * gk)
            dg = _block_cumsum(db, pos, not reverse)
            df = dg / f - dk
            dq_ref[pl.ds(r0, HG_ROWS), :] = dq * (sq + zq * sq * (1.0 - sq))
            dz_ref[pl.ds(r0, HG_ROWS), :] = (df * (1.0 - lbv) * s * (1.0 - s)).astype(BF16)
            dv_ref[pl.ds(r0, HG_ROWS), :] = _nt(kd16, dst16) + _tn(a.astype(BF16), do16)
            return (decay * dst + _tn(do16, qe16),
                    dlb + jnp.sum(df * (1.0 - s), axis=0, keepdims=True))

        _, dlb = lax.fori_loop(0, n_blocks, block,
                               (jnp.zeros((HG_DK, HG_DK), F32), jnp.zeros((1, HG_DK), F32)))
        dlb_ref[...] = dlb

    cspec = lambda col: pl.BlockSpec((lp, HG_DK), lambda h: (0, col // HG_DK + h))
    ospec = pl.BlockSpec((lp, HG_DK), lambda h: (0, h))
    sds = jax.ShapeDtypeStruct((lp, hh * HG_DK), F32)
    return pl.pallas_call(
        body, grid=(hh,),
        in_specs=[cspec(col_q), cspec(col_z), cspec(col_i),
                  pl.BlockSpec((None, 1, HG_DK), lambda h: (h, 0, 0)),
                  pl.BlockSpec((None, n_blocks, HG_DK, HG_DK), lambda h: (h, 0, 0, 0)),
                  ospec],
        out_specs=[ospec, ospec, ospec, pl.BlockSpec((None, 1, HG_DK), lambda h: (h, 0, 0))],
        out_shape=[sds, jax.ShapeDtypeStruct((lp, hh * HG_DK), BF16), sds,
                   jax.ShapeDtypeStruct((hh, 1, HG_DK), F32)],
        scratch_shapes=[pltpu.VMEM((len(HG_HALVES), HG_ROWS, HG_ROWS), F32)],
        compiler_params=_cp("parallel"), name=name)(proj, proj, proj, lb, states, do)


NA_HB = LANES // NA_HEAD_DIM
NA_G = 4
NA_U = NA_G + NA_WIN_H
NA_QN = NA_G * GRID_W
NA_KN = NA_U * GRID_W


def _na_table_index(pattern, a, j):
    if pattern == 0:
        return j - a + NA_WIN_H - 1 if j < NA_WIN_H else None
    if pattern == 2:
        return j - a - 1 if j >= NA_U - NA_WIN_H else None
    return j - a + NA_WIN_H // 2 - 1 if a <= j < a + NA_WIN_H else None


def _na_step_rows(pattern, t, rows):
    if pattern == 0:
        r0, us = 0, 0
    elif pattern == 2:
        r0, us = rows - NA_G, rows - NA_U
    else:
        r0 = NA_G * t
        us = r0 - NA_WIN_H // 2
    q0, k0 = N_META + GRID_W * r0, N_META + GRID_W * us
    if pattern == 1:
        q0, k0 = pl.multiple_of(q0, 16), pl.multiple_of(k0, 16)
    return q0, k0


def _na_fill_bias(tb_ref, bias_ref):
    neg = jnp.full((GRID_W, GRID_W), -1e30, F32)
    for h in range(NA_HB):
        for pattern in range(3):
            for a in range(NA_G):
                for j in range(NA_U):
                    idx = _na_table_index(pattern, a, j)
                    bias_ref[h, pattern, a * GRID_W:(a + 1) * GRID_W, j * GRID_W:(j + 1) * GRID_W] = (
                        neg if idx is None else tb_ref[h, idx])


def _na_steps(rows, step, carry):
    n_steps = rows // NA_G
    carry = step(0, 0, carry)
    carry = lax.fori_loop(1, n_steps - 1, functools.partial(step, 1), carry)
    return step(2, n_steps - 1, carry)


def _na_head_lanes():
    lane = lax.broadcasted_iota(jnp.int32, (1, LANES), 1)
    return [lane // NA_HEAD_DIM == h for h in range(NA_HB)]


def _na_only(mask, x):
    return jnp.where(mask, x, jnp.zeros_like(x))


def _na_stack(heads, x):
    return jnp.concatenate([_na_only(mask, x) for mask in heads], axis=0)


def _na_unstack(heads, y):
    rows = y.shape[0] // NA_HB
    out = y[0:rows]
    for h in range(1, NA_HB):
        out = jnp.where(heads[h], y[h * rows:(h + 1) * rows], out)
    return out


def _na_fwd(proj, tb, *, n_tok, nh, name):
    lp = proj.shape[0]
    dh, hb = NA_HEAD_DIM, NA_HB
    naw = nh * dh
    rows = n_tok // GRID_W
    scale = dh ** -0.5

    def body(q_ref, k_ref, v_ref, tb_ref, o_ref, lse_ref, q16_ref, k16_ref, v16_ref, bias_ref):
        o_ref[...] = jnp.zeros_like(o_ref)
        lse_ref[...] = jnp.zeros_like(lse_ref)
        q16_ref[...] = q_ref[...].astype(BF16)
        k16_ref[...] = k_ref[...].astype(BF16)
        v16_ref[...] = v_ref[...].astype(BF16)
        _na_fill_bias(tb_ref, bias_ref)
        heads = _na_head_lanes()
        km = k16_ref[0:N_META, :]
        vm = v16_ref[0:N_META, :]
        qm = q16_ref[0:N_META, :]
        o_m = None
        for h in range(hb):
            s = _nt(_na_only(heads[h], qm), km) * scale
            m = jnp.max(s, axis=1, keepdims=True)
            p = jnp.exp(s - m)
            l = jnp.sum(p, axis=1, keepdims=True)
            o_h = _nn(p.astype(BF16), vm) / l
            o_m = o_h if o_m is None else jnp.where(heads[h], o_h, o_m)
            lse_ref[h, 0:N_META, :] = m + jnp.log(l)
        o_ref[0:N_META, :] = o_m

        def step(pattern, t, carry):
            q0, k0 = _na_step_rows(pattern, t, rows)
            k16 = k16_ref[pl.ds(k0, NA_KN), :]
            v16 = v16_ref[pl.ds(k0, NA_KN), :]
            q2 = _na_stack(heads, q16_ref[pl.ds(q0, NA_QN), :])
            s = _nt(q2, k16) * scale + bias_ref[:, pattern].reshape(hb * NA_QN, NA_KN)
            sm = _nt(q2, km) * scale
            m = jnp.maximum(jnp.max(s, axis=1, keepdims=True), jnp.max(sm, axis=1, keepdims=True))
            p = jnp.exp(s - m)
            pm = jnp.exp(sm - m)
            l = jnp.sum(p, axis=1, keepdims=True) + jnp.sum(pm, axis=1, keepdims=True)
            o2 = (_nn(p.astype(BF16), v16) + _nn(pm.astype(BF16), vm)) / l
            o_ref[pl.ds(q0, NA_QN), :] = _na_unstack(heads, o2)
            lse2 = m + jnp.log(l)
            for h in range(hb):
                lse_ref[h, pl.ds(q0, NA_QN), :] = lse2[h * NA_QN:(h + 1) * NA_QN]
            return carry

        _na_steps(rows, step, 0)

    cblk = lambda col: pl.BlockSpec((lp, LANES), lambda g: (0, col // LANES + g))
    return pl.pallas_call(
        body, grid=(nh // hb,),
        in_specs=[cblk(0), cblk(naw), cblk(2 * naw),
                  pl.BlockSpec((hb, 2 * NA_WIN_H - 1, GRID_W, GRID_W), lambda g: (g, 0, 0, 0))],
        out_specs=[cblk(0), pl.BlockSpec((hb, lp, 1), lambda g: (g, 0, 0))],
        out_shape=[jax.ShapeDtypeStruct((lp, naw), F32), jax.ShapeDtypeStruct((nh, lp, 1), F32)],
        scratch_shapes=[pltpu.VMEM((lp, LANES), BF16)] * 3 + [pltpu.VMEM((hb, 3, NA_QN, NA_KN), F32)],
        compiler_params=_cp("parallel"), name=name)(proj, proj, proj, tb)


def _na_bwd(proj, tb, o, lse, do, *, n_tok, nh, name):
    lp = proj.shape[0]
    dh, hb = NA_HEAD_DIM, NA_HB
    naw = nh * dh
    rows = n_tok // GRID_W
    scale = dh ** -0.5

    def body(q_ref, k_ref, v_ref, tb_ref, o_ref, lse_ref, do_ref, dq_ref, dk_ref, dv_ref, dtb_ref,
             q16_ref, k16_ref, v16_ref, bias_ref):
        dq_ref[...] = jnp.zeros_like(dq_ref)
        dk_ref[...] = jnp.zeros_like(dk_ref)
        dv_ref[...] = jnp.zeros_like(dv_ref)
        dtb_ref[...] = jnp.zeros_like(dtb_ref)
        q16_ref[...] = q_ref[...].astype(BF16)
        k16_ref[...] = k_ref[...].astype(BF16)
        v16_ref[...] = v_ref[...].astype(BF16)
        _na_fill_bias(tb_ref, bias_ref)
        heads = _na_head_lanes()
        km = k16_ref[0:N_META, :]
        vm = v16_ref[0:N_META, :]
        qm = q16_ref[0:N_META, :]
        dom = do_ref[0:N_META, :]
        prod = dom * o_ref[0:N_META, :]
        dq_m = None
        dkm0 = jnp.zeros((N_META, LANES), F32)
        dvm0 = jnp.zeros((N_META, LANES), F32)
        for h in range(hb):
            q_h = _na_only(heads[h], qm)
            do_h = _na_only(heads[h], dom).astype(BF16)
            p = jnp.exp(_nt(q_h, km) * scale - lse_ref[h, 0:N_META, :])
            delta = jnp.sum(_na_only(heads[h], prod), axis=1, keepdims=True)
            ds = (p * (_nt(do_h, vm) - delta)).astype(BF16)
            dq_h = _nn(ds, km) * scale
            dq_m = dq_h if dq_m is None else jnp.where(heads[h], dq_h, dq_m)
            dkm0 = dkm0 + _tn(ds, q_h) * scale
            dvm0 = dvm0 + _tn(p.astype(BF16), do_h)
        dq_ref[0:N_META, :] = dq_m

        def step(pattern, t, carry):
            dkm, dvm = carry
            q0, k0 = _na_step_rows(pattern, t, rows)
            k16 = k16_ref[pl.ds(k0, NA_KN), :]
            v16 = v16_ref[pl.ds(k0, NA_KN), :]
            q2 = _na_stack(heads, q16_ref[pl.ds(q0, NA_QN), :])
            do2 = _na_stack(heads, do_ref[pl.ds(q0, NA_QN), :])
            do16 = do2.astype(BF16)
            ov = o_ref[pl.ds(q0, NA_QN), :]
            delta = jnp.sum(do2 * jnp.concatenate([ov] * hb, axis=0), axis=1, keepdims=True)
            lse = jnp.concatenate([lse_ref[h, pl.ds(q0, NA_QN), :] for h in range(hb)], axis=0)
            p = jnp.exp(_nt(q2, k16) * scale + bias_ref[:, pattern].reshape(hb * NA_QN, NA_KN)
                        - lse)
            pm = jnp.exp(_nt(q2, km) * scale - lse)
            ds = p * (_nt(do16, v16) - delta)
            dsm = (pm * (_nt(do16, vm) - delta)).astype(BF16)
            ds16 = ds.astype(BF16)
            dq2 = (_nn(ds16, k16) + _nn(dsm, km)) * scale
            dq_ref[pl.ds(q0, NA_QN), :] = _na_unstack(heads, dq2)
            dk_ref[pl.ds(k0, NA_KN), :] += _tn(ds16, q2) * scale
            dv_ref[pl.ds(k0, NA_KN), :] += _tn(p.astype(BF16), do16)
            for h in range(hb):
                for a in range(NA_G):
                    for j in range(NA_U):
                        idx = _na_table_index(pattern, a, j)
                        if idx is not None:
                            r = h * NA_QN + a * GRID_W
                            dtb_ref[h, idx] += ds[r:r + GRID_W, j * GRID_W:(j + 1) * GRID_W]
            return dkm + _tn(dsm, q2) * scale, dvm + _tn(pm.astype(BF16), do16)

        dkm, dvm = _na_steps(rows, step, (dkm0, dvm0))
        dk_ref[0:N_META, :] += dkm
        dv_ref[0:N_META, :] += dvm

    cblk = lambda col: pl.BlockSpec((lp, LANES), lambda g: (0, col // LANES + g))
    tbs = pl.BlockSpec((hb, 2 * NA_WIN_H - 1, GRID_W, GRID_W), lambda g: (g, 0, 0, 0))
    sds = jax.ShapeDtypeStruct((lp, naw), F32)
    return pl.pallas_call(
        body, grid=(nh // hb,),
        in_specs=[cblk(0), cblk(naw), cblk(2 * naw), tbs, cblk(0),
                  pl.BlockSpec((hb, lp, 1), lambda g: (g, 0, 0)), cblk(0)],
        out_specs=[cblk(0), cblk(0), cblk(0), tbs],
        out_shape=[sds, sds, sds, jax.ShapeDtypeStruct(tb.shape, F32)],
        scratch_shapes=[pltpu.VMEM((lp, LANES), BF16)] * 3 + [pltpu.VMEM((hb, 3, NA_QN, NA_KN), F32)],
        compiler_params=_cp("parallel"), name=name)(proj, proj, proj, tb, o, lse, do)


def _rpb_onehot():
    c = np.arange(GRID_W)[:, None]
    w = np.arange(GRID_W)[None, :]
    cs = np.clip(c - NA_WIN_W // 2, 0, GRID_W - NA_WIN_W)
    in_win = (w >= cs) & (w < cs + NA_WIN_W)
    dc = np.clip(w - c, -(NA_WIN_W - 1), NA_WIN_W - 1) + NA_WIN_W - 1
    oh = np.zeros((LANES, GRID_W * GRID_W), np.float32)
    flat = np.arange(GRID_W * GRID_W).reshape(GRID_W, GRID_W)
    oh[dc[in_win], flat[in_win]] = 1.0
    neg = np.where(in_win, 0.0, -1e30).astype(np.float32).reshape(1, -1)
    return oh, neg


def _assemble_dproj(dq_na, dk_na, dv_na, dq_f, dq_b, dz_f, dz_b, dv_f, dv_b, dg, dgn, dgh, *, name):
    lp, naw = dq_na.shape
    hgw = dq_f.shape[1]
    d = dgn.shape[1]
    cols = 3 * naw + 5 * hgw + 2 * d
    tr = _row_tile(lp, 3 * naw * 4 + 4 * hgw * 4 + 3 * hgw * 2 + 2 * d * 2 + cols * 2)

    def body(nq_ref, nk_ref, nv_ref, qf_ref, qb_ref, zf_ref, zb_ref, vf_ref, vb_ref, g_ref, gn_ref,
             gh_ref, o_ref):
        o_ref[:, 0:naw] = nq_ref[...].astype(BF16)
        o_ref[:, naw:2 * naw] = nk_ref[...].astype(BF16)
        o_ref[:, 2 * naw:3 * naw] = nv_ref[...].astype(BF16)
        c0 = 3 * naw
        o_ref[:, c0:c0 + hgw] = (qf_ref[...] + qb_ref[...]).astype(BF16)
        o_ref[:, c0 + hgw:c0 + 2 * hgw] = zf_ref[...]
        o_ref[:, c0 + 2 * hgw:c0 + 3 * hgw] = zb_ref[...]
        o_ref[:, c0 + 3 * hgw:c0 + 4 * hgw] = (vf_ref[...] + vb_ref[...]).astype(BF16)
        o_ref[:, c0 + 4 * hgw:c0 + 5 * hgw] = g_ref[...]
        o_ref[:, c0 + 5 * hgw:c0 + 5 * hgw + d] = gn_ref[...]
        o_ref[:, c0 + 5 * hgw + d:] = gh_ref[...]

    hg, na = _rspec(tr, hgw), _rspec(tr, naw)
    return pl.pallas_call(
        body, grid=(lp // tr,),
        in_specs=[na, na, na, hg, hg, hg, hg, hg, hg, hg, _rspec(tr, d), _rspec(tr, d)],
        out_specs=_rspec(tr, cols),
        out_shape=jax.ShapeDtypeStruct((lp, cols), BF16),
        compiler_params=_cp("parallel"), name=name)(dq_na, dk_na, dv_na, dq_f, dq_b, dz_f, dz_b,
                                                    dv_f, dv_b, dg, dgn, dgh)


GROUP_STEPS = 8


def _group_tiles(rows, align):
    steps = GROUP_STEPS if all(r % (GROUP_STEPS * align) == 0 for r in rows) else 1
    return steps, [r // steps for r in rows]


def _adamw(ws, gs, ms, vs, *, name):
    n = len(ws)
    steps, trs = _group_tiles([w.shape[0] for w in ws], 8)

    def body(*refs):
        for i in range(n):
            w_ref, g_ref, m_ref, v_ref = refs[4 * i:4 * i + 4]
            d_ref, mo_ref, vo_ref, go_ref = refs[4 * n + 4 * i:4 * n + 4 * i + 4]
            gv = g_ref[...]
            go_ref[...] = gv
            mn = ADAM_B1 * m_ref[...] + (1.0 - ADAM_B1) * gv
            vn = ADAM_B2 * v_ref[...] + (1.0 - ADAM_B2) * (gv * gv)
            m_hat = mn / (1.0 - ADAM_B1 ** ADAM_STEP)
            v_hat = vn / (1.0 - ADAM_B2 ** ADAM_STEP)
            d_ref[...] = -ADAM_LR * (m_hat / (jnp.sqrt(v_hat) + ADAM_EPS) + ADAM_WD * w_ref[...])
            mo_ref[...] = mn
            vo_ref[...] = vn

    specs = [_rspec(tr, w.shape[1]) for tr, w in zip(trs, ws)]
    out = pl.pallas_call(
        body, grid=(steps,),
        in_specs=[s for s in specs for _ in range(4)],
        out_specs=[s for s in specs for _ in range(4)],
        out_shape=[jax.ShapeDtypeStruct(w.shape, F32) for w in ws for _ in range(4)],
        compiler_params=_cp("parallel"), name=name)(*[a for q in zip(ws, gs, ms, vs) for a in q])
    return [tuple(out[4 * i:4 * i + 4]) for i in range(n)]


def _local_step(x, tgt, meta, first_weight, rest_weights, g_mix, g_mlp, g_fin, hg_gain, rpb, lb,
                early_grads=None, mid_grads=None, late_grad=None, rest_landed=None,
                last_grads=None):
    n_tok, d = x.shape
    hgw = hg_gain.shape[1]
    nh, hh = rpb.shape[0], hgw // HG_DK
    naw = nh * NA_HEAD_DIM
    l_real = N_META + n_tok
    lp = -(-l_real // ROW_ALIGN) * ROW_ALIGN
    col_qhg = 3 * naw
    col_zf, col_zb, col_i, col_g = (col_qhg + hgw, col_qhg + 2 * hgw, col_qhg + 3 * hgw,
                                    col_qhg + 4 * hgw)
    col_gate = col_qhg + 5 * hgw

    oh_np, neg_np = _rpb_onehot()
    oh = jnp.asarray(oh_np)
    rpb_p = jnp.pad(rpb.reshape(nh * (2 * NA_WIN_H - 1), 2 * NA_WIN_W - 1),
                    ((0, 0), (0, LANES - (2 * NA_WIN_W - 1))))
    tb = _matmul(rpb_p, oh, tm=rpb_p.shape[0], tn=512, tk=LANES, precision=HIGHEST,
                 name="rpb_expand")
    tb = (tb + jnp.asarray(neg_np)).reshape(nh, 2 * NA_WIN_H - 1, GRID_W, GRID_W)

    h0, a, tgt_p = _embed_norm(x, tgt, meta, g_mix, lp=lp, name="norm_mix")
    w_in = first_weight((a, tb))
    proj = _matmul(a, w_in, tm=lp, name="mm_in")
    o_na, lse = _na_fwd(proj, tb, n_tok=n_tok, nh=nh, name="na_fwd")
    lb_f = lb[0].reshape(hh, 1, HG_DK)
    lb_b = lb[1].reshape(hh, 1, HG_DK)
    scan_kw = dict(col_q=col_qhg, col_i=col_i, hh=hh)
    o_f, st_f = _hg_scan_fwd(proj, lb_f, reverse=False, col_z=col_zf, name="hg_scan_f", **scan_kw)
    token = rest_landed(o_f) if rest_landed else None
    lb_b_late = lb_b if token is None else lb_b + token[0:1, 0:1]
    o_b, st_b = _hg_scan_fwd(proj, lb_b_late, reverse=True, col_z=col_zb, name="hg_scan_b",
                             **scan_kw)
    o_hg = _hg_out(o_f, o_b, proj, hg_gain, col_g=col_g, name="hg_out")
    w_na, w_hg, w_o, w_up, w_down = rest_weights(o_hg)
    y_na = _matmul(o_na, w_na, name="mm_na_out", out_dtype=BF16)
    gates = ((proj, col_gate), (proj, col_gate + d))

    def mix_gates(acc, gn, gh, yn):
        return acc, _sigmoid(gn) * yn + _sigmoid(gh) * acc

    def mix_gates_bwd(dmix, gn, gh, yn, yh):
        sn, sh = _sigmoid(gn), _sigmoid(gh)
        return dmix * sn, dmix * sh, dmix * yn * sn * (1.0 - sn), dmix * yh * sh * (1.0 - sh)

    y_hg, mix = _matmul(o_hg, w_hg, name="mm_hg_out", epilogue=mix_gates,
                        tiles=(*gates, (y_na, 0)), out_dtypes=(BF16, BF16))
    t1 = _matmul(mix, w_o, name="mm_o")
    h1, mlp_in = _residual_norm(h0, t1, g_mlp, name="resid_norm_mlp")
    u, act = _matmul(mlp_in, w_up, name="mm_up", out_dtypes=(BF16, BF16),
                     epilogue=lambda acc: (acc, jnp.square(jnp.maximum(acc, 0.0))))
    t2 = _matmul(act, w_down, name="mm_down")
    dh2, dh2_16, loss, dg_fin = _final_loss(h1, t2, g_fin, tgt_p, n_tok=n_tok, name="final_loss")

    (du,) = _matmul(dh2_16, w_down, tb=True, name="mm_down_dx", tiles=((u, 0),),
                    out_dtypes=(BF16,),
                    epilogue=lambda acc, uv: (acc * 2.0 * jnp.maximum(uv, 0.0),))
    dw_down = _matmul(act, dh2_16, ta=True, name="mm_down_dw")
    dm = _matmul(du, w_up, tb=True, name="mm_up_dx")
    dw_up = _matmul(mlp_in, du, ta=True, name="mm_up_dw")
    dh1, dh1_16, dg_mlp = _rmsnorm_bwd_add(h1, g_mlp, dm, dh2, name="norm_mlp_bwd")
    dy_na, dy_hg, dgn, dgh = _matmul(dh1_16, w_o, tb=True, name="mm_o_dx", epilogue=mix_gates_bwd,
                                     tiles=(*gates, (y_na, 0), (y_hg, 0)), out_dtypes=(BF16,) * 4)
    dw_o = _matmul(mix, dh1_16, ta=True, name="mm_o_dw")
    do_na = _matmul(dy_na, w_na, tb=True, name="mm_na_out_dx")
    dw_na = _matmul(o_na, dy_na, ta=True, name="mm_na_out_dw")
    do_hg = _matmul(dy_hg, w_hg, tb=True, name="mm_hg_out_dx")
    dw_hg = _matmul(o_hg, dy_hg, ta=True, name="mm_hg_out_dw")
    token = early_grads([dw_na, dw_hg, dw_o, dw_up, dw_down]) if early_grads else None
    if token is not None:
        hg_gain = hg_gain + token[0:1, 0:1]
    d_o, dg_hg, d_gain = _hg_out_bwd(o_f, o_b, proj, hg_gain, do_hg, col_g=col_g, name="hg_out_bwd")
    dq_f, dz_f, dv_f, dlb_f = _hg_scan_bwd(proj, lb_f, st_f, d_o, reverse=False, col_z=col_zf,
                                           name="hg_scan_f_bwd", **scan_kw)
    token = mid_grads(dq_f) if mid_grads else None
    lb_b_late = lb_b if token is None else lb_b + token[0:1, 0:1]
    dq_b, dz_b, dv_b, dlb_b = _hg_scan_bwd(proj, lb_b_late, st_b, d_o, reverse=True, col_z=col_zb,
                                           name="hg_scan_b_bwd", **scan_kw)
    dq_na, dk_na, dv_na, dtb = _na_bwd(proj, tb, o_na, lse, do_na, n_tok=n_tok, nh=nh, name="na_bwd")
    dproj = _assemble_dproj(dq_na, dk_na, dv_na, dq_f, dq_b, dz_f, dz_b, dv_f, dv_b, dg_hg, dgn,
                            dgh, name="assemble_dproj")
    dw_in = _matmul(a, dproj, ta=True, name="mm_in_dw")
    token = late_grad(dw_in) if late_grad else None
    da = _matmul(dproj, w_in, tb=True, name="mm_in_dx", after=token)
    token = last_grads(da) if last_grads else None
    g_mix_late = g_mix if token is None else g_mix + token[0:1, 0:1]
    dx, dmeta, dg_mix = _rmsnorm_bwd_tokens(h0, g_mix_late, da, dh1, n_tok=n_tok,
                                            name="norm_mix_bwd")
    d_rpb = _matmul(dtb.reshape(nh * (2 * NA_WIN_H - 1), GRID_W * GRID_W), oh, tb=True,
                    tm=nh * (2 * NA_WIN_H - 1), tn=LANES, tk=1024, precision=HIGHEST,
                    name="rpb_reduce")
    d_lb = jnp.concatenate([dlb_f.reshape(1, hgw), dlb_b.reshape(1, hgw)], axis=0)
    return (loss, dx, dmeta, dw_in, dw_na, dw_hg, dw_o, dw_up, dw_down,
            dg_mix, dg_mlp, dg_fin, d_gain, d_rpb, d_lb)


N_CHIPS = 4
N_DEV = 8
ANY = pl.BlockSpec(memory_space=pl.ANY)


def _place():
    x, y, c = lax.axis_index("x"), lax.axis_index("y"), lax.axis_index("c")
    others = []
    for j in (1, 2, 3):
        tx = (1 - x) if (j >> 1) else x
        ty = (1 - y) if (j & 1) else y
        others.append((tx, ty))
    return x, y, c, others


def _piece(ref, axis, k, half, rh, cs):
    if axis == 1:
        return ref.at[pl.ds(pl.multiple_of(half * rh, 16), rh), pl.ds(pl.multiple_of(k * cs, LANES), cs)]
    return ref.at[pl.ds(pl.multiple_of(k * 2 * rh + half * rh, 16), rh), :]


def _cast_into_full(shards, axes, place, *, name):
    n = len(shards)
    steps, trs = _group_tiles([s.shape[0] for s in shards], 16)

    def body(p_ref, *refs):
        for i in range(n):
            refs[n + i][...] = refs[i][...].astype(BF16)

    def out_spec(tr, cs, axis):
        if axis == 1:
            return pl.BlockSpec((tr, cs), lambda i, p_ref: (i, p_ref[0]))
        return pl.BlockSpec((tr, cs), lambda i, p_ref: (p_ref[0] * steps + i, 0))

    return pl.pallas_call(
        body,
        grid_spec=pltpu.PrefetchScalarGridSpec(
            num_scalar_prefetch=1, grid=(steps,),
            in_specs=[pl.BlockSpec((tr, s.shape[1]), lambda i, p_ref: (i, 0))
                      for tr, s in zip(trs, shards)],
            out_specs=[out_spec(tr, s.shape[1], ax) for tr, s, ax in zip(trs, shards, axes)]),
        out_shape=[jax.ShapeDtypeStruct((s.shape[0], s.shape[1] * N_CHIPS) if ax == 1
                                        else (s.shape[0] * N_CHIPS, s.shape[1]), BF16)
                   for s, ax in zip(shards, axes)],
        compiler_params=_cp("parallel"), name=name)(place, *shards)


HBM_SPEC = pl.BlockSpec(memory_space=pltpu.HBM)
SEM_SPEC = pl.BlockSpec(memory_space=pltpu.SEMAPHORE)
SPLIT_COPY = pltpu.CompilerParams(has_side_effects=pltpu.SideEffectType.DATAFLOW_SIDE_EFFECTING)
TOKEN = jax.ShapeDtypeStruct((8, LANES), F32)


def _geo(fulls, axes):
    out = []
    for f, ax in zip(fulls, axes):
        r, cs = (f.shape[0], f.shape[1] // N_CHIPS) if ax == 1 else (f.shape[0] // N_CHIPS, f.shape[1])
        out.append((ax, r // 2, cs))
    return out


def _gather_copies(refs, geo, send_sems, recv_sems):
    x, y, c, others = _place()
    chip = 2 * x + y
    cps = []
    for i, (ax, rh, cs) in enumerate(geo):
        mine = _piece(refs[i], ax, chip, c, rh, cs)
        for j, (tx, ty) in enumerate(others):
            cps.append(pltpu.make_async_remote_copy(
                src_ref=mine, dst_ref=mine, send_sem=send_sems.at[3 * i + j],
                recv_sem=recv_sems.at[3 * i + j], device_id=(tx, ty, c), device_id_type=MESH))
    return cps


def _allgather_start(fulls, axes, after, *, name):
    n = len(fulls)
    geo = _geo(fulls, axes)

    def body(*refs):
        w_refs = refs[:n]
        send_sems, recv_sems = refs[n + 1], refs[n + 2]
        token = refs[2 * n + 3]
        for cp in _gather_copies(w_refs, geo, send_sems, recv_sems):
            cp.start()
        token[...] = jnp.zeros_like(token)

    out = pl.pallas_call(
        body, name=name,
        out_shape=(pltpu.SemaphoreType.DMA((3 * n,)), pltpu.SemaphoreType.DMA((3 * n,)),
                   *[pltpu.HBM(f.shape, f.dtype) for f in fulls], TOKEN),
        in_specs=[HBM_SPEC] * n + [ANY],
        out_specs=(SEM_SPEC, SEM_SPEC, *[HBM_SPEC] * n, pl.BlockSpec(memory_space=pltpu.VMEM)),
        input_output_aliases={i: 2 + i for i in range(n)},
        compiler_params=SPLIT_COPY,
    )(*[pltpu.with_memory_space_constraint(f, pltpu.HBM) for f in fulls], after)
    return out[0], out[1], list(out[2:2 + n]), out[2 + n]


def _allgather_wait(send_sems, recv_sems, fulls, axes, after, *, name):
    n = len(fulls)
    geo = _geo(fulls, axes)
    afters = tuple(after) if isinstance(after, (tuple, list)) else (after,)

    def body(*refs):
        w_refs = refs[:n]
        for cp in _gather_copies(w_refs, geo, refs[n], refs[n + 1]):
            cp.wait_send()
            cp.wait_recv()

    return list(pl.pallas_call(
        body, name=name,
        out_shape=[pltpu.HBM(f.shape, f.dtype) for f in fulls],
        in_specs=[HBM_SPEC] * n + [SEM_SPEC, SEM_SPEC] + [ANY] * len(afters),
        out_specs=[HBM_SPEC] * n,
        input_output_aliases={i: i for i in range(n)},
        compiler_params=SPLIT_COPY,
    )(*fulls, send_sems, recv_sems, *afters))


def _allgather_forward(fulls, axes, *, name):
    n = len(fulls)
    geo = _geo(fulls, axes)

    def body(*refs):
        o_refs = refs[n:2 * n]
        send_sems, recv_sems = refs[2 * n:]
        x, y, c, others = _place()

        def rcopy(i, j, half, to):
            ax, rh, cs = geo[i]
            ref = _piece(o_refs[i], ax, 2 * others[j][0] + others[j][1], half, rh, cs)
            return pltpu.make_async_remote_copy(
                src_ref=ref, dst_ref=ref, send_sem=send_sems.at[3 * i + j],
                recv_sem=recv_sems.at[3 * i + j], device_id=to, device_id_type=MESH)

        cps = [rcopy(i, j, c, (x, y, 1 - c)) for i in range(n) for j in range(3)]
        for cp in cps:
            cp.start()
        for i in range(n):
            for j in range(3):
                rcopy(i, j, 1 - c, (x, y, c)).wait_recv()
        for cp in cps:
            cp.wait_send()

    return list(pl.pallas_call(
        body, in_specs=[ANY] * n, out_specs=[ANY] * n,
        out_shape=[jax.ShapeDtypeStruct(f.shape, f.dtype) for f in fulls],
        input_output_aliases={i: i for i in range(n)},
        scratch_shapes=[pltpu.SemaphoreType.DMA((3 * n,)), pltpu.SemaphoreType.DMA((3 * n,))],
        name=name)(*fulls))


def _forward_copies(refs, geo, send_sems, recv_sems):
    x, y, c, others = _place()
    cps = []
    for i, (ax, rh, cs) in enumerate(geo):
        for j, (tx, ty) in enumerate(others):
            ref = _piece(refs[i], ax, 2 * tx + ty, c, rh, cs)
            cps.append(pltpu.make_async_remote_copy(
                src_ref=ref, dst_ref=ref, send_sem=send_sems.at[3 * i + j],
                recv_sem=recv_sems.at[3 * i + j], device_id=(x, y, 1 - c), device_id_type=MESH))
    return cps


def _allgather_forward_start(fulls, axes, *, name):
    n = len(fulls)
    geo = _geo(fulls, axes)

    def body(*refs):
        token = refs[2 * n + 2]
        for cp in _forward_copies(refs[:n], geo, refs[n], refs[n + 1]):
            cp.start()
        token[...] = jnp.zeros_like(token)

    out = pl.pallas_call(
        body, name=name,
        out_shape=(pltpu.SemaphoreType.DMA((3 * n,)), pltpu.SemaphoreType.DMA((3 * n,)),
                   *[pltpu.HBM(f.shape, f.dtype) for f in fulls], TOKEN),
        in_specs=[HBM_SPEC] * n,
        out_specs=(SEM_SPEC, SEM_SPEC, *[HBM_SPEC] * n, pl.BlockSpec(memory_space=pltpu.VMEM)),
        input_output_aliases={i: 2 + i for i in range(n)},
        compiler_params=SPLIT_COPY,
    )(*fulls)
    return out[0], out[1], list(out[2:2 + n]), out[2 + n]


def _allgather_forward_wait(send_sems, recv_sems, fulls, axes, after, *, name):
    n = len(fulls)
    geo = _geo(fulls, axes)

    def body(*refs):
        for cp in _forward_copies(refs[:n], geo, refs[n], refs[n + 1]):
            cp.wait_send()
            cp.wait_recv()

    return list(pl.pallas_call(
        body, name=name,
        out_shape=[pltpu.HBM(f.shape, f.dtype) for f in fulls],
        in_specs=[HBM_SPEC] * n + [SEM_SPEC, SEM_SPEC, ANY],
        out_specs=[HBM_SPEC] * n,
        input_output_aliases={i: i for i in range(n)},
        compiler_params=SPLIT_COPY,
    )(*fulls, send_sems, recv_sems, after))


def _chip_copies(blk_ref, land_ref, send_sems, recv_sems):
    x, y, c, others = _place()
    return [pltpu.make_async_remote_copy(
        src_ref=blk_ref, dst_ref=land_ref.at[2 * x + y], send_sem=send_sems.at[j],
        recv_sem=recv_sems.at[j], device_id=(tx, ty, c), device_id_type=MESH)
        for j, (tx, ty) in enumerate(others)]


def _chip_exchange_start(blk, *, name):
    land = pltpu.with_memory_space_constraint(lax.empty((N_CHIPS, *blk.shape), blk.dtype), pltpu.HBM)

    def body(blk_ref, land_ref, send_sems, recv_sems, blk_out, land_out, token):
        for cp in _chip_copies(blk_ref, land_ref, send_sems, recv_sems):
            cp.start()
        token[...] = jnp.zeros_like(token)

    return pl.pallas_call(
        body, name=name,
        out_shape=(pltpu.SemaphoreType.DMA((3,)), pltpu.SemaphoreType.DMA((3,)),
                   pltpu.HBM(blk.shape, blk.dtype), pltpu.HBM(land.shape, land.dtype), TOKEN),
        in_specs=[HBM_SPEC] * 2,
        out_specs=(SEM_SPEC, SEM_SPEC, HBM_SPEC, HBM_SPEC, pl.BlockSpec(memory_space=pltpu.VMEM)),
        input_output_aliases={0: 2, 1: 3},
        compiler_params=SPLIT_COPY,
    )(pltpu.with_memory_space_constraint(blk, pltpu.HBM), land)


def _chip_exchange_wait(send_sems, recv_sems, blk, land, after, *, name):
    def body(blk_ref, land_ref, send_sems, recv_sems, after_ref, blk_out, land_out):
        for cp in _chip_copies(blk_ref, land_ref, send_sems, recv_sems):
            cp.wait_send()
            cp.wait_recv()

    return pl.pallas_call(
        body, name=name,
        out_shape=[pltpu.HBM(blk.shape, blk.dtype), pltpu.HBM(land.shape, land.dtype)],
        in_specs=[HBM_SPEC] * 2 + [SEM_SPEC, SEM_SPEC, ANY],
        out_specs=[HBM_SPEC] * 2,
        input_output_aliases={0: 0, 1: 1},
        compiler_params=SPLIT_COPY,
    )(blk, land, send_sems, recv_sems, after)[1]


def _scatter_geo(parts, axes):
    out = []
    for p, ax in zip(parts, axes):
        _, rh, cols = p.shape
        out.append((ax, rh, cols // N_CHIPS if ax == 1 else cols))
    return out


def _scatter_copies(p_refs, q_refs, geo, send_sems, recv_sems):
    x, y, c, others = _place()
    chip = 2 * x + y
    cps = []
    for i, (ax, rh, cw) in enumerate(geo):
        for j, (tx, ty) in enumerate(others):
            k = 2 * tx + ty
            src = (p_refs[i].at[0, :, pl.ds(pl.multiple_of(k * cw, LANES), cw)] if ax == 1
                   else p_refs[i].at[k])
            cps.append(pltpu.make_async_remote_copy(
                src_ref=src, dst_ref=q_refs[i].at[chip], send_sem=send_sems.at[3 * i + j],
                recv_sem=recv_sems.at[3 * i + j], device_id=(tx, ty, c), device_id_type=MESH))
    return cps


def _scatter_start(parts, axes, *, name):
    n = len(parts)
    geo = _scatter_geo(parts, axes)
    slots = [pltpu.HBM((N_CHIPS, rh, cw), p.dtype) for p, (_, rh, cw) in zip(parts, geo)]

    def body(*refs):
        p_refs, q_refs = refs[:n], refs[n:2 * n]
        send_sems, recv_sems = refs[2 * n], refs[2 * n + 1]
        token = refs[4 * n + 2]
        for cp in _scatter_copies(p_refs, q_refs, geo, send_sems, recv_sems):
            cp.start()
        token[...] = jnp.zeros_like(token)

    land = [pltpu.with_memory_space_constraint(lax.empty(s.inner_aval.shape, s.inner_aval.dtype), pltpu.HBM)
            for s in slots]
    out = pl.pallas_call(
        body, name=name,
        out_shape=(pltpu.SemaphoreType.DMA((3 * n,)), pltpu.SemaphoreType.DMA((3 * n,)),
                   *[pltpu.HBM(p.shape, p.dtype) for p in parts], *slots, TOKEN),
        in_specs=[HBM_SPEC] * (2 * n),
        out_specs=(SEM_SPEC, SEM_SPEC, *[HBM_SPEC] * (2 * n), pl.BlockSpec(memory_space=pltpu.VMEM)),
        input_output_aliases={i: 2 + i for i in range(2 * n)},
        compiler_params=SPLIT_COPY,
    )(*[pltpu.with_memory_space_constraint(p, pltpu.HBM) for p in parts], *land)
    return out[0], out[1], list(out[2:2 + n]), list(out[2 + n:2 + 2 * n]), out[2 + 2 * n]


def _scatter_wait(send_sems, recv_sems, parts, slots, axes, after, *, name):
    n = len(parts)
    geo = _scatter_geo(parts, axes)

    def body(*refs):
        p_refs, q_refs = refs[:n], refs[n:2 * n]
        for cp in _scatter_copies(p_refs, q_refs, geo, refs[2 * n], refs[2 * n + 1]):
            cp.wait_send()
            cp.wait_recv()

    out = pl.pallas_call(
        body, name=name,
        out_shape=[pltpu.HBM(a.shape, a.dtype) for a in (*parts, *slots)],
        in_specs=[HBM_SPEC] * (2 * n) + [SEM_SPEC, SEM_SPEC, ANY],
        out_specs=[HBM_SPEC] * (2 * n),
        input_output_aliases={i: i for i in range(2 * n)},
        compiler_params=SPLIT_COPY,
    )(*parts, *slots, send_sems, recv_sems, after)
    return list(out[:n]), list(out[n:])


def _swap_copies(g_refs, r_refs, shapes, send_sems, recv_sems):
    x, y, c, _ = _place()
    cps = []
    for i, shape in enumerate(shapes):
        rh = shape[1] // 2
        src = g_refs[i].at[:, pl.ds(pl.multiple_of((1 - c) * rh, 16), rh), :]
        cps.append(pltpu.make_async_remote_copy(
            src_ref=src, dst_ref=r_refs[i], send_sem=send_sems.at[i], recv_sem=recv_sems.at[i],
            device_id=(x, y, 1 - c), device_id_type=MESH))
    return cps


def _sibling_swap_start(grads, *, name):
    n = len(grads)
    shapes = [g.shape for g in grads]
    lands = [pltpu.HBM((s[0], s[1] // 2, s[2]), g.dtype) for s, g in zip(shapes, grads)]

    def body(*refs):
        g_refs, r_refs = refs[:n], refs[n:2 * n]
        token = refs[4 * n + 2]
        for cp in _swap_copies(g_refs, r_refs, shapes, refs[2 * n], refs[2 * n + 1]):
            cp.start()
        token[...] = jnp.zeros_like(token)

    land = [pltpu.with_memory_space_constraint(lax.empty(s.inner_aval.shape, s.inner_aval.dtype), pltpu.HBM)
            for s in lands]
    out = pl.pallas_call(
        body, name=name,
        out_shape=(pltpu.SemaphoreType.DMA((n,)), pltpu.SemaphoreType.DMA((n,)),
                   *[pltpu.HBM(g.shape, g.dtype) for g in grads], *lands, TOKEN),
        in_specs=[HBM_SPEC] * (2 * n),
        out_specs=(SEM_SPEC, SEM_SPEC, *[HBM_SPEC] * (2 * n), pl.BlockSpec(memory_space=pltpu.VMEM)),
        input_output_aliases={i: 2 + i for i in range(2 * n)},
        compiler_params=SPLIT_COPY,
    )(*[pltpu.with_memory_space_constraint(g, pltpu.HBM) for g in grads], *land)
    return out[0], out[1], list(out[2:2 + n]), list(out[2 + n:2 + 2 * n]), out[2 + 2 * n]


def _sibling_swap_wait(send_sems, recv_sems, grads, lands, after, *, name):
    n = len(grads)
    shapes = [g.shape for g in grads]

    def body(*refs):
        g_refs, r_refs = refs[:n], refs[n:2 * n]
        for cp in _swap_copies(g_refs, r_refs, shapes, refs[2 * n], refs[2 * n + 1]):
            cp.wait_send()
            cp.wait_recv()

    out = pl.pallas_call(
        body, name=name,
        out_shape=[pltpu.HBM(a.shape, a.dtype) for a in (*grads, *lands)],
        in_specs=[HBM_SPEC] * (2 * n) + [SEM_SPEC, SEM_SPEC, ANY],
        out_specs=[HBM_SPEC] * (2 * n),
        input_output_aliases={i: i for i in range(2 * n)},
        compiler_params=SPLIT_COPY,
    )(*grads, *lands, send_sems, recv_sems, after)
    return list(out[:n]), list(out[n:])


def _pair_add(g3s, rxs, place, *, out_dtype, name):
    n = len(g3s)
    steps, trs = _group_tiles([g.shape[1] // 2 for g in g3s], 16)

    def body(p_ref, *refs):
        for i in range(n):
            refs[2 * n + i][...] = (refs[2 * i][...] + refs[2 * i + 1][...]).astype(out_dtype)

    in_specs, out_specs = [], []
    for g, tr in zip(g3s, trs):
        blk = (g.shape[0], tr, g.shape[2])
        in_specs += [pl.BlockSpec(blk, lambda i, p_ref: (0, p_ref[1] * steps + i, 0)),
                     pl.BlockSpec(blk, lambda i, p_ref: (0, i, 0))]
        out_specs.append(pl.BlockSpec(blk, lambda i, p_ref: (0, i, 0)))
    return pl.pallas_call(
        body,
        grid_spec=pltpu.PrefetchScalarGridSpec(
            num_scalar_prefetch=1, grid=(steps,), in_specs=in_specs, out_specs=out_specs),
        out_shape=[jax.ShapeDtypeStruct((g.shape[0], g.shape[1] // 2, g.shape[2]), out_dtype)
                   for g in g3s],
        compiler_params=_cp("parallel"), name=name)(place, *[a for q in zip(g3s, rxs) for a in q])


def _sum_slots(q, *, name):
    ns, rows, cols = q.shape
    tr = next(t for t in (128, 64, 32, 16, 8) if rows % t == 0)

    def body(q_ref, o_ref):
        acc = q_ref[0].astype(F32)
        for k in range(1, ns):
            acc = acc + q_ref[k].astype(F32)
        o_ref[...] = acc

    return pl.pallas_call(
        body, grid=(rows // tr,),
        in_specs=[pl.BlockSpec((ns, tr, cols), lambda i: (0, i, 0))],
        out_specs=_rspec(tr, cols),
        out_shape=jax.ShapeDtypeStruct((rows, cols), F32),
        compiler_params=_cp("parallel"), name=name)(q)


def _sum_chips(qs, ps, place, axes, *, name):
    n = len(qs)
    per = N_CHIPS + 1
    steps, trs = _group_tiles([q.shape[1] for q in qs], 16)

    def body(p_ref, *refs):
        chip = p_ref[0]
        for i in range(n):
            q_refs, own_ref = refs[per * i:per * i + N_CHIPS], refs[per * i + N_CHIPS]
            acc = jnp.where(chip == 0, own_ref[...], q_refs[0][...]).astype(F32)
            for k in range(1, N_CHIPS):
                acc = acc + jnp.where(chip == k, own_ref[...], q_refs[k][...]).astype(F32)
            refs[per * n + i][...] = acc

    def slot_spec(k, tr, cw):
        return pl.BlockSpec((None, tr, cw),
                            lambda i, p_ref: (jnp.where(p_ref[0] == k, (k + 1) % N_CHIPS, k), i, 0))

    in_specs, out_specs, operands = [], [], []
    for q, p, ax, tr in zip(qs, ps, axes, trs):
        cw = q.shape[2]
        in_specs += [slot_spec(k, tr, cw) for k in range(N_CHIPS)]
        in_specs.append(pl.BlockSpec((None, tr, cw), (lambda i, p_ref: (0, i, p_ref[0])) if ax == 1
                                     else (lambda i, p_ref: (p_ref[0], i, 0))))
        out_specs.append(pl.BlockSpec((tr, cw), lambda i, p_ref: (p_ref[1] * steps + i, 0)))
        operands += [q] * N_CHIPS + [p]
    return pl.pallas_call(
        body,
        grid_spec=pltpu.PrefetchScalarGridSpec(
            num_scalar_prefetch=1, grid=(steps,), in_specs=in_specs, out_specs=out_specs),
        out_shape=[jax.ShapeDtypeStruct((2 * q.shape[1], q.shape[2]), F32) for q in qs],
        compiler_params=_cp("parallel"), name=name)(place, *operands)


def _sibling_share(shards, *, name):
    n = len(shards)

    def body(*refs):
        o_refs = refs[n:2 * n]
        send_sems, recv_sems = refs[2 * n:]
        x, y, c, _ = _place()
        cps = []
        for i in range(n):
            rh = shards[i].shape[0] // 2
            mine = o_refs[i].at[pl.ds(pl.multiple_of(c * rh, 8), rh), :]
            cp = pltpu.make_async_remote_copy(
                src_ref=mine, dst_ref=mine, send_sem=send_sems.at[i], recv_sem=recv_sems.at[i],
                device_id=(x, y, 1 - c), device_id_type=MESH)
            cp.start()
            cps.append(cp)
        for i in range(n):
            rh = shards[i].shape[0] // 2
            theirs = o_refs[i].at[pl.ds(pl.multiple_of((1 - c) * rh, 8), rh), :]
            pltpu.make_async_remote_copy(
                src_ref=theirs, dst_ref=theirs, send_sem=send_sems.at[i], recv_sem=recv_sems.at[i],
                device_id=(x, y, c), device_id_type=MESH).wait_recv()
        for cp in cps:
            cp.wait_send()

    return pl.pallas_call(
        body, in_specs=[ANY] * n, out_specs=[ANY] * n,
        out_shape=[jax.ShapeDtypeStruct(h.shape, h.dtype) for h in shards],
        input_output_aliases={i: i for i in range(n)},
        scratch_shapes=[pltpu.SemaphoreType.DMA((n,)), pltpu.SemaphoreType.DMA((n,))],
        name=name)(*shards)


def _gather_all(blk, *, name, after=None, shares=()):
    rows, cols = blk.shape
    extra = [] if after is None else [after]
    n_s = len(shares)

    def body(x_ref, *refs):
        s_refs = refs[len(extra) + n_s + 1:len(extra) + 2 * n_s + 1]
        out_ref = refs[len(extra) + n_s]
        send_sems, recv_sems, local_sem, s_send, s_recv = refs[len(extra) + 2 * n_s + 1:]
        x, y, c = lax.axis_index("x"), lax.axis_index("y"), lax.axis_index("c")
        me = 4 * x + 2 * y + c
        mine = pltpu.make_async_copy(x_ref, out_ref.at[me], local_sem)
        mine.start()
        cps = []
        for i in range(n_s):
            rh = shares[i].shape[0] // 2
            half = s_refs[i].at[pl.ds(pl.multiple_of(c * rh, 8), rh), :]
            cp = pltpu.make_async_remote_copy(
                src_ref=half, dst_ref=half, send_sem=s_send.at[i], recv_sem=s_recv.at[i],
                device_id=(x, y, 1 - c), device_id_type=MESH)
            cp.start()
            cps.append(cp)
        for k in range(1, N_DEV):
            tx = (1 - x) if (k >> 2) & 1 else x
            ty = (1 - y) if (k >> 1) & 1 else y
            tc = (1 - c) if k & 1 else c
            cp = pltpu.make_async_remote_copy(
                src_ref=x_ref, dst_ref=out_ref.at[me], send_sem=send_sems.at[k - 1],
                recv_sem=recv_sems.at[k - 1], device_id=(tx, ty, tc), device_id_type=MESH)
            cp.start()
            cps.append(cp)
        for k in range(1, N_DEV):
            tx = (1 - x) if (k >> 2) & 1 else x
            ty = (1 - y) if (k >> 1) & 1 else y
            tc = (1 - c) if k & 1 else c
            got = out_ref.at[4 * tx + 2 * ty + tc]
            pltpu.make_async_remote_copy(
                src_ref=got, dst_ref=got, send_sem=send_sems.at[k - 1], recv_sem=recv_sems.at[k - 1],
                device_id=(x, y, c), device_id_type=MESH).wait_recv()
        for i in range(n_s):
            rh = shares[i].shape[0] // 2
            theirs = s_refs[i].at[pl.ds(pl.multiple_of((1 - c) * rh, 8), rh), :]
            pltpu.make_async_remote_copy(
                src_ref=theirs, dst_ref=theirs, send_sem=s_send.at[i], recv_sem=s_recv.at[i],
                device_id=(x, y, c), device_id_type=MESH).wait_recv()
        for cp in cps:
            cp.wait_send()
        mine.wait()

    vm = pl.BlockSpec(memory_space=pltpu.VMEM)
    out = pl.pallas_call(
        body, in_specs=[vm] + [ANY] * (len(extra) + n_s), out_specs=[vm] + [ANY] * n_s,
        out_shape=[jax.ShapeDtypeStruct((N_DEV, rows, cols), blk.dtype)]
        + [jax.ShapeDtypeStruct(s.shape, s.dtype) for s in shares],
        input_output_aliases={1 + len(extra) + i: 1 + i for i in range(n_s)},
        scratch_shapes=[pltpu.SemaphoreType.DMA((N_DEV - 1,)), pltpu.SemaphoreType.DMA((N_DEV - 1,)),
                        pltpu.SemaphoreType.DMA, pltpu.SemaphoreType.DMA((max(n_s, 1),)),
                        pltpu.SemaphoreType.DMA((max(n_s, 1),))],
        name=name)(blk, *extra, *shares)
    return (out[0], *out[1:]) if n_s else out[0]


def _as_rows(a):
    flat = a.reshape(-1)
    n = flat.shape[0]
    rows = -(-n // (8 * LANES)) * 8
    return jnp.pad(flat, (0, rows * LANES - n)).reshape(rows, LANES)


def _from_rows(p, shape):
    n = int(np.prod(shape))
    return p.reshape(-1)[:n].reshape(shape)


WEIGHT_AXES = (1, 1, 1, 0, 1, 0)
WIRE = BF16


def kernel(x, meta_tokens, w_in, w_na_out, w_hg_out, w_o, w_up, w_down, norm_mix, norm_mlp, norm_final, hg_norm, na_rpb, hg_lb_logits, loss_target, m_meta_tokens, m_w_in, m_w_na_out, m_w_hg_out, m_w_o, m_w_up, m_w_down, m_norm_mix, m_norm_mlp, m_norm_final, m_hg_norm, m_na_rpb, m_hg_lb_logits, v_meta_tokens, v_w_in, v_w_na_out, v_w_hg_out, v_w_o, v_w_up, v_w_down, v_norm_mix, v_norm_mlp, v_norm_final, v_hg_norm, v_na_rpb, v_hg_lb_logits):
    xi, yi, ci = lax.axis_index("x"), lax.axis_index("y"), lax.axis_index("c")
    chip = 2 * xi + yi
    d = x.shape[-1]
    dshard = meta_tokens.shape[1]
    hgw = hg_norm.shape[1]
    lbs = hg_lb_logits.shape[2]
    big = [w_in[0], w_na_out[0], w_hg_out[0], w_o[0], w_up[0], w_down[0]]
    big_m = [m_w_in[0], m_w_na_out[0], m_w_hg_out[0], m_w_o[0], m_w_up[0], m_w_down[0]]
    big_v = [v_w_in[0], v_w_na_out[0], v_w_hg_out[0], v_w_o[0], v_w_up[0], v_w_down[0]]

    place = jnp.stack([chip, ci]).astype(jnp.int32)
    in_axes, rest_axes = WEIGHT_AXES[:1], WEIGHT_AXES[1:]
    own_w = _cast_into_full(big, WEIGHT_AXES, place, name="cast_shards")
    small_in = jnp.concatenate([_as_rows(meta_tokens), _as_rows(hg_lb_logits)], axis=0)
    sm_send, sm_recv, sm_blk, sm_land, sm_token = _chip_exchange_start(small_in,
                                                                       name="small_params_start")
    in_send, in_recv, in_bufs, in_token = _allgather_start(own_w[:1], in_axes, sm_token,
                                                           name="weight_allgather_in_start")
    ag_send, ag_recv, ag_bufs, ag_token = _allgather_start(own_w[1:], rest_axes, in_token,
                                                           name="weight_allgather_rest_start")
    sm_land = _chip_exchange_wait(sm_send, sm_recv, sm_blk, sm_land, ag_token,
                                  name="small_params_wait")
    small_all = lax.dynamic_update_slice(sm_land, small_in[None], (chip, 0, 0))
    forward = {}

    def first_weight(after):
        got = _allgather_wait(in_send, in_recv, in_bufs, in_axes, after,
                              name="weight_allgather_in_wait")
        return _allgather_forward(got, in_axes, name="weight_allgather_in_forward")[0]

    def rest_landed(after):
        got = _allgather_wait(ag_send, ag_recv, ag_bufs, rest_axes, after,
                              name="weight_allgather_rest_wait")
        send, recv, bufs, token = _allgather_forward_start(
            got, rest_axes, name="weight_allgather_rest_forward_start")
        forward["rest"] = (send, recv, bufs)
        return token

    def rest_weights(after):
        return _allgather_forward_wait(*forward["rest"], rest_axes, after,
                                       name="weight_allgather_rest_forward_wait")

    n_meta_rows = N_META * dshard // LANES
    meta_full = (small_all[:, :n_meta_rows].reshape(N_CHIPS, N_META, dshard)
                 .transpose(1, 0, 2).reshape(N_META, d))
    lbl_full = (small_all[:, n_meta_rows:].reshape(N_CHIPS, -1)[:, :4 * lbs]
                .reshape(N_CHIPS, 2, 2, lbs).transpose(1, 2, 0, 3).reshape(2, 2, N_CHIPS * lbs))
    lb = jax.nn.softmax(lbl_full, axis=1)[:, 0]

    def by_chip(dws, axes):
        return [g.reshape(1, *g.shape) if ax == 1
                else g.reshape(N_CHIPS, g.shape[0] // N_CHIPS, g.shape[1]) for g, ax in zip(dws, axes)]

    flying = {}

    def scatter(tag, axes, g3, rx):
        parts = _pair_add(g3, rx, place, out_dtype=WIRE, name=f"grad_pair_add_{tag}")
        send, recv, parts, slots, token = _scatter_start(parts, axes,
                                                         name=f"grad_scatter_{tag}_start")
        flying[tag] = (send, recv, parts, slots)
        return token

    def swap(tag, axes):
        def start(dws):
            send, recv, g3, lands, token = _sibling_swap_start(
                by_chip(dws, axes), name=f"grad_sibling_swap_{tag}_start")
            flying["swap_" + tag] = (send, recv, g3, lands)
            return token

        def finish(after):
            g3, rx = _sibling_swap_wait(*flying["swap_" + tag], after,
                                        name=f"grad_sibling_swap_{tag}_wait")
            return scatter(tag, axes, g3, rx)
        return start, finish

    swap_rest, scatter_rest = swap("rest", rest_axes)
    swap_in, scatter_in = swap("in", in_axes)

    def landed(tag, axes, after):
        return _scatter_wait(*flying[tag], axes, after, name=f"grad_scatter_{tag}_wait")

    (loss, dx, dmeta, *_, dg_mix, dg_mlp, dg_fin, d_gain, d_rpb, d_lb) = _local_step(
        x[0], loss_target[0], meta_full, first_weight, rest_weights, norm_mix, norm_mlp,
        norm_final.reshape(1, d), hg_norm, na_rpb[0], lb, swap_rest, scatter_rest,
        lambda dw_in: swap_in([dw_in]), rest_landed, scatter_in)

    parts_rest, slots_rest = landed("rest", rest_axes, dx)
    g_rest = _sibling_share(_sum_chips(slots_rest, parts_rest, place, rest_axes,
                                       name="grad_sum_chips_rest"),
                            name="grad_sibling_share_rest")
    out_rest = _adamw(big[1:], g_rest, big_m[1:], big_v[1:], name="adamw_rest")
    parts_in, slots_in = landed("in", in_axes, out_rest[-1][0])
    half_in = _sum_chips(slots_in, parts_in, place, in_axes, name="grad_sum_chips_in")

    d_rpb_c = d_rpb[:, :2 * NA_WIN_W - 1]
    small_g = [dmeta, dg_mix, dg_mlp, dg_fin, d_gain, d_rpb_c, d_lb, loss]
    packed = jnp.concatenate([_as_rows(a) for a in small_g], axis=0)
    gathered, g_in = _gather_all(packed, shares=half_in, name="gather_small_grads")
    total = _sum_slots(gathered, name="sum_small_grads")
    offs = np.cumsum([0] + [_as_rows(a).shape[0] for a in small_g])
    take = lambda i, shape: _from_rows(total[offs[i]:offs[i + 1]], shape)
    g_meta_full = take(0, (N_META, d))
    g_norm_mix, g_norm_mlp = take(1, (1, d)), take(2, (1, d))
    g_norm_final = take(3, (d,))
    g_hg_norm = take(4, (1, hgw))
    g_rpb = take(5, na_rpb.shape)
    g_lb = take(6, (2, hgw))
    loss_total = take(7, (1, LANES))[0, 0]
    g_meta = lax.dynamic_slice_in_dim(g_meta_full, chip * dshard, dshard, axis=1)
    dl0 = lb * (1.0 - lb) * g_lb
    g_lbl_full = jnp.stack([dl0, -dl0], axis=1)
    g_lbl = lax.dynamic_slice_in_dim(g_lbl_full, chip * lbs, lbs, axis=2)

    big_out = _adamw(big[:1], [g_in], big_m[:1], big_v[:1], name="adamw_in") + out_rest
    small_w = [meta_tokens, norm_mix, norm_mlp, norm_final, hg_norm, na_rpb, hg_lb_logits]
    small_gr = [g_meta, g_norm_mix, g_norm_mlp, g_norm_final, g_hg_norm, g_rpb, g_lbl]
    small_m = [m_meta_tokens, m_norm_mix, m_norm_mlp, m_norm_final, m_hg_norm, m_na_rpb, m_hg_lb_logits]
    small_v = [v_meta_tokens, v_norm_mix, v_norm_mlp, v_norm_final, v_hg_norm, v_na_rpb, v_hg_lb_logits]
    pk = lambda lst: jnp.concatenate([_as_rows(a) for a in lst], axis=0)
    ((sd, sm, sv, _),) = _adamw([pk(small_w)], [pk(small_gr)], [pk(small_m)], [pk(small_v)],
                             name="adamw_small")
    soffs = np.cumsum([0] + [_as_rows(a).shape[0] for a in small_w])
    unpk = lambda p: [_from_rows(p[soffs[i]:soffs[i + 1]], small_w[i].shape) for i in range(len(small_w))]
    sd, sm, sv = unpk(sd), unpk(sm), unpk(sv)

    def order(bigs, smalls):
        return [smalls[0]] + [b.reshape(1, *b.shape) for b in bigs] + smalls[1:]

    grads = order([o[3] for o in big_out], small_gr)
    deltas = order([o[0] for o in big_out], sd)
    new_m = order([o[1] for o in big_out], sm)
    new_v = order([o[2] for o in big_out], sv)
    return (loss_total, dx.reshape(1, *dx.shape), *grads, *deltas, *new_m, *new_v)
```

```python
import functools

import numpy as np
import jax
import jax.numpy as jnp
from jax import lax
from jax.experimental import pallas as pl
from jax.experimental.pallas import tpu as pltpu

F32 = jnp.float32
BF16 = jnp.bfloat16
HIGHEST = lax.Precision.HIGHEST

GRID_W = 64
N_META = 16
EPS = 1e-6
NA_HEAD_DIM = 64
NA_WIN_H = 8
NA_WIN_W = 16
HG_DK = 128
LANES = 128
ROW_ALIGN = 128
VMEM_LIMIT = 48 * 1024 * 1024
ADAM_LR = 0.001
ADAM_B1 = 0.9
ADAM_B2 = 0.999
ADAM_EPS = 1e-08
ADAM_WD = 0.01
ADAM_STEP = 10

MESH = pl.DeviceIdType.MESH


def _cp(*sem):
    return pltpu.CompilerParams(dimension_semantics=sem, vmem_limit_bytes=VMEM_LIMIT)


def _sigmoid(x):
    return 0.5 * jnp.tanh(0.5 * x) + 0.5


def _dot(a, b, dims, precision=None):
    return lax.dot_general(a, b, (dims, ((), ())), preferred_element_type=F32, precision=precision)


def _nn(a, b, **kw):
    return _dot(a, b, ((1,), (0,)), **kw)


def _nt(a, b, **kw):
    return _dot(a, b, ((1,), (1,)), **kw)


def _tn(a, b, **kw):
    return _dot(a, b, ((0,), (0,)), **kw)


def _matmul(a, b, *, ta=False, tb=False, tm=None, tn=None, tk=None, out_dtype=F32, name,
            precision=None, after=None, epilogue=None, tiles=(), out_dtypes=None):
    extra = [] if after is None else [after]
    single = out_dtypes is None
    if single:
        out_dtypes = (out_dtype,)
    n_t, n_o = len(tiles), len(out_dtypes)
    if ta:
        kdim, m = a.shape
    else:
        m, kdim = a.shape
    if tb:
        n, k2 = b.shape
    else:
        k2, n = b.shape
    assert kdim == k2, (a.shape, b.shape, ta, tb)
    if tm is None:
        if ta:
            tm = next(t for t in (1024, 512, 256, 128, m) if m % t == 0)
        else:
            tm = m // 2 if (m // 2) % 16 == 0 and m > 512 else m
    if tn is None:
        wide = (1024,) if not ta and len(tiles) <= 1 else ()
        tn = next(t for t in (*wide, 512, 256, 128, n) if n % t == 0)
    if tk is None:
        tk = kdim if ta else next(t for t in (2048, 1024, 512, 256, 128, kdim) if kdim % t == 0)
    assert m % tm == 0 and n % tn == 0 and kdim % tk == 0, (m, n, kdim, tm, tn, tk)
    nk = kdim // tk
    op_dtype = F32 if precision is not None else BF16

    def body(a_ref, b_ref, *refs):
        t_refs = refs[:n_t]
        o_refs = refs[n_t + len(extra):n_t + len(extra) + n_o]
        av = a_ref[...].astype(op_dtype)
        bv = b_ref[...].astype(op_dtype)
        dims = ((0 if ta else 1,), (1 if tb else 0,))
        part = _dot(av, bv, dims, precision=precision)

        def finish(acc):
            outs = (acc,) if epilogue is None else epilogue(acc, *[t[...] for t in t_refs])
            for o_ref, val in zip(o_refs, outs):
                o_ref[...] = val.astype(o_ref.dtype)

        if nk == 1:
            finish(part)
            return
        acc_ref = refs[-1]
        kk = pl.program_id(2)

        @pl.when(kk == 0)
        def _():
            acc_ref[...] = part

        @pl.when((kk > 0) & (kk < nk - 1))
        def _():
            acc_ref[...] += part

        @pl.when(kk == nk - 1)
        def _():
            finish(acc_ref[...] + part)

    a_spec = (pl.BlockSpec((tk, tm), lambda i, j, k: (k, i)) if ta
              else pl.BlockSpec((tm, tk), lambda i, j, k: (i, k)))
    b_spec = (pl.BlockSpec((tn, tk), lambda i, j, k: (j, k)) if tb
              else pl.BlockSpec((tk, tn), lambda i, j, k: (k, j)))
    for _, off in tiles:
        assert off % tn == 0, (off, tn)
    t_specs = [pl.BlockSpec((tm, tn), functools.partial(lambda i, j, k, o: (i, o + j), o=off // tn))
               for _, off in tiles]
    o_spec = pl.BlockSpec((tm, tn), lambda i, j, k: (i, j))
    outs = pl.pallas_call(
        body,
        grid=(m // tm, n // tn, nk),
        in_specs=[a_spec, b_spec] + t_specs + [pl.BlockSpec(memory_space=pl.ANY)] * len(extra),
        out_specs=[o_spec] * n_o,
        out_shape=[jax.ShapeDtypeStruct((m, n), dt) for dt in out_dtypes],
        scratch_shapes=[pltpu.VMEM((tm, tn), F32)] if nk > 1 else [],
        compiler_params=_cp("parallel", "parallel", "arbitrary"),
        name=name,
    )(a, b, *[t for t, _ in tiles], *extra)
    return outs[0] if single else outs


def _rspec(tr, w, cb=0):
    return pl.BlockSpec((tr, w), lambda i: (i, cb))


def _fspec(shape):
    nd = len(shape)
    return pl.BlockSpec(shape, lambda i: (0,) * nd)


ROW_VMEM_BUDGET = 20 * 1024 * 1024
ROW_MIN_STEPS = 4


def _row_tile(lp, row_bytes):
    for k in range(ROW_MIN_STEPS, lp // 16 + 1):
        tr = lp // k
        if lp % k == 0 and tr % 16 == 0 and 2 * tr * row_bytes <= ROW_VMEM_BUDGET:
            return tr
    return lp


def _token_rows_copy(i, n_tiles, tr, n_tok, tok_ref, buf_ref, sem, *, to_tokens, start=True,
                     wait=True):
    assert n_tiles >= 2 and 0 < n_tok + N_META - (n_tiles - 1) * tr <= tr

    def run(tok_row, buf_row, count):
        tok = tok_ref.at[pl.ds(tok_row, count), :]
        buf = buf_ref.at[pl.ds(buf_row, count), :]
        cp = pltpu.make_async_copy(buf, tok, sem) if to_tokens else pltpu.make_async_copy(tok, buf, sem)
        if start:
            cp.start()
        if wait:
            cp.wait()

    @pl.when(i == 0)
    def _():
        run(0, N_META, tr - N_META)

    if n_tiles > 2:
        @pl.when((i > 0) & (i < n_tiles - 1))
        def _():
            run(pl.multiple_of(i * tr - N_META, 8), 0, tr)

    @pl.when(i == n_tiles - 1)
    def _():
        run((n_tiles - 1) * tr - N_META, 0, n_tok + N_META - (n_tiles - 1) * tr)


def _embed_norm(x, tgt, meta, g, *, lp, name):
    n_tok, d = x.shape
    tr = _row_tile(lp, d * (4 + 2 + 4))
    n_tiles = lp // tr

    def body(x_ref, tgt_ref, meta_ref, g_ref, h_ref, o_ref, tp_ref, buf_ref, tbuf_ref, sems):
        i = pl.program_id(0)
        buf_ref[...] = jnp.zeros_like(buf_ref)
        tbuf_ref[...] = jnp.zeros_like(tbuf_ref)

        @pl.when(i == 0)
        def _():
            buf_ref[0:N_META, :] = meta_ref[...]

        _token_rows_copy(i, n_tiles, tr, n_tok, tgt_ref, tbuf_ref, sems.at[1], to_tokens=False,
                         wait=False)
        _token_rows_copy(i, n_tiles, tr, n_tok, x_ref, buf_ref, sems.at[0], to_tokens=False)
        xv = buf_ref[...]
        h_ref[...] = xv
        r = lax.rsqrt(jnp.mean(xv * xv, axis=-1, keepdims=True) + EPS)
        o_ref[...] = (xv * r * g_ref[...]).astype(BF16)
        _token_rows_copy(i, n_tiles, tr, n_tok, tgt_ref, tbuf_ref, sems.at[1], to_tokens=False,
                         start=False)
        tp_ref[...] = tbuf_ref[...]

    return pl.pallas_call(
        body, grid=(n_tiles,),
        in_specs=[ANY, ANY, _fspec((N_META, d)), _fspec((1, d))],
        out_specs=[_rspec(tr, d), _rspec(tr, d), _rspec(tr, d)],
        out_shape=[jax.ShapeDtypeStruct((lp, d), F32), jax.ShapeDtypeStruct((lp, d), BF16),
                   jax.ShapeDtypeStruct((lp, d), F32)],
        scratch_shapes=[pltpu.VMEM((tr, d), F32), pltpu.VMEM((tr, d), F32),
                        pltpu.SemaphoreType.DMA((2,))],
        compiler_params=_cp("parallel"), name=name)(x, tgt, meta, g)


def _residual_norm(h, t, g, *, name):
    lp, d = h.shape
    tr = _row_tile(lp, d * (4 + 4 + 4 + 2))

    def body(h_ref, t_ref, g_ref, h1_ref, m_ref):
        xv = h_ref[...] + t_ref[...]
        h1_ref[...] = xv
        r = lax.rsqrt(jnp.mean(xv * xv, axis=-1, keepdims=True) + EPS)
        m_ref[...] = (xv * r * g_ref[...]).astype(BF16)

    return pl.pallas_call(
        body, grid=(lp // tr,),
        in_specs=[_rspec(tr, d), _rspec(tr, d), _fspec((1, d))],
        out_specs=[_rspec(tr, d), _rspec(tr, d)],
        out_shape=[jax.ShapeDtypeStruct((lp, d), F32), jax.ShapeDtypeStruct((lp, d), BF16)],
        compiler_params=_cp("parallel"), name=name)(h, t, g)


def _rmsnorm_bwd_add(x, g, dy, dres, *, name):
    lp, d = x.shape
    tr = _row_tile(lp, d * (4 * 4 + 2))

    def body(x_ref, g_ref, dy_ref, dr_ref, dx_ref, dx16_ref, dg_ref):
        @pl.when(pl.program_id(0) == 0)
        def _():
            dg_ref[...] = jnp.zeros_like(dg_ref)

        xv = x_ref[...]
        r = lax.rsqrt(jnp.mean(xv * xv, axis=-1, keepdims=True) + EPS)
        xh = xv * r
        dyv = dy_ref[...]
        dg_ref[...] += jnp.sum(dyv * xh, axis=0, keepdims=True)
        dxh = dyv * g_ref[...]
        dx = dr_ref[...] + r * (dxh - xh * jnp.mean(dxh * xh, axis=-1, keepdims=True))
        dx_ref[...] = dx
        dx16_ref[...] = dx.astype(BF16)

    return pl.pallas_call(
        body, grid=(lp // tr,),
        in_specs=[_rspec(tr, d), _fspec((1, d)), _rspec(tr, d), _rspec(tr, d)],
        out_specs=[_rspec(tr, d), _rspec(tr, d), _fspec((1, d))],
        out_shape=[jax.ShapeDtypeStruct((lp, d), F32), jax.ShapeDtypeStruct((lp, d), BF16),
                   jax.ShapeDtypeStruct((1, d), F32)],
        compiler_params=_cp("arbitrary"), name=name)(x, g, dy, dres)


def _rmsnorm_bwd_tokens(x, g, dy, dres, *, n_tok, name):
    lp, d = x.shape
    tr = _row_tile(lp, d * 4 * 4)
    n_tiles = lp // tr

    def body(x_ref, g_ref, dy_ref, dr_ref, dtok_ref, dmeta_ref, dg_ref, buf_ref, sem):
        i = pl.program_id(0)

        @pl.when(i == 0)
        def _():
            dg_ref[...] = jnp.zeros_like(dg_ref)

        xv = x_ref[...]
        r = lax.rsqrt(jnp.mean(xv * xv, axis=-1, keepdims=True) + EPS)
        xh = xv * r
        dyv = dy_ref[...]
        dg_ref[...] += jnp.sum(dyv * xh, axis=0, keepdims=True)
        dxh = dyv * g_ref[...]
        buf_ref[...] = dr_ref[...] + r * (dxh - xh * jnp.mean(dxh * xh, axis=-1, keepdims=True))

        @pl.when(i == 0)
        def _():
            dmeta_ref[...] = buf_ref[0:N_META, :]

        _token_rows_copy(i, n_tiles, tr, n_tok, dtok_ref, buf_ref, sem, to_tokens=True)

    return pl.pallas_call(
        body, grid=(n_tiles,),
        in_specs=[_rspec(tr, d), _fspec((1, d)), _rspec(tr, d), _rspec(tr, d)],
        out_specs=[ANY, _fspec((N_META, d)), _fspec((1, d))],
        out_shape=[jax.ShapeDtypeStruct((n_tok, d), F32), jax.ShapeDtypeStruct((N_META, d), F32),
                   jax.ShapeDtypeStruct((1, d), F32)],
        scratch_shapes=[pltpu.VMEM((tr, d), F32), pltpu.SemaphoreType.DMA],
        compiler_params=_cp("arbitrary"), name=name)(x, g, dy, dres)


def _final_loss(h1, t2, g, tgt, *, n_tok, name):
    lp, d = h1.shape
    tr = _row_tile(lp, d * (4 * 4 + 2))
    n_tiles = lp // tr

    def body(h_ref, t_ref, g_ref, tg_ref, dh_ref, dh16_ref, loss_ref, dg_ref):
        i = pl.program_id(0)

        @pl.when(i == 0)
        def _():
            loss_ref[...] = jnp.zeros_like(loss_ref)
            dg_ref[...] = jnp.zeros_like(dg_ref)

        xv = h_ref[...] + t_ref[...]
        r = lax.rsqrt(jnp.mean(xv * xv, axis=-1, keepdims=True) + EPS)
        xh = xv * r
        gv = g_ref[...]
        row = i * tr + lax.broadcasted_iota(jnp.int32, (tr, 1), 0)
        valid = (row >= N_META) & (row < N_META + n_tok)
        err = jnp.where(valid, xh * gv - tg_ref[...], 0.0)
        loss_ref[...] += jnp.sum(0.5 * err * err) / d
        dy = err / d
        dg_ref[...] += jnp.sum(dy * xh, axis=0, keepdims=True)
        dxh = dy * gv
        dh = r * (dxh - xh * jnp.mean(dxh * xh, axis=-1, keepdims=True))
        dh_ref[...] = dh
        dh16_ref[...] = dh.astype(BF16)

    return pl.pallas_call(
        body, grid=(n_tiles,),
        in_specs=[_rspec(tr, d), _rspec(tr, d), _fspec((1, d)), _rspec(tr, d)],
        out_specs=[_rspec(tr, d), _rspec(tr, d), _fspec((1, LANES)), _fspec((1, d))],
        out_shape=[jax.ShapeDtypeStruct((lp, d), F32), jax.ShapeDtypeStruct((lp, d), BF16),
                   jax.ShapeDtypeStruct((1, LANES), F32), jax.ShapeDtypeStruct((1, d), F32)],
        compiler_params=_cp("arbitrary"), name=name)(h1, t2, g, tgt)


def _hg_out(o_f, o_b, proj, gain, *, col_g, name):
    lp, w = o_f.shape
    tr = _row_tile(lp, w * (3 * 4 + 2))
    hh = w // HG_DK

    def body(of_ref, ob_ref, g_ref, gain_ref, y_ref):
        gv = g_ref[...]
        sg = gv * _sigmoid(gv)
        for h in range(hh):
            sl = slice(h * HG_DK, (h + 1) * HG_DK)
            o = of_ref[:, sl] + ob_ref[:, sl]
            r = lax.rsqrt(jnp.mean(o * o, axis=-1, keepdims=True) + EPS)
            y_ref[:, sl] = (o * r * gain_ref[:, sl] * sg[:, sl]).astype(BF16)

    return pl.pallas_call(
        body, grid=(lp // tr,),
        in_specs=[_rspec(tr, w), _rspec(tr, w), _rspec(tr, w, col_g // w), _fspec((1, w))],
        out_specs=_rspec(tr, w),
        out_shape=jax.ShapeDtypeStruct((lp, w), BF16),
        compiler_params=_cp("parallel"), name=name)(o_f, o_b, proj, gain)


def _hg_out_bwd(o_f, o_b, proj, gain, dy, *, col_g, name):
    lp, w = o_f.shape
    tr = _row_tile(lp, w * (5 * 4 + 2))
    hh = w // HG_DK

    def body(of_ref, ob_ref, g_ref, gain_ref, dy_ref, do_ref, dg_ref, dgain_ref):
        @pl.when(pl.program_id(0) == 0)
        def _():
            dgain_ref[...] = jnp.zeros_like(dgain_ref)

        for h in range(hh):
            sl = slice(h * HG_DK, (h + 1) * HG_DK)
            gv = g_ref[:, sl]
            s = _sigmoid(gv)
            sg = gv * s
            dsg = s + gv * s * (1.0 - s)
            o = of_ref[:, sl] + ob_ref[:, sl]
            r = lax.rsqrt(jnp.mean(o * o, axis=-1, keepdims=True) + EPS)
            on = o * r
            dyv = dy_ref[:, sl]
            gn = gain_ref[:, sl]
            dgain_ref[:, sl] += jnp.sum(dyv * on * sg, axis=0, keepdims=True)
            dg_ref[:, sl] = (dyv * on * gn * dsg).astype(BF16)
            don = dyv * gn * sg
            do_ref[:, sl] = (r * (don - on * jnp.mean(don * on, axis=-1, keepdims=True))
                             ).astype(BF16)

    return pl.pallas_call(
        body, grid=(lp // tr,),
        in_specs=[_rspec(tr, w), _rspec(tr, w), _rspec(tr, w, col_g // w), _fspec((1, w)),
                  _rspec(tr, w)],
        out_specs=[_rspec(tr, w), _rspec(tr, w), _fspec((1, w))],
        out_shape=[jax.ShapeDtypeStruct((lp, w), BF16), jax.ShapeDtypeStruct((lp, w), BF16),
                   jax.ShapeDtypeStruct((1, w), F32)],
        compiler_params=_cp("arbitrary"), name=name)(o_f, o_b, proj, gain, dy)


HG_ROWS = 128
HG_HALVES = (1, 2, 4, 8, 16, 32, 64)


def _hg_gates(zq, z, lbv):
    sq = _sigmoid(zq)
    s = _sigmoid(z)
    f = lbv + (1.0 - lbv) * s
    kk = (1.0 - lbv) * (1.0 - s)
    return zq * sq, sq, s, f, jnp.log(f), kk


def _block_cumsum(g, pos, suffix):
    x = g
    for k in HG_HALVES:
        if suffix:
            x = x + jnp.where(pos < HG_ROWS - k, pltpu.roll(x, HG_ROWS - k, 0), 0.0)
        else:
            x = x + jnp.where(pos >= k, pltpu.roll(x, k, 0), 0.0)
    return x


def _pair_levels(b, pos, reverse):
    out = []
    first = b
    for m in HG_HALVES:
        if m > 1:
            first = jnp.where((pos & (m - 1)) >= m // 2, pltpu.roll(first, m // 2, 0), first)
        nxt = pltpu.roll(first, HG_ROWS - m, 0)
        upper = (pos & (2 * m - 1)) >= m
        if reverse:
            eq = jnp.where(upper, 0.0, jnp.exp(b - nxt))
            ek = jnp.where(upper, jnp.exp(first - b), 0.0)
        else:
            eq = jnp.where(upper, jnp.exp(b - first), 0.0)
            ek = jnp.where(upper, 0.0, jnp.exp(nxt - b))
        out.append((eq, ek))
    return out


def _pair_masks(mask_ref):
    ri = lax.broadcasted_iota(jnp.int32, (HG_ROWS, HG_ROWS), 0)
    ci = lax.broadcasted_iota(jnp.int32, (HG_ROWS, HG_ROWS), 1)
    for i, m in enumerate(HG_HALVES):
        sh = m.bit_length()
        mask_ref[i] = jnp.where((ri >> sh) == (ci >> sh), 1.0, 0.0)


def _hg_scan_fwd(proj, lb, *, reverse, col_q, col_z, col_i, hh, name):
    lp = proj.shape[0]
    n_blocks = lp // HG_ROWS
    last = 0 if reverse else HG_ROWS - 1

    def body(q_ref, z_ref, i_ref, lb_ref, o_ref, st_ref, mask_ref):
        lbv = lb_ref[...]
        pos = lax.broadcasted_iota(jnp.int32, (HG_ROWS, 1), 0)
        ri = lax.broadcasted_iota(jnp.int32, (HG_ROWS, HG_ROWS), 0)
        ci = lax.broadcasted_iota(jnp.int32, (HG_ROWS, HG_ROWS), 1)
        _pair_masks(mask_ref)

        def block(bi, st):
            bb = (n_blocks - 1 - bi) if reverse else bi
            r0 = pl.multiple_of(bb * HG_ROWS, HG_ROWS)
            v16 = i_ref[pl.ds(r0, HG_ROWS), :].astype(BF16)
            qh, _, _, _, g, kk = _hg_gates(q_ref[pl.ds(r0, HG_ROWS), :],
                                           z_ref[pl.ds(r0, HG_ROWS), :], lbv)
            b = _block_cumsum(g, pos, reverse)
            bl = b[last:last + 1, :]
            qe = (qh * jnp.exp(b)).astype(BF16)
            kd = (kk * jnp.exp(bl - b)).astype(BF16)
            a = jnp.where(ri == ci, jnp.sum(qh * kk, axis=1, keepdims=True), 0.0)
            for i, (eq, ek) in enumerate(_pair_levels(b, pos, reverse)):
                a = a + mask_ref[i] * _nt((qh * eq).astype(BF16), (kk * ek).astype(BF16))
            st_ref[bb] = st
            o_ref[pl.ds(r0, HG_ROWS), :] = _nn(a.astype(BF16), v16) + _nt(qe, st.astype(BF16))
            return jnp.exp(bl) * st + _tn(v16, kd)

        lax.fori_loop(0, n_blocks, block, jnp.zeros((HG_DK, HG_DK), F32))

    cspec = lambda col: pl.BlockSpec((lp, HG_DK), lambda h: (0, col // HG_DK + h))
    return pl.pallas_call(
        body, grid=(hh,),
        in_specs=[cspec(col_q), cspec(col_z), cspec(col_i),
                  pl.BlockSpec((None, 1, HG_DK), lambda h: (h, 0, 0))],
        out_specs=[pl.BlockSpec((lp, HG_DK), lambda h: (0, h)),
                   pl.BlockSpec((None, n_blocks, HG_DK, HG_DK), lambda h: (h, 0, 0, 0))],
        out_shape=[jax.ShapeDtypeStruct((lp, hh * HG_DK), F32),
                   jax.ShapeDtypeStruct((hh, n_blocks, HG_DK, HG_DK), F32)],
        scratch_shapes=[pltpu.VMEM((len(HG_HALVES), HG_ROWS, HG_ROWS), F32)],
        compiler_params=_cp("parallel"), name=name)(proj, proj, proj, lb)


def _hg_scan_bwd(proj, lb, states, do, *, reverse, col_q, col_z, col_i, hh, name):
    lp = proj.shape[0]
    n_blocks = lp // HG_ROWS
    last = 0 if reverse else HG_ROWS - 1

    def body(q_ref, z_ref, i_ref, lb_ref, st_ref, do_ref, dq_ref, dz_ref, dv_ref, dlb_ref, mask_ref):
        lbv = lb_ref[...]
        pos = lax.broadcasted_iota(jnp.int32, (HG_ROWS, 1), 0)
        ri = lax.broadcasted_iota(jnp.int32, (HG_ROWS, HG_ROWS), 0)
        ci = lax.broadcasted_iota(jnp.int32, (HG_ROWS, HG_ROWS), 1)
        _pair_masks(mask_ref)

        def block(bi, carry):
            dst, dlb = carry
            bb = bi if reverse else (n_blocks - 1 - bi)
            r0 = pl.multiple_of(bb * HG_ROWS, HG_ROWS)
            zq = q_ref[pl.ds(r0, HG_ROWS), :]
            v16 = i_ref[pl.ds(r0, HG_ROWS), :].astype(BF16)
            do16 = do_ref[pl.ds(r0, HG_ROWS), :].astype(BF16)
            qh, sq, s, f, g, kk = _hg_gates(zq, z_ref[pl.ds(r0, HG_ROWS), :], lbv)
            b = _block_cumsum(g, pos, reverse)
            bl = b[last:last + 1, :]
            eb = jnp.exp(b)
            ebl = jnp.exp(bl - b)
            decay = jnp.exp(bl)
            qe16 = (qh * eb).astype(BF16)
            kd16 = (kk * ebl).astype(BF16)
            st = st_ref[bb]
            st16, dst16 = st.astype(BF16), dst.astype(BF16)
            same_row = ri == ci
            da = _nt(do16, v16)
            da_diag = jnp.sum(jnp.where(same_row, da, 0.0), axis=1, keepdims=True)
            dq_state = eb * _nn(do16, st16)
            dk_state = ebl * _nn(v16, dst16)
            dq = dq_state + da_diag * kk
            dk = dk_state + da_diag * qh
            dbl = (decay * jnp.sum(st * dst, axis=0, keepdims=True)
                   + jnp.sum(kk * dk_state, axis=0, keepdims=True))
            db = qh * dq_state - kk * dk_state + jnp.where(pos == last, dbl, 0.0)
            a = jnp.where(same_row, jnp.sum(qh * kk, axis=1, keepdims=True), 0.0)
            for i, (eq, ek) in enumerate(_pair_levels(b, pos, reverse)):
                same = mask_ref[i]
                q16, k16 = (qh * eq).astype(BF16), (kk * ek).astype(BF16)
                a = a + same * _nt(q16, k16)
                da16 = (same * da).astype(BF16)
                gq, gk = _nn(da16, k16), _tn(da16, q16)
                dq = dq + eq * gq
                dk = dk + ek * gk
                db = db + (q16.astype(F32) * gq - k16.astype(F32) * gk)
            dg = _block_cumsum(db, pos, not reverse)
            df = dg / f - dk
            dq_ref[pl.ds(r0, HG_ROWS), :] = (dq * (sq + zq * sq * (1.0 - sq))).astype(BF16)
            dz_ref[pl.ds(r0, HG_ROWS), :] = (df * (1.0 - lbv) * s * (1.0 - s)).astype(BF16)
            dv_ref[pl.ds(r0, HG_ROWS), :] = (_nt(kd16, dst16)
                                             + _tn(a.astype(BF16), do16)).astype(BF16)
            return (decay * dst + _tn(do16, qe16),
                    dlb + jnp.sum(df * (1.0 - s), axis=0, keepdims=True))

        _, dlb = lax.fori_loop(0, n_blocks, block,
                               (jnp.zeros((HG_DK, HG_DK), F32), jnp.zeros((1, HG_DK), F32)))
        dlb_ref[...] = dlb

    cspec = lambda col: pl.BlockSpec((lp, HG_DK), lambda h: (0, col // HG_DK + h))
    ospec = pl.BlockSpec((lp, HG_DK), lambda h: (0, h))
    sds = jax.ShapeDtypeStruct((lp, hh * HG_DK), BF16)
    return pl.pallas_call(
        body, grid=(hh,),
        in_specs=[cspec(col_q), cspec(col_z), cspec(col_i),
                  pl.BlockSpec((None, 1, HG_DK), lambda h: (h, 0, 0)),
                  pl.BlockSpec((None, n_blocks, HG_DK, HG_DK), lambda h: (h, 0, 0, 0)),
                  ospec],
        out_specs=[ospec, ospec, ospec, pl.BlockSpec((None, 1, HG_DK), lambda h: (h, 0, 0))],
        out_shape=[sds, sds, sds, jax.ShapeDtypeStruct((hh, 1, HG_DK), F32)],
        scratch_shapes=[pltpu.VMEM((len(HG_HALVES), HG_ROWS, HG_ROWS), F32)],
        compiler_params=_cp("parallel"), name=name)(proj, proj, proj, lb, states, do)


NA_HB = LANES // NA_HEAD_DIM
NA_G = 4
NA_U = NA_G + NA_WIN_H
NA_QN = NA_G * GRID_W
NA_KN = NA_U * GRID_W


def _na_table_index(pattern, a, j):
    if pattern == 0:
        return j - a + NA_WIN_H - 1 if j < NA_WIN_H else None
    if pattern == 2:
        return j - a - 1 if j >= NA_U - NA_WIN_H else None
    return j - a + NA_WIN_H // 2 - 1 if a <= j < a + NA_WIN_H else None


def _na_step_rows(pattern, t, rows):
    if pattern == 0:
        r0, us = 0, 0
    elif pattern == 2:
        r0, us = rows - NA_G, rows - NA_U
    else:
        r0 = NA_G * t
        us = r0 - NA_WIN_H // 2
    q0, k0 = N_META + GRID_W * r0, N_META + GRID_W * us
    if pattern == 1:
        q0, k0 = pl.multiple_of(q0, 16), pl.multiple_of(k0, 16)
    return q0, k0


def _na_fill_bias(tb_ref, bias_ref):
    neg = jnp.full((GRID_W, GRID_W), -1e30, F32)
    for h in range(NA_HB):
        for pattern in range(3):
            for a in range(NA_G):
                for j in range(NA_U):
                    idx = _na_table_index(pattern, a, j)
                    bias_ref[h, pattern, a * GRID_W:(a + 1) * GRID_W, j * GRID_W:(j + 1) * GRID_W] = (
                        neg if idx is None else tb_ref[h, idx])


def _na_steps(rows, step, carry):
    n_steps = rows // NA_G
    carry = step(0, 0, carry)
    carry = lax.fori_loop(1, n_steps - 1, functools.partial(step, 1), carry)
    return step(2, n_steps - 1, carry)


def _na_head_lanes():
    lane = lax.broadcasted_iota(jnp.int32, (1, LANES), 1)
    return [lane // NA_HEAD_DIM == h for h in range(NA_HB)]


def _na_only(mask, x):
    return jnp.where(mask, x, jnp.zeros_like(x))


def _na_stack(heads, x):
    return jnp.concatenate([_na_only(mask, x) for mask in heads], axis=0)


def _na_unstack(heads, y):
    rows = y.shape[0] // NA_HB
    out = y[0:rows]
    for h in range(1, NA_HB):
        out = jnp.where(heads[h], y[h * rows:(h + 1) * rows], out)
    return out


def _na_fwd(proj, tb, *, n_tok, nh, name):
    lp = proj.shape[0]
    dh, hb = NA_HEAD_DIM, NA_HB
    naw = nh * dh
    rows = n_tok // GRID_W
    scale = dh ** -0.5

    def body(q_ref, k_ref, v_ref, tb_ref, o_ref, lse_ref, q16_ref, k16_ref, v16_ref, bias_ref):
        o_ref[...] = jnp.zeros_like(o_ref)
        lse_ref[...] = jnp.zeros_like(lse_ref)
        q16_ref[...] = q_ref[...].astype(BF16)
        k16_ref[...] = k_ref[...].astype(BF16)
        v16_ref[...] = v_ref[...].astype(BF16)
        _na_fill_bias(tb_ref, bias_ref)
        heads = _na_head_lanes()
        km = k16_ref[0:N_META, :]
        vm = v16_ref[0:N_META, :]
        qm = q16_ref[0:N_META, :]
        o_m = None
        for h in range(hb):
            s = _nt(_na_only(heads[h], qm), km) * scale
            m = jnp.max(s, axis=1, keepdims=True)
            p = jnp.exp(s - m)
            l = jnp.sum(p, axis=1, keepdims=True)
            o_h = _nn(p.astype(BF16), vm) / l
            o_m = o_h if o_m is None else jnp.where(heads[h], o_h, o_m)
            lse_ref[h, 0:N_META, :] = m + jnp.log(l)
        o_ref[0:N_META, :] = o_m

        def step(pattern, t, carry):
            q0, k0 = _na_step_rows(pattern, t, rows)
            k16 = k16_ref[pl.ds(k0, NA_KN), :]
            v16 = v16_ref[pl.ds(k0, NA_KN), :]
            q2 = _na_stack(heads, q16_ref[pl.ds(q0, NA_QN), :])
            s = _nt(q2, k16) * scale + bias_ref[:, pattern].reshape(hb * NA_QN, NA_KN)
            sm = _nt(q2, km) * scale
            m = jnp.maximum(jnp.max(s, axis=1, keepdims=True), jnp.max(sm, axis=1, keepdims=True))
            p = jnp.exp(s - m)
            pm = jnp.exp(sm - m)
            l = jnp.sum(p, axis=1, keepdims=True) + jnp.sum(pm, axis=1, keepdims=True)
            o2 = (_nn(p.astype(BF16), v16) + _nn(pm.astype(BF16), vm)) / l
            o_ref[pl.ds(q0, NA_QN), :] = _na_unstack(heads, o2)
            lse2 = m + jnp.log(l)
            for h in range(hb):
                lse_ref[h, pl.ds(q0, NA_QN), :] = lse2[h * NA_QN:(h + 1) * NA_QN]
            return carry

        _na_steps(rows, step, 0)

    cblk = lambda col: pl.BlockSpec((lp, LANES), lambda g: (0, col // LANES + g))
    return pl.pallas_call(
        body, grid=(nh // hb,),
        in_specs=[cblk(0), cblk(naw), cblk(2 * naw),
                  pl.BlockSpec((hb, 2 * NA_WIN_H - 1, GRID_W, GRID_W), lambda g: (g, 0, 0, 0))],
        out_specs=[cblk(0), pl.BlockSpec((hb, lp, 1), lambda g: (g, 0, 0))],
        out_shape=[jax.ShapeDtypeStruct((lp, naw), F32), jax.ShapeDtypeStruct((nh, lp, 1), F32)],
        scratch_shapes=[pltpu.VMEM((lp, LANES), BF16)] * 3 + [pltpu.VMEM((hb, 3, NA_QN, NA_KN), F32)],
        compiler_params=_cp("parallel"), name=name)(proj, proj, proj, tb)


def _na_bwd(proj, tb, o, lse, do, *, n_tok, nh, name):
    lp = proj.shape[0]
    dh, hb = NA_HEAD_DIM, NA_HB
    naw = nh * dh
    rows = n_tok // GRID_W
    scale = dh ** -0.5

    def body(q_ref, k_ref, v_ref, tb_ref, o_ref, lse_ref, do_ref, dq_ref, dk_ref, dv_ref, dtb_ref,
             q16_ref, k16_ref, v16_ref, bias_ref):
        dq_ref[...] = jnp.zeros_like(dq_ref)
        dk_ref[...] = jnp.zeros_like(dk_ref)
        dv_ref[...] = jnp.zeros_like(dv_ref)
        dtb_ref[...] = jnp.zeros_like(dtb_ref)
        q16_ref[...] = q_ref[...].astype(BF16)
        k16_ref[...] = k_ref[...].astype(BF16)
        v16_ref[...] = v_ref[...].astype(BF16)
        _na_fill_bias(tb_ref, bias_ref)
        heads = _na_head_lanes()
        km = k16_ref[0:N_META, :]
        vm = v16_ref[0:N_META, :]
        qm = q16_ref[0:N_META, :]
        dom = do_ref[0:N_META, :]
        prod = dom * o_ref[0:N_META, :]
        dq_m = None
        dkm0 = jnp.zeros((N_META, LANES), F32)
        dvm0 = jnp.zeros((N_META, LANES), F32)
        for h in range(hb):
            q_h = _na_only(heads[h], qm)
            do_h = _na_only(heads[h], dom).astype(BF16)
            p = jnp.exp(_nt(q_h, km) * scale - lse_ref[h, 0:N_META, :])
            delta = jnp.sum(_na_only(heads[h], prod), axis=1, keepdims=True)
            ds = (p * (_nt(do_h, vm) - delta)).astype(BF16)
            dq_h = _nn(ds, km) * scale
            dq_m = dq_h if dq_m is None else jnp.where(heads[h], dq_h, dq_m)
            dkm0 = dkm0 + _tn(ds, q_h) * scale
            dvm0 = dvm0 + _tn(p.astype(BF16), do_h)
        dq_ref[0:N_META, :] = dq_m

        def step(pattern, t, carry):
            dkm, dvm = carry
            q0, k0 = _na_step_rows(pattern, t, rows)
            k16 = k16_ref[pl.ds(k0, NA_KN), :]
            v16 = v16_ref[pl.ds(k0, NA_KN), :]
            q2 = _na_stack(heads, q16_ref[pl.ds(q0, NA_QN), :])
            do2 = _na_stack(heads, do_ref[pl.ds(q0, NA_QN), :])
            do16 = do2.astype(BF16)
            ov = o_ref[pl.ds(q0, NA_QN), :]
            delta = jnp.sum(do2 * jnp.concatenate([ov] * hb, axis=0), axis=1, keepdims=True)
            lse = jnp.concatenate([lse_ref[h, pl.ds(q0, NA_QN), :] for h in range(hb)], axis=0)
            p = jnp.exp(_nt(q2, k16) * scale + bias_ref[:, pattern].reshape(hb * NA_QN, NA_KN)
                        - lse)
            pm = jnp.exp(_nt(q2, km) * scale - lse)
            ds = p * (_nt(do16, v16) - delta)
            dsm = (pm * (_nt(do16, vm) - delta)).astype(BF16)
            ds16 = ds.astype(BF16)
            dq2 = (_nn(ds16, k16) + _nn(dsm, km)) * scale
            dq_ref[pl.ds(q0, NA_QN), :] = _na_unstack(heads, dq2)
            dk_ref[pl.ds(k0, NA_KN), :] += _tn(ds16, q2) * scale
            dv_ref[pl.ds(k0, NA_KN), :] += _tn(p.astype(BF16), do16)
            for h in range(hb):
                for a in range(NA_G):
                    for j in range(NA_U):
                        idx = _na_table_index(pattern, a, j)
                        if idx is not None:
                            r = h * NA_QN + a * GRID_W
                            dtb_ref[h, idx] += ds[r:r + GRID_W, j * GRID_W:(j + 1) * GRID_W]
            return dkm + _tn(dsm, q2) * scale, dvm + _tn(pm.astype(BF16), do16)

        dkm, dvm = _na_steps(rows, step, (dkm0, dvm0))
        dk_ref[0:N_META, :] += dkm
        dv_ref[0:N_META, :] += dvm

    cblk = lambda col: pl.BlockSpec((lp, LANES), lambda g: (0, col // LANES + g))
    tbs = pl.BlockSpec((hb, 2 * NA_WIN_H - 1, GRID_W, GRID_W), lambda g: (g, 0, 0, 0))
    sds = jax.ShapeDtypeStruct((lp, naw), F32)
    return pl.pallas_call(
        body, grid=(nh // hb,),
        in_specs=[cblk(0), cblk(naw), cblk(2 * naw), tbs, cblk(0),
                  pl.BlockSpec((hb, lp, 1), lambda g: (g, 0, 0)), cblk(0)],
        out_specs=[cblk(0), cblk(0), cblk(0), tbs],
        out_shape=[sds, sds, sds, jax.ShapeDtypeStruct(tb.shape, F32)],
        scratch_shapes=[pltpu.VMEM((lp, LANES), BF16)] * 3 + [pltpu.VMEM((hb, 3, NA_QN, NA_KN), F32)],
        compiler_params=_cp("parallel"), name=name)(proj, proj, proj, tb, o, lse, do)


def _rpb_onehot():
    c = np.arange(GRID_W)[:, None]
    w = np.arange(GRID_W)[None, :]
    cs = np.clip(c - NA_WIN_W // 2, 0, GRID_W - NA_WIN_W)
    in_win = (w >= cs) & (w < cs + NA_WIN_W)
    dc = np.clip(w - c, -(NA_WIN_W - 1), NA_WIN_W - 1) + NA_WIN_W - 1
    oh = np.zeros((LANES, GRID_W * GRID_W), np.float32)
    flat = np.arange(GRID_W * GRID_W).reshape(GRID_W, GRID_W)
    oh[dc[in_win], flat[in_win]] = 1.0
    neg = np.where(in_win, 0.0, -1e30).astype(np.float32).reshape(1, -1)
    return oh, neg


def _assemble_dproj(dq_na, dk_na, dv_na, dq_f, dq_b, dz_f, dz_b, dv_f, dv_b, dg, dgn, dgh, *, name):
    lp, naw = dq_na.shape
    hgw = dq_f.shape[1]
    d = dgn.shape[1]
    cols = 3 * naw + 5 * hgw + 2 * d
    tr = _row_tile(lp, 3 * naw * 4 + 7 * hgw * 2 + 2 * d * 2 + cols * 2)

    def body(nq_ref, nk_ref, nv_ref, qf_ref, qb_ref, zf_ref, zb_ref, vf_ref, vb_ref, g_ref, gn_ref,
             gh_ref, o_ref):
        o_ref[:, 0:naw] = nq_ref[...].astype(BF16)
        o_ref[:, naw:2 * naw] = nk_ref[...].astype(BF16)
        o_ref[:, 2 * naw:3 * naw] = nv_ref[...].astype(BF16)
        c0 = 3 * naw
        o_ref[:, c0:c0 + hgw] = (qf_ref[...].astype(F32) + qb_ref[...].astype(F32)).astype(BF16)
        o_ref[:, c0 + hgw:c0 + 2 * hgw] = zf_ref[...]
        o_ref[:, c0 + 2 * hgw:c0 + 3 * hgw] = zb_ref[...]
        o_ref[:, c0 + 3 * hgw:c0 + 4 * hgw] = (vf_ref[...].astype(F32)
                                               + vb_ref[...].astype(F32)).astype(BF16)
        o_ref[:, c0 + 4 * hgw:c0 + 5 * hgw] = g_ref[...]
        o_ref[:, c0 + 5 * hgw:c0 + 5 * hgw + d] = gn_ref[...]
        o_ref[:, c0 + 5 * hgw + d:] = gh_ref[...]

    hg, na = _rspec(tr, hgw), _rspec(tr, naw)
    return pl.pallas_call(
        body, grid=(lp // tr,),
        in_specs=[na, na, na, hg, hg, hg, hg, hg, hg, hg, _rspec(tr, d), _rspec(tr, d)],
        out_specs=_rspec(tr, cols),
        out_shape=jax.ShapeDtypeStruct((lp, cols), BF16),
        compiler_params=_cp("parallel"), name=name)(dq_na, dk_na, dv_na, dq_f, dq_b, dz_f, dz_b,
                                                    dv_f, dv_b, dg, dgn, dgh)


GROUP_STEPS = 8


def _group_tiles(rows, align):
    steps = GROUP_STEPS if all(r % (GROUP_STEPS * align) == 0 for r in rows) else 1
    return steps, [r // steps for r in rows]


def _adamw(ws, gs, ms, vs, *, name):
    n = len(ws)
    steps, trs = _group_tiles([w.shape[0] for w in ws], 8)

    def body(*refs):
        for i in range(n):
            w_ref, g_ref, m_ref, v_ref = refs[4 * i:4 * i + 4]
            d_ref, mo_ref, vo_ref, go_ref = refs[4 * n + 4 * i:4 * n + 4 * i + 4]
            gv = g_ref[...]
            go_ref[...] = gv
            mn = ADAM_B1 * m_ref[...] + (1.0 - ADAM_B1) * gv
            vn = ADAM_B2 * v_ref[...] + (1.0 - ADAM_B2) * (gv * gv)
            m_hat = mn / (1.0 - ADAM_B1 ** ADAM_STEP)
            v_hat = vn / (1.0 - ADAM_B2 ** ADAM_STEP)
            d_ref[...] = -ADAM_LR * (m_hat / (jnp.sqrt(v_hat) + ADAM_EPS) + ADAM_WD * w_ref[...])
            mo_ref[...] = mn
            vo_ref[...] = vn

    specs = [_rspec(tr, w.shape[1]) for tr, w in zip(trs, ws)]
    out = pl.pallas_call(
        body, grid=(steps,),
        in_specs=[s for s in specs for _ in range(4)],
        out_specs=[s for s in specs for _ in range(4)],
        out_shape=[jax.ShapeDtypeStruct(w.shape, F32) for w in ws for _ in range(4)],
        compiler_params=_cp("parallel"), name=name)(*[a for q in zip(ws, gs, ms, vs) for a in q])
    return [tuple(out[4 * i:4 * i + 4]) for i in range(n)]


def _local_step(x, tgt, meta, first_weight, rest_weights, g_mix, g_mlp, g_fin, hg_gain, rpb, lb,
                early_grads=None, mid_grads=None, late_grad=None, rest_landed=None,
                last_grads=None):
    n_tok, d = x.shape
    hgw = hg_gain.shape[1]
    nh, hh = rpb.shape[0], hgw // HG_DK
    naw = nh * NA_HEAD_DIM
    l_real = N_META + n_tok
    lp = -(-l_real // ROW_ALIGN) * ROW_ALIGN
    col_qhg = 3 * naw
    col_zf, col_zb, col_i, col_g = (col_qhg + hgw, col_qhg + 2 * hgw, col_qhg + 3 * hgw,
                                    col_qhg + 4 * hgw)
    col_gate = col_qhg + 5 * hgw

    oh_np, neg_np = _rpb_onehot()
    oh = jnp.asarray(oh_np)
    rpb_p = jnp.pad(rpb.reshape(nh * (2 * NA_WIN_H - 1), 2 * NA_WIN_W - 1),
                    ((0, 0), (0, LANES - (2 * NA_WIN_W - 1))))
    tb = _matmul(rpb_p, oh, tm=rpb_p.shape[0], tn=512, tk=LANES, precision=HIGHEST,
                 name="rpb_expand")
    tb = (tb + jnp.asarray(neg_np)).reshape(nh, 2 * NA_WIN_H - 1, GRID_W, GRID_W)

    h0, a, tgt_p = _embed_norm(x, tgt, meta, g_mix, lp=lp, name="norm_mix")
    w_in = first_weight((a, tb))
    proj = _matmul(a, w_in, tm=lp, name="mm_in")
    o_na, lse = _na_fwd(proj, tb, n_tok=n_tok, nh=nh, name="na_fwd")
    lb_f = lb[0].reshape(hh, 1, HG_DK)
    lb_b = lb[1].reshape(hh, 1, HG_DK)
    scan_kw = dict(col_q=col_qhg, col_i=col_i, hh=hh)
    o_f, st_f = _hg_scan_fwd(proj, lb_f, reverse=False, col_z=col_zf, name="hg_scan_f", **scan_kw)
    token = rest_landed(o_f) if rest_landed else None
    lb_b_late = lb_b if token is None else lb_b + token[0:1, 0:1]
    o_b, st_b = _hg_scan_fwd(proj, lb_b_late, reverse=True, col_z=col_zb, name="hg_scan_b",
                             **scan_kw)
    o_hg = _hg_out(o_f, o_b, proj, hg_gain, col_g=col_g, name="hg_out")
    w_na, w_hg, w_o, w_up, w_down = rest_weights(o_hg)
    y_na = _matmul(o_na, w_na, name="mm_na_out", out_dtype=BF16)
    gates = ((proj, col_gate), (proj, col_gate + d))

    def mix_gates(acc, gn, gh, yn):
        return acc, _sigmoid(gn) * yn + _sigmoid(gh) * acc

    def mix_gates_bwd(dmix, gn, gh, yn, yh):
        sn, sh = _sigmoid(gn), _sigmoid(gh)
        return dmix * sn, dmix * sh, dmix * yn * sn * (1.0 - sn), dmix * yh * sh * (1.0 - sh)

    y_hg, mix = _matmul(o_hg, w_hg, name="mm_hg_out", epilogue=mix_gates,
                        tiles=(*gates, (y_na, 0)), out_dtypes=(BF16, BF16))
    t1 = _matmul(mix, w_o, name="mm_o")
    h1, mlp_in = _residual_norm(h0, t1, g_mlp, name="resid_norm_mlp")
    u, act = _matmul(mlp_in, w_up, name="mm_up", out_dtypes=(BF16, BF16),
                     epilogue=lambda acc: (acc, jnp.square(jnp.maximum(acc, 0.0))))
    t2 = _matmul(act, w_down, name="mm_down")
    dh2, dh2_16, loss, dg_fin = _final_loss(h1, t2, g_fin, tgt_p, n_tok=n_tok, name="final_loss")

    (du,) = _matmul(dh2_16, w_down, tb=True, name="mm_down_dx", tiles=((u, 0),),
                    out_dtypes=(BF16,),
                    epilogue=lambda acc, uv: (acc * 2.0 * jnp.maximum(uv, 0.0),))
    dw_down = _matmul(act, dh2_16, ta=True, name="mm_down_dw")
    dm = _matmul(du, w_up, tb=True, name="mm_up_dx")
    dw_up = _matmul(mlp_in, du, ta=True, name="mm_up_dw")
    dh1, dh1_16, dg_mlp = _rmsnorm_bwd_add(h1, g_mlp, dm, dh2, name="norm_mlp_bwd")
    dy_na, dy_hg, dgn, dgh = _matmul(dh1_16, w_o, tb=True, name="mm_o_dx", epilogue=mix_gates_bwd,
                                     tiles=(*gates, (y_na, 0), (y_hg, 0)), out_dtypes=(BF16,) * 4)
    dw_o = _matmul(mix, dh1_16, ta=True, name="mm_o_dw")
    do_na = _matmul(dy_na, w_na, tb=True, name="mm_na_out_dx")
    dw_na = _matmul(o_na, dy_na, ta=True, name="mm_na_out_dw")
    do_hg = _matmul(dy_hg, w_hg, tb=True, name="mm_hg_out_dx")
    dw_hg = _matmul(o_hg, dy_hg, ta=True, name="mm_hg_out_dw")
    token = early_grads([dw_na, dw_hg, dw_o, dw_up, dw_down]) if early_grads else None
    if token is not None:
        hg_gain = hg_gain + token[0:1, 0:1]
    d_o, dg_hg, d_gain = _hg_out_bwd(o_f, o_b, proj, hg_gain, do_hg, col_g=col_g, name="hg_out_bwd")
    dq_f, dz_f, dv_f, dlb_f = _hg_scan_bwd(proj, lb_f, st_f, d_o, reverse=False, col_z=col_zf,
                                           name="hg_scan_f_bwd", **scan_kw)
    token = mid_grads(dq_f) if mid_grads else None
    lb_b_late = lb_b if token is None else lb_b + token[0:1, 0:1]
    dq_b, dz_b, dv_b, dlb_b = _hg_scan_bwd(proj, lb_b_late, st_b, d_o, reverse=True, col_z=col_zb,
                                           name="hg_scan_b_bwd", **scan_kw)
    dq_na, dk_na, dv_na, dtb = _na_bwd(proj, tb, o_na, lse, do_na, n_tok=n_tok, nh=nh, name="na_bwd")
    dproj = _assemble_dproj(dq_na, dk_na, dv_na, dq_f, dq_b, dz_f, dz_b, dv_f, dv_b, dg_hg, dgn,
                            dgh, name="assemble_dproj")
    dw_in = _matmul(a, dproj, ta=True, name="mm_in_dw")
    token = late_grad(dw_in) if late_grad else None
    da = _matmul(dproj, w_in, tb=True, name="mm_in_dx", after=token)
    token = last_grads(da) if last_grads else None
    g_mix_late = g_mix if token is None else g_mix + token[0:1, 0:1]
    dx, dmeta, dg_mix = _rmsnorm_bwd_tokens(h0, g_mix_late, da, dh1, n_tok=n_tok,
                                            name="norm_mix_bwd")
    d_rpb = _matmul(dtb.reshape(nh * (2 * NA_WIN_H - 1), GRID_W * GRID_W), oh, tb=True,
                    tm=nh * (2 * NA_WIN_H - 1), tn=LANES, tk=1024, precision=HIGHEST,
                    name="rpb_reduce")
    d_lb = jnp.concatenate([dlb_f.reshape(1, hgw), dlb_b.reshape(1, hgw)], axis=0)
    return (loss, dx, dmeta, dw_in, dw_na, dw_hg, dw_o, dw_up, dw_down,
            dg_mix, dg_mlp, dg_fin, d_gain, d_rpb, d_lb)


N_CHIPS = 4
N_DEV = 8
ANY = pl.BlockSpec(memory_space=pl.ANY)


def _place():
    x, y, c = lax.axis_index("x"), lax.axis_index("y"), lax.axis_index("c")
    others = []
    for j in (1, 2, 3):
        tx = (1 - x) if (j >> 1) else x
        ty = (1 - y) if (j & 1) else y
        others.append((tx, ty))
    return x, y, c, others


def _piece(ref, axis, k, half, rh, cs):
    if axis == 1:
        return ref.at[pl.ds(pl.multiple_of(half * rh, 16), rh), pl.ds(pl.multiple_of(k * cs, LANES), cs)]
    return ref.at[pl.ds(pl.multiple_of(k * 2 * rh + half * rh, 16), rh), :]


def _cast_into_full(shards, axes, place, *, name):
    n = len(shards)
    steps, trs = _group_tiles([s.shape[0] for s in shards], 16)

    def body(p_ref, *refs):
        for i in range(n):
            refs[n + i][...] = refs[i][...].astype(BF16)

    def out_spec(tr, cs, axis):
        if axis == 1:
            return pl.BlockSpec((tr, cs), lambda i, p_ref: (i, p_ref[0]))
        return pl.BlockSpec((tr, cs), lambda i, p_ref: (p_ref[0] * steps + i, 0))

    return pl.pallas_call(
        body,
        grid_spec=pltpu.PrefetchScalarGridSpec(
            num_scalar_prefetch=1, grid=(steps,),
            in_specs=[pl.BlockSpec((tr, s.shape[1]), lambda i, p_ref: (i, 0))
                      for tr, s in zip(trs, shards)],
            out_specs=[out_spec(tr, s.shape[1], ax) for tr, s, ax in zip(trs, shards, axes)]),
        out_shape=[jax.ShapeDtypeStruct((s.shape[0], s.shape[1] * N_CHIPS) if ax == 1
                                        else (s.shape[0] * N_CHIPS, s.shape[1]), BF16)
                   for s, ax in zip(shards, axes)],
        compiler_params=_cp("parallel"), name=name)(place, *shards)


HBM_SPEC = pl.BlockSpec(memory_space=pltpu.HBM)
SEM_SPEC = pl.BlockSpec(memory_space=pltpu.SEMAPHORE)
SPLIT_COPY = pltpu.CompilerParams(has_side_effects=pltpu.SideEffectType.DATAFLOW_SIDE_EFFECTING)
TOKEN = jax.ShapeDtypeStruct((8, LANES), F32)


def _geo(fulls, axes):
    out = []
    for f, ax in zip(fulls, axes):
        r, cs = (f.shape[0], f.shape[1] // N_CHIPS) if ax == 1 else (f.shape[0] // N_CHIPS, f.shape[1])
        out.append((ax, r // 2, cs))
    return out


def _gather_copies(refs, geo, send_sems, recv_sems):
    x, y, c, others = _place()
    chip = 2 * x + y
    cps = []
    for i, (ax, rh, cs) in enumerate(geo):
        mine = _piece(refs[i], ax, chip, c, rh, cs)
        for j, (tx, ty) in enumerate(others):
            cps.append(pltpu.make_async_remote_copy(
                src_ref=mine, dst_ref=mine, send_sem=send_sems.at[3 * i + j],
                recv_sem=recv_sems.at[3 * i + j], device_id=(tx, ty, c), device_id_type=MESH))
    return cps


def _allgather_start(fulls, axes, after, *, name):
    n = len(fulls)
    geo = _geo(fulls, axes)

    def body(*refs):
        w_refs = refs[:n]
        send_sems, recv_sems = refs[n + 1], refs[n + 2]
        token = refs[2 * n + 3]
        for cp in _gather_copies(w_refs, geo, send_sems, recv_sems):
            cp.start()
        token[...] = jnp.zeros_like(token)

    out = pl.pallas_call(
        body, name=name,
        out_shape=(pltpu.SemaphoreType.DMA((3 * n,)), pltpu.SemaphoreType.DMA((3 * n,)),
                   *[pltpu.HBM(f.shape, f.dtype) for f in fulls], TOKEN),
        in_specs=[HBM_SPEC] * n + [ANY],
        out_specs=(SEM_SPEC, SEM_SPEC, *[HBM_SPEC] * n, pl.BlockSpec(memory_space=pltpu.VMEM)),
        input_output_aliases={i: 2 + i for i in range(n)},
        compiler_params=SPLIT_COPY,
    )(*[pltpu.with_memory_space_constraint(f, pltpu.HBM) for f in fulls], after)
    return out[0], out[1], list(out[2:2 + n]), out[2 + n]


def _allgather_wait(send_sems, recv_sems, fulls, axes, after, *, name):
    n = len(fulls)
    geo = _geo(fulls, axes)
    afters = tuple(after) if isinstance(after, (tuple, list)) else (after,)

    def body(*refs):
        w_refs = refs[:n]
        for cp in _gather_copies(w_refs, geo, refs[n], refs[n + 1]):
            cp.wait_send()
            cp.wait_recv()

    return list(pl.pallas_call(
        body, name=name,
        out_shape=[pltpu.HBM(f.shape, f.dtype) for f in fulls],
        in_specs=[HBM_SPEC] * n + [SEM_SPEC, SEM_SPEC] + [ANY] * len(afters),
        out_specs=[HBM_SPEC] * n,
        input_output_aliases={i: i for i in range(n)},
        compiler_params=SPLIT_COPY,
    )(*fulls, send_sems, recv_sems, *afters))


def _allgather_forward(fulls, axes, *, name):
    n = len(fulls)
    geo = _geo(fulls, axes)

    def body(*refs):
        o_refs = refs[n:2 * n]
        send_sems, recv_sems = refs[2 * n:]
        x, y, c, others = _place()

        def rcopy(i, j, half, to):
            ax, rh, cs = geo[i]
            ref = _piece(o_refs[i], ax, 2 * others[j][0] + others[j][1], half, rh, cs)
            return pltpu.make_async_remote_copy(
                src_ref=ref, dst_ref=ref, send_sem=send_sems.at[3 * i + j],
                recv_sem=recv_sems.at[3 * i + j], device_id=to, device_id_type=MESH)

        cps = [rcopy(i, j, c, (x, y, 1 - c)) for i in range(n) for j in range(3)]
        for cp in cps:
            cp.start()
        for i in range(n):
            for j in range(3):
                rcopy(i, j, 1 - c, (x, y, c)).wait_recv()
        for cp in cps:
            cp.wait_send()

    return list(pl.pallas_call(
        body, in_specs=[ANY] * n, out_specs=[ANY] * n,
        out_shape=[jax.ShapeDtypeStruct(f.shape, f.dtype) for f in fulls],
        input_output_aliases={i: i for i in range(n)},
        scratch_shapes=[pltpu.SemaphoreType.DMA((3 * n,)), pltpu.SemaphoreType.DMA((3 * n,))],
        name=name)(*fulls))


def _forward_copies(refs, geo, send_sems, recv_sems):
    x, y, c, others = _place()
    cps = []
    for i, (ax, rh, cs) in enumerate(geo):
        for j, (tx, ty) in enumerate(others):
            ref = _piece(refs[i], ax, 2 * tx + ty, c, rh, cs)
            cps.append(pltpu.make_async_remote_copy(
                src_ref=ref, dst_ref=ref, send_sem=send_sems.at[3 * i + j],
                recv_sem=recv_sems.at[3 * i + j], device_id=(x, y, 1 - c), device_id_type=MESH))
    return cps


def _allgather_forward_start(fulls, axes, *, name):
    n = len(fulls)
    geo = _geo(fulls, axes)

    def body(*refs):
        token = refs[2 * n + 2]
        for cp in _forward_copies(refs[:n], geo, refs[n], refs[n + 1]):
            cp.start()
        token[...] = jnp.zeros_like(token)

    out = pl.pallas_call(
        body, name=name,
        out_shape=(pltpu.SemaphoreType.DMA((3 * n,)), pltpu.SemaphoreType.DMA((3 * n,)),
                   *[pltpu.HBM(f.shape, f.dtype) for f in fulls], TOKEN),
        in_specs=[HBM_SPEC] * n,
        out_specs=(SEM_SPEC, SEM_SPEC, *[HBM_SPEC] * n, pl.BlockSpec(memory_space=pltpu.VMEM)),
        input_output_aliases={i: 2 + i for i in range(n)},
        compiler_params=SPLIT_COPY,
    )(*fulls)
    return out[0], out[1], list(out[2:2 + n]), out[2 + n]


def _allgather_forward_wait(send_sems, recv_sems, fulls, axes, after, *, name):
    n = len(fulls)
    geo = _geo(fulls, axes)

    def body(*refs):
        for cp in _forward_copies(refs[:n], geo, refs[n], refs[n + 1]):
            cp.wait_send()
            cp.wait_recv()

    return list(pl.pallas_call(
        body, name=name,
        out_shape=[pltpu.HBM(f.shape, f.dtype) for f in fulls],
        in_specs=[HBM_SPEC] * n + [SEM_SPEC, SEM_SPEC, ANY],
        out_specs=[HBM_SPEC] * n,
        input_output_aliases={i: i for i in range(n)},
        compiler_params=SPLIT_COPY,
    )(*fulls, send_sems, recv_sems, after))


def _chip_copies(blk_ref, land_ref, send_sems, recv_sems):
    x, y, c, others = _place()
    return [pltpu.make_async_remote_copy(
        src_ref=blk_ref, dst_ref=land_ref.at[2 * x + y], send_sem=send_sems.at[j],
        recv_sem=recv_sems.at[j], device_id=(tx, ty, c), device_id_type=MESH)
        for j, (tx, ty) in enumerate(others)]


def _chip_exchange_start(blk, *, name):
    land = pltpu.with_memory_space_constraint(lax.empty((N_CHIPS, *blk.shape), blk.dtype), pltpu.HBM)

    def body(blk_ref, land_ref, send_sems, recv_sems, blk_out, land_out, token):
        for cp in _chip_copies(blk_ref, land_ref, send_sems, recv_sems):
            cp.start()
        token[...] = jnp.zeros_like(token)

    return pl.pallas_call(
        body, name=name,
        out_shape=(pltpu.SemaphoreType.DMA((3,)), pltpu.SemaphoreType.DMA((3,)),
                   pltpu.HBM(blk.shape, blk.dtype), pltpu.HBM(land.shape, land.dtype), TOKEN),
        in_specs=[HBM_SPEC] * 2,
        out_specs=(SEM_SPEC, SEM_SPEC, HBM_SPEC, HBM_SPEC, pl.BlockSpec(memory_space=pltpu.VMEM)),
        input_output_aliases={0: 2, 1: 3},
        compiler_params=SPLIT_COPY,
    )(pltpu.with_memory_space_constraint(blk, pltpu.HBM), land)


def _chip_exchange_wait(send_sems, recv_sems, blk, land, after, *, name):
    def body(blk_ref, land_ref, send_sems, recv_sems, after_ref, blk_out, land_out):
        for cp in _chip_copies(blk_ref, land_ref, send_sems, recv_sems):
            cp.wait_send()
            cp.wait_recv()

    return pl.pallas_call(
        body, name=name,
        out_shape=[pltpu.HBM(blk.shape, blk.dtype), pltpu.HBM(land.shape, land.dtype)],
        in_specs=[HBM_SPEC] * 2 + [SEM_SPEC, SEM_SPEC, ANY],
        out_specs=[HBM_SPEC] * 2,
        input_output_aliases={0: 0, 1: 1},
        compiler_params=SPLIT_COPY,
    )(blk, land, send_sems, recv_sems, after)[1]


def _scatter_geo(parts, axes):
    out = []
    for p, ax in zip(parts, axes):
        _, rh, cols = p.shape
        out.append((ax, rh, cols // N_CHIPS if ax == 1 else cols))
    return out


def _scatter_copies(p_refs, q_refs, geo, send_sems, recv_sems):
    x, y, c, others = _place()
    chip = 2 * x + y
    cps = []
    for i, (ax, rh, cw) in enumerate(geo):
        for j, (tx, ty) in enumerate(others):
            k = 2 * tx + ty
            src = (p_refs[i].at[0, :, pl.ds(pl.multiple_of(k * cw, LANES), cw)] if ax == 1
                   else p_refs[i].at[k])
            cps.append(pltpu.make_async_remote_copy(
                src_ref=src, dst_ref=q_refs[i].at[chip], send_sem=send_sems.at[3 * i + j],
                recv_sem=recv_sems.at[3 * i + j], device_id=(tx, ty, c), device_id_type=MESH))
    return cps


def _scatter_start(parts, axes, *, name):
    n = len(parts)
    geo = _scatter_geo(parts, axes)
    slots = [pltpu.HBM((N_CHIPS, rh, cw), p.dtype) for p, (_, rh, cw) in zip(parts, geo)]

    def body(*refs):
        p_refs, q_refs = refs[:n], refs[n:2 * n]
        send_sems, recv_sems = refs[2 * n], refs[2 * n + 1]
        token = refs[4 * n + 2]
        for cp in _scatter_copies(p_refs, q_refs, geo, send_sems, recv_sems):
            cp.start()
        token[...] = jnp.zeros_like(token)

    land = [pltpu.with_memory_space_constraint(lax.empty(s.inner_aval.shape, s.inner_aval.dtype), pltpu.HBM)
            for s in slots]
    out = pl.pallas_call(
        body, name=name,
        out_shape=(pltpu.SemaphoreType.DMA((3 * n,)), pltpu.SemaphoreType.DMA((3 * n,)),
                   *[pltpu.HBM(p.shape, p.dtype) for p in parts], *slots, TOKEN),
        in_specs=[HBM_SPEC] * (2 * n),
        out_specs=(SEM_SPEC, SEM_SPEC, *[HBM_SPEC] * (2 * n), pl.BlockSpec(memory_space=pltpu.VMEM)),
        input_output_aliases={i: 2 + i for i in range(2 * n)},
        compiler_params=SPLIT_COPY,
    )(*[pltpu.with_memory_space_constraint(p, pltpu.HBM) for p in parts], *land)
    return out[0], out[1], list(out[2:2 + n]), list(out[2 + n:2 + 2 * n]), out[2 + 2 * n]


def _scatter_wait(send_sems, recv_sems, parts, slots, axes, after, *, name):
    n = len(parts)
    geo = _scatter_geo(parts, axes)

    def body(*refs):
        p_refs, q_refs = refs[:n], refs[n:2 * n]
        for cp in _scatter_copies(p_refs, q_refs, geo, refs[2 * n], refs[2 * n + 1]):
            cp.wait_send()
            cp.wait_recv()

    out = pl.pallas_call(
        body, name=name,
        out_shape=[pltpu.HBM(a.shape, a.dtype) for a in (*parts, *slots)],
        in_specs=[HBM_SPEC] * (2 * n) + [SEM_SPEC, SEM_SPEC, ANY],
        out_specs=[HBM_SPEC] * (2 * n),
        input_output_aliases={i: i for i in range(2 * n)},
        compiler_params=SPLIT_COPY,
    )(*parts, *slots, send_sems, recv_sems, after)
    return list(out[:n]), list(out[n:])


def _swap_copies(g_refs, r_refs, shapes, send_sems, recv_sems):
    x, y, c, _ = _place()
    cps = []
    for i, shape in enumerate(shapes):
        rh = shape[1] // 2
        src = g_refs[i].at[:, pl.ds(pl.multiple_of((1 - c) * rh, 16), rh), :]
        cps.append(pltpu.make_async_remote_copy(
            src_ref=src, dst_ref=r_refs[i], send_sem=send_sems.at[i], recv_sem=recv_sems.at[i],
            device_id=(x, y, 1 - c), device_id_type=MESH))
    return cps


def _sibling_swap_start(grads, *, name):
    n = len(grads)
    shapes = [g.shape for g in grads]
    lands = [pltpu.HBM((s[0], s[1] // 2, s[2]), g.dtype) for s, g in zip(shapes, grads)]

    def body(*refs):
        g_refs, r_refs = refs[:n], refs[n:2 * n]
        token = refs[4 * n + 2]
        for cp in _swap_copies(g_refs, r_refs, shapes, refs[2 * n], refs[2 * n + 1]):
            cp.start()
        token[...] = jnp.zeros_like(token)

    land = [pltpu.with_memory_space_constraint(lax.empty(s.inner_aval.shape, s.inner_aval.dtype), pltpu.HBM)
            for s in lands]
    out = pl.pallas_call(
        body, name=name,
        out_shape=(pltpu.SemaphoreType.DMA((n,)), pltpu.SemaphoreType.DMA((n,)),
                   *[pltpu.HBM(g.shape, g.dtype) for g in grads], *lands, TOKEN),
        in_specs=[HBM_SPEC] * (2 * n),
        out_specs=(SEM_SPEC, SEM_SPEC, *[HBM_SPEC] * (2 * n), pl.BlockSpec(memory_space=pltpu.VMEM)),
        input_output_aliases={i: 2 + i for i in range(2 * n)},
        compiler_params=SPLIT_COPY,
    )(*[pltpu.with_memory_space_constraint(g, pltpu.HBM) for g in grads], *land)
    return out[0], out[1], list(out[2:2 + n]), list(out[2 + n:2 + 2 * n]), out[2 + 2 * n]


def _sibling_swap_wait(send_sems, recv_sems, grads, lands, after, *, name):
    n = len(grads)
    shapes = [g.shape for g in grads]

    def body(*refs):
        g_refs, r_refs = refs[:n], refs[n:2 * n]
        for cp in _swap_copies(g_refs, r_refs, shapes, refs[2 * n], refs[2 * n + 1]):
            cp.wait_send()
            cp.wait_recv()

    out = pl.pallas_call(
        body, name=name,
        out_shape=[pltpu.HBM(a.shape, a.dtype) for a in (*grads, *lands)],
        in_specs=[HBM_SPEC] * (2 * n) + [SEM_SPEC, SEM_SPEC, ANY],
        out_specs=[HBM_SPEC] * (2 * n),
        input_output_aliases={i: i for i in range(2 * n)},
        compiler_params=SPLIT_COPY,
    )(*grads, *lands, send_sems, recv_sems, after)
    return list(out[:n]), list(out[n:])


def _pair_add(g3s, rxs, place, *, out_dtype, name):
    n = len(g3s)
    steps, trs = _group_tiles([g.shape[1] // 2 for g in g3s], 16)

    def body(p_ref, *refs):
        for i in range(n):
            refs[2 * n + i][...] = (refs[2 * i][...] + refs[2 * i + 1][...]).astype(out_dtype)

    in_specs, out_specs = [], []
    for g, tr in zip(g3s, trs):
        blk = (g.shape[0], tr, g.shape[2])
        in_specs += [pl.BlockSpec(blk, lambda i, p_ref: (0, p_ref[1] * steps + i, 0)),
                     pl.BlockSpec(blk, lambda i, p_ref: (0, i, 0))]
        out_specs.append(pl.BlockSpec(blk, lambda i, p_ref: (0, i, 0)))
    return pl.pallas_call(
        body,
        grid_spec=pltpu.PrefetchScalarGridSpec(
            num_scalar_prefetch=1, grid=(steps,), in_specs=in_specs, out_specs=out_specs),
        out_shape=[jax.ShapeDtypeStruct((g.shape[0], g.shape[1] // 2, g.shape[2]), out_dtype)
                   for g in g3s],
        compiler_params=_cp("parallel"), name=name)(place, *[a for q in zip(g3s, rxs) for a in q])


def _sum_slots(q, *, name):
    ns, rows, cols = q.shape
    tr = next(t for t in (128, 64, 32, 16, 8) if rows % t == 0)

    def body(q_ref, o_ref):
        acc = q_ref[0].astype(F32)
        for k in range(1, ns):
            acc = acc + q_ref[k].astype(F32)
        o_ref[...] = acc

    return pl.pallas_call(
        body, grid=(rows // tr,),
        in_specs=[pl.BlockSpec((ns, tr, cols), lambda i: (0, i, 0))],
        out_specs=_rspec(tr, cols),
        out_shape=jax.ShapeDtypeStruct((rows, cols), F32),
        compiler_params=_cp("parallel"), name=name)(q)


def _sum_chips(qs, ps, place, axes, *, name):
    n = len(qs)
    per = N_CHIPS + 1
    steps, trs = _group_tiles([q.shape[1] for q in qs], 16)

    def body(p_ref, *refs):
        chip = p_ref[0]
        for i in range(n):
            q_refs, own_ref = refs[per * i:per * i + N_CHIPS], refs[per * i + N_CHIPS]
            acc = jnp.where(chip == 0, own_ref[...], q_refs[0][...]).astype(F32)
            for k in range(1, N_CHIPS):
                acc = acc + jnp.where(chip == k, own_ref[...], q_refs[k][...]).astype(F32)
            refs[per * n + i][...] = acc

    def slot_spec(k, tr, cw):
        return pl.BlockSpec((None, tr, cw),
                            lambda i, p_ref: (jnp.where(p_ref[0] == k, (k + 1) % N_CHIPS, k), i, 0))

    in_specs, out_specs, operands = [], [], []
    for q, p, ax, tr in zip(qs, ps, axes, trs):
        cw = q.shape[2]
        in_specs += [slot_spec(k, tr, cw) for k in range(N_CHIPS)]
        in_specs.append(pl.BlockSpec((None, tr, cw), (lambda i, p_ref: (0, i, p_ref[0])) if ax == 1
                                     else (lambda i, p_ref: (p_ref[0], i, 0))))
        out_specs.append(pl.BlockSpec((tr, cw), lambda i, p_ref: (p_ref[1] * steps + i, 0)))
        operands += [q] * N_CHIPS + [p]
    return pl.pallas_call(
        body,
        grid_spec=pltpu.PrefetchScalarGridSpec(
            num_scalar_prefetch=1, grid=(steps,), in_specs=in_specs, out_specs=out_specs),
        out_shape=[jax.ShapeDtypeStruct((2 * q.shape[1], q.shape[2]), F32) for q in qs],
        compiler_params=_cp("parallel"), name=name)(place, *operands)


def _sibling_share(shards, *, name):
    n = len(shards)

    def body(*refs):
        o_refs = refs[n:2 * n]
        send_sems, recv_sems = refs[2 * n:]
        x, y, c, _ = _place()
        cps = []
        for i in range(n):
            rh = shards[i].shape[0] // 2
            mine = o_refs[i].at[pl.ds(pl.multiple_of(c * rh, 8), rh), :]
            cp = pltpu.make_async_remote_copy(
                src_ref=mine, dst_ref=mine, send_sem=send_sems.at[i], recv_sem=recv_sems.at[i],
                device_id=(x, y, 1 - c), device_id_type=MESH)
            cp.start()
            cps.append(cp)
        for i in range(n):
            rh = shards[i].shape[0] // 2
            theirs = o_refs[i].at[pl.ds(pl.multiple_of((1 - c) * rh, 8), rh), :]
            pltpu.make_async_remote_copy(
                src_ref=theirs, dst_ref=theirs, send_sem=send_sems.at[i], recv_sem=recv_sems.at[i],
                device_id=(x, y, c), device_id_type=MESH).wait_recv()
        for cp in cps:
            cp.wait_send()

    return pl.pallas_call(
        body, in_specs=[ANY] * n, out_specs=[ANY] * n,
        out_shape=[jax.ShapeDtypeStruct(h.shape, h.dtype) for h in shards],
        input_output_aliases={i: i for i in range(n)},
        scratch_shapes=[pltpu.SemaphoreType.DMA((n,)), pltpu.SemaphoreType.DMA((n,))],
        name=name)(*shards)


def _gather_all(blk, *, name, after=None, shares=()):
    rows, cols = blk.shape
    extra = [] if after is None else [after]
    n_s = len(shares)

    def body(x_ref, *refs):
        s_refs = refs[len(extra) + n_s + 1:len(extra) + 2 * n_s + 1]
        out_ref = refs[len(extra) + n_s]
        send_sems, recv_sems, local_sem, s_send, s_recv = refs[len(extra) + 2 * n_s + 1:]
        x, y, c = lax.axis_index("x"), lax.axis_index("y"), lax.axis_index("c")
        me = 4 * x + 2 * y + c
        mine = pltpu.make_async_copy(x_ref, out_ref.at[me], local_sem)
        mine.start()
        cps = []
        for i in range(n_s):
            rh = shares[i].shape[0] // 2
            half = s_refs[i].at[pl.ds(pl.multiple_of(c * rh, 8), rh), :]
            cp = pltpu.make_async_remote_copy(
                src_ref=half, dst_ref=half, send_sem=s_send.at[i], recv_sem=s_recv.at[i],
                device_id=(x, y, 1 - c), device_id_type=MESH)
            cp.start()
            cps.append(cp)
        for k in range(1, N_DEV):
            tx = (1 - x) if (k >> 2) & 1 else x
            ty = (1 - y) if (k >> 1) & 1 else y
            tc = (1 - c) if k & 1 else c
            cp = pltpu.make_async_remote_copy(
                src_ref=x_ref, dst_ref=out_ref.at[me], send_sem=send_sems.at[k - 1],
                recv_sem=recv_sems.at[k - 1], device_id=(tx, ty, tc), device_id_type=MESH)
            cp.start()
            cps.append(cp)
        for k in range(1, N_DEV):
            tx = (1 - x) if (k >> 2) & 1 else x
            ty = (1 - y) if (k >> 1) & 1 else y
            tc = (1 - c) if k & 1 else c
            got = out_ref.at[4 * tx + 2 * ty + tc]
            pltpu.make_async_remote_copy(
                src_ref=got, dst_ref=got, send_sem=send_sems.at[k - 1], recv_sem=recv_sems.at[k - 1],
                device_id=(x, y, c), device_id_type=MESH).wait_recv()
        for i in range(n_s):
            rh = shares[i].shape[0] // 2
            theirs = s_refs[i].at[pl.ds(pl.multiple_of((1 - c) * rh, 8), rh), :]
            pltpu.make_async_remote_copy(
                src_ref=theirs, dst_ref=theirs, send_sem=s_send.at[i], recv_sem=s_recv.at[i],
                device_id=(x, y, c), device_id_type=MESH).wait_recv()
        for cp in cps:
            cp.wait_send()
        mine.wait()

    vm = pl.BlockSpec(memory_space=pltpu.VMEM)
    out = pl.pallas_call(
        body, in_specs=[vm] + [ANY] * (len(extra) + n_s), out_specs=[vm] + [ANY] * n_s,
        out_shape=[jax.ShapeDtypeStruct((N_DEV, rows, cols), blk.dtype)]
        + [jax.ShapeDtypeStruct(s.shape, s.dtype) for s in shares],
        input_output_aliases={1 + len(extra) + i: 1 + i for i in range(n_s)},
        scratch_shapes=[pltpu.SemaphoreType.DMA((N_DEV - 1,)), pltpu.SemaphoreType.DMA((N_DEV - 1,)),
                        pltpu.SemaphoreType.DMA, pltpu.SemaphoreType.DMA((max(n_s, 1),)),
                        pltpu.SemaphoreType.DMA((max(n_s, 1),))],
        name=name)(blk, *extra, *shares)
    return (out[0], *out[1:]) if n_s else out[0]


def _as_rows(a):
    flat = a.reshape(-1)
    n = flat.shape[0]
    rows = -(-n // (8 * LANES)) * 8
    return jnp.pad(flat, (0, rows * LANES - n)).reshape(rows, LANES)


def _from_rows(p, shape):
    n = int(np.prod(shape))
    return p.reshape(-1)[:n].reshape(shape)


WEIGHT_AXES = (1, 1, 1, 0, 1, 0)
WIRE = BF16


def kernel(x, meta_tokens, w_in, w_na_out, w_hg_out, w_o, w_up, w_down, norm_mix, norm_mlp, norm_final, hg_norm, na_rpb, hg_lb_logits, loss_target, m_meta_tokens, m_w_in, m_w_na_out, m_w_hg_out, m_w_o, m_w_up, m_w_down, m_norm_mix, m_norm_mlp, m_norm_final, m_hg_norm, m_na_rpb, m_hg_lb_logits, v_meta_tokens, v_w_in, v_w_na_out, v_w_hg_out, v_w_o, v_w_up, v_w_down, v_norm_mix, v_norm_mlp, v_norm_final, v_hg_norm, v_na_rpb, v_hg_lb_logits):
    xi, yi, ci = lax.axis_index("x"), lax.axis_index("y"), lax.axis_index("c")
    chip = 2 * xi + yi
    d = x.shape[-1]
    dshard = meta_tokens.shape[1]
    hgw = hg_norm.shape[1]
    lbs = hg_lb_logits.shape[2]
    big = [w_in[0], w_na_out[0], w_hg_out[0], w_o[0], w_up[0], w_down[0]]
    big_m = [m_w_in[0], m_w_na_out[0], m_w_hg_out[0], m_w_o[0], m_w_up[0], m_w_down[0]]
    big_v = [v_w_in[0], v_w_na_out[0], v_w_hg_out[0], v_w_o[0], v_w_up[0], v_w_down[0]]

    place = jnp.stack([chip, ci]).astype(jnp.int32)
    in_axes, rest_axes = WEIGHT_AXES[:1], WEIGHT_AXES[1:]
    own_w = _cast_into_full(big, WEIGHT_AXES, place, name="cast_shards")
    small_in = jnp.concatenate([_as_rows(meta_tokens), _as_rows(hg_lb_logits)], axis=0)
    sm_send, sm_recv, sm_blk, sm_land, sm_token = _chip_exchange_start(small_in,
                                                                       name="small_params_start")
    in_send, in_recv, in_bufs, in_token = _allgather_start(own_w[:1], in_axes, sm_token,
                                                           name="weight_allgather_in_start")
    ag_send, ag_recv, ag_bufs, ag_token = _allgather_start(own_w[1:], rest_axes, in_token,
                                                           name="weight_allgather_rest_start")
    sm_land = _chip_exchange_wait(sm_send, sm_recv, sm_blk, sm_land, ag_token,
                                  name="small_params_wait")
    small_all = lax.dynamic_update_slice(sm_land, small_in[None], (chip, 0, 0))
    forward = {}

    def first_weight(after):
        got = _allgather_wait(in_send, in_recv, in_bufs, in_axes, after,
                              name="weight_allgather_in_wait")
        return _allgather_forward(got, in_axes, name="weight_allgather_in_forward")[0]

    def rest_landed(after):
        got = _allgather_wait(ag_send, ag_recv, ag_bufs, rest_axes, after,
                              name="weight_allgather_rest_wait")
        send, recv, bufs, token = _allgather_forward_start(
            got, rest_axes, name="weight_allgather_rest_forward_start")
        forward["rest"] = (send, recv, bufs)
        return token

    def rest_weights(after):
        return _allgather_forward_wait(*forward["rest"], rest_axes, after,
                                       name="weight_allgather_rest_forward_wait")

    n_meta_rows = N_META * dshard // LANES
    meta_full = (small_all[:, :n_meta_rows].reshape(N_CHIPS, N_META, dshard)
                 .transpose(1, 0, 2).reshape(N_META, d))
    lbl_full = (small_all[:, n_meta_rows:].reshape(N_CHIPS, -1)[:, :4 * lbs]
                .reshape(N_CHIPS, 2, 2, lbs).transpose(1, 2, 0, 3).reshape(2, 2, N_CHIPS * lbs))
    lb = jax.nn.softmax(lbl_full, axis=1)[:, 0]

    def by_chip(dws, axes):
        return [g.reshape(1, *g.shape) if ax == 1
                else g.reshape(N_CHIPS, g.shape[0] // N_CHIPS, g.shape[1]) for g, ax in zip(dws, axes)]

    flying = {}

    def scatter(tag, axes, g3, rx):
        parts = _pair_add(g3, rx, place, out_dtype=WIRE, name=f"grad_pair_add_{tag}")
        send, recv, parts, slots, token = _scatter_start(parts, axes,
                                                         name=f"grad_scatter_{tag}_start")
        flying[tag] = (send, recv, parts, slots)
        return token

    def swap(tag, axes):
        def start(dws):
            send, recv, g3, lands, token = _sibling_swap_start(
                by_chip(dws, axes), name=f"grad_sibling_swap_{tag}_start")
            flying["swap_" + tag] = (send, recv, g3, lands)
            return token

        def finish(after):
            g3, rx = _sibling_swap_wait(*flying["swap_" + tag], after,
                                        name=f"grad_sibling_swap_{tag}_wait")
            return scatter(tag, axes, g3, rx)
        return start, finish

    swap_rest, scatter_rest = swap("rest", rest_axes)
    swap_in, scatter_in = swap("in", in_axes)

    def landed(tag, axes, after):
        return _scatter_wait(*flying[tag], axes, after, name=f"grad_scatter_{tag}_wait")

    (loss, dx, dmeta, *_, dg_mix, dg_mlp, dg_fin, d_gain, d_rpb, d_lb) = _local_step(
        x[0], loss_target[0], meta_full, first_weight, rest_weights, norm_mix, norm_mlp,
        norm_final.reshape(1, d), hg_norm, na_rpb[0], lb, swap_rest, scatter_rest,
        lambda dw_in: swap_in([dw_in]), rest_landed, scatter_in)

    parts_rest, slots_rest = landed("rest", rest_axes, dx)
    g_rest = _sibling_share(_sum_chips(slots_rest, parts_rest, place, rest_axes,
                                       name="grad_sum_chips_rest"),
                            name="grad_sibling_share_rest")
    out_rest = _adamw(big[1:], g_rest, big_m[1:], big_v[1:], name="adamw_rest")
    parts_in, slots_in = landed("in", in_axes, out_rest[-1][0])
    half_in = _sum_chips(slots_in, parts_in, place, in_axes, name="grad_sum_chips_in")

    d_rpb_c = d_rpb[:, :2 * NA_WIN_W - 1]
    small_g = [dmeta, dg_mix, dg_mlp, dg_fin, d_gain, d_rpb_c, d_lb, loss]
    packed = jnp.concatenate([_as_rows(a) for a in small_g], axis=0)
    gathered, g_in = _gather_all(packed, shares=half_in, name="gather_small_grads")
    total = _sum_slots(gathered, name="sum_small_grads")
    offs = np.cumsum([0] + [_as_rows(a).shape[0] for a in small_g])
    take = lambda i, shape: _from_rows(total[offs[i]:offs[i + 1]], shape)
    g_meta_full = take(0, (N_META, d))
    g_norm_mix, g_norm_mlp = take(1, (1, d)), take(2, (1, d))
    g_norm_final = take(3, (d,))
    g_hg_norm = take(4, (1, hgw))
    g_rpb = take(5, na_rpb.shape)
    g_lb = take(6, (2, hgw))
    loss_total = take(7, (1, LANES))[0, 0]
    g_meta = lax.dynamic_slice_in_dim(g_meta_full, chip * dshard, dshard, axis=1)
    dl0 = lb * (1.0 - lb) * g_lb
    g_lbl_full = jnp.stack([dl0, -dl0], axis=1)
    g_lbl = lax.dynamic_slice_in_dim(g_lbl_full, chip * lbs, lbs, axis=2)

    big_out = _adamw(big[:1], [g_in], big_m[:1], big_v[:1], name="adamw_in") + out_rest
    small_w = [meta_tokens, norm_mix, norm_mlp, norm_final, hg_norm, na_rpb, hg_lb_logits]
    small_gr = [g_meta, g_norm_mix, g_norm_mlp, g_norm_final, g_hg_norm, g_rpb, g_lbl]
    small_m = [m_meta_tokens, m_norm_mix, m_norm_mlp, m_norm_final, m_hg_norm, m_na_rpb, m_hg_lb_logits]
    small_v = [v_meta_tokens, v_norm_mix, v_norm_mlp, v_norm_final, v_hg_norm, v_na_rpb, v_hg_lb_logits]
    pk = lambda lst: jnp.concatenate([_as_rows(a) for a in lst], axis=0)
    ((sd, sm, sv, _),) = _adamw([pk(small_w)], [pk(small_gr)], [pk(small_m)], [pk(small_v)],
                             name="adamw_small")
    soffs = np.cumsum([0] + [_as_rows(a).shape[0] for a in small_w])
    unpk = lambda p: [_from_rows(p[soffs[i]:soffs[i + 1]], small_w[i].shape) for i in range(len(small_w))]
    sd, sm, sv = unpk(sd), unpk(sm), unpk(sv)

    def order(bigs, smalls):
        return [smalls[0]] + [b.reshape(1, *b.shape) for b in bigs] + smalls[1:]

    grads = order([o[3] for o in big_out], small_gr)
    deltas = order([o[0] for o in big_out], sd)
    new_m = order([o[1] for o in big_out], sm)
    new_v = order([o[2] for o in big_out], sv)
    return (loss_total, dx.reshape(1, *dx.shape), *grads, *deltas, *new_m, *new_v)
```

```python
import functools

import numpy as np
import jax
import jax.numpy as jnp
from jax import lax
from jax.experimental import pallas as pl
from jax.experimental.pallas import tpu as pltpu

F32 = jnp.float32
BF16 = jnp.bfloat16
HIGHEST = lax.Precision.HIGHEST

GRID_W = 64
N_META = 16
EPS = 1e-6
NA_HEAD_DIM = 64
NA_WIN_H = 8
NA_WIN_W = 16
HG_DK = 128
LANES = 128
ROW_ALIGN = 128
VMEM_LIMIT = 48 * 1024 * 1024
ADAM_LR = 0.001
ADAM_B1 = 0.9
ADAM_B2 = 0.999
ADAM_EPS = 1e-08
ADAM_WD = 0.01
ADAM_STEP = 10

MESH = pl.DeviceIdType.MESH


def _cp(*sem):
    return pltpu.CompilerParams(dimension_semantics=sem, vmem_limit_bytes=VMEM_LIMIT)


def _sigmoid(x):
    return 0.5 * jnp.tanh(0.5 * x) + 0.5


def _dot(a, b, dims, precision=None):
    return lax.dot_general(a, b, (dims, ((), ())), preferred_element_type=F32, precision=precision)


def _nn(a, b, **kw):
    return _dot(a, b, ((1,), (0,)), **kw)


def _nt(a, b, **kw):
    return _dot(a, b, ((1,), (1,)), **kw)


def _tn(a, b, **kw):
    return _dot(a, b, ((0,), (0,)), **kw)


def _matmul(a, b, *, ta=False, tb=False, tm=None, tn=None, tk=None, out_dtype=F32, name,
            precision=None, after=None, epilogue=None, tiles=(), out_dtypes=None):
    extra = [] if after is None else [after]
    single = out_dtypes is None
    if single:
        out_dtypes = (out_dtype,)
    n_t, n_o = len(tiles), len(out_dtypes)
    if ta:
        kdim, m = a.shape
    else:
        m, kdim = a.shape
    if tb:
        n, k2 = b.shape
    else:
        k2, n = b.shape
    assert kdim == k2, (a.shape, b.shape, ta, tb)
    if tm is None:
        if ta:
            tm = next(t for t in (1024, 512, 256, 128, m) if m % t == 0)
        else:
            tm = m // 2 if (m // 2) % 16 == 0 and m > 512 else m
    if tn is None:
        wide = (1024,) if not ta and len(tiles) <= 1 else ()
        tn = next(t for t in (*wide, 512, 256, 128, n) if n % t == 0)
    if tk is None:
        tk = kdim if ta else next(t for t in (2048, 1024, 512, 256, 128, kdim) if kdim % t == 0)
    assert m % tm == 0 and n % tn == 0 and kdim % tk == 0, (m, n, kdim, tm, tn, tk)
    nk = kdim // tk
    op_dtype = F32 if precision is not None else BF16

    def body(a_ref, b_ref, *refs):
        t_refs = refs[:n_t]
        o_refs = refs[n_t + len(extra):n_t + len(extra) + n_o]
        av = a_ref[...].astype(op_dtype)
        bv = b_ref[...].astype(op_dtype)
        dims = ((0 if ta else 1,), (1 if tb else 0,))
        part = _dot(av, bv, dims, precision=precision)

        def finish(acc):
            outs = (acc,) if epilogue is None else epilogue(acc, *[t[...] for t in t_refs])
            for o_ref, val in zip(o_refs, outs):
                o_ref[...] = val.astype(o_ref.dtype)

        if nk == 1:
            finish(part)
            return
        acc_ref = refs[-1]
        kk = pl.program_id(2)

        @pl.when(kk == 0)
        def _():
            acc_ref[...] = part

        @pl.when((kk > 0) & (kk < nk - 1))
        def _():
            acc_ref[...] += part

        @pl.when(kk == nk - 1)
        def _():
            finish(acc_ref[...] + part)

    a_spec = (pl.BlockSpec((tk, tm), lambda i, j, k: (k, i)) if ta
              else pl.BlockSpec((tm, tk), lambda i, j, k: (i, k)))
    b_spec = (pl.BlockSpec((tn, tk), lambda i, j, k: (j, k)) if tb
              else pl.BlockSpec((tk, tn), lambda i, j, k: (k, j)))
    for _, off in tiles:
        assert off % tn == 0, (off, tn)
    t_specs = [pl.BlockSpec((tm, tn), functools.partial(lambda i, j, k, o: (i, o + j), o=off // tn))
               for _, off in tiles]
    o_spec = pl.BlockSpec((tm, tn), lambda i, j, k: (i, j))
    outs = pl.pallas_call(
        body,
        grid=(m // tm, n // tn, nk),
        in_specs=[a_spec, b_spec] + t_specs + [pl.BlockSpec(memory_space=pl.ANY)] * len(extra),
        out_specs=[o_spec] * n_o,
        out_shape=[jax.ShapeDtypeStruct((m, n), dt) for dt in out_dtypes],
        scratch_shapes=[pltpu.VMEM((tm, tn), F32)] if nk > 1 else [],
        compiler_params=_cp("parallel", "parallel", "arbitrary"),
        name=name,
    )(a, b, *[t for t, _ in tiles], *extra)
    return outs[0] if single else outs


def _rspec(tr, w, cb=0):
    return pl.BlockSpec((tr, w), lambda i: (i, cb))


def _fspec(shape):
    nd = len(shape)
    return pl.BlockSpec(shape, lambda i: (0,) * nd)


ROW_VMEM_BUDGET = 20 * 1024 * 1024
ROW_MIN_STEPS = 4


def _row_tile(lp, row_bytes):
    for k in range(ROW_MIN_STEPS, lp // 16 + 1):
        tr = lp // k
        if lp % k == 0 and tr % 16 == 0 and 2 * tr * row_bytes <= ROW_VMEM_BUDGET:
            return tr
    return lp


def _token_rows_copy(i, n_tiles, tr, n_tok, tok_ref, buf_ref, sem, *, to_tokens, start=True,
                     wait=True):
    assert n_tiles >= 2 and 0 < n_tok + N_META - (n_tiles - 1) * tr <= tr

    def run(tok_row, buf_row, count):
        tok = tok_ref.at[pl.ds(tok_row, count), :]
        buf = buf_ref.at[pl.ds(buf_row, count), :]
        cp = pltpu.make_async_copy(buf, tok, sem) if to_tokens else pltpu.make_async_copy(tok, buf, sem)
        if start:
            cp.start()
        if wait:
            cp.wait()

    @pl.when(i == 0)
    def _():
        run(0, N_META, tr - N_META)

    if n_tiles > 2:
        @pl.when((i > 0) & (i < n_tiles - 1))
        def _():
            run(pl.multiple_of(i * tr - N_META, 8), 0, tr)

    @pl.when(i == n_tiles - 1)
    def _():
        run((n_tiles - 1) * tr - N_META, 0, n_tok + N_META - (n_tiles - 1) * tr)


def _embed_norm(x, tgt, meta, g, *, lp, name):
    n_tok, d = x.shape
    tr = _row_tile(lp, d * (4 + 2 + 4))
    n_tiles = lp // tr

    def body(x_ref, tgt_ref, meta_ref, g_ref, h_ref, o_ref, tp_ref, buf_ref, tbuf_ref, sems):
        i = pl.program_id(0)
        buf_ref[...] = jnp.zeros_like(buf_ref)
        tbuf_ref[...] = jnp.zeros_like(tbuf_ref)

        @pl.when(i == 0)
        def _():
            buf_ref[0:N_META, :] = meta_ref[...]

        _token_rows_copy(i, n_tiles, tr, n_tok, tgt_ref, tbuf_ref, sems.at[1], to_tokens=False,
                         wait=False)
        _token_rows_copy(i, n_tiles, tr, n_tok, x_ref, buf_ref, sems.at[0], to_tokens=False)
        xv = buf_ref[...]
        h_ref[...] = xv
        r = lax.rsqrt(jnp.mean(xv * xv, axis=-1, keepdims=True) + EPS)
        o_ref[...] = (xv * r * g_ref[...]).astype(BF16)
        _token_rows_copy(i, n_tiles, tr, n_tok, tgt_ref, tbuf_ref, sems.at[1], to_tokens=False,
                         start=False)
        tp_ref[...] = tbuf_ref[...]

    return pl.pallas_call(
        body, grid=(n_tiles,),
        in_specs=[ANY, ANY, _fspec((N_META, d)), _fspec((1, d))],
        out_specs=[_rspec(tr, d), _rspec(tr, d), _rspec(tr, d)],
        out_shape=[jax.ShapeDtypeStruct((lp, d), F32), jax.ShapeDtypeStruct((lp, d), BF16),
                   jax.ShapeDtypeStruct((lp, d), F32)],
        scratch_shapes=[pltpu.VMEM((tr, d), F32), pltpu.VMEM((tr, d), F32),
                        pltpu.SemaphoreType.DMA((2,))],
        compiler_params=_cp("parallel"), name=name)(x, tgt, meta, g)


def _residual_norm(h, t, g, *, name):
    lp, d = h.shape
    tr = _row_tile(lp, d * (4 + 4 + 4 + 2))

    def body(h_ref, t_ref, g_ref, h1_ref, m_ref):
        xv = h_ref[...] + t_ref[...]
        h1_ref[...] = xv
        r = lax.rsqrt(jnp.mean(xv * xv, axis=-1, keepdims=True) + EPS)
        m_ref[...] = (xv * r * g_ref[...]).astype(BF16)

    return pl.pallas_call(
        body, grid=(lp // tr,),
        in_specs=[_rspec(tr, d), _rspec(tr, d), _fspec((1, d))],
        out_specs=[_rspec(tr, d), _rspec(tr, d)],
        out_shape=[jax.ShapeDtypeStruct((lp, d), F32), jax.ShapeDtypeStruct((lp, d), BF16)],
        compiler_params=_cp("parallel"), name=name)(h, t, g)


def _rmsnorm_bwd_add(x, g, dy, dres, *, name):
    lp, d = x.shape
    tr = _row_tile(lp, d * (4 * 4 + 2))

    def body(x_ref, g_ref, dy_ref, dr_ref, dx_ref, dx16_ref, dg_ref):
        @pl.when(pl.program_id(0) == 0)
        def _():
            dg_ref[...] = jnp.zeros_like(dg_ref)

        xv = x_ref[...]
        r = lax.rsqrt(jnp.mean(xv * xv, axis=-1, keepdims=True) + EPS)
        xh = xv * r
        dyv = dy_ref[...]
        dg_ref[...] += jnp.sum(dyv * xh, axis=0, keepdims=True)
        dxh = dyv * g_ref[...]
        dx = dr_ref[...] + r * (dxh - xh * jnp.mean(dxh * xh, axis=-1, keepdims=True))
        dx_ref[...] = dx
        dx16_ref[...] = dx.astype(BF16)

    return pl.pallas_call(
        body, grid=(lp // tr,),
        in_specs=[_rspec(tr, d), _fspec((1, d)), _rspec(tr, d), _rspec(tr, d)],
        out_specs=[_rspec(tr, d), _rspec(tr, d), _fspec((1, d))],
        out_shape=[jax.ShapeDtypeStruct((lp, d), F32), jax.ShapeDtypeStruct((lp, d), BF16),
                   jax.ShapeDtypeStruct((1, d), F32)],
        compiler_params=_cp("arbitrary"), name=name)(x, g, dy, dres)


def _rmsnorm_bwd_tokens(x, g, dy, dres, *, n_tok, name):
    lp, d = x.shape
    tr = _row_tile(lp, d * 4 * 4)
    n_tiles = lp // tr

    def body(x_ref, g_ref, dy_ref, dr_ref, dtok_ref, dmeta_ref, dg_ref, buf_ref, sem):
        i = pl.program_id(0)

        @pl.when(i == 0)
        def _():
            dg_ref[...] = jnp.zeros_like(dg_ref)

        xv = x_ref[...]
        r = lax.rsqrt(jnp.mean(xv * xv, axis=-1, keepdims=True) + EPS)
        xh = xv * r
        dyv = dy_ref[...]
        dg_ref[...] += jnp.sum(dyv * xh, axis=0, keepdims=True)
        dxh = dyv * g_ref[...]
        buf_ref[...] = dr_ref[...] + r * (dxh - xh * jnp.mean(dxh * xh, axis=-1, keepdims=True))

        @pl.when(i == 0)
        def _():
            dmeta_ref[...] = buf_ref[0:N_META, :]

        _token_rows_copy(i, n_tiles, tr, n_tok, dtok_ref, buf_ref, sem, to_tokens=True)

    return pl.pallas_call(
        body, grid=(n_tiles,),
        in_specs=[_rspec(tr, d), _fspec((1, d)), _rspec(tr, d), _rspec(tr, d)],
        out_specs=[ANY, _fspec((N_META, d)), _fspec((1, d))],
        out_shape=[jax.ShapeDtypeStruct((n_tok, d), F32), jax.ShapeDtypeStruct((N_META, d), F32),
                   jax.ShapeDtypeStruct((1, d), F32)],
        scratch_shapes=[pltpu.VMEM((tr, d), F32), pltpu.SemaphoreType.DMA],
        compiler_params=_cp("arbitrary"), name=name)(x, g, dy, dres)


def _final_loss(h1, t2, g, tgt, *, n_tok, name):
    lp, d = h1.shape
    tr = _row_tile(lp, d * (4 * 4 + 2))
    n_tiles = lp // tr

    def body(h_ref, t_ref, g_ref, tg_ref, dh_ref, dh16_ref, loss_ref, dg_ref):
        i = pl.program_id(0)

        @pl.when(i == 0)
        def _():
            loss_ref[...] = jnp.zeros_like(loss_ref)
            dg_ref[...] = jnp.zeros_like(dg_ref)

        xv = h_ref[...] + t_ref[...]
        r = lax.rsqrt(jnp.mean(xv * xv, axis=-1, keepdims=True) + EPS)
        xh = xv * r
        gv = g_ref[...]
        row = i * tr + lax.broadcasted_iota(jnp.int32, (tr, 1), 0)
        valid = (row >= N_META) & (row < N_META + n_tok)
        err = jnp.where(valid, xh * gv - tg_ref[...], 0.0)
        loss_ref[...] += jnp.sum(0.5 * err * err) / d
        dy = err / d
        dg_ref[...] += jnp.sum(dy * xh, axis=0, keepdims=True)
        dxh = dy * gv
        dh = r * (dxh - xh * jnp.mean(dxh * xh, axis=-1, keepdims=True))
        dh_ref[...] = dh
        dh16_ref[...] = dh.astype(BF16)

    return pl.pallas_call(
        body, grid=(n_tiles,),
        in_specs=[_rspec(tr, d), _rspec(tr, d), _fspec((1, d)), _rspec(tr, d)],
        out_specs=[_rspec(tr, d), _rspec(tr, d), _fspec((1, LANES)), _fspec((1, d))],
        out_shape=[jax.ShapeDtypeStruct((lp, d), F32), jax.ShapeDtypeStruct((lp, d), BF16),
                   jax.ShapeDtypeStruct((1, LANES), F32), jax.ShapeDtypeStruct((1, d), F32)],
        compiler_params=_cp("arbitrary"), name=name)(h1, t2, g, tgt)


def _hg_out(o_f, o_b, proj, gain, *, col_g, name):
    lp, w = o_f.shape
    tr = _row_tile(lp, w * (3 * 4 + 2))
    hh = w // HG_DK

    def body(of_ref, ob_ref, g_ref, gain_ref, y_ref):
        gv = g_ref[...]
        sg = gv * _sigmoid(gv)
        for h in range(hh):
            sl = slice(h * HG_DK, (h + 1) * HG_DK)
            o = of_ref[:, sl] + ob_ref[:, sl]
            r = lax.rsqrt(jnp.mean(o * o, axis=-1, keepdims=True) + EPS)
            y_ref[:, sl] = (o * r * gain_ref[:, sl] * sg[:, sl]).astype(BF16)

    return pl.pallas_call(
        body, grid=(lp // tr,),
        in_specs=[_rspec(tr, w), _rspec(tr, w), _rspec(tr, w, col_g // w), _fspec((1, w))],
        out_specs=_rspec(tr, w),
        out_shape=jax.ShapeDtypeStruct((lp, w), BF16),
        compiler_params=_cp("parallel"), name=name)(o_f, o_b, proj, gain)


def _hg_out_bwd(o_f, o_b, proj, gain, dy, *, col_g, name):
    lp, w = o_f.shape
    tr = _row_tile(lp, w * (5 * 4 + 2))
    hh = w // HG_DK

    def body(of_ref, ob_ref, g_ref, gain_ref, dy_ref, do_ref, dg_ref, dgain_ref):
        @pl.when(pl.program_id(0) == 0)
        def _():
            dgain_ref[...] = jnp.zeros_like(dgain_ref)

        for h in range(hh):
            sl = slice(h * HG_DK, (h + 1) * HG_DK)
            gv = g_ref[:, sl]
            s = _sigmoid(gv)
            sg = gv * s
            dsg = s + gv * s * (1.0 - s)
            o = of_ref[:, sl] + ob_ref[:, sl]
            r = lax.rsqrt(jnp.mean(o * o, axis=-1, keepdims=True) + EPS)
            on = o * r
            dyv = dy_ref[:, sl]
            gn = gain_ref[:, sl]
            dgain_ref[:, sl] += jnp.sum(dyv * on * sg, axis=0, keepdims=True)
            dg_ref[:, sl] = (dyv * on * gn * dsg).astype(BF16)
            don = dyv * gn * sg
            do_ref[:, sl] = r * (don - on * jnp.mean(don * on, axis=-1, keepdims=True))

    return pl.pallas_call(
        body, grid=(lp // tr,),
        in_specs=[_rspec(tr, w), _rspec(tr, w), _rspec(tr, w, col_g // w), _fspec((1, w)),
                  _rspec(tr, w)],
        out_specs=[_rspec(tr, w), _rspec(tr, w), _fspec((1, w))],
        out_shape=[jax.ShapeDtypeStruct((lp, w), F32), jax.ShapeDtypeStruct((lp, w), BF16),
                   jax.ShapeDtypeStruct((1, w), F32)],
        compiler_params=_cp("arbitrary"), name=name)(o_f, o_b, proj, gain, dy)


HG_ROWS = 64
HG_HALVES = (1, 2, 4, 8, 16, 32)


def _hg_gates(zq, z, lbv):
    sq = _sigmoid(zq)
    s = _sigmoid(z)
    f = lbv + (1.0 - lbv) * s
    kk = (1.0 - lbv) * (1.0 - s)
    return zq * sq, sq, s, f, jnp.log(f), kk


def _block_cumsum(g, pos, suffix):
    x = g
    for k in HG_HALVES:
        if suffix:
            x = x + jnp.where(pos < HG_ROWS - k, pltpu.roll(x, HG_ROWS - k, 0), 0.0)
        else:
            x = x + jnp.where(pos >= k, pltpu.roll(x, k, 0), 0.0)
    return x


def _pair_levels(b, pos, reverse):
    out = []
    first = b
    for m in HG_HALVES:
        if m > 1:
            first = jnp.where((pos & (m - 1)) >= m // 2, pltpu.roll(first, m // 2, 0), first)
        nxt = pltpu.roll(first, HG_ROWS - m, 0)
        upper = (pos & (2 * m - 1)) >= m
        if reverse:
            eq = jnp.where(upper, 0.0, jnp.exp(b - nxt))
            ek = jnp.where(upper, jnp.exp(first - b), 0.0)
        else:
            eq = jnp.where(upper, jnp.exp(b - first), 0.0)
            ek = jnp.where(upper, 0.0, jnp.exp(nxt - b))
        out.append((eq, ek))
    return out


def _pair_masks(mask_ref):
    ri = lax.broadcasted_iota(jnp.int32, (HG_ROWS, HG_ROWS), 0)
    ci = lax.broadcasted_iota(jnp.int32, (HG_ROWS, HG_ROWS), 1)
    for i, m in enumerate(HG_HALVES):
        sh = m.bit_length()
        mask_ref[i] = jnp.where((ri >> sh) == (ci >> sh), 1.0, 0.0)


def _hg_scan_fwd(proj, lb, *, reverse, col_q, col_z, col_i, hh, name):
    lp = proj.shape[0]
    n_blocks = lp // HG_ROWS
    last = 0 if reverse else HG_ROWS - 1

    def body(q_ref, z_ref, i_ref, lb_ref, o_ref, st_ref, mask_ref):
        lbv = lb_ref[...]
        pos = lax.broadcasted_iota(jnp.int32, (HG_ROWS, 1), 0)
        ri = lax.broadcasted_iota(jnp.int32, (HG_ROWS, HG_ROWS), 0)
        ci = lax.broadcasted_iota(jnp.int32, (HG_ROWS, HG_ROWS), 1)
        _pair_masks(mask_ref)

        def block(bi, st):
            bb = (n_blocks - 1 - bi) if reverse else bi
            r0 = pl.multiple_of(bb * HG_ROWS, HG_ROWS)
            v16 = i_ref[pl.ds(r0, HG_ROWS), :].astype(BF16)
            qh, _, _, _, g, kk = _hg_gates(q_ref[pl.ds(r0, HG_ROWS), :],
                                           z_ref[pl.ds(r0, HG_ROWS), :], lbv)
            b = _block_cumsum(g, pos, reverse)
            bl = b[last:last + 1, :]
            qe = (qh * jnp.exp(b)).astype(BF16)
            kd = (kk * jnp.exp(bl - b)).astype(BF16)
            a = jnp.where(ri == ci, jnp.sum(qh * kk, axis=1, keepdims=True), 0.0)
            for i, (eq, ek) in enumerate(_pair_levels(b, pos, reverse)):
                a = a + mask_ref[i] * _nt((qh * eq).astype(BF16), (kk * ek).astype(BF16))
            st_ref[bb] = st
            o_ref[pl.ds(r0, HG_ROWS), :] = _nn(a.astype(BF16), v16) + _nt(qe, st.astype(BF16))
            return jnp.exp(bl) * st + _tn(v16, kd)

        lax.fori_loop(0, n_blocks, block, jnp.zeros((HG_DK, HG_DK), F32))

    cspec = lambda col: pl.BlockSpec((lp, HG_DK), lambda h: (0, col // HG_DK + h))
    return pl.pallas_call(
        body, grid=(hh,),
        in_specs=[cspec(col_q), cspec(col_z), cspec(col_i),
                  pl.BlockSpec((None, 1, HG_DK), lambda h: (h, 0, 0))],
        out_specs=[pl.BlockSpec((lp, HG_DK), lambda h: (0, h)),
                   pl.BlockSpec((None, n_blocks, HG_DK, HG_DK), lambda h: (h, 0, 0, 0))],
        out_shape=[jax.ShapeDtypeStruct((lp, hh * HG_DK), F32),
                   jax.ShapeDtypeStruct((hh, n_blocks, HG_DK, HG_DK), F32)],
        scratch_shapes=[pltpu.VMEM((len(HG_HALVES), HG_ROWS, HG_ROWS), F32)],
        compiler_params=_cp("parallel"), name=name)(proj, proj, proj, lb)


def _hg_scan_bwd(proj, lb, states, do, *, reverse, col_q, col_z, col_i, hh, name):
    lp = proj.shape[0]
    n_blocks = lp // HG_ROWS
    last = 0 if reverse else HG_ROWS - 1

    def body(q_ref, z_ref, i_ref, lb_ref, st_ref, do_ref, dq_ref, dz_ref, dv_ref, dlb_ref, mask_ref):
        lbv = lb_ref[...]
        pos = lax.broadcasted_iota(jnp.int32, (HG_ROWS, 1), 0)
        ri = lax.broadcasted_iota(jnp.int32, (HG_ROWS, HG_ROWS), 0)
        ci = lax.broadcasted_iota(jnp.int32, (HG_ROWS, HG_ROWS), 1)
        _pair_masks(mask_ref)

        def block(bi, carry):
            dst, dlb = carry
            bb = bi if reverse else (n_blocks - 1 - bi)
            r0 = pl.multiple_of(bb * HG_ROWS, HG_ROWS)
            zq = q_ref[pl.ds(r0, HG_ROWS), :]
            v16 = i_ref[pl.ds(r0, HG_ROWS), :].astype(BF16)
            do16 = do_ref[pl.ds(r0, HG_ROWS), :].astype(BF16)
            qh, sq, s, f, g, kk = _hg_gates(zq, z_ref[pl.ds(r0, HG_ROWS), :], lbv)
            b = _block_cumsum(g, pos, reverse)
            bl = b[last:last + 1, :]
            eb = jnp.exp(b)
            ebl = jnp.exp(bl - b)
            decay = jnp.exp(bl)
            qe16 = (qh * eb).astype(BF16)
            kd16 = (kk * ebl).astype(BF16)
            st = st_ref[bb]
            st16, dst16 = st.astype(BF16), dst.astype(BF16)
            same_row = ri == ci
            da = _nt(do16, v16)
            da_diag = jnp.sum(jnp.where(same_row, da, 0.0), axis=1, keepdims=True)
            dq_state = eb * _nn(do16, st16)
            dk_state = ebl * _nn(v16, dst16)
            dq = dq_state + da_diag * kk
            dk = dk_state + da_diag * qh
            dbl = (decay * jnp.sum(st * dst, axis=0, keepdims=True)
                   + jnp.sum(kk * dk_state, axis=0, keepdims=True))
            db = qh * dq_state - kk * dk_state + jnp.where(pos == last, dbl, 0.0)
            a = jnp.where(same_row, jnp.sum(qh * kk, axis=1, keepdims=True), 0.0)
            for i, (eq, ek) in enumerate(_pair_levels(b, pos, reverse)):
                same = mask_ref[i]
                q16, k16 = (qh * eq).astype(BF16), (kk * ek).astype(BF16)
                a = a + same * _nt(q16, k16)
                da16 = (same * da).astype(BF16)
                gq, gk = _nn(da16, k16), _tn(da16, q16)
                dq = dq + eq * gq
                dk = dk + ek * gk
                db = db + (q16.astype(F32) * gq - k16.astype(F32) * gk)
            dg = _block_cumsum(db, pos, not reverse)
            df = dg / f - dk
            dq_ref[pl.ds(r0, HG_ROWS), :] = dq * (sq + zq * sq * (1.0 - sq))
            dz_ref[pl.ds(r0, HG_ROWS), :] = (df * (1.0 - lbv) * s * (1.0 - s)).astype(BF16)
            dv_ref[pl.ds(r0, HG_ROWS), :] = _nt(kd16, dst16) + _tn(a.astype(BF16), do16)
            return (decay * dst + _tn(do16, qe16),
                    dlb + jnp.sum(df * (1.0 - s), axis=0, keepdims=True))

        _, dlb = lax.fori_loop(0, n_blocks, block,
                               (jnp.zeros((HG_DK, HG_DK), F32), jnp.zeros((1, HG_DK), F32)))
        dlb_ref[...] = dlb

    cspec = lambda col: pl.BlockSpec((lp, HG_DK), lambda h: (0, col // HG_DK + h))
    ospec = pl.BlockSpec((lp, HG_DK), lambda h: (0, h))
    sds = jax.ShapeDtypeStruct((lp, hh * HG_DK), F32)
    return pl.pallas_call(
        body, grid=(hh,),
        in_specs=[cspec(col_q), cspec(col_z), cspec(col_i),
                  pl.BlockSpec((None, 1, HG_DK), lambda h: (h, 0, 0)),
                  pl.BlockSpec((None, n_blocks, HG_DK, HG_DK), lambda h: (h, 0, 0, 0)),
                  ospec],
        out_specs=[ospec, ospec, ospec, pl.BlockSpec((None, 1, HG_DK), lambda h: (h, 0, 0))],
        out_shape=[sds, jax.ShapeDtypeStruct((lp, hh * HG_DK), BF16), sds,
                   jax.ShapeDtypeStruct((hh, 1, HG_DK), F32)],
        scratch_shapes=[pltpu.VMEM((len(HG_HALVES), HG_ROWS, HG_ROWS), F32)],
        compiler_params=_cp("parallel"), name=name)(proj, proj, proj, lb, states, do)


NA_HB = LANES // NA_HEAD_DIM
NA_G = 4
NA_U = NA_G + NA_WIN_H
NA_QN = NA_G * GRID_W
NA_KN = NA_U * GRID_W


def _na_table_index(pattern, a, j):
    if pattern == 0:
        return j - a + NA_WIN_H - 1 if j < NA_WIN_H else None
    if pattern == 2:
        return j - a - 1 if j >= NA_U - NA_WIN_H else None
    return j - a + NA_WIN_H // 2 - 1 if a <= j < a + NA_WIN_H else None


def _na_step_rows(pattern, t, rows):
    if pattern == 0:
        r0, us = 0, 0
    elif pattern == 2:
        r0, us = rows - NA_G, rows - NA_U
    else:
        r0 = NA_G * t
        us = r0 - NA_WIN_H // 2
    q0, k0 = N_META + GRID_W * r0, N_META + GRID_W * us
    if pattern == 1:
        q0, k0 = pl.multiple_of(q0, 16), pl.multiple_of(k0, 16)
    return q0, k0


def _na_fill_bias(tb_ref, bias_ref):
    neg = jnp.full((GRID_W, GRID_W), -1e30, F32)
    for h in range(NA_HB):
        for pattern in range(3):
            for a in range(NA_G):
                for j in range(NA_U):
                    idx = _na_table_index(pattern, a, j)
                    bias_ref[h, pattern, a * GRID_W:(a + 1) * GRID_W, j * GRID_W:(j + 1) * GRID_W] = (
                        neg if idx is None else tb_ref[h, idx])


def _na_steps(rows, step, carry):
    n_steps = rows // NA_G
    carry = step(0, 0, carry)
    carry = lax.fori_loop(1, n_steps - 1, functools.partial(step, 1), carry)
    return step(2, n_steps - 1, carry)


def _na_head_lanes():
    lane = lax.broadcasted_iota(jnp.int32, (1, LANES), 1)
    return [lane // NA_HEAD_DIM == h for h in range(NA_HB)]


def _na_only(mask, x):
    return jnp.where(mask, x, jnp.zeros_like(x))


def _na_stack(heads, x):
    return jnp.concatenate([_na_only(mask, x) for mask in heads], axis=0)


def _na_unstack(heads, y):
    rows = y.shape[0] // NA_HB
    out = y[0:rows]
    for h in range(1, NA_HB):
        out = jnp.where(heads[h], y[h * rows:(h + 1) * rows], out)
    return out


def _na_fwd(proj, tb, *, n_tok, nh, name):
    lp = proj.shape[0]
    dh, hb = NA_HEAD_DIM, NA_HB
    naw = nh * dh
    rows = n_tok // GRID_W
    scale = dh ** -0.5

    def body(q_ref, k_ref, v_ref, tb_ref, o_ref, lse_ref, q16_ref, k16_ref, v16_ref, bias_ref):
        o_ref[...] = jnp.zeros_like(o_ref)
        lse_ref[...] = jnp.zeros_like(lse_ref)
        q16_ref[...] = q_ref[...].astype(BF16)
        k16_ref[...] = k_ref[...].astype(BF16)
        v16_ref[...] = v_ref[...].astype(BF16)
        _na_fill_bias(tb_ref, bias_ref)
        heads = _na_head_lanes()
        km = k16_ref[0:N_META, :]
        vm = v16_ref[0:N_META, :]
        qm = q16_ref[0:N_META, :]
        o_m = None
        for h in range(hb):
            s = _nt(_na_only(heads[h], qm), km) * scale
            m = jnp.max(s, axis=1, keepdims=True)
            p = jnp.exp(s - m)
            l = jnp.sum(p, axis=1, keepdims=True)
            o_h = _nn(p.astype(BF16), vm) / l
            o_m = o_h if o_m is None else jnp.where(heads[h], o_h, o_m)
            lse_ref[h, 0:N_META, :] = m + jnp.log(l)
        o_ref[0:N_META, :] = o_m

        def step(pattern, t, carry):
            q0, k0 = _na_step_rows(pattern, t, rows)
            k16 = k16_ref[pl.ds(k0, NA_KN), :]
            v16 = v16_ref[pl.ds(k0, NA_KN), :]
            q2 = _na_stack(heads, q16_ref[pl.ds(q0, NA_QN), :])
            s = _nt(q2, k16) * scale + bias_ref[:, pattern].reshape(hb * NA_QN, NA_KN)
            sm = _nt(q2, km) * scale
            m = jnp.maximum(jnp.max(s, axis=1, keepdims=True), jnp.max(sm, axis=1, keepdims=True))
            p = jnp.exp(s - m)
            pm = jnp.exp(sm - m)
            l = jnp.sum(p, axis=1, keepdims=True) + jnp.sum(pm, axis=1, keepdims=True)
            o2 = (_nn(p.astype(BF16), v16) + _nn(pm.astype(BF16), vm)) / l
            o_ref[pl.ds(q0, NA_QN), :] = _na_unstack(heads, o2)
            lse2 = m + jnp.log(l)
            for h in range(hb):
                lse_ref[h, pl.ds(q0, NA_QN), :] = lse2[h * NA_QN:(h + 1) * NA_QN]
            return carry

        _na_steps(rows, step, 0)

    cblk = lambda col: pl.BlockSpec((lp, LANES), lambda g: (0, col // LANES + g))
    return pl.pallas_call(
        body, grid=(nh // hb,),
        in_specs=[cblk(0), cblk(naw), cblk(2 * naw),
                  pl.BlockSpec((hb, 2 * NA_WIN_H - 1, GRID_W, GRID_W), lambda g: (g, 0, 0, 0))],
        out_specs=[cblk(0), pl.BlockSpec((hb, lp, 1), lambda g: (g, 0, 0))],
        out_shape=[jax.ShapeDtypeStruct((lp, naw), F32), jax.ShapeDtypeStruct((nh, lp, 1), F32)],
        scratch_shapes=[pltpu.VMEM((lp, LANES), BF16)] * 3 + [pltpu.VMEM((hb, 3, NA_QN, NA_KN), F32)],
        compiler_params=_cp("parallel"), name=name)(proj, proj, proj, tb)


def _na_bwd(proj, tb, o, lse, do, *, n_tok, nh, name):
    lp = proj.shape[0]
    dh, hb = NA_HEAD_DIM, NA_HB
    naw = nh * dh
    rows = n_tok // GRID_W
    scale = dh ** -0.5

    def body(q_ref, k_ref, v_ref, tb_ref, o_ref, lse_ref, do_ref, dq_ref, dk_ref, dv_ref, dtb_ref,
             q16_ref, k16_ref, v16_ref, bias_ref):
        dq_ref[...] = jnp.zeros_like(dq_ref)
        dk_ref[...] = jnp.zeros_like(dk_ref)
        dv_ref[...] = jnp.zeros_like(dv_ref)
        dtb_ref[...] = jnp.zeros_like(dtb_ref)
        q16_ref[...] = q_ref[...].astype(BF16)
        k16_ref[...] = k_ref[...].astype(BF16)
        v16_ref[...] = v_ref[...].astype(BF16)
        _na_fill_bias(tb_ref, bias_ref)
        heads = _na_head_lanes()
        km = k16_ref[0:N_META, :]
        vm = v16_ref[0:N_META, :]
        qm = q16_ref[0:N_META, :]
        dom = do_ref[0:N_META, :]
        prod = dom * o_ref[0:N_META, :]
        dq_m = None
        dkm0 = jnp.zeros((N_META, LANES), F32)
        dvm0 = jnp.zeros((N_META, LANES), F32)
        for h in range(hb):
            q_h = _na_only(heads[h], qm)
            do_h = _na_only(heads[h], dom).astype(BF16)
            p = jnp.exp(_nt(q_h, km) * scale - lse_ref[h, 0:N_META, :])
            delta = jnp.sum(_na_only(heads[h], prod), axis=1, keepdims=True)
            ds = (p * (_nt(do_h, vm) - delta)).astype(BF16)
            dq_h = _nn(ds, km) * scale
            dq_m = dq_h if dq_m is None else jnp.where(heads[h], dq_h, dq_m)
            dkm0 = dkm0 + _tn(ds, q_h) * scale
            dvm0 = dvm0 + _tn(p.astype(BF16), do_h)
        dq_ref[0:N_META, :] = dq_m

        def step(pattern, t, carry):
            dkm, dvm = carry
            q0, k0 = _na_step_rows(pattern, t, rows)
            k16 = k16_ref[pl.ds(k0, NA_KN), :]
            v16 = v16_ref[pl.ds(k0, NA_KN), :]
            q2 = _na_stack(heads, q16_ref[pl.ds(q0, NA_QN), :])
            do2 = _na_stack(heads, do_ref[pl.ds(q0, NA_QN), :])
            do16 = do2.astype(BF16)
            ov = o_ref[pl.ds(q0, NA_QN), :]
            delta = jnp.sum(do2 * jnp.concatenate([ov] * hb, axis=0), axis=1, keepdims=True)
            lse = jnp.concatenate([lse_ref[h, pl.ds(q0, NA_QN), :] for h in range(hb)], axis=0)
            p = jnp.exp(_nt(q2, k16) * scale + bias_ref[:, pattern].reshape(hb * NA_QN, NA_KN)
                        - lse)
            pm = jnp.exp(_nt(q2, km) * scale - lse)
            ds = p * (_nt(do16, v16) - delta)
            dsm = (pm * (_nt(do16, vm) - delta)).astype(BF16)
            ds16 = ds.astype(BF16)
            dq2 = (_nn(ds16, k16) + _nn(dsm, km)) * scale
            dq_ref[pl.ds(q0, NA_QN), :] = _na_unstack(heads, dq2)
            dk_ref[pl.ds(k0, NA_KN), :] += _tn(ds16, q2) * scale
            dv_ref[pl.ds(k0, NA_KN), :] += _tn(p.astype(BF16), do16)
            for h in range(hb):
                for a in range(NA_G):
                    for j in range(NA_U):
                        idx = _na_table_index(pattern, a, j)
                        if idx is not None:
                            r = h * NA_QN + a * GRID_W
                            dtb_ref[h, idx] += ds[r:r + GRID_W, j * GRID_W:(j + 1) * GRID_W]
            return dkm + _tn(dsm, q2) * scale, dvm + _tn(pm.astype(BF16), do16)

        dkm, dvm = _na_steps(rows, step, (dkm0, dvm0))
        dk_ref[0:N_META, :] += dkm
        dv_ref[0:N_META, :] += dvm

    cblk = lambda col: pl.BlockSpec((lp, LANES), lambda g: (0, col // LANES + g))
    tbs = pl.BlockSpec((hb, 2 * NA_WIN_H - 1, GRID_W, GRID_W), lambda g: (g, 0, 0, 0))
    sds = jax.ShapeDtypeStruct((lp, naw), F32)
    return pl.pallas_call(
        body, grid=(nh // hb,),
        in_specs=[cblk(0), cblk(naw), cblk(2 * naw), tbs, cblk(0),
                  pl.BlockSpec((hb, lp, 1), lambda g: (g, 0, 0)), cblk(0)],
        out_specs=[cblk(0), cblk(0), cblk(0), tbs],
        out_shape=[sds, sds, sds, jax.ShapeDtypeStruct(tb.shape, F32)],
        scratch_shapes=[pltpu.VMEM((lp, LANES), BF16)] * 3 + [pltpu.VMEM((hb, 3, NA_QN, NA_KN), F32)],
        compiler_params=_cp("parallel"), name=name)(proj, proj, proj, tb, o, lse, do)


def _rpb_onehot():
    c = np.arange(GRID_W)[:, None]
    w = np.arange(GRID_W)[None, :]
    cs = np.clip(c - NA_WIN_W // 2, 0, GRID_W - NA_WIN_W)
    in_win = (w >= cs) & (w < cs + NA_WIN_W)
    dc = np.clip(w - c, -(NA_WIN_W - 1), NA_WIN_W - 1) + NA_WIN_W - 1
    oh = np.zeros((LANES, GRID_W * GRID_W), np.float32)
    flat = np.arange(GRID_W * GRID_W).reshape(GRID_W, GRID_W)
    oh[dc[in_win], flat[in_win]] = 1.0
    neg = np.where(in_win, 0.0, -1e30).astype(np.float32).reshape(1, -1)
    return oh, neg


def _assemble_dproj(dq_na, dk_na, dv_na, dq_f, dq_b, dz_f, dz_b, dv_f, dv_b, dg, dgn, dgh, *, name):
    lp, naw = dq_na.shape
    hgw = dq_f.shape[1]
    d = dgn.shape[1]
    cols = 3 * naw + 5 * hgw + 2 * d
    tr = _row_tile(lp, 3 * naw * 4 + 4 * hgw * 4 + 3 * hgw * 2 + 2 * d * 2 + cols * 2)

    def body(nq_ref, nk_ref, nv_ref, qf_ref, qb_ref, zf_ref, zb_ref, vf_ref, vb_ref, g_ref, gn_ref,
             gh_ref, o_ref):
        o_ref[:, 0:naw] = nq_ref[...].astype(BF16)
        o_ref[:, naw:2 * naw] = nk_ref[...].astype(BF16)
        o_ref[:, 2 * naw:3 * naw] = nv_ref[...].astype(BF16)
        c0 = 3 * naw
        o_ref[:, c0:c0 + hgw] = (qf_ref[...] + qb_ref[...]).astype(BF16)
        o_ref[:, c0 + hgw:c0 + 2 * hgw] = zf_ref[...]
        o_ref[:, c0 + 2 * hgw:c0 + 3 * hgw] = zb_ref[...]
        o_ref[:, c0 + 3 * hgw:c0 + 4 * hgw] = (vf_ref[...] + vb_ref[...]).astype(BF16)
        o_ref[:, c0 + 4 * hgw:c0 + 5 * hgw] = g_ref[...]
        o_ref[:, c0 + 5 * hgw:c0 + 5 * hgw + d] = gn_ref[...]
        o_ref[:, c0 + 5 * hgw + d:] = gh_ref[...]

    hg, na = _rspec(tr, hgw), _rspec(tr, naw)
    return pl.pallas_call(
        body, grid=(lp // tr,),
        in_specs=[na, na, na, hg, hg, hg, hg, hg, hg, hg, _rspec(tr, d), _rspec(tr, d)],
        out_specs=_rspec(tr, cols),
        out_shape=jax.ShapeDtypeStruct((lp, cols), BF16),
        compiler_params=_cp("parallel"), name=name)(dq_na, dk_na, dv_na, dq_f, dq_b, dz_f, dz_b,
                                                    dv_f, dv_b, dg, dgn, dgh)


GROUP_STEPS = 8


def _group_tiles(rows, align):
    steps = GROUP_STEPS if all(r % (GROUP_STEPS * align) == 0 for r in rows) else 1
    return steps, [r // steps for r in rows]


def _adamw(ws, gs, ms, vs, *, name):
    n = len(ws)
    steps, trs = _group_tiles([w.shape[0] for w in ws], 8)

    def body(*refs):
        for i in range(n):
            w_ref, g_ref, m_ref, v_ref = refs[4 * i:4 * i + 4]
            d_ref, mo_ref, vo_ref, go_ref = refs[4 * n + 4 * i:4 * n + 4 * i + 4]
            gv = g_ref[...]
            go_ref[...] = gv
            mn = ADAM_B1 * m_ref[...] + (1.0 - ADAM_B1) * gv
            vn = ADAM_B2 * v_ref[...] + (1.0 - ADAM_B2) * (gv * gv)
            m_hat = mn / (1.0 - ADAM_B1 ** ADAM_STEP)
            v_hat = vn / (1.0 - ADAM_B2 ** ADAM_STEP)
            d_ref[...] = -ADAM_LR * (m_hat / (jnp.sqrt(v_hat) + ADAM_EPS) + ADAM_WD * w_ref[...])
            mo_ref[...] = mn
            vo_ref[...] = vn

    specs = [_rspec(tr, w.shape[1]) for tr, w in zip(trs, ws)]
    out = pl.pallas_call(
        body, grid=(steps,),
        in_specs=[s for s in specs for _ in range(4)],
        out_specs=[s for s in specs for _ in range(4)],
        out_shape=[jax.ShapeDtypeStruct(w.shape, F32) for w in ws for _ in range(4)],
        compiler_params=_cp("parallel"), name=name)(*[a for q in zip(ws, gs, ms, vs) for a in q])
    return [tuple(out[4 * i:4 * i + 4]) for i in range(n)]


def _local_step(x, tgt, meta, first_weight, rest_weights, g_mix, g_mlp, g_fin, hg_gain, rpb, lb,
                early_grads=None, mid_grads=None, late_grad=None, rest_landed=None,
                last_grads=None):
    n_tok, d = x.shape
    hgw = hg_gain.shape[1]
    nh, hh = rpb.shape[0], hgw // HG_DK
    naw = nh * NA_HEAD_DIM
    l_real = N_META + n_tok
    lp = -(-l_real // ROW_ALIGN) * ROW_ALIGN
    col_qhg = 3 * naw
    col_zf, col_zb, col_i, col_g = (col_qhg + hgw, col_qhg + 2 * hgw, col_qhg + 3 * hgw,
                                    col_qhg + 4 * hgw)
    col_gate = col_qhg + 5 * hgw

    oh_np, neg_np = _rpb_onehot()
    oh = jnp.asarray(oh_np)
    rpb_p = jnp.pad(rpb.reshape(nh * (2 * NA_WIN_H - 1), 2 * NA_WIN_W - 1),
                    ((0, 0), (0, LANES - (2 * NA_WIN_W - 1))))
    tb = _matmul(rpb_p, oh, tm=rpb_p.shape[0], tn=512, tk=LANES, precision=HIGHEST,
                 name="rpb_expand")
    tb = (tb + jnp.asarray(neg_np)).reshape(nh, 2 * NA_WIN_H - 1, GRID_W, GRID_W)

    h0, a, tgt_p = _embed_norm(x, tgt, meta, g_mix, lp=lp, name="norm_mix")
    w_in = first_weight((a, tb))
    proj = _matmul(a, w_in, tm=lp, name="mm_in")
    o_na, lse = _na_fwd(proj, tb, n_tok=n_tok, nh=nh, name="na_fwd")
    lb_f = lb[0].reshape(hh, 1, HG_DK)
    lb_b = lb[1].reshape(hh, 1, HG_DK)
    scan_kw = dict(col_q=col_qhg, col_i=col_i, hh=hh)
    o_f, st_f = _hg_scan_fwd(proj, lb_f, reverse=False, col_z=col_zf, name="hg_scan_f", **scan_kw)
    token = rest_landed(o_f) if rest_landed else None
    lb_b_late = lb_b if token is None else lb_b + token[0:1, 0:1]
    o_b, st_b = _hg_scan_fwd(proj, lb_b_late, reverse=True, col_z=col_zb, name="hg_scan_b",
                             **scan_kw)
    o_hg = _hg_out(o_f, o_b, proj, hg_gain, col_g=col_g, name="hg_out")
    w_na, w_hg, w_o, w_up, w_down = rest_weights(o_hg)
    y_na = _matmul(o_na, w_na, name="mm_na_out", out_dtype=BF16)
    gates = ((proj, col_gate), (proj, col_gate + d))

    def mix_gates(acc, gn, gh, yn):
        return acc, _sigmoid(gn) * yn + _sigmoid(gh) * acc

    def mix_gates_bwd(dmix, gn, gh, yn, yh):
        sn, sh = _sigmoid(gn), _sigmoid(gh)
        return dmix * sn, dmix * sh, dmix * yn * sn * (1.0 - sn), dmix * yh * sh * (1.0 - sh)

    y_hg, mix = _matmul(o_hg, w_hg, name="mm_hg_out", epilogue=mix_gates,
                        tiles=(*gates, (y_na, 0)), out_dtypes=(BF16, BF16))
    t1 = _matmul(mix, w_o, name="mm_o")
    h1, mlp_in = _residual_norm(h0, t1, g_mlp, name="resid_norm_mlp")
    u, act = _matmul(mlp_in, w_up, name="mm_up", out_dtypes=(BF16, BF16),
                     epilogue=lambda acc: (acc, jnp.square(jnp.maximum(acc, 0.0))))
    t2 = _matmul(act, w_down, name="mm_down")
    dh2, dh2_16, loss, dg_fin = _final_loss(h1, t2, g_fin, tgt_p, n_tok=n_tok, name="final_loss")

    (du,) = _matmul(dh2_16, w_down, tb=True, name="mm_down_dx", tiles=((u, 0),),
                    out_dtypes=(BF16,),
                    epilogue=lambda acc, uv: (acc * 2.0 * jnp.maximum(uv, 0.0),))
    dw_down = _matmul(act, dh2_16, ta=True, name="mm_down_dw")
    dm = _matmul(du, w_up, tb=True, name="mm_up_dx")
    dw_up = _matmul(mlp_in, du, ta=True, name="mm_up_dw")
    dh1, dh1_16, dg_mlp = _rmsnorm_bwd_add(h1, g_mlp, dm, dh2, name="norm_mlp_bwd")
    dy_na, dy_hg, dgn, dgh = _matmul(dh1_16, w_o, tb=True, name="mm_o_dx", epilogue=mix_gates_bwd,
                                     tiles=(*gates, (y_na, 0), (y_hg, 0)), out_dtypes=(BF16,) * 4)
    dw_o = _matmul(mix, dh1_16, ta=True, name="mm_o_dw")
    do_na = _matmul(dy_na, w_na, tb=True, name="mm_na_out_dx")
    dw_na = _matmul(o_na, dy_na, ta=True, name="mm_na_out_dw")
    do_hg = _matmul(dy_hg, w_hg, tb=True, name="mm_hg_out_dx")
    dw_hg = _matmul(o_hg, dy_hg, ta=True, name="mm_hg_out_dw")
    token = early_grads([dw_na, dw_hg, dw_o, dw_up, dw_down]) if early_grads else None
    if token is not None:
        hg_gain = hg_gain + token[0:1, 0:1]
    d_o, dg_hg, d_gain = _hg_out_bwd(o_f, o_b, proj, hg_gain, do_hg, col_g=col_g, name="hg_out_bwd")
    dq_f, dz_f, dv_f, dlb_f = _hg_scan_bwd(proj, lb_f, st_f, d_o, reverse=False, col_z=col_zf,
                                           name="hg_scan_f_bwd", **scan_kw)
    token = mid_grads(dq_f) if mid_grads else None
    lb_b_late = lb_b if token is None else lb_b + token[0:1, 0:1]
    dq_b, dz_b, dv_b, dlb_b = _hg_scan_bwd(proj, lb_b_late, st_b, d_o, reverse=True, col_z=col_zb,
                                           name="hg_scan_b_bwd", **scan_kw)
    dq_na, dk_na, dv_na, dtb = _na_bwd(proj, tb, o_na, lse, do_na, n_tok=n_tok, nh=nh, name="na_bwd")
    dproj = _assemble_dproj(dq_na, dk_na, dv_na, dq_f, dq_b, dz_f, dz_b, dv_f, dv_b, dg_hg, dgn,
                            dgh, name="assemble_dproj")
    dw_in = _matmul(a, dproj, ta=True, name="mm_in_dw")
    token = late_grad(dw_in) if late_grad else None
    da = _matmul(dproj, w_in, tb=True, name="mm_in_dx", after=token)
    token = last_grads(da) if last_grads else None
    g_mix_late = g_mix if token is None else g_mix + token[0:1, 0:1]
    dx, dmeta, dg_mix = _rmsnorm_bwd_tokens(h0, g_mix_late, da, dh1, n_tok=n_tok,
                                            name="norm_mix_bwd")
    d_rpb = _matmul(dtb.reshape(nh * (2 * NA_WIN_H - 1), GRID_W * GRID_W), oh, tb=True,
                    tm=nh * (2 * NA_WIN_H - 1), tn=LANES, tk=1024, precision=HIGHEST,
                    name="rpb_reduce")
    d_lb = jnp.concatenate([dlb_f.reshape(1, hgw), dlb_b.reshape(1, hgw)], axis=0)
    return (loss, dx, dmeta, dw_in, dw_na, dw_hg, dw_o, dw_up, dw_down,
            dg_mix, dg_mlp, dg_fin, d_gain, d_rpb, d_lb)


N_CHIPS = 4
N_DEV = 8
ANY = pl.BlockSpec(memory_space=pl.ANY)


def _place():
    x, y, c = lax.axis_index("x"), lax.axis_index("y"), lax.axis_index("c")
    others = []
    for j in (1, 2, 3):
        tx = (1 - x) if (j >> 1) else x
        ty = (1 - y) if (j & 1) else y
        others.append((tx, ty))
    return x, y, c, others


def _piece(ref, axis, k, half, rh, cs):
    if axis == 1:
        return ref.at[pl.ds(pl.multiple_of(half * rh, 16), rh), pl.ds(pl.multiple_of(k * cs, LANES), cs)]
    return ref.at[pl.ds(pl.multiple_of(k * 2 * rh + half * rh, 16), rh), :]


def _cast_into_full(shards, axes, place, *, name):
    n = len(shards)
    steps, trs = _group_tiles([s.shape[0] for s in shards], 16)

    def body(p_ref, *refs):
        for i in range(n):
            refs[n + i][...] = refs[i][...].astype(BF16)

    def out_spec(tr, cs, axis):
        if axis == 1:
            return pl.BlockSpec((tr, cs), lambda i, p_ref: (i, p_ref[0]))
        return pl.BlockSpec((tr, cs), lambda i, p_ref: (p_ref[0] * steps + i, 0))

    return pl.pallas_call(
        body,
        grid_spec=pltpu.PrefetchScalarGridSpec(
            num_scalar_prefetch=1, grid=(steps,),
            in_specs=[pl.BlockSpec((tr, s.shape[1]), lambda i, p_ref: (i, 0))
                      for tr, s in zip(trs, shards)],
            out_specs=[out_spec(tr, s.shape[1], ax) for tr, s, ax in zip(trs, shards, axes)]),
        out_shape=[jax.ShapeDtypeStruct((s.shape[0], s.shape[1] * N_CHIPS) if ax == 1
                                        else (s.shape[0] * N_CHIPS, s.shape[1]), BF16)
                   for s, ax in zip(shards, axes)],
        compiler_params=_cp("parallel"), name=name)(place, *shards)


HBM_SPEC = pl.BlockSpec(memory_space=pltpu.HBM)
SEM_SPEC = pl.BlockSpec(memory_space=pltpu.SEMAPHORE)
SPLIT_COPY = pltpu.CompilerParams(has_side_effects=pltpu.SideEffectType.DATAFLOW_SIDE_EFFECTING)
TOKEN = jax.ShapeDtypeStruct((8, LANES), F32)


def _geo(fulls, axes):
    out = []
    for f, ax in zip(fulls, axes):
        r, cs = (f.shape[0], f.shape[1] // N_CHIPS) if ax == 1 else (f.shape[0] // N_CHIPS, f.shape[1])
        out.append((ax, r // 2, cs))
    return out


def _gather_copies(refs, geo, send_sems, recv_sems):
    x, y, c, others = _place()
    chip = 2 * x + y
    cps = []
    for i, (ax, rh, cs) in enumerate(geo):
        mine = _piece(refs[i], ax, chip, c, rh, cs)
        for j, (tx, ty) in enumerate(others):
            cps.append(pltpu.make_async_remote_copy(
                src_ref=mine, dst_ref=mine, send_sem=send_sems.at[3 * i + j],
                recv_sem=recv_sems.at[3 * i + j], device_id=(tx, ty, c), device_id_type=MESH))
    return cps


def _allgather_start(fulls, axes, after, *, name):
    n = len(fulls)
    geo = _geo(fulls, axes)

    def body(*refs):
        w_refs = refs[:n]
        send_sems, recv_sems = refs[n + 1], refs[n + 2]
        token = refs[2 * n + 3]
        for cp in _gather_copies(w_refs, geo, send_sems, recv_sems):
            cp.start()
        token[...] = jnp.zeros_like(token)

    out = pl.pallas_call(
        body, name=name,
        out_shape=(pltpu.SemaphoreType.DMA((3 * n,)), pltpu.SemaphoreType.DMA((3 * n,)),
                   *[pltpu.HBM(f.shape, f.dtype) for f in fulls], TOKEN),
        in_specs=[HBM_SPEC] * n + [ANY],
        out_specs=(SEM_SPEC, SEM_SPEC, *[HBM_SPEC] * n, pl.BlockSpec(memory_space=pltpu.VMEM)),
        input_output_aliases={i: 2 + i for i in range(n)},
        compiler_params=SPLIT_COPY,
    )(*[pltpu.with_memory_space_constraint(f, pltpu.HBM) for f in fulls], after)
    return out[0], out[1], list(out[2:2 + n]), out[2 + n]


def _allgather_wait(send_sems, recv_sems, fulls, axes, after, *, name):
    n = len(fulls)
    geo = _geo(fulls, axes)
    afters = tuple(after) if isinstance(after, (tuple, list)) else (after,)

    def body(*refs):
        w_refs = refs[:n]
        for cp in _gather_copies(w_refs, geo, refs[n], refs[n + 1]):
            cp.wait_send()
            cp.wait_recv()

    return list(pl.pallas_call(
        body, name=name,
        out_shape=[pltpu.HBM(f.shape, f.dtype) for f in fulls],
        in_specs=[HBM_SPEC] * n + [SEM_SPEC, SEM_SPEC] + [ANY] * len(afters),
        out_specs=[HBM_SPEC] * n,
        input_output_aliases={i: i for i in range(n)},
        compiler_params=SPLIT_COPY,
    )(*fulls, send_sems, recv_sems, *afters))


def _allgather_forward(fulls, axes, *, name):
    n = len(fulls)
    geo = _geo(fulls, axes)

    def body(*refs):
        o_refs = refs[n:2 * n]
        send_sems, recv_sems = refs[2 * n:]
        x, y, c, others = _place()

        def rcopy(i, j, half, to):
            ax, rh, cs = geo[i]
            ref = _piece(o_refs[i], ax, 2 * others[j][0] + others[j][1], half, rh, cs)
            return pltpu.make_async_remote_copy(
                src_ref=ref, dst_ref=ref, send_sem=send_sems.at[3 * i + j],
                recv_sem=recv_sems.at[3 * i + j], device_id=to, device_id_type=MESH)

        cps = [rcopy(i, j, c, (x, y, 1 - c)) for i in range(n) for j in range(3)]
        for cp in cps:
            cp.start()
        for i in range(n):
            for j in range(3):
                rcopy(i, j, 1 - c, (x, y, c)).wait_recv()
        for cp in cps:
            cp.wait_send()

    return list(pl.pallas_call(
        body, in_specs=[ANY] * n, out_specs=[ANY] * n,
        out_shape=[jax.ShapeDtypeStruct(f.shape, f.dtype) for f in fulls],
        input_output_aliases={i: i for i in range(n)},
        scratch_shapes=[pltpu.SemaphoreType.DMA((3 * n,)), pltpu.SemaphoreType.DMA((3 * n,))],
        name=name)(*fulls))


def _forward_copies(refs, geo, send_sems, recv_sems):
    x, y, c, others = _place()
    cps = []
    for i, (ax, rh, cs) in enumerate(geo):
        for j, (tx, ty) in enumerate(others):
            ref = _piece(refs[i], ax, 2 * tx + ty, c, rh, cs)
            cps.append(pltpu.make_async_remote_copy(
                src_ref=ref, dst_ref=ref, send_sem=send_sems.at[3 * i + j],
                recv_sem=recv_sems.at[3 * i + j], device_id=(x, y, 1 - c), device_id_type=MESH))
    return cps


def _allgather_forward_start(fulls, axes, *, name):
    n = len(fulls)
    geo = _geo(fulls, axes)

    def body(*refs):
        token = refs[2 * n + 2]
        for cp in _forward_copies(refs[:n], geo, refs[n], refs[n + 1]):
            cp.start()
        token[...] = jnp.zeros_like(token)

    out = pl.pallas_call(
        body, name=name,
        out_shape=(pltpu.SemaphoreType.DMA((3 * n,)), pltpu.SemaphoreType.DMA((3 * n,)),
                   *[pltpu.HBM(f.shape, f.dtype) for f in fulls], TOKEN),
        in_specs=[HBM_SPEC] * n,
        out_specs=(SEM_SPEC, SEM_SPEC, *[HBM_SPEC] * n, pl.BlockSpec(memory_space=pltpu.VMEM)),
        input_output_aliases={i: 2 + i for i in range(n)},
        compiler_params=SPLIT_COPY,
    )(*fulls)
    return out[0], out[1], list(out[2:2 + n]), out[2 + n]


def _allgather_forward_wait(send_sems, recv_sems, fulls, axes, after, *, name):
    n = len(fulls)
    geo = _geo(fulls, axes)

    def body(*refs):
        for cp in _forward_copies(refs[:n], geo, refs[n], refs[n + 1]):
            cp.wait_send()
            cp.wait_recv()

    return list(pl.pallas_call(
        body, name=name,
        out_shape=[pltpu.HBM(f.shape, f.dtype) for f in fulls],
        in_specs=[HBM_SPEC] * n + [SEM_SPEC, SEM_SPEC, ANY],
        out_specs=[HBM_SPEC] * n,
        input_output_aliases={i: i for i in range(n)},
        compiler_params=SPLIT_COPY,
    )(*fulls, send_sems, recv_sems, after))


def _chip_copies(blk_ref, land_ref, send_sems, recv_sems):
    x, y, c, others = _place()
    return [pltpu.make_async_remote_copy(
        src_ref=blk_ref, dst_ref=land_ref.at[2 * x + y], send_sem=send_sems.at[j],
        recv_sem=recv_sems.at[j], device_id=(tx, ty, c), device_id_type=MESH)
        for j, (tx, ty) in enumerate(others)]


def _chip_exchange_start(blk, *, name):
    land = pltpu.with_memory_space_constraint(lax.empty((N_CHIPS, *blk.shape), blk.dtype), pltpu.HBM)

    def body(blk_ref, land_ref, send_sems, recv_sems, blk_out, land_out, token):
        for cp in _chip_copies(blk_ref, land_ref, send_sems, recv_sems):
            cp.start()
        token[...] = jnp.zeros_like(token)

    return pl.pallas_call(
        body, name=name,
        out_shape=(pltpu.SemaphoreType.DMA((3,)), pltpu.SemaphoreType.DMA((3,)),
                   pltpu.HBM(blk.shape, blk.dtype), pltpu.HBM(land.shape, land.dtype), TOKEN),
        in_specs=[HBM_SPEC] * 2,
        out_specs=(SEM_SPEC, SEM_SPEC, HBM_SPEC, HBM_SPEC, pl.BlockSpec(memory_space=pltpu.VMEM)),
        input_output_aliases={0: 2, 1: 3},
        compiler_params=SPLIT_COPY,
    )(pltpu.with_memory_space_constraint(blk, pltpu.HBM), land)


def _chip_exchange_wait(send_sems, recv_sems, blk, land, after, *, name):
    def body(blk_ref, land_ref, send_sems, recv_sems, after_ref, blk_out, land_out):
        for cp in _chip_copies(blk_ref, land_ref, send_sems, recv_sems):
            cp.wait_send()
            cp.wait_recv()

    return pl.pallas_call(
        body, name=name,
        out_shape=[pltpu.HBM(blk.shape, blk.dtype), pltpu.HBM(land.shape, land.dtype)],
        in_specs=[HBM_SPEC] * 2 + [SEM_SPEC, SEM_SPEC, ANY],
        out_specs=[HBM_SPEC] * 2,
        input_output_aliases={0: 0, 1: 1},
        compiler_params=SPLIT_COPY,
    )(blk, land, send_sems, recv_sems, after)[1]


def _scatter_geo(parts, axes):
    out = []
    for p, ax in zip(parts, axes):
        _, rh, cols = p.shape
        out.append((ax, rh, cols // N_CHIPS if ax == 1 else cols))
    return out


def _scatter_copies(p_refs, q_refs, geo, send_sems, recv_sems):
    x, y, c, others = _place()
    chip = 2 * x + y
    cps = []
    for i, (ax, rh, cw) in enumerate(geo):
        for j, (tx, ty) in enumerate(others):
            k = 2 * tx + ty
            src = (p_refs[i].at[0, :, pl.ds(pl.multiple_of(k * cw, LANES), cw)] if ax == 1
                   else p_refs[i].at[k])
            cps.append(pltpu.make_async_remote_copy(
                src_ref=src, dst_ref=q_refs[i].at[chip], send_sem=send_sems.at[3 * i + j],
                recv_sem=recv_sems.at[3 * i + j], device_id=(tx, ty, c), device_id_type=MESH))
    return cps


def _scatter_start(parts, axes, *, name):
    n = len(parts)
    geo = _scatter_geo(parts, axes)
    slots = [pltpu.HBM((N_CHIPS, rh, cw), p.dtype) for p, (_, rh, cw) in zip(parts, geo)]

    def body(*refs):
        p_refs, q_refs = refs[:n], refs[n:2 * n]
        send_sems, recv_sems = refs[2 * n], refs[2 * n + 1]
        token = refs[4 * n + 2]
        for cp in _scatter_copies(p_refs, q_refs, geo, send_sems, recv_sems):
            cp.start()
        token[...] = jnp.zeros_like(token)

    land = [pltpu.with_memory_space_constraint(lax.empty(s.inner_aval.shape, s.inner_aval.dtype), pltpu.HBM)
            for s in slots]
    out = pl.pallas_call(
        body, name=name,
        out_shape=(pltpu.SemaphoreType.DMA((3 * n,)), pltpu.SemaphoreType.DMA((3 * n,)),
                   *[pltpu.HBM(p.shape, p.dtype) for p in parts], *slots, TOKEN),
        in_specs=[HBM_SPEC] * (2 * n),
        out_specs=(SEM_SPEC, SEM_SPEC, *[HBM_SPEC] * (2 * n), pl.BlockSpec(memory_space=pltpu.VMEM)),
        input_output_aliases={i: 2 + i for i in range(2 * n)},
        compiler_params=SPLIT_COPY,
    )(*[pltpu.with_memory_space_constraint(p, pltpu.HBM) for p in parts], *land)
    return out[0], out[1], list(out[2:2 + n]), list(out[2 + n:2 + 2 * n]), out[2 + 2 * n]


def _scatter_wait(send_sems, recv_sems, parts, slots, axes, after, *, name):
    n = len(parts)
    geo = _scatter_geo(parts, axes)

    def body(*refs):
        p_refs, q_refs = refs[:n], refs[n:2 * n]
        for cp in _scatter_copies(p_refs, q_refs, geo, refs[2 * n], refs[2 * n + 1]):
            cp.wait_send()
            cp.wait_recv()

    out = pl.pallas_call(
        body, name=name,
        out_shape=[pltpu.HBM(a.shape, a.dtype) for a in (*parts, *slots)],
        in_specs=[HBM_SPEC] * (2 * n) + [SEM_SPEC, SEM_SPEC, ANY],
        out_specs=[HBM_SPEC] * (2 * n),
        input_output_aliases={i: i for i in range(2 * n)},
        compiler_params=SPLIT_COPY,
    )(*parts, *slots, send_sems, recv_sems, after)
    return list(out[:n]), list(out[n:])


def _swap_copies(g_refs, r_refs, shapes, send_sems, recv_sems):
    x, y, c, _ = _place()
    cps = []
    for i, shape in enumerate(shapes):
        rh = shape[1] // 2
        src = g_refs[i].at[:, pl.ds(pl.multiple_of((1 - c) * rh, 16), rh), :]
        cps.append(pltpu.make_async_remote_copy(
            src_ref=src, dst_ref=r_refs[i], send_sem=send_sems.at[i], recv_sem=recv_sems.at[i],
            device_id=(x, y, 1 - c), device_id_type=MESH))
    return cps


def _sibling_swap_start(grads, *, name):
    n = len(grads)
    shapes = [g.shape for g in grads]
    lands = [pltpu.HBM((s[0], s[1] // 2, s[2]), g.dtype) for s, g in zip(shapes, grads)]

    def body(*refs):
        g_refs, r_refs = refs[:n], refs[n:2 * n]
        token = refs[4 * n + 2]
        for cp in _swap_copies(g_refs, r_refs, shapes, refs[2 * n], refs[2 * n + 1]):
            cp.start()
        token[...] = jnp.zeros_like(token)

    land = [pltpu.with_memory_space_constraint(lax.empty(s.inner_aval.shape, s.inner_aval.dtype), pltpu.HBM)
            for s in lands]
    out = pl.pallas_call(
        body, name=name,
        out_shape=(pltpu.SemaphoreType.DMA((n,)), pltpu.SemaphoreType.DMA((n,)),
                   *[pltpu.HBM(g.shape, g.dtype) for g in grads], *lands, TOKEN),
        in_specs=[HBM_SPEC] * (2 * n),
        out_specs=(SEM_SPEC, SEM_SPEC, *[HBM_SPEC] * (2 * n), pl.BlockSpec(memory_space=pltpu.VMEM)),
        input_output_aliases={i: 2 + i for i in range(2 * n)},
        compiler_params=SPLIT_COPY,
    )(*[pltpu.with_memory_space_constraint(g, pltpu.HBM) for g in grads], *land)
    return out[0], out[1], list(out[2:2 + n]), list(out[2 + n:2 + 2 * n]), out[2 + 2 * n]


def _sibling_swap_wait(send_sems, recv_sems, grads, lands, after, *, name):
    n = len(grads)
    shapes = [g.shape for g in grads]

    def body(*refs):
        g_refs, r_refs = refs[:n], refs[n:2 * n]
        for cp in _swap_copies(g_refs, r_refs, shapes, refs[2 * n], refs[2 * n + 1]):
            cp.wait_send()
            cp.wait_recv()

    out = pl.pallas_call(
        body, name=name,
        out_shape=[pltpu.HBM(a.shape, a.dtype) for a in (*grads, *lands)],
        in_specs=[HBM_SPEC] * (2 * n) + [SEM_SPEC, SEM_SPEC, ANY],
        out_specs=[HBM_SPEC] * (2 * n),
        input_output_aliases={i: i for i in range(2 * n)},
        compiler_params=SPLIT_COPY,
    )(*grads, *lands, send_sems, recv_sems, after)
    return list(out[:n]), list(out[n:])


def _pair_add(g3s, rxs, place, *, out_dtype, name):
    n = len(g3s)
    steps, trs = _group_tiles([g.shape[1] // 2 for g in g3s], 16)

    def body(p_ref, *refs):
        for i in range(n):
            refs[2 * n + i][...] = (refs[2 * i][...] + refs[2 * i + 1][...]).astype(out_dtype)

    in_specs, out_specs = [], []
    for g, tr in zip(g3s, trs):
        blk = (g.shape[0], tr, g.shape[2])
        in_specs += [pl.BlockSpec(blk, lambda i, p_ref: (0, p_ref[1] * steps + i, 0)),
                     pl.BlockSpec(blk, lambda i, p_ref: (0, i, 0))]
        out_specs.append(pl.BlockSpec(blk, lambda i, p_ref: (0, i, 0)))
    return pl.pallas_call(
        body,
        grid_spec=pltpu.PrefetchScalarGridSpec(
            num_scalar_prefetch=1, grid=(steps,), in_specs=in_specs, out_specs=out_specs),
        out_shape=[jax.ShapeDtypeStruct((g.shape[0], g.shape[1] // 2, g.shape[2]), out_dtype)
                   for g in g3s],
        compiler_params=_cp("parallel"), name=name)(place, *[a for q in zip(g3s, rxs) for a in q])


def _sum_slots(q, *, name):
    ns, rows, cols = q.shape
    tr = next(t for t in (128, 64, 32, 16, 8) if rows % t == 0)

    def body(q_ref, o_ref):
        acc = q_ref[0].astype(F32)
        for k in range(1, ns):
            acc = acc + q_ref[k].astype(F32)
        o_ref[...] = acc

    return pl.pallas_call(
        body, grid=(rows // tr,),
        in_specs=[pl.BlockSpec((ns, tr, cols), lambda i: (0, i, 0))],
        out_specs=_rspec(tr, cols),
        out_shape=jax.ShapeDtypeStruct((rows, cols), F32),
        compiler_params=_cp("parallel"), name=name)(q)


def _sum_chips(qs, ps, place, axes, *, name):
    n = len(qs)
    per = N_CHIPS + 1
    steps, trs = _group_tiles([q.shape[1] for q in qs], 16)

    def body(p_ref, *refs):
        chip = p_ref[0]
        for i in range(n):
            q_refs, own_ref = refs[per * i:per * i + N_CHIPS], refs[per * i + N_CHIPS]
            acc = jnp.where(chip == 0, own_ref[...], q_refs[0][...]).astype(F32)
            for k in range(1, N_CHIPS):
                acc = acc + jnp.where(chip == k, own_ref[...], q_refs[k][...]).astype(F32)
            refs[per * n + i][...] = acc

    def slot_spec(k, tr, cw):
        return pl.BlockSpec((None, tr, cw),
                            lambda i, p_ref: (jnp.where(p_ref[0] == k, (k + 1) % N_CHIPS, k), i, 0))

    in_specs, out_specs, operands = [], [], []
    for q, p, ax, tr in zip(qs, ps, axes, trs):
        cw = q.shape[2]
        in_specs += [slot_spec(k, tr, cw) for k in range(N_CHIPS)]
        in_specs.append(pl.BlockSpec((None, tr, cw), (lambda i, p_ref: (0, i, p_ref[0])) if ax == 1
                                     else (lambda i, p_ref: (p_ref[0], i, 0))))
        out_specs.append(pl.BlockSpec((tr, cw), lambda i, p_ref: (p_ref[1] * steps + i, 0)))
        operands += [q] * N_CHIPS + [p]
    return pl.pallas_call(
        body,
        grid_spec=pltpu.PrefetchScalarGridSpec(
            num_scalar_prefetch=1, grid=(steps,), in_specs=in_specs, out_specs=out_specs),
        out_shape=[jax.ShapeDtypeStruct((2 * q.shape[1], q.shape[2]), F32) for q in qs],
        compiler_params=_cp("parallel"), name=name)(place, *operands)


def _sibling_share(shards, *, name):
    n = len(shards)

    def body(*refs):
        o_refs = refs[n:2 * n]
        send_sems, recv_sems = refs[2 * n:]
        x, y, c, _ = _place()
        cps = []
        for i in range(n):
            rh = shards[i].shape[0] // 2
            mine = o_refs[i].at[pl.ds(pl.multiple_of(c * rh, 8), rh), :]
            cp = pltpu.make_async_remote_copy(
                src_ref=mine, dst_ref=mine, send_sem=send_sems.at[i], recv_sem=recv_sems.at[i],
                device_id=(x, y, 1 - c), device_id_type=MESH)
            cp.start()
            cps.append(cp)
        for i in range(n):
            rh = shards[i].shape[0] // 2
            theirs = o_refs[i].at[pl.ds(pl.multiple_of((1 - c) * rh, 8), rh), :]
            pltpu.make_async_remote_copy(
                src_ref=theirs, dst_ref=theirs, send_sem=send_sems.at[i], recv_sem=recv_sems.at[i],
                device_id=(x, y, c), device_id_type=MESH).wait_recv()
        for cp in cps:
            cp.wait_send()

    return pl.pallas_call(
        body, in_specs=[ANY] * n, out_specs=[ANY] * n,
        out_shape=[jax.ShapeDtypeStruct(h.shape, h.dtype) for h in shards],
        input_output_aliases={i: i for i in range(n)},
        scratch_shapes=[pltpu.SemaphoreType.DMA((n,)), pltpu.SemaphoreType.DMA((n,))],
        name=name)(*shards)


def _gather_all(blk, *, name, after=None, shares=()):
    rows, cols = blk.shape
    extra = [] if after is None else [after]
    n_s = len(shares)

    def body(x_ref, *refs):
        s_refs = refs[len(extra) + n_s + 1:len(extra) + 2 * n_s + 1]
        out_ref = refs[len(extra) + n_s]
        send_sems, recv_sems, local_sem, s_send, s_recv = refs[len(extra) + 2 * n_s + 1:]
        x, y, c = lax.axis_index("x"), lax.axis_index("y"), lax.axis_index("c")
        me = 4 * x + 2 * y + c
        mine = pltpu.make_async_copy(x_ref, out_ref.at[me], local_sem)
        mine.start()
        cps = []
        for i in range(n_s):
            rh = shares[i].shape[0] // 2
            half = s_refs[i].at[pl.ds(pl.multiple_of(c * rh, 8), rh), :]
            cp = pltpu.make_async_remote_copy(
                src_ref=half, dst_ref=half, send_sem=s_send.at[i], recv_sem=s_recv.at[i],
                device_id=(x, y, 1 - c), device_id_type=MESH)
            cp.start()
            cps.append(cp)
        for k in range(1, N_DEV):
            tx = (1 - x) if (k >> 2) & 1 else x
            ty = (1 - y) if (k >> 1) & 1 else y
            tc = (1 - c) if k & 1 else c
            cp = pltpu.make_async_remote_copy(
                src_ref=x_ref, dst_ref=out_ref.at[me], send_sem=send_sems.at[k - 1],
                recv_sem=recv_sems.at[k - 1], device_id=(tx, ty, tc), device_id_type=MESH)
            cp.start()
            cps.append(cp)
        for k in range(1, N_DEV):
            tx = (1 - x) if (k >> 2) & 1 else x
            ty = (1 - y) if (k >> 1) & 1 else y
            tc = (1 - c) if k & 1 else c
            got = out_ref.at[4 * tx + 2 * ty + tc]
            pltpu.make_async_remote_copy(
                src_ref=got, dst_ref=got, send_sem=send_sems.at[k - 1], recv_sem=recv_sems.at[k - 1],
                device_id=(x, y, c), device_id_type=MESH).wait_recv()
        for i in range(n_s):
            rh = shares[i].shape[0] // 2
            theirs = s_refs[i].at[pl.ds(pl.multiple_of((1 - c) * rh, 8), rh), :]
            pltpu.make_async_remote_copy(
                src_ref=theirs, dst_ref=theirs, send_sem=s_send.at[i], recv_sem=s_recv.at[i],
                device_id=(x, y, c), device_id_type=MESH).wait_recv()
        for cp in cps:
            cp.wait_send()
        mine.wait()

    vm = pl.BlockSpec(memory_space=pltpu.VMEM)
    out = pl.pallas_call(
        body, in_specs=[vm] + [ANY] * (len(extra) + n_s), out_specs=[vm] + [ANY] * n_s,
        out_shape=[jax.ShapeDtypeStruct((N_DEV, rows, cols), blk.dtype)]
        + [jax.ShapeDtypeStruct(s.shape, s.dtype) for s in shares],
        input_output_aliases={1 + len(extra) + i: 1 + i for i in range(n_s)},
        scratch_shapes=[pltpu.SemaphoreType.DMA((N_DEV - 1,)), pltpu.SemaphoreType.DMA((N_DEV - 1,)),
                        pltpu.SemaphoreType.DMA, pltpu.SemaphoreType.DMA((max(n_s, 1),)),
                        pltpu.SemaphoreType.DMA((max(n_s, 1),))],
        name=name)(blk, *extra, *shares)
    return (out[0], *out[1:]) if n_s else out[0]


def _as_rows(a):
    flat = a.reshape(-1)
    n = flat.shape[0]
    rows = -(-n // (8 * LANES)) * 8
    return jnp.pad(flat, (0, rows * LANES - n)).reshape(rows, LANES)


def _from_rows(p, shape):
    n = int(np.prod(shape))
    return p.reshape(-1)[:n].reshape(shape)


WEIGHT_AXES = (1, 1, 1, 0, 1, 0)
WIRE = BF16


def kernel(x, meta_tokens, w_in, w_na_out, w_hg_out, w_o, w_up, w_down, norm_mix, norm_mlp, norm_final, hg_norm, na_rpb, hg_lb_logits, loss_target, m_meta_tokens, m_w_in, m_w_na_out, m_w_hg_out, m_w_o, m_w_up, m_w_down, m_norm_mix, m_norm_mlp, m_norm_final, m_hg_norm, m_na_rpb, m_hg_lb_logits, v_meta_tokens, v_w_in, v_w_na_out, v_w_hg_out, v_w_o, v_w_up, v_w_down, v_norm_mix, v_norm_mlp, v_norm_final, v_hg_norm, v_na_rpb, v_hg_lb_logits):
    xi, yi, ci = lax.axis_index("x"), lax.axis_index("y"), lax.axis_index("c")
    chip = 2 * xi + yi
    d = x.shape[-1]
    dshard = meta_tokens.shape[1]
    hgw = hg_norm.shape[1]
    lbs = hg_lb_logits.shape[2]
    big = [w_in[0], w_na_out[0], w_hg_out[0], w_o[0], w_up[0], w_down[0]]
    big_m = [m_w_in[0], m_w_na_out[0], m_w_hg_out[0], m_w_o[0], m_w_up[0], m_w_down[0]]
    big_v = [v_w_in[0], v_w_na_out[0], v_w_hg_out[0], v_w_o[0], v_w_up[0], v_w_down[0]]

    place = jnp.stack([chip, ci]).astype(jnp.int32)
    in_axes, rest_axes = WEIGHT_AXES[:1], WEIGHT_AXES[1:]
    own_w = _cast_into_full(big, WEIGHT_AXES, place, name="cast_shards")
    small_in = jnp.concatenate([_as_rows(meta_tokens), _as_rows(hg_lb_logits)], axis=0)
    sm_send, sm_recv, sm_blk, sm_land, sm_token = _chip_exchange_start(small_in,
                                                                       name="small_params_start")
    in_send, in_recv, in_bufs, in_token = _allgather_start(own_w[:1], in_axes, sm_token,
                                                           name="weight_allgather_in_start")
    ag_send, ag_recv, ag_bufs, ag_token = _allgather_start(own_w[1:], rest_axes, in_token,
                                                           name="weight_allgather_rest_start")
    sm_land = _chip_exchange_wait(sm_send, sm_recv, sm_blk, sm_land, ag_token,
                                  name="small_params_wait")
    small_all = lax.dynamic_update_slice(sm_land, small_in[None], (chip, 0, 0))
    forward = {}

    def first_weight(after):
        got = _allgather_wait(in_send, in_recv, in_bufs, in_axes, after,
                              name="weight_allgather_in_wait")
        return _allgather_forward(got, in_axes, name="weight_allgather_in_forward")[0]

    def rest_landed(after):
        got = _allgather_wait(ag_send, ag_recv, ag_bufs, rest_axes, after,
                              name="weight_allgather_rest_wait")
        send, recv, bufs, token = _allgather_forward_start(
            got, rest_axes, name="weight_allgather_rest_forward_start")
        forward["rest"] = (send, recv, bufs)
        return token

    def rest_weights(after):
        return _allgather_forward_wait(*forward["rest"], rest_axes, after,
                                       name="weight_allgather_rest_forward_wait")

    n_meta_rows = N_META * dshard // LANES
    meta_full = (small_all[:, :n_meta_rows].reshape(N_CHIPS, N_META, dshard)
                 .transpose(1, 0, 2).reshape(N_META, d))
    lbl_full = (small_all[:, n_meta_rows:].reshape(N_CHIPS, -1)[:, :4 * lbs]
                .reshape(N_CHIPS, 2, 2, lbs).transpose(1, 2, 0, 3).reshape(2, 2, N_CHIPS * lbs))
    lb = jax.nn.softmax(lbl_full, axis=1)[:, 0]

    def by_chip(dws, axes):
        return [g.reshape(1, *g.shape) if ax == 1
                else g.reshape(N_CHIPS, g.shape[0] // N_CHIPS, g.shape[1]) for g, ax in zip(dws, axes)]

    flying = {}

    def scatter(tag, axes, g3, rx):
        parts = _pair_add(g3, rx, place, out_dtype=WIRE, name=f"grad_pair_add_{tag}")
        send, recv, parts, slots, token = _scatter_start(parts, axes,
                                                         name=f"grad_scatter_{tag}_start")
        flying[tag] = (send, recv, parts, slots)
        return token

    def swap(tag, axes):
        def start(dws):
            send, recv, g3, lands, token = _sibling_swap_start(
                by_chip(dws, axes), name=f"grad_sibling_swap_{tag}_start")
            flying["swap_" + tag] = (send, recv, g3, lands)
            return token

        def finish(after):
            g3, rx = _sibling_swap_wait(*flying["swap_" + tag], after,
                                        name=f"grad_sibling_swap_{tag}_wait")
            return scatter(tag, axes, g3, rx)
        return start, finish

    swap_rest, scatter_rest = swap("rest", rest_axes)
    swap_in, scatter_in = swap("in", in_axes)

    def landed(tag, axes, after):
        return _scatter_wait(*flying[tag], axes, after, name=f"grad_scatter_{tag}_wait")

    (loss, dx, dmeta, *_, dg_mix, dg_mlp, dg_fin, d_gain, d_rpb, d_lb) = _local_step(
        x[0], loss_target[0], meta_full, first_weight, rest_weights, norm_mix, norm_mlp,
        norm_final.reshape(1, d), hg_norm, na_rpb[0], lb, swap_rest, scatter_rest,
        lambda dw_in: swap_in([dw_in]), rest_landed, scatter_in)

    parts_rest, slots_rest = landed("rest", rest_axes, dx)
    g_rest = _sibling_share(_sum_chips(slots_rest, parts_rest, place, rest_axes,
                                       name="grad_sum_chips_rest"),
                            name="grad_sibling_share_rest")
    out_rest = _adamw(big[1:], g_rest, big_m[1:], big_v[1:], name="adamw_rest")
    parts_in, slots_in = landed("in", in_axes, out_rest[-1][0])
    half_in = _sum_chips(slots_in, parts_in, place, in_axes, name="grad_sum_chips_in")

    d_rpb_c = d_rpb[:, :2 * NA_WIN_W - 1]
    small_g = [dmeta, dg_mix, dg_mlp, dg_fin, d_gain, d_rpb_c, d_lb, loss]
    packed = jnp.concatenate([_as_rows(a) for a in small_g], axis=0)
    gathered, g_in = _gather_all(packed, shares=half_in, name="gather_small_grads")
    total = _sum_slots(gathered, name="sum_small_grads")
    offs = np.cumsum([0] + [_as_rows(a).shape[0] for a in small_g])
    take = lambda i, shape: _from_rows(total[offs[i]:offs[i + 1]], shape)
    g_meta_full = take(0, (N_META, d))
    g_norm_mix, g_norm_mlp = take(1, (1, d)), take(2, (1, d))
    g_norm_final = take(3, (d,))
    g_hg_norm = take(4, (1, hgw))
    g_rpb = take(5, na_rpb.shape)
    g_lb = take(6, (2, hgw))
    loss_total = take(7, (1, LANES))[0, 0]
    g_meta = lax.dynamic_slice_in_dim(g_meta_full, chip * dshard, dshard, axis=1)
    dl0 = lb * (1.0 - lb) * g_lb
    g_lbl_full = jnp.stack([dl0, -dl0], axis=1)
    g_lbl = lax.dynamic_slice_in_dim(g_lbl_full, chip * lbs, lbs, axis=2)

    big_out = _adamw(big[:1], [g_in], big_m[:1], big_v[:1], name="adamw_in") + out_rest
    small_w = [meta_tokens, norm_mix, norm_mlp, norm_final, hg_norm, na_rpb, hg_lb_logits]
    small_gr = [g_meta, g_norm_mix, g_norm_mlp, g_norm_final, g_hg_norm, g_rpb, g_lbl]
    small_m = [m_meta_tokens, m_norm_mix, m_norm_mlp, m_norm_final, m_hg_norm, m_na_rpb, m_hg_lb_logits]
    small_v = [v_meta_tokens, v_norm_mix, v_norm_mlp, v_norm_final, v_hg_norm, v_na_rpb, v_hg_lb_logits]
    pk = lambda lst: jnp.concatenate([_as_rows(a) for a in lst], axis=0)
    ((sd, sm, sv, _),) = _adamw([pk(small_w)], [pk(small_gr)], [pk(small_m)], [pk(small_v)],
                             name="adamw_small")
    soffs = np.cumsum([0] + [_as_rows(a).shape[0] for a in small_w])
    unpk = lambda p: [_from_rows(p[soffs[i]:soffs[i + 1]], small_w[i].shape) for i in range(len(small_w))]
    sd, sm, sv = unpk(sd), unpk(sm), unpk(sv)

    def order(bigs, smalls):
        return [smalls[0]] + [b.reshape(1, *b.shape) for b in bigs] + smalls[1:]

    grads = order([o[3] for o in big_out], small_gr)
    deltas = order([o[0] for o in big_out], sd)
    new_m = order([o[1] for o in big_out], sm)
    new_v = order([o[2] for o in big_out], sv)
    return (loss_total, dx.reshape(1, *dx.shape), *grads, *deltas, *new_m, *new_v)
```

```python
import functools

import numpy as np
import jax
import jax.numpy as jnp
from jax import lax
from jax.experimental import pallas as pl
from jax.experimental.pallas import tpu as pltpu

F32 = jnp.float32
BF16 = jnp.bfloat16
HIGHEST = lax.Precision.HIGHEST

GRID_W = 64
N_META = 16
EPS = 1e-6
NA_HEAD_DIM = 64
NA_WIN_H = 8
NA_WIN_W = 16
HG_DK = 128
LANES = 128
ROW_ALIGN = 128
VMEM_LIMIT = 48 * 1024 * 1024
ADAM_LR = 0.001
ADAM_B1 = 0.9
ADAM_B2 = 0.999
ADAM_EPS = 1e-08
ADAM_WD = 0.01
ADAM_STEP = 10

MESH = pl.DeviceIdType.MESH


def _cp(*sem):
    return pltpu.CompilerParams(dimension_semantics=sem, vmem_limit_bytes=VMEM_LIMIT)


def _sigmoid(x):
    return 0.5 * jnp.tanh(0.5 * x) + 0.5


def _dot(a, b, dims, precision=None):
    return lax.dot_general(a, b, (dims, ((), ())), preferred_element_type=F32, precision=precision)


def _nn(a, b, **kw):
    return _dot(a, b, ((1,), (0,)), **kw)


def _nt(a, b, **kw):
    return _dot(a, b, ((1,), (1,)), **kw)


def _tn(a, b, **kw):
    return _dot(a, b, ((0,), (0,)), **kw)


def _matmul(a, b, *, ta=False, tb=False, tm=None, tn=None, tk=None, out_dtype=F32, name,
            precision=None, after=None, epilogue=None, tiles=(), out_dtypes=None):
    extra = [] if after is None else [after]
    single = out_dtypes is None
    if single:
        out_dtypes = (out_dtype,)
    n_t, n_o = len(tiles), len(out_dtypes)
    if ta:
        kdim, m = a.shape
    else:
        m, kdim = a.shape
    if tb:
        n, k2 = b.shape
    else:
        k2, n = b.shape
    assert kdim == k2, (a.shape, b.shape, ta, tb)
    if tm is None:
        if ta:
            tm = next(t for t in (1024, 512, 256, 128, m) if m % t == 0)
        else:
            tm = m // 2 if (m // 2) % 16 == 0 and m > 512 else m
    if tn is None:
        wide = (1024,) if len(tiles) <= 1 else ()
        tn = next(t for t in (*wide, 512, 256, 128, n) if n % t == 0)
    if tk is None:
        tk = kdim if ta else next(t for t in (2048, 1024, 512, 256, 128, kdim) if kdim % t == 0)
    assert m % tm == 0 and n % tn == 0 and kdim % tk == 0, (m, n, kdim, tm, tn, tk)
    nk = kdim // tk
    op_dtype = F32 if precision is not None else BF16

    def body(a_ref, b_ref, *refs):
        t_refs = refs[:n_t]
        o_refs = refs[n_t + len(extra):n_t + len(extra) + n_o]
        av = a_ref[...].astype(op_dtype)
        bv = b_ref[...].astype(op_dtype)
        dims = ((0 if ta else 1,), (1 if tb else 0,))
        part = _dot(av, bv, dims, precision=precision)

        def finish(acc):
            outs = (acc,) if epilogue is None else epilogue(acc, *[t[...] for t in t_refs])
            for o_ref, val in zip(o_refs, outs):
                o_ref[...] = val.astype(o_ref.dtype)

        if nk == 1:
            finish(part)
            return
        acc_ref = refs[-1]
        kk = pl.program_id(2)

        @pl.when(kk == 0)
        def _():
            acc_ref[...] = part

        @pl.when((kk > 0) & (kk < nk - 1))
        def _():
            acc_ref[...] += part

        @pl.when(kk == nk - 1)
        def _():
            finish(acc_ref[...] + part)

    a_spec = (pl.BlockSpec((tk, tm), lambda i, j, k: (k, i)) if ta
              else pl.BlockSpec((tm, tk), lambda i, j, k: (i, k)))
    b_spec = (pl.BlockSpec((tn, tk), lambda i, j, k: (j, k)) if tb
              else pl.BlockSpec((tk, tn), lambda i, j, k: (k, j)))
    for _, off in tiles:
        assert off % tn == 0, (off, tn)
    t_specs = [pl.BlockSpec((tm, tn), functools.partial(lambda i, j, k, o: (i, o + j), o=off // tn))
               for _, off in tiles]
    o_spec = pl.BlockSpec((tm, tn), lambda i, j, k: (i, j))
    outs = pl.pallas_call(
        body,
        grid=(m // tm, n // tn, nk),
        in_specs=[a_spec, b_spec] + t_specs + [pl.BlockSpec(memory_space=pl.ANY)] * len(extra),
        out_specs=[o_spec] * n_o,
        out_shape=[jax.ShapeDtypeStruct((m, n), dt) for dt in out_dtypes],
        scratch_shapes=[pltpu.VMEM((tm, tn), F32)] if nk > 1 else [],
        compiler_params=_cp("parallel", "parallel", "arbitrary"),
        name=name,
    )(a, b, *[t for t, _ in tiles], *extra)
    return outs[0] if single else outs


def _rspec(tr, w, cb=0):
    return pl.BlockSpec((tr, w), lambda i: (i, cb))


def _fspec(shape):
    nd = len(shape)
    return pl.BlockSpec(shape, lambda i: (0,) * nd)


ROW_VMEM_BUDGET = 20 * 1024 * 1024
ROW_MIN_STEPS = 4


def _row_tile(lp, row_bytes):
    for k in range(ROW_MIN_STEPS, lp // 16 + 1):
        tr = lp // k
        if lp % k == 0 and tr % 16 == 0 and 2 * tr * row_bytes <= ROW_VMEM_BUDGET:
            return tr
    return lp


def _token_rows_copy(i, n_tiles, tr, n_tok, tok_ref, buf_ref, sem, *, to_tokens, start=True,
                     wait=True):
    assert n_tiles >= 2 and 0 < n_tok + N_META - (n_tiles - 1) * tr <= tr

    def run(tok_row, buf_row, count):
        tok = tok_ref.at[pl.ds(tok_row, count), :]
        buf = buf_ref.at[pl.ds(buf_row, count), :]
        cp = pltpu.make_async_copy(buf, tok, sem) if to_tokens else pltpu.make_async_copy(tok, buf, sem)
        if start:
            cp.start()
        if wait:
            cp.wait()

    @pl.when(i == 0)
    def _():
        run(0, N_META, tr - N_META)

    if n_tiles > 2:
        @pl.when((i > 0) & (i < n_tiles - 1))
        def _():
            run(pl.multiple_of(i * tr - N_META, 8), 0, tr)

    @pl.when(i == n_tiles - 1)
    def _():
        run((n_tiles - 1) * tr - N_META, 0, n_tok + N_META - (n_tiles - 1) * tr)


def _embed_norm(x, tgt, meta, g, *, lp, name):
    n_tok, d = x.shape
    tr = _row_tile(lp, d * (4 + 2 + 4))
    n_tiles = lp // tr

    def body(x_ref, tgt_ref, meta_ref, g_ref, h_ref, o_ref, tp_ref, buf_ref, tbuf_ref, sems):
        i = pl.program_id(0)
        buf_ref[...] = jnp.zeros_like(buf_ref)
        tbuf_ref[...] = jnp.zeros_like(tbuf_ref)

        @pl.when(i == 0)
        def _():
            buf_ref[0:N_META, :] = meta_ref[...]

        _token_rows_copy(i, n_tiles, tr, n_tok, tgt_ref, tbuf_ref, sems.at[1], to_tokens=False,
                         wait=False)
        _token_rows_copy(i, n_tiles, tr, n_tok, x_ref, buf_ref, sems.at[0], to_tokens=False)
        xv = buf_ref[...]
        h_ref[...] = xv
        r = lax.rsqrt(jnp.mean(xv * xv, axis=-1, keepdims=True) + EPS)
        o_ref[...] = (xv * r * g_ref[...]).astype(BF16)
        _token_rows_copy(i, n_tiles, tr, n_tok, tgt_ref, tbuf_ref, sems.at[1], to_tokens=False,
                         start=False)
        tp_ref[...] = tbuf_ref[...]

    return pl.pallas_call(
        body, grid=(n_tiles,),
        in_specs=[ANY, ANY, _fspec((N_META, d)), _fspec((1, d))],
        out_specs=[_rspec(tr, d), _rspec(tr, d), _rspec(tr, d)],
        out_shape=[jax.ShapeDtypeStruct((lp, d), F32), jax.ShapeDtypeStruct((lp, d), BF16),
                   jax.ShapeDtypeStruct((lp, d), F32)],
        scratch_shapes=[pltpu.VMEM((tr, d), F32), pltpu.VMEM((tr, d), F32),
                        pltpu.SemaphoreType.DMA((2,))],
        compiler_params=_cp("parallel"), name=name)(x, tgt, meta, g)


def _residual_norm(h, t, g, *, name):
    lp, d = h.shape
    tr = _row_tile(lp, d * (4 + 4 + 4 + 2))

    def body(h_ref, t_ref, g_ref, h1_ref, m_ref):
        xv = h_ref[...] + t_ref[...]
        h1_ref[...] = xv
        r = lax.rsqrt(jnp.mean(xv * xv, axis=-1, keepdims=True) + EPS)
        m_ref[...] = (xv * r * g_ref[...]).astype(BF16)

    return pl.pallas_call(
        body, grid=(lp // tr,),
        in_specs=[_rspec(tr, d), _rspec(tr, d), _fspec((1, d))],
        out_specs=[_rspec(tr, d), _rspec(tr, d)],
        out_shape=[jax.ShapeDtypeStruct((lp, d), F32), jax.ShapeDtypeStruct((lp, d), BF16)],
        compiler_params=_cp("parallel"), name=name)(h, t, g)


def _rmsnorm_bwd_add(x, g, dy, dres, *, name):
    lp, d = x.shape
    tr = _row_tile(lp, d * (4 * 4 + 2))

    def body(x_ref, g_ref, dy_ref, dr_ref, dx_ref, dx16_ref, dg_ref):
        @pl.when(pl.program_id(0) == 0)
        def _():
            dg_ref[...] = jnp.zeros_like(dg_ref)

        xv = x_ref[...]
        r = lax.rsqrt(jnp.mean(xv * xv, axis=-1, keepdims=True) + EPS)
        xh = xv * r
        dyv = dy_ref[...]
        dg_ref[...] += jnp.sum(dyv * xh, axis=0, keepdims=True)
        dxh = dyv * g_ref[...]
        dx = dr_ref[...] + r * (dxh - xh * jnp.mean(dxh * xh, axis=-1, keepdims=True))
        dx_ref[...] = dx
        dx16_ref[...] = dx.astype(BF16)

    return pl.pallas_call(
        body, grid=(lp // tr,),
        in_specs=[_rspec(tr, d), _fspec((1, d)), _rspec(tr, d), _rspec(tr, d)],
        out_specs=[_rspec(tr, d), _rspec(tr, d), _fspec((1, d))],
        out_shape=[jax.ShapeDtypeStruct((lp, d), F32), jax.ShapeDtypeStruct((lp, d), BF16),
                   jax.ShapeDtypeStruct((1, d), F32)],
        compiler_params=_cp("arbitrary"), name=name)(x, g, dy, dres)


def _rmsnorm_bwd_tokens(x, g, dy, dres, *, n_tok, name):
    lp, d = x.shape
    tr = _row_tile(lp, d * 4 * 4)
    n_tiles = lp // tr

    def body(x_ref, g_ref, dy_ref, dr_ref, dtok_ref, dmeta_ref, dg_ref, buf_ref, sem):
        i = pl.program_id(0)

        @pl.when(i == 0)
        def _():
            dg_ref[...] = jnp.zeros_like(dg_ref)

        xv = x_ref[...]
        r = lax.rsqrt(jnp.mean(xv * xv, axis=-1, keepdims=True) + EPS)
        xh = xv * r
        dyv = dy_ref[...]
        dg_ref[...] += jnp.sum(dyv * xh, axis=0, keepdims=True)
        dxh = dyv * g_ref[...]
        buf_ref[...] = dr_ref[...] + r * (dxh - xh * jnp.mean(dxh * xh, axis=-1, keepdims=True))

        @pl.when(i == 0)
        def _():
            dmeta_ref[...] = buf_ref[0:N_META, :]

        _token_rows_copy(i, n_tiles, tr, n_tok, dtok_ref, buf_ref, sem, to_tokens=True)

    return pl.pallas_call(
        body, grid=(n_tiles,),
        in_specs=[_rspec(tr, d), _fspec((1, d)), _rspec(tr, d), _rspec(tr, d)],
        out_specs=[ANY, _fspec((N_META, d)), _fspec((1, d))],
        out_shape=[jax.ShapeDtypeStruct((n_tok, d), F32), jax.ShapeDtypeStruct((N_META, d), F32),
                   jax.ShapeDtypeStruct((1, d), F32)],
        scratch_shapes=[pltpu.VMEM((tr, d), F32), pltpu.SemaphoreType.DMA],
        compiler_params=_cp("arbitrary"), name=name)(x, g, dy, dres)


def _final_loss(h1, t2, g, tgt, *, n_tok, name):
    lp, d = h1.shape
    tr = _row_tile(lp, d * (4 * 4 + 2))
    n_tiles = lp // tr

    def body(h_ref, t_ref, g_ref, tg_ref, dh_ref, dh16_ref, loss_ref, dg_ref):
        i = pl.program_id(0)

        @pl.when(i == 0)
        def _():
            loss_ref[...] = jnp.zeros_like(loss_ref)
            dg_ref[...] = jnp.zeros_like(dg_ref)

        xv = h_ref[...] + t_ref[...]
        r = lax.rsqrt(jnp.mean(xv * xv, axis=-1, keepdims=True) + EPS)
        xh = xv * r
        gv = g_ref[...]
        row = i * tr + lax.broadcasted_iota(jnp.int32, (tr, 1), 0)
        valid = (row >= N_META) & (row < N_META + n_tok)
        err = jnp.where(valid, xh * gv - tg_ref[...], 0.0)
        loss_ref[...] += jnp.sum(0.5 * err * err) / d
        dy = err / d
        dg_ref[...] += jnp.sum(dy * xh, axis=0, keepdims=True)
        dxh = dy * gv
        dh = r * (dxh - xh * jnp.mean(dxh * xh, axis=-1, keepdims=True))
        dh_ref[...] = dh
        dh16_ref[...] = dh.astype(BF16)

    return pl.pallas_call(
        body, grid=(n_tiles,),
        in_specs=[_rspec(tr, d), _rspec(tr, d), _fspec((1, d)), _rspec(tr, d)],
        out_specs=[_rspec(tr, d), _rspec(tr, d), _fspec((1, LANES)), _fspec((1, d))],
        out_shape=[jax.ShapeDtypeStruct((lp, d), F32), jax.ShapeDtypeStruct((lp, d), BF16),
                   jax.ShapeDtypeStruct((1, LANES), F32), jax.ShapeDtypeStruct((1, d), F32)],
        compiler_params=_cp("arbitrary"), name=name)(h1, t2, g, tgt)


def _hg_out(o_f, o_b, proj, gain, *, col_g, name):
    lp, w = o_f.shape
    tr = _row_tile(lp, w * (3 * 4 + 2))
    hh = w // HG_DK

    def body(of_ref, ob_ref, g_ref, gain_ref, y_ref):
        gv = g_ref[...]
        sg = gv * _sigmoid(gv)
        for h in range(hh):
            sl = slice(h * HG_DK, (h + 1) * HG_DK)
            o = of_ref[:, sl] + ob_ref[:, sl]
            r = lax.rsqrt(jnp.mean(o * o, axis=-1, keepdims=True) + EPS)
            y_ref[:, sl] = (o * r * gain_ref[:, sl] * sg[:, sl]).astype(BF16)

    return pl.pallas_call(
        body, grid=(lp // tr,),
        in_specs=[_rspec(tr, w), _rspec(tr, w), _rspec(tr, w, col_g // w), _fspec((1, w))],
        out_specs=_rspec(tr, w),
        out_shape=jax.ShapeDtypeStruct((lp, w), BF16),
        compiler_params=_cp("parallel"), name=name)(o_f, o_b, proj, gain)


def _hg_out_bwd(o_f, o_b, proj, gain, dy, *, col_g, name):
    lp, w = o_f.shape
    tr = _row_tile(lp, w * (5 * 4 + 2))
    hh = w // HG_DK

    def body(of_ref, ob_ref, g_ref, gain_ref, dy_ref, do_ref, dg_ref, dgain_ref):
        @pl.when(pl.program_id(0) == 0)
        def _():
            dgain_ref[...] = jnp.zeros_like(dgain_ref)

        for h in range(hh):
            sl = slice(h * HG_DK, (h + 1) * HG_DK)
            gv = g_ref[:, sl]
            s = _sigmoid(gv)
            sg = gv * s
            dsg = s + gv * s * (1.0 - s)
            o = of_ref[:, sl] + ob_ref[:, sl]
            r = lax.rsqrt(jnp.mean(o * o, axis=-1, keepdims=True) + EPS)
            on = o * r
            dyv = dy_ref[:, sl]
            gn = gain_ref[:, sl]
            dgain_ref[:, sl] += jnp.sum(dyv * on * sg, axis=0, keepdims=True)
            dg_ref[:, sl] = (dyv * on * gn * dsg).astype(BF16)
            don = dyv * gn * sg
            do_ref[:, sl] = r * (don - on * jnp.mean(don * on, axis=-1, keepdims=True))

    return pl.pallas_call(
        body, grid=(lp // tr,),
        in_specs=[_rspec(tr, w), _rspec(tr, w), _rspec(tr, w, col_g // w), _fspec((1, w)),
                  _rspec(tr, w)],
        out_specs=[_rspec(tr, w), _rspec(tr, w), _fspec((1, w))],
        out_shape=[jax.ShapeDtypeStruct((lp, w), F32), jax.ShapeDtypeStruct((lp, w), BF16),
                   jax.ShapeDtypeStruct((1, w), F32)],
        compiler_params=_cp("arbitrary"), name=name)(o_f, o_b, proj, gain, dy)


HG_ROWS = 128
HG_HALVES = (1, 2, 4, 8, 16, 32, 64)


def _hg_gates(zq, z, lbv):
    sq = _sigmoid(zq)
    s = _sigmoid(z)
    f = lbv + (1.0 - lbv) * s
    kk = (1.0 - lbv) * (1.0 - s)
    return zq * sq, sq, s, f, jnp.log(f), kk


def _block_cumsum(g, pos, suffix):
    x = g
    for k in HG_HALVES:
        if suffix:
            x = x + jnp.where(pos < HG_ROWS - k, pltpu.roll(x, HG_ROWS - k, 0), 0.0)
        else:
            x = x + jnp.where(pos >= k, pltpu.roll(x, k, 0), 0.0)
    return x


def _pair_levels(b, pos, reverse):
    out = []
    first = b
    for m in HG_HALVES:
        if m > 1:
            first = jnp.where((pos & (m - 1)) >= m // 2, pltpu.roll(first, m // 2, 0), first)
        nxt = pltpu.roll(first, HG_ROWS - m, 0)
        upper = (pos & (2 * m - 1)) >= m
        if reverse:
            eq = jnp.where(upper, 0.0, jnp.exp(b - nxt))
            ek = jnp.where(upper, jnp.exp(first - b), 0.0)
        else:
            eq = jnp.where(upper, jnp.exp(b - first), 0.0)
            ek = jnp.where(upper, 0.0, jnp.exp(nxt - b))
        out.append((eq, ek))
    return out


def _pair_masks(mask_ref):
    ri = lax.broadcasted_iota(jnp.int32, (HG_ROWS, HG_ROWS), 0)
    ci = lax.broadcasted_iota(jnp.int32, (HG_ROWS, HG_ROWS), 1)
    for i, m in enumerate(HG_HALVES):
        sh = m.bit_length()
        mask_ref[i] = jnp.where((ri >> sh) == (ci >> sh), 1.0, 0.0)


def _hg_scan_fwd(proj, lb, *, reverse, col_q, col_z, col_i, hh, name):
    lp = proj.shape[0]
    n_blocks = lp // HG_ROWS
    last = 0 if reverse else HG_ROWS - 1

    def body(q_ref, z_ref, i_ref, lb_ref, o_ref, st_ref, mask_ref):
        lbv = lb_ref[...]
        pos = lax.broadcasted_iota(jnp.int32, (HG_ROWS, 1), 0)
        ri = lax.broadcasted_iota(jnp.int32, (HG_ROWS, HG_ROWS), 0)
        ci = lax.broadcasted_iota(jnp.int32, (HG_ROWS, HG_ROWS), 1)
        _pair_masks(mask_ref)

        def block(bi, st):
            bb = (n_blocks - 1 - bi) if reverse else bi
            r0 = pl.multiple_of(bb * HG_ROWS, HG_ROWS)
            v16 = i_ref[pl.ds(r0, HG_ROWS), :].astype(BF16)
            qh, _, _, _, g, kk = _hg_gates(q_ref[pl.ds(r0, HG_ROWS), :],
                                           z_ref[pl.ds(r0, HG_ROWS), :], lbv)
            b = _block_cumsum(g, pos, reverse)
            bl = b[last:last + 1, :]
            qe = (qh * jnp.exp(b)).astype(BF16)
            kd = (kk * jnp.exp(bl - b)).astype(BF16)
            a = jnp.where(ri == ci, jnp.sum(qh * kk, axis=1, keepdims=True), 0.0)
            for i, (eq, ek) in enumerate(_pair_levels(b, pos, reverse)):
                a = a + mask_ref[i] * _nt((qh * eq).astype(BF16), (kk * ek).astype(BF16))
            st_ref[bb] = st
            o_ref[pl.ds(r0, HG_ROWS), :] = _nn(a.astype(BF16), v16) + _nt(qe, st.astype(BF16))
            return jnp.exp(bl) * st + _tn(v16, kd)

        lax.fori_loop(0, n_blocks, block, jnp.zeros((HG_DK, HG_DK), F32))

    cspec = lambda col: pl.BlockSpec((lp, HG_DK), lambda h: (0, col // HG_DK + h))
    return pl.pallas_call(
        body, grid=(hh,),
        in_specs=[cspec(col_q), cspec(col_z), cspec(col_i),
                  pl.BlockSpec((None, 1, HG_DK), lambda h: (h, 0, 0))],
        out_specs=[pl.BlockSpec((lp, HG_DK), lambda h: (0, h)),
                   pl.BlockSpec((None, n_blocks, HG_DK, HG_DK), lambda h: (h, 0, 0, 0))],
        out_shape=[jax.ShapeDtypeStruct((lp, hh * HG_DK), F32),
                   jax.ShapeDtypeStruct((hh, n_blocks, HG_DK, HG_DK), F32)],
        scratch_shapes=[pltpu.VMEM((len(HG_HALVES), HG_ROWS, HG_ROWS), F32)],
        compiler_params=_cp("parallel"), name=name)(proj, proj, proj, lb)


def _hg_scan_bwd(proj, lb, states, do, *, reverse, col_q, col_z, col_i, hh, name):
    lp = proj.shape[0]
    n_blocks = lp // HG_ROWS
    last = 0 if reverse else HG_ROWS - 1

    def body(q_ref, z_ref, i_ref, lb_ref, st_ref, do_ref, dq_ref, dz_ref, dv_ref, dlb_ref, mask_ref):
        lbv = lb_ref[...]
        pos = lax.broadcasted_iota(jnp.int32, (HG_ROWS, 1), 0)
        ri = lax.broadcasted_iota(jnp.int32, (HG_ROWS, HG_ROWS), 0)
        ci = lax.broadcasted_iota(jnp.int32, (HG_ROWS, HG_ROWS), 1)
        _pair_masks(mask_ref)

        def block(bi, carry):
            dst, dlb = carry
            bb = bi if reverse else (n_blocks - 1 - bi)
            r0 = pl.multiple_of(bb * HG_ROWS, HG_ROWS)
            zq = q_ref[pl.ds(r0, HG_ROWS), :]
            v16 = i_ref[pl.ds(r0, HG_ROWS), :].astype(BF16)
            do16 = do_ref[pl.ds(r0, HG_ROWS), :].astype(BF16)
            qh, sq, s, f, g, kk = _hg_gates(zq, z_ref[pl.ds(r0, HG_ROWS), :], lbv)
            b = _block_cumsum(g, pos, reverse)
            bl = b[last:last + 1, :]
            eb = jnp.exp(b)
            ebl = jnp.exp(bl - b)
            decay = jnp.exp(bl)
            qe16 = (qh * eb).astype(BF16)
            kd16 = (kk * ebl).astype(BF16)
            st = st_ref[bb]
            st16, dst16 = st.astype(BF16), dst.astype(BF16)
            same_row = ri == ci
            da = _nt(do16, v16)
            da_diag = jnp.sum(jnp.where(same_row, da, 0.0), axis=1, keepdims=True)
            dq_state = eb * _nn(do16, st16)
            dk_state = ebl * _nn(v16, dst16)
            dq = dq_state + da_diag * kk
            dk = dk_state + da_diag * qh
            dbl = (decay * jnp.sum(st * dst, axis=0, keepdims=True)
                   + jnp.sum(kk * dk_state, axis=0, keepdims=True))
            db = qh * dq_state - kk * dk_state + jnp.where(pos == last, dbl, 0.0)
            a = jnp.where(same_row, jnp.sum(qh * kk, axis=1, keepdims=True), 0.0)
            for i, (eq, ek) in enumerate(_pair_levels(b, pos, reverse)):
                same = mask_ref[i]
                q16, k16 = (qh * eq).astype(BF16), (kk * ek).astype(BF16)
                a = a + same * _nt(q16, k16)
                da16 = (same * da).astype(BF16)
                gq, gk = _nn(da16, k16), _tn(da16, q16)
                dq = dq + eq * gq
                dk = dk + ek * gk
                db = db + (q16.astype(F32) * gq - k16.astype(F32) * gk)
            dg = _block_cumsum(db, pos, not reverse)
            df = dg / f - dk
            dq_ref[pl.ds(r0, HG_ROWS), :] = dq * (sq + zq * sq * (1.0 - sq))
            dz_ref[pl.ds(r0, HG_ROWS), :] = (df * (1.0 - lbv) * s * (1.0 - s)).astype(BF16)
            dv_ref[pl.ds(r0, HG_ROWS), :] = _nt(kd16, dst16) + _tn(a.astype(BF16), do16)
            return (decay * dst + _tn(do16, qe16),
                    dlb + jnp.sum(df * (1.0 - s), axis=0, keepdims=True))

        _, dlb = lax.fori_loop(0, n_blocks, block,
                               (jnp.zeros((HG_DK, HG_DK), F32), jnp.zeros((1, HG_DK), F32)))
        dlb_ref[...] = dlb

    cspec = lambda col: pl.BlockSpec((lp, HG_DK), lambda h: (0, col // HG_DK + h))
    ospec = pl.BlockSpec((lp, HG_DK), lambda h: (0, h))
    sds = jax.ShapeDtypeStruct((lp, hh * HG_DK), F32)
    return pl.pallas_call(
        body, grid=(hh,),
        in_specs=[cspec(col_q), cspec(col_z), cspec(col_i),
                  pl.BlockSpec((None, 1, HG_DK), lambda h: (h, 0, 0)),
                  pl.BlockSpec((None, n_blocks, HG_DK, HG_DK), lambda h: (h, 0, 0, 0)),
                  ospec],
        out_specs=[ospec, ospec, ospec, pl.BlockSpec((None, 1, HG_DK), lambda h: (h, 0, 0))],
        out_shape=[sds, jax.ShapeDtypeStruct((lp, hh * HG_DK), BF16), sds,
                   jax.ShapeDtypeStruct((hh, 1, HG_DK), F32)],
        scratch_shapes=[pltpu.VMEM((len(HG_HALVES), HG_ROWS, HG_ROWS), F32)],
        compiler_params=_cp("parallel"), name=name)(proj, proj, proj, lb, states, do)


NA_HB = LANES // NA_HEAD_DIM
NA_G = 4
NA_U = NA_G + NA_WIN_H
NA_QN = NA_G * GRID_W
NA_KN = NA_U * GRID_W


def _na_table_index(pattern, a, j):
    if pattern == 0:
        return j - a + NA_WIN_H - 1 if j < NA_WIN_H else None
    if pattern == 2:
        return j - a - 1 if j >= NA_U - NA_WIN_H else None
    return j - a + NA_WIN_H // 2 - 1 if a <= j < a + NA_WIN_H else None


def _na_step_rows(pattern, t, rows):
    if pattern == 0:
        r0, us = 0, 0
    elif pattern == 2:
        r0, us = rows - NA_G, rows - NA_U
    else:
        r0 = NA_G * t
        us = r0 - NA_WIN_H // 2
    q0, k0 = N_META + GRID_W * r0, N_META + GRID_W * us
    if pattern == 1:
        q0, k0 = pl.multiple_of(q0, 16), pl.multiple_of(k0, 16)
    return q0, k0


def _na_fill_bias(tb_ref, bias_ref):
    neg = jnp.full((GRID_W, GRID_W), -1e30, F32)
    for h in range(NA_HB):
        for pattern in range(3):
            for a in range(NA_G):
                for j in range(NA_U):
                    idx = _na_table_index(pattern, a, j)
                    bias_ref[h, pattern, a * GRID_W:(a + 1) * GRID_W, j * GRID_W:(j + 1) * GRID_W] = (
                        neg if idx is None else tb_ref[h, idx])


def _na_steps(rows, step, carry):
    n_steps = rows // NA_G
    carry = step(0, 0, carry)
    carry = lax.fori_loop(1, n_steps - 1, functools.partial(step, 1), carry)
    return step(2, n_steps - 1, carry)


def _na_head_lanes():
    lane = lax.broadcasted_iota(jnp.int32, (1, LANES), 1)
    return [lane // NA_HEAD_DIM == h for h in range(NA_HB)]


def _na_only(mask, x):
    return jnp.where(mask, x, jnp.zeros_like(x))


def _na_stack(heads, x):
    return jnp.concatenate([_na_only(mask, x) for mask in heads], axis=0)


def _na_unstack(heads, y):
    rows = y.shape[0] // NA_HB
    out = y[0:rows]
    for h in range(1, NA_HB):
        out = jnp.where(heads[h], y[h * rows:(h + 1) * rows], out)
    return out


def _na_fwd(proj, tb, *, n_tok, nh, name):
    lp = proj.shape[0]
    dh, hb = NA_HEAD_DIM, NA_HB
    naw = nh * dh
    rows = n_tok // GRID_W
    scale = dh ** -0.5

    def body(q_ref, k_ref, v_ref, tb_ref, o_ref, lse_ref, q16_ref, k16_ref, v16_ref, bias_ref):
        o_ref[...] = jnp.zeros_like(o_ref)
        lse_ref[...] = jnp.zeros_like(lse_ref)
        q16_ref[...] = q_ref[...].astype(BF16)
        k16_ref[...] = k_ref[...].astype(BF16)
        v16_ref[...] = v_ref[...].astype(BF16)
        _na_fill_bias(tb_ref, bias_ref)
        heads = _na_head_lanes()
        km = k16_ref[0:N_META, :]
        vm = v16_ref[0:N_META, :]
        qm = q16_ref[0:N_META, :]
        o_m = None
        for h in range(hb):
            s = _nt(_na_only(heads[h], qm), km) * scale
            m = jnp.max(s, axis=1, keepdims=True)
            p = jnp.exp(s - m)
            l = jnp.sum(p, axis=1, keepdims=True)
            o_h = _nn(p.astype(BF16), vm) / l
            o_m = o_h if o_m is None else jnp.where(heads[h], o_h, o_m)
            lse_ref[h, 0:N_META, :] = m + jnp.log(l)
        o_ref[0:N_META, :] = o_m

        def step(pattern, t, carry):
            q0, k0 = _na_step_rows(pattern, t, rows)
            k16 = k16_ref[pl.ds(k0, NA_KN), :]
            v16 = v16_ref[pl.ds(k0, NA_KN), :]
            q2 = _na_stack(heads, q16_ref[pl.ds(q0, NA_QN), :])
            s = _nt(q2, k16) * scale + bias_ref[:, pattern].reshape(hb * NA_QN, NA_KN)
            sm = _nt(q2, km) * scale
            m = jnp.maximum(jnp.max(s, axis=1, keepdims=True), jnp.max(sm, axis=1, keepdims=True))
            p = jnp.exp(s - m)
            pm = jnp.exp(sm - m)
            l = jnp.sum(p, axis=1, keepdims=True) + jnp.sum(pm, axis=1, keepdims=True)
            o2 = (_nn(p.astype(BF16), v16) + _nn(pm.astype(BF16), vm)) / l
            o_ref[pl.ds(q0, NA_QN), :] = _na_unstack(heads, o2)
            lse2 = m + jnp.log(l)
            for h in range(hb):
                lse_ref[h, pl.ds(q0, NA_QN), :] = lse2[h * NA_QN:(h + 1) * NA_QN]
            return carry

        _na_steps(rows, step, 0)

    cblk = lambda col: pl.BlockSpec((lp, LANES), lambda g: (0, col // LANES + g))
    return pl.pallas_call(
        body, grid=(nh // hb,),
        in_specs=[cblk(0), cblk(naw), cblk(2 * naw),
                  pl.BlockSpec((hb, 2 * NA_WIN_H - 1, GRID_W, GRID_W), lambda g: (g, 0, 0, 0))],
        out_specs=[cblk(0), pl.BlockSpec((hb, lp, 1), lambda g: (g, 0, 0))],
        out_shape=[jax.ShapeDtypeStruct((lp, naw), F32), jax.ShapeDtypeStruct((nh, lp, 1), F32)],
        scratch_shapes=[pltpu.VMEM((lp, LANES), BF16)] * 3 + [pltpu.VMEM((hb, 3, NA_QN, NA_KN), F32)],
        compiler_params=_cp("parallel"), name=name)(proj, proj, proj, tb)


def _na_bwd(proj, tb, o, lse, do, *, n_tok, nh, name):
    lp = proj.shape[0]
    dh, hb = NA_HEAD_DIM, NA_HB
    naw = nh * dh
    rows = n_tok // GRID_W
    scale = dh ** -0.5

    def body(q_ref, k_ref, v_ref, tb_ref, o_ref, lse_ref, do_ref, dq_ref, dk_ref, dv_ref, dtb_ref,
             q16_ref, k16_ref, v16_ref, bias_ref):
        dq_ref[...] = jnp.zeros_like(dq_ref)
        dk_ref[...] = jnp.zeros_like(dk_ref)
        dv_ref[...] = jnp.zeros_like(dv_ref)
        dtb_ref[...] = jnp.zeros_like(dtb_ref)
        q16_ref[...] = q_ref[...].astype(BF16)
        k16_ref[...] = k_ref[...].astype(BF16)
        v16_ref[...] = v_ref[...].astype(BF16)
        _na_fill_bias(tb_ref, bias_ref)
        heads = _na_head_lanes()
        km = k16_ref[0:N_META, :]
        vm = v16_ref[0:N_META, :]
        qm = q16_ref[0:N_META, :]
        dom = do_ref[0:N_META, :]
        prod = dom * o_ref[0:N_META, :]
        dq_m = None
        dkm0 = jnp.zeros((N_META, LANES), F32)
        dvm0 = jnp.zeros((N_META, LANES), F32)
        for h in range(hb):
            q_h = _na_only(heads[h], qm)
            do_h = _na_only(heads[h], dom).astype(BF16)
            p = jnp.exp(_nt(q_h, km) * scale - lse_ref[h, 0:N_META, :])
            delta = jnp.sum(_na_only(heads[h], prod), axis=1, keepdims=True)
            ds = (p * (_nt(do_h, vm) - delta)).astype(BF16)
            dq_h = _nn(ds, km) * scale
            dq_m = dq_h if dq_m is None else jnp.where(heads[h], dq_h, dq_m)
            dkm0 = dkm0 + _tn(ds, q_h) * scale
            dvm0 = dvm0 + _tn(p.astype(BF16), do_h)
        dq_ref[0:N_META, :] = dq_m

        def step(pattern, t, carry):
            dkm, dvm = carry
            q0, k0 = _na_step_rows(pattern, t, rows)
            k16 = k16_ref[pl.ds(k0, NA_KN), :]
            v16 = v16_ref[pl.ds(k0, NA_KN), :]
            q2 = _na_stack(heads, q16_ref[pl.ds(q0, NA_QN), :])
            do2 = _na_stack(heads, do_ref[pl.ds(q0, NA_QN), :])
            do16 = do2.astype(BF16)
            ov = o_ref[pl.ds(q0, NA_QN), :]
            delta = jnp.sum(do2 * jnp.concatenate([ov] * hb, axis=0), axis=1, keepdims=True)
            lse = jnp.concatenate([lse_ref[h, pl.ds(q0, NA_QN), :] for h in range(hb)], axis=0)
            p = jnp.exp(_nt(q2, k16) * scale + bias_ref[:, pattern].reshape(hb * NA_QN, NA_KN)
                        - lse)
            pm = jnp.exp(_nt(q2, km) * scale - lse)
            ds = p * (_nt(do16, v16) - delta)
            dsm = (pm * (_nt(do16, vm) - delta)).astype(BF16)
            ds16 = ds.astype(BF16)
            dq2 = (_nn(ds16, k16) + _nn(dsm, km)) * scale
            dq_ref[pl.ds(q0, NA_QN), :] = _na_unstack(heads, dq2)
            dk_ref[pl.ds(k0, NA_KN), :] += _tn(ds16, q2) * scale
            dv_ref[pl.ds(k0, NA_KN), :] += _tn(p.astype(BF16), do16)
            for h in range(hb):
                for a in range(NA_G):
                    for j in range(NA_U):
                        idx = _na_table_index(pattern, a, j)
                        if idx is not None:
                            r = h * NA_QN + a * GRID_W
                            dtb_ref[h, idx] += ds[r:r + GRID_W, j * GRID_W:(j + 1) * GRID_W]
            return dkm + _tn(dsm, q2) * scale, dvm + _tn(pm.astype(BF16), do16)

        dkm, dvm = _na_steps(rows, step, (dkm0, dvm0))
        dk_ref[0:N_META, :] += dkm
        dv_ref[0:N_META, :] += dvm

    cblk = lambda col: pl.BlockSpec((lp, LANES), lambda g: (0, col // LANES + g))
    tbs = pl.BlockSpec((hb, 2 * NA_WIN_H - 1, GRID_W, GRID_W), lambda g: (g, 0, 0, 0))
    sds = jax.ShapeDtypeStruct((lp, naw), F32)
    return pl.pallas_call(
        body, grid=(nh // hb,),
        in_specs=[cblk(0), cblk(naw), cblk(2 * naw), tbs, cblk(0),
                  pl.BlockSpec((hb, lp, 1), lambda g: (g, 0, 0)), cblk(0)],
        out_specs=[cblk(0), cblk(0), cblk(0), tbs],
        out_shape=[sds, sds, sds, jax.ShapeDtypeStruct(tb.shape, F32)],
        scratch_shapes=[pltpu.VMEM((lp, LANES), BF16)] * 3 + [pltpu.VMEM((hb, 3, NA_QN, NA_KN), F32)],
        compiler_params=_cp("parallel"), name=name)(proj, proj, proj, tb, o, lse, do)


def _rpb_onehot():
    c = np.arange(GRID_W)[:, None]
    w = np.arange(GRID_W)[None, :]
    cs = np.clip(c - NA_WIN_W // 2, 0, GRID_W - NA_WIN_W)
    in_win = (w >= cs) & (w < cs + NA_WIN_W)
    dc = np.clip(w - c, -(NA_WIN_W - 1), NA_WIN_W - 1) + NA_WIN_W - 1
    oh = np.zeros((LANES, GRID_W * GRID_W), np.float32)
    flat = np.arange(GRID_W * GRID_W).reshape(GRID_W, GRID_W)
    oh[dc[in_win], flat[in_win]] = 1.0
    neg = np.where(in_win, 0.0, -1e30).astype(np.float32).reshape(1, -1)
    return oh, neg


def _assemble_dproj(dq_na, dk_na, dv_na, dq_f, dq_b, dz_f, dz_b, dv_f, dv_b, dg, dgn, dgh, *, name):
    lp, naw = dq_na.shape
    hgw = dq_f.shape[1]
    d = dgn.shape[1]
    cols = 3 * naw + 5 * hgw + 2 * d
    tr = _row_tile(lp, 3 * naw * 4 + 4 * hgw * 4 + 3 * hgw * 2 + 2 * d * 2 + cols * 2)

    def body(nq_ref, nk_ref, nv_ref, qf_ref, qb_ref, zf_ref, zb_ref, vf_ref, vb_ref, g_ref, gn_ref,
             gh_ref, o_ref):
        o_ref[:, 0:naw] = nq_ref[...].astype(BF16)
        o_ref[:, naw:2 * naw] = nk_ref[...].astype(BF16)
        o_ref[:, 2 * naw:3 * naw] = nv_ref[...].astype(BF16)
        c0 = 3 * naw
        o_ref[:, c0:c0 + hgw] = (qf_ref[...] + qb_ref[...]).astype(BF16)
        o_ref[:, c0 + hgw:c0 + 2 * hgw] = zf_ref[...]
        o_ref[:, c0 + 2 * hgw:c0 + 3 * hgw] = zb_ref[...]
        o_ref[:, c0 + 3 * hgw:c0 + 4 * hgw] = (vf_ref[...] + vb_ref[...]).astype(BF16)
        o_ref[:, c0 + 4 * hgw:c0 + 5 * hgw] = g_ref[...]
        o_ref[:, c0 + 5 * hgw:c0 + 5 * hgw + d] = gn_ref[...]
        o_ref[:, c0 + 5 * hgw + d:] = gh_ref[...]

    hg, na = _rspec(tr, hgw), _rspec(tr, naw)
    return pl.pallas_call(
        body, grid=(lp // tr,),
        in_specs=[na, na, na, hg, hg, hg, hg, hg, hg, hg, _rspec(tr, d), _rspec(tr, d)],
        out_specs=_rspec(tr, cols),
        out_shape=jax.ShapeDtypeStruct((lp, cols), BF16),
        compiler_params=_cp("parallel"), name=name)(dq_na, dk_na, dv_na, dq_f, dq_b, dz_f, dz_b,
                                                    dv_f, dv_b, dg, dgn, dgh)


GROUP_STEPS = 8


def _group_tiles(rows, align):
    steps = GROUP_STEPS if all(r % (GROUP_STEPS * align) == 0 for r in rows) else 1
    return steps, [r // steps for r in rows]


def _adamw(ws, gs, ms, vs, *, name):
    n = len(ws)
    steps, trs = _group_tiles([w.shape[0] for w in ws], 8)

    def body(*refs):
        for i in range(n):
            w_ref, g_ref, m_ref, v_ref = refs[4 * i:4 * i + 4]
            d_ref, mo_ref, vo_ref, go_ref = refs[4 * n + 4 * i:4 * n + 4 * i + 4]
            gv = g_ref[...]
            go_ref[...] = gv
            mn = ADAM_B1 * m_ref[...] + (1.0 - ADAM_B1) * gv
            vn = ADAM_B2 * v_ref[...] + (1.0 - ADAM_B2) * (gv * gv)
            m_hat = mn / (1.0 - ADAM_B1 ** ADAM_STEP)
            v_hat = vn / (1.0 - ADAM_B2 ** ADAM_STEP)
            d_ref[...] = -ADAM_LR * (m_hat / (jnp.sqrt(v_hat) + ADAM_EPS) + ADAM_WD * w_ref[...])
            mo_ref[...] = mn
            vo_ref[...] = vn

    specs = [_rspec(tr, w.shape[1]) for tr, w in zip(trs, ws)]
    out = pl.pallas_call(
        body, grid=(steps,),
        in_specs=[s for s in specs for _ in range(4)],
        out_specs=[s for s in specs for _ in range(4)],
        out_shape=[jax.ShapeDtypeStruct(w.shape, F32) for w in ws for _ in range(4)],
        compiler_params=_cp("parallel"), name=name)(*[a for q in zip(ws, gs, ms, vs) for a in q])
    return [tuple(out[4 * i:4 * i + 4]) for i in range(n)]


def _local_step(x, tgt, meta, first_weight, rest_weights, g_mix, g_mlp, g_fin, hg_gain, rpb, lb,
                early_grads=None, mid_grads=None, late_grad=None, rest_landed=None,
                last_grads=None):
    n_tok, d = x.shape
    hgw = hg_gain.shape[1]
    nh, hh = rpb.shape[0], hgw // HG_DK
    naw = nh * NA_HEAD_DIM
    l_real = N_META + n_tok
    lp = -(-l_real // ROW_ALIGN) * ROW_ALIGN
    col_qhg = 3 * naw
    col_zf, col_zb, col_i, col_g = (col_qhg + hgw, col_qhg + 2 * hgw, col_qhg + 3 * hgw,
                                    col_qhg + 4 * hgw)
    col_gate = col_qhg + 5 * hgw

    oh_np, neg_np = _rpb_onehot()
    oh = jnp.asarray(oh_np)
    rpb_p = jnp.pad(rpb.reshape(nh * (2 * NA_WIN_H - 1), 2 * NA_WIN_W - 1),
                    ((0, 0), (0, LANES - (2 * NA_WIN_W - 1))))
    tb = _matmul(rpb_p, oh, tm=rpb_p.shape[0], tn=512, tk=LANES, precision=HIGHEST,
                 name="rpb_expand")
    tb = (tb + jnp.asarray(neg_np)).reshape(nh, 2 * NA_WIN_H - 1, GRID_W, GRID_W)

    h0, a, tgt_p = _embed_norm(x, tgt, meta, g_mix, lp=lp, name="norm_mix")
    w_in = first_weight((a, tb))
    proj = _matmul(a, w_in, tm=lp, name="mm_in")
    o_na, lse = _na_fwd(proj, tb, n_tok=n_tok, nh=nh, name="na_fwd")
    lb_f = lb[0].reshape(hh, 1, HG_DK)
    lb_b = lb[1].reshape(hh, 1, HG_DK)
    scan_kw = dict(col_q=col_qhg, col_i=col_i, hh=hh)
    o_f, st_f = _hg_scan_fwd(proj, lb_f, reverse=False, col_z=col_zf, name="hg_scan_f", **scan_kw)
    token = rest_landed(o_f) if rest_landed else None
    lb_b_late = lb_b if token is None else lb_b + token[0:1, 0:1]
    o_b, st_b = _hg_scan_fwd(proj, lb_b_late, reverse=True, col_z=col_zb, name="hg_scan_b",
                             **scan_kw)
    o_hg = _hg_out(o_f, o_b, proj, hg_gain, col_g=col_g, name="hg_out")
    w_na, w_hg, w_o, w_up, w_down = rest_weights(o_hg)
    y_na = _matmul(o_na, w_na, name="mm_na_out", out_dtype=BF16)
    gates = ((proj, col_gate), (proj, col_gate + d))

    def mix_gates(acc, gn, gh, yn):
        return acc, _sigmoid(gn) * yn + _sigmoid(gh) * acc

    def mix_gates_bwd(dmix, gn, gh, yn, yh):
        sn, sh = _sigmoid(gn), _sigmoid(gh)
        return dmix * sn, dmix * sh, dmix * yn * sn * (1.0 - sn), dmix * yh * sh * (1.0 - sh)

    y_hg, mix = _matmul(o_hg, w_hg, name="mm_hg_out", epilogue=mix_gates,
                        tiles=(*gates, (y_na, 0)), out_dtypes=(BF16, BF16))
    t1 = _matmul(mix, w_o, name="mm_o")
    h1, mlp_in = _residual_norm(h0, t1, g_mlp, name="resid_norm_mlp")
    u, act = _matmul(mlp_in, w_up, name="mm_up", out_dtypes=(BF16, BF16),
                     epilogue=lambda acc: (acc, jnp.square(jnp.maximum(acc, 0.0))))
    t2 = _matmul(act, w_down, name="mm_down")
    dh2, dh2_16, loss, dg_fin = _final_loss(h1, t2, g_fin, tgt_p, n_tok=n_tok, name="final_loss")

    (du,) = _matmul(dh2_16, w_down, tb=True, name="mm_down_dx", tiles=((u, 0),),
                    out_dtypes=(BF16,),
                    epilogue=lambda acc, uv: (acc * 2.0 * jnp.maximum(uv, 0.0),))
    dw_down = _matmul(act, dh2_16, ta=True, name="mm_down_dw")
    dm = _matmul(du, w_up, tb=True, name="mm_up_dx")
    dw_up = _matmul(mlp_in, du, ta=True, name="mm_up_dw")
    dh1, dh1_16, dg_mlp = _rmsnorm_bwd_add(h1, g_mlp, dm, dh2, name="norm_mlp_bwd")
    dy_na, dy_hg, dgn, dgh = _matmul(dh1_16, w_o, tb=True, name="mm_o_dx", epilogue=mix_gates_bwd,
                                     tiles=(*gates, (y_na, 0), (y_hg, 0)), out_dtypes=(BF16,) * 4)
    dw_o = _matmul(mix, dh1_16, ta=True, name="mm_o_dw")
    do_na = _matmul(dy_na, w_na, tb=True, name="mm_na_out_dx")
    dw_na = _matmul(o_na, dy_na, ta=True, name="mm_na_out_dw")
    do_hg = _matmul(dy_hg, w_hg, tb=True, name="mm_hg_out_dx")
    dw_hg = _matmul(o_hg, dy_hg, ta=True, name="mm_hg_out_dw")
    token = early_grads([dw_na, dw_hg, dw_o, dw_up, dw_down]) if early_grads else None
    if token is not None:
        hg_gain = hg_gain + token[0:1, 0:1]
    d_o, dg_hg, d_gain = _hg_out_bwd(o_f, o_b, proj, hg_gain, do_hg, col_g=col_g, name="hg_out_bwd")
    dq_f, dz_f, dv_f, dlb_f = _hg_scan_bwd(proj, lb_f, st_f, d_o, reverse=False, col_z=col_zf,
                                           name="hg_scan_f_bwd", **scan_kw)
    token = mid_grads(dq_f) if mid_grads else None
    lb_b_late = lb_b if token is None else lb_b + token[0:1, 0:1]
    dq_b, dz_b, dv_b, dlb_b = _hg_scan_bwd(proj, lb_b_late, st_b, d_o, reverse=True, col_z=col_zb,
                                           name="hg_scan_b_bwd", **scan_kw)
    dq_na, dk_na, dv_na, dtb = _na_bwd(proj, tb, o_na, lse, do_na, n_tok=n_tok, nh=nh, name="na_bwd")
    dproj = _assemble_dproj(dq_na, dk_na, dv_na, dq_f, dq_b, dz_f, dz_b, dv_f, dv_b, dg_hg, dgn,
                            dgh, name="assemble_dproj")
    dw_in = _matmul(a, dproj, ta=True, name="mm_in_dw")
    token = late_grad(dw_in) if late_grad else None
    da = _matmul(dproj, w_in, tb=True, name="mm_in_dx", after=token)
    token = last_grads(da) if last_grads else None
    g_mix_late = g_mix if token is None else g_mix + token[0:1, 0:1]
    dx, dmeta, dg_mix = _rmsnorm_bwd_tokens(h0, g_mix_late, da, dh1, n_tok=n_tok,
                                            name="norm_mix_bwd")
    d_rpb = _matmul(dtb.reshape(nh * (2 * NA_WIN_H - 1), GRID_W * GRID_W), oh, tb=True,
                    tm=nh * (2 * NA_WIN_H - 1), tn=LANES, tk=1024, precision=HIGHEST,
                    name="rpb_reduce")
    d_lb = jnp.concatenate([dlb_f.reshape(1, hgw), dlb_b.reshape(1, hgw)], axis=0)
    return (loss, dx, dmeta, dw_in, dw_na, dw_hg, dw_o, dw_up, dw_down,
            dg_mix, dg_mlp, dg_fin, d_gain, d_rpb, d_lb)


N_CHIPS = 4
N_DEV = 8
ANY = pl.BlockSpec(memory_space=pl.ANY)


def _place():
    x, y, c = lax.axis_index("x"), lax.axis_index("y"), lax.axis_index("c")
    others = []
    for j in (1, 2, 3):
        tx = (1 - x) if (j >> 1) else x
        ty = (1 - y) if (j & 1) else y
        others.append((tx, ty))
    return x, y, c, others


def _piece(ref, axis, k, half, rh, cs):
    if axis == 1:
        return ref.at[pl.ds(pl.multiple_of(half * rh, 16), rh), pl.ds(pl.multiple_of(k * cs, LANES), cs)]
    return ref.at[pl.ds(pl.multiple_of(k * 2 * rh + half * rh, 16), rh), :]


def _cast_into_full(shards, axes, place, *, name):
    n = len(shards)
    steps, trs = _group_tiles([s.shape[0] for s in shards], 16)

    def body(p_ref, *refs):
        for i in range(n):
            refs[n + i][...] = refs[i][...].astype(BF16)

    def out_spec(tr, cs, axis):
        if axis == 1:
            return pl.BlockSpec((tr, cs), lambda i, p_ref: (i, p_ref[0]))
        return pl.BlockSpec((tr, cs), lambda i, p_ref: (p_ref[0] * steps + i, 0))

    return pl.pallas_call(
        body,
        grid_spec=pltpu.PrefetchScalarGridSpec(
            num_scalar_prefetch=1, grid=(steps,),
            in_specs=[pl.BlockSpec((tr, s.shape[1]), lambda i, p_ref: (i, 0))
                      for tr, s in zip(trs, shards)],
            out_specs=[out_spec(tr, s.shape[1], ax) for tr, s, ax in zip(trs, shards, axes)]),
        out_shape=[jax.ShapeDtypeStruct((s.shape[0], s.shape[1] * N_CHIPS) if ax == 1
                                        else (s.shape[0] * N_CHIPS, s.shape[1]), BF16)
                   for s, ax in zip(shards, axes)],
        compiler_params=_cp("parallel"), name=name)(place, *shards)


HBM_SPEC = pl.BlockSpec(memory_space=pltpu.HBM)
SEM_SPEC = pl.BlockSpec(memory_space=pltpu.SEMAPHORE)
SPLIT_COPY = pltpu.CompilerParams(has_side_effects=pltpu.SideEffectType.DATAFLOW_SIDE_EFFECTING)
TOKEN = jax.ShapeDtypeStruct((8, LANES), F32)


def _geo(fulls, axes):
    out = []
    for f, ax in zip(fulls, axes):
        r, cs = (f.shape[0], f.shape[1] // N_CHIPS) if ax == 1 else (f.shape[0] // N_CHIPS, f.shape[1])
        out.append((ax, r // 2, cs))
    return out


def _gather_copies(refs, geo, send_sems, recv_sems):
    x, y, c, others = _place()
    chip = 2 * x + y
    cps = []
    for i, (ax, rh, cs) in enumerate(geo):
        mine = _piece(refs[i], ax, chip, c, rh, cs)
        for j, (tx, ty) in enumerate(others):
            cps.append(pltpu.make_async_remote_copy(
                src_ref=mine, dst_ref=mine, send_sem=send_sems.at[3 * i + j],
                recv_sem=recv_sems.at[3 * i + j], device_id=(tx, ty, c), device_id_type=MESH))
    return cps


def _allgather_start(fulls, axes, after, *, name):
    n = len(fulls)
    geo = _geo(fulls, axes)

    def body(*refs):
        w_refs = refs[:n]
        send_sems, recv_sems = refs[n + 1], refs[n + 2]
        token = refs[2 * n + 3]
        for cp in _gather_copies(w_refs, geo, send_sems, recv_sems):
            cp.start()
        token[...] = jnp.zeros_like(token)

    out = pl.pallas_call(
        body, name=name,
        out_shape=(pltpu.SemaphoreType.DMA((3 * n,)), pltpu.SemaphoreType.DMA((3 * n,)),
                   *[pltpu.HBM(f.shape, f.dtype) for f in fulls], TOKEN),
        in_specs=[HBM_SPEC] * n + [ANY],
        out_specs=(SEM_SPEC, SEM_SPEC, *[HBM_SPEC] * n, pl.BlockSpec(memory_space=pltpu.VMEM)),
        input_output_aliases={i: 2 + i for i in range(n)},
        compiler_params=SPLIT_COPY,
    )(*[pltpu.with_memory_space_constraint(f, pltpu.HBM) for f in fulls], after)
    return out[0], out[1], list(out[2:2 + n]), out[2 + n]


def _allgather_wait(send_sems, recv_sems, fulls, axes, after, *, name):
    n = len(fulls)
    geo = _geo(fulls, axes)
    afters = tuple(after) if isinstance(after, (tuple, list)) else (after,)

    def body(*refs):
        w_refs = refs[:n]
        for cp in _gather_copies(w_refs, geo, refs[n], refs[n + 1]):
            cp.wait_send()
            cp.wait_recv()

    return list(pl.pallas_call(
        body, name=name,
        out_shape=[pltpu.HBM(f.shape, f.dtype) for f in fulls],
        in_specs=[HBM_SPEC] * n + [SEM_SPEC, SEM_SPEC] + [ANY] * len(afters),
        out_specs=[HBM_SPEC] * n,
        input_output_aliases={i: i for i in range(n)},
        compiler_params=SPLIT_COPY,
    )(*fulls, send_sems, recv_sems, *afters))


def _allgather_forward(fulls, axes, *, name):
    n = len(fulls)
    geo = _geo(fulls, axes)

    def body(*refs):
        o_refs = refs[n:2 * n]
        send_sems, recv_sems = refs[2 * n:]
        x, y, c, others = _place()

        def rcopy(i, j, half, to):
            ax, rh, cs = geo[i]
            ref = _piece(o_refs[i], ax, 2 * others[j][0] + others[j][1], half, rh, cs)
            return pltpu.make_async_remote_copy(
                src_ref=ref, dst_ref=ref, send_sem=send_sems.at[3 * i + j],
                recv_sem=recv_sems.at[3 * i + j], device_id=to, device_id_type=MESH)

        cps = [rcopy(i, j, c, (x, y, 1 - c)) for i in range(n) for j in range(3)]
        for cp in cps:
            cp.start()
        for i in range(n):
            for j in range(3):
                rcopy(i, j, 1 - c, (x, y, c)).wait_recv()
        for cp in cps:
            cp.wait_send()

    return list(pl.pallas_call(
        body, in_specs=[ANY] * n, out_specs=[ANY] * n,
        out_shape=[jax.ShapeDtypeStruct(f.shape, f.dtype) for f in fulls],
        input_output_aliases={i: i for i in range(n)},
        scratch_shapes=[pltpu.SemaphoreType.DMA((3 * n,)), pltpu.SemaphoreType.DMA((3 * n,))],
        name=name)(*fulls))


def _forward_copies(refs, geo, send_sems, recv_sems):
    x, y, c, others = _place()
    cps = []
    for i, (ax, rh, cs) in enumerate(geo):
        for j, (tx, ty) in enumerate(others):
            ref = _piece(refs[i], ax, 2 * tx + ty, c, rh, cs)
            cps.append(pltpu.make_async_remote_copy(
                src_ref=ref, dst_ref=ref, send_sem=send_sems.at[3 * i + j],
                recv_sem=recv_sems.at[3 * i + j], device_id=(x, y, 1 - c), device_id_type=MESH))
    return cps


def _allgather_forward_start(fulls, axes, *, name):
    n = len(fulls)
    geo = _geo(fulls, axes)

    def body(*refs):
        token = refs[2 * n + 2]
        for cp in _forward_copies(refs[:n], geo, refs[n], refs[n + 1]):
            cp.start()
        token[...] = jnp.zeros_like(token)

    out = pl.pallas_call(
        body, name=name,
        out_shape=(pltpu.SemaphoreType.DMA((3 * n,)), pltpu.SemaphoreType.DMA((3 * n,)),
                   *[pltpu.HBM(f.shape, f.dtype) for f in fulls], TOKEN),
        in_specs=[HBM_SPEC] * n,
        out_specs=(SEM_SPEC, SEM_SPEC, *[HBM_SPEC] * n, pl.BlockSpec(memory_space=pltpu.VMEM)),
        input_output_aliases={i: 2 + i for i in range(n)},
        compiler_params=SPLIT_COPY,
    )(*fulls)
    return out[0], out[1], list(out[2:2 + n]), out[2 + n]


def _allgather_forward_wait(send_sems, recv_sems, fulls, axes, after, *, name):
    n = len(fulls)
    geo = _geo(fulls, axes)

    def body(*refs):
        for cp in _forward_copies(refs[:n], geo, refs[n], refs[n + 1]):
            cp.wait_send()
            cp.wait_recv()

    return list(pl.pallas_call(
        body, name=name,
        out_shape=[pltpu.HBM(f.shape, f.dtype) for f in fulls],
        in_specs=[HBM_SPEC] * n + [SEM_SPEC, SEM_SPEC, ANY],
        out_specs=[HBM_SPEC] * n,
        input_output_aliases={i: i for i in range(n)},
        compiler_params=SPLIT_COPY,
    )(*fulls, send_sems, recv_sems, after))


def _chip_copies(blk_ref, land_ref, send_sems, recv_sems):
    x, y, c, others = _place()
    return [pltpu.make_async_remote_copy(
        src_ref=blk_ref, dst_ref=land_ref.at[2 * x + y], send_sem=send_sems.at[j],
        recv_sem=recv_sems.at[j], device_id=(tx, ty, c), device_id_type=MESH)
        for j, (tx, ty) in enumerate(others)]


def _chip_exchange_start(blk, *, name):
    land = pltpu.with_memory_space_constraint(lax.empty((N_CHIPS, *blk.shape), blk.dtype), pltpu.HBM)

    def body(blk_ref, land_ref, send_sems, recv_sems, blk_out, land_out, token):
        for cp in _chip_copies(blk_ref, land_ref, send_sems, recv_sems):
            cp.start()
        token[...] = jnp.zeros_like(token)

    return pl.pallas_call(
        body, name=name,
        out_shape=(pltpu.SemaphoreType.DMA((3,)), pltpu.SemaphoreType.DMA((3,)),
                   pltpu.HBM(blk.shape, blk.dtype), pltpu.HBM(land.shape, land.dtype), TOKEN),
        in_specs=[HBM_SPEC] * 2,
        out_specs=(SEM_SPEC, SEM_SPEC, HBM_SPEC, HBM_SPEC, pl.BlockSpec(memory_space=pltpu.VMEM)),
        input_output_aliases={0: 2, 1: 3},
        compiler_params=SPLIT_COPY,
    )(pltpu.with_memory_space_constraint(blk, pltpu.HBM), land)


def _chip_exchange_wait(send_sems, recv_sems, blk, land, after, *, name):
    def body(blk_ref, land_ref, send_sems, recv_sems, after_ref, blk_out, land_out):
        for cp in _chip_copies(blk_ref, land_ref, send_sems, recv_sems):
            cp.wait_send()
            cp.wait_recv()

    return pl.pallas_call(
        body, name=name,
        out_shape=[pltpu.HBM(blk.shape, blk.dtype), pltpu.HBM(land.shape, land.dtype)],
        in_specs=[HBM_SPEC] * 2 + [SEM_SPEC, SEM_SPEC, ANY],
        out_specs=[HBM_SPEC] * 2,
        input_output_aliases={0: 0, 1: 1},
        compiler_params=SPLIT_COPY,
    )(blk, land, send_sems, recv_sems, after)[1]


def _scatter_geo(parts, axes):
    out = []
    for p, ax in zip(parts, axes):
        _, rh, cols = p.shape
        out.append((ax, rh, cols // N_CHIPS if ax == 1 else cols))
    return out


def _scatter_copies(p_refs, q_refs, geo, send_sems, recv_sems):
    x, y, c, others = _place()
    chip = 2 * x + y
    cps = []
    for i, (ax, rh, cw) in enumerate(geo):
        for j, (tx, ty) in enumerate(others):
            k = 2 * tx + ty
            src = (p_refs[i].at[0, :, pl.ds(pl.multiple_of(k * cw, LANES), cw)] if ax == 1
                   else p_refs[i].at[k])
            cps.append(pltpu.make_async_remote_copy(
                src_ref=src, dst_ref=q_refs[i].at[chip], send_sem=send_sems.at[3 * i + j],
                recv_sem=recv_sems.at[3 * i + j], device_id=(tx, ty, c), device_id_type=MESH))
    return cps


def _scatter_start(parts, axes, *, name):
    n = len(parts)
    geo = _scatter_geo(parts, axes)
    slots = [pltpu.HBM((N_CHIPS, rh, cw), p.dtype) for p, (_, rh, cw) in zip(parts, geo)]

    def body(*refs):
        p_refs, q_refs = refs[:n], refs[n:2 * n]
        send_sems, recv_sems = refs[2 * n], refs[2 * n + 1]
        token = refs[4 * n + 2]
        for cp in _scatter_copies(p_refs, q_refs, geo, send_sems, recv_sems):
            cp.start()
        token[...] = jnp.zeros_like(token)

    land = [pltpu.with_memory_space_constraint(lax.empty(s.inner_aval.shape, s.inner_aval.dtype), pltpu.HBM)
            for s in slots]
    out = pl.pallas_call(
        body, name=name,
        out_shape=(pltpu.SemaphoreType.DMA((3 * n,)), pltpu.SemaphoreType.DMA((3 * n,)),
                   *[pltpu.HBM(p.shape, p.dtype) for p in parts], *slots, TOKEN),
        in_specs=[HBM_SPEC] * (2 * n),
        out_specs=(SEM_SPEC, SEM_SPEC, *[HBM_SPEC] * (2 * n), pl.BlockSpec(memory_space=pltpu.VMEM)),
        input_output_aliases={i: 2 + i for i in range(2 * n)},
        compiler_params=SPLIT_COPY,
    )(*[pltpu.with_memory_space_constraint(p, pltpu.HBM) for p in parts], *land)
    return out[0], out[1], list(out[2:2 + n]), list(out[2 + n:2 + 2 * n]), out[2 + 2 * n]


def _scatter_wait(send_sems, recv_sems, parts, slots, axes, after, *, name):
    n = len(parts)
    geo = _scatter_geo(parts, axes)

    def body(*refs):
        p_refs, q_refs = refs[:n], refs[n:2 * n]
        for cp in _scatter_copies(p_refs, q_refs, geo, refs[2 * n], refs[2 * n + 1]):
            cp.wait_send()
            cp.wait_recv()

    out = pl.pallas_call(
        body, name=name,
        out_shape=[pltpu.HBM(a.shape, a.dtype) for a in (*parts, *slots)],
        in_specs=[HBM_SPEC] * (2 * n) + [SEM_SPEC, SEM_SPEC, ANY],
        out_specs=[HBM_SPEC] * (2 * n),
        input_output_aliases={i: i for i in range(2 * n)},
        compiler_params=SPLIT_COPY,
    )(*parts, *slots, send_sems, recv_sems, after)
    return list(out[:n]), list(out[n:])


def _swap_copies(g_refs, r_refs, shapes, send_sems, recv_sems):
    x, y, c, _ = _place()
    cps = []
    for i, shape in enumerate(shapes):
        rh = shape[1] // 2
        src = g_refs[i].at[:, pl.ds(pl.multiple_of((1 - c) * rh, 16), rh), :]
        cps.append(pltpu.make_async_remote_copy(
            src_ref=src, dst_ref=r_refs[i], send_sem=send_sems.at[i], recv_sem=recv_sems.at[i],
            device_id=(x, y, 1 - c), device_id_type=MESH))
    return cps


def _sibling_swap_start(grads, *, name):
    n = len(grads)
    shapes = [g.shape for g in grads]
    lands = [pltpu.HBM((s[0], s[1] // 2, s[2]), g.dtype) for s, g in zip(shapes, grads)]

    def body(*refs):
        g_refs, r_refs = refs[:n], refs[n:2 * n]
        token = refs[4 * n + 2]
        for cp in _swap_copies(g_refs, r_refs, shapes, refs[2 * n], refs[2 * n + 1]):
            cp.start()
        token[...] = jnp.zeros_like(token)

    land = [pltpu.with_memory_space_constraint(lax.empty(s.inner_aval.shape, s.inner_aval.dtype), pltpu.HBM)
            for s in lands]
    out = pl.pallas_call(
        body, name=name,
        out_shape=(pltpu.SemaphoreType.DMA((n,)), pltpu.SemaphoreType.DMA((n,)),
                   *[pltpu.HBM(g.shape, g.dtype) for g in grads], *lands, TOKEN),
        in_specs=[HBM_SPEC] * (2 * n),
        out_specs=(SEM_SPEC, SEM_SPEC, *[HBM_SPEC] * (2 * n), pl.BlockSpec(memory_space=pltpu.VMEM)),
        input_output_aliases={i: 2 + i for i in range(2 * n)},
        compiler_params=SPLIT_COPY,
    )(*[pltpu.with_memory_space_constraint(g, pltpu.HBM) for g in grads], *land)
    return out[0], out[1], list(out[2:2 + n]), list(out[2 + n:2 + 2 * n]), out[2 + 2 * n]


def _sibling_swap_wait(send_sems, recv_sems, grads, lands, after, *, name):
    n = len(grads)
    shapes = [g.shape for g in grads]

    def body(*refs):
        g_refs, r_refs = refs[:n], refs[n:2 * n]
        for cp in _swap_copies(g_refs, r_refs, shapes, refs[2 * n], refs[2 * n + 1]):
            cp.wait_send()
            cp.wait_recv()

    out = pl.pallas_call(
        body, name=name,
        out_shape=[pltpu.HBM(a.shape, a.dtype) for a in (*grads, *lands)],
        in_specs=[HBM_SPEC] * (2 * n) + [SEM_SPEC, SEM_SPEC, ANY],
        out_specs=[HBM_SPEC] * (2 * n),
        input_output_aliases={i: i for i in range(2 * n)},
        compiler_params=SPLIT_COPY,
    )(*grads, *lands, send_sems, recv_sems, after)
    return list(out[:n]), list(out[n:])


def _pair_add(g3s, rxs, place, *, out_dtype, name):
    n = len(g3s)
    steps, trs = _group_tiles([g.shape[1] // 2 for g in g3s], 16)

    def body(p_ref, *refs):
        for i in range(n):
            refs[2 * n + i][...] = (refs[2 * i][...] + refs[2 * i + 1][...]).astype(out_dtype)

    in_specs, out_specs = [], []
    for g, tr in zip(g3s, trs):
        blk = (g.shape[0], tr, g.shape[2])
        in_specs += [pl.BlockSpec(blk, lambda i, p_ref: (0, p_ref[1] * steps + i, 0)),
                     pl.BlockSpec(blk, lambda i, p_ref: (0, i, 0))]
        out_specs.append(pl.BlockSpec(blk, lambda i, p_ref: (0, i, 0)))
    return pl.pallas_call(
        body,
        grid_spec=pltpu.PrefetchScalarGridSpec(
            num_scalar_prefetch=1, grid=(steps,), in_specs=in_specs, out_specs=out_specs),
        out_shape=[jax.ShapeDtypeStruct((g.shape[0], g.shape[1] // 2, g.shape[2]), out_dtype)
                   for g in g3s],
        compiler_params=_cp("parallel"), name=name)(place, *[a for q in zip(g3s, rxs) for a in q])


def _sum_slots(q, *, name):
    ns, rows, cols = q.shape
    tr = next(t for t in (128, 64, 32, 16, 8) if rows % t == 0)

    def body(q_ref, o_ref):
        acc = q_ref[0].astype(F32)
        for k in range(1, ns):
            acc = acc + q_ref[k].astype(F32)
        o_ref[...] = acc

    return pl.pallas_call(
        body, grid=(rows // tr,),
        in_specs=[pl.BlockSpec((ns, tr, cols), lambda i: (0, i, 0))],
        out_specs=_rspec(tr, cols),
        out_shape=jax.ShapeDtypeStruct((rows, cols), F32),
        compiler_params=_cp("parallel"), name=name)(q)


def _sum_chips(qs, ps, place, axes, *, name):
    n = len(qs)
    per = N_CHIPS + 1
    steps, trs = _group_tiles([q.shape[1] for q in qs], 16)

    def body(p_ref, *refs):
        chip = p_ref[0]
        for i in range(n):
            q_refs, own_ref = refs[per * i:per * i + N_CHIPS], refs[per * i + N_CHIPS]
            acc = jnp.where(chip == 0, own_ref[...], q_refs[0][...]).astype(F32)
            for k in range(1, N_CHIPS):
                acc = acc + jnp.where(chip == k, own_ref[...], q_refs[k][...]).astype(F32)
            refs[per * n + i][...] = acc

    def slot_spec(k, tr, cw):
        return pl.BlockSpec((None, tr, cw),
                            lambda i, p_ref: (jnp.where(p_ref[0] == k, (k + 1) % N_CHIPS, k), i, 0))

    in_specs, out_specs, operands = [], [], []
    for q, p, ax, tr in zip(qs, ps, axes, trs):
        cw = q.shape[2]
        in_specs += [slot_spec(k, tr, cw) for k in range(N_CHIPS)]
        in_specs.append(pl.BlockSpec((None, tr, cw), (lambda i, p_ref: (0, i, p_ref[0])) if ax == 1
                                     else (lambda i, p_ref: (p_ref[0], i, 0))))
        out_specs.append(pl.BlockSpec((tr, cw), lambda i, p_ref: (p_ref[1] * steps + i, 0)))
        operands += [q] * N_CHIPS + [p]
    return pl.pallas_call(
        body,
        grid_spec=pltpu.PrefetchScalarGridSpec(
            num_scalar_prefetch=1, grid=(steps,), in_specs=in_specs, out_specs=out_specs),
        out_shape=[jax.ShapeDtypeStruct((2 * q.shape[1], q.shape[2]), F32) for q in qs],
        compiler_params=_cp("parallel"), name=name)(place, *operands)


def _sibling_share(shards, *, name):
    n = len(shards)

    def body(*refs):
        o_refs = refs[n:2 * n]
        send_sems, recv_sems = refs[2 * n:]
        x, y, c, _ = _place()
        cps = []
        for i in range(n):
            rh = shards[i].shape[0] // 2
            mine = o_refs[i].at[pl.ds(pl.multiple_of(c * rh, 8), rh), :]
            cp = pltpu.make_async_remote_copy(
                src_ref=mine, dst_ref=mine, send_sem=send_sems.at[i], recv_sem=recv_sems.at[i],
                device_id=(x, y, 1 - c), device_id_type=MESH)
            cp.start()
            cps.append(cp)
        for i in range(n):
            rh = shards[i].shape[0] // 2
            theirs = o_refs[i].at[pl.ds(pl.multiple_of((1 - c) * rh, 8), rh), :]
            pltpu.make_async_remote_copy(
                src_ref=theirs, dst_ref=theirs, send_sem=send_sems.at[i], recv_sem=recv_sems.at[i],
                device_id=(x, y, c), device_id_type=MESH).wait_recv()
        for cp in cps:
            cp.wait_send()

    return pl.pallas_call(
        body, in_specs=[ANY] * n, out_specs=[ANY] * n,
        out_shape=[jax.ShapeDtypeStruct(h.shape, h.dtype) for h in shards],
        input_output_aliases={i: i for i in range(n)},
        scratch_shapes=[pltpu.SemaphoreType.DMA((n,)), pltpu.SemaphoreType.DMA((n,))],
        name=name)(*shards)


def _gather_all(blk, *, name, after=None, shares=()):
    rows, cols = blk.shape
    extra = [] if after is None else [after]
    n_s = len(shares)

    def body(x_ref, *refs):
        s_refs = refs[len(extra) + n_s + 1:len(extra) + 2 * n_s + 1]
        out_ref = refs[len(extra) + n_s]
        send_sems, recv_sems, local_sem, s_send, s_recv = refs[len(extra) + 2 * n_s + 1:]
        x, y, c = lax.axis_index("x"), lax.axis_index("y"), lax.axis_index("c")
        me = 4 * x + 2 * y + c
        mine = pltpu.make_async_copy(x_ref, out_ref.at[me], local_sem)
        mine.start()
        cps = []
        for i in range(n_s):
            rh = shares[i].shape[0] // 2
            half = s_refs[i].at[pl.ds(pl.multiple_of(c * rh, 8), rh), :]
            cp = pltpu.make_async_remote_copy(
                src_ref=half, dst_ref=half, send_sem=s_send.at[i], recv_sem=s_recv.at[i],
                device_id=(x, y, 1 - c), device_id_type=MESH)
            cp.start()
            cps.append(cp)
        for k in range(1, N_DEV):
            tx = (1 - x) if (k >> 2) & 1 else x
            ty = (1 - y) if (k >> 1) & 1 else y
            tc = (1 - c) if k & 1 else c
            cp = pltpu.make_async_remote_copy(
                src_ref=x_ref, dst_ref=out_ref.at[me], send_sem=send_sems.at[k - 1],
                recv_sem=recv_sems.at[k - 1], device_id=(tx, ty, tc), device_id_type=MESH)
            cp.start()
            cps.append(cp)
        for k in range(1, N_DEV):
            tx = (1 - x) if (k >> 2) & 1 else x
            ty = (1 - y) if (k >> 1) & 1 else y
            tc = (1 - c) if k & 1 else c
            got = out_ref.at[4 * tx + 2 * ty + tc]
            pltpu.make_async_remote_copy(
                src_ref=got, dst_ref=got, send_sem=send_sems.at[k - 1], recv_sem=recv_sems.at[k - 1],
                device_id=(x, y, c), device_id_type=MESH).wait_recv()
        for i in range(n_s):
            rh = shares[i].shape[0] // 2
            theirs = s_refs[i].at[pl.ds(pl.multiple_of((1 - c) * rh, 8), rh), :]
            pltpu.make_async_remote_copy(
                src_ref=theirs, dst_ref=theirs, send_sem=s_send.at[i], recv_sem=s_recv.at[i],
                device_id=(x, y, c), device_id_type=MESH).wait_recv()
        for cp in cps:
            cp.wait_send()
        mine.wait()

    vm = pl.BlockSpec(memory_space=pltpu.VMEM)
    out = pl.pallas_call(
        body, in_specs=[vm] + [ANY] * (len(extra) + n_s), out_specs=[vm] + [ANY] * n_s,
        out_shape=[jax.ShapeDtypeStruct((N_DEV, rows, cols), blk.dtype)]
        + [jax.ShapeDtypeStruct(s.shape, s.dtype) for s in shares],
        input_output_aliases={1 + len(extra) + i: 1 + i for i in range(n_s)},
        scratch_shapes=[pltpu.SemaphoreType.DMA((N_DEV - 1,)), pltpu.SemaphoreType.DMA((N_DEV - 1,)),
                        pltpu.SemaphoreType.DMA, pltpu.SemaphoreType.DMA((max(n_s, 1),)),
                        pltpu.SemaphoreType.DMA((max(n_s, 1),))],
        name=name)(blk, *extra, *shares)
    return (out[0], *out[1:]) if n_s else out[0]


def _as_rows(a):
    flat = a.reshape(-1)
    n = flat.shape[0]
    rows = -(-n // (8 * LANES)) * 8
    return jnp.pad(flat, (0, rows * LANES - n)).reshape(rows, LANES)


def _from_rows(p, shape):
    n = int(np.prod(shape))
    return p.reshape(-1)[:n].reshape(shape)


WEIGHT_AXES = (1, 1, 1, 0, 1, 0)
WIRE = BF16


def kernel(x, meta_tokens, w_in, w_na_out, w_hg_out, w_o, w_up, w_down, norm_mix, norm_mlp, norm_final, hg_norm, na_rpb, hg_lb_logits, loss_target, m_meta_tokens, m_w_in, m_w_na_out, m_w_hg_out, m_w_o, m_w_up, m_w_down, m_norm_mix, m_norm_mlp, m_norm_final, m_hg_norm, m_na_rpb, m_hg_lb_logits, v_meta_tokens, v_w_in, v_w_na_out, v_w_hg_out, v_w_o, v_w_up, v_w_down, v_norm_mix, v_norm_mlp, v_norm_final, v_hg_norm, v_na_rpb, v_hg_lb_logits):
    xi, yi, ci = lax.axis_index("x"), lax.axis_index("y"), lax.axis_index("c")
    chip = 2 * xi + yi
    d = x.shape[-1]
    dshard = meta_tokens.shape[1]
    hgw = hg_norm.shape[1]
    lbs = hg_lb_logits.shape[2]
    big = [w_in[0], w_na_out[0], w_hg_out[0], w_o[0], w_up[0], w_down[0]]
    big_m = [m_w_in[0], m_w_na_out[0], m_w_hg_out[0], m_w_o[0], m_w_up[0], m_w_down[0]]
    big_v = [v_w_in[0], v_w_na_out[0], v_w_hg_out[0], v_w_o[0], v_w_up[0], v_w_down[0]]

    place = jnp.stack([chip, ci]).astype(jnp.int32)
    in_axes, rest_axes = WEIGHT_AXES[:1], WEIGHT_AXES[1:]
    own_w = _cast_into_full(big, WEIGHT_AXES, place, name="cast_shards")
    small_in = jnp.concatenate([_as_rows(meta_tokens), _as_rows(hg_lb_logits)], axis=0)
    sm_send, sm_recv, sm_blk, sm_land, sm_token = _chip_exchange_start(small_in,
                                                                       name="small_params_start")
    in_send, in_recv, in_bufs, in_token = _allgather_start(own_w[:1], in_axes, sm_token,
                                                           name="weight_allgather_in_start")
    ag_send, ag_recv, ag_bufs, ag_token = _allgather_start(own_w[1:], rest_axes, in_token,
                                                           name="weight_allgather_rest_start")
    sm_land = _chip_exchange_wait(sm_send, sm_recv, sm_blk, sm_land, ag_token,
                                  name="small_params_wait")
    small_all = lax.dynamic_update_slice(sm_land, small_in[None], (chip, 0, 0))
    forward = {}

    def first_weight(after):
        got = _allgather_wait(in_send, in_recv, in_bufs, in_axes, after,
                              name="weight_allgather_in_wait")
        return _allgather_forward(got, in_axes, name="weight_allgather_in_forward")[0]

    def rest_landed(after):
        got = _allgather_wait(ag_send, ag_recv, ag_bufs, rest_axes, after,
                              name="weight_allgather_rest_wait")
        send, recv, bufs, token = _allgather_forward_start(
            got, rest_axes, name="weight_allgather_rest_forward_start")
        forward["rest"] = (send, recv, bufs)
        return token

    def rest_weights(after):
        return _allgather_forward_wait(*forward["rest"], rest_axes, after,
                                       name="weight_allgather_rest_forward_wait")

    n_meta_rows = N_META * dshard // LANES
    meta_full = (small_all[:, :n_meta_rows].reshape(N_CHIPS, N_META, dshard)
                 .transpose(1, 0, 2).reshape(N_META, d))
    lbl_full = (small_all[:, n_meta_rows:].reshape(N_CHIPS, -1)[:, :4 * lbs]
                .reshape(N_CHIPS, 2, 2, lbs).transpose(1, 2, 0, 3).reshape(2, 2, N_CHIPS * lbs))
    lb = jax.nn.softmax(lbl_full, axis=1)[:, 0]

    def by_chip(dws, axes):
        return [g.reshape(1, *g.shape) if ax == 1
                else g.reshape(N_CHIPS, g.shape[0] // N_CHIPS, g.shape[1]) for g, ax in zip(dws, axes)]

    flying = {}

    def scatter(tag, axes, g3, rx):
        parts = _pair_add(g3, rx, place, out_dtype=WIRE, name=f"grad_pair_add_{tag}")
        send, recv, parts, slots, token = _scatter_start(parts, axes,
                                                         name=f"grad_scatter_{tag}_start")
        flying[tag] = (send, recv, parts, slots)
        return token

    def swap(tag, axes):
        def start(dws):
            send, recv, g3, lands, token = _sibling_swap_start(
                by_chip(dws, axes), name=f"grad_sibling_swap_{tag}_start")
            flying["swap_" + tag] = (send, recv, g3, lands)
            return token

        def finish(after):
            g3, rx = _sibling_swap_wait(*flying["swap_" + tag], after,
                                        name=f"grad_sibling_swap_{tag}_wait")
            return scatter(tag, axes, g3, rx)
        return start, finish

    swap_rest, scatter_rest = swap("rest", rest_axes)
    swap_in, scatter_in = swap("in", in_axes)

    def landed(tag, axes, after):
        return _scatter_wait(*flying[tag], axes, after, name=f"grad_scatter_{tag}_wait")

    (loss, dx, dmeta, *_, dg_mix, dg_mlp, dg_fin, d_gain, d_rpb, d_lb) = _local_step(
        x[0], loss_target[0], meta_full, first_weight, rest_weights, norm_mix, norm_mlp,
        norm_final.reshape(1, d), hg_norm, na_rpb[0], lb, swap_rest, scatter_rest,
        lambda dw_in: swap_in([dw_in]), rest_landed, scatter_in)

    parts_rest, slots_rest = landed("rest", rest_axes, dx)
    g_rest = _sibling_share(_sum_chips(slots_rest, parts_rest, place, rest_axes,
                                       name="grad_sum_chips_rest"),
                            name="grad_sibling_share_rest")
    out_rest = _adamw(big[1:], g_rest, big_m[1:], big_v[1:], name="adamw_rest")
    parts_in, slots_in = landed("in", in_axes, out_rest[-1][0])
    half_in = _sum_chips(slots_in, parts_in, place, in_axes, name="grad_sum_chips_in")

    d_rpb_c = d_rpb[:, :2 * NA_WIN_W - 1]
    small_g = [dmeta, dg_mix, dg_mlp, dg_fin, d_gain, d_rpb_c, d_lb, loss]
    packed = jnp.concatenate([_as_rows(a) for a in small_g], axis=0)
    gathered, g_in = _gather_all(packed, shares=half_in, name="gather_small_grads")
    total = _sum_slots(gathered, name="sum_small_grads")
    offs = np.cumsum([0] + [_as_rows(a).shape[0] for a in small_g])
    take = lambda i, shape: _from_rows(total[offs[i]:offs[i + 1]], shape)
    g_meta_full = take(0, (N_META, d))
    g_norm_mix, g_norm_mlp = take(1, (1, d)), take(2, (1, d))
    g_norm_final = take(3, (d,))
    g_hg_norm = take(4, (1, hgw))
    g_rpb = take(5, na_rpb.shape)
    g_lb = take(6, (2, hgw))
    loss_total = take(7, (1, LANES))[0, 0]
    g_meta = lax.dynamic_slice_in_dim(g_meta_full, chip * dshard, dshard, axis=1)
    dl0 = lb * (1.0 - lb) * g_lb
    g_lbl_full = jnp.stack([dl0, -dl0], axis=1)
    g_lbl = lax.dynamic_slice_in_dim(g_lbl_full, chip * lbs, lbs, axis=2)

    big_out = _adamw(big[:1], [g_in], big_m[:1], big_v[:1], name="adamw_in") + out_rest
    small_w = [meta_tokens, norm_mix, norm_mlp, norm_final, hg_norm, na_rpb, hg_lb_logits]
    small_gr = [g_meta, g_norm_mix, g_norm_mlp, g_norm_final, g_hg_norm, g_rpb, g_lbl]
    small_m = [m_meta_tokens, m_norm_mix, m_norm_mlp, m_norm_final, m_hg_norm, m_na_rpb, m_hg_lb_logits]
    small_v = [v_meta_tokens, v_norm_mix, v_norm_mlp, v_norm_final, v_hg_norm, v_na_rpb, v_hg_lb_logits]
    pk = lambda lst: jnp.concatenate([_as_rows(a) for a in lst], axis=0)
    ((sd, sm, sv, _),) = _adamw([pk(small_w)], [pk(small_gr)], [pk(small_m)], [pk(small_v)],
                             name="adamw_small")
    soffs = np.cumsum([0] + [_as_rows(a).shape[0] for a in small_w])
    unpk = lambda p: [_from_rows(p[soffs[i]:soffs[i + 1]], small_w[i].shape) for i in range(len(small_w))]
    sd, sm, sv = unpk(sd), unpk(sm), unpk(sv)

    def order(bigs, smalls):
        return [smalls[0]] + [b.reshape(1, *b.shape) for b in bigs] + smalls[1:]

    grads = order([o[3] for o in big_out], small_gr)
    deltas = order([o[0] for o in big_out], sd)
    new_m = order([o[1] for o in big_out], sm)
    new_v = order([o[2] for o in big_out], sv)
    return (loss_total, dx.reshape(1, *dx.shape), *grads, *deltas, *new_m, *new_v)
```

```python
import functools

import numpy as np
import jax
import jax.numpy as jnp
from jax import lax
from jax.experimental import pallas as pl
from jax.experimental.pallas import tpu as pltpu

F32 = jnp.float32
BF16 = jnp.bfloat16
HIGHEST = lax.Precision.HIGHEST

GRID_W = 64
N_META = 16
EPS = 1e-6
NA_HEAD_DIM = 64
NA_WIN_H = 8
NA_WIN_W = 16
HG_DK = 128
LANES = 128
ROW_ALIGN = 128
VMEM_LIMIT = 48 * 1024 * 1024
ADAM_LR = 0.001
ADAM_B1 = 0.9
ADAM_B2 = 0.999
ADAM_EPS = 1e-08
ADAM_WD = 0.01
ADAM_STEP = 10

MESH = pl.DeviceIdType.MESH


def _cp(*sem):
    return pltpu.CompilerParams(dimension_semantics=sem, vmem_limit_bytes=VMEM_LIMIT)


def _sigmoid(x):
    return 0.5 * jnp.tanh(0.5 * x) + 0.5


def _dot(a, b, dims, precision=None):
    return lax.dot_general(a, b, (dims, ((), ())), preferred_element_type=F32, precision=precision)


def _nn(a, b, **kw):
    return _dot(a, b, ((1,), (0,)), **kw)


def _nt(a, b, **kw):
    return _dot(a, b, ((1,), (1,)), **kw)


def _tn(a, b, **kw):
    return _dot(a, b, ((0,), (0,)), **kw)


def _matmul(a, b, *, ta=False, tb=False, tm=None, tn=None, tk=None, out_dtype=F32, name,
            precision=None, after=None, epilogue=None, tiles=(), out_dtypes=None):
    extra = [] if after is None else [after]
    single = out_dtypes is None
    if single:
        out_dtypes = (out_dtype,)
    n_t, n_o = len(tiles), len(out_dtypes)
    if ta:
        kdim, m = a.shape
    else:
        m, kdim = a.shape
    if tb:
        n, k2 = b.shape
    else:
        k2, n = b.shape
    assert kdim == k2, (a.shape, b.shape, ta, tb)
    if tm is None:
        if ta:
            tm = next(t for t in (1024, 512, 256, 128, m) if m % t == 0)
        else:
            tm = m // 2 if (m // 2) % 16 == 0 and m > 512 else m
    if tn is None:
        wide = (1024,) if not ta and len(tiles) <= 1 else ()
        tn = next(t for t in (*wide, 512, 256, 128, n) if n % t == 0)
    if tk is None:
        tk = kdim if ta else next(t for t in (2048, 1024, 512, 256, 128, kdim) if kdim % t == 0)
    assert m % tm == 0 and n % tn == 0 and kdim % tk == 0, (m, n, kdim, tm, tn, tk)
    nk = kdim // tk
    op_dtype = F32 if precision is not None else BF16

    def body(a_ref, b_ref, *refs):
        t_refs = refs[:n_t]
        o_refs = refs[n_t + len(extra):n_t + len(extra) + n_o]
        av = a_ref[...].astype(op_dtype)
        bv = b_ref[...].astype(op_dtype)
        dims = ((0 if ta else 1,), (1 if tb else 0,))
        part = _dot(av, bv, dims, precision=precision)

        def finish(acc):
            outs = (acc,) if epilogue is None else epilogue(acc, *[t[...] for t in t_refs])
            for o_ref, val in zip(o_refs, outs):
                o_ref[...] = val.astype(o_ref.dtype)

        if nk == 1:
            finish(part)
            return
        acc_ref = refs[-1]
        kk = pl.program_id(2)

        @pl.when(kk == 0)
        def _():
            acc_ref[...] = part

        @pl.when((kk > 0) & (kk < nk - 1))
        def _():
            acc_ref[...] += part

        @pl.when(kk == nk - 1)
        def _():
            finish(acc_ref[...] + part)

    a_spec = (pl.BlockSpec((tk, tm), lambda i, j, k: (k, i)) if ta
              else pl.BlockSpec((tm, tk), lambda i, j, k: (i, k)))
    b_spec = (pl.BlockSpec((tn, tk), lambda i, j, k: (j, k)) if tb
              else pl.BlockSpec((tk, tn), lambda i, j, k: (k, j)))
    for _, off in tiles:
        assert off % tn == 0, (off, tn)
    t_specs = [pl.BlockSpec((tm, tn), functools.partial(lambda i, j, k, o: (i, o + j), o=off // tn))
               for _, off in tiles]
    o_spec = pl.BlockSpec((tm, tn), lambda i, j, k: (i, j))
    outs = pl.pallas_call(
        body,
        grid=(m // tm, n // tn, nk),
        in_specs=[a_spec, b_spec] + t_specs + [pl.BlockSpec(memory_space=pl.ANY)] * len(extra),
        out_specs=[o_spec] * n_o,
        out_shape=[jax.ShapeDtypeStruct((m, n), dt) for dt in out_dtypes],
        scratch_shapes=[pltpu.VMEM((tm, tn), F32)] if nk > 1 else [],
        compiler_params=_cp("parallel", "parallel", "arbitrary"),
        name=name,
    )(a, b, *[t for t, _ in tiles], *extra)
    return outs[0] if single else outs


def _rspec(tr, w, cb=0):
    return pl.BlockSpec((tr, w), lambda i: (i, cb))


def _fspec(shape):
    nd = len(shape)
    return pl.BlockSpec(shape, lambda i: (0,) * nd)


ROW_VMEM_BUDGET = 20 * 1024 * 1024
ROW_MIN_STEPS = 4


def _row_tile(lp, row_bytes):
    for k in range(ROW_MIN_STEPS, lp // 16 + 1):
        tr = lp // k
        if lp % k == 0 and tr % 16 == 0 and 2 * tr * row_bytes <= ROW_VMEM_BUDGET:
            return tr
    return lp


def _token_rows_copy(i, n_tiles, tr, n_tok, tok_ref, buf_ref, sem, *, to_tokens, start=True,
                     wait=True):
    assert n_tiles >= 2 and 0 < n_tok + N_META - (n_tiles - 1) * tr <= tr

    def run(tok_row, buf_row, count):
        tok = tok_ref.at[pl.ds(tok_row, count), :]
        buf = buf_ref.at[pl.ds(buf_row, count), :]
        cp = pltpu.make_async_copy(buf, tok, sem) if to_tokens else pltpu.make_async_copy(tok, buf, sem)
        if start:
            cp.start()
        if wait:
            cp.wait()

    @pl.when(i == 0)
    def _():
        run(0, N_META, tr - N_META)

    if n_tiles > 2:
        @pl.when((i > 0) & (i < n_tiles - 1))
        def _():
            run(pl.multiple_of(i * tr - N_META, 8), 0, tr)

    @pl.when(i == n_tiles - 1)
    def _():
        run((n_tiles - 1) * tr - N_META, 0, n_tok + N_META - (n_tiles - 1) * tr)


def _embed_norm(x, tgt, meta, g, *, lp, name):
    n_tok, d = x.shape
    tr = _row_tile(lp, d * (4 + 2 + 4))
    n_tiles = lp // tr

    def body(x_ref, tgt_ref, meta_ref, g_ref, h_ref, o_ref, tp_ref, buf_ref, tbuf_ref, sems):
        i = pl.program_id(0)
        buf_ref[...] = jnp.zeros_like(buf_ref)
        tbuf_ref[...] = jnp.zeros_like(tbuf_ref)

        @pl.when(i == 0)
        def _():
            buf_ref[0:N_META, :] = meta_ref[...]

        _token_rows_copy(i, n_tiles, tr, n_tok, tgt_ref, tbuf_ref, sems.at[1], to_tokens=False,
                         wait=False)
        _token_rows_copy(i, n_tiles, tr, n_tok, x_ref, buf_ref, sems.at[0], to_tokens=False)
        xv = buf_ref[...]
        h_ref[...] = xv
        r = lax.rsqrt(jnp.mean(xv * xv, axis=-1, keepdims=True) + EPS)
        o_ref[...] = (xv * r * g_ref[...]).astype(BF16)
        _token_rows_copy(i, n_tiles, tr, n_tok, tgt_ref, tbuf_ref, sems.at[1], to_tokens=False,
                         start=False)
        tp_ref[...] = tbuf_ref[...]

    return pl.pallas_call(
        body, grid=(n_tiles,),
        in_specs=[ANY, ANY, _fspec((N_META, d)), _fspec((1, d))],
        out_specs=[_rspec(tr, d), _rspec(tr, d), _rspec(tr, d)],
        out_shape=[jax.ShapeDtypeStruct((lp, d), F32), jax.ShapeDtypeStruct((lp, d), BF16),
                   jax.ShapeDtypeStruct((lp, d), F32)],
        scratch_shapes=[pltpu.VMEM((tr, d), F32), pltpu.VMEM((tr, d), F32),
                        pltpu.SemaphoreType.DMA((2,))],
        compiler_params=_cp("parallel"), name=name)(x, tgt, meta, g)


def _residual_norm(h, t, g, *, name):
    lp, d = h.shape
    tr = _row_tile(lp, d * (4 + 4 + 4 + 2))

    def body(h_ref, t_ref, g_ref, h1_ref, m_ref):
        xv = h_ref[...] + t_ref[...]
        h1_ref[...] = xv
        r = lax.rsqrt(jnp.mean(xv * xv, axis=-1, keepdims=True) + EPS)
        m_ref[...] = (xv * r * g_ref[...]).astype(BF16)

    return pl.pallas_call(
        body, grid=(lp // tr,),
        in_specs=[_rspec(tr, d), _rspec(tr, d), _fspec((1, d))],
        out_specs=[_rspec(tr, d), _rspec(tr, d)],
        out_shape=[jax.ShapeDtypeStruct((lp, d), F32), jax.ShapeDtypeStruct((lp, d), BF16)],
        compiler_params=_cp("parallel"), name=name)(h, t, g)


def _rmsnorm_bwd_add(x, g, dy, dres, *, name):
    lp, d = x.shape
    tr = _row_tile(lp, d * (4 * 4 + 2))

    def body(x_ref, g_ref, dy_ref, dr_ref, dx_ref, dx16_ref, dg_ref):
        @pl.when(pl.program_id(0) == 0)
        def _():
            dg_ref[...] = jnp.zeros_like(dg_ref)

        xv = x_ref[...]
        r = lax.rsqrt(jnp.mean(xv * xv, axis=-1, keepdims=True) + EPS)
        xh = xv * r
        dyv = dy_ref[...]
        dg_ref[...] += jnp.sum(dyv * xh, axis=0, keepdims=True)
        dxh = dyv * g_ref[...]
        dx = dr_ref[...] + r * (dxh - xh * jnp.mean(dxh * xh, axis=-1, keepdims=True))
        dx_ref[...] = dx
        dx16_ref[...] = dx.astype(BF16)

    return pl.pallas_call(
        body, grid=(lp // tr,),
        in_specs=[_rspec(tr, d), _fspec((1, d)), _rspec(tr, d), _rspec(tr, d)],
        out_specs=[_rspec(tr, d), _rspec(tr, d), _fspec((1, d))],
        out_shape=[jax.ShapeDtypeStruct((lp, d), F32), jax.ShapeDtypeStruct((lp, d), BF16),
                   jax.ShapeDtypeStruct((1, d), F32)],
        compiler_params=_cp("arbitrary"), name=name)(x, g, dy, dres)


def _rmsnorm_bwd_tokens(x, g, dy, dres, *, n_tok, name):
    lp, d = x.shape
    tr = _row_tile(lp, d * 4 * 4)
    n_tiles = lp // tr

    def body(x_ref, g_ref, dy_ref, dr_ref, dtok_ref, dmeta_ref, dg_ref, buf_ref, sem):
        i = pl.program_id(0)

        @pl.when(i == 0)
        def _():
            dg_ref[...] = jnp.zeros_like(dg_ref)

        xv = x_ref[...]
        r = lax.rsqrt(jnp.mean(xv * xv, axis=-1, keepdims=True) + EPS)
        xh = xv * r
        dyv = dy_ref[...]
        dg_ref[...] += jnp.sum(dyv * xh, axis=0, keepdims=True)
        dxh = dyv * g_ref[...]
        buf_ref[...] = dr_ref[...] + r * (dxh - xh * jnp.mean(dxh * xh, axis=-1, keepdims=True))

        @pl.when(i == 0)
        def _():
            dmeta_ref[...] = buf_ref[0:N_META, :]

        _token_rows_copy(i, n_tiles, tr, n_tok, dtok_ref, buf_ref, sem, to_tokens=True)

    return pl.pallas_call(
        body, grid=(n_tiles,),
        in_specs=[_rspec(tr, d), _fspec((1, d)), _rspec(tr, d), _rspec(tr, d)],
        out_specs=[ANY, _fspec((N_META, d)), _fspec((1, d))],
        out_shape=[jax.ShapeDtypeStruct((n_tok, d), F32), jax.ShapeDtypeStruct((N_META, d), F32),
                   jax.ShapeDtypeStruct((1, d), F32)],
        scratch_shapes=[pltpu.VMEM((tr, d), F32), pltpu.SemaphoreType.DMA],
        compiler_params=_cp("arbitrary"), name=name)(x, g, dy, dres)


def _final_loss(h1, t2, g, tgt, *, n_tok, name):
    lp, d = h1.shape
    tr = _row_tile(lp, d * (4 * 4 + 2))
    n_tiles = lp // tr

    def body(h_ref, t_ref, g_ref, tg_ref, dh_ref, dh16_ref, loss_ref, dg_ref):
        i = pl.program_id(0)

        @pl.when(i == 0)
        def _():
            loss_ref[...] = jnp.zeros_like(loss_ref)
            dg_ref[...] = jnp.zeros_like(dg_ref)

        xv = h_ref[...] + t_ref[...]
        r = lax.rsqrt(jnp.mean(xv * xv, axis=-1, keepdims=True) + EPS)
        xh = xv * r
        gv = g_ref[...]
        row = i * tr + lax.broadcasted_iota(jnp.int32, (tr, 1), 0)
        valid = (row >= N_META) & (row < N_META + n_tok)
        err = jnp.where(valid, xh * gv - tg_ref[...], 0.0)
        loss_ref[...] += jnp.sum(0.5 * err * err) / d
        dy = err / d
        dg_ref[...] += jnp.sum(dy * xh, axis=0, keepdims=True)
        dxh = dy * gv
        dh = r * (dxh - xh * jnp.mean(dxh * xh, axis=-1, keepdims=True))
        dh_ref[...] = dh
        dh16_ref[...] = dh.astype(BF16)

    return pl.pallas_call(
        body, grid=(n_tiles,),
        in_specs=[_rspec(tr, d), _rspec(tr, d), _fspec((1, d)), _rspec(tr, d)],
        out_specs=[_rspec(tr, d), _rspec(tr, d), _fspec((1, LANES)), _fspec((1, d))],
        out_shape=[jax.ShapeDtypeStruct((lp, d), F32), jax.ShapeDtypeStruct((lp, d), BF16),
                   jax.ShapeDtypeStruct((1, LANES), F32), jax.ShapeDtypeStruct((1, d), F32)],
        compiler_params=_cp("arbitrary"), name=name)(h1, t2, g, tgt)


def _hg_out(o_f, o_b, proj, gain, *, col_g, name):
    lp, w = o_f.shape
    tr = _row_tile(lp, w * (3 * 4 + 2))
    hh = w // HG_DK

    def body(of_ref, ob_ref, g_ref, gain_ref, y_ref):
        gv = g_ref[...]
        sg = gv * _sigmoid(gv)
        for h in range(hh):
            sl = slice(h * HG_DK, (h + 1) * HG_DK)
            o = of_ref[:, sl] + ob_ref[:, sl]
            r = lax.rsqrt(jnp.mean(o * o, axis=-1, keepdims=True) + EPS)
            y_ref[:, sl] = (o * r * gain_ref[:, sl] * sg[:, sl]).astype(BF16)

    return pl.pallas_call(
        body, grid=(lp // tr,),
        in_specs=[_rspec(tr, w), _rspec(tr, w), _rspec(tr, w, col_g // w), _fspec((1, w))],
        out_specs=_rspec(tr, w),
        out_shape=jax.ShapeDtypeStruct((lp, w), BF16),
        compiler_params=_cp("parallel"), name=name)(o_f, o_b, proj, gain)


def _hg_out_bwd(o_f, o_b, proj, gain, dy, *, col_g, name):
    lp, w = o_f.shape
    tr = _row_tile(lp, w * (5 * 4 + 2))
    hh = w // HG_DK

    def body(of_ref, ob_ref, g_ref, gain_ref, dy_ref, do_ref, dg_ref, dgain_ref):
        @pl.when(pl.program_id(0) == 0)
        def _():
            dgain_ref[...] = jnp.zeros_like(dgain_ref)

        for h in range(hh):
            sl = slice(h * HG_DK, (h + 1) * HG_DK)
            gv = g_ref[:, sl]
            s = _sigmoid(gv)
            sg = gv * s
            dsg = s + gv * s * (1.0 - s)
            o = of_ref[:, sl] + ob_ref[:, sl]
            r = lax.rsqrt(jnp.mean(o * o, axis=-1, keepdims=True) + EPS)
            on = o * r
            dyv = dy_ref[:, sl]
            gn = gain_ref[:, sl]
            dgain_ref[:, sl] += jnp.sum(dyv * on * sg, axis=0, keepdims=True)
            dg_ref[:, sl] = (dyv * on * gn * dsg).astype(BF16)
            don = dyv * gn * sg
            do_ref[:, sl] = r * (don - on * jnp.mean(don * on, axis=-1, keepdims=True))

    return pl.pallas_call(
        body, grid=(lp // tr,),
        in_specs=[_rspec(tr, w), _rspec(tr, w), _rspec(tr, w, col_g // w), _fspec((1, w)),
                  _rspec(tr, w)],
        out_specs=[_rspec(tr, w), _rspec(tr, w), _fspec((1, w))],
        out_shape=[jax.ShapeDtypeStruct((lp, w), F32), jax.ShapeDtypeStruct((lp, w), BF16),
                   jax.ShapeDtypeStruct((1, w), F32)],
        compiler_params=_cp("arbitrary"), name=name)(o_f, o_b, proj, gain, dy)


HG_ROWS = 128
HG_HALVES = (1, 2, 4, 8, 16, 32, 64)


def _hg_gates(zq, z, lbv):
    sq = _sigmoid(zq)
    s = _sigmoid(z)
    f = lbv + (1.0 - lbv) * s
    kk = (1.0 - lbv) * (1.0 - s)
    return zq * sq, sq, s, f, jnp.log(f), kk


def _block_cumsum(g, pos, suffix):
    x = g
    for k in HG_HALVES:
        if suffix:
            x = x + jnp.where(pos < HG_ROWS - k, pltpu.roll(x, HG_ROWS - k, 0), 0.0)
        else:
            x = x + jnp.where(pos >= k, pltpu.roll(x, k, 0), 0.0)
    return x


def _pair_levels(b, pos, reverse):
    out = []
    first = b
    for m in HG_HALVES:
        if m > 1:
            first = jnp.where((pos & (m - 1)) >= m // 2, pltpu.roll(first, m // 2, 0), first)
        nxt = pltpu.roll(first, HG_ROWS - m, 0)
        upper = (pos & (2 * m - 1)) >= m
        if reverse:
            eq = jnp.where(upper, 0.0, jnp.exp(b - nxt))
            ek = jnp.where(upper, jnp.exp(first - b), 0.0)
        else:
            eq = jnp.where(upper, jnp.exp(b - first), 0.0)
            ek = jnp.where(upper, 0.0, jnp.exp(nxt - b))
        out.append((eq, ek))
    return out


def _pair_masks(mask_ref):
    ri = lax.broadcasted_iota(jnp.int32, (HG_ROWS, HG_ROWS), 0)
    ci = lax.broadcasted_iota(jnp.int32, (HG_ROWS, HG_ROWS), 1)
    for i, m in enumerate(HG_HALVES):
        sh = m.bit_length()
        mask_ref[i] = jnp.where((ri >> sh) == (ci >> sh), 1.0, 0.0)


def _hg_scan_fwd(proj, lb, *, reverse, col_q, col_z, col_i, hh, name):
    lp = proj.shape[0]
    n_blocks = lp // HG_ROWS
    last = 0 if reverse else HG_ROWS - 1

    def body(q_ref, z_ref, i_ref, lb_ref, o_ref, st_ref, mask_ref):
        lbv = lb_ref[...]
        pos = lax.broadcasted_iota(jnp.int32, (HG_ROWS, 1), 0)
        ri = lax.broadcasted_iota(jnp.int32, (HG_ROWS, HG_ROWS), 0)
        ci = lax.broadcasted_iota(jnp.int32, (HG_ROWS, HG_ROWS), 1)
        _pair_masks(mask_ref)

        def block(bi, st):
            bb = (n_blocks - 1 - bi) if reverse else bi
            r0 = pl.multiple_of(bb * HG_ROWS, HG_ROWS)
            v16 = i_ref[pl.ds(r0, HG_ROWS), :].astype(BF16)
            qh, _, _, _, g, kk = _hg_gates(q_ref[pl.ds(r0, HG_ROWS), :],
                                           z_ref[pl.ds(r0, HG_ROWS), :], lbv)
            b = _block_cumsum(g, pos, reverse)
            bl = b[last:last + 1, :]
            qe = (qh * jnp.exp(b)).astype(BF16)
            kd = (kk * jnp.exp(bl - b)).astype(BF16)
            a = jnp.where(ri == ci, jnp.sum(qh * kk, axis=1, keepdims=True), 0.0)
            for i, (eq, ek) in enumerate(_pair_levels(b, pos, reverse)):
                a = a + mask_ref[i] * _nt((qh * eq).astype(BF16), (kk * ek).astype(BF16))
            st_ref[bb] = st
            o_ref[pl.ds(r0, HG_ROWS), :] = _nn(a.astype(BF16), v16) + _nt(qe, st.astype(BF16))
            return jnp.exp(bl) * st + _tn(v16, kd)

        lax.fori_loop(0, n_blocks, block, jnp.zeros((HG_DK, HG_DK), F32))

    cspec = lambda col: pl.BlockSpec((lp, HG_DK), lambda h: (0, col // HG_DK + h))
    return pl.pallas_call(
        body, grid=(hh,),
        in_specs=[cspec(col_q), cspec(col_z), cspec(col_i),
                  pl.BlockSpec((None, 1, HG_DK), lambda h: (h, 0, 0))],
        out_specs=[pl.BlockSpec((lp, HG_DK), lambda h: (0, h)),
                   pl.BlockSpec((None, n_blocks, HG_DK, HG_DK), lambda h: (h, 0, 0, 0))],
        out_shape=[jax.ShapeDtypeStruct((lp, hh * HG_DK), F32),
                   jax.ShapeDtypeStruct((hh, n_blocks, HG_DK, HG_DK), F32)],
        scratch_shapes=[pltpu.VMEM((len(HG_HALVES), HG_ROWS, HG_ROWS), F32)],
        compiler_params=_cp("parallel"), name=name)(proj, proj, proj, lb)


def _hg_scan_bwd(proj, lb, states, do, *, reverse, col_q, col_z, col_i, hh, name):
    lp = proj.shape[0]
    n_blocks = lp // HG_ROWS
    last = 0 if reverse else HG_ROWS - 1

    def body(q_ref, z_ref, i_ref, lb_ref, st_ref, do_ref, dq_ref, dz_ref, dv_ref, dlb_ref, mask_ref):
        lbv = lb_ref[...]
        pos = lax.broadcasted_iota(jnp.int32, (HG_ROWS, 1), 0)
        ri = lax.broadcasted_iota(jnp.int32, (HG_ROWS, HG_ROWS), 0)
        ci = lax.broadcasted_iota(jnp.int32, (HG_ROWS, HG_ROWS), 1)
        _pair_masks(mask_ref)

        def block(bi, carry):
            dst, dlb = carry
            bb = bi if reverse else (n_blocks - 1 - bi)
            r0 = pl.multiple_of(bb * HG_ROWS, HG_ROWS)
            zq = q_ref[pl.ds(r0, HG_ROWS), :]
            v16 = i_ref[pl.ds(r0, HG_ROWS), :].astype(BF16)
            do16 = do_ref[pl.ds(r0, HG_ROWS), :].astype(BF16)
            qh, sq, s, f, g, kk = _hg_gates(zq, z_ref[pl.ds(r0, HG_ROWS), :], lbv)
            b = _block_cumsum(g, pos, reverse)
            bl = b[last:last + 1, :]
            eb = jnp.exp(b)
            ebl = jnp.exp(bl - b)
            decay = jnp.exp(bl)
            qe16 = (qh * eb).astype(BF16)
            kd16 = (kk * ebl).astype(BF16)
            st = st_ref[bb]
            st16, dst16 = st.astype(BF16), dst.astype(BF16)
            same_row = ri == ci
            da = _nt(do16, v16)
            da_diag = jnp.sum(jnp.where(same_row, da, 0.0), axis=1, keepdims=True)
            dq_state = eb * _nn(do16, st16)
            dk_state = ebl * _nn(v16, dst16)
            dq = dq_state + da_diag * kk
            dk = dk_state + da_diag * qh
            dbl = (decay * jnp.sum(st * dst, axis=0, keepdims=True)
                   + jnp.sum(kk * dk_state, axis=0, keepdims=True))
            db = qh * dq_state - kk * dk_state + jnp.where(pos == last, dbl, 0.0)
            a = jnp.where(same_row, jnp.sum(qh * kk, axis=1, keepdims=True), 0.0)
            for i, (eq, ek) in enumerate(_pair_levels(b, pos, reverse)):
                same = mask_ref[i]
                q16, k16 = (qh * eq).astype(BF16), (kk * ek).astype(BF16)
                a = a + same * _nt(q16, k16)
                da16 = (same * da).astype(BF16)
                gq, gk = _nn(da16, k16), _tn(da16, q16)
                dq = dq + eq * gq
                dk = dk + ek * gk
                db = db + (q16.astype(F32) * gq - k16.astype(F32) * gk)
            dg = _block_cumsum(db, pos, not reverse)
            df = dg / f - dk
            dq_ref[pl.ds(r0, HG_ROWS), :] = dq * (sq + zq * sq * (1.0 - sq))
            dz_ref[pl.ds(r0, HG_ROWS), :] = (df * (1.0 - lbv) * s * (1.0 - s)).astype(BF16)
            dv_ref[pl.ds(r0, HG_ROWS), :] = _nt(kd16, dst16) + _tn(a.astype(BF16), do16)
            return (decay * dst + _tn(do16, qe16),
                    dlb + jnp.sum(df * (1.0 - s), axis=0, keepdims=True))

        _, dlb = lax.fori_loop(0, n_blocks, block,
                               (jnp.zeros((HG_DK, HG_DK), F32), jnp.zeros((1, HG_DK), F32)))
        dlb_ref[...] = dlb

    cspec = lambda col: pl.BlockSpec((lp, HG_DK), lambda h: (0, col // HG_DK + h))
    ospec = pl.BlockSpec((lp, HG_DK), lambda h: (0, h))
    sds = jax.ShapeDtypeStruct((lp, hh * HG_DK), F32)
    return pl.pallas_call(
        body, grid=(hh,),
        in_specs=[cspec(col_q), cspec(col_z), cspec(col_i),
                  pl.BlockSpec((None, 1, HG_DK), lambda h: (h, 0, 0)),
                  pl.BlockSpec((None, n_blocks, HG_DK, HG_DK), lambda h: (h, 0, 0, 0)),
                  ospec],
        out_specs=[ospec, ospec, ospec, pl.BlockSpec((None, 1, HG_DK), lambda h: (h, 0, 0))],
        out_shape=[sds, jax.ShapeDtypeStruct((lp, hh * HG_DK), BF16), sds,
                   jax.ShapeDtypeStruct((hh, 1, HG_DK), F32)],
        scratch_shapes=[pltpu.VMEM((len(HG_HALVES), HG_ROWS, HG_ROWS), F32)],
        compiler_params=_cp("parallel"), name=name)(proj, proj, proj, lb, states, do)


NA_HB = LANES // NA_HEAD_DIM
NA_G = 4
NA_U = NA_G + NA_WIN_H
NA_QN = NA_G * GRID_W
NA_KN = NA_U * GRID_W


def _na_table_index(pattern, a, j):
    if pattern == 0:
        return j - a + NA_WIN_H - 1 if j < NA_WIN_H else None
    if pattern == 2:
        return j - a - 1 if j >= NA_U - NA_WIN_H else None
    return j - a + NA_WIN_H // 2 - 1 if a <= j < a + NA_WIN_H else None


def _na_step_rows(pattern, t, rows):
    if pattern == 0:
        r0, us = 0, 0
    elif pattern == 2:
        r0, us = rows - NA_G, rows - NA_U
    else:
        r0 = NA_G * t
        us = r0 - NA_WIN_H // 2
    q0, k0 = N_META + GRID_W * r0, N_META + GRID_W * us
    if pattern == 1:
        q0, k0 = pl.multiple_of(q0, 16), pl.multiple_of(k0, 16)
    return q0, k0


def _na_fill_bias(tb_ref, bias_ref):
    neg = jnp.full((GRID_W, GRID_W), -1e30, F32)
    for h in range(NA_HB):
        for pattern in range(3):
            for a in range(NA_G):
                for j in range(NA_U):
                    idx = _na_table_index(pattern, a, j)
                    bias_ref[h, pattern, a * GRID_W:(a + 1) * GRID_W, j * GRID_W:(j + 1) * GRID_W] = (
                        neg if idx is None else tb_ref[h, idx])


def _na_steps(rows, step, carry):
    n_steps = rows // NA_G
    carry = step(0, 0, carry)
    carry = lax.fori_loop(1, n_steps - 1, functools.partial(step, 1), carry, unroll=2)
    return step(2, n_steps - 1, carry)


def _na_head_lanes():
    lane = lax.broadcasted_iota(jnp.int32, (1, LANES), 1)
    return [lane // NA_HEAD_DIM == h for h in range(NA_HB)]


def _na_only(mask, x):
    return jnp.where(mask, x, jnp.zeros_like(x))


def _na_stack(heads, x):
    return jnp.concatenate([_na_only(mask, x) for mask in heads], axis=0)


def _na_unstack(heads, y):
    rows = y.shape[0] // NA_HB
    out = y[0:rows]
    for h in range(1, NA_HB):
        out = jnp.where(heads[h], y[h * rows:(h + 1) * rows], out)
    return out


def _na_fwd(proj, tb, *, n_tok, nh, name):
    lp = proj.shape[0]
    dh, hb = NA_HEAD_DIM, NA_HB
    naw = nh * dh
    rows = n_tok // GRID_W
    scale = dh ** -0.5

    def body(q_ref, k_ref, v_ref, tb_ref, o_ref, lse_ref, q16_ref, k16_ref, v16_ref, bias_ref):
        o_ref[...] = jnp.zeros_like(o_ref)
        lse_ref[...] = jnp.zeros_like(lse_ref)
        q16_ref[...] = q_ref[...].astype(BF16)
        k16_ref[...] = k_ref[...].astype(BF16)
        v16_ref[...] = v_ref[...].astype(BF16)
        _na_fill_bias(tb_ref, bias_ref)
        heads = _na_head_lanes()
        km = k16_ref[0:N_META, :]
        vm = v16_ref[0:N_META, :]
        qm = q16_ref[0:N_META, :]
        o_m = None
        for h in range(hb):
            s = _nt(_na_only(heads[h], qm), km) * scale
            m = jnp.max(s, axis=1, keepdims=True)
            p = jnp.exp(s - m)
            l = jnp.sum(p, axis=1, keepdims=True)
            o_h = _nn(p.astype(BF16), vm) / l
            o_m = o_h if o_m is None else jnp.where(heads[h], o_h, o_m)
            lse_ref[h, 0:N_META, :] = m + jnp.log(l)
        o_ref[0:N_META, :] = o_m

        def step(pattern, t, carry):
            q0, k0 = _na_step_rows(pattern, t, rows)
            k16 = k16_ref[pl.ds(k0, NA_KN), :]
            v16 = v16_ref[pl.ds(k0, NA_KN), :]
            q2 = _na_stack(heads, q16_ref[pl.ds(q0, NA_QN), :])
            s = _nt(q2, k16) * scale + bias_ref[:, pattern].reshape(hb * NA_QN, NA_KN)
            sm = _nt(q2, km) * scale
            m = jnp.maximum(jnp.max(s, axis=1, keepdims=True), jnp.max(sm, axis=1, keepdims=True))
            p = jnp.exp(s - m)
            pm = jnp.exp(sm - m)
            l = jnp.sum(p, axis=1, keepdims=True) + jnp.sum(pm, axis=1, keepdims=True)
            o2 = (_nn(p.astype(BF16), v16) + _nn(pm.astype(BF16), vm)) / l
            o_ref[pl.ds(q0, NA_QN), :] = _na_unstack(heads, o2)
            lse2 = m + jnp.log(l)
            for h in range(hb):
                lse_ref[h, pl.ds(q0, NA_QN), :] = lse2[h * NA_QN:(h + 1) * NA_QN]
            return carry

        _na_steps(rows, step, 0)

    cblk = lambda col: pl.BlockSpec((lp, LANES), lambda g: (0, col // LANES + g))
    return pl.pallas_call(
        body, grid=(nh // hb,),
        in_specs=[cblk(0), cblk(naw), cblk(2 * naw),
                  pl.BlockSpec((hb, 2 * NA_WIN_H - 1, GRID_W, GRID_W), lambda g: (g, 0, 0, 0))],
        out_specs=[cblk(0), pl.BlockSpec((hb, lp, 1), lambda g: (g, 0, 0))],
        out_shape=[jax.ShapeDtypeStruct((lp, naw), F32), jax.ShapeDtypeStruct((nh, lp, 1), F32)],
        scratch_shapes=[pltpu.VMEM((lp, LANES), BF16)] * 3 + [pltpu.VMEM((hb, 3, NA_QN, NA_KN), F32)],
        compiler_params=_cp("parallel"), name=name)(proj, proj, proj, tb)


def _na_bwd(proj, tb, o, lse, do, *, n_tok, nh, name):
    lp = proj.shape[0]
    dh, hb = NA_HEAD_DIM, NA_HB
    naw = nh * dh
    rows = n_tok // GRID_W
    scale = dh ** -0.5

    def body(q_ref, k_ref, v_ref, tb_ref, o_ref, lse_ref, do_ref, dq_ref, dk_ref, dv_ref, dtb_ref,
             q16_ref, k16_ref, v16_ref, bias_ref):
        dq_ref[...] = jnp.zeros_like(dq_ref)
        dk_ref[...] = jnp.zeros_like(dk_ref)
        dv_ref[...] = jnp.zeros_like(dv_ref)
        dtb_ref[...] = jnp.zeros_like(dtb_ref)
        q16_ref[...] = q_ref[...].astype(BF16)
        k16_ref[...] = k_ref[...].astype(BF16)
        v16_ref[...] = v_ref[...].astype(BF16)
        _na_fill_bias(tb_ref, bias_ref)
        heads = _na_head_lanes()
        km = k16_ref[0:N_META, :]
        vm = v16_ref[0:N_META, :]
        qm = q16_ref[0:N_META, :]
        dom = do_ref[0:N_META, :]
        prod = dom * o_ref[0:N_META, :]
        dq_m = None
        dkm0 = jnp.zeros((N_META, LANES), F32)
        dvm0 = jnp.zeros((N_META, LANES), F32)
        for h in range(hb):
            q_h = _na_only(heads[h], qm)
            do_h = _na_only(heads[h], dom).astype(BF16)
            p = jnp.exp(_nt(q_h, km) * scale - lse_ref[h, 0:N_META, :])
            delta = jnp.sum(_na_only(heads[h], prod), axis=1, keepdims=True)
            ds = (p * (_nt(do_h, vm) - delta)).astype(BF16)
            dq_h = _nn(ds, km) * scale
            dq_m = dq_h if dq_m is None else jnp.where(heads[h], dq_h, dq_m)
            dkm0 = dkm0 + _tn(ds, q_h) * scale
            dvm0 = dvm0 + _tn(p.astype(BF16), do_h)
        dq_ref[0:N_META, :] = dq_m

        def step(pattern, t, carry):
            dkm, dvm = carry
            q0, k0 = _na_step_rows(pattern, t, rows)
            k16 = k16_ref[pl.ds(k0, NA_KN), :]
            v16 = v16_ref[pl.ds(k0, NA_KN), :]
            q2 = _na_stack(heads, q16_ref[pl.ds(q0, NA_QN), :])
            do2 = _na_stack(heads, do_ref[pl.ds(q0, NA_QN), :])
            do16 = do2.astype(BF16)
            ov = o_ref[pl.ds(q0, NA_QN), :]
            delta = jnp.sum(do2 * jnp.concatenate([ov] * hb, axis=0), axis=1, keepdims=True)
            lse = jnp.concatenate([lse_ref[h, pl.ds(q0, NA_QN), :] for h in range(hb)], axis=0)
            p = jnp.exp(_nt(q2, k16) * scale + bias_ref[:, pattern].reshape(hb * NA_QN, NA_KN)
                        - lse)
            pm = jnp.exp(_nt(q2, km) * scale - lse)
            ds = p * (_nt(do16, v16) - delta)
            dsm = (pm * (_nt(do16, vm) - delta)).astype(BF16)
            ds16 = ds.astype(BF16)
            dq2 = (_nn(ds16, k16) + _nn(dsm, km)) * scale
            dq_ref[pl.ds(q0, NA_QN), :] = _na_unstack(heads, dq2)
            dk_ref[pl.ds(k0, NA_KN), :] += _tn(ds16, q2) * scale
            dv_ref[pl.ds(k0, NA_KN), :] += _tn(p.astype(BF16), do16)
            for h in range(hb):
                for a in range(NA_G):
                    for j in range(NA_U):
                        idx = _na_table_index(pattern, a, j)
                        if idx is not None:
                            r = h * NA_QN + a * GRID_W
                            dtb_ref[h, idx] += ds[r:r + GRID_W, j * GRID_W:(j + 1) * GRID_W]
            return dkm + _tn(dsm, q2) * scale, dvm + _tn(pm.astype(BF16), do16)

        dkm, dvm = _na_steps(rows, step, (dkm0, dvm0))
        dk_ref[0:N_META, :] += dkm
        dv_ref[0:N_META, :] += dvm

    cblk = lambda col: pl.BlockSpec((lp, LANES), lambda g: (0, col // LANES + g))
    tbs = pl.BlockSpec((hb, 2 * NA_WIN_H - 1, GRID_W, GRID_W), lambda g: (g, 0, 0, 0))
    sds = jax.ShapeDtypeStruct((lp, naw), F32)
    return pl.pallas_call(
        body, grid=(nh // hb,),
        in_specs=[cblk(0), cblk(naw), cblk(2 * naw), tbs, cblk(0),
                  pl.BlockSpec((hb, lp, 1), lambda g: (g, 0, 0)), cblk(0)],
        out_specs=[cblk(0), cblk(0), cblk(0), tbs],
        out_shape=[sds, sds, sds, jax.ShapeDtypeStruct(tb.shape, F32)],
        scratch_shapes=[pltpu.VMEM((lp, LANES), BF16)] * 3 + [pltpu.VMEM((hb, 3, NA_QN, NA_KN), F32)],
        compiler_params=_cp("parallel"), name=name)(proj, proj, proj, tb, o, lse, do)


def _rpb_onehot():
    c = np.arange(GRID_W)[:, None]
    w = np.arange(GRID_W)[None, :]
    cs = np.clip(c - NA_WIN_W // 2, 0, GRID_W - NA_WIN_W)
    in_win = (w >= cs) & (w < cs + NA_WIN_W)
    dc = np.clip(w - c, -(NA_WIN_W - 1), NA_WIN_W - 1) + NA_WIN_W - 1
    oh = np.zeros((LANES, GRID_W * GRID_W), np.float32)
    flat = np.arange(GRID_W * GRID_W).reshape(GRID_W, GRID_W)
    oh[dc[in_win], flat[in_win]] = 1.0
    neg = np.where(in_win, 0.0, -1e30).astype(np.float32).reshape(1, -1)
    return oh, neg


def _assemble_dproj(dq_na, dk_na, dv_na, dq_f, dq_b, dz_f, dz_b, dv_f, dv_b, dg, dgn, dgh, *, name):
    lp, naw = dq_na.shape
    hgw = dq_f.shape[1]
    d = dgn.shape[1]
    cols = 3 * naw + 5 * hgw + 2 * d
    tr = _row_tile(lp, 3 * naw * 4 + 4 * hgw * 4 + 3 * hgw * 2 + 2 * d * 2 + cols * 2)

    def body(nq_ref, nk_ref, nv_ref, qf_ref, qb_ref, zf_ref, zb_ref, vf_ref, vb_ref, g_ref, gn_ref,
             gh_ref, o_ref):
        o_ref[:, 0:naw] = nq_ref[...].astype(BF16)
        o_ref[:, naw:2 * naw] = nk_ref[...].astype(BF16)
        o_ref[:, 2 * naw:3 * naw] = nv_ref[...].astype(BF16)
        c0 = 3 * naw
        o_ref[:, c0:c0 + hgw] = (qf_ref[...] + qb_ref[...]).astype(BF16)
        o_ref[:, c0 + hgw:c0 + 2 * hgw] = zf_ref[...]
        o_ref[:, c0 + 2 * hgw:c0 + 3 * hgw] = zb_ref[...]
        o_ref[:, c0 + 3 * hgw:c0 + 4 * hgw] = (vf_ref[...] + vb_ref[...]).astype(BF16)
        o_ref[:, c0 + 4 * hgw:c0 + 5 * hgw] = g_ref[...]
        o_ref[:, c0 + 5 * hgw:c0 + 5 * hgw + d] = gn_ref[...]
        o_ref[:, c0 + 5 * hgw + d:] = gh_ref[...]

    hg, na = _rspec(tr, hgw), _rspec(tr, naw)
    return pl.pallas_call(
        body, grid=(lp // tr,),
        in_specs=[na, na, na, hg, hg, hg, hg, hg, hg, hg, _rspec(tr, d), _rspec(tr, d)],
        out_specs=_rspec(tr, cols),
        out_shape=jax.ShapeDtypeStruct((lp, cols), BF16),
        compiler_params=_cp("parallel"), name=name)(dq_na, dk_na, dv_na, dq_f, dq_b, dz_f, dz_b,
                                                    dv_f, dv_b, dg, dgn, dgh)


GROUP_STEPS = 8


def _group_tiles(rows, align):
    steps = GROUP_STEPS if all(r % (GROUP_STEPS * align) == 0 for r in rows) else 1
    return steps, [r // steps for r in rows]


def _adamw(ws, gs, ms, vs, *, name):
    n = len(ws)
    steps, trs = _group_tiles([w.shape[0] for w in ws], 8)

    def body(*refs):
        for i in range(n):
            w_ref, g_ref, m_ref, v_ref = refs[4 * i:4 * i + 4]
            d_ref, mo_ref, vo_ref, go_ref = refs[4 * n + 4 * i:4 * n + 4 * i + 4]
            gv = g_ref[...]
            go_ref[...] = gv
            mn = ADAM_B1 * m_ref[...] + (1.0 - ADAM_B1) * gv
            vn = ADAM_B2 * v_ref[...] + (1.0 - ADAM_B2) * (gv * gv)
            m_hat = mn / (1.0 - ADAM_B1 ** ADAM_STEP)
            v_hat = vn / (1.0 - ADAM_B2 ** ADAM_STEP)
            d_ref[...] = -ADAM_LR * (m_hat / (jnp.sqrt(v_hat) + ADAM_EPS) + ADAM_WD * w_ref[...])
            mo_ref[...] = mn
            vo_ref[...] = vn

    specs = [_rspec(tr, w.shape[1]) for tr, w in zip(trs, ws)]
    out = pl.pallas_call(
        body, grid=(steps,),
        in_specs=[s for s in specs for _ in range(4)],
        out_specs=[s for s in specs for _ in range(4)],
        out_shape=[jax.ShapeDtypeStruct(w.shape, F32) for w in ws for _ in range(4)],
        compiler_params=_cp("parallel"), name=name)(*[a for q in zip(ws, gs, ms, vs) for a in q])
    return [tuple(out[4 * i:4 * i + 4]) for i in range(n)]


def _local_step(x, tgt, meta, first_weight, rest_weights, g_mix, g_mlp, g_fin, hg_gain, rpb, lb,
                early_grads=None, mid_grads=None, late_grad=None, rest_landed=None,
                last_grads=None):
    n_tok, d = x.shape
    hgw = hg_gain.shape[1]
    nh, hh = rpb.shape[0], hgw // HG_DK
    naw = nh * NA_HEAD_DIM
    l_real = N_META + n_tok
    lp = -(-l_real // ROW_ALIGN) * ROW_ALIGN
    col_qhg = 3 * naw
    col_zf, col_zb, col_i, col_g = (col_qhg + hgw, col_qhg + 2 * hgw, col_qhg + 3 * hgw,
                                    col_qhg + 4 * hgw)
    col_gate = col_qhg + 5 * hgw

    oh_np, neg_np = _rpb_onehot()
    oh = jnp.asarray(oh_np)
    rpb_p = jnp.pad(rpb.reshape(nh * (2 * NA_WIN_H - 1), 2 * NA_WIN_W - 1),
                    ((0, 0), (0, LANES - (2 * NA_WIN_W - 1))))
    tb = _matmul(rpb_p, oh, tm=rpb_p.shape[0], tn=512, tk=LANES, precision=HIGHEST,
                 name="rpb_expand")
    tb = (tb + jnp.asarray(neg_np)).reshape(nh, 2 * NA_WIN_H - 1, GRID_W, GRID_W)

    h0, a, tgt_p = _embed_norm(x, tgt, meta, g_mix, lp=lp, name="norm_mix")
    w_in = first_weight((a, tb))
    proj = _matmul(a, w_in, tm=lp, name="mm_in")
    o_na, lse = _na_fwd(proj, tb, n_tok=n_tok, nh=nh, name="na_fwd")
    lb_f = lb[0].reshape(hh, 1, HG_DK)
    lb_b = lb[1].reshape(hh, 1, HG_DK)
    scan_kw = dict(col_q=col_qhg, col_i=col_i, hh=hh)
    o_f, st_f = _hg_scan_fwd(proj, lb_f, reverse=False, col_z=col_zf, name="hg_scan_f", **scan_kw)
    token = rest_landed(o_f) if rest_landed else None
    lb_b_late = lb_b if token is None else lb_b + token[0:1, 0:1]
    o_b, st_b = _hg_scan_fwd(proj, lb_b_late, reverse=True, col_z=col_zb, name="hg_scan_b",
                             **scan_kw)
    o_hg = _hg_out(o_f, o_b, proj, hg_gain, col_g=col_g, name="hg_out")
    w_na, w_hg, w_o, w_up, w_down = rest_weights(o_hg)
    y_na = _matmul(o_na, w_na, name="mm_na_out", out_dtype=BF16)
    gates = ((proj, col_gate), (proj, col_gate + d))

    def mix_gates(acc, gn, gh, yn):
        return acc, _sigmoid(gn) * yn + _sigmoid(gh) * acc

    def mix_gates_bwd(dmix, gn, gh, yn, yh):
        sn, sh = _sigmoid(gn), _sigmoid(gh)
        return dmix * sn, dmix * sh, dmix * yn * sn * (1.0 - sn), dmix * yh * sh * (1.0 - sh)

    y_hg, mix = _matmul(o_hg, w_hg, name="mm_hg_out", epilogue=mix_gates,
                        tiles=(*gates, (y_na, 0)), out_dtypes=(BF16, BF16))
    t1 = _matmul(mix, w_o, name="mm_o")
    h1, mlp_in = _residual_norm(h0, t1, g_mlp, name="resid_norm_mlp")
    u, act = _matmul(mlp_in, w_up, name="mm_up", out_dtypes=(BF16, BF16),
                     epilogue=lambda acc: (acc, jnp.square(jnp.maximum(acc, 0.0))))
    t2 = _matmul(act, w_down, name="mm_down")
    dh2, dh2_16, loss, dg_fin = _final_loss(h1, t2, g_fin, tgt_p, n_tok=n_tok, name="final_loss")

    (du,) = _matmul(dh2_16, w_down, tb=True, name="mm_down_dx", tiles=((u, 0),),
                    out_dtypes=(BF16,),
                    epilogue=lambda acc, uv: (acc * 2.0 * jnp.maximum(uv, 0.0),))
    dw_down = _matmul(act, dh2_16, ta=True, name="mm_down_dw")
    dm = _matmul(du, w_up, tb=True, name="mm_up_dx")
    dw_up = _matmul(mlp_in, du, ta=True, name="mm_up_dw")
    dh1, dh1_16, dg_mlp = _rmsnorm_bwd_add(h1, g_mlp, dm, dh2, name="norm_mlp_bwd")
    dy_na, dy_hg, dgn, dgh = _matmul(dh1_16, w_o, tb=True, name="mm_o_dx", epilogue=mix_gates_bwd,
                                     tiles=(*gates, (y_na, 0), (y_hg, 0)), out_dtypes=(BF16,) * 4)
    dw_o = _matmul(mix, dh1_16, ta=True, name="mm_o_dw")
    do_na = _matmul(dy_na, w_na, tb=True, name="mm_na_out_dx")
    dw_na = _matmul(o_na, dy_na, ta=True, name="mm_na_out_dw")
    do_hg = _matmul(dy_hg, w_hg, tb=True, name="mm_hg_out_dx")
    dw_hg = _matmul(o_hg, dy_hg, ta=True, name="mm_hg_out_dw")
    token = early_grads([dw_na, dw_hg, dw_o, dw_up, dw_down]) if early_grads else None
    if token is not None:
        hg_gain = hg_gain + token[0:1, 0:1]
    d_o, dg_hg, d_gain = _hg_out_bwd(o_f, o_b, proj, hg_gain, do_hg, col_g=col_g, name="hg_out_bwd")
    dq_f, dz_f, dv_f, dlb_f = _hg_scan_bwd(proj, lb_f, st_f, d_o, reverse=False, col_z=col_zf,
                                           name="hg_scan_f_bwd", **scan_kw)
    token = mid_grads(dq_f) if mid_grads else None
    lb_b_late = lb_b if token is None else lb_b + token[0:1, 0:1]
    dq_b, dz_b, dv_b, dlb_b = _hg_scan_bwd(proj, lb_b_late, st_b, d_o, reverse=True, col_z=col_zb,
                                           name="hg_scan_b_bwd", **scan_kw)
    dq_na, dk_na, dv_na, dtb = _na_bwd(proj, tb, o_na, lse, do_na, n_tok=n_tok, nh=nh, name="na_bwd")
    dproj = _assemble_dproj(dq_na, dk_na, dv_na, dq_f, dq_b, dz_f, dz_b, dv_f, dv_b, dg_hg, dgn,
                            dgh, name="assemble_dproj")
    dw_in = _matmul(a, dproj, ta=True, name="mm_in_dw")
    token = late_grad(dw_in) if late_grad else None
    da = _matmul(dproj, w_in, tb=True, name="mm_in_dx", after=token)
    token = last_grads(da) if last_grads else None
    g_mix_late = g_mix if token is None else g_mix + token[0:1, 0:1]
    dx, dmeta, dg_mix = _rmsnorm_bwd_tokens(h0, g_mix_late, da, dh1, n_tok=n_tok,
                                            name="norm_mix_bwd")
    d_rpb = _matmul(dtb.reshape(nh * (2 * NA_WIN_H - 1), GRID_W * GRID_W), oh, tb=True,
                    tm=nh * (2 * NA_WIN_H - 1), tn=LANES, tk=1024, precision=HIGHEST,
                    name="rpb_reduce")
    d_lb = jnp.concatenate([dlb_f.reshape(1, hgw), dlb_b.reshape(1, hgw)], axis=0)
    return (loss, dx, dmeta, dw_in, dw_na, dw_hg, dw_o, dw_up, dw_down,
            dg_mix, dg_mlp, dg_fin, d_gain, d_rpb, d_lb)


N_CHIPS = 4
N_DEV = 8
ANY = pl.BlockSpec(memory_space=pl.ANY)


def _place():
    x, y, c = lax.axis_index("x"), lax.axis_index("y"), lax.axis_index("c")
    others = []
    for j in (1, 2, 3):
        tx = (1 - x) if (j >> 1) else x
        ty = (1 - y) if (j & 1) else y
        others.append((tx, ty))
    return x, y, c, others


def _piece(ref, axis, k, half, rh, cs):
    if axis == 1:
        return ref.at[pl.ds(pl.multiple_of(half * rh, 16), rh), pl.ds(pl.multiple_of(k * cs, LANES), cs)]
    return ref.at[pl.ds(pl.multiple_of(k * 2 * rh + half * rh, 16), rh), :]


def _cast_into_full(shards, axes, place, *, name):
    n = len(shards)
    steps, trs = _group_tiles([s.shape[0] for s in shards], 16)

    def body(p_ref, *refs):
        for i in range(n):
            refs[n + i][...] = refs[i][...].astype(BF16)

    def out_spec(tr, cs, axis):
        if axis == 1:
            return pl.BlockSpec((tr, cs), lambda i, p_ref: (i, p_ref[0]))
        return pl.BlockSpec((tr, cs), lambda i, p_ref: (p_ref[0] * steps + i, 0))

    return pl.pallas_call(
        body,
        grid_spec=pltpu.PrefetchScalarGridSpec(
            num_scalar_prefetch=1, grid=(steps,),
            in_specs=[pl.BlockSpec((tr, s.shape[1]), lambda i, p_ref: (i, 0))
                      for tr, s in zip(trs, shards)],
            out_specs=[out_spec(tr, s.shape[1], ax) for tr, s, ax in zip(trs, shards, axes)]),
        out_shape=[jax.ShapeDtypeStruct((s.shape[0], s.shape[1] * N_CHIPS) if ax == 1
                                        else (s.shape[0] * N_CHIPS, s.shape[1]), BF16)
                   for s, ax in zip(shards, axes)],
        compiler_params=_cp("parallel"), name=name)(place, *shards)


HBM_SPEC = pl.BlockSpec(memory_space=pltpu.HBM)
SEM_SPEC = pl.BlockSpec(memory_space=pltpu.SEMAPHORE)
SPLIT_COPY = pltpu.CompilerParams(has_side_effects=pltpu.SideEffectType.DATAFLOW_SIDE_EFFECTING)
TOKEN = jax.ShapeDtypeStruct((8, LANES), F32)


def _geo(fulls, axes):
    out = []
    for f, ax in zip(fulls, axes):
        r, cs = (f.shape[0], f.shape[1] // N_CHIPS) if ax == 1 else (f.shape[0] // N_CHIPS, f.shape[1])
        out.append((ax, r // 2, cs))
    return out


def _gather_copies(refs, geo, send_sems, recv_sems):
    x, y, c, others = _place()
    chip = 2 * x + y
    cps = []
    for i, (ax, rh, cs) in enumerate(geo):
        mine = _piece(refs[i], ax, chip, c, rh, cs)
        for j, (tx, ty) in enumerate(others):
            cps.append(pltpu.make_async_remote_copy(
                src_ref=mine, dst_ref=mine, send_sem=send_sems.at[3 * i + j],
                recv_sem=recv_sems.at[3 * i + j], device_id=(tx, ty, c), device_id_type=MESH))
    return cps


def _allgather_start(fulls, axes, after, *, name):
    n = len(fulls)
    geo = _geo(fulls, axes)

    def body(*refs):
        w_refs = refs[:n]
        send_sems, recv_sems = refs[n + 1], refs[n + 2]
        token = refs[2 * n + 3]
        for cp in _gather_copies(w_refs, geo, send_sems, recv_sems):
            cp.start()
        token[...] = jnp.zeros_like(token)

    out = pl.pallas_call(
        body, name=name,
        out_shape=(pltpu.SemaphoreType.DMA((3 * n,)), pltpu.SemaphoreType.DMA((3 * n,)),
                   *[pltpu.HBM(f.shape, f.dtype) for f in fulls], TOKEN),
        in_specs=[HBM_SPEC] * n + [ANY],
        out_specs=(SEM_SPEC, SEM_SPEC, *[HBM_SPEC] * n, pl.BlockSpec(memory_space=pltpu.VMEM)),
        input_output_aliases={i: 2 + i for i in range(n)},
        compiler_params=SPLIT_COPY,
    )(*[pltpu.with_memory_space_constraint(f, pltpu.HBM) for f in fulls], after)
    return out[0], out[1], list(out[2:2 + n]), out[2 + n]


def _allgather_wait(send_sems, recv_sems, fulls, axes, after, *, name):
    n = len(fulls)
    geo = _geo(fulls, axes)
    afters = tuple(after) if isinstance(after, (tuple, list)) else (after,)

    def body(*refs):
        w_refs = refs[:n]
        for cp in _gather_copies(w_refs, geo, refs[n], refs[n + 1]):
            cp.wait_send()
            cp.wait_recv()

    return list(pl.pallas_call(
        body, name=name,
        out_shape=[pltpu.HBM(f.shape, f.dtype) for f in fulls],
        in_specs=[HBM_SPEC] * n + [SEM_SPEC, SEM_SPEC] + [ANY] * len(afters),
        out_specs=[HBM_SPEC] * n,
        input_output_aliases={i: i for i in range(n)},
        compiler_params=SPLIT_COPY,
    )(*fulls, send_sems, recv_sems, *afters))


def _allgather_forward(fulls, axes, *, name):
    n = len(fulls)
    geo = _geo(fulls, axes)

    def body(*refs):
        o_refs = refs[n:2 * n]
        send_sems, recv_sems = refs[2 * n:]
        x, y, c, others = _place()

        def rcopy(i, j, half, to):
            ax, rh, cs = geo[i]
            ref = _piece(o_refs[i], ax, 2 * others[j][0] + others[j][1], half, rh, cs)
            return pltpu.make_async_remote_copy(
                src_ref=ref, dst_ref=ref, send_sem=send_sems.at[3 * i + j],
                recv_sem=recv_sems.at[3 * i + j], device_id=to, device_id_type=MESH)

        cps = [rcopy(i, j, c, (x, y, 1 - c)) for i in range(n) for j in range(3)]
        for cp in cps:
            cp.start()
        for i in range(n):
            for j in range(3):
                rcopy(i, j, 1 - c, (x, y, c)).wait_recv()
        for cp in cps:
            cp.wait_send()

    return list(pl.pallas_call(
        body, in_specs=[ANY] * n, out_specs=[ANY] * n,
        out_shape=[jax.ShapeDtypeStruct(f.shape, f.dtype) for f in fulls],
        input_output_aliases={i: i for i in range(n)},
        scratch_shapes=[pltpu.SemaphoreType.DMA((3 * n,)), pltpu.SemaphoreType.DMA((3 * n,))],
        name=name)(*fulls))


def _forward_copies(refs, geo, send_sems, recv_sems):
    x, y, c, others = _place()
    cps = []
    for i, (ax, rh, cs) in enumerate(geo):
        for j, (tx, ty) in enumerate(others):
            ref = _piece(refs[i], ax, 2 * tx + ty, c, rh, cs)
            cps.append(pltpu.make_async_remote_copy(
                src_ref=ref, dst_ref=ref, send_sem=send_sems.at[3 * i + j],
                recv_sem=recv_sems.at[3 * i + j], device_id=(x, y, 1 - c), device_id_type=MESH))
    return cps


def _allgather_forward_start(fulls, axes, *, name):
    n = len(fulls)
    geo = _geo(fulls, axes)

    def body(*refs):
        token = refs[2 * n + 2]
        for cp in _forward_copies(refs[:n], geo, refs[n], refs[n + 1]):
            cp.start()
        token[...] = jnp.zeros_like(token)

    out = pl.pallas_call(
        body, name=name,
        out_shape=(pltpu.SemaphoreType.DMA((3 * n,)), pltpu.SemaphoreType.DMA((3 * n,)),
                   *[pltpu.HBM(f.shape, f.dtype) for f in fulls], TOKEN),
        in_specs=[HBM_SPEC] * n,
        out_specs=(SEM_SPEC, SEM_SPEC, *[HBM_SPEC] * n, pl.BlockSpec(memory_space=pltpu.VMEM)),
        input_output_aliases={i: 2 + i for i in range(n)},
        compiler_params=SPLIT_COPY,
    )(*fulls)
    return out[0], out[1], list(out[2:2 + n]), out[2 + n]


def _allgather_forward_wait(send_sems, recv_sems, fulls, axes, after, *, name):
    n = len(fulls)
    geo = _geo(fulls, axes)

    def body(*refs):
        for cp in _forward_copies(refs[:n], geo, refs[n], refs[n + 1]):
            cp.wait_send()
            cp.wait_recv()

    return list(pl.pallas_call(
        body, name=name,
        out_shape=[pltpu.HBM(f.shape, f.dtype) for f in fulls],
        in_specs=[HBM_SPEC] * n + [SEM_SPEC, SEM_SPEC, ANY],
        out_specs=[HBM_SPEC] * n,
        input_output_aliases={i: i for i in range(n)},
        compiler_params=SPLIT_COPY,
    )(*fulls, send_sems, recv_sems, after))


def _chip_copies(blk_ref, land_ref, send_sems, recv_sems):
    x, y, c, others = _place()
    return [pltpu.make_async_remote_copy(
        src_ref=blk_ref, dst_ref=land_ref.at[2 * x + y], send_sem=send_sems.at[j],
        recv_sem=recv_sems.at[j], device_id=(tx, ty, c), device_id_type=MESH)
        for j, (tx, ty) in enumerate(others)]


def _chip_exchange_start(blk, *, name):
    land = pltpu.with_memory_space_constraint(lax.empty((N_CHIPS, *blk.shape), blk.dtype), pltpu.HBM)

    def body(blk_ref, land_ref, send_sems, recv_sems, blk_out, land_out, token):
        for cp in _chip_copies(blk_ref, land_ref, send_sems, recv_sems):
            cp.start()
        token[...] = jnp.zeros_like(token)

    return pl.pallas_call(
        body, name=name,
        out_shape=(pltpu.SemaphoreType.DMA((3,)), pltpu.SemaphoreType.DMA((3,)),
                   pltpu.HBM(blk.shape, blk.dtype), pltpu.HBM(land.shape, land.dtype), TOKEN),
        in_specs=[HBM_SPEC] * 2,
        out_specs=(SEM_SPEC, SEM_SPEC, HBM_SPEC, HBM_SPEC, pl.BlockSpec(memory_space=pltpu.VMEM)),
        input_output_aliases={0: 2, 1: 3},
        compiler_params=SPLIT_COPY,
    )(pltpu.with_memory_space_constraint(blk, pltpu.HBM), land)


def _chip_exchange_wait(send_sems, recv_sems, blk, land, after, *, name):
    def body(blk_ref, land_ref, send_sems, recv_sems, after_ref, blk_out, land_out):
        for cp in _chip_copies(blk_ref, land_ref, send_sems, recv_sems):
            cp.wait_send()
            cp.wait_recv()

    return pl.pallas_call(
        body, name=name,
        out_shape=[pltpu.HBM(blk.shape, blk.dtype), pltpu.HBM(land.shape, land.dtype)],
        in_specs=[HBM_SPEC] * 2 + [SEM_SPEC, SEM_SPEC, ANY],
        out_specs=[HBM_SPEC] * 2,
        input_output_aliases={0: 0, 1: 1},
        compiler_params=SPLIT_COPY,
    )(blk, land, send_sems, recv_sems, after)[1]


def _scatter_geo(parts, axes):
    out = []
    for p, ax in zip(parts, axes):
        _, rh, cols = p.shape
        out.append((ax, rh, cols // N_CHIPS if ax == 1 else cols))
    return out


def _scatter_copies(p_refs, q_refs, geo, send_sems, recv_sems):
    x, y, c, others = _place()
    chip = 2 * x + y
    cps = []
    for i, (ax, rh, cw) in enumerate(geo):
        for j, (tx, ty) in enumerate(others):
            k = 2 * tx + ty
            src = (p_refs[i].at[0, :, pl.ds(pl.multiple_of(k * cw, LANES), cw)] if ax == 1
                   else p_refs[i].at[k])
            cps.append(pltpu.make_async_remote_copy(
                src_ref=src, dst_ref=q_refs[i].at[chip], send_sem=send_sems.at[3 * i + j],
                recv_sem=recv_sems.at[3 * i + j], device_id=(tx, ty, c), device_id_type=MESH))
    return cps


def _scatter_start(parts, axes, *, name):
    n = len(parts)
    geo = _scatter_geo(parts, axes)
    slots = [pltpu.HBM((N_CHIPS, rh, cw), p.dtype) for p, (_, rh, cw) in zip(parts, geo)]

    def body(*refs):
        p_refs, q_refs = refs[:n], refs[n:2 * n]
        send_sems, recv_sems = refs[2 * n], refs[2 * n + 1]
        token = refs[4 * n + 2]
        for cp in _scatter_copies(p_refs, q_refs, geo, send_sems, recv_sems):
            cp.start()
        token[...] = jnp.zeros_like(token)

    land = [pltpu.with_memory_space_constraint(lax.empty(s.inner_aval.shape, s.inner_aval.dtype), pltpu.HBM)
            for s in slots]
    out = pl.pallas_call(
        body, name=name,
        out_shape=(pltpu.SemaphoreType.DMA((3 * n,)), pltpu.SemaphoreType.DMA((3 * n,)),
                   *[pltpu.HBM(p.shape, p.dtype) for p in parts], *slots, TOKEN),
        in_specs=[HBM_SPEC] * (2 * n),
        out_specs=(SEM_SPEC, SEM_SPEC, *[HBM_SPEC] * (2 * n), pl.BlockSpec(memory_space=pltpu.VMEM)),
        input_output_aliases={i: 2 + i for i in range(2 * n)},
        compiler_params=SPLIT_COPY,
    )(*[pltpu.with_memory_space_constraint(p, pltpu.HBM) for p in parts], *land)
    return out[0], out[1], list(out[2:2 + n]), list(out[2 + n:2 + 2 * n]), out[2 + 2 * n]


def _scatter_wait(send_sems, recv_sems, parts, slots, axes, after, *, name):
    n = len(parts)
    geo = _scatter_geo(parts, axes)

    def body(*refs):
        p_refs, q_refs = refs[:n], refs[n:2 * n]
        for cp in _scatter_copies(p_refs, q_refs, geo, refs[2 * n], refs[2 * n + 1]):
            cp.wait_send()
            cp.wait_recv()

    out = pl.pallas_call(
        body, name=name,
        out_shape=[pltpu.HBM(a.shape, a.dtype) for a in (*parts, *slots)],
        in_specs=[HBM_SPEC] * (2 * n) + [SEM_SPEC, SEM_SPEC, ANY],
        out_specs=[HBM_SPEC] * (2 * n),
        input_output_aliases={i: i for i in range(2 * n)},
        compiler_params=SPLIT_COPY,
    )(*parts, *slots, send_sems, recv_sems, after)
    return list(out[:n]), list(out[n:])


def _swap_copies(g_refs, r_refs, shapes, send_sems, recv_sems):
    x, y, c, _ = _place()
    cps = []
    for i, shape in enumerate(shapes):
        rh = shape[1] // 2
        src = g_refs[i].at[:, pl.ds(pl.multiple_of((1 - c) * rh, 16), rh), :]
        cps.append(pltpu.make_async_remote_copy(
            src_ref=src, dst_ref=r_refs[i], send_sem=send_sems.at[i], recv_sem=recv_sems.at[i],
            device_id=(x, y, 1 - c), device_id_type=MESH))
    return cps


def _sibling_swap_start(grads, *, name):
    n = len(grads)
    shapes = [g.shape for g in grads]
    lands = [pltpu.HBM((s[0], s[1] // 2, s[2]), g.dtype) for s, g in zip(shapes, grads)]

    def body(*refs):
        g_refs, r_refs = refs[:n], refs[n:2 * n]
        token = refs[4 * n + 2]
        for cp in _swap_copies(g_refs, r_refs, shapes, refs[2 * n], refs[2 * n + 1]):
            cp.start()
        token[...] = jnp.zeros_like(token)

    land = [pltpu.with_memory_space_constraint(lax.empty(s.inner_aval.shape, s.inner_aval.dtype), pltpu.HBM)
            for s in lands]
    out = pl.pallas_call(
        body, name=name,
        out_shape=(pltpu.SemaphoreType.DMA((n,)), pltpu.SemaphoreType.DMA((n,)),
                   *[pltpu.HBM(g.shape, g.dtype) for g in grads], *lands, TOKEN),
        in_specs=[HBM_SPEC] * (2 * n),
        out_specs=(SEM_SPEC, SEM_SPEC, *[HBM_SPEC] * (2 * n), pl.BlockSpec(memory_space=pltpu.VMEM)),
        input_output_aliases={i: 2 + i for i in range(2 * n)},
        compiler_params=SPLIT_COPY,
    )(*[pltpu.with_memory_space_constraint(g, pltpu.HBM) for g in grads], *land)
    return out[0], out[1], list(out[2:2 + n]), list(out[2 + n:2 + 2 * n]), out[2 + 2 * n]


def _sibling_swap_wait(send_sems, recv_sems, grads, lands, after, *, name):
    n = len(grads)
    shapes = [g.shape for g in grads]

    def body(*refs):
        g_refs, r_refs = refs[:n], refs[n:2 * n]
        for cp in _swap_copies(g_refs, r_refs, shapes, refs[2 * n], refs[2 * n + 1]):
            cp.wait_send()
            cp.wait_recv()

    out = pl.pallas_call(
        body, name=name,
        out_shape=[pltpu.HBM(a.shape, a.dtype) for a in (*grads, *lands)],
        in_specs=[HBM_SPEC] * (2 * n) + [SEM_SPEC, SEM_SPEC, ANY],
        out_specs=[HBM_SPEC] * (2 * n),
        input_output_aliases={i: i for i in range(2 * n)},
        compiler_params=SPLIT_COPY,
    )(*grads, *lands, send_sems, recv_sems, after)
    return list(out[:n]), list(out[n:])


def _pair_add(g3s, rxs, place, *, out_dtype, name):
    n = len(g3s)
    steps, trs = _group_tiles([g.shape[1] // 2 for g in g3s], 16)

    def body(p_ref, *refs):
        for i in range(n):
            refs[2 * n + i][...] = (refs[2 * i][...] + refs[2 * i + 1][...]).astype(out_dtype)

    in_specs, out_specs = [], []
    for g, tr in zip(g3s, trs):
        blk = (g.shape[0], tr, g.shape[2])
        in_specs += [pl.BlockSpec(blk, lambda i, p_ref: (0, p_ref[1] * steps + i, 0)),
                     pl.BlockSpec(blk, lambda i, p_ref: (0, i, 0))]
        out_specs.append(pl.BlockSpec(blk, lambda i, p_ref: (0, i, 0)))
    return pl.pallas_call(
        body,
        grid_spec=pltpu.PrefetchScalarGridSpec(
            num_scalar_prefetch=1, grid=(steps,), in_specs=in_specs, out_specs=out_specs),
        out_shape=[jax.ShapeDtypeStruct((g.shape[0], g.shape[1] // 2, g.shape[2]), out_dtype)
                   for g in g3s],
        compiler_params=_cp("parallel"), name=name)(place, *[a for q in zip(g3s, rxs) for a in q])


def _sum_slots(q, *, name):
    ns, rows, cols = q.shape
    tr = next(t for t in (128, 64, 32, 16, 8) if rows % t == 0)

    def body(q_ref, o_ref):
        acc = q_ref[0].astype(F32)
        for k in range(1, ns):
            acc = acc + q_ref[k].astype(F32)
        o_ref[...] = acc

    return pl.pallas_call(
        body, grid=(rows // tr,),
        in_specs=[pl.BlockSpec((ns, tr, cols), lambda i: (0, i, 0))],
        out_specs=_rspec(tr, cols),
        out_shape=jax.ShapeDtypeStruct((rows, cols), F32),
        compiler_params=_cp("parallel"), name=name)(q)


def _sum_chips(qs, ps, place, axes, *, name):
    n = len(qs)
    per = N_CHIPS + 1
    steps, trs = _group_tiles([q.shape[1] for q in qs], 16)

    def body(p_ref, *refs):
        chip = p_ref[0]
        for i in range(n):
            q_refs, own_ref = refs[per * i:per * i + N_CHIPS], refs[per * i + N_CHIPS]
            acc = jnp.where(chip == 0, own_ref[...], q_refs[0][...]).astype(F32)
            for k in range(1, N_CHIPS):
                acc = acc + jnp.where(chip == k, own_ref[...], q_refs[k][...]).astype(F32)
            refs[per * n + i][...] = acc

    def slot_spec(k, tr, cw):
        return pl.BlockSpec((None, tr, cw),
                            lambda i, p_ref: (jnp.where(p_ref[0] == k, (k + 1) % N_CHIPS, k), i, 0))

    in_specs, out_specs, operands = [], [], []
    for q, p, ax, tr in zip(qs, ps, axes, trs):
        cw = q.shape[2]
        in_specs += [slot_spec(k, tr, cw) for k in range(N_CHIPS)]
        in_specs.append(pl.BlockSpec((None, tr, cw), (lambda i, p_ref: (0, i, p_ref[0])) if ax == 1
                                     else (lambda i, p_ref: (p_ref[0], i, 0))))
        out_specs.append(pl.BlockSpec((tr, cw), lambda i, p_ref: (p_ref[1] * steps + i, 0)))
        operands += [q] * N_CHIPS + [p]
    return pl.pallas_call(
        body,
        grid_spec=pltpu.PrefetchScalarGridSpec(
            num_scalar_prefetch=1, grid=(steps,), in_specs=in_specs, out_specs=out_specs),
        out_shape=[jax.ShapeDtypeStruct((2 * q.shape[1], q.shape[2]), F32) for q in qs],
        compiler_params=_cp("parallel"), name=name)(place, *operands)


def _sibling_share(shards, *, name):
    n = len(shards)

    def body(*refs):
        o_refs = refs[n:2 * n]
        send_sems, recv_sems = refs[2 * n:]
        x, y, c, _ = _place()
        cps = []
        for i in range(n):
            rh = shards[i].shape[0] // 2
            mine = o_refs[i].at[pl.ds(pl.multiple_of(c * rh, 8), rh), :]
            cp = pltpu.make_async_remote_copy(
                src_ref=mine, dst_ref=mine, send_sem=send_sems.at[i], recv_sem=recv_sems.at[i],
                device_id=(x, y, 1 - c), device_id_type=MESH)
            cp.start()
            cps.append(cp)
        for i in range(n):
            rh = shards[i].shape[0] // 2
            theirs = o_refs[i].at[pl.ds(pl.multiple_of((1 - c) * rh, 8), rh), :]
            pltpu.make_async_remote_copy(
                src_ref=theirs, dst_ref=theirs, send_sem=send_sems.at[i], recv_sem=recv_sems.at[i],
                device_id=(x, y, c), device_id_type=MESH).wait_recv()
        for cp in cps:
            cp.wait_send()

    return pl.pallas_call(
        body, in_specs=[ANY] * n, out_specs=[ANY] * n,
        out_shape=[jax.ShapeDtypeStruct(h.shape, h.dtype) for h in shards],
        input_output_aliases={i: i for i in range(n)},
        scratch_shapes=[pltpu.SemaphoreType.DMA((n,)), pltpu.SemaphoreType.DMA((n,))],
        name=name)(*shards)


def _gather_all(blk, *, name, after=None, shares=()):
    rows, cols = blk.shape
    extra = [] if after is None else [after]
    n_s = len(shares)

    def body(x_ref, *refs):
        s_refs = refs[len(extra) + n_s + 1:len(extra) + 2 * n_s + 1]
        out_ref = refs[len(extra) + n_s]
        send_sems, recv_sems, local_sem, s_send, s_recv = refs[len(extra) + 2 * n_s + 1:]
        x, y, c = lax.axis_index("x"), lax.axis_index("y"), lax.axis_index("c")
        me = 4 * x + 2 * y + c
        mine = pltpu.make_async_copy(x_ref, out_ref.at[me], local_sem)
        mine.start()
        cps = []
        for i in range(n_s):
            rh = shares[i].shape[0] // 2
            half = s_refs[i].at[pl.ds(pl.multiple_of(c * rh, 8), rh), :]
            cp = pltpu.make_async_remote_copy(
                src_ref=half, dst_ref=half, send_sem=s_send.at[i], recv_sem=s_recv.at[i],
                device_id=(x, y, 1 - c), device_id_type=MESH)
            cp.start()
            cps.append(cp)
        for k in range(1, N_DEV):
            tx = (1 - x) if (k >> 2) & 1 else x
            ty = (1 - y) if (k >> 1) & 1 else y
            tc = (1 - c) if k & 1 else c
            cp = pltpu.make_async_remote_copy(
                src_ref=x_ref, dst_ref=out_ref.at[me], send_sem=send_sems.at[k - 1],
                recv_sem=recv_sems.at[k - 1], device_id=(tx, ty, tc), device_id_type=MESH)
            cp.start()
            cps.append(cp)
        for k in range(1, N_DEV):
            tx = (1 - x) if (k >> 2) & 1 else x
            ty = (1 - y) if (k >> 1) & 1 else y
            tc = (1 - c) if k & 1 else c
            got = out_ref.at[4 * tx + 2 * ty + tc]
            pltpu.make_async_remote_copy(
                src_ref=got, dst_ref=got, send_sem=send_sems.at[k - 1], recv_sem=recv_sems.at[k - 1],
                device_id=(x, y, c), device_id_type=MESH).wait_recv()
        for i in range(n_s):
            rh = shares[i].shape[0] // 2
            theirs = s_refs[i].at[pl.ds(pl.multiple_of((1 - c) * rh, 8), rh), :]
            pltpu.make_async_remote_copy(
                src_ref=theirs, dst_ref=theirs, send_sem=s_send.at[i], recv_sem=s_recv.at[i],
                device_id=(x, y, c), device_id_type=MESH).wait_recv()
        for cp in cps:
            cp.wait_send()
        mine.wait()

    vm = pl.BlockSpec(memory_space=pltpu.VMEM)
    out = pl.pallas_call(
        body, in_specs=[vm] + [ANY] * (len(extra) + n_s), out_specs=[vm] + [ANY] * n_s,
        out_shape=[jax.ShapeDtypeStruct((N_DEV, rows, cols), blk.dtype)]
        + [jax.ShapeDtypeStruct(s.shape, s.dtype) for s in shares],
        input_output_aliases={1 + len(extra) + i: 1 + i for i in range(n_s)},
        scratch_shapes=[pltpu.SemaphoreType.DMA((N_DEV - 1,)), pltpu.SemaphoreType.DMA((N_DEV - 1,)),
                        pltpu.SemaphoreType.DMA, pltpu.SemaphoreType.DMA((max(n_s, 1),)),
                        pltpu.SemaphoreType.DMA((max(n_s, 1),))],
        name=name)(blk, *extra, *shares)
    return (out[0], *out[1:]) if n_s else out[0]


def _as_rows(a):
    flat = a.reshape(-1)
    n = flat.shape[0]
    rows = -(-n // (8 * LANES)) * 8
    return jnp.pad(flat, (0, rows * LANES - n)).reshape(rows, LANES)


def _from_rows(p, shape):
    n = int(np.prod(shape))
    return p.reshape(-1)[:n].reshape(shape)


WEIGHT_AXES = (1, 1, 1, 0, 1, 0)
WIRE = BF16


def kernel(x, meta_tokens, w_in, w_na_out, w_hg_out, w_o, w_up, w_down, norm_mix, norm_mlp, norm_final, hg_norm, na_rpb, hg_lb_logits, loss_target, m_meta_tokens, m_w_in, m_w_na_out, m_w_hg_out, m_w_o, m_w_up, m_w_down, m_norm_mix, m_norm_mlp, m_norm_final, m_hg_norm, m_na_rpb, m_hg_lb_logits, v_meta_tokens, v_w_in, v_w_na_out, v_w_hg_out, v_w_o, v_w_up, v_w_down, v_norm_mix, v_norm_mlp, v_norm_final, v_hg_norm, v_na_rpb, v_hg_lb_logits):
    xi, yi, ci = lax.axis_index("x"), lax.axis_index("y"), lax.axis_index("c")
    chip = 2 * xi + yi
    d = x.shape[-1]
    dshard = meta_tokens.shape[1]
    hgw = hg_norm.shape[1]
    lbs = hg_lb_logits.shape[2]
    big = [w_in[0], w_na_out[0], w_hg_out[0], w_o[0], w_up[0], w_down[0]]
    big_m = [m_w_in[0], m_w_na_out[0], m_w_hg_out[0], m_w_o[0], m_w_up[0], m_w_down[0]]
    big_v = [v_w_in[0], v_w_na_out[0], v_w_hg_out[0], v_w_o[0], v_w_up[0], v_w_down[0]]

    place = jnp.stack([chip, ci]).astype(jnp.int32)
    in_axes, rest_axes = WEIGHT_AXES[:1], WEIGHT_AXES[1:]
    own_w = _cast_into_full(big, WEIGHT_AXES, place, name="cast_shards")
    small_in = jnp.concatenate([_as_rows(meta_tokens), _as_rows(hg_lb_logits)], axis=0)
    sm_send, sm_recv, sm_blk, sm_land, sm_token = _chip_exchange_start(small_in,
                                                                       name="small_params_start")
    in_send, in_recv, in_bufs, in_token = _allgather_start(own_w[:1], in_axes, sm_token,
                                                           name="weight_allgather_in_start")
    ag_send, ag_recv, ag_bufs, ag_token = _allgather_start(own_w[1:], rest_axes, in_token,
                                                           name="weight_allgather_rest_start")
    sm_land = _chip_exchange_wait(sm_send, sm_recv, sm_blk, sm_land, ag_token,
                                  name="small_params_wait")
    small_all = lax.dynamic_update_slice(sm_land, small_in[None], (chip, 0, 0))
    forward = {}

    def first_weight(after):
        got = _allgather_wait(in_send, in_recv, in_bufs, in_axes, after,
                              name="weight_allgather_in_wait")
        return _allgather_forward(got, in_axes, name="weight_allgather_in_forward")[0]

    def rest_landed(after):
        got = _allgather_wait(ag_send, ag_recv, ag_bufs, rest_axes, after,
                              name="weight_allgather_rest_wait")
        send, recv, bufs, token = _allgather_forward_start(
            got, rest_axes, name="weight_allgather_rest_forward_start")
        forward["rest"] = (send, recv, bufs)
        return token

    def rest_weights(after):
        return _allgather_forward_wait(*forward["rest"], rest_axes, after,
                                       name="weight_allgather_rest_forward_wait")

    n_meta_rows = N_META * dshard // LANES
    meta_full = (small_all[:, :n_meta_rows].reshape(N_CHIPS, N_META, dshard)
                 .transpose(1, 0, 2).reshape(N_META, d))
    lbl_full = (small_all[:, n_meta_rows:].reshape(N_CHIPS, -1)[:, :4 * lbs]
                .reshape(N_CHIPS, 2, 2, lbs).transpose(1, 2, 0, 3).reshape(2, 2, N_CHIPS * lbs))
    lb = jax.nn.softmax(lbl_full, axis=1)[:, 0]

    def by_chip(dws, axes):
        return [g.reshape(1, *g.shape) if ax == 1
                else g.reshape(N_CHIPS, g.shape[0] // N_CHIPS, g.shape[1]) for g, ax in zip(dws, axes)]

    flying = {}

    def scatter(tag, axes, g3, rx):
        parts = _pair_add(g3, rx, place, out_dtype=WIRE, name=f"grad_pair_add_{tag}")
        send, recv, parts, slots, token = _scatter_start(parts, axes,
                                                         name=f"grad_scatter_{tag}_start")
        flying[tag] = (send, recv, parts, slots)
        return token

    def swap(tag, axes):
        def start(dws):
            send, recv, g3, lands, token = _sibling_swap_start(
                by_chip(dws, axes), name=f"grad_sibling_swap_{tag}_start")
            flying["swap_" + tag] = (send, recv, g3, lands)
            return token

        def finish(after):
            g3, rx = _sibling_swap_wait(*flying["swap_" + tag], after,
                                        name=f"grad_sibling_swap_{tag}_wait")
            return scatter(tag, axes, g3, rx)
        return start, finish

    swap_rest, scatter_rest = swap("rest", rest_axes)
    swap_in, scatter_in = swap("in", in_axes)

    def landed(tag, axes, after):
        return _scatter_wait(*flying[tag], axes, after, name=f"grad_scatter_{tag}_wait")

    (loss, dx, dmeta, *_, dg_mix, dg_mlp, dg_fin, d_gain, d_rpb, d_lb) = _local_step(
        x[0], loss_target[0], meta_full, first_weight, rest_weights, norm_mix, norm_mlp,
        norm_final.reshape(1, d), hg_norm, na_rpb[0], lb, swap_rest, scatter_rest,
        lambda dw_in: swap_in([dw_in]), rest_landed, scatter_in)

    parts_rest, slots_rest = landed("rest", rest_axes, dx)
    g_rest = _sibling_share(_sum_chips(slots_rest, parts_rest, place, rest_axes,
                                       name="grad_sum_chips_rest"),
                            name="grad_sibling_share_rest")
    out_rest = _adamw(big[1:], g_rest, big_m[1:], big_v[1:], name="adamw_rest")
    parts_in, slots_in = landed("in", in_axes, out_rest[-1][0])
    half_in = _sum_chips(slots_in, parts_in, place, in_axes, name="grad_sum_chips_in")

    d_rpb_c = d_rpb[:, :2 * NA_WIN_W - 1]
    small_g = [dmeta, dg_mix, dg_mlp, dg_fin, d_gain, d_rpb_c, d_lb, loss]
    packed = jnp.concatenate([_as_rows(a) for a in small_g], axis=0)
    gathered, g_in = _gather_all(packed, shares=half_in, name="gather_small_grads")
    total = _sum_slots(gathered, name="sum_small_grads")
    offs = np.cumsum([0] + [_as_rows(a).shape[0] for a in small_g])
    take = lambda i, shape: _from_rows(total[offs[i]:offs[i + 1]], shape)
    g_meta_full = take(0, (N_META, d))
    g_norm_mix, g_norm_mlp = take(1, (1, d)), take(2, (1, d))
    g_norm_final = take(3, (d,))
    g_hg_norm = take(4, (1, hgw))
    g_rpb = take(5, na_rpb.shape)
    g_lb = take(6, (2, hgw))
    loss_total = take(7, (1, LANES))[0, 0]
    g_meta = lax.dynamic_slice_in_dim(g_meta_full, chip * dshard, dshard, axis=1)
    dl0 = lb * (1.0 - lb) * g_lb
    g_lbl_full = jnp.stack([dl0, -dl0], axis=1)
    g_lbl = lax.dynamic_slice_in_dim(g_lbl_full, chip * lbs, lbs, axis=2)

    big_out = _adamw(big[:1], [g_in], big_m[:1], big_v[:1], name="adamw_in") + out_rest
    small_w = [meta_tokens, norm_mix, norm_mlp, norm_final, hg_norm, na_rpb, hg_lb_logits]
    small_gr = [g_meta, g_norm_mix, g_norm_mlp, g_norm_final, g_hg_norm, g_rpb, g_lbl]
    small_m = [m_meta_tokens, m_norm_mix, m_norm_mlp, m_norm_final, m_hg_norm, m_na_rpb, m_hg_lb_logits]
    small_v = [v_meta_tokens, v_norm_mix, v_norm_mlp, v_norm_final, v_hg_norm, v_na_rpb, v_hg_lb_logits]
    pk = lambda lst: jnp.concatenate([_as_rows(a) for a in lst], axis=0)
    ((sd, sm, sv, _),) = _adamw([pk(small_w)], [pk(small_gr)], [pk(small_m)], [pk(small_v)],
                             name="adamw_small")
    soffs = np.cumsum([0] + [_as_rows(a).shape[0] for a in small_w])
    unpk = lambda p: [_from_rows(p[soffs[i]:soffs[i + 1]], small_w[i].shape) for i in range(len(small_w))]
    sd, sm, sv = unpk(sd), unpk(sm), unpk(sv)

    def order(bigs, smalls):
        return [smalls[0]] + [b.reshape(1, *b.shape) for b in bigs] + smalls[1:]

    grads = order([o[3] for o in big_out], small_gr)
    deltas = order([o[0] for o in big_out], sd)
    new_m = order([o[1] for o in big_out], sm)
    new_v = order([o[2] for o in big_out], sv)
    return (loss_total, dx.reshape(1, *dx.shape), *grads, *deltas, *new_m, *new_v)
```

```python
import functools

import numpy as np
import jax
import jax.numpy as jnp
from jax import lax
from jax.experimental import pallas as pl
from jax.experimental.pallas import tpu as pltpu

F32 = jnp.float32
BF16 = jnp.bfloat16
HIGHEST = lax.Precision.HIGHEST

GRID_W = 64
N_META = 16
EPS = 1e-6
NA_HEAD_DIM = 64
NA_WIN_H = 8
NA_WIN_W = 16
HG_DK = 128
LANES = 128
ROW_ALIGN = 128
VMEM_LIMIT = 48 * 1024 * 1024
ADAM_LR = 0.001
ADAM_B1 = 0.9
ADAM_B2 = 0.999
ADAM_EPS = 1e-08
ADAM_WD = 0.01
ADAM_STEP = 10

MESH = pl.DeviceIdType.MESH


def _cp(*sem):
    return pltpu.CompilerParams(dimension_semantics=sem, vmem_limit_bytes=VMEM_LIMIT)


def _sigmoid(x):
    return 0.5 * jnp.tanh(0.5 * x) + 0.5


def _dot(a, b, dims, precision=None):
    return lax.dot_general(a, b, (dims, ((), ())), preferred_element_type=F32, precision=precision)


def _nn(a, b, **kw):
    return _dot(a, b, ((1,), (0,)), **kw)


def _nt(a, b, **kw):
    return _dot(a, b, ((1,), (1,)), **kw)


def _tn(a, b, **kw):
    return _dot(a, b, ((0,), (0,)), **kw)


def _matmul(a, b, *, ta=False, tb=False, tm=None, tn=None, tk=None, out_dtype=F32, name,
            precision=None, after=None, epilogue=None, tiles=(), out_dtypes=None):
    extra = [] if after is None else [after]
    single = out_dtypes is None
    if single:
        out_dtypes = (out_dtype,)
    n_t, n_o = len(tiles), len(out_dtypes)
    if ta:
        kdim, m = a.shape
    else:
        m, kdim = a.shape
    if tb:
        n, k2 = b.shape
    else:
        k2, n = b.shape
    assert kdim == k2, (a.shape, b.shape, ta, tb)
    if tm is None:
        if ta:
            tm = next(t for t in (1024, 512, 256, 128, m) if m % t == 0)
        else:
            tm = m // 2 if (m // 2) % 16 == 0 and m > 512 else m
    if tn is None:
        wide = (1024,) if not ta and len(tiles) <= 1 else ()
        tn = next(t for t in (*wide, 512, 256, 128, n) if n % t == 0)
    if tk is None:
        tk = kdim if ta else next(t for t in (2048, 1024, 512, 256, 128, kdim) if kdim % t == 0)
    assert m % tm == 0 and n % tn == 0 and kdim % tk == 0, (m, n, kdim, tm, tn, tk)
    nk = kdim // tk
    op_dtype = F32 if precision is not None else BF16

    def body(a_ref, b_ref, *refs):
        t_refs = refs[:n_t]
        o_refs = refs[n_t + len(extra):n_t + len(extra) + n_o]
        av = a_ref[...].astype(op_dtype)
        bv = b_ref[...].astype(op_dtype)
        dims = ((0 if ta else 1,), (1 if tb else 0,))
        part = _dot(av, bv, dims, precision=precision)

        def finish(acc):
            outs = (acc,) if epilogue is None else epilogue(acc, *[t[...] for t in t_refs])
            for o_ref, val in zip(o_refs, outs):
                o_ref[...] = val.astype(o_ref.dtype)

        if nk == 1:
            finish(part)
            return
        acc_ref = refs[-1]
        kk = pl.program_id(2)

        @pl.when(kk == 0)
        def _():
            acc_ref[...] = part

        @pl.when((kk > 0) & (kk < nk - 1))
        def _():
            acc_ref[...] += part

        @pl.when(kk == nk - 1)
        def _():
            finish(acc_ref[...] + part)

    a_spec = (pl.BlockSpec((tk, tm), lambda i, j, k: (k, i)) if ta
              else pl.BlockSpec((tm, tk), lambda i, j, k: (i, k)))
    b_spec = (pl.BlockSpec((tn, tk), lambda i, j, k: (j, k)) if tb
              else pl.BlockSpec((tk, tn), lambda i, j, k: (k, j)))
    for _, off in tiles:
        assert off % tn == 0, (off, tn)
    t_specs = [pl.BlockSpec((tm, tn), functools.partial(lambda i, j, k, o: (i, o + j), o=off // tn))
               for _, off in tiles]
    o_spec = pl.BlockSpec((tm, tn), lambda i, j, k: (i, j))
    outs = pl.pallas_call(
        body,
        grid=(m // tm, n // tn, nk),
        in_specs=[a_spec, b_spec] + t_specs + [pl.BlockSpec(memory_space=pl.ANY)] * len(extra),
        out_specs=[o_spec] * n_o,
        out_shape=[jax.ShapeDtypeStruct((m, n), dt) for dt in out_dtypes],
        scratch_shapes=[pltpu.VMEM((tm, tn), F32)] if nk > 1 else [],
        compiler_params=_cp("parallel", "parallel", "arbitrary"),
        name=name,
    )(a, b, *[t for t, _ in tiles], *extra)
    return outs[0] if single else outs


def _rspec(tr, w, cb=0):
    return pl.BlockSpec((tr, w), lambda i: (i, cb))


def _fspec(shape):
    nd = len(shape)
    return pl.BlockSpec(shape, lambda i: (0,) * nd)


ROW_VMEM_BUDGET = 20 * 1024 * 1024
ROW_MIN_STEPS = 4


def _row_tile(lp, row_bytes):
    for k in range(ROW_MIN_STEPS, lp // 16 + 1):
        tr = lp // k
        if lp % k == 0 and tr % 16 == 0 and 2 * tr * row_bytes <= ROW_VMEM_BUDGET:
            return tr
    return lp


def _token_rows_copy(i, n_tiles, tr, n_tok, tok_ref, buf_ref, sem, *, to_tokens, start=True,
                     wait=True):
    assert n_tiles >= 2 and 0 < n_tok + N_META - (n_tiles - 1) * tr <= tr

    def run(tok_row, buf_row, count):
        tok = tok_ref.at[pl.ds(tok_row, count), :]
        buf = buf_ref.at[pl.ds(buf_row, count), :]
        cp = pltpu.make_async_copy(buf, tok, sem) if to_tokens else pltpu.make_async_copy(tok, buf, sem)
        if start:
            cp.start()
        if wait:
            cp.wait()

    @pl.when(i == 0)
    def _():
        run(0, N_META, tr - N_META)

    if n_tiles > 2:
        @pl.when((i > 0) & (i < n_tiles - 1))
        def _():
            run(pl.multiple_of(i * tr - N_META, 8), 0, tr)

    @pl.when(i == n_tiles - 1)
    def _():
        run((n_tiles - 1) * tr - N_META, 0, n_tok + N_META - (n_tiles - 1) * tr)


def _embed_norm(x, tgt, meta, g, *, lp, name):
    n_tok, d = x.shape
    tr = _row_tile(lp, d * (4 + 2 + 4))
    n_tiles = lp // tr

    def body(x_ref, tgt_ref, meta_ref, g_ref, h_ref, o_ref, tp_ref, buf_ref, tbuf_ref, sems):
        i = pl.program_id(0)
        buf_ref[...] = jnp.zeros_like(buf_ref)
        tbuf_ref[...] = jnp.zeros_like(tbuf_ref)

        @pl.when(i == 0)
        def _():
            buf_ref[0:N_META, :] = meta_ref[...]

        _token_rows_copy(i, n_tiles, tr, n_tok, tgt_ref, tbuf_ref, sems.at[1], to_tokens=False,
                         wait=False)
        _token_rows_copy(i, n_tiles, tr, n_tok, x_ref, buf_ref, sems.at[0], to_tokens=False)
        xv = buf_ref[...]
        h_ref[...] = xv
        r = lax.rsqrt(jnp.mean(xv * xv, axis=-1, keepdims=True) + EPS)
        o_ref[...] = (xv * r * g_ref[...]).astype(BF16)
        _token_rows_copy(i, n_tiles, tr, n_tok, tgt_ref, tbuf_ref, sems.at[1], to_tokens=False,
                         start=False)
        tp_ref[...] = tbuf_ref[...]

    return pl.pallas_call(
        body, grid=(n_tiles,),
        in_specs=[ANY, ANY, _fspec((N_META, d)), _fspec((1, d))],
        out_specs=[_rspec(tr, d), _rspec(tr, d), _rspec(tr, d)],
        out_shape=[jax.ShapeDtypeStruct((lp, d), F32), jax.ShapeDtypeStruct((lp, d), BF16),
                   jax.ShapeDtypeStruct((lp, d), F32)],
        scratch_shapes=[pltpu.VMEM((tr, d), F32), pltpu.VMEM((tr, d), F32),
                        pltpu.SemaphoreType.DMA((2,))],
        compiler_params=_cp("parallel"), name=name)(x, tgt, meta, g)


def _residual_norm(h, t, g, *, name):
    lp, d = h.shape
    tr = _row_tile(lp, d * (4 + 4 + 4 + 2))

    def body(h_ref, t_ref, g_ref, h1_ref, m_ref):
        xv = h_ref[...] + t_ref[...]
        h1_ref[...] = xv
        r = lax.rsqrt(jnp.mean(xv * xv, axis=-1, keepdims=True) + EPS)
        m_ref[...] = (xv * r * g_ref[...]).astype(BF16)

    return pl.pallas_call(
        body, grid=(lp // tr,),
        in_specs=[_rspec(tr, d), _rspec(tr, d), _fspec((1, d))],
        out_specs=[_rspec(tr, d), _rspec(tr, d)],
        out_shape=[jax.ShapeDtypeStruct((lp, d), F32), jax.ShapeDtypeStruct((lp, d), BF16)],
        compiler_params=_cp("parallel"), name=name)(h, t, g)


def _rmsnorm_bwd_add(x, g, dy, dres, *, name):
    lp, d = x.shape
    tr = _row_tile(lp, d * (4 * 4 + 2))

    def body(x_ref, g_ref, dy_ref, dr_ref, dx_ref, dx16_ref, dg_ref):
        @pl.when(pl.program_id(0) == 0)
        def _():
            dg_ref[...] = jnp.zeros_like(dg_ref)

        xv = x_ref[...]
        r = lax.rsqrt(jnp.mean(xv * xv, axis=-1, keepdims=True) + EPS)
        xh = xv * r
        dyv = dy_ref[...]
        dg_ref[...] += jnp.sum(dyv * xh, axis=0, keepdims=True)
        dxh = dyv * g_ref[...]
        dx = dr_ref[...] + r * (dxh - xh * jnp.mean(dxh * xh, axis=-1, keepdims=True))
        dx_ref[...] = dx
        dx16_ref[...] = dx.astype(BF16)

    return pl.pallas_call(
        body, grid=(lp // tr,),
        in_specs=[_rspec(tr, d), _fspec((1, d)), _rspec(tr, d), _rspec(tr, d)],
        out_specs=[_rspec(tr, d), _rspec(tr, d), _fspec((1, d))],
        out_shape=[jax.ShapeDtypeStruct((lp, d), F32), jax.ShapeDtypeStruct((lp, d), BF16),
                   jax.ShapeDtypeStruct((1, d), F32)],
        compiler_params=_cp("arbitrary"), name=name)(x, g, dy, dres)


def _rmsnorm_bwd_tokens(x, g, dy, dres, *, n_tok, name):
    lp, d = x.shape
    tr = _row_tile(lp, d * 4 * 4)
    n_tiles = lp // tr

    def body(x_ref, g_ref, dy_ref, dr_ref, dtok_ref, dmeta_ref, dg_ref, buf_ref, sem):
        i = pl.program_id(0)

        @pl.when(i == 0)
        def _():
            dg_ref[...] = jnp.zeros_like(dg_ref)

        xv = x_ref[...]
        r = lax.rsqrt(jnp.mean(xv * xv, axis=-1, keepdims=True) + EPS)
        xh = xv * r
        dyv = dy_ref[...]
        dg_ref[...] += jnp.sum(dyv * xh, axis=0, keepdims=True)
        dxh = dyv * g_ref[...]
        buf_ref[...] = dr_ref[...] + r * (dxh - xh * jnp.mean(dxh * xh, axis=-1, keepdims=True))

        @pl.when(i == 0)
        def _():
            dmeta_ref[...] = buf_ref[0:N_META, :]

        _token_rows_copy(i, n_tiles, tr, n_tok, dtok_ref, buf_ref, sem, to_tokens=True)

    return pl.pallas_call(
        body, grid=(n_tiles,),
        in_specs=[_rspec(tr, d), _fspec((1, d)), _rspec(tr, d), _rspec(tr, d)],
        out_specs=[ANY, _fspec((N_META, d)), _fspec((1, d))],
        out_shape=[jax.ShapeDtypeStruct((n_tok, d), F32), jax.ShapeDtypeStruct((N_META, d), F32),
                   jax.ShapeDtypeStruct((1, d), F32)],
        scratch_shapes=[pltpu.VMEM((tr, d), F32), pltpu.SemaphoreType.DMA],
        compiler_params=_cp("arbitrary"), name=name)(x, g, dy, dres)


def _final_loss(h1, t2, g, tgt, *, n_tok, name):
    lp, d = h1.shape
    tr = _row_tile(lp, d * (4 * 4 + 2))
    n_tiles = lp // tr

    def body(h_ref, t_ref, g_ref, tg_ref, dh_ref, dh16_ref, loss_ref, dg_ref):
        i = pl.program_id(0)

        @pl.when(i == 0)
        def _():
            loss_ref[...] = jnp.zeros_like(loss_ref)
            dg_ref[...] = jnp.zeros_like(dg_ref)

        xv = h_ref[...] + t_ref[...]
        r = lax.rsqrt(jnp.mean(xv * xv, axis=-1, keepdims=True) + EPS)
        xh = xv * r
        gv = g_ref[...]
        row = i * tr + lax.broadcasted_iota(jnp.int32, (tr, 1), 0)
        valid = (row >= N_META) & (row < N_META + n_tok)
        err = jnp.where(valid, xh * gv - tg_ref[...], 0.0)
        loss_ref[...] += jnp.sum(0.5 * err * err) / d
        dy = err / d
        dg_ref[...] += jnp.sum(dy * xh, axis=0, keepdims=True)
        dxh = dy * gv
        dh = r * (dxh - xh * jnp.mean(dxh * xh, axis=-1, keepdims=True))
        dh_ref[...] = dh
        dh16_ref[...] = dh.astype(BF16)

    return pl.pallas_call(
        body, grid=(n_tiles,),
        in_specs=[_rspec(tr, d), _rspec(tr, d), _fspec((1, d)), _rspec(tr, d)],
        out_specs=[_rspec(tr, d), _rspec(tr, d), _fspec((1, LANES)), _fspec((1, d))],
        out_shape=[jax.ShapeDtypeStruct((lp, d), F32), jax.ShapeDtypeStruct((lp, d), BF16),
                   jax.ShapeDtypeStruct((1, LANES), F32), jax.ShapeDtypeStruct((1, d), F32)],
        compiler_params=_cp("arbitrary"), name=name)(h1, t2, g, tgt)


def _hg_out(o_f, o_b, proj, gain, *, col_g, name):
    lp, w = o_f.shape
    tr = _row_tile(lp, w * (3 * 4 + 2))
    hh = w // HG_DK

    def body(of_ref, ob_ref, g_ref, gain_ref, y_ref):
        gv = g_ref[...]
        sg = gv * _sigmoid(gv)
        for h in range(hh):
            sl = slice(h * HG_DK, (h + 1) * HG_DK)
            o = of_ref[:, sl] + ob_ref[:, sl]
            r = lax.rsqrt(jnp.mean(o * o, axis=-1, keepdims=True) + EPS)
            y_ref[:, sl] = (o * r * gain_ref[:, sl] * sg[:, sl]).astype(BF16)

    return pl.pallas_call(
        body, grid=(lp // tr,),
        in_specs=[_rspec(tr, w), _rspec(tr, w), _rspec(tr, w, col_g // w), _fspec((1, w))],
        out_specs=_rspec(tr, w),
        out_shape=jax.ShapeDtypeStruct((lp, w), BF16),
        compiler_params=_cp("parallel"), name=name)(o_f, o_b, proj, gain)


def _hg_out_bwd(o_f, o_b, proj, gain, dy, *, col_g, name):
    lp, w = o_f.shape
    tr = _row_tile(lp, w * (5 * 4 + 2))
    hh = w // HG_DK

    def body(of_ref, ob_ref, g_ref, gain_ref, dy_ref, do_ref, dg_ref, dgain_ref):
        @pl.when(pl.program_id(0) == 0)
        def _():
            dgain_ref[...] = jnp.zeros_like(dgain_ref)

        for h in range(hh):
            sl = slice(h * HG_DK, (h + 1) * HG_DK)
            gv = g_ref[:, sl]
            s = _sigmoid(gv)
            sg = gv * s
            dsg = s + gv * s * (1.0 - s)
            o = of_ref[:, sl] + ob_ref[:, sl]
            r = lax.rsqrt(jnp.mean(o * o, axis=-1, keepdims=True) + EPS)
            on = o * r
            dyv = dy_ref[:, sl]
            gn = gain_ref[:, sl]
            dgain_ref[:, sl] += jnp.sum(dyv * on * sg, axis=0, keepdims=True)
            dg_ref[:, sl] = (dyv * on * gn * dsg).astype(BF16)
            don = dyv * gn * sg
            do_ref[:, sl] = r * (don - on * jnp.mean(don * on, axis=-1, keepdims=True))

    return pl.pallas_call(
        body, grid=(lp // tr,),
        in_specs=[_rspec(tr, w), _rspec(tr, w), _rspec(tr, w, col_g // w), _fspec((1, w)),
                  _rspec(tr, w)],
        out_specs=[_rspec(tr, w), _rspec(tr, w), _fspec((1, w))],
        out_shape=[jax.ShapeDtypeStruct((lp, w), F32), jax.ShapeDtypeStruct((lp, w), BF16),
                   jax.ShapeDtypeStruct((1, w), F32)],
        compiler_params=_cp("arbitrary"), name=name)(o_f, o_b, proj, gain, dy)


HG_ROWS = 128
HG_HALVES = (1, 2, 4, 8, 16, 32, 64)


def _hg_gates(zq, z, lbv):
    sq = _sigmoid(zq)
    s = _sigmoid(z)
    f = lbv + (1.0 - lbv) * s
    kk = (1.0 - lbv) * (1.0 - s)
    return zq * sq, sq, s, f, jnp.log(f), kk


def _block_cumsum(g, pos, suffix):
    x = g
    for k in HG_HALVES:
        if suffix:
            x = x + jnp.where(pos < HG_ROWS - k, pltpu.roll(x, HG_ROWS - k, 0), 0.0)
        else:
            x = x + jnp.where(pos >= k, pltpu.roll(x, k, 0), 0.0)
    return x


def _pair_levels(b, pos, reverse):
    out = []
    first = b
    for m in HG_HALVES:
        if m > 1:
            first = jnp.where((pos & (m - 1)) >= m // 2, pltpu.roll(first, m // 2, 0), first)
        nxt = pltpu.roll(first, HG_ROWS - m, 0)
        upper = (pos & (2 * m - 1)) >= m
        if reverse:
            eq = jnp.where(upper, 0.0, jnp.exp(b - nxt))
            ek = jnp.where(upper, jnp.exp(first - b), 0.0)
        else:
            eq = jnp.where(upper, jnp.exp(b - first), 0.0)
            ek = jnp.where(upper, 0.0, jnp.exp(nxt - b))
        out.append((eq, ek))
    return out


def _pair_masks(mask_ref):
    ri = lax.broadcasted_iota(jnp.int32, (HG_ROWS, HG_ROWS), 0)
    ci = lax.broadcasted_iota(jnp.int32, (HG_ROWS, HG_ROWS), 1)
    for i, m in enumerate(HG_HALVES):
        sh = m.bit_length()
        mask_ref[i] = jnp.where((ri >> sh) == (ci >> sh), 1.0, 0.0)


def _hg_scan_fwd(proj, lb, *, reverse, col_q, col_z, col_i, hh, name):
    lp = proj.shape[0]
    n_blocks = lp // HG_ROWS
    last = 0 if reverse else HG_ROWS - 1

    def body(q_ref, z_ref, i_ref, lb_ref, o_ref, st_ref, mask_ref):
        lbv = lb_ref[...]
        pos = lax.broadcasted_iota(jnp.int32, (HG_ROWS, 1), 0)
        ri = lax.broadcasted_iota(jnp.int32, (HG_ROWS, HG_ROWS), 0)
        ci = lax.broadcasted_iota(jnp.int32, (HG_ROWS, HG_ROWS), 1)
        _pair_masks(mask_ref)

        def block(bi, st):
            bb = (n_blocks - 1 - bi) if reverse else bi
            r0 = pl.multiple_of(bb * HG_ROWS, HG_ROWS)
            v16 = i_ref[pl.ds(r0, HG_ROWS), :].astype(BF16)
            qh, _, _, _, g, kk = _hg_gates(q_ref[pl.ds(r0, HG_ROWS), :],
                                           z_ref[pl.ds(r0, HG_ROWS), :], lbv)
            b = _block_cumsum(g, pos, reverse)
            bl = b[last:last + 1, :]
            qe = (qh * jnp.exp(b)).astype(BF16)
            kd = (kk * jnp.exp(bl - b)).astype(BF16)
            a = jnp.where(ri == ci, jnp.sum(qh * kk, axis=1, keepdims=True), 0.0)
            for i, (eq, ek) in enumerate(_pair_levels(b, pos, reverse)):
                a = a + mask_ref[i] * _nt((qh * eq).astype(BF16), (kk * ek).astype(BF16))
            st_ref[bb] = st
            o_ref[pl.ds(r0, HG_ROWS), :] = _nn(a.astype(BF16), v16) + _nt(qe, st.astype(BF16))
            return jnp.exp(bl) * st + _tn(v16, kd)

        lax.fori_loop(0, n_blocks, block, jnp.zeros((HG_DK, HG_DK), F32), unroll=2)

    cspec = lambda col: pl.BlockSpec((lp, HG_DK), lambda h: (0, col // HG_DK + h))
    return pl.pallas_call(
        body, grid=(hh,),
        in_specs=[cspec(col_q), cspec(col_z), cspec(col_i),
                  pl.BlockSpec((None, 1, HG_DK), lambda h: (h, 0, 0))],
        out_specs=[pl.BlockSpec((lp, HG_DK), lambda h: (0, h)),
                   pl.BlockSpec((None, n_blocks, HG_DK, HG_DK), lambda h: (h, 0, 0, 0))],
        out_shape=[jax.ShapeDtypeStruct((lp, hh * HG_DK), F32),
                   jax.ShapeDtypeStruct((hh, n_blocks, HG_DK, HG_DK), F32)],
        scratch_shapes=[pltpu.VMEM((len(HG_HALVES), HG_ROWS, HG_ROWS), F32)],
        compiler_params=_cp("parallel"), name=name)(proj, proj, proj, lb)


def _hg_scan_bwd(proj, lb, states, do, *, reverse, col_q, col_z, col_i, hh, name):
    lp = proj.shape[0]
    n_blocks = lp // HG_ROWS
    last = 0 if reverse else HG_ROWS - 1

    def body(q_ref, z_ref, i_ref, lb_ref, st_ref, do_ref, dq_ref, dz_ref, dv_ref, dlb_ref, mask_ref):
        lbv = lb_ref[...]
        pos = lax.broadcasted_iota(jnp.int32, (HG_ROWS, 1), 0)
        ri = lax.broadcasted_iota(jnp.int32, (HG_ROWS, HG_ROWS), 0)
        ci = lax.broadcasted_iota(jnp.int32, (HG_ROWS, HG_ROWS), 1)
        _pair_masks(mask_ref)

        def block(bi, carry):
            dst, dlb = carry
            bb = bi if reverse else (n_blocks - 1 - bi)
            r0 = pl.multiple_of(bb * HG_ROWS, HG_ROWS)
            zq = q_ref[pl.ds(r0, HG_ROWS), :]
            v16 = i_ref[pl.ds(r0, HG_ROWS), :].astype(BF16)
            do16 = do_ref[pl.ds(r0, HG_ROWS), :].astype(BF16)
            qh, sq, s, f, g, kk = _hg_gates(zq, z_ref[pl.ds(r0, HG_ROWS), :], lbv)
            b = _block_cumsum(g, pos, reverse)
            bl = b[last:last + 1, :]
            eb = jnp.exp(b)
            ebl = jnp.exp(bl - b)
            decay = jnp.exp(bl)
            qe16 = (qh * eb).astype(BF16)
            kd16 = (kk * ebl).astype(BF16)
            st = st_ref[bb]
            st16, dst16 = st.astype(BF16), dst.astype(BF16)
            same_row = ri == ci
            da = _nt(do16, v16)
            da_diag = jnp.sum(jnp.where(same_row, da, 0.0), axis=1, keepdims=True)
            dq_state = eb * _nn(do16, st16)
            dk_state = ebl * _nn(v16, dst16)
            dq = dq_state + da_diag * kk
            dk = dk_state + da_diag * qh
            dbl = (decay * jnp.sum(st * dst, axis=0, keepdims=True)
                   + jnp.sum(kk * dk_state, axis=0, keepdims=True))
            db = qh * dq_state - kk * dk_state + jnp.where(pos == last, dbl, 0.0)
            a = jnp.where(same_row, jnp.sum(qh * kk, axis=1, keepdims=True), 0.0)
            for i, (eq, ek) in enumerate(_pair_levels(b, pos, reverse)):
                same = mask_ref[i]
                q16, k16 = (qh * eq).astype(BF16), (kk * ek).astype(BF16)
                a = a + same * _nt(q16, k16)
                da16 = (same * da).astype(BF16)
                gq, gk = _nn(da16, k16), _tn(da16, q16)
                dq = dq + eq * gq
                dk = dk + ek * gk
                db = db + (q16.astype(F32) * gq - k16.astype(F32) * gk)
            dg = _block_cumsum(db, pos, not reverse)
            df = dg / f - dk
            dq_ref[pl.ds(r0, HG_ROWS), :] = dq * (sq + zq * sq * (1.0 - sq))
            dz_ref[pl.ds(r0, HG_ROWS), :] = (df * (1.0 - lbv) * s * (1.0 - s)).astype(BF16)
            dv_ref[pl.ds(r0, HG_ROWS), :] = _nt(kd16, dst16) + _tn(a.astype(BF16), do16)
            return (decay * dst + _tn(do16, qe16),
                    dlb + jnp.sum(df * (1.0 - s), axis=0, keepdims=True))

        _, dlb = lax.fori_loop(0, n_blocks, block,
                               (jnp.zeros((HG_DK, HG_DK), F32), jnp.zeros((1, HG_DK), F32)))
        dlb_ref[...] = dlb

    cspec = lambda col: pl.BlockSpec((lp, HG_DK), lambda h: (0, col // HG_DK + h))
    ospec = pl.BlockSpec((lp, HG_DK), lambda h: (0, h))
    sds = jax.ShapeDtypeStruct((lp, hh * HG_DK), F32)
    return pl.pallas_call(
        body, grid=(hh,),
        in_specs=[cspec(col_q), cspec(col_z), cspec(col_i),
                  pl.BlockSpec((None, 1, HG_DK), lambda h: (h, 0, 0)),
                  pl.BlockSpec((None, n_blocks, HG_DK, HG_DK), lambda h: (h, 0, 0, 0)),
                  ospec],
        out_specs=[ospec, ospec, ospec, pl.BlockSpec((None, 1, HG_DK), lambda h: (h, 0, 0))],
        out_shape=[sds, jax.ShapeDtypeStruct((lp, hh * HG_DK), BF16), sds,
                   jax.ShapeDtypeStruct((hh, 1, HG_DK), F32)],
        scratch_shapes=[pltpu.VMEM((len(HG_HALVES), HG_ROWS, HG_ROWS), F32)],
        compiler_params=_cp("parallel"), name=name)(proj, proj, proj, lb, states, do)


NA_HB = LANES // NA_HEAD_DIM
NA_G = 4
NA_U = NA_G + NA_WIN_H
NA_QN = NA_G * GRID_W
NA_KN = NA_U * GRID_W


def _na_table_index(pattern, a, j):
    if pattern == 0:
        return j - a + NA_WIN_H - 1 if j < NA_WIN_H else None
    if pattern == 2:
        return j - a - 1 if j >= NA_U - NA_WIN_H else None
    return j - a + NA_WIN_H // 2 - 1 if a <= j < a + NA_WIN_H else None


def _na_step_rows(pattern, t, rows):
    if pattern == 0:
        r0, us = 0, 0
    elif pattern == 2:
        r0, us = rows - NA_G, rows - NA_U
    else:
        r0 = NA_G * t
        us = r0 - NA_WIN_H // 2
    q0, k0 = N_META + GRID_W * r0, N_META + GRID_W * us
    if pattern == 1:
        q0, k0 = pl.multiple_of(q0, 16), pl.multiple_of(k0, 16)
    return q0, k0


def _na_fill_bias(tb_ref, bias_ref):
    neg = jnp.full((GRID_W, GRID_W), -1e30, F32)
    for h in range(NA_HB):
        for pattern in range(3):
            for a in range(NA_G):
                for j in range(NA_U):
                    idx = _na_table_index(pattern, a, j)
                    bias_ref[h, pattern, a * GRID_W:(a + 1) * GRID_W, j * GRID_W:(j + 1) * GRID_W] = (
                        neg if idx is None else tb_ref[h, idx])


def _na_steps(rows, step, carry):
    n_steps = rows // NA_G
    carry = step(0, 0, carry)
    carry = lax.fori_loop(1, n_steps - 1, functools.partial(step, 1), carry, unroll=3)
    return step(2, n_steps - 1, carry)


def _na_head_lanes():
    lane = lax.broadcasted_iota(jnp.int32, (1, LANES), 1)
    return [lane // NA_HEAD_DIM == h for h in range(NA_HB)]


def _na_only(mask, x):
    return jnp.where(mask, x, jnp.zeros_like(x))


def _na_stack(heads, x):
    return jnp.concatenate([_na_only(mask, x) for mask in heads], axis=0)


def _na_unstack(heads, y):
    rows = y.shape[0] // NA_HB
    out = y[0:rows]
    for h in range(1, NA_HB):
        out = jnp.where(heads[h], y[h * rows:(h + 1) * rows], out)
    return out


def _na_fwd(proj, tb, *, n_tok, nh, name):
    lp = proj.shape[0]
    dh, hb = NA_HEAD_DIM, NA_HB
    naw = nh * dh
    rows = n_tok // GRID_W
    scale = dh ** -0.5

    def body(q_ref, k_ref, v_ref, tb_ref, o_ref, lse_ref, q16_ref, k16_ref, v16_ref, bias_ref):
        o_ref[...] = jnp.zeros_like(o_ref)
        lse_ref[...] = jnp.zeros_like(lse_ref)
        q16_ref[...] = q_ref[...].astype(BF16)
        k16_ref[...] = k_ref[...].astype(BF16)
        v16_ref[...] = v_ref[...].astype(BF16)
        _na_fill_bias(tb_ref, bias_ref)
        heads = _na_head_lanes()
        km = k16_ref[0:N_META, :]
        vm = v16_ref[0:N_META, :]
        qm = q16_ref[0:N_META, :]
        o_m = None
        for h in range(hb):
            s = _nt(_na_only(heads[h], qm), km) * scale
            m = jnp.max(s, axis=1, keepdims=True)
            p = jnp.exp(s - m)
            l = jnp.sum(p, axis=1, keepdims=True)
            o_h = _nn(p.astype(BF16), vm) / l
            o_m = o_h if o_m is None else jnp.where(heads[h], o_h, o_m)
            lse_ref[h, 0:N_META, :] = m + jnp.log(l)
        o_ref[0:N_META, :] = o_m

        def step(pattern, t, carry):
            q0, k0 = _na_step_rows(pattern, t, rows)
            k16 = k16_ref[pl.ds(k0, NA_KN), :]
            v16 = v16_ref[pl.ds(k0, NA_KN), :]
            q2 = _na_stack(heads, q16_ref[pl.ds(q0, NA_QN), :])
            s = _nt(q2, k16) * scale + bias_ref[:, pattern].reshape(hb * NA_QN, NA_KN)
            sm = _nt(q2, km) * scale
            m = jnp.maximum(jnp.max(s, axis=1, keepdims=True), jnp.max(sm, axis=1, keepdims=True))
            p = jnp.exp(s - m)
            pm = jnp.exp(sm - m)
            l = jnp.sum(p, axis=1, keepdims=True) + jnp.sum(pm, axis=1, keepdims=True)
            o2 = (_nn(p.astype(BF16), v16) + _nn(pm.astype(BF16), vm)) / l
            o_ref[pl.ds(q0, NA_QN), :] = _na_unstack(heads, o2)
            lse2 = m + jnp.log(l)
            for h in range(hb):
                lse_ref[h, pl.ds(q0, NA_QN), :] = lse2[h * NA_QN:(h + 1) * NA_QN]
            return carry

        _na_steps(rows, step, 0)

    cblk = lambda col: pl.BlockSpec((lp, LANES), lambda g: (0, col // LANES + g))
    return pl.pallas_call(
        body, grid=(nh // hb,),
        in_specs=[cblk(0), cblk(naw), cblk(2 * naw),
                  pl.BlockSpec((hb, 2 * NA_WIN_H - 1, GRID_W, GRID_W), lambda g: (g, 0, 0, 0))],
        out_specs=[cblk(0), pl.BlockSpec((hb, lp, 1), lambda g: (g, 0, 0))],
        out_shape=[jax.ShapeDtypeStruct((lp, naw), F32), jax.ShapeDtypeStruct((nh, lp, 1), F32)],
        scratch_shapes=[pltpu.VMEM((lp, LANES), BF16)] * 3 + [pltpu.VMEM((hb, 3, NA_QN, NA_KN), F32)],
        compiler_params=_cp("parallel"), name=name)(proj, proj, proj, tb)


def _na_bwd(proj, tb, o, lse, do, *, n_tok, nh, name):
    lp = proj.shape[0]
    dh, hb = NA_HEAD_DIM, NA_HB
    naw = nh * dh
    rows = n_tok // GRID_W
    scale = dh ** -0.5

    def body(q_ref, k_ref, v_ref, tb_ref, o_ref, lse_ref, do_ref, dq_ref, dk_ref, dv_ref, dtb_ref,
             q16_ref, k16_ref, v16_ref, bias_ref):
        dq_ref[...] = jnp.zeros_like(dq_ref)
        dk_ref[...] = jnp.zeros_like(dk_ref)
        dv_ref[...] = jnp.zeros_like(dv_ref)
        dtb_ref[...] = jnp.zeros_like(dtb_ref)
        q16_ref[...] = q_ref[...].astype(BF16)
        k16_ref[...] = k_ref[...].astype(BF16)
        v16_ref[...] = v_ref[...].astype(BF16)
        _na_fill_bias(tb_ref, bias_ref)
        heads = _na_head_lanes()
        km = k16_ref[0:N_META, :]
        vm = v16_ref[0:N_META, :]
        qm = q16_ref[0:N_META, :]
        dom = do_ref[0:N_META, :]
        prod = dom * o_ref[0:N_META, :]
        dq_m = None
        dkm0 = jnp.zeros((N_META, LANES), F32)
        dvm0 = jnp.zeros((N_META, LANES), F32)
        for h in range(hb):
            q_h = _na_only(heads[h], qm)
            do_h = _na_only(heads[h], dom).astype(BF16)
            p = jnp.exp(_nt(q_h, km) * scale - lse_ref[h, 0:N_META, :])
            delta = jnp.sum(_na_only(heads[h], prod), axis=1, keepdims=True)
            ds = (p * (_nt(do_h, vm) - delta)).astype(BF16)
            dq_h = _nn(ds, km) * scale
            dq_m = dq_h if dq_m is None else jnp.where(heads[h], dq_h, dq_m)
            dkm0 = dkm0 + _tn(ds, q_h) * scale
            dvm0 = dvm0 + _tn(p.astype(BF16), do_h)
        dq_ref[0:N_META, :] = dq_m

        def step(pattern, t, carry):
            dkm, dvm = carry
            q0, k0 = _na_step_rows(pattern, t, rows)
            k16 = k16_ref[pl.ds(k0, NA_KN), :]
            v16 = v16_ref[pl.ds(k0, NA_KN), :]
            q2 = _na_stack(heads, q16_ref[pl.ds(q0, NA_QN), :])
            do2 = _na_stack(heads, do_ref[pl.ds(q0, NA_QN), :])
            do16 = do2.astype(BF16)
            ov = o_ref[pl.ds(q0, NA_QN), :]
            delta = jnp.sum(do2 * jnp.concatenate([ov] * hb, axis=0), axis=1, keepdims=True)
            lse = jnp.concatenate([lse_ref[h, pl.ds(q0, NA_QN), :] for h in range(hb)], axis=0)
            p = jnp.exp(_nt(q2, k16) * scale + bias_ref[:, pattern].reshape(hb * NA_QN, NA_KN)
                        - lse)
            pm = jnp.exp(_nt(q2, km) * scale - lse)
            ds = p * (_nt(do16, v16) - delta)
            dsm = (pm * (_nt(do16, vm) - delta)).astype(BF16)
            ds16 = ds.astype(BF16)
            dq2 = (_nn(ds16, k16) + _nn(dsm, km)) * scale
            dq_ref[pl.ds(q0, NA_QN), :] = _na_unstack(heads, dq2)
            dk_ref[pl.ds(k0, NA_KN), :] += _tn(ds16, q2) * scale
            dv_ref[pl.ds(k0, NA_KN), :] += _tn(p.astype(BF16), do16)
            for h in range(hb):
                for a in range(NA_G):
                    for j in range(NA_U):
                        idx = _na_table_index(pattern, a, j)
                        if idx is not None:
                            r = h * NA_QN + a * GRID_W
                            dtb_ref[h, idx] += ds[r:r + GRID_W, j * GRID_W:(j + 1) * GRID_W]
            return dkm + _tn(dsm, q2) * scale, dvm + _tn(pm.astype(BF16), do16)

        dkm, dvm = _na_steps(rows, step, (dkm0, dvm0))
        dk_ref[0:N_META, :] += dkm
        dv_ref[0:N_META, :] += dvm

    cblk = lambda col: pl.BlockSpec((lp, LANES), lambda g: (0, col // LANES + g))
    tbs = pl.BlockSpec((hb, 2 * NA_WIN_H - 1, GRID_W, GRID_W), lambda g: (g, 0, 0, 0))
    sds = jax.ShapeDtypeStruct((lp, naw), F32)
    return pl.pallas_call(
        body, grid=(nh // hb,),
        in_specs=[cblk(0), cblk(naw), cblk(2 * naw), tbs, cblk(0),
                  pl.BlockSpec((hb, lp, 1), lambda g: (g, 0, 0)), cblk(0)],
        out_specs=[cblk(0), cblk(0), cblk(0), tbs],
        out_shape=[sds, sds, sds, jax.ShapeDtypeStruct(tb.shape, F32)],
        scratch_shapes=[pltpu.VMEM((lp, LANES), BF16)] * 3 + [pltpu.VMEM((hb, 3, NA_QN, NA_KN), F32)],
        compiler_params=_cp("parallel"), name=name)(proj, proj, proj, tb, o, lse, do)


def _rpb_onehot():
    c = np.arange(GRID_W)[:, None]
    w = np.arange(GRID_W)[None, :]
    cs = np.clip(c - NA_WIN_W // 2, 0, GRID_W - NA_WIN_W)
    in_win = (w >= cs) & (w < cs + NA_WIN_W)
    dc = np.clip(w - c, -(NA_WIN_W - 1), NA_WIN_W - 1) + NA_WIN_W - 1
    oh = np.zeros((LANES, GRID_W * GRID_W), np.float32)
    flat = np.arange(GRID_W * GRID_W).reshape(GRID_W, GRID_W)
    oh[dc[in_win], flat[in_win]] = 1.0
    neg = np.where(in_win, 0.0, -1e30).astype(np.float32).reshape(1, -1)
    return oh, neg


def _assemble_dproj(dq_na, dk_na, dv_na, dq_f, dq_b, dz_f, dz_b, dv_f, dv_b, dg, dgn, dgh, *, name):
    lp, naw = dq_na.shape
    hgw = dq_f.shape[1]
    d = dgn.shape[1]
    cols = 3 * naw + 5 * hgw + 2 * d
    tr = _row_tile(lp, 3 * naw * 4 + 4 * hgw * 4 + 3 * hgw * 2 + 2 * d * 2 + cols * 2)

    def body(nq_ref, nk_ref, nv_ref, qf_ref, qb_ref, zf_ref, zb_ref, vf_ref, vb_ref, g_ref, gn_ref,
             gh_ref, o_ref):
        o_ref[:, 0:naw] = nq_ref[...].astype(BF16)
        o_ref[:, naw:2 * naw] = nk_ref[...].astype(BF16)
        o_ref[:, 2 * naw:3 * naw] = nv_ref[...].astype(BF16)
        c0 = 3 * naw
        o_ref[:, c0:c0 + hgw] = (qf_ref[...] + qb_ref[...]).astype(BF16)
        o_ref[:, c0 + hgw:c0 + 2 * hgw] = zf_ref[...]
        o_ref[:, c0 + 2 * hgw:c0 + 3 * hgw] = zb_ref[...]
        o_ref[:, c0 + 3 * hgw:c0 + 4 * hgw] = (vf_ref[...] + vb_ref[...]).astype(BF16)
        o_ref[:, c0 + 4 * hgw:c0 + 5 * hgw] = g_ref[...]
        o_ref[:, c0 + 5 * hgw:c0 + 5 * hgw + d] = gn_ref[...]
        o_ref[:, c0 + 5 * hgw + d:] = gh_ref[...]

    hg, na = _rspec(tr, hgw), _rspec(tr, naw)
    return pl.pallas_call(
        body, grid=(lp // tr,),
        in_specs=[na, na, na, hg, hg, hg, hg, hg, hg, hg, _rspec(tr, d), _rspec(tr, d)],
        out_specs=_rspec(tr, cols),
        out_shape=jax.ShapeDtypeStruct((lp, cols), BF16),
        compiler_params=_cp("parallel"), name=name)(dq_na, dk_na, dv_na, dq_f, dq_b, dz_f, dz_b,
                                                    dv_f, dv_b, dg, dgn, dgh)


GROUP_STEPS = 8


def _group_tiles(rows, align):
    steps = GROUP_STEPS if all(r % (GROUP_STEPS * align) == 0 for r in rows) else 1
    return steps, [r // steps for r in rows]


def _adamw(ws, gs, ms, vs, *, name):
    n = len(ws)
    steps, trs = _group_tiles([w.shape[0] for w in ws], 8)

    def body(*refs):
        for i in range(n):
            w_ref, g_ref, m_ref, v_ref = refs[4 * i:4 * i + 4]
            d_ref, mo_ref, vo_ref, go_ref = refs[4 * n + 4 * i:4 * n + 4 * i + 4]
            gv = g_ref[...]
            go_ref[...] = gv
            mn = ADAM_B1 * m_ref[...] + (1.0 - ADAM_B1) * gv
            vn = ADAM_B2 * v_ref[...] + (1.0 - ADAM_B2) * (gv * gv)
            m_hat = mn / (1.0 - ADAM_B1 ** ADAM_STEP)
            v_hat = vn / (1.0 - ADAM_B2 ** ADAM_STEP)
            d_ref[...] = -ADAM_LR * (m_hat / (jnp.sqrt(v_hat) + ADAM_EPS) + ADAM_WD * w_ref[...])
            mo_ref[...] = mn
            vo_ref[...] = vn

    specs = [_rspec(tr, w.shape[1]) for tr, w in zip(trs, ws)]
    out = pl.pallas_call(
        body, grid=(steps,),
        in_specs=[s for s in specs for _ in range(4)],
        out_specs=[s for s in specs for _ in range(4)],
        out_shape=[jax.ShapeDtypeStruct(w.shape, F32) for w in ws for _ in range(4)],
        compiler_params=_cp("parallel"), name=name)(*[a for q in zip(ws, gs, ms, vs) for a in q])
    return [tuple(out[4 * i:4 * i + 4]) for i in range(n)]


def _local_step(x, tgt, meta, first_weight, rest_weights, g_mix, g_mlp, g_fin, hg_gain, rpb, lb,
                early_grads=None, mid_grads=None, late_grad=None, rest_landed=None,
                last_grads=None):
    n_tok, d = x.shape
    hgw = hg_gain.shape[1]
    nh, hh = rpb.shape[0], hgw // HG_DK
    naw = nh * NA_HEAD_DIM
    l_real = N_META + n_tok
    lp = -(-l_real // ROW_ALIGN) * ROW_ALIGN
    col_qhg = 3 * naw
    col_zf, col_zb, col_i, col_g = (col_qhg + hgw, col_qhg + 2 * hgw, col_qhg + 3 * hgw,
                                    col_qhg + 4 * hgw)
    col_gate = col_qhg + 5 * hgw

    oh_np, neg_np = _rpb_onehot()
    oh = jnp.asarray(oh_np)
    rpb_p = jnp.pad(rpb.reshape(nh * (2 * NA_WIN_H - 1), 2 * NA_WIN_W - 1),
                    ((0, 0), (0, LANES - (2 * NA_WIN_W - 1))))
    tb = _matmul(rpb_p, oh, tm=rpb_p.shape[0], tn=512, tk=LANES, precision=HIGHEST,
                 name="rpb_expand")
    tb = (tb + jnp.asarray(neg_np)).reshape(nh, 2 * NA_WIN_H - 1, GRID_W, GRID_W)

    h0, a, tgt_p = _embed_norm(x, tgt, meta, g_mix, lp=lp, name="norm_mix")
    w_in = first_weight((a, tb))
    proj = _matmul(a, w_in, tm=lp, name="mm_in")
    o_na, lse = _na_fwd(proj, tb, n_tok=n_tok, nh=nh, name="na_fwd")
    lb_f = lb[0].reshape(hh, 1, HG_DK)
    lb_b = lb[1].reshape(hh, 1, HG_DK)
    scan_kw = dict(col_q=col_qhg, col_i=col_i, hh=hh)
    o_f, st_f = _hg_scan_fwd(proj, lb_f, reverse=False, col_z=col_zf, name="hg_scan_f", **scan_kw)
    token = rest_landed(o_f) if rest_landed else None
    lb_b_late = lb_b if token is None else lb_b + token[0:1, 0:1]
    o_b, st_b = _hg_scan_fwd(proj, lb_b_late, reverse=True, col_z=col_zb, name="hg_scan_b",
                             **scan_kw)
    o_hg = _hg_out(o_f, o_b, proj, hg_gain, col_g=col_g, name="hg_out")
    w_na, w_hg, w_o, w_up, w_down = rest_weights(o_hg)
    y_na = _matmul(o_na, w_na, name="mm_na_out", out_dtype=BF16)
    gates = ((proj, col_gate), (proj, col_gate + d))

    def mix_gates(acc, gn, gh, yn):
        return acc, _sigmoid(gn) * yn + _sigmoid(gh) * acc

    def mix_gates_bwd(dmix, gn, gh, yn, yh):
        sn, sh = _sigmoid(gn), _sigmoid(gh)
        return dmix * sn, dmix * sh, dmix * yn * sn * (1.0 - sn), dmix * yh * sh * (1.0 - sh)

    y_hg, mix = _matmul(o_hg, w_hg, name="mm_hg_out", epilogue=mix_gates,
                        tiles=(*gates, (y_na, 0)), out_dtypes=(BF16, BF16))
    t1 = _matmul(mix, w_o, name="mm_o")
    h1, mlp_in = _residual_norm(h0, t1, g_mlp, name="resid_norm_mlp")
    u, act = _matmul(mlp_in, w_up, name="mm_up", out_dtypes=(BF16, BF16),
                     epilogue=lambda acc: (acc, jnp.square(jnp.maximum(acc, 0.0))))
    t2 = _matmul(act, w_down, name="mm_down")
    dh2, dh2_16, loss, dg_fin = _final_loss(h1, t2, g_fin, tgt_p, n_tok=n_tok, name="final_loss")

    (du,) = _matmul(dh2_16, w_down, tb=True, name="mm_down_dx", tiles=((u, 0),),
                    out_dtypes=(BF16,),
                    epilogue=lambda acc, uv: (acc * 2.0 * jnp.maximum(uv, 0.0),))
    dw_down = _matmul(act, dh2_16, ta=True, name="mm_down_dw")
    dm = _matmul(du, w_up, tb=True, name="mm_up_dx")
    dw_up = _matmul(mlp_in, du, ta=True, name="mm_up_dw")
    dh1, dh1_16, dg_mlp = _rmsnorm_bwd_add(h1, g_mlp, dm, dh2, name="norm_mlp_bwd")
    dy_na, dy_hg, dgn, dgh = _matmul(dh1_16, w_o, tb=True, name="mm_o_dx", epilogue=mix_gates_bwd,
                                     tiles=(*gates, (y_na, 0), (y_hg, 0)), out_dtypes=(BF16,) * 4)
    dw_o = _matmul(mix, dh1_16, ta=True, name="mm_o_dw")
    do_na = _matmul(dy_na, w_na, tb=True, name="mm_na_out_dx")
    dw_na = _matmul(o_na, dy_na, ta=True, name="mm_na_out_dw")
    do_hg = _matmul(dy_hg, w_hg, tb=True, name="mm_hg_out_dx")
    dw_hg = _matmul(o_hg, dy_hg, ta=True, name="mm_hg_out_dw")
    token = early_grads([dw_na, dw_hg, dw_o, dw_up, dw_down]) if early_grads else None
    if token is not None:
        hg_gain = hg_gain + token[0:1, 0:1]
    d_o, dg_hg, d_gain = _hg_out_bwd(o_f, o_b, proj, hg_gain, do_hg, col_g=col_g, name="hg_out_bwd")
    dq_f, dz_f, dv_f, dlb_f = _hg_scan_bwd(proj, lb_f, st_f, d_o, reverse=False, col_z=col_zf,
                                           name="hg_scan_f_bwd", **scan_kw)
    token = mid_grads(dq_f) if mid_grads else None
    lb_b_late = lb_b if token is None else lb_b + token[0:1, 0:1]
    dq_b, dz_b, dv_b, dlb_b = _hg_scan_bwd(proj, lb_b_late, st_b, d_o, reverse=True, col_z=col_zb,
                                           name="hg_scan_b_bwd", **scan_kw)
    dq_na, dk_na, dv_na, dtb = _na_bwd(proj, tb, o_na, lse, do_na, n_tok=n_tok, nh=nh, name="na_bwd")
    dproj = _assemble_dproj(dq_na, dk_na, dv_na, dq_f, dq_b, dz_f, dz_b, dv_f, dv_b, dg_hg, dgn,
                            dgh, name="assemble_dproj")
    dw_in = _matmul(a, dproj, ta=True, name="mm_in_dw")
    token = late_grad(dw_in) if late_grad else None
    da = _matmul(dproj, w_in, tb=True, name="mm_in_dx", after=token)
    token = last_grads(da) if last_grads else None
    g_mix_late = g_mix if token is None else g_mix + token[0:1, 0:1]
    dx, dmeta, dg_mix = _rmsnorm_bwd_tokens(h0, g_mix_late, da, dh1, n_tok=n_tok,
                                            name="norm_mix_bwd")
    d_rpb = _matmul(dtb.reshape(nh * (2 * NA_WIN_H - 1), GRID_W * GRID_W), oh, tb=True,
                    tm=nh * (2 * NA_WIN_H - 1), tn=LANES, tk=1024, precision=HIGHEST,
                    name="rpb_reduce")
    d_lb = jnp.concatenate([dlb_f.reshape(1, hgw), dlb_b.reshape(1, hgw)], axis=0)
    return (loss, dx, dmeta, dw_in, dw_na, dw_hg, dw_o, dw_up, dw_down,
            dg_mix, dg_mlp, dg_fin, d_gain, d_rpb, d_lb)


N_CHIPS = 4
N_DEV = 8
ANY = pl.BlockSpec(memory_space=pl.ANY)


def _place():
    x, y, c = lax.axis_index("x"), lax.axis_index("y"), lax.axis_index("c")
    others = []
    for j in (1, 2, 3):
        tx = (1 - x) if (j >> 1) else x
        ty = (1 - y) if (j & 1) else y
        others.append((tx, ty))
    return x, y, c, others


def _piece(ref, axis, k, half, rh, cs):
    if axis == 1:
        return ref.at[pl.ds(pl.multiple_of(half * rh, 16), rh), pl.ds(pl.multiple_of(k * cs, LANES), cs)]
    return ref.at[pl.ds(pl.multiple_of(k * 2 * rh + half * rh, 16), rh), :]


def _cast_into_full(shards, axes, place, *, name):
    n = len(shards)
    steps, trs = _group_tiles([s.shape[0] for s in shards], 16)

    def body(p_ref, *refs):
        for i in range(n):
            refs[n + i][...] = refs[i][...].astype(BF16)

    def out_spec(tr, cs, axis):
        if axis == 1:
            return pl.BlockSpec((tr, cs), lambda i, p_ref: (i, p_ref[0]))
        return pl.BlockSpec((tr, cs), lambda i, p_ref: (p_ref[0] * steps + i, 0))

    return pl.pallas_call(
        body,
        grid_spec=pltpu.PrefetchScalarGridSpec(
            num_scalar_prefetch=1, grid=(steps,),
            in_specs=[pl.BlockSpec((tr, s.shape[1]), lambda i, p_ref: (i, 0))
                      for tr, s in zip(trs, shards)],
            out_specs=[out_spec(tr, s.shape[1], ax) for tr, s, ax in zip(trs, shards, axes)]),
        out_shape=[jax.ShapeDtypeStruct((s.shape[0], s.shape[1] * N_CHIPS) if ax == 1
                                        else (s.shape[0] * N_CHIPS, s.shape[1]), BF16)
                   for s, ax in zip(shards, axes)],
        compiler_params=_cp("parallel"), name=name)(place, *shards)


HBM_SPEC = pl.BlockSpec(memory_space=pltpu.HBM)
SEM_SPEC = pl.BlockSpec(memory_space=pltpu.SEMAPHORE)
SPLIT_COPY = pltpu.CompilerParams(has_side_effects=pltpu.SideEffectType.DATAFLOW_SIDE_EFFECTING)
TOKEN = jax.ShapeDtypeStruct((8, LANES), F32)


def _geo(fulls, axes):
    out = []
    for f, ax in zip(fulls, axes):
        r, cs = (f.shape[0], f.shape[1] // N_CHIPS) if ax == 1 else (f.shape[0] // N_CHIPS, f.shape[1])
        out.append((ax, r // 2, cs))
    return out


def _gather_copies(refs, geo, send_sems, recv_sems):
    x, y, c, others = _place()
    chip = 2 * x + y
    cps = []
    for i, (ax, rh, cs) in enumerate(geo):
        mine = _piece(refs[i], ax, chip, c, rh, cs)
        for j, (tx, ty) in enumerate(others):
            cps.append(pltpu.make_async_remote_copy(
                src_ref=mine, dst_ref=mine, send_sem=send_sems.at[3 * i + j],
                recv_sem=recv_sems.at[3 * i + j], device_id=(tx, ty, c), device_id_type=MESH))
    return cps


def _allgather_start(fulls, axes, after, *, name):
    n = len(fulls)
    geo = _geo(fulls, axes)

    def body(*refs):
        w_refs = refs[:n]
        send_sems, recv_sems = refs[n + 1], refs[n + 2]
        token = refs[2 * n + 3]
        for cp in _gather_copies(w_refs, geo, send_sems, recv_sems):
            cp.start()
        token[...] = jnp.zeros_like(token)

    out = pl.pallas_call(
        body, name=name,
        out_shape=(pltpu.SemaphoreType.DMA((3 * n,)), pltpu.SemaphoreType.DMA((3 * n,)),
                   *[pltpu.HBM(f.shape, f.dtype) for f in fulls], TOKEN),
        in_specs=[HBM_SPEC] * n + [ANY],
        out_specs=(SEM_SPEC, SEM_SPEC, *[HBM_SPEC] * n, pl.BlockSpec(memory_space=pltpu.VMEM)),
        input_output_aliases={i: 2 + i for i in range(n)},
        compiler_params=SPLIT_COPY,
    )(*[pltpu.with_memory_space_constraint(f, pltpu.HBM) for f in fulls], after)
    return out[0], out[1], list(out[2:2 + n]), out[2 + n]


def _allgather_wait(send_sems, recv_sems, fulls, axes, after, *, name):
    n = len(fulls)
    geo = _geo(fulls, axes)
    afters = tuple(after) if isinstance(after, (tuple, list)) else (after,)

    def body(*refs):
        w_refs = refs[:n]
        for cp in _gather_copies(w_refs, geo, refs[n], refs[n + 1]):
            cp.wait_send()
            cp.wait_recv()

    return list(pl.pallas_call(
        body, name=name,
        out_shape=[pltpu.HBM(f.shape, f.dtype) for f in fulls],
        in_specs=[HBM_SPEC] * n + [SEM_SPEC, SEM_SPEC] + [ANY] * len(afters),
        out_specs=[HBM_SPEC] * n,
        input_output_aliases={i: i for i in range(n)},
        compiler_params=SPLIT_COPY,
    )(*fulls, send_sems, recv_sems, *afters))


def _allgather_forward(fulls, axes, *, name):
    n = len(fulls)
    geo = _geo(fulls, axes)

    def body(*refs):
        o_refs = refs[n:2 * n]
        send_sems, recv_sems = refs[2 * n:]
        x, y, c, others = _place()

        def rcopy(i, j, half, to):
            ax, rh, cs = geo[i]
            ref = _piece(o_refs[i], ax, 2 * others[j][0] + others[j][1], half, rh, cs)
            return pltpu.make_async_remote_copy(
                src_ref=ref, dst_ref=ref, send_sem=send_sems.at[3 * i + j],
                recv_sem=recv_sems.at[3 * i + j], device_id=to, device_id_type=MESH)

        cps = [rcopy(i, j, c, (x, y, 1 - c)) for i in range(n) for j in range(3)]
        for cp in cps:
            cp.start()
        for i in range(n):
            for j in range(3):
                rcopy(i, j, 1 - c, (x, y, c)).wait_recv()
        for cp in cps:
            cp.wait_send()

    return list(pl.pallas_call(
        body, in_specs=[ANY] * n, out_specs=[ANY] * n,
        out_shape=[jax.ShapeDtypeStruct(f.shape, f.dtype) for f in fulls],
        input_output_aliases={i: i for i in range(n)},
        scratch_shapes=[pltpu.SemaphoreType.DMA((3 * n,)), pltpu.SemaphoreType.DMA((3 * n,))],
        name=name)(*fulls))


def _forward_copies(refs, geo, send_sems, recv_sems):
    x, y, c, others = _place()
    cps = []
    for i, (ax, rh, cs) in enumerate(geo):
        for j, (tx, ty) in enumerate(others):
            ref = _piece(refs[i], ax, 2 * tx + ty, c, rh, cs)
            cps.append(pltpu.make_async_remote_copy(
                src_ref=ref, dst_ref=ref, send_sem=send_sems.at[3 * i + j],
                recv_sem=recv_sems.at[3 * i + j], device_id=(x, y, 1 - c), device_id_type=MESH))
    return cps


def _allgather_forward_start(fulls, axes, *, name):
    n = len(fulls)
    geo = _geo(fulls, axes)

    def body(*refs):
        token = refs[2 * n + 2]
        for cp in _forward_copies(refs[:n], geo, refs[n], refs[n + 1]):
            cp.start()
        token[...] = jnp.zeros_like(token)

    out = pl.pallas_call(
        body, name=name,
        out_shape=(pltpu.SemaphoreType.DMA((3 * n,)), pltpu.SemaphoreType.DMA((3 * n,)),
                   *[pltpu.HBM(f.shape, f.dtype) for f in fulls], TOKEN),
        in_specs=[HBM_SPEC] * n,
        out_specs=(SEM_SPEC, SEM_SPEC, *[HBM_SPEC] * n, pl.BlockSpec(memory_space=pltpu.VMEM)),
        input_output_aliases={i: 2 + i for i in range(n)},
        compiler_params=SPLIT_COPY,
    )(*fulls)
    return out[0], out[1], list(out[2:2 + n]), out[2 + n]


def _allgather_forward_wait(send_sems, recv_sems, fulls, axes, after, *, name):
    n = len(fulls)
    geo = _geo(fulls, axes)

    def body(*refs):
        for cp in _forward_copies(refs[:n], geo, refs[n], refs[n + 1]):
            cp.wait_send()
            cp.wait_recv()

    return list(pl.pallas_call(
        body, name=name,
        out_shape=[pltpu.HBM(f.shape, f.dtype) for f in fulls],
        in_specs=[HBM_SPEC] * n + [SEM_SPEC, SEM_SPEC, ANY],
        out_specs=[HBM_SPEC] * n,
        input_output_aliases={i: i for i in range(n)},
        compiler_params=SPLIT_COPY,
    )(*fulls, send_sems, recv_sems, after))


def _chip_copies(blk_ref, land_ref, send_sems, recv_sems):
    x, y, c, others = _place()
    return [pltpu.make_async_remote_copy(
        src_ref=blk_ref, dst_ref=land_ref.at[2 * x + y], send_sem=send_sems.at[j],
        recv_sem=recv_sems.at[j], device_id=(tx, ty, c), device_id_type=MESH)
        for j, (tx, ty) in enumerate(others)]


def _chip_exchange_start(blk, *, name):
    land = pltpu.with_memory_space_constraint(lax.empty((N_CHIPS, *blk.shape), blk.dtype), pltpu.HBM)

    def body(blk_ref, land_ref, send_sems, recv_sems, blk_out, land_out, token):
        for cp in _chip_copies(blk_ref, land_ref, send_sems, recv_sems):
            cp.start()
        token[...] = jnp.zeros_like(token)

    return pl.pallas_call(
        body, name=name,
        out_shape=(pltpu.SemaphoreType.DMA((3,)), pltpu.SemaphoreType.DMA((3,)),
                   pltpu.HBM(blk.shape, blk.dtype), pltpu.HBM(land.shape, land.dtype), TOKEN),
        in_specs=[HBM_SPEC] * 2,
        out_specs=(SEM_SPEC, SEM_SPEC, HBM_SPEC, HBM_SPEC, pl.BlockSpec(memory_space=pltpu.VMEM)),
        input_output_aliases={0: 2, 1: 3},
        compiler_params=SPLIT_COPY,
    )(pltpu.with_memory_space_constraint(blk, pltpu.HBM), land)


def _chip_exchange_wait(send_sems, recv_sems, blk, land, after, *, name):
    def body(blk_ref, land_ref, send_sems, recv_sems, after_ref, blk_out, land_out):
        for cp in _chip_copies(blk_ref, land_ref, send_sems, recv_sems):
            cp.wait_send()
            cp.wait_recv()

    return pl.pallas_call(
        body, name=name,
        out_shape=[pltpu.HBM(blk.shape, blk.dtype), pltpu.HBM(land.shape, land.dtype)],
        in_specs=[HBM_SPEC] * 2 + [SEM_SPEC, SEM_SPEC, ANY],
        out_specs=[HBM_SPEC] * 2,
        input_output_aliases={0: 0, 1: 1},
        compiler_params=SPLIT_COPY,
    )(blk, land, send_sems, recv_sems, after)[1]


def _scatter_geo(parts, axes):
    out = []
    for p, ax in zip(parts, axes):
        _, rh, cols = p.shape
        out.append((ax, rh, cols // N_CHIPS if ax == 1 else cols))
    return out


def _scatter_copies(p_refs, q_refs, geo, send_sems, recv_sems):
    x, y, c, others = _place()
    chip = 2 * x + y
    cps = []
    for i, (ax, rh, cw) in enumerate(geo):
        for j, (tx, ty) in enumerate(others):
            k = 2 * tx + ty
            src = (p_refs[i].at[0, :, pl.ds(pl.multiple_of(k * cw, LANES), cw)] if ax == 1
                   else p_refs[i].at[k])
            cps.append(pltpu.make_async_remote_copy(
                src_ref=src, dst_ref=q_refs[i].at[chip], send_sem=send_sems.at[3 * i + j],
                recv_sem=recv_sems.at[3 * i + j], device_id=(tx, ty, c), device_id_type=MESH))
    return cps


def _scatter_start(parts, axes, *, name):
    n = len(parts)
    geo = _scatter_geo(parts, axes)
    slots = [pltpu.HBM((N_CHIPS, rh, cw), p.dtype) for p, (_, rh, cw) in zip(parts, geo)]

    def body(*refs):
        p_refs, q_refs = refs[:n], refs[n:2 * n]
        send_sems, recv_sems = refs[2 * n], refs[2 * n + 1]
        token = refs[4 * n + 2]
        for cp in _scatter_copies(p_refs, q_refs, geo, send_sems, recv_sems):
            cp.start()
        token[...] = jnp.zeros_like(token)

    land = [pltpu.with_memory_space_constraint(lax.empty(s.inner_aval.shape, s.inner_aval.dtype), pltpu.HBM)
            for s in slots]
    out = pl.pallas_call(
        body, name=name,
        out_shape=(pltpu.SemaphoreType.DMA((3 * n,)), pltpu.SemaphoreType.DMA((3 * n,)),
                   *[pltpu.HBM(p.shape, p.dtype) for p in parts], *slots, TOKEN),
        in_specs=[HBM_SPEC] * (2 * n),
        out_specs=(SEM_SPEC, SEM_SPEC, *[HBM_SPEC] * (2 * n), pl.BlockSpec(memory_space=pltpu.VMEM)),
        input_output_aliases={i: 2 + i for i in range(2 * n)},
        compiler_params=SPLIT_COPY,
    )(*[pltpu.with_memory_space_constraint(p, pltpu.HBM) for p in parts], *land)
    return out[0], out[1], list(out[2:2 + n]), list(out[2 + n:2 + 2 * n]), out[2 + 2 * n]


def _scatter_wait(send_sems, recv_sems, parts, slots, axes, after, *, name):
    n = len(parts)
    geo = _scatter_geo(parts, axes)

    def body(*refs):
        p_refs, q_refs = refs[:n], refs[n:2 * n]
        for cp in _scatter_copies(p_refs, q_refs, geo, refs[2 * n], refs[2 * n + 1]):
            cp.wait_send()
            cp.wait_recv()

    out = pl.pallas_call(
        body, name=name,
        out_shape=[pltpu.HBM(a.shape, a.dtype) for a in (*parts, *slots)],
        in_specs=[HBM_SPEC] * (2 * n) + [SEM_SPEC, SEM_SPEC, ANY],
        out_specs=[HBM_SPEC] * (2 * n),
        input_output_aliases={i: i for i in range(2 * n)},
        compiler_params=SPLIT_COPY,
    )(*parts, *slots, send_sems, recv_sems, after)
    return list(out[:n]), list(out[n:])


def _swap_copies(g_refs, r_refs, shapes, send_sems, recv_sems):
    x, y, c, _ = _place()
    cps = []
    for i, shape in enumerate(shapes):
        rh = shape[1] // 2
        src = g_refs[i].at[:, pl.ds(pl.multiple_of((1 - c) * rh, 16), rh), :]
        cps.append(pltpu.make_async_remote_copy(
            src_ref=src, dst_ref=r_refs[i], send_sem=send_sems.at[i], recv_sem=recv_sems.at[i],
            device_id=(x, y, 1 - c), device_id_type=MESH))
    return cps


def _sibling_swap_start(grads, *, name):
    n = len(grads)
    shapes = [g.shape for g in grads]
    lands = [pltpu.HBM((s[0], s[1] // 2, s[2]), g.dtype) for s, g in zip(shapes, grads)]

    def body(*refs):
        g_refs, r_refs = refs[:n], refs[n:2 * n]
        token = refs[4 * n + 2]
        for cp in _swap_copies(g_refs, r_refs, shapes, refs[2 * n], refs[2 * n + 1]):
            cp.start()
        token[...] = jnp.zeros_like(token)

    land = [pltpu.with_memory_space_constraint(lax.empty(s.inner_aval.shape, s.inner_aval.dtype), pltpu.HBM)
            for s in lands]
    out = pl.pallas_call(
        body, name=name,
        out_shape=(pltpu.SemaphoreType.DMA((n,)), pltpu.SemaphoreType.DMA((n,)),
                   *[pltpu.HBM(g.shape, g.dtype) for g in grads], *lands, TOKEN),
        in_specs=[HBM_SPEC] * (2 * n),
        out_specs=(SEM_SPEC, SEM_SPEC, *[HBM_SPEC] * (2 * n), pl.BlockSpec(memory_space=pltpu.VMEM)),
        input_output_aliases={i: 2 + i for i in range(2 * n)},
        compiler_params=SPLIT_COPY,
    )(*[pltpu.with_memory_space_constraint(g, pltpu.HBM) for g in grads], *land)
    return out[0], out[1], list(out[2:2 + n]), list(out[2 + n:2 + 2 * n]), out[2 + 2 * n]


def _sibling_swap_wait(send_sems, recv_sems, grads, lands, after, *, name):
    n = len(grads)
    shapes = [g.shape for g in grads]

    def body(*refs):
        g_refs, r_refs = refs[:n], refs[n:2 * n]
        for cp in _swap_copies(g_refs, r_refs, shapes, refs[2 * n], refs[2 * n + 1]):
            cp.wait_send()
            cp.wait_recv()

    out = pl.pallas_call(
        body, name=name,
        out_shape=[pltpu.HBM(a.shape, a.dtype) for a in (*grads, *lands)],
        in_specs=[HBM_SPEC] * (2 * n) + [SEM_SPEC, SEM_SPEC, ANY],
        out_specs=[HBM_SPEC] * (2 * n),
        input_output_aliases={i: i for i in range(2 * n)},
        compiler_params=SPLIT_COPY,
    )(*grads, *lands, send_sems, recv_sems, after)
    return list(out[:n]), list(out[n:])


def _pair_add(g3s, rxs, place, *, out_dtype, name):
    n = len(g3s)
    steps, trs = _group_tiles([g.shape[1] // 2 for g in g3s], 16)

    def body(p_ref, *refs):
        for i in range(n):
            refs[2 * n + i][...] = (refs[2 * i][...] + refs[2 * i + 1][...]).astype(out_dtype)

    in_specs, out_specs = [], []
    for g, tr in zip(g3s, trs):
        blk = (g.shape[0], tr, g.shape[2])
        in_specs += [pl.BlockSpec(blk, lambda i, p_ref: (0, p_ref[1] * steps + i, 0)),
                     pl.BlockSpec(blk, lambda i, p_ref: (0, i, 0))]
        out_specs.append(pl.BlockSpec(blk, lambda i, p_ref: (0, i, 0)))
    return pl.pallas_call(
        body,
        grid_spec=pltpu.PrefetchScalarGridSpec(
            num_scalar_prefetch=1, grid=(steps,), in_specs=in_specs, out_specs=out_specs),
        out_shape=[jax.ShapeDtypeStruct((g.shape[0], g.shape[1] // 2, g.shape[2]), out_dtype)
                   for g in g3s],
        compiler_params=_cp("parallel"), name=name)(place, *[a for q in zip(g3s, rxs) for a in q])


def _sum_slots(q, *, name):
    ns, rows, cols = q.shape
    tr = next(t for t in (128, 64, 32, 16, 8) if rows % t == 0)

    def body(q_ref, o_ref):
        acc = q_ref[0].astype(F32)
        for k in range(1, ns):
            acc = acc + q_ref[k].astype(F32)
        o_ref[...] = acc

    return pl.pallas_call(
        body, grid=(rows // tr,),
        in_specs=[pl.BlockSpec((ns, tr, cols), lambda i: (0, i, 0))],
        out_specs=_rspec(tr, cols),
        out_shape=jax.ShapeDtypeStruct((rows, cols), F32),
        compiler_params=_cp("parallel"), name=name)(q)


def _sum_chips(qs, ps, place, axes, *, name):
    n = len(qs)
    per = N_CHIPS + 1
    steps, trs = _group_tiles([q.shape[1] for q in qs], 16)

    def body(p_ref, *refs):
        chip = p_ref[0]
        for i in range(n):
            q_refs, own_ref = refs[per * i:per * i + N_CHIPS], refs[per * i + N_CHIPS]
            acc = jnp.where(chip == 0, own_ref[...], q_refs[0][...]).astype(F32)
            for k in range(1, N_CHIPS):
                acc = acc + jnp.where(chip == k, own_ref[...], q_refs[k][...]).astype(F32)
            refs[per * n + i][...] = acc

    def slot_spec(k, tr, cw):
        return pl.BlockSpec((None, tr, cw),
                            lambda i, p_ref: (jnp.where(p_ref[0] == k, (k + 1) % N_CHIPS, k), i, 0))

    in_specs, out_specs, operands = [], [], []
    for q, p, ax, tr in zip(qs, ps, axes, trs):
        cw = q.shape[2]
        in_specs += [slot_spec(k, tr, cw) for k in range(N_CHIPS)]
        in_specs.append(pl.BlockSpec((None, tr, cw), (lambda i, p_ref: (0, i, p_ref[0])) if ax == 1
                                     else (lambda i, p_ref: (p_ref[0], i, 0))))
        out_specs.append(pl.BlockSpec((tr, cw), lambda i, p_ref: (p_ref[1] * steps + i, 0)))
        operands += [q] * N_CHIPS + [p]
    return pl.pallas_call(
        body,
        grid_spec=pltpu.PrefetchScalarGridSpec(
            num_scalar_prefetch=1, grid=(steps,), in_specs=in_specs, out_specs=out_specs),
        out_shape=[jax.ShapeDtypeStruct((2 * q.shape[1], q.shape[2]), F32) for q in qs],
        compiler_params=_cp("parallel"), name=name)(place, *operands)


def _sibling_share(shards, *, name):
    n = len(shards)

    def body(*refs):
        o_refs = refs[n:2 * n]
        send_sems, recv_sems = refs[2 * n:]
        x, y, c, _ = _place()
        cps = []
        for i in range(n):
            rh = shards[i].shape[0] // 2
            mine = o_refs[i].at[pl.ds(pl.multiple_of(c * rh, 8), rh), :]
            cp = pltpu.make_async_remote_copy(
                src_ref=mine, dst_ref=mine, send_sem=send_sems.at[i], recv_sem=recv_sems.at[i],
                device_id=(x, y, 1 - c), device_id_type=MESH)
            cp.start()
            cps.append(cp)
        for i in range(n):
            rh = shards[i].shape[0] // 2
            theirs = o_refs[i].at[pl.ds(pl.multiple_of((1 - c) * rh, 8), rh), :]
            pltpu.make_async_remote_copy(
                src_ref=theirs, dst_ref=theirs, send_sem=send_sems.at[i], recv_sem=recv_sems.at[i],
                device_id=(x, y, c), device_id_type=MESH).wait_recv()
        for cp in cps:
            cp.wait_send()

    return pl.pallas_call(
        body, in_specs=[ANY] * n, out_specs=[ANY] * n,
        out_shape=[jax.ShapeDtypeStruct(h.shape, h.dtype) for h in shards],
        input_output_aliases={i: i for i in range(n)},
        scratch_shapes=[pltpu.SemaphoreType.DMA((n,)), pltpu.SemaphoreType.DMA((n,))],
        name=name)(*shards)


def _gather_all(blk, *, name, after=None, shares=()):
    rows, cols = blk.shape
    extra = [] if after is None else [after]
    n_s = len(shares)

    def body(x_ref, *refs):
        s_refs = refs[len(extra) + n_s + 1:len(extra) + 2 * n_s + 1]
        out_ref = refs[len(extra) + n_s]
        send_sems, recv_sems, local_sem, s_send, s_recv = refs[len(extra) + 2 * n_s + 1:]
        x, y, c = lax.axis_index("x"), lax.axis_index("y"), lax.axis_index("c")
        me = 4 * x + 2 * y + c
        mine = pltpu.make_async_copy(x_ref, out_ref.at[me], local_sem)
        mine.start()
        cps = []
        for i in range(n_s):
            rh = shares[i].shape[0] // 2
            half = s_refs[i].at[pl.ds(pl.multiple_of(c * rh, 8), rh), :]
            cp = pltpu.make_async_remote_copy(
                src_ref=half, dst_ref=half, send_sem=s_send.at[i], recv_sem=s_recv.at[i],
                device_id=(x, y, 1 - c), device_id_type=MESH)
            cp.start()
            cps.append(cp)
        for k in range(1, N_DEV):
            tx = (1 - x) if (k >> 2) & 1 else x
            ty = (1 - y) if (k >> 1) & 1 else y
            tc = (1 - c) if k & 1 else c
            cp = pltpu.make_async_remote_copy(
                src_ref=x_ref, dst_ref=out_ref.at[me], send_sem=send_sems.at[k - 1],
                recv_sem=recv_sems.at[k - 1], device_id=(tx, ty, tc), device_id_type=MESH)
            cp.start()
            cps.append(cp)
        for k in range(1, N_DEV):
            tx = (1 - x) if (k >> 2) & 1 else x
            ty = (1 - y) if (k >> 1) & 1 else y
            tc = (1 - c) if k & 1 else c
            got = out_ref.at[4 * tx + 2 * ty + tc]
            pltpu.make_async_remote_copy(
                src_ref=got, dst_ref=got, send_sem=send_sems.at[k - 1], recv_sem=recv_sems.at[k - 1],
                device_id=(x, y, c), device_id_type=MESH).wait_recv()
        for i in range(n_s):
            rh = shares[i].shape[0] // 2
            theirs = s_refs[i].at[pl.ds(pl.multiple_of((1 - c) * rh, 8), rh), :]
            pltpu.make_async_remote_copy(
                src_ref=theirs, dst_ref=theirs, send_sem=s_send.at[i], recv_sem=s_recv.at[i],
                device_id=(x, y, c), device_id_type=MESH).wait_recv()
        for cp in cps:
            cp.wait_send()
        mine.wait()

    vm = pl.BlockSpec(memory_space=pltpu.VMEM)
    out = pl.pallas_call(
        body, in_specs=[vm] + [ANY] * (len(extra) + n_s), out_specs=[vm] + [ANY] * n_s,
        out_shape=[jax.ShapeDtypeStruct((N_DEV, rows, cols), blk.dtype)]
        + [jax.ShapeDtypeStruct(s.shape, s.dtype) for s in shares],
        input_output_aliases={1 + len(extra) + i: 1 + i for i in range(n_s)},
        scratch_shapes=[pltpu.SemaphoreType.DMA((N_DEV - 1,)), pltpu.SemaphoreType.DMA((N_DEV - 1,)),
                        pltpu.SemaphoreType.DMA, pltpu.SemaphoreType.DMA((max(n_s, 1),)),
                        pltpu.SemaphoreType.DMA((max(n_s, 1),))],
        name=name)(blk, *extra, *shares)
    return (out[0], *out[1:]) if n_s else out[0]


def _as_rows(a):
    flat = a.reshape(-1)
    n = flat.shape[0]
    rows = -(-n // (8 * LANES)) * 8
    return jnp.pad(flat, (0, rows * LANES - n)).reshape(rows, LANES)


def _from_rows(p, shape):
    n = int(np.prod(shape))
    return p.reshape(-1)[:n].reshape(shape)


WEIGHT_AXES = (1, 1, 1, 0, 1, 0)
WIRE = BF16


def kernel(x, meta_tokens, w_in, w_na_out, w_hg_out, w_o, w_up, w_down, norm_mix, norm_mlp, norm_final, hg_norm, na_rpb, hg_lb_logits, loss_target, m_meta_tokens, m_w_in, m_w_na_out, m_w_hg_out, m_w_o, m_w_up, m_w_down, m_norm_mix, m_norm_mlp, m_norm_final, m_hg_norm, m_na_rpb, m_hg_lb_logits, v_meta_tokens, v_w_in, v_w_na_out, v_w_hg_out, v_w_o, v_w_up, v_w_down, v_norm_mix, v_norm_mlp, v_norm_final, v_hg_norm, v_na_rpb, v_hg_lb_logits):
    xi, yi, ci = lax.axis_index("x"), lax.axis_index("y"), lax.axis_index("c")
    chip = 2 * xi + yi
    d = x.shape[-1]
    dshard = meta_tokens.shape[1]
    hgw = hg_norm.shape[1]
    lbs = hg_lb_logits.shape[2]
    big = [w_in[0], w_na_out[0], w_hg_out[0], w_o[0], w_up[0], w_down[0]]
    big_m = [m_w_in[0], m_w_na_out[0], m_w_hg_out[0], m_w_o[0], m_w_up[0], m_w_down[0]]
    big_v = [v_w_in[0], v_w_na_out[0], v_w_hg_out[0], v_w_o[0], v_w_up[0], v_w_down[0]]

    place = jnp.stack([chip, ci]).astype(jnp.int32)
    in_axes, rest_axes = WEIGHT_AXES[:1], WEIGHT_AXES[1:]
    own_w = _cast_into_full(big, WEIGHT_AXES, place, name="cast_shards")
    small_in = jnp.concatenate([_as_rows(meta_tokens), _as_rows(hg_lb_logits)], axis=0)
    sm_send, sm_recv, sm_blk, sm_land, sm_token = _chip_exchange_start(small_in,
                                                                       name="small_params_start")
    in_send, in_recv, in_bufs, in_token = _allgather_start(own_w[:1], in_axes, sm_token,
                                                           name="weight_allgather_in_start")
    ag_send, ag_recv, ag_bufs, ag_token = _allgather_start(own_w[1:], rest_axes, in_token,
                                                           name="weight_allgather_rest_start")
    sm_land = _chip_exchange_wait(sm_send, sm_recv, sm_blk, sm_land, ag_token,
                                  name="small_params_wait")
    small_all = lax.dynamic_update_slice(sm_land, small_in[None], (chip, 0, 0))
    forward = {}

    def first_weight(after):
        got = _allgather_wait(in_send, in_recv, in_bufs, in_axes, after,
                              name="weight_allgather_in_wait")
        return _allgather_forward(got, in_axes, name="weight_allgather_in_forward")[0]

    def rest_landed(after):
        got = _allgather_wait(ag_send, ag_recv, ag_bufs, rest_axes, after,
                              name="weight_allgather_rest_wait")
        send, recv, bufs, token = _allgather_forward_start(
            got, rest_axes, name="weight_allgather_rest_forward_start")
        forward["rest"] = (send, recv, bufs)
        return token

    def rest_weights(after):
        return _allgather_forward_wait(*forward["rest"], rest_axes, after,
                                       name="weight_allgather_rest_forward_wait")

    n_meta_rows = N_META * dshard // LANES
    meta_full = (small_all[:, :n_meta_rows].reshape(N_CHIPS, N_META, dshard)
                 .transpose(1, 0, 2).reshape(N_META, d))
    lbl_full = (small_all[:, n_meta_rows:].reshape(N_CHIPS, -1)[:, :4 * lbs]
                .reshape(N_CHIPS, 2, 2, lbs).transpose(1, 2, 0, 3).reshape(2, 2, N_CHIPS * lbs))
    lb = jax.nn.softmax(lbl_full, axis=1)[:, 0]

    def by_chip(dws, axes):
        return [g.reshape(1, *g.shape) if ax == 1
                else g.reshape(N_CHIPS, g.shape[0] // N_CHIPS, g.shape[1]) for g, ax in zip(dws, axes)]

    flying = {}

    def scatter(tag, axes, g3, rx):
        parts = _pair_add(g3, rx, place, out_dtype=WIRE, name=f"grad_pair_add_{tag}")
        send, recv, parts, slots, token = _scatter_start(parts, axes,
                                                         name=f"grad_scatter_{tag}_start")
        flying[tag] = (send, recv, parts, slots)
        return token

    def swap(tag, axes):
        def start(dws):
            send, recv, g3, lands, token = _sibling_swap_start(
                by_chip(dws, axes), name=f"grad_sibling_swap_{tag}_start")
            flying["swap_" + tag] = (send, recv, g3, lands)
            return token

        def finish(after):
            g3, rx = _sibling_swap_wait(*flying["swap_" + tag], after,
                                        name=f"grad_sibling_swap_{tag}_wait")
            return scatter(tag, axes, g3, rx)
        return start, finish

    swap_rest, scatter_rest = swap("rest", rest_axes)
    swap_in, scatter_in = swap("in", in_axes)

    def landed(tag, axes, after):
        return _scatter_wait(*flying[tag], axes, after, name=f"grad_scatter_{tag}_wait")

    (loss, dx, dmeta, *_, dg_mix, dg_mlp, dg_fin, d_gain, d_rpb, d_lb) = _local_step(
        x[0], loss_target[0], meta_full, first_weight, rest_weights, norm_mix, norm_mlp,
        norm_final.reshape(1, d), hg_norm, na_rpb[0], lb, swap_rest, scatter_rest,
        lambda dw_in: swap_in([dw_in]), rest_landed, scatter_in)

    parts_rest, slots_rest = landed("rest", rest_axes, dx)
    g_rest = _sibling_share(_sum_chips(slots_rest, parts_rest, place, rest_axes,
                                       name="grad_sum_chips_rest"),
                            name="grad_sibling_share_rest")
    out_rest = _adamw(big[1:], g_rest, big_m[1:], big_v[1:], name="adamw_rest")
    parts_in, slots_in = landed("in", in_axes, out_rest[-1][0])
    half_in = _sum_chips(slots_in, parts_in, place, in_axes, name="grad_sum_chips_in")

    d_rpb_c = d_rpb[:, :2 * NA_WIN_W - 1]
    small_g = [dmeta, dg_mix, dg_mlp, dg_fin, d_gain, d_rpb_c, d_lb, loss]
    packed = jnp.concatenate([_as_rows(a) for a in small_g], axis=0)
    gathered, g_in = _gather_all(packed, shares=half_in, name="gather_small_grads")
    total = _sum_slots(gathered, name="sum_small_grads")
    offs = np.cumsum([0] + [_as_rows(a).shape[0] for a in small_g])
    take = lambda i, shape: _from_rows(total[offs[i]:offs[i + 1]], shape)
    g_meta_full = take(0, (N_META, d))
    g_norm_mix, g_norm_mlp = take(1, (1, d)), take(2, (1, d))
    g_norm_final = take(3, (d,))
    g_hg_norm = take(4, (1, hgw))
    g_rpb = take(5, na_rpb.shape)
    g_lb = take(6, (2, hgw))
    loss_total = take(7, (1, LANES))[0, 0]
    g_meta = lax.dynamic_slice_in_dim(g_meta_full, chip * dshard, dshard, axis=1)
    dl0 = lb * (1.0 - lb) * g_lb
    g_lbl_full = jnp.stack([dl0, -dl0], axis=1)
    g_lbl = lax.dynamic_slice_in_dim(g_lbl_full, chip * lbs, lbs, axis=2)

    big_out = _adamw(big[:1], [g_in], big_m[:1], big_v[:1], name="adamw_in") + out_rest
    small_w = [meta_tokens, norm_mix, norm_mlp, norm_final, hg_norm, na_rpb, hg_lb_logits]
    small_gr = [g_meta, g_norm_mix, g_norm_mlp, g_norm_final, g_hg_norm, g_rpb, g_lbl]
    small_m = [m_meta_tokens, m_norm_mix, m_norm_mlp, m_norm_final, m_hg_norm, m_na_rpb, m_hg_lb_logits]
    small_v = [v_meta_tokens, v_norm_mix, v_norm_mlp, v_norm_final, v_hg_norm, v_na_rpb, v_hg_lb_logits]
    pk = lambda lst: jnp.concatenate([_as_rows(a) for a in lst], axis=0)
    ((sd, sm, sv, _),) = _adamw([pk(small_w)], [pk(small_gr)], [pk(small_m)], [pk(small_v)],
                             name="adamw_small")
    soffs = np.cumsum([0] + [_as_rows(a).shape[0] for a in small_w])
    unpk = lambda p: [_from_rows(p[soffs[i]:soffs[i + 1]], small_w[i].shape) for i in range(len(small_w))]
    sd, sm, sv = unpk(sd), unpk(sm), unpk(sv)

    def order(bigs, smalls):
        return [smalls[0]] + [b.reshape(1, *b.shape) for b in bigs] + smalls[1:]

    grads = order([o[3] for o in big_out], small_gr)
    deltas = order([o[0] for o in big_out], sd)
    new_m = order([o[1] for o in big_out], sm)
    new_v = order([o[2] for o in big_out], sv)
    return (loss_total, dx.reshape(1, *dx.shape), *grads, *deltas, *new_m, *new_v)
```

```python
import functools

import numpy as np
import jax
import jax.numpy as jnp
from jax import lax
from jax.experimental import pallas as pl
from jax.experimental.pallas import tpu as pltpu

F32 = jnp.float32
BF16 = jnp.bfloat16
HIGHEST = lax.Precision.HIGHEST

GRID_W = 64
N_META = 16
EPS = 1e-6
NA_HEAD_DIM = 64
NA_WIN_H = 8
NA_WIN_W = 16
HG_DK = 128
LANES = 128
ROW_ALIGN = 128
VMEM_LIMIT = 48 * 1024 * 1024
ADAM_LR = 0.001
ADAM_B1 = 0.9
ADAM_B2 = 0.999
ADAM_EPS = 1e-08
ADAM_WD = 0.01
ADAM_STEP = 10

MESH = pl.DeviceIdType.MESH


def _cp(*sem):
    return pltpu.CompilerParams(dimension_semantics=sem, vmem_limit_bytes=VMEM_LIMIT)


def _sigmoid(x):
    return 0.5 * jnp.tanh(0.5 * x) + 0.5


def _dot(a, b, dims, precision=None):
    return lax.dot_general(a, b, (dims, ((), ())), preferred_element_type=F32, precision=precision)


def _nn(a, b, **kw):
    return _dot(a, b, ((1,), (0,)), **kw)


def _nt(a, b, **kw):
    return _dot(a, b, ((1,), (1,)), **kw)


def _tn(a, b, **kw):
    return _dot(a, b, ((0,), (0,)), **kw)


def _matmul(a, b, *, ta=False, tb=False, tm=None, tn=None, tk=None, out_dtype=F32, name,
            precision=None, after=None, epilogue=None, tiles=(), out_dtypes=None):
    extra = [] if after is None else [after]
    single = out_dtypes is None
    if single:
        out_dtypes = (out_dtype,)
    n_t, n_o = len(tiles), len(out_dtypes)
    if ta:
        kdim, m = a.shape
    else:
        m, kdim = a.shape
    if tb:
        n, k2 = b.shape
    else:
        k2, n = b.shape
    assert kdim == k2, (a.shape, b.shape, ta, tb)
    if tm is None:
        if ta:
            tm = next(t for t in (1024, 512, 256, 128, m) if m % t == 0)
        else:
            tm = m // 2 if (m // 2) % 16 == 0 and m > 512 else m
    if tn is None:
        wide = (1024,) if not ta and len(tiles) <= 1 else ()
        tn = next(t for t in (*wide, 512, 256, 128, n) if n % t == 0)
    if tk is None:
        tk = kdim if ta else next(t for t in (2048, 1024, 512, 256, 128, kdim) if kdim % t == 0)
    assert m % tm == 0 and n % tn == 0 and kdim % tk == 0, (m, n, kdim, tm, tn, tk)
    nk = kdim // tk
    op_dtype = F32 if precision is not None else BF16

    def body(a_ref, b_ref, *refs):
        t_refs = refs[:n_t]
        o_refs = refs[n_t + len(extra):n_t + len(extra) + n_o]
        av = a_ref[...].astype(op_dtype)
        bv = b_ref[...].astype(op_dtype)
        dims = ((0 if ta else 1,), (1 if tb else 0,))
        part = _dot(av, bv, dims, precision=precision)

        def finish(acc):
            outs = (acc,) if epilogue is None else epilogue(acc, *[t[...] for t in t_refs])
            for o_ref, val in zip(o_refs, outs):
                o_ref[...] = val.astype(o_ref.dtype)

        if nk == 1:
            finish(part)
            return
        acc_ref = refs[-1]
        kk = pl.program_id(2)

        @pl.when(kk == 0)
        def _():
            acc_ref[...] = part

        @pl.when((kk > 0) & (kk < nk - 1))
        def _():
            acc_ref[...] += part

        @pl.when(kk == nk - 1)
        def _():
            finish(acc_ref[...] + part)

    a_spec = (pl.BlockSpec((tk, tm), lambda i, j, k: (k, i)) if ta
              else pl.BlockSpec((tm, tk), lambda i, j, k: (i, k)))
    b_spec = (pl.BlockSpec((tn, tk), lambda i, j, k: (j, k)) if tb
              else pl.BlockSpec((tk, tn), lambda i, j, k: (k, j)))
    for _, off in tiles:
        assert off % tn == 0, (off, tn)
    t_specs = [pl.BlockSpec((tm, tn), functools.partial(lambda i, j, k, o: (i, o + j), o=off // tn))
               for _, off in tiles]
    o_spec = pl.BlockSpec((tm, tn), lambda i, j, k: (i, j))
    outs = pl.pallas_call(
        body,
        grid=(m // tm, n // tn, nk),
        in_specs=[a_spec, b_spec] + t_specs + [pl.BlockSpec(memory_space=pl.ANY)] * len(extra),
        out_specs=[o_spec] * n_o,
        out_shape=[jax.ShapeDtypeStruct((m, n), dt) for dt in out_dtypes],
        scratch_shapes=[pltpu.VMEM((tm, tn), F32)] if nk > 1 else [],
        compiler_params=_cp("parallel", "parallel", "arbitrary"),
        name=name,
    )(a, b, *[t for t, _ in tiles], *extra)
    return outs[0] if single else outs


def _matmul_windows(a, b, place, *, total, into=None, name):
    m, kdim = a.shape
    cs = total // N_CHIPS
    tn = next(t for t in (640, 512, 256, 128, cs) if cs % t == 0)
    nb = cs // tn
    own = into is None

    def body(p_ref, a_ref, b_ref, *refs):
        o_ref = refs[-1]
        o_ref[...] = _dot(a_ref[...].astype(BF16), b_ref[...].astype(BF16), ((1,), (0,)),
                          precision=None)

    def col(j, p_ref):
        return j + jnp.where(j >= p_ref[0] * nb, nb, 0)

    b_spec = (pl.BlockSpec((kdim, tn), lambda j, p_ref: (0, j)) if own
              else pl.BlockSpec((kdim, tn), lambda j, p_ref: (0, col(j, p_ref))))
    o_spec = (pl.BlockSpec((m, tn), lambda j, p_ref: (0, p_ref[0] * nb + j)) if own
              else pl.BlockSpec((m, tn), lambda j, p_ref: (0, col(j, p_ref))))
    return pl.pallas_call(
        body,
        grid_spec=pltpu.PrefetchScalarGridSpec(
            num_scalar_prefetch=1, grid=(nb if own else (N_CHIPS - 1) * nb,),
            in_specs=[pl.BlockSpec((m, kdim), lambda j, p_ref: (0, 0)), b_spec]
            + ([] if own else [pl.BlockSpec(memory_space=pl.ANY)]),
            out_specs=o_spec),
        out_shape=jax.ShapeDtypeStruct((m, total), F32),
        input_output_aliases={} if own else {3: 0},
        compiler_params=_cp("parallel"), name=name,
    )(place, a, b, *([] if own else [into]))


def _rspec(tr, w, cb=0):
    return pl.BlockSpec((tr, w), lambda i: (i, cb))


def _fspec(shape):
    nd = len(shape)
    return pl.BlockSpec(shape, lambda i: (0,) * nd)


ROW_VMEM_BUDGET = 20 * 1024 * 1024
ROW_MIN_STEPS = 4


def _row_tile(lp, row_bytes):
    for k in range(ROW_MIN_STEPS, lp // 16 + 1):
        tr = lp // k
        if lp % k == 0 and tr % 16 == 0 and 2 * tr * row_bytes <= ROW_VMEM_BUDGET:
            return tr
    return lp


def _token_rows_copy(i, n_tiles, tr, n_tok, tok_ref, buf_ref, sem, *, to_tokens, start=True,
                     wait=True):
    assert n_tiles >= 2 and 0 < n_tok + N_META - (n_tiles - 1) * tr <= tr

    def run(tok_row, buf_row, count):
        tok = tok_ref.at[pl.ds(tok_row, count), :]
        buf = buf_ref.at[pl.ds(buf_row, count), :]
        cp = pltpu.make_async_copy(buf, tok, sem) if to_tokens else pltpu.make_async_copy(tok, buf, sem)
        if start:
            cp.start()
        if wait:
            cp.wait()

    @pl.when(i == 0)
    def _():
        run(0, N_META, tr - N_META)

    if n_tiles > 2:
        @pl.when((i > 0) & (i < n_tiles - 1))
        def _():
            run(pl.multiple_of(i * tr - N_META, 8), 0, tr)

    @pl.when(i == n_tiles - 1)
    def _():
        run((n_tiles - 1) * tr - N_META, 0, n_tok + N_META - (n_tiles - 1) * tr)


def _embed_norm(x, tgt, meta, g, *, lp, name):
    n_tok, d = x.shape
    tr = _row_tile(lp, d * (4 + 2 + 4))
    n_tiles = lp // tr

    def body(x_ref, tgt_ref, meta_ref, g_ref, h_ref, o_ref, tp_ref, buf_ref, tbuf_ref, sems):
        i = pl.program_id(0)
        buf_ref[...] = jnp.zeros_like(buf_ref)
        tbuf_ref[...] = jnp.zeros_like(tbuf_ref)

        @pl.when(i == 0)
        def _():
            buf_ref[0:N_META, :] = meta_ref[...]

        _token_rows_copy(i, n_tiles, tr, n_tok, tgt_ref, tbuf_ref, sems.at[1], to_tokens=False,
                         wait=False)
        _token_rows_copy(i, n_tiles, tr, n_tok, x_ref, buf_ref, sems.at[0], to_tokens=False)
        xv = buf_ref[...]
        h_ref[...] = xv
        r = lax.rsqrt(jnp.mean(xv * xv, axis=-1, keepdims=True) + EPS)
        o_ref[...] = (xv * r * g_ref[...]).astype(BF16)
        _token_rows_copy(i, n_tiles, tr, n_tok, tgt_ref, tbuf_ref, sems.at[1], to_tokens=False,
                         start=False)
        tp_ref[...] = tbuf_ref[...]

    return pl.pallas_call(
        body, grid=(n_tiles,),
        in_specs=[ANY, ANY, _fspec((N_META, d)), _fspec((1, d))],
        out_specs=[_rspec(tr, d), _rspec(tr, d), _rspec(tr, d)],
        out_shape=[jax.ShapeDtypeStruct((lp, d), F32), jax.ShapeDtypeStruct((lp, d), BF16),
                   jax.ShapeDtypeStruct((lp, d), F32)],
        scratch_shapes=[pltpu.VMEM((tr, d), F32), pltpu.VMEM((tr, d), F32),
                        pltpu.SemaphoreType.DMA((2,))],
        compiler_params=_cp("parallel"), name=name)(x, tgt, meta, g)


def _residual_norm(h, t, g, *, name):
    lp, d = h.shape
    tr = _row_tile(lp, d * (4 + 4 + 4 + 2))

    def body(h_ref, t_ref, g_ref, h1_ref, m_ref):
        xv = h_ref[...] + t_ref[...]
        h1_ref[...] = xv
        r = lax.rsqrt(jnp.mean(xv * xv, axis=-1, keepdims=True) + EPS)
        m_ref[...] = (xv * r * g_ref[...]).astype(BF16)

    return pl.pallas_call(
        body, grid=(lp // tr,),
        in_specs=[_rspec(tr, d), _rspec(tr, d), _fspec((1, d))],
        out_specs=[_rspec(tr, d), _rspec(tr, d)],
        out_shape=[jax.ShapeDtypeStruct((lp, d), F32), jax.ShapeDtypeStruct((lp, d), BF16)],
        compiler_params=_cp("parallel"), name=name)(h, t, g)


def _rmsnorm_bwd_add(x, g, dy, dres, *, name):
    lp, d = x.shape
    tr = _row_tile(lp, d * (4 * 4 + 2))

    def body(x_ref, g_ref, dy_ref, dr_ref, dx_ref, dx16_ref, dg_ref):
        @pl.when(pl.program_id(0) == 0)
        def _():
            dg_ref[...] = jnp.zeros_like(dg_ref)

        xv = x_ref[...]
        r = lax.rsqrt(jnp.mean(xv * xv, axis=-1, keepdims=True) + EPS)
        xh = xv * r
        dyv = dy_ref[...]
        dg_ref[...] += jnp.sum(dyv * xh, axis=0, keepdims=True)
        dxh = dyv * g_ref[...]
        dx = dr_ref[...] + r * (dxh - xh * jnp.mean(dxh * xh, axis=-1, keepdims=True))
        dx_ref[...] = dx
        dx16_ref[...] = dx.astype(BF16)

    return pl.pallas_call(
        body, grid=(lp // tr,),
        in_specs=[_rspec(tr, d), _fspec((1, d)), _rspec(tr, d), _rspec(tr, d)],
        out_specs=[_rspec(tr, d), _rspec(tr, d), _fspec((1, d))],
        out_shape=[jax.ShapeDtypeStruct((lp, d), F32), jax.ShapeDtypeStruct((lp, d), BF16),
                   jax.ShapeDtypeStruct((1, d), F32)],
        compiler_params=_cp("arbitrary"), name=name)(x, g, dy, dres)


def _rmsnorm_bwd_tokens(x, g, dy, dres, *, n_tok, name):
    lp, d = x.shape
    tr = _row_tile(lp, d * 4 * 4)
    n_tiles = lp // tr

    def body(x_ref, g_ref, dy_ref, dr_ref, dtok_ref, dmeta_ref, dg_ref, buf_ref, sem):
        i = pl.program_id(0)

        @pl.when(i == 0)
        def _():
            dg_ref[...] = jnp.zeros_like(dg_ref)

        xv = x_ref[...]
        r = lax.rsqrt(jnp.mean(xv * xv, axis=-1, keepdims=True) + EPS)
        xh = xv * r
        dyv = dy_ref[...]
        dg_ref[...] += jnp.sum(dyv * xh, axis=0, keepdims=True)
        dxh = dyv * g_ref[...]
        buf_ref[...] = dr_ref[...] + r * (dxh - xh * jnp.mean(dxh * xh, axis=-1, keepdims=True))

        @pl.when(i == 0)
        def _():
            dmeta_ref[...] = buf_ref[0:N_META, :]

        _token_rows_copy(i, n_tiles, tr, n_tok, dtok_ref, buf_ref, sem, to_tokens=True)

    return pl.pallas_call(
        body, grid=(n_tiles,),
        in_specs=[_rspec(tr, d), _fspec((1, d)), _rspec(tr, d), _rspec(tr, d)],
        out_specs=[ANY, _fspec((N_META, d)), _fspec((1, d))],
        out_shape=[jax.ShapeDtypeStruct((n_tok, d), F32), jax.ShapeDtypeStruct((N_META, d), F32),
                   jax.ShapeDtypeStruct((1, d), F32)],
        scratch_shapes=[pltpu.VMEM((tr, d), F32), pltpu.SemaphoreType.DMA],
        compiler_params=_cp("arbitrary"), name=name)(x, g, dy, dres)


def _final_loss(h1, t2, g, tgt, *, n_tok, name):
    lp, d = h1.shape
    tr = _row_tile(lp, d * (4 * 4 + 2))
    n_tiles = lp // tr

    def body(h_ref, t_ref, g_ref, tg_ref, dh_ref, dh16_ref, loss_ref, dg_ref):
        i = pl.program_id(0)

        @pl.when(i == 0)
        def _():
            loss_ref[...] = jnp.zeros_like(loss_ref)
            dg_ref[...] = jnp.zeros_like(dg_ref)

        xv = h_ref[...] + t_ref[...]
        r = lax.rsqrt(jnp.mean(xv * xv, axis=-1, keepdims=True) + EPS)
        xh = xv * r
        gv = g_ref[...]
        row = i * tr + lax.broadcasted_iota(jnp.int32, (tr, 1), 0)
        valid = (row >= N_META) & (row < N_META + n_tok)
        err = jnp.where(valid, xh * gv - tg_ref[...], 0.0)
        loss_ref[...] += jnp.sum(0.5 * err * err) / d
        dy = err / d
        dg_ref[...] += jnp.sum(dy * xh, axis=0, keepdims=True)
        dxh = dy * gv
        dh = r * (dxh - xh * jnp.mean(dxh * xh, axis=-1, keepdims=True))
        dh_ref[...] = dh
        dh16_ref[...] = dh.astype(BF16)

    return pl.pallas_call(
        body, grid=(n_tiles,),
        in_specs=[_rspec(tr, d), _rspec(tr, d), _fspec((1, d)), _rspec(tr, d)],
        out_specs=[_rspec(tr, d), _rspec(tr, d), _fspec((1, LANES)), _fspec((1, d))],
        out_shape=[jax.ShapeDtypeStruct((lp, d), F32), jax.ShapeDtypeStruct((lp, d), BF16),
                   jax.ShapeDtypeStruct((1, LANES), F32), jax.ShapeDtypeStruct((1, d), F32)],
        compiler_params=_cp("arbitrary"), name=name)(h1, t2, g, tgt)


def _hg_out(o_f, o_b, proj, gain, *, col_g, name):
    lp, w = o_f.shape
    tr = _row_tile(lp, w * (3 * 4 + 2))
    hh = w // HG_DK

    def body(of_ref, ob_ref, g_ref, gain_ref, y_ref):
        gv = g_ref[...]
        sg = gv * _sigmoid(gv)
        for h in range(hh):
            sl = slice(h * HG_DK, (h + 1) * HG_DK)
            o = of_ref[:, sl] + ob_ref[:, sl]
            r = lax.rsqrt(jnp.mean(o * o, axis=-1, keepdims=True) + EPS)
            y_ref[:, sl] = (o * r * gain_ref[:, sl] * sg[:, sl]).astype(BF16)

    return pl.pallas_call(
        body, grid=(lp // tr,),
        in_specs=[_rspec(tr, w), _rspec(tr, w), _rspec(tr, w, col_g // w), _fspec((1, w))],
        out_specs=_rspec(tr, w),
        out_shape=jax.ShapeDtypeStruct((lp, w), BF16),
        compiler_params=_cp("parallel"), name=name)(o_f, o_b, proj, gain)


def _hg_out_bwd(o_f, o_b, proj, gain, dy, *, col_g, name):
    lp, w = o_f.shape
    tr = _row_tile(lp, w * (5 * 4 + 2))
    hh = w // HG_DK

    def body(of_ref, ob_ref, g_ref, gain_ref, dy_ref, do_ref, dg_ref, dgain_ref):
        @pl.when(pl.program_id(0) == 0)
        def _():
            dgain_ref[...] = jnp.zeros_like(dgain_ref)

        for h in range(hh):
            sl = slice(h * HG_DK, (h + 1) * HG_DK)
            gv = g_ref[:, sl]
            s = _sigmoid(gv)
            sg = gv * s
            dsg = s + gv * s * (1.0 - s)
            o = of_ref[:, sl] + ob_ref[:, sl]
            r = lax.rsqrt(jnp.mean(o * o, axis=-1, keepdims=True) + EPS)
            on = o * r
            dyv = dy_ref[:, sl]
            gn = gain_ref[:, sl]
            dgain_ref[:, sl] += jnp.sum(dyv * on * sg, axis=0, keepdims=True)
            dg_ref[:, sl] = (dyv * on * gn * dsg).astype(BF16)
            don = dyv * gn * sg
            do_ref[:, sl] = r * (don - on * jnp.mean(don * on, axis=-1, keepdims=True))

    return pl.pallas_call(
        body, grid=(lp // tr,),
        in_specs=[_rspec(tr, w), _rspec(tr, w), _rspec(tr, w, col_g // w), _fspec((1, w)),
                  _rspec(tr, w)],
        out_specs=[_rspec(tr, w), _rspec(tr, w), _fspec((1, w))],
        out_shape=[jax.ShapeDtypeStruct((lp, w), F32), jax.ShapeDtypeStruct((lp, w), BF16),
                   jax.ShapeDtypeStruct((1, w), F32)],
        compiler_params=_cp("arbitrary"), name=name)(o_f, o_b, proj, gain, dy)


HG_ROWS = 128
HG_HALVES = (1, 2, 4, 8, 16, 32, 64)


def _hg_gates(zq, z, lbv):
    sq = _sigmoid(zq)
    s = _sigmoid(z)
    f = lbv + (1.0 - lbv) * s
    kk = (1.0 - lbv) * (1.0 - s)
    return zq * sq, sq, s, f, jnp.log(f), kk


def _block_cumsum(g, pos, suffix):
    x = g
    for k in HG_HALVES:
        if suffix:
            x = x + jnp.where(pos < HG_ROWS - k, pltpu.roll(x, HG_ROWS - k, 0), 0.0)
        else:
            x = x + jnp.where(pos >= k, pltpu.roll(x, k, 0), 0.0)
    return x


def _pair_levels(b, pos, reverse):
    out = []
    first = b
    for m in HG_HALVES:
        if m > 1:
            first = jnp.where((pos & (m - 1)) >= m // 2, pltpu.roll(first, m // 2, 0), first)
        nxt = pltpu.roll(first, HG_ROWS - m, 0)
        upper = (pos & (2 * m - 1)) >= m
        if reverse:
            eq = jnp.where(upper, 0.0, jnp.exp(b - nxt))
            ek = jnp.where(upper, jnp.exp(first - b), 0.0)
        else:
            eq = jnp.where(upper, jnp.exp(b - first), 0.0)
            ek = jnp.where(upper, 0.0, jnp.exp(nxt - b))
        out.append((eq, ek))
    return out


def _pair_masks(mask_ref):
    ri = lax.broadcasted_iota(jnp.int32, (HG_ROWS, HG_ROWS), 0)
    ci = lax.broadcasted_iota(jnp.int32, (HG_ROWS, HG_ROWS), 1)
    for i, m in enumerate(HG_HALVES):
        sh = m.bit_length()
        mask_ref[i] = jnp.where((ri >> sh) == (ci >> sh), 1.0, 0.0)


def _hg_scan_fwd(proj, lb, *, reverse, col_q, col_z, col_i, hh, name):
    lp = proj.shape[0]
    n_blocks = lp // HG_ROWS
    last = 0 if reverse else HG_ROWS - 1

    def body(q_ref, z_ref, i_ref, lb_ref, o_ref, st_ref, mask_ref):
        lbv = lb_ref[...]
        pos = lax.broadcasted_iota(jnp.int32, (HG_ROWS, 1), 0)
        ri = lax.broadcasted_iota(jnp.int32, (HG_ROWS, HG_ROWS), 0)
        ci = lax.broadcasted_iota(jnp.int32, (HG_ROWS, HG_ROWS), 1)
        _pair_masks(mask_ref)

        def block(bi, st):
            bb = (n_blocks - 1 - bi) if reverse else bi
            r0 = pl.multiple_of(bb * HG_ROWS, HG_ROWS)
            v16 = i_ref[pl.ds(r0, HG_ROWS), :].astype(BF16)
            qh, _, _, _, g, kk = _hg_gates(q_ref[pl.ds(r0, HG_ROWS), :],
                                           z_ref[pl.ds(r0, HG_ROWS), :], lbv)
            b = _block_cumsum(g, pos, reverse)
            bl = b[last:last + 1, :]
            qe = (qh * jnp.exp(b)).astype(BF16)
            kd = (kk * jnp.exp(bl - b)).astype(BF16)
            a = jnp.where(ri == ci, jnp.sum(qh * kk, axis=1, keepdims=True), 0.0)
            for i, (eq, ek) in enumerate(_pair_levels(b, pos, reverse)):
                a = a + mask_ref[i] * _nt((qh * eq).astype(BF16), (kk * ek).astype(BF16))
            st_ref[bb] = st
            o_ref[pl.ds(r0, HG_ROWS), :] = _nn(a.astype(BF16), v16) + _nt(qe, st.astype(BF16))
            return jnp.exp(bl) * st + _tn(v16, kd)

        lax.fori_loop(0, n_blocks, block, jnp.zeros((HG_DK, HG_DK), F32))

    cspec = lambda col: pl.BlockSpec((lp, HG_DK), lambda h: (0, col // HG_DK + h))
    return pl.pallas_call(
        body, grid=(hh,),
        in_specs=[cspec(col_q), cspec(col_z), cspec(col_i),
                  pl.BlockSpec((None, 1, HG_DK), lambda h: (h, 0, 0))],
        out_specs=[pl.BlockSpec((lp, HG_DK), lambda h: (0, h)),
                   pl.BlockSpec((None, n_blocks, HG_DK, HG_DK), lambda h: (h, 0, 0, 0))],
        out_shape=[jax.ShapeDtypeStruct((lp, hh * HG_DK), F32),
                   jax.ShapeDtypeStruct((hh, n_blocks, HG_DK, HG_DK), F32)],
        scratch_shapes=[pltpu.VMEM((len(HG_HALVES), HG_ROWS, HG_ROWS), F32)],
        compiler_params=_cp("parallel"), name=name)(proj, proj, proj, lb)


def _hg_scan_bwd(proj, lb, states, do, *, reverse, col_q, col_z, col_i, hh, name):
    lp = proj.shape[0]
    n_blocks = lp // HG_ROWS
    last = 0 if reverse else HG_ROWS - 1

    def body(q_ref, z_ref, i_ref, lb_ref, st_ref, do_ref, dq_ref, dz_ref, dv_ref, dlb_ref, mask_ref):
        lbv = lb_ref[...]
        pos = lax.broadcasted_iota(jnp.int32, (HG_ROWS, 1), 0)
        ri = lax.broadcasted_iota(jnp.int32, (HG_ROWS, HG_ROWS), 0)
        ci = lax.broadcasted_iota(jnp.int32, (HG_ROWS, HG_ROWS), 1)
        _pair_masks(mask_ref)

        def block(bi, carry):
            dst, dlb = carry
            bb = bi if reverse else (n_blocks - 1 - bi)
            r0 = pl.multiple_of(bb * HG_ROWS, HG_ROWS)
            zq = q_ref[pl.ds(r0, HG_ROWS), :]
            v16 = i_ref[pl.ds(r0, HG_ROWS), :].astype(BF16)
            do16 = do_ref[pl.ds(r0, HG_ROWS), :].astype(BF16)
            qh, sq, s, f, g, kk = _hg_gates(zq, z_ref[pl.ds(r0, HG_ROWS), :], lbv)
            b = _block_cumsum(g, pos, reverse)
            bl = b[last:last + 1, :]
            eb = jnp.exp(b)
            ebl = jnp.exp(bl - b)
            decay = jnp.exp(bl)
            qe16 = (qh * eb).astype(BF16)
            kd16 = (kk * ebl).astype(BF16)
            st = st_ref[bb]
            st16, dst16 = st.astype(BF16), dst.astype(BF16)
            same_row = ri == ci
            da = _nt(do16, v16)
            da_diag = jnp.sum(jnp.where(same_row, da, 0.0), axis=1, keepdims=True)
            dq_state = eb * _nn(do16, st16)
            dk_state = ebl * _nn(v16, dst16)
            dq = dq_state + da_diag * kk
            dk = dk_state + da_diag * qh
            dbl = (decay * jnp.sum(st * dst, axis=0, keepdims=True)
                   + jnp.sum(kk * dk_state, axis=0, keepdims=True))
            db = qh * dq_state - kk * dk_state + jnp.where(pos == last, dbl, 0.0)
            a = jnp.where(same_row, jnp.sum(qh * kk, axis=1, keepdims=True), 0.0)
            for i, (eq, ek) in enumerate(_pair_levels(b, pos, reverse)):
                same = mask_ref[i]
                q16, k16 = (qh * eq).astype(BF16), (kk * ek).astype(BF16)
                a = a + same * _nt(q16, k16)
                da16 = (same * da).astype(BF16)
                gq, gk = _nn(da16, k16), _tn(da16, q16)
                dq = dq + eq * gq
                dk = dk + ek * gk
                db = db + (q16.astype(F32) * gq - k16.astype(F32) * gk)
            dg = _block_cumsum(db, pos, not reverse)
            df = dg / f - dk
            dq_ref[pl.ds(r0, HG_ROWS), :] = dq * (sq + zq * sq * (1.0 - sq))
            dz_ref[pl.ds(r0, HG_ROWS), :] = (df * (1.0 - lbv) * s * (1.0 - s)).astype(BF16)
            dv_ref[pl.ds(r0, HG_ROWS), :] = _nt(kd16, dst16) + _tn(a.astype(BF16), do16)
            return (decay * dst + _tn(do16, qe16),
                    dlb + jnp.sum(df * (1.0 - s), axis=0, keepdims=True))

        _, dlb = lax.fori_loop(0, n_blocks, block,
                               (jnp.zeros((HG_DK, HG_DK), F32), jnp.zeros((1, HG_DK), F32)))
        dlb_ref[...] = dlb

    cspec = lambda col: pl.BlockSpec((lp, HG_DK), lambda h: (0, col // HG_DK + h))
    ospec = pl.BlockSpec((lp, HG_DK), lambda h: (0, h))
    sds = jax.ShapeDtypeStruct((lp, hh * HG_DK), F32)
    return pl.pallas_call(
        body, grid=(hh,),
        in_specs=[cspec(col_q), cspec(col_z), cspec(col_i),
                  pl.BlockSpec((None, 1, HG_DK), lambda h: (h, 0, 0)),
                  pl.BlockSpec((None, n_blocks, HG_DK, HG_DK), lambda h: (h, 0, 0, 0)),
                  ospec],
        out_specs=[ospec, ospec, ospec, pl.BlockSpec((None, 1, HG_DK), lambda h: (h, 0, 0))],
        out_shape=[sds, jax.ShapeDtypeStruct((lp, hh * HG_DK), BF16), sds,
                   jax.ShapeDtypeStruct((hh, 1, HG_DK), F32)],
        scratch_shapes=[pltpu.VMEM((len(HG_HALVES), HG_ROWS, HG_ROWS), F32)],
        compiler_params=_cp("parallel"), name=name)(proj, proj, proj, lb, states, do)


NA_HB = LANES // NA_HEAD_DIM
NA_G = 4
NA_U = NA_G + NA_WIN_H
NA_QN = NA_G * GRID_W
NA_KN = NA_U * GRID_W


def _na_table_index(pattern, a, j):
    if pattern == 0:
        return j - a + NA_WIN_H - 1 if j < NA_WIN_H else None
    if pattern == 2:
        return j - a - 1 if j >= NA_U - NA_WIN_H else None
    return j - a + NA_WIN_H // 2 - 1 if a <= j < a + NA_WIN_H else None


def _na_step_rows(pattern, t, rows):
    if pattern == 0:
        r0, us = 0, 0
    elif pattern == 2:
        r0, us = rows - NA_G, rows - NA_U
    else:
        r0 = NA_G * t
        us = r0 - NA_WIN_H // 2
    q0, k0 = N_META + GRID_W * r0, N_META + GRID_W * us
    if pattern == 1:
        q0, k0 = pl.multiple_of(q0, 16), pl.multiple_of(k0, 16)
    return q0, k0


def _na_fill_bias(tb_ref, bias_ref):
    neg = jnp.full((GRID_W, GRID_W), -1e30, F32)
    for h in range(NA_HB):
        for pattern in range(3):
            for a in range(NA_G):
                for j in range(NA_U):
                    idx = _na_table_index(pattern, a, j)
                    bias_ref[h, pattern, a * GRID_W:(a + 1) * GRID_W, j * GRID_W:(j + 1) * GRID_W] = (
                        neg if idx is None else tb_ref[h, idx])


def _na_steps(rows, step, carry):
    n_steps = rows // NA_G
    carry = step(0, 0, carry)
    carry = lax.fori_loop(1, n_steps - 1, functools.partial(step, 1), carry, unroll=2)
    return step(2, n_steps - 1, carry)


def _na_head_lanes():
    lane = lax.broadcasted_iota(jnp.int32, (1, LANES), 1)
    return [lane // NA_HEAD_DIM == h for h in range(NA_HB)]


def _na_only(mask, x):
    return jnp.where(mask, x, jnp.zeros_like(x))


def _na_stack(heads, x):
    return jnp.concatenate([_na_only(mask, x) for mask in heads], axis=0)


def _na_unstack(heads, y):
    rows = y.shape[0] // NA_HB
    out = y[0:rows]
    for h in range(1, NA_HB):
        out = jnp.where(heads[h], y[h * rows:(h + 1) * rows], out)
    return out


def _na_fwd(proj, tb, *, n_tok, nh, name):
    lp = proj.shape[0]
    dh, hb = NA_HEAD_DIM, NA_HB
    naw = nh * dh
    rows = n_tok // GRID_W
    scale = dh ** -0.5

    def body(q_ref, k_ref, v_ref, tb_ref, o_ref, lse_ref, q16_ref, k16_ref, v16_ref, bias_ref):
        o_ref[...] = jnp.zeros_like(o_ref)
        lse_ref[...] = jnp.zeros_like(lse_ref)
        q16_ref[...] = q_ref[...].astype(BF16)
        k16_ref[...] = k_ref[...].astype(BF16)
        v16_ref[...] = v_ref[...].astype(BF16)
        _na_fill_bias(tb_ref, bias_ref)
        heads = _na_head_lanes()
        km = k16_ref[0:N_META, :]
        vm = v16_ref[0:N_META, :]
        qm = q16_ref[0:N_META, :]
        o_m = None
        for h in range(hb):
            s = _nt(_na_only(heads[h], qm), km) * scale
            m = jnp.max(s, axis=1, keepdims=True)
            p = jnp.exp(s - m)
            l = jnp.sum(p, axis=1, keepdims=True)
            o_h = _nn(p.astype(BF16), vm) / l
            o_m = o_h if o_m is None else jnp.where(heads[h], o_h, o_m)
            lse_ref[h, 0:N_META, :] = m + jnp.log(l)
        o_ref[0:N_META, :] = o_m

        def step(pattern, t, carry):
            q0, k0 = _na_step_rows(pattern, t, rows)
            k16 = k16_ref[pl.ds(k0, NA_KN), :]
            v16 = v16_ref[pl.ds(k0, NA_KN), :]
            q2 = _na_stack(heads, q16_ref[pl.ds(q0, NA_QN), :])
            s = _nt(q2, k16) * scale + bias_ref[:, pattern].reshape(hb * NA_QN, NA_KN)
            sm = _nt(q2, km) * scale
            m = jnp.maximum(jnp.max(s, axis=1, keepdims=True), jnp.max(sm, axis=1, keepdims=True))
            p = jnp.exp(s - m)
            pm = jnp.exp(sm - m)
            l = jnp.sum(p, axis=1, keepdims=True) + jnp.sum(pm, axis=1, keepdims=True)
            o2 = (_nn(p.astype(BF16), v16) + _nn(pm.astype(BF16), vm)) / l
            o_ref[pl.ds(q0, NA_QN), :] = _na_unstack(heads, o2)
            lse2 = m + jnp.log(l)
            for h in range(hb):
                lse_ref[h, pl.ds(q0, NA_QN), :] = lse2[h * NA_QN:(h + 1) * NA_QN]
            return carry

        _na_steps(rows, step, 0)

    cblk = lambda col: pl.BlockSpec((lp, LANES), lambda g: (0, col // LANES + g))
    return pl.pallas_call(
        body, grid=(nh // hb,),
        in_specs=[cblk(0), cblk(naw), cblk(2 * naw),
                  pl.BlockSpec((hb, 2 * NA_WIN_H - 1, GRID_W, GRID_W), lambda g: (g, 0, 0, 0))],
        out_specs=[cblk(0), pl.BlockSpec((hb, lp, 1), lambda g: (g, 0, 0))],
        out_shape=[jax.ShapeDtypeStruct((lp, naw), F32), jax.ShapeDtypeStruct((nh, lp, 1), F32)],
        scratch_shapes=[pltpu.VMEM((lp, LANES), BF16)] * 3 + [pltpu.VMEM((hb, 3, NA_QN, NA_KN), F32)],
        compiler_params=_cp("parallel"), name=name)(proj, proj, proj, tb)


def _na_bwd(proj, tb, o, lse, do, *, n_tok, nh, name):
    lp = proj.shape[0]
    dh, hb = NA_HEAD_DIM, NA_HB
    naw = nh * dh
    rows = n_tok // GRID_W
    scale = dh ** -0.5

    def body(q_ref, k_ref, v_ref, tb_ref, o_ref, lse_ref, do_ref, dq_ref, dk_ref, dv_ref, dtb_ref,
             q16_ref, k16_ref, v16_ref, bias_ref):
        dq_ref[...] = jnp.zeros_like(dq_ref)
        dk_ref[...] = jnp.zeros_like(dk_ref)
        dv_ref[...] = jnp.zeros_like(dv_ref)
        dtb_ref[...] = jnp.zeros_like(dtb_ref)
        q16_ref[...] = q_ref[...].astype(BF16)
        k16_ref[...] = k_ref[...].astype(BF16)
        v16_ref[...] = v_ref[...].astype(BF16)
        _na_fill_bias(tb_ref, bias_ref)
        heads = _na_head_lanes()
        km = k16_ref[0:N_META, :]
        vm = v16_ref[0:N_META, :]
        qm = q16_ref[0:N_META, :]
        dom = do_ref[0:N_META, :]
        prod = dom * o_ref[0:N_META, :]
        dq_m = None
        dkm0 = jnp.zeros((N_META, LANES), F32)
        dvm0 = jnp.zeros((N_META, LANES), F32)
        for h in range(hb):
            q_h = _na_only(heads[h], qm)
            do_h = _na_only(heads[h], dom).astype(BF16)
            p = jnp.exp(_nt(q_h, km) * scale - lse_ref[h, 0:N_META, :])
            delta = jnp.sum(_na_only(heads[h], prod), axis=1, keepdims=True)
            ds = (p * (_nt(do_h, vm) - delta)).astype(BF16)
            dq_h = _nn(ds, km) * scale
            dq_m = dq_h if dq_m is None else jnp.where(heads[h], dq_h, dq_m)
            dkm0 = dkm0 + _tn(ds, q_h) * scale
            dvm0 = dvm0 + _tn(p.astype(BF16), do_h)
        dq_ref[0:N_META, :] = dq_m

        def step(pattern, t, carry):
            dkm, dvm = carry
            q0, k0 = _na_step_rows(pattern, t, rows)
            k16 = k16_ref[pl.ds(k0, NA_KN), :]
            v16 = v16_ref[pl.ds(k0, NA_KN), :]
            q2 = _na_stack(heads, q16_ref[pl.ds(q0, NA_QN), :])
            do2 = _na_stack(heads, do_ref[pl.ds(q0, NA_QN), :])
            do16 = do2.astype(BF16)
            ov = o_ref[pl.ds(q0, NA_QN), :]
            delta = jnp.sum(do2 * jnp.concatenate([ov] * hb, axis=0), axis=1, keepdims=True)
            lse = jnp.concatenate([lse_ref[h, pl.ds(q0, NA_QN), :] for h in range(hb)], axis=0)
            p = jnp.exp(_nt(q2, k16) * scale + bias_ref[:, pattern].reshape(hb * NA_QN, NA_KN)
                        - lse)
            pm = jnp.exp(_nt(q2, km) * scale - lse)
            ds = p * (_nt(do16, v16) - delta)
            dsm = (pm * (_nt(do16, vm) - delta)).astype(BF16)
            ds16 = ds.astype(BF16)
            dq2 = (_nn(ds16, k16) + _nn(dsm, km)) * scale
            dq_ref[pl.ds(q0, NA_QN), :] = _na_unstack(heads, dq2)
            dk_ref[pl.ds(k0, NA_KN), :] += _tn(ds16, q2) * scale
            dv_ref[pl.ds(k0, NA_KN), :] += _tn(p.astype(BF16), do16)
            for h in range(hb):
                for a in range(NA_G):
                    for j in range(NA_U):
                        idx = _na_table_index(pattern, a, j)
                        if idx is not None:
                            r = h * NA_QN + a * GRID_W
                            dtb_ref[h, idx] += ds[r:r + GRID_W, j * GRID_W:(j + 1) * GRID_W]
            return dkm + _tn(dsm, q2) * scale, dvm + _tn(pm.astype(BF16), do16)

        dkm, dvm = _na_steps(rows, step, (dkm0, dvm0))
        dk_ref[0:N_META, :] += dkm
        dv_ref[0:N_META, :] += dvm

    cblk = lambda col: pl.BlockSpec((lp, LANES), lambda g: (0, col // LANES + g))
    tbs = pl.BlockSpec((hb, 2 * NA_WIN_H - 1, GRID_W, GRID_W), lambda g: (g, 0, 0, 0))
    sds = jax.ShapeDtypeStruct((lp, naw), F32)
    return pl.pallas_call(
        body, grid=(nh // hb,),
        in_specs=[cblk(0), cblk(naw), cblk(2 * naw), tbs, cblk(0),
                  pl.BlockSpec((hb, lp, 1), lambda g: (g, 0, 0)), cblk(0)],
        out_specs=[cblk(0), cblk(0), cblk(0), tbs],
        out_shape=[sds, sds, sds, jax.ShapeDtypeStruct(tb.shape, F32)],
        scratch_shapes=[pltpu.VMEM((lp, LANES), BF16)] * 3 + [pltpu.VMEM((hb, 3, NA_QN, NA_KN), F32)],
        compiler_params=_cp("parallel"), name=name)(proj, proj, proj, tb, o, lse, do)


def _rpb_onehot():
    c = np.arange(GRID_W)[:, None]
    w = np.arange(GRID_W)[None, :]
    cs = np.clip(c - NA_WIN_W // 2, 0, GRID_W - NA_WIN_W)
    in_win = (w >= cs) & (w < cs + NA_WIN_W)
    dc = np.clip(w - c, -(NA_WIN_W - 1), NA_WIN_W - 1) + NA_WIN_W - 1
    oh = np.zeros((LANES, GRID_W * GRID_W), np.float32)
    flat = np.arange(GRID_W * GRID_W).reshape(GRID_W, GRID_W)
    oh[dc[in_win], flat[in_win]] = 1.0
    neg = np.where(in_win, 0.0, -1e30).astype(np.float32).reshape(1, -1)
    return oh, neg


def _assemble_dproj(dq_na, dk_na, dv_na, dq_f, dq_b, dz_f, dz_b, dv_f, dv_b, dg, dgn, dgh, *, name):
    lp, naw = dq_na.shape
    hgw = dq_f.shape[1]
    d = dgn.shape[1]
    cols = 3 * naw + 5 * hgw + 2 * d
    tr = _row_tile(lp, 3 * naw * 4 + 4 * hgw * 4 + 3 * hgw * 2 + 2 * d * 2 + cols * 2)

    def body(nq_ref, nk_ref, nv_ref, qf_ref, qb_ref, zf_ref, zb_ref, vf_ref, vb_ref, g_ref, gn_ref,
             gh_ref, o_ref):
        o_ref[:, 0:naw] = nq_ref[...].astype(BF16)
        o_ref[:, naw:2 * naw] = nk_ref[...].astype(BF16)
        o_ref[:, 2 * naw:3 * naw] = nv_ref[...].astype(BF16)
        c0 = 3 * naw
        o_ref[:, c0:c0 + hgw] = (qf_ref[...] + qb_ref[...]).astype(BF16)
        o_ref[:, c0 + hgw:c0 + 2 * hgw] = zf_ref[...]
        o_ref[:, c0 + 2 * hgw:c0 + 3 * hgw] = zb_ref[...]
        o_ref[:, c0 + 3 * hgw:c0 + 4 * hgw] = (vf_ref[...] + vb_ref[...]).astype(BF16)
        o_ref[:, c0 + 4 * hgw:c0 + 5 * hgw] = g_ref[...]
        o_ref[:, c0 + 5 * hgw:c0 + 5 * hgw + d] = gn_ref[...]
        o_ref[:, c0 + 5 * hgw + d:] = gh_ref[...]

    hg, na = _rspec(tr, hgw), _rspec(tr, naw)
    return pl.pallas_call(
        body, grid=(lp // tr,),
        in_specs=[na, na, na, hg, hg, hg, hg, hg, hg, hg, _rspec(tr, d), _rspec(tr, d)],
        out_specs=_rspec(tr, cols),
        out_shape=jax.ShapeDtypeStruct((lp, cols), BF16),
        compiler_params=_cp("parallel"), name=name)(dq_na, dk_na, dv_na, dq_f, dq_b, dz_f, dz_b,
                                                    dv_f, dv_b, dg, dgn, dgh)


GROUP_STEPS = 8


def _group_tiles(rows, align):
    steps = GROUP_STEPS if all(r % (GROUP_STEPS * align) == 0 for r in rows) else 1
    return steps, [r // steps for r in rows]


def _adamw(ws, gs, ms, vs, *, name):
    n = len(ws)
    steps, trs = _group_tiles([w.shape[0] for w in ws], 8)

    def body(*refs):
        for i in range(n):
            w_ref, g_ref, m_ref, v_ref = refs[4 * i:4 * i + 4]
            d_ref, mo_ref, vo_ref, go_ref = refs[4 * n + 4 * i:4 * n + 4 * i + 4]
            gv = g_ref[...]
            go_ref[...] = gv
            mn = ADAM_B1 * m_ref[...] + (1.0 - ADAM_B1) * gv
            vn = ADAM_B2 * v_ref[...] + (1.0 - ADAM_B2) * (gv * gv)
            m_hat = mn / (1.0 - ADAM_B1 ** ADAM_STEP)
            v_hat = vn / (1.0 - ADAM_B2 ** ADAM_STEP)
            d_ref[...] = -ADAM_LR * (m_hat / (jnp.sqrt(v_hat) + ADAM_EPS) + ADAM_WD * w_ref[...])
            mo_ref[...] = mn
            vo_ref[...] = vn

    specs = [_rspec(tr, w.shape[1]) for tr, w in zip(trs, ws)]
    out = pl.pallas_call(
        body, grid=(steps,),
        in_specs=[s for s in specs for _ in range(4)],
        out_specs=[s for s in specs for _ in range(4)],
        out_shape=[jax.ShapeDtypeStruct(w.shape, F32) for w in ws for _ in range(4)],
        compiler_params=_cp("parallel"), name=name)(*[a for q in zip(ws, gs, ms, vs) for a in q])
    return [tuple(out[4 * i:4 * i + 4]) for i in range(n)]


def _local_step(x, tgt, meta, first_weight, rest_weights, g_mix, g_mlp, g_fin, hg_gain, rpb, lb,
                early_grads=None, mid_grads=None, late_grad=None, rest_landed=None,
                last_grads=None):
    n_tok, d = x.shape
    hgw = hg_gain.shape[1]
    nh, hh = rpb.shape[0], hgw // HG_DK
    naw = nh * NA_HEAD_DIM
    l_real = N_META + n_tok
    lp = -(-l_real // ROW_ALIGN) * ROW_ALIGN
    col_qhg = 3 * naw
    col_zf, col_zb, col_i, col_g = (col_qhg + hgw, col_qhg + 2 * hgw, col_qhg + 3 * hgw,
                                    col_qhg + 4 * hgw)
    col_gate = col_qhg + 5 * hgw

    oh_np, neg_np = _rpb_onehot()
    oh = jnp.asarray(oh_np)
    rpb_p = jnp.pad(rpb.reshape(nh * (2 * NA_WIN_H - 1), 2 * NA_WIN_W - 1),
                    ((0, 0), (0, LANES - (2 * NA_WIN_W - 1))))
    tb = _matmul(rpb_p, oh, tm=rpb_p.shape[0], tn=512, tk=LANES, precision=HIGHEST,
                 name="rpb_expand")
    tb = (tb + jnp.asarray(neg_np)).reshape(nh, 2 * NA_WIN_H - 1, GRID_W, GRID_W)

    h0, a, tgt_p = _embed_norm(x, tgt, meta, g_mix, lp=lp, name="norm_mix")
    w_in, proj = first_weight(a, tb)
    o_na, lse = _na_fwd(proj, tb, n_tok=n_tok, nh=nh, name="na_fwd")
    lb_f = lb[0].reshape(hh, 1, HG_DK)
    lb_b = lb[1].reshape(hh, 1, HG_DK)
    scan_kw = dict(col_q=col_qhg, col_i=col_i, hh=hh)
    o_f, st_f = _hg_scan_fwd(proj, lb_f, reverse=False, col_z=col_zf, name="hg_scan_f", **scan_kw)
    token = rest_landed(o_f) if rest_landed else None
    lb_b_late = lb_b if token is None else lb_b + token[0:1, 0:1]
    o_b, st_b = _hg_scan_fwd(proj, lb_b_late, reverse=True, col_z=col_zb, name="hg_scan_b",
                             **scan_kw)
    o_hg = _hg_out(o_f, o_b, proj, hg_gain, col_g=col_g, name="hg_out")
    w_na, w_hg, w_o, w_up, w_down = rest_weights(o_hg)
    y_na = _matmul(o_na, w_na, name="mm_na_out", out_dtype=BF16)
    gates = ((proj, col_gate), (proj, col_gate + d))

    def mix_gates(acc, gn, gh, yn):
        return acc, _sigmoid(gn) * yn + _sigmoid(gh) * acc

    def mix_gates_bwd(dmix, gn, gh, yn, yh):
        sn, sh = _sigmoid(gn), _sigmoid(gh)
        return dmix * sn, dmix * sh, dmix * yn * sn * (1.0 - sn), dmix * yh * sh * (1.0 - sh)

    y_hg, mix = _matmul(o_hg, w_hg, name="mm_hg_out", epilogue=mix_gates,
                        tiles=(*gates, (y_na, 0)), out_dtypes=(BF16, BF16))
    t1 = _matmul(mix, w_o, name="mm_o")
    h1, mlp_in = _residual_norm(h0, t1, g_mlp, name="resid_norm_mlp")
    u, act = _matmul(mlp_in, w_up, name="mm_up", out_dtypes=(BF16, BF16),
                     epilogue=lambda acc: (acc, jnp.square(jnp.maximum(acc, 0.0))))
    t2 = _matmul(act, w_down, name="mm_down")
    dh2, dh2_16, loss, dg_fin = _final_loss(h1, t2, g_fin, tgt_p, n_tok=n_tok, name="final_loss")

    (du,) = _matmul(dh2_16, w_down, tb=True, name="mm_down_dx", tiles=((u, 0),),
                    out_dtypes=(BF16,),
                    epilogue=lambda acc, uv: (acc * 2.0 * jnp.maximum(uv, 0.0),))
    dw_down = _matmul(act, dh2_16, ta=True, name="mm_down_dw")
    dm = _matmul(du, w_up, tb=True, name="mm_up_dx")
    dw_up = _matmul(mlp_in, du, ta=True, name="mm_up_dw")
    dh1, dh1_16, dg_mlp = _rmsnorm_bwd_add(h1, g_mlp, dm, dh2, name="norm_mlp_bwd")
    dy_na, dy_hg, dgn, dgh = _matmul(dh1_16, w_o, tb=True, name="mm_o_dx", epilogue=mix_gates_bwd,
                                     tiles=(*gates, (y_na, 0), (y_hg, 0)), out_dtypes=(BF16,) * 4)
    dw_o = _matmul(mix, dh1_16, ta=True, name="mm_o_dw")
    do_na = _matmul(dy_na, w_na, tb=True, name="mm_na_out_dx")
    dw_na = _matmul(o_na, dy_na, ta=True, name="mm_na_out_dw")
    do_hg = _matmul(dy_hg, w_hg, tb=True, name="mm_hg_out_dx")
    dw_hg = _matmul(o_hg, dy_hg, ta=True, name="mm_hg_out_dw")
    token = early_grads([dw_na, dw_hg, dw_o, dw_up, dw_down]) if early_grads else None
    if token is not None:
        hg_gain = hg_gain + token[0:1, 0:1]
    d_o, dg_hg, d_gain = _hg_out_bwd(o_f, o_b, proj, hg_gain, do_hg, col_g=col_g, name="hg_out_bwd")
    dq_f, dz_f, dv_f, dlb_f = _hg_scan_bwd(proj, lb_f, st_f, d_o, reverse=False, col_z=col_zf,
                                           name="hg_scan_f_bwd", **scan_kw)
    token = mid_grads(dq_f) if mid_grads else None
    lb_b_late = lb_b if token is None else lb_b + token[0:1, 0:1]
    dq_b, dz_b, dv_b, dlb_b = _hg_scan_bwd(proj, lb_b_late, st_b, d_o, reverse=True, col_z=col_zb,
                                           name="hg_scan_b_bwd", **scan_kw)
    dq_na, dk_na, dv_na, dtb = _na_bwd(proj, tb, o_na, lse, do_na, n_tok=n_tok, nh=nh, name="na_bwd")
    dproj = _assemble_dproj(dq_na, dk_na, dv_na, dq_f, dq_b, dz_f, dz_b, dv_f, dv_b, dg_hg, dgn,
                            dgh, name="assemble_dproj")
    dw_in = _matmul(a, dproj, ta=True, name="mm_in_dw")
    token = late_grad(dw_in) if late_grad else None
    da = _matmul(dproj, w_in, tb=True, name="mm_in_dx", after=token)
    token = last_grads(da) if last_grads else None
    g_mix_late = g_mix if token is None else g_mix + token[0:1, 0:1]
    dx, dmeta, dg_mix = _rmsnorm_bwd_tokens(h0, g_mix_late, da, dh1, n_tok=n_tok,
                                            name="norm_mix_bwd")
    d_rpb = _matmul(dtb.reshape(nh * (2 * NA_WIN_H - 1), GRID_W * GRID_W), oh, tb=True,
                    tm=nh * (2 * NA_WIN_H - 1), tn=LANES, tk=1024, precision=HIGHEST,
                    name="rpb_reduce")
    d_lb = jnp.concatenate([dlb_f.reshape(1, hgw), dlb_b.reshape(1, hgw)], axis=0)
    return (loss, dx, dmeta, dw_in, dw_na, dw_hg, dw_o, dw_up, dw_down,
            dg_mix, dg_mlp, dg_fin, d_gain, d_rpb, d_lb)


N_CHIPS = 4
N_DEV = 8
ANY = pl.BlockSpec(memory_space=pl.ANY)


def _place():
    x, y, c = lax.axis_index("x"), lax.axis_index("y"), lax.axis_index("c")
    others = []
    for j in (1, 2, 3):
        tx = (1 - x) if (j >> 1) else x
        ty = (1 - y) if (j & 1) else y
        others.append((tx, ty))
    return x, y, c, others


def _piece(ref, axis, k, half, rh, cs):
    if axis == 1:
        return ref.at[pl.ds(pl.multiple_of(half * rh, 16), rh), pl.ds(pl.multiple_of(k * cs, LANES), cs)]
    return ref.at[pl.ds(pl.multiple_of(k * 2 * rh + half * rh, 16), rh), :]


def _cast_into_full(shards, axes, place, *, name):
    n = len(shards)
    steps, trs = _group_tiles([s.shape[0] for s in shards], 16)

    def body(p_ref, *refs):
        for i in range(n):
            refs[n + i][...] = refs[i][...].astype(BF16)

    def out_spec(tr, cs, axis):
        if axis == 1:
            return pl.BlockSpec((tr, cs), lambda i, p_ref: (i, p_ref[0]))
        return pl.BlockSpec((tr, cs), lambda i, p_ref: (p_ref[0] * steps + i, 0))

    return pl.pallas_call(
        body,
        grid_spec=pltpu.PrefetchScalarGridSpec(
            num_scalar_prefetch=1, grid=(steps,),
            in_specs=[pl.BlockSpec((tr, s.shape[1]), lambda i, p_ref: (i, 0))
                      for tr, s in zip(trs, shards)],
            out_specs=[out_spec(tr, s.shape[1], ax) for tr, s, ax in zip(trs, shards, axes)]),
        out_shape=[jax.ShapeDtypeStruct((s.shape[0], s.shape[1] * N_CHIPS) if ax == 1
                                        else (s.shape[0] * N_CHIPS, s.shape[1]), BF16)
                   for s, ax in zip(shards, axes)],
        compiler_params=_cp("parallel"), name=name)(place, *shards)


HBM_SPEC = pl.BlockSpec(memory_space=pltpu.HBM)
SEM_SPEC = pl.BlockSpec(memory_space=pltpu.SEMAPHORE)
SPLIT_COPY = pltpu.CompilerParams(has_side_effects=pltpu.SideEffectType.DATAFLOW_SIDE_EFFECTING)
TOKEN = jax.ShapeDtypeStruct((8, LANES), F32)


def _geo(fulls, axes):
    out = []
    for f, ax in zip(fulls, axes):
        r, cs = (f.shape[0], f.shape[1] // N_CHIPS) if ax == 1 else (f.shape[0] // N_CHIPS, f.shape[1])
        out.append((ax, r // 2, cs))
    return out


def _gather_copies(refs, geo, send_sems, recv_sems):
    x, y, c, others = _place()
    chip = 2 * x + y
    cps = []
    for i, (ax, rh, cs) in enumerate(geo):
        mine = _piece(refs[i], ax, chip, c, rh, cs)
        for j, (tx, ty) in enumerate(others):
            cps.append(pltpu.make_async_remote_copy(
                src_ref=mine, dst_ref=mine, send_sem=send_sems.at[3 * i + j],
                recv_sem=recv_sems.at[3 * i + j], device_id=(tx, ty, c), device_id_type=MESH))
    return cps


def _allgather_start(fulls, axes, after, *, name):
    n = len(fulls)
    geo = _geo(fulls, axes)

    def body(*refs):
        w_refs = refs[:n]
        send_sems, recv_sems = refs[n + 1], refs[n + 2]
        token = refs[2 * n + 3]
        for cp in _gather_copies(w_refs, geo, send_sems, recv_sems):
            cp.start()
        token[...] = jnp.zeros_like(token)

    out = pl.pallas_call(
        body, name=name,
        out_shape=(pltpu.SemaphoreType.DMA((3 * n,)), pltpu.SemaphoreType.DMA((3 * n,)),
                   *[pltpu.HBM(f.shape, f.dtype) for f in fulls], TOKEN),
        in_specs=[HBM_SPEC] * n + [ANY],
        out_specs=(SEM_SPEC, SEM_SPEC, *[HBM_SPEC] * n, pl.BlockSpec(memory_space=pltpu.VMEM)),
        input_output_aliases={i: 2 + i for i in range(n)},
        compiler_params=SPLIT_COPY,
    )(*[pltpu.with_memory_space_constraint(f, pltpu.HBM) for f in fulls], after)
    return out[0], out[1], list(out[2:2 + n]), out[2 + n]


def _allgather_wait(send_sems, recv_sems, fulls, axes, after, *, name):
    n = len(fulls)
    geo = _geo(fulls, axes)
    afters = tuple(after) if isinstance(after, (tuple, list)) else (after,)

    def body(*refs):
        w_refs = refs[:n]
        for cp in _gather_copies(w_refs, geo, refs[n], refs[n + 1]):
            cp.wait_send()
            cp.wait_recv()

    return list(pl.pallas_call(
        body, name=name,
        out_shape=[pltpu.HBM(f.shape, f.dtype) for f in fulls],
        in_specs=[HBM_SPEC] * n + [SEM_SPEC, SEM_SPEC] + [ANY] * len(afters),
        out_specs=[HBM_SPEC] * n,
        input_output_aliases={i: i for i in range(n)},
        compiler_params=SPLIT_COPY,
    )(*fulls, send_sems, recv_sems, *afters))


def _allgather_forward(fulls, axes, *, name):
    n = len(fulls)
    geo = _geo(fulls, axes)

    def body(*refs):
        o_refs = refs[n:2 * n]
        send_sems, recv_sems = refs[2 * n:]
        x, y, c, others = _place()

        def rcopy(i, j, half, to):
            ax, rh, cs = geo[i]
            ref = _piece(o_refs[i], ax, 2 * others[j][0] + others[j][1], half, rh, cs)
            return pltpu.make_async_remote_copy(
                src_ref=ref, dst_ref=ref, send_sem=send_sems.at[3 * i + j],
                recv_sem=recv_sems.at[3 * i + j], device_id=to, device_id_type=MESH)

        cps = [rcopy(i, j, c, (x, y, 1 - c)) for i in range(n) for j in range(3)]
        for cp in cps:
            cp.start()
        for i in range(n):
            for j in range(3):
                rcopy(i, j, 1 - c, (x, y, c)).wait_recv()
        for cp in cps:
            cp.wait_send()

    return list(pl.pallas_call(
        body, in_specs=[ANY] * n, out_specs=[ANY] * n,
        out_shape=[jax.ShapeDtypeStruct(f.shape, f.dtype) for f in fulls],
        input_output_aliases={i: i for i in range(n)},
        scratch_shapes=[pltpu.SemaphoreType.DMA((3 * n,)), pltpu.SemaphoreType.DMA((3 * n,))],
        name=name)(*fulls))


def _forward_copies(refs, geo, send_sems, recv_sems):
    x, y, c, others = _place()
    cps = []
    for i, (ax, rh, cs) in enumerate(geo):
        for j, (tx, ty) in enumerate(others):
            ref = _piece(refs[i], ax, 2 * tx + ty, c, rh, cs)
            cps.append(pltpu.make_async_remote_copy(
                src_ref=ref, dst_ref=ref, send_sem=send_sems.at[3 * i + j],
                recv_sem=recv_sems.at[3 * i + j], device_id=(x, y, 1 - c), device_id_type=MESH))
    return cps


def _allgather_forward_start(fulls, axes, *, name):
    n = len(fulls)
    geo = _geo(fulls, axes)

    def body(*refs):
        token = refs[2 * n + 2]
        for cp in _forward_copies(refs[:n], geo, refs[n], refs[n + 1]):
            cp.start()
        token[...] = jnp.zeros_like(token)

    out = pl.pallas_call(
        body, name=name,
        out_shape=(pltpu.SemaphoreType.DMA((3 * n,)), pltpu.SemaphoreType.DMA((3 * n,)),
                   *[pltpu.HBM(f.shape, f.dtype) for f in fulls], TOKEN),
        in_specs=[HBM_SPEC] * n,
        out_specs=(SEM_SPEC, SEM_SPEC, *[HBM_SPEC] * n, pl.BlockSpec(memory_space=pltpu.VMEM)),
        input_output_aliases={i: 2 + i for i in range(n)},
        compiler_params=SPLIT_COPY,
    )(*fulls)
    return out[0], out[1], list(out[2:2 + n]), out[2 + n]


def _allgather_forward_wait(send_sems, recv_sems, fulls, axes, after, *, name):
    n = len(fulls)
    geo = _geo(fulls, axes)

    def body(*refs):
        for cp in _forward_copies(refs[:n], geo, refs[n], refs[n + 1]):
            cp.wait_send()
            cp.wait_recv()

    return list(pl.pallas_call(
        body, name=name,
        out_shape=[pltpu.HBM(f.shape, f.dtype) for f in fulls],
        in_specs=[HBM_SPEC] * n + [SEM_SPEC, SEM_SPEC, ANY],
        out_specs=[HBM_SPEC] * n,
        input_output_aliases={i: i for i in range(n)},
        compiler_params=SPLIT_COPY,
    )(*fulls, send_sems, recv_sems, after))


def _chip_copies(blk_ref, land_ref, send_sems, recv_sems):
    x, y, c, others = _place()
    return [pltpu.make_async_remote_copy(
        src_ref=blk_ref, dst_ref=land_ref.at[2 * x + y], send_sem=send_sems.at[j],
        recv_sem=recv_sems.at[j], device_id=(tx, ty, c), device_id_type=MESH)
        for j, (tx, ty) in enumerate(others)]


def _chip_exchange_start(blk, *, name):
    land = pltpu.with_memory_space_constraint(lax.empty((N_CHIPS, *blk.shape), blk.dtype), pltpu.HBM)

    def body(blk_ref, land_ref, send_sems, recv_sems, blk_out, land_out, token):
        for cp in _chip_copies(blk_ref, land_ref, send_sems, recv_sems):
            cp.start()
        token[...] = jnp.zeros_like(token)

    return pl.pallas_call(
        body, name=name,
        out_shape=(pltpu.SemaphoreType.DMA((3,)), pltpu.SemaphoreType.DMA((3,)),
                   pltpu.HBM(blk.shape, blk.dtype), pltpu.HBM(land.shape, land.dtype), TOKEN),
        in_specs=[HBM_SPEC] * 2,
        out_specs=(SEM_SPEC, SEM_SPEC, HBM_SPEC, HBM_SPEC, pl.BlockSpec(memory_space=pltpu.VMEM)),
        input_output_aliases={0: 2, 1: 3},
        compiler_params=SPLIT_COPY,
    )(pltpu.with_memory_space_constraint(blk, pltpu.HBM), land)


def _chip_exchange_wait(send_sems, recv_sems, blk, land, after, *, name):
    def body(blk_ref, land_ref, send_sems, recv_sems, after_ref, blk_out, land_out):
        for cp in _chip_copies(blk_ref, land_ref, send_sems, recv_sems):
            cp.wait_send()
            cp.wait_recv()

    return pl.pallas_call(
        body, name=name,
        out_shape=[pltpu.HBM(blk.shape, blk.dtype), pltpu.HBM(land.shape, land.dtype)],
        in_specs=[HBM_SPEC] * 2 + [SEM_SPEC, SEM_SPEC, ANY],
        out_specs=[HBM_SPEC] * 2,
        input_output_aliases={0: 0, 1: 1},
        compiler_params=SPLIT_COPY,
    )(blk, land, send_sems, recv_sems, after)[1]


def _scatter_geo(parts, axes):
    out = []
    for p, ax in zip(parts, axes):
        _, rh, cols = p.shape
        out.append((ax, rh, cols // N_CHIPS if ax == 1 else cols))
    return out


def _scatter_copies(p_refs, q_refs, geo, send_sems, recv_sems):
    x, y, c, others = _place()
    chip = 2 * x + y
    cps = []
    for i, (ax, rh, cw) in enumerate(geo):
        for j, (tx, ty) in enumerate(others):
            k = 2 * tx + ty
            src = (p_refs[i].at[0, :, pl.ds(pl.multiple_of(k * cw, LANES), cw)] if ax == 1
                   else p_refs[i].at[k])
            cps.append(pltpu.make_async_remote_copy(
                src_ref=src, dst_ref=q_refs[i].at[chip], send_sem=send_sems.at[3 * i + j],
                recv_sem=recv_sems.at[3 * i + j], device_id=(tx, ty, c), device_id_type=MESH))
    return cps


def _scatter_start(parts, axes, *, name):
    n = len(parts)
    geo = _scatter_geo(parts, axes)
    slots = [pltpu.HBM((N_CHIPS, rh, cw), p.dtype) for p, (_, rh, cw) in zip(parts, geo)]

    def body(*refs):
        p_refs, q_refs = refs[:n], refs[n:2 * n]
        send_sems, recv_sems = refs[2 * n], refs[2 * n + 1]
        token = refs[4 * n + 2]
        for cp in _scatter_copies(p_refs, q_refs, geo, send_sems, recv_sems):
            cp.start()
        token[...] = jnp.zeros_like(token)

    land = [pltpu.with_memory_space_constraint(lax.empty(s.inner_aval.shape, s.inner_aval.dtype), pltpu.HBM)
            for s in slots]
    out = pl.pallas_call(
        body, name=name,
        out_shape=(pltpu.SemaphoreType.DMA((3 * n,)), pltpu.SemaphoreType.DMA((3 * n,)),
                   *[pltpu.HBM(p.shape, p.dtype) for p in parts], *slots, TOKEN),
        in_specs=[HBM_SPEC] * (2 * n),
        out_specs=(SEM_SPEC, SEM_SPEC, *[HBM_SPEC] * (2 * n), pl.BlockSpec(memory_space=pltpu.VMEM)),
        input_output_aliases={i: 2 + i for i in range(2 * n)},
        compiler_params=SPLIT_COPY,
    )(*[pltpu.with_memory_space_constraint(p, pltpu.HBM) for p in parts], *land)
    return out[0], out[1], list(out[2:2 + n]), list(out[2 + n:2 + 2 * n]), out[2 + 2 * n]


def _scatter_wait(send_sems, recv_sems, parts, slots, axes, after, *, name):
    n = len(parts)
    geo = _scatter_geo(parts, axes)

    def body(*refs):
        p_refs, q_refs = refs[:n], refs[n:2 * n]
        for cp in _scatter_copies(p_refs, q_refs, geo, refs[2 * n], refs[2 * n + 1]):
            cp.wait_send()
            cp.wait_recv()

    out = pl.pallas_call(
        body, name=name,
        out_shape=[pltpu.HBM(a.shape, a.dtype) for a in (*parts, *slots)],
        in_specs=[HBM_SPEC] * (2 * n) + [SEM_SPEC, SEM_SPEC, ANY],
        out_specs=[HBM_SPEC] * (2 * n),
        input_output_aliases={i: i for i in range(2 * n)},
        compiler_params=SPLIT_COPY,
    )(*parts, *slots, send_sems, recv_sems, after)
    return list(out[:n]), list(out[n:])


def _swap_copies(g_refs, r_refs, shapes, send_sems, recv_sems):
    x, y, c, _ = _place()
    cps = []
    for i, shape in enumerate(shapes):
        rh = shape[1] // 2
        src = g_refs[i].at[:, pl.ds(pl.multiple_of((1 - c) * rh, 16), rh), :]
        cps.append(pltpu.make_async_remote_copy(
            src_ref=src, dst_ref=r_refs[i], send_sem=send_sems.at[i], recv_sem=recv_sems.at[i],
            device_id=(x, y, 1 - c), device_id_type=MESH))
    return cps


def _sibling_swap_start(grads, *, name):
    n = len(grads)
    shapes = [g.shape for g in grads]
    lands = [pltpu.HBM((s[0], s[1] // 2, s[2]), g.dtype) for s, g in zip(shapes, grads)]

    def body(*refs):
        g_refs, r_refs = refs[:n], refs[n:2 * n]
        token = refs[4 * n + 2]
        for cp in _swap_copies(g_refs, r_refs, shapes, refs[2 * n], refs[2 * n + 1]):
            cp.start()
        token[...] = jnp.zeros_like(token)

    land = [pltpu.with_memory_space_constraint(lax.empty(s.inner_aval.shape, s.inner_aval.dtype), pltpu.HBM)
            for s in lands]
    out = pl.pallas_call(
        body, name=name,
        out_shape=(pltpu.SemaphoreType.DMA((n,)), pltpu.SemaphoreType.DMA((n,)),
                   *[pltpu.HBM(g.shape, g.dtype) for g in grads], *lands, TOKEN),
        in_specs=[HBM_SPEC] * (2 * n),
        out_specs=(SEM_SPEC, SEM_SPEC, *[HBM_SPEC] * (2 * n), pl.BlockSpec(memory_space=pltpu.VMEM)),
        input_output_aliases={i: 2 + i for i in range(2 * n)},
        compiler_params=SPLIT_COPY,
    )(*[pltpu.with_memory_space_constraint(g, pltpu.HBM) for g in grads], *land)
    return out[0], out[1], list(out[2:2 + n]), list(out[2 + n:2 + 2 * n]), out[2 + 2 * n]


def _sibling_swap_wait(send_sems, recv_sems, grads, lands, after, *, name):
    n = len(grads)
    shapes = [g.shape for g in grads]

    def body(*refs):
        g_refs, r_refs = refs[:n], refs[n:2 * n]
        for cp in _swap_copies(g_refs, r_refs, shapes, refs[2 * n], refs[2 * n + 1]):
            cp.wait_send()
            cp.wait_recv()

    out = pl.pallas_call(
        body, name=name,
        out_shape=[pltpu.HBM(a.shape, a.dtype) for a in (*grads, *lands)],
        in_specs=[HBM_SPEC] * (2 * n) + [SEM_SPEC, SEM_SPEC, ANY],
        out_specs=[HBM_SPEC] * (2 * n),
        input_output_aliases={i: i for i in range(2 * n)},
        compiler_params=SPLIT_COPY,
    )(*grads, *lands, send_sems, recv_sems, after)
    return list(out[:n]), list(out[n:])


def _pair_add(g3s, rxs, place, *, out_dtype, name):
    n = len(g3s)
    steps, trs = _group_tiles([g.shape[1] // 2 for g in g3s], 16)

    def body(p_ref, *refs):
        for i in range(n):
            refs[2 * n + i][...] = (refs[2 * i][...] + refs[2 * i + 1][...]).astype(out_dtype)

    in_specs, out_specs = [], []
    for g, tr in zip(g3s, trs):
        blk = (g.shape[0], tr, g.shape[2])
        in_specs += [pl.BlockSpec(blk, lambda i, p_ref: (0, p_ref[1] * steps + i, 0)),
                     pl.BlockSpec(blk, lambda i, p_ref: (0, i, 0))]
        out_specs.append(pl.BlockSpec(blk, lambda i, p_ref: (0, i, 0)))
    return pl.pallas_call(
        body,
        grid_spec=pltpu.PrefetchScalarGridSpec(
            num_scalar_prefetch=1, grid=(steps,), in_specs=in_specs, out_specs=out_specs),
        out_shape=[jax.ShapeDtypeStruct((g.shape[0], g.shape[1] // 2, g.shape[2]), out_dtype)
                   for g in g3s],
        compiler_params=_cp("parallel"), name=name)(place, *[a for q in zip(g3s, rxs) for a in q])


def _sum_slots(q, *, name):
    ns, rows, cols = q.shape
    tr = next(t for t in (128, 64, 32, 16, 8) if rows % t == 0)

    def body(q_ref, o_ref):
        acc = q_ref[0].astype(F32)
        for k in range(1, ns):
            acc = acc + q_ref[k].astype(F32)
        o_ref[...] = acc

    return pl.pallas_call(
        body, grid=(rows // tr,),
        in_specs=[pl.BlockSpec((ns, tr, cols), lambda i: (0, i, 0))],
        out_specs=_rspec(tr, cols),
        out_shape=jax.ShapeDtypeStruct((rows, cols), F32),
        compiler_params=_cp("parallel"), name=name)(q)


def _sum_chips(qs, ps, place, axes, *, name):
    n = len(qs)
    per = N_CHIPS + 1
    steps, trs = _group_tiles([q.shape[1] for q in qs], 16)

    def body(p_ref, *refs):
        chip = p_ref[0]
        for i in range(n):
            q_refs, own_ref = refs[per * i:per * i + N_CHIPS], refs[per * i + N_CHIPS]
            acc = jnp.where(chip == 0, own_ref[...], q_refs[0][...]).astype(F32)
            for k in range(1, N_CHIPS):
                acc = acc + jnp.where(chip == k, own_ref[...], q_refs[k][...]).astype(F32)
            refs[per * n + i][...] = acc

    def slot_spec(k, tr, cw):
        return pl.BlockSpec((None, tr, cw),
                            lambda i, p_ref: (jnp.where(p_ref[0] == k, (k + 1) % N_CHIPS, k), i, 0))

    in_specs, out_specs, operands = [], [], []
    for q, p, ax, tr in zip(qs, ps, axes, trs):
        cw = q.shape[2]
        in_specs += [slot_spec(k, tr, cw) for k in range(N_CHIPS)]
        in_specs.append(pl.BlockSpec((None, tr, cw), (lambda i, p_ref: (0, i, p_ref[0])) if ax == 1
                                     else (lambda i, p_ref: (p_ref[0], i, 0))))
        out_specs.append(pl.BlockSpec((tr, cw), lambda i, p_ref: (p_ref[1] * steps + i, 0)))
        operands += [q] * N_CHIPS + [p]
    return pl.pallas_call(
        body,
        grid_spec=pltpu.PrefetchScalarGridSpec(
            num_scalar_prefetch=1, grid=(steps,), in_specs=in_specs, out_specs=out_specs),
        out_shape=[jax.ShapeDtypeStruct((2 * q.shape[1], q.shape[2]), F32) for q in qs],
        compiler_params=_cp("parallel"), name=name)(place, *operands)


def _sibling_share(shards, *, name):
    n = len(shards)

    def body(*refs):
        o_refs = refs[n:2 * n]
        send_sems, recv_sems = refs[2 * n:]
        x, y, c, _ = _place()
        cps = []
        for i in range(n):
            rh = shards[i].shape[0] // 2
            mine = o_refs[i].at[pl.ds(pl.multiple_of(c * rh, 8), rh), :]
            cp = pltpu.make_async_remote_copy(
                src_ref=mine, dst_ref=mine, send_sem=send_sems.at[i], recv_sem=recv_sems.at[i],
                device_id=(x, y, 1 - c), device_id_type=MESH)
            cp.start()
            cps.append(cp)
        for i in range(n):
            rh = shards[i].shape[0] // 2
            theirs = o_refs[i].at[pl.ds(pl.multiple_of((1 - c) * rh, 8), rh), :]
            pltpu.make_async_remote_copy(
                src_ref=theirs, dst_ref=theirs, send_sem=send_sems.at[i], recv_sem=recv_sems.at[i],
                device_id=(x, y, c), device_id_type=MESH).wait_recv()
        for cp in cps:
            cp.wait_send()

    return pl.pallas_call(
        body, in_specs=[ANY] * n, out_specs=[ANY] * n,
        out_shape=[jax.ShapeDtypeStruct(h.shape, h.dtype) for h in shards],
        input_output_aliases={i: i for i in range(n)},
        scratch_shapes=[pltpu.SemaphoreType.DMA((n,)), pltpu.SemaphoreType.DMA((n,))],
        name=name)(*shards)


def _gather_all(blk, *, name, after=None, shares=()):
    rows, cols = blk.shape
    extra = [] if after is None else [after]
    n_s = len(shares)

    def body(x_ref, *refs):
        s_refs = refs[len(extra) + n_s + 1:len(extra) + 2 * n_s + 1]
        out_ref = refs[len(extra) + n_s]
        send_sems, recv_sems, local_sem, s_send, s_recv = refs[len(extra) + 2 * n_s + 1:]
        x, y, c = lax.axis_index("x"), lax.axis_index("y"), lax.axis_index("c")
        me = 4 * x + 2 * y + c
        mine = pltpu.make_async_copy(x_ref, out_ref.at[me], local_sem)
        mine.start()
        cps = []
        for i in range(n_s):
            rh = shares[i].shape[0] // 2
            half = s_refs[i].at[pl.ds(pl.multiple_of(c * rh, 8), rh), :]
            cp = pltpu.make_async_remote_copy(
                src_ref=half, dst_ref=half, send_sem=s_send.at[i], recv_sem=s_recv.at[i],
                device_id=(x, y, 1 - c), device_id_type=MESH)
            cp.start()
            cps.append(cp)
        for k in range(1, N_DEV):
            tx = (1 - x) if (k >> 2) & 1 else x
            ty = (1 - y) if (k >> 1) & 1 else y
            tc = (1 - c) if k & 1 else c
            cp = pltpu.make_async_remote_copy(
                src_ref=x_ref, dst_ref=out_ref.at[me], send_sem=send_sems.at[k - 1],
                recv_sem=recv_sems.at[k - 1], device_id=(tx, ty, tc), device_id_type=MESH)
            cp.start()
            cps.append(cp)
        for k in range(1, N_DEV):
            tx = (1 - x) if (k >> 2) & 1 else x
            ty = (1 - y) if (k >> 1) & 1 else y
            tc = (1 - c) if k & 1 else c
            got = out_ref.at[4 * tx + 2 * ty + tc]
            pltpu.make_async_remote_copy(
                src_ref=got, dst_ref=got, send_sem=send_sems.at[k - 1], recv_sem=recv_sems.at[k - 1],
                device_id=(x, y, c), device_id_type=MESH).wait_recv()
        for i in range(n_s):
            rh = shares[i].shape[0] // 2
            theirs = s_refs[i].at[pl.ds(pl.multiple_of((1 - c) * rh, 8), rh), :]
            pltpu.make_async_remote_copy(
                src_ref=theirs, dst_ref=theirs, send_sem=s_send.at[i], recv_sem=s_recv.at[i],
                device_id=(x, y, c), device_id_type=MESH).wait_recv()
        for cp in cps:
            cp.wait_send()
        mine.wait()

    vm = pl.BlockSpec(memory_space=pltpu.VMEM)
    out = pl.pallas_call(
        body, in_specs=[vm] + [ANY] * (len(extra) + n_s), out_specs=[vm] + [ANY] * n_s,
        out_shape=[jax.ShapeDtypeStruct((N_DEV, rows, cols), blk.dtype)]
        + [jax.ShapeDtypeStruct(s.shape, s.dtype) for s in shares],
        input_output_aliases={1 + len(extra) + i: 1 + i for i in range(n_s)},
        scratch_shapes=[pltpu.SemaphoreType.DMA((N_DEV - 1,)), pltpu.SemaphoreType.DMA((N_DEV - 1,)),
                        pltpu.SemaphoreType.DMA, pltpu.SemaphoreType.DMA((max(n_s, 1),)),
                        pltpu.SemaphoreType.DMA((max(n_s, 1),))],
        name=name)(blk, *extra, *shares)
    return (out[0], *out[1:]) if n_s else out[0]


def _as_rows(a):
    flat = a.reshape(-1)
    n = flat.shape[0]
    rows = -(-n // (8 * LANES)) * 8
    return jnp.pad(flat, (0, rows * LANES - n)).reshape(rows, LANES)


def _from_rows(p, shape):
    n = int(np.prod(shape))
    return p.reshape(-1)[:n].reshape(shape)


WEIGHT_AXES = (1, 1, 1, 0, 1, 0)
WIRE = BF16


def kernel(x, meta_tokens, w_in, w_na_out, w_hg_out, w_o, w_up, w_down, norm_mix, norm_mlp, norm_final, hg_norm, na_rpb, hg_lb_logits, loss_target, m_meta_tokens, m_w_in, m_w_na_out, m_w_hg_out, m_w_o, m_w_up, m_w_down, m_norm_mix, m_norm_mlp, m_norm_final, m_hg_norm, m_na_rpb, m_hg_lb_logits, v_meta_tokens, v_w_in, v_w_na_out, v_w_hg_out, v_w_o, v_w_up, v_w_down, v_norm_mix, v_norm_mlp, v_norm_final, v_hg_norm, v_na_rpb, v_hg_lb_logits):
    xi, yi, ci = lax.axis_index("x"), lax.axis_index("y"), lax.axis_index("c")
    chip = 2 * xi + yi
    d = x.shape[-1]
    dshard = meta_tokens.shape[1]
    hgw = hg_norm.shape[1]
    lbs = hg_lb_logits.shape[2]
    big = [w_in[0], w_na_out[0], w_hg_out[0], w_o[0], w_up[0], w_down[0]]
    big_m = [m_w_in[0], m_w_na_out[0], m_w_hg_out[0], m_w_o[0], m_w_up[0], m_w_down[0]]
    big_v = [v_w_in[0], v_w_na_out[0], v_w_hg_out[0], v_w_o[0], v_w_up[0], v_w_down[0]]

    place = jnp.stack([chip, ci]).astype(jnp.int32)
    in_axes, rest_axes = WEIGHT_AXES[:1], WEIGHT_AXES[1:]
    own_w = _cast_into_full(big, WEIGHT_AXES, place, name="cast_shards")
    small_in = jnp.concatenate([_as_rows(meta_tokens), _as_rows(hg_lb_logits)], axis=0)
    sm_send, sm_recv, sm_blk, sm_land, sm_token = _chip_exchange_start(small_in,
                                                                       name="small_params_start")
    in_send, in_recv, in_bufs, in_token = _allgather_start(own_w[:1], in_axes, sm_token,
                                                           name="weight_allgather_in_start")
    ag_send, ag_recv, ag_bufs, ag_token = _allgather_start(own_w[1:], rest_axes, in_token,
                                                           name="weight_allgather_rest_start")
    sm_land = _chip_exchange_wait(sm_send, sm_recv, sm_blk, sm_land, ag_token,
                                  name="small_params_wait")
    small_all = lax.dynamic_update_slice(sm_land, small_in[None], (chip, 0, 0))
    forward = {}

    def first_weight(a, tb):
        total = big[0].shape[1] * N_CHIPS
        part = _matmul_windows(a, big[0], place, total=total, name="mm_in_own")
        got = _allgather_wait(in_send, in_recv, in_bufs, in_axes, (part, tb),
                              name="weight_allgather_in_wait")
        w = _allgather_forward(got, in_axes, name="weight_allgather_in_forward")[0]
        return w, _matmul_windows(a, w, place, total=total, into=part, name="mm_in")

    def rest_landed(after):
        got = _allgather_wait(ag_send, ag_recv, ag_bufs, rest_axes, after,
                              name="weight_allgather_rest_wait")
        send, recv, bufs, token = _allgather_forward_start(
            got, rest_axes, name="weight_allgather_rest_forward_start")
        forward["rest"] = (send, recv, bufs)
        return token

    def rest_weights(after):
        return _allgather_forward_wait(*forward["rest"], rest_axes, after,
                                       name="weight_allgather_rest_forward_wait")

    n_meta_rows = N_META * dshard // LANES
    meta_full = (small_all[:, :n_meta_rows].reshape(N_CHIPS, N_META, dshard)
                 .transpose(1, 0, 2).reshape(N_META, d))
    lbl_full = (small_all[:, n_meta_rows:].reshape(N_CHIPS, -1)[:, :4 * lbs]
                .reshape(N_CHIPS, 2, 2, lbs).transpose(1, 2, 0, 3).reshape(2, 2, N_CHIPS * lbs))
    lb = jax.nn.softmax(lbl_full, axis=1)[:, 0]

    def by_chip(dws, axes):
        return [g.reshape(1, *g.shape) if ax == 1
                else g.reshape(N_CHIPS, g.shape[0] // N_CHIPS, g.shape[1]) for g, ax in zip(dws, axes)]

    flying = {}

    def scatter(tag, axes, g3, rx):
        parts = _pair_add(g3, rx, place, out_dtype=WIRE, name=f"grad_pair_add_{tag}")
        send, recv, parts, slots, token = _scatter_start(parts, axes,
                                                         name=f"grad_scatter_{tag}_start")
        flying[tag] = (send, recv, parts, slots)
        return token

    def swap(tag, axes):
        def start(dws):
            send, recv, g3, lands, token = _sibling_swap_start(
                by_chip(dws, axes), name=f"grad_sibling_swap_{tag}_start")
            flying["swap_" + tag] = (send, recv, g3, lands)
            return token

        def finish(after):
            g3, rx = _sibling_swap_wait(*flying["swap_" + tag], after,
                                        name=f"grad_sibling_swap_{tag}_wait")
            return scatter(tag, axes, g3, rx)
        return start, finish

    swap_rest, scatter_rest = swap("rest", rest_axes)
    swap_in, scatter_in = swap("in", in_axes)

    def landed(tag, axes, after):
        return _scatter_wait(*flying[tag], axes, after, name=f"grad_scatter_{tag}_wait")

    (loss, dx, dmeta, *_, dg_mix, dg_mlp, dg_fin, d_gain, d_rpb, d_lb) = _local_step(
        x[0], loss_target[0], meta_full, first_weight, rest_weights, norm_mix, norm_mlp,
        norm_final.reshape(1, d), hg_norm, na_rpb[0], lb, swap_rest, scatter_rest,
        lambda dw_in: swap_in([dw_in]), rest_landed, scatter_in)

    parts_rest, slots_rest = landed("rest", rest_axes, dx)
    g_rest = _sibling_share(_sum_chips(slots_rest, parts_rest, place, rest_axes,
                                       name="grad_sum_chips_rest"),
                            name="grad_sibling_share_rest")
    out_rest = _adamw(big[1:], g_rest, big_m[1:], big_v[1:], name="adamw_rest")
    parts_in, slots_in = landed("in", in_axes, out_rest[-1][0])
    half_in = _sum_chips(slots_in, parts_in, place, in_axes, name="grad_sum_chips_in")

    d_rpb_c = d_rpb[:, :2 * NA_WIN_W - 1]
    small_g = [dmeta, dg_mix, dg_mlp, dg_fin, d_gain, d_rpb_c, d_lb, loss]
    packed = jnp.concatenate([_as_rows(a) for a in small_g], axis=0)
    gathered, g_in = _gather_all(packed, shares=half_in, name="gather_small_grads")
    total = _sum_slots(gathered, name="sum_small_grads")
    offs = np.cumsum([0] + [_as_rows(a).shape[0] for a in small_g])
    take = lambda i, shape: _from_rows(total[offs[i]:offs[i + 1]], shape)
    g_meta_full = take(0, (N_META, d))
    g_norm_mix, g_norm_mlp = take(1, (1, d)), take(2, (1, d))
    g_norm_final = take(3, (d,))
    g_hg_norm = take(4, (1, hgw))
    g_rpb = take(5, na_rpb.shape)
    g_lb = take(6, (2, hgw))
    loss_total = take(7, (1, LANES))[0, 0]
    g_meta = lax.dynamic_slice_in_dim(g_meta_full, chip * dshard, dshard, axis=1)
    dl0 = lb * (1.0 - lb) * g_lb
    g_lbl_full = jnp.stack([dl0, -dl0], axis=1)
    g_lbl = lax.dynamic_slice_in_dim(g_lbl_full, chip * lbs, lbs, axis=2)

    big_out = _adamw(big[:1], [g_in], big_m[:1], big_v[:1], name="adamw_in") + out_rest
    small_w = [meta_tokens, norm_mix, norm_mlp, norm_final, hg_norm, na_rpb, hg_lb_logits]
    small_gr = [g_meta, g_norm_mix, g_norm_mlp, g_norm_final, g_hg_norm, g_rpb, g_lbl]
    small_m = [m_meta_tokens, m_norm_mix, m_norm_mlp, m_norm_final, m_hg_norm, m_na_rpb, m_hg_lb_logits]
    small_v = [v_meta_tokens, v_norm_mix, v_norm_mlp, v_norm_final, v_hg_norm, v_na_rpb, v_hg_lb_logits]
    pk = lambda lst: jnp.concatenate([_as_rows(a) for a in lst], axis=0)
    ((sd, sm, sv, _),) = _adamw([pk(small_w)], [pk(small_gr)], [pk(small_m)], [pk(small_v)],
                             name="adamw_small")
    soffs = np.cumsum([0] + [_as_rows(a).shape[0] for a in small_w])
    unpk = lambda p: [_from_rows(p[soffs[i]:soffs[i + 1]], small_w[i].shape) for i in range(len(small_w))]
    sd, sm, sv = unpk(sd), unpk(sm), unpk(sv)

    def order(bigs, smalls):
        return [smalls[0]] + [b.reshape(1, *b.shape) for b in bigs] + smalls[1:]

    grads = order([o[3] for o in big_out], small_gr)
    deltas = order([o[0] for o in big_out], sd)
    new_m = order([o[1] for o in big_out], sm)
    new_v = order([o[2] for o in big_out], sv)
    return (loss_total, dx.reshape(1, *dx.shape), *grads, *deltas, *new_m, *new_v)
```
